```python
import math
import jax
import jax.numpy as jnp
from jax import lax
import numpy as np

D_MODEL = 1024
BATCH = 8
SEQ = 4096
DEPTH = 2


N_EVEN = (DEPTH + 1) // 2
N_ODD = DEPTH // 2
EPS = 1e-6

HG_HEADS = 4
HG_DIM = 128
HG_WIDTH = HG_HEADS * HG_DIM
HG_CHUNK = 64

MLA_HEADS = 4
MLA_Q_RANK = 256
MLA_KV_RANK = 128
MLA_NOPE = 128
MLA_ROPE = 64
MLA_V = 128
MLA_QK = MLA_NOPE + MLA_ROPE
MLA_WIDTH = MLA_HEADS * MLA_V
ATTN_BLOCK = 128
ROPE_BASE = 10000.0

IN_SPLITS = (HG_WIDTH, HG_WIDTH, HG_WIDTH, HG_WIDTH, MLA_Q_RANK, MLA_KV_RANK, MLA_ROPE)
IN_WIDTH = 4 * HG_WIDTH + MLA_Q_RANK + MLA_KV_RANK + MLA_ROPE
MIX_WIDTH = HG_WIDTH + MLA_WIDTH

S5_GROUP = 16
S5_GROUPS = D_MODEL // S5_GROUP
S5_STATE = 64
S5_CHUNK = 128
DT_MIN = 1e-3
DT_MAX = 1e-1

MEM_LEN = 256
XA_HEADS = 4
XA_DIM = D_MODEL // XA_HEADS

D_FF = 2816
CONV_W = 3

kernel_name = 'hybrid_hgrn2_mla_s5_memxattn_convffn'


def rms_norm(x, gain):
    xf = x.astype(jnp.float32)
    y = xf * lax.rsqrt(jnp.mean(xf * xf, axis=-1, keepdims=True) + EPS)
    return (y * gain.astype(jnp.float32)).astype(x.dtype)


def rope_tables(positions):
    inv_freq = 1.0 / (ROPE_BASE ** (jnp.arange(0, MLA_ROPE, 2, dtype=jnp.float32) / MLA_ROPE))
    ang = positions.astype(jnp.float32)[..., None] * inv_freq
    return jnp.cos(ang), jnp.sin(ang)


def apply_rope(x, cos, sin):
    xf = x.astype(jnp.float32)
    x1, x2 = jnp.split(xf, 2, axis=-1)
    c = cos[:, :, None, :]
    s = sin[:, :, None, :]
    return jnp.concatenate([x1 * c - x2 * s, x1 * s + x2 * c], axis=-1).astype(x.dtype)


def hgrn2_recurrence(q, f_logit, i_val, lb):
    bsz, seq, heads, dh = q.shape
    nc = seq // HG_CHUNK
    lbh = lb.astype(jnp.float32).reshape(heads, dh)
    f = lbh + (1.0 - lbh) * jax.nn.sigmoid(f_logit.astype(jnp.float32))
    log_f = jnp.log(f)
    k = 1.0 - f
    qf = jax.nn.silu(q.astype(jnp.float32))
    v = i_val.astype(jnp.float32)

    def to_chunks(t):
        return t.reshape(bsz, nc, HG_CHUNK, heads, dh).transpose(1, 0, 3, 2, 4)

    qc, kc, vc = to_chunks(qf), to_chunks(k), to_chunks(v)
    bc = jnp.cumsum(to_chunks(log_f), axis=3)
    causal = jnp.tril(jnp.ones((HG_CHUNK, HG_CHUNK), dtype=bool))[:, :, None]

    def chunk_step(state, inp):
        qt, kt, vt, bt = inp
        diff = bt[:, :, :, None, :] - bt[:, :, None, :, :]
        decay = jnp.exp(jnp.where(causal, diff, -jnp.inf))
        scores = jnp.einsum('bhtk,bhsk,bhtsk->bhts', qt, kt, decay)
        out = (jnp.einsum('bhts,bhsv->bhtv', scores, vt)
               + jnp.einsum('bhtk,bhkv->bhtv', qt * jnp.exp(bt), state))
        b_end = bt[:, :, -1:, :]
        new_state = (jnp.exp(b_end[:, :, 0, :])[..., None] * state
                     + jnp.einsum('bhsk,bhsv->bhkv', kt * jnp.exp(b_end - bt), vt))
        return new_state, out

    state0 = jnp.zeros((bsz, heads, dh, dh), jnp.float32)
    _, out = lax.scan(chunk_step, state0, (qc, kc, vc, bc))
    return out.transpose(1, 0, 3, 2, 4).reshape(bsz, seq, heads, dh)


def causal_block_attention(q, k, v, scale):
    bsz, seq, heads, dk = q.shape
    nb = seq // ATTN_BLOCK
    q_blocks = q.reshape(bsz, nb, ATTN_BLOCK, heads, dk).transpose(1, 0, 2, 3, 4)
    k_pos = jnp.arange(seq)

    def one_block(args):
        qb, b_idx = args
        s = jnp.einsum('bqhd,bkhd->bhqk', qb, k, preferred_element_type=jnp.float32) * scale
        q_pos = b_idx * ATTN_BLOCK + jnp.arange(ATTN_BLOCK)
        s = jnp.where(k_pos[None, :] <= q_pos[:, None], s, -jnp.inf)
        p = jax.nn.softmax(s, axis=-1).astype(v.dtype)
        return jnp.einsum('bhqk,bkhd->bqhd', p, v)

    out = lax.map(one_block, (q_blocks, jnp.arange(nb)))
    return out.transpose(1, 0, 2, 3, 4).reshape(bsz, seq, heads, v.shape[-1])


def hgrn2_mla_mixer(h, cos, sin, lb, w_in, hg_out_norm, q_a_norm, w_uq, kv_a_norm, w_ukv,
                    qn_nope, qn_rope, kn_nope, kn_rope, w_out):
    bsz, seq, _ = h.shape
    split_at = np.cumsum(IN_SPLITS)[:-1].tolist()
    q_hg, f_hg, i_hg, g_hg, c_q, c_kv, k_pe = jnp.split(h @ w_in, split_at, axis=-1)

    def as_heads(t):
        return t.reshape(bsz, seq, HG_HEADS, HG_DIM)
    o_hg = hgrn2_recurrence(as_heads(q_hg), as_heads(f_hg), as_heads(i_hg), lb).astype(h.dtype)
    o_hg = rms_norm(o_hg, hg_out_norm.reshape(HG_HEADS, HG_DIM)).reshape(bsz, seq, HG_WIDTH)
    o_hg = o_hg * jax.nn.silu(g_hg)

    q = (rms_norm(c_q, q_a_norm) @ w_uq).reshape(bsz, seq, MLA_HEADS, MLA_QK)
    kv = (rms_norm(c_kv, kv_a_norm) @ w_ukv).reshape(bsz, seq, MLA_HEADS, MLA_NOPE + MLA_V)
    q_nope = rms_norm(q[..., :MLA_NOPE], qn_nope)
    q_pe = apply_rope(rms_norm(q[..., MLA_NOPE:], qn_rope), cos, sin)
    k_nope = rms_norm(kv[..., :MLA_NOPE], kn_nope)
    v = kv[..., MLA_NOPE:]
    k_pe = apply_rope(rms_norm(k_pe, kn_rope)[:, :, None, :], cos, sin)
    k_pe = jnp.broadcast_to(k_pe, (bsz, seq, MLA_HEADS, MLA_ROPE))
    q_full = jnp.concatenate([q_nope, q_pe], axis=-1)
    k_full = jnp.concatenate([k_nope, k_pe], axis=-1)
    o_mla = causal_block_attention(q_full, k_full, v, MLA_QK ** -0.5).reshape(bsz, seq, MLA_WIDTH)

    return jnp.concatenate([o_hg, o_mla], axis=-1) @ w_out


def complex_affine_combine(e1, e2):
    a1r, a1i, b1r, b1i = e1
    a2r, a2i, b2r, b2i = e2
    ar = a2r * a1r - a2i * a1i
    ai = a2r * a1i + a2i * a1r
    br = a2r * b1r - a2i * b1i + b2r
    bi = a2r * b1i + a2i * b1r + b2i
    return ar, ai, br, bi


def s5_mixer(u, lam_re, lam_im, log_dt, b_re, b_im, c_re, c_im, d_skip, w_glu_a, w_glu_b):
    bsz, seq, _ = u.shape
    nc = seq // S5_CHUNK
    f32 = jnp.float32
    lr = lam_re.astype(f32)
    li = lam_im.astype(f32)
    dt = jnp.exp(log_dt.astype(f32))[:, None]
    mag = jnp.exp(lr * dt)
    ab_re = mag * jnp.cos(li * dt)
    ab_im = mag * jnp.sin(li * dt)
    den = lr * lr + li * li
    z_re = ((ab_re - 1.0) * lr + ab_im * li) / den
    z_im = (ab_im * lr - (ab_re - 1.0) * li) / den
    br = b_re.astype(f32)
    bi = b_im.astype(f32)
    bb_re = z_re[..., None] * br - z_im[..., None] * bi
    bb_im = z_re[..., None] * bi + z_im[..., None] * br
    cr = c_re.astype(f32)
    ci = c_im.astype(f32)
    uf = u.astype(f32)
    u_chunks = uf.reshape(bsz, nc, S5_CHUNK, S5_GROUPS, S5_GROUP).transpose(1, 0, 2, 3, 4)

    def chunk_step(carry, uc):
        hr0, hi0 = carry
        bu_re = jnp.einsum('bcgm,gpm->bcgp', uc, bb_re)
        bu_im = jnp.einsum('bcgm,gpm->bcgp', uc, bb_im)
        a_re = jnp.broadcast_to(ab_re, bu_re.shape)
        a_im = jnp.broadcast_to(ab_im, bu_im.shape)
        p_re, p_im, s_re, s_im = lax.associative_scan(
            complex_affine_combine, (a_re, a_im, bu_re, bu_im), axis=1)
        hr = s_re + p_re * hr0[:, None] - p_im * hi0[:, None]
        hi = s_im + p_re * hi0[:, None] + p_im * hr0[:, None]
        y = jnp.einsum('bcgp,gmp->bcgm', hr, cr) - jnp.einsum('bcgp,gmp->bcgm', hi, ci)
        return (hr[:, -1], hi[:, -1]), y

    init = (jnp.zeros((bsz, S5_GROUPS, S5_STATE), f32), jnp.zeros((bsz, S5_GROUPS, S5_STATE), f32))
    _, y = lax.scan(chunk_step, init, u_chunks)
    y = y.transpose(1, 0, 2, 3, 4).reshape(bsz, seq, D_MODEL) + d_skip.astype(f32) * uf
    y = jax.nn.gelu(y).astype(u.dtype)
    return (y @ w_glu_a) * jax.nn.sigmoid(y @ w_glu_b)


def memory_cross_attention(h, mem, mem_gain, wq, wk, wv, wo, q_gain, k_gain):
    bsz, seq, _ = h.shape
    m_len = mem.shape[1]
    m = rms_norm(mem, mem_gain)
    q = rms_norm((h @ wq).reshape(bsz, seq, XA_HEADS, XA_DIM), q_gain)
    k = rms_norm((m @ wk).reshape(bsz, m_len, XA_HEADS, XA_DIM), k_gain)
    v = (m @ wv).reshape(bsz, m_len, XA_HEADS, XA_DIM)
    s = jnp.einsum('bqhd,bkhd->bhqk', q, k, preferred_element_type=jnp.float32) * (XA_DIM ** -0.5)
    p = jax.nn.softmax(s, axis=-1).astype(v.dtype)
    o = jnp.einsum('bhqk,bkhd->bqhd', p, v).reshape(bsz, seq, D_MODEL)
    return o @ wo


def conv_gated_mlp(h, w_up, conv_w, conv_b, w_down):
    u = h @ w_up
    u = lax.conv_general_dilated(
        u, conv_w[:, None, :].astype(u.dtype), window_strides=(1,), padding=[(CONV_W - 1, 0)],
        dimension_numbers=('NWC', 'WIO', 'NWC'), feature_group_count=2 * D_FF) + conv_b
    gate, val = jnp.split(u, 2, axis=-1)
    return (jax.nn.silu(gate) * val) @ w_down


def _fwd_setup_inputs(seed: int = 0) -> dict:
    key = jax.random.key(seed)
    keys = jax.random.split(key, 48)
    counter = [0]

    def nxt():
        kk = keys[counter[0]]
        counter[0] += 1
        return kk

    def nrm(shape, scale):
        return scale * jax.random.normal(nxt(), shape, jnp.float32)

    def gain(shape):
        return 1.0 + nrm(shape, 0.02)

    d = D_MODEL
    inp = {}
    inp['x'] = nrm((BATCH, SEQ, d), 1.0)
    inp['mem'] = nrm((BATCH, MEM_LEN, d), 1.0)
    inp['positions'] = (jnp.arange(SEQ, dtype=jnp.int32)[None, :]
                        + jax.random.randint(nxt(), (BATCH, 1), 0, 1024, dtype=jnp.int32))
    inp['norm_mix'] = gain((DEPTH, d))
    inp['norm_xa'] = gain((DEPTH, d))
    inp['norm_mem'] = gain((DEPTH, d))
    inp['norm_ffn'] = gain((DEPTH, d))
    inp['xa_wq'] = nrm((DEPTH, d, d), d ** -0.5)
    inp['xa_wk'] = nrm((DEPTH, d, d), d ** -0.5)
    inp['xa_wv'] = nrm((DEPTH, d, d), d ** -0.5)
    inp['xa_wo'] = nrm((DEPTH, d, d), d ** -0.5)
    inp['xa_q_norm'] = gain((DEPTH, XA_DIM))
    inp['xa_k_norm'] = gain((DEPTH, XA_DIM))
    inp['ffn_w_up'] = nrm((DEPTH, d, 2 * D_FF), d ** -0.5)
    inp['ffn_conv_w'] = nrm((DEPTH, CONV_W, 2 * D_FF), CONV_W ** -0.5)
    inp['ffn_conv_b'] = nrm((DEPTH, 2 * D_FF), 0.01)
    inp['ffn_w_down'] = nrm((DEPTH, D_FF, d), D_FF ** -0.5)
    inp['hg_lb_logits'] = nrm((N_EVEN + 1, HG_WIDTH), 0.1)
    inp['mix_w_in'] = nrm((N_EVEN, d, IN_WIDTH), d ** -0.5)
    inp['hg_out_norm'] = gain((N_EVEN, HG_WIDTH))
    inp['mla_q_a_norm'] = gain((N_EVEN, MLA_Q_RANK))
    inp['mla_w_uq'] = nrm((N_EVEN, MLA_Q_RANK, MLA_HEADS * MLA_QK), MLA_Q_RANK ** -0.5)
    inp['mla_kv_a_norm'] = gain((N_EVEN, MLA_KV_RANK))
    inp['mla_w_ukv'] = nrm((N_EVEN, MLA_KV_RANK, MLA_HEADS * (MLA_NOPE + MLA_V)), MLA_KV_RANK ** -0.5)
    inp['mla_qn_nope'] = gain((N_EVEN, MLA_NOPE))
    inp['mla_qn_rope'] = gain((N_EVEN, MLA_ROPE))
    inp['mla_kn_nope'] = gain((N_EVEN, MLA_NOPE))
    inp['mla_kn_rope'] = gain((N_EVEN, MLA_ROPE))
    inp['mix_w_out'] = nrm((N_EVEN, MIX_WIDTH, d), MIX_WIDTH ** -0.5)
    inp['s5_lam_re'] = -0.5 + nrm((N_ODD, S5_GROUPS, S5_STATE), 0.01)
    inp['s5_lam_im'] = (math.pi * jnp.arange(S5_STATE, dtype=jnp.float32)[None, None, :]
                        + nrm((N_ODD, S5_GROUPS, S5_STATE), 0.01))
    inp['s5_log_dt'] = jax.random.uniform(nxt(), (N_ODD, S5_GROUPS), jnp.float32,
                                          math.log(DT_MIN), math.log(DT_MAX))
    inp['s5_b_re'] = nrm((N_ODD, S5_GROUPS, S5_STATE, S5_GROUP), (2 * S5_GROUP) ** -0.5)
    inp['s5_b_im'] = nrm((N_ODD, S5_GROUPS, S5_STATE, S5_GROUP), (2 * S5_GROUP) ** -0.5)
    inp['s5_c_re'] = nrm((N_ODD, S5_GROUPS, S5_GROUP, S5_STATE), S5_STATE ** -0.5)
    inp['s5_c_im'] = nrm((N_ODD, S5_GROUPS, S5_GROUP, S5_STATE), S5_STATE ** -0.5)
    inp['s5_d'] = nrm((N_ODD, d), 1.0)
    inp['s5_w_glu_a'] = nrm((N_ODD, d, d), d ** -0.5)
    inp['s5_w_glu_b'] = nrm((N_ODD, d, d), d ** -0.5)
    return inp


def _fwd_reference(x, mem, positions, norm_mix, norm_xa, norm_mem, norm_ffn,
              xa_wq, xa_wk, xa_wv, xa_wo, xa_q_norm, xa_k_norm,
              ffn_w_up, ffn_conv_w, ffn_conv_b, ffn_w_down,
              hg_lb_logits, mix_w_in, hg_out_norm, mla_q_a_norm, mla_w_uq, mla_kv_a_norm, mla_w_ukv,
              mla_qn_nope, mla_qn_rope, mla_kn_nope, mla_kn_rope, mix_w_out,
              s5_lam_re, s5_lam_im, s5_log_dt, s5_b_re, s5_b_im, s5_c_re, s5_c_im, s5_d,
              s5_w_glu_a, s5_w_glu_b):
    cos, sin = rope_tables(positions)
    lb_all = jnp.cumsum(jax.nn.softmax(hg_lb_logits.astype(jnp.float32), axis=0), axis=0)
    h = x
    for layer in range(DEPTH):
        j = layer // 2
        hn = rms_norm(h, norm_mix[layer])
        if layer % 2 == 0:
            mix = hgrn2_mla_mixer(hn, cos, sin, lb_all[j], mix_w_in[j], hg_out_norm[j],
                                  mla_q_a_norm[j], mla_w_uq[j], mla_kv_a_norm[j], mla_w_ukv[j],
                                  mla_qn_nope[j], mla_qn_rope[j], mla_kn_nope[j], mla_kn_rope[j],
                                  mix_w_out[j])
        else:
            mix = s5_mixer(hn, s5_lam_re[j], s5_lam_im[j], s5_log_dt[j], s5_b_re[j], s5_b_im[j],
                           s5_c_re[j], s5_c_im[j], s5_d[j], s5_w_glu_a[j], s5_w_glu_b[j])
        h = h + mix
        h = h + memory_cross_attention(rms_norm(h, norm_xa[layer]), mem, norm_mem[layer],
                                       xa_wq[layer], xa_wk[layer], xa_wv[layer], xa_wo[layer],
                                       xa_q_norm[layer], xa_k_norm[layer])
        h = h + conv_gated_mlp(rms_norm(h, norm_ffn[layer]), ffn_w_up[layer], ffn_conv_w[layer],
                               ffn_conv_b[layer], ffn_w_down[layer])
    return h


import jax as _jax
import jax.numpy as _jnp

TWIN_FORMAT = 'train_step'
FWD_PARAMS = ['x', 'mem', 'positions', 'norm_mix', 'norm_xa', 'norm_mem', 'norm_ffn', 'xa_wq', 'xa_wk', 'xa_wv', 'xa_wo', 'xa_q_norm', 'xa_k_norm', 'ffn_w_up', 'ffn_conv_w', 'ffn_conv_b', 'ffn_w_down', 'hg_lb_logits', 'mix_w_in', 'hg_out_norm', 'mla_q_a_norm', 'mla_w_uq', 'mla_kv_a_norm', 'mla_w_ukv', 'mla_qn_nope', 'mla_qn_rope', 'mla_kn_nope', 'mla_kn_rope', 'mix_w_out', 's5_lam_re', 's5_lam_im', 's5_log_dt', 's5_b_re', 's5_b_im', 's5_c_re', 's5_c_im', 's5_d', 's5_w_glu_a', 's5_w_glu_b']
TWIN_WEIGHTS = ['norm_mix', 'norm_xa', 'norm_mem', 'norm_ffn', 'xa_wq', 'xa_wk', 'xa_wv', 'xa_wo', 'xa_q_norm', 'xa_k_norm', 'ffn_w_up', 'ffn_conv_w', 'ffn_conv_b', 'ffn_w_down', 'hg_lb_logits', 'mix_w_in', 'hg_out_norm', 'mla_q_a_norm', 'mla_w_uq', 'mla_kv_a_norm', 'mla_w_ukv', 'mla_qn_nope', 'mla_qn_rope', 'mla_kn_nope', 'mla_kn_rope', 'mix_w_out', 's5_lam_re', 's5_lam_im', 's5_log_dt', 's5_b_re', 's5_b_im', 's5_c_re', 's5_c_im', 's5_d', 's5_w_glu_a', 's5_w_glu_b']
TWIN_DIFF_INPUT = 'x'
TWIN_INPUTS = ['x', 'mem', 'positions', 'norm_mix', 'norm_xa', 'norm_mem', 'norm_ffn', 'xa_wq', 'xa_wk', 'xa_wv', 'xa_wo', 'xa_q_norm', 'xa_k_norm', 'ffn_w_up', 'ffn_conv_w', 'ffn_conv_b', 'ffn_w_down', 'hg_lb_logits', 'mix_w_in', 'hg_out_norm', 'mla_q_a_norm', 'mla_w_uq', 'mla_kv_a_norm', 'mla_w_ukv', 'mla_qn_nope', 'mla_qn_rope', 'mla_kn_nope', 'mla_kn_rope', 'mix_w_out', 's5_lam_re', 's5_lam_im', 's5_log_dt', 's5_b_re', 's5_b_im', 's5_c_re', 's5_c_im', 's5_d', 's5_w_glu_a', 's5_w_glu_b', 'loss_target', 'm_norm_mix', 'm_norm_xa', 'm_norm_mem', 'm_norm_ffn', 'm_xa_wq', 'm_xa_wk', 'm_xa_wv', 'm_xa_wo', 'm_xa_q_norm', 'm_xa_k_norm', 'm_ffn_w_up', 'm_ffn_conv_w', 'm_ffn_conv_b', 'm_ffn_w_down', 'm_hg_lb_logits', 'm_mix_w_in', 'm_hg_out_norm', 'm_mla_q_a_norm', 'm_mla_w_uq', 'm_mla_kv_a_norm', 'm_mla_w_ukv', 'm_mla_qn_nope', 'm_mla_qn_rope', 'm_mla_kn_nope', 'm_mla_kn_rope', 'm_mix_w_out', 'm_s5_lam_re', 'm_s5_lam_im', 'm_s5_log_dt', 'm_s5_b_re', 'm_s5_b_im', 'm_s5_c_re', 'm_s5_c_im', 'm_s5_d', 'm_s5_w_glu_a', 'm_s5_w_glu_b', 'v_norm_mix', 'v_norm_xa', 'v_norm_mem', 'v_norm_ffn', 'v_xa_wq', 'v_xa_wk', 'v_xa_wv', 'v_xa_wo', 'v_xa_q_norm', 'v_xa_k_norm', 'v_ffn_w_up', 'v_ffn_conv_w', 'v_ffn_conv_b', 'v_ffn_w_down', 'v_hg_lb_logits', 'v_mix_w_in', 'v_hg_out_norm', 'v_mla_q_a_norm', 'v_mla_w_uq', 'v_mla_kv_a_norm', 'v_mla_w_ukv', 'v_mla_qn_nope', 'v_mla_qn_rope', 'v_mla_kn_nope', 'v_mla_kn_rope', 'v_mix_w_out', 'v_s5_lam_re', 'v_s5_lam_im', 'v_s5_log_dt', 'v_s5_b_re', 'v_s5_b_im', 'v_s5_c_re', 'v_s5_c_im', 'v_s5_d', 'v_s5_w_glu_a', 'v_s5_w_glu_b']
TWIN_OUTPUTS = ['loss', 'grad_x', 'grad_norm_mix', 'grad_norm_xa', 'grad_norm_mem', 'grad_norm_ffn', 'grad_xa_wq', 'grad_xa_wk', 'grad_xa_wv', 'grad_xa_wo', 'grad_xa_q_norm', 'grad_xa_k_norm', 'grad_ffn_w_up', 'grad_ffn_conv_w', 'grad_ffn_conv_b', 'grad_ffn_w_down', 'grad_hg_lb_logits', 'grad_mix_w_in', 'grad_hg_out_norm', 'grad_mla_q_a_norm', 'grad_mla_w_uq', 'grad_mla_kv_a_norm', 'grad_mla_w_ukv', 'grad_mla_qn_nope', 'grad_mla_qn_rope', 'grad_mla_kn_nope', 'grad_mla_kn_rope', 'grad_mix_w_out', 'grad_s5_lam_re', 'grad_s5_lam_im', 'grad_s5_log_dt', 'grad_s5_b_re', 'grad_s5_b_im', 'grad_s5_c_re', 'grad_s5_c_im', 'grad_s5_d', 'grad_s5_w_glu_a', 'grad_s5_w_glu_b', 'delta_norm_mix', 'delta_norm_xa', 'delta_norm_mem', 'delta_norm_ffn', 'delta_xa_wq', 'delta_xa_wk', 'delta_xa_wv', 'delta_xa_wo', 'delta_xa_q_norm', 'delta_xa_k_norm', 'delta_ffn_w_up', 'delta_ffn_conv_w', 'delta_ffn_conv_b', 'delta_ffn_w_down', 'delta_hg_lb_logits', 'delta_mix_w_in', 'delta_hg_out_norm', 'delta_mla_q_a_norm', 'delta_mla_w_uq', 'delta_mla_kv_a_norm', 'delta_mla_w_ukv', 'delta_mla_qn_nope', 'delta_mla_qn_rope', 'delta_mla_kn_nope', 'delta_mla_kn_rope', 'delta_mix_w_out', 'delta_s5_lam_re', 'delta_s5_lam_im', 'delta_s5_log_dt', 'delta_s5_b_re', 'delta_s5_b_im', 'delta_s5_c_re', 'delta_s5_c_im', 'delta_s5_d', 'delta_s5_w_glu_a', 'delta_s5_w_glu_b', 'new_m_norm_mix', 'new_m_norm_xa', 'new_m_norm_mem', 'new_m_norm_ffn', 'new_m_xa_wq', 'new_m_xa_wk', 'new_m_xa_wv', 'new_m_xa_wo', 'new_m_xa_q_norm', 'new_m_xa_k_norm', 'new_m_ffn_w_up', 'new_m_ffn_conv_w', 'new_m_ffn_conv_b', 'new_m_ffn_w_down', 'new_m_hg_lb_logits', 'new_m_mix_w_in', 'new_m_hg_out_norm', 'new_m_mla_q_a_norm', 'new_m_mla_w_uq', 'new_m_mla_kv_a_norm', 'new_m_mla_w_ukv', 'new_m_mla_qn_nope', 'new_m_mla_qn_rope', 'new_m_mla_kn_nope', 'new_m_mla_kn_rope', 'new_m_mix_w_out', 'new_m_s5_lam_re', 'new_m_s5_lam_im', 'new_m_s5_log_dt', 'new_m_s5_b_re', 'new_m_s5_b_im', 'new_m_s5_c_re', 'new_m_s5_c_im', 'new_m_s5_d', 'new_m_s5_w_glu_a', 'new_m_s5_w_glu_b', 'new_v_norm_mix', 'new_v_norm_xa', 'new_v_norm_mem', 'new_v_norm_ffn', 'new_v_xa_wq', 'new_v_xa_wk', 'new_v_xa_wv', 'new_v_xa_wo', 'new_v_xa_q_norm', 'new_v_xa_k_norm', 'new_v_ffn_w_up', 'new_v_ffn_conv_w', 'new_v_ffn_conv_b', 'new_v_ffn_w_down', 'new_v_hg_lb_logits', 'new_v_mix_w_in', 'new_v_hg_out_norm', 'new_v_mla_q_a_norm', 'new_v_mla_w_uq', 'new_v_mla_kv_a_norm', 'new_v_mla_w_ukv', 'new_v_mla_qn_nope', 'new_v_mla_qn_rope', 'new_v_mla_kn_nope', 'new_v_mla_kn_rope', 'new_v_mix_w_out', 'new_v_s5_lam_re', 'new_v_s5_lam_im', 'new_v_s5_log_dt', 'new_v_s5_b_re', 'new_v_s5_b_im', 'new_v_s5_c_re', 'new_v_s5_c_im', 'new_v_s5_d', 'new_v_s5_w_glu_a', 'new_v_s5_w_glu_b']
TWIN_LEAF_KINDS = {'loss': 'loss', 'grad_x': 'grad_x', 'grad_norm_mix': 'grad_w', 'grad_norm_xa': 'grad_w', 'grad_norm_mem': 'grad_w', 'grad_norm_ffn': 'grad_w', 'grad_xa_wq': 'grad_w', 'grad_xa_wk': 'grad_w', 'grad_xa_wv': 'grad_w', 'grad_xa_wo': 'grad_w', 'grad_xa_q_norm': 'grad_w', 'grad_xa_k_norm': 'grad_w', 'grad_ffn_w_up': 'grad_w', 'grad_ffn_conv_w': 'grad_w', 'grad_ffn_conv_b': 'grad_w', 'grad_ffn_w_down': 'grad_w', 'grad_hg_lb_logits': 'grad_w', 'grad_mix_w_in': 'grad_w', 'grad_hg_out_norm': 'grad_w', 'grad_mla_q_a_norm': 'grad_w', 'grad_mla_w_uq': 'grad_w', 'grad_mla_kv_a_norm': 'grad_w', 'grad_mla_w_ukv': 'grad_w', 'grad_mla_qn_nope': 'grad_w', 'grad_mla_qn_rope': 'grad_w', 'grad_mla_kn_nope': 'grad_w', 'grad_mla_kn_rope': 'grad_w', 'grad_mix_w_out': 'grad_w', 'grad_s5_lam_re': 'grad_w', 'grad_s5_lam_im': 'grad_w', 'grad_s5_log_dt': 'grad_w', 'grad_s5_b_re': 'grad_w', 'grad_s5_b_im': 'grad_w', 'grad_s5_c_re': 'grad_w', 'grad_s5_c_im': 'grad_w', 'grad_s5_d': 'grad_w', 'grad_s5_w_glu_a': 'grad_w', 'grad_s5_w_glu_b': 'grad_w', 'delta_norm_mix': 'delta_w', 'delta_norm_xa': 'delta_w', 'delta_norm_mem': 'delta_w', 'delta_norm_ffn': 'delta_w', 'delta_xa_wq': 'delta_w', 'delta_xa_wk': 'delta_w', 'delta_xa_wv': 'delta_w', 'delta_xa_wo': 'delta_w', 'delta_xa_q_norm': 'delta_w', 'delta_xa_k_norm': 'delta_w', 'delta_ffn_w_up': 'delta_w', 'delta_ffn_conv_w': 'delta_w', 'delta_ffn_conv_b': 'delta_w', 'delta_ffn_w_down': 'delta_w', 'delta_hg_lb_logits': 'delta_w', 'delta_mix_w_in': 'delta_w', 'delta_hg_out_norm': 'delta_w', 'delta_mla_q_a_norm': 'delta_w', 'delta_mla_w_uq': 'delta_w', 'delta_mla_kv_a_norm': 'delta_w', 'delta_mla_w_ukv': 'delta_w', 'delta_mla_qn_nope': 'delta_w', 'delta_mla_qn_rope': 'delta_w', 'delta_mla_kn_nope': 'delta_w', 'delta_mla_kn_rope': 'delta_w', 'delta_mix_w_out': 'delta_w', 'delta_s5_lam_re': 'delta_w', 'delta_s5_lam_im': 'delta_w', 'delta_s5_log_dt': 'delta_w', 'delta_s5_b_re': 'delta_w', 'delta_s5_b_im': 'delta_w', 'delta_s5_c_re': 'delta_w', 'delta_s5_c_im': 'delta_w', 'delta_s5_d': 'delta_w', 'delta_s5_w_glu_a': 'delta_w', 'delta_s5_w_glu_b': 'delta_w', 'new_m_norm_mix': 'new_m', 'new_m_norm_xa': 'new_m', 'new_m_norm_mem': 'new_m', 'new_m_norm_ffn': 'new_m', 'new_m_xa_wq': 'new_m', 'new_m_xa_wk': 'new_m', 'new_m_xa_wv': 'new_m', 'new_m_xa_wo': 'new_m', 'new_m_xa_q_norm': 'new_m', 'new_m_xa_k_norm': 'new_m', 'new_m_ffn_w_up': 'new_m', 'new_m_ffn_conv_w': 'new_m', 'new_m_ffn_conv_b': 'new_m', 'new_m_ffn_w_down': 'new_m', 'new_m_hg_lb_logits': 'new_m', 'new_m_mix_w_in': 'new_m', 'new_m_hg_out_norm': 'new_m', 'new_m_mla_q_a_norm': 'new_m', 'new_m_mla_w_uq': 'new_m', 'new_m_mla_kv_a_norm': 'new_m', 'new_m_mla_w_ukv': 'new_m', 'new_m_mla_qn_nope': 'new_m', 'new_m_mla_qn_rope': 'new_m', 'new_m_mla_kn_nope': 'new_m', 'new_m_mla_kn_rope': 'new_m', 'new_m_mix_w_out': 'new_m', 'new_m_s5_lam_re': 'new_m', 'new_m_s5_lam_im': 'new_m', 'new_m_s5_log_dt': 'new_m', 'new_m_s5_b_re': 'new_m', 'new_m_s5_b_im': 'new_m', 'new_m_s5_c_re': 'new_m', 'new_m_s5_c_im': 'new_m', 'new_m_s5_d': 'new_m', 'new_m_s5_w_glu_a': 'new_m', 'new_m_s5_w_glu_b': 'new_m', 'new_v_norm_mix': 'new_v', 'new_v_norm_xa': 'new_v', 'new_v_norm_mem': 'new_v', 'new_v_norm_ffn': 'new_v', 'new_v_xa_wq': 'new_v', 'new_v_xa_wk': 'new_v', 'new_v_xa_wv': 'new_v', 'new_v_xa_wo': 'new_v', 'new_v_xa_q_norm': 'new_v', 'new_v_xa_k_norm': 'new_v', 'new_v_ffn_w_up': 'new_v', 'new_v_ffn_conv_w': 'new_v', 'new_v_ffn_conv_b': 'new_v', 'new_v_ffn_w_down': 'new_v', 'new_v_hg_lb_logits': 'new_v', 'new_v_mix_w_in': 'new_v', 'new_v_hg_out_norm': 'new_v', 'new_v_mla_q_a_norm': 'new_v', 'new_v_mla_w_uq': 'new_v', 'new_v_mla_kv_a_norm': 'new_v', 'new_v_mla_w_ukv': 'new_v', 'new_v_mla_qn_nope': 'new_v', 'new_v_mla_qn_rope': 'new_v', 'new_v_mla_kn_nope': 'new_v', 'new_v_mla_kn_rope': 'new_v', 'new_v_mix_w_out': 'new_v', 'new_v_s5_lam_re': 'new_v', 'new_v_s5_lam_im': 'new_v', 'new_v_s5_log_dt': 'new_v', 'new_v_s5_b_re': 'new_v', 'new_v_s5_b_im': 'new_v', 'new_v_s5_c_re': 'new_v', 'new_v_s5_c_im': 'new_v', 'new_v_s5_d': 'new_v', 'new_v_s5_w_glu_a': 'new_v', 'new_v_s5_w_glu_b': 'new_v'}


def _forward(args):
    return _fwd_reference(*[args[k] for k in FWD_PARAMS])


def _output_shape():
    out = _jax.eval_shape(lambda: _forward(_fwd_setup_inputs(0)))
    return out.shape, out.dtype

N_MICROBATCH = 1
ADAM_LR = 0.001
ADAM_B1 = 0.9
ADAM_B2 = 0.999
ADAM_EPS = 1e-08
ADAM_WD = 0.01
ADAM_STEP = 10
PER_EXAMPLE_BATCH_AXIS = {'x': 0, 'mem': 0, 'positions': 0, 'loss_target': 0}
SHARED_INPUTS = []
_WEIGHT_DTYPES = {'norm_mix': _jnp.float32, 'norm_xa': _jnp.float32, 'norm_mem': _jnp.float32, 'norm_ffn': _jnp.float32, 'xa_wq': _jnp.float32, 'xa_wk': _jnp.float32, 'xa_wv': _jnp.float32, 'xa_wo': _jnp.float32, 'xa_q_norm': _jnp.float32, 'xa_k_norm': _jnp.float32, 'ffn_w_up': _jnp.float32, 'ffn_conv_w': _jnp.float32, 'ffn_conv_b': _jnp.float32, 'ffn_w_down': _jnp.float32, 'hg_lb_logits': _jnp.float32, 'mix_w_in': _jnp.float32, 'hg_out_norm': _jnp.float32, 'mla_q_a_norm': _jnp.float32, 'mla_w_uq': _jnp.float32, 'mla_kv_a_norm': _jnp.float32, 'mla_w_ukv': _jnp.float32, 'mla_qn_nope': _jnp.float32, 'mla_qn_rope': _jnp.float32, 'mla_kn_nope': _jnp.float32, 'mla_kn_rope': _jnp.float32, 'mix_w_out': _jnp.float32, 's5_lam_re': _jnp.float32, 's5_lam_im': _jnp.float32, 's5_log_dt': _jnp.float32, 's5_b_re': _jnp.float32, 's5_b_im': _jnp.float32, 's5_c_re': _jnp.float32, 's5_c_im': _jnp.float32, 's5_d': _jnp.float32, 's5_w_glu_a': _jnp.float32, 's5_w_glu_b': _jnp.float32}
MOMENT_SCALE = {'norm_mix': 7.796455e+00, 'norm_xa': 7.479997e-02, 'norm_mem': 4.923922e-01, 'norm_ffn': 2.525463e+01, 'xa_wq': 7.318518e-02, 'xa_wk': 7.316564e-02, 'xa_wv': 3.629433e-01, 'xa_wo': 3.467663e-01, 'xa_q_norm': 1.273450e+00, 'xa_k_norm': 1.268736e+00, 'ffn_w_up': 4.014301e-01, 'ffn_conv_w': 3.572628e+00, 'ffn_conv_b': 3.285003e+00, 'ffn_w_down': 3.828127e-01, 'hg_lb_logits': 3.892806e-02, 'mix_w_in': 3.000295e-01, 'hg_out_norm': 1.076507e+01, 'mla_q_a_norm': 1.762339e-01, 'mla_w_uq': 1.042737e-01, 'mla_kv_a_norm': 1.184837e+00, 'mla_w_ukv': 1.604460e-01, 'mla_qn_nope': 5.602998e-01, 'mla_qn_rope': 4.293870e-01, 'mla_kn_nope': 5.638407e-01, 'mla_kn_rope': 4.418940e-01, 'mix_w_out': 3.412490e-01, 's5_lam_re': 3.593188e-02, 's5_lam_im': 1.893825e-02, 's5_log_dt': 1.252592e+01, 's5_b_re': 1.758583e-02, 's5_b_im': 1.753040e-02, 's5_c_re': 1.971100e-02, 's5_c_im': 2.227926e-02, 's5_d': 5.203361e+00, 's5_w_glu_a': 1.813149e+00, 's5_w_glu_b': 6.532006e-01}


def _to_microbatches(a, axis):
    t = _jnp.moveaxis(a, axis, 0)
    t = t.reshape((N_MICROBATCH, t.shape[0] // N_MICROBATCH) + t.shape[1:])
    return _jnp.moveaxis(t, 1, axis + 1)


def setup_inputs(seed: int = 0) -> dict:
    inp = _fwd_setup_inputs(seed)
    key = _jax.random.fold_in(_jax.random.key(seed), 7919)
    shape, _ = _output_shape()
    out = dict(inp)
    out["loss_target"] = _jax.random.normal(_jax.random.fold_in(key, 0), shape, _jnp.float32)
    for i, name in enumerate(TWIN_WEIGHTS):
        w = inp[name].astype(_jnp.float32)
        if MOMENT_SCALE is None:
            s = _jnp.sqrt(_jnp.mean(_jnp.square(w)) + 1e-30)
        else:
            s = MOMENT_SCALE[name]
        km, kv = _jax.random.split(_jax.random.fold_in(key, i + 1))
        out[name] = w
        out["m_" + name] = s * _jax.random.normal(km, w.shape, _jnp.float32)
        out["v_" + name] = (s * s) * _jax.random.uniform(kv, w.shape, _jnp.float32, 0.5, 1.5)
    if N_MICROBATCH > 1:
        for name, axis in PER_EXAMPLE_BATCH_AXIS.items():
            out[name] = _to_microbatches(out[name], axis)
    return {'x': out['x'], 'mem': out['mem'], 'positions': out['positions'], 'norm_mix': out['norm_mix'], 'norm_xa': out['norm_xa'], 'norm_mem': out['norm_mem'], 'norm_ffn': out['norm_ffn'], 'xa_wq': out['xa_wq'], 'xa_wk': out['xa_wk'], 'xa_wv': out['xa_wv'], 'xa_wo': out['xa_wo'], 'xa_q_norm': out['xa_q_norm'], 'xa_k_norm': out['xa_k_norm'], 'ffn_w_up': out['ffn_w_up'], 'ffn_conv_w': out['ffn_conv_w'], 'ffn_conv_b': out['ffn_conv_b'], 'ffn_w_down': out['ffn_w_down'], 'hg_lb_logits': out['hg_lb_logits'], 'mix_w_in': out['mix_w_in'], 'hg_out_norm': out['hg_out_norm'], 'mla_q_a_norm': out['mla_q_a_norm'], 'mla_w_uq': out['mla_w_uq'], 'mla_kv_a_norm': out['mla_kv_a_norm'], 'mla_w_ukv': out['mla_w_ukv'], 'mla_qn_nope': out['mla_qn_nope'], 'mla_qn_rope': out['mla_qn_rope'], 'mla_kn_nope': out['mla_kn_nope'], 'mla_kn_rope': out['mla_kn_rope'], 'mix_w_out': out['mix_w_out'], 's5_lam_re': out['s5_lam_re'], 's5_lam_im': out['s5_lam_im'], 's5_log_dt': out['s5_log_dt'], 's5_b_re': out['s5_b_re'], 's5_b_im': out['s5_b_im'], 's5_c_re': out['s5_c_re'], 's5_c_im': out['s5_c_im'], 's5_d': out['s5_d'], 's5_w_glu_a': out['s5_w_glu_a'], 's5_w_glu_b': out['s5_w_glu_b'], 'loss_target': out['loss_target'], 'm_norm_mix': out['m_norm_mix'], 'm_norm_xa': out['m_norm_xa'], 'm_norm_mem': out['m_norm_mem'], 'm_norm_ffn': out['m_norm_ffn'], 'm_xa_wq': out['m_xa_wq'], 'm_xa_wk': out['m_xa_wk'], 'm_xa_wv': out['m_xa_wv'], 'm_xa_wo': out['m_xa_wo'], 'm_xa_q_norm': out['m_xa_q_norm'], 'm_xa_k_norm': out['m_xa_k_norm'], 'm_ffn_w_up': out['m_ffn_w_up'], 'm_ffn_conv_w': out['m_ffn_conv_w'], 'm_ffn_conv_b': out['m_ffn_conv_b'], 'm_ffn_w_down': out['m_ffn_w_down'], 'm_hg_lb_logits': out['m_hg_lb_logits'], 'm_mix_w_in': out['m_mix_w_in'], 'm_hg_out_norm': out['m_hg_out_norm'], 'm_mla_q_a_norm': out['m_mla_q_a_norm'], 'm_mla_w_uq': out['m_mla_w_uq'], 'm_mla_kv_a_norm': out['m_mla_kv_a_norm'], 'm_mla_w_ukv': out['m_mla_w_ukv'], 'm_mla_qn_nope': out['m_mla_qn_nope'], 'm_mla_qn_rope': out['m_mla_qn_rope'], 'm_mla_kn_nope': out['m_mla_kn_nope'], 'm_mla_kn_rope': out['m_mla_kn_rope'], 'm_mix_w_out': out['m_mix_w_out'], 'm_s5_lam_re': out['m_s5_lam_re'], 'm_s5_lam_im': out['m_s5_lam_im'], 'm_s5_log_dt': out['m_s5_log_dt'], 'm_s5_b_re': out['m_s5_b_re'], 'm_s5_b_im': out['m_s5_b_im'], 'm_s5_c_re': out['m_s5_c_re'], 'm_s5_c_im': out['m_s5_c_im'], 'm_s5_d': out['m_s5_d'], 'm_s5_w_glu_a': out['m_s5_w_glu_a'], 'm_s5_w_glu_b': out['m_s5_w_glu_b'], 'v_norm_mix': out['v_norm_mix'], 'v_norm_xa': out['v_norm_xa'], 'v_norm_mem': out['v_norm_mem'], 'v_norm_ffn': out['v_norm_ffn'], 'v_xa_wq': out['v_xa_wq'], 'v_xa_wk': out['v_xa_wk'], 'v_xa_wv': out['v_xa_wv'], 'v_xa_wo': out['v_xa_wo'], 'v_xa_q_norm': out['v_xa_q_norm'], 'v_xa_k_norm': out['v_xa_k_norm'], 'v_ffn_w_up': out['v_ffn_w_up'], 'v_ffn_conv_w': out['v_ffn_conv_w'], 'v_ffn_conv_b': out['v_ffn_conv_b'], 'v_ffn_w_down': out['v_ffn_w_down'], 'v_hg_lb_logits': out['v_hg_lb_logits'], 'v_mix_w_in': out['v_mix_w_in'], 'v_hg_out_norm': out['v_hg_out_norm'], 'v_mla_q_a_norm': out['v_mla_q_a_norm'], 'v_mla_w_uq': out['v_mla_w_uq'], 'v_mla_kv_a_norm': out['v_mla_kv_a_norm'], 'v_mla_w_ukv': out['v_mla_w_ukv'], 'v_mla_qn_nope': out['v_mla_qn_nope'], 'v_mla_qn_rope': out['v_mla_qn_rope'], 'v_mla_kn_nope': out['v_mla_kn_nope'], 'v_mla_kn_rope': out['v_mla_kn_rope'], 'v_mix_w_out': out['v_mix_w_out'], 'v_s5_lam_re': out['v_s5_lam_re'], 'v_s5_lam_im': out['v_s5_lam_im'], 'v_s5_log_dt': out['v_s5_log_dt'], 'v_s5_b_re': out['v_s5_b_re'], 'v_s5_b_im': out['v_s5_b_im'], 'v_s5_c_re': out['v_s5_c_re'], 'v_s5_c_im': out['v_s5_c_im'], 'v_s5_d': out['v_s5_d'], 'v_s5_w_glu_a': out['v_s5_w_glu_a'], 'v_s5_w_glu_b': out['v_s5_w_glu_b']}


def _loss(weights, diff, rest, loss_target):
    with _jax.named_scope("forward"):
        args = {**rest, TWIN_DIFF_INPUT: diff, **{k: w.astype(_WEIGHT_DTYPES[k]) for k, w in weights.items()}}
        y = _forward(args)
    with _jax.named_scope("loss_head"):
        err = _jnp.square(y.astype(_jnp.float32) - loss_target)
        return 0.5 * _jnp.sum(_jnp.mean(err, axis=-1)) if err.ndim else 0.5 * err


def _adamw(w, g, m, v):
    m = ADAM_B1 * m + (1.0 - ADAM_B1) * g
    v = ADAM_B2 * v + (1.0 - ADAM_B2) * _jnp.square(g)
    m_hat = m / (1.0 - ADAM_B1 ** ADAM_STEP)
    v_hat = v / (1.0 - ADAM_B2 ** ADAM_STEP)
    delta = -ADAM_LR * (m_hat / (_jnp.sqrt(v_hat) + ADAM_EPS) + ADAM_WD * w)
    return delta, m, v


def reference(x, mem, positions, norm_mix, norm_xa, norm_mem, norm_ffn, xa_wq, xa_wk, xa_wv, xa_wo, xa_q_norm, xa_k_norm, ffn_w_up, ffn_conv_w, ffn_conv_b, ffn_w_down, hg_lb_logits, mix_w_in, hg_out_norm, mla_q_a_norm, mla_w_uq, mla_kv_a_norm, mla_w_ukv, mla_qn_nope, mla_qn_rope, mla_kn_nope, mla_kn_rope, mix_w_out, s5_lam_re, s5_lam_im, s5_log_dt, s5_b_re, s5_b_im, s5_c_re, s5_c_im, s5_d, s5_w_glu_a, s5_w_glu_b, loss_target, m_norm_mix, m_norm_xa, m_norm_mem, m_norm_ffn, m_xa_wq, m_xa_wk, m_xa_wv, m_xa_wo, m_xa_q_norm, m_xa_k_norm, m_ffn_w_up, m_ffn_conv_w, m_ffn_conv_b, m_ffn_w_down, m_hg_lb_logits, m_mix_w_in, m_hg_out_norm, m_mla_q_a_norm, m_mla_w_uq, m_mla_kv_a_norm, m_mla_w_ukv, m_mla_qn_nope, m_mla_qn_rope, m_mla_kn_nope, m_mla_kn_rope, m_mix_w_out, m_s5_lam_re, m_s5_lam_im, m_s5_log_dt, m_s5_b_re, m_s5_b_im, m_s5_c_re, m_s5_c_im, m_s5_d, m_s5_w_glu_a, m_s5_w_glu_b, v_norm_mix, v_norm_xa, v_norm_mem, v_norm_ffn, v_xa_wq, v_xa_wk, v_xa_wv, v_xa_wo, v_xa_q_norm, v_xa_k_norm, v_ffn_w_up, v_ffn_conv_w, v_ffn_conv_b, v_ffn_w_down, v_hg_lb_logits, v_mix_w_in, v_hg_out_norm, v_mla_q_a_norm, v_mla_w_uq, v_mla_kv_a_norm, v_mla_w_ukv, v_mla_qn_nope, v_mla_qn_rope, v_mla_kn_nope, v_mla_kn_rope, v_mix_w_out, v_s5_lam_re, v_s5_lam_im, v_s5_log_dt, v_s5_b_re, v_s5_b_im, v_s5_c_re, v_s5_c_im, v_s5_d, v_s5_w_glu_a, v_s5_w_glu_b):
    given = dict(x=x, mem=mem, positions=positions, norm_mix=norm_mix, norm_xa=norm_xa, norm_mem=norm_mem, norm_ffn=norm_ffn, xa_wq=xa_wq, xa_wk=xa_wk, xa_wv=xa_wv, xa_wo=xa_wo, xa_q_norm=xa_q_norm, xa_k_norm=xa_k_norm, ffn_w_up=ffn_w_up, ffn_conv_w=ffn_conv_w, ffn_conv_b=ffn_conv_b, ffn_w_down=ffn_w_down, hg_lb_logits=hg_lb_logits, mix_w_in=mix_w_in, hg_out_norm=hg_out_norm, mla_q_a_norm=mla_q_a_norm, mla_w_uq=mla_w_uq, mla_kv_a_norm=mla_kv_a_norm, mla_w_ukv=mla_w_ukv, mla_qn_nope=mla_qn_nope, mla_qn_rope=mla_qn_rope, mla_kn_nope=mla_kn_nope, mla_kn_rope=mla_kn_rope, mix_w_out=mix_w_out, s5_lam_re=s5_lam_re, s5_lam_im=s5_lam_im, s5_log_dt=s5_log_dt, s5_b_re=s5_b_re, s5_b_im=s5_b_im, s5_c_re=s5_c_re, s5_c_im=s5_c_im, s5_d=s5_d, s5_w_glu_a=s5_w_glu_a, s5_w_glu_b=s5_w_glu_b, loss_target=loss_target, m_norm_mix=m_norm_mix, m_norm_xa=m_norm_xa, m_norm_mem=m_norm_mem, m_norm_ffn=m_norm_ffn, m_xa_wq=m_xa_wq, m_xa_wk=m_xa_wk, m_xa_wv=m_xa_wv, m_xa_wo=m_xa_wo, m_xa_q_norm=m_xa_q_norm, m_xa_k_norm=m_xa_k_norm, m_ffn_w_up=m_ffn_w_up, m_ffn_conv_w=m_ffn_conv_w, m_ffn_conv_b=m_ffn_conv_b, m_ffn_w_down=m_ffn_w_down, m_hg_lb_logits=m_hg_lb_logits, m_mix_w_in=m_mix_w_in, m_hg_out_norm=m_hg_out_norm, m_mla_q_a_norm=m_mla_q_a_norm, m_mla_w_uq=m_mla_w_uq, m_mla_kv_a_norm=m_mla_kv_a_norm, m_mla_w_ukv=m_mla_w_ukv, m_mla_qn_nope=m_mla_qn_nope, m_mla_qn_rope=m_mla_qn_rope, m_mla_kn_nope=m_mla_kn_nope, m_mla_kn_rope=m_mla_kn_rope, m_mix_w_out=m_mix_w_out, m_s5_lam_re=m_s5_lam_re, m_s5_lam_im=m_s5_lam_im, m_s5_log_dt=m_s5_log_dt, m_s5_b_re=m_s5_b_re, m_s5_b_im=m_s5_b_im, m_s5_c_re=m_s5_c_re, m_s5_c_im=m_s5_c_im, m_s5_d=m_s5_d, m_s5_w_glu_a=m_s5_w_glu_a, m_s5_w_glu_b=m_s5_w_glu_b, v_norm_mix=v_norm_mix, v_norm_xa=v_norm_xa, v_norm_mem=v_norm_mem, v_norm_ffn=v_norm_ffn, v_xa_wq=v_xa_wq, v_xa_wk=v_xa_wk, v_xa_wv=v_xa_wv, v_xa_wo=v_xa_wo, v_xa_q_norm=v_xa_q_norm, v_xa_k_norm=v_xa_k_norm, v_ffn_w_up=v_ffn_w_up, v_ffn_conv_w=v_ffn_conv_w, v_ffn_conv_b=v_ffn_conv_b, v_ffn_w_down=v_ffn_w_down, v_hg_lb_logits=v_hg_lb_logits, v_mix_w_in=v_mix_w_in, v_hg_out_norm=v_hg_out_norm, v_mla_q_a_norm=v_mla_q_a_norm, v_mla_w_uq=v_mla_w_uq, v_mla_kv_a_norm=v_mla_kv_a_norm, v_mla_w_ukv=v_mla_w_ukv, v_mla_qn_nope=v_mla_qn_nope, v_mla_qn_rope=v_mla_qn_rope, v_mla_kn_nope=v_mla_kn_nope, v_mla_kn_rope=v_mla_kn_rope, v_mix_w_out=v_mix_w_out, v_s5_lam_re=v_s5_lam_re, v_s5_lam_im=v_s5_lam_im, v_s5_log_dt=v_s5_log_dt, v_s5_b_re=v_s5_b_re, v_s5_b_im=v_s5_b_im, v_s5_c_re=v_s5_c_re, v_s5_c_im=v_s5_c_im, v_s5_d=v_s5_d, v_s5_w_glu_a=v_s5_w_glu_a, v_s5_w_glu_b=v_s5_w_glu_b)
    weights = {n: given[n] for n in TWIN_WEIGHTS}
    shared = {n: given[n] for n in SHARED_INPUTS}
    per_example = {n: given[n] for n in ['x', 'mem', 'positions']}
    grad_fn = _jax.value_and_grad(_loss, argnums=(0, 1))

    def one_microbatch(ex, loss_target):
        ex = dict(ex)
        diff = ex.pop(TWIN_DIFF_INPUT)
        return grad_fn(weights, diff, {**shared, **ex}, loss_target)

    if N_MICROBATCH == 1:
        loss, (grad_w, grad_x) = one_microbatch(per_example, given["loss_target"])
    else:
        def body(carry, xs):
            loss_sum, grad_sum = carry
            l_k, (gw_k, gx_k) = one_microbatch(xs[0], xs[1])
            with _jax.named_scope("update"):
                return (loss_sum + l_k, _jax.tree.map(_jnp.add, grad_sum, gw_k)), gx_k

        init = (_jnp.zeros((), _jnp.float32), _jax.tree.map(_jnp.zeros_like, weights))
        (loss, grad_w), grad_x = _jax.lax.scan(body, init, (per_example, given["loss_target"]))
    with _jax.named_scope("update"):
        delta_w, new_m, new_v = {}, {}, {}
        for n in TWIN_WEIGHTS:
            delta_w[n], new_m[n], new_v[n] = _adamw(weights[n], grad_w[n], given["m_" + n], given["v_" + n])
    return (loss, grad_x, *[grad_w[n] for n in TWIN_WEIGHTS], *[delta_w[n] for n in TWIN_WEIGHTS],
            *[new_m[n] for n in TWIN_WEIGHTS], *[new_v[n] for n in TWIN_WEIGHTS])
```

```python
import functools
import math

import jax
import jax.numpy as jnp
import numpy as np
from jax import lax
from jax.experimental import pallas as pl
from jax.experimental.pallas import tpu as pltpu

F32 = jnp.float32
BF16 = jnp.bfloat16
MXU_DTYPE = BF16
HI = lax.Precision.HIGHEST
V7X_VMEM_LIMIT_BYTES = 56 * 1024 * 1024
EPS = 1e-6
MESH = pl.DeviceIdType.MESH

HG_HEADS, HG_DIM = 4, 128
HG_WIDTH = HG_HEADS * HG_DIM
HG_SUB = 16
HG_BLOCK = 64
MLA_HEADS, MLA_Q_RANK, MLA_KV_RANK = 4, 256, 128
MLA_NOPE, MLA_ROPE, MLA_V = 128, 64, 128
MLA_QK = MLA_NOPE + MLA_ROPE
MLA_DK = 256
ROPE_BASE = 10000.0
IN_WIDTH = 4 * HG_WIDTH + MLA_Q_RANK + MLA_KV_RANK + MLA_ROPE
IN_PAD = 4 * HG_WIDTH + MLA_Q_RANK + MLA_KV_RANK + 128
S5_GROUP, S5_STATE = 16, 64
S5_GB = 8
DT_MIN, DT_MAX = 1e-3, 1e-1
XA_HEADS = 4
CONV_W = 3
ADAM_LR, ADAM_B1, ADAM_B2, ADAM_EPS, ADAM_WD, ADAM_STEP = 0.001, 0.9, 0.999, 1e-08, 0.01, 10


def _cparams(sem):
    return pltpu.CompilerParams(dimension_semantics=sem, vmem_limit_bytes=V7X_VMEM_LIMIT_BYTES)


class Opd:
    def __init__(self, arr, block, imap, grad=None, gshape=None, gimap=None):
        self.arr, self.block, self.imap, self.grad = arr, block, imap, grad
        self.gshape = arr.shape if gshape is None else gshape
        self.gimap = imap if gimap is None else gimap

    def spec(self):
        return pl.BlockSpec(self.block, self.imap)

    def gspec(self):
        return pl.BlockSpec(self.block, self.gimap)


def rows(arr, tm, grad=None, col=0, width=None):
    width = arr.shape[1] if width is None else width
    return Opd(arr, (tm, width), lambda i, c=col: (i, c), grad, (arr.shape[0], width), lambda i: (i, 0))


def cols(arr, tn, grad=None):
    return Opd(arr, (arr.shape[0], tn), lambda j: (0, j), grad)


def full(arr, grad=None):
    return Opd(arr, arr.shape, lambda i: (0, 0), grad)


def _load(ref):
    v = ref[...]
    return v.astype(F32) if jnp.issubdtype(v.dtype, jnp.floating) else v


def blocked_fwd(f, opds, outs, n, name):
    n_in = len(opds)

    def body(*refs):
        ys = f(*[_load(r) for r in refs[:n_in]])
        for r, y in zip(refs[n_in:], ys):
            r[...] = y.astype(r.dtype)

    res = pl.pallas_call(
        body, name=name, grid=(n,),
        in_specs=[o.spec() for o in opds],
        out_specs=[pl.BlockSpec(b, m) for (_, _, b, m) in outs],
        out_shape=[jax.ShapeDtypeStruct(s, d) for (s, d, _, _) in outs],
        compiler_params=_cparams(("parallel",)),
    )(*[o.arr for o in opds])
    return res


def blocked_bwd(f, opds, dys, n, name):
    n_in, n_dy = len(opds), len(dys)
    diff = [i for i, o in enumerate(opds) if o.grad]

    def body(*refs):
        vals = [_load(r) for r in refs[:n_in]]

        def fd(*dv):
            allv = list(vals)
            for i, v in zip(diff, dv):
                allv[i] = v
            return tuple(f(*allv))

        ys, vjp = jax.vjp(fd, *[vals[i] for i in diff])
        cts = tuple(_load(r).astype(y.dtype) for r, y in zip(refs[n_in:n_in + n_dy], ys))
        gs = vjp(cts)
        for r, g, i in zip(refs[n_in + n_dy:], gs, diff):
            if opds[i].grad == 'acc':
                @pl.when(pl.program_id(0) == 0)
                def _(r=r):
                    r[...] = jnp.zeros(r.shape, r.dtype)
                r[...] += g.astype(r.dtype)
            else:
                r[...] = g.astype(r.dtype)

    any_acc = any(opds[i].grad == 'acc' for i in diff)
    res = pl.pallas_call(
        body, name=name, grid=(n,),
        in_specs=[o.spec() for o in opds] + [o.spec() for o in dys],
        out_specs=[opds[i].gspec() for i in diff],
        out_shape=[jax.ShapeDtypeStruct(opds[i].gshape, F32) for i in diff],
        compiler_params=_cparams(("arbitrary" if any_acc else "parallel",)),
    )(*[o.arr for o in opds], *[o.arr for o in dys])
    return res


def _tile(dim, want):
    for t in range(want - want % 128, 0, -128):
        if dim % t == 0:
            return t
    assert dim <= want, (dim, want)
    return dim


def matmul(a, b, mode="nn", out_dtype=F32, add=None, name="matmul", tm=512, tn=512, tk=512):
    if mode == "nn":
        (M, K), (K2, N) = a.shape, b.shape
    elif mode == "tn":
        (K, M), (K2, N) = a.shape, b.shape
    else:
        (M, K), (N, K2) = a.shape, b.shape
    assert K == K2, (a.shape, b.shape, mode)
    tm, tn, tk = _tile(M, tm), _tile(N, tn), _tile(K, tk)
    nk = K // tk
    if mode == "nn":
        a_spec = pl.BlockSpec((tm, tk), lambda i, j, k: (i, k))
        b_spec = pl.BlockSpec((tk, tn), lambda i, j, k: (k, j))
        dims = (((1,), (0,)), ((), ()))
    elif mode == "tn":
        a_spec = pl.BlockSpec((tk, tm), lambda i, j, k: (k, i))
        b_spec = pl.BlockSpec((tk, tn), lambda i, j, k: (k, j))
        dims = (((0,), (0,)), ((), ()))
    else:
        a_spec = pl.BlockSpec((tm, tk), lambda i, j, k: (i, k))
        b_spec = pl.BlockSpec((tn, tk), lambda i, j, k: (j, k))
        dims = (((1,), (1,)), ((), ()))
    has_add = add is not None

    def body(*refs):
        if has_add:
            a_ref, b_ref, add_ref, o_ref, acc = refs
        else:
            a_ref, b_ref, o_ref, acc = refs
        k = pl.program_id(2)

        @pl.when(k == 0)
        def _():
            acc[...] = jnp.zeros(acc.shape, F32)

        acc[...] += lax.dot_general(a_ref[...].astype(MXU_DTYPE), b_ref[...].astype(MXU_DTYPE), dims,
                                    preferred_element_type=F32)

        @pl.when(k == nk - 1)
        def _():
            r = acc[...]
            if has_add:
                r = r + add_ref[...].astype(F32)
            o_ref[...] = r.astype(o_ref.dtype)

    in_specs = [a_spec, b_spec]
    args = [a, b]
    if has_add:
        in_specs.append(pl.BlockSpec((tm, tn), lambda i, j, k: (i, j)))
        args.append(add)
    return pl.pallas_call(
        body, name=name, grid=(M // tm, N // tn, nk),
        in_specs=in_specs,
        out_specs=pl.BlockSpec((tm, tn), lambda i, j, k: (i, j)),
        out_shape=jax.ShapeDtypeStruct((M, N), out_dtype),
        scratch_shapes=[pltpu.VMEM((tm, tn), F32)],
        compiler_params=_cparams(("parallel", "parallel", "arbitrary")),
    )(*args)


def _dot(a, b, dims, precision=None):
    if precision is None:
        a, b = a.astype(MXU_DTYPE), b.astype(MXU_DTYPE)
    return lax.dot_general(a, b, (dims, ((), ())), precision=precision, preferred_element_type=F32)


_NN = ((1,), (0,))
_NT = ((1,), (1,))
_TN = ((0,), (0,))


def _rms(x, gain):
    return x * lax.rsqrt(jnp.mean(x * x, axis=-1, keepdims=True) + EPS) * gain


def _hg_block(st_t, q, fl, iv, g, lb, gain):
    row = lax.broadcasted_iota(jnp.int32, (HG_SUB, HG_SUB), 0)
    col = lax.broadcasted_iota(jnp.int32, (HG_SUB, HG_SUB), 1)
    tri = (row >= col).astype(F32)
    outs, states = [], []
    for h in range(HG_HEADS):
        sl = slice(h * HG_DIM, (h + 1) * HG_DIM)
        st = st_t[h * HG_DIM:(h + 1) * HG_DIM, :]
        lbh = lb[:, sl]
        fg = lbh + (1.0 - lbh) * jax.nn.sigmoid(fl[:, sl])
        lf, kk, qf, v = jnp.log(fg), 1.0 - fg, jax.nn.silu(q[:, sl]), iv[:, sl]
        parts = []
        for s in range(q.shape[0] // HG_SUB):
            r = slice(s * HG_SUB, (s + 1) * HG_SUB)
            b = _dot(tri, lf[r], _NN, HI)
            b_end = jnp.sum(lf[r], axis=0, keepdims=True)
            qe = qf[r] * jnp.exp(b)
            sc = _dot(qe, kk[r] * jnp.exp(-b), _NT) * tri
            parts.append(_dot(sc, v[r], _NN) + _dot(qe, st, _NT))
            st = st * jnp.exp(b_end) + _dot(v[r], kk[r] * jnp.exp(b_end - b), _TN)
        o = jnp.concatenate(parts, axis=0)
        outs.append(_rms(o, gain[:, sl]) * jax.nn.silu(g[:, sl]))
        states.append(st)
    return jnp.concatenate(states, axis=0), jnp.concatenate(outs, axis=1)


def _hg_specs(proj, nb):
    return [pl.BlockSpec((HG_BLOCK, HG_WIDTH), lambda i, c=c, f=nb: (f(i), c)) for c in range(4)]


def hgrn2_fwd(proj, lb, gain):
    L = proj.shape[0]
    n = L // HG_BLOCK

    def body(q, fl, iv, g, lb_r, gain_r, o_ref, st_ref, st):
        @pl.when(pl.program_id(0) == 0)
        def _():
            st[...] = jnp.zeros(st.shape, F32)

        st_ref[0] = st[...]
        new, o = _hg_block(st[...], q[...], fl[...], iv[...], g[...], lb_r[...], gain_r[...])
        st[...] = new
        o_ref[...] = o.astype(o_ref.dtype)

    pspec = pl.BlockSpec((1, HG_WIDTH), lambda i: (0, 0))
    return pl.pallas_call(
        body, name="hgrn2_fwd", grid=(n,),
        in_specs=_hg_specs(proj, lambda i: i) + [pspec, pspec],
        out_specs=[pl.BlockSpec((HG_BLOCK, HG_WIDTH), lambda i: (i, 0)),
                   pl.BlockSpec((1, HG_WIDTH, HG_DIM), lambda i: (i, 0, 0))],
        out_shape=[jax.ShapeDtypeStruct((L, HG_WIDTH), MXU_DTYPE),
                   jax.ShapeDtypeStruct((n, HG_WIDTH, HG_DIM), F32)],
        scratch_shapes=[pltpu.VMEM((HG_WIDTH, HG_DIM), F32)],
        compiler_params=_cparams(("arbitrary",)),
    )(proj, proj, proj, proj, lb, gain)


def hgrn2_bwd(proj, lb, gain, states, do):
    L = proj.shape[0]
    n = L // HG_BLOCK

    def body(q, fl, iv, g, lb_r, gain_r, st_r, do_r, dproj, dlb, dgain, dst):
        @pl.when(pl.program_id(0) == 0)
        def _():
            dst[...] = jnp.zeros(dst.shape, F32)
            dlb[...] = jnp.zeros(dlb.shape, F32)
            dgain[...] = jnp.zeros(dgain.shape, F32)

        _, vjp = jax.vjp(_hg_block, st_r[0], q[...], fl[...], iv[...], g[...], lb_r[...], gain_r[...])
        d_st, dq, dfl, div, dg, d_lb, d_gain = vjp((dst[...], do_r[...].astype(F32)))
        dst[...] = d_st
        dproj[:, 0 * HG_WIDTH:1 * HG_WIDTH] = dq
        dproj[:, 1 * HG_WIDTH:2 * HG_WIDTH] = dfl
        dproj[:, 2 * HG_WIDTH:3 * HG_WIDTH] = div
        dproj[:, 3 * HG_WIDTH:4 * HG_WIDTH] = dg
        dlb[...] += d_lb
        dgain[...] += d_gain

    rev = lambda i: n - 1 - i
    pspec = pl.BlockSpec((1, HG_WIDTH), lambda i: (0, 0))
    return pl.pallas_call(
        body, name="hgrn2_bwd", grid=(n,),
        in_specs=_hg_specs(proj, rev) + [pspec, pspec,
                                         pl.BlockSpec((1, HG_WIDTH, HG_DIM), lambda i: (rev(i), 0, 0)),
                                         pl.BlockSpec((HG_BLOCK, HG_WIDTH), lambda i: (rev(i), 0))],
        out_specs=[pl.BlockSpec((HG_BLOCK, 4 * HG_WIDTH), lambda i: (rev(i), 0)), pspec, pspec],
        out_shape=[jax.ShapeDtypeStruct((L, 4 * HG_WIDTH), F32),
                   jax.ShapeDtypeStruct((1, HG_WIDTH), F32), jax.ShapeDtypeStruct((1, HG_WIDTH), F32)],
        scratch_shapes=[pltpu.VMEM((HG_WIDTH, HG_DIM), F32)],
        compiler_params=_cparams(("arbitrary",)),
    )(proj, proj, proj, proj, lb, gain, states, do)


def _rope_rms(x, gain_p, cos_p, sin_p):
    n = x * lax.rsqrt(jnp.sum(x * x, axis=-1, keepdims=True) * (1.0 / MLA_ROPE) + EPS) * gain_p
    r = lax.broadcasted_iota(jnp.int32, (128, 128), 0)
    c = lax.broadcasted_iota(jnp.int32, (128, 128), 1)
    swap = (r == (c + 64) % 128).astype(F32)
    return n * cos_p + _dot(n, swap, _NN, HI) * sin_p


def _mla_prep(c_q, c_kv, kpe, cos_p, sin_p, q_a, w_uq, kv_a, w_ukv, qn_nope, qn_rope, kn_nope, kn_rope):
    q = _dot(_rms(c_q, q_a), w_uq, _NN)
    kv = _dot(_rms(c_kv, kv_a), w_ukv, _NN)
    k_pe = _rope_rms(kpe, kn_rope, cos_p, sin_p)
    qs, ks = [], []
    for h in range(MLA_HEADS):
        qs.append(_rms(q[:, h * MLA_DK:h * MLA_DK + MLA_NOPE], qn_nope))
        qs.append(_rope_rms(q[:, h * MLA_DK + MLA_NOPE:(h + 1) * MLA_DK], qn_rope, cos_p, sin_p))
        ks.append(_rms(kv[:, h * MLA_NOPE:(h + 1) * MLA_NOPE], kn_nope))
        ks.append(k_pe)
    return jnp.concatenate(qs, axis=1), jnp.concatenate(ks, axis=1), kv[:, MLA_HEADS * MLA_NOPE:]


def _mla_prep_opds(proj, cos_p, sin_p, params, tm, grads):
    g = (lambda k: k) if grads else (lambda k: None)
    c0 = 4 * HG_WIDTH
    return ([rows(proj, tm, g('blk'), col=c0 // MLA_Q_RANK, width=MLA_Q_RANK),
             rows(proj, tm, g('blk'), col=(c0 + MLA_Q_RANK) // 128, width=128),
             rows(proj, tm, g('blk'), col=(c0 + MLA_Q_RANK) // 128 + 1, width=128),
             rows(cos_p, tm), rows(sin_p, tm)] + [full(p, g('acc')) for p in params])


def mla_prep_fwd(proj, cos_p, sin_p, params, tm):
    L = proj.shape[0]
    W = MLA_HEADS * MLA_DK
    rb = lambda w: (tm, w)
    outs = [((L, W), MXU_DTYPE, rb(W), lambda i: (i, 0)), ((L, W), MXU_DTYPE, rb(W), lambda i: (i, 0)),
            ((L, MLA_HEADS * MLA_V), MXU_DTYPE, rb(MLA_HEADS * MLA_V), lambda i: (i, 0))]
    return blocked_fwd(_mla_prep, _mla_prep_opds(proj, cos_p, sin_p, params, tm, False), outs, L // tm, "mla_prep_fwd")


def mla_prep_bwd(proj, cos_p, sin_p, params, dq, dk, dv, tm):
    L = proj.shape[0]
    return blocked_bwd(_mla_prep, _mla_prep_opds(proj, cos_p, sin_p, params, tm, True),
                       [rows(dq, tm), rows(dk, tm), rows(dv, tm)], L // tm, "mla_prep_bwd")


def _causal_scores(q, k, scale, row0, col0):
    s = _dot(q, k, _NT) * scale
    row = row0 + lax.broadcasted_iota(jnp.int32, s.shape, 0)
    col = col0 + lax.broadcasted_iota(jnp.int32, s.shape, 1)
    return jnp.where(col <= row, s, -jnp.inf)


def attn_fwd(q, k, v, scale, t):
    L = q.shape[0]
    n = L // t

    def body(q_ref, k_ref, v_ref, o_ref, lse_ref):
        i = pl.program_id(1)
        qb = q_ref[...]

        def step(j, carry):
            m, l, acc = carry
            kj = k_ref[pl.ds(pl.multiple_of(j * t, t), t), :]
            vj = v_ref[pl.ds(pl.multiple_of(j * t, t), t), :]
            s = _causal_scores(qb, kj, scale, i * t, j * t)
            m_new = jnp.maximum(m, jnp.max(s, axis=-1, keepdims=True))
            p = jnp.exp(s - m_new)
            alpha = jnp.exp(m - m_new)
            return m_new, alpha * l + jnp.sum(p, axis=-1, keepdims=True), alpha * acc + _dot(p, vj, _NN)

        init = (jnp.full((t, 1), -jnp.inf, F32), jnp.zeros((t, 1), F32), jnp.zeros((t, MLA_V), F32))
        m, l, acc = lax.fori_loop(0, i + 1, step, init)
        o_ref[...] = acc / l
        lse_ref[...] = jnp.broadcast_to(m + jnp.log(l), lse_ref.shape)

    hspec = lambda rows_, w: pl.BlockSpec((rows_, w), lambda h, i: (0, h))
    bspec = lambda w: pl.BlockSpec((t, w), lambda h, i: (i, h))
    return pl.pallas_call(
        body, name="attn_fwd", grid=(MLA_HEADS, n),
        in_specs=[bspec(MLA_DK), hspec(L, MLA_DK), hspec(L, MLA_V)],
        out_specs=[bspec(MLA_V), bspec(MLA_V)],
        out_shape=[jax.ShapeDtypeStruct((L, MLA_HEADS * MLA_V), F32)] * 2,
        compiler_params=_cparams(("parallel", "parallel")),
    )(q, k, v)


def attn_bwd_dq(q, k, v, o, lse, do, scale, t):
    L = q.shape[0]
    n = L // t

    def body(q_ref, k_ref, v_ref, o_ref, lse_ref, do_ref, dq_ref):
        i = pl.program_id(1)
        qb, dob = q_ref[...], do_ref[...]
        delta = jnp.sum(dob * o_ref[...], axis=-1, keepdims=True)
        lse_c = jnp.max(lse_ref[...], axis=-1, keepdims=True)

        def step(j, dq):
            kj = k_ref[pl.ds(pl.multiple_of(j * t, t), t), :]
            vj = v_ref[pl.ds(pl.multiple_of(j * t, t), t), :]
            p = jnp.exp(_causal_scores(qb, kj, scale, i * t, j * t) - lse_c)
            ds = p * (_dot(dob, vj, _NT) - delta) * scale
            return dq + _dot(ds, kj, _NN)

        dq_ref[...] = lax.fori_loop(0, i + 1, step, jnp.zeros((t, MLA_DK), F32))

    hspec = lambda w: pl.BlockSpec((L, w), lambda h, i: (0, h))
    bspec = lambda w: pl.BlockSpec((t, w), lambda h, i: (i, h))
    return pl.pallas_call(
        body, name="attn_bwd_dq", grid=(MLA_HEADS, n),
        in_specs=[bspec(MLA_DK), hspec(MLA_DK), hspec(MLA_V), bspec(MLA_V), bspec(MLA_V), bspec(MLA_V)],
        out_specs=bspec(MLA_DK),
        out_shape=jax.ShapeDtypeStruct((L, MLA_HEADS * MLA_DK), F32),
        compiler_params=_cparams(("parallel", "parallel")),
    )(q, k, v, o, lse, do)


def attn_bwd_dkv(q, k, v, o, lse, do, scale, t):
    L = q.shape[0]
    n = L // t

    def body(q_ref, k_ref, v_ref, o_ref, lse_ref, do_ref, dk_ref, dv_ref):
        j = pl.program_id(1)
        kb, vb = k_ref[...], v_ref[...]

        def step(i, carry):
            dk, dv = carry
            r = pl.ds(pl.multiple_of(i * t, t), t)
            qi, doi = q_ref[r, :], do_ref[r, :]
            delta = jnp.sum(doi * o_ref[r, :], axis=-1, keepdims=True)
            lse_c = jnp.max(lse_ref[r, :], axis=-1, keepdims=True)
            p = jnp.exp(_causal_scores(qi, kb, scale, i * t, j * t) - lse_c)
            ds = p * (_dot(doi, vb, _NT) - delta) * scale
            return dk + _dot(ds, qi, _TN), dv + _dot(p, doi, _TN)

        dk, dv = lax.fori_loop(j, n, step, (jnp.zeros((t, MLA_DK), F32), jnp.zeros((t, MLA_V), F32)))
        dk_ref[...] = dk
        dv_ref[...] = dv

    hspec = lambda w: pl.BlockSpec((L, w), lambda h, j: (0, h))
    bspec = lambda w: pl.BlockSpec((t, w), lambda h, j: (j, h))
    return pl.pallas_call(
        body, name="attn_bwd_dkv", grid=(MLA_HEADS, n),
        in_specs=[hspec(MLA_DK), bspec(MLA_DK), bspec(MLA_V), hspec(MLA_V), hspec(MLA_V), hspec(MLA_V)],
        out_specs=[bspec(MLA_DK), bspec(MLA_V)],
        out_shape=[jax.ShapeDtypeStruct((L, MLA_HEADS * MLA_DK), F32), jax.ShapeDtypeStruct((L, MLA_HEADS * MLA_V), F32)],
        compiler_params=_cparams(("parallel", "parallel")),
    )(q, k, v, o, lse, do)


S5_LANES = S5_GB * S5_STATE


def _cmul(ar, ai, br, bi):
    return ar * br - ai * bi, ar * bi + ai * br


def _a_powers(ar, ai, reverse):
    a2 = _cmul(ar, ai, ar, ai)
    a4 = _cmul(*a2, *a2)
    row = lax.broadcasted_iota(jnp.int32, (8, ar.shape[1]), 0)
    e = (8 - row) if reverse else (row + 1)
    tr, ti = jnp.ones((8, ar.shape[1]), F32), jnp.zeros((8, ar.shape[1]), F32)
    for bit, (pr, pi) in ((1, (ar, ai)), (2, a2), (4, a4), (8, _cmul(*a4, *a4))):
        nr, ni = _cmul(tr, ti, pr, pi)
        sel = (e & bit) != 0
        tr, ti = jnp.where(sel, nr, tr), jnp.where(sel, ni, ti)
    return ((ar, ai), a2, a4), (tr, ti)


def _scan8(xr, xi, pows, table, cr, ci, reverse):
    row = lax.broadcasted_iota(jnp.int32, xr.shape, 0)
    for d, (pr, pi) in zip((1, 2, 4), pows):
        if reverse:
            keep = row < 8 - d
            sr, si = pltpu.roll(xr, 8 - d, 0), pltpu.roll(xi, 8 - d, 0)
        else:
            keep = row >= d
            sr, si = pltpu.roll(xr, d, 0), pltpu.roll(xi, d, 0)
        sr, si = jnp.where(keep, sr, 0.0), jnp.where(keep, si, 0.0)
        mr, mi = _cmul(pr, pi, sr, si)
        xr, xi = xr + mr, xi + mi
    mr, mi = _cmul(table[0], table[1], cr, ci)
    return xr + mr, xi + mi


def _row_of(x, r):
    row = lax.broadcasted_iota(jnp.int32, x.shape, 0)
    return jnp.sum(jnp.where(row == r, x, 0.0), axis=0, keepdims=True)


def _s5_scan_fwd(h_re, h_im, ar, ai, L):
    pows, table = _a_powers(ar, ai, False)

    def step(i, carry):
        r = pl.ds(pl.multiple_of(i * 8, 8), 8)
        xr, xi = _scan8(h_re[r, :], h_im[r, :], pows, table, carry[0], carry[1], False)
        h_re[r, :] = xr
        h_im[r, :] = xi
        return _row_of(xr, 7), _row_of(xi, 7)

    z = jnp.zeros((1, ar.shape[1]), F32)
    lax.fori_loop(0, L // 8, step, (z, z))


def _s5_specs(L):
    return [pl.BlockSpec((L, 128), lambda g: (0, g)),
            pl.BlockSpec((1, 128, S5_LANES), lambda g: (g, 0, 0)), pl.BlockSpec((1, 128, S5_LANES), lambda g: (g, 0, 0)),
            pl.BlockSpec((1, 1, S5_LANES), lambda g: (g, 0, 0)), pl.BlockSpec((1, 1, S5_LANES), lambda g: (g, 0, 0)),
            pl.BlockSpec((1, S5_LANES, 128), lambda g: (g, 0, 0)), pl.BlockSpec((1, S5_LANES, 128), lambda g: (g, 0, 0))]


def s5_fwd(u, w_re, w_im, a_re, a_im, c_re, c_im):
    L, D = u.shape

    def body(u_ref, wr, wi, ar, ai, cr, ci, y_ref, h_re, h_im):
        ub = u_ref[...]
        h_re[...] = _dot(ub, wr[0], _NN)
        h_im[...] = _dot(ub, wi[0], _NN)
        _s5_scan_fwd(h_re, h_im, ar[0], ai[0], L)
        y_ref[...] = _dot(h_re[...], cr[0], _NN) - _dot(h_im[...], ci[0], _NN)

    return pl.pallas_call(
        body, name="s5_fwd", grid=(D // 128,),
        in_specs=_s5_specs(L), out_specs=pl.BlockSpec((L, 128), lambda g: (0, g)),
        out_shape=jax.ShapeDtypeStruct((L, D), F32),
        scratch_shapes=[pltpu.VMEM((L, S5_LANES), F32), pltpu.VMEM((L, S5_LANES), F32)],
        compiler_params=_cparams(("parallel",)),
    )(u, w_re, w_im, a_re, a_im, c_re, c_im)


def s5_bwd(u, w_re, w_im, a_re, a_im, c_re, c_im, dy, tc):
    L, D = u.shape
    nch = L // tc

    def body(u_ref, wr, wi, ar_ref, ai_ref, cr, ci, dy_ref, du_ref, dwr, dwi, dar, dai, dcr, dci, h_re, h_im, g_re, g_im):
        ar, ai = ar_ref[0], ai_ref[0]
        ub = u_ref[...]
        h_re[...] = _dot(ub, wr[0], _NN)
        h_im[...] = _dot(ub, wi[0], _NN)
        _s5_scan_fwd(h_re, h_im, ar, ai, L)
        dyb = dy_ref[...]
        dcr[0] = _dot(h_re[...], dyb, _TN)
        dci[0] = -_dot(h_im[...], dyb, _TN)
        pows, table = _a_powers(ar, -ai, True)
        dwr[0] = jnp.zeros((128, S5_LANES), F32)
        dwi[0] = jnp.zeros((128, S5_LANES), F32)
        z1 = jnp.zeros((1, S5_LANES), F32)
        z8 = jnp.zeros((8, S5_LANES), F32)

        def chunk(cc, carry):
            c0 = pl.multiple_of((nch - 1 - cc) * tc, tc)
            rows_c = pl.ds(c0, tc)
            dyc = dy_ref[rows_c, :]
            g_re[...] = _dot(dyc, cr[0], _NT)
            g_im[...] = -_dot(dyc, ci[0], _NT)

            def step(ii, cy):
                gr_c, gi_c, acc_r, acc_i = cy
                i8 = pl.multiple_of((tc // 8 - 1 - ii) * 8, 8)
                rl = pl.ds(i8, 8)
                xr, xi = _scan8(g_re[rl, :], g_im[rl, :], pows, table, gr_c, gi_c, True)
                g_re[rl, :] = xr
                g_im[rl, :] = xi
                t0 = c0 + i8
                hb_r, hb_i = h_re[pl.ds(t0, 8), :], h_im[pl.ds(t0, 8), :]
                tp = pl.multiple_of(jnp.maximum(t0 - 8, 0), 8)
                first = (t0 > 0).astype(F32)
                pr = _row_of(h_re[pl.ds(tp, 8), :], 7) * first
                pi = _row_of(h_im[pl.ds(tp, 8), :], 7) * first
                row = lax.broadcasted_iota(jnp.int32, xr.shape, 0)
                hp_r = jnp.where(row == 0, pr, pltpu.roll(hb_r, 1, 0))
                hp_i = jnp.where(row == 0, pi, pltpu.roll(hb_i, 1, 0))
                return (_row_of(xr, 0), _row_of(xi, 0),
                        acc_r + xr * hp_r + xi * hp_i, acc_i + xi * hp_r - xr * hp_i)

            cy = lax.fori_loop(0, tc // 8, step, carry)
            uc = u_ref[rows_c, :]
            gr, gi = g_re[...], g_im[...]
            du_ref[rows_c, :] = _dot(gr, wr[0], _NT) + _dot(gi, wi[0], _NT)
            dwr[0] += _dot(uc, gr, _TN)
            dwi[0] += _dot(uc, gi, _TN)
            return cy

        _, _, acc_r, acc_i = lax.fori_loop(0, nch, chunk, (z1, z1, z8, z8))
        dar[0] = jnp.sum(acc_r, axis=0, keepdims=True)
        dai[0] = jnp.sum(acc_i, axis=0, keepdims=True)

    specs = _s5_specs(L)
    return pl.pallas_call(
        body, name="s5_bwd", grid=(D // 128,),
        in_specs=specs + [pl.BlockSpec((L, 128), lambda g: (0, g))],
        out_specs=[pl.BlockSpec((L, 128), lambda g: (0, g))] + specs[1:],
        out_shape=[jax.ShapeDtypeStruct((L, D), F32)] + [jax.ShapeDtypeStruct(x.shape, F32)
                                                        for x in (w_re, w_im, a_re, a_im, c_re, c_im)],
        scratch_shapes=[pltpu.VMEM((L, S5_LANES), F32), pltpu.VMEM((L, S5_LANES), F32),
                        pltpu.VMEM((tc, S5_LANES), F32), pltpu.VMEM((tc, S5_LANES), F32)],
        compiler_params=_cparams(("parallel",)),
    )(u, w_re, w_im, a_re, a_im, c_re, c_im, dy)


def _s5_discretize(lr, li, ldt, br, bi):
    dt = jnp.exp(ldt)
    mag = jnp.exp(lr * dt)
    ar, ai = mag * jnp.cos(li * dt), mag * jnp.sin(li * dt)
    den = lr * lr + li * li
    zr = ((ar - 1.0) * lr + ai * li) / den
    zi = (ai * lr - (ar - 1.0) * li) / den
    p = lax.broadcasted_iota(jnp.int32, (S5_STATE, S5_STATE * S5_GROUP), 0)
    c = lax.broadcasted_iota(jnp.int32, (S5_STATE, S5_STATE * S5_GROUP), 1)
    rep = (c // S5_GROUP == p).astype(F32)
    zr, zi = _dot(zr, rep, _NN, HI), _dot(zi, rep, _NN, HI)
    return ar, ai, zr * br - zi * bi, zr * bi + zi * br


def _conv_shift(x, d):
    row = lax.broadcasted_iota(jnp.int32, x.shape, 0)
    return jnp.where(row >= d, pltpu.roll(x, d, 0), 0.0)


def _conv_unshift(x, d):
    n = x.shape[0]
    row = lax.broadcasted_iota(jnp.int32, x.shape, 0)
    return jnp.where(row < n - d, pltpu.roll(x, n - d, 0), 0.0)


@functools.partial(jax.custom_vjp, nondiff_argnums=(1,))
def _shift_rows(x, d):
    return _conv_shift(x, d)


_shift_rows.defvjp(lambda x, d: (_conv_shift(x, d), None), lambda d, _, g: (_conv_unshift(g, d),))


def _conv_gate(ug, uv, wg0, wg1, wg2, wv0, wv1, wv2, bg, bv):
    def conv(u, w0, w1, w2, b):
        return u * w2 + _shift_rows(u, 1) * w1 + _shift_rows(u, 2) * w0 + b
    return (jax.nn.silu(conv(ug, wg0, wg1, wg2, bg)) * conv(uv, wv0, wv1, wv2, bv),)


def _rms_fn(x, gain):
    return (_rms(x, gain),)


def _softmax_rows(s):
    e = jnp.exp(s - lax.stop_gradient(jnp.max(s, axis=-1, keepdims=True)))
    return e / jnp.sum(e, axis=-1, keepdims=True)


def _xa_core(qp, k, v, q_gain):
    dh = qp.shape[1] // XA_HEADS
    outs = []
    for h in range(XA_HEADS):
        sl = slice(h * dh, (h + 1) * dh)
        p = _softmax_rows(_dot(_rms(qp[:, sl], q_gain), k[:, sl], _NT) * (dh ** -0.5))
        outs.append(_dot(p, v[:, sl], _NN))
    return (jnp.concatenate(outs, axis=1),)


def _mem_kv(mem, mem_gain, wk, wv, k_gain):
    m = _rms(mem, mem_gain)
    kp = _dot(m, wk, _NN)
    dh = kp.shape[1] // XA_HEADS
    k = jnp.concatenate([_rms(kp[:, h * dh:(h + 1) * dh], k_gain) for h in range(XA_HEADS)], axis=1)
    return k, _dot(m, wv, _NN)


def _s5_post(y, u, d):
    return (jax.nn.gelu(y + d * u),)


def _glu(a, b):
    return (a * jax.nn.sigmoid(b),)


def _lb_first(logits):
    e = jnp.exp(logits - lax.stop_gradient(jnp.max(logits, axis=0, keepdims=True)))
    return (_row_of(e, 0) / jnp.sum(e, axis=0, keepdims=True),)


def _loss_fn(y, t):
    e = y - t
    part = 0.5 * jnp.sum(e * e) / y.shape[1]
    return e * (1.0 / y.shape[1]), jnp.full((8, 128), part / (8 * 128), F32)


def _out(shape, dtype, tm):
    return (shape, dtype, (tm, shape[1]), lambda i: (i, 0))


def rms_fwd(h, gain, tm, dtype):
    return blocked_fwd(_rms_fn, [rows(h, tm), full(gain)], [_out(h.shape, dtype, tm)], h.shape[0] // tm, "rms_fwd")[0]


def rms_bwd(h, gain, dy, tm):
    return blocked_bwd(_rms_fn, [rows(h, tm, 'blk'), full(gain, 'acc')], [rows(dy, tm)], h.shape[0] // tm, "rms_bwd")


def adamw(w, g, m, v, name):
    R = w.shape[0]
    tm = _tile(R, 256)

    def body(w_ref, g_ref, m_ref, v_ref, d_ref, nm_ref, nv_ref):
        g_ = g_ref[...]
        m_ = ADAM_B1 * m_ref[...] + (1.0 - ADAM_B1) * g_
        v_ = ADAM_B2 * v_ref[...] + (1.0 - ADAM_B2) * jnp.square(g_)
        m_hat = m_ / (1.0 - ADAM_B1 ** ADAM_STEP)
        v_hat = v_ / (1.0 - ADAM_B2 ** ADAM_STEP)
        d_ref[...] = -ADAM_LR * (m_hat / (jnp.sqrt(v_hat) + ADAM_EPS) + ADAM_WD * w_ref[...])
        nm_ref[...] = m_
        nv_ref[...] = v_

    spec = pl.BlockSpec((tm, w.shape[1]), lambda i: (i, 0))
    return pl.pallas_call(
        body, name=name, grid=(R // tm,), in_specs=[spec] * 4, out_specs=[spec] * 3,
        out_shape=[jax.ShapeDtypeStruct(w.shape, F32)] * 3, compiler_params=_cparams(("parallel",)),
    )(w, g, m, v)


def add_n(xs, name):
    R, C = xs[0].shape
    tm = _tile(R, 256)

    def body(*refs):
        acc = refs[0][...]
        for r in refs[1:-1]:
            acc = acc + r[...]
        refs[-1][...] = acc

    spec = pl.BlockSpec((tm, C), lambda i: (i, 0))
    return pl.pallas_call(
        body, name=name, grid=(R // tm,), in_specs=[spec] * len(xs), out_specs=spec,
        out_shape=jax.ShapeDtypeStruct((R, C), F32), compiler_params=_cparams(("parallel",)),
    )(*xs)


_HBM = pl.BlockSpec(memory_space=pltpu.HBM)


def _my_place():
    return lax.axis_index("x"), lax.axis_index("y"), lax.axis_index("c")


def pair_exchange(x, name):
    def body(x_ref, o_ref, send_sem, recv_sem):
        mx, my, mc = _my_place()
        cp = pltpu.make_async_remote_copy(src_ref=x_ref, dst_ref=o_ref, send_sem=send_sem, recv_sem=recv_sem,
                                          device_id=(mx, my, 1 - mc), device_id_type=MESH)
        cp.start()
        cp.wait()

    return pl.pallas_call(
        body, name=name, in_specs=[_HBM], out_specs=_HBM, out_shape=jax.ShapeDtypeStruct(x.shape, x.dtype),
        scratch_shapes=[pltpu.SemaphoreType.DMA, pltpu.SemaphoreType.DMA],
        compiler_params=pltpu.CompilerParams(has_side_effects=True),
    )(x)


def chip_exchange(x, gather, name):
    shape = ((4,) + x.shape) if gather else x.shape

    def body(x_ref, o_ref, send_sems, recv_sems, local_sem):
        mx, my, mc = _my_place()
        q = 2 * mx + my
        mine = pltpu.make_async_copy(x_ref if gather else x_ref.at[q], o_ref.at[q], local_sem)
        mine.start()
        copies = []
        for j in (1, 2, 3):
            tx, ty = mx ^ (j >> 1), my ^ (j & 1)
            src = x_ref if gather else x_ref.at[2 * tx + ty]
            copies.append(pltpu.make_async_remote_copy(src_ref=src, dst_ref=o_ref.at[q], send_sem=send_sems.at[j - 1],
                                                       recv_sem=recv_sems.at[j - 1], device_id=(tx, ty, mc),
                                                       device_id_type=MESH))
        for cp in copies:
            cp.start()
        for cp in copies:
            cp.wait()
        mine.wait()

    return pl.pallas_call(
        body, name=name, in_specs=[_HBM], out_specs=_HBM, out_shape=jax.ShapeDtypeStruct(shape, x.dtype),
        scratch_shapes=[pltpu.SemaphoreType.DMA((3,)), pltpu.SemaphoreType.DMA((3,)), pltpu.SemaphoreType.DMA],
        compiler_params=pltpu.CompilerParams(has_side_effects=True),
    )(x)


WEIGHTS = ['norm_mix', 'norm_xa', 'norm_mem', 'norm_ffn', 'xa_wq', 'xa_wk', 'xa_wv', 'xa_wo', 'xa_q_norm', 'xa_k_norm',
           'ffn_w_up', 'ffn_conv_w', 'ffn_conv_b', 'ffn_w_down', 'hg_lb_logits', 'mix_w_in', 'hg_out_norm',
           'mla_q_a_norm', 'mla_w_uq', 'mla_kv_a_norm', 'mla_w_ukv', 'mla_qn_nope', 'mla_qn_rope', 'mla_kn_nope',
           'mla_kn_rope', 'mix_w_out', 's5_lam_re', 's5_lam_im', 's5_log_dt', 's5_b_re', 's5_b_im', 's5_c_re',
           's5_c_im', 's5_d', 's5_w_glu_a', 's5_w_glu_b']
INPUTS = ['x', 'mem', 'positions'] + WEIGHTS + ['loss_target'] + ['m_' + n for n in WEIGHTS] + ['v_' + n for n in WEIGHTS]
SHARD_AXIS = {'xa_wq': 1, 'xa_wk': 1, 'xa_wv': 1, 'xa_wo': 1, 'ffn_w_up': 2, 'ffn_conv_w': 2, 'ffn_w_down': 1,
              'mix_w_in': 2, 'mla_w_uq': 2, 'mla_w_ukv': 2, 'mix_w_out': 1, 's5_d': 1, 's5_w_glu_a': 1, 's5_w_glu_b': 1}
SHARDED = [n for n in WEIGHTS if n in SHARD_AXIS]
REPLICATED = [n for n in WEIGHTS if n not in SHARD_AXIS]
ELEMENTWISE_SHARDED = ('ffn_conv_w', 's5_d')
N_CHIPS = 4
PACK_W = 1024
ROW_MULT = 256


def _pack(flats, mult=ROW_MULT):
    flat = jnp.concatenate([f.reshape(-1) for f in flats])
    unit = mult * PACK_W
    n = -(-flat.shape[0] // unit) * unit
    return jnp.pad(flat, (0, n - flat.shape[0])).reshape(n // PACK_W, PACK_W)


def _unpack(packed, shapes):
    flat, out, o = packed.reshape(-1), [], 0
    for s in shapes:
        n = math.prod(s)
        out.append(flat[o:o + n].reshape(s))
        o += n
    return out


def _rope_pad(w):
    z = jnp.zeros(w.shape[:-1] + (MLA_ROPE // 2,), w.dtype)
    return jnp.concatenate([w[..., :MLA_ROPE // 2], z, w[..., MLA_ROPE // 2:], z], axis=-1)


def _rope_unpad(g):
    return jnp.concatenate([g[..., :MLA_ROPE // 2], g[..., 64:64 + MLA_ROPE // 2]], axis=-1)


def _blockdiag_in(bb):
    nb = bb.shape[0] // S5_GB
    t = bb.reshape(nb, S5_GB, S5_STATE, S5_GROUP).transpose(0, 1, 3, 2)
    return jnp.einsum('bgmp,gh->bgmhp', t, jnp.eye(S5_GB, dtype=bb.dtype)).reshape(nb, S5_GB * S5_GROUP, S5_LANES)


def _blockdiag_in_t(dw):
    nb = dw.shape[0]
    t = jnp.einsum('bgmhp,gh->bgmp', dw.reshape(nb, S5_GB, S5_GROUP, S5_GB, S5_STATE), jnp.eye(S5_GB, dtype=dw.dtype))
    return t.transpose(0, 1, 3, 2).reshape(nb * S5_GB, S5_STATE, S5_GROUP)


def _blockdiag_out(c):
    nb = c.shape[0] // S5_GB
    t = c.reshape(nb, S5_GB, S5_GROUP, S5_STATE).transpose(0, 1, 3, 2)
    return jnp.einsum('bgpm,gh->bgphm', t, jnp.eye(S5_GB, dtype=c.dtype)).reshape(nb, S5_LANES, S5_GB * S5_GROUP)


def _blockdiag_out_t(dc):
    nb = dc.shape[0]
    t = jnp.einsum('bgphm,gh->bgpm', dc.reshape(nb, S5_GB, S5_STATE, S5_GB, S5_GROUP), jnp.eye(S5_GB, dtype=dc.dtype))
    return t.transpose(0, 1, 3, 2).reshape(nb * S5_GB, S5_GROUP, S5_STATE)


def _gather_weights(P):
    flats = []
    for n in SHARDED:
        w = P[n]
        flats.append(lax.bitcast_convert_type(w, BF16) if n in ELEMENTWISE_SHARDED else w.astype(BF16))
    got = chip_exchange(_pack(flats), True, "gather_weights")
    shapes = [P[n].shape + ((2,) if n in ELEMENTWISE_SHARDED else ()) for n in SHARDED]
    per_chip = [_unpack(got[q], shapes) for q in range(N_CHIPS)]
    full_w = {}
    for i, n in enumerate(SHARDED):
        parts = [per_chip[q][i] for q in range(N_CHIPS)]
        if n in ELEMENTWISE_SHARDED:
            parts = [lax.bitcast_convert_type(p, F32) for p in parts]
        full_w[n] = jnp.concatenate(parts, axis=SHARD_AXIS[n])
    return full_w


def _reduce_grads(G, P):
    _, _, mc = _my_place()
    rep = _pack([G[n] for n in REPLICATED], N_CHIPS * ROW_MULT)
    rr = rep.shape[0] // N_CHIPS
    per_chip = []
    for q in range(N_CHIPS):
        parts = []
        for n in SHARDED:
            s = P[n].shape[SHARD_AXIS[n]]
            parts.append(lax.slice_in_dim(G[n], q * s, (q + 1) * s, axis=SHARD_AXIS[n]))
        per_chip.append(jnp.concatenate([_pack(parts), rep[q * rr:(q + 1) * rr]], axis=0))
    rq = per_chip[0].shape[0]
    rs, h = rq - rr, rq // 2
    halves = jnp.stack(per_chip).reshape(N_CHIPS, 2, h, PACK_W)
    keep = lax.dynamic_index_in_dim(halves, mc, axis=1, keepdims=False).reshape(N_CHIPS * h, PACK_W)
    give = lax.dynamic_index_in_dim(halves, 1 - mc, axis=1, keepdims=False).reshape(N_CHIPS * h, PACK_W)
    pair = add_n([keep, pair_exchange(give, "grads_pair_exchange")], "grads_pair_sum")
    got = chip_exchange(pair.reshape(N_CHIPS, h, PACK_W), False, "grads_chip_all_to_all")
    mine = add_n([got[q] for q in range(N_CHIPS)], "grads_chip_sum")
    theirs = pair_exchange(mine, "grads_pair_gather")
    both = jnp.zeros((2, h, PACK_W), F32)
    both = lax.dynamic_update_index_in_dim(both, mine, mc, axis=0)
    both = lax.dynamic_update_index_in_dim(both, theirs, 1 - mc, axis=0).reshape(rq, PACK_W)
    g_sh_packed = both[:rs]
    g_rep_packed = chip_exchange(both[rs:], True, "grads_replicated_gather").reshape(N_CHIPS * rr, PACK_W)
    out = dict(zip(SHARDED, _unpack(g_sh_packed, [P[n].shape for n in SHARDED])))
    out.update(zip(REPLICATED, _unpack(g_rep_packed, [P[n].shape for n in REPLICATED])))
    return out, g_sh_packed, g_rep_packed


def _row(v):
    return v.reshape(1, -1)


def _xattn_fwd(h, mem, W, lyr, tm):
    g_xa, g_mem = _row(W['norm_xa'][lyr]), _row(W['norm_mem'][lyr])
    g_q, g_k = _row(W['xa_q_norm'][lyr]), _row(W['xa_k_norm'][lyr])
    wq, wk, wv, wo = (W[n][lyr] for n in ('xa_wq', 'xa_wk', 'xa_wv', 'xa_wo'))
    L, D = h.shape
    M = mem.shape[0]
    hx = rms_fwd(h, g_xa, tm, MXU_DTYPE)
    qp = matmul(hx, wq, name="xa_q")
    kv_opds = [full(mem), full(g_mem), full(wk), full(wv), full(g_k)]
    k, v = blocked_fwd(_mem_kv, kv_opds, [((M, D), F32, (M, D), lambda i: (0, 0))] * 2, 1, "xa_mem_kv")
    o = blocked_fwd(_xa_core, [rows(qp, tm), full(k), full(v), full(g_q)], [_out((L, D), MXU_DTYPE, tm)], L // tm,
                    "xa_core")[0]
    out = matmul(o, wo, add=h, name="xa_o")
    return out, (h, hx, qp, k, v, o)


def _xattn_bwd(dout, saved, mem, W, lyr, tm):
    h, hx, qp, k, v, o = saved
    g_xa, g_mem = _row(W['norm_xa'][lyr]), _row(W['norm_mem'][lyr])
    g_q, g_k = _row(W['xa_q_norm'][lyr]), _row(W['xa_k_norm'][lyr])
    wq, wk, wv, wo = (W[n][lyr] for n in ('xa_wq', 'xa_wk', 'xa_wv', 'xa_wo'))
    L = h.shape[0]
    do = matmul(dout, wo, "nt", name="xa_do")
    d_wo = matmul(o, dout, "tn", name="xa_dwo")
    dqp, dk, dv, d_gq = blocked_bwd(_xa_core, [rows(qp, tm, 'blk'), full(k, 'acc'), full(v, 'acc'), full(g_q, 'acc')],
                                    [rows(do, tm)], L // tm, "xa_core_bwd")
    d_wq = matmul(hx, dqp, "tn", name="xa_dwq")
    dhx = matmul(dqp, wq, "nt", name="xa_dhx")
    dh, d_gxa = rms_bwd(h, g_xa, dhx, tm)
    d_gmem, d_wk, d_wv, d_gk = blocked_bwd(
        _mem_kv, [full(mem), full(g_mem, 'acc'), full(wk, 'acc'), full(wv, 'acc'), full(g_k, 'acc')],
        [full(dk), full(dv)], 1, "xa_mem_kv_bwd")
    grads = {'norm_xa': d_gxa, 'norm_mem': d_gmem, 'xa_q_norm': d_gq, 'xa_k_norm': d_gk,
             'xa_wq': d_wq, 'xa_wk': d_wk, 'xa_wv': d_wv, 'xa_wo': d_wo}
    return dout + dh, grads


def _conv_params(W, lyr):
    cw, cb = W['ffn_conv_w'][lyr], W['ffn_conv_b'][lyr]
    F = cw.shape[1] // 2
    return [cw[0:1, :F], cw[1:2, :F], cw[2:3, :F], cw[0:1, F:], cw[1:2, F:], cw[2:3, F:], _row(cb[:F]), _row(cb[F:])]


def _ffn_fwd(h, W, lyr, tm):
    L, D = h.shape
    w_up, w_down = W['ffn_w_up'][lyr], W['ffn_w_down'][lyr]
    F = w_down.shape[0]
    hf = rms_fwd(h, _row(W['norm_ffn'][lyr]), tm, MXU_DTYPE)
    ug = matmul(hf, w_up[:, :F], name="ffn_up_gate")
    uv = matmul(hf, w_up[:, F:], name="ffn_up_value")
    opds = [cols(ug, 128), cols(uv, 128)] + [cols(p, 128) for p in _conv_params(W, lyr)]
    a = blocked_fwd(_conv_gate, opds, [((L, F), MXU_DTYPE, (L, 128), lambda j: (0, j))], F // 128, "ffn_conv_gate")[0]
    out = matmul(a, w_down, add=h, name="ffn_down")
    return out, (h, hf, ug, uv, a)


def _ffn_bwd(dout, saved, W, lyr, tm):
    h, hf, ug, uv, a = saved
    w_up, w_down = W['ffn_w_up'][lyr], W['ffn_w_down'][lyr]
    F = w_down.shape[0]
    da = matmul(dout, w_down, "nt", name="ffn_da")
    d_wdown = matmul(a, dout, "tn", name="ffn_dwdown")
    opds = [cols(ug, 128, 'blk'), cols(uv, 128, 'blk')] + [cols(p, 128, 'blk') for p in _conv_params(W, lyr)]
    gs = blocked_bwd(_conv_gate, opds, [cols(da, 128)], F // 128, "ffn_conv_gate_bwd")
    dug, duv = gs[0], gs[1]
    d_cw = jnp.concatenate([jnp.concatenate(gs[2:5], axis=0), jnp.concatenate(gs[5:8], axis=0)], axis=1)
    d_cb = jnp.concatenate([gs[8], gs[9]], axis=1)[0]
    d_wup = jnp.concatenate([matmul(hf, dug, "tn", name="ffn_dwup_gate"), matmul(hf, duv, "tn", name="ffn_dwup_value")],
                            axis=1)
    dhf = matmul(dug, w_up[:, :F], "nt", name="ffn_dhf_gate")
    dhf = matmul(duv, w_up[:, F:], "nt", add=dhf, name="ffn_dhf_value")
    dh, d_g = rms_bwd(h, _row(W['norm_ffn'][lyr]), dhf, tm)
    return dout + dh, {'norm_ffn': d_g, 'ffn_w_up': d_wup, 'ffn_conv_w': d_cw, 'ffn_conv_b': d_cb, 'ffn_w_down': d_wdown}


def _mla_params(W):
    w_uq = W['mla_w_uq'][0].reshape(MLA_Q_RANK, MLA_HEADS, MLA_QK)
    w_uq = jnp.concatenate([w_uq[..., :MLA_NOPE], _rope_pad(w_uq[..., MLA_NOPE:])], axis=-1)
    w_ukv = W['mla_w_ukv'][0].reshape(MLA_KV_RANK, MLA_HEADS, MLA_NOPE + MLA_V)
    w_ukv = jnp.concatenate([w_ukv[..., :MLA_NOPE].reshape(MLA_KV_RANK, -1), w_ukv[..., MLA_NOPE:].reshape(MLA_KV_RANK, -1)],
                            axis=1)
    return [_row(W['mla_q_a_norm'][0]), w_uq.reshape(MLA_Q_RANK, MLA_HEADS * MLA_DK), _row(W['mla_kv_a_norm'][0]), w_ukv,
            _row(W['mla_qn_nope'][0]), _row(_rope_pad(W['mla_qn_rope'][0])), _row(W['mla_kn_nope'][0]),
            _row(_rope_pad(W['mla_kn_rope'][0]))]


def _w_in_padded(W):
    w = W['mix_w_in'][0]
    return jnp.concatenate([w[:, :IN_WIDTH - MLA_ROPE], _rope_pad(w[:, IN_WIDTH - MLA_ROPE:])], axis=1)


def _mixer0_fwd(h, W, cos_p, sin_p, tm):
    L = h.shape[0]
    t = min(256, L)
    hn = rms_fwd(h, _row(W['norm_mix'][0]), tm, MXU_DTYPE)
    proj = matmul(hn, _w_in_padded(W), name="mix_in")
    logits = W['hg_lb_logits']
    lb = blocked_fwd(_lb_first, [full(logits)], [((1, HG_WIDTH), F32, (1, HG_WIDTH), lambda i: (0, 0))], 1, "hg_lb")[0]
    gain = _row(W['hg_out_norm'][0])
    o_hg, states = hgrn2_fwd(proj, lb, gain)
    mp = _mla_params(W)
    q, k, v = mla_prep_fwd(proj, cos_p, sin_p, mp, tm)
    scale = MLA_QK ** -0.5
    o_mla, lse = attn_fwd(q, k, v, scale, t)
    w_out = W['mix_w_out'][0]
    out = matmul(o_hg, w_out[:HG_WIDTH], add=h, name="mix_out_hg")
    out = matmul(o_mla, w_out[HG_WIDTH:], add=out, name="mix_out_mla")
    return out, (h, hn, proj, lb, o_hg, states, q, k, v, o_mla, lse)


def _mixer0_bwd(dout, saved, W, cos_p, sin_p, tm):
    h, hn, proj, lb, o_hg, states, q, k, v, o_mla, lse = saved
    L = h.shape[0]
    t = min(256, L)
    scale = MLA_QK ** -0.5
    w_out = W['mix_w_out'][0]
    gain = _row(W['hg_out_norm'][0])
    do_hg = matmul(dout, w_out[:HG_WIDTH], "nt", name="mix_do_hg")
    do_mla = matmul(dout, w_out[HG_WIDTH:], "nt", name="mix_do_mla")
    d_wout = jnp.concatenate([matmul(o_hg, dout, "tn", name="mix_dwout_hg"), matmul(o_mla, dout, "tn", name="mix_dwout_mla")],
                             axis=0)
    dq = attn_bwd_dq(q, k, v, o_mla, lse, do_mla, scale, t)
    dk, dv = attn_bwd_dkv(q, k, v, o_mla, lse, do_mla, scale, t)
    mp = _mla_params(W)
    dcq, dckv, dkpe, d_qa, d_wuq, d_kva, d_wukv, d_qnn, d_qnr, d_knn, d_knr = mla_prep_bwd(proj, cos_p, sin_p, mp, dq, dk, dv, tm)
    d_hg, d_lb, d_gain = hgrn2_bwd(proj, lb, gain, states, do_hg)
    dproj = jnp.concatenate([d_hg, dcq, dckv, dkpe], axis=1)
    d_win = matmul(hn, dproj, "tn", name="mix_dwin")
    dhn = matmul(dproj, _w_in_padded(W), "nt", name="mix_dhn")
    dh, d_g = rms_bwd(h, _row(W['norm_mix'][0]), dhn, tm)
    logits = W['hg_lb_logits']
    d_logits = blocked_bwd(_lb_first, [full(logits, 'acc')], [full(d_lb)], 1, "hg_lb_bwd")[0]
    d_wuq = d_wuq.reshape(MLA_Q_RANK, MLA_HEADS, MLA_DK)
    d_wuq = jnp.concatenate([d_wuq[..., :MLA_NOPE], _rope_unpad(d_wuq[..., MLA_NOPE:])], axis=-1)
    hw = MLA_HEADS * MLA_NOPE
    d_wukv = jnp.concatenate([d_wukv[:, :hw].reshape(MLA_KV_RANK, MLA_HEADS, MLA_NOPE),
                              d_wukv[:, hw:].reshape(MLA_KV_RANK, MLA_HEADS, MLA_V)], axis=-1)
    d_win = jnp.concatenate([d_win[:, :IN_WIDTH - MLA_ROPE], _rope_unpad(d_win[:, IN_WIDTH - MLA_ROPE:])], axis=1)
    grads = {'norm_mix': d_g, 'hg_lb_logits': d_logits, 'mix_w_in': d_win[None], 'hg_out_norm': d_gain,
             'mla_q_a_norm': d_qa, 'mla_w_uq': d_wuq.reshape(1, MLA_Q_RANK, -1), 'mla_kv_a_norm': d_kva,
             'mla_w_ukv': d_wukv.reshape(1, MLA_KV_RANK, -1), 'mla_qn_nope': d_qnn, 'mla_qn_rope': _rope_unpad(d_qnr),
             'mla_kn_nope': d_knn, 'mla_kn_rope': _rope_unpad(d_knr), 'mix_w_out': d_wout[None]}
    return dout + dh, grads


def _s5_inputs(W):
    G = W['s5_lam_re'].shape[1]
    return [W['s5_lam_re'][0], W['s5_lam_im'][0], W['s5_log_dt'][0].reshape(G, 1),
            W['s5_b_re'][0].reshape(G, -1), W['s5_b_im'][0].reshape(G, -1)]


def _mixer1_fwd(h, W, tm):
    L, D = h.shape
    u = rms_fwd(h, _row(W['norm_mix'][1]), tm, F32)
    di = _s5_inputs(W)
    G = di[0].shape[0]
    sq, wide = ((G, S5_STATE), F32, (G, S5_STATE), lambda i: (0, 0)), ((G, S5_STATE * S5_GROUP), F32, (G, S5_STATE * S5_GROUP), lambda i: (0, 0))
    ar, ai, bbr, bbi = blocked_fwd(_s5_discretize, [full(a) for a in di], [sq, sq, wide, wide], 1, "s5_discretize")
    nb = G // S5_GB
    core = (_blockdiag_in(bbr.reshape(G, S5_STATE, S5_GROUP)), _blockdiag_in(bbi.reshape(G, S5_STATE, S5_GROUP)),
            ar.reshape(nb, 1, S5_LANES), ai.reshape(nb, 1, S5_LANES),
            _blockdiag_out(W['s5_c_re'][0]), _blockdiag_out(W['s5_c_im'][0]))
    y = s5_fwd(u, *core)
    d = W['s5_d']
    y2 = blocked_fwd(_s5_post, [rows(y, tm), rows(u, tm), full(d)], [_out((L, D), MXU_DTYPE, tm)], L // tm, "s5_post")[0]
    w_ab = jnp.concatenate([W['s5_w_glu_a'][0], W['s5_w_glu_b'][0]], axis=1)
    ab = matmul(y2, w_ab, name="s5_glu_in")
    mix = blocked_fwd(_glu, [rows(ab, tm, col=0, width=D), rows(ab, tm, col=1, width=D)], [_out((L, D), F32, tm)], L // tm,
                      "s5_glu")[0]
    return h + mix, (h, u, core, y, y2, ab)


def _mixer1_bwd(dout, saved, W, tm):
    h, u, core, y, y2, ab = saved
    L, D = h.shape
    da, db = blocked_bwd(_glu, [rows(ab, tm, 'blk', col=0, width=D), rows(ab, tm, 'blk', col=1, width=D)], [rows(dout, tm)],
                         L // tm, "s5_glu_bwd")
    w_a, w_b = W['s5_w_glu_a'][0], W['s5_w_glu_b'][0]
    dy2 = matmul(da, w_a, "nt", name="s5_dy2_a")
    dy2 = matmul(db, w_b, "nt", add=dy2, name="s5_dy2_b")
    d_wa = matmul(y2, da, "tn", name="s5_dwa")
    d_wb = matmul(y2, db, "tn", name="s5_dwb")
    d = W['s5_d']
    dy, du_skip, d_d = blocked_bwd(_s5_post, [rows(y, tm, 'blk'), rows(u, tm, 'blk'), full(d, 'acc')], [rows(dy2, tm)], L // tm,
                                   "s5_post_bwd")
    du, dwr, dwi, dar, dai, dcr, dci = s5_bwd(u, *core, dy, min(256, L))
    di = _s5_inputs(W)
    G = di[0].shape[0]
    cts = [dar.reshape(G, S5_STATE), dai.reshape(G, S5_STATE), _blockdiag_in_t(dwr).reshape(G, -1), _blockdiag_in_t(dwi).reshape(G, -1)]
    d_lr, d_li, d_ldt, d_br, d_bi = blocked_bwd(_s5_discretize, [full(a, 'acc') for a in di], [full(c) for c in cts], 1,
                                                "s5_discretize_bwd")
    dh, d_g = rms_bwd(h, _row(W['norm_mix'][1]), du + du_skip, tm)
    bshape = W['s5_b_re'].shape
    grads = {'norm_mix': d_g, 's5_lam_re': d_lr[None], 's5_lam_im': d_li[None], 's5_log_dt': d_ldt.reshape(1, G),
             's5_b_re': d_br.reshape(bshape), 's5_b_im': d_bi.reshape(bshape), 's5_c_re': _blockdiag_out_t(dcr)[None],
             's5_c_im': _blockdiag_out_t(dci)[None], 's5_d': d_d, 's5_w_glu_a': d_wa[None], 's5_w_glu_b': d_wb[None]}
    return dout + dh, grads


def kernel(x, mem, positions, norm_mix, norm_xa, norm_mem, norm_ffn, xa_wq, xa_wk, xa_wv, xa_wo, xa_q_norm, xa_k_norm, ffn_w_up, ffn_conv_w, ffn_conv_b, ffn_w_down, hg_lb_logits, mix_w_in, hg_out_norm, mla_q_a_norm, mla_w_uq, mla_kv_a_norm, mla_w_ukv, mla_qn_nope, mla_qn_rope, mla_kn_nope, mla_kn_rope, mix_w_out, s5_lam_re, s5_lam_im, s5_log_dt, s5_b_re, s5_b_im, s5_c_re, s5_c_im, s5_d, s5_w_glu_a, s5_w_glu_b, loss_target, m_norm_mix, m_norm_xa, m_norm_mem, m_norm_ffn, m_xa_wq, m_xa_wk, m_xa_wv, m_xa_wo, m_xa_q_norm, m_xa_k_norm, m_ffn_w_up, m_ffn_conv_w, m_ffn_conv_b, m_ffn_w_down, m_hg_lb_logits, m_mix_w_in, m_hg_out_norm, m_mla_q_a_norm, m_mla_w_uq, m_mla_kv_a_norm, m_mla_w_ukv, m_mla_qn_nope, m_mla_qn_rope, m_mla_kn_nope, m_mla_kn_rope, m_mix_w_out, m_s5_lam_re, m_s5_lam_im, m_s5_log_dt, m_s5_b_re, m_s5_b_im, m_s5_c_re, m_s5_c_im, m_s5_d, m_s5_w_glu_a, m_s5_w_glu_b, v_norm_mix, v_norm_xa, v_norm_mem, v_norm_ffn, v_xa_wq, v_xa_wk, v_xa_wv, v_xa_wo, v_xa_q_norm, v_xa_k_norm, v_ffn_w_up, v_ffn_conv_w, v_ffn_conv_b, v_ffn_w_down, v_hg_lb_logits, v_mix_w_in, v_hg_out_norm, v_mla_q_a_norm, v_mla_w_uq, v_mla_kv_a_norm, v_mla_w_ukv, v_mla_qn_nope, v_mla_qn_rope, v_mla_kn_nope, v_mla_kn_rope, v_mix_w_out, v_s5_lam_re, v_s5_lam_im, v_s5_log_dt, v_s5_b_re, v_s5_b_im, v_s5_c_re, v_s5_c_im, v_s5_d, v_s5_w_glu_a, v_s5_w_glu_b):
    P = dict(locals())
    assert sorted(P) == sorted(INPUTS) and norm_mix.shape[0] == 2 and mix_w_in.shape[0] == 1
    x, mem, target = P['x'][0], P['mem'][0], P['loss_target'][0]
    L, D = x.shape
    tm = min(256, L)

    W = {n: P[n] for n in REPLICATED}
    W.update(_gather_weights(P))

    inv_freq = 1.0 / (ROPE_BASE ** (jnp.arange(0, MLA_ROPE, 2, dtype=F32) / MLA_ROPE))
    ang = P['positions'][0].astype(F32)[:, None] * inv_freq
    cos, sin, z = jnp.cos(ang), jnp.sin(ang), jnp.zeros_like(ang)
    cos_p = jnp.concatenate([cos, z, cos, z], axis=1)
    sin_p = jnp.concatenate([-sin, z, sin, z], axis=1)

    h, s_mix0 = _mixer0_fwd(x, W, cos_p, sin_p, tm)
    h, s_xa0 = _xattn_fwd(h, mem, W, 0, tm)
    h, s_ffn0 = _ffn_fwd(h, W, 0, tm)
    h, s_mix1 = _mixer1_fwd(h, W, tm)
    h, s_xa1 = _xattn_fwd(h, mem, W, 1, tm)
    h, s_ffn1 = _ffn_fwd(h, W, 1, tm)
    n = L // tm
    dh, parts = blocked_fwd(_loss_fn, [rows(h, tm), rows(target, tm)],
                            [_out((L, D), F32, tm), ((n * 8, 128), F32, (8, 128), lambda i: (i, 0))], n, "loss")
    loss = lax.psum(jnp.sum(parts), ("x", "y", "c"))

    layered = {}

    def collect(g, lyr):
        for k_, v_ in g.items():
            layered.setdefault(k_, {})[lyr] = v_

    dh, g = _ffn_bwd(dh, s_ffn1, W, 1, tm)
    collect(g, 1)
    dh, g = _xattn_bwd(dh, s_xa1, mem, W, 1, tm)
    collect(g, 1)
    dh, g = _mixer1_bwd(dh, s_mix1, W, tm)
    collect(g, 1)
    dh, g = _ffn_bwd(dh, s_ffn0, W, 0, tm)
    collect(g, 0)
    dh, g = _xattn_bwd(dh, s_xa0, mem, W, 0, tm)
    collect(g, 0)
    dx, g = _mixer0_bwd(dh, s_mix0, W, cos_p, sin_p, tm)
    collect(g, 0)

    G = {}
    for name in WEIGHTS:
        full_shape = W[name].shape
        by_layer = layered[name]
        if len(by_layer) == 2:
            G[name] = jnp.stack([by_layer[0].reshape(full_shape[1:]), by_layer[1].reshape(full_shape[1:])])
        else:
            G[name] = next(iter(by_layer.values())).reshape(full_shape)

    grads, g_sh, g_rep = _reduce_grads(G, P)

    def packed(prefix, names, mult):
        return _pack([P[prefix + n_] for n_ in names], mult)

    d_sh, m_sh, v_sh = adamw(packed('', SHARDED, ROW_MULT), g_sh, packed('m_', SHARDED, ROW_MULT),
                             packed('v_', SHARDED, ROW_MULT), "adamw_sharded")
    rep_mult = N_CHIPS * ROW_MULT
    d_rp, m_rp, v_rp = adamw(packed('', REPLICATED, rep_mult), g_rep, packed('m_', REPLICATED, rep_mult),
                             packed('v_', REPLICATED, rep_mult), "adamw_replicated")
    sh_shapes = [P[n_].shape for n_ in SHARDED]
    rp_shapes = [P[n_].shape for n_ in REPLICATED]
    outs = []
    for sh_pack, rp_pack in ((d_sh, d_rp), (m_sh, m_rp), (v_sh, v_rp)):
        d = dict(zip(SHARDED, _unpack(sh_pack, sh_shapes)))
        d.update(zip(REPLICATED, _unpack(rp_pack, rp_shapes)))
        outs.append([d[n_] for n_ in WEIGHTS])
    return (loss, dx[None], *[grads[n_] for n_ in WEIGHTS], *outs[0], *outs[1], *outs[2])
```

```python
import functools
import math

import jax
import jax.numpy as jnp
import numpy as np
from jax import lax
from jax.experimental import pallas as pl
from jax.experimental.pallas import tpu as pltpu

F32 = jnp.float32
BF16 = jnp.bfloat16
MXU_DTYPE = BF16
HI = lax.Precision.HIGHEST
V7X_VMEM_LIMIT_BYTES = 56 * 1024 * 1024
EPS = 1e-6
MESH = pl.DeviceIdType.MESH

HG_HEADS, HG_DIM = 4, 128
HG_WIDTH = HG_HEADS * HG_DIM
HG_SUB = 16
HG_BLOCK = 64
MLA_HEADS, MLA_Q_RANK, MLA_KV_RANK = 4, 256, 128
MLA_NOPE, MLA_ROPE, MLA_V = 128, 64, 128
MLA_QK = MLA_NOPE + MLA_ROPE
MLA_DK = 256
ROPE_BASE = 10000.0
IN_WIDTH = 4 * HG_WIDTH + MLA_Q_RANK + MLA_KV_RANK + MLA_ROPE
IN_PAD = 4 * HG_WIDTH + MLA_Q_RANK + MLA_KV_RANK + 128
S5_GROUP, S5_STATE = 16, 64
S5_GB = 8
DT_MIN, DT_MAX = 1e-3, 1e-1
XA_HEADS = 4
CONV_W = 3
ADAM_LR, ADAM_B1, ADAM_B2, ADAM_EPS, ADAM_WD, ADAM_STEP = 0.001, 0.9, 0.999, 1e-08, 0.01, 10


def _cparams(sem):
    return pltpu.CompilerParams(dimension_semantics=sem, vmem_limit_bytes=V7X_VMEM_LIMIT_BYTES)


class Opd:
    def __init__(self, arr, block, imap, grad=None, gshape=None, gimap=None):
        self.arr, self.block, self.imap, self.grad = arr, block, imap, grad
        self.gshape = arr.shape if gshape is None else gshape
        self.gimap = imap if gimap is None else gimap

    def spec(self):
        return pl.BlockSpec(self.block, self.imap)

    def gspec(self):
        return pl.BlockSpec(self.block, self.gimap)


def rows(arr, tm, grad=None, col=0, width=None):
    width = arr.shape[1] if width is None else width
    return Opd(arr, (tm, width), lambda i, c=col: (i, c), grad, (arr.shape[0], width), lambda i: (i, 0))


def cols(arr, tn, grad=None):
    return Opd(arr, (arr.shape[0], tn), lambda j: (0, j), grad)


def full(arr, grad=None):
    return Opd(arr, arr.shape, lambda i: (0, 0), grad)


def _load(ref):
    v = ref[...]
    return v.astype(F32) if jnp.issubdtype(v.dtype, jnp.floating) else v


def blocked_fwd(f, opds, outs, n, name):
    n_in = len(opds)

    def body(*refs):
        ys = f(*[_load(r) for r in refs[:n_in]])
        for r, y in zip(refs[n_in:], ys):
            r[...] = y.astype(r.dtype)

    res = pl.pallas_call(
        body, name=name, grid=(n,),
        in_specs=[o.spec() for o in opds],
        out_specs=[pl.BlockSpec(b, m) for (_, _, b, m) in outs],
        out_shape=[jax.ShapeDtypeStruct(s, d) for (s, d, _, _) in outs],
        compiler_params=_cparams(("parallel",)),
    )(*[o.arr for o in opds])
    return res


def blocked_bwd(f, opds, dys, n, name):
    n_in, n_dy = len(opds), len(dys)
    diff = [i for i, o in enumerate(opds) if o.grad]

    def body(*refs):
        vals = [_load(r) for r in refs[:n_in]]

        def fd(*dv):
            allv = list(vals)
            for i, v in zip(diff, dv):
                allv[i] = v
            return tuple(f(*allv))

        ys, vjp = jax.vjp(fd, *[vals[i] for i in diff])
        cts = tuple(_load(r).astype(y.dtype) for r, y in zip(refs[n_in:n_in + n_dy], ys))
        gs = vjp(cts)
        for r, g, i in zip(refs[n_in + n_dy:], gs, diff):
            if opds[i].grad == 'acc':
                @pl.when(pl.program_id(0) == 0)
                def _(r=r):
                    r[...] = jnp.zeros(r.shape, r.dtype)
                r[...] += g.astype(r.dtype)
            else:
                r[...] = g.astype(r.dtype)

    any_acc = any(opds[i].grad == 'acc' for i in diff)
    res = pl.pallas_call(
        body, name=name, grid=(n,),
        in_specs=[o.spec() for o in opds] + [o.spec() for o in dys],
        out_specs=[opds[i].gspec() for i in diff],
        out_shape=[jax.ShapeDtypeStruct(opds[i].gshape, F32) for i in diff],
        compiler_params=_cparams(("arbitrary" if any_acc else "parallel",)),
    )(*[o.arr for o in opds], *[o.arr for o in dys])
    return res


def _tile(dim, want):
    for t in range(want - want % 128, 0, -128):
        if dim % t == 0:
            return t
    assert dim <= want, (dim, want)
    return dim


MATMUL_VMEM_BUDGET = 40 * 1024 * 1024
MATMUL_ROWS = 512


def _widest(N, fits):
    for t in range(N - N % 128, 0, -128):
        if N % t == 0 and fits(t):
            return t
    return N


def matmul(a, b, mode="nn", out_dtype=F32, add=None, name="matmul"):
    sa, sb, so = a.dtype.itemsize, b.dtype.itemsize, jnp.dtype(out_dtype).itemsize
    has_add = add is not None
    if mode == "tn":
        (K, M), (K2, N) = a.shape, b.shape
        assert K == K2 and not has_add and out_dtype == F32, (a.shape, b.shape)
        tk = _tile(K, MATMUL_ROWS)
        tn = _widest(N, lambda t: 2 * (tk * M * sa + tk * t * sb + M * t * 4) <= MATMUL_VMEM_BUDGET)

        def body(a_ref, b_ref, o_ref):
            r = lax.dot_general(a_ref[...].astype(MXU_DTYPE), b_ref[...].astype(MXU_DTYPE), ((_TN), ((), ())),
                                preferred_element_type=F32)

            @pl.when(pl.program_id(1) == 0)
            def _():
                o_ref[...] = r

            @pl.when(pl.program_id(1) > 0)
            def _():
                o_ref[...] += r

        return pl.pallas_call(
            body, name=name, grid=(N // tn, K // tk),
            in_specs=[pl.BlockSpec((tk, M), lambda j, k: (k, 0)), pl.BlockSpec((tk, tn), lambda j, k: (k, j))],
            out_specs=pl.BlockSpec((M, tn), lambda j, k: (0, j)),
            out_shape=jax.ShapeDtypeStruct((M, N), F32),
            compiler_params=_cparams(("parallel", "arbitrary")),
        )(a, b)

    (M, K) = a.shape
    N = b.shape[1] if mode == "nn" else b.shape[0]
    assert K == (b.shape[0] if mode == "nn" else b.shape[1]), (a.shape, b.shape, mode)
    tm = _tile(M, MATMUL_ROWS)
    tn = _widest(N, lambda t: 2 * (tm * K * sa + K * t * sb + tm * t * (so + 4 * has_add)) <= MATMUL_VMEM_BUDGET)
    dims = ((_NN if mode == "nn" else _NT), ((), ()))

    def body(*refs):
        r = lax.dot_general(refs[0][...].astype(MXU_DTYPE), refs[1][...].astype(MXU_DTYPE), dims, preferred_element_type=F32)
        if has_add:
            r = r + refs[2][...].astype(F32)
        refs[-1][...] = r.astype(refs[-1].dtype)

    b_spec = pl.BlockSpec((K, tn), lambda j, i: (0, j)) if mode == "nn" else pl.BlockSpec((tn, K), lambda j, i: (j, 0))
    in_specs = [pl.BlockSpec((tm, K), lambda j, i: (i, 0)), b_spec]
    args = [a, b]
    if has_add:
        in_specs.append(pl.BlockSpec((tm, tn), lambda j, i: (i, j)))
        args.append(add)
    return pl.pallas_call(
        body, name=name, grid=(N // tn, M // tm),
        in_specs=in_specs,
        out_specs=pl.BlockSpec((tm, tn), lambda j, i: (i, j)),
        out_shape=jax.ShapeDtypeStruct((M, N), out_dtype),
        compiler_params=_cparams(("parallel", "parallel")),
    )(*args)


def _dot(a, b, dims, precision=None):
    if precision is None:
        a, b = a.astype(MXU_DTYPE), b.astype(MXU_DTYPE)
    return lax.dot_general(a, b, (dims, ((), ())), precision=precision, preferred_element_type=F32)


_NN = ((1,), (0,))
_NT = ((1,), (1,))
_TN = ((0,), (0,))


def _rms(x, gain):
    return x * lax.rsqrt(jnp.mean(x * x, axis=-1, keepdims=True) + EPS) * gain


def _hg_block(st_t, q, fl, iv, g, lb, gain):
    row = lax.broadcasted_iota(jnp.int32, (HG_SUB, HG_SUB), 0)
    col = lax.broadcasted_iota(jnp.int32, (HG_SUB, HG_SUB), 1)
    tri = (row >= col).astype(F32)
    outs, states = [], []
    for h in range(HG_HEADS):
        sl = slice(h * HG_DIM, (h + 1) * HG_DIM)
        st = st_t[h * HG_DIM:(h + 1) * HG_DIM, :]
        lbh = lb[:, sl]
        fg = lbh + (1.0 - lbh) * jax.nn.sigmoid(fl[:, sl])
        lf, kk, qf, v = jnp.log(fg), 1.0 - fg, jax.nn.silu(q[:, sl]), iv[:, sl]
        parts = []
        for s in range(q.shape[0] // HG_SUB):
            r = slice(s * HG_SUB, (s + 1) * HG_SUB)
            b = _dot(tri, lf[r], _NN, HI)
            b_end = jnp.sum(lf[r], axis=0, keepdims=True)
            qe = qf[r] * jnp.exp(b)
            sc = _dot(qe, kk[r] * jnp.exp(-b), _NT) * tri
            parts.append(_dot(sc, v[r], _NN) + _dot(qe, st, _NT))
            st = st * jnp.exp(b_end) + _dot(v[r], kk[r] * jnp.exp(b_end - b), _TN)
        o = jnp.concatenate(parts, axis=0)
        outs.append(_rms(o, gain[:, sl]) * jax.nn.silu(g[:, sl]))
        states.append(st)
    return jnp.concatenate(states, axis=0), jnp.concatenate(outs, axis=1)


def _hg_specs(proj, nb):
    return [pl.BlockSpec((HG_BLOCK, HG_WIDTH), lambda i, c=c, f=nb: (f(i), c)) for c in range(4)]


def hgrn2_fwd(proj, lb, gain):
    L = proj.shape[0]
    n = L // HG_BLOCK

    def body(q, fl, iv, g, lb_r, gain_r, o_ref, st_ref, st):
        @pl.when(pl.program_id(0) == 0)
        def _():
            st[...] = jnp.zeros(st.shape, F32)

        st_ref[0] = st[...]
        new, o = _hg_block(st[...], q[...], fl[...], iv[...], g[...], lb_r[...], gain_r[...])
        st[...] = new
        o_ref[...] = o.astype(o_ref.dtype)

    pspec = pl.BlockSpec((1, HG_WIDTH), lambda i: (0, 0))
    return pl.pallas_call(
        body, name="hgrn2_fwd", grid=(n,),
        in_specs=_hg_specs(proj, lambda i: i) + [pspec, pspec],
        out_specs=[pl.BlockSpec((HG_BLOCK, HG_WIDTH), lambda i: (i, 0)),
                   pl.BlockSpec((1, HG_WIDTH, HG_DIM), lambda i: (i, 0, 0))],
        out_shape=[jax.ShapeDtypeStruct((L, HG_WIDTH), MXU_DTYPE),
                   jax.ShapeDtypeStruct((n, HG_WIDTH, HG_DIM), F32)],
        scratch_shapes=[pltpu.VMEM((HG_WIDTH, HG_DIM), F32)],
        compiler_params=_cparams(("arbitrary",)),
    )(proj, proj, proj, proj, lb, gain)


def hgrn2_bwd(proj, lb, gain, states, do):
    L = proj.shape[0]
    n = L // HG_BLOCK

    def body(q, fl, iv, g, lb_r, gain_r, st_r, do_r, dproj, dlb, dgain, dst):
        @pl.when(pl.program_id(0) == 0)
        def _():
            dst[...] = jnp.zeros(dst.shape, F32)
            dlb[...] = jnp.zeros(dlb.shape, F32)
            dgain[...] = jnp.zeros(dgain.shape, F32)

        _, vjp = jax.vjp(_hg_block, st_r[0], q[...], fl[...], iv[...], g[...], lb_r[...], gain_r[...])
        d_st, dq, dfl, div, dg, d_lb, d_gain = vjp((dst[...], do_r[...].astype(F32)))
        dst[...] = d_st
        dproj[:, 0 * HG_WIDTH:1 * HG_WIDTH] = dq
        dproj[:, 1 * HG_WIDTH:2 * HG_WIDTH] = dfl
        dproj[:, 2 * HG_WIDTH:3 * HG_WIDTH] = div
        dproj[:, 3 * HG_WIDTH:4 * HG_WIDTH] = dg
        dlb[...] += d_lb
        dgain[...] += d_gain

    rev = lambda i: n - 1 - i
    pspec = pl.BlockSpec((1, HG_WIDTH), lambda i: (0, 0))
    return pl.pallas_call(
        body, name="hgrn2_bwd", grid=(n,),
        in_specs=_hg_specs(proj, rev) + [pspec, pspec,
                                         pl.BlockSpec((1, HG_WIDTH, HG_DIM), lambda i: (rev(i), 0, 0)),
                                         pl.BlockSpec((HG_BLOCK, HG_WIDTH), lambda i: (rev(i), 0))],
        out_specs=[pl.BlockSpec((HG_BLOCK, 4 * HG_WIDTH), lambda i: (rev(i), 0)), pspec, pspec],
        out_shape=[jax.ShapeDtypeStruct((L, 4 * HG_WIDTH), F32),
                   jax.ShapeDtypeStruct((1, HG_WIDTH), F32), jax.ShapeDtypeStruct((1, HG_WIDTH), F32)],
        scratch_shapes=[pltpu.VMEM((HG_WIDTH, HG_DIM), F32)],
        compiler_params=_cparams(("arbitrary",)),
    )(proj, proj, proj, proj, lb, gain, states, do)


def _rope_rms(x, gain_p, cos_p, sin_p):
    n = x * lax.rsqrt(jnp.sum(x * x, axis=-1, keepdims=True) * (1.0 / MLA_ROPE) + EPS) * gain_p
    r = lax.broadcasted_iota(jnp.int32, (128, 128), 0)
    c = lax.broadcasted_iota(jnp.int32, (128, 128), 1)
    swap = (r == (c + 64) % 128).astype(F32)
    return n * cos_p + _dot(n, swap, _NN, HI) * sin_p


def _mla_prep(c_q, c_kv, kpe, cos_p, sin_p, q_a, w_uq, kv_a, w_ukv, qn_nope, qn_rope, kn_nope, kn_rope):
    q = _dot(_rms(c_q, q_a), w_uq, _NN)
    kv = _dot(_rms(c_kv, kv_a), w_ukv, _NN)
    k_pe = _rope_rms(kpe, kn_rope, cos_p, sin_p)
    qs, ks = [], []
    for h in range(MLA_HEADS):
        qs.append(_rms(q[:, h * MLA_DK:h * MLA_DK + MLA_NOPE], qn_nope))
        qs.append(_rope_rms(q[:, h * MLA_DK + MLA_NOPE:(h + 1) * MLA_DK], qn_rope, cos_p, sin_p))
        ks.append(_rms(kv[:, h * MLA_NOPE:(h + 1) * MLA_NOPE], kn_nope))
        ks.append(k_pe)
    return jnp.concatenate(qs, axis=1), jnp.concatenate(ks, axis=1), kv[:, MLA_HEADS * MLA_NOPE:]


def _mla_prep_opds(proj, cos_p, sin_p, params, tm, grads):
    g = (lambda k: k) if grads else (lambda k: None)
    c0 = 4 * HG_WIDTH
    return ([rows(proj, tm, g('blk'), col=c0 // MLA_Q_RANK, width=MLA_Q_RANK),
             rows(proj, tm, g('blk'), col=(c0 + MLA_Q_RANK) // 128, width=128),
             rows(proj, tm, g('blk'), col=(c0 + MLA_Q_RANK) // 128 + 1, width=128),
             rows(cos_p, tm), rows(sin_p, tm)] + [full(p, g('acc')) for p in params])


def mla_prep_fwd(proj, cos_p, sin_p, params, tm):
    L = proj.shape[0]
    W = MLA_HEADS * MLA_DK
    rb = lambda w: (tm, w)
    outs = [((L, W), MXU_DTYPE, rb(W), lambda i: (i, 0)), ((L, W), MXU_DTYPE, rb(W), lambda i: (i, 0)),
            ((L, MLA_HEADS * MLA_V), MXU_DTYPE, rb(MLA_HEADS * MLA_V), lambda i: (i, 0))]
    return blocked_fwd(_mla_prep, _mla_prep_opds(proj, cos_p, sin_p, params, tm, False), outs, L // tm, "mla_prep_fwd")


def mla_prep_bwd(proj, cos_p, sin_p, params, dq, dk, dv, tm):
    L = proj.shape[0]
    return blocked_bwd(_mla_prep, _mla_prep_opds(proj, cos_p, sin_p, params, tm, True),
                       [rows(dq, tm), rows(dk, tm), rows(dv, tm)], L // tm, "mla_prep_bwd")


def _causal_scores(q, k, scale, row0, col0):
    s = _dot(q, k, _NT) * scale
    row = row0 + lax.broadcasted_iota(jnp.int32, s.shape, 0)
    col = col0 + lax.broadcasted_iota(jnp.int32, s.shape, 1)
    return jnp.where(col <= row, s, -jnp.inf)


def attn_fwd(q, k, v, scale, t):
    L = q.shape[0]
    n = L // t

    def body(q_ref, k_ref, v_ref, o_ref, lse_ref):
        i = pl.program_id(1)
        qb = q_ref[...]

        def step(j, carry):
            m, l, acc = carry
            kj = k_ref[pl.ds(pl.multiple_of(j * t, t), t), :]
            vj = v_ref[pl.ds(pl.multiple_of(j * t, t), t), :]
            s = _causal_scores(qb, kj, scale, i * t, j * t)
            m_new = jnp.maximum(m, jnp.max(s, axis=-1, keepdims=True))
            p = jnp.exp(s - m_new)
            alpha = jnp.exp(m - m_new)
            return m_new, alpha * l + jnp.sum(p, axis=-1, keepdims=True), alpha * acc + _dot(p, vj, _NN)

        init = (jnp.full((t, 1), -jnp.inf, F32), jnp.zeros((t, 1), F32), jnp.zeros((t, MLA_V), F32))
        m, l, acc = lax.fori_loop(0, i + 1, step, init)
        o_ref[...] = acc / l
        lse_ref[...] = jnp.broadcast_to(m + jnp.log(l), lse_ref.shape)

    hspec = lambda rows_, w: pl.BlockSpec((rows_, w), lambda h, i: (0, h))
    bspec = lambda w: pl.BlockSpec((t, w), lambda h, i: (i, h))
    return pl.pallas_call(
        body, name="attn_fwd", grid=(MLA_HEADS, n),
        in_specs=[bspec(MLA_DK), hspec(L, MLA_DK), hspec(L, MLA_V)],
        out_specs=[bspec(MLA_V), bspec(MLA_V)],
        out_shape=[jax.ShapeDtypeStruct((L, MLA_HEADS * MLA_V), F32)] * 2,
        compiler_params=_cparams(("parallel", "parallel")),
    )(q, k, v)


def attn_bwd_dq(q, k, v, o, lse, do, scale, t):
    L = q.shape[0]
    n = L // t

    def body(q_ref, k_ref, v_ref, o_ref, lse_ref, do_ref, dq_ref):
        i = pl.program_id(1)
        qb, dob = q_ref[...], do_ref[...]
        delta = jnp.sum(dob * o_ref[...], axis=-1, keepdims=True)
        lse_c = jnp.max(lse_ref[...], axis=-1, keepdims=True)

        def step(j, dq):
            kj = k_ref[pl.ds(pl.multiple_of(j * t, t), t), :]
            vj = v_ref[pl.ds(pl.multiple_of(j * t, t), t), :]
            p = jnp.exp(_causal_scores(qb, kj, scale, i * t, j * t) - lse_c)
            ds = p * (_dot(dob, vj, _NT) - delta) * scale
            return dq + _dot(ds, kj, _NN)

        dq_ref[...] = lax.fori_loop(0, i + 1, step, jnp.zeros((t, MLA_DK), F32))

    hspec = lambda w: pl.BlockSpec((L, w), lambda h, i: (0, h))
    bspec = lambda w: pl.BlockSpec((t, w), lambda h, i: (i, h))
    return pl.pallas_call(
        body, name="attn_bwd_dq", grid=(MLA_HEADS, n),
        in_specs=[bspec(MLA_DK), hspec(MLA_DK), hspec(MLA_V), bspec(MLA_V), bspec(MLA_V), bspec(MLA_V)],
        out_specs=bspec(MLA_DK),
        out_shape=jax.ShapeDtypeStruct((L, MLA_HEADS * MLA_DK), F32),
        compiler_params=_cparams(("parallel", "parallel")),
    )(q, k, v, o, lse, do)


def attn_bwd_dkv(q, k, v, o, lse, do, scale, t):
    L = q.shape[0]
    n = L // t

    def body(q_ref, k_ref, v_ref, o_ref, lse_ref, do_ref, dk_ref, dv_ref):
        j = pl.program_id(1)
        kb, vb = k_ref[...], v_ref[...]

        def step(i, carry):
            dk, dv = carry
            r = pl.ds(pl.multiple_of(i * t, t), t)
            qi, doi = q_ref[r, :], do_ref[r, :]
            delta = jnp.sum(doi * o_ref[r, :], axis=-1, keepdims=True)
            lse_c = jnp.max(lse_ref[r, :], axis=-1, keepdims=True)
            p = jnp.exp(_causal_scores(qi, kb, scale, i * t, j * t) - lse_c)
            ds = p * (_dot(doi, vb, _NT) - delta) * scale
            return dk + _dot(ds, qi, _TN), dv + _dot(p, doi, _TN)

        dk, dv = lax.fori_loop(j, n, step, (jnp.zeros((t, MLA_DK), F32), jnp.zeros((t, MLA_V), F32)))
        dk_ref[...] = dk
        dv_ref[...] = dv

    hspec = lambda w: pl.BlockSpec((L, w), lambda h, j: (0, h))
    bspec = lambda w: pl.BlockSpec((t, w), lambda h, j: (j, h))
    return pl.pallas_call(
        body, name="attn_bwd_dkv", grid=(MLA_HEADS, n),
        in_specs=[hspec(MLA_DK), bspec(MLA_DK), bspec(MLA_V), hspec(MLA_V), hspec(MLA_V), hspec(MLA_V)],
        out_specs=[bspec(MLA_DK), bspec(MLA_V)],
        out_shape=[jax.ShapeDtypeStruct((L, MLA_HEADS * MLA_DK), F32), jax.ShapeDtypeStruct((L, MLA_HEADS * MLA_V), F32)],
        compiler_params=_cparams(("parallel", "parallel")),
    )(q, k, v, o, lse, do)


S5_LANES = S5_GB * S5_STATE


def _cmul(ar, ai, br, bi):
    return ar * br - ai * bi, ar * bi + ai * br


def _a_powers(ar, ai, reverse):
    a2 = _cmul(ar, ai, ar, ai)
    a4 = _cmul(*a2, *a2)
    row = lax.broadcasted_iota(jnp.int32, (8, ar.shape[1]), 0)
    e = (8 - row) if reverse else (row + 1)
    tr, ti = jnp.ones((8, ar.shape[1]), F32), jnp.zeros((8, ar.shape[1]), F32)
    for bit, (pr, pi) in ((1, (ar, ai)), (2, a2), (4, a4), (8, _cmul(*a4, *a4))):
        nr, ni = _cmul(tr, ti, pr, pi)
        sel = (e & bit) != 0
        tr, ti = jnp.where(sel, nr, tr), jnp.where(sel, ni, ti)
    return ((ar, ai), a2, a4), (tr, ti)


def _scan8(xr, xi, pows, table, cr, ci, reverse):
    row = lax.broadcasted_iota(jnp.int32, xr.shape, 0)
    for d, (pr, pi) in zip((1, 2, 4), pows):
        if reverse:
            keep = row < 8 - d
            sr, si = pltpu.roll(xr, 8 - d, 0), pltpu.roll(xi, 8 - d, 0)
        else:
            keep = row >= d
            sr, si = pltpu.roll(xr, d, 0), pltpu.roll(xi, d, 0)
        sr, si = jnp.where(keep, sr, 0.0), jnp.where(keep, si, 0.0)
        mr, mi = _cmul(pr, pi, sr, si)
        xr, xi = xr + mr, xi + mi
    mr, mi = _cmul(table[0], table[1], cr, ci)
    return xr + mr, xi + mi


def _row_of(x, r):
    row = lax.broadcasted_iota(jnp.int32, x.shape, 0)
    return jnp.sum(jnp.where(row == r, x, 0.0), axis=0, keepdims=True)


def _s5_scan_fwd(h_re, h_im, ar, ai, L):
    pows, table = _a_powers(ar, ai, False)

    def step(i, carry):
        r = pl.ds(pl.multiple_of(i * 8, 8), 8)
        xr, xi = _scan8(h_re[r, :], h_im[r, :], pows, table, carry[0], carry[1], False)
        h_re[r, :] = xr
        h_im[r, :] = xi
        return _row_of(xr, 7), _row_of(xi, 7)

    z = jnp.zeros((1, ar.shape[1]), F32)
    lax.fori_loop(0, L // 8, step, (z, z))


def _s5_specs(L):
    return [pl.BlockSpec((L, 128), lambda g: (0, g)),
            pl.BlockSpec((1, 128, S5_LANES), lambda g: (g, 0, 0)), pl.BlockSpec((1, 128, S5_LANES), lambda g: (g, 0, 0)),
            pl.BlockSpec((1, 1, S5_LANES), lambda g: (g, 0, 0)), pl.BlockSpec((1, 1, S5_LANES), lambda g: (g, 0, 0)),
            pl.BlockSpec((1, S5_LANES, 128), lambda g: (g, 0, 0)), pl.BlockSpec((1, S5_LANES, 128), lambda g: (g, 0, 0))]


def s5_fwd(u, w_re, w_im, a_re, a_im, c_re, c_im):
    L, D = u.shape

    def body(u_ref, wr, wi, ar, ai, cr, ci, y_ref, h_re, h_im):
        ub = u_ref[...]
        h_re[...] = _dot(ub, wr[0], _NN)
        h_im[...] = _dot(ub, wi[0], _NN)
        _s5_scan_fwd(h_re, h_im, ar[0], ai[0], L)
        y_ref[...] = _dot(h_re[...], cr[0], _NN) - _dot(h_im[...], ci[0], _NN)

    return pl.pallas_call(
        body, name="s5_fwd", grid=(D // 128,),
        in_specs=_s5_specs(L), out_specs=pl.BlockSpec((L, 128), lambda g: (0, g)),
        out_shape=jax.ShapeDtypeStruct((L, D), F32),
        scratch_shapes=[pltpu.VMEM((L, S5_LANES), F32), pltpu.VMEM((L, S5_LANES), F32)],
        compiler_params=_cparams(("parallel",)),
    )(u, w_re, w_im, a_re, a_im, c_re, c_im)


def s5_bwd(u, w_re, w_im, a_re, a_im, c_re, c_im, dy, tc):
    L, D = u.shape
    nch = L // tc

    def body(u_ref, wr, wi, ar_ref, ai_ref, cr, ci, dy_ref, du_ref, dwr, dwi, dar, dai, dcr, dci, h_re, h_im, g_re, g_im):
        ar, ai = ar_ref[0], ai_ref[0]
        ub = u_ref[...]
        h_re[...] = _dot(ub, wr[0], _NN)
        h_im[...] = _dot(ub, wi[0], _NN)
        _s5_scan_fwd(h_re, h_im, ar, ai, L)
        dyb = dy_ref[...]
        dcr[0] = _dot(h_re[...], dyb, _TN)
        dci[0] = -_dot(h_im[...], dyb, _TN)
        pows, table = _a_powers(ar, -ai, True)
        dwr[0] = jnp.zeros((128, S5_LANES), F32)
        dwi[0] = jnp.zeros((128, S5_LANES), F32)
        z1 = jnp.zeros((1, S5_LANES), F32)
        z8 = jnp.zeros((8, S5_LANES), F32)

        def chunk(cc, carry):
            c0 = pl.multiple_of((nch - 1 - cc) * tc, tc)
            rows_c = pl.ds(c0, tc)
            dyc = dy_ref[rows_c, :]
            g_re[...] = _dot(dyc, cr[0], _NT)
            g_im[...] = -_dot(dyc, ci[0], _NT)

            def step(ii, cy):
                gr_c, gi_c, acc_r, acc_i = cy
                i8 = pl.multiple_of((tc // 8 - 1 - ii) * 8, 8)
                rl = pl.ds(i8, 8)
                xr, xi = _scan8(g_re[rl, :], g_im[rl, :], pows, table, gr_c, gi_c, True)
                g_re[rl, :] = xr
                g_im[rl, :] = xi
                t0 = c0 + i8
                hb_r, hb_i = h_re[pl.ds(t0, 8), :], h_im[pl.ds(t0, 8), :]
                tp = pl.multiple_of(jnp.maximum(t0 - 8, 0), 8)
                first = (t0 > 0).astype(F32)
                pr = _row_of(h_re[pl.ds(tp, 8), :], 7) * first
                pi = _row_of(h_im[pl.ds(tp, 8), :], 7) * first
                row = lax.broadcasted_iota(jnp.int32, xr.shape, 0)
                hp_r = jnp.where(row == 0, pr, pltpu.roll(hb_r, 1, 0))
                hp_i = jnp.where(row == 0, pi, pltpu.roll(hb_i, 1, 0))
                return (_row_of(xr, 0), _row_of(xi, 0),
                        acc_r + xr * hp_r + xi * hp_i, acc_i + xi * hp_r - xr * hp_i)

            cy = lax.fori_loop(0, tc // 8, step, carry)
            uc = u_ref[rows_c, :]
            gr, gi = g_re[...], g_im[...]
            du_ref[rows_c, :] = _dot(gr, wr[0], _NT) + _dot(gi, wi[0], _NT)
            dwr[0] += _dot(uc, gr, _TN)
            dwi[0] += _dot(uc, gi, _TN)
            return cy

        _, _, acc_r, acc_i = lax.fori_loop(0, nch, chunk, (z1, z1, z8, z8))
        dar[0] = jnp.sum(acc_r, axis=0, keepdims=True)
        dai[0] = jnp.sum(acc_i, axis=0, keepdims=True)

    specs = _s5_specs(L)
    return pl.pallas_call(
        body, name="s5_bwd", grid=(D // 128,),
        in_specs=specs + [pl.BlockSpec((L, 128), lambda g: (0, g))],
        out_specs=[pl.BlockSpec((L, 128), lambda g: (0, g))] + specs[1:],
        out_shape=[jax.ShapeDtypeStruct((L, D), F32)] + [jax.ShapeDtypeStruct(x.shape, F32)
                                                        for x in (w_re, w_im, a_re, a_im, c_re, c_im)],
        scratch_shapes=[pltpu.VMEM((L, S5_LANES), F32), pltpu.VMEM((L, S5_LANES), F32),
                        pltpu.VMEM((tc, S5_LANES), F32), pltpu.VMEM((tc, S5_LANES), F32)],
        compiler_params=_cparams(("parallel",)),
    )(u, w_re, w_im, a_re, a_im, c_re, c_im, dy)


def _s5_discretize(lr, li, ldt, br, bi):
    dt = jnp.exp(ldt)
    mag = jnp.exp(lr * dt)
    ar, ai = mag * jnp.cos(li * dt), mag * jnp.sin(li * dt)
    den = lr * lr + li * li
    zr = ((ar - 1.0) * lr + ai * li) / den
    zi = (ai * lr - (ar - 1.0) * li) / den
    p = lax.broadcasted_iota(jnp.int32, (S5_STATE, S5_STATE * S5_GROUP), 0)
    c = lax.broadcasted_iota(jnp.int32, (S5_STATE, S5_STATE * S5_GROUP), 1)
    rep = (c // S5_GROUP == p).astype(F32)
    zr, zi = _dot(zr, rep, _NN, HI), _dot(zi, rep, _NN, HI)
    return ar, ai, zr * br - zi * bi, zr * bi + zi * br


def _conv_shift(x, d):
    row = lax.broadcasted_iota(jnp.int32, x.shape, 0)
    return jnp.where(row >= d, pltpu.roll(x, d, 0), 0.0)


def _conv_unshift(x, d):
    n = x.shape[0]
    row = lax.broadcasted_iota(jnp.int32, x.shape, 0)
    return jnp.where(row < n - d, pltpu.roll(x, n - d, 0), 0.0)


@functools.partial(jax.custom_vjp, nondiff_argnums=(1,))
def _shift_rows(x, d):
    return _conv_shift(x, d)


_shift_rows.defvjp(lambda x, d: (_conv_shift(x, d), None), lambda d, _, g: (_conv_unshift(g, d),))


def _conv_gate(ug, uv, wg0, wg1, wg2, wv0, wv1, wv2, bg, bv):
    def conv(u, w0, w1, w2, b):
        return u * w2 + _shift_rows(u, 1) * w1 + _shift_rows(u, 2) * w0 + b
    return (jax.nn.silu(conv(ug, wg0, wg1, wg2, bg)) * conv(uv, wv0, wv1, wv2, bv),)


def _rms_fn(x, gain):
    return (_rms(x, gain),)


def _softmax_rows(s):
    e = jnp.exp(s - lax.stop_gradient(jnp.max(s, axis=-1, keepdims=True)))
    return e / jnp.sum(e, axis=-1, keepdims=True)


def _xa_core(qp, k, v, q_gain):
    dh = qp.shape[1] // XA_HEADS
    outs = []
    for h in range(XA_HEADS):
        sl = slice(h * dh, (h + 1) * dh)
        p = _softmax_rows(_dot(_rms(qp[:, sl], q_gain), k[:, sl], _NT) * (dh ** -0.5))
        outs.append(_dot(p, v[:, sl], _NN))
    return (jnp.concatenate(outs, axis=1),)


def _mem_kv(mem, mem_gain, wk, wv, k_gain):
    m = _rms(mem, mem_gain)
    kp = _dot(m, wk, _NN)
    dh = kp.shape[1] // XA_HEADS
    k = jnp.concatenate([_rms(kp[:, h * dh:(h + 1) * dh], k_gain) for h in range(XA_HEADS)], axis=1)
    return k, _dot(m, wv, _NN)


def _s5_post(y, u, d):
    return (jax.nn.gelu(y + d * u),)


def _glu(a, b):
    return (a * jax.nn.sigmoid(b),)


def _lb_first(logits):
    e = jnp.exp(logits - lax.stop_gradient(jnp.max(logits, axis=0, keepdims=True)))
    return (_row_of(e, 0) / jnp.sum(e, axis=0, keepdims=True),)


def _loss_fn(y, t):
    e = y - t
    part = 0.5 * jnp.sum(e * e) / y.shape[1]
    return e * (1.0 / y.shape[1]), jnp.full((8, 128), part / (8 * 128), F32)


def _out(shape, dtype, tm):
    return (shape, dtype, (tm, shape[1]), lambda i: (i, 0))


def rms_fwd(h, gain, tm, dtype):
    return blocked_fwd(_rms_fn, [rows(h, tm), full(gain)], [_out(h.shape, dtype, tm)], h.shape[0] // tm, "rms_fwd")[0]


def rms_bwd(h, gain, dy, tm):
    return blocked_bwd(_rms_fn, [rows(h, tm, 'blk'), full(gain, 'acc')], [rows(dy, tm)], h.shape[0] // tm, "rms_bwd")


def adamw(w, g, m, v, name):
    R = w.shape[0]
    tm = _tile(R, 256)

    def body(w_ref, g_ref, m_ref, v_ref, d_ref, nm_ref, nv_ref):
        g_ = g_ref[...]
        m_ = ADAM_B1 * m_ref[...] + (1.0 - ADAM_B1) * g_
        v_ = ADAM_B2 * v_ref[...] + (1.0 - ADAM_B2) * jnp.square(g_)
        m_hat = m_ / (1.0 - ADAM_B1 ** ADAM_STEP)
        v_hat = v_ / (1.0 - ADAM_B2 ** ADAM_STEP)
        d_ref[...] = -ADAM_LR * (m_hat / (jnp.sqrt(v_hat) + ADAM_EPS) + ADAM_WD * w_ref[...])
        nm_ref[...] = m_
        nv_ref[...] = v_

    spec = pl.BlockSpec((tm, w.shape[1]), lambda i: (i, 0))
    return pl.pallas_call(
        body, name=name, grid=(R // tm,), in_specs=[spec] * 4, out_specs=[spec] * 3,
        out_shape=[jax.ShapeDtypeStruct(w.shape, F32)] * 3, compiler_params=_cparams(("parallel",)),
    )(w, g, m, v)


def add_n(xs, name):
    R, C = xs[0].shape
    tm = _tile(R, 256)

    def body(*refs):
        acc = refs[0][...]
        for r in refs[1:-1]:
            acc = acc + r[...]
        refs[-1][...] = acc

    spec = pl.BlockSpec((tm, C), lambda i: (i, 0))
    return pl.pallas_call(
        body, name=name, grid=(R // tm,), in_specs=[spec] * len(xs), out_specs=spec,
        out_shape=jax.ShapeDtypeStruct((R, C), F32), compiler_params=_cparams(("parallel",)),
    )(*xs)


_HBM = pl.BlockSpec(memory_space=pltpu.HBM)


def _my_place():
    return lax.axis_index("x"), lax.axis_index("y"), lax.axis_index("c")


def pair_exchange(x, name):
    def body(x_ref, o_ref, send_sem, recv_sem):
        mx, my, mc = _my_place()
        cp = pltpu.make_async_remote_copy(src_ref=x_ref, dst_ref=o_ref, send_sem=send_sem, recv_sem=recv_sem,
                                          device_id=(mx, my, 1 - mc), device_id_type=MESH)
        cp.start()
        cp.wait()

    return pl.pallas_call(
        body, name=name, in_specs=[_HBM], out_specs=_HBM, out_shape=jax.ShapeDtypeStruct(x.shape, x.dtype),
        scratch_shapes=[pltpu.SemaphoreType.DMA, pltpu.SemaphoreType.DMA],
        compiler_params=pltpu.CompilerParams(has_side_effects=True),
    )(x)


def chip_exchange(x, gather, name):
    shape = ((4,) + x.shape) if gather else x.shape

    def body(x_ref, o_ref, send_sems, recv_sems, local_sem):
        mx, my, mc = _my_place()
        q = 2 * mx + my
        mine = pltpu.make_async_copy(x_ref if gather else x_ref.at[q], o_ref.at[q], local_sem)
        mine.start()
        copies = []
        for j in (1, 2, 3):
            tx, ty = mx ^ (j >> 1), my ^ (j & 1)
            src = x_ref if gather else x_ref.at[2 * tx + ty]
            copies.append(pltpu.make_async_remote_copy(src_ref=src, dst_ref=o_ref.at[q], send_sem=send_sems.at[j - 1],
                                                       recv_sem=recv_sems.at[j - 1], device_id=(tx, ty, mc),
                                                       device_id_type=MESH))
        for cp in copies:
            cp.start()
        for cp in copies:
            cp.wait()
        mine.wait()

    return pl.pallas_call(
        body, name=name, in_specs=[_HBM], out_specs=_HBM, out_shape=jax.ShapeDtypeStruct(shape, x.dtype),
        scratch_shapes=[pltpu.SemaphoreType.DMA((3,)), pltpu.SemaphoreType.DMA((3,)), pltpu.SemaphoreType.DMA],
        compiler_params=pltpu.CompilerParams(has_side_effects=True),
    )(x)


WEIGHTS = ['norm_mix', 'norm_xa', 'norm_mem', 'norm_ffn', 'xa_wq', 'xa_wk', 'xa_wv', 'xa_wo', 'xa_q_norm', 'xa_k_norm',
           'ffn_w_up', 'ffn_conv_w', 'ffn_conv_b', 'ffn_w_down', 'hg_lb_logits', 'mix_w_in', 'hg_out_norm',
           'mla_q_a_norm', 'mla_w_uq', 'mla_kv_a_norm', 'mla_w_ukv', 'mla_qn_nope', 'mla_qn_rope', 'mla_kn_nope',
           'mla_kn_rope', 'mix_w_out', 's5_lam_re', 's5_lam_im', 's5_log_dt', 's5_b_re', 's5_b_im', 's5_c_re',
           's5_c_im', 's5_d', 's5_w_glu_a', 's5_w_glu_b']
INPUTS = ['x', 'mem', 'positions'] + WEIGHTS + ['loss_target'] + ['m_' + n for n in WEIGHTS] + ['v_' + n for n in WEIGHTS]
SHARD_AXIS = {'xa_wq': 1, 'xa_wk': 1, 'xa_wv': 1, 'xa_wo': 1, 'ffn_w_up': 2, 'ffn_conv_w': 2, 'ffn_w_down': 1,
              'mix_w_in': 2, 'mla_w_uq': 2, 'mla_w_ukv': 2, 'mix_w_out': 1, 's5_d': 1, 's5_w_glu_a': 1, 's5_w_glu_b': 1}
SHARDED = [n for n in WEIGHTS if n in SHARD_AXIS]
REPLICATED = [n for n in WEIGHTS if n not in SHARD_AXIS]
ELEMENTWISE_SHARDED = ('ffn_conv_w', 's5_d')
N_CHIPS = 4
PACK_W = 1024
ROW_MULT = 256


def _pack(flats, mult=ROW_MULT):
    flat = jnp.concatenate([f.reshape(-1) for f in flats])
    unit = mult * PACK_W
    n = -(-flat.shape[0] // unit) * unit
    return jnp.pad(flat, (0, n - flat.shape[0])).reshape(n // PACK_W, PACK_W)


def _unpack(packed, shapes):
    flat, out, o = packed.reshape(-1), [], 0
    for s in shapes:
        n = math.prod(s)
        out.append(flat[o:o + n].reshape(s))
        o += n
    return out


def _rope_pad(w):
    z = jnp.zeros(w.shape[:-1] + (MLA_ROPE // 2,), w.dtype)
    return jnp.concatenate([w[..., :MLA_ROPE // 2], z, w[..., MLA_ROPE // 2:], z], axis=-1)


def _rope_unpad(g):
    return jnp.concatenate([g[..., :MLA_ROPE // 2], g[..., 64:64 + MLA_ROPE // 2]], axis=-1)


def _blockdiag_in(bb):
    nb = bb.shape[0] // S5_GB
    t = bb.reshape(nb, S5_GB, S5_STATE, S5_GROUP).transpose(0, 1, 3, 2)
    return jnp.einsum('bgmp,gh->bgmhp', t, jnp.eye(S5_GB, dtype=bb.dtype)).reshape(nb, S5_GB * S5_GROUP, S5_LANES)


def _blockdiag_in_t(dw):
    nb = dw.shape[0]
    t = jnp.einsum('bgmhp,gh->bgmp', dw.reshape(nb, S5_GB, S5_GROUP, S5_GB, S5_STATE), jnp.eye(S5_GB, dtype=dw.dtype))
    return t.transpose(0, 1, 3, 2).reshape(nb * S5_GB, S5_STATE, S5_GROUP)


def _blockdiag_out(c):
    nb = c.shape[0] // S5_GB
    t = c.reshape(nb, S5_GB, S5_GROUP, S5_STATE).transpose(0, 1, 3, 2)
    return jnp.einsum('bgpm,gh->bgphm', t, jnp.eye(S5_GB, dtype=c.dtype)).reshape(nb, S5_LANES, S5_GB * S5_GROUP)


def _blockdiag_out_t(dc):
    nb = dc.shape[0]
    t = jnp.einsum('bgphm,gh->bgpm', dc.reshape(nb, S5_GB, S5_STATE, S5_GB, S5_GROUP), jnp.eye(S5_GB, dtype=dc.dtype))
    return t.transpose(0, 1, 3, 2).reshape(nb * S5_GB, S5_GROUP, S5_STATE)


def _gather_weights(P):
    flats = []
    for n in SHARDED:
        w = P[n]
        flats.append(lax.bitcast_convert_type(w, BF16) if n in ELEMENTWISE_SHARDED else w.astype(BF16))
    got = chip_exchange(_pack(flats), True, "gather_weights")
    shapes = [P[n].shape + ((2,) if n in ELEMENTWISE_SHARDED else ()) for n in SHARDED]
    per_chip = [_unpack(got[q], shapes) for q in range(N_CHIPS)]
    full_w = {}
    for i, n in enumerate(SHARDED):
        parts = [per_chip[q][i] for q in range(N_CHIPS)]
        if n in ELEMENTWISE_SHARDED:
            parts = [lax.bitcast_convert_type(p, F32) for p in parts]
        full_w[n] = jnp.concatenate(parts, axis=SHARD_AXIS[n])
    return full_w


def _reduce_grads(G, P):
    _, _, mc = _my_place()
    rep = _pack([G[n] for n in REPLICATED], N_CHIPS * ROW_MULT)
    rr = rep.shape[0] // N_CHIPS
    per_chip = []
    for q in range(N_CHIPS):
        parts = []
        for n in SHARDED:
            s = P[n].shape[SHARD_AXIS[n]]
            parts.append(lax.slice_in_dim(G[n], q * s, (q + 1) * s, axis=SHARD_AXIS[n]))
        per_chip.append(jnp.concatenate([_pack(parts), rep[q * rr:(q + 1) * rr]], axis=0))
    rq = per_chip[0].shape[0]
    rs, h = rq - rr, rq // 2
    halves = jnp.stack(per_chip).reshape(N_CHIPS, 2, h, PACK_W)
    keep = lax.dynamic_index_in_dim(halves, mc, axis=1, keepdims=False).reshape(N_CHIPS * h, PACK_W)
    give = lax.dynamic_index_in_dim(halves, 1 - mc, axis=1, keepdims=False).reshape(N_CHIPS * h, PACK_W)
    pair = add_n([keep, pair_exchange(give, "grads_pair_exchange")], "grads_pair_sum")
    got = chip_exchange(pair.reshape(N_CHIPS, h, PACK_W), False, "grads_chip_all_to_all")
    mine = add_n([got[q] for q in range(N_CHIPS)], "grads_chip_sum")
    theirs = pair_exchange(mine, "grads_pair_gather")
    both = jnp.zeros((2, h, PACK_W), F32)
    both = lax.dynamic_update_index_in_dim(both, mine, mc, axis=0)
    both = lax.dynamic_update_index_in_dim(both, theirs, 1 - mc, axis=0).reshape(rq, PACK_W)
    g_sh_packed = both[:rs]
    g_rep_packed = chip_exchange(both[rs:], True, "grads_replicated_gather").reshape(N_CHIPS * rr, PACK_W)
    out = dict(zip(SHARDED, _unpack(g_sh_packed, [P[n].shape for n in SHARDED])))
    out.update(zip(REPLICATED, _unpack(g_rep_packed, [P[n].shape for n in REPLICATED])))
    return out, g_sh_packed, g_rep_packed


def _row(v):
    return v.reshape(1, -1)


def _xattn_fwd(h, mem, W, lyr, tm):
    g_xa, g_mem = _row(W['norm_xa'][lyr]), _row(W['norm_mem'][lyr])
    g_q, g_k = _row(W['xa_q_norm'][lyr]), _row(W['xa_k_norm'][lyr])
    wq, wk, wv, wo = (W[n][lyr] for n in ('xa_wq', 'xa_wk', 'xa_wv', 'xa_wo'))
    L, D = h.shape
    M = mem.shape[0]
    hx = rms_fwd(h, g_xa, tm, MXU_DTYPE)
    qp = matmul(hx, wq, name="xa_q")
    kv_opds = [full(mem), full(g_mem), full(wk), full(wv), full(g_k)]
    k, v = blocked_fwd(_mem_kv, kv_opds, [((M, D), F32, (M, D), lambda i: (0, 0))] * 2, 1, "xa_mem_kv")
    o = blocked_fwd(_xa_core, [rows(qp, tm), full(k), full(v), full(g_q)], [_out((L, D), MXU_DTYPE, tm)], L // tm,
                    "xa_core")[0]
    out = matmul(o, wo, add=h, name="xa_o")
    return out, (h, hx, qp, k, v, o)


def _xattn_bwd(dout, saved, mem, W, lyr, tm):
    h, hx, qp, k, v, o = saved
    g_xa, g_mem = _row(W['norm_xa'][lyr]), _row(W['norm_mem'][lyr])
    g_q, g_k = _row(W['xa_q_norm'][lyr]), _row(W['xa_k_norm'][lyr])
    wq, wk, wv, wo = (W[n][lyr] for n in ('xa_wq', 'xa_wk', 'xa_wv', 'xa_wo'))
    L = h.shape[0]
    do = matmul(dout, wo, "nt", name="xa_do")
    d_wo = matmul(o, dout, "tn", name="xa_dwo")
    dqp, dk, dv, d_gq = blocked_bwd(_xa_core, [rows(qp, tm, 'blk'), full(k, 'acc'), full(v, 'acc'), full(g_q, 'acc')],
                                    [rows(do, tm)], L // tm, "xa_core_bwd")
    d_wq = matmul(hx, dqp, "tn", name="xa_dwq")
    dhx = matmul(dqp, wq, "nt", name="xa_dhx")
    dh, d_gxa = rms_bwd(h, g_xa, dhx, tm)
    d_gmem, d_wk, d_wv, d_gk = blocked_bwd(
        _mem_kv, [full(mem), full(g_mem, 'acc'), full(wk, 'acc'), full(wv, 'acc'), full(g_k, 'acc')],
        [full(dk), full(dv)], 1, "xa_mem_kv_bwd")
    grads = {'norm_xa': d_gxa, 'norm_mem': d_gmem, 'xa_q_norm': d_gq, 'xa_k_norm': d_gk,
             'xa_wq': d_wq, 'xa_wk': d_wk, 'xa_wv': d_wv, 'xa_wo': d_wo}
    return dout + dh, grads


def _conv_params(W, lyr):
    cw, cb = W['ffn_conv_w'][lyr], W['ffn_conv_b'][lyr]
    F = cw.shape[1] // 2
    return [cw[0:1, :F], cw[1:2, :F], cw[2:3, :F], cw[0:1, F:], cw[1:2, F:], cw[2:3, F:], _row(cb[:F]), _row(cb[F:])]


def _ffn_fwd(h, W, lyr, tm):
    L, D = h.shape
    w_up, w_down = W['ffn_w_up'][lyr], W['ffn_w_down'][lyr]
    F = w_down.shape[0]
    hf = rms_fwd(h, _row(W['norm_ffn'][lyr]), tm, MXU_DTYPE)
    ug = matmul(hf, w_up[:, :F], name="ffn_up_gate")
    uv = matmul(hf, w_up[:, F:], name="ffn_up_value")
    opds = [cols(ug, 128), cols(uv, 128)] + [cols(p, 128) for p in _conv_params(W, lyr)]
    a = blocked_fwd(_conv_gate, opds, [((L, F), MXU_DTYPE, (L, 128), lambda j: (0, j))], F // 128, "ffn_conv_gate")[0]
    out = matmul(a, w_down, add=h, name="ffn_down")
    return out, (h, hf, ug, uv, a)


def _ffn_bwd(dout, saved, W, lyr, tm):
    h, hf, ug, uv, a = saved
    w_up, w_down = W['ffn_w_up'][lyr], W['ffn_w_down'][lyr]
    F = w_down.shape[0]
    da = matmul(dout, w_down, "nt", name="ffn_da")
    d_wdown = matmul(a, dout, "tn", name="ffn_dwdown")
    opds = [cols(ug, 128, 'blk'), cols(uv, 128, 'blk')] + [cols(p, 128, 'blk') for p in _conv_params(W, lyr)]
    gs = blocked_bwd(_conv_gate, opds, [cols(da, 128)], F // 128, "ffn_conv_gate_bwd")
    dug, duv = gs[0], gs[1]
    d_cw = jnp.concatenate([jnp.concatenate(gs[2:5], axis=0), jnp.concatenate(gs[5:8], axis=0)], axis=1)
    d_cb = jnp.concatenate([gs[8], gs[9]], axis=1)[0]
    d_wup = jnp.concatenate([matmul(hf, dug, "tn", name="ffn_dwup_gate"), matmul(hf, duv, "tn", name="ffn_dwup_value")],
                            axis=1)
    dhf = matmul(dug, w_up[:, :F], "nt", name="ffn_dhf_gate")
    dhf = matmul(duv, w_up[:, F:], "nt", add=dhf, name="ffn_dhf_value")
    dh, d_g = rms_bwd(h, _row(W['norm_ffn'][lyr]), dhf, tm)
    return dout + dh, {'norm_ffn': d_g, 'ffn_w_up': d_wup, 'ffn_conv_w': d_cw, 'ffn_conv_b': d_cb, 'ffn_w_down': d_wdown}


def _mla_params(W):
    w_uq = W['mla_w_uq'][0].reshape(MLA_Q_RANK, MLA_HEADS, MLA_QK)
    w_uq = jnp.concatenate([w_uq[..., :MLA_NOPE], _rope_pad(w_uq[..., MLA_NOPE:])], axis=-1)
    w_ukv = W['mla_w_ukv'][0].reshape(MLA_KV_RANK, MLA_HEADS, MLA_NOPE + MLA_V)
    w_ukv = jnp.concatenate([w_ukv[..., :MLA_NOPE].reshape(MLA_KV_RANK, -1), w_ukv[..., MLA_NOPE:].reshape(MLA_KV_RANK, -1)],
                            axis=1)
    return [_row(W['mla_q_a_norm'][0]), w_uq.reshape(MLA_Q_RANK, MLA_HEADS * MLA_DK), _row(W['mla_kv_a_norm'][0]), w_ukv,
            _row(W['mla_qn_nope'][0]), _row(_rope_pad(W['mla_qn_rope'][0])), _row(W['mla_kn_nope'][0]),
            _row(_rope_pad(W['mla_kn_rope'][0]))]


def _w_in_padded(W):
    w = W['mix_w_in'][0]
    return jnp.concatenate([w[:, :IN_WIDTH - MLA_ROPE], _rope_pad(w[:, IN_WIDTH - MLA_ROPE:])], axis=1)


def _mixer0_fwd(h, W, cos_p, sin_p, tm):
    L = h.shape[0]
    t = min(256, L)
    hn = rms_fwd(h, _row(W['norm_mix'][0]), tm, MXU_DTYPE)
    proj = matmul(hn, _w_in_padded(W), name="mix_in")
    logits = W['hg_lb_logits']
    lb = blocked_fwd(_lb_first, [full(logits)], [((1, HG_WIDTH), F32, (1, HG_WIDTH), lambda i: (0, 0))], 1, "hg_lb")[0]
    gain = _row(W['hg_out_norm'][0])
    o_hg, states = hgrn2_fwd(proj, lb, gain)
    mp = _mla_params(W)
    q, k, v = mla_prep_fwd(proj, cos_p, sin_p, mp, tm)
    scale = MLA_QK ** -0.5
    o_mla, lse = attn_fwd(q, k, v, scale, t)
    w_out = W['mix_w_out'][0]
    out = matmul(o_hg, w_out[:HG_WIDTH], add=h, name="mix_out_hg")
    out = matmul(o_mla, w_out[HG_WIDTH:], add=out, name="mix_out_mla")
    return out, (h, hn, proj, lb, o_hg, states, q, k, v, o_mla, lse)


def _mixer0_bwd(dout, saved, W, cos_p, sin_p, tm):
    h, hn, proj, lb, o_hg, states, q, k, v, o_mla, lse = saved
    L = h.shape[0]
    t = min(256, L)
    scale = MLA_QK ** -0.5
    w_out = W['mix_w_out'][0]
    gain = _row(W['hg_out_norm'][0])
    do_hg = matmul(dout, w_out[:HG_WIDTH], "nt", name="mix_do_hg")
    do_mla = matmul(dout, w_out[HG_WIDTH:], "nt", name="mix_do_mla")
    d_wout = jnp.concatenate([matmul(o_hg, dout, "tn", name="mix_dwout_hg"), matmul(o_mla, dout, "tn", name="mix_dwout_mla")],
                             axis=0)
    dq = attn_bwd_dq(q, k, v, o_mla, lse, do_mla, scale, t)
    dk, dv = attn_bwd_dkv(q, k, v, o_mla, lse, do_mla, scale, t)
    mp = _mla_params(W)
    dcq, dckv, dkpe, d_qa, d_wuq, d_kva, d_wukv, d_qnn, d_qnr, d_knn, d_knr = mla_prep_bwd(proj, cos_p, sin_p, mp, dq, dk, dv, tm)
    d_hg, d_lb, d_gain = hgrn2_bwd(proj, lb, gain, states, do_hg)
    dproj = jnp.concatenate([d_hg, dcq, dckv, dkpe], axis=1)
    d_win = matmul(hn, dproj, "tn", name="mix_dwin")
    dhn = matmul(dproj, _w_in_padded(W), "nt", name="mix_dhn")
    dh, d_g = rms_bwd(h, _row(W['norm_mix'][0]), dhn, tm)
    logits = W['hg_lb_logits']
    d_logits = blocked_bwd(_lb_first, [full(logits, 'acc')], [full(d_lb)], 1, "hg_lb_bwd")[0]
    d_wuq = d_wuq.reshape(MLA_Q_RANK, MLA_HEADS, MLA_DK)
    d_wuq = jnp.concatenate([d_wuq[..., :MLA_NOPE], _rope_unpad(d_wuq[..., MLA_NOPE:])], axis=-1)
    hw = MLA_HEADS * MLA_NOPE
    d_wukv = jnp.concatenate([d_wukv[:, :hw].reshape(MLA_KV_RANK, MLA_HEADS, MLA_NOPE),
                              d_wukv[:, hw:].reshape(MLA_KV_RANK, MLA_HEADS, MLA_V)], axis=-1)
    d_win = jnp.concatenate([d_win[:, :IN_WIDTH - MLA_ROPE], _rope_unpad(d_win[:, IN_WIDTH - MLA_ROPE:])], axis=1)
    grads = {'norm_mix': d_g, 'hg_lb_logits': d_logits, 'mix_w_in': d_win[None], 'hg_out_norm': d_gain,
             'mla_q_a_norm': d_qa, 'mla_w_uq': d_wuq.reshape(1, MLA_Q_RANK, -1), 'mla_kv_a_norm': d_kva,
             'mla_w_ukv': d_wukv.reshape(1, MLA_KV_RANK, -1), 'mla_qn_nope': d_qnn, 'mla_qn_rope': _rope_unpad(d_qnr),
             'mla_kn_nope': d_knn, 'mla_kn_rope': _rope_unpad(d_knr), 'mix_w_out': d_wout[None]}
    return dout + dh, grads


def _s5_inputs(W):
    G = W['s5_lam_re'].shape[1]
    return [W['s5_lam_re'][0], W['s5_lam_im'][0], W['s5_log_dt'][0].reshape(G, 1),
            W['s5_b_re'][0].reshape(G, -1), W['s5_b_im'][0].reshape(G, -1)]


def _mixer1_fwd(h, W, tm):
    L, D = h.shape
    u = rms_fwd(h, _row(W['norm_mix'][1]), tm, F32)
    di = _s5_inputs(W)
    G = di[0].shape[0]
    sq, wide = ((G, S5_STATE), F32, (G, S5_STATE), lambda i: (0, 0)), ((G, S5_STATE * S5_GROUP), F32, (G, S5_STATE * S5_GROUP), lambda i: (0, 0))
    ar, ai, bbr, bbi = blocked_fwd(_s5_discretize, [full(a) for a in di], [sq, sq, wide, wide], 1, "s5_discretize")
    nb = G // S5_GB
    core = (_blockdiag_in(bbr.reshape(G, S5_STATE, S5_GROUP)), _blockdiag_in(bbi.reshape(G, S5_STATE, S5_GROUP)),
            ar.reshape(nb, 1, S5_LANES), ai.reshape(nb, 1, S5_LANES),
            _blockdiag_out(W['s5_c_re'][0]), _blockdiag_out(W['s5_c_im'][0]))
    y = s5_fwd(u, *core)
    d = W['s5_d']
    y2 = blocked_fwd(_s5_post, [rows(y, tm), rows(u, tm), full(d)], [_out((L, D), MXU_DTYPE, tm)], L // tm, "s5_post")[0]
    w_ab = jnp.concatenate([W['s5_w_glu_a'][0], W['s5_w_glu_b'][0]], axis=1)
    ab = matmul(y2, w_ab, name="s5_glu_in")
    mix = blocked_fwd(_glu, [rows(ab, tm, col=0, width=D), rows(ab, tm, col=1, width=D)], [_out((L, D), F32, tm)], L // tm,
                      "s5_glu")[0]
    return h + mix, (h, u, core, y, y2, ab)


def _mixer1_bwd(dout, saved, W, tm):
    h, u, core, y, y2, ab = saved
    L, D = h.shape
    da, db = blocked_bwd(_glu, [rows(ab, tm, 'blk', col=0, width=D), rows(ab, tm, 'blk', col=1, width=D)], [rows(dout, tm)],
                         L // tm, "s5_glu_bwd")
    w_a, w_b = W['s5_w_glu_a'][0], W['s5_w_glu_b'][0]
    dy2 = matmul(da, w_a, "nt", name="s5_dy2_a")
    dy2 = matmul(db, w_b, "nt", add=dy2, name="s5_dy2_b")
    d_wa = matmul(y2, da, "tn", name="s5_dwa")
    d_wb = matmul(y2, db, "tn", name="s5_dwb")
    d = W['s5_d']
    dy, du_skip, d_d = blocked_bwd(_s5_post, [rows(y, tm, 'blk'), rows(u, tm, 'blk'), full(d, 'acc')], [rows(dy2, tm)], L // tm,
                                   "s5_post_bwd")
    du, dwr, dwi, dar, dai, dcr, dci = s5_bwd(u, *core, dy, min(256, L))
    di = _s5_inputs(W)
    G = di[0].shape[0]
    cts = [dar.reshape(G, S5_STATE), dai.reshape(G, S5_STATE), _blockdiag_in_t(dwr).reshape(G, -1), _blockdiag_in_t(dwi).reshape(G, -1)]
    d_lr, d_li, d_ldt, d_br, d_bi = blocked_bwd(_s5_discretize, [full(a, 'acc') for a in di], [full(c) for c in cts], 1,
                                                "s5_discretize_bwd")
    dh, d_g = rms_bwd(h, _row(W['norm_mix'][1]), du + du_skip, tm)
    bshape = W['s5_b_re'].shape
    grads = {'norm_mix': d_g, 's5_lam_re': d_lr[None], 's5_lam_im': d_li[None], 's5_log_dt': d_ldt.reshape(1, G),
             's5_b_re': d_br.reshape(bshape), 's5_b_im': d_bi.reshape(bshape), 's5_c_re': _blockdiag_out_t(dcr)[None],
             's5_c_im': _blockdiag_out_t(dci)[None], 's5_d': d_d, 's5_w_glu_a': d_wa[None], 's5_w_glu_b': d_wb[None]}
    return dout + dh, grads


def kernel(x, mem, positions, norm_mix, norm_xa, norm_mem, norm_ffn, xa_wq, xa_wk, xa_wv, xa_wo, xa_q_norm, xa_k_norm, ffn_w_up, ffn_conv_w, ffn_conv_b, ffn_w_down, hg_lb_logits, mix_w_in, hg_out_norm, mla_q_a_norm, mla_w_uq, mla_kv_a_norm, mla_w_ukv, mla_qn_nope, mla_qn_rope, mla_kn_nope, mla_kn_rope, mix_w_out, s5_lam_re, s5_lam_im, s5_log_dt, s5_b_re, s5_b_im, s5_c_re, s5_c_im, s5_d, s5_w_glu_a, s5_w_glu_b, loss_target, m_norm_mix, m_norm_xa, m_norm_mem, m_norm_ffn, m_xa_wq, m_xa_wk, m_xa_wv, m_xa_wo, m_xa_q_norm, m_xa_k_norm, m_ffn_w_up, m_ffn_conv_w, m_ffn_conv_b, m_ffn_w_down, m_hg_lb_logits, m_mix_w_in, m_hg_out_norm, m_mla_q_a_norm, m_mla_w_uq, m_mla_kv_a_norm, m_mla_w_ukv, m_mla_qn_nope, m_mla_qn_rope, m_mla_kn_nope, m_mla_kn_rope, m_mix_w_out, m_s5_lam_re, m_s5_lam_im, m_s5_log_dt, m_s5_b_re, m_s5_b_im, m_s5_c_re, m_s5_c_im, m_s5_d, m_s5_w_glu_a, m_s5_w_glu_b, v_norm_mix, v_norm_xa, v_norm_mem, v_norm_ffn, v_xa_wq, v_xa_wk, v_xa_wv, v_xa_wo, v_xa_q_norm, v_xa_k_norm, v_ffn_w_up, v_ffn_conv_w, v_ffn_conv_b, v_ffn_w_down, v_hg_lb_logits, v_mix_w_in, v_hg_out_norm, v_mla_q_a_norm, v_mla_w_uq, v_mla_kv_a_norm, v_mla_w_ukv, v_mla_qn_nope, v_mla_qn_rope, v_mla_kn_nope, v_mla_kn_rope, v_mix_w_out, v_s5_lam_re, v_s5_lam_im, v_s5_log_dt, v_s5_b_re, v_s5_b_im, v_s5_c_re, v_s5_c_im, v_s5_d, v_s5_w_glu_a, v_s5_w_glu_b):
    P = dict(locals())
    assert sorted(P) == sorted(INPUTS) and norm_mix.shape[0] == 2 and mix_w_in.shape[0] == 1
    x, mem, target = P['x'][0], P['mem'][0], P['loss_target'][0]
    L, D = x.shape
    tm = min(256, L)

    W = {n: P[n] for n in REPLICATED}
    W.update(_gather_weights(P))

    inv_freq = 1.0 / (ROPE_BASE ** (jnp.arange(0, MLA_ROPE, 2, dtype=F32) / MLA_ROPE))
    ang = P['positions'][0].astype(F32)[:, None] * inv_freq
    cos, sin, z = jnp.cos(ang), jnp.sin(ang), jnp.zeros_like(ang)
    cos_p = jnp.concatenate([cos, z, cos, z], axis=1)
    sin_p = jnp.concatenate([-sin, z, sin, z], axis=1)

    h, s_mix0 = _mixer0_fwd(x, W, cos_p, sin_p, tm)
    h, s_xa0 = _xattn_fwd(h, mem, W, 0, tm)
    h, s_ffn0 = _ffn_fwd(h, W, 0, tm)
    h, s_mix1 = _mixer1_fwd(h, W, tm)
    h, s_xa1 = _xattn_fwd(h, mem, W, 1, tm)
    h, s_ffn1 = _ffn_fwd(h, W, 1, tm)
    n = L // tm
    dh, parts = blocked_fwd(_loss_fn, [rows(h, tm), rows(target, tm)],
                            [_out((L, D), F32, tm), ((n * 8, 128), F32, (8, 128), lambda i: (i, 0))], n, "loss")
    loss = lax.psum(jnp.sum(parts), ("x", "y", "c"))

    layered = {}

    def collect(g, lyr):
        for k_, v_ in g.items():
            layered.setdefault(k_, {})[lyr] = v_

    dh, g = _ffn_bwd(dh, s_ffn1, W, 1, tm)
    collect(g, 1)
    dh, g = _xattn_bwd(dh, s_xa1, mem, W, 1, tm)
    collect(g, 1)
    dh, g = _mixer1_bwd(dh, s_mix1, W, tm)
    collect(g, 1)
    dh, g = _ffn_bwd(dh, s_ffn0, W, 0, tm)
    collect(g, 0)
    dh, g = _xattn_bwd(dh, s_xa0, mem, W, 0, tm)
    collect(g, 0)
    dx, g = _mixer0_bwd(dh, s_mix0, W, cos_p, sin_p, tm)
    collect(g, 0)

    G = {}
    for name in WEIGHTS:
        full_shape = W[name].shape
        by_layer = layered[name]
        if len(by_layer) == 2:
            G[name] = jnp.stack([by_layer[0].reshape(full_shape[1:]), by_layer[1].reshape(full_shape[1:])])
        else:
            G[name] = next(iter(by_layer.values())).reshape(full_shape)

    grads, g_sh, g_rep = _reduce_grads(G, P)

    def packed(prefix, names, mult):
        return _pack([P[prefix + n_] for n_ in names], mult)

    d_sh, m_sh, v_sh = adamw(packed('', SHARDED, ROW_MULT), g_sh, packed('m_', SHARDED, ROW_MULT),
                             packed('v_', SHARDED, ROW_MULT), "adamw_sharded")
    rep_mult = N_CHIPS * ROW_MULT
    d_rp, m_rp, v_rp = adamw(packed('', REPLICATED, rep_mult), g_rep, packed('m_', REPLICATED, rep_mult),
                             packed('v_', REPLICATED, rep_mult), "adamw_replicated")
    sh_shapes = [P[n_].shape for n_ in SHARDED]
    rp_shapes = [P[n_].shape for n_ in REPLICATED]
    outs = []
    for sh_pack, rp_pack in ((d_sh, d_rp), (m_sh, m_rp), (v_sh, v_rp)):
        d = dict(zip(SHARDED, _unpack(sh_pack, sh_shapes)))
        d.update(zip(REPLICATED, _unpack(rp_pack, rp_shapes)))
        outs.append([d[n_] for n_ in WEIGHTS])
    return (loss, dx[None], *[grads[n_] for n_ in WEIGHTS], *outs[0], *outs[1], *outs[2])
```

```python
import functools
import math

import jax
import jax.numpy as jnp
import numpy as np
from jax import lax
from jax.experimental import pallas as pl
from jax.experimental.pallas import tpu as pltpu

F32 = jnp.float32
BF16 = jnp.bfloat16
MXU_DTYPE = BF16
HI = lax.Precision.HIGHEST
V7X_VMEM_LIMIT_BYTES = 56 * 1024 * 1024
EPS = 1e-6
MESH = pl.DeviceIdType.MESH

HG_HEADS, HG_DIM = 4, 128
HG_WIDTH = HG_HEADS * HG_DIM
HG_SUB = 16
HG_BLOCK = 64
MLA_HEADS, MLA_Q_RANK, MLA_KV_RANK = 4, 256, 128
MLA_NOPE, MLA_ROPE, MLA_V = 128, 64, 128
MLA_QK = MLA_NOPE + MLA_ROPE
MLA_DK = 256
ROPE_BASE = 10000.0
IN_WIDTH = 4 * HG_WIDTH + MLA_Q_RANK + MLA_KV_RANK + MLA_ROPE
IN_PAD = 4 * HG_WIDTH + MLA_Q_RANK + MLA_KV_RANK + 128
S5_GROUP, S5_STATE = 16, 64
S5_GB = 8
DT_MIN, DT_MAX = 1e-3, 1e-1
XA_HEADS = 4
CONV_W = 3
ADAM_LR, ADAM_B1, ADAM_B2, ADAM_EPS, ADAM_WD, ADAM_STEP = 0.001, 0.9, 0.999, 1e-08, 0.01, 10


def _cparams(sem):
    return pltpu.CompilerParams(dimension_semantics=sem, vmem_limit_bytes=V7X_VMEM_LIMIT_BYTES)


class Opd:
    def __init__(self, arr, block, imap, grad=None, gshape=None, gimap=None):
        self.arr, self.block, self.imap, self.grad = arr, block, imap, grad
        self.gshape = arr.shape if gshape is None else gshape
        self.gimap = imap if gimap is None else gimap

    def spec(self):
        return pl.BlockSpec(self.block, self.imap)

    def gspec(self):
        return pl.BlockSpec(self.block, self.gimap)


def rows(arr, tm, grad=None, col=0, width=None):
    width = arr.shape[1] if width is None else width
    return Opd(arr, (tm, width), lambda i, c=col: (i, c), grad, (arr.shape[0], width), lambda i: (i, 0))


def cols(arr, tn, grad=None):
    return Opd(arr, (arr.shape[0], tn), lambda j: (0, j), grad)


def full(arr, grad=None):
    return Opd(arr, arr.shape, lambda i: (0, 0), grad)


def _load(ref):
    v = ref[...]
    return v.astype(F32) if jnp.issubdtype(v.dtype, jnp.floating) else v


def blocked_fwd(f, opds, outs, n, name):
    n_in = len(opds)

    def body(*refs):
        ys = f(*[_load(r) for r in refs[:n_in]])
        for r, y in zip(refs[n_in:], ys):
            r[...] = y.astype(r.dtype)

    res = pl.pallas_call(
        body, name=name, grid=(n,),
        in_specs=[o.spec() for o in opds],
        out_specs=[pl.BlockSpec(b, m) for (_, _, b, m) in outs],
        out_shape=[jax.ShapeDtypeStruct(s, d) for (s, d, _, _) in outs],
        compiler_params=_cparams(("parallel",)),
    )(*[o.arr for o in opds])
    return res


def blocked_bwd(f, opds, dys, n, name):
    n_in, n_dy = len(opds), len(dys)
    diff = [i for i, o in enumerate(opds) if o.grad]

    def body(*refs):
        vals = [_load(r) for r in refs[:n_in]]

        def fd(*dv):
            allv = list(vals)
            for i, v in zip(diff, dv):
                allv[i] = v
            return tuple(f(*allv))

        ys, vjp = jax.vjp(fd, *[vals[i] for i in diff])
        cts = tuple(_load(r).astype(y.dtype) for r, y in zip(refs[n_in:n_in + n_dy], ys))
        gs = vjp(cts)
        for r, g, i in zip(refs[n_in + n_dy:], gs, diff):
            if opds[i].grad == 'acc':
                @pl.when(pl.program_id(0) == 0)
                def _(r=r):
                    r[...] = jnp.zeros(r.shape, r.dtype)
                r[...] += g.astype(r.dtype)
            else:
                r[...] = g.astype(r.dtype)

    any_acc = any(opds[i].grad == 'acc' for i in diff)
    res = pl.pallas_call(
        body, name=name, grid=(n,),
        in_specs=[o.spec() for o in opds] + [o.spec() for o in dys],
        out_specs=[opds[i].gspec() for i in diff],
        out_shape=[jax.ShapeDtypeStruct(opds[i].gshape, F32) for i in diff],
        compiler_params=_cparams(("arbitrary" if any_acc else "parallel",)),
    )(*[o.arr for o in opds], *[o.arr for o in dys])
    return res


def _tile(dim, want):
    for t in range(want - want % 16, 0, -16):
        if dim % t == 0:
            return t
    assert dim <= want, (dim, want)
    return dim


MATMUL_VMEM_BUDGET = 40 * 1024 * 1024
MATMUL_ROWS = 512


def _widest(N, fits):
    for t in range(N - N % 128, 0, -128):
        if N % t == 0 and fits(t):
            return t
    return N


def matmul(a, b, mode="nn", out_dtype=F32, add=None, name="matmul", col_blocks=None):
    sa, sb, so = a.dtype.itemsize, b.dtype.itemsize, jnp.dtype(out_dtype).itemsize
    has_add = add is not None
    if mode == "tn":
        (K, M), (K2, N) = a.shape, b.shape
        assert K == K2 and not has_add and out_dtype == F32, (a.shape, b.shape)
        tk = _tile(K, MATMUL_ROWS)
        tn = _widest(N, lambda t: 2 * (tk * M * sa + tk * t * sb + M * t * 4) <= MATMUL_VMEM_BUDGET)
        if col_blocks is not None:
            assert N % col_blocks == 0 and (N // col_blocks) % 128 == 0 and tn >= N // col_blocks, (N, col_blocks, tn)
            tn = N // col_blocks
            out_spec = pl.BlockSpec((None, M, tn), lambda j, k: (j, 0, 0))
            out_shape = jax.ShapeDtypeStruct((col_blocks, M, tn), F32)
        else:
            out_spec = pl.BlockSpec((M, tn), lambda j, k: (0, j))
            out_shape = jax.ShapeDtypeStruct((M, N), F32)

        def body(a_ref, b_ref, o_ref):
            r = lax.dot_general(a_ref[...].astype(MXU_DTYPE), b_ref[...].astype(MXU_DTYPE), ((_TN), ((), ())),
                                preferred_element_type=F32)

            @pl.when(pl.program_id(1) == 0)
            def _():
                o_ref[...] = r

            @pl.when(pl.program_id(1) > 0)
            def _():
                o_ref[...] += r

        return pl.pallas_call(
            body, name=name, grid=(N // tn, K // tk),
            in_specs=[pl.BlockSpec((tk, M), lambda j, k: (k, 0)), pl.BlockSpec((tk, tn), lambda j, k: (k, j))],
            out_specs=out_spec, out_shape=out_shape,
            compiler_params=_cparams(("parallel", "arbitrary")),
        )(a, b)

    (M, K) = a.shape
    N = b.shape[1] if mode == "nn" else b.shape[0]
    assert K == (b.shape[0] if mode == "nn" else b.shape[1]), (a.shape, b.shape, mode)
    tm = _tile(M, MATMUL_ROWS)
    tn = _widest(N, lambda t: 2 * (tm * K * sa + K * t * sb + tm * t * (so + 4 * has_add)) <= MATMUL_VMEM_BUDGET)
    dims = ((_NN if mode == "nn" else _NT), ((), ()))

    def body(*refs):
        r = lax.dot_general(refs[0][...].astype(MXU_DTYPE), refs[1][...].astype(MXU_DTYPE), dims, preferred_element_type=F32)
        if has_add:
            r = r + refs[2][...].astype(F32)
        refs[-1][...] = r.astype(refs[-1].dtype)

    b_spec = pl.BlockSpec((K, tn), lambda j, i: (0, j)) if mode == "nn" else pl.BlockSpec((tn, K), lambda j, i: (j, 0))
    in_specs = [pl.BlockSpec((tm, K), lambda j, i: (i, 0)), b_spec]
    args = [a, b]
    if has_add:
        in_specs.append(pl.BlockSpec((tm, tn), lambda j, i: (i, j)))
        args.append(add)
    return pl.pallas_call(
        body, name=name, grid=(N // tn, M // tm),
        in_specs=in_specs,
        out_specs=pl.BlockSpec((tm, tn), lambda j, i: (i, j)),
        out_shape=jax.ShapeDtypeStruct((M, N), out_dtype),
        compiler_params=_cparams(("parallel", "parallel")),
    )(*args)


def _dot(a, b, dims, precision=None):
    if precision is None:
        a, b = a.astype(MXU_DTYPE), b.astype(MXU_DTYPE)
    return lax.dot_general(a, b, (dims, ((), ())), precision=precision, preferred_element_type=F32)


_NN = ((1,), (0,))
_NT = ((1,), (1,))
_TN = ((0,), (0,))


def _rms(x, gain):
    return x * lax.rsqrt(jnp.mean(x * x, axis=-1, keepdims=True) + EPS) * gain


def _hg_block(st_t, q, fl, iv, g, lb, gain):
    row = lax.broadcasted_iota(jnp.int32, (HG_SUB, HG_SUB), 0)
    col = lax.broadcasted_iota(jnp.int32, (HG_SUB, HG_SUB), 1)
    tri = (row >= col).astype(F32)
    outs, states = [], []
    for h in range(HG_HEADS):
        sl = slice(h * HG_DIM, (h + 1) * HG_DIM)
        st = st_t[h * HG_DIM:(h + 1) * HG_DIM, :]
        lbh = lb[:, sl]
        fg = lbh + (1.0 - lbh) * jax.nn.sigmoid(fl[:, sl])
        lf, kk, qf, v = jnp.log(fg), 1.0 - fg, jax.nn.silu(q[:, sl]), iv[:, sl]
        parts = []
        for s in range(q.shape[0] // HG_SUB):
            r = slice(s * HG_SUB, (s + 1) * HG_SUB)
            b = _dot(tri, lf[r], _NN, HI)
            b_end = jnp.sum(lf[r], axis=0, keepdims=True)
            qe = qf[r] * jnp.exp(b)
            sc = _dot(qe, kk[r] * jnp.exp(-b), _NT) * tri
            parts.append(_dot(sc, v[r], _NN) + _dot(qe, st, _NT))
            st = st * jnp.exp(b_end) + _dot(v[r], kk[r] * jnp.exp(b_end - b), _TN)
        o = jnp.concatenate(parts, axis=0)
        outs.append(_rms(o, gain[:, sl]) * jax.nn.silu(g[:, sl]))
        states.append(st)
    return jnp.concatenate(states, axis=0), jnp.concatenate(outs, axis=1)


def _hg_specs(proj, nb):
    return [pl.BlockSpec((HG_BLOCK, HG_WIDTH), lambda i, c=c, f=nb: (f(i), c)) for c in range(4)]


def hgrn2_fwd(proj, lb, gain):
    L = proj.shape[0]
    n = L // HG_BLOCK

    def body(q, fl, iv, g, lb_r, gain_r, o_ref, st_ref, st):
        @pl.when(pl.program_id(0) == 0)
        def _():
            st[...] = jnp.zeros(st.shape, F32)

        st_ref[0] = st[...]
        new, o = _hg_block(st[...], q[...], fl[...], iv[...], g[...], lb_r[...], gain_r[...])
        st[...] = new
        o_ref[...] = o.astype(o_ref.dtype)

    pspec = pl.BlockSpec((1, HG_WIDTH), lambda i: (0, 0))
    return pl.pallas_call(
        body, name="hgrn2_fwd", grid=(n,),
        in_specs=_hg_specs(proj, lambda i: i) + [pspec, pspec],
        out_specs=[pl.BlockSpec((HG_BLOCK, HG_WIDTH), lambda i: (i, 0)),
                   pl.BlockSpec((1, HG_WIDTH, HG_DIM), lambda i: (i, 0, 0))],
        out_shape=[jax.ShapeDtypeStruct((L, HG_WIDTH), MXU_DTYPE),
                   jax.ShapeDtypeStruct((n, HG_WIDTH, HG_DIM), F32)],
        scratch_shapes=[pltpu.VMEM((HG_WIDTH, HG_DIM), F32)],
        compiler_params=_cparams(("arbitrary",)),
    )(proj, proj, proj, proj, lb, gain)


def hgrn2_bwd(proj, lb, gain, states, do):
    L = proj.shape[0]
    n = L // HG_BLOCK

    def body(q, fl, iv, g, lb_r, gain_r, st_r, do_r, dproj, dlb, dgain, dst):
        @pl.when(pl.program_id(0) == 0)
        def _():
            dst[...] = jnp.zeros(dst.shape, F32)
            dlb[...] = jnp.zeros(dlb.shape, F32)
            dgain[...] = jnp.zeros(dgain.shape, F32)

        _, vjp = jax.vjp(_hg_block, st_r[0], q[...], fl[...], iv[...], g[...], lb_r[...], gain_r[...])
        d_st, dq, dfl, div, dg, d_lb, d_gain = vjp((dst[...], do_r[...].astype(F32)))
        dst[...] = d_st
        dproj[:, 0 * HG_WIDTH:1 * HG_WIDTH] = dq
        dproj[:, 1 * HG_WIDTH:2 * HG_WIDTH] = dfl
        dproj[:, 2 * HG_WIDTH:3 * HG_WIDTH] = div
        dproj[:, 3 * HG_WIDTH:4 * HG_WIDTH] = dg
        dlb[...] += d_lb
        dgain[...] += d_gain

    rev = lambda i: n - 1 - i
    pspec = pl.BlockSpec((1, HG_WIDTH), lambda i: (0, 0))
    return pl.pallas_call(
        body, name="hgrn2_bwd", grid=(n,),
        in_specs=_hg_specs(proj, rev) + [pspec, pspec,
                                         pl.BlockSpec((1, HG_WIDTH, HG_DIM), lambda i: (rev(i), 0, 0)),
                                         pl.BlockSpec((HG_BLOCK, HG_WIDTH), lambda i: (rev(i), 0))],
        out_specs=[pl.BlockSpec((HG_BLOCK, 4 * HG_WIDTH), lambda i: (rev(i), 0)), pspec, pspec],
        out_shape=[jax.ShapeDtypeStruct((L, 4 * HG_WIDTH), F32),
                   jax.ShapeDtypeStruct((1, HG_WIDTH), F32), jax.ShapeDtypeStruct((1, HG_WIDTH), F32)],
        scratch_shapes=[pltpu.VMEM((HG_WIDTH, HG_DIM), F32)],
        compiler_params=_cparams(("arbitrary",)),
    )(proj, proj, proj, proj, lb, gain, states, do)


def _rope_rms(x, gain_p, cos_p, sin_p):
    n = x * lax.rsqrt(jnp.sum(x * x, axis=-1, keepdims=True) * (1.0 / MLA_ROPE) + EPS) * gain_p
    r = lax.broadcasted_iota(jnp.int32, (128, 128), 0)
    c = lax.broadcasted_iota(jnp.int32, (128, 128), 1)
    swap = (r == (c + 64) % 128).astype(F32)
    return n * cos_p + _dot(n, swap, _NN, HI) * sin_p


def _mla_prep(c_q, c_kv, kpe, cos_p, sin_p, q_a, w_uq, kv_a, w_ukv, qn_nope, qn_rope, kn_nope, kn_rope):
    q = _dot(_rms(c_q, q_a), w_uq, _NN)
    kv = _dot(_rms(c_kv, kv_a), w_ukv, _NN)
    k_pe = _rope_rms(kpe, kn_rope, cos_p, sin_p)
    qs, ks = [], []
    for h in range(MLA_HEADS):
        qs.append(_rms(q[:, h * MLA_DK:h * MLA_DK + MLA_NOPE], qn_nope))
        qs.append(_rope_rms(q[:, h * MLA_DK + MLA_NOPE:(h + 1) * MLA_DK], qn_rope, cos_p, sin_p))
        ks.append(_rms(kv[:, h * MLA_NOPE:(h + 1) * MLA_NOPE], kn_nope))
        ks.append(k_pe)
    return jnp.concatenate(qs, axis=1), jnp.concatenate(ks, axis=1), kv[:, MLA_HEADS * MLA_NOPE:]


def _mla_prep_opds(proj, cos_p, sin_p, params, tm, grads):
    g = (lambda k: k) if grads else (lambda k: None)
    c0 = 4 * HG_WIDTH
    return ([rows(proj, tm, g('blk'), col=c0 // MLA_Q_RANK, width=MLA_Q_RANK),
             rows(proj, tm, g('blk'), col=(c0 + MLA_Q_RANK) // 128, width=128),
             rows(proj, tm, g('blk'), col=(c0 + MLA_Q_RANK) // 128 + 1, width=128),
             rows(cos_p, tm), rows(sin_p, tm)] + [full(p, g('acc')) for p in params])


def mla_prep_fwd(proj, cos_p, sin_p, params, tm):
    L = proj.shape[0]
    W = MLA_HEADS * MLA_DK
    rb = lambda w: (tm, w)
    outs = [((L, W), MXU_DTYPE, rb(W), lambda i: (i, 0)), ((L, W), MXU_DTYPE, rb(W), lambda i: (i, 0)),
            ((L, MLA_HEADS * MLA_V), MXU_DTYPE, rb(MLA_HEADS * MLA_V), lambda i: (i, 0))]
    return blocked_fwd(_mla_prep, _mla_prep_opds(proj, cos_p, sin_p, params, tm, False), outs, L // tm, "mla_prep_fwd")


def mla_prep_bwd(proj, cos_p, sin_p, params, dq, dk, dv, tm):
    L = proj.shape[0]
    return blocked_bwd(_mla_prep, _mla_prep_opds(proj, cos_p, sin_p, params, tm, True),
                       [rows(dq, tm), rows(dk, tm), rows(dv, tm)], L // tm, "mla_prep_bwd")


def _causal_scores(q, k, scale, row0, col0):
    s = _dot(q, k, _NT) * scale
    row = row0 + lax.broadcasted_iota(jnp.int32, s.shape, 0)
    col = col0 + lax.broadcasted_iota(jnp.int32, s.shape, 1)
    return jnp.where(col <= row, s, -jnp.inf)


def attn_fwd(q, k, v, scale, t):
    L = q.shape[0]
    n = L // t

    def body(q_ref, k_ref, v_ref, o_ref, lse_ref):
        i = pl.program_id(1)
        qb = q_ref[...]

        def step(j, carry):
            m, l, acc = carry
            kj = k_ref[pl.ds(pl.multiple_of(j * t, t), t), :]
            vj = v_ref[pl.ds(pl.multiple_of(j * t, t), t), :]
            s = _causal_scores(qb, kj, scale, i * t, j * t)
            m_new = jnp.maximum(m, jnp.max(s, axis=-1, keepdims=True))
            p = jnp.exp(s - m_new)
            alpha = jnp.exp(m - m_new)
            return m_new, alpha * l + jnp.sum(p, axis=-1, keepdims=True), alpha * acc + _dot(p, vj, _NN)

        init = (jnp.full((t, 1), -jnp.inf, F32), jnp.zeros((t, 1), F32), jnp.zeros((t, MLA_V), F32))
        m, l, acc = lax.fori_loop(0, i + 1, step, init)
        o_ref[...] = acc / l
        lse_ref[...] = jnp.broadcast_to(m + jnp.log(l), lse_ref.shape)

    hspec = lambda rows_, w: pl.BlockSpec((rows_, w), lambda h, i: (0, h))
    bspec = lambda w: pl.BlockSpec((t, w), lambda h, i: (i, h))
    return pl.pallas_call(
        body, name="attn_fwd", grid=(MLA_HEADS, n),
        in_specs=[bspec(MLA_DK), hspec(L, MLA_DK), hspec(L, MLA_V)],
        out_specs=[bspec(MLA_V), bspec(MLA_V)],
        out_shape=[jax.ShapeDtypeStruct((L, MLA_HEADS * MLA_V), F32)] * 2,
        compiler_params=_cparams(("parallel", "parallel")),
    )(q, k, v)


def attn_bwd_dq(q, k, v, o, lse, do, scale, t):
    L = q.shape[0]
    n = L // t

    def body(q_ref, k_ref, v_ref, o_ref, lse_ref, do_ref, dq_ref):
        i = pl.program_id(1)
        qb, dob = q_ref[...], do_ref[...]
        delta = jnp.sum(dob * o_ref[...], axis=-1, keepdims=True)
        lse_c = jnp.max(lse_ref[...], axis=-1, keepdims=True)

        def step(j, dq):
            kj = k_ref[pl.ds(pl.multiple_of(j * t, t), t), :]
            vj = v_ref[pl.ds(pl.multiple_of(j * t, t), t), :]
            p = jnp.exp(_causal_scores(qb, kj, scale, i * t, j * t) - lse_c)
            ds = p * (_dot(dob, vj, _NT) - delta) * scale
            return dq + _dot(ds, kj, _NN)

        dq_ref[...] = lax.fori_loop(0, i + 1, step, jnp.zeros((t, MLA_DK), F32))

    hspec = lambda w: pl.BlockSpec((L, w), lambda h, i: (0, h))
    bspec = lambda w: pl.BlockSpec((t, w), lambda h, i: (i, h))
    return pl.pallas_call(
        body, name="attn_bwd_dq", grid=(MLA_HEADS, n),
        in_specs=[bspec(MLA_DK), hspec(MLA_DK), hspec(MLA_V), bspec(MLA_V), bspec(MLA_V), bspec(MLA_V)],
        out_specs=bspec(MLA_DK),
        out_shape=jax.ShapeDtypeStruct((L, MLA_HEADS * MLA_DK), F32),
        compiler_params=_cparams(("parallel", "parallel")),
    )(q, k, v, o, lse, do)


def attn_bwd_dkv(q, k, v, o, lse, do, scale, t):
    L = q.shape[0]
    n = L // t

    def body(q_ref, k_ref, v_ref, o_ref, lse_ref, do_ref, dk_ref, dv_ref):
        j = pl.program_id(1)
        kb, vb = k_ref[...], v_ref[...]

        def step(i, carry):
            dk, dv = carry
            r = pl.ds(pl.multiple_of(i * t, t), t)
            qi, doi = q_ref[r, :], do_ref[r, :]
            delta = jnp.sum(doi * o_ref[r, :], axis=-1, keepdims=True)
            lse_c = jnp.max(lse_ref[r, :], axis=-1, keepdims=True)
            p = jnp.exp(_causal_scores(qi, kb, scale, i * t, j * t) - lse_c)
            ds = p * (_dot(doi, vb, _NT) - delta) * scale
            return dk + _dot(ds, qi, _TN), dv + _dot(p, doi, _TN)

        dk, dv = lax.fori_loop(j, n, step, (jnp.zeros((t, MLA_DK), F32), jnp.zeros((t, MLA_V), F32)))
        dk_ref[...] = dk
        dv_ref[...] = dv

    hspec = lambda w: pl.BlockSpec((L, w), lambda h, j: (0, h))
    bspec = lambda w: pl.BlockSpec((t, w), lambda h, j: (j, h))
    return pl.pallas_call(
        body, name="attn_bwd_dkv", grid=(MLA_HEADS, n),
        in_specs=[hspec(MLA_DK), bspec(MLA_DK), bspec(MLA_V), hspec(MLA_V), hspec(MLA_V), hspec(MLA_V)],
        out_specs=[bspec(MLA_DK), bspec(MLA_V)],
        out_shape=[jax.ShapeDtypeStruct((L, MLA_HEADS * MLA_DK), F32), jax.ShapeDtypeStruct((L, MLA_HEADS * MLA_V), F32)],
        compiler_params=_cparams(("parallel", "parallel")),
    )(q, k, v, o, lse, do)


S5_LANES = S5_GB * S5_STATE


def _cmul(ar, ai, br, bi):
    return ar * br - ai * bi, ar * bi + ai * br


def _a_powers(ar, ai, reverse):
    a2 = _cmul(ar, ai, ar, ai)
    a4 = _cmul(*a2, *a2)
    row = lax.broadcasted_iota(jnp.int32, (8, ar.shape[1]), 0)
    e = (8 - row) if reverse else (row + 1)
    tr, ti = jnp.ones((8, ar.shape[1]), F32), jnp.zeros((8, ar.shape[1]), F32)
    for bit, (pr, pi) in ((1, (ar, ai)), (2, a2), (4, a4), (8, _cmul(*a4, *a4))):
        nr, ni = _cmul(tr, ti, pr, pi)
        sel = (e & bit) != 0
        tr, ti = jnp.where(sel, nr, tr), jnp.where(sel, ni, ti)
    return ((ar, ai), a2, a4), (tr, ti)


def _scan8(xr, xi, pows, table, cr, ci, reverse):
    row = lax.broadcasted_iota(jnp.int32, xr.shape, 0)
    for d, (pr, pi) in zip((1, 2, 4), pows):
        if reverse:
            keep = row < 8 - d
            sr, si = pltpu.roll(xr, 8 - d, 0), pltpu.roll(xi, 8 - d, 0)
        else:
            keep = row >= d
            sr, si = pltpu.roll(xr, d, 0), pltpu.roll(xi, d, 0)
        sr, si = jnp.where(keep, sr, 0.0), jnp.where(keep, si, 0.0)
        mr, mi = _cmul(pr, pi, sr, si)
        xr, xi = xr + mr, xi + mi
    mr, mi = _cmul(table[0], table[1], cr, ci)
    return xr + mr, xi + mi


def _row_of(x, r):
    row = lax.broadcasted_iota(jnp.int32, x.shape, 0)
    return jnp.sum(jnp.where(row == r, x, 0.0), axis=0, keepdims=True)


def _s5_scan_fwd(h_re, h_im, ar, ai, L):
    pows, table = _a_powers(ar, ai, False)

    def step(i, carry):
        r = pl.ds(pl.multiple_of(i * 8, 8), 8)
        xr, xi = _scan8(h_re[r, :], h_im[r, :], pows, table, carry[0], carry[1], False)
        h_re[r, :] = xr
        h_im[r, :] = xi
        return _row_of(xr, 7), _row_of(xi, 7)

    z = jnp.zeros((1, ar.shape[1]), F32)
    lax.fori_loop(0, L // 8, step, (z, z))


def _s5_specs(L):
    return [pl.BlockSpec((L, 128), lambda g: (0, g)),
            pl.BlockSpec((1, 128, S5_LANES), lambda g: (g, 0, 0)), pl.BlockSpec((1, 128, S5_LANES), lambda g: (g, 0, 0)),
            pl.BlockSpec((1, 1, S5_LANES), lambda g: (g, 0, 0)), pl.BlockSpec((1, 1, S5_LANES), lambda g: (g, 0, 0)),
            pl.BlockSpec((1, S5_LANES, 128), lambda g: (g, 0, 0)), pl.BlockSpec((1, S5_LANES, 128), lambda g: (g, 0, 0))]


def s5_fwd(u, w_re, w_im, a_re, a_im, c_re, c_im):
    L, D = u.shape

    def body(u_ref, wr, wi, ar, ai, cr, ci, y_ref, h_re, h_im):
        ub = u_ref[...]
        h_re[...] = _dot(ub, wr[0], _NN)
        h_im[...] = _dot(ub, wi[0], _NN)
        _s5_scan_fwd(h_re, h_im, ar[0], ai[0], L)
        y_ref[...] = _dot(h_re[...], cr[0], _NN) - _dot(h_im[...], ci[0], _NN)

    return pl.pallas_call(
        body, name="s5_fwd", grid=(D // 128,),
        in_specs=_s5_specs(L), out_specs=pl.BlockSpec((L, 128), lambda g: (0, g)),
        out_shape=jax.ShapeDtypeStruct((L, D), F32),
        scratch_shapes=[pltpu.VMEM((L, S5_LANES), F32), pltpu.VMEM((L, S5_LANES), F32)],
        compiler_params=_cparams(("parallel",)),
    )(u, w_re, w_im, a_re, a_im, c_re, c_im)


def s5_bwd(u, w_re, w_im, a_re, a_im, c_re, c_im, dy, tc):
    L, D = u.shape
    nch = L // tc

    def body(u_ref, wr, wi, ar_ref, ai_ref, cr, ci, dy_ref, du_ref, dwr, dwi, dar, dai, dcr, dci, h_re, h_im, g_re, g_im):
        ar, ai = ar_ref[0], ai_ref[0]
        ub = u_ref[...]
        h_re[...] = _dot(ub, wr[0], _NN)
        h_im[...] = _dot(ub, wi[0], _NN)
        _s5_scan_fwd(h_re, h_im, ar, ai, L)
        dyb = dy_ref[...]
        dcr[0] = _dot(h_re[...], dyb, _TN)
        dci[0] = -_dot(h_im[...], dyb, _TN)
        pows, table = _a_powers(ar, -ai, True)
        dwr[0] = jnp.zeros((128, S5_LANES), F32)
        dwi[0] = jnp.zeros((128, S5_LANES), F32)
        z1 = jnp.zeros((1, S5_LANES), F32)
        z8 = jnp.zeros((8, S5_LANES), F32)

        def chunk(cc, carry):
            c0 = pl.multiple_of((nch - 1 - cc) * tc, tc)
            rows_c = pl.ds(c0, tc)
            dyc = dy_ref[rows_c, :]
            g_re[...] = _dot(dyc, cr[0], _NT)
            g_im[...] = -_dot(dyc, ci[0], _NT)

            def step(ii, cy):
                gr_c, gi_c, acc_r, acc_i = cy
                i8 = pl.multiple_of((tc // 8 - 1 - ii) * 8, 8)
                rl = pl.ds(i8, 8)
                xr, xi = _scan8(g_re[rl, :], g_im[rl, :], pows, table, gr_c, gi_c, True)
                g_re[rl, :] = xr
                g_im[rl, :] = xi
                t0 = c0 + i8
                hb_r, hb_i = h_re[pl.ds(t0, 8), :], h_im[pl.ds(t0, 8), :]
                tp = pl.multiple_of(jnp.maximum(t0 - 8, 0), 8)
                first = (t0 > 0).astype(F32)
                pr = _row_of(h_re[pl.ds(tp, 8), :], 7) * first
                pi = _row_of(h_im[pl.ds(tp, 8), :], 7) * first
                row = lax.broadcasted_iota(jnp.int32, xr.shape, 0)
                hp_r = jnp.where(row == 0, pr, pltpu.roll(hb_r, 1, 0))
                hp_i = jnp.where(row == 0, pi, pltpu.roll(hb_i, 1, 0))
                return (_row_of(xr, 0), _row_of(xi, 0),
                        acc_r + xr * hp_r + xi * hp_i, acc_i + xi * hp_r - xr * hp_i)

            cy = lax.fori_loop(0, tc // 8, step, carry)
            uc = u_ref[rows_c, :]
            gr, gi = g_re[...], g_im[...]
            du_ref[rows_c, :] = _dot(gr, wr[0], _NT) + _dot(gi, wi[0], _NT)
            dwr[0] += _dot(uc, gr, _TN)
            dwi[0] += _dot(uc, gi, _TN)
            return cy

        _, _, acc_r, acc_i = lax.fori_loop(0, nch, chunk, (z1, z1, z8, z8))
        dar[0] = jnp.sum(acc_r, axis=0, keepdims=True)
        dai[0] = jnp.sum(acc_i, axis=0, keepdims=True)

    specs = _s5_specs(L)
    return pl.pallas_call(
        body, name="s5_bwd", grid=(D // 128,),
        in_specs=specs + [pl.BlockSpec((L, 128), lambda g: (0, g))],
        out_specs=[pl.BlockSpec((L, 128), lambda g: (0, g))] + specs[1:],
        out_shape=[jax.ShapeDtypeStruct((L, D), F32)] + [jax.ShapeDtypeStruct(x.shape, F32)
                                                        for x in (w_re, w_im, a_re, a_im, c_re, c_im)],
        scratch_shapes=[pltpu.VMEM((L, S5_LANES), F32), pltpu.VMEM((L, S5_LANES), F32),
                        pltpu.VMEM((tc, S5_LANES), F32), pltpu.VMEM((tc, S5_LANES), F32)],
        compiler_params=_cparams(("parallel",)),
    )(u, w_re, w_im, a_re, a_im, c_re, c_im, dy)


def _s5_discretize(lr, li, ldt, br, bi):
    dt = jnp.exp(ldt)
    mag = jnp.exp(lr * dt)
    ar, ai = mag * jnp.cos(li * dt), mag * jnp.sin(li * dt)
    den = lr * lr + li * li
    zr = ((ar - 1.0) * lr + ai * li) / den
    zi = (ai * lr - (ar - 1.0) * li) / den
    p = lax.broadcasted_iota(jnp.int32, (S5_STATE, S5_STATE * S5_GROUP), 0)
    c = lax.broadcasted_iota(jnp.int32, (S5_STATE, S5_STATE * S5_GROUP), 1)
    rep = (c // S5_GROUP == p).astype(F32)
    zr, zi = _dot(zr, rep, _NN, HI), _dot(zi, rep, _NN, HI)
    return ar, ai, zr * br - zi * bi, zr * bi + zi * br


def _conv_shift(x, d):
    row = lax.broadcasted_iota(jnp.int32, x.shape, 0)
    return jnp.where(row >= d, pltpu.roll(x, d, 0), 0.0)


def _conv_unshift(x, d):
    n = x.shape[0]
    row = lax.broadcasted_iota(jnp.int32, x.shape, 0)
    return jnp.where(row < n - d, pltpu.roll(x, n - d, 0), 0.0)


@functools.partial(jax.custom_vjp, nondiff_argnums=(1,))
def _shift_rows(x, d):
    return _conv_shift(x, d)


_shift_rows.defvjp(lambda x, d: (_conv_shift(x, d), None), lambda d, _, g: (_conv_unshift(g, d),))


def _conv_gate(ug, uv, wg0, wg1, wg2, wv0, wv1, wv2, bg, bv):
    def conv(u, w0, w1, w2, b):
        return u * w2 + _shift_rows(u, 1) * w1 + _shift_rows(u, 2) * w0 + b
    return (jax.nn.silu(conv(ug, wg0, wg1, wg2, bg)) * conv(uv, wv0, wv1, wv2, bv),)


def _rms_fn(x, gain):
    return (_rms(x, gain),)


def _softmax_rows(s):
    e = jnp.exp(s - lax.stop_gradient(jnp.max(s, axis=-1, keepdims=True)))
    return e / jnp.sum(e, axis=-1, keepdims=True)


def _xa_core(qp, k, v, q_gain):
    dh = qp.shape[1] // XA_HEADS
    outs = []
    for h in range(XA_HEADS):
        sl = slice(h * dh, (h + 1) * dh)
        p = _softmax_rows(_dot(_rms(qp[:, sl], q_gain), k[:, sl], _NT) * (dh ** -0.5))
        outs.append(_dot(p, v[:, sl], _NN))
    return (jnp.concatenate(outs, axis=1),)


def _mem_kv(mem, mem_gain, wk, wv, k_gain):
    m = _rms(mem, mem_gain)
    kp = _dot(m, wk, _NN)
    dh = kp.shape[1] // XA_HEADS
    k = jnp.concatenate([_rms(kp[:, h * dh:(h + 1) * dh], k_gain) for h in range(XA_HEADS)], axis=1)
    return k, _dot(m, wv, _NN)


def _s5_post(y, u, d):
    return (jax.nn.gelu(y + d * u),)


def _glu(a, b):
    return (a * jax.nn.sigmoid(b),)


def _lb_first(logits):
    e = jnp.exp(logits - lax.stop_gradient(jnp.max(logits, axis=0, keepdims=True)))
    return (_row_of(e, 0) / jnp.sum(e, axis=0, keepdims=True),)


def _loss_fn(y, t):
    e = y - t
    part = 0.5 * jnp.sum(e * e) / y.shape[1]
    return e * (1.0 / y.shape[1]), jnp.full((8, 128), part / (8 * 128), F32)


def _out(shape, dtype, tm):
    return (shape, dtype, (tm, shape[1]), lambda i: (i, 0))


def rms_fwd(h, gain, tm, dtype):
    return blocked_fwd(_rms_fn, [rows(h, tm), full(gain)], [_out(h.shape, dtype, tm)], h.shape[0] // tm, "rms_fwd")[0]


def rms_bwd(h, gain, dy, tm):
    return blocked_bwd(_rms_fn, [rows(h, tm, 'blk'), full(gain, 'acc')], [rows(dy, tm)], h.shape[0] // tm, "rms_bwd")


def adamw(w, g, m, v, name):
    R = w.shape[0]
    tm = _tile(R, 256)
    assert g.shape == w.shape == m.shape == v.shape, (name, w.shape, g.shape)

    def body(w_ref, g_ref, m_ref, v_ref, d_ref, nm_ref, nv_ref):
        g_ = g_ref[...]
        m_ = ADAM_B1 * m_ref[...] + (1.0 - ADAM_B1) * g_
        v_ = ADAM_B2 * v_ref[...] + (1.0 - ADAM_B2) * jnp.square(g_)
        m_hat = m_ / (1.0 - ADAM_B1 ** ADAM_STEP)
        v_hat = v_ / (1.0 - ADAM_B2 ** ADAM_STEP)
        d_ref[...] = -ADAM_LR * (m_hat / (jnp.sqrt(v_hat) + ADAM_EPS) + ADAM_WD * w_ref[...])
        nm_ref[...] = m_
        nv_ref[...] = v_

    spec = pl.BlockSpec((tm, w.shape[1]), lambda i: (i, 0))
    return pl.pallas_call(
        body, name=name, grid=(R // tm,), in_specs=[spec] * 4, out_specs=[spec] * 3,
        out_shape=[jax.ShapeDtypeStruct(w.shape, F32)] * 3, compiler_params=_cparams(("parallel",)),
    )(w, g, m, v)


def add2(x, y, name):
    shape = x.shape
    x, y = x.reshape(-1, shape[-1]), y.reshape(-1, shape[-1])
    R, C = x.shape
    tm = _tile(R, 256)

    def body(x_ref, y_ref, o_ref):
        o_ref[...] = x_ref[...] + y_ref[...]

    spec = pl.BlockSpec((tm, C), lambda i: (i, 0))
    return pl.pallas_call(
        body, name=name, grid=(R // tm,), in_specs=[spec, spec], out_specs=spec,
        out_shape=jax.ShapeDtypeStruct((R, C), F32), compiler_params=_cparams(("parallel",)),
    )(x, y).reshape(shape)


def add_chips(x, name):
    n, R, C = x.shape
    tm = _tile(R, 256)

    def body(*refs):
        acc = refs[0][...]
        for r in refs[1:-1]:
            acc = acc + r[...]
        refs[-1][...] = acc

    return pl.pallas_call(
        body, name=name, grid=(R // tm,),
        in_specs=[pl.BlockSpec((None, tm, C), lambda i, q=q: (q, i, 0)) for q in range(n)],
        out_specs=pl.BlockSpec((tm, C), lambda i: (i, 0)),
        out_shape=jax.ShapeDtypeStruct((R, C), F32), compiler_params=_cparams(("parallel",)),
    )(*([x] * n))


_HBM = pl.BlockSpec(memory_space=pltpu.HBM)
N_CHIPS = 4


def _my_place():
    return lax.axis_index("x"), lax.axis_index("y"), lax.axis_index("c")


def _window(ref, axis, start, size):
    idx = [slice(None)] * len(ref.shape)
    idx[axis] = pl.ds(start, size)
    return ref.at[tuple(idx)]


def _comm_call(body, name, xs, out_shapes, n_remote, n_local):
    return pl.pallas_call(
        body, name=name, in_specs=[_HBM] * len(xs), out_specs=[_HBM] * len(out_shapes), out_shape=out_shapes,
        scratch_shapes=[pltpu.SemaphoreType.DMA((n_remote,)), pltpu.SemaphoreType.DMA((n_remote,)),
                        pltpu.SemaphoreType.DMA((max(n_local, 1),))],
        compiler_params=pltpu.CompilerParams(has_side_effects=True),
    )(*xs)


def _run(copies):
    for cp in copies:
        cp.start()
    for cp in copies:
        cp.wait()


def _other_chips(mx, my):
    return [(mx ^ (j >> 1), my ^ (j & 1)) for j in (1, 2, 3)]


def chip_gather(xs, axes, name):
    n = len(xs)
    shapes = []
    for x, ax in zip(xs, axes):
        s = list(x.shape)
        if ax is None:
            s = [N_CHIPS] + s
        else:
            s[ax] *= N_CHIPS
        shapes.append(jax.ShapeDtypeStruct(tuple(s), x.dtype))

    def body(*refs):
        x_refs, o_refs = refs[:n], refs[n:2 * n]
        send_sems, recv_sems, local_sems = refs[2 * n:]
        mx, my, mc = _my_place()
        q = 2 * mx + my
        copies = []
        for i, (x_ref, o_ref, ax) in enumerate(zip(x_refs, o_refs, axes)):
            dst = o_ref.at[q] if ax is None else _window(o_ref, ax, q * x_ref.shape[ax], x_ref.shape[ax])
            copies.append(pltpu.make_async_copy(x_ref, dst, local_sems.at[i]))
            for j, (tx, ty) in enumerate(_other_chips(mx, my)):
                copies.append(pltpu.make_async_remote_copy(
                    src_ref=x_ref, dst_ref=dst, send_sem=send_sems.at[3 * i + j], recv_sem=recv_sems.at[3 * i + j],
                    device_id=(tx, ty, mc), device_id_type=MESH))
        _run(copies)

    return _comm_call(body, name, xs, shapes, 3 * n, n)


def pair_split(xs, name):
    n = len(xs)
    shapes = [jax.ShapeDtypeStruct((x.shape[0], x.shape[1] // 2, x.shape[2]), x.dtype) for x in xs]

    def body(*refs):
        x_refs, mine_refs, theirs_refs = refs[:n], refs[n:2 * n], refs[2 * n:3 * n]
        send_sems, recv_sems, local_sems = refs[3 * n:]
        mx, my, mc = _my_place()
        copies = []
        for i, x_ref in enumerate(x_refs):
            h = x_ref.shape[1] // 2
            copies.append(pltpu.make_async_copy(_window(x_ref, 1, mc * h, h), mine_refs[i], local_sems.at[i]))
            copies.append(pltpu.make_async_remote_copy(
                src_ref=_window(x_ref, 1, (1 - mc) * h, h), dst_ref=theirs_refs[i], send_sem=send_sems.at[i],
                recv_sem=recv_sems.at[i], device_id=(mx, my, 1 - mc), device_id_type=MESH))
        _run(copies)

    out = _comm_call(body, name, xs, shapes + shapes, n, n)
    return out[:n], out[n:]


def chip_all_to_all(xs, name):
    n = len(xs)
    shapes = [jax.ShapeDtypeStruct(x.shape, x.dtype) for x in xs]

    def body(*refs):
        x_refs, o_refs = refs[:n], refs[n:2 * n]
        send_sems, recv_sems, local_sems = refs[2 * n:]
        mx, my, mc = _my_place()
        q = 2 * mx + my
        copies = []
        for i, (x_ref, o_ref) in enumerate(zip(x_refs, o_refs)):
            copies.append(pltpu.make_async_copy(x_ref.at[q], o_ref.at[q], local_sems.at[i]))
            for j, (tx, ty) in enumerate(_other_chips(mx, my)):
                copies.append(pltpu.make_async_remote_copy(
                    src_ref=x_ref.at[2 * tx + ty], dst_ref=o_ref.at[q], send_sem=send_sems.at[3 * i + j],
                    recv_sem=recv_sems.at[3 * i + j], device_id=(tx, ty, mc), device_id_type=MESH))
        _run(copies)

    return _comm_call(body, name, xs, shapes, 3 * n, n)


def pair_join(xs, name):
    n = len(xs)
    shapes = [jax.ShapeDtypeStruct((2 * x.shape[0], x.shape[1]), x.dtype) for x in xs]

    def body(*refs):
        x_refs, o_refs = refs[:n], refs[n:2 * n]
        send_sems, recv_sems, local_sems = refs[2 * n:]
        mx, my, mc = _my_place()
        copies = []
        for i, (x_ref, o_ref) in enumerate(zip(x_refs, o_refs)):
            h = x_ref.shape[0]
            dst = _window(o_ref, 0, mc * h, h)
            copies.append(pltpu.make_async_copy(x_ref, dst, local_sems.at[i]))
            copies.append(pltpu.make_async_remote_copy(
                src_ref=x_ref, dst_ref=dst, send_sem=send_sems.at[i], recv_sem=recv_sems.at[i],
                device_id=(mx, my, 1 - mc), device_id_type=MESH))
        _run(copies)

    return _comm_call(body, name, xs, shapes, n, n)


WEIGHTS = ['norm_mix', 'norm_xa', 'norm_mem', 'norm_ffn', 'xa_wq', 'xa_wk', 'xa_wv', 'xa_wo', 'xa_q_norm', 'xa_k_norm',
           'ffn_w_up', 'ffn_conv_w', 'ffn_conv_b', 'ffn_w_down', 'hg_lb_logits', 'mix_w_in', 'hg_out_norm',
           'mla_q_a_norm', 'mla_w_uq', 'mla_kv_a_norm', 'mla_w_ukv', 'mla_qn_nope', 'mla_qn_rope', 'mla_kn_nope',
           'mla_kn_rope', 'mix_w_out', 's5_lam_re', 's5_lam_im', 's5_log_dt', 's5_b_re', 's5_b_im', 's5_c_re',
           's5_c_im', 's5_d', 's5_w_glu_a', 's5_w_glu_b']
INPUTS = ['x', 'mem', 'positions'] + WEIGHTS + ['loss_target'] + ['m_' + n for n in WEIGHTS] + ['v_' + n for n in WEIGHTS]
SHARD_AXIS = {'xa_wq': 1, 'xa_wk': 1, 'xa_wv': 1, 'xa_wo': 1, 'ffn_w_up': 2, 'ffn_conv_w': 2, 'ffn_w_down': 1,
              'mix_w_in': 2, 'mla_w_uq': 2, 'mla_w_ukv': 2, 'mix_w_out': 1, 's5_d': 1, 's5_w_glu_a': 1, 's5_w_glu_b': 1}
BIG = ['xa_wq', 'xa_wk', 'xa_wv', 'xa_wo', 'ffn_w_up', 'ffn_w_down', 'mix_w_in', 'mix_w_out', 's5_w_glu_a', 's5_w_glu_b']
SMALL_SHARDED = [n for n in WEIGHTS if n in SHARD_AXIS and n not in BIG]
REPLICATED = [n for n in WEIGHTS if n not in SHARD_AXIS]
SMALL = SMALL_SHARDED + REPLICATED
PACK_W = 1024
ROW_MULT = 16
W_IN_SHARD = IN_WIDTH // N_CHIPS
W_IN_SHARD_PAD = 640


def _pack(flats, mult=ROW_MULT):
    flat = jnp.concatenate([f.reshape(-1) for f in flats])
    unit = mult * PACK_W
    n = -(-flat.shape[0] // unit) * unit
    return jnp.pad(flat, (0, n - flat.shape[0])).reshape(n // PACK_W, PACK_W)


def _unpack(packed, shapes):
    flat, out, o = packed.reshape(-1), [], 0
    for s in shapes:
        n = math.prod(s)
        out.append(flat[o:o + n].reshape(s))
        o += n
    return out


def _rope_pad(w):
    z = jnp.zeros(w.shape[:-1] + (MLA_ROPE // 2,), w.dtype)
    return jnp.concatenate([w[..., :MLA_ROPE // 2], z, w[..., MLA_ROPE // 2:], z], axis=-1)


def _rope_unpad(g):
    return jnp.concatenate([g[..., :MLA_ROPE // 2], g[..., 64:64 + MLA_ROPE // 2]], axis=-1)


def _blockdiag_in(bb):
    nb = bb.shape[0] // S5_GB
    t = bb.reshape(nb, S5_GB, S5_STATE, S5_GROUP).transpose(0, 1, 3, 2)
    return jnp.einsum('bgmp,gh->bgmhp', t, jnp.eye(S5_GB, dtype=bb.dtype)).reshape(nb, S5_GB * S5_GROUP, S5_LANES)


def _blockdiag_in_t(dw):
    nb = dw.shape[0]
    t = jnp.einsum('bgmhp,gh->bgmp', dw.reshape(nb, S5_GB, S5_GROUP, S5_GB, S5_STATE), jnp.eye(S5_GB, dtype=dw.dtype))
    return t.transpose(0, 1, 3, 2).reshape(nb * S5_GB, S5_STATE, S5_GROUP)


def _blockdiag_out(c):
    nb = c.shape[0] // S5_GB
    t = c.reshape(nb, S5_GB, S5_GROUP, S5_STATE).transpose(0, 1, 3, 2)
    return jnp.einsum('bgpm,gh->bgphm', t, jnp.eye(S5_GB, dtype=c.dtype)).reshape(nb, S5_LANES, S5_GB * S5_GROUP)


def _blockdiag_out_t(dc):
    nb = dc.shape[0]
    t = jnp.einsum('bgphm,gh->bgpm', dc.reshape(nb, S5_GB, S5_STATE, S5_GB, S5_GROUP), jnp.eye(S5_GB, dtype=dc.dtype))
    return t.transpose(0, 1, 3, 2).reshape(nb * S5_GB, S5_GROUP, S5_STATE)


def _gather_weights(P):
    xs = [P[n].astype(BF16) for n in BIG] + [_pack([P[n] for n in SMALL_SHARDED])]
    axes = [None if n == 'mix_w_in' else SHARD_AXIS[n] for n in BIG] + [None]
    got = chip_gather(xs, axes, "gather_weights")
    full_w = dict(zip(BIG, got[:-1]))
    full_w['mix_w_in'] = jnp.concatenate([full_w['mix_w_in'][q] for q in range(N_CHIPS)], axis=SHARD_AXIS['mix_w_in'])
    per_chip = [_unpack(got[-1][q], [P[n].shape for n in SMALL_SHARDED]) for q in range(N_CHIPS)]
    for i, n in enumerate(SMALL_SHARDED):
        full_w[n] = jnp.concatenate([per_chip[q][i] for q in range(N_CHIPS)], axis=SHARD_AXIS[n])
    return full_w


def _reduce_and_update(GB, GS, P):
    mx, my, _ = _my_place()
    q = 2 * mx + my
    small = _pack([GS[n] for n in SMALL], 2 * N_CHIPS * ROW_MULT)
    xs = [GB[n] for n in BIG] + [small.reshape(N_CHIPS, -1, PACK_W)]
    mine, theirs = pair_split(xs, "grads_pair_split")
    names = BIG + ['small']
    pair = [add2(a, b, "grads_pair_sum_" + n) for a, b, n in zip(mine, theirs, names)]
    got = chip_all_to_all(pair, "grads_chip_all_to_all")
    summed = [add_chips(g, "grads_chip_sum_" + n) for g, n in zip(got, names)]
    joined = pair_join(summed, "grads_pair_join")
    small_sum = chip_gather([joined[-1]], [0], "grads_small_gather")[0]
    g_small = dict(zip(SMALL, _unpack(small_sum, [GS[n].shape for n in SMALL])))
    for n in SMALL_SHARDED:
        s = P[n].shape[SHARD_AXIS[n]]
        g_small[n] = lax.dynamic_slice_in_dim(g_small[n], q * s, s, axis=SHARD_AXIS[n])

    grad, delta, new_m, new_v = {}, {}, {}, {}
    for n, g in zip(BIG, joined[:-1]):
        shape = P[n].shape
        if n == 'mix_w_in':
            g = g[:, :W_IN_SHARD]
        two_d = (g.shape[0], shape[-1])
        d, m_, v_ = adamw(P[n].reshape(two_d), g, P['m_' + n].reshape(two_d), P['v_' + n].reshape(two_d), "adamw_" + n)
        grad[n], delta[n], new_m[n], new_v[n] = (t.reshape(shape) for t in (g, d, m_, v_))
    packed = lambda prefix: _pack([P[prefix + n] for n in SMALL])
    d, m_, v_ = adamw(packed(''), _pack([g_small[n] for n in SMALL]), packed('m_'), packed('v_'), "adamw_small")
    shapes = [P[n].shape for n in SMALL]
    grad.update(g_small)
    for out, pk in ((delta, d), (new_m, m_), (new_v, v_)):
        out.update(zip(SMALL, _unpack(pk, shapes)))
    return grad, delta, new_m, new_v


def _row(v):
    return v.reshape(1, -1)


def _xattn_fwd(h, mem, W, lyr, tm):
    g_xa, g_mem = _row(W['norm_xa'][lyr]), _row(W['norm_mem'][lyr])
    g_q, g_k = _row(W['xa_q_norm'][lyr]), _row(W['xa_k_norm'][lyr])
    wq, wk, wv, wo = (W[n][lyr] for n in ('xa_wq', 'xa_wk', 'xa_wv', 'xa_wo'))
    L, D = h.shape
    M = mem.shape[0]
    hx = rms_fwd(h, g_xa, tm, MXU_DTYPE)
    qp = matmul(hx, wq, name="xa_q")
    kv_opds = [full(mem), full(g_mem), full(wk), full(wv), full(g_k)]
    k, v = blocked_fwd(_mem_kv, kv_opds, [((M, D), F32, (M, D), lambda i: (0, 0))] * 2, 1, "xa_mem_kv")
    o = blocked_fwd(_xa_core, [rows(qp, tm), full(k), full(v), full(g_q)], [_out((L, D), MXU_DTYPE, tm)], L // tm,
                    "xa_core")[0]
    out = matmul(o, wo, add=h, name="xa_o")
    return out, (h, hx, qp, k, v, o)


def _xattn_bwd(dout, saved, mem, W, lyr, tm):
    h, hx, qp, k, v, o = saved
    g_xa, g_mem = _row(W['norm_xa'][lyr]), _row(W['norm_mem'][lyr])
    g_q, g_k = _row(W['xa_q_norm'][lyr]), _row(W['xa_k_norm'][lyr])
    wq, wk, wv, wo = (W[n][lyr] for n in ('xa_wq', 'xa_wk', 'xa_wv', 'xa_wo'))
    L = h.shape[0]
    do = matmul(dout, wo, "nt", name="xa_do")
    d_wo = matmul(o, dout, "tn", name="xa_dwo")
    dqp, dk, dv, d_gq = blocked_bwd(_xa_core, [rows(qp, tm, 'blk'), full(k, 'acc'), full(v, 'acc'), full(g_q, 'acc')],
                                    [rows(do, tm)], L // tm, "xa_core_bwd")
    d_wq = matmul(hx, dqp, "tn", name="xa_dwq")
    dhx = matmul(dqp, wq, "nt", name="xa_dhx")
    dh, d_gxa = rms_bwd(h, g_xa, dhx, tm)
    d_gmem, d_wk, d_wv, d_gk = blocked_bwd(
        _mem_kv, [full(mem), full(g_mem, 'acc'), full(wk, 'acc'), full(wv, 'acc'), full(g_k, 'acc')],
        [full(dk), full(dv)], 1, "xa_mem_kv_bwd")
    by_chip = lambda g: g.reshape(N_CHIPS, g.shape[0] // N_CHIPS, g.shape[1])
    grads = {'norm_xa': d_gxa, 'norm_mem': d_gmem, 'xa_q_norm': d_gq, 'xa_k_norm': d_gk,
             'xa_wq': by_chip(d_wq), 'xa_wk': by_chip(d_wk), 'xa_wv': by_chip(d_wv), 'xa_wo': by_chip(d_wo)}
    return dout + dh, grads


def _conv_params(W, lyr):
    cw, cb = W['ffn_conv_w'][lyr], W['ffn_conv_b'][lyr]
    F = cw.shape[1] // 2
    return [cw[0:1, :F], cw[1:2, :F], cw[2:3, :F], cw[0:1, F:], cw[1:2, F:], cw[2:3, F:], _row(cb[:F]), _row(cb[F:])]


def _ffn_fwd(h, W, lyr, tm):
    L, D = h.shape
    w_up, w_down = W['ffn_w_up'][lyr], W['ffn_w_down'][lyr]
    F = w_down.shape[0]
    hf = rms_fwd(h, _row(W['norm_ffn'][lyr]), tm, MXU_DTYPE)
    ug = matmul(hf, w_up[:, :F], name="ffn_up_gate")
    uv = matmul(hf, w_up[:, F:], name="ffn_up_value")
    opds = [cols(ug, 128), cols(uv, 128)] + [cols(p, 128) for p in _conv_params(W, lyr)]
    a = blocked_fwd(_conv_gate, opds, [((L, F), MXU_DTYPE, (L, 128), lambda j: (0, j))], F // 128, "ffn_conv_gate")[0]
    out = matmul(a, w_down, add=h, name="ffn_down")
    return out, (h, hf, ug, uv, a)


def _ffn_bwd(dout, saved, W, lyr, tm):
    h, hf, ug, uv, a = saved
    w_up, w_down = W['ffn_w_up'][lyr], W['ffn_w_down'][lyr]
    F = w_down.shape[0]
    da = matmul(dout, w_down, "nt", name="ffn_da")
    d_wdown = matmul(a, dout, "tn", name="ffn_dwdown")
    opds = [cols(ug, 128, 'blk'), cols(uv, 128, 'blk')] + [cols(p, 128, 'blk') for p in _conv_params(W, lyr)]
    gs = blocked_bwd(_conv_gate, opds, [cols(da, 128)], F // 128, "ffn_conv_gate_bwd")
    dug, duv = gs[0], gs[1]
    d_cw = jnp.concatenate([jnp.concatenate(gs[2:5], axis=0), jnp.concatenate(gs[5:8], axis=0)], axis=1)
    d_cb = jnp.concatenate([gs[8], gs[9]], axis=1)[0]
    d_wup = jnp.concatenate([matmul(hf, dug, "tn", name="ffn_dwup_gate", col_blocks=N_CHIPS // 2),
                             matmul(hf, duv, "tn", name="ffn_dwup_value", col_blocks=N_CHIPS // 2)], axis=0)
    dhf = matmul(dug, w_up[:, :F], "nt", name="ffn_dhf_gate")
    dhf = matmul(duv, w_up[:, F:], "nt", add=dhf, name="ffn_dhf_value")
    dh, d_g = rms_bwd(h, _row(W['norm_ffn'][lyr]), dhf, tm)
    d_wdown = d_wdown.reshape(N_CHIPS, F // N_CHIPS, d_wdown.shape[1])
    return dout + dh, {'norm_ffn': d_g, 'ffn_w_up': d_wup, 'ffn_conv_w': d_cw, 'ffn_conv_b': d_cb, 'ffn_w_down': d_wdown}


def _mla_params(W):
    w_uq = W['mla_w_uq'][0].reshape(MLA_Q_RANK, MLA_HEADS, MLA_QK)
    w_uq = jnp.concatenate([w_uq[..., :MLA_NOPE], _rope_pad(w_uq[..., MLA_NOPE:])], axis=-1)
    w_ukv = W['mla_w_ukv'][0].reshape(MLA_KV_RANK, MLA_HEADS, MLA_NOPE + MLA_V)
    w_ukv = jnp.concatenate([w_ukv[..., :MLA_NOPE].reshape(MLA_KV_RANK, -1), w_ukv[..., MLA_NOPE:].reshape(MLA_KV_RANK, -1)],
                            axis=1)
    return [_row(W['mla_q_a_norm'][0]), w_uq.reshape(MLA_Q_RANK, MLA_HEADS * MLA_DK), _row(W['mla_kv_a_norm'][0]), w_ukv,
            _row(W['mla_qn_nope'][0]), _row(_rope_pad(W['mla_qn_rope'][0])), _row(W['mla_kn_nope'][0]),
            _row(_rope_pad(W['mla_kn_rope'][0]))]


def _w_in_padded(W):
    w = W['mix_w_in'][0]
    return jnp.concatenate([w[:, :IN_WIDTH - MLA_ROPE], _rope_pad(w[:, IN_WIDTH - MLA_ROPE:])], axis=1)


def _mixer0_fwd(h, W, cos_p, sin_p, tm):
    L = h.shape[0]
    t = min(256, L)
    hn = rms_fwd(h, _row(W['norm_mix'][0]), tm, MXU_DTYPE)
    proj = matmul(hn, _w_in_padded(W), name="mix_in")
    logits = W['hg_lb_logits']
    lb = blocked_fwd(_lb_first, [full(logits)], [((1, HG_WIDTH), F32, (1, HG_WIDTH), lambda i: (0, 0))], 1, "hg_lb")[0]
    gain = _row(W['hg_out_norm'][0])
    o_hg, states = hgrn2_fwd(proj, lb, gain)
    mp = _mla_params(W)
    q, k, v = mla_prep_fwd(proj, cos_p, sin_p, mp, tm)
    scale = MLA_QK ** -0.5
    o_mla, lse = attn_fwd(q, k, v, scale, t)
    w_out = W['mix_w_out'][0]
    out = matmul(o_hg, w_out[:HG_WIDTH], add=h, name="mix_out_hg")
    out = matmul(o_mla, w_out[HG_WIDTH:], add=out, name="mix_out_mla")
    return out, (h, hn, proj, lb, o_hg, states, q, k, v, o_mla, lse)


def _mixer0_bwd(dout, saved, W, cos_p, sin_p, tm):
    h, hn, proj, lb, o_hg, states, q, k, v, o_mla, lse = saved
    L = h.shape[0]
    t = min(256, L)
    scale = MLA_QK ** -0.5
    w_out = W['mix_w_out'][0]
    gain = _row(W['hg_out_norm'][0])
    do_hg = matmul(dout, w_out[:HG_WIDTH], "nt", name="mix_do_hg")
    do_mla = matmul(dout, w_out[HG_WIDTH:], "nt", name="mix_do_mla")
    d_wout = jnp.concatenate([matmul(o_hg, dout, "tn", name="mix_dwout_hg"), matmul(o_mla, dout, "tn", name="mix_dwout_mla")],
                             axis=0)
    dq = attn_bwd_dq(q, k, v, o_mla, lse, do_mla, scale, t)
    dk, dv = attn_bwd_dkv(q, k, v, o_mla, lse, do_mla, scale, t)
    mp = _mla_params(W)
    dcq, dckv, dkpe, d_qa, d_wuq, d_kva, d_wukv, d_qnn, d_qnr, d_knn, d_knr = mla_prep_bwd(proj, cos_p, sin_p, mp, dq, dk, dv, tm)
    d_hg, d_lb, d_gain = hgrn2_bwd(proj, lb, gain, states, do_hg)
    dproj = jnp.concatenate([d_hg, dcq, dckv, dkpe], axis=1)
    d_win = matmul(hn, dproj, "tn", name="mix_dwin")
    dhn = matmul(dproj, _w_in_padded(W), "nt", name="mix_dhn")
    dh, d_g = rms_bwd(h, _row(W['norm_mix'][0]), dhn, tm)
    logits = W['hg_lb_logits']
    d_logits = blocked_bwd(_lb_first, [full(logits, 'acc')], [full(d_lb)], 1, "hg_lb_bwd")[0]
    d_wuq = d_wuq.reshape(MLA_Q_RANK, MLA_HEADS, MLA_DK)
    d_wuq = jnp.concatenate([d_wuq[..., :MLA_NOPE], _rope_unpad(d_wuq[..., MLA_NOPE:])], axis=-1)
    hw = MLA_HEADS * MLA_NOPE
    d_wukv = jnp.concatenate([d_wukv[:, :hw].reshape(MLA_KV_RANK, MLA_HEADS, MLA_NOPE),
                              d_wukv[:, hw:].reshape(MLA_KV_RANK, MLA_HEADS, MLA_V)], axis=-1)
    d_win = jnp.concatenate([d_win[:, :IN_WIDTH - MLA_ROPE], _rope_unpad(d_win[:, IN_WIDTH - MLA_ROPE:])], axis=1)
    d_win = d_win.reshape(d_win.shape[0], N_CHIPS, W_IN_SHARD).transpose(1, 0, 2)
    d_win = jnp.pad(d_win, ((0, 0), (0, 0), (0, W_IN_SHARD_PAD - W_IN_SHARD)))
    d_wout = d_wout.reshape(N_CHIPS, d_wout.shape[0] // N_CHIPS, d_wout.shape[1])
    grads = {'norm_mix': d_g, 'hg_lb_logits': d_logits, 'mix_w_in': d_win, 'hg_out_norm': d_gain,
             'mla_q_a_norm': d_qa, 'mla_w_uq': d_wuq.reshape(1, MLA_Q_RANK, -1), 'mla_kv_a_norm': d_kva,
             'mla_w_ukv': d_wukv.reshape(1, MLA_KV_RANK, -1), 'mla_qn_nope': d_qnn, 'mla_qn_rope': _rope_unpad(d_qnr),
             'mla_kn_nope': d_knn, 'mla_kn_rope': _rope_unpad(d_knr), 'mix_w_out': d_wout}
    return dout + dh, grads


def _s5_inputs(W):
    G = W['s5_lam_re'].shape[1]
    return [W['s5_lam_re'][0], W['s5_lam_im'][0], W['s5_log_dt'][0].reshape(G, 1),
            W['s5_b_re'][0].reshape(G, -1), W['s5_b_im'][0].reshape(G, -1)]


def _mixer1_fwd(h, W, tm):
    L, D = h.shape
    u = rms_fwd(h, _row(W['norm_mix'][1]), tm, F32)
    di = _s5_inputs(W)
    G = di[0].shape[0]
    sq, wide = ((G, S5_STATE), F32, (G, S5_STATE), lambda i: (0, 0)), ((G, S5_STATE * S5_GROUP), F32, (G, S5_STATE * S5_GROUP), lambda i: (0, 0))
    ar, ai, bbr, bbi = blocked_fwd(_s5_discretize, [full(a) for a in di], [sq, sq, wide, wide], 1, "s5_discretize")
    nb = G // S5_GB
    core = (_blockdiag_in(bbr.reshape(G, S5_STATE, S5_GROUP)), _blockdiag_in(bbi.reshape(G, S5_STATE, S5_GROUP)),
            ar.reshape(nb, 1, S5_LANES), ai.reshape(nb, 1, S5_LANES),
            _blockdiag_out(W['s5_c_re'][0]), _blockdiag_out(W['s5_c_im'][0]))
    y = s5_fwd(u, *core)
    d = W['s5_d']
    y2 = blocked_fwd(_s5_post, [rows(y, tm), rows(u, tm), full(d)], [_out((L, D), MXU_DTYPE, tm)], L // tm, "s5_post")[0]
    w_ab = jnp.concatenate([W['s5_w_glu_a'][0], W['s5_w_glu_b'][0]], axis=1)
    ab = matmul(y2, w_ab, name="s5_glu_in")
    mix = blocked_fwd(_glu, [rows(ab, tm, col=0, width=D), rows(ab, tm, col=1, width=D)], [_out((L, D), F32, tm)], L // tm,
                      "s5_glu")[0]
    return h + mix, (h, u, core, y, y2, ab)


def _mixer1_bwd(dout, saved, W, tm):
    h, u, core, y, y2, ab = saved
    L, D = h.shape
    da, db = blocked_bwd(_glu, [rows(ab, tm, 'blk', col=0, width=D), rows(ab, tm, 'blk', col=1, width=D)], [rows(dout, tm)],
                         L // tm, "s5_glu_bwd")
    w_a, w_b = W['s5_w_glu_a'][0], W['s5_w_glu_b'][0]
    dy2 = matmul(da, w_a, "nt", name="s5_dy2_a")
    dy2 = matmul(db, w_b, "nt", add=dy2, name="s5_dy2_b")
    d_wa = matmul(y2, da, "tn", name="s5_dwa")
    d_wb = matmul(y2, db, "tn", name="s5_dwb")
    d = W['s5_d']
    dy, du_skip, d_d = blocked_bwd(_s5_post, [rows(y, tm, 'blk'), rows(u, tm, 'blk'), full(d, 'acc')], [rows(dy2, tm)], L // tm,
                                   "s5_post_bwd")
    du, dwr, dwi, dar, dai, dcr, dci = s5_bwd(u, *core, dy, min(256, L))
    di = _s5_inputs(W)
    G = di[0].shape[0]
    cts = [dar.reshape(G, S5_STATE), dai.reshape(G, S5_STATE), _blockdiag_in_t(dwr).reshape(G, -1), _blockdiag_in_t(dwi).reshape(G, -1)]
    d_lr, d_li, d_ldt, d_br, d_bi = blocked_bwd(_s5_discretize, [full(a, 'acc') for a in di], [full(c) for c in cts], 1,
                                                "s5_discretize_bwd")
    dh, d_g = rms_bwd(h, _row(W['norm_mix'][1]), du + du_skip, tm)
    bshape = W['s5_b_re'].shape
    grads = {'norm_mix': d_g, 's5_lam_re': d_lr[None], 's5_lam_im': d_li[None], 's5_log_dt': d_ldt.reshape(1, G),
             's5_b_re': d_br.reshape(bshape), 's5_b_im': d_bi.reshape(bshape), 's5_c_re': _blockdiag_out_t(dcr)[None],
             's5_c_im': _blockdiag_out_t(dci)[None], 's5_d': d_d, 's5_w_glu_a': d_wa.reshape(N_CHIPS, -1, D), 's5_w_glu_b': d_wb.reshape(N_CHIPS, -1, D)}
    return dout + dh, grads


def kernel(x, mem, positions, norm_mix, norm_xa, norm_mem, norm_ffn, xa_wq, xa_wk, xa_wv, xa_wo, xa_q_norm, xa_k_norm, ffn_w_up, ffn_conv_w, ffn_conv_b, ffn_w_down, hg_lb_logits, mix_w_in, hg_out_norm, mla_q_a_norm, mla_w_uq, mla_kv_a_norm, mla_w_ukv, mla_qn_nope, mla_qn_rope, mla_kn_nope, mla_kn_rope, mix_w_out, s5_lam_re, s5_lam_im, s5_log_dt, s5_b_re, s5_b_im, s5_c_re, s5_c_im, s5_d, s5_w_glu_a, s5_w_glu_b, loss_target, m_norm_mix, m_norm_xa, m_norm_mem, m_norm_ffn, m_xa_wq, m_xa_wk, m_xa_wv, m_xa_wo, m_xa_q_norm, m_xa_k_norm, m_ffn_w_up, m_ffn_conv_w, m_ffn_conv_b, m_ffn_w_down, m_hg_lb_logits, m_mix_w_in, m_hg_out_norm, m_mla_q_a_norm, m_mla_w_uq, m_mla_kv_a_norm, m_mla_w_ukv, m_mla_qn_nope, m_mla_qn_rope, m_mla_kn_nope, m_mla_kn_rope, m_mix_w_out, m_s5_lam_re, m_s5_lam_im, m_s5_log_dt, m_s5_b_re, m_s5_b_im, m_s5_c_re, m_s5_c_im, m_s5_d, m_s5_w_glu_a, m_s5_w_glu_b, v_norm_mix, v_norm_xa, v_norm_mem, v_norm_ffn, v_xa_wq, v_xa_wk, v_xa_wv, v_xa_wo, v_xa_q_norm, v_xa_k_norm, v_ffn_w_up, v_ffn_conv_w, v_ffn_conv_b, v_ffn_w_down, v_hg_lb_logits, v_mix_w_in, v_hg_out_norm, v_mla_q_a_norm, v_mla_w_uq, v_mla_kv_a_norm, v_mla_w_ukv, v_mla_qn_nope, v_mla_qn_rope, v_mla_kn_nope, v_mla_kn_rope, v_mix_w_out, v_s5_lam_re, v_s5_lam_im, v_s5_log_dt, v_s5_b_re, v_s5_b_im, v_s5_c_re, v_s5_c_im, v_s5_d, v_s5_w_glu_a, v_s5_w_glu_b):
    P = dict(locals())
    assert sorted(P) == sorted(INPUTS) and norm_mix.shape[0] == 2 and mix_w_in.shape[0] == 1
    x, mem, target = P['x'][0], P['mem'][0], P['loss_target'][0]
    L, D = x.shape
    tm = min(256, L)

    W = {n: P[n] for n in REPLICATED}
    W.update(_gather_weights(P))

    inv_freq = 1.0 / (ROPE_BASE ** (jnp.arange(0, MLA_ROPE, 2, dtype=F32) / MLA_ROPE))
    ang = P['positions'][0].astype(F32)[:, None] * inv_freq
    cos, sin, z = jnp.cos(ang), jnp.sin(ang), jnp.zeros_like(ang)
    cos_p = jnp.concatenate([cos, z, cos, z], axis=1)
    sin_p = jnp.concatenate([-sin, z, sin, z], axis=1)

    h, s_mix0 = _mixer0_fwd(x, W, cos_p, sin_p, tm)
    h, s_xa0 = _xattn_fwd(h, mem, W, 0, tm)
    h, s_ffn0 = _ffn_fwd(h, W, 0, tm)
    h, s_mix1 = _mixer1_fwd(h, W, tm)
    h, s_xa1 = _xattn_fwd(h, mem, W, 1, tm)
    h, s_ffn1 = _ffn_fwd(h, W, 1, tm)
    n = L // tm
    dh, parts = blocked_fwd(_loss_fn, [rows(h, tm), rows(target, tm)],
                            [_out((L, D), F32, tm), ((n * 8, 128), F32, (8, 128), lambda i: (i, 0))], n, "loss")
    loss = lax.psum(jnp.sum(parts), ("x", "y", "c"))

    layered = {}

    def collect(g, lyr):
        for k_, v_ in g.items():
            layered.setdefault(k_, {})[lyr] = v_

    dh, g = _ffn_bwd(dh, s_ffn1, W, 1, tm)
    collect(g, 1)
    dh, g = _xattn_bwd(dh, s_xa1, mem, W, 1, tm)
    collect(g, 1)
    dh, g = _mixer1_bwd(dh, s_mix1, W, tm)
    collect(g, 1)
    dh, g = _ffn_bwd(dh, s_ffn0, W, 0, tm)
    collect(g, 0)
    dh, g = _xattn_bwd(dh, s_xa0, mem, W, 0, tm)
    collect(g, 0)
    dx, g = _mixer0_bwd(dh, s_mix0, W, cos_p, sin_p, tm)
    collect(g, 0)

    GB, GS = {}, {}
    for name in WEIGHTS:
        by_layer = [layered[name][lyr] for lyr in sorted(layered[name])]
        if name in BIG:
            GB[name] = by_layer[0] if len(by_layer) == 1 else jnp.concatenate(by_layer, axis=1)
        else:
            full_shape = W[name].shape
            GS[name] = (by_layer[0].reshape(full_shape) if len(by_layer) == 1
                        else jnp.stack([g_.reshape(full_shape[1:]) for g_ in by_layer]))

    outs = _reduce_and_update(GB, GS, P)
    return (loss, dx[None], *[d[n_] for d in outs for n_ in WEIGHTS])
```

```python
import functools
import math

import jax
import jax.numpy as jnp
import numpy as np
from jax import lax
from jax.experimental import pallas as pl
from jax.experimental.pallas import tpu as pltpu

F32 = jnp.float32
BF16 = jnp.bfloat16
MXU_DTYPE = BF16
HI = lax.Precision.HIGHEST
V7X_VMEM_LIMIT_BYTES = 56 * 1024 * 1024
EPS = 1e-6
MESH = pl.DeviceIdType.MESH

HG_HEADS, HG_DIM = 4, 128
HG_WIDTH = HG_HEADS * HG_DIM
HG_SUB = 16
HG_BLOCK = 64
MLA_HEADS, MLA_Q_RANK, MLA_KV_RANK = 4, 256, 128
MLA_NOPE, MLA_ROPE, MLA_V = 128, 64, 128
MLA_QK = MLA_NOPE + MLA_ROPE
MLA_DK = 256
ROPE_BASE = 10000.0
IN_WIDTH = 4 * HG_WIDTH + MLA_Q_RANK + MLA_KV_RANK + MLA_ROPE
IN_PAD = 4 * HG_WIDTH + MLA_Q_RANK + MLA_KV_RANK + 128
S5_GROUP, S5_STATE = 16, 64
S5_GB = 8
DT_MIN, DT_MAX = 1e-3, 1e-1
XA_HEADS = 4
CONV_W = 3
ADAM_LR, ADAM_B1, ADAM_B2, ADAM_EPS, ADAM_WD, ADAM_STEP = 0.001, 0.9, 0.999, 1e-08, 0.01, 10


def _cparams(sem):
    return pltpu.CompilerParams(dimension_semantics=sem, vmem_limit_bytes=V7X_VMEM_LIMIT_BYTES)


class Opd:
    def __init__(self, arr, block, imap, grad=None, gshape=None, gimap=None):
        self.arr, self.block, self.imap, self.grad = arr, block, imap, grad
        self.gshape = arr.shape if gshape is None else gshape
        self.gimap = imap if gimap is None else gimap

    def spec(self):
        return pl.BlockSpec(self.block, self.imap)

    def gspec(self):
        return pl.BlockSpec(self.block, self.gimap)


def rows(arr, tm, grad=None, col=0, width=None):
    width = arr.shape[1] if width is None else width
    return Opd(arr, (tm, width), lambda i, c=col: (i, c), grad, (arr.shape[0], width), lambda i: (i, 0))


def cols(arr, tn, grad=None):
    return Opd(arr, (arr.shape[0], tn), lambda j: (0, j), grad)


def full(arr, grad=None):
    return Opd(arr, arr.shape, lambda i: (0, 0), grad)


def _load(ref):
    v = ref[...]
    return v.astype(F32) if jnp.issubdtype(v.dtype, jnp.floating) else v


def blocked_fwd(f, opds, outs, n, name):
    n_in = len(opds)

    def body(*refs):
        ys = f(*[_load(r) for r in refs[:n_in]])
        for r, y in zip(refs[n_in:], ys):
            r[...] = y.astype(r.dtype)

    res = pl.pallas_call(
        body, name=name, grid=(n,),
        in_specs=[o.spec() for o in opds],
        out_specs=[pl.BlockSpec(b, m) for (_, _, b, m) in outs],
        out_shape=[jax.ShapeDtypeStruct(s, d) for (s, d, _, _) in outs],
        compiler_params=_cparams(("parallel",)),
    )(*[o.arr for o in opds])
    return res


def blocked_bwd(f, opds, dys, n, name):
    n_in, n_dy = len(opds), len(dys)
    diff = [i for i, o in enumerate(opds) if o.grad]

    def body(*refs):
        vals = [_load(r) for r in refs[:n_in]]

        def fd(*dv):
            allv = list(vals)
            for i, v in zip(diff, dv):
                allv[i] = v
            return tuple(f(*allv))

        ys, vjp = jax.vjp(fd, *[vals[i] for i in diff])
        cts = tuple(_load(r).astype(y.dtype) for r, y in zip(refs[n_in:n_in + n_dy], ys))
        gs = vjp(cts)
        for r, g, i in zip(refs[n_in + n_dy:], gs, diff):
            if opds[i].grad == 'acc':
                @pl.when(pl.program_id(0) == 0)
                def _(r=r):
                    r[...] = jnp.zeros(r.shape, r.dtype)
                r[...] += g.astype(r.dtype)
            else:
                r[...] = g.astype(r.dtype)

    any_acc = any(opds[i].grad == 'acc' for i in diff)
    res = pl.pallas_call(
        body, name=name, grid=(n,),
        in_specs=[o.spec() for o in opds] + [o.spec() for o in dys],
        out_specs=[opds[i].gspec() for i in diff],
        out_shape=[jax.ShapeDtypeStruct(opds[i].gshape, F32) for i in diff],
        compiler_params=_cparams(("arbitrary" if any_acc else "parallel",)),
    )(*[o.arr for o in opds], *[o.arr for o in dys])
    return res


def _tile(dim, want):
    for t in range(want - want % 16, 0, -16):
        if dim % t == 0:
            return t
    assert dim <= want, (dim, want)
    return dim


MATMUL_VMEM_BUDGET = 40 * 1024 * 1024
MATMUL_ROWS = 512


def _widest(N, fits):
    for t in range(N - N % 128, 0, -128):
        if N % t == 0 and fits(t):
            return t
    return N


def matmul(a, b, mode="nn", out_dtype=F32, add=None, name="matmul", col_blocks=None):
    sa, sb, so = a.dtype.itemsize, b.dtype.itemsize, jnp.dtype(out_dtype).itemsize
    has_add = add is not None
    if mode == "tn":
        (K, M), (K2, N) = a.shape, b.shape
        assert K == K2 and not has_add and out_dtype == F32, (a.shape, b.shape)
        tk = _tile(K, MATMUL_ROWS)
        tn = _widest(N, lambda t: 2 * (tk * M * sa + tk * t * sb + M * t * 4) <= MATMUL_VMEM_BUDGET)
        if col_blocks is not None:
            assert N % col_blocks == 0 and (N // col_blocks) % 128 == 0 and tn >= N // col_blocks, (N, col_blocks, tn)
            tn = N // col_blocks
            out_spec = pl.BlockSpec((None, M, tn), lambda j, k: (j, 0, 0))
            out_shape = jax.ShapeDtypeStruct((col_blocks, M, tn), F32)
        else:
            out_spec = pl.BlockSpec((M, tn), lambda j, k: (0, j))
            out_shape = jax.ShapeDtypeStruct((M, N), F32)

        def body(a_ref, b_ref, o_ref):
            r = lax.dot_general(a_ref[...].astype(MXU_DTYPE), b_ref[...].astype(MXU_DTYPE), ((_TN), ((), ())),
                                preferred_element_type=F32)

            @pl.when(pl.program_id(1) == 0)
            def _():
                o_ref[...] = r

            @pl.when(pl.program_id(1) > 0)
            def _():
                o_ref[...] += r

        return pl.pallas_call(
            body, name=name, grid=(N // tn, K // tk),
            in_specs=[pl.BlockSpec((tk, M), lambda j, k: (k, 0)), pl.BlockSpec((tk, tn), lambda j, k: (k, j))],
            out_specs=out_spec, out_shape=out_shape,
            compiler_params=_cparams(("parallel", "arbitrary")),
        )(a, b)

    (M, K) = a.shape
    N = b.shape[1] if mode == "nn" else b.shape[0]
    assert K == (b.shape[0] if mode == "nn" else b.shape[1]), (a.shape, b.shape, mode)
    tm = _tile(M, MATMUL_ROWS)
    tn = _widest(N, lambda t: 2 * (tm * K * sa + K * t * sb + tm * t * (so + 4 * has_add)) <= MATMUL_VMEM_BUDGET)
    dims = ((_NN if mode == "nn" else _NT), ((), ()))

    def body(*refs):
        r = lax.dot_general(refs[0][...].astype(MXU_DTYPE), refs[1][...].astype(MXU_DTYPE), dims, preferred_element_type=F32)
        if has_add:
            r = r + refs[2][...].astype(F32)
        refs[-1][...] = r.astype(refs[-1].dtype)

    b_spec = pl.BlockSpec((K, tn), lambda j, i: (0, j)) if mode == "nn" else pl.BlockSpec((tn, K), lambda j, i: (j, 0))
    in_specs = [pl.BlockSpec((tm, K), lambda j, i: (i, 0)), b_spec]
    args = [a, b]
    if has_add:
        in_specs.append(pl.BlockSpec((tm, tn), lambda j, i: (i, j)))
        args.append(add)
    return pl.pallas_call(
        body, name=name, grid=(N // tn, M // tm),
        in_specs=in_specs,
        out_specs=pl.BlockSpec((tm, tn), lambda j, i: (i, j)),
        out_shape=jax.ShapeDtypeStruct((M, N), out_dtype),
        compiler_params=_cparams(("parallel", "parallel")),
    )(*args)


def _dot(a, b, dims, precision=None):
    if precision is None:
        a, b = a.astype(MXU_DTYPE), b.astype(MXU_DTYPE)
    return lax.dot_general(a, b, (dims, ((), ())), precision=precision, preferred_element_type=F32)


_NN = ((1,), (0,))
_NT = ((1,), (1,))
_TN = ((0,), (0,))


def _rms(x, gain):
    return x * lax.rsqrt(jnp.mean(x * x, axis=-1, keepdims=True) + EPS) * gain


def _hg_block(st_t, q, fl, iv, g, lb, gain):
    row = lax.broadcasted_iota(jnp.int32, (HG_SUB, HG_SUB), 0)
    col = lax.broadcasted_iota(jnp.int32, (HG_SUB, HG_SUB), 1)
    tri = (row >= col).astype(F32)
    outs, states = [], []
    for h in range(HG_HEADS):
        sl = slice(h * HG_DIM, (h + 1) * HG_DIM)
        st = st_t[h * HG_DIM:(h + 1) * HG_DIM, :]
        lbh = lb[:, sl]
        fg = lbh + (1.0 - lbh) * jax.nn.sigmoid(fl[:, sl])
        lf, kk, qf, v = jnp.log(fg), 1.0 - fg, jax.nn.silu(q[:, sl]), iv[:, sl]
        parts = []
        for s in range(q.shape[0] // HG_SUB):
            r = slice(s * HG_SUB, (s + 1) * HG_SUB)
            b = _dot(tri, lf[r], _NN, HI)
            b_end = jnp.sum(lf[r], axis=0, keepdims=True)
            qe = qf[r] * jnp.exp(b)
            sc = _dot(qe, kk[r] * jnp.exp(-b), _NT) * tri
            parts.append(_dot(sc, v[r], _NN) + _dot(qe, st, _NT))
            st = st * jnp.exp(b_end) + _dot(v[r], kk[r] * jnp.exp(b_end - b), _TN)
        o = jnp.concatenate(parts, axis=0)
        outs.append(_rms(o, gain[:, sl]) * jax.nn.silu(g[:, sl]))
        states.append(st)
    return jnp.concatenate(states, axis=0), jnp.concatenate(outs, axis=1)


def _hg_specs(proj, nb):
    return [pl.BlockSpec((HG_BLOCK, HG_WIDTH), lambda i, c=c, f=nb: (f(i), c)) for c in range(4)]


def hgrn2_fwd(proj, lb, gain):
    L = proj.shape[0]
    n = L // HG_BLOCK

    def body(q, fl, iv, g, lb_r, gain_r, o_ref, st_ref, st):
        @pl.when(pl.program_id(0) == 0)
        def _():
            st[...] = jnp.zeros(st.shape, F32)

        st_ref[0] = st[...]
        new, o = _hg_block(st[...], q[...], fl[...], iv[...], g[...], lb_r[...], gain_r[...])
        st[...] = new
        o_ref[...] = o.astype(o_ref.dtype)

    pspec = pl.BlockSpec((1, HG_WIDTH), lambda i: (0, 0))
    return pl.pallas_call(
        body, name="hgrn2_fwd", grid=(n,),
        in_specs=_hg_specs(proj, lambda i: i) + [pspec, pspec],
        out_specs=[pl.BlockSpec((HG_BLOCK, HG_WIDTH), lambda i: (i, 0)),
                   pl.BlockSpec((1, HG_WIDTH, HG_DIM), lambda i: (i, 0, 0))],
        out_shape=[jax.ShapeDtypeStruct((L, HG_WIDTH), MXU_DTYPE),
                   jax.ShapeDtypeStruct((n, HG_WIDTH, HG_DIM), F32)],
        scratch_shapes=[pltpu.VMEM((HG_WIDTH, HG_DIM), F32)],
        compiler_params=_cparams(("arbitrary",)),
    )(proj, proj, proj, proj, lb, gain)


def hgrn2_bwd(proj, lb, gain, states, do):
    L = proj.shape[0]
    n = L // HG_BLOCK

    def body(q, fl, iv, g, lb_r, gain_r, st_r, do_r, dproj, dlb, dgain, dst):
        @pl.when(pl.program_id(0) == 0)
        def _():
            dst[...] = jnp.zeros(dst.shape, F32)
            dlb[...] = jnp.zeros(dlb.shape, F32)
            dgain[...] = jnp.zeros(dgain.shape, F32)

        _, vjp = jax.vjp(_hg_block, st_r[0], q[...], fl[...], iv[...], g[...], lb_r[...], gain_r[...])
        d_st, dq, dfl, div, dg, d_lb, d_gain = vjp((dst[...], do_r[...].astype(F32)))
        dst[...] = d_st
        dproj[:, 0 * HG_WIDTH:1 * HG_WIDTH] = dq
        dproj[:, 1 * HG_WIDTH:2 * HG_WIDTH] = dfl
        dproj[:, 2 * HG_WIDTH:3 * HG_WIDTH] = div
        dproj[:, 3 * HG_WIDTH:4 * HG_WIDTH] = dg
        dlb[...] += d_lb
        dgain[...] += d_gain

    rev = lambda i: n - 1 - i
    pspec = pl.BlockSpec((1, HG_WIDTH), lambda i: (0, 0))
    return pl.pallas_call(
        body, name="hgrn2_bwd", grid=(n,),
        in_specs=_hg_specs(proj, rev) + [pspec, pspec,
                                         pl.BlockSpec((1, HG_WIDTH, HG_DIM), lambda i: (rev(i), 0, 0)),
                                         pl.BlockSpec((HG_BLOCK, HG_WIDTH), lambda i: (rev(i), 0))],
        out_specs=[pl.BlockSpec((HG_BLOCK, 4 * HG_WIDTH), lambda i: (rev(i), 0)), pspec, pspec],
        out_shape=[jax.ShapeDtypeStruct((L, 4 * HG_WIDTH), F32),
                   jax.ShapeDtypeStruct((1, HG_WIDTH), F32), jax.ShapeDtypeStruct((1, HG_WIDTH), F32)],
        scratch_shapes=[pltpu.VMEM((HG_WIDTH, HG_DIM), F32)],
        compiler_params=_cparams(("arbitrary",)),
    )(proj, proj, proj, proj, lb, gain, states, do)


def _rope_rms(x, gain_p, cos_p, sin_p):
    n = x * lax.rsqrt(jnp.sum(x * x, axis=-1, keepdims=True) * (1.0 / MLA_ROPE) + EPS) * gain_p
    r = lax.broadcasted_iota(jnp.int32, (128, 128), 0)
    c = lax.broadcasted_iota(jnp.int32, (128, 128), 1)
    swap = (r == (c + 64) % 128).astype(F32)
    return n * cos_p + _dot(n, swap, _NN, HI) * sin_p


def _mla_prep(c_q, c_kv, kpe, cos_p, sin_p, q_a, w_uq, kv_a, w_ukv, qn_nope, qn_rope, kn_nope, kn_rope):
    q = _dot(_rms(c_q, q_a), w_uq, _NN)
    kv = _dot(_rms(c_kv, kv_a), w_ukv, _NN)
    k_pe = _rope_rms(kpe, kn_rope, cos_p, sin_p)
    qs, ks = [], []
    for h in range(MLA_HEADS):
        qs.append(_rms(q[:, h * MLA_DK:h * MLA_DK + MLA_NOPE], qn_nope))
        qs.append(_rope_rms(q[:, h * MLA_DK + MLA_NOPE:(h + 1) * MLA_DK], qn_rope, cos_p, sin_p))
        ks.append(_rms(kv[:, h * MLA_NOPE:(h + 1) * MLA_NOPE], kn_nope))
        ks.append(k_pe)
    return jnp.concatenate(qs, axis=1), jnp.concatenate(ks, axis=1), kv[:, MLA_HEADS * MLA_NOPE:]


def _mla_prep_opds(proj, cos_p, sin_p, params, tm, grads):
    g = (lambda k: k) if grads else (lambda k: None)
    c0 = 4 * HG_WIDTH
    return ([rows(proj, tm, g('blk'), col=c0 // MLA_Q_RANK, width=MLA_Q_RANK),
             rows(proj, tm, g('blk'), col=(c0 + MLA_Q_RANK) // 128, width=128),
             rows(proj, tm, g('blk'), col=(c0 + MLA_Q_RANK) // 128 + 1, width=128),
             rows(cos_p, tm), rows(sin_p, tm)] + [full(p, g('acc')) for p in params])


def mla_prep_fwd(proj, cos_p, sin_p, params, tm):
    L = proj.shape[0]
    W = MLA_HEADS * MLA_DK
    rb = lambda w: (tm, w)
    outs = [((L, W), MXU_DTYPE, rb(W), lambda i: (i, 0)), ((L, W), MXU_DTYPE, rb(W), lambda i: (i, 0)),
            ((L, MLA_HEADS * MLA_V), MXU_DTYPE, rb(MLA_HEADS * MLA_V), lambda i: (i, 0))]
    return blocked_fwd(_mla_prep, _mla_prep_opds(proj, cos_p, sin_p, params, tm, False), outs, L // tm, "mla_prep_fwd")


def mla_prep_bwd(proj, cos_p, sin_p, params, dq, dk, dv, tm):
    L = proj.shape[0]
    return blocked_bwd(_mla_prep, _mla_prep_opds(proj, cos_p, sin_p, params, tm, True),
                       [rows(dq, tm), rows(dk, tm), rows(dv, tm)], L // tm, "mla_prep_bwd")


def _causal_scores(q, k, scale, row0, col0):
    s = _dot(q, k, _NT) * scale
    row = row0 + lax.broadcasted_iota(jnp.int32, s.shape, 0)
    col = col0 + lax.broadcasted_iota(jnp.int32, s.shape, 1)
    return jnp.where(col <= row, s, -jnp.inf)


def attn_fwd(q, k, v, scale, t):
    L = q.shape[0]
    n = L // t

    def body(q_ref, k_ref, v_ref, o_ref, lse_ref):
        i = pl.program_id(1)
        qb = q_ref[...]

        def step(j, carry):
            m, l, acc = carry
            kj = k_ref[pl.ds(pl.multiple_of(j * t, t), t), :]
            vj = v_ref[pl.ds(pl.multiple_of(j * t, t), t), :]
            s = _causal_scores(qb, kj, scale, i * t, j * t)
            m_new = jnp.maximum(m, jnp.max(s, axis=-1, keepdims=True))
            p = jnp.exp(s - m_new)
            alpha = jnp.exp(m - m_new)
            return m_new, alpha * l + jnp.sum(p, axis=-1, keepdims=True), alpha * acc + _dot(p, vj, _NN)

        init = (jnp.full((t, 1), -jnp.inf, F32), jnp.zeros((t, 1), F32), jnp.zeros((t, MLA_V), F32))
        m, l, acc = lax.fori_loop(0, i + 1, step, init)
        o_ref[...] = acc / l
        lse_ref[...] = jnp.broadcast_to(m + jnp.log(l), lse_ref.shape)

    hspec = lambda rows_, w: pl.BlockSpec((rows_, w), lambda h, i: (0, h))
    bspec = lambda w: pl.BlockSpec((t, w), lambda h, i: (i, h))
    return pl.pallas_call(
        body, name="attn_fwd", grid=(MLA_HEADS, n),
        in_specs=[bspec(MLA_DK), hspec(L, MLA_DK), hspec(L, MLA_V)],
        out_specs=[bspec(MLA_V), bspec(MLA_V)],
        out_shape=[jax.ShapeDtypeStruct((L, MLA_HEADS * MLA_V), F32)] * 2,
        compiler_params=_cparams(("parallel", "parallel")),
    )(q, k, v)


def attn_bwd_dq(q, k, v, o, lse, do, scale, t):
    L = q.shape[0]
    n = L // t

    def body(q_ref, k_ref, v_ref, o_ref, lse_ref, do_ref, dq_ref):
        i = pl.program_id(1)
        qb, dob = q_ref[...], do_ref[...]
        delta = jnp.sum(dob * o_ref[...], axis=-1, keepdims=True)
        lse_c = jnp.max(lse_ref[...], axis=-1, keepdims=True)

        def step(j, dq):
            kj = k_ref[pl.ds(pl.multiple_of(j * t, t), t), :]
            vj = v_ref[pl.ds(pl.multiple_of(j * t, t), t), :]
            p = jnp.exp(_causal_scores(qb, kj, scale, i * t, j * t) - lse_c)
            ds = p * (_dot(dob, vj, _NT) - delta) * scale
            return dq + _dot(ds, kj, _NN)

        dq_ref[...] = lax.fori_loop(0, i + 1, step, jnp.zeros((t, MLA_DK), F32))

    hspec = lambda w: pl.BlockSpec((L, w), lambda h, i: (0, h))
    bspec = lambda w: pl.BlockSpec((t, w), lambda h, i: (i, h))
    return pl.pallas_call(
        body, name="attn_bwd_dq", grid=(MLA_HEADS, n),
        in_specs=[bspec(MLA_DK), hspec(MLA_DK), hspec(MLA_V), bspec(MLA_V), bspec(MLA_V), bspec(MLA_V)],
        out_specs=bspec(MLA_DK),
        out_shape=jax.ShapeDtypeStruct((L, MLA_HEADS * MLA_DK), F32),
        compiler_params=_cparams(("parallel", "parallel")),
    )(q, k, v, o, lse, do)


def attn_bwd_dkv(q, k, v, o, lse, do, scale, t):
    L = q.shape[0]
    n = L // t

    def body(q_ref, k_ref, v_ref, o_ref, lse_ref, do_ref, dk_ref, dv_ref):
        j = pl.program_id(1)
        kb, vb = k_ref[...], v_ref[...]

        def step(i, carry):
            dk, dv = carry
            r = pl.ds(pl.multiple_of(i * t, t), t)
            qi, doi = q_ref[r, :], do_ref[r, :]
            delta = jnp.sum(doi * o_ref[r, :], axis=-1, keepdims=True)
            lse_c = jnp.max(lse_ref[r, :], axis=-1, keepdims=True)
            p = jnp.exp(_causal_scores(qi, kb, scale, i * t, j * t) - lse_c)
            ds = p * (_dot(doi, vb, _NT) - delta) * scale
            return dk + _dot(ds, qi, _TN), dv + _dot(p, doi, _TN)

        dk, dv = lax.fori_loop(j, n, step, (jnp.zeros((t, MLA_DK), F32), jnp.zeros((t, MLA_V), F32)))
        dk_ref[...] = dk
        dv_ref[...] = dv

    hspec = lambda w: pl.BlockSpec((L, w), lambda h, j: (0, h))
    bspec = lambda w: pl.BlockSpec((t, w), lambda h, j: (j, h))
    return pl.pallas_call(
        body, name="attn_bwd_dkv", grid=(MLA_HEADS, n),
        in_specs=[hspec(MLA_DK), bspec(MLA_DK), bspec(MLA_V), hspec(MLA_V), hspec(MLA_V), hspec(MLA_V)],
        out_specs=[bspec(MLA_DK), bspec(MLA_V)],
        out_shape=[jax.ShapeDtypeStruct((L, MLA_HEADS * MLA_DK), F32), jax.ShapeDtypeStruct((L, MLA_HEADS * MLA_V), F32)],
        compiler_params=_cparams(("parallel", "parallel")),
    )(q, k, v, o, lse, do)


S5_LANES = S5_GB * S5_STATE


def _cmul(ar, ai, br, bi):
    return ar * br - ai * bi, ar * bi + ai * br


def _a_powers(ar, ai, reverse):
    a2 = _cmul(ar, ai, ar, ai)
    a4 = _cmul(*a2, *a2)
    row = lax.broadcasted_iota(jnp.int32, (8, ar.shape[1]), 0)
    e = (8 - row) if reverse else (row + 1)
    tr, ti = jnp.ones((8, ar.shape[1]), F32), jnp.zeros((8, ar.shape[1]), F32)
    for bit, (pr, pi) in ((1, (ar, ai)), (2, a2), (4, a4), (8, _cmul(*a4, *a4))):
        nr, ni = _cmul(tr, ti, pr, pi)
        sel = (e & bit) != 0
        tr, ti = jnp.where(sel, nr, tr), jnp.where(sel, ni, ti)
    return ((ar, ai), a2, a4), (tr, ti)


def _scan8(xr, xi, pows, table, cr, ci, reverse):
    row = lax.broadcasted_iota(jnp.int32, xr.shape, 0)
    for d, (pr, pi) in zip((1, 2, 4), pows):
        if reverse:
            keep = row < 8 - d
            sr, si = pltpu.roll(xr, 8 - d, 0), pltpu.roll(xi, 8 - d, 0)
        else:
            keep = row >= d
            sr, si = pltpu.roll(xr, d, 0), pltpu.roll(xi, d, 0)
        sr, si = jnp.where(keep, sr, 0.0), jnp.where(keep, si, 0.0)
        mr, mi = _cmul(pr, pi, sr, si)
        xr, xi = xr + mr, xi + mi
    mr, mi = _cmul(table[0], table[1], cr, ci)
    return xr + mr, xi + mi


def _row_of(x, r):
    row = lax.broadcasted_iota(jnp.int32, x.shape, 0)
    return jnp.sum(jnp.where(row == r, x, 0.0), axis=0, keepdims=True)


def _s5_scan_fwd(h_re, h_im, ar, ai, L):
    pows, table = _a_powers(ar, ai, False)

    def step(i, carry):
        r = pl.ds(pl.multiple_of(i * 8, 8), 8)
        xr, xi = _scan8(h_re[r, :], h_im[r, :], pows, table, carry[0], carry[1], False)
        h_re[r, :] = xr
        h_im[r, :] = xi
        return _row_of(xr, 7), _row_of(xi, 7)

    z = jnp.zeros((1, ar.shape[1]), F32)
    lax.fori_loop(0, L // 8, step, (z, z))


def _s5_specs(L):
    return [pl.BlockSpec((L, 128), lambda g: (0, g)),
            pl.BlockSpec((1, 128, S5_LANES), lambda g: (g, 0, 0)), pl.BlockSpec((1, 128, S5_LANES), lambda g: (g, 0, 0)),
            pl.BlockSpec((1, 1, S5_LANES), lambda g: (g, 0, 0)), pl.BlockSpec((1, 1, S5_LANES), lambda g: (g, 0, 0)),
            pl.BlockSpec((1, S5_LANES, 128), lambda g: (g, 0, 0)), pl.BlockSpec((1, S5_LANES, 128), lambda g: (g, 0, 0))]


def s5_fwd(u, w_re, w_im, a_re, a_im, c_re, c_im):
    L, D = u.shape

    def body(u_ref, wr, wi, ar, ai, cr, ci, y_ref, h_re, h_im):
        ub = u_ref[...]
        h_re[...] = _dot(ub, wr[0], _NN)
        h_im[...] = _dot(ub, wi[0], _NN)
        _s5_scan_fwd(h_re, h_im, ar[0], ai[0], L)
        y_ref[...] = _dot(h_re[...], cr[0], _NN) - _dot(h_im[...], ci[0], _NN)

    return pl.pallas_call(
        body, name="s5_fwd", grid=(D // 128,),
        in_specs=_s5_specs(L), out_specs=pl.BlockSpec((L, 128), lambda g: (0, g)),
        out_shape=jax.ShapeDtypeStruct((L, D), F32),
        scratch_shapes=[pltpu.VMEM((L, S5_LANES), F32), pltpu.VMEM((L, S5_LANES), F32)],
        compiler_params=_cparams(("parallel",)),
    )(u, w_re, w_im, a_re, a_im, c_re, c_im)


def s5_bwd(u, w_re, w_im, a_re, a_im, c_re, c_im, dy, tc):
    L, D = u.shape
    nch = L // tc

    def body(u_ref, wr, wi, ar_ref, ai_ref, cr, ci, dy_ref, du_ref, dwr, dwi, dar, dai, dcr, dci, h_re, h_im, g_re, g_im):
        ar, ai = ar_ref[0], ai_ref[0]
        ub = u_ref[...]
        h_re[...] = _dot(ub, wr[0], _NN)
        h_im[...] = _dot(ub, wi[0], _NN)
        _s5_scan_fwd(h_re, h_im, ar, ai, L)
        dyb = dy_ref[...]
        dcr[0] = _dot(h_re[...], dyb, _TN)
        dci[0] = -_dot(h_im[...], dyb, _TN)
        pows, table = _a_powers(ar, -ai, True)
        dwr[0] = jnp.zeros((128, S5_LANES), F32)
        dwi[0] = jnp.zeros((128, S5_LANES), F32)
        z1 = jnp.zeros((1, S5_LANES), F32)
        z8 = jnp.zeros((8, S5_LANES), F32)

        def chunk(cc, carry):
            c0 = pl.multiple_of((nch - 1 - cc) * tc, tc)
            rows_c = pl.ds(c0, tc)
            dyc = dy_ref[rows_c, :]
            g_re[...] = _dot(dyc, cr[0], _NT)
            g_im[...] = -_dot(dyc, ci[0], _NT)

            def step(ii, cy):
                gr_c, gi_c, acc_r, acc_i = cy
                i8 = pl.multiple_of((tc // 8 - 1 - ii) * 8, 8)
                rl = pl.ds(i8, 8)
                xr, xi = _scan8(g_re[rl, :], g_im[rl, :], pows, table, gr_c, gi_c, True)
                g_re[rl, :] = xr
                g_im[rl, :] = xi
                t0 = c0 + i8
                hb_r, hb_i = h_re[pl.ds(t0, 8), :], h_im[pl.ds(t0, 8), :]
                tp = pl.multiple_of(jnp.maximum(t0 - 8, 0), 8)
                first = (t0 > 0).astype(F32)
                pr = _row_of(h_re[pl.ds(tp, 8), :], 7) * first
                pi = _row_of(h_im[pl.ds(tp, 8), :], 7) * first
                row = lax.broadcasted_iota(jnp.int32, xr.shape, 0)
                hp_r = jnp.where(row == 0, pr, pltpu.roll(hb_r, 1, 0))
                hp_i = jnp.where(row == 0, pi, pltpu.roll(hb_i, 1, 0))
                return (_row_of(xr, 0), _row_of(xi, 0),
                        acc_r + xr * hp_r + xi * hp_i, acc_i + xi * hp_r - xr * hp_i)

            cy = lax.fori_loop(0, tc // 8, step, carry)
            uc = u_ref[rows_c, :]
            gr, gi = g_re[...], g_im[...]
            du_ref[rows_c, :] = _dot(gr, wr[0], _NT) + _dot(gi, wi[0], _NT)
            dwr[0] += _dot(uc, gr, _TN)
            dwi[0] += _dot(uc, gi, _TN)
            return cy

        _, _, acc_r, acc_i = lax.fori_loop(0, nch, chunk, (z1, z1, z8, z8))
        dar[0] = jnp.sum(acc_r, axis=0, keepdims=True)
        dai[0] = jnp.sum(acc_i, axis=0, keepdims=True)

    specs = _s5_specs(L)
    return pl.pallas_call(
        body, name="s5_bwd", grid=(D // 128,),
        in_specs=specs + [pl.BlockSpec((L, 128), lambda g: (0, g))],
        out_specs=[pl.BlockSpec((L, 128), lambda g: (0, g))] + specs[1:],
        out_shape=[jax.ShapeDtypeStruct((L, D), F32)] + [jax.ShapeDtypeStruct(x.shape, F32)
                                                        for x in (w_re, w_im, a_re, a_im, c_re, c_im)],
        scratch_shapes=[pltpu.VMEM((L, S5_LANES), F32), pltpu.VMEM((L, S5_LANES), F32),
                        pltpu.VMEM((tc, S5_LANES), F32), pltpu.VMEM((tc, S5_LANES), F32)],
        compiler_params=_cparams(("parallel",)),
    )(u, w_re, w_im, a_re, a_im, c_re, c_im, dy)


def _s5_discretize(lr, li, ldt, br, bi):
    dt = jnp.exp(ldt)
    mag = jnp.exp(lr * dt)
    ar, ai = mag * jnp.cos(li * dt), mag * jnp.sin(li * dt)
    den = lr * lr + li * li
    zr = ((ar - 1.0) * lr + ai * li) / den
    zi = (ai * lr - (ar - 1.0) * li) / den
    p = lax.broadcasted_iota(jnp.int32, (S5_STATE, S5_STATE * S5_GROUP), 0)
    c = lax.broadcasted_iota(jnp.int32, (S5_STATE, S5_STATE * S5_GROUP), 1)
    rep = (c // S5_GROUP == p).astype(F32)
    zr, zi = _dot(zr, rep, _NN, HI), _dot(zi, rep, _NN, HI)
    return ar, ai, zr * br - zi * bi, zr * bi + zi * br


def _conv_shift(x, d):
    row = lax.broadcasted_iota(jnp.int32, x.shape, 0)
    return jnp.where(row >= d, pltpu.roll(x, d, 0), 0.0)


def _conv_unshift(x, d):
    n = x.shape[0]
    row = lax.broadcasted_iota(jnp.int32, x.shape, 0)
    return jnp.where(row < n - d, pltpu.roll(x, n - d, 0), 0.0)


@functools.partial(jax.custom_vjp, nondiff_argnums=(1,))
def _shift_rows(x, d):
    return _conv_shift(x, d)


_shift_rows.defvjp(lambda x, d: (_conv_shift(x, d), None), lambda d, _, g: (_conv_unshift(g, d),))


def _conv_gate(ug, uv, wg0, wg1, wg2, wv0, wv1, wv2, bg, bv):
    def conv(u, w0, w1, w2, b):
        return u * w2 + _shift_rows(u, 1) * w1 + _shift_rows(u, 2) * w0 + b
    return (jax.nn.silu(conv(ug, wg0, wg1, wg2, bg)) * conv(uv, wv0, wv1, wv2, bv),)


def _rms_fn(x, gain):
    return (_rms(x, gain),)


def _softmax_rows(s):
    e = jnp.exp(s - lax.stop_gradient(jnp.max(s, axis=-1, keepdims=True)))
    return e / jnp.sum(e, axis=-1, keepdims=True)


def _xa_core(qp, k, v, q_gain):
    dh = qp.shape[1] // XA_HEADS
    outs = []
    for h in range(XA_HEADS):
        sl = slice(h * dh, (h + 1) * dh)
        p = _softmax_rows(_dot(_rms(qp[:, sl], q_gain), k[:, sl], _NT) * (dh ** -0.5))
        outs.append(_dot(p, v[:, sl], _NN))
    return (jnp.concatenate(outs, axis=1),)


def _mem_kv(mem, mem_gain, wk, wv, k_gain):
    m = _rms(mem, mem_gain)
    kp = _dot(m, wk, _NN)
    dh = kp.shape[1] // XA_HEADS
    k = jnp.concatenate([_rms(kp[:, h * dh:(h + 1) * dh], k_gain) for h in range(XA_HEADS)], axis=1)
    return k, _dot(m, wv, _NN)


def _s5_post(y, u, d):
    return (jax.nn.gelu(y + d * u),)


def _glu(a, b):
    return (a * jax.nn.sigmoid(b),)


def _lb_first(logits):
    e = jnp.exp(logits - lax.stop_gradient(jnp.max(logits, axis=0, keepdims=True)))
    return (_row_of(e, 0) / jnp.sum(e, axis=0, keepdims=True),)


def _loss_fn(y, t):
    e = y - t
    part = 0.5 * jnp.sum(e * e) / y.shape[1]
    return e * (1.0 / y.shape[1]), jnp.full((8, 128), part / (8 * 128), F32)


def _out(shape, dtype, tm):
    return (shape, dtype, (tm, shape[1]), lambda i: (i, 0))


def rms_fwd(h, gain, tm, dtype):
    return blocked_fwd(_rms_fn, [rows(h, tm), full(gain)], [_out(h.shape, dtype, tm)], h.shape[0] // tm, "rms_fwd")[0]


def rms_bwd(h, gain, dy, tm):
    return blocked_bwd(_rms_fn, [rows(h, tm, 'blk'), full(gain, 'acc')], [rows(dy, tm)], h.shape[0] // tm, "rms_bwd")


def adamw(w, g, m, v, name):
    R = w.shape[0]
    tm = _tile(R, 256)
    assert g.shape == w.shape == m.shape == v.shape, (name, w.shape, g.shape)

    def body(w_ref, g_ref, m_ref, v_ref, d_ref, nm_ref, nv_ref):
        g_ = g_ref[...]
        m_ = ADAM_B1 * m_ref[...] + (1.0 - ADAM_B1) * g_
        v_ = ADAM_B2 * v_ref[...] + (1.0 - ADAM_B2) * jnp.square(g_)
        m_hat = m_ / (1.0 - ADAM_B1 ** ADAM_STEP)
        v_hat = v_ / (1.0 - ADAM_B2 ** ADAM_STEP)
        d_ref[...] = -ADAM_LR * (m_hat / (jnp.sqrt(v_hat) + ADAM_EPS) + ADAM_WD * w_ref[...])
        nm_ref[...] = m_
        nv_ref[...] = v_

    spec = pl.BlockSpec((tm, w.shape[1]), lambda i: (i, 0))
    return pl.pallas_call(
        body, name=name, grid=(R // tm,), in_specs=[spec] * 4, out_specs=[spec] * 3,
        out_shape=[jax.ShapeDtypeStruct(w.shape, F32)] * 3, compiler_params=_cparams(("parallel",)),
    )(w, g, m, v)


def add2(x, y, name, out_dtype=F32):
    shape = x.shape
    x, y = x.reshape(-1, shape[-1]), y.reshape(-1, shape[-1])
    R, C = x.shape
    tm = _tile(R, 256)

    def body(x_ref, y_ref, o_ref):
        o_ref[...] = (x_ref[...] + y_ref[...]).astype(o_ref.dtype)

    spec = pl.BlockSpec((tm, C), lambda i: (i, 0))
    return pl.pallas_call(
        body, name=name, grid=(R // tm,), in_specs=[spec, spec], out_specs=spec,
        out_shape=jax.ShapeDtypeStruct((R, C), out_dtype), compiler_params=_cparams(("parallel",)),
    )(x, y).reshape(shape)


def add_chips(x, name):
    n, R, C = x.shape
    tm = _tile(R, 256)

    def body(*refs):
        acc = refs[0][...].astype(F32)
        for r in refs[1:-1]:
            acc = acc + r[...].astype(F32)
        refs[-1][...] = acc

    return pl.pallas_call(
        body, name=name, grid=(R // tm,),
        in_specs=[pl.BlockSpec((None, tm, C), lambda i, q=q: (q, i, 0)) for q in range(n)],
        out_specs=pl.BlockSpec((tm, C), lambda i: (i, 0)),
        out_shape=jax.ShapeDtypeStruct((R, C), F32), compiler_params=_cparams(("parallel",)),
    )(*([x] * n))


_HBM = pl.BlockSpec(memory_space=pltpu.HBM)
N_CHIPS = 4


def _my_place():
    return lax.axis_index("x"), lax.axis_index("y"), lax.axis_index("c")


def _window(ref, axis, start, size):
    idx = [slice(None)] * len(ref.shape)
    idx[axis] = pl.ds(start, size)
    return ref.at[tuple(idx)]


def _comm_call(body, name, xs, out_shapes, n_remote, n_local):
    return pl.pallas_call(
        body, name=name, in_specs=[_HBM] * len(xs), out_specs=[_HBM] * len(out_shapes), out_shape=out_shapes,
        scratch_shapes=[pltpu.SemaphoreType.DMA((n_remote,)), pltpu.SemaphoreType.DMA((n_remote,)),
                        pltpu.SemaphoreType.DMA((max(n_local, 1),))],
        compiler_params=pltpu.CompilerParams(has_side_effects=True),
    )(*xs)


def _run(copies):
    for cp in copies:
        cp.start()
    for cp in copies:
        cp.wait()


def _other_chips(mx, my):
    return [(mx ^ (j >> 1), my ^ (j & 1)) for j in (1, 2, 3)]


def chip_gather(xs, axes, name):
    n = len(xs)
    shapes, final = [], []
    for x, ax in zip(xs, axes):
        s = list(x.shape)
        if ax is None:
            shapes.append([N_CHIPS] + s)
            final.append(shapes[-1])
        elif ax < x.ndim - 1:
            shapes.append(s[:ax] + [N_CHIPS] + s[ax:])
            final.append(s[:ax] + [N_CHIPS * s[ax]] + s[ax + 1:])
        else:
            assert s[ax] % 128 == 0, (name, s)
            shapes.append(s[:ax] + [N_CHIPS * s[ax]])
            final.append(shapes[-1])

    def body(*refs):
        x_refs, o_refs = refs[:n], refs[n:2 * n]
        send_sems, recv_sems, local_sems = refs[2 * n:]
        mx, my, mc = _my_place()
        q = 2 * mx + my
        copies = []
        for i, (x_ref, o_ref, ax) in enumerate(zip(x_refs, o_refs, axes)):
            if ax is None or ax < len(x_ref.shape) - 1:
                dst = o_ref.at[(slice(None),) * (ax or 0) + (q,)]
            else:
                dst = _window(o_ref, ax, q * x_ref.shape[ax], x_ref.shape[ax])
            copies.append(pltpu.make_async_copy(x_ref, dst, local_sems.at[i]))
            for j, (tx, ty) in enumerate(_other_chips(mx, my)):
                copies.append(pltpu.make_async_remote_copy(
                    src_ref=x_ref, dst_ref=dst, send_sem=send_sems.at[3 * i + j], recv_sem=recv_sems.at[3 * i + j],
                    device_id=(tx, ty, mc), device_id_type=MESH))
        _run(copies)

    out_shapes = [jax.ShapeDtypeStruct(tuple(s), x.dtype) for s, x in zip(shapes, xs)]
    return [o.reshape(f) for o, f in zip(_comm_call(body, name, xs, out_shapes, 3 * n, n), final)]


def pair_split(xs, name):
    n = len(xs)
    shapes = [jax.ShapeDtypeStruct((x.shape[0], x.shape[1] // 2, x.shape[2]), x.dtype) for x in xs]
    xs = [x.reshape(x.shape[0], 2, x.shape[1] // 2, x.shape[2]) for x in xs]

    def body(*refs):
        x_refs, mine_refs, theirs_refs = refs[:n], refs[n:2 * n], refs[2 * n:3 * n]
        send_sems, recv_sems, local_sems = refs[3 * n:]
        mx, my, mc = _my_place()
        copies = []
        for i, x_ref in enumerate(x_refs):
            copies.append(pltpu.make_async_copy(x_ref.at[:, mc], mine_refs[i], local_sems.at[i]))
            copies.append(pltpu.make_async_remote_copy(
                src_ref=x_ref.at[:, 1 - mc], dst_ref=theirs_refs[i], send_sem=send_sems.at[i],
                recv_sem=recv_sems.at[i], device_id=(mx, my, 1 - mc), device_id_type=MESH))
        _run(copies)

    out = _comm_call(body, name, xs, shapes + shapes, n, n)
    return out[:n], out[n:]


def chip_all_to_all(xs, name):
    n = len(xs)
    shapes = [jax.ShapeDtypeStruct(x.shape, x.dtype) for x in xs]

    def body(*refs):
        x_refs, o_refs = refs[:n], refs[n:2 * n]
        send_sems, recv_sems, local_sems = refs[2 * n:]
        mx, my, mc = _my_place()
        q = 2 * mx + my
        copies = []
        for i, (x_ref, o_ref) in enumerate(zip(x_refs, o_refs)):
            copies.append(pltpu.make_async_copy(x_ref.at[q], o_ref.at[q], local_sems.at[i]))
            for j, (tx, ty) in enumerate(_other_chips(mx, my)):
                copies.append(pltpu.make_async_remote_copy(
                    src_ref=x_ref.at[2 * tx + ty], dst_ref=o_ref.at[q], send_sem=send_sems.at[3 * i + j],
                    recv_sem=recv_sems.at[3 * i + j], device_id=(tx, ty, mc), device_id_type=MESH))
        _run(copies)

    return _comm_call(body, name, xs, shapes, 3 * n, n)


def pair_join(xs, name):
    n = len(xs)
    shapes = [jax.ShapeDtypeStruct((2,) + x.shape, x.dtype) for x in xs]

    def body(*refs):
        x_refs, o_refs = refs[:n], refs[n:2 * n]
        send_sems, recv_sems, local_sems = refs[2 * n:]
        mx, my, mc = _my_place()
        copies = []
        for i, (x_ref, o_ref) in enumerate(zip(x_refs, o_refs)):
            copies.append(pltpu.make_async_copy(x_ref, o_ref.at[mc], local_sems.at[i]))
            copies.append(pltpu.make_async_remote_copy(
                src_ref=x_ref, dst_ref=o_ref.at[mc], send_sem=send_sems.at[i], recv_sem=recv_sems.at[i],
                device_id=(mx, my, 1 - mc), device_id_type=MESH))
        _run(copies)

    return [o.reshape(2 * o.shape[1], o.shape[2]) for o in _comm_call(body, name, xs, shapes, n, n)]


WEIGHTS = ['norm_mix', 'norm_xa', 'norm_mem', 'norm_ffn', 'xa_wq', 'xa_wk', 'xa_wv', 'xa_wo', 'xa_q_norm', 'xa_k_norm',
           'ffn_w_up', 'ffn_conv_w', 'ffn_conv_b', 'ffn_w_down', 'hg_lb_logits', 'mix_w_in', 'hg_out_norm',
           'mla_q_a_norm', 'mla_w_uq', 'mla_kv_a_norm', 'mla_w_ukv', 'mla_qn_nope', 'mla_qn_rope', 'mla_kn_nope',
           'mla_kn_rope', 'mix_w_out', 's5_lam_re', 's5_lam_im', 's5_log_dt', 's5_b_re', 's5_b_im', 's5_c_re',
           's5_c_im', 's5_d', 's5_w_glu_a', 's5_w_glu_b']
INPUTS = ['x', 'mem', 'positions'] + WEIGHTS + ['loss_target'] + ['m_' + n for n in WEIGHTS] + ['v_' + n for n in WEIGHTS]
SHARD_AXIS = {'xa_wq': 1, 'xa_wk': 1, 'xa_wv': 1, 'xa_wo': 1, 'ffn_w_up': 2, 'ffn_conv_w': 2, 'ffn_w_down': 1,
              'mix_w_in': 2, 'mla_w_uq': 2, 'mla_w_ukv': 2, 'mix_w_out': 1, 's5_d': 1, 's5_w_glu_a': 1, 's5_w_glu_b': 1}
BIG = ['xa_wq', 'xa_wk', 'xa_wv', 'xa_wo', 'ffn_w_up', 'ffn_w_down', 'mix_w_in', 'mix_w_out', 's5_w_glu_a', 's5_w_glu_b']
SMALL_SHARDED = [n for n in WEIGHTS if n in SHARD_AXIS and n not in BIG]
REPLICATED = [n for n in WEIGHTS if n not in SHARD_AXIS]
SMALL = SMALL_SHARDED + REPLICATED
PACK_W = 1024
ROW_MULT = 16
W_IN_SHARD = IN_WIDTH // N_CHIPS
W_IN_SHARD_PAD = 640


def _pack(flats, mult=ROW_MULT):
    flat = jnp.concatenate([f.reshape(-1) for f in flats])
    unit = mult * PACK_W
    n = -(-flat.shape[0] // unit) * unit
    return jnp.pad(flat, (0, n - flat.shape[0])).reshape(n // PACK_W, PACK_W)


def _unpack(packed, shapes):
    flat, out, o = packed.reshape(-1), [], 0
    for s in shapes:
        n = math.prod(s)
        out.append(flat[o:o + n].reshape(s))
        o += n
    return out


def _rope_pad(w):
    z = jnp.zeros(w.shape[:-1] + (MLA_ROPE // 2,), w.dtype)
    return jnp.concatenate([w[..., :MLA_ROPE // 2], z, w[..., MLA_ROPE // 2:], z], axis=-1)


def _rope_unpad(g):
    return jnp.concatenate([g[..., :MLA_ROPE // 2], g[..., 64:64 + MLA_ROPE // 2]], axis=-1)


def _blockdiag_in(bb):
    nb = bb.shape[0] // S5_GB
    t = bb.reshape(nb, S5_GB, S5_STATE, S5_GROUP).transpose(0, 1, 3, 2)
    return jnp.einsum('bgmp,gh->bgmhp', t, jnp.eye(S5_GB, dtype=bb.dtype)).reshape(nb, S5_GB * S5_GROUP, S5_LANES)


def _blockdiag_in_t(dw):
    nb = dw.shape[0]
    t = jnp.einsum('bgmhp,gh->bgmp', dw.reshape(nb, S5_GB, S5_GROUP, S5_GB, S5_STATE), jnp.eye(S5_GB, dtype=dw.dtype))
    return t.transpose(0, 1, 3, 2).reshape(nb * S5_GB, S5_STATE, S5_GROUP)


def _blockdiag_out(c):
    nb = c.shape[0] // S5_GB
    t = c.reshape(nb, S5_GB, S5_GROUP, S5_STATE).transpose(0, 1, 3, 2)
    return jnp.einsum('bgpm,gh->bgphm', t, jnp.eye(S5_GB, dtype=c.dtype)).reshape(nb, S5_LANES, S5_GB * S5_GROUP)


def _blockdiag_out_t(dc):
    nb = dc.shape[0]
    t = jnp.einsum('bgphm,gh->bgpm', dc.reshape(nb, S5_GB, S5_STATE, S5_GB, S5_GROUP), jnp.eye(S5_GB, dtype=dc.dtype))
    return t.transpose(0, 1, 3, 2).reshape(nb * S5_GB, S5_GROUP, S5_STATE)


def _gather_weights(P):
    xs = [P[n].astype(BF16) for n in BIG] + [_pack([P[n] for n in SMALL_SHARDED])]
    axes = [None if n == 'mix_w_in' else SHARD_AXIS[n] for n in BIG] + [None]
    got = chip_gather(xs, axes, "gather_weights")
    full_w = dict(zip(BIG, got[:-1]))
    full_w['mix_w_in'] = jnp.concatenate([full_w['mix_w_in'][q] for q in range(N_CHIPS)], axis=SHARD_AXIS['mix_w_in'])
    per_chip = [_unpack(got[-1][q], [P[n].shape for n in SMALL_SHARDED]) for q in range(N_CHIPS)]
    for i, n in enumerate(SMALL_SHARDED):
        full_w[n] = jnp.concatenate([per_chip[q][i] for q in range(N_CHIPS)], axis=SHARD_AXIS[n])
    return full_w


def _reduce_and_update(GB, GS, P):
    mx, my, _ = _my_place()
    q = 2 * mx + my
    small = _pack([GS[n] for n in SMALL], 2 * N_CHIPS * ROW_MULT)
    xs = [GB[n] for n in BIG] + [small.reshape(N_CHIPS, -1, PACK_W)]
    mine, theirs = pair_split(xs, "grads_pair_split")
    names = BIG + ['small']
    pair = [add2(a, b, "grads_pair_sum_" + n, F32 if n == 'small' else BF16) for a, b, n in zip(mine, theirs, names)]
    got = chip_all_to_all(pair, "grads_chip_all_to_all")
    summed = [add_chips(g, "grads_chip_sum_" + n) for g, n in zip(got, names)]
    joined = pair_join(summed, "grads_pair_join")
    small_sum = chip_gather([joined[-1]], [0], "grads_small_gather")[0]
    g_small = dict(zip(SMALL, _unpack(small_sum, [GS[n].shape for n in SMALL])))
    for n in SMALL_SHARDED:
        s = P[n].shape[SHARD_AXIS[n]]
        g_small[n] = lax.dynamic_slice_in_dim(g_small[n], q * s, s, axis=SHARD_AXIS[n])

    grad, delta, new_m, new_v = {}, {}, {}, {}
    for n, g in zip(BIG, joined[:-1]):
        shape = P[n].shape
        if n == 'mix_w_in':
            g = g[:, :W_IN_SHARD]
        two_d = (g.shape[0], shape[-1])
        d, m_, v_ = adamw(P[n].reshape(two_d), g, P['m_' + n].reshape(two_d), P['v_' + n].reshape(two_d), "adamw_" + n)
        grad[n], delta[n], new_m[n], new_v[n] = (t.reshape(shape) for t in (g, d, m_, v_))
    packed = lambda prefix: _pack([P[prefix + n] for n in SMALL])
    d, m_, v_ = adamw(packed(''), _pack([g_small[n] for n in SMALL]), packed('m_'), packed('v_'), "adamw_small")
    shapes = [P[n].shape for n in SMALL]
    grad.update(g_small)
    for out, pk in ((delta, d), (new_m, m_), (new_v, v_)):
        out.update(zip(SMALL, _unpack(pk, shapes)))
    return grad, delta, new_m, new_v


def _row(v):
    return v.reshape(1, -1)


def _xattn_fwd(h, mem, W, lyr, tm):
    g_xa, g_mem = _row(W['norm_xa'][lyr]), _row(W['norm_mem'][lyr])
    g_q, g_k = _row(W['xa_q_norm'][lyr]), _row(W['xa_k_norm'][lyr])
    wq, wk, wv, wo = (W[n][lyr] for n in ('xa_wq', 'xa_wk', 'xa_wv', 'xa_wo'))
    L, D = h.shape
    M = mem.shape[0]
    hx = rms_fwd(h, g_xa, tm, MXU_DTYPE)
    qp = matmul(hx, wq, name="xa_q")
    kv_opds = [full(mem), full(g_mem), full(wk), full(wv), full(g_k)]
    k, v = blocked_fwd(_mem_kv, kv_opds, [((M, D), F32, (M, D), lambda i: (0, 0))] * 2, 1, "xa_mem_kv")
    o = blocked_fwd(_xa_core, [rows(qp, tm), full(k), full(v), full(g_q)], [_out((L, D), MXU_DTYPE, tm)], L // tm,
                    "xa_core")[0]
    out = matmul(o, wo, add=h, name="xa_o")
    return out, (h, hx, qp, k, v, o)


def _xattn_bwd(dout, saved, mem, W, lyr, tm):
    h, hx, qp, k, v, o = saved
    g_xa, g_mem = _row(W['norm_xa'][lyr]), _row(W['norm_mem'][lyr])
    g_q, g_k = _row(W['xa_q_norm'][lyr]), _row(W['xa_k_norm'][lyr])
    wq, wk, wv, wo = (W[n][lyr] for n in ('xa_wq', 'xa_wk', 'xa_wv', 'xa_wo'))
    L = h.shape[0]
    do = matmul(dout, wo, "nt", name="xa_do")
    d_wo = matmul(o, dout, "tn", name="xa_dwo")
    dqp, dk, dv, d_gq = blocked_bwd(_xa_core, [rows(qp, tm, 'blk'), full(k, 'acc'), full(v, 'acc'), full(g_q, 'acc')],
                                    [rows(do, tm)], L // tm, "xa_core_bwd")
    d_wq = matmul(hx, dqp, "tn", name="xa_dwq")
    dhx = matmul(dqp, wq, "nt", name="xa_dhx")
    dh, d_gxa = rms_bwd(h, g_xa, dhx, tm)
    d_gmem, d_wk, d_wv, d_gk = blocked_bwd(
        _mem_kv, [full(mem), full(g_mem, 'acc'), full(wk, 'acc'), full(wv, 'acc'), full(g_k, 'acc')],
        [full(dk), full(dv)], 1, "xa_mem_kv_bwd")
    by_chip = lambda g: g.reshape(N_CHIPS, g.shape[0] // N_CHIPS, g.shape[1])
    grads = {'norm_xa': d_gxa, 'norm_mem': d_gmem, 'xa_q_norm': d_gq, 'xa_k_norm': d_gk,
             'xa_wq': by_chip(d_wq), 'xa_wk': by_chip(d_wk), 'xa_wv': by_chip(d_wv), 'xa_wo': by_chip(d_wo)}
    return dout + dh, grads


def _conv_params(W, lyr):
    cw, cb = W['ffn_conv_w'][lyr], W['ffn_conv_b'][lyr]
    F = cw.shape[1] // 2
    return [cw[0:1, :F], cw[1:2, :F], cw[2:3, :F], cw[0:1, F:], cw[1:2, F:], cw[2:3, F:], _row(cb[:F]), _row(cb[F:])]


def _ffn_fwd(h, W, lyr, tm):
    L, D = h.shape
    w_up, w_down = W['ffn_w_up'][lyr], W['ffn_w_down'][lyr]
    F = w_down.shape[0]
    hf = rms_fwd(h, _row(W['norm_ffn'][lyr]), tm, MXU_DTYPE)
    ug = matmul(hf, w_up[:, :F], name="ffn_up_gate")
    uv = matmul(hf, w_up[:, F:], name="ffn_up_value")
    opds = [cols(ug, 128), cols(uv, 128)] + [cols(p, 128) for p in _conv_params(W, lyr)]
    a = blocked_fwd(_conv_gate, opds, [((L, F), MXU_DTYPE, (L, 128), lambda j: (0, j))], F // 128, "ffn_conv_gate")[0]
    out = matmul(a, w_down, add=h, name="ffn_down")
    return out, (h, hf, ug, uv, a)


def _ffn_bwd(dout, saved, W, lyr, tm):
    h, hf, ug, uv, a = saved
    w_up, w_down = W['ffn_w_up'][lyr], W['ffn_w_down'][lyr]
    F = w_down.shape[0]
    da = matmul(dout, w_down, "nt", name="ffn_da")
    d_wdown = matmul(a, dout, "tn", name="ffn_dwdown")
    opds = [cols(ug, 128, 'blk'), cols(uv, 128, 'blk')] + [cols(p, 128, 'blk') for p in _conv_params(W, lyr)]
    gs = blocked_bwd(_conv_gate, opds, [cols(da, 128)], F // 128, "ffn_conv_gate_bwd")
    dug, duv = gs[0], gs[1]
    d_cw = jnp.concatenate([jnp.concatenate(gs[2:5], axis=0), jnp.concatenate(gs[5:8], axis=0)], axis=1)
    d_cb = jnp.concatenate([gs[8], gs[9]], axis=1)[0]
    d_wup = jnp.concatenate([matmul(hf, dug, "tn", name="ffn_dwup_gate", col_blocks=N_CHIPS // 2),
                             matmul(hf, duv, "tn", name="ffn_dwup_value", col_blocks=N_CHIPS // 2)], axis=0)
    dhf = matmul(dug, w_up[:, :F], "nt", name="ffn_dhf_gate")
    dhf = matmul(duv, w_up[:, F:], "nt", add=dhf, name="ffn_dhf_value")
    dh, d_g = rms_bwd(h, _row(W['norm_ffn'][lyr]), dhf, tm)
    d_wdown = d_wdown.reshape(N_CHIPS, F // N_CHIPS, d_wdown.shape[1])
    return dout + dh, {'norm_ffn': d_g, 'ffn_w_up': d_wup, 'ffn_conv_w': d_cw, 'ffn_conv_b': d_cb, 'ffn_w_down': d_wdown}


def _mla_params(W):
    w_uq = W['mla_w_uq'][0].reshape(MLA_Q_RANK, MLA_HEADS, MLA_QK)
    w_uq = jnp.concatenate([w_uq[..., :MLA_NOPE], _rope_pad(w_uq[..., MLA_NOPE:])], axis=-1)
    w_ukv = W['mla_w_ukv'][0].reshape(MLA_KV_RANK, MLA_HEADS, MLA_NOPE + MLA_V)
    w_ukv = jnp.concatenate([w_ukv[..., :MLA_NOPE].reshape(MLA_KV_RANK, -1), w_ukv[..., MLA_NOPE:].reshape(MLA_KV_RANK, -1)],
                            axis=1)
    return [_row(W['mla_q_a_norm'][0]), w_uq.reshape(MLA_Q_RANK, MLA_HEADS * MLA_DK), _row(W['mla_kv_a_norm'][0]), w_ukv,
            _row(W['mla_qn_nope'][0]), _row(_rope_pad(W['mla_qn_rope'][0])), _row(W['mla_kn_nope'][0]),
            _row(_rope_pad(W['mla_kn_rope'][0]))]


def _w_in_padded(W):
    w = W['mix_w_in'][0]
    return jnp.concatenate([w[:, :IN_WIDTH - MLA_ROPE], _rope_pad(w[:, IN_WIDTH - MLA_ROPE:])], axis=1)


def _mixer0_fwd(h, W, cos_p, sin_p, tm):
    L = h.shape[0]
    t = min(256, L)
    hn = rms_fwd(h, _row(W['norm_mix'][0]), tm, MXU_DTYPE)
    proj = matmul(hn, _w_in_padded(W), name="mix_in")
    logits = W['hg_lb_logits']
    lb = blocked_fwd(_lb_first, [full(logits)], [((1, HG_WIDTH), F32, (1, HG_WIDTH), lambda i: (0, 0))], 1, "hg_lb")[0]
    gain = _row(W['hg_out_norm'][0])
    o_hg, states = hgrn2_fwd(proj, lb, gain)
    mp = _mla_params(W)
    q, k, v = mla_prep_fwd(proj, cos_p, sin_p, mp, tm)
    scale = MLA_QK ** -0.5
    o_mla, lse = attn_fwd(q, k, v, scale, t)
    w_out = W['mix_w_out'][0]
    out = matmul(o_hg, w_out[:HG_WIDTH], add=h, name="mix_out_hg")
    out = matmul(o_mla, w_out[HG_WIDTH:], add=out, name="mix_out_mla")
    return out, (h, hn, proj, lb, o_hg, states, q, k, v, o_mla, lse)


def _mixer0_bwd(dout, saved, W, cos_p, sin_p, tm):
    h, hn, proj, lb, o_hg, states, q, k, v, o_mla, lse = saved
    L = h.shape[0]
    t = min(256, L)
    scale = MLA_QK ** -0.5
    w_out = W['mix_w_out'][0]
    gain = _row(W['hg_out_norm'][0])
    do_hg = matmul(dout, w_out[:HG_WIDTH], "nt", name="mix_do_hg")
    do_mla = matmul(dout, w_out[HG_WIDTH:], "nt", name="mix_do_mla")
    d_wout = jnp.concatenate([matmul(o_hg, dout, "tn", name="mix_dwout_hg"), matmul(o_mla, dout, "tn", name="mix_dwout_mla")],
                             axis=0)
    dq = attn_bwd_dq(q, k, v, o_mla, lse, do_mla, scale, t)
    dk, dv = attn_bwd_dkv(q, k, v, o_mla, lse, do_mla, scale, t)
    mp = _mla_params(W)
    dcq, dckv, dkpe, d_qa, d_wuq, d_kva, d_wukv, d_qnn, d_qnr, d_knn, d_knr = mla_prep_bwd(proj, cos_p, sin_p, mp, dq, dk, dv, tm)
    d_hg, d_lb, d_gain = hgrn2_bwd(proj, lb, gain, states, do_hg)
    dproj = jnp.concatenate([d_hg, dcq, dckv, dkpe], axis=1)
    d_win = matmul(hn, dproj, "tn", name="mix_dwin")
    dhn = matmul(dproj, _w_in_padded(W), "nt", name="mix_dhn")
    dh, d_g = rms_bwd(h, _row(W['norm_mix'][0]), dhn, tm)
    logits = W['hg_lb_logits']
    d_logits = blocked_bwd(_lb_first, [full(logits, 'acc')], [full(d_lb)], 1, "hg_lb_bwd")[0]
    d_wuq = d_wuq.reshape(MLA_Q_RANK, MLA_HEADS, MLA_DK)
    d_wuq = jnp.concatenate([d_wuq[..., :MLA_NOPE], _rope_unpad(d_wuq[..., MLA_NOPE:])], axis=-1)
    hw = MLA_HEADS * MLA_NOPE
    d_wukv = jnp.concatenate([d_wukv[:, :hw].reshape(MLA_KV_RANK, MLA_HEADS, MLA_NOPE),
                              d_wukv[:, hw:].reshape(MLA_KV_RANK, MLA_HEADS, MLA_V)], axis=-1)
    d_win = jnp.concatenate([d_win[:, :IN_WIDTH - MLA_ROPE], _rope_unpad(d_win[:, IN_WIDTH - MLA_ROPE:])], axis=1)
    d_win = d_win.reshape(d_win.shape[0], N_CHIPS, W_IN_SHARD).transpose(1, 0, 2)
    d_win = jnp.pad(d_win, ((0, 0), (0, 0), (0, W_IN_SHARD_PAD - W_IN_SHARD)))
    d_wout = d_wout.reshape(N_CHIPS, d_wout.shape[0] // N_CHIPS, d_wout.shape[1])
    grads = {'norm_mix': d_g, 'hg_lb_logits': d_logits, 'mix_w_in': d_win, 'hg_out_norm': d_gain,
             'mla_q_a_norm': d_qa, 'mla_w_uq': d_wuq.reshape(1, MLA_Q_RANK, -1), 'mla_kv_a_norm': d_kva,
             'mla_w_ukv': d_wukv.reshape(1, MLA_KV_RANK, -1), 'mla_qn_nope': d_qnn, 'mla_qn_rope': _rope_unpad(d_qnr),
             'mla_kn_nope': d_knn, 'mla_kn_rope': _rope_unpad(d_knr), 'mix_w_out': d_wout}
    return dout + dh, grads


def _s5_inputs(W):
    G = W['s5_lam_re'].shape[1]
    return [W['s5_lam_re'][0], W['s5_lam_im'][0], W['s5_log_dt'][0].reshape(G, 1),
            W['s5_b_re'][0].reshape(G, -1), W['s5_b_im'][0].reshape(G, -1)]


def _mixer1_fwd(h, W, tm):
    L, D = h.shape
    u = rms_fwd(h, _row(W['norm_mix'][1]), tm, F32)
    di = _s5_inputs(W)
    G = di[0].shape[0]
    sq, wide = ((G, S5_STATE), F32, (G, S5_STATE), lambda i: (0, 0)), ((G, S5_STATE * S5_GROUP), F32, (G, S5_STATE * S5_GROUP), lambda i: (0, 0))
    ar, ai, bbr, bbi = blocked_fwd(_s5_discretize, [full(a) for a in di], [sq, sq, wide, wide], 1, "s5_discretize")
    nb = G // S5_GB
    core = (_blockdiag_in(bbr.reshape(G, S5_STATE, S5_GROUP)), _blockdiag_in(bbi.reshape(G, S5_STATE, S5_GROUP)),
            ar.reshape(nb, 1, S5_LANES), ai.reshape(nb, 1, S5_LANES),
            _blockdiag_out(W['s5_c_re'][0]), _blockdiag_out(W['s5_c_im'][0]))
    y = s5_fwd(u, *core)
    d = W['s5_d']
    y2 = blocked_fwd(_s5_post, [rows(y, tm), rows(u, tm), full(d)], [_out((L, D), MXU_DTYPE, tm)], L // tm, "s5_post")[0]
    w_ab = jnp.concatenate([W['s5_w_glu_a'][0], W['s5_w_glu_b'][0]], axis=1)
    ab = matmul(y2, w_ab, name="s5_glu_in")
    mix = blocked_fwd(_glu, [rows(ab, tm, col=0, width=D), rows(ab, tm, col=1, width=D)], [_out((L, D), F32, tm)], L // tm,
                      "s5_glu")[0]
    return h + mix, (h, u, core, y, y2, ab)


def _mixer1_bwd(dout, saved, W, tm):
    h, u, core, y, y2, ab = saved
    L, D = h.shape
    da, db = blocked_bwd(_glu, [rows(ab, tm, 'blk', col=0, width=D), rows(ab, tm, 'blk', col=1, width=D)], [rows(dout, tm)],
                         L // tm, "s5_glu_bwd")
    w_a, w_b = W['s5_w_glu_a'][0], W['s5_w_glu_b'][0]
    dy2 = matmul(da, w_a, "nt", name="s5_dy2_a")
    dy2 = matmul(db, w_b, "nt", add=dy2, name="s5_dy2_b")
    d_wa = matmul(y2, da, "tn", name="s5_dwa")
    d_wb = matmul(y2, db, "tn", name="s5_dwb")
    d = W['s5_d']
    dy, du_skip, d_d = blocked_bwd(_s5_post, [rows(y, tm, 'blk'), rows(u, tm, 'blk'), full(d, 'acc')], [rows(dy2, tm)], L // tm,
                                   "s5_post_bwd")
    du, dwr, dwi, dar, dai, dcr, dci = s5_bwd(u, *core, dy, min(256, L))
    di = _s5_inputs(W)
    G = di[0].shape[0]
    cts = [dar.reshape(G, S5_STATE), dai.reshape(G, S5_STATE), _blockdiag_in_t(dwr).reshape(G, -1), _blockdiag_in_t(dwi).reshape(G, -1)]
    d_lr, d_li, d_ldt, d_br, d_bi = blocked_bwd(_s5_discretize, [full(a, 'acc') for a in di], [full(c) for c in cts], 1,
                                                "s5_discretize_bwd")
    dh, d_g = rms_bwd(h, _row(W['norm_mix'][1]), du + du_skip, tm)
    bshape = W['s5_b_re'].shape
    grads = {'norm_mix': d_g, 's5_lam_re': d_lr[None], 's5_lam_im': d_li[None], 's5_log_dt': d_ldt.reshape(1, G),
             's5_b_re': d_br.reshape(bshape), 's5_b_im': d_bi.reshape(bshape), 's5_c_re': _blockdiag_out_t(dcr)[None],
             's5_c_im': _blockdiag_out_t(dci)[None], 's5_d': d_d, 's5_w_glu_a': d_wa.reshape(N_CHIPS, -1, D), 's5_w_glu_b': d_wb.reshape(N_CHIPS, -1, D)}
    return dout + dh, grads


def kernel(x, mem, positions, norm_mix, norm_xa, norm_mem, norm_ffn, xa_wq, xa_wk, xa_wv, xa_wo, xa_q_norm, xa_k_norm, ffn_w_up, ffn_conv_w, ffn_conv_b, ffn_w_down, hg_lb_logits, mix_w_in, hg_out_norm, mla_q_a_norm, mla_w_uq, mla_kv_a_norm, mla_w_ukv, mla_qn_nope, mla_qn_rope, mla_kn_nope, mla_kn_rope, mix_w_out, s5_lam_re, s5_lam_im, s5_log_dt, s5_b_re, s5_b_im, s5_c_re, s5_c_im, s5_d, s5_w_glu_a, s5_w_glu_b, loss_target, m_norm_mix, m_norm_xa, m_norm_mem, m_norm_ffn, m_xa_wq, m_xa_wk, m_xa_wv, m_xa_wo, m_xa_q_norm, m_xa_k_norm, m_ffn_w_up, m_ffn_conv_w, m_ffn_conv_b, m_ffn_w_down, m_hg_lb_logits, m_mix_w_in, m_hg_out_norm, m_mla_q_a_norm, m_mla_w_uq, m_mla_kv_a_norm, m_mla_w_ukv, m_mla_qn_nope, m_mla_qn_rope, m_mla_kn_nope, m_mla_kn_rope, m_mix_w_out, m_s5_lam_re, m_s5_lam_im, m_s5_log_dt, m_s5_b_re, m_s5_b_im, m_s5_c_re, m_s5_c_im, m_s5_d, m_s5_w_glu_a, m_s5_w_glu_b, v_norm_mix, v_norm_xa, v_norm_mem, v_norm_ffn, v_xa_wq, v_xa_wk, v_xa_wv, v_xa_wo, v_xa_q_norm, v_xa_k_norm, v_ffn_w_up, v_ffn_conv_w, v_ffn_conv_b, v_ffn_w_down, v_hg_lb_logits, v_mix_w_in, v_hg_out_norm, v_mla_q_a_norm, v_mla_w_uq, v_mla_kv_a_norm, v_mla_w_ukv, v_mla_qn_nope, v_mla_qn_rope, v_mla_kn_nope, v_mla_kn_rope, v_mix_w_out, v_s5_lam_re, v_s5_lam_im, v_s5_log_dt, v_s5_b_re, v_s5_b_im, v_s5_c_re, v_s5_c_im, v_s5_d, v_s5_w_glu_a, v_s5_w_glu_b):
    P = dict(locals())
    assert sorted(P) == sorted(INPUTS) and norm_mix.shape[0] == 2 and mix_w_in.shape[0] == 1
    x, mem, target = P['x'][0], P['mem'][0], P['loss_target'][0]
    L, D = x.shape
    tm = min(256, L)

    W = {n: P[n] for n in REPLICATED}
    W.update(_gather_weights(P))

    inv_freq = 1.0 / (ROPE_BASE ** (jnp.arange(0, MLA_ROPE, 2, dtype=F32) / MLA_ROPE))
    ang = P['positions'][0].astype(F32)[:, None] * inv_freq
    cos, sin, z = jnp.cos(ang), jnp.sin(ang), jnp.zeros_like(ang)
    cos_p = jnp.concatenate([cos, z, cos, z], axis=1)
    sin_p = jnp.concatenate([-sin, z, sin, z], axis=1)

    h, s_mix0 = _mixer0_fwd(x, W, cos_p, sin_p, tm)
    h, s_xa0 = _xattn_fwd(h, mem, W, 0, tm)
    h, s_ffn0 = _ffn_fwd(h, W, 0, tm)
    h, s_mix1 = _mixer1_fwd(h, W, tm)
    h, s_xa1 = _xattn_fwd(h, mem, W, 1, tm)
    h, s_ffn1 = _ffn_fwd(h, W, 1, tm)
    n = L // tm
    dh, parts = blocked_fwd(_loss_fn, [rows(h, tm), rows(target, tm)],
                            [_out((L, D), F32, tm), ((n * 8, 128), F32, (8, 128), lambda i: (i, 0))], n, "loss")
    loss = lax.psum(jnp.sum(parts), ("x", "y", "c"))

    layered = {}

    def collect(g, lyr):
        for k_, v_ in g.items():
            layered.setdefault(k_, {})[lyr] = v_

    dh, g = _ffn_bwd(dh, s_ffn1, W, 1, tm)
    collect(g, 1)
    dh, g = _xattn_bwd(dh, s_xa1, mem, W, 1, tm)
    collect(g, 1)
    dh, g = _mixer1_bwd(dh, s_mix1, W, tm)
    collect(g, 1)
    dh, g = _ffn_bwd(dh, s_ffn0, W, 0, tm)
    collect(g, 0)
    dh, g = _xattn_bwd(dh, s_xa0, mem, W, 0, tm)
    collect(g, 0)
    dx, g = _mixer0_bwd(dh, s_mix0, W, cos_p, sin_p, tm)
    collect(g, 0)

    GB, GS = {}, {}
    for name in WEIGHTS:
        by_layer = [layered[name][lyr] for lyr in sorted(layered[name])]
        if name in BIG:
            GB[name] = by_layer[0] if len(by_layer) == 1 else jnp.concatenate(by_layer, axis=1)
        else:
            full_shape = W[name].shape
            GS[name] = (by_layer[0].reshape(full_shape) if len(by_layer) == 1
                        else jnp.stack([g_.reshape(full_shape[1:]) for g_ in by_layer]))

    outs = _reduce_and_update(GB, GS, P)
    return (loss, dx[None], *[d[n_] for d in outs for n_ in WEIGHTS])
```

```python
import functools
import math

import jax
import jax.numpy as jnp
import numpy as np
from jax import lax
from jax.experimental import pallas as pl
from jax.experimental.pallas import tpu as pltpu

F32 = jnp.float32
BF16 = jnp.bfloat16
MXU_DTYPE = BF16
HI = lax.Precision.HIGHEST
V7X_VMEM_LIMIT_BYTES = 56 * 1024 * 1024
EPS = 1e-6
MESH = pl.DeviceIdType.MESH

HG_HEADS, HG_DIM = 4, 128
HG_WIDTH = HG_HEADS * HG_DIM
HG_SUB = 16
HG_BLOCK = 64
MLA_HEADS, MLA_Q_RANK, MLA_KV_RANK = 4, 256, 128
MLA_NOPE, MLA_ROPE, MLA_V = 128, 64, 128
MLA_QK = MLA_NOPE + MLA_ROPE
MLA_DK = 256
ROPE_BASE = 10000.0
IN_WIDTH = 4 * HG_WIDTH + MLA_Q_RANK + MLA_KV_RANK + MLA_ROPE
IN_PAD = 4 * HG_WIDTH + MLA_Q_RANK + MLA_KV_RANK + 128
S5_GROUP, S5_STATE = 16, 64
S5_GB = 8
DT_MIN, DT_MAX = 1e-3, 1e-1
XA_HEADS = 4
CONV_W = 3
ADAM_LR, ADAM_B1, ADAM_B2, ADAM_EPS, ADAM_WD, ADAM_STEP = 0.001, 0.9, 0.999, 1e-08, 0.01, 10


def _cparams(sem):
    return pltpu.CompilerParams(dimension_semantics=sem, vmem_limit_bytes=V7X_VMEM_LIMIT_BYTES)


class Opd:
    def __init__(self, arr, block, imap, grad=None, gshape=None, gimap=None):
        self.arr, self.block, self.imap, self.grad = arr, block, imap, grad
        self.gshape = arr.shape if gshape is None else gshape
        self.gimap = imap if gimap is None else gimap

    def spec(self):
        return pl.BlockSpec(self.block, self.imap)

    def gspec(self):
        return pl.BlockSpec(self.block, self.gimap)


def rows(arr, tm, grad=None, col=0, width=None):
    width = arr.shape[1] if width is None else width
    return Opd(arr, (tm, width), lambda i, c=col: (i, c), grad, (arr.shape[0], width), lambda i: (i, 0))


def cols(arr, tn, grad=None):
    return Opd(arr, (arr.shape[0], tn), lambda j: (0, j), grad)


def full(arr, grad=None):
    return Opd(arr, arr.shape, lambda i: (0, 0), grad)


def _load(ref):
    v = ref[...]
    return v.astype(F32) if jnp.issubdtype(v.dtype, jnp.floating) else v


def blocked_fwd(f, opds, outs, n, name):
    n_in = len(opds)

    def body(*refs):
        ys = f(*[_load(r) for r in refs[:n_in]])
        for r, y in zip(refs[n_in:], ys):
            r[...] = y.astype(r.dtype)

    res = pl.pallas_call(
        body, name=name, grid=(n,),
        in_specs=[o.spec() for o in opds],
        out_specs=[pl.BlockSpec(b, m) for (_, _, b, m) in outs],
        out_shape=[jax.ShapeDtypeStruct(s, d) for (s, d, _, _) in outs],
        compiler_params=_cparams(("parallel",)),
    )(*[o.arr for o in opds])
    return res


def blocked_bwd(f, opds, dys, n, name):
    n_in, n_dy = len(opds), len(dys)
    diff = [i for i, o in enumerate(opds) if o.grad]

    def body(*refs):
        vals = [_load(r) for r in refs[:n_in]]

        def fd(*dv):
            allv = list(vals)
            for i, v in zip(diff, dv):
                allv[i] = v
            return tuple(f(*allv))

        ys, vjp = jax.vjp(fd, *[vals[i] for i in diff])
        cts = tuple(_load(r).astype(y.dtype) for r, y in zip(refs[n_in:n_in + n_dy], ys))
        gs = vjp(cts)
        for r, g, i in zip(refs[n_in + n_dy:], gs, diff):
            if opds[i].grad == 'acc':
                @pl.when(pl.program_id(0) == 0)
                def _(r=r):
                    r[...] = jnp.zeros(r.shape, r.dtype)
                r[...] += g.astype(r.dtype)
            else:
                r[...] = g.astype(r.dtype)

    any_acc = any(opds[i].grad == 'acc' for i in diff)
    res = pl.pallas_call(
        body, name=name, grid=(n,),
        in_specs=[o.spec() for o in opds] + [o.spec() for o in dys],
        out_specs=[opds[i].gspec() for i in diff],
        out_shape=[jax.ShapeDtypeStruct(opds[i].gshape, F32) for i in diff],
        compiler_params=_cparams(("arbitrary" if any_acc else "parallel",)),
    )(*[o.arr for o in opds], *[o.arr for o in dys])
    return res


def _tile(dim, want):
    for t in range(want - want % 16, 0, -16):
        if dim % t == 0:
            return t
    assert dim <= want, (dim, want)
    return dim


MATMUL_VMEM_BUDGET = 40 * 1024 * 1024
MATMUL_ROWS = 512


def _widest(N, fits):
    for t in range(N - N % 128, 0, -128):
        if N % t == 0 and fits(t):
            return t
    return N


def matmul(a, b, mode="nn", out_dtype=F32, add=None, name="matmul", col_blocks=None):
    sa, sb, so = a.dtype.itemsize, b.dtype.itemsize, jnp.dtype(out_dtype).itemsize
    has_add = add is not None
    if mode == "tn":
        (K, M), (K2, N) = a.shape, b.shape
        assert K == K2 and not has_add and out_dtype == F32, (a.shape, b.shape)
        tk = _tile(K, MATMUL_ROWS)
        tn = _widest(N, lambda t: 2 * (tk * M * sa + tk * t * sb + M * t * 4) <= MATMUL_VMEM_BUDGET)
        if col_blocks is not None:
            assert N % col_blocks == 0 and (N // col_blocks) % 128 == 0 and tn >= N // col_blocks, (N, col_blocks, tn)
            tn = N // col_blocks
            out_spec = pl.BlockSpec((None, M, tn), lambda j, k: (j, 0, 0))
            out_shape = jax.ShapeDtypeStruct((col_blocks, M, tn), F32)
        else:
            out_spec = pl.BlockSpec((M, tn), lambda j, k: (0, j))
            out_shape = jax.ShapeDtypeStruct((M, N), F32)

        def body(a_ref, b_ref, o_ref):
            r = lax.dot_general(a_ref[...].astype(MXU_DTYPE), b_ref[...].astype(MXU_DTYPE), ((_TN), ((), ())),
                                preferred_element_type=F32)

            @pl.when(pl.program_id(1) == 0)
            def _():
                o_ref[...] = r

            @pl.when(pl.program_id(1) > 0)
            def _():
                o_ref[...] += r

        return pl.pallas_call(
            body, name=name, grid=(N // tn, K // tk),
            in_specs=[pl.BlockSpec((tk, M), lambda j, k: (k, 0)), pl.BlockSpec((tk, tn), lambda j, k: (k, j))],
            out_specs=out_spec, out_shape=out_shape,
            compiler_params=_cparams(("parallel", "arbitrary")),
        )(a, b)

    (M, K) = a.shape
    N = b.shape[1] if mode == "nn" else b.shape[0]
    assert K == (b.shape[0] if mode == "nn" else b.shape[1]), (a.shape, b.shape, mode)
    tm = _tile(M, MATMUL_ROWS)
    tn = _widest(N, lambda t: 2 * (tm * K * sa + K * t * sb + tm * t * (so + 4 * has_add)) <= MATMUL_VMEM_BUDGET)
    dims = ((_NN if mode == "nn" else _NT), ((), ()))

    def body(*refs):
        r = lax.dot_general(refs[0][...].astype(MXU_DTYPE), refs[1][...].astype(MXU_DTYPE), dims, preferred_element_type=F32)
        if has_add:
            r = r + refs[2][...].astype(F32)
        refs[-1][...] = r.astype(refs[-1].dtype)

    b_spec = pl.BlockSpec((K, tn), lambda j, i: (0, j)) if mode == "nn" else pl.BlockSpec((tn, K), lambda j, i: (j, 0))
    in_specs = [pl.BlockSpec((tm, K), lambda j, i: (i, 0)), b_spec]
    args = [a, b]
    if has_add:
        in_specs.append(pl.BlockSpec((tm, tn), lambda j, i: (i, j)))
        args.append(add)
    return pl.pallas_call(
        body, name=name, grid=(N // tn, M // tm),
        in_specs=in_specs,
        out_specs=pl.BlockSpec((tm, tn), lambda j, i: (i, j)),
        out_shape=jax.ShapeDtypeStruct((M, N), out_dtype),
        compiler_params=_cparams(("parallel", "parallel")),
    )(*args)


def _dot(a, b, dims, precision=None):
    if precision is None:
        a, b = a.astype(MXU_DTYPE), b.astype(MXU_DTYPE)
    return lax.dot_general(a, b, (dims, ((), ())), precision=precision, preferred_element_type=F32)


_NN = ((1,), (0,))
_NT = ((1,), (1,))
_TN = ((0,), (0,))


def _rms(x, gain):
    return x * lax.rsqrt(jnp.mean(x * x, axis=-1, keepdims=True) + EPS) * gain


def _hg_block(st_t, q, fl, iv, g, lb, gain):
    row = lax.broadcasted_iota(jnp.int32, (HG_SUB, HG_SUB), 0)
    col = lax.broadcasted_iota(jnp.int32, (HG_SUB, HG_SUB), 1)
    tri = (row >= col).astype(F32)
    outs, states = [], []
    for h in range(HG_HEADS):
        sl = slice(h * HG_DIM, (h + 1) * HG_DIM)
        st = st_t[h * HG_DIM:(h + 1) * HG_DIM, :]
        lbh = lb[:, sl]
        fg = lbh + (1.0 - lbh) * jax.nn.sigmoid(fl[:, sl])
        lf, kk, qf, v = jnp.log(fg), 1.0 - fg, jax.nn.silu(q[:, sl]), iv[:, sl]
        parts = []
        for s in range(q.shape[0] // HG_SUB):
            r = slice(s * HG_SUB, (s + 1) * HG_SUB)
            b = _dot(tri, lf[r], _NN, HI)
            b_end = jnp.sum(lf[r], axis=0, keepdims=True)
            qe = qf[r] * jnp.exp(b)
            sc = _dot(qe, kk[r] * jnp.exp(-b), _NT) * tri
            parts.append(_dot(sc, v[r], _NN) + _dot(qe, st, _NT))
            st = st * jnp.exp(b_end) + _dot(v[r], kk[r] * jnp.exp(b_end - b), _TN)
        o = jnp.concatenate(parts, axis=0)
        outs.append(_rms(o, gain[:, sl]) * jax.nn.silu(g[:, sl]))
        states.append(st)
    return jnp.concatenate(states, axis=0), jnp.concatenate(outs, axis=1)


def _hg_specs(proj, nb):
    return [pl.BlockSpec((HG_BLOCK, HG_WIDTH), lambda i, c=c, f=nb: (f(i), c)) for c in range(4)]


def hgrn2_fwd(proj, lb, gain):
    L = proj.shape[0]
    n = L // HG_BLOCK

    def body(q, fl, iv, g, lb_r, gain_r, o_ref, st_ref, st):
        @pl.when(pl.program_id(0) == 0)
        def _():
            st[...] = jnp.zeros(st.shape, F32)

        st_ref[0] = st[...]
        new, o = _hg_block(st[...], q[...], fl[...], iv[...], g[...], lb_r[...], gain_r[...])
        st[...] = new
        o_ref[...] = o.astype(o_ref.dtype)

    pspec = pl.BlockSpec((1, HG_WIDTH), lambda i: (0, 0))
    return pl.pallas_call(
        body, name="hgrn2_fwd", grid=(n,),
        in_specs=_hg_specs(proj, lambda i: i) + [pspec, pspec],
        out_specs=[pl.BlockSpec((HG_BLOCK, HG_WIDTH), lambda i: (i, 0)),
                   pl.BlockSpec((1, HG_WIDTH, HG_DIM), lambda i: (i, 0, 0))],
        out_shape=[jax.ShapeDtypeStruct((L, HG_WIDTH), MXU_DTYPE),
                   jax.ShapeDtypeStruct((n, HG_WIDTH, HG_DIM), F32)],
        scratch_shapes=[pltpu.VMEM((HG_WIDTH, HG_DIM), F32)],
        compiler_params=_cparams(("arbitrary",)),
    )(proj, proj, proj, proj, lb, gain)


def hgrn2_bwd(proj, lb, gain, states, do):
    L = proj.shape[0]
    n = L // HG_BLOCK

    def body(q, fl, iv, g, lb_r, gain_r, st_r, do_r, dproj, dlb, dgain, dst):
        @pl.when(pl.program_id(0) == 0)
        def _():
            dst[...] = jnp.zeros(dst.shape, F32)
            dlb[...] = jnp.zeros(dlb.shape, F32)
            dgain[...] = jnp.zeros(dgain.shape, F32)

        _, vjp = jax.vjp(_hg_block, st_r[0], q[...], fl[...], iv[...], g[...], lb_r[...], gain_r[...])
        d_st, dq, dfl, div, dg, d_lb, d_gain = vjp((dst[...], do_r[...].astype(F32)))
        dst[...] = d_st
        dproj[:, 0 * HG_WIDTH:1 * HG_WIDTH] = dq
        dproj[:, 1 * HG_WIDTH:2 * HG_WIDTH] = dfl
        dproj[:, 2 * HG_WIDTH:3 * HG_WIDTH] = div
        dproj[:, 3 * HG_WIDTH:4 * HG_WIDTH] = dg
        dlb[...] += d_lb
        dgain[...] += d_gain

    rev = lambda i: n - 1 - i
    pspec = pl.BlockSpec((1, HG_WIDTH), lambda i: (0, 0))
    return pl.pallas_call(
        body, name="hgrn2_bwd", grid=(n,),
        in_specs=_hg_specs(proj, rev) + [pspec, pspec,
                                         pl.BlockSpec((1, HG_WIDTH, HG_DIM), lambda i: (rev(i), 0, 0)),
                                         pl.BlockSpec((HG_BLOCK, HG_WIDTH), lambda i: (rev(i), 0))],
        out_specs=[pl.BlockSpec((HG_BLOCK, 4 * HG_WIDTH), lambda i: (rev(i), 0)), pspec, pspec],
        out_shape=[jax.ShapeDtypeStruct((L, 4 * HG_WIDTH), F32),
                   jax.ShapeDtypeStruct((1, HG_WIDTH), F32), jax.ShapeDtypeStruct((1, HG_WIDTH), F32)],
        scratch_shapes=[pltpu.VMEM((HG_WIDTH, HG_DIM), F32)],
        compiler_params=_cparams(("arbitrary",)),
    )(proj, proj, proj, proj, lb, gain, states, do)


def _rope_rms(x, gain_p, cos_p, sin_p):
    n = x * lax.rsqrt(jnp.sum(x * x, axis=-1, keepdims=True) * (1.0 / MLA_ROPE) + EPS) * gain_p
    r = lax.broadcasted_iota(jnp.int32, (128, 128), 0)
    c = lax.broadcasted_iota(jnp.int32, (128, 128), 1)
    swap = (r == (c + 64) % 128).astype(F32)
    return n * cos_p + _dot(n, swap, _NN, HI) * sin_p


def _mla_prep(c_q, c_kv, kpe, cos_p, sin_p, q_a, w_uq, kv_a, w_ukv, qn_nope, qn_rope, kn_nope, kn_rope):
    q = _dot(_rms(c_q, q_a), w_uq, _NN)
    kv = _dot(_rms(c_kv, kv_a), w_ukv, _NN)
    k_pe = _rope_rms(kpe, kn_rope, cos_p, sin_p)
    qs, ks = [], []
    for h in range(MLA_HEADS):
        qs.append(_rms(q[:, h * MLA_DK:h * MLA_DK + MLA_NOPE], qn_nope))
        qs.append(_rope_rms(q[:, h * MLA_DK + MLA_NOPE:(h + 1) * MLA_DK], qn_rope, cos_p, sin_p))
        ks.append(_rms(kv[:, h * MLA_NOPE:(h + 1) * MLA_NOPE], kn_nope))
        ks.append(k_pe)
    return jnp.concatenate(qs, axis=1), jnp.concatenate(ks, axis=1), kv[:, MLA_HEADS * MLA_NOPE:]


def _mla_prep_opds(proj, cos_p, sin_p, params, tm, grads):
    g = (lambda k: k) if grads else (lambda k: None)
    c0 = 4 * HG_WIDTH
    return ([rows(proj, tm, g('blk'), col=c0 // MLA_Q_RANK, width=MLA_Q_RANK),
             rows(proj, tm, g('blk'), col=(c0 + MLA_Q_RANK) // 128, width=128),
             rows(proj, tm, g('blk'), col=(c0 + MLA_Q_RANK) // 128 + 1, width=128),
             rows(cos_p, tm), rows(sin_p, tm)] + [full(p, g('acc')) for p in params])


def mla_prep_fwd(proj, cos_p, sin_p, params, tm):
    L = proj.shape[0]
    W = MLA_HEADS * MLA_DK
    rb = lambda w: (tm, w)
    outs = [((L, W), MXU_DTYPE, rb(W), lambda i: (i, 0)), ((L, W), MXU_DTYPE, rb(W), lambda i: (i, 0)),
            ((L, MLA_HEADS * MLA_V), MXU_DTYPE, rb(MLA_HEADS * MLA_V), lambda i: (i, 0))]
    return blocked_fwd(_mla_prep, _mla_prep_opds(proj, cos_p, sin_p, params, tm, False), outs, L // tm, "mla_prep_fwd")


def mla_prep_bwd(proj, cos_p, sin_p, params, dq, dk, dv, tm):
    L = proj.shape[0]
    return blocked_bwd(_mla_prep, _mla_prep_opds(proj, cos_p, sin_p, params, tm, True),
                       [rows(dq, tm), rows(dk, tm), rows(dv, tm)], L // tm, "mla_prep_bwd")


def _causal_scores(q, k, scale, row0, col0):
    s = _dot(q, k, _NT) * scale
    row = row0 + lax.broadcasted_iota(jnp.int32, s.shape, 0)
    col = col0 + lax.broadcasted_iota(jnp.int32, s.shape, 1)
    return jnp.where(col <= row, s, -jnp.inf)


def attn_fwd(q, k, v, scale, t):
    L = q.shape[0]
    n = L // t

    def body(q_ref, k_ref, v_ref, o_ref, lse_ref):
        i = pl.program_id(1)
        qb = q_ref[...]

        def step(j, carry):
            m, l, acc = carry
            kj = k_ref[pl.ds(pl.multiple_of(j * t, t), t), :]
            vj = v_ref[pl.ds(pl.multiple_of(j * t, t), t), :]
            s = _causal_scores(qb, kj, scale, i * t, j * t)
            m_new = jnp.maximum(m, jnp.max(s, axis=-1, keepdims=True))
            p = jnp.exp(s - m_new)
            alpha = jnp.exp(m - m_new)
            return m_new, alpha * l + jnp.sum(p, axis=-1, keepdims=True), alpha * acc + _dot(p, vj, _NN)

        init = (jnp.full((t, 1), -jnp.inf, F32), jnp.zeros((t, 1), F32), jnp.zeros((t, MLA_V), F32))
        m, l, acc = lax.fori_loop(0, i + 1, step, init)
        o_ref[...] = acc / l
        lse_ref[...] = jnp.broadcast_to(m + jnp.log(l), lse_ref.shape)

    hspec = lambda rows_, w: pl.BlockSpec((rows_, w), lambda h, i: (0, h))
    bspec = lambda w: pl.BlockSpec((t, w), lambda h, i: (i, h))
    return pl.pallas_call(
        body, name="attn_fwd", grid=(MLA_HEADS, n),
        in_specs=[bspec(MLA_DK), hspec(L, MLA_DK), hspec(L, MLA_V)],
        out_specs=[bspec(MLA_V), bspec(MLA_V)],
        out_shape=[jax.ShapeDtypeStruct((L, MLA_HEADS * MLA_V), F32)] * 2,
        compiler_params=_cparams(("parallel", "parallel")),
    )(q, k, v)


def attn_bwd_dq(q, k, v, o, lse, do, scale, t):
    L = q.shape[0]
    n = L // t

    def body(q_ref, k_ref, v_ref, o_ref, lse_ref, do_ref, dq_ref):
        i = pl.program_id(1)
        qb, dob = q_ref[...], do_ref[...]
        delta = jnp.sum(dob * o_ref[...], axis=-1, keepdims=True)
        lse_c = jnp.max(lse_ref[...], axis=-1, keepdims=True)

        def step(j, dq):
            kj = k_ref[pl.ds(pl.multiple_of(j * t, t), t), :]
            vj = v_ref[pl.ds(pl.multiple_of(j * t, t), t), :]
            p = jnp.exp(_causal_scores(qb, kj, scale, i * t, j * t) - lse_c)
            ds = p * (_dot(dob, vj, _NT) - delta) * scale
            return dq + _dot(ds, kj, _NN)

        dq_ref[...] = lax.fori_loop(0, i + 1, step, jnp.zeros((t, MLA_DK), F32))

    hspec = lambda w: pl.BlockSpec((L, w), lambda h, i: (0, h))
    bspec = lambda w: pl.BlockSpec((t, w), lambda h, i: (i, h))
    return pl.pallas_call(
        body, name="attn_bwd_dq", grid=(MLA_HEADS, n),
        in_specs=[bspec(MLA_DK), hspec(MLA_DK), hspec(MLA_V), bspec(MLA_V), bspec(MLA_V), bspec(MLA_V)],
        out_specs=bspec(MLA_DK),
        out_shape=jax.ShapeDtypeStruct((L, MLA_HEADS * MLA_DK), F32),
        compiler_params=_cparams(("parallel", "parallel")),
    )(q, k, v, o, lse, do)


def attn_bwd_dkv(q, k, v, o, lse, do, scale, t):
    L = q.shape[0]
    n = L // t

    def body(q_ref, k_ref, v_ref, o_ref, lse_ref, do_ref, dk_ref, dv_ref):
        j = pl.program_id(1)
        kb, vb = k_ref[...], v_ref[...]

        def step(i, carry):
            dk, dv = carry
            r = pl.ds(pl.multiple_of(i * t, t), t)
            qi, doi = q_ref[r, :], do_ref[r, :]
            delta = jnp.sum(doi * o_ref[r, :], axis=-1, keepdims=True)
            lse_c = jnp.max(lse_ref[r, :], axis=-1, keepdims=True)
            p = jnp.exp(_causal_scores(qi, kb, scale, i * t, j * t) - lse_c)
            ds = p * (_dot(doi, vb, _NT) - delta) * scale
            return dk + _dot(ds, qi, _TN), dv + _dot(p, doi, _TN)

        dk, dv = lax.fori_loop(j, n, step, (jnp.zeros((t, MLA_DK), F32), jnp.zeros((t, MLA_V), F32)))
        dk_ref[...] = dk
        dv_ref[...] = dv

    hspec = lambda w: pl.BlockSpec((L, w), lambda h, j: (0, h))
    bspec = lambda w: pl.BlockSpec((t, w), lambda h, j: (j, h))
    return pl.pallas_call(
        body, name="attn_bwd_dkv", grid=(MLA_HEADS, n),
        in_specs=[hspec(MLA_DK), bspec(MLA_DK), bspec(MLA_V), hspec(MLA_V), hspec(MLA_V), hspec(MLA_V)],
        out_specs=[bspec(MLA_DK), bspec(MLA_V)],
        out_shape=[jax.ShapeDtypeStruct((L, MLA_HEADS * MLA_DK), F32), jax.ShapeDtypeStruct((L, MLA_HEADS * MLA_V), F32)],
        compiler_params=_cparams(("parallel", "parallel")),
    )(q, k, v, o, lse, do)


S5_LANES = S5_GB * S5_STATE


def _cmul(ar, ai, br, bi):
    return ar * br - ai * bi, ar * bi + ai * br


def _a_powers(ar, ai, reverse):
    a2 = _cmul(ar, ai, ar, ai)
    a4 = _cmul(*a2, *a2)
    row = lax.broadcasted_iota(jnp.int32, (8, ar.shape[1]), 0)
    e = (8 - row) if reverse else (row + 1)
    tr, ti = jnp.ones((8, ar.shape[1]), F32), jnp.zeros((8, ar.shape[1]), F32)
    for bit, (pr, pi) in ((1, (ar, ai)), (2, a2), (4, a4), (8, _cmul(*a4, *a4))):
        nr, ni = _cmul(tr, ti, pr, pi)
        sel = (e & bit) != 0
        tr, ti = jnp.where(sel, nr, tr), jnp.where(sel, ni, ti)
    return ((ar, ai), a2, a4), (tr, ti)


def _scan8(xr, xi, pows, table, cr, ci, reverse):
    row = lax.broadcasted_iota(jnp.int32, xr.shape, 0)
    for d, (pr, pi) in zip((1, 2, 4), pows):
        if reverse:
            keep = row < 8 - d
            sr, si = pltpu.roll(xr, 8 - d, 0), pltpu.roll(xi, 8 - d, 0)
        else:
            keep = row >= d
            sr, si = pltpu.roll(xr, d, 0), pltpu.roll(xi, d, 0)
        sr, si = jnp.where(keep, sr, 0.0), jnp.where(keep, si, 0.0)
        mr, mi = _cmul(pr, pi, sr, si)
        xr, xi = xr + mr, xi + mi
    mr, mi = _cmul(table[0], table[1], cr, ci)
    return xr + mr, xi + mi


def _row_of(x, r):
    row = lax.broadcasted_iota(jnp.int32, x.shape, 0)
    return jnp.sum(jnp.where(row == r, x, 0.0), axis=0, keepdims=True)


def _s5_scan_fwd(h_re, h_im, ar, ai, L):
    pows, table = _a_powers(ar, ai, False)

    def step(i, carry):
        r = pl.ds(pl.multiple_of(i * 8, 8), 8)
        xr, xi = _scan8(h_re[r, :], h_im[r, :], pows, table, carry[0], carry[1], False)
        h_re[r, :] = xr
        h_im[r, :] = xi
        return _row_of(xr, 7), _row_of(xi, 7)

    z = jnp.zeros((1, ar.shape[1]), F32)
    lax.fori_loop(0, L // 8, step, (z, z))


def _s5_specs(L):
    return [pl.BlockSpec((L, 128), lambda g: (0, g)),
            pl.BlockSpec((1, 128, S5_LANES), lambda g: (g, 0, 0)), pl.BlockSpec((1, 128, S5_LANES), lambda g: (g, 0, 0)),
            pl.BlockSpec((1, 1, S5_LANES), lambda g: (g, 0, 0)), pl.BlockSpec((1, 1, S5_LANES), lambda g: (g, 0, 0)),
            pl.BlockSpec((1, S5_LANES, 128), lambda g: (g, 0, 0)), pl.BlockSpec((1, S5_LANES, 128), lambda g: (g, 0, 0))]


def s5_fwd(u, w_re, w_im, a_re, a_im, c_re, c_im):
    L, D = u.shape

    def body(u_ref, wr, wi, ar, ai, cr, ci, y_ref, h_re, h_im):
        ub = u_ref[...]
        h_re[...] = _dot(ub, wr[0], _NN)
        h_im[...] = _dot(ub, wi[0], _NN)
        _s5_scan_fwd(h_re, h_im, ar[0], ai[0], L)
        y_ref[...] = _dot(h_re[...], cr[0], _NN) - _dot(h_im[...], ci[0], _NN)

    return pl.pallas_call(
        body, name="s5_fwd", grid=(D // 128,),
        in_specs=_s5_specs(L), out_specs=pl.BlockSpec((L, 128), lambda g: (0, g)),
        out_shape=jax.ShapeDtypeStruct((L, D), F32),
        scratch_shapes=[pltpu.VMEM((L, S5_LANES), F32), pltpu.VMEM((L, S5_LANES), F32)],
        compiler_params=_cparams(("parallel",)),
    )(u, w_re, w_im, a_re, a_im, c_re, c_im)


def s5_bwd(u, w_re, w_im, a_re, a_im, c_re, c_im, dy, tc):
    L, D = u.shape
    nch = L // tc

    def body(u_ref, wr, wi, ar_ref, ai_ref, cr, ci, dy_ref, du_ref, dwr, dwi, dar, dai, dcr, dci, h_re, h_im, g_re, g_im):
        ar, ai = ar_ref[0], ai_ref[0]
        ub = u_ref[...]
        h_re[...] = _dot(ub, wr[0], _NN)
        h_im[...] = _dot(ub, wi[0], _NN)
        _s5_scan_fwd(h_re, h_im, ar, ai, L)
        dyb = dy_ref[...]
        dcr[0] = _dot(h_re[...], dyb, _TN)
        dci[0] = -_dot(h_im[...], dyb, _TN)
        pows, table = _a_powers(ar, -ai, True)
        dwr[0] = jnp.zeros((128, S5_LANES), F32)
        dwi[0] = jnp.zeros((128, S5_LANES), F32)
        z1 = jnp.zeros((1, S5_LANES), F32)
        z8 = jnp.zeros((8, S5_LANES), F32)

        def chunk(cc, carry):
            c0 = pl.multiple_of((nch - 1 - cc) * tc, tc)
            rows_c = pl.ds(c0, tc)
            dyc = dy_ref[rows_c, :]
            g_re[...] = _dot(dyc, cr[0], _NT)
            g_im[...] = -_dot(dyc, ci[0], _NT)

            def step(ii, cy):
                gr_c, gi_c, acc_r, acc_i = cy
                i8 = pl.multiple_of((tc // 8 - 1 - ii) * 8, 8)
                rl = pl.ds(i8, 8)
                xr, xi = _scan8(g_re[rl, :], g_im[rl, :], pows, table, gr_c, gi_c, True)
                g_re[rl, :] = xr
                g_im[rl, :] = xi
                t0 = c0 + i8
                hb_r, hb_i = h_re[pl.ds(t0, 8), :], h_im[pl.ds(t0, 8), :]
                tp = pl.multiple_of(jnp.maximum(t0 - 8, 0), 8)
                first = (t0 > 0).astype(F32)
                pr = _row_of(h_re[pl.ds(tp, 8), :], 7) * first
                pi = _row_of(h_im[pl.ds(tp, 8), :], 7) * first
                row = lax.broadcasted_iota(jnp.int32, xr.shape, 0)
                hp_r = jnp.where(row == 0, pr, pltpu.roll(hb_r, 1, 0))
                hp_i = jnp.where(row == 0, pi, pltpu.roll(hb_i, 1, 0))
                return (_row_of(xr, 0), _row_of(xi, 0),
                        acc_r + xr * hp_r + xi * hp_i, acc_i + xi * hp_r - xr * hp_i)

            cy = lax.fori_loop(0, tc // 8, step, carry)
            uc = u_ref[rows_c, :]
            gr, gi = g_re[...], g_im[...]
            du_ref[rows_c, :] = _dot(gr, wr[0], _NT) + _dot(gi, wi[0], _NT)
            dwr[0] += _dot(uc, gr, _TN)
            dwi[0] += _dot(uc, gi, _TN)
            return cy

        _, _, acc_r, acc_i = lax.fori_loop(0, nch, chunk, (z1, z1, z8, z8))
        dar[0] = jnp.sum(acc_r, axis=0, keepdims=True)
        dai[0] = jnp.sum(acc_i, axis=0, keepdims=True)

    specs = _s5_specs(L)
    return pl.pallas_call(
        body, name="s5_bwd", grid=(D // 128,),
        in_specs=specs + [pl.BlockSpec((L, 128), lambda g: (0, g))],
        out_specs=[pl.BlockSpec((L, 128), lambda g: (0, g))] + specs[1:],
        out_shape=[jax.ShapeDtypeStruct((L, D), F32)] + [jax.ShapeDtypeStruct(x.shape, F32)
                                                        for x in (w_re, w_im, a_re, a_im, c_re, c_im)],
        scratch_shapes=[pltpu.VMEM((L, S5_LANES), F32), pltpu.VMEM((L, S5_LANES), F32),
                        pltpu.VMEM((tc, S5_LANES), F32), pltpu.VMEM((tc, S5_LANES), F32)],
        compiler_params=_cparams(("parallel",)),
    )(u, w_re, w_im, a_re, a_im, c_re, c_im, dy)


def _s5_discretize(lr, li, ldt, br, bi):
    dt = jnp.exp(ldt)
    mag = jnp.exp(lr * dt)
    ar, ai = mag * jnp.cos(li * dt), mag * jnp.sin(li * dt)
    den = lr * lr + li * li
    zr = ((ar - 1.0) * lr + ai * li) / den
    zi = (ai * lr - (ar - 1.0) * li) / den
    p = lax.broadcasted_iota(jnp.int32, (S5_STATE, S5_STATE * S5_GROUP), 0)
    c = lax.broadcasted_iota(jnp.int32, (S5_STATE, S5_STATE * S5_GROUP), 1)
    rep = (c // S5_GROUP == p).astype(F32)
    zr, zi = _dot(zr, rep, _NN, HI), _dot(zi, rep, _NN, HI)
    return ar, ai, zr * br - zi * bi, zr * bi + zi * br


def _conv_shift(x, d):
    row = lax.broadcasted_iota(jnp.int32, x.shape, 0)
    return jnp.where(row >= d, pltpu.roll(x, d, 0), 0.0)


def _conv_unshift(x, d):
    n = x.shape[0]
    row = lax.broadcasted_iota(jnp.int32, x.shape, 0)
    return jnp.where(row < n - d, pltpu.roll(x, n - d, 0), 0.0)


@functools.partial(jax.custom_vjp, nondiff_argnums=(1,))
def _shift_rows(x, d):
    return _conv_shift(x, d)


_shift_rows.defvjp(lambda x, d: (_conv_shift(x, d), None), lambda d, _, g: (_conv_unshift(g, d),))


def _conv_gate(ug, uv, wg0, wg1, wg2, wv0, wv1, wv2, bg, bv):
    def conv(u, w0, w1, w2, b):
        return u * w2 + _shift_rows(u, 1) * w1 + _shift_rows(u, 2) * w0 + b
    return (jax.nn.silu(conv(ug, wg0, wg1, wg2, bg)) * conv(uv, wv0, wv1, wv2, bv),)


def _rms_fn(x, gain):
    return (_rms(x, gain),)


def _softmax_rows(s):
    e = jnp.exp(s - lax.stop_gradient(jnp.max(s, axis=-1, keepdims=True)))
    return e / jnp.sum(e, axis=-1, keepdims=True)


def _xa_core(qp, k, v, q_gain):
    dh = qp.shape[1] // XA_HEADS
    outs = []
    for h in range(XA_HEADS):
        sl = slice(h * dh, (h + 1) * dh)
        p = _softmax_rows(_dot(_rms(qp[:, sl], q_gain), k[:, sl], _NT) * (dh ** -0.5))
        outs.append(_dot(p, v[:, sl], _NN))
    return (jnp.concatenate(outs, axis=1),)


def _mem_kv(mem, mem_gain, wk, wv, k_gain):
    m = _rms(mem, mem_gain)
    kp = _dot(m, wk, _NN)
    dh = kp.shape[1] // XA_HEADS
    k = jnp.concatenate([_rms(kp[:, h * dh:(h + 1) * dh], k_gain) for h in range(XA_HEADS)], axis=1)
    return k, _dot(m, wv, _NN)


def _s5_post(y, u, d):
    return (jax.nn.gelu(y + d * u),)


def _glu(a, b):
    return (a * jax.nn.sigmoid(b),)


def _lb_first(logits):
    e = jnp.exp(logits - lax.stop_gradient(jnp.max(logits, axis=0, keepdims=True)))
    return (_row_of(e, 0) / jnp.sum(e, axis=0, keepdims=True),)


def _loss_fn(y, t):
    e = y - t
    part = 0.5 * jnp.sum(e * e) / y.shape[1]
    return e * (1.0 / y.shape[1]), jnp.full((8, 128), part / (8 * 128), F32)


def _out(shape, dtype, tm):
    return (shape, dtype, (tm, shape[1]), lambda i: (i, 0))


def rms_fwd(h, gain, tm, dtype):
    return blocked_fwd(_rms_fn, [rows(h, tm), full(gain)], [_out(h.shape, dtype, tm)], h.shape[0] // tm, "rms_fwd")[0]


def rms_bwd(h, gain, dy, tm):
    return blocked_bwd(_rms_fn, [rows(h, tm, 'blk'), full(gain, 'acc')], [rows(dy, tm)], h.shape[0] // tm, "rms_bwd")


def adamw(w, g, m, v, name):
    R = w.shape[0]
    tm = _tile(R, 256)
    assert g.shape == w.shape == m.shape == v.shape, (name, w.shape, g.shape)

    def body(w_ref, g_ref, m_ref, v_ref, d_ref, nm_ref, nv_ref):
        g_ = g_ref[...]
        m_ = ADAM_B1 * m_ref[...] + (1.0 - ADAM_B1) * g_
        v_ = ADAM_B2 * v_ref[...] + (1.0 - ADAM_B2) * jnp.square(g_)
        m_hat = m_ / (1.0 - ADAM_B1 ** ADAM_STEP)
        v_hat = v_ / (1.0 - ADAM_B2 ** ADAM_STEP)
        d_ref[...] = -ADAM_LR * (m_hat / (jnp.sqrt(v_hat) + ADAM_EPS) + ADAM_WD * w_ref[...])
        nm_ref[...] = m_
        nv_ref[...] = v_

    spec = pl.BlockSpec((tm, w.shape[1]), lambda i: (i, 0))
    return pl.pallas_call(
        body, name=name, grid=(R // tm,), in_specs=[spec] * 4, out_specs=[spec] * 3,
        out_shape=[jax.ShapeDtypeStruct(w.shape, F32)] * 3, compiler_params=_cparams(("parallel",)),
    )(w, g, m, v)


def add2(x, y, name, out_dtype=F32):
    shape = x.shape
    x, y = x.reshape(-1, shape[-1]), y.reshape(-1, shape[-1])
    R, C = x.shape
    tm = _tile(R, 256)

    def body(x_ref, y_ref, o_ref):
        o_ref[...] = (x_ref[...] + y_ref[...]).astype(o_ref.dtype)

    spec = pl.BlockSpec((tm, C), lambda i: (i, 0))
    return pl.pallas_call(
        body, name=name, grid=(R // tm,), in_specs=[spec, spec], out_specs=spec,
        out_shape=jax.ShapeDtypeStruct((R, C), out_dtype), compiler_params=_cparams(("parallel",)),
    )(x, y).reshape(shape)


def add_chips(own, got, name):
    n, R, C = got.shape
    tm = _tile(R, 256)

    def body(*refs):
        acc = refs[0][...].astype(F32)
        for r in refs[1:-1]:
            acc = acc + r[...].astype(F32)
        refs[-1][...] = acc

    return pl.pallas_call(
        body, name=name, grid=(R // tm,),
        in_specs=[pl.BlockSpec((tm, C), lambda i: (i, 0))] + [pl.BlockSpec((None, tm, C), lambda i, j=j: (j, i, 0))
                                                            for j in range(n)],
        out_specs=pl.BlockSpec((tm, C), lambda i: (i, 0)),
        out_shape=jax.ShapeDtypeStruct((R, C), F32), compiler_params=_cparams(("parallel",)),
    )(own, *([got] * n))


_HBM = pl.BlockSpec(memory_space=pltpu.HBM)
N_CHIPS = 4


def _my_place():
    return lax.axis_index("x"), lax.axis_index("y"), lax.axis_index("c")


def _window(ref, axis, start, size):
    idx = [slice(None)] * len(ref.shape)
    idx[axis] = pl.ds(start, size)
    return ref.at[tuple(idx)]


def _comm_call(body, name, xs, out_shapes, n_remote, n_local):
    return pl.pallas_call(
        body, name=name, in_specs=[_HBM] * len(xs), out_specs=[_HBM] * len(out_shapes), out_shape=out_shapes,
        scratch_shapes=[pltpu.SemaphoreType.DMA((n_remote,)), pltpu.SemaphoreType.DMA((n_remote,)),
                        pltpu.SemaphoreType.DMA((max(n_local, 1),))],
        compiler_params=pltpu.CompilerParams(has_side_effects=True),
    )(*xs)


def _run(copies):
    for cp in copies:
        cp.start()
    for cp in copies:
        cp.wait()


def _other_chips(mx, my):
    return [(mx ^ (j >> 1), my ^ (j & 1)) for j in (1, 2, 3)]


def chip_gather(xs, axes, name):
    n = len(xs)
    shapes, final = [], []
    for x, ax in zip(xs, axes):
        s = list(x.shape)
        if ax is None:
            shapes.append([N_CHIPS] + s)
            final.append(shapes[-1])
        elif ax < x.ndim - 1:
            shapes.append(s[:ax] + [N_CHIPS] + s[ax:])
            final.append(s[:ax] + [N_CHIPS * s[ax]] + s[ax + 1:])
        else:
            assert s[ax] % 128 == 0, (name, s)
            shapes.append(s[:ax] + [N_CHIPS * s[ax]])
            final.append(shapes[-1])

    def body(*refs):
        x_refs, o_refs = refs[:n], refs[n:2 * n]
        send_sems, recv_sems, local_sems = refs[2 * n:]
        mx, my, mc = _my_place()
        q = 2 * mx + my
        copies = []
        for i, (x_ref, o_ref, ax) in enumerate(zip(x_refs, o_refs, axes)):
            if ax is None or ax < len(x_ref.shape) - 1:
                dst = o_ref.at[(slice(None),) * (ax or 0) + (q,)]
            else:
                dst = _window(o_ref, ax, q * x_ref.shape[ax], x_ref.shape[ax])
            copies.append(pltpu.make_async_copy(x_ref, dst, local_sems.at[i]))
            for j, (tx, ty) in enumerate(_other_chips(mx, my)):
                copies.append(pltpu.make_async_remote_copy(
                    src_ref=x_ref, dst_ref=dst, send_sem=send_sems.at[3 * i + j], recv_sem=recv_sems.at[3 * i + j],
                    device_id=(tx, ty, mc), device_id_type=MESH))
        _run(copies)

    out_shapes = [jax.ShapeDtypeStruct(tuple(s), x.dtype) for s, x in zip(shapes, xs)]
    return [o.reshape(f) for o, f in zip(_comm_call(body, name, xs, out_shapes, 3 * n, n), final)]


def pair_swap(xs, name, halves):
    n = len(xs)
    shapes = [jax.ShapeDtypeStruct(x.shape[1:] if halves else x.shape, x.dtype) for x in xs]

    def body(*refs):
        x_refs, o_refs = refs[:n], refs[n:2 * n]
        send_sems, recv_sems, _ = refs[2 * n:]
        mx, my, mc = _my_place()
        _run([pltpu.make_async_remote_copy(
            src_ref=x_ref.at[1 - mc] if halves else x_ref, dst_ref=o_ref, send_sem=send_sems.at[i],
            recv_sem=recv_sems.at[i], device_id=(mx, my, 1 - mc), device_id_type=MESH)
            for i, (x_ref, o_ref) in enumerate(zip(x_refs, o_refs))])

    return _comm_call(body, name, xs, shapes, n, 0)


def chip_all_to_all(xs, name):
    n = len(xs)
    shapes = [jax.ShapeDtypeStruct((N_CHIPS - 1,) + x.shape[1:], x.dtype) for x in xs]

    def body(*refs):
        x_refs, o_refs = refs[:n], refs[n:2 * n]
        send_sems, recv_sems, _ = refs[2 * n:]
        mx, my, mc = _my_place()
        copies = []
        for i, (x_ref, o_ref) in enumerate(zip(x_refs, o_refs)):
            for j, (tx, ty) in enumerate(_other_chips(mx, my)):
                copies.append(pltpu.make_async_remote_copy(
                    src_ref=x_ref.at[2 * tx + ty], dst_ref=o_ref.at[j], send_sem=send_sems.at[3 * i + j],
                    recv_sem=recv_sems.at[3 * i + j], device_id=(tx, ty, mc), device_id_type=MESH))
        _run(copies)

    return _comm_call(body, name, xs, shapes, 3 * n, 0)


WEIGHTS = ['norm_mix', 'norm_xa', 'norm_mem', 'norm_ffn', 'xa_wq', 'xa_wk', 'xa_wv', 'xa_wo', 'xa_q_norm', 'xa_k_norm',
           'ffn_w_up', 'ffn_conv_w', 'ffn_conv_b', 'ffn_w_down', 'hg_lb_logits', 'mix_w_in', 'hg_out_norm',
           'mla_q_a_norm', 'mla_w_uq', 'mla_kv_a_norm', 'mla_w_ukv', 'mla_qn_nope', 'mla_qn_rope', 'mla_kn_nope',
           'mla_kn_rope', 'mix_w_out', 's5_lam_re', 's5_lam_im', 's5_log_dt', 's5_b_re', 's5_b_im', 's5_c_re',
           's5_c_im', 's5_d', 's5_w_glu_a', 's5_w_glu_b']
INPUTS = ['x', 'mem', 'positions'] + WEIGHTS + ['loss_target'] + ['m_' + n for n in WEIGHTS] + ['v_' + n for n in WEIGHTS]
SHARD_AXIS = {'xa_wq': 1, 'xa_wk': 1, 'xa_wv': 1, 'xa_wo': 1, 'ffn_w_up': 2, 'ffn_conv_w': 2, 'ffn_w_down': 1,
              'mix_w_in': 2, 'mla_w_uq': 2, 'mla_w_ukv': 2, 'mix_w_out': 1, 's5_d': 1, 's5_w_glu_a': 1, 's5_w_glu_b': 1}
BIG = ['xa_wq', 'xa_wk', 'xa_wv', 'xa_wo', 'ffn_w_up', 'ffn_w_down', 'mix_w_in', 'mix_w_out', 's5_w_glu_a', 's5_w_glu_b']
SMALL_SHARDED = [n for n in WEIGHTS if n in SHARD_AXIS and n not in BIG]
REPLICATED = [n for n in WEIGHTS if n not in SHARD_AXIS]
SMALL = SMALL_SHARDED + REPLICATED
PACK_W = 1024
ROW_MULT = 16
W_IN_SHARD = IN_WIDTH // N_CHIPS
W_IN_SHARD_PAD = 640


def _pack(flats, mult=ROW_MULT):
    flat = jnp.concatenate([f.reshape(-1) for f in flats])
    unit = mult * PACK_W
    n = -(-flat.shape[0] // unit) * unit
    return jnp.pad(flat, (0, n - flat.shape[0])).reshape(n // PACK_W, PACK_W)


def _unpack(packed, shapes):
    flat, out, o = packed.reshape(-1), [], 0
    for s in shapes:
        n = math.prod(s)
        out.append(flat[o:o + n].reshape(s))
        o += n
    return out


def _rope_pad(w):
    z = jnp.zeros(w.shape[:-1] + (MLA_ROPE // 2,), w.dtype)
    return jnp.concatenate([w[..., :MLA_ROPE // 2], z, w[..., MLA_ROPE // 2:], z], axis=-1)


def _rope_unpad(g):
    return jnp.concatenate([g[..., :MLA_ROPE // 2], g[..., 64:64 + MLA_ROPE // 2]], axis=-1)


def _blockdiag_in(bb):
    nb = bb.shape[0] // S5_GB
    t = bb.reshape(nb, S5_GB, S5_STATE, S5_GROUP).transpose(0, 1, 3, 2)
    return jnp.einsum('bgmp,gh->bgmhp', t, jnp.eye(S5_GB, dtype=bb.dtype)).reshape(nb, S5_GB * S5_GROUP, S5_LANES)


def _blockdiag_in_t(dw):
    nb = dw.shape[0]
    t = jnp.einsum('bgmhp,gh->bgmp', dw.reshape(nb, S5_GB, S5_GROUP, S5_GB, S5_STATE), jnp.eye(S5_GB, dtype=dw.dtype))
    return t.transpose(0, 1, 3, 2).reshape(nb * S5_GB, S5_STATE, S5_GROUP)


def _blockdiag_out(c):
    nb = c.shape[0] // S5_GB
    t = c.reshape(nb, S5_GB, S5_GROUP, S5_STATE).transpose(0, 1, 3, 2)
    return jnp.einsum('bgpm,gh->bgphm', t, jnp.eye(S5_GB, dtype=c.dtype)).reshape(nb, S5_LANES, S5_GB * S5_GROUP)


def _blockdiag_out_t(dc):
    nb = dc.shape[0]
    t = jnp.einsum('bgphm,gh->bgpm', dc.reshape(nb, S5_GB, S5_STATE, S5_GB, S5_GROUP), jnp.eye(S5_GB, dtype=dc.dtype))
    return t.transpose(0, 1, 3, 2).reshape(nb * S5_GB, S5_GROUP, S5_STATE)


def _gather_weights(P):
    xs = [P[n].astype(BF16) for n in BIG] + [_pack([P[n] for n in SMALL_SHARDED])]
    axes = [None if n == 'mix_w_in' else SHARD_AXIS[n] for n in BIG] + [None]
    got = chip_gather(xs, axes, "gather_weights")
    full_w = dict(zip(BIG, got[:-1]))
    full_w['mix_w_in'] = jnp.concatenate([full_w['mix_w_in'][q] for q in range(N_CHIPS)], axis=SHARD_AXIS['mix_w_in'])
    per_chip = [_unpack(got[-1][q], [P[n].shape for n in SMALL_SHARDED]) for q in range(N_CHIPS)]
    for i, n in enumerate(SMALL_SHARDED):
        full_w[n] = jnp.concatenate([per_chip[q][i] for q in range(N_CHIPS)], axis=SHARD_AXIS[n])
    return full_w


def _halves_first(x):
    return x.reshape(x.shape[0], 2, x.shape[1] // 2, x.shape[2]).transpose(1, 0, 2, 3)


def _reduce_and_update(GB, GS, P):
    mx, my, mc = _my_place()
    q = 2 * mx + my
    small = _pack([GS[n] for n in SMALL], 2 * N_CHIPS * ROW_MULT)
    xs = [GB[n] for n in BIG] + [_halves_first(small.reshape(N_CHIPS, -1, PACK_W))]
    theirs = pair_swap(xs, "grads_pair_swap", True)
    names = BIG + ['small']
    pair = [add2(lax.dynamic_index_in_dim(x, mc, 0, False), t, "grads_pair_sum_" + n, F32 if n == 'small' else BF16)
            for x, t, n in zip(xs, theirs, names)]
    got = chip_all_to_all(pair, "grads_chip_all_to_all")
    summed = [add_chips(lax.dynamic_index_in_dim(p, q, 0, False), g, "grads_chip_sum_" + n)
              for p, g, n in zip(pair, got, names)]
    other = pair_swap(summed, "grads_pair_join", False)
    joined = [lax.cond(mc == 0, lambda a, b: jnp.concatenate([a, b], axis=0), lambda a, b: jnp.concatenate([b, a], axis=0),
                       s, o) for s, o in zip(summed, other)]
    small_sum = chip_gather([joined[-1]], [0], "grads_small_gather")[0]
    g_small = dict(zip(SMALL, _unpack(small_sum, [GS[n].shape for n in SMALL])))
    for n in SMALL_SHARDED:
        s = P[n].shape[SHARD_AXIS[n]]
        g_small[n] = lax.dynamic_slice_in_dim(g_small[n], q * s, s, axis=SHARD_AXIS[n])

    grad, delta, new_m, new_v = {}, {}, {}, {}
    for n, g in zip(BIG, joined[:-1]):
        shape = P[n].shape
        if n == 'mix_w_in':
            g = g[:, :W_IN_SHARD]
        two_d = (g.shape[0], shape[-1])
        d, m_, v_ = adamw(P[n].reshape(two_d), g, P['m_' + n].reshape(two_d), P['v_' + n].reshape(two_d), "adamw_" + n)
        grad[n], delta[n], new_m[n], new_v[n] = (t.reshape(shape) for t in (g, d, m_, v_))
    packed = lambda prefix: _pack([P[prefix + n] for n in SMALL])
    d, m_, v_ = adamw(packed(''), _pack([g_small[n] for n in SMALL]), packed('m_'), packed('v_'), "adamw_small")
    shapes = [P[n].shape for n in SMALL]
    grad.update(g_small)
    for out, pk in ((delta, d), (new_m, m_), (new_v, v_)):
        out.update(zip(SMALL, _unpack(pk, shapes)))
    return grad, delta, new_m, new_v


def _row(v):
    return v.reshape(1, -1)


def _xattn_fwd(h, mem, W, lyr, tm):
    g_xa, g_mem = _row(W['norm_xa'][lyr]), _row(W['norm_mem'][lyr])
    g_q, g_k = _row(W['xa_q_norm'][lyr]), _row(W['xa_k_norm'][lyr])
    wq, wk, wv, wo = (W[n][lyr] for n in ('xa_wq', 'xa_wk', 'xa_wv', 'xa_wo'))
    L, D = h.shape
    M = mem.shape[0]
    hx = rms_fwd(h, g_xa, tm, MXU_DTYPE)
    qp = matmul(hx, wq, name="xa_q")
    kv_opds = [full(mem), full(g_mem), full(wk), full(wv), full(g_k)]
    k, v = blocked_fwd(_mem_kv, kv_opds, [((M, D), F32, (M, D), lambda i: (0, 0))] * 2, 1, "xa_mem_kv")
    o = blocked_fwd(_xa_core, [rows(qp, tm), full(k), full(v), full(g_q)], [_out((L, D), MXU_DTYPE, tm)], L // tm,
                    "xa_core")[0]
    out = matmul(o, wo, add=h, name="xa_o")
    return out, (h, hx, qp, k, v, o)


def _xattn_bwd(dout, saved, mem, W, lyr, tm):
    h, hx, qp, k, v, o = saved
    g_xa, g_mem = _row(W['norm_xa'][lyr]), _row(W['norm_mem'][lyr])
    g_q, g_k = _row(W['xa_q_norm'][lyr]), _row(W['xa_k_norm'][lyr])
    wq, wk, wv, wo = (W[n][lyr] for n in ('xa_wq', 'xa_wk', 'xa_wv', 'xa_wo'))
    L = h.shape[0]
    do = matmul(dout, wo, "nt", name="xa_do")
    d_wo = matmul(o, dout, "tn", name="xa_dwo")
    dqp, dk, dv, d_gq = blocked_bwd(_xa_core, [rows(qp, tm, 'blk'), full(k, 'acc'), full(v, 'acc'), full(g_q, 'acc')],
                                    [rows(do, tm)], L // tm, "xa_core_bwd")
    d_wq = matmul(hx, dqp, "tn", name="xa_dwq")
    dhx = matmul(dqp, wq, "nt", name="xa_dhx")
    dh, d_gxa = rms_bwd(h, g_xa, dhx, tm)
    d_gmem, d_wk, d_wv, d_gk = blocked_bwd(
        _mem_kv, [full(mem), full(g_mem, 'acc'), full(wk, 'acc'), full(wv, 'acc'), full(g_k, 'acc')],
        [full(dk), full(dv)], 1, "xa_mem_kv_bwd")
    by_chip = lambda g: g.reshape(N_CHIPS, g.shape[0] // N_CHIPS, g.shape[1])
    grads = {'norm_xa': d_gxa, 'norm_mem': d_gmem, 'xa_q_norm': d_gq, 'xa_k_norm': d_gk,
             'xa_wq': by_chip(d_wq), 'xa_wk': by_chip(d_wk), 'xa_wv': by_chip(d_wv), 'xa_wo': by_chip(d_wo)}
    return dout + dh, grads


def _conv_params(W, lyr):
    cw, cb = W['ffn_conv_w'][lyr], W['ffn_conv_b'][lyr]
    F = cw.shape[1] // 2
    return [cw[0:1, :F], cw[1:2, :F], cw[2:3, :F], cw[0:1, F:], cw[1:2, F:], cw[2:3, F:], _row(cb[:F]), _row(cb[F:])]


def _ffn_fwd(h, W, lyr, tm):
    L, D = h.shape
    w_up, w_down = W['ffn_w_up'][lyr], W['ffn_w_down'][lyr]
    F = w_down.shape[0]
    hf = rms_fwd(h, _row(W['norm_ffn'][lyr]), tm, MXU_DTYPE)
    ug = matmul(hf, w_up[:, :F], name="ffn_up_gate")
    uv = matmul(hf, w_up[:, F:], name="ffn_up_value")
    opds = [cols(ug, 128), cols(uv, 128)] + [cols(p, 128) for p in _conv_params(W, lyr)]
    a = blocked_fwd(_conv_gate, opds, [((L, F), MXU_DTYPE, (L, 128), lambda j: (0, j))], F // 128, "ffn_conv_gate")[0]
    out = matmul(a, w_down, add=h, name="ffn_down")
    return out, (h, hf, ug, uv, a)


def _ffn_bwd(dout, saved, W, lyr, tm):
    h, hf, ug, uv, a = saved
    w_up, w_down = W['ffn_w_up'][lyr], W['ffn_w_down'][lyr]
    F = w_down.shape[0]
    da = matmul(dout, w_down, "nt", name="ffn_da")
    d_wdown = matmul(a, dout, "tn", name="ffn_dwdown")
    opds = [cols(ug, 128, 'blk'), cols(uv, 128, 'blk')] + [cols(p, 128, 'blk') for p in _conv_params(W, lyr)]
    gs = blocked_bwd(_conv_gate, opds, [cols(da, 128)], F // 128, "ffn_conv_gate_bwd")
    dug, duv = gs[0], gs[1]
    d_cw = jnp.concatenate([jnp.concatenate(gs[2:5], axis=0), jnp.concatenate(gs[5:8], axis=0)], axis=1)
    d_cb = jnp.concatenate([gs[8], gs[9]], axis=1)[0]
    d_wup = jnp.concatenate([matmul(hf, dug, "tn", name="ffn_dwup_gate", col_blocks=N_CHIPS // 2),
                             matmul(hf, duv, "tn", name="ffn_dwup_value", col_blocks=N_CHIPS // 2)], axis=0)
    dhf = matmul(dug, w_up[:, :F], "nt", name="ffn_dhf_gate")
    dhf = matmul(duv, w_up[:, F:], "nt", add=dhf, name="ffn_dhf_value")
    dh, d_g = rms_bwd(h, _row(W['norm_ffn'][lyr]), dhf, tm)
    d_wdown = d_wdown.reshape(N_CHIPS, F // N_CHIPS, d_wdown.shape[1])
    return dout + dh, {'norm_ffn': d_g, 'ffn_w_up': d_wup, 'ffn_conv_w': d_cw, 'ffn_conv_b': d_cb, 'ffn_w_down': d_wdown}


def _mla_params(W):
    w_uq = W['mla_w_uq'][0].reshape(MLA_Q_RANK, MLA_HEADS, MLA_QK)
    w_uq = jnp.concatenate([w_uq[..., :MLA_NOPE], _rope_pad(w_uq[..., MLA_NOPE:])], axis=-1)
    w_ukv = W['mla_w_ukv'][0].reshape(MLA_KV_RANK, MLA_HEADS, MLA_NOPE + MLA_V)
    w_ukv = jnp.concatenate([w_ukv[..., :MLA_NOPE].reshape(MLA_KV_RANK, -1), w_ukv[..., MLA_NOPE:].reshape(MLA_KV_RANK, -1)],
                            axis=1)
    return [_row(W['mla_q_a_norm'][0]), w_uq.reshape(MLA_Q_RANK, MLA_HEADS * MLA_DK), _row(W['mla_kv_a_norm'][0]), w_ukv,
            _row(W['mla_qn_nope'][0]), _row(_rope_pad(W['mla_qn_rope'][0])), _row(W['mla_kn_nope'][0]),
            _row(_rope_pad(W['mla_kn_rope'][0]))]


def _w_in_padded(W):
    w = W['mix_w_in'][0]
    return jnp.concatenate([w[:, :IN_WIDTH - MLA_ROPE], _rope_pad(w[:, IN_WIDTH - MLA_ROPE:])], axis=1)


def _mixer0_fwd(h, W, cos_p, sin_p, tm):
    L = h.shape[0]
    t = min(256, L)
    hn = rms_fwd(h, _row(W['norm_mix'][0]), tm, MXU_DTYPE)
    proj = matmul(hn, _w_in_padded(W), name="mix_in")
    logits = W['hg_lb_logits']
    lb = blocked_fwd(_lb_first, [full(logits)], [((1, HG_WIDTH), F32, (1, HG_WIDTH), lambda i: (0, 0))], 1, "hg_lb")[0]
    gain = _row(W['hg_out_norm'][0])
    o_hg, states = hgrn2_fwd(proj, lb, gain)
    mp = _mla_params(W)
    q, k, v = mla_prep_fwd(proj, cos_p, sin_p, mp, tm)
    scale = MLA_QK ** -0.5
    o_mla, lse = attn_fwd(q, k, v, scale, t)
    w_out = W['mix_w_out'][0]
    out = matmul(o_hg, w_out[:HG_WIDTH], add=h, name="mix_out_hg")
    out = matmul(o_mla, w_out[HG_WIDTH:], add=out, name="mix_out_mla")
    return out, (h, hn, proj, lb, o_hg, states, q, k, v, o_mla, lse)


def _mixer0_bwd(dout, saved, W, cos_p, sin_p, tm):
    h, hn, proj, lb, o_hg, states, q, k, v, o_mla, lse = saved
    L = h.shape[0]
    t = min(256, L)
    scale = MLA_QK ** -0.5
    w_out = W['mix_w_out'][0]
    gain = _row(W['hg_out_norm'][0])
    do_hg = matmul(dout, w_out[:HG_WIDTH], "nt", name="mix_do_hg")
    do_mla = matmul(dout, w_out[HG_WIDTH:], "nt", name="mix_do_mla")
    d_wout = jnp.concatenate([matmul(o_hg, dout, "tn", name="mix_dwout_hg"), matmul(o_mla, dout, "tn", name="mix_dwout_mla")],
                             axis=0)
    dq = attn_bwd_dq(q, k, v, o_mla, lse, do_mla, scale, t)
    dk, dv = attn_bwd_dkv(q, k, v, o_mla, lse, do_mla, scale, t)
    mp = _mla_params(W)
    dcq, dckv, dkpe, d_qa, d_wuq, d_kva, d_wukv, d_qnn, d_qnr, d_knn, d_knr = mla_prep_bwd(proj, cos_p, sin_p, mp, dq, dk, dv, tm)
    d_hg, d_lb, d_gain = hgrn2_bwd(proj, lb, gain, states, do_hg)
    dproj = jnp.concatenate([d_hg, dcq, dckv, dkpe], axis=1)
    d_win = matmul(hn, dproj, "tn", name="mix_dwin")
    dhn = matmul(dproj, _w_in_padded(W), "nt", name="mix_dhn")
    dh, d_g = rms_bwd(h, _row(W['norm_mix'][0]), dhn, tm)
    logits = W['hg_lb_logits']
    d_logits = blocked_bwd(_lb_first, [full(logits, 'acc')], [full(d_lb)], 1, "hg_lb_bwd")[0]
    d_wuq = d_wuq.reshape(MLA_Q_RANK, MLA_HEADS, MLA_DK)
    d_wuq = jnp.concatenate([d_wuq[..., :MLA_NOPE], _rope_unpad(d_wuq[..., MLA_NOPE:])], axis=-1)
    hw = MLA_HEADS * MLA_NOPE
    d_wukv = jnp.concatenate([d_wukv[:, :hw].reshape(MLA_KV_RANK, MLA_HEADS, MLA_NOPE),
                              d_wukv[:, hw:].reshape(MLA_KV_RANK, MLA_HEADS, MLA_V)], axis=-1)
    d_win = jnp.concatenate([d_win[:, :IN_WIDTH - MLA_ROPE], _rope_unpad(d_win[:, IN_WIDTH - MLA_ROPE:])], axis=1)
    d_win = d_win.reshape(d_win.shape[0], N_CHIPS, W_IN_SHARD).transpose(1, 0, 2)
    d_win = jnp.pad(d_win, ((0, 0), (0, 0), (0, W_IN_SHARD_PAD - W_IN_SHARD)))
    d_wout = d_wout.reshape(N_CHIPS, d_wout.shape[0] // N_CHIPS, d_wout.shape[1])
    grads = {'norm_mix': d_g, 'hg_lb_logits': d_logits, 'mix_w_in': d_win, 'hg_out_norm': d_gain,
             'mla_q_a_norm': d_qa, 'mla_w_uq': d_wuq.reshape(1, MLA_Q_RANK, -1), 'mla_kv_a_norm': d_kva,
             'mla_w_ukv': d_wukv.reshape(1, MLA_KV_RANK, -1), 'mla_qn_nope': d_qnn, 'mla_qn_rope': _rope_unpad(d_qnr),
             'mla_kn_nope': d_knn, 'mla_kn_rope': _rope_unpad(d_knr), 'mix_w_out': d_wout}
    return dout + dh, grads


def _s5_inputs(W):
    G = W['s5_lam_re'].shape[1]
    return [W['s5_lam_re'][0], W['s5_lam_im'][0], W['s5_log_dt'][0].reshape(G, 1),
            W['s5_b_re'][0].reshape(G, -1), W['s5_b_im'][0].reshape(G, -1)]


def _mixer1_fwd(h, W, tm):
    L, D = h.shape
    u = rms_fwd(h, _row(W['norm_mix'][1]), tm, F32)
    di = _s5_inputs(W)
    G = di[0].shape[0]
    sq, wide = ((G, S5_STATE), F32, (G, S5_STATE), lambda i: (0, 0)), ((G, S5_STATE * S5_GROUP), F32, (G, S5_STATE * S5_GROUP), lambda i: (0, 0))
    ar, ai, bbr, bbi = blocked_fwd(_s5_discretize, [full(a) for a in di], [sq, sq, wide, wide], 1, "s5_discretize")
    nb = G // S5_GB
    core = (_blockdiag_in(bbr.reshape(G, S5_STATE, S5_GROUP)), _blockdiag_in(bbi.reshape(G, S5_STATE, S5_GROUP)),
            ar.reshape(nb, 1, S5_LANES), ai.reshape(nb, 1, S5_LANES),
            _blockdiag_out(W['s5_c_re'][0]), _blockdiag_out(W['s5_c_im'][0]))
    y = s5_fwd(u, *core)
    d = W['s5_d']
    y2 = blocked_fwd(_s5_post, [rows(y, tm), rows(u, tm), full(d)], [_out((L, D), MXU_DTYPE, tm)], L // tm, "s5_post")[0]
    w_ab = jnp.concatenate([W['s5_w_glu_a'][0], W['s5_w_glu_b'][0]], axis=1)
    ab = matmul(y2, w_ab, name="s5_glu_in")
    mix = blocked_fwd(_glu, [rows(ab, tm, col=0, width=D), rows(ab, tm, col=1, width=D)], [_out((L, D), F32, tm)], L // tm,
                      "s5_glu")[0]
    return h + mix, (h, u, core, y, y2, ab)


def _mixer1_bwd(dout, saved, W, tm):
    h, u, core, y, y2, ab = saved
    L, D = h.shape
    da, db = blocked_bwd(_glu, [rows(ab, tm, 'blk', col=0, width=D), rows(ab, tm, 'blk', col=1, width=D)], [rows(dout, tm)],
                         L // tm, "s5_glu_bwd")
    w_a, w_b = W['s5_w_glu_a'][0], W['s5_w_glu_b'][0]
    dy2 = matmul(da, w_a, "nt", name="s5_dy2_a")
    dy2 = matmul(db, w_b, "nt", add=dy2, name="s5_dy2_b")
    d_wa = matmul(y2, da, "tn", name="s5_dwa")
    d_wb = matmul(y2, db, "tn", name="s5_dwb")
    d = W['s5_d']
    dy, du_skip, d_d = blocked_bwd(_s5_post, [rows(y, tm, 'blk'), rows(u, tm, 'blk'), full(d, 'acc')], [rows(dy2, tm)], L // tm,
                                   "s5_post_bwd")
    du, dwr, dwi, dar, dai, dcr, dci = s5_bwd(u, *core, dy, min(256, L))
    di = _s5_inputs(W)
    G = di[0].shape[0]
    cts = [dar.reshape(G, S5_STATE), dai.reshape(G, S5_STATE), _blockdiag_in_t(dwr).reshape(G, -1), _blockdiag_in_t(dwi).reshape(G, -1)]
    d_lr, d_li, d_ldt, d_br, d_bi = blocked_bwd(_s5_discretize, [full(a, 'acc') for a in di], [full(c) for c in cts], 1,
                                                "s5_discretize_bwd")
    dh, d_g = rms_bwd(h, _row(W['norm_mix'][1]), du + du_skip, tm)
    bshape = W['s5_b_re'].shape
    grads = {'norm_mix': d_g, 's5_lam_re': d_lr[None], 's5_lam_im': d_li[None], 's5_log_dt': d_ldt.reshape(1, G),
             's5_b_re': d_br.reshape(bshape), 's5_b_im': d_bi.reshape(bshape), 's5_c_re': _blockdiag_out_t(dcr)[None],
             's5_c_im': _blockdiag_out_t(dci)[None], 's5_d': d_d, 's5_w_glu_a': d_wa.reshape(N_CHIPS, -1, D), 's5_w_glu_b': d_wb.reshape(N_CHIPS, -1, D)}
    return dout + dh, grads


def kernel(x, mem, positions, norm_mix, norm_xa, norm_mem, norm_ffn, xa_wq, xa_wk, xa_wv, xa_wo, xa_q_norm, xa_k_norm, ffn_w_up, ffn_conv_w, ffn_conv_b, ffn_w_down, hg_lb_logits, mix_w_in, hg_out_norm, mla_q_a_norm, mla_w_uq, mla_kv_a_norm, mla_w_ukv, mla_qn_nope, mla_qn_rope, mla_kn_nope, mla_kn_rope, mix_w_out, s5_lam_re, s5_lam_im, s5_log_dt, s5_b_re, s5_b_im, s5_c_re, s5_c_im, s5_d, s5_w_glu_a, s5_w_glu_b, loss_target, m_norm_mix, m_norm_xa, m_norm_mem, m_norm_ffn, m_xa_wq, m_xa_wk, m_xa_wv, m_xa_wo, m_xa_q_norm, m_xa_k_norm, m_ffn_w_up, m_ffn_conv_w, m_ffn_conv_b, m_ffn_w_down, m_hg_lb_logits, m_mix_w_in, m_hg_out_norm, m_mla_q_a_norm, m_mla_w_uq, m_mla_kv_a_norm, m_mla_w_ukv, m_mla_qn_nope, m_mla_qn_rope, m_mla_kn_nope, m_mla_kn_rope, m_mix_w_out, m_s5_lam_re, m_s5_lam_im, m_s5_log_dt, m_s5_b_re, m_s5_b_im, m_s5_c_re, m_s5_c_im, m_s5_d, m_s5_w_glu_a, m_s5_w_glu_b, v_norm_mix, v_norm_xa, v_norm_mem, v_norm_ffn, v_xa_wq, v_xa_wk, v_xa_wv, v_xa_wo, v_xa_q_norm, v_xa_k_norm, v_ffn_w_up, v_ffn_conv_w, v_ffn_conv_b, v_ffn_w_down, v_hg_lb_logits, v_mix_w_in, v_hg_out_norm, v_mla_q_a_norm, v_mla_w_uq, v_mla_kv_a_norm, v_mla_w_ukv, v_mla_qn_nope, v_mla_qn_rope, v_mla_kn_nope, v_mla_kn_rope, v_mix_w_out, v_s5_lam_re, v_s5_lam_im, v_s5_log_dt, v_s5_b_re, v_s5_b_im, v_s5_c_re, v_s5_c_im, v_s5_d, v_s5_w_glu_a, v_s5_w_glu_b):
    P = dict(locals())
    assert sorted(P) == sorted(INPUTS) and norm_mix.shape[0] == 2 and mix_w_in.shape[0] == 1
    x, mem, target = P['x'][0], P['mem'][0], P['loss_target'][0]
    L, D = x.shape
    tm = min(256, L)

    W = {n: P[n] for n in REPLICATED}
    W.update(_gather_weights(P))

    inv_freq = 1.0 / (ROPE_BASE ** (jnp.arange(0, MLA_ROPE, 2, dtype=F32) / MLA_ROPE))
    ang = P['positions'][0].astype(F32)[:, None] * inv_freq
    cos, sin, z = jnp.cos(ang), jnp.sin(ang), jnp.zeros_like(ang)
    cos_p = jnp.concatenate([cos, z, cos, z], axis=1)
    sin_p = jnp.concatenate([-sin, z, sin, z], axis=1)

    h, s_mix0 = _mixer0_fwd(x, W, cos_p, sin_p, tm)
    h, s_xa0 = _xattn_fwd(h, mem, W, 0, tm)
    h, s_ffn0 = _ffn_fwd(h, W, 0, tm)
    h, s_mix1 = _mixer1_fwd(h, W, tm)
    h, s_xa1 = _xattn_fwd(h, mem, W, 1, tm)
    h, s_ffn1 = _ffn_fwd(h, W, 1, tm)
    n = L // tm
    dh, parts = blocked_fwd(_loss_fn, [rows(h, tm), rows(target, tm)],
                            [_out((L, D), F32, tm), ((n * 8, 128), F32, (8, 128), lambda i: (i, 0))], n, "loss")
    loss = lax.psum(jnp.sum(parts), ("x", "y", "c"))

    layered = {}

    def collect(g, lyr):
        for k_, v_ in g.items():
            layered.setdefault(k_, {})[lyr] = v_

    dh, g = _ffn_bwd(dh, s_ffn1, W, 1, tm)
    collect(g, 1)
    dh, g = _xattn_bwd(dh, s_xa1, mem, W, 1, tm)
    collect(g, 1)
    dh, g = _mixer1_bwd(dh, s_mix1, W, tm)
    collect(g, 1)
    dh, g = _ffn_bwd(dh, s_ffn0, W, 0, tm)
    collect(g, 0)
    dh, g = _xattn_bwd(dh, s_xa0, mem, W, 0, tm)
    collect(g, 0)
    dx, g = _mixer0_bwd(dh, s_mix0, W, cos_p, sin_p, tm)
    collect(g, 0)

    GB, GS = {}, {}
    for name in WEIGHTS:
        by_layer = [layered[name][lyr] for lyr in sorted(layered[name])]
        if name in BIG:
            GB[name] = _halves_first(by_layer[0]) if len(by_layer) == 1 else jnp.stack(by_layer)
        else:
            full_shape = W[name].shape
            GS[name] = (by_layer[0].reshape(full_shape) if len(by_layer) == 1
                        else jnp.stack([g_.reshape(full_shape[1:]) for g_ in by_layer]))

    outs = _reduce_and_update(GB, GS, P)
    return (loss, dx[None], *[d[n_] for d in outs for n_ in WEIGHTS])
```

```python
import functools
import math

import jax
import jax.numpy as jnp
import numpy as np
from jax import lax
from jax.experimental import pallas as pl
from jax.experimental.pallas import tpu as pltpu

F32 = jnp.float32
BF16 = jnp.bfloat16
MXU_DTYPE = BF16
HI = lax.Precision.HIGHEST
V7X_VMEM_LIMIT_BYTES = 56 * 1024 * 1024
EPS = 1e-6
MESH = pl.DeviceIdType.MESH

HG_HEADS, HG_DIM = 4, 128
HG_WIDTH = HG_HEADS * HG_DIM
HG_SUB = 32
HG_BLOCK = 64
MLA_HEADS, MLA_Q_RANK, MLA_KV_RANK = 4, 256, 128
MLA_NOPE, MLA_ROPE, MLA_V = 128, 64, 128
MLA_QK = MLA_NOPE + MLA_ROPE
MLA_DK = 256
ROPE_BASE = 10000.0
IN_WIDTH = 4 * HG_WIDTH + MLA_Q_RANK + MLA_KV_RANK + MLA_ROPE
IN_PAD = 4 * HG_WIDTH + MLA_Q_RANK + MLA_KV_RANK + 128
S5_GROUP, S5_STATE = 16, 64
S5_GB = 8
DT_MIN, DT_MAX = 1e-3, 1e-1
XA_HEADS = 4
CONV_W = 3
ADAM_LR, ADAM_B1, ADAM_B2, ADAM_EPS, ADAM_WD, ADAM_STEP = 0.001, 0.9, 0.999, 1e-08, 0.01, 10


def _cparams(sem):
    return pltpu.CompilerParams(dimension_semantics=sem, vmem_limit_bytes=V7X_VMEM_LIMIT_BYTES)


class Opd:
    def __init__(self, arr, block, imap, grad=None, gshape=None, gimap=None):
        self.arr, self.block, self.imap, self.grad = arr, block, imap, grad
        self.gshape = arr.shape if gshape is None else gshape
        self.gimap = imap if gimap is None else gimap

    def spec(self):
        return pl.BlockSpec(self.block, self.imap)

    def gspec(self):
        return pl.BlockSpec(self.block, self.gimap)


def rows(arr, tm, grad=None, col=0, width=None):
    width = arr.shape[1] if width is None else width
    return Opd(arr, (tm, width), lambda i, c=col: (i, c), grad, (arr.shape[0], width), lambda i: (i, 0))


def cols(arr, tn, grad=None):
    return Opd(arr, (arr.shape[0], tn), lambda j: (0, j), grad)


def full(arr, grad=None):
    return Opd(arr, arr.shape, lambda i: (0, 0), grad)


def _load(ref):
    v = ref[...]
    return v.astype(F32) if jnp.issubdtype(v.dtype, jnp.floating) else v


def blocked_fwd(f, opds, outs, n, name):
    n_in = len(opds)

    def body(*refs):
        ys = f(*[_load(r) for r in refs[:n_in]])
        for r, y in zip(refs[n_in:], ys):
            r[...] = y.astype(r.dtype)

    res = pl.pallas_call(
        body, name=name, grid=(n,),
        in_specs=[o.spec() for o in opds],
        out_specs=[pl.BlockSpec(b, m) for (_, _, b, m) in outs],
        out_shape=[jax.ShapeDtypeStruct(s, d) for (s, d, _, _) in outs],
        compiler_params=_cparams(("parallel",)),
    )(*[o.arr for o in opds])
    return res


def blocked_bwd(f, opds, dys, n, name, plus=None):
    n_in, n_dy = len(opds), len(dys)
    diff = [i for i, o in enumerate(opds) if o.grad]
    extra = [] if plus is None else [plus]

    def body(*refs):
        vals = [_load(r) for r in refs[:n_in]]

        def fd(*dv):
            allv = list(vals)
            for i, v in zip(diff, dv):
                allv[i] = v
            return tuple(f(*allv))

        ys, vjp = jax.vjp(fd, *[vals[i] for i in diff])
        cts = tuple(_load(r).astype(y.dtype) for r, y in zip(refs[n_in:n_in + n_dy], ys))
        gs = list(vjp(cts))
        if extra:
            gs[0] = gs[0] + _load(refs[n_in + n_dy])
        for r, g, i in zip(refs[n_in + n_dy + len(extra):], gs, diff):
            if opds[i].grad == 'acc':
                @pl.when(pl.program_id(0) == 0)
                def _(r=r):
                    r[...] = jnp.zeros(r.shape, r.dtype)
                r[...] += g.astype(r.dtype)
            else:
                r[...] = g.astype(r.dtype)

    any_acc = any(opds[i].grad == 'acc' for i in diff)
    res = pl.pallas_call(
        body, name=name, grid=(n,),
        in_specs=[o.spec() for o in opds + dys + extra],
        out_specs=[opds[i].gspec() for i in diff],
        out_shape=[jax.ShapeDtypeStruct(opds[i].gshape, F32) for i in diff],
        compiler_params=_cparams(("arbitrary" if any_acc else "parallel",)),
    )(*[o.arr for o in opds + dys + extra])
    return res


def _tile(dim, want):
    for t in range(want - want % 16, 0, -16):
        if dim % t == 0:
            return t
    assert dim <= want, (dim, want)
    return dim


MATMUL_VMEM_BUDGET = 40 * 1024 * 1024
MATMUL_ROWS = 512


def _widest(N, fits):
    for t in range(N - N % 128, 0, -128):
        if N % t == 0 and fits(t):
            return t
    return N


def matmul(a, b, mode="nn", out_dtype=F32, add=None, name="matmul", col_blocks=None):
    sa, sb, so = a.dtype.itemsize, b.dtype.itemsize, jnp.dtype(out_dtype).itemsize
    has_add = add is not None
    if mode == "tn":
        (K, M), (K2, N) = a.shape, b.shape
        assert K == K2 and not has_add and out_dtype == F32, (a.shape, b.shape)
        tk = _tile(K, MATMUL_ROWS)
        tn = _widest(N, lambda t: 2 * (tk * M * sa + tk * t * sb + M * t * 4) <= MATMUL_VMEM_BUDGET)
        if col_blocks is not None:
            assert N % col_blocks == 0 and (N // col_blocks) % 128 == 0 and tn >= N // col_blocks, (N, col_blocks, tn)
            tn = N // col_blocks
            out_spec = pl.BlockSpec((None, M, tn), lambda j, k: (j, 0, 0))
            out_shape = jax.ShapeDtypeStruct((col_blocks, M, tn), F32)
        else:
            out_spec = pl.BlockSpec((M, tn), lambda j, k: (0, j))
            out_shape = jax.ShapeDtypeStruct((M, N), F32)

        def body(a_ref, b_ref, o_ref):
            r = lax.dot_general(a_ref[...].astype(MXU_DTYPE), b_ref[...].astype(MXU_DTYPE), ((_TN), ((), ())),
                                preferred_element_type=F32)

            @pl.when(pl.program_id(1) == 0)
            def _():
                o_ref[...] = r

            @pl.when(pl.program_id(1) > 0)
            def _():
                o_ref[...] += r

        return pl.pallas_call(
            body, name=name, grid=(N // tn, K // tk),
            in_specs=[pl.BlockSpec((tk, M), lambda j, k: (k, 0)), pl.BlockSpec((tk, tn), lambda j, k: (k, j))],
            out_specs=out_spec, out_shape=out_shape,
            compiler_params=_cparams(("parallel", "arbitrary")),
        )(a, b)

    (M, K) = a.shape
    N = b.shape[1] if mode == "nn" else b.shape[0]
    assert K == (b.shape[0] if mode == "nn" else b.shape[1]), (a.shape, b.shape, mode)
    tm = _tile(M, MATMUL_ROWS)
    tn = _widest(N, lambda t: 2 * (tm * K * sa + K * t * sb + tm * t * (so + 4 * has_add)) <= MATMUL_VMEM_BUDGET)
    dims = ((_NN if mode == "nn" else _NT), ((), ()))

    def body(*refs):
        r = lax.dot_general(refs[0][...].astype(MXU_DTYPE), refs[1][...].astype(MXU_DTYPE), dims, preferred_element_type=F32)
        if has_add:
            r = r + refs[2][...].astype(F32)
        refs[-1][...] = r.astype(refs[-1].dtype)

    b_spec = pl.BlockSpec((K, tn), lambda j, i: (0, j)) if mode == "nn" else pl.BlockSpec((tn, K), lambda j, i: (j, 0))
    in_specs = [pl.BlockSpec((tm, K), lambda j, i: (i, 0)), b_spec]
    args = [a, b]
    if has_add:
        in_specs.append(pl.BlockSpec((tm, tn), lambda j, i: (i, j)))
        args.append(add)
    return pl.pallas_call(
        body, name=name, grid=(N // tn, M // tm),
        in_specs=in_specs,
        out_specs=pl.BlockSpec((tm, tn), lambda j, i: (i, j)),
        out_shape=jax.ShapeDtypeStruct((M, N), out_dtype),
        compiler_params=_cparams(("parallel", "parallel")),
    )(*args)


def _dot(a, b, dims, precision=None):
    if precision is None:
        a, b = a.astype(MXU_DTYPE), b.astype(MXU_DTYPE)
    return lax.dot_general(a, b, (dims, ((), ())), precision=precision, preferred_element_type=F32)


_NN = ((1,), (0,))
_NT = ((1,), (1,))
_TN = ((0,), (0,))


def _rms(x, gain):
    return x * lax.rsqrt(jnp.mean(x * x, axis=-1, keepdims=True) + EPS) * gain


def _hg_block(st_t, q, fl, iv, g, lb, gain):
    row = lax.broadcasted_iota(jnp.int32, (HG_SUB, HG_SUB), 0)
    col = lax.broadcasted_iota(jnp.int32, (HG_SUB, HG_SUB), 1)
    tri = (row >= col).astype(F32)
    outs, states = [], []
    for h in range(HG_HEADS):
        sl = slice(h * HG_DIM, (h + 1) * HG_DIM)
        st = st_t[h * HG_DIM:(h + 1) * HG_DIM, :]
        lbh = lb[:, sl]
        fg = lbh + (1.0 - lbh) * jax.nn.sigmoid(fl[:, sl])
        lf, kk, qf, v = jnp.log(fg), 1.0 - fg, jax.nn.silu(q[:, sl]), iv[:, sl]
        parts = []
        for s in range(q.shape[0] // HG_SUB):
            r = slice(s * HG_SUB, (s + 1) * HG_SUB)
            b = _dot(tri, lf[r], _NN, HI)
            b_mid = jnp.sum(lf[r][:HG_SUB // 2], axis=0, keepdims=True)
            b_end = jnp.sum(lf[r], axis=0, keepdims=True)
            sc = _dot(qf[r] * jnp.exp(b - b_mid), kk[r] * jnp.exp(b_mid - b), _NT) * tri
            parts.append(_dot(sc, v[r], _NN) + _dot(qf[r] * jnp.exp(b), st, _NT))
            st = st * jnp.exp(b_end) + _dot(v[r], kk[r] * jnp.exp(b_end - b), _TN)
        o = jnp.concatenate(parts, axis=0)
        outs.append(_rms(o, gain[:, sl]) * jax.nn.silu(g[:, sl]))
        states.append(st)
    return jnp.concatenate(states, axis=0), jnp.concatenate(outs, axis=1)


def _hg_specs(proj, nb):
    return [pl.BlockSpec((HG_BLOCK, HG_WIDTH), lambda i, c=c, f=nb: (f(i), c)) for c in range(4)]


def hgrn2_fwd(proj, lb, gain):
    L = proj.shape[0]
    n = L // HG_BLOCK

    def body(q, fl, iv, g, lb_r, gain_r, o_ref, st_ref, st):
        @pl.when(pl.program_id(0) == 0)
        def _():
            st[...] = jnp.zeros(st.shape, F32)

        st_ref[0] = st[...]
        new, o = _hg_block(st[...], q[...], fl[...], iv[...], g[...], lb_r[...], gain_r[...])
        st[...] = new
        o_ref[...] = o.astype(o_ref.dtype)

    pspec = pl.BlockSpec((1, HG_WIDTH), lambda i: (0, 0))
    return pl.pallas_call(
        body, name="hgrn2_fwd", grid=(n,),
        in_specs=_hg_specs(proj, lambda i: i) + [pspec, pspec],
        out_specs=[pl.BlockSpec((HG_BLOCK, HG_WIDTH), lambda i: (i, 0)),
                   pl.BlockSpec((1, HG_WIDTH, HG_DIM), lambda i: (i, 0, 0))],
        out_shape=[jax.ShapeDtypeStruct((L, HG_WIDTH), MXU_DTYPE),
                   jax.ShapeDtypeStruct((n, HG_WIDTH, HG_DIM), F32)],
        scratch_shapes=[pltpu.VMEM((HG_WIDTH, HG_DIM), F32)],
        compiler_params=_cparams(("arbitrary",)),
    )(proj, proj, proj, proj, lb, gain)


def hgrn2_bwd(proj, lb, gain, states, do):
    L = proj.shape[0]
    n = L // HG_BLOCK

    def body(q, fl, iv, g, lb_r, gain_r, st_r, do_r, dproj, dlb, dgain, dst):
        @pl.when(pl.program_id(0) == 0)
        def _():
            dst[...] = jnp.zeros(dst.shape, F32)
            dlb[...] = jnp.zeros(dlb.shape, F32)
            dgain[...] = jnp.zeros(dgain.shape, F32)

        _, vjp = jax.vjp(_hg_block, st_r[0], q[...], fl[...], iv[...], g[...], lb_r[...], gain_r[...])
        d_st, dq, dfl, div, dg, d_lb, d_gain = vjp((dst[...], do_r[...].astype(F32)))
        dst[...] = d_st
        dproj[:, 0 * HG_WIDTH:1 * HG_WIDTH] = dq
        dproj[:, 1 * HG_WIDTH:2 * HG_WIDTH] = dfl
        dproj[:, 2 * HG_WIDTH:3 * HG_WIDTH] = div
        dproj[:, 3 * HG_WIDTH:4 * HG_WIDTH] = dg
        dlb[...] += d_lb
        dgain[...] += d_gain

    rev = lambda i: n - 1 - i
    pspec = pl.BlockSpec((1, HG_WIDTH), lambda i: (0, 0))
    return pl.pallas_call(
        body, name="hgrn2_bwd", grid=(n,),
        in_specs=_hg_specs(proj, rev) + [pspec, pspec,
                                         pl.BlockSpec((1, HG_WIDTH, HG_DIM), lambda i: (rev(i), 0, 0)),
                                         pl.BlockSpec((HG_BLOCK, HG_WIDTH), lambda i: (rev(i), 0))],
        out_specs=[pl.BlockSpec((HG_BLOCK, 4 * HG_WIDTH), lambda i: (rev(i), 0)), pspec, pspec],
        out_shape=[jax.ShapeDtypeStruct((L, 4 * HG_WIDTH), F32),
                   jax.ShapeDtypeStruct((1, HG_WIDTH), F32), jax.ShapeDtypeStruct((1, HG_WIDTH), F32)],
        scratch_shapes=[pltpu.VMEM((HG_WIDTH, HG_DIM), F32)],
        compiler_params=_cparams(("arbitrary",)),
    )(proj, proj, proj, proj, lb, gain, states, do)


def _rope_rms(x, gain_p, cos_p, sin_p):
    n = x * lax.rsqrt(jnp.sum(x * x, axis=-1, keepdims=True) * (1.0 / MLA_ROPE) + EPS) * gain_p
    r = lax.broadcasted_iota(jnp.int32, (128, 128), 0)
    c = lax.broadcasted_iota(jnp.int32, (128, 128), 1)
    swap = (r == (c + 64) % 128).astype(F32)
    return n * cos_p + _dot(n, swap, _NN, HI) * sin_p


def _mla_prep(c_q, c_kv, kpe, cos_p, sin_p, q_a, w_uq, kv_a, w_ukv, qn_nope, qn_rope, kn_nope, kn_rope):
    q = _dot(_rms(c_q, q_a), w_uq, _NN)
    kv = _dot(_rms(c_kv, kv_a), w_ukv, _NN)
    k_pe = _rope_rms(kpe, kn_rope, cos_p, sin_p)
    qs, ks = [], []
    for h in range(MLA_HEADS):
        qs.append(_rms(q[:, h * MLA_DK:h * MLA_DK + MLA_NOPE], qn_nope))
        qs.append(_rope_rms(q[:, h * MLA_DK + MLA_NOPE:(h + 1) * MLA_DK], qn_rope, cos_p, sin_p))
        ks.append(_rms(kv[:, h * MLA_NOPE:(h + 1) * MLA_NOPE], kn_nope))
        ks.append(k_pe)
    return jnp.concatenate(qs, axis=1), jnp.concatenate(ks, axis=1), kv[:, MLA_HEADS * MLA_NOPE:]


def _mla_prep_opds(proj, cos_p, sin_p, params, tm, grads):
    g = (lambda k: k) if grads else (lambda k: None)
    c0 = 4 * HG_WIDTH
    return ([rows(proj, tm, g('blk'), col=c0 // MLA_Q_RANK, width=MLA_Q_RANK),
             rows(proj, tm, g('blk'), col=(c0 + MLA_Q_RANK) // 128, width=128),
             rows(proj, tm, g('blk'), col=(c0 + MLA_Q_RANK) // 128 + 1, width=128),
             rows(cos_p, tm), rows(sin_p, tm)] + [full(p, g('acc')) for p in params])


def mla_prep_fwd(proj, cos_p, sin_p, params, tm):
    L = proj.shape[0]
    W = MLA_HEADS * MLA_DK
    rb = lambda w: (tm, w)
    outs = [((L, W), MXU_DTYPE, rb(W), lambda i: (i, 0)), ((L, W), MXU_DTYPE, rb(W), lambda i: (i, 0)),
            ((L, MLA_HEADS * MLA_V), MXU_DTYPE, rb(MLA_HEADS * MLA_V), lambda i: (i, 0))]
    return blocked_fwd(_mla_prep, _mla_prep_opds(proj, cos_p, sin_p, params, tm, False), outs, L // tm, "mla_prep_fwd")


def mla_prep_bwd(proj, cos_p, sin_p, params, dq, dk, dv, tm):
    L = proj.shape[0]
    return blocked_bwd(_mla_prep, _mla_prep_opds(proj, cos_p, sin_p, params, tm, True),
                       [rows(dq, tm), rows(dk, tm), rows(dv, tm)], L // tm, "mla_prep_bwd")


def _scores(q, k, scale, diagonal):
    s = _dot(q, k, _NT) * scale
    if not diagonal:
        return s
    row = lax.broadcasted_iota(jnp.int32, s.shape, 0)
    col = lax.broadcasted_iota(jnp.int32, s.shape, 1)
    return jnp.where(col <= row, s, -jnp.inf)


def attn_fwd(q, k, v, scale, t):
    L = q.shape[0]
    n = L // t

    def body(q_ref, k_ref, v_ref, o_ref, lse_ref):
        i = pl.program_id(1)
        qb = q_ref[...]

        def step(j, carry, diagonal):
            m, l, acc = carry
            kj = k_ref[pl.ds(pl.multiple_of(j * t, t), t), :]
            vj = v_ref[pl.ds(pl.multiple_of(j * t, t), t), :]
            s = _scores(qb, kj, scale, diagonal)
            m_new = jnp.maximum(m, jnp.max(s, axis=-1, keepdims=True))
            p = jnp.exp(s - m_new)
            alpha = jnp.exp(m - m_new)
            return m_new, alpha * l + jnp.sum(p, axis=-1, keepdims=True), alpha * acc + _dot(p, vj, _NN)

        init = (jnp.full((t, 1), -jnp.inf, F32), jnp.zeros((t, 1), F32), jnp.zeros((t, MLA_V), F32))
        m, l, acc = step(i, lax.fori_loop(0, i, lambda j, c: step(j, c, False), init), True)
        o_ref[...] = acc / l
        lse_ref[...] = jnp.broadcast_to(m + jnp.log(l), lse_ref.shape)

    hspec = lambda rows_, w: pl.BlockSpec((rows_, w), lambda h, i: (0, h))
    bspec = lambda w: pl.BlockSpec((t, w), lambda h, i: (i, h))
    return pl.pallas_call(
        body, name="attn_fwd", grid=(MLA_HEADS, n),
        in_specs=[bspec(MLA_DK), hspec(L, MLA_DK), hspec(L, MLA_V)],
        out_specs=[bspec(MLA_V), bspec(MLA_V)],
        out_shape=[jax.ShapeDtypeStruct((L, MLA_HEADS * MLA_V), F32)] * 2,
        compiler_params=_cparams(("parallel", "parallel")),
    )(q, k, v)


def attn_bwd_dq(q, k, v, o, lse, do, scale, t):
    L = q.shape[0]
    n = L // t

    def body(q_ref, k_ref, v_ref, o_ref, lse_ref, do_ref, dq_ref):
        i = pl.program_id(1)
        qb, dob = q_ref[...], do_ref[...]
        delta = jnp.sum(dob * o_ref[...], axis=-1, keepdims=True)
        lse_c = jnp.max(lse_ref[...], axis=-1, keepdims=True)

        def step(j, dq, diagonal):
            kj = k_ref[pl.ds(pl.multiple_of(j * t, t), t), :]
            vj = v_ref[pl.ds(pl.multiple_of(j * t, t), t), :]
            p = jnp.exp(_scores(qb, kj, scale, diagonal) - lse_c)
            ds = p * (_dot(dob, vj, _NT) - delta) * scale
            return dq + _dot(ds, kj, _NN)

        dq = lax.fori_loop(0, i, lambda j, c: step(j, c, False), jnp.zeros((t, MLA_DK), F32))
        dq_ref[...] = step(i, dq, True)

    hspec = lambda w: pl.BlockSpec((L, w), lambda h, i: (0, h))
    bspec = lambda w: pl.BlockSpec((t, w), lambda h, i: (i, h))
    return pl.pallas_call(
        body, name="attn_bwd_dq", grid=(MLA_HEADS, n),
        in_specs=[bspec(MLA_DK), hspec(MLA_DK), hspec(MLA_V), bspec(MLA_V), bspec(MLA_V), bspec(MLA_V)],
        out_specs=bspec(MLA_DK),
        out_shape=jax.ShapeDtypeStruct((L, MLA_HEADS * MLA_DK), F32),
        compiler_params=_cparams(("parallel", "parallel")),
    )(q, k, v, o, lse, do)


def attn_bwd_dkv(q, k, v, o, lse, do, scale, t):
    L = q.shape[0]
    n = L // t

    def body(q_ref, k_ref, v_ref, o_ref, lse_ref, do_ref, dk_ref, dv_ref):
        j = pl.program_id(1)
        kb, vb = k_ref[...], v_ref[...]

        def step(i, carry, diagonal):
            dk, dv = carry
            r = pl.ds(pl.multiple_of(i * t, t), t)
            qi, doi = q_ref[r, :], do_ref[r, :]
            delta = jnp.sum(doi * o_ref[r, :], axis=-1, keepdims=True)
            lse_c = jnp.max(lse_ref[r, :], axis=-1, keepdims=True)
            p = jnp.exp(_scores(qi, kb, scale, diagonal) - lse_c)
            ds = p * (_dot(doi, vb, _NT) - delta) * scale
            return dk + _dot(ds, qi, _TN), dv + _dot(p, doi, _TN)

        first = step(j, (jnp.zeros((t, MLA_DK), F32), jnp.zeros((t, MLA_V), F32)), True)
        dk, dv = lax.fori_loop(j + 1, n, lambda i, c: step(i, c, False), first)
        dk_ref[...] = dk
        dv_ref[...] = dv

    hspec = lambda w: pl.BlockSpec((L, w), lambda h, j: (0, h))
    bspec = lambda w: pl.BlockSpec((t, w), lambda h, j: (j, h))
    return pl.pallas_call(
        body, name="attn_bwd_dkv", grid=(MLA_HEADS, n),
        in_specs=[hspec(MLA_DK), bspec(MLA_DK), bspec(MLA_V), hspec(MLA_V), hspec(MLA_V), hspec(MLA_V)],
        out_specs=[bspec(MLA_DK), bspec(MLA_V)],
        out_shape=[jax.ShapeDtypeStruct((L, MLA_HEADS * MLA_DK), F32), jax.ShapeDtypeStruct((L, MLA_HEADS * MLA_V), F32)],
        compiler_params=_cparams(("parallel", "parallel")),
    )(q, k, v, o, lse, do)


S5_LANES = S5_GB * S5_STATE


def _cmul(ar, ai, br, bi):
    return ar * br - ai * bi, ar * bi + ai * br


def _a_powers(ar, ai, reverse):
    a2 = _cmul(ar, ai, ar, ai)
    a4 = _cmul(*a2, *a2)
    row = lax.broadcasted_iota(jnp.int32, (8, ar.shape[1]), 0)
    e = (8 - row) if reverse else (row + 1)
    tr, ti = jnp.ones((8, ar.shape[1]), F32), jnp.zeros((8, ar.shape[1]), F32)
    for bit, (pr, pi) in ((1, (ar, ai)), (2, a2), (4, a4), (8, _cmul(*a4, *a4))):
        nr, ni = _cmul(tr, ti, pr, pi)
        sel = (e & bit) != 0
        tr, ti = jnp.where(sel, nr, tr), jnp.where(sel, ni, ti)
    pows = []
    for d, (pr, pi) in zip((1, 2, 4), ((ar, ai), a2, a4)):
        keep = (row < 8 - d) if reverse else (row >= d)
        pows.append((jnp.where(keep, pr, 0.0), jnp.where(keep, pi, 0.0)))
    return pows, (tr, ti)


def _scan8(xr, xi, pows, table, cr, ci, reverse):
    for d, (pr, pi) in zip((1, 2, 4), pows):
        shift = 8 - d if reverse else d
        mr, mi = _cmul(pr, pi, pltpu.roll(xr, shift, 0), pltpu.roll(xi, shift, 0))
        xr, xi = xr + mr, xi + mi
    mr, mi = _cmul(table[0], table[1], cr, ci)
    return xr + mr, xi + mi


def _row_of(x, r):
    row = lax.broadcasted_iota(jnp.int32, x.shape, 0)
    return jnp.sum(jnp.where(row == r, x, 0.0), axis=0, keepdims=True)


def _s5_scan_fwd(h_re, h_im, ar, ai, L):
    pows, table = _a_powers(ar, ai, False)

    def step(i, carry):
        r = pl.ds(pl.multiple_of(i * 8, 8), 8)
        xr, xi = _scan8(h_re[r, :], h_im[r, :], pows, table, carry[0], carry[1], False)
        h_re[r, :] = xr
        h_im[r, :] = xi
        return xr[7:8, :], xi[7:8, :]

    z = jnp.zeros((1, ar.shape[1]), F32)
    lax.fori_loop(0, L // 8, step, (z, z))


def _s5_specs(L):
    return [pl.BlockSpec((L, 128), lambda g: (0, g)),
            pl.BlockSpec((1, 128, S5_LANES), lambda g: (g, 0, 0)), pl.BlockSpec((1, 128, S5_LANES), lambda g: (g, 0, 0)),
            pl.BlockSpec((1, 1, S5_LANES), lambda g: (g, 0, 0)), pl.BlockSpec((1, 1, S5_LANES), lambda g: (g, 0, 0)),
            pl.BlockSpec((1, S5_LANES, 128), lambda g: (g, 0, 0)), pl.BlockSpec((1, S5_LANES, 128), lambda g: (g, 0, 0))]


def s5_fwd(u, w_re, w_im, a_re, a_im, c_re, c_im):
    L, D = u.shape

    def body(u_ref, wr, wi, ar, ai, cr, ci, y_ref, h_re, h_im):
        ub = u_ref[...]
        h_re[...] = _dot(ub, wr[0], _NN)
        h_im[...] = _dot(ub, wi[0], _NN)
        _s5_scan_fwd(h_re, h_im, ar[0], ai[0], L)
        y_ref[...] = _dot(h_re[...], cr[0], _NN) - _dot(h_im[...], ci[0], _NN)

    return pl.pallas_call(
        body, name="s5_fwd", grid=(D // 128,),
        in_specs=_s5_specs(L), out_specs=pl.BlockSpec((L, 128), lambda g: (0, g)),
        out_shape=jax.ShapeDtypeStruct((L, D), F32),
        scratch_shapes=[pltpu.VMEM((L, S5_LANES), F32), pltpu.VMEM((L, S5_LANES), F32)],
        compiler_params=_cparams(("parallel",)),
    )(u, w_re, w_im, a_re, a_im, c_re, c_im)


def s5_bwd(u, w_re, w_im, a_re, a_im, c_re, c_im, dy, tc):
    L, D = u.shape
    nch = L // tc

    def body(u_ref, wr, wi, ar_ref, ai_ref, cr, ci, dy_ref, du_ref, dwr, dwi, dar, dai, dcr, dci, h_re, h_im, g_re, g_im):
        ar, ai = ar_ref[0], ai_ref[0]
        ub = u_ref[...]
        h_re[...] = _dot(ub, wr[0], _NN)
        h_im[...] = _dot(ub, wi[0], _NN)
        _s5_scan_fwd(h_re, h_im, ar, ai, L)
        dyb = dy_ref[...]
        dcr[0] = _dot(h_re[...], dyb, _TN)
        dci[0] = -_dot(h_im[...], dyb, _TN)
        pows, table = _a_powers(ar, -ai, True)
        dwr[0] = jnp.zeros((128, S5_LANES), F32)
        dwi[0] = jnp.zeros((128, S5_LANES), F32)
        z1 = jnp.zeros((1, S5_LANES), F32)
        z8 = jnp.zeros((8, S5_LANES), F32)

        def chunk(cc, carry):
            c0 = pl.multiple_of((nch - 1 - cc) * tc, tc)
            rows_c = pl.ds(c0, tc)
            dyc = dy_ref[rows_c, :]
            g_re[...] = _dot(dyc, cr[0], _NT)
            g_im[...] = -_dot(dyc, ci[0], _NT)

            def step(ii, cy):
                gr_c, gi_c, acc_r, acc_i = cy
                i8 = pl.multiple_of((tc // 8 - 1 - ii) * 8, 8)
                rl = pl.ds(i8, 8)
                xr, xi = _scan8(g_re[rl, :], g_im[rl, :], pows, table, gr_c, gi_c, True)
                g_re[rl, :] = xr
                g_im[rl, :] = xi
                t0 = c0 + i8
                hb_r, hb_i = h_re[pl.ds(t0, 8), :], h_im[pl.ds(t0, 8), :]
                tp = pl.multiple_of(jnp.maximum(t0 - 8, 0), 8)
                first = (t0 > 0).astype(F32)
                pr = h_re[pl.ds(tp, 8), :][7:8, :] * first
                pi = h_im[pl.ds(tp, 8), :][7:8, :] * first
                row = lax.broadcasted_iota(jnp.int32, xr.shape, 0)
                hp_r = jnp.where(row == 0, pr, pltpu.roll(hb_r, 1, 0))
                hp_i = jnp.where(row == 0, pi, pltpu.roll(hb_i, 1, 0))
                return (xr[0:1, :], xi[0:1, :],
                        acc_r + xr * hp_r + xi * hp_i, acc_i + xi * hp_r - xr * hp_i)

            cy = lax.fori_loop(0, tc // 8, step, carry)
            uc = u_ref[rows_c, :]
            gr, gi = g_re[...], g_im[...]
            du_ref[rows_c, :] = _dot(gr, wr[0], _NT) + _dot(gi, wi[0], _NT)
            dwr[0] += _dot(uc, gr, _TN)
            dwi[0] += _dot(uc, gi, _TN)
            return cy

        _, _, acc_r, acc_i = lax.fori_loop(0, nch, chunk, (z1, z1, z8, z8))
        dar[0] = jnp.sum(acc_r, axis=0, keepdims=True)
        dai[0] = jnp.sum(acc_i, axis=0, keepdims=True)

    specs = _s5_specs(L)
    return pl.pallas_call(
        body, name="s5_bwd", grid=(D // 128,),
        in_specs=specs + [pl.BlockSpec((L, 128), lambda g: (0, g))],
        out_specs=[pl.BlockSpec((L, 128), lambda g: (0, g))] + specs[1:],
        out_shape=[jax.ShapeDtypeStruct((L, D), F32)] + [jax.ShapeDtypeStruct(x.shape, F32)
                                                        for x in (w_re, w_im, a_re, a_im, c_re, c_im)],
        scratch_shapes=[pltpu.VMEM((L, S5_LANES), F32), pltpu.VMEM((L, S5_LANES), F32),
                        pltpu.VMEM((tc, S5_LANES), F32), pltpu.VMEM((tc, S5_LANES), F32)],
        compiler_params=_cparams(("parallel",)),
    )(u, w_re, w_im, a_re, a_im, c_re, c_im, dy)


def _s5_discretize(lr, li, ldt, br, bi):
    dt = jnp.exp(ldt)
    mag = jnp.exp(lr * dt)
    ar, ai = mag * jnp.cos(li * dt), mag * jnp.sin(li * dt)
    den = lr * lr + li * li
    zr = ((ar - 1.0) * lr + ai * li) / den
    zi = (ai * lr - (ar - 1.0) * li) / den
    p = lax.broadcasted_iota(jnp.int32, (S5_STATE, S5_STATE * S5_GROUP), 0)
    c = lax.broadcasted_iota(jnp.int32, (S5_STATE, S5_STATE * S5_GROUP), 1)
    rep = (c // S5_GROUP == p).astype(F32)
    zr, zi = _dot(zr, rep, _NN, HI), _dot(zi, rep, _NN, HI)
    return ar, ai, zr * br - zi * bi, zr * bi + zi * br


def _conv_shift(x, d):
    row = lax.broadcasted_iota(jnp.int32, x.shape, 0)
    return jnp.where(row >= d, pltpu.roll(x, d, 0), 0.0)


def _conv_unshift(x, d):
    n = x.shape[0]
    row = lax.broadcasted_iota(jnp.int32, x.shape, 0)
    return jnp.where(row < n - d, pltpu.roll(x, n - d, 0), 0.0)


@functools.partial(jax.custom_vjp, nondiff_argnums=(1,))
def _shift_rows(x, d):
    return _conv_shift(x, d)


_shift_rows.defvjp(lambda x, d: (_conv_shift(x, d), None), lambda d, _, g: (_conv_unshift(g, d),))


def _conv_gate(ug, uv, wg0, wg1, wg2, wv0, wv1, wv2, bg, bv):
    def conv(u, w0, w1, w2, b):
        return u * w2 + _shift_rows(u, 1) * w1 + _shift_rows(u, 2) * w0 + b
    return (jax.nn.silu(conv(ug, wg0, wg1, wg2, bg)) * conv(uv, wv0, wv1, wv2, bv),)


def _rms_fn(x, gain):
    return (_rms(x, gain),)


def _softmax_rows(s):
    e = jnp.exp(s - lax.stop_gradient(jnp.max(s, axis=-1, keepdims=True)))
    return e / jnp.sum(e, axis=-1, keepdims=True)


def _xa_core(qp, k, v, q_gain):
    dh = qp.shape[1] // XA_HEADS
    outs = []
    for h in range(XA_HEADS):
        sl = slice(h * dh, (h + 1) * dh)
        p = _softmax_rows(_dot(_rms(qp[:, sl], q_gain), k[:, sl], _NT) * (dh ** -0.5))
        outs.append(_dot(p, v[:, sl], _NN))
    return (jnp.concatenate(outs, axis=1),)


def _mem_kv(mem, mem_gain, wk, wv, k_gain):
    m = _rms(mem, mem_gain)
    kp = _dot(m, wk, _NN)
    dh = kp.shape[1] // XA_HEADS
    k = jnp.concatenate([_rms(kp[:, h * dh:(h + 1) * dh], k_gain) for h in range(XA_HEADS)], axis=1)
    return k, _dot(m, wv, _NN)


def _s5_post(y, u, d):
    return (jax.nn.gelu(y + d * u),)


def _glu(a, b):
    return (a * jax.nn.sigmoid(b),)


def _lb_first(logits):
    e = jnp.exp(logits - lax.stop_gradient(jnp.max(logits, axis=0, keepdims=True)))
    return (_row_of(e, 0) / jnp.sum(e, axis=0, keepdims=True),)


def _loss_fn(y, t):
    e = y - t
    part = 0.5 * jnp.sum(e * e) / y.shape[1]
    return e * (1.0 / y.shape[1]), jnp.full((8, 128), part / (8 * 128), F32)


def _out(shape, dtype, tm):
    return (shape, dtype, (tm, shape[1]), lambda i: (i, 0))


def rms_fwd(h, gain, tm, dtype):
    return blocked_fwd(_rms_fn, [rows(h, tm), full(gain)], [_out(h.shape, dtype, tm)], h.shape[0] // tm, "rms_fwd")[0]


def rms_bwd(h, gain, dy, tm, residual):
    return blocked_bwd(_rms_fn, [rows(h, tm, 'blk'), full(gain, 'acc')], [rows(dy, tm)], h.shape[0] // tm, "rms_bwd",
                       plus=rows(residual, tm))


def adamw(w, g, m, v, name):
    R = w.shape[0]
    tm = _tile(R, 256)
    assert g.shape == w.shape == m.shape == v.shape, (name, w.shape, g.shape)

    def body(w_ref, g_ref, m_ref, v_ref, d_ref, nm_ref, nv_ref):
        g_ = g_ref[...]
        m_ = ADAM_B1 * m_ref[...] + (1.0 - ADAM_B1) * g_
        v_ = ADAM_B2 * v_ref[...] + (1.0 - ADAM_B2) * jnp.square(g_)
        m_hat = m_ / (1.0 - ADAM_B1 ** ADAM_STEP)
        v_hat = v_ / (1.0 - ADAM_B2 ** ADAM_STEP)
        d_ref[...] = -ADAM_LR * (m_hat / (jnp.sqrt(v_hat) + ADAM_EPS) + ADAM_WD * w_ref[...])
        nm_ref[...] = m_
        nv_ref[...] = v_

    spec = pl.BlockSpec((tm, w.shape[1]), lambda i: (i, 0))
    return pl.pallas_call(
        body, name=name, grid=(R // tm,), in_specs=[spec] * 4, out_specs=[spec] * 3,
        out_shape=[jax.ShapeDtypeStruct(w.shape, F32)] * 3, compiler_params=_cparams(("parallel",)),
    )(w, g, m, v)


def add2(x, y, name, out_dtype=F32):
    shape = x.shape
    x, y = x.reshape(-1, shape[-1]), y.reshape(-1, shape[-1])
    R, C = x.shape
    tm = _tile(R, 256)

    def body(x_ref, y_ref, o_ref):
        o_ref[...] = (x_ref[...] + y_ref[...]).astype(o_ref.dtype)

    spec = pl.BlockSpec((tm, C), lambda i: (i, 0))
    return pl.pallas_call(
        body, name=name, grid=(R // tm,), in_specs=[spec, spec], out_specs=spec,
        out_shape=jax.ShapeDtypeStruct((R, C), out_dtype), compiler_params=_cparams(("parallel",)),
    )(x, y).reshape(shape)


def add_chips(own, got, name):
    n, R, C = got.shape
    tm = _tile(R, 256)

    def body(*refs):
        acc = refs[0][...].astype(F32)
        for r in refs[1:-1]:
            acc = acc + r[...].astype(F32)
        refs[-1][...] = acc

    return pl.pallas_call(
        body, name=name, grid=(R // tm,),
        in_specs=[pl.BlockSpec((tm, C), lambda i: (i, 0))] + [pl.BlockSpec((None, tm, C), lambda i, j=j: (j, i, 0))
                                                            for j in range(n)],
        out_specs=pl.BlockSpec((tm, C), lambda i: (i, 0)),
        out_shape=jax.ShapeDtypeStruct((R, C), F32), compiler_params=_cparams(("parallel",)),
    )(own, *([got] * n))


_HBM = pl.BlockSpec(memory_space=pltpu.HBM)
N_CHIPS = 4


def _my_place():
    return lax.axis_index("x"), lax.axis_index("y"), lax.axis_index("c")


def _window(ref, axis, start, size):
    idx = [slice(None)] * len(ref.shape)
    idx[axis] = pl.ds(start, size)
    return ref.at[tuple(idx)]


def _comm_call(body, name, xs, out_shapes, n_remote, n_local):
    return pl.pallas_call(
        body, name=name, in_specs=[_HBM] * len(xs), out_specs=[_HBM] * len(out_shapes), out_shape=out_shapes,
        scratch_shapes=[pltpu.SemaphoreType.DMA((n_remote,)), pltpu.SemaphoreType.DMA((n_remote,)),
                        pltpu.SemaphoreType.DMA((max(n_local, 1),))],
        compiler_params=pltpu.CompilerParams(has_side_effects=True),
    )(*xs)


def _run(copies):
    for cp in copies:
        cp.start()
    for cp in copies:
        cp.wait()


def _other_chips(mx, my):
    return [(mx ^ (j >> 1), my ^ (j & 1)) for j in (1, 2, 3)]


def chip_gather(xs, axes, name):
    n = len(xs)
    shapes, final = [], []
    for x, ax in zip(xs, axes):
        s = list(x.shape)
        if ax is None:
            shapes.append([N_CHIPS] + s)
            final.append(shapes[-1])
        elif ax < x.ndim - 1:
            shapes.append(s[:ax] + [N_CHIPS] + s[ax:])
            final.append(s[:ax] + [N_CHIPS * s[ax]] + s[ax + 1:])
        else:
            assert s[ax] % 128 == 0, (name, s)
            shapes.append(s[:ax] + [N_CHIPS * s[ax]])
            final.append(shapes[-1])

    def body(*refs):
        x_refs, o_refs = refs[:n], refs[n:2 * n]
        send_sems, recv_sems, local_sems = refs[2 * n:]
        mx, my, mc = _my_place()
        q = 2 * mx + my
        copies = []
        for i, (x_ref, o_ref, ax) in enumerate(zip(x_refs, o_refs, axes)):
            if ax is None or ax < len(x_ref.shape) - 1:
                dst = o_ref.at[(slice(None),) * (ax or 0) + (q,)]
            else:
                dst = _window(o_ref, ax, q * x_ref.shape[ax], x_ref.shape[ax])
            copies.append(pltpu.make_async_copy(x_ref, dst, local_sems.at[i]))
            for j, (tx, ty) in enumerate(_other_chips(mx, my)):
                copies.append(pltpu.make_async_remote_copy(
                    src_ref=x_ref, dst_ref=dst, send_sem=send_sems.at[3 * i + j], recv_sem=recv_sems.at[3 * i + j],
                    device_id=(tx, ty, mc), device_id_type=MESH))
        _run(copies)

    out_shapes = [jax.ShapeDtypeStruct(tuple(s), x.dtype) for s, x in zip(shapes, xs)]
    return [o.reshape(f) for o, f in zip(_comm_call(body, name, xs, out_shapes, 3 * n, n), final)]


def pair_swap(xs, name, halves):
    n = len(xs)
    shapes = [jax.ShapeDtypeStruct(x.shape[1:] if halves else x.shape, x.dtype) for x in xs]

    def body(*refs):
        x_refs, o_refs = refs[:n], refs[n:2 * n]
        send_sems, recv_sems, _ = refs[2 * n:]
        mx, my, mc = _my_place()
        _run([pltpu.make_async_remote_copy(
            src_ref=x_ref.at[1 - mc] if halves else x_ref, dst_ref=o_ref, send_sem=send_sems.at[i],
            recv_sem=recv_sems.at[i], device_id=(mx, my, 1 - mc), device_id_type=MESH)
            for i, (x_ref, o_ref) in enumerate(zip(x_refs, o_refs))])

    return _comm_call(body, name, xs, shapes, n, 0)


def chip_all_to_all(xs, name):
    n = len(xs)
    shapes = [jax.ShapeDtypeStruct((N_CHIPS - 1,) + x.shape[1:], x.dtype) for x in xs]

    def body(*refs):
        x_refs, o_refs = refs[:n], refs[n:2 * n]
        send_sems, recv_sems, _ = refs[2 * n:]
        mx, my, mc = _my_place()
        copies = []
        for i, (x_ref, o_ref) in enumerate(zip(x_refs, o_refs)):
            for j, (tx, ty) in enumerate(_other_chips(mx, my)):
                copies.append(pltpu.make_async_remote_copy(
                    src_ref=x_ref.at[2 * tx + ty], dst_ref=o_ref.at[j], send_sem=send_sems.at[3 * i + j],
                    recv_sem=recv_sems.at[3 * i + j], device_id=(tx, ty, mc), device_id_type=MESH))
        _run(copies)

    return _comm_call(body, name, xs, shapes, 3 * n, 0)


WEIGHTS = ['norm_mix', 'norm_xa', 'norm_mem', 'norm_ffn', 'xa_wq', 'xa_wk', 'xa_wv', 'xa_wo', 'xa_q_norm', 'xa_k_norm',
           'ffn_w_up', 'ffn_conv_w', 'ffn_conv_b', 'ffn_w_down', 'hg_lb_logits', 'mix_w_in', 'hg_out_norm',
           'mla_q_a_norm', 'mla_w_uq', 'mla_kv_a_norm', 'mla_w_ukv', 'mla_qn_nope', 'mla_qn_rope', 'mla_kn_nope',
           'mla_kn_rope', 'mix_w_out', 's5_lam_re', 's5_lam_im', 's5_log_dt', 's5_b_re', 's5_b_im', 's5_c_re',
           's5_c_im', 's5_d', 's5_w_glu_a', 's5_w_glu_b']
INPUTS = ['x', 'mem', 'positions'] + WEIGHTS + ['loss_target'] + ['m_' + n for n in WEIGHTS] + ['v_' + n for n in WEIGHTS]
SHARD_AXIS = {'xa_wq': 1, 'xa_wk': 1, 'xa_wv': 1, 'xa_wo': 1, 'ffn_w_up': 2, 'ffn_conv_w': 2, 'ffn_w_down': 1,
              'mix_w_in': 2, 'mla_w_uq': 2, 'mla_w_ukv': 2, 'mix_w_out': 1, 's5_d': 1, 's5_w_glu_a': 1, 's5_w_glu_b': 1}
BIG = ['xa_wq', 'xa_wk', 'xa_wv', 'xa_wo', 'ffn_w_up', 'ffn_w_down', 'mix_w_in', 'mix_w_out', 's5_w_glu_a', 's5_w_glu_b']
SMALL_SHARDED = [n for n in WEIGHTS if n in SHARD_AXIS and n not in BIG]
REPLICATED = [n for n in WEIGHTS if n not in SHARD_AXIS]
SMALL = SMALL_SHARDED + REPLICATED
PACK_W = 1024
ROW_MULT = 16
W_IN_SHARD = IN_WIDTH // N_CHIPS
W_IN_SHARD_PAD = 640


def _pack(flats, mult=ROW_MULT):
    flat = jnp.concatenate([f.reshape(-1) for f in flats])
    unit = mult * PACK_W
    n = -(-flat.shape[0] // unit) * unit
    return jnp.pad(flat, (0, n - flat.shape[0])).reshape(n // PACK_W, PACK_W)


def _unpack(packed, shapes):
    flat, out, o = packed.reshape(-1), [], 0
    for s in shapes:
        n = math.prod(s)
        out.append(flat[o:o + n].reshape(s))
        o += n
    return out


def _rope_pad(w):
    z = jnp.zeros(w.shape[:-1] + (MLA_ROPE // 2,), w.dtype)
    return jnp.concatenate([w[..., :MLA_ROPE // 2], z, w[..., MLA_ROPE // 2:], z], axis=-1)


def _rope_unpad(g):
    return jnp.concatenate([g[..., :MLA_ROPE // 2], g[..., 64:64 + MLA_ROPE // 2]], axis=-1)


def _blockdiag_in(bb):
    nb = bb.shape[0] // S5_GB
    t = bb.reshape(nb, S5_GB, S5_STATE, S5_GROUP).transpose(0, 1, 3, 2)
    return jnp.einsum('bgmp,gh->bgmhp', t, jnp.eye(S5_GB, dtype=bb.dtype)).reshape(nb, S5_GB * S5_GROUP, S5_LANES)


def _blockdiag_in_t(dw):
    nb = dw.shape[0]
    t = jnp.einsum('bgmhp,gh->bgmp', dw.reshape(nb, S5_GB, S5_GROUP, S5_GB, S5_STATE), jnp.eye(S5_GB, dtype=dw.dtype))
    return t.transpose(0, 1, 3, 2).reshape(nb * S5_GB, S5_STATE, S5_GROUP)


def _blockdiag_out(c):
    nb = c.shape[0] // S5_GB
    t = c.reshape(nb, S5_GB, S5_GROUP, S5_STATE).transpose(0, 1, 3, 2)
    return jnp.einsum('bgpm,gh->bgphm', t, jnp.eye(S5_GB, dtype=c.dtype)).reshape(nb, S5_LANES, S5_GB * S5_GROUP)


def _blockdiag_out_t(dc):
    nb = dc.shape[0]
    t = jnp.einsum('bgphm,gh->bgpm', dc.reshape(nb, S5_GB, S5_STATE, S5_GB, S5_GROUP), jnp.eye(S5_GB, dtype=dc.dtype))
    return t.transpose(0, 1, 3, 2).reshape(nb * S5_GB, S5_GROUP, S5_STATE)


def _gather_weights(P):
    xs = [P[n].astype(BF16) for n in BIG] + [_pack([P[n] for n in SMALL_SHARDED])]
    axes = [None if n == 'mix_w_in' else SHARD_AXIS[n] for n in BIG] + [None]
    got = chip_gather(xs, axes, "gather_weights")
    full_w = dict(zip(BIG, got[:-1]))
    full_w['mix_w_in'] = jnp.concatenate([full_w['mix_w_in'][q] for q in range(N_CHIPS)], axis=SHARD_AXIS['mix_w_in'])
    per_chip = [_unpack(got[-1][q], [P[n].shape for n in SMALL_SHARDED]) for q in range(N_CHIPS)]
    for i, n in enumerate(SMALL_SHARDED):
        full_w[n] = jnp.concatenate([per_chip[q][i] for q in range(N_CHIPS)], axis=SHARD_AXIS[n])
    return full_w


def _halves_first(x):
    return x.reshape(x.shape[0], 2, x.shape[1] // 2, x.shape[2]).transpose(1, 0, 2, 3)


def _reduce_and_update(GB, GS, P):
    mx, my, mc = _my_place()
    q = 2 * mx + my
    small = _pack([GS[n] for n in SMALL], 2 * N_CHIPS * ROW_MULT)
    xs = [GB[n] for n in BIG] + [_halves_first(small.reshape(N_CHIPS, -1, PACK_W))]
    theirs = pair_swap(xs, "grads_pair_swap", True)
    names = BIG + ['small']
    pair = [add2(lax.dynamic_index_in_dim(x, mc, 0, False), t, "grads_pair_sum_" + n, F32 if n == 'small' else BF16)
            for x, t, n in zip(xs, theirs, names)]
    got = chip_all_to_all(pair, "grads_chip_all_to_all")
    summed = [add_chips(lax.dynamic_index_in_dim(p, q, 0, False), g, "grads_chip_sum_" + n)
              for p, g, n in zip(pair, got, names)]
    other = pair_swap(summed, "grads_pair_join", False)
    joined = [lax.cond(mc == 0, lambda a, b: jnp.concatenate([a, b], axis=0), lambda a, b: jnp.concatenate([b, a], axis=0),
                       s, o) for s, o in zip(summed, other)]
    small_sum = chip_gather([joined[-1]], [0], "grads_small_gather")[0]
    g_small = dict(zip(SMALL, _unpack(small_sum, [GS[n].shape for n in SMALL])))
    for n in SMALL_SHARDED:
        s = P[n].shape[SHARD_AXIS[n]]
        g_small[n] = lax.dynamic_slice_in_dim(g_small[n], q * s, s, axis=SHARD_AXIS[n])

    grad, delta, new_m, new_v = {}, {}, {}, {}
    for n, g in zip(BIG, joined[:-1]):
        shape = P[n].shape
        if n == 'mix_w_in':
            g = g[:, :W_IN_SHARD]
        two_d = (g.shape[0], shape[-1])
        d, m_, v_ = adamw(P[n].reshape(two_d), g, P['m_' + n].reshape(two_d), P['v_' + n].reshape(two_d), "adamw_" + n)
        grad[n], delta[n], new_m[n], new_v[n] = (t.reshape(shape) for t in (g, d, m_, v_))
    packed = lambda prefix: _pack([P[prefix + n] for n in SMALL])
    d, m_, v_ = adamw(packed(''), _pack([g_small[n] for n in SMALL]), packed('m_'), packed('v_'), "adamw_small")
    shapes = [P[n].shape for n in SMALL]
    grad.update(g_small)
    for out, pk in ((delta, d), (new_m, m_), (new_v, v_)):
        out.update(zip(SMALL, _unpack(pk, shapes)))
    return grad, delta, new_m, new_v


def _row(v):
    return v.reshape(1, -1)


def _xattn_fwd(h, mem, W, lyr, tm):
    g_xa, g_mem = _row(W['norm_xa'][lyr]), _row(W['norm_mem'][lyr])
    g_q, g_k = _row(W['xa_q_norm'][lyr]), _row(W['xa_k_norm'][lyr])
    wq, wk, wv, wo = (W[n][lyr] for n in ('xa_wq', 'xa_wk', 'xa_wv', 'xa_wo'))
    L, D = h.shape
    M = mem.shape[0]
    hx = rms_fwd(h, g_xa, tm, MXU_DTYPE)
    qp = matmul(hx, wq, name="xa_q")
    kv_opds = [full(mem), full(g_mem), full(wk), full(wv), full(g_k)]
    k, v = blocked_fwd(_mem_kv, kv_opds, [((M, D), F32, (M, D), lambda i: (0, 0))] * 2, 1, "xa_mem_kv")
    o = blocked_fwd(_xa_core, [rows(qp, tm), full(k), full(v), full(g_q)], [_out((L, D), MXU_DTYPE, tm)], L // tm,
                    "xa_core")[0]
    out = matmul(o, wo, add=h, name="xa_o")
    return out, (h, hx, qp, k, v, o)


def _xattn_bwd(dout, saved, mem, W, lyr, tm):
    h, hx, qp, k, v, o = saved
    g_xa, g_mem = _row(W['norm_xa'][lyr]), _row(W['norm_mem'][lyr])
    g_q, g_k = _row(W['xa_q_norm'][lyr]), _row(W['xa_k_norm'][lyr])
    wq, wk, wv, wo = (W[n][lyr] for n in ('xa_wq', 'xa_wk', 'xa_wv', 'xa_wo'))
    L = h.shape[0]
    do = matmul(dout, wo, "nt", name="xa_do")
    d_wo = matmul(o, dout, "tn", name="xa_dwo")
    dqp, dk, dv, d_gq = blocked_bwd(_xa_core, [rows(qp, tm, 'blk'), full(k, 'acc'), full(v, 'acc'), full(g_q, 'acc')],
                                    [rows(do, tm)], L // tm, "xa_core_bwd")
    d_wq = matmul(hx, dqp, "tn", name="xa_dwq")
    dhx = matmul(dqp, wq, "nt", name="xa_dhx")
    dh, d_gxa = rms_bwd(h, g_xa, dhx, tm, dout)
    d_gmem, d_wk, d_wv, d_gk = blocked_bwd(
        _mem_kv, [full(mem), full(g_mem, 'acc'), full(wk, 'acc'), full(wv, 'acc'), full(g_k, 'acc')],
        [full(dk), full(dv)], 1, "xa_mem_kv_bwd")
    by_chip = lambda g: g.reshape(N_CHIPS, g.shape[0] // N_CHIPS, g.shape[1])
    grads = {'norm_xa': d_gxa, 'norm_mem': d_gmem, 'xa_q_norm': d_gq, 'xa_k_norm': d_gk,
             'xa_wq': by_chip(d_wq), 'xa_wk': by_chip(d_wk), 'xa_wv': by_chip(d_wv), 'xa_wo': by_chip(d_wo)}
    return dh,grads


def _conv_params(W, lyr):
    cw, cb = W['ffn_conv_w'][lyr], W['ffn_conv_b'][lyr]
    F = cw.shape[1] // 2
    return [cw[0:1, :F], cw[1:2, :F], cw[2:3, :F], cw[0:1, F:], cw[1:2, F:], cw[2:3, F:], _row(cb[:F]), _row(cb[F:])]


def _ffn_fwd(h, W, lyr, tm):
    L, D = h.shape
    w_up, w_down = W['ffn_w_up'][lyr], W['ffn_w_down'][lyr]
    F = w_down.shape[0]
    hf = rms_fwd(h, _row(W['norm_ffn'][lyr]), tm, MXU_DTYPE)
    ug = matmul(hf, w_up[:, :F], name="ffn_up_gate")
    uv = matmul(hf, w_up[:, F:], name="ffn_up_value")
    opds = [cols(ug, 128), cols(uv, 128)] + [cols(p, 128) for p in _conv_params(W, lyr)]
    a = blocked_fwd(_conv_gate, opds, [((L, F), MXU_DTYPE, (L, 128), lambda j: (0, j))], F // 128, "ffn_conv_gate")[0]
    out = matmul(a, w_down, add=h, name="ffn_down")
    return out, (h, hf, ug, uv, a)


def _ffn_bwd(dout, saved, W, lyr, tm):
    h, hf, ug, uv, a = saved
    w_up, w_down = W['ffn_w_up'][lyr], W['ffn_w_down'][lyr]
    F = w_down.shape[0]
    da = matmul(dout, w_down, "nt", name="ffn_da")
    d_wdown = matmul(a, dout, "tn", name="ffn_dwdown")
    opds = [cols(ug, 128, 'blk'), cols(uv, 128, 'blk')] + [cols(p, 128, 'blk') for p in _conv_params(W, lyr)]
    gs = blocked_bwd(_conv_gate, opds, [cols(da, 128)], F // 128, "ffn_conv_gate_bwd")
    dug, duv = gs[0], gs[1]
    d_cw = jnp.concatenate([jnp.concatenate(gs[2:5], axis=0), jnp.concatenate(gs[5:8], axis=0)], axis=1)
    d_cb = jnp.concatenate([gs[8], gs[9]], axis=1)[0]
    d_wup = jnp.concatenate([matmul(hf, dug, "tn", name="ffn_dwup_gate", col_blocks=N_CHIPS // 2),
                             matmul(hf, duv, "tn", name="ffn_dwup_value", col_blocks=N_CHIPS // 2)], axis=0)
    dhf = matmul(dug, w_up[:, :F], "nt", name="ffn_dhf_gate")
    dhf = matmul(duv, w_up[:, F:], "nt", add=dhf, name="ffn_dhf_value")
    dh, d_g = rms_bwd(h, _row(W['norm_ffn'][lyr]), dhf, tm, dout)
    d_wdown = d_wdown.reshape(N_CHIPS, F // N_CHIPS, d_wdown.shape[1])
    return dh,{'norm_ffn': d_g, 'ffn_w_up': d_wup, 'ffn_conv_w': d_cw, 'ffn_conv_b': d_cb, 'ffn_w_down': d_wdown}


def _mla_params(W):
    w_uq = W['mla_w_uq'][0].reshape(MLA_Q_RANK, MLA_HEADS, MLA_QK)
    w_uq = jnp.concatenate([w_uq[..., :MLA_NOPE], _rope_pad(w_uq[..., MLA_NOPE:])], axis=-1)
    w_ukv = W['mla_w_ukv'][0].reshape(MLA_KV_RANK, MLA_HEADS, MLA_NOPE + MLA_V)
    w_ukv = jnp.concatenate([w_ukv[..., :MLA_NOPE].reshape(MLA_KV_RANK, -1), w_ukv[..., MLA_NOPE:].reshape(MLA_KV_RANK, -1)],
                            axis=1)
    return [_row(W['mla_q_a_norm'][0]), w_uq.reshape(MLA_Q_RANK, MLA_HEADS * MLA_DK), _row(W['mla_kv_a_norm'][0]), w_ukv,
            _row(W['mla_qn_nope'][0]), _row(_rope_pad(W['mla_qn_rope'][0])), _row(W['mla_kn_nope'][0]),
            _row(_rope_pad(W['mla_kn_rope'][0]))]


def _w_in_padded(W):
    w = W['mix_w_in'][0]
    return jnp.concatenate([w[:, :IN_WIDTH - MLA_ROPE], _rope_pad(w[:, IN_WIDTH - MLA_ROPE:])], axis=1)


def _mixer0_fwd(h, W, cos_p, sin_p, tm):
    L = h.shape[0]
    t = min(256, L)
    hn = rms_fwd(h, _row(W['norm_mix'][0]), tm, MXU_DTYPE)
    proj = matmul(hn, _w_in_padded(W), name="mix_in")
    logits = W['hg_lb_logits']
    lb = blocked_fwd(_lb_first, [full(logits)], [((1, HG_WIDTH), F32, (1, HG_WIDTH), lambda i: (0, 0))], 1, "hg_lb")[0]
    gain = _row(W['hg_out_norm'][0])
    o_hg, states = hgrn2_fwd(proj, lb, gain)
    mp = _mla_params(W)
    q, k, v = mla_prep_fwd(proj, cos_p, sin_p, mp, tm)
    scale = MLA_QK ** -0.5
    o_mla, lse = attn_fwd(q, k, v, scale, t)
    w_out = W['mix_w_out'][0]
    out = matmul(o_hg, w_out[:HG_WIDTH], add=h, name="mix_out_hg")
    out = matmul(o_mla, w_out[HG_WIDTH:], add=out, name="mix_out_mla")
    return out, (h, hn, proj, lb, o_hg, states, q, k, v, o_mla, lse)


def _mixer0_bwd(dout, saved, W, cos_p, sin_p, tm):
    h, hn, proj, lb, o_hg, states, q, k, v, o_mla, lse = saved
    L = h.shape[0]
    t = min(256, L)
    scale = MLA_QK ** -0.5
    w_out = W['mix_w_out'][0]
    gain = _row(W['hg_out_norm'][0])
    do_hg = matmul(dout, w_out[:HG_WIDTH], "nt", name="mix_do_hg")
    do_mla = matmul(dout, w_out[HG_WIDTH:], "nt", name="mix_do_mla")
    d_wout = jnp.concatenate([matmul(o_hg, dout, "tn", name="mix_dwout_hg"), matmul(o_mla, dout, "tn", name="mix_dwout_mla")],
                             axis=0)
    dq = attn_bwd_dq(q, k, v, o_mla, lse, do_mla, scale, t)
    dk, dv = attn_bwd_dkv(q, k, v, o_mla, lse, do_mla, scale, t)
    mp = _mla_params(W)
    dcq, dckv, dkpe, d_qa, d_wuq, d_kva, d_wukv, d_qnn, d_qnr, d_knn, d_knr = mla_prep_bwd(proj, cos_p, sin_p, mp, dq, dk, dv, tm)
    d_hg, d_lb, d_gain = hgrn2_bwd(proj, lb, gain, states, do_hg)
    dproj = jnp.concatenate([d_hg, dcq, dckv, dkpe], axis=1)
    d_win = matmul(hn, dproj, "tn", name="mix_dwin")
    dhn = matmul(dproj, _w_in_padded(W), "nt", name="mix_dhn")
    dh, d_g = rms_bwd(h, _row(W['norm_mix'][0]), dhn, tm, dout)
    logits = W['hg_lb_logits']
    d_logits = blocked_bwd(_lb_first, [full(logits, 'acc')], [full(d_lb)], 1, "hg_lb_bwd")[0]
    d_wuq = d_wuq.reshape(MLA_Q_RANK, MLA_HEADS, MLA_DK)
    d_wuq = jnp.concatenate([d_wuq[..., :MLA_NOPE], _rope_unpad(d_wuq[..., MLA_NOPE:])], axis=-1)
    hw = MLA_HEADS * MLA_NOPE
    d_wukv = jnp.concatenate([d_wukv[:, :hw].reshape(MLA_KV_RANK, MLA_HEADS, MLA_NOPE),
                              d_wukv[:, hw:].reshape(MLA_KV_RANK, MLA_HEADS, MLA_V)], axis=-1)
    d_win = jnp.concatenate([d_win[:, :IN_WIDTH - MLA_ROPE], _rope_unpad(d_win[:, IN_WIDTH - MLA_ROPE:])], axis=1)
    d_win = d_win.reshape(d_win.shape[0], N_CHIPS, W_IN_SHARD).transpose(1, 0, 2)
    d_win = jnp.pad(d_win, ((0, 0), (0, 0), (0, W_IN_SHARD_PAD - W_IN_SHARD)))
    d_wout = d_wout.reshape(N_CHIPS, d_wout.shape[0] // N_CHIPS, d_wout.shape[1])
    grads = {'norm_mix': d_g, 'hg_lb_logits': d_logits, 'mix_w_in': d_win, 'hg_out_norm': d_gain,
             'mla_q_a_norm': d_qa, 'mla_w_uq': d_wuq.reshape(1, MLA_Q_RANK, -1), 'mla_kv_a_norm': d_kva,
             'mla_w_ukv': d_wukv.reshape(1, MLA_KV_RANK, -1), 'mla_qn_nope': d_qnn, 'mla_qn_rope': _rope_unpad(d_qnr),
             'mla_kn_nope': d_knn, 'mla_kn_rope': _rope_unpad(d_knr), 'mix_w_out': d_wout}
    return dh,grads


def _s5_inputs(W):
    G = W['s5_lam_re'].shape[1]
    return [W['s5_lam_re'][0], W['s5_lam_im'][0], W['s5_log_dt'][0].reshape(G, 1),
            W['s5_b_re'][0].reshape(G, -1), W['s5_b_im'][0].reshape(G, -1)]


def _mixer1_fwd(h, W, tm):
    L, D = h.shape
    u = rms_fwd(h, _row(W['norm_mix'][1]), tm, F32)
    di = _s5_inputs(W)
    G = di[0].shape[0]
    sq, wide = ((G, S5_STATE), F32, (G, S5_STATE), lambda i: (0, 0)), ((G, S5_STATE * S5_GROUP), F32, (G, S5_STATE * S5_GROUP), lambda i: (0, 0))
    ar, ai, bbr, bbi = blocked_fwd(_s5_discretize, [full(a) for a in di], [sq, sq, wide, wide], 1, "s5_discretize")
    nb = G // S5_GB
    core = (_blockdiag_in(bbr.reshape(G, S5_STATE, S5_GROUP)), _blockdiag_in(bbi.reshape(G, S5_STATE, S5_GROUP)),
            ar.reshape(nb, 1, S5_LANES), ai.reshape(nb, 1, S5_LANES),
            _blockdiag_out(W['s5_c_re'][0]), _blockdiag_out(W['s5_c_im'][0]))
    y = s5_fwd(u, *core)
    d = W['s5_d']
    y2 = blocked_fwd(_s5_post, [rows(y, tm), rows(u, tm), full(d)], [_out((L, D), MXU_DTYPE, tm)], L // tm, "s5_post")[0]
    w_ab = jnp.concatenate([W['s5_w_glu_a'][0], W['s5_w_glu_b'][0]], axis=1)
    ab = matmul(y2, w_ab, name="s5_glu_in")
    mix = blocked_fwd(_glu, [rows(ab, tm, col=0, width=D), rows(ab, tm, col=1, width=D)], [_out((L, D), F32, tm)], L // tm,
                      "s5_glu")[0]
    return h + mix, (h, u, core, y, y2, ab)


def _mixer1_bwd(dout, saved, W, tm):
    h, u, core, y, y2, ab = saved
    L, D = h.shape
    da, db = blocked_bwd(_glu, [rows(ab, tm, 'blk', col=0, width=D), rows(ab, tm, 'blk', col=1, width=D)], [rows(dout, tm)],
                         L // tm, "s5_glu_bwd")
    w_a, w_b = W['s5_w_glu_a'][0], W['s5_w_glu_b'][0]
    dy2 = matmul(da, w_a, "nt", name="s5_dy2_a")
    dy2 = matmul(db, w_b, "nt", add=dy2, name="s5_dy2_b")
    d_wa = matmul(y2, da, "tn", name="s5_dwa")
    d_wb = matmul(y2, db, "tn", name="s5_dwb")
    d = W['s5_d']
    dy, du_skip, d_d = blocked_bwd(_s5_post, [rows(y, tm, 'blk'), rows(u, tm, 'blk'), full(d, 'acc')], [rows(dy2, tm)], L // tm,
                                   "s5_post_bwd")
    du, dwr, dwi, dar, dai, dcr, dci = s5_bwd(u, *core, dy, min(256, L))
    di = _s5_inputs(W)
    G = di[0].shape[0]
    cts = [dar.reshape(G, S5_STATE), dai.reshape(G, S5_STATE), _blockdiag_in_t(dwr).reshape(G, -1), _blockdiag_in_t(dwi).reshape(G, -1)]
    d_lr, d_li, d_ldt, d_br, d_bi = blocked_bwd(_s5_discretize, [full(a, 'acc') for a in di], [full(c) for c in cts], 1,
                                                "s5_discretize_bwd")
    dh, d_g = rms_bwd(h, _row(W['norm_mix'][1]), du + du_skip, tm, dout)
    bshape = W['s5_b_re'].shape
    grads = {'norm_mix': d_g, 's5_lam_re': d_lr[None], 's5_lam_im': d_li[None], 's5_log_dt': d_ldt.reshape(1, G),
             's5_b_re': d_br.reshape(bshape), 's5_b_im': d_bi.reshape(bshape), 's5_c_re': _blockdiag_out_t(dcr)[None],
             's5_c_im': _blockdiag_out_t(dci)[None], 's5_d': d_d, 's5_w_glu_a': d_wa.reshape(N_CHIPS, -1, D), 's5_w_glu_b': d_wb.reshape(N_CHIPS, -1, D)}
    return dh,grads


def kernel(x, mem, positions, norm_mix, norm_xa, norm_mem, norm_ffn, xa_wq, xa_wk, xa_wv, xa_wo, xa_q_norm, xa_k_norm, ffn_w_up, ffn_conv_w, ffn_conv_b, ffn_w_down, hg_lb_logits, mix_w_in, hg_out_norm, mla_q_a_norm, mla_w_uq, mla_kv_a_norm, mla_w_ukv, mla_qn_nope, mla_qn_rope, mla_kn_nope, mla_kn_rope, mix_w_out, s5_lam_re, s5_lam_im, s5_log_dt, s5_b_re, s5_b_im, s5_c_re, s5_c_im, s5_d, s5_w_glu_a, s5_w_glu_b, loss_target, m_norm_mix, m_norm_xa, m_norm_mem, m_norm_ffn, m_xa_wq, m_xa_wk, m_xa_wv, m_xa_wo, m_xa_q_norm, m_xa_k_norm, m_ffn_w_up, m_ffn_conv_w, m_ffn_conv_b, m_ffn_w_down, m_hg_lb_logits, m_mix_w_in, m_hg_out_norm, m_mla_q_a_norm, m_mla_w_uq, m_mla_kv_a_norm, m_mla_w_ukv, m_mla_qn_nope, m_mla_qn_rope, m_mla_kn_nope, m_mla_kn_rope, m_mix_w_out, m_s5_lam_re, m_s5_lam_im, m_s5_log_dt, m_s5_b_re, m_s5_b_im, m_s5_c_re, m_s5_c_im, m_s5_d, m_s5_w_glu_a, m_s5_w_glu_b, v_norm_mix, v_norm_xa, v_norm_mem, v_norm_ffn, v_xa_wq, v_xa_wk, v_xa_wv, v_xa_wo, v_xa_q_norm, v_xa_k_norm, v_ffn_w_up, v_ffn_conv_w, v_ffn_conv_b, v_ffn_w_down, v_hg_lb_logits, v_mix_w_in, v_hg_out_norm, v_mla_q_a_norm, v_mla_w_uq, v_mla_kv_a_norm, v_mla_w_ukv, v_mla_qn_nope, v_mla_qn_rope, v_mla_kn_nope, v_mla_kn_rope, v_mix_w_out, v_s5_lam_re, v_s5_lam_im, v_s5_log_dt, v_s5_b_re, v_s5_b_im, v_s5_c_re, v_s5_c_im, v_s5_d, v_s5_w_glu_a, v_s5_w_glu_b):
    P = dict(locals())
    assert sorted(P) == sorted(INPUTS) and norm_mix.shape[0] == 2 and mix_w_in.shape[0] == 1
    x, mem, target = P['x'][0], P['mem'][0], P['loss_target'][0]
    L, D = x.shape
    tm = min(256, L)

    W = {n: P[n] for n in REPLICATED}
    W.update(_gather_weights(P))

    inv_freq = 1.0 / (ROPE_BASE ** (jnp.arange(0, MLA_ROPE, 2, dtype=F32) / MLA_ROPE))
    ang = P['positions'][0].astype(F32)[:, None] * inv_freq
    cos, sin, z = jnp.cos(ang), jnp.sin(ang), jnp.zeros_like(ang)
    cos_p = jnp.concatenate([cos, z, cos, z], axis=1)
    sin_p = jnp.concatenate([-sin, z, sin, z], axis=1)

    h, s_mix0 = _mixer0_fwd(x, W, cos_p, sin_p, tm)
    h, s_xa0 = _xattn_fwd(h, mem, W, 0, tm)
    h, s_ffn0 = _ffn_fwd(h, W, 0, tm)
    h, s_mix1 = _mixer1_fwd(h, W, tm)
    h, s_xa1 = _xattn_fwd(h, mem, W, 1, tm)
    h, s_ffn1 = _ffn_fwd(h, W, 1, tm)
    n = L // tm
    dh, parts = blocked_fwd(_loss_fn, [rows(h, tm), rows(target, tm)],
                            [_out((L, D), F32, tm), ((n * 8, 128), F32, (8, 128), lambda i: (i, 0))], n, "loss")
    loss = lax.psum(jnp.sum(parts), ("x", "y", "c"))

    layered = {}

    def collect(g, lyr):
        for k_, v_ in g.items():
            layered.setdefault(k_, {})[lyr] = v_

    dh, g = _ffn_bwd(dh, s_ffn1, W, 1, tm)
    collect(g, 1)
    dh, g = _xattn_bwd(dh, s_xa1, mem, W, 1, tm)
    collect(g, 1)
    dh, g = _mixer1_bwd(dh, s_mix1, W, tm)
    collect(g, 1)
    dh, g = _ffn_bwd(dh, s_ffn0, W, 0, tm)
    collect(g, 0)
    dh, g = _xattn_bwd(dh, s_xa0, mem, W, 0, tm)
    collect(g, 0)
    dx, g = _mixer0_bwd(dh, s_mix0, W, cos_p, sin_p, tm)
    collect(g, 0)

    GB, GS = {}, {}
    for name in WEIGHTS:
        by_layer = [layered[name][lyr] for lyr in sorted(layered[name])]
        if name in BIG:
            GB[name] = _halves_first(by_layer[0]) if len(by_layer) == 1 else jnp.stack(by_layer)
        else:
            full_shape = W[name].shape
            GS[name] = (by_layer[0].reshape(full_shape) if len(by_layer) == 1
                        else jnp.stack([g_.reshape(full_shape[1:]) for g_ in by_layer]))

    outs = _reduce_and_update(GB, GS, P)
    return (loss, dx[None], *[d[n_] for d in outs for n_ in WEIGHTS])
```

```python
import functools
import math

import jax
import jax.numpy as jnp
import numpy as np
from jax import lax
from jax.experimental import pallas as pl
from jax.experimental.pallas import tpu as pltpu

F32 = jnp.float32
BF16 = jnp.bfloat16
MXU_DTYPE = BF16
HI = lax.Precision.HIGHEST
V7X_VMEM_LIMIT_BYTES = 56 * 1024 * 1024
EPS = 1e-6
MESH = pl.DeviceIdType.MESH

HG_HEADS, HG_DIM = 4, 128
HG_WIDTH = HG_HEADS * HG_DIM
HG_SUB = 32
HG_BLOCK = 64
MLA_HEADS, MLA_Q_RANK, MLA_KV_RANK = 4, 256, 128
MLA_NOPE, MLA_ROPE, MLA_V = 128, 64, 128
MLA_QK = MLA_NOPE + MLA_ROPE
MLA_DK = 256
ROPE_BASE = 10000.0
IN_WIDTH = 4 * HG_WIDTH + MLA_Q_RANK + MLA_KV_RANK + MLA_ROPE
IN_PAD = 4 * HG_WIDTH + MLA_Q_RANK + MLA_KV_RANK + 128
S5_GROUP, S5_STATE = 16, 64
S5_GB = 8
DT_MIN, DT_MAX = 1e-3, 1e-1
XA_HEADS = 4
CONV_W = 3
ADAM_LR, ADAM_B1, ADAM_B2, ADAM_EPS, ADAM_WD, ADAM_STEP = 0.001, 0.9, 0.999, 1e-08, 0.01, 10


def _cparams(sem):
    return pltpu.CompilerParams(dimension_semantics=sem, vmem_limit_bytes=V7X_VMEM_LIMIT_BYTES)


class Opd:
    def __init__(self, arr, block, imap, grad=None, gshape=None, gimap=None):
        self.arr, self.block, self.imap, self.grad = arr, block, imap, grad
        self.gshape = arr.shape if gshape is None else gshape
        self.gimap = imap if gimap is None else gimap

    def spec(self):
        return pl.BlockSpec(self.block, self.imap)

    def gspec(self):
        return pl.BlockSpec(self.block, self.gimap)


def rows(arr, tm, grad=None, col=0, width=None):
    width = arr.shape[1] if width is None else width
    return Opd(arr, (tm, width), lambda i, c=col: (i, c), grad, (arr.shape[0], width), lambda i: (i, 0))


def cols(arr, tn, grad=None):
    return Opd(arr, (arr.shape[0], tn), lambda j: (0, j), grad)


def full(arr, grad=None):
    return Opd(arr, arr.shape, lambda i: (0, 0), grad)


def _load(ref):
    v = ref[...]
    return v.astype(F32) if jnp.issubdtype(v.dtype, jnp.floating) else v


def blocked_fwd(f, opds, outs, n, name):
    n_in = len(opds)

    def body(*refs):
        ys = f(*[_load(r) for r in refs[:n_in]])
        for r, y in zip(refs[n_in:], ys):
            r[...] = y.astype(r.dtype)

    res = pl.pallas_call(
        body, name=name, grid=(n,),
        in_specs=[o.spec() for o in opds],
        out_specs=[pl.BlockSpec(b, m) for (_, _, b, m) in outs],
        out_shape=[jax.ShapeDtypeStruct(s, d) for (s, d, _, _) in outs],
        compiler_params=_cparams(("parallel",)),
    )(*[o.arr for o in opds])
    return res


def blocked_bwd(f, opds, dys, n, name, plus=None):
    n_in, n_dy = len(opds), len(dys)
    diff = [i for i, o in enumerate(opds) if o.grad]
    extra = [] if plus is None else [plus]

    def body(*refs):
        vals = [_load(r) for r in refs[:n_in]]

        def fd(*dv):
            allv = list(vals)
            for i, v in zip(diff, dv):
                allv[i] = v
            return tuple(f(*allv))

        ys, vjp = jax.vjp(fd, *[vals[i] for i in diff])
        cts = tuple(_load(r).astype(y.dtype) for r, y in zip(refs[n_in:n_in + n_dy], ys))
        gs = list(vjp(cts))
        if extra:
            gs[0] = gs[0] + _load(refs[n_in + n_dy])
        for r, g, i in zip(refs[n_in + n_dy + len(extra):], gs, diff):
            if opds[i].grad == 'acc':
                @pl.when(pl.program_id(0) == 0)
                def _(r=r):
                    r[...] = jnp.zeros(r.shape, r.dtype)
                r[...] += g.astype(r.dtype)
            else:
                r[...] = g.astype(r.dtype)

    any_acc = any(opds[i].grad == 'acc' for i in diff)
    res = pl.pallas_call(
        body, name=name, grid=(n,),
        in_specs=[o.spec() for o in opds + dys + extra],
        out_specs=[opds[i].gspec() for i in diff],
        out_shape=[jax.ShapeDtypeStruct(opds[i].gshape, F32) for i in diff],
        compiler_params=_cparams(("arbitrary" if any_acc else "parallel",)),
    )(*[o.arr for o in opds + dys + extra])
    return res


def _tile(dim, want):
    for t in range(want - want % 16, 0, -16):
        if dim % t == 0:
            return t
    assert dim <= want, (dim, want)
    return dim


MATMUL_VMEM_BUDGET = 40 * 1024 * 1024
MATMUL_ROWS = 512


def _widest(N, fits):
    for t in range(N - N % 128, 0, -128):
        if N % t == 0 and fits(t):
            return t
    return N


def matmul(a, b, mode="nn", out_dtype=F32, add=None, name="matmul", col_blocks=None):
    sa, sb, so = a.dtype.itemsize, b.dtype.itemsize, jnp.dtype(out_dtype).itemsize
    has_add = add is not None
    if mode == "tn":
        (K, M), (K2, N) = a.shape, b.shape
        assert K == K2 and not has_add and out_dtype == F32, (a.shape, b.shape)
        tk = _tile(K, MATMUL_ROWS)
        tn = _widest(N, lambda t: 2 * (tk * M * sa + tk * t * sb + M * t * 4) <= MATMUL_VMEM_BUDGET)
        if col_blocks is not None:
            assert N % col_blocks == 0 and (N // col_blocks) % 128 == 0 and tn >= N // col_blocks, (N, col_blocks, tn)
            tn = N // col_blocks
            out_spec = pl.BlockSpec((None, M, tn), lambda j, k: (j, 0, 0))
            out_shape = jax.ShapeDtypeStruct((col_blocks, M, tn), F32)
        else:
            out_spec = pl.BlockSpec((M, tn), lambda j, k: (0, j))
            out_shape = jax.ShapeDtypeStruct((M, N), F32)

        def body(a_ref, b_ref, o_ref):
            r = lax.dot_general(a_ref[...].astype(MXU_DTYPE), b_ref[...].astype(MXU_DTYPE), ((_TN), ((), ())),
                                preferred_element_type=F32)

            @pl.when(pl.program_id(1) == 0)
            def _():
                o_ref[...] = r

            @pl.when(pl.program_id(1) > 0)
            def _():
                o_ref[...] += r

        return pl.pallas_call(
            body, name=name, grid=(N // tn, K // tk),
            in_specs=[pl.BlockSpec((tk, M), lambda j, k: (k, 0)), pl.BlockSpec((tk, tn), lambda j, k: (k, j))],
            out_specs=out_spec, out_shape=out_shape,
            compiler_params=_cparams(("parallel", "arbitrary")),
        )(a, b)

    (M, K) = a.shape
    N = b.shape[1] if mode == "nn" else b.shape[0]
    assert K == (b.shape[0] if mode == "nn" else b.shape[1]), (a.shape, b.shape, mode)
    tm = _tile(M, MATMUL_ROWS)
    tn = _widest(N, lambda t: 2 * (tm * K * sa + K * t * sb + tm * t * (so + 4 * has_add)) <= MATMUL_VMEM_BUDGET)
    dims = ((_NN if mode == "nn" else _NT), ((), ()))

    def body(*refs):
        r = lax.dot_general(refs[0][...].astype(MXU_DTYPE), refs[1][...].astype(MXU_DTYPE), dims, preferred_element_type=F32)
        if has_add:
            r = r + refs[2][...].astype(F32)
        refs[-1][...] = r.astype(refs[-1].dtype)

    b_spec = pl.BlockSpec((K, tn), lambda j, i: (0, j)) if mode == "nn" else pl.BlockSpec((tn, K), lambda j, i: (j, 0))
    in_specs = [pl.BlockSpec((tm, K), lambda j, i: (i, 0)), b_spec]
    args = [a, b]
    if has_add:
        in_specs.append(pl.BlockSpec((tm, tn), lambda j, i: (i, j)))
        args.append(add)
    return pl.pallas_call(
        body, name=name, grid=(N // tn, M // tm),
        in_specs=in_specs,
        out_specs=pl.BlockSpec((tm, tn), lambda j, i: (i, j)),
        out_shape=jax.ShapeDtypeStruct((M, N), out_dtype),
        compiler_params=_cparams(("parallel", "parallel")),
    )(*args)


def _dot(a, b, dims, precision=None):
    if precision is None:
        a, b = a.astype(MXU_DTYPE), b.astype(MXU_DTYPE)
    return lax.dot_general(a, b, (dims, ((), ())), precision=precision, preferred_element_type=F32)


_NN = ((1,), (0,))
_NT = ((1,), (1,))
_TN = ((0,), (0,))


def _rms(x, gain):
    return x * lax.rsqrt(jnp.mean(x * x, axis=-1, keepdims=True) + EPS) * gain


def _hg_block(st_t, q, fl, iv, g, lb, gain):
    row = lax.broadcasted_iota(jnp.int32, (HG_SUB, HG_SUB), 0)
    col = lax.broadcasted_iota(jnp.int32, (HG_SUB, HG_SUB), 1)
    tri = (row >= col).astype(F32)
    outs, states = [], []
    for h in range(HG_HEADS):
        sl = slice(h * HG_DIM, (h + 1) * HG_DIM)
        st = st_t[h * HG_DIM:(h + 1) * HG_DIM, :]
        lbh = lb[:, sl]
        fg = lbh + (1.0 - lbh) * jax.nn.sigmoid(fl[:, sl])
        lf, kk, qf, v = jnp.log(fg), 1.0 - fg, jax.nn.silu(q[:, sl]), iv[:, sl]
        parts = []
        for s in range(q.shape[0] // HG_SUB):
            r = slice(s * HG_SUB, (s + 1) * HG_SUB)
            b = _dot(tri, lf[r], _NN, HI)
            b_mid = jnp.sum(lf[r][:HG_SUB // 2], axis=0, keepdims=True)
            b_end = jnp.sum(lf[r], axis=0, keepdims=True)
            sc = _dot(qf[r] * jnp.exp(b - b_mid), kk[r] * jnp.exp(b_mid - b), _NT) * tri
            parts.append(_dot(sc, v[r], _NN) + _dot(qf[r] * jnp.exp(b), st, _NT))
            st = st * jnp.exp(b_end) + _dot(v[r], kk[r] * jnp.exp(b_end - b), _TN)
        o = jnp.concatenate(parts, axis=0)
        outs.append(_rms(o, gain[:, sl]) * jax.nn.silu(g[:, sl]))
        states.append(st)
    return jnp.concatenate(states, axis=0), jnp.concatenate(outs, axis=1)


def _hg_specs(proj, nb):
    return [pl.BlockSpec((HG_BLOCK, HG_WIDTH), lambda i, c=c, f=nb: (f(i), c)) for c in range(4)]


def hgrn2_fwd(proj, lb, gain):
    L = proj.shape[0]
    n = L // HG_BLOCK

    def body(q, fl, iv, g, lb_r, gain_r, o_ref, st_ref, st):
        @pl.when(pl.program_id(0) == 0)
        def _():
            st[...] = jnp.zeros(st.shape, F32)

        st_ref[0] = st[...]
        new, o = _hg_block(st[...], q[...], fl[...], iv[...], g[...], lb_r[...], gain_r[...])
        st[...] = new
        o_ref[...] = o.astype(o_ref.dtype)

    pspec = pl.BlockSpec((1, HG_WIDTH), lambda i: (0, 0))
    return pl.pallas_call(
        body, name="hgrn2_fwd", grid=(n,),
        in_specs=_hg_specs(proj, lambda i: i) + [pspec, pspec],
        out_specs=[pl.BlockSpec((HG_BLOCK, HG_WIDTH), lambda i: (i, 0)),
                   pl.BlockSpec((1, HG_WIDTH, HG_DIM), lambda i: (i, 0, 0))],
        out_shape=[jax.ShapeDtypeStruct((L, HG_WIDTH), MXU_DTYPE),
                   jax.ShapeDtypeStruct((n, HG_WIDTH, HG_DIM), F32)],
        scratch_shapes=[pltpu.VMEM((HG_WIDTH, HG_DIM), F32)],
        compiler_params=_cparams(("arbitrary",)),
    )(proj, proj, proj, proj, lb, gain)


def hgrn2_bwd(proj, lb, gain, states, do):
    L = proj.shape[0]
    n = L // HG_BLOCK

    def body(q, fl, iv, g, lb_r, gain_r, st_r, do_r, dproj, dlb, dgain, dst):
        @pl.when(pl.program_id(0) == 0)
        def _():
            dst[...] = jnp.zeros(dst.shape, F32)
            dlb[...] = jnp.zeros(dlb.shape, F32)
            dgain[...] = jnp.zeros(dgain.shape, F32)

        _, vjp = jax.vjp(_hg_block, st_r[0], q[...], fl[...], iv[...], g[...], lb_r[...], gain_r[...])
        d_st, dq, dfl, div, dg, d_lb, d_gain = vjp((dst[...], do_r[...].astype(F32)))
        dst[...] = d_st
        dproj[:, 0 * HG_WIDTH:1 * HG_WIDTH] = dq
        dproj[:, 1 * HG_WIDTH:2 * HG_WIDTH] = dfl
        dproj[:, 2 * HG_WIDTH:3 * HG_WIDTH] = div
        dproj[:, 3 * HG_WIDTH:4 * HG_WIDTH] = dg
        dlb[...] += d_lb
        dgain[...] += d_gain

    rev = lambda i: n - 1 - i
    pspec = pl.BlockSpec((1, HG_WIDTH), lambda i: (0, 0))
    return pl.pallas_call(
        body, name="hgrn2_bwd", grid=(n,),
        in_specs=_hg_specs(proj, rev) + [pspec, pspec,
                                         pl.BlockSpec((1, HG_WIDTH, HG_DIM), lambda i: (rev(i), 0, 0)),
                                         pl.BlockSpec((HG_BLOCK, HG_WIDTH), lambda i: (rev(i), 0))],
        out_specs=[pl.BlockSpec((HG_BLOCK, 4 * HG_WIDTH), lambda i: (rev(i), 0)), pspec, pspec],
        out_shape=[jax.ShapeDtypeStruct((L, 4 * HG_WIDTH), F32),
                   jax.ShapeDtypeStruct((1, HG_WIDTH), F32), jax.ShapeDtypeStruct((1, HG_WIDTH), F32)],
        scratch_shapes=[pltpu.VMEM((HG_WIDTH, HG_DIM), F32)],
        compiler_params=_cparams(("arbitrary",)),
    )(proj, proj, proj, proj, lb, gain, states, do)


def _rope_rms(x, gain_p, cos_p, sin_p):
    n = x * lax.rsqrt(jnp.sum(x * x, axis=-1, keepdims=True) * (1.0 / MLA_ROPE) + EPS) * gain_p
    r = lax.broadcasted_iota(jnp.int32, (128, 128), 0)
    c = lax.broadcasted_iota(jnp.int32, (128, 128), 1)
    swap = (r == (c + 64) % 128).astype(F32)
    return n * cos_p + _dot(n, swap, _NN, HI) * sin_p


def _mla_prep(c_q, c_kv, kpe, cos_p, sin_p, q_a, w_uq, kv_a, w_ukv, qn_nope, qn_rope, kn_nope, kn_rope):
    q = _dot(_rms(c_q, q_a), w_uq, _NN)
    kv = _dot(_rms(c_kv, kv_a), w_ukv, _NN)
    k_pe = _rope_rms(kpe, kn_rope, cos_p, sin_p)
    qs, ks = [], []
    for h in range(MLA_HEADS):
        qs.append(_rms(q[:, h * MLA_DK:h * MLA_DK + MLA_NOPE], qn_nope))
        qs.append(_rope_rms(q[:, h * MLA_DK + MLA_NOPE:(h + 1) * MLA_DK], qn_rope, cos_p, sin_p))
        ks.append(_rms(kv[:, h * MLA_NOPE:(h + 1) * MLA_NOPE], kn_nope))
        ks.append(k_pe)
    return jnp.concatenate(qs, axis=1), jnp.concatenate(ks, axis=1), kv[:, MLA_HEADS * MLA_NOPE:]


def _mla_prep_opds(proj, cos_p, sin_p, params, tm, grads):
    g = (lambda k: k) if grads else (lambda k: None)
    c0 = 4 * HG_WIDTH
    return ([rows(proj, tm, g('blk'), col=c0 // MLA_Q_RANK, width=MLA_Q_RANK),
             rows(proj, tm, g('blk'), col=(c0 + MLA_Q_RANK) // 128, width=128),
             rows(proj, tm, g('blk'), col=(c0 + MLA_Q_RANK) // 128 + 1, width=128),
             rows(cos_p, tm), rows(sin_p, tm)] + [full(p, g('acc')) for p in params])


def mla_prep_fwd(proj, cos_p, sin_p, params, tm):
    L = proj.shape[0]
    W = MLA_HEADS * MLA_DK
    rb = lambda w: (tm, w)
    outs = [((L, W), MXU_DTYPE, rb(W), lambda i: (i, 0)), ((L, W), MXU_DTYPE, rb(W), lambda i: (i, 0)),
            ((L, MLA_HEADS * MLA_V), MXU_DTYPE, rb(MLA_HEADS * MLA_V), lambda i: (i, 0))]
    return blocked_fwd(_mla_prep, _mla_prep_opds(proj, cos_p, sin_p, params, tm, False), outs, L // tm, "mla_prep_fwd")


def mla_prep_bwd(proj, cos_p, sin_p, params, dq, dk, dv, tm):
    L = proj.shape[0]
    return blocked_bwd(_mla_prep, _mla_prep_opds(proj, cos_p, sin_p, params, tm, True),
                       [rows(dq, tm), rows(dk, tm), rows(dv, tm)], L // tm, "mla_prep_bwd")


def _scores(q, k, scale, shift=None):
    s = _dot(q, k, _NT) * scale
    if shift is None:
        return s
    row = lax.broadcasted_iota(jnp.int32, s.shape, 0)
    col = lax.broadcasted_iota(jnp.int32, s.shape, 1)
    return jnp.where(col <= row + shift, s, -jnp.inf)


ATTN_WIDE = 2


def attn_fwd(q, k, v, scale, t):
    L = q.shape[0]
    tq = ATTN_WIDE * t

    def body(q_ref, k_ref, v_ref, o_ref, lse_ref):
        i = pl.program_id(1)
        qb = q_ref[...]

        def step(j, carry, shift=None):
            m, l, acc = carry
            kj = k_ref[pl.ds(pl.multiple_of(j * t, t), t), :]
            vj = v_ref[pl.ds(pl.multiple_of(j * t, t), t), :]
            s = _scores(qb, kj, scale, shift)
            m_new = jnp.maximum(m, jnp.max(s, axis=-1, keepdims=True))
            p = jnp.exp(s - m_new)
            alpha = jnp.exp(m - m_new)
            return m_new, alpha * l + jnp.sum(p, axis=-1, keepdims=True), alpha * acc + _dot(p, vj, _NN)

        carry = (jnp.full((tq, 1), -jnp.inf, F32), jnp.zeros((tq, 1), F32), jnp.zeros((tq, MLA_V), F32))
        carry = lax.fori_loop(0, ATTN_WIDE * i, step, carry)
        for d in range(ATTN_WIDE):
            carry = step(ATTN_WIDE * i + d, carry, -d * t)
        m, l, acc = carry
        o_ref[...] = acc / l
        lse_ref[...] = jnp.broadcast_to(m + jnp.log(l), lse_ref.shape)

    hspec = lambda rows_, w: pl.BlockSpec((rows_, w), lambda h, i: (0, h))
    bspec = lambda w: pl.BlockSpec((tq, w), lambda h, i: (i, h))
    return pl.pallas_call(
        body, name="attn_fwd", grid=(MLA_HEADS, L // tq),
        in_specs=[bspec(MLA_DK), hspec(L, MLA_DK), hspec(L, MLA_V)],
        out_specs=[bspec(MLA_V), bspec(MLA_V)],
        out_shape=[jax.ShapeDtypeStruct((L, MLA_HEADS * MLA_V), F32)] * 2,
        compiler_params=_cparams(("parallel", "parallel")),
    )(q, k, v)


def attn_bwd_dq(q, k, v, o, lse, do, scale, t):
    L = q.shape[0]
    tq = ATTN_WIDE * t

    def body(q_ref, k_ref, v_ref, o_ref, lse_ref, do_ref, dq_ref):
        i = pl.program_id(1)
        qb, dob = q_ref[...], do_ref[...]
        delta = jnp.sum(dob * o_ref[...], axis=-1, keepdims=True)
        lse_c = jnp.max(lse_ref[...], axis=-1, keepdims=True)

        def step(j, dq, shift=None):
            kj = k_ref[pl.ds(pl.multiple_of(j * t, t), t), :]
            vj = v_ref[pl.ds(pl.multiple_of(j * t, t), t), :]
            p = jnp.exp(_scores(qb, kj, scale, shift) - lse_c)
            ds = p * (_dot(dob, vj, _NT) - delta) * scale
            return dq + _dot(ds, kj, _NN)

        dq = lax.fori_loop(0, ATTN_WIDE * i, step, jnp.zeros((tq, MLA_DK), F32))
        for d in range(ATTN_WIDE):
            dq = step(ATTN_WIDE * i + d, dq, -d * t)
        dq_ref[...] = dq

    hspec = lambda w: pl.BlockSpec((L, w), lambda h, i: (0, h))
    bspec = lambda w: pl.BlockSpec((tq, w), lambda h, i: (i, h))
    return pl.pallas_call(
        body, name="attn_bwd_dq", grid=(MLA_HEADS, L // tq),
        in_specs=[bspec(MLA_DK), hspec(MLA_DK), hspec(MLA_V), bspec(MLA_V), bspec(MLA_V), bspec(MLA_V)],
        out_specs=bspec(MLA_DK),
        out_shape=jax.ShapeDtypeStruct((L, MLA_HEADS * MLA_DK), F32),
        compiler_params=_cparams(("parallel", "parallel")),
    )(q, k, v, o, lse, do)


def attn_bwd_dkv(q, k, v, o, lse, do, scale, t):
    L = q.shape[0]
    tk = ATTN_WIDE * t

    def body(q_ref, k_ref, v_ref, o_ref, lse_ref, do_ref, dk_ref, dv_ref):
        j = pl.program_id(1)
        kb, vb = k_ref[...], v_ref[...]

        def step(i, carry, shift=None):
            dk, dv = carry
            r = pl.ds(pl.multiple_of(i * t, t), t)
            qi, doi = q_ref[r, :], do_ref[r, :]
            delta = jnp.sum(doi * o_ref[r, :], axis=-1, keepdims=True)
            lse_c = jnp.max(lse_ref[r, :], axis=-1, keepdims=True)
            p = jnp.exp(_scores(qi, kb, scale, shift) - lse_c)
            ds = p * (_dot(doi, vb, _NT) - delta) * scale
            return dk + _dot(ds, qi, _TN), dv + _dot(p, doi, _TN)

        carry = (jnp.zeros((tk, MLA_DK), F32), jnp.zeros((tk, MLA_V), F32))
        for d in range(ATTN_WIDE):
            carry = step(ATTN_WIDE * j + d, carry, d * t)
        dk, dv = lax.fori_loop(ATTN_WIDE * (j + 1), L // t, step, carry)
        dk_ref[...] = dk
        dv_ref[...] = dv

    hspec = lambda w: pl.BlockSpec((L, w), lambda h, j: (0, h))
    bspec = lambda w: pl.BlockSpec((tk, w), lambda h, j: (j, h))
    return pl.pallas_call(
        body, name="attn_bwd_dkv", grid=(MLA_HEADS, L // tk),
        in_specs=[hspec(MLA_DK), bspec(MLA_DK), bspec(MLA_V), hspec(MLA_V), hspec(MLA_V), hspec(MLA_V)],
        out_specs=[bspec(MLA_DK), bspec(MLA_V)],
        out_shape=[jax.ShapeDtypeStruct((L, MLA_HEADS * MLA_DK), F32), jax.ShapeDtypeStruct((L, MLA_HEADS * MLA_V), F32)],
        compiler_params=_cparams(("parallel", "parallel")),
    )(q, k, v, o, lse, do)


S5_LANES = S5_GB * S5_STATE


def _cmul(ar, ai, br, bi):
    return ar * br - ai * bi, ar * bi + ai * br


def _a_powers(ar, ai, reverse):
    a2 = _cmul(ar, ai, ar, ai)
    a4 = _cmul(*a2, *a2)
    row = lax.broadcasted_iota(jnp.int32, (8, ar.shape[1]), 0)
    e = (8 - row) if reverse else (row + 1)
    tr, ti = jnp.ones((8, ar.shape[1]), F32), jnp.zeros((8, ar.shape[1]), F32)
    for bit, (pr, pi) in ((1, (ar, ai)), (2, a2), (4, a4), (8, _cmul(*a4, *a4))):
        nr, ni = _cmul(tr, ti, pr, pi)
        sel = (e & bit) != 0
        tr, ti = jnp.where(sel, nr, tr), jnp.where(sel, ni, ti)
    pows = []
    for d, (pr, pi) in zip((1, 2, 4), ((ar, ai), a2, a4)):
        keep = (row < 8 - d) if reverse else (row >= d)
        pows.append((jnp.where(keep, pr, 0.0), jnp.where(keep, pi, 0.0)))
    return pows, (tr, ti)


def _scan8(xr, xi, pows, table, cr, ci, reverse):
    for d, (pr, pi) in zip((1, 2, 4), pows):
        shift = 8 - d if reverse else d
        mr, mi = _cmul(pr, pi, pltpu.roll(xr, shift, 0), pltpu.roll(xi, shift, 0))
        xr, xi = xr + mr, xi + mi
    mr, mi = _cmul(table[0], table[1], cr, ci)
    return xr + mr, xi + mi


def _row_of(x, r):
    row = lax.broadcasted_iota(jnp.int32, x.shape, 0)
    return jnp.sum(jnp.where(row == r, x, 0.0), axis=0, keepdims=True)


def _s5_scan_fwd(h_re, h_im, ar, ai, L):
    pows, table = _a_powers(ar, ai, False)

    def step(i, carry):
        r = pl.ds(pl.multiple_of(i * 8, 8), 8)
        xr, xi = _scan8(h_re[r, :], h_im[r, :], pows, table, carry[0], carry[1], False)
        h_re[r, :] = xr
        h_im[r, :] = xi
        return xr[7:8, :], xi[7:8, :]

    z = jnp.zeros((1, ar.shape[1]), F32)
    lax.fori_loop(0, L // 8, step, (z, z))


def _s5_specs(L):
    return [pl.BlockSpec((L, 128), lambda g: (0, g)),
            pl.BlockSpec((1, 128, S5_LANES), lambda g: (g, 0, 0)), pl.BlockSpec((1, 128, S5_LANES), lambda g: (g, 0, 0)),
            pl.BlockSpec((1, 1, S5_LANES), lambda g: (g, 0, 0)), pl.BlockSpec((1, 1, S5_LANES), lambda g: (g, 0, 0)),
            pl.BlockSpec((1, S5_LANES, 128), lambda g: (g, 0, 0)), pl.BlockSpec((1, S5_LANES, 128), lambda g: (g, 0, 0))]


def s5_fwd(u, w_re, w_im, a_re, a_im, c_re, c_im):
    L, D = u.shape

    def body(u_ref, wr, wi, ar, ai, cr, ci, y_ref, h_re, h_im):
        ub = u_ref[...]
        h_re[...] = _dot(ub, wr[0], _NN)
        h_im[...] = _dot(ub, wi[0], _NN)
        _s5_scan_fwd(h_re, h_im, ar[0], ai[0], L)
        y_ref[...] = _dot(h_re[...], cr[0], _NN) - _dot(h_im[...], ci[0], _NN)

    return pl.pallas_call(
        body, name="s5_fwd", grid=(D // 128,),
        in_specs=_s5_specs(L), out_specs=pl.BlockSpec((L, 128), lambda g: (0, g)),
        out_shape=jax.ShapeDtypeStruct((L, D), F32),
        scratch_shapes=[pltpu.VMEM((L, S5_LANES), F32), pltpu.VMEM((L, S5_LANES), F32)],
        compiler_params=_cparams(("parallel",)),
    )(u, w_re, w_im, a_re, a_im, c_re, c_im)


def s5_bwd(u, w_re, w_im, a_re, a_im, c_re, c_im, dy, tc):
    L, D = u.shape
    nch = L // tc

    def body(u_ref, wr, wi, ar_ref, ai_ref, cr, ci, dy_ref, du_ref, dwr, dwi, dar, dai, dcr, dci, h_re, h_im, g_re, g_im):
        ar, ai = ar_ref[0], ai_ref[0]
        ub = u_ref[...]
        h_re[...] = _dot(ub, wr[0], _NN)
        h_im[...] = _dot(ub, wi[0], _NN)
        _s5_scan_fwd(h_re, h_im, ar, ai, L)
        dyb = dy_ref[...]
        dcr[0] = _dot(h_re[...], dyb, _TN)
        dci[0] = -_dot(h_im[...], dyb, _TN)
        pows, table = _a_powers(ar, -ai, True)
        dwr[0] = jnp.zeros((128, S5_LANES), F32)
        dwi[0] = jnp.zeros((128, S5_LANES), F32)
        z1 = jnp.zeros((1, S5_LANES), F32)
        z8 = jnp.zeros((8, S5_LANES), F32)

        def chunk(cc, carry):
            c0 = pl.multiple_of((nch - 1 - cc) * tc, tc)
            rows_c = pl.ds(c0, tc)
            dyc = dy_ref[rows_c, :]
            g_re[...] = _dot(dyc, cr[0], _NT)
            g_im[...] = -_dot(dyc, ci[0], _NT)

            def step(ii, cy):
                gr_c, gi_c, acc_r, acc_i = cy
                i8 = pl.multiple_of((tc // 8 - 1 - ii) * 8, 8)
                rl = pl.ds(i8, 8)
                xr, xi = _scan8(g_re[rl, :], g_im[rl, :], pows, table, gr_c, gi_c, True)
                g_re[rl, :] = xr
                g_im[rl, :] = xi
                t0 = c0 + i8
                hb_r, hb_i = h_re[pl.ds(t0, 8), :], h_im[pl.ds(t0, 8), :]
                tp = pl.multiple_of(jnp.maximum(t0 - 8, 0), 8)
                first = (t0 > 0).astype(F32)
                pr = h_re[pl.ds(tp, 8), :][7:8, :] * first
                pi = h_im[pl.ds(tp, 8), :][7:8, :] * first
                row = lax.broadcasted_iota(jnp.int32, xr.shape, 0)
                hp_r = jnp.where(row == 0, pr, pltpu.roll(hb_r, 1, 0))
                hp_i = jnp.where(row == 0, pi, pltpu.roll(hb_i, 1, 0))
                return (xr[0:1, :], xi[0:1, :],
                        acc_r + xr * hp_r + xi * hp_i, acc_i + xi * hp_r - xr * hp_i)

            cy = lax.fori_loop(0, tc // 8, step, carry)
            uc = u_ref[rows_c, :]
            gr, gi = g_re[...], g_im[...]
            du_ref[rows_c, :] = _dot(gr, wr[0], _NT) + _dot(gi, wi[0], _NT)
            dwr[0] += _dot(uc, gr, _TN)
            dwi[0] += _dot(uc, gi, _TN)
            return cy

        _, _, acc_r, acc_i = lax.fori_loop(0, nch, chunk, (z1, z1, z8, z8))
        dar[0] = jnp.sum(acc_r, axis=0, keepdims=True)
        dai[0] = jnp.sum(acc_i, axis=0, keepdims=True)

    specs = _s5_specs(L)
    return pl.pallas_call(
        body, name="s5_bwd", grid=(D // 128,),
        in_specs=specs + [pl.BlockSpec((L, 128), lambda g: (0, g))],
        out_specs=[pl.BlockSpec((L, 128), lambda g: (0, g))] + specs[1:],
        out_shape=[jax.ShapeDtypeStruct((L, D), F32)] + [jax.ShapeDtypeStruct(x.shape, F32)
                                                        for x in (w_re, w_im, a_re, a_im, c_re, c_im)],
        scratch_shapes=[pltpu.VMEM((L, S5_LANES), F32), pltpu.VMEM((L, S5_LANES), F32),
                        pltpu.VMEM((tc, S5_LANES), F32), pltpu.VMEM((tc, S5_LANES), F32)],
        compiler_params=_cparams(("parallel",)),
    )(u, w_re, w_im, a_re, a_im, c_re, c_im, dy)


def _s5_discretize(lr, li, ldt, br, bi):
    dt = jnp.exp(ldt)
    mag = jnp.exp(lr * dt)
    ar, ai = mag * jnp.cos(li * dt), mag * jnp.sin(li * dt)
    den = lr * lr + li * li
    zr = ((ar - 1.0) * lr + ai * li) / den
    zi = (ai * lr - (ar - 1.0) * li) / den
    p = lax.broadcasted_iota(jnp.int32, (S5_STATE, S5_STATE * S5_GROUP), 0)
    c = lax.broadcasted_iota(jnp.int32, (S5_STATE, S5_STATE * S5_GROUP), 1)
    rep = (c // S5_GROUP == p).astype(F32)
    zr, zi = _dot(zr, rep, _NN, HI), _dot(zi, rep, _NN, HI)
    return ar, ai, zr * br - zi * bi, zr * bi + zi * br


def _conv_shift(x, d):
    row = lax.broadcasted_iota(jnp.int32, x.shape, 0)
    return jnp.where(row >= d, pltpu.roll(x, d, 0), 0.0)


def _conv_unshift(x, d):
    n = x.shape[0]
    row = lax.broadcasted_iota(jnp.int32, x.shape, 0)
    return jnp.where(row < n - d, pltpu.roll(x, n - d, 0), 0.0)


@functools.partial(jax.custom_vjp, nondiff_argnums=(1,))
def _shift_rows(x, d):
    return _conv_shift(x, d)


_shift_rows.defvjp(lambda x, d: (_conv_shift(x, d), None), lambda d, _, g: (_conv_unshift(g, d),))


def _conv_gate(ug, uv, wg0, wg1, wg2, wv0, wv1, wv2, bg, bv):
    def conv(u, w0, w1, w2, b):
        return u * w2 + _shift_rows(u, 1) * w1 + _shift_rows(u, 2) * w0 + b
    return (jax.nn.silu(conv(ug, wg0, wg1, wg2, bg)) * conv(uv, wv0, wv1, wv2, bv),)


def _rms_fn(x, gain):
    return (_rms(x, gain),)


def _softmax_rows(s):
    e = jnp.exp(s - lax.stop_gradient(jnp.max(s, axis=-1, keepdims=True)))
    return e / jnp.sum(e, axis=-1, keepdims=True)


def _xa_core(qp, k, v, q_gain):
    dh = qp.shape[1] // XA_HEADS
    outs = []
    for h in range(XA_HEADS):
        sl = slice(h * dh, (h + 1) * dh)
        p = _softmax_rows(_dot(_rms(qp[:, sl], q_gain), k[:, sl], _NT) * (dh ** -0.5))
        outs.append(_dot(p, v[:, sl], _NN))
    return (jnp.concatenate(outs, axis=1),)


def _mem_kv(mem, mem_gain, wk, wv, k_gain):
    m = _rms(mem, mem_gain)
    kp = _dot(m, wk, _NN)
    dh = kp.shape[1] // XA_HEADS
    k = jnp.concatenate([_rms(kp[:, h * dh:(h + 1) * dh], k_gain) for h in range(XA_HEADS)], axis=1)
    return k, _dot(m, wv, _NN)


def _s5_post(y, u, d):
    return (jax.nn.gelu(y + d * u),)


def _glu(a, b):
    return (a * jax.nn.sigmoid(b),)


def _lb_first(logits):
    e = jnp.exp(logits - lax.stop_gradient(jnp.max(logits, axis=0, keepdims=True)))
    return (_row_of(e, 0) / jnp.sum(e, axis=0, keepdims=True),)


def _loss_fn(y, t):
    e = y - t
    part = 0.5 * jnp.sum(e * e) / y.shape[1]
    return e * (1.0 / y.shape[1]), jnp.full((8, 128), part / (8 * 128), F32)


def _out(shape, dtype, tm):
    return (shape, dtype, (tm, shape[1]), lambda i: (i, 0))


def rms_fwd(h, gain, tm, dtype):
    return blocked_fwd(_rms_fn, [rows(h, tm), full(gain)], [_out(h.shape, dtype, tm)], h.shape[0] // tm, "rms_fwd")[0]


def rms_bwd(h, gain, dy, tm, residual):
    return blocked_bwd(_rms_fn, [rows(h, tm, 'blk'), full(gain, 'acc')], [rows(dy, tm)], h.shape[0] // tm, "rms_bwd",
                       plus=rows(residual, tm))


def adamw(w, g, m, v, name):
    R = w.shape[0]
    tm = _tile(R, 256)
    assert g.shape == w.shape == m.shape == v.shape, (name, w.shape, g.shape)

    def body(w_ref, g_ref, m_ref, v_ref, d_ref, nm_ref, nv_ref):
        g_ = g_ref[...]
        m_ = ADAM_B1 * m_ref[...] + (1.0 - ADAM_B1) * g_
        v_ = ADAM_B2 * v_ref[...] + (1.0 - ADAM_B2) * jnp.square(g_)
        m_hat = m_ / (1.0 - ADAM_B1 ** ADAM_STEP)
        v_hat = v_ / (1.0 - ADAM_B2 ** ADAM_STEP)
        d_ref[...] = -ADAM_LR * (m_hat / (jnp.sqrt(v_hat) + ADAM_EPS) + ADAM_WD * w_ref[...])
        nm_ref[...] = m_
        nv_ref[...] = v_

    spec = pl.BlockSpec((tm, w.shape[1]), lambda i: (i, 0))
    return pl.pallas_call(
        body, name=name, grid=(R // tm,), in_specs=[spec] * 4, out_specs=[spec] * 3,
        out_shape=[jax.ShapeDtypeStruct(w.shape, F32)] * 3, compiler_params=_cparams(("parallel",)),
    )(w, g, m, v)


def add2(x, y, name, out_dtype=F32):
    shape = x.shape
    x, y = x.reshape(-1, shape[-1]), y.reshape(-1, shape[-1])
    R, C = x.shape
    tm = _tile(R, 256)

    def body(x_ref, y_ref, o_ref):
        o_ref[...] = (x_ref[...] + y_ref[...]).astype(o_ref.dtype)

    spec = pl.BlockSpec((tm, C), lambda i: (i, 0))
    return pl.pallas_call(
        body, name=name, grid=(R // tm,), in_specs=[spec, spec], out_specs=spec,
        out_shape=jax.ShapeDtypeStruct((R, C), out_dtype), compiler_params=_cparams(("parallel",)),
    )(x, y).reshape(shape)


def add_chips(own, got, name):
    n, R, C = got.shape
    tm = _tile(R, 256)

    def body(*refs):
        acc = refs[0][...].astype(F32)
        for r in refs[1:-1]:
            acc = acc + r[...].astype(F32)
        refs[-1][...] = acc

    return pl.pallas_call(
        body, name=name, grid=(R // tm,),
        in_specs=[pl.BlockSpec((tm, C), lambda i: (i, 0))] + [pl.BlockSpec((None, tm, C), lambda i, j=j: (j, i, 0))
                                                            for j in range(n)],
        out_specs=pl.BlockSpec((tm, C), lambda i: (i, 0)),
        out_shape=jax.ShapeDtypeStruct((R, C), F32), compiler_params=_cparams(("parallel",)),
    )(own, *([got] * n))


_HBM = pl.BlockSpec(memory_space=pltpu.HBM)
N_CHIPS = 4


def _my_place():
    return lax.axis_index("x"), lax.axis_index("y"), lax.axis_index("c")


def _window(ref, axis, start, size):
    idx = [slice(None)] * len(ref.shape)
    idx[axis] = pl.ds(start, size)
    return ref.at[tuple(idx)]


def _comm_call(body, name, xs, out_shapes, n_remote, n_local):
    return pl.pallas_call(
        body, name=name, in_specs=[_HBM] * len(xs), out_specs=[_HBM] * len(out_shapes), out_shape=out_shapes,
        scratch_shapes=[pltpu.SemaphoreType.DMA((n_remote,)), pltpu.SemaphoreType.DMA((n_remote,)),
                        pltpu.SemaphoreType.DMA((max(n_local, 1),))],
        compiler_params=pltpu.CompilerParams(has_side_effects=True),
    )(*xs)


def _run(copies):
    for cp in copies:
        cp.start()
    for cp in copies:
        cp.wait()


def _other_chips(mx, my):
    return [(mx ^ (j >> 1), my ^ (j & 1)) for j in (1, 2, 3)]


def chip_gather(xs, axes, name):
    n = len(xs)
    shapes, final = [], []
    for x, ax in zip(xs, axes):
        s = list(x.shape)
        if ax is None:
            shapes.append([N_CHIPS] + s)
            final.append(shapes[-1])
        elif ax < x.ndim - 1:
            shapes.append(s[:ax] + [N_CHIPS] + s[ax:])
            final.append(s[:ax] + [N_CHIPS * s[ax]] + s[ax + 1:])
        else:
            assert s[ax] % 128 == 0, (name, s)
            shapes.append(s[:ax] + [N_CHIPS * s[ax]])
            final.append(shapes[-1])

    def body(*refs):
        x_refs, o_refs = refs[:n], refs[n:2 * n]
        send_sems, recv_sems, local_sems = refs[2 * n:]
        mx, my, mc = _my_place()
        q = 2 * mx + my
        copies = []
        for i, (x_ref, o_ref, ax) in enumerate(zip(x_refs, o_refs, axes)):
            if ax is None or ax < len(x_ref.shape) - 1:
                dst = o_ref.at[(slice(None),) * (ax or 0) + (q,)]
            else:
                dst = _window(o_ref, ax, q * x_ref.shape[ax], x_ref.shape[ax])
            copies.append(pltpu.make_async_copy(x_ref, dst, local_sems.at[i]))
            for j, (tx, ty) in enumerate(_other_chips(mx, my)):
                copies.append(pltpu.make_async_remote_copy(
                    src_ref=x_ref, dst_ref=dst, send_sem=send_sems.at[3 * i + j], recv_sem=recv_sems.at[3 * i + j],
                    device_id=(tx, ty, mc), device_id_type=MESH))
        _run(copies)

    out_shapes = [jax.ShapeDtypeStruct(tuple(s), x.dtype) for s, x in zip(shapes, xs)]
    return [o.reshape(f) for o, f in zip(_comm_call(body, name, xs, out_shapes, 3 * n, n), final)]


def pair_swap(xs, name, halves):
    n = len(xs)
    shapes = [jax.ShapeDtypeStruct(x.shape[1:] if halves else x.shape, x.dtype) for x in xs]

    def body(*refs):
        x_refs, o_refs = refs[:n], refs[n:2 * n]
        send_sems, recv_sems, _ = refs[2 * n:]
        mx, my, mc = _my_place()
        _run([pltpu.make_async_remote_copy(
            src_ref=x_ref.at[1 - mc] if halves else x_ref, dst_ref=o_ref, send_sem=send_sems.at[i],
            recv_sem=recv_sems.at[i], device_id=(mx, my, 1 - mc), device_id_type=MESH)
            for i, (x_ref, o_ref) in enumerate(zip(x_refs, o_refs))])

    return _comm_call(body, name, xs, shapes, n, 0)


def chip_all_to_all(xs, name):
    n = len(xs)
    shapes = [jax.ShapeDtypeStruct((N_CHIPS - 1,) + x.shape[1:], x.dtype) for x in xs]

    def body(*refs):
        x_refs, o_refs = refs[:n], refs[n:2 * n]
        send_sems, recv_sems, _ = refs[2 * n:]
        mx, my, mc = _my_place()
        copies = []
        for i, (x_ref, o_ref) in enumerate(zip(x_refs, o_refs)):
            for j, (tx, ty) in enumerate(_other_chips(mx, my)):
                copies.append(pltpu.make_async_remote_copy(
                    src_ref=x_ref.at[2 * tx + ty], dst_ref=o_ref.at[j], send_sem=send_sems.at[3 * i + j],
                    recv_sem=recv_sems.at[3 * i + j], device_id=(tx, ty, mc), device_id_type=MESH))
        _run(copies)

    return _comm_call(body, name, xs, shapes, 3 * n, 0)


WEIGHTS = ['norm_mix', 'norm_xa', 'norm_mem', 'norm_ffn', 'xa_wq', 'xa_wk', 'xa_wv', 'xa_wo', 'xa_q_norm', 'xa_k_norm',
           'ffn_w_up', 'ffn_conv_w', 'ffn_conv_b', 'ffn_w_down', 'hg_lb_logits', 'mix_w_in', 'hg_out_norm',
           'mla_q_a_norm', 'mla_w_uq', 'mla_kv_a_norm', 'mla_w_ukv', 'mla_qn_nope', 'mla_qn_rope', 'mla_kn_nope',
           'mla_kn_rope', 'mix_w_out', 's5_lam_re', 's5_lam_im', 's5_log_dt', 's5_b_re', 's5_b_im', 's5_c_re',
           's5_c_im', 's5_d', 's5_w_glu_a', 's5_w_glu_b']
INPUTS = ['x', 'mem', 'positions'] + WEIGHTS + ['loss_target'] + ['m_' + n for n in WEIGHTS] + ['v_' + n for n in WEIGHTS]
SHARD_AXIS = {'xa_wq': 1, 'xa_wk': 1, 'xa_wv': 1, 'xa_wo': 1, 'ffn_w_up': 2, 'ffn_conv_w': 2, 'ffn_w_down': 1,
              'mix_w_in': 2, 'mla_w_uq': 2, 'mla_w_ukv': 2, 'mix_w_out': 1, 's5_d': 1, 's5_w_glu_a': 1, 's5_w_glu_b': 1}
BIG = ['xa_wq', 'xa_wk', 'xa_wv', 'xa_wo', 'ffn_w_up', 'ffn_w_down', 'mix_w_in', 'mix_w_out', 's5_w_glu_a', 's5_w_glu_b']
SMALL_SHARDED = [n for n in WEIGHTS if n in SHARD_AXIS and n not in BIG]
REPLICATED = [n for n in WEIGHTS if n not in SHARD_AXIS]
SMALL = SMALL_SHARDED + REPLICATED
PACK_W = 1024
ROW_MULT = 16
W_IN_SHARD = IN_WIDTH // N_CHIPS
W_IN_SHARD_PAD = 640


def _pack(flats, mult=ROW_MULT):
    flat = jnp.concatenate([f.reshape(-1) for f in flats])
    unit = mult * PACK_W
    n = -(-flat.shape[0] // unit) * unit
    return jnp.pad(flat, (0, n - flat.shape[0])).reshape(n // PACK_W, PACK_W)


def _unpack(packed, shapes):
    flat, out, o = packed.reshape(-1), [], 0
    for s in shapes:
        n = math.prod(s)
        out.append(flat[o:o + n].reshape(s))
        o += n
    return out


def _rope_pad(w):
    z = jnp.zeros(w.shape[:-1] + (MLA_ROPE // 2,), w.dtype)
    return jnp.concatenate([w[..., :MLA_ROPE // 2], z, w[..., MLA_ROPE // 2:], z], axis=-1)


def _rope_unpad(g):
    return jnp.concatenate([g[..., :MLA_ROPE // 2], g[..., 64:64 + MLA_ROPE // 2]], axis=-1)


def _blockdiag_in(bb):
    nb = bb.shape[0] // S5_GB
    t = bb.reshape(nb, S5_GB, S5_STATE, S5_GROUP).transpose(0, 1, 3, 2)
    return jnp.einsum('bgmp,gh->bgmhp', t, jnp.eye(S5_GB, dtype=bb.dtype)).reshape(nb, S5_GB * S5_GROUP, S5_LANES)


def _blockdiag_in_t(dw):
    nb = dw.shape[0]
    t = jnp.einsum('bgmhp,gh->bgmp', dw.reshape(nb, S5_GB, S5_GROUP, S5_GB, S5_STATE), jnp.eye(S5_GB, dtype=dw.dtype))
    return t.transpose(0, 1, 3, 2).reshape(nb * S5_GB, S5_STATE, S5_GROUP)


def _blockdiag_out(c):
    nb = c.shape[0] // S5_GB
    t = c.reshape(nb, S5_GB, S5_GROUP, S5_STATE).transpose(0, 1, 3, 2)
    return jnp.einsum('bgpm,gh->bgphm', t, jnp.eye(S5_GB, dtype=c.dtype)).reshape(nb, S5_LANES, S5_GB * S5_GROUP)


def _blockdiag_out_t(dc):
    nb = dc.shape[0]
    t = jnp.einsum('bgphm,gh->bgpm', dc.reshape(nb, S5_GB, S5_STATE, S5_GB, S5_GROUP), jnp.eye(S5_GB, dtype=dc.dtype))
    return t.transpose(0, 1, 3, 2).reshape(nb * S5_GB, S5_GROUP, S5_STATE)


def _gather_weights(P):
    xs = [P[n].astype(BF16) for n in BIG] + [_pack([P[n] for n in SMALL_SHARDED])]
    axes = [None if n == 'mix_w_in' else SHARD_AXIS[n] for n in BIG] + [None]
    got = chip_gather(xs, axes, "gather_weights")
    full_w = dict(zip(BIG, got[:-1]))
    full_w['mix_w_in'] = jnp.concatenate([full_w['mix_w_in'][q] for q in range(N_CHIPS)], axis=SHARD_AXIS['mix_w_in'])
    per_chip = [_unpack(got[-1][q], [P[n].shape for n in SMALL_SHARDED]) for q in range(N_CHIPS)]
    for i, n in enumerate(SMALL_SHARDED):
        full_w[n] = jnp.concatenate([per_chip[q][i] for q in range(N_CHIPS)], axis=SHARD_AXIS[n])
    return full_w


def _halves_first(x):
    return x.reshape(x.shape[0], 2, x.shape[1] // 2, x.shape[2]).transpose(1, 0, 2, 3)


def _reduce_and_update(GB, GS, P):
    mx, my, mc = _my_place()
    q = 2 * mx + my
    small = _pack([GS[n] for n in SMALL], 2 * N_CHIPS * ROW_MULT)
    xs = [GB[n] for n in BIG] + [_halves_first(small.reshape(N_CHIPS, -1, PACK_W))]
    theirs = pair_swap(xs, "grads_pair_swap", True)
    names = BIG + ['small']
    pair = [add2(lax.dynamic_index_in_dim(x, mc, 0, False), t, "grads_pair_sum_" + n, F32 if n == 'small' else BF16)
            for x, t, n in zip(xs, theirs, names)]
    got = chip_all_to_all(pair, "grads_chip_all_to_all")
    summed = [add_chips(lax.dynamic_index_in_dim(p, q, 0, False), g, "grads_chip_sum_" + n)
              for p, g, n in zip(pair, got, names)]
    other = pair_swap(summed, "grads_pair_join", False)
    joined = [lax.cond(mc == 0, lambda a, b: jnp.concatenate([a, b], axis=0), lambda a, b: jnp.concatenate([b, a], axis=0),
                       s, o) for s, o in zip(summed, other)]
    small_sum = chip_gather([joined[-1]], [0], "grads_small_gather")[0]
    g_small = dict(zip(SMALL, _unpack(small_sum, [GS[n].shape for n in SMALL])))
    for n in SMALL_SHARDED:
        s = P[n].shape[SHARD_AXIS[n]]
        g_small[n] = lax.dynamic_slice_in_dim(g_small[n], q * s, s, axis=SHARD_AXIS[n])

    grad, delta, new_m, new_v = {}, {}, {}, {}
    for n, g in zip(BIG, joined[:-1]):
        shape = P[n].shape
        if n == 'mix_w_in':
            g = g[:, :W_IN_SHARD]
        two_d = (g.shape[0], shape[-1])
        d, m_, v_ = adamw(P[n].reshape(two_d), g, P['m_' + n].reshape(two_d), P['v_' + n].reshape(two_d), "adamw_" + n)
        grad[n], delta[n], new_m[n], new_v[n] = (t.reshape(shape) for t in (g, d, m_, v_))
    packed = lambda prefix: _pack([P[prefix + n] for n in SMALL])
    d, m_, v_ = adamw(packed(''), _pack([g_small[n] for n in SMALL]), packed('m_'), packed('v_'), "adamw_small")
    shapes = [P[n].shape for n in SMALL]
    grad.update(g_small)
    for out, pk in ((delta, d), (new_m, m_), (new_v, v_)):
        out.update(zip(SMALL, _unpack(pk, shapes)))
    return grad, delta, new_m, new_v


def _row(v):
    return v.reshape(1, -1)


def _xattn_fwd(h, mem, W, lyr, tm):
    g_xa, g_mem = _row(W['norm_xa'][lyr]), _row(W['norm_mem'][lyr])
    g_q, g_k = _row(W['xa_q_norm'][lyr]), _row(W['xa_k_norm'][lyr])
    wq, wk, wv, wo = (W[n][lyr] for n in ('xa_wq', 'xa_wk', 'xa_wv', 'xa_wo'))
    L, D = h.shape
    M = mem.shape[0]
    hx = rms_fwd(h, g_xa, tm, MXU_DTYPE)
    qp = matmul(hx, wq, name="xa_q")
    kv_opds = [full(mem), full(g_mem), full(wk), full(wv), full(g_k)]
    k, v = blocked_fwd(_mem_kv, kv_opds, [((M, D), F32, (M, D), lambda i: (0, 0))] * 2, 1, "xa_mem_kv")
    o = blocked_fwd(_xa_core, [rows(qp, tm), full(k), full(v), full(g_q)], [_out((L, D), MXU_DTYPE, tm)], L // tm,
                    "xa_core")[0]
    out = matmul(o, wo, add=h, name="xa_o")
    return out, (h, hx, qp, k, v, o)


def _xattn_bwd(dout, saved, mem, W, lyr, tm):
    h, hx, qp, k, v, o = saved
    g_xa, g_mem = _row(W['norm_xa'][lyr]), _row(W['norm_mem'][lyr])
    g_q, g_k = _row(W['xa_q_norm'][lyr]), _row(W['xa_k_norm'][lyr])
    wq, wk, wv, wo = (W[n][lyr] for n in ('xa_wq', 'xa_wk', 'xa_wv', 'xa_wo'))
    L = h.shape[0]
    do = matmul(dout, wo, "nt", name="xa_do")
    d_wo = matmul(o, dout, "tn", name="xa_dwo")
    dqp, dk, dv, d_gq = blocked_bwd(_xa_core, [rows(qp, tm, 'blk'), full(k, 'acc'), full(v, 'acc'), full(g_q, 'acc')],
                                    [rows(do, tm)], L // tm, "xa_core_bwd")
    d_wq = matmul(hx, dqp, "tn", name="xa_dwq")
    dhx = matmul(dqp, wq, "nt", name="xa_dhx")
    dh, d_gxa = rms_bwd(h, g_xa, dhx, tm, dout)
    d_gmem, d_wk, d_wv, d_gk = blocked_bwd(
        _mem_kv, [full(mem), full(g_mem, 'acc'), full(wk, 'acc'), full(wv, 'acc'), full(g_k, 'acc')],
        [full(dk), full(dv)], 1, "xa_mem_kv_bwd")
    by_chip = lambda g: g.reshape(N_CHIPS, g.shape[0] // N_CHIPS, g.shape[1])
    grads = {'norm_xa': d_gxa, 'norm_mem': d_gmem, 'xa_q_norm': d_gq, 'xa_k_norm': d_gk,
             'xa_wq': by_chip(d_wq), 'xa_wk': by_chip(d_wk), 'xa_wv': by_chip(d_wv), 'xa_wo': by_chip(d_wo)}
    return dh,grads


def _conv_params(W, lyr):
    cw, cb = W['ffn_conv_w'][lyr], W['ffn_conv_b'][lyr]
    F = cw.shape[1] // 2
    return [cw[0:1, :F], cw[1:2, :F], cw[2:3, :F], cw[0:1, F:], cw[1:2, F:], cw[2:3, F:], _row(cb[:F]), _row(cb[F:])]


def _ffn_fwd(h, W, lyr, tm):
    L, D = h.shape
    w_up, w_down = W['ffn_w_up'][lyr], W['ffn_w_down'][lyr]
    F = w_down.shape[0]
    hf = rms_fwd(h, _row(W['norm_ffn'][lyr]), tm, MXU_DTYPE)
    ug = matmul(hf, w_up[:, :F], name="ffn_up_gate")
    uv = matmul(hf, w_up[:, F:], name="ffn_up_value")
    opds = [cols(ug, 128), cols(uv, 128)] + [cols(p, 128) for p in _conv_params(W, lyr)]
    a = blocked_fwd(_conv_gate, opds, [((L, F), MXU_DTYPE, (L, 128), lambda j: (0, j))], F // 128, "ffn_conv_gate")[0]
    out = matmul(a, w_down, add=h, name="ffn_down")
    return out, (h, hf, ug, uv, a)


def _ffn_bwd(dout, saved, W, lyr, tm):
    h, hf, ug, uv, a = saved
    w_up, w_down = W['ffn_w_up'][lyr], W['ffn_w_down'][lyr]
    F = w_down.shape[0]
    da = matmul(dout, w_down, "nt", name="ffn_da")
    d_wdown = matmul(a, dout, "tn", name="ffn_dwdown")
    opds = [cols(ug, 128, 'blk'), cols(uv, 128, 'blk')] + [cols(p, 128, 'blk') for p in _conv_params(W, lyr)]
    gs = blocked_bwd(_conv_gate, opds, [cols(da, 128)], F // 128, "ffn_conv_gate_bwd")
    dug, duv = gs[0], gs[1]
    d_cw = jnp.concatenate([jnp.concatenate(gs[2:5], axis=0), jnp.concatenate(gs[5:8], axis=0)], axis=1)
    d_cb = jnp.concatenate([gs[8], gs[9]], axis=1)[0]
    d_wup = jnp.concatenate([matmul(hf, dug, "tn", name="ffn_dwup_gate", col_blocks=N_CHIPS // 2),
                             matmul(hf, duv, "tn", name="ffn_dwup_value", col_blocks=N_CHIPS // 2)], axis=0)
    dhf = matmul(dug, w_up[:, :F], "nt", name="ffn_dhf_gate")
    dhf = matmul(duv, w_up[:, F:], "nt", add=dhf, name="ffn_dhf_value")
    dh, d_g = rms_bwd(h, _row(W['norm_ffn'][lyr]), dhf, tm, dout)
    d_wdown = d_wdown.reshape(N_CHIPS, F // N_CHIPS, d_wdown.shape[1])
    return dh,{'norm_ffn': d_g, 'ffn_w_up': d_wup, 'ffn_conv_w': d_cw, 'ffn_conv_b': d_cb, 'ffn_w_down': d_wdown}


def _mla_params(W):
    w_uq = W['mla_w_uq'][0].reshape(MLA_Q_RANK, MLA_HEADS, MLA_QK)
    w_uq = jnp.concatenate([w_uq[..., :MLA_NOPE], _rope_pad(w_uq[..., MLA_NOPE:])], axis=-1)
    w_ukv = W['mla_w_ukv'][0].reshape(MLA_KV_RANK, MLA_HEADS, MLA_NOPE + MLA_V)
    w_ukv = jnp.concatenate([w_ukv[..., :MLA_NOPE].reshape(MLA_KV_RANK, -1), w_ukv[..., MLA_NOPE:].reshape(MLA_KV_RANK, -1)],
                            axis=1)
    return [_row(W['mla_q_a_norm'][0]), w_uq.reshape(MLA_Q_RANK, MLA_HEADS * MLA_DK), _row(W['mla_kv_a_norm'][0]), w_ukv,
            _row(W['mla_qn_nope'][0]), _row(_rope_pad(W['mla_qn_rope'][0])), _row(W['mla_kn_nope'][0]),
            _row(_rope_pad(W['mla_kn_rope'][0]))]


def _w_in_padded(W):
    w = W['mix_w_in'][0]
    return jnp.concatenate([w[:, :IN_WIDTH - MLA_ROPE], _rope_pad(w[:, IN_WIDTH - MLA_ROPE:])], axis=1)


def _mixer0_fwd(h, W, cos_p, sin_p, tm):
    L = h.shape[0]
    t = min(256, L // ATTN_WIDE)
    hn = rms_fwd(h, _row(W['norm_mix'][0]), tm, MXU_DTYPE)
    proj = matmul(hn, _w_in_padded(W), name="mix_in")
    logits = W['hg_lb_logits']
    lb = blocked_fwd(_lb_first, [full(logits)], [((1, HG_WIDTH), F32, (1, HG_WIDTH), lambda i: (0, 0))], 1, "hg_lb")[0]
    gain = _row(W['hg_out_norm'][0])
    o_hg, states = hgrn2_fwd(proj, lb, gain)
    mp = _mla_params(W)
    q, k, v = mla_prep_fwd(proj, cos_p, sin_p, mp, tm)
    scale = MLA_QK ** -0.5
    o_mla, lse = attn_fwd(q, k, v, scale, t)
    w_out = W['mix_w_out'][0]
    out = matmul(o_hg, w_out[:HG_WIDTH], add=h, name="mix_out_hg")
    out = matmul(o_mla, w_out[HG_WIDTH:], add=out, name="mix_out_mla")
    return out, (h, hn, proj, lb, o_hg, states, q, k, v, o_mla, lse)


def _mixer0_bwd(dout, saved, W, cos_p, sin_p, tm):
    h, hn, proj, lb, o_hg, states, q, k, v, o_mla, lse = saved
    L = h.shape[0]
    t = min(256, L // ATTN_WIDE)
    scale = MLA_QK ** -0.5
    w_out = W['mix_w_out'][0]
    gain = _row(W['hg_out_norm'][0])
    do_hg = matmul(dout, w_out[:HG_WIDTH], "nt", name="mix_do_hg")
    do_mla = matmul(dout, w_out[HG_WIDTH:], "nt", name="mix_do_mla")
    d_wout = jnp.concatenate([matmul(o_hg, dout, "tn", name="mix_dwout_hg"), matmul(o_mla, dout, "tn", name="mix_dwout_mla")],
                             axis=0)
    dq = attn_bwd_dq(q, k, v, o_mla, lse, do_mla, scale, t)
    dk, dv = attn_bwd_dkv(q, k, v, o_mla, lse, do_mla, scale, t)
    mp = _mla_params(W)
    dcq, dckv, dkpe, d_qa, d_wuq, d_kva, d_wukv, d_qnn, d_qnr, d_knn, d_knr = mla_prep_bwd(proj, cos_p, sin_p, mp, dq, dk, dv, tm)
    d_hg, d_lb, d_gain = hgrn2_bwd(proj, lb, gain, states, do_hg)
    dproj = jnp.concatenate([d_hg, dcq, dckv, dkpe], axis=1)
    d_win = matmul(hn, dproj, "tn", name="mix_dwin")
    dhn = matmul(dproj, _w_in_padded(W), "nt", name="mix_dhn")
    dh, d_g = rms_bwd(h, _row(W['norm_mix'][0]), dhn, tm, dout)
    logits = W['hg_lb_logits']
    d_logits = blocked_bwd(_lb_first, [full(logits, 'acc')], [full(d_lb)], 1, "hg_lb_bwd")[0]
    d_wuq = d_wuq.reshape(MLA_Q_RANK, MLA_HEADS, MLA_DK)
    d_wuq = jnp.concatenate([d_wuq[..., :MLA_NOPE], _rope_unpad(d_wuq[..., MLA_NOPE:])], axis=-1)
    hw = MLA_HEADS * MLA_NOPE
    d_wukv = jnp.concatenate([d_wukv[:, :hw].reshape(MLA_KV_RANK, MLA_HEADS, MLA_NOPE),
                              d_wukv[:, hw:].reshape(MLA_KV_RANK, MLA_HEADS, MLA_V)], axis=-1)
    d_win = jnp.concatenate([d_win[:, :IN_WIDTH - MLA_ROPE], _rope_unpad(d_win[:, IN_WIDTH - MLA_ROPE:])], axis=1)
    d_win = d_win.reshape(d_win.shape[0], N_CHIPS, W_IN_SHARD).transpose(1, 0, 2)
    d_win = jnp.pad(d_win, ((0, 0), (0, 0), (0, W_IN_SHARD_PAD - W_IN_SHARD)))
    d_wout = d_wout.reshape(N_CHIPS, d_wout.shape[0] // N_CHIPS, d_wout.shape[1])
    grads = {'norm_mix': d_g, 'hg_lb_logits': d_logits, 'mix_w_in': d_win, 'hg_out_norm': d_gain,
             'mla_q_a_norm': d_qa, 'mla_w_uq': d_wuq.reshape(1, MLA_Q_RANK, -1), 'mla_kv_a_norm': d_kva,
             'mla_w_ukv': d_wukv.reshape(1, MLA_KV_RANK, -1), 'mla_qn_nope': d_qnn, 'mla_qn_rope': _rope_unpad(d_qnr),
             'mla_kn_nope': d_knn, 'mla_kn_rope': _rope_unpad(d_knr), 'mix_w_out': d_wout}
    return dh,grads


def _s5_inputs(W):
    G = W['s5_lam_re'].shape[1]
    return [W['s5_lam_re'][0], W['s5_lam_im'][0], W['s5_log_dt'][0].reshape(G, 1),
            W['s5_b_re'][0].reshape(G, -1), W['s5_b_im'][0].reshape(G, -1)]


def _mixer1_fwd(h, W, tm):
    L, D = h.shape
    u = rms_fwd(h, _row(W['norm_mix'][1]), tm, F32)
    di = _s5_inputs(W)
    G = di[0].shape[0]
    sq, wide = ((G, S5_STATE), F32, (G, S5_STATE), lambda i: (0, 0)), ((G, S5_STATE * S5_GROUP), F32, (G, S5_STATE * S5_GROUP), lambda i: (0, 0))
    ar, ai, bbr, bbi = blocked_fwd(_s5_discretize, [full(a) for a in di], [sq, sq, wide, wide], 1, "s5_discretize")
    nb = G // S5_GB
    core = (_blockdiag_in(bbr.reshape(G, S5_STATE, S5_GROUP)), _blockdiag_in(bbi.reshape(G, S5_STATE, S5_GROUP)),
            ar.reshape(nb, 1, S5_LANES), ai.reshape(nb, 1, S5_LANES),
            _blockdiag_out(W['s5_c_re'][0]), _blockdiag_out(W['s5_c_im'][0]))
    y = s5_fwd(u, *core)
    d = W['s5_d']
    y2 = blocked_fwd(_s5_post, [rows(y, tm), rows(u, tm), full(d)], [_out((L, D), MXU_DTYPE, tm)], L // tm, "s5_post")[0]
    w_ab = jnp.concatenate([W['s5_w_glu_a'][0], W['s5_w_glu_b'][0]], axis=1)
    ab = matmul(y2, w_ab, name="s5_glu_in")
    mix = blocked_fwd(_glu, [rows(ab, tm, col=0, width=D), rows(ab, tm, col=1, width=D)], [_out((L, D), F32, tm)], L // tm,
                      "s5_glu")[0]
    return h + mix, (h, u, core, y, y2, ab)


def _mixer1_bwd(dout, saved, W, tm):
    h, u, core, y, y2, ab = saved
    L, D = h.shape
    da, db = blocked_bwd(_glu, [rows(ab, tm, 'blk', col=0, width=D), rows(ab, tm, 'blk', col=1, width=D)], [rows(dout, tm)],
                         L // tm, "s5_glu_bwd")
    w_a, w_b = W['s5_w_glu_a'][0], W['s5_w_glu_b'][0]
    dy2 = matmul(da, w_a, "nt", name="s5_dy2_a")
    dy2 = matmul(db, w_b, "nt", add=dy2, name="s5_dy2_b")
    d_wa = matmul(y2, da, "tn", name="s5_dwa")
    d_wb = matmul(y2, db, "tn", name="s5_dwb")
    d = W['s5_d']
    dy, du_skip, d_d = blocked_bwd(_s5_post, [rows(y, tm, 'blk'), rows(u, tm, 'blk'), full(d, 'acc')], [rows(dy2, tm)], L // tm,
                                   "s5_post_bwd")
    du, dwr, dwi, dar, dai, dcr, dci = s5_bwd(u, *core, dy, min(256, L))
    di = _s5_inputs(W)
    G = di[0].shape[0]
    cts = [dar.reshape(G, S5_STATE), dai.reshape(G, S5_STATE), _blockdiag_in_t(dwr).reshape(G, -1), _blockdiag_in_t(dwi).reshape(G, -1)]
    d_lr, d_li, d_ldt, d_br, d_bi = blocked_bwd(_s5_discretize, [full(a, 'acc') for a in di], [full(c) for c in cts], 1,
                                                "s5_discretize_bwd")
    dh, d_g = rms_bwd(h, _row(W['norm_mix'][1]), du + du_skip, tm, dout)
    bshape = W['s5_b_re'].shape
    grads = {'norm_mix': d_g, 's5_lam_re': d_lr[None], 's5_lam_im': d_li[None], 's5_log_dt': d_ldt.reshape(1, G),
             's5_b_re': d_br.reshape(bshape), 's5_b_im': d_bi.reshape(bshape), 's5_c_re': _blockdiag_out_t(dcr)[None],
             's5_c_im': _blockdiag_out_t(dci)[None], 's5_d': d_d, 's5_w_glu_a': d_wa.reshape(N_CHIPS, -1, D), 's5_w_glu_b': d_wb.reshape(N_CHIPS, -1, D)}
    return dh,grads


def kernel(x, mem, positions, norm_mix, norm_xa, norm_mem, norm_ffn, xa_wq, xa_wk, xa_wv, xa_wo, xa_q_norm, xa_k_norm, ffn_w_up, ffn_conv_w, ffn_conv_b, ffn_w_down, hg_lb_logits, mix_w_in, hg_out_norm, mla_q_a_norm, mla_w_uq, mla_kv_a_norm, mla_w_ukv, mla_qn_nope, mla_qn_rope, mla_kn_nope, mla_kn_rope, mix_w_out, s5_lam_re, s5_lam_im, s5_log_dt, s5_b_re, s5_b_im, s5_c_re, s5_c_im, s5_d, s5_w_glu_a, s5_w_glu_b, loss_target, m_norm_mix, m_norm_xa, m_norm_mem, m_norm_ffn, m_xa_wq, m_xa_wk, m_xa_wv, m_xa_wo, m_xa_q_norm, m_xa_k_norm, m_ffn_w_up, m_ffn_conv_w, m_ffn_conv_b, m_ffn_w_down, m_hg_lb_logits, m_mix_w_in, m_hg_out_norm, m_mla_q_a_norm, m_mla_w_uq, m_mla_kv_a_norm, m_mla_w_ukv, m_mla_qn_nope, m_mla_qn_rope, m_mla_kn_nope, m_mla_kn_rope, m_mix_w_out, m_s5_lam_re, m_s5_lam_im, m_s5_log_dt, m_s5_b_re, m_s5_b_im, m_s5_c_re, m_s5_c_im, m_s5_d, m_s5_w_glu_a, m_s5_w_glu_b, v_norm_mix, v_norm_xa, v_norm_mem, v_norm_ffn, v_xa_wq, v_xa_wk, v_xa_wv, v_xa_wo, v_xa_q_norm, v_xa_k_norm, v_ffn_w_up, v_ffn_conv_w, v_ffn_conv_b, v_ffn_w_down, v_hg_lb_logits, v_mix_w_in, v_hg_out_norm, v_mla_q_a_norm, v_mla_w_uq, v_mla_kv_a_norm, v_mla_w_ukv, v_mla_qn_nope, v_mla_qn_rope, v_mla_kn_nope, v_mla_kn_rope, v_mix_w_out, v_s5_lam_re, v_s5_lam_im, v_s5_log_dt, v_s5_b_re, v_s5_b_im, v_s5_c_re, v_s5_c_im, v_s5_d, v_s5_w_glu_a, v_s5_w_glu_b):
    P = dict(locals())
    assert sorted(P) == sorted(INPUTS) and norm_mix.shape[0] == 2 and mix_w_in.shape[0] == 1
    x, mem, target = P['x'][0], P['mem'][0], P['loss_target'][0]
    L, D = x.shape
    tm = min(256, L)

    W = {n: P[n] for n in REPLICATED}
    W.update(_gather_weights(P))

    inv_freq = 1.0 / (ROPE_BASE ** (jnp.arange(0, MLA_ROPE, 2, dtype=F32) / MLA_ROPE))
    ang = P['positions'][0].astype(F32)[:, None] * inv_freq
    cos, sin, z = jnp.cos(ang), jnp.sin(ang), jnp.zeros_like(ang)
    cos_p = jnp.concatenate([cos, z, cos, z], axis=1)
    sin_p = jnp.concatenate([-sin, z, sin, z], axis=1)

    h, s_mix0 = _mixer0_fwd(x, W, cos_p, sin_p, tm)
    h, s_xa0 = _xattn_fwd(h, mem, W, 0, tm)
    h, s_ffn0 = _ffn_fwd(h, W, 0, tm)
    h, s_mix1 = _mixer1_fwd(h, W, tm)
    h, s_xa1 = _xattn_fwd(h, mem, W, 1, tm)
    h, s_ffn1 = _ffn_fwd(h, W, 1, tm)
    n = L // tm
    dh, parts = blocked_fwd(_loss_fn, [rows(h, tm), rows(target, tm)],
                            [_out((L, D), F32, tm), ((n * 8, 128), F32, (8, 128), lambda i: (i, 0))], n, "loss")
    loss = lax.psum(jnp.sum(parts), ("x", "y", "c"))

    layered = {}

    def collect(g, lyr):
        for k_, v_ in g.items():
            layered.setdefault(k_, {})[lyr] = v_

    dh, g = _ffn_bwd(dh, s_ffn1, W, 1, tm)
    collect(g, 1)
    dh, g = _xattn_bwd(dh, s_xa1, mem, W, 1, tm)
    collect(g, 1)
    dh, g = _mixer1_bwd(dh, s_mix1, W, tm)
    collect(g, 1)
    dh, g = _ffn_bwd(dh, s_ffn0, W, 0, tm)
    collect(g, 0)
    dh, g = _xattn_bwd(dh, s_xa0, mem, W, 0, tm)
    collect(g, 0)
    dx, g = _mixer0_bwd(dh, s_mix0, W, cos_p, sin_p, tm)
    collect(g, 0)

    GB, GS = {}, {}
    for name in WEIGHTS:
        by_layer = [layered[name][lyr] for lyr in sorted(layered[name])]
        if name in BIG:
            GB[name] = _halves_first(by_layer[0]) if len(by_layer) == 1 else jnp.stack(by_layer)
        else:
            full_shape = W[name].shape
            GS[name] = (by_layer[0].reshape(full_shape) if len(by_layer) == 1
                        else jnp.stack([g_.reshape(full_shape[1:]) for g_ in by_layer]))

    outs = _reduce_and_update(GB, GS, P)
    return (loss, dx[None], *[d[n_] for d in outs for n_ in WEIGHTS])
```

```python
import functools
import math

import jax
import jax.numpy as jnp
import numpy as np
from jax import lax
from jax.experimental import pallas as pl
from jax.experimental.pallas import tpu as pltpu

F32 = jnp.float32
BF16 = jnp.bfloat16
MXU_DTYPE = BF16
HI = lax.Precision.HIGHEST
V7X_VMEM_LIMIT_BYTES = 56 * 1024 * 1024
EPS = 1e-6
MESH = pl.DeviceIdType.MESH

HG_HEADS, HG_DIM = 4, 128
HG_WIDTH = HG_HEADS * HG_DIM
HG_SUB = 32
HG_BLOCK = 64
MLA_HEADS, MLA_Q_RANK, MLA_KV_RANK = 4, 256, 128
MLA_NOPE, MLA_ROPE, MLA_V = 128, 64, 128
MLA_QK = MLA_NOPE + MLA_ROPE
MLA_DK = 256
ROPE_BASE = 10000.0
IN_WIDTH = 4 * HG_WIDTH + MLA_Q_RANK + MLA_KV_RANK + MLA_ROPE
IN_PAD = 4 * HG_WIDTH + MLA_Q_RANK + MLA_KV_RANK + 128
S5_GROUP, S5_STATE = 16, 64
S5_GB = 8
DT_MIN, DT_MAX = 1e-3, 1e-1
XA_HEADS = 4
CONV_W = 3
ADAM_LR, ADAM_B1, ADAM_B2, ADAM_EPS, ADAM_WD, ADAM_STEP = 0.001, 0.9, 0.999, 1e-08, 0.01, 10


def _cparams(sem):
    return pltpu.CompilerParams(dimension_semantics=sem, vmem_limit_bytes=V7X_VMEM_LIMIT_BYTES)


class Opd:
    def __init__(self, arr, block, imap, grad=None, gshape=None, gimap=None):
        self.arr, self.block, self.imap, self.grad = arr, block, imap, grad
        self.gshape = arr.shape if gshape is None else gshape
        self.gimap = imap if gimap is None else gimap

    def spec(self):
        return pl.BlockSpec(self.block, self.imap)

    def gspec(self):
        return pl.BlockSpec(self.block, self.gimap)


def rows(arr, tm, grad=None, col=0, width=None):
    width = arr.shape[1] if width is None else width
    return Opd(arr, (tm, width), lambda i, c=col: (i, c), grad, (arr.shape[0], width), lambda i: (i, 0))


def cols(arr, tn, grad=None):
    return Opd(arr, (arr.shape[0], tn), lambda j: (0, j), grad)


def full(arr, grad=None):
    return Opd(arr, arr.shape, lambda i: (0, 0), grad)


def _load(ref):
    v = ref[...]
    return v.astype(F32) if jnp.issubdtype(v.dtype, jnp.floating) else v


def blocked_fwd(f, opds, outs, n, name):
    n_in = len(opds)

    def body(*refs):
        ys = f(*[_load(r) for r in refs[:n_in]])
        for r, y in zip(refs[n_in:], ys):
            r[...] = y.astype(r.dtype)

    res = pl.pallas_call(
        body, name=name, grid=(n,),
        in_specs=[o.spec() for o in opds],
        out_specs=[pl.BlockSpec(b, m) for (_, _, b, m) in outs],
        out_shape=[jax.ShapeDtypeStruct(s, d) for (s, d, _, _) in outs],
        compiler_params=_cparams(("parallel",)),
    )(*[o.arr for o in opds])
    return res


def blocked_bwd(f, opds, dys, n, name, plus=None):
    n_in, n_dy = len(opds), len(dys)
    diff = [i for i, o in enumerate(opds) if o.grad]
    extra = [] if plus is None else [plus]

    def body(*refs):
        vals = [_load(r) for r in refs[:n_in]]

        def fd(*dv):
            allv = list(vals)
            for i, v in zip(diff, dv):
                allv[i] = v
            return tuple(f(*allv))

        ys, vjp = jax.vjp(fd, *[vals[i] for i in diff])
        cts = tuple(_load(r).astype(y.dtype) for r, y in zip(refs[n_in:n_in + n_dy], ys))
        gs = list(vjp(cts))
        if extra:
            gs[0] = gs[0] + _load(refs[n_in + n_dy])
        for r, g, i in zip(refs[n_in + n_dy + len(extra):], gs, diff):
            if opds[i].grad == 'acc':
                @pl.when(pl.program_id(0) == 0)
                def _(r=r):
                    r[...] = jnp.zeros(r.shape, r.dtype)
                r[...] += g.astype(r.dtype)
            else:
                r[...] = g.astype(r.dtype)

    any_acc = any(opds[i].grad == 'acc' for i in diff)
    res = pl.pallas_call(
        body, name=name, grid=(n,),
        in_specs=[o.spec() for o in opds + dys + extra],
        out_specs=[opds[i].gspec() for i in diff],
        out_shape=[jax.ShapeDtypeStruct(opds[i].gshape, F32) for i in diff],
        compiler_params=_cparams(("arbitrary" if any_acc else "parallel",)),
    )(*[o.arr for o in opds + dys + extra])
    return res


def _tile(dim, want):
    for t in range(want - want % 16, 0, -16):
        if dim % t == 0:
            return t
    assert dim <= want, (dim, want)
    return dim


MATMUL_VMEM_BUDGET = 40 * 1024 * 1024
MATMUL_ROWS = 512


def _widest(N, fits):
    for t in range(N - N % 128, 0, -128):
        if N % t == 0 and fits(t):
            return t
    return N


def matmul(a, b, mode="nn", out_dtype=F32, add=None, name="matmul", col_blocks=None):
    sa, sb, so = a.dtype.itemsize, b.dtype.itemsize, jnp.dtype(out_dtype).itemsize
    has_add = add is not None
    if mode == "tn":
        (K, M), (K2, N) = a.shape, b.shape
        assert K == K2 and not has_add and out_dtype == F32, (a.shape, b.shape)
        tk = _tile(K, MATMUL_ROWS)
        tn = _widest(N, lambda t: 2 * (tk * M * sa + tk * t * sb + M * t * 4) <= MATMUL_VMEM_BUDGET)
        if col_blocks is not None:
            assert N % col_blocks == 0 and (N // col_blocks) % 128 == 0 and tn >= N // col_blocks, (N, col_blocks, tn)
            tn = N // col_blocks
            out_spec = pl.BlockSpec((None, M, tn), lambda j, k: (j, 0, 0))
            out_shape = jax.ShapeDtypeStruct((col_blocks, M, tn), F32)
        else:
            out_spec = pl.BlockSpec((M, tn), lambda j, k: (0, j))
            out_shape = jax.ShapeDtypeStruct((M, N), F32)

        def body(a_ref, b_ref, o_ref):
            r = lax.dot_general(a_ref[...].astype(MXU_DTYPE), b_ref[...].astype(MXU_DTYPE), ((_TN), ((), ())),
                                preferred_element_type=F32)

            @pl.when(pl.program_id(1) == 0)
            def _():
                o_ref[...] = r

            @pl.when(pl.program_id(1) > 0)
            def _():
                o_ref[...] += r

        return pl.pallas_call(
            body, name=name, grid=(N // tn, K // tk),
            in_specs=[pl.BlockSpec((tk, M), lambda j, k: (k, 0)), pl.BlockSpec((tk, tn), lambda j, k: (k, j))],
            out_specs=out_spec, out_shape=out_shape,
            compiler_params=_cparams(("parallel", "arbitrary")),
        )(a, b)

    (M, K) = a.shape
    N = b.shape[1] if mode == "nn" else b.shape[0]
    assert K == (b.shape[0] if mode == "nn" else b.shape[1]), (a.shape, b.shape, mode)
    tm = _tile(M, MATMUL_ROWS)
    tn = _widest(N, lambda t: 2 * (tm * K * sa + K * t * sb + tm * t * (so + 4 * has_add)) <= MATMUL_VMEM_BUDGET)
    dims = ((_NN if mode == "nn" else _NT), ((), ()))

    def body(*refs):
        r = lax.dot_general(refs[0][...].astype(MXU_DTYPE), refs[1][...].astype(MXU_DTYPE), dims, preferred_element_type=F32)
        if has_add:
            r = r + refs[2][...].astype(F32)
        refs[-1][...] = r.astype(refs[-1].dtype)

    b_spec = pl.BlockSpec((K, tn), lambda j, i: (0, j)) if mode == "nn" else pl.BlockSpec((tn, K), lambda j, i: (j, 0))
    in_specs = [pl.BlockSpec((tm, K), lambda j, i: (i, 0)), b_spec]
    args = [a, b]
    if has_add:
        in_specs.append(pl.BlockSpec((tm, tn), lambda j, i: (i, j)))
        args.append(add)
    return pl.pallas_call(
        body, name=name, grid=(N // tn, M // tm),
        in_specs=in_specs,
        out_specs=pl.BlockSpec((tm, tn), lambda j, i: (i, j)),
        out_shape=jax.ShapeDtypeStruct((M, N), out_dtype),
        compiler_params=_cparams(("parallel", "parallel")),
    )(*args)


def _dot(a, b, dims, precision=None):
    if precision is None:
        a, b = a.astype(MXU_DTYPE), b.astype(MXU_DTYPE)
    return lax.dot_general(a, b, (dims, ((), ())), precision=precision, preferred_element_type=F32)


_NN = ((1,), (0,))
_NT = ((1,), (1,))
_TN = ((0,), (0,))


def _rms(x, gain):
    return x * lax.rsqrt(jnp.mean(x * x, axis=-1, keepdims=True) + EPS) * gain


def _hg_block(st_t, q, fl, iv, g, lb, gain):
    row = lax.broadcasted_iota(jnp.int32, (HG_SUB, HG_SUB), 0)
    col = lax.broadcasted_iota(jnp.int32, (HG_SUB, HG_SUB), 1)
    tri = (row >= col).astype(F32)
    outs, states = [], []
    for h in range(HG_HEADS):
        sl = slice(h * HG_DIM, (h + 1) * HG_DIM)
        st = st_t[h * HG_DIM:(h + 1) * HG_DIM, :]
        lbh = lb[:, sl]
        fg = lbh + (1.0 - lbh) * jax.nn.sigmoid(fl[:, sl])
        lf, kk, qf, v = jnp.log(fg), 1.0 - fg, jax.nn.silu(q[:, sl]), iv[:, sl]
        parts = []
        for s in range(q.shape[0] // HG_SUB):
            r = slice(s * HG_SUB, (s + 1) * HG_SUB)
            b = _dot(tri, lf[r], _NN, HI)
            b_mid = jnp.sum(lf[r][:HG_SUB // 2], axis=0, keepdims=True)
            b_end = jnp.sum(lf[r], axis=0, keepdims=True)
            sc = _dot(qf[r] * jnp.exp(b - b_mid), kk[r] * jnp.exp(b_mid - b), _NT) * tri
            parts.append(_dot(sc, v[r], _NN) + _dot(qf[r] * jnp.exp(b), st, _NT))
            st = st * jnp.exp(b_end) + _dot(v[r], kk[r] * jnp.exp(b_end - b), _TN)
        o = jnp.concatenate(parts, axis=0)
        outs.append(_rms(o, gain[:, sl]) * jax.nn.silu(g[:, sl]))
        states.append(st)
    return jnp.concatenate(states, axis=0), jnp.concatenate(outs, axis=1)


def _hg_specs(proj, nb):
    return [pl.BlockSpec((HG_BLOCK, HG_WIDTH), lambda i, c=c, f=nb: (f(i), c)) for c in range(4)]


def hgrn2_fwd(proj, lb, gain):
    L = proj.shape[0]
    n = L // HG_BLOCK

    def body(q, fl, iv, g, lb_r, gain_r, o_ref, st_ref, st):
        @pl.when(pl.program_id(0) == 0)
        def _():
            st[...] = jnp.zeros(st.shape, F32)

        st_ref[0] = st[...]
        new, o = _hg_block(st[...], q[...], fl[...], iv[...], g[...], lb_r[...], gain_r[...])
        st[...] = new
        o_ref[...] = o.astype(o_ref.dtype)

    pspec = pl.BlockSpec((1, HG_WIDTH), lambda i: (0, 0))
    return pl.pallas_call(
        body, name="hgrn2_fwd", grid=(n,),
        in_specs=_hg_specs(proj, lambda i: i) + [pspec, pspec],
        out_specs=[pl.BlockSpec((HG_BLOCK, HG_WIDTH), lambda i: (i, 0)),
                   pl.BlockSpec((1, HG_WIDTH, HG_DIM), lambda i: (i, 0, 0))],
        out_shape=[jax.ShapeDtypeStruct((L, HG_WIDTH), MXU_DTYPE),
                   jax.ShapeDtypeStruct((n, HG_WIDTH, HG_DIM), F32)],
        scratch_shapes=[pltpu.VMEM((HG_WIDTH, HG_DIM), F32)],
        compiler_params=_cparams(("arbitrary",)),
    )(proj, proj, proj, proj, lb, gain)


def hgrn2_bwd(proj, lb, gain, states, do):
    L = proj.shape[0]
    n = L // HG_BLOCK

    def body(q, fl, iv, g, lb_r, gain_r, st_r, do_r, dproj, dlb, dgain, dst):
        @pl.when(pl.program_id(0) == 0)
        def _():
            dst[...] = jnp.zeros(dst.shape, F32)
            dlb[...] = jnp.zeros(dlb.shape, F32)
            dgain[...] = jnp.zeros(dgain.shape, F32)

        _, vjp = jax.vjp(_hg_block, st_r[0], q[...], fl[...], iv[...], g[...], lb_r[...], gain_r[...])
        d_st, dq, dfl, div, dg, d_lb, d_gain = vjp((dst[...], do_r[...].astype(F32)))
        dst[...] = d_st
        dproj[:, 0 * HG_WIDTH:1 * HG_WIDTH] = dq
        dproj[:, 1 * HG_WIDTH:2 * HG_WIDTH] = dfl
        dproj[:, 2 * HG_WIDTH:3 * HG_WIDTH] = div
        dproj[:, 3 * HG_WIDTH:4 * HG_WIDTH] = dg
        dlb[...] += d_lb
        dgain[...] += d_gain

    rev = lambda i: n - 1 - i
    pspec = pl.BlockSpec((1, HG_WIDTH), lambda i: (0, 0))
    return pl.pallas_call(
        body, name="hgrn2_bwd", grid=(n,),
        in_specs=_hg_specs(proj, rev) + [pspec, pspec,
                                         pl.BlockSpec((1, HG_WIDTH, HG_DIM), lambda i: (rev(i), 0, 0)),
                                         pl.BlockSpec((HG_BLOCK, HG_WIDTH), lambda i: (rev(i), 0))],
        out_specs=[pl.BlockSpec((HG_BLOCK, 4 * HG_WIDTH), lambda i: (rev(i), 0)), pspec, pspec],
        out_shape=[jax.ShapeDtypeStruct((L, 4 * HG_WIDTH), F32),
                   jax.ShapeDtypeStruct((1, HG_WIDTH), F32), jax.ShapeDtypeStruct((1, HG_WIDTH), F32)],
        scratch_shapes=[pltpu.VMEM((HG_WIDTH, HG_DIM), F32)],
        compiler_params=_cparams(("arbitrary",)),
    )(proj, proj, proj, proj, lb, gain, states, do)


def _rope_rms(x, gain_p, cos_p, sin_p):
    n = x * lax.rsqrt(jnp.sum(x * x, axis=-1, keepdims=True) * (1.0 / MLA_ROPE) + EPS) * gain_p
    r = lax.broadcasted_iota(jnp.int32, (128, 128), 0)
    c = lax.broadcasted_iota(jnp.int32, (128, 128), 1)
    swap = (r == (c + 64) % 128).astype(F32)
    return n * cos_p + _dot(n, swap, _NN, HI) * sin_p


def _mla_prep(c_q, c_kv, kpe, cos_p, sin_p, q_a, w_uq, kv_a, w_ukv, qn_nope, qn_rope, kn_nope, kn_rope):
    q = _dot(_rms(c_q, q_a), w_uq, _NN)
    kv = _dot(_rms(c_kv, kv_a), w_ukv, _NN)
    k_pe = _rope_rms(kpe, kn_rope, cos_p, sin_p)
    qs, ks = [], []
    for h in range(MLA_HEADS):
        qs.append(_rms(q[:, h * MLA_DK:h * MLA_DK + MLA_NOPE], qn_nope))
        qs.append(_rope_rms(q[:, h * MLA_DK + MLA_NOPE:(h + 1) * MLA_DK], qn_rope, cos_p, sin_p))
        ks.append(_rms(kv[:, h * MLA_NOPE:(h + 1) * MLA_NOPE], kn_nope))
        ks.append(k_pe)
    return jnp.concatenate(qs, axis=1), jnp.concatenate(ks, axis=1), kv[:, MLA_HEADS * MLA_NOPE:]


def _mla_prep_opds(proj, cos_p, sin_p, params, tm, grads):
    g = (lambda k: k) if grads else (lambda k: None)
    c0 = 4 * HG_WIDTH
    return ([rows(proj, tm, g('blk'), col=c0 // MLA_Q_RANK, width=MLA_Q_RANK),
             rows(proj, tm, g('blk'), col=(c0 + MLA_Q_RANK) // 128, width=128),
             rows(proj, tm, g('blk'), col=(c0 + MLA_Q_RANK) // 128 + 1, width=128),
             rows(cos_p, tm), rows(sin_p, tm)] + [full(p, g('acc')) for p in params])


def mla_prep_fwd(proj, cos_p, sin_p, params, tm):
    L = proj.shape[0]
    W = MLA_HEADS * MLA_DK
    rb = lambda w: (tm, w)
    outs = [((L, W), MXU_DTYPE, rb(W), lambda i: (i, 0)), ((L, W), MXU_DTYPE, rb(W), lambda i: (i, 0)),
            ((L, MLA_HEADS * MLA_V), MXU_DTYPE, rb(MLA_HEADS * MLA_V), lambda i: (i, 0))]
    return blocked_fwd(_mla_prep, _mla_prep_opds(proj, cos_p, sin_p, params, tm, False), outs, L // tm, "mla_prep_fwd")


def mla_prep_bwd(proj, cos_p, sin_p, params, dq, dk, dv, tm):
    L = proj.shape[0]
    return blocked_bwd(_mla_prep, _mla_prep_opds(proj, cos_p, sin_p, params, tm, True),
                       [rows(dq, tm), rows(dk, tm), rows(dv, tm)], L // tm, "mla_prep_bwd")


def _scores(q, k, scale, shift=None):
    s = _dot(q, k, _NT) * scale
    if shift is None:
        return s
    row = lax.broadcasted_iota(jnp.int32, s.shape, 0)
    col = lax.broadcasted_iota(jnp.int32, s.shape, 1)
    return jnp.where(col <= row + shift, s, -jnp.inf)


ATTN_WIDE = 4


def attn_fwd(q, k, v, scale, t):
    L = q.shape[0]
    tq = ATTN_WIDE * t

    def body(q_ref, k_ref, v_ref, o_ref, lse_ref):
        i = pl.program_id(1)
        qb = q_ref[...]

        def step(j, carry, shift=None):
            m, l, acc = carry
            kj = k_ref[pl.ds(pl.multiple_of(j * t, t), t), :]
            vj = v_ref[pl.ds(pl.multiple_of(j * t, t), t), :]
            s = _scores(qb, kj, scale, shift)
            m_new = jnp.maximum(m, jnp.max(s, axis=-1, keepdims=True))
            p = jnp.exp(s - m_new)
            alpha = jnp.exp(m - m_new)
            return m_new, alpha * l + jnp.sum(p, axis=-1, keepdims=True), alpha * acc + _dot(p, vj, _NN)

        carry = (jnp.full((tq, 1), -jnp.inf, F32), jnp.zeros((tq, 1), F32), jnp.zeros((tq, MLA_V), F32))
        carry = lax.fori_loop(0, ATTN_WIDE * i, step, carry)
        for d in range(ATTN_WIDE):
            carry = step(ATTN_WIDE * i + d, carry, -d * t)
        m, l, acc = carry
        o_ref[...] = acc / l
        lse_ref[...] = jnp.broadcast_to(m + jnp.log(l), lse_ref.shape)

    hspec = lambda rows_, w: pl.BlockSpec((rows_, w), lambda h, i: (0, h))
    bspec = lambda w: pl.BlockSpec((tq, w), lambda h, i: (i, h))
    return pl.pallas_call(
        body, name="attn_fwd", grid=(MLA_HEADS, L // tq),
        in_specs=[bspec(MLA_DK), hspec(L, MLA_DK), hspec(L, MLA_V)],
        out_specs=[bspec(MLA_V), bspec(MLA_V)],
        out_shape=[jax.ShapeDtypeStruct((L, MLA_HEADS * MLA_V), F32)] * 2,
        compiler_params=_cparams(("parallel", "parallel")),
    )(q, k, v)


def attn_bwd_dq(q, k, v, o, lse, do, scale, t):
    L = q.shape[0]
    tq = ATTN_WIDE * t

    def body(q_ref, k_ref, v_ref, o_ref, lse_ref, do_ref, dq_ref):
        i = pl.program_id(1)
        qb, dob = q_ref[...], do_ref[...]
        delta = jnp.sum(dob * o_ref[...], axis=-1, keepdims=True)
        lse_c = jnp.max(lse_ref[...], axis=-1, keepdims=True)

        def step(j, dq, shift=None):
            kj = k_ref[pl.ds(pl.multiple_of(j * t, t), t), :]
            vj = v_ref[pl.ds(pl.multiple_of(j * t, t), t), :]
            p = jnp.exp(_scores(qb, kj, scale, shift) - lse_c)
            ds = p * (_dot(dob, vj, _NT) - delta) * scale
            return dq + _dot(ds, kj, _NN)

        dq = lax.fori_loop(0, ATTN_WIDE * i, step, jnp.zeros((tq, MLA_DK), F32))
        for d in range(ATTN_WIDE):
            dq = step(ATTN_WIDE * i + d, dq, -d * t)
        dq_ref[...] = dq

    hspec = lambda w: pl.BlockSpec((L, w), lambda h, i: (0, h))
    bspec = lambda w: pl.BlockSpec((tq, w), lambda h, i: (i, h))
    return pl.pallas_call(
        body, name="attn_bwd_dq", grid=(MLA_HEADS, L // tq),
        in_specs=[bspec(MLA_DK), hspec(MLA_DK), hspec(MLA_V), bspec(MLA_V), bspec(MLA_V), bspec(MLA_V)],
        out_specs=bspec(MLA_DK),
        out_shape=jax.ShapeDtypeStruct((L, MLA_HEADS * MLA_DK), F32),
        compiler_params=_cparams(("parallel", "parallel")),
    )(q, k, v, o, lse, do)


def attn_bwd_dkv(q, k, v, o, lse, do, scale, t):
    L = q.shape[0]
    tk = ATTN_WIDE * t

    def body(q_ref, k_ref, v_ref, o_ref, lse_ref, do_ref, dk_ref, dv_ref):
        j = pl.program_id(1)
        kb, vb = k_ref[...], v_ref[...]

        def step(i, carry, shift=None):
            dk, dv = carry
            r = pl.ds(pl.multiple_of(i * t, t), t)
            qi, doi = q_ref[r, :], do_ref[r, :]
            delta = jnp.sum(doi * o_ref[r, :], axis=-1, keepdims=True)
            lse_c = jnp.max(lse_ref[r, :], axis=-1, keepdims=True)
            p = jnp.exp(_scores(qi, kb, scale, shift) - lse_c)
            ds = p * (_dot(doi, vb, _NT) - delta) * scale
            return dk + _dot(ds, qi, _TN), dv + _dot(p, doi, _TN)

        carry = (jnp.zeros((tk, MLA_DK), F32), jnp.zeros((tk, MLA_V), F32))
        for d in range(ATTN_WIDE):
            carry = step(ATTN_WIDE * j + d, carry, d * t)
        dk, dv = lax.fori_loop(ATTN_WIDE * (j + 1), L // t, step, carry)
        dk_ref[...] = dk
        dv_ref[...] = dv

    hspec = lambda w: pl.BlockSpec((L, w), lambda h, j: (0, h))
    bspec = lambda w: pl.BlockSpec((tk, w), lambda h, j: (j, h))
    return pl.pallas_call(
        body, name="attn_bwd_dkv", grid=(MLA_HEADS, L // tk),
        in_specs=[hspec(MLA_DK), bspec(MLA_DK), bspec(MLA_V), hspec(MLA_V), hspec(MLA_V), hspec(MLA_V)],
        out_specs=[bspec(MLA_DK), bspec(MLA_V)],
        out_shape=[jax.ShapeDtypeStruct((L, MLA_HEADS * MLA_DK), F32), jax.ShapeDtypeStruct((L, MLA_HEADS * MLA_V), F32)],
        compiler_params=_cparams(("parallel", "parallel")),
    )(q, k, v, o, lse, do)


S5_LANES = S5_GB * S5_STATE


def _cmul(ar, ai, br, bi):
    return ar * br - ai * bi, ar * bi + ai * br


def _a_powers(ar, ai, reverse):
    a2 = _cmul(ar, ai, ar, ai)
    a4 = _cmul(*a2, *a2)
    row = lax.broadcasted_iota(jnp.int32, (8, ar.shape[1]), 0)
    e = (8 - row) if reverse else (row + 1)
    tr, ti = jnp.ones((8, ar.shape[1]), F32), jnp.zeros((8, ar.shape[1]), F32)
    for bit, (pr, pi) in ((1, (ar, ai)), (2, a2), (4, a4), (8, _cmul(*a4, *a4))):
        nr, ni = _cmul(tr, ti, pr, pi)
        sel = (e & bit) != 0
        tr, ti = jnp.where(sel, nr, tr), jnp.where(sel, ni, ti)
    pows = []
    for d, (pr, pi) in zip((1, 2, 4), ((ar, ai), a2, a4)):
        keep = (row < 8 - d) if reverse else (row >= d)
        pows.append((jnp.where(keep, pr, 0.0), jnp.where(keep, pi, 0.0)))
    return pows, (tr, ti)


def _scan8(xr, xi, pows, table, cr, ci, reverse):
    for d, (pr, pi) in zip((1, 2, 4), pows):
        shift = 8 - d if reverse else d
        mr, mi = _cmul(pr, pi, pltpu.roll(xr, shift, 0), pltpu.roll(xi, shift, 0))
        xr, xi = xr + mr, xi + mi
    mr, mi = _cmul(table[0], table[1], cr, ci)
    return xr + mr, xi + mi


def _row_of(x, r):
    row = lax.broadcasted_iota(jnp.int32, x.shape, 0)
    return jnp.sum(jnp.where(row == r, x, 0.0), axis=0, keepdims=True)


def _s5_scan_fwd(h_re, h_im, ar, ai, L):
    pows, table = _a_powers(ar, ai, False)

    def step(i, carry):
        r = pl.ds(pl.multiple_of(i * 8, 8), 8)
        xr, xi = _scan8(h_re[r, :], h_im[r, :], pows, table, carry[0], carry[1], False)
        h_re[r, :] = xr
        h_im[r, :] = xi
        return xr[7:8, :], xi[7:8, :]

    z = jnp.zeros((1, ar.shape[1]), F32)
    lax.fori_loop(0, L // 8, step, (z, z))


def _s5_specs(L):
    return [pl.BlockSpec((L, 128), lambda g: (0, g)),
            pl.BlockSpec((1, 128, S5_LANES), lambda g: (g, 0, 0)), pl.BlockSpec((1, 128, S5_LANES), lambda g: (g, 0, 0)),
            pl.BlockSpec((1, 1, S5_LANES), lambda g: (g, 0, 0)), pl.BlockSpec((1, 1, S5_LANES), lambda g: (g, 0, 0)),
            pl.BlockSpec((1, S5_LANES, 128), lambda g: (g, 0, 0)), pl.BlockSpec((1, S5_LANES, 128), lambda g: (g, 0, 0))]


def s5_fwd(u, w_re, w_im, a_re, a_im, c_re, c_im):
    L, D = u.shape

    def body(u_ref, wr, wi, ar, ai, cr, ci, y_ref, h_re, h_im):
        ub = u_ref[...]
        h_re[...] = _dot(ub, wr[0], _NN)
        h_im[...] = _dot(ub, wi[0], _NN)
        _s5_scan_fwd(h_re, h_im, ar[0], ai[0], L)
        y_ref[...] = _dot(h_re[...], cr[0], _NN) - _dot(h_im[...], ci[0], _NN)

    return pl.pallas_call(
        body, name="s5_fwd", grid=(D // 128,),
        in_specs=_s5_specs(L), out_specs=pl.BlockSpec((L, 128), lambda g: (0, g)),
        out_shape=jax.ShapeDtypeStruct((L, D), F32),
        scratch_shapes=[pltpu.VMEM((L, S5_LANES), F32), pltpu.VMEM((L, S5_LANES), F32)],
        compiler_params=_cparams(("parallel",)),
    )(u, w_re, w_im, a_re, a_im, c_re, c_im)


def s5_bwd(u, w_re, w_im, a_re, a_im, c_re, c_im, dy, tc):
    L, D = u.shape
    nch = L // tc

    def body(u_ref, wr, wi, ar_ref, ai_ref, cr, ci, dy_ref, du_ref, dwr, dwi, dar, dai, dcr, dci, h_re, h_im, g_re, g_im):
        ar, ai = ar_ref[0], ai_ref[0]
        ub = u_ref[...]
        h_re[...] = _dot(ub, wr[0], _NN)
        h_im[...] = _dot(ub, wi[0], _NN)
        _s5_scan_fwd(h_re, h_im, ar, ai, L)
        dyb = dy_ref[...]
        dcr[0] = _dot(h_re[...], dyb, _TN)
        dci[0] = -_dot(h_im[...], dyb, _TN)
        pows, table = _a_powers(ar, -ai, True)
        dwr[0] = jnp.zeros((128, S5_LANES), F32)
        dwi[0] = jnp.zeros((128, S5_LANES), F32)
        z1 = jnp.zeros((1, S5_LANES), F32)
        z8 = jnp.zeros((8, S5_LANES), F32)

        def chunk(cc, carry):
            c0 = pl.multiple_of((nch - 1 - cc) * tc, tc)
            rows_c = pl.ds(c0, tc)
            dyc = dy_ref[rows_c, :]
            g_re[...] = _dot(dyc, cr[0], _NT)
            g_im[...] = -_dot(dyc, ci[0], _NT)

            def step(ii, cy):
                gr_c, gi_c, acc_r, acc_i = cy
                i8 = pl.multiple_of((tc // 8 - 1 - ii) * 8, 8)
                rl = pl.ds(i8, 8)
                xr, xi = _scan8(g_re[rl, :], g_im[rl, :], pows, table, gr_c, gi_c, True)
                g_re[rl, :] = xr
                g_im[rl, :] = xi
                t0 = c0 + i8
                hb_r, hb_i = h_re[pl.ds(t0, 8), :], h_im[pl.ds(t0, 8), :]
                tp = pl.multiple_of(jnp.maximum(t0 - 8, 0), 8)
                first = (t0 > 0).astype(F32)
                pr = h_re[pl.ds(tp, 8), :][7:8, :] * first
                pi = h_im[pl.ds(tp, 8), :][7:8, :] * first
                row = lax.broadcasted_iota(jnp.int32, xr.shape, 0)
                hp_r = jnp.where(row == 0, pr, pltpu.roll(hb_r, 1, 0))
                hp_i = jnp.where(row == 0, pi, pltpu.roll(hb_i, 1, 0))
                return (xr[0:1, :], xi[0:1, :],
                        acc_r + xr * hp_r + xi * hp_i, acc_i + xi * hp_r - xr * hp_i)

            cy = lax.fori_loop(0, tc // 8, step, carry)
            uc = u_ref[rows_c, :]
            gr, gi = g_re[...], g_im[...]
            du_ref[rows_c, :] = _dot(gr, wr[0], _NT) + _dot(gi, wi[0], _NT)
            dwr[0] += _dot(uc, gr, _TN)
            dwi[0] += _dot(uc, gi, _TN)
            return cy

        _, _, acc_r, acc_i = lax.fori_loop(0, nch, chunk, (z1, z1, z8, z8))
        dar[0] = jnp.sum(acc_r, axis=0, keepdims=True)
        dai[0] = jnp.sum(acc_i, axis=0, keepdims=True)

    specs = _s5_specs(L)
    return pl.pallas_call(
        body, name="s5_bwd", grid=(D // 128,),
        in_specs=specs + [pl.BlockSpec((L, 128), lambda g: (0, g))],
        out_specs=[pl.BlockSpec((L, 128), lambda g: (0, g))] + specs[1:],
        out_shape=[jax.ShapeDtypeStruct((L, D), F32)] + [jax.ShapeDtypeStruct(x.shape, F32)
                                                        for x in (w_re, w_im, a_re, a_im, c_re, c_im)],
        scratch_shapes=[pltpu.VMEM((L, S5_LANES), F32), pltpu.VMEM((L, S5_LANES), F32),
                        pltpu.VMEM((tc, S5_LANES), F32), pltpu.VMEM((tc, S5_LANES), F32)],
        compiler_params=_cparams(("parallel",)),
    )(u, w_re, w_im, a_re, a_im, c_re, c_im, dy)


def _s5_discretize(lr, li, ldt, br, bi):
    dt = jnp.exp(ldt)
    mag = jnp.exp(lr * dt)
    ar, ai = mag * jnp.cos(li * dt), mag * jnp.sin(li * dt)
    den = lr * lr + li * li
    zr = ((ar - 1.0) * lr + ai * li) / den
    zi = (ai * lr - (ar - 1.0) * li) / den
    p = lax.broadcasted_iota(jnp.int32, (S5_STATE, S5_STATE * S5_GROUP), 0)
    c = lax.broadcasted_iota(jnp.int32, (S5_STATE, S5_STATE * S5_GROUP), 1)
    rep = (c // S5_GROUP == p).astype(F32)
    zr, zi = _dot(zr, rep, _NN, HI), _dot(zi, rep, _NN, HI)
    return ar, ai, zr * br - zi * bi, zr * bi + zi * br


def _conv_shift(x, d):
    row = lax.broadcasted_iota(jnp.int32, x.shape, 0)
    return jnp.where(row >= d, pltpu.roll(x, d, 0), 0.0)


def _conv_unshift(x, d):
    n = x.shape[0]
    row = lax.broadcasted_iota(jnp.int32, x.shape, 0)
    return jnp.where(row < n - d, pltpu.roll(x, n - d, 0), 0.0)


@functools.partial(jax.custom_vjp, nondiff_argnums=(1,))
def _shift_rows(x, d):
    return _conv_shift(x, d)


_shift_rows.defvjp(lambda x, d: (_conv_shift(x, d), None), lambda d, _, g: (_conv_unshift(g, d),))


def _conv_gate(ug, uv, wg0, wg1, wg2, wv0, wv1, wv2, bg, bv):
    def conv(u, w0, w1, w2, b):
        return u * w2 + _shift_rows(u, 1) * w1 + _shift_rows(u, 2) * w0 + b
    return (jax.nn.silu(conv(ug, wg0, wg1, wg2, bg)) * conv(uv, wv0, wv1, wv2, bv),)


def _rms_fn(x, gain):
    return (_rms(x, gain),)


def _softmax_rows(s):
    e = jnp.exp(s - lax.stop_gradient(jnp.max(s, axis=-1, keepdims=True)))
    return e / jnp.sum(e, axis=-1, keepdims=True)


def _xa_core(qp, k, v, q_gain):
    dh = qp.shape[1] // XA_HEADS
    outs = []
    for h in range(XA_HEADS):
        sl = slice(h * dh, (h + 1) * dh)
        p = _softmax_rows(_dot(_rms(qp[:, sl], q_gain), k[:, sl], _NT) * (dh ** -0.5))
        outs.append(_dot(p, v[:, sl], _NN))
    return (jnp.concatenate(outs, axis=1),)


def _mem_kv(mem, mem_gain, wk, wv, k_gain):
    m = _rms(mem, mem_gain)
    kp = _dot(m, wk, _NN)
    dh = kp.shape[1] // XA_HEADS
    k = jnp.concatenate([_rms(kp[:, h * dh:(h + 1) * dh], k_gain) for h in range(XA_HEADS)], axis=1)
    return k, _dot(m, wv, _NN)


def _s5_post(y, u, d):
    return (jax.nn.gelu(y + d * u),)


def _glu(a, b):
    return (a * jax.nn.sigmoid(b),)


def _lb_first(logits):
    e = jnp.exp(logits - lax.stop_gradient(jnp.max(logits, axis=0, keepdims=True)))
    return (_row_of(e, 0) / jnp.sum(e, axis=0, keepdims=True),)


def _loss_fn(y, t):
    e = y - t
    part = 0.5 * jnp.sum(e * e) / y.shape[1]
    return e * (1.0 / y.shape[1]), jnp.full((8, 128), part / (8 * 128), F32)


def _out(shape, dtype, tm):
    return (shape, dtype, (tm, shape[1]), lambda i: (i, 0))


def rms_fwd(h, gain, tm, dtype):
    return blocked_fwd(_rms_fn, [rows(h, tm), full(gain)], [_out(h.shape, dtype, tm)], h.shape[0] // tm, "rms_fwd")[0]


def rms_bwd(h, gain, dy, tm, residual):
    return blocked_bwd(_rms_fn, [rows(h, tm, 'blk'), full(gain, 'acc')], [rows(dy, tm)], h.shape[0] // tm, "rms_bwd",
                       plus=rows(residual, tm))


def adamw(w, g, m, v, name):
    R = w.shape[0]
    tm = _tile(R, 256)
    assert g.shape == w.shape == m.shape == v.shape, (name, w.shape, g.shape)

    def body(w_ref, g_ref, m_ref, v_ref, d_ref, nm_ref, nv_ref):
        g_ = g_ref[...]
        m_ = ADAM_B1 * m_ref[...] + (1.0 - ADAM_B1) * g_
        v_ = ADAM_B2 * v_ref[...] + (1.0 - ADAM_B2) * jnp.square(g_)
        m_hat = m_ / (1.0 - ADAM_B1 ** ADAM_STEP)
        v_hat = v_ / (1.0 - ADAM_B2 ** ADAM_STEP)
        d_ref[...] = -ADAM_LR * (m_hat / (jnp.sqrt(v_hat) + ADAM_EPS) + ADAM_WD * w_ref[...])
        nm_ref[...] = m_
        nv_ref[...] = v_

    spec = pl.BlockSpec((tm, w.shape[1]), lambda i: (i, 0))
    return pl.pallas_call(
        body, name=name, grid=(R // tm,), in_specs=[spec] * 4, out_specs=[spec] * 3,
        out_shape=[jax.ShapeDtypeStruct(w.shape, F32)] * 3, compiler_params=_cparams(("parallel",)),
    )(w, g, m, v)


def add2(x, y, name, out_dtype=F32):
    shape = x.shape
    x, y = x.reshape(-1, shape[-1]), y.reshape(-1, shape[-1])
    R, C = x.shape
    tm = _tile(R, 256)

    def body(x_ref, y_ref, o_ref):
        o_ref[...] = (x_ref[...] + y_ref[...]).astype(o_ref.dtype)

    spec = pl.BlockSpec((tm, C), lambda i: (i, 0))
    return pl.pallas_call(
        body, name=name, grid=(R // tm,), in_specs=[spec, spec], out_specs=spec,
        out_shape=jax.ShapeDtypeStruct((R, C), out_dtype), compiler_params=_cparams(("parallel",)),
    )(x, y).reshape(shape)


def add_chips(own, got, name):
    n, R, C = got.shape
    tm = _tile(R, 256)

    def body(*refs):
        acc = refs[0][...].astype(F32)
        for r in refs[1:-1]:
            acc = acc + r[...].astype(F32)
        refs[-1][...] = acc

    return pl.pallas_call(
        body, name=name, grid=(R // tm,),
        in_specs=[pl.BlockSpec((tm, C), lambda i: (i, 0))] + [pl.BlockSpec((None, tm, C), lambda i, j=j: (j, i, 0))
                                                            for j in range(n)],
        out_specs=pl.BlockSpec((tm, C), lambda i: (i, 0)),
        out_shape=jax.ShapeDtypeStruct((R, C), F32), compiler_params=_cparams(("parallel",)),
    )(own, *([got] * n))


_HBM = pl.BlockSpec(memory_space=pltpu.HBM)
N_CHIPS = 4


def _my_place():
    return lax.axis_index("x"), lax.axis_index("y"), lax.axis_index("c")


def _window(ref, axis, start, size):
    idx = [slice(None)] * len(ref.shape)
    idx[axis] = pl.ds(start, size)
    return ref.at[tuple(idx)]


def _comm_call(body, name, xs, out_shapes, n_remote, n_local):
    return pl.pallas_call(
        body, name=name, in_specs=[_HBM] * len(xs), out_specs=[_HBM] * len(out_shapes), out_shape=out_shapes,
        scratch_shapes=[pltpu.SemaphoreType.DMA((n_remote,)), pltpu.SemaphoreType.DMA((n_remote,)),
                        pltpu.SemaphoreType.DMA((max(n_local, 1),))],
        compiler_params=pltpu.CompilerParams(has_side_effects=True),
    )(*xs)


def _run(copies):
    for cp in copies:
        cp.start()
    for cp in copies:
        cp.wait()


def _other_chips(mx, my):
    return [(mx ^ (j >> 1), my ^ (j & 1)) for j in (1, 2, 3)]


def chip_gather(xs, axes, name):
    n = len(xs)
    shapes, final = [], []
    for x, ax in zip(xs, axes):
        s = list(x.shape)
        if ax is None:
            shapes.append([N_CHIPS] + s)
            final.append(shapes[-1])
        elif ax < x.ndim - 1:
            shapes.append(s[:ax] + [N_CHIPS] + s[ax:])
            final.append(s[:ax] + [N_CHIPS * s[ax]] + s[ax + 1:])
        else:
            assert s[ax] % 128 == 0, (name, s)
            shapes.append(s[:ax] + [N_CHIPS * s[ax]])
            final.append(shapes[-1])

    def body(*refs):
        x_refs, o_refs = refs[:n], refs[n:2 * n]
        send_sems, recv_sems, local_sems = refs[2 * n:]
        mx, my, mc = _my_place()
        q = 2 * mx + my
        copies = []
        for i, (x_ref, o_ref, ax) in enumerate(zip(x_refs, o_refs, axes)):
            if ax is None or ax < len(x_ref.shape) - 1:
                dst = o_ref.at[(slice(None),) * (ax or 0) + (q,)]
            else:
                dst = _window(o_ref, ax, q * x_ref.shape[ax], x_ref.shape[ax])
            copies.append(pltpu.make_async_copy(x_ref, dst, local_sems.at[i]))
            for j, (tx, ty) in enumerate(_other_chips(mx, my)):
                copies.append(pltpu.make_async_remote_copy(
                    src_ref=x_ref, dst_ref=dst, send_sem=send_sems.at[3 * i + j], recv_sem=recv_sems.at[3 * i + j],
                    device_id=(tx, ty, mc), device_id_type=MESH))
        _run(copies)

    out_shapes = [jax.ShapeDtypeStruct(tuple(s), x.dtype) for s, x in zip(shapes, xs)]
    return [o.reshape(f) for o, f in zip(_comm_call(body, name, xs, out_shapes, 3 * n, n), final)]


def gather_two_level(xs, name):
    n = len(xs)
    shapes = [jax.ShapeDtypeStruct((2, N_CHIPS) + x.shape[1:], x.dtype) for x in xs]

    def body(*refs):
        x_refs, o_refs = refs[:n], refs[n:2 * n]
        send_sems, recv_sems, local_sems = refs[2 * n:]
        mx, my, mc = _my_place()
        q = 2 * mx + my
        first, local, second = [], [], []
        for i, (x_ref, o_ref) in enumerate(zip(x_refs, o_refs)):
            local.append(pltpu.make_async_copy(x_ref.at[mc], o_ref.at[mc, q], local_sems.at[i]))
            for j, (tx, ty) in enumerate(_other_chips(mx, my)):
                first.append(pltpu.make_async_remote_copy(
                    src_ref=x_ref.at[mc], dst_ref=o_ref.at[mc, q], send_sem=send_sems.at[4 * i + j],
                    recv_sem=recv_sems.at[4 * i + j], device_id=(tx, ty, mc), device_id_type=MESH))
            second.append(pltpu.make_async_remote_copy(
                src_ref=o_ref.at[mc], dst_ref=o_ref.at[mc], send_sem=send_sems.at[4 * i + 3],
                recv_sem=recv_sems.at[4 * i + 3], device_id=(mx, my, 1 - mc), device_id_type=MESH))
        for cp in local + first:
            cp.start()
        for cp in local:
            cp.wait()
        for cp in first:
            cp.wait_recv()
        _run(second)
        for cp in first:
            cp.wait_send()

    return _comm_call(body, name, xs, shapes, 4 * n, n)


def pair_swap(xs, name, halves):
    n = len(xs)
    shapes = [jax.ShapeDtypeStruct(x.shape[1:] if halves else x.shape, x.dtype) for x in xs]

    def body(*refs):
        x_refs, o_refs = refs[:n], refs[n:2 * n]
        send_sems, recv_sems, _ = refs[2 * n:]
        mx, my, mc = _my_place()
        _run([pltpu.make_async_remote_copy(
            src_ref=x_ref.at[1 - mc] if halves else x_ref, dst_ref=o_ref, send_sem=send_sems.at[i],
            recv_sem=recv_sems.at[i], device_id=(mx, my, 1 - mc), device_id_type=MESH)
            for i, (x_ref, o_ref) in enumerate(zip(x_refs, o_refs))])

    return _comm_call(body, name, xs, shapes, n, 0)


def chip_all_to_all(xs, name):
    n = len(xs)
    shapes = [jax.ShapeDtypeStruct((N_CHIPS - 1,) + x.shape[1:], x.dtype) for x in xs]

    def body(*refs):
        x_refs, o_refs = refs[:n], refs[n:2 * n]
        send_sems, recv_sems, _ = refs[2 * n:]
        mx, my, mc = _my_place()
        copies = []
        for i, (x_ref, o_ref) in enumerate(zip(x_refs, o_refs)):
            for j, (tx, ty) in enumerate(_other_chips(mx, my)):
                copies.append(pltpu.make_async_remote_copy(
                    src_ref=x_ref.at[2 * tx + ty], dst_ref=o_ref.at[j], send_sem=send_sems.at[3 * i + j],
                    recv_sem=recv_sems.at[3 * i + j], device_id=(tx, ty, mc), device_id_type=MESH))
        _run(copies)

    return _comm_call(body, name, xs, shapes, 3 * n, 0)


WEIGHTS = ['norm_mix', 'norm_xa', 'norm_mem', 'norm_ffn', 'xa_wq', 'xa_wk', 'xa_wv', 'xa_wo', 'xa_q_norm', 'xa_k_norm',
           'ffn_w_up', 'ffn_conv_w', 'ffn_conv_b', 'ffn_w_down', 'hg_lb_logits', 'mix_w_in', 'hg_out_norm',
           'mla_q_a_norm', 'mla_w_uq', 'mla_kv_a_norm', 'mla_w_ukv', 'mla_qn_nope', 'mla_qn_rope', 'mla_kn_nope',
           'mla_kn_rope', 'mix_w_out', 's5_lam_re', 's5_lam_im', 's5_log_dt', 's5_b_re', 's5_b_im', 's5_c_re',
           's5_c_im', 's5_d', 's5_w_glu_a', 's5_w_glu_b']
INPUTS = ['x', 'mem', 'positions'] + WEIGHTS + ['loss_target'] + ['m_' + n for n in WEIGHTS] + ['v_' + n for n in WEIGHTS]
SHARD_AXIS = {'xa_wq': 1, 'xa_wk': 1, 'xa_wv': 1, 'xa_wo': 1, 'ffn_w_up': 2, 'ffn_conv_w': 2, 'ffn_w_down': 1,
              'mix_w_in': 2, 'mla_w_uq': 2, 'mla_w_ukv': 2, 'mix_w_out': 1, 's5_d': 1, 's5_w_glu_a': 1, 's5_w_glu_b': 1}
BIG = ['xa_wq', 'xa_wk', 'xa_wv', 'xa_wo', 'ffn_w_up', 'ffn_w_down', 'mix_w_in', 'mix_w_out', 's5_w_glu_a', 's5_w_glu_b']
SMALL_SHARDED = [n for n in WEIGHTS if n in SHARD_AXIS and n not in BIG]
REPLICATED = [n for n in WEIGHTS if n not in SHARD_AXIS]
SMALL = SMALL_SHARDED + REPLICATED
PACK_W = 1024
ROW_MULT = 16
W_IN_SHARD = IN_WIDTH // N_CHIPS
W_IN_SHARD_PAD = 640


def _pack(flats, mult=ROW_MULT):
    flat = jnp.concatenate([f.reshape(-1) for f in flats])
    unit = mult * PACK_W
    n = -(-flat.shape[0] // unit) * unit
    return jnp.pad(flat, (0, n - flat.shape[0])).reshape(n // PACK_W, PACK_W)


def _unpack(packed, shapes):
    flat, out, o = packed.reshape(-1), [], 0
    for s in shapes:
        n = math.prod(s)
        out.append(flat[o:o + n].reshape(s))
        o += n
    return out


def _rope_pad(w):
    z = jnp.zeros(w.shape[:-1] + (MLA_ROPE // 2,), w.dtype)
    return jnp.concatenate([w[..., :MLA_ROPE // 2], z, w[..., MLA_ROPE // 2:], z], axis=-1)


def _rope_unpad(g):
    return jnp.concatenate([g[..., :MLA_ROPE // 2], g[..., 64:64 + MLA_ROPE // 2]], axis=-1)


def _blockdiag_in(bb):
    nb = bb.shape[0] // S5_GB
    t = bb.reshape(nb, S5_GB, S5_STATE, S5_GROUP).transpose(0, 1, 3, 2)
    return jnp.einsum('bgmp,gh->bgmhp', t, jnp.eye(S5_GB, dtype=bb.dtype)).reshape(nb, S5_GB * S5_GROUP, S5_LANES)


def _blockdiag_in_t(dw):
    nb = dw.shape[0]
    t = jnp.einsum('bgmhp,gh->bgmp', dw.reshape(nb, S5_GB, S5_GROUP, S5_GB, S5_STATE), jnp.eye(S5_GB, dtype=dw.dtype))
    return t.transpose(0, 1, 3, 2).reshape(nb * S5_GB, S5_STATE, S5_GROUP)


def _blockdiag_out(c):
    nb = c.shape[0] // S5_GB
    t = c.reshape(nb, S5_GB, S5_GROUP, S5_STATE).transpose(0, 1, 3, 2)
    return jnp.einsum('bgpm,gh->bgphm', t, jnp.eye(S5_GB, dtype=c.dtype)).reshape(nb, S5_LANES, S5_GB * S5_GROUP)


def _blockdiag_out_t(dc):
    nb = dc.shape[0]
    t = jnp.einsum('bgphm,gh->bgpm', dc.reshape(nb, S5_GB, S5_STATE, S5_GB, S5_GROUP), jnp.eye(S5_GB, dtype=dc.dtype))
    return t.transpose(0, 1, 3, 2).reshape(nb * S5_GB, S5_GROUP, S5_STATE)


def _gather_weights(P):
    def halves(x):
        return x if x.shape[0] == 2 else x.reshape(2, x.shape[1] // 2, x.shape[2])

    xs = [halves(P[n].astype(BF16)) for n in BIG] + [halves(_pack([P[n] for n in SMALL_SHARDED], 2 * ROW_MULT)[None])]
    got = gather_two_level(xs, "gather_weights")
    full_w = {}
    for n, g in zip(BIG, got[:-1]):
        two_layers, by_rows = P[n].shape[0] == 2, SHARD_AXIS[n] == 1
        if two_layers and by_rows:
            full_w[n] = g.reshape(2, N_CHIPS * g.shape[2], g.shape[3])
        elif two_layers:
            full_w[n] = g.transpose(0, 2, 1, 3).reshape(2, g.shape[2], N_CHIPS * g.shape[3])
        elif by_rows:
            full_w[n] = g.transpose(1, 0, 2, 3).reshape(1, 2 * N_CHIPS * g.shape[2], g.shape[3])
        else:
            full_w[n] = g.transpose(0, 2, 1, 3).reshape(1, 2 * g.shape[2], N_CHIPS * g.shape[3])
    small = got[-1].transpose(1, 0, 2, 3).reshape(N_CHIPS, -1, PACK_W)
    per_chip = [_unpack(small[q], [P[n].shape for n in SMALL_SHARDED]) for q in range(N_CHIPS)]
    for i, n in enumerate(SMALL_SHARDED):
        full_w[n] = jnp.concatenate([per_chip[q][i] for q in range(N_CHIPS)], axis=SHARD_AXIS[n])
    return full_w


def _halves_first(x):
    return x.reshape(x.shape[0], 2, x.shape[1] // 2, x.shape[2]).transpose(1, 0, 2, 3)


def _reduce_and_update(GB, GS, P):
    mx, my, mc = _my_place()
    q = 2 * mx + my
    small = _pack([GS[n] for n in SMALL], 2 * N_CHIPS * ROW_MULT)
    xs = [GB[n] for n in BIG] + [_halves_first(small.reshape(N_CHIPS, -1, PACK_W))]
    theirs = pair_swap(xs, "grads_pair_swap", True)
    names = BIG + ['small']
    pair = [add2(lax.dynamic_index_in_dim(x, mc, 0, False), t, "grads_pair_sum_" + n, F32 if n == 'small' else BF16)
            for x, t, n in zip(xs, theirs, names)]
    got = chip_all_to_all(pair, "grads_chip_all_to_all")
    summed = [add_chips(lax.dynamic_index_in_dim(p, q, 0, False), g, "grads_chip_sum_" + n)
              for p, g, n in zip(pair, got, names)]
    other = pair_swap(summed, "grads_pair_join", False)
    joined = [lax.cond(mc == 0, lambda a, b: jnp.concatenate([a, b], axis=0), lambda a, b: jnp.concatenate([b, a], axis=0),
                       s, o) for s, o in zip(summed, other)]
    small_sum = chip_gather([joined[-1]], [0], "grads_small_gather")[0]
    g_small = dict(zip(SMALL, _unpack(small_sum, [GS[n].shape for n in SMALL])))
    for n in SMALL_SHARDED:
        s = P[n].shape[SHARD_AXIS[n]]
        g_small[n] = lax.dynamic_slice_in_dim(g_small[n], q * s, s, axis=SHARD_AXIS[n])

    grad, delta, new_m, new_v = {}, {}, {}, {}
    for n, g in zip(BIG, joined[:-1]):
        shape = P[n].shape
        if n == 'mix_w_in':
            g = g[:, :W_IN_SHARD]
        two_d = (g.shape[0], shape[-1])
        d, m_, v_ = adamw(P[n].reshape(two_d), g, P['m_' + n].reshape(two_d), P['v_' + n].reshape(two_d), "adamw_" + n)
        grad[n], delta[n], new_m[n], new_v[n] = (t.reshape(shape) for t in (g, d, m_, v_))
    packed = lambda prefix: _pack([P[prefix + n] for n in SMALL])
    d, m_, v_ = adamw(packed(''), _pack([g_small[n] for n in SMALL]), packed('m_'), packed('v_'), "adamw_small")
    shapes = [P[n].shape for n in SMALL]
    grad.update(g_small)
    for out, pk in ((delta, d), (new_m, m_), (new_v, v_)):
        out.update(zip(SMALL, _unpack(pk, shapes)))
    return grad, delta, new_m, new_v


def _row(v):
    return v.reshape(1, -1)


def _xattn_fwd(h, mem, W, lyr, tm):
    g_xa, g_mem = _row(W['norm_xa'][lyr]), _row(W['norm_mem'][lyr])
    g_q, g_k = _row(W['xa_q_norm'][lyr]), _row(W['xa_k_norm'][lyr])
    wq, wk, wv, wo = (W[n][lyr] for n in ('xa_wq', 'xa_wk', 'xa_wv', 'xa_wo'))
    L, D = h.shape
    M = mem.shape[0]
    hx = rms_fwd(h, g_xa, tm, MXU_DTYPE)
    qp = matmul(hx, wq, name="xa_q")
    kv_opds = [full(mem), full(g_mem), full(wk), full(wv), full(g_k)]
    k, v = blocked_fwd(_mem_kv, kv_opds, [((M, D), F32, (M, D), lambda i: (0, 0))] * 2, 1, "xa_mem_kv")
    o = blocked_fwd(_xa_core, [rows(qp, tm), full(k), full(v), full(g_q)], [_out((L, D), MXU_DTYPE, tm)], L // tm,
                    "xa_core")[0]
    out = matmul(o, wo, add=h, name="xa_o")
    return out, (h, hx, qp, k, v, o)


def _xattn_bwd(dout, saved, mem, W, lyr, tm):
    h, hx, qp, k, v, o = saved
    g_xa, g_mem = _row(W['norm_xa'][lyr]), _row(W['norm_mem'][lyr])
    g_q, g_k = _row(W['xa_q_norm'][lyr]), _row(W['xa_k_norm'][lyr])
    wq, wk, wv, wo = (W[n][lyr] for n in ('xa_wq', 'xa_wk', 'xa_wv', 'xa_wo'))
    L = h.shape[0]
    do = matmul(dout, wo, "nt", name="xa_do")
    d_wo = matmul(o, dout, "tn", name="xa_dwo")
    dqp, dk, dv, d_gq = blocked_bwd(_xa_core, [rows(qp, tm, 'blk'), full(k, 'acc'), full(v, 'acc'), full(g_q, 'acc')],
                                    [rows(do, tm)], L // tm, "xa_core_bwd")
    d_wq = matmul(hx, dqp, "tn", name="xa_dwq")
    dhx = matmul(dqp, wq, "nt", name="xa_dhx")
    dh, d_gxa = rms_bwd(h, g_xa, dhx, tm, dout)
    d_gmem, d_wk, d_wv, d_gk = blocked_bwd(
        _mem_kv, [full(mem), full(g_mem, 'acc'), full(wk, 'acc'), full(wv, 'acc'), full(g_k, 'acc')],
        [full(dk), full(dv)], 1, "xa_mem_kv_bwd")
    by_chip = lambda g: g.reshape(N_CHIPS, g.shape[0] // N_CHIPS, g.shape[1])
    grads = {'norm_xa': d_gxa, 'norm_mem': d_gmem, 'xa_q_norm': d_gq, 'xa_k_norm': d_gk,
             'xa_wq': by_chip(d_wq), 'xa_wk': by_chip(d_wk), 'xa_wv': by_chip(d_wv), 'xa_wo': by_chip(d_wo)}
    return dh,grads


def _conv_params(W, lyr):
    cw, cb = W['ffn_conv_w'][lyr], W['ffn_conv_b'][lyr]
    F = cw.shape[1] // 2
    return [cw[0:1, :F], cw[1:2, :F], cw[2:3, :F], cw[0:1, F:], cw[1:2, F:], cw[2:3, F:], _row(cb[:F]), _row(cb[F:])]


def _ffn_fwd(h, W, lyr, tm):
    L, D = h.shape
    w_up, w_down = W['ffn_w_up'][lyr], W['ffn_w_down'][lyr]
    F = w_down.shape[0]
    hf = rms_fwd(h, _row(W['norm_ffn'][lyr]), tm, MXU_DTYPE)
    ug = matmul(hf, w_up[:, :F], name="ffn_up_gate")
    uv = matmul(hf, w_up[:, F:], name="ffn_up_value")
    opds = [cols(ug, 128), cols(uv, 128)] + [cols(p, 128) for p in _conv_params(W, lyr)]
    a = blocked_fwd(_conv_gate, opds, [((L, F), MXU_DTYPE, (L, 128), lambda j: (0, j))], F // 128, "ffn_conv_gate")[0]
    out = matmul(a, w_down, add=h, name="ffn_down")
    return out, (h, hf, ug, uv, a)


def _ffn_bwd(dout, saved, W, lyr, tm):
    h, hf, ug, uv, a = saved
    w_up, w_down = W['ffn_w_up'][lyr], W['ffn_w_down'][lyr]
    F = w_down.shape[0]
    da = matmul(dout, w_down, "nt", name="ffn_da")
    d_wdown = matmul(a, dout, "tn", name="ffn_dwdown")
    opds = [cols(ug, 128, 'blk'), cols(uv, 128, 'blk')] + [cols(p, 128, 'blk') for p in _conv_params(W, lyr)]
    gs = blocked_bwd(_conv_gate, opds, [cols(da, 128)], F // 128, "ffn_conv_gate_bwd")
    dug, duv = gs[0], gs[1]
    d_cw = jnp.concatenate([jnp.concatenate(gs[2:5], axis=0), jnp.concatenate(gs[5:8], axis=0)], axis=1)
    d_cb = jnp.concatenate([gs[8], gs[9]], axis=1)[0]
    d_wup = jnp.concatenate([matmul(hf, dug, "tn", name="ffn_dwup_gate", col_blocks=N_CHIPS // 2),
                             matmul(hf, duv, "tn", name="ffn_dwup_value", col_blocks=N_CHIPS // 2)], axis=0)
    dhf = matmul(dug, w_up[:, :F], "nt", name="ffn_dhf_gate")
    dhf = matmul(duv, w_up[:, F:], "nt", add=dhf, name="ffn_dhf_value")
    dh, d_g = rms_bwd(h, _row(W['norm_ffn'][lyr]), dhf, tm, dout)
    d_wdown = d_wdown.reshape(N_CHIPS, F // N_CHIPS, d_wdown.shape[1])
    return dh,{'norm_ffn': d_g, 'ffn_w_up': d_wup, 'ffn_conv_w': d_cw, 'ffn_conv_b': d_cb, 'ffn_w_down': d_wdown}


def _mla_params(W):
    w_uq = W['mla_w_uq'][0].reshape(MLA_Q_RANK, MLA_HEADS, MLA_QK)
    w_uq = jnp.concatenate([w_uq[..., :MLA_NOPE], _rope_pad(w_uq[..., MLA_NOPE:])], axis=-1)
    w_ukv = W['mla_w_ukv'][0].reshape(MLA_KV_RANK, MLA_HEADS, MLA_NOPE + MLA_V)
    w_ukv = jnp.concatenate([w_ukv[..., :MLA_NOPE].reshape(MLA_KV_RANK, -1), w_ukv[..., MLA_NOPE:].reshape(MLA_KV_RANK, -1)],
                            axis=1)
    return [_row(W['mla_q_a_norm'][0]), w_uq.reshape(MLA_Q_RANK, MLA_HEADS * MLA_DK), _row(W['mla_kv_a_norm'][0]), w_ukv,
            _row(W['mla_qn_nope'][0]), _row(_rope_pad(W['mla_qn_rope'][0])), _row(W['mla_kn_nope'][0]),
            _row(_rope_pad(W['mla_kn_rope'][0]))]


def _w_in_padded(W):
    w = W['mix_w_in'][0]
    return jnp.concatenate([w[:, :IN_WIDTH - MLA_ROPE], _rope_pad(w[:, IN_WIDTH - MLA_ROPE:])], axis=1)


def _mixer0_fwd(h, W, cos_p, sin_p, tm):
    L = h.shape[0]
    t = min(256, L // ATTN_WIDE)
    hn = rms_fwd(h, _row(W['norm_mix'][0]), tm, MXU_DTYPE)
    proj = matmul(hn, _w_in_padded(W), name="mix_in")
    logits = W['hg_lb_logits']
    lb = blocked_fwd(_lb_first, [full(logits)], [((1, HG_WIDTH), F32, (1, HG_WIDTH), lambda i: (0, 0))], 1, "hg_lb")[0]
    gain = _row(W['hg_out_norm'][0])
    o_hg, states = hgrn2_fwd(proj, lb, gain)
    mp = _mla_params(W)
    q, k, v = mla_prep_fwd(proj, cos_p, sin_p, mp, tm)
    scale = MLA_QK ** -0.5
    o_mla, lse = attn_fwd(q, k, v, scale, t)
    w_out = W['mix_w_out'][0]
    out = matmul(o_hg, w_out[:HG_WIDTH], add=h, name="mix_out_hg")
    out = matmul(o_mla, w_out[HG_WIDTH:], add=out, name="mix_out_mla")
    return out, (h, hn, proj, lb, o_hg, states, q, k, v, o_mla, lse)


def _mixer0_bwd(dout, saved, W, cos_p, sin_p, tm):
    h, hn, proj, lb, o_hg, states, q, k, v, o_mla, lse = saved
    L = h.shape[0]
    t = min(256, L // ATTN_WIDE)
    scale = MLA_QK ** -0.5
    w_out = W['mix_w_out'][0]
    gain = _row(W['hg_out_norm'][0])
    do_hg = matmul(dout, w_out[:HG_WIDTH], "nt", name="mix_do_hg")
    do_mla = matmul(dout, w_out[HG_WIDTH:], "nt", name="mix_do_mla")
    d_wout = jnp.concatenate([matmul(o_hg, dout, "tn", name="mix_dwout_hg"), matmul(o_mla, dout, "tn", name="mix_dwout_mla")],
                             axis=0)
    dq = attn_bwd_dq(q, k, v, o_mla, lse, do_mla, scale, t)
    dk, dv = attn_bwd_dkv(q, k, v, o_mla, lse, do_mla, scale, t)
    mp = _mla_params(W)
    dcq, dckv, dkpe, d_qa, d_wuq, d_kva, d_wukv, d_qnn, d_qnr, d_knn, d_knr = mla_prep_bwd(proj, cos_p, sin_p, mp, dq, dk, dv, tm)
    d_hg, d_lb, d_gain = hgrn2_bwd(proj, lb, gain, states, do_hg)
    dproj = jnp.concatenate([d_hg, dcq, dckv, dkpe], axis=1)
    d_win = matmul(hn, dproj, "tn", name="mix_dwin")
    dhn = matmul(dproj, _w_in_padded(W), "nt", name="mix_dhn")
    dh, d_g = rms_bwd(h, _row(W['norm_mix'][0]), dhn, tm, dout)
    logits = W['hg_lb_logits']
    d_logits = blocked_bwd(_lb_first, [full(logits, 'acc')], [full(d_lb)], 1, "hg_lb_bwd")[0]
    d_wuq = d_wuq.reshape(MLA_Q_RANK, MLA_HEADS, MLA_DK)
    d_wuq = jnp.concatenate([d_wuq[..., :MLA_NOPE], _rope_unpad(d_wuq[..., MLA_NOPE:])], axis=-1)
    hw = MLA_HEADS * MLA_NOPE
    d_wukv = jnp.concatenate([d_wukv[:, :hw].reshape(MLA_KV_RANK, MLA_HEADS, MLA_NOPE),
                              d_wukv[:, hw:].reshape(MLA_KV_RANK, MLA_HEADS, MLA_V)], axis=-1)
    d_win = jnp.concatenate([d_win[:, :IN_WIDTH - MLA_ROPE], _rope_unpad(d_win[:, IN_WIDTH - MLA_ROPE:])], axis=1)
    d_win = d_win.reshape(d_win.shape[0], N_CHIPS, W_IN_SHARD).transpose(1, 0, 2)
    d_win = jnp.pad(d_win, ((0, 0), (0, 0), (0, W_IN_SHARD_PAD - W_IN_SHARD)))
    d_wout = d_wout.reshape(N_CHIPS, d_wout.shape[0] // N_CHIPS, d_wout.shape[1])
    grads = {'norm_mix': d_g, 'hg_lb_logits': d_logits, 'mix_w_in': d_win, 'hg_out_norm': d_gain,
             'mla_q_a_norm': d_qa, 'mla_w_uq': d_wuq.reshape(1, MLA_Q_RANK, -1), 'mla_kv_a_norm': d_kva,
             'mla_w_ukv': d_wukv.reshape(1, MLA_KV_RANK, -1), 'mla_qn_nope': d_qnn, 'mla_qn_rope': _rope_unpad(d_qnr),
             'mla_kn_nope': d_knn, 'mla_kn_rope': _rope_unpad(d_knr), 'mix_w_out': d_wout}
    return dh,grads


def _s5_inputs(W):
    G = W['s5_lam_re'].shape[1]
    return [W['s5_lam_re'][0], W['s5_lam_im'][0], W['s5_log_dt'][0].reshape(G, 1),
            W['s5_b_re'][0].reshape(G, -1), W['s5_b_im'][0].reshape(G, -1)]


def _mixer1_fwd(h, W, tm):
    L, D = h.shape
    u = rms_fwd(h, _row(W['norm_mix'][1]), tm, F32)
    di = _s5_inputs(W)
    G = di[0].shape[0]
    sq, wide = ((G, S5_STATE), F32, (G, S5_STATE), lambda i: (0, 0)), ((G, S5_STATE * S5_GROUP), F32, (G, S5_STATE * S5_GROUP), lambda i: (0, 0))
    ar, ai, bbr, bbi = blocked_fwd(_s5_discretize, [full(a) for a in di], [sq, sq, wide, wide], 1, "s5_discretize")
    nb = G // S5_GB
    core = (_blockdiag_in(bbr.reshape(G, S5_STATE, S5_GROUP)), _blockdiag_in(bbi.reshape(G, S5_STATE, S5_GROUP)),
            ar.reshape(nb, 1, S5_LANES), ai.reshape(nb, 1, S5_LANES),
            _blockdiag_out(W['s5_c_re'][0]), _blockdiag_out(W['s5_c_im'][0]))
    y = s5_fwd(u, *core)
    d = W['s5_d']
    y2 = blocked_fwd(_s5_post, [rows(y, tm), rows(u, tm), full(d)], [_out((L, D), MXU_DTYPE, tm)], L // tm, "s5_post")[0]
    w_ab = jnp.concatenate([W['s5_w_glu_a'][0], W['s5_w_glu_b'][0]], axis=1)
    ab = matmul(y2, w_ab, name="s5_glu_in")
    mix = blocked_fwd(_glu, [rows(ab, tm, col=0, width=D), rows(ab, tm, col=1, width=D)], [_out((L, D), F32, tm)], L // tm,
                      "s5_glu")[0]
    return h + mix, (h, u, core, y, y2, ab)


def _mixer1_bwd(dout, saved, W, tm):
    h, u, core, y, y2, ab = saved
    L, D = h.shape
    da, db = blocked_bwd(_glu, [rows(ab, tm, 'blk', col=0, width=D), rows(ab, tm, 'blk', col=1, width=D)], [rows(dout, tm)],
                         L // tm, "s5_glu_bwd")
    w_a, w_b = W['s5_w_glu_a'][0], W['s5_w_glu_b'][0]
    dy2 = matmul(da, w_a, "nt", name="s5_dy2_a")
    dy2 = matmul(db, w_b, "nt", add=dy2, name="s5_dy2_b")
    d_wa = matmul(y2, da, "tn", name="s5_dwa")
    d_wb = matmul(y2, db, "tn", name="s5_dwb")
    d = W['s5_d']
    dy, du_skip, d_d = blocked_bwd(_s5_post, [rows(y, tm, 'blk'), rows(u, tm, 'blk'), full(d, 'acc')], [rows(dy2, tm)], L // tm,
                                   "s5_post_bwd")
    du, dwr, dwi, dar, dai, dcr, dci = s5_bwd(u, *core, dy, min(256, L))
    di = _s5_inputs(W)
    G = di[0].shape[0]
    cts = [dar.reshape(G, S5_STATE), dai.reshape(G, S5_STATE), _blockdiag_in_t(dwr).reshape(G, -1), _blockdiag_in_t(dwi).reshape(G, -1)]
    d_lr, d_li, d_ldt, d_br, d_bi = blocked_bwd(_s5_discretize, [full(a, 'acc') for a in di], [full(c) for c in cts], 1,
                                                "s5_discretize_bwd")
    dh, d_g = rms_bwd(h, _row(W['norm_mix'][1]), du + du_skip, tm, dout)
    bshape = W['s5_b_re'].shape
    grads = {'norm_mix': d_g, 's5_lam_re': d_lr[None], 's5_lam_im': d_li[None], 's5_log_dt': d_ldt.reshape(1, G),
             's5_b_re': d_br.reshape(bshape), 's5_b_im': d_bi.reshape(bshape), 's5_c_re': _blockdiag_out_t(dcr)[None],
             's5_c_im': _blockdiag_out_t(dci)[None], 's5_d': d_d, 's5_w_glu_a': d_wa.reshape(N_CHIPS, -1, D), 's5_w_glu_b': d_wb.reshape(N_CHIPS, -1, D)}
    return dh,grads


def kernel(x, mem, positions, norm_mix, norm_xa, norm_mem, norm_ffn, xa_wq, xa_wk, xa_wv, xa_wo, xa_q_norm, xa_k_norm, ffn_w_up, ffn_conv_w, ffn_conv_b, ffn_w_down, hg_lb_logits, mix_w_in, hg_out_norm, mla_q_a_norm, mla_w_uq, mla_kv_a_norm, mla_w_ukv, mla_qn_nope, mla_qn_rope, mla_kn_nope, mla_kn_rope, mix_w_out, s5_lam_re, s5_lam_im, s5_log_dt, s5_b_re, s5_b_im, s5_c_re, s5_c_im, s5_d, s5_w_glu_a, s5_w_glu_b, loss_target, m_norm_mix, m_norm_xa, m_norm_mem, m_norm_ffn, m_xa_wq, m_xa_wk, m_xa_wv, m_xa_wo, m_xa_q_norm, m_xa_k_norm, m_ffn_w_up, m_ffn_conv_w, m_ffn_conv_b, m_ffn_w_down, m_hg_lb_logits, m_mix_w_in, m_hg_out_norm, m_mla_q_a_norm, m_mla_w_uq, m_mla_kv_a_norm, m_mla_w_ukv, m_mla_qn_nope, m_mla_qn_rope, m_mla_kn_nope, m_mla_kn_rope, m_mix_w_out, m_s5_lam_re, m_s5_lam_im, m_s5_log_dt, m_s5_b_re, m_s5_b_im, m_s5_c_re, m_s5_c_im, m_s5_d, m_s5_w_glu_a, m_s5_w_glu_b, v_norm_mix, v_norm_xa, v_norm_mem, v_norm_ffn, v_xa_wq, v_xa_wk, v_xa_wv, v_xa_wo, v_xa_q_norm, v_xa_k_norm, v_ffn_w_up, v_ffn_conv_w, v_ffn_conv_b, v_ffn_w_down, v_hg_lb_logits, v_mix_w_in, v_hg_out_norm, v_mla_q_a_norm, v_mla_w_uq, v_mla_kv_a_norm, v_mla_w_ukv, v_mla_qn_nope, v_mla_qn_rope, v_mla_kn_nope, v_mla_kn_rope, v_mix_w_out, v_s5_lam_re, v_s5_lam_im, v_s5_log_dt, v_s5_b_re, v_s5_b_im, v_s5_c_re, v_s5_c_im, v_s5_d, v_s5_w_glu_a, v_s5_w_glu_b):
    P = dict(locals())
    assert sorted(P) == sorted(INPUTS) and norm_mix.shape[0] == 2 and mix_w_in.shape[0] == 1
    x, mem, target = P['x'][0], P['mem'][0], P['loss_target'][0]
    L, D = x.shape
    tm = min(256, L)

    W = {n: P[n] for n in REPLICATED}
    W.update(_gather_weights(P))

    inv_freq = 1.0 / (ROPE_BASE ** (jnp.arange(0, MLA_ROPE, 2, dtype=F32) / MLA_ROPE))
    ang = P['positions'][0].astype(F32)[:, None] * inv_freq
    cos, sin, z = jnp.cos(ang), jnp.sin(ang), jnp.zeros_like(ang)
    cos_p = jnp.concatenate([cos, z, cos, z], axis=1)
    sin_p = jnp.concatenate([-sin, z, sin, z], axis=1)

    h, s_mix0 = _mixer0_fwd(x, W, cos_p, sin_p, tm)
    h, s_xa0 = _xattn_fwd(h, mem, W, 0, tm)
    h, s_ffn0 = _ffn_fwd(h, W, 0, tm)
    h, s_mix1 = _mixer1_fwd(h, W, tm)
    h, s_xa1 = _xattn_fwd(h, mem, W, 1, tm)
    h, s_ffn1 = _ffn_fwd(h, W, 1, tm)
    n = L // tm
    dh, parts = blocked_fwd(_loss_fn, [rows(h, tm), rows(target, tm)],
                            [_out((L, D), F32, tm), ((n * 8, 128), F32, (8, 128), lambda i: (i, 0))], n, "loss")
    loss = lax.psum(jnp.sum(parts), ("x", "y", "c"))

    layered = {}

    def collect(g, lyr):
        for k_, v_ in g.items():
            layered.setdefault(k_, {})[lyr] = v_

    dh, g = _ffn_bwd(dh, s_ffn1, W, 1, tm)
    collect(g, 1)
    dh, g = _xattn_bwd(dh, s_xa1, mem, W, 1, tm)
    collect(g, 1)
    dh, g = _mixer1_bwd(dh, s_mix1, W, tm)
    collect(g, 1)
    dh, g = _ffn_bwd(dh, s_ffn0, W, 0, tm)
    collect(g, 0)
    dh, g = _xattn_bwd(dh, s_xa0, mem, W, 0, tm)
    collect(g, 0)
    dx, g = _mixer0_bwd(dh, s_mix0, W, cos_p, sin_p, tm)
    collect(g, 0)

    GB, GS = {}, {}
    for name in WEIGHTS:
        by_layer = [layered[name][lyr] for lyr in sorted(layered[name])]
        if name in BIG:
            GB[name] = _halves_first(by_layer[0]) if len(by_layer) == 1 else jnp.stack(by_layer)
        else:
            full_shape = W[name].shape
            GS[name] = (by_layer[0].reshape(full_shape) if len(by_layer) == 1
                        else jnp.stack([g_.reshape(full_shape[1:]) for g_ in by_layer]))

    outs = _reduce_and_update(GB, GS, P)
    return (loss, dx[None], *[d[n_] for d in outs for n_ in WEIGHTS])
```

```python
import functools
import math

import jax
import jax.numpy as jnp
import numpy as np
from jax import lax
from jax.experimental import pallas as pl
from jax.experimental.pallas import tpu as pltpu

F32 = jnp.float32
BF16 = jnp.bfloat16
MXU_DTYPE = BF16
HI = lax.Precision.HIGHEST
V7X_VMEM_LIMIT_BYTES = 56 * 1024 * 1024
EPS = 1e-6
MESH = pl.DeviceIdType.MESH

HG_HEADS, HG_DIM = 4, 128
HG_WIDTH = HG_HEADS * HG_DIM
HG_SUB = 32
HG_BLOCK = 64
MLA_HEADS, MLA_Q_RANK, MLA_KV_RANK = 4, 256, 128
MLA_NOPE, MLA_ROPE, MLA_V = 128, 64, 128
MLA_QK = MLA_NOPE + MLA_ROPE
MLA_DK = 256
ROPE_BASE = 10000.0
IN_WIDTH = 4 * HG_WIDTH + MLA_Q_RANK + MLA_KV_RANK + MLA_ROPE
IN_PAD = 4 * HG_WIDTH + MLA_Q_RANK + MLA_KV_RANK + 128
S5_GROUP, S5_STATE = 16, 64
S5_GB = 8
DT_MIN, DT_MAX = 1e-3, 1e-1
XA_HEADS = 4
CONV_W = 3
ADAM_LR, ADAM_B1, ADAM_B2, ADAM_EPS, ADAM_WD, ADAM_STEP = 0.001, 0.9, 0.999, 1e-08, 0.01, 10


def _cparams(sem):
    return pltpu.CompilerParams(dimension_semantics=sem, vmem_limit_bytes=V7X_VMEM_LIMIT_BYTES)


class Opd:
    def __init__(self, arr, block, imap, grad=None, gshape=None, gimap=None):
        self.arr, self.block, self.imap, self.grad = arr, block, imap, grad
        self.gshape = arr.shape if gshape is None else gshape
        self.gimap = imap if gimap is None else gimap

    def spec(self):
        return pl.BlockSpec(self.block, self.imap)

    def gspec(self):
        return pl.BlockSpec(self.block, self.gimap)


def rows(arr, tm, grad=None, col=0, width=None):
    width = arr.shape[1] if width is None else width
    return Opd(arr, (tm, width), lambda i, c=col: (i, c), grad, (arr.shape[0], width), lambda i: (i, 0))


def cols(arr, tn, grad=None):
    return Opd(arr, (arr.shape[0], tn), lambda j: (0, j), grad)


def full(arr, grad=None):
    return Opd(arr, arr.shape, lambda i: (0, 0), grad)


def _load(ref):
    v = ref[...]
    return v.astype(F32) if jnp.issubdtype(v.dtype, jnp.floating) else v


def blocked_fwd(f, opds, outs, n, name):
    n_in = len(opds)

    def body(*refs):
        ys = f(*[_load(r) for r in refs[:n_in]])
        for r, y in zip(refs[n_in:], ys):
            r[...] = y.astype(r.dtype)

    res = pl.pallas_call(
        body, name=name, grid=(n,),
        in_specs=[o.spec() for o in opds],
        out_specs=[pl.BlockSpec(b, m) for (_, _, b, m) in outs],
        out_shape=[jax.ShapeDtypeStruct(s, d) for (s, d, _, _) in outs],
        compiler_params=_cparams(("parallel",)),
    )(*[o.arr for o in opds])
    return res


def blocked_bwd(f, opds, dys, n, name, plus=None):
    n_in, n_dy = len(opds), len(dys)
    diff = [i for i, o in enumerate(opds) if o.grad]
    extra = [] if plus is None else [plus]

    def body(*refs):
        vals = [_load(r) for r in refs[:n_in]]

        def fd(*dv):
            allv = list(vals)
            for i, v in zip(diff, dv):
                allv[i] = v
            return tuple(f(*allv))

        ys, vjp = jax.vjp(fd, *[vals[i] for i in diff])
        cts = tuple(_load(r).astype(y.dtype) for r, y in zip(refs[n_in:n_in + n_dy], ys))
        gs = list(vjp(cts))
        if extra:
            gs[0] = gs[0] + _load(refs[n_in + n_dy])
        for r, g, i in zip(refs[n_in + n_dy + len(extra):], gs, diff):
            if opds[i].grad == 'acc':
                @pl.when(pl.program_id(0) == 0)
                def _(r=r):
                    r[...] = jnp.zeros(r.shape, r.dtype)
                r[...] += g.astype(r.dtype)
            else:
                r[...] = g.astype(r.dtype)

    any_acc = any(opds[i].grad == 'acc' for i in diff)
    res = pl.pallas_call(
        body, name=name, grid=(n,),
        in_specs=[o.spec() for o in opds + dys + extra],
        out_specs=[opds[i].gspec() for i in diff],
        out_shape=[jax.ShapeDtypeStruct(opds[i].gshape, F32) for i in diff],
        compiler_params=_cparams(("arbitrary" if any_acc else "parallel",)),
    )(*[o.arr for o in opds + dys + extra])
    return res


def _tile(dim, want):
    for t in range(want - want % 16, 0, -16):
        if dim % t == 0:
            return t
    assert dim <= want, (dim, want)
    return dim


MATMUL_VMEM_BUDGET = 40 * 1024 * 1024
MATMUL_ROWS = 512


def _widest(N, fits):
    for t in range(N - N % 128, 0, -128):
        if N % t == 0 and fits(t):
            return t
    return N


def matmul(a, b, mode="nn", out_dtype=F32, add=None, name="matmul", into=None, col_blocks=None):
    sa, sb, so = a.dtype.itemsize, b.dtype.itemsize, jnp.dtype(out_dtype).itemsize
    has_add = add is not None
    if mode == "tn":
        (K, M), (K2, N) = a.shape, b.shape
        assert K == K2 and not has_add and out_dtype == F32, (a.shape, b.shape)
        tk = _tile(K, MATMUL_ROWS)
        tn = _widest(N, lambda t: 2 * (tk * M * sa + tk * t * sb + M * t * 4) <= MATMUL_VMEM_BUDGET)
        extra, alias = [], {}
        if into is not None:
            buf, lead = into[0], tuple(into[1:])
            if col_blocks is not None:
                assert N % col_blocks == 0 and (N // col_blocks) % 128 == 0 and tn >= N // col_blocks, (N, col_blocks, tn)
                tn = N // col_blocks
                assert buf.shape[len(lead):] == (M, tn), (buf.shape, lead, M, tn)
                out_spec = pl.BlockSpec((None,) * len(lead) + (M, tn), lambda j, k: lead[:-1] + (lead[-1] + j, 0, 0))
            else:
                assert buf.shape[len(lead):] == (M, N), (buf.shape, lead, M, N)
                out_spec = pl.BlockSpec((None,) * len(lead) + (M, tn), lambda j, k: lead + (0, j))
            out_shape = jax.ShapeDtypeStruct(buf.shape, F32)
            extra, alias = [buf], {2: 0}
        else:
            assert col_blocks is None
            out_spec = pl.BlockSpec((M, tn), lambda j, k: (0, j))
            out_shape = jax.ShapeDtypeStruct((M, N), F32)

        def body(a_ref, b_ref, *rest):
            o_ref = rest[-1]
            r = lax.dot_general(a_ref[...].astype(MXU_DTYPE), b_ref[...].astype(MXU_DTYPE), ((_TN), ((), ())),
                                preferred_element_type=F32)

            @pl.when(pl.program_id(1) == 0)
            def _():
                o_ref[...] = r

            @pl.when(pl.program_id(1) > 0)
            def _():
                o_ref[...] += r

        return pl.pallas_call(
            body, name=name, grid=(N // tn, K // tk),
            in_specs=[pl.BlockSpec((tk, M), lambda j, k: (k, 0)), pl.BlockSpec((tk, tn), lambda j, k: (k, j))]
            + [pl.BlockSpec(memory_space=pl.ANY)] * len(extra),
            out_specs=out_spec, out_shape=out_shape, input_output_aliases=alias,
            compiler_params=_cparams(("parallel", "arbitrary")),
        )(a, b, *extra)

    (M, K) = a.shape
    N = b.shape[1] if mode == "nn" else b.shape[0]
    assert K == (b.shape[0] if mode == "nn" else b.shape[1]), (a.shape, b.shape, mode)
    tm = _tile(M, MATMUL_ROWS)
    tn = _widest(N, lambda t: 2 * (tm * K * sa + K * t * sb + tm * t * (so + 4 * has_add)) <= MATMUL_VMEM_BUDGET)
    dims = ((_NN if mode == "nn" else _NT), ((), ()))

    def body(*refs):
        r = lax.dot_general(refs[0][...].astype(MXU_DTYPE), refs[1][...].astype(MXU_DTYPE), dims, preferred_element_type=F32)
        if has_add:
            r = r + refs[2][...].astype(F32)
        refs[-1][...] = r.astype(refs[-1].dtype)

    b_spec = pl.BlockSpec((K, tn), lambda j, i: (0, j)) if mode == "nn" else pl.BlockSpec((tn, K), lambda j, i: (j, 0))
    in_specs = [pl.BlockSpec((tm, K), lambda j, i: (i, 0)), b_spec]
    args = [a, b]
    if has_add:
        in_specs.append(pl.BlockSpec((tm, tn), lambda j, i: (i, j)))
        args.append(add)
    return pl.pallas_call(
        body, name=name, grid=(N // tn, M // tm),
        in_specs=in_specs,
        out_specs=pl.BlockSpec((tm, tn), lambda j, i: (i, j)),
        out_shape=jax.ShapeDtypeStruct((M, N), out_dtype),
        compiler_params=_cparams(("parallel", "parallel")),
    )(*args)


def _dot(a, b, dims, precision=None):
    if precision is None:
        a, b = a.astype(MXU_DTYPE), b.astype(MXU_DTYPE)
    return lax.dot_general(a, b, (dims, ((), ())), precision=precision, preferred_element_type=F32)


_NN = ((1,), (0,))
_NT = ((1,), (1,))
_TN = ((0,), (0,))


def _rms(x, gain):
    return x * lax.rsqrt(jnp.mean(x * x, axis=-1, keepdims=True) + EPS) * gain


def _hg_block(st_t, q, fl, iv, g, lb, gain):
    row = lax.broadcasted_iota(jnp.int32, (HG_SUB, HG_SUB), 0)
    col = lax.broadcasted_iota(jnp.int32, (HG_SUB, HG_SUB), 1)
    tri = (row >= col).astype(F32)
    outs, states = [], []
    for h in range(HG_HEADS):
        sl = slice(h * HG_DIM, (h + 1) * HG_DIM)
        st = st_t[h * HG_DIM:(h + 1) * HG_DIM, :]
        lbh = lb[:, sl]
        fg = lbh + (1.0 - lbh) * jax.nn.sigmoid(fl[:, sl])
        lf, kk, qf, v = jnp.log(fg), 1.0 - fg, jax.nn.silu(q[:, sl]), iv[:, sl]
        parts = []
        for s in range(q.shape[0] // HG_SUB):
            r = slice(s * HG_SUB, (s + 1) * HG_SUB)
            b = _dot(tri, lf[r], _NN, HI)
            b_mid = jnp.sum(lf[r][:HG_SUB // 2], axis=0, keepdims=True)
            b_end = jnp.sum(lf[r], axis=0, keepdims=True)
            sc = _dot(qf[r] * jnp.exp(b - b_mid), kk[r] * jnp.exp(b_mid - b), _NT) * tri
            parts.append(_dot(sc, v[r], _NN) + _dot(qf[r] * jnp.exp(b), st, _NT))
            st = st * jnp.exp(b_end) + _dot(v[r], kk[r] * jnp.exp(b_end - b), _TN)
        o = jnp.concatenate(parts, axis=0)
        outs.append(_rms(o, gain[:, sl]) * jax.nn.silu(g[:, sl]))
        states.append(st)
    return jnp.concatenate(states, axis=0), jnp.concatenate(outs, axis=1)


def _hg_specs(proj, nb):
    return [pl.BlockSpec((HG_BLOCK, HG_WIDTH), lambda i, c=c, f=nb: (f(i), c)) for c in range(4)]


def hgrn2_fwd(proj, lb, gain):
    L = proj.shape[0]
    n = L // HG_BLOCK

    def body(q, fl, iv, g, lb_r, gain_r, o_ref, st_ref, st):
        @pl.when(pl.program_id(0) == 0)
        def _():
            st[...] = jnp.zeros(st.shape, F32)

        st_ref[0] = st[...]
        new, o = _hg_block(st[...], q[...], fl[...], iv[...], g[...], lb_r[...], gain_r[...])
        st[...] = new
        o_ref[...] = o.astype(o_ref.dtype)

    pspec = pl.BlockSpec((1, HG_WIDTH), lambda i: (0, 0))
    return pl.pallas_call(
        body, name="hgrn2_fwd", grid=(n,),
        in_specs=_hg_specs(proj, lambda i: i) + [pspec, pspec],
        out_specs=[pl.BlockSpec((HG_BLOCK, HG_WIDTH), lambda i: (i, 0)),
                   pl.BlockSpec((1, HG_WIDTH, HG_DIM), lambda i: (i, 0, 0))],
        out_shape=[jax.ShapeDtypeStruct((L, HG_WIDTH), MXU_DTYPE),
                   jax.ShapeDtypeStruct((n, HG_WIDTH, HG_DIM), F32)],
        scratch_shapes=[pltpu.VMEM((HG_WIDTH, HG_DIM), F32)],
        compiler_params=_cparams(("arbitrary",)),
    )(proj, proj, proj, proj, lb, gain)


def hgrn2_bwd(proj, lb, gain, states, do):
    L = proj.shape[0]
    n = L // HG_BLOCK

    def body(q, fl, iv, g, lb_r, gain_r, st_r, do_r, dproj, dlb, dgain, dst):
        @pl.when(pl.program_id(0) == 0)
        def _():
            dst[...] = jnp.zeros(dst.shape, F32)
            dlb[...] = jnp.zeros(dlb.shape, F32)
            dgain[...] = jnp.zeros(dgain.shape, F32)

        _, vjp = jax.vjp(_hg_block, st_r[0], q[...], fl[...], iv[...], g[...], lb_r[...], gain_r[...])
        d_st, dq, dfl, div, dg, d_lb, d_gain = vjp((dst[...], do_r[...].astype(F32)))
        dst[...] = d_st
        dproj[:, 0 * HG_WIDTH:1 * HG_WIDTH] = dq
        dproj[:, 1 * HG_WIDTH:2 * HG_WIDTH] = dfl
        dproj[:, 2 * HG_WIDTH:3 * HG_WIDTH] = div
        dproj[:, 3 * HG_WIDTH:4 * HG_WIDTH] = dg
        dlb[...] += d_lb
        dgain[...] += d_gain

    rev = lambda i: n - 1 - i
    pspec = pl.BlockSpec((1, HG_WIDTH), lambda i: (0, 0))
    return pl.pallas_call(
        body, name="hgrn2_bwd", grid=(n,),
        in_specs=_hg_specs(proj, rev) + [pspec, pspec,
                                         pl.BlockSpec((1, HG_WIDTH, HG_DIM), lambda i: (rev(i), 0, 0)),
                                         pl.BlockSpec((HG_BLOCK, HG_WIDTH), lambda i: (rev(i), 0))],
        out_specs=[pl.BlockSpec((HG_BLOCK, 4 * HG_WIDTH), lambda i: (rev(i), 0)), pspec, pspec],
        out_shape=[jax.ShapeDtypeStruct((L, 4 * HG_WIDTH), F32),
                   jax.ShapeDtypeStruct((1, HG_WIDTH), F32), jax.ShapeDtypeStruct((1, HG_WIDTH), F32)],
        scratch_shapes=[pltpu.VMEM((HG_WIDTH, HG_DIM), F32)],
        compiler_params=_cparams(("arbitrary",)),
    )(proj, proj, proj, proj, lb, gain, states, do)


def _rope_rms(x, gain_p, cos_p, sin_p):
    n = x * lax.rsqrt(jnp.sum(x * x, axis=-1, keepdims=True) * (1.0 / MLA_ROPE) + EPS) * gain_p
    r = lax.broadcasted_iota(jnp.int32, (128, 128), 0)
    c = lax.broadcasted_iota(jnp.int32, (128, 128), 1)
    swap = (r == (c + 64) % 128).astype(F32)
    return n * cos_p + _dot(n, swap, _NN, HI) * sin_p


MLA_IN = MLA_Q_RANK + MLA_KV_RANK + 128


def _mla_prep(x, cos_p, sin_p, q_a, w_uq, kv_a, w_ukv, qn_nope, qn_rope, kn_nope, kn_rope):
    c_q, c_kv, kpe = x[:, :MLA_Q_RANK], x[:, MLA_Q_RANK:MLA_Q_RANK + MLA_KV_RANK], x[:, MLA_Q_RANK + MLA_KV_RANK:]
    q = _dot(_rms(c_q, q_a), w_uq, _NN)
    kv = _dot(_rms(c_kv, kv_a), w_ukv, _NN)
    k_pe = _rope_rms(kpe, kn_rope, cos_p, sin_p)
    qs, ks = [], []
    for h in range(MLA_HEADS):
        qs.append(_rms(q[:, h * MLA_DK:h * MLA_DK + MLA_NOPE], qn_nope))
        qs.append(_rope_rms(q[:, h * MLA_DK + MLA_NOPE:(h + 1) * MLA_DK], qn_rope, cos_p, sin_p))
        ks.append(_rms(kv[:, h * MLA_NOPE:(h + 1) * MLA_NOPE], kn_nope))
        ks.append(k_pe)
    return jnp.concatenate(qs, axis=1), jnp.concatenate(ks, axis=1), kv[:, MLA_HEADS * MLA_NOPE:]


def _mla_prep_opds(proj, cos_p, sin_p, params, tm, grads):
    g = (lambda k: k) if grads else (lambda k: None)
    assert (4 * HG_WIDTH) % MLA_IN == 0
    return ([rows(proj, tm, g('blk'), col=4 * HG_WIDTH // MLA_IN, width=MLA_IN), rows(cos_p, tm), rows(sin_p, tm)]
            + [full(p, g('acc')) for p in params])


def mla_prep_fwd(proj, cos_p, sin_p, params, tm):
    L = proj.shape[0]
    W = MLA_HEADS * MLA_DK
    rb = lambda w: (tm, w)
    outs = [((L, W), MXU_DTYPE, rb(W), lambda i: (i, 0)), ((L, W), MXU_DTYPE, rb(W), lambda i: (i, 0)),
            ((L, MLA_HEADS * MLA_V), MXU_DTYPE, rb(MLA_HEADS * MLA_V), lambda i: (i, 0))]
    return blocked_fwd(_mla_prep, _mla_prep_opds(proj, cos_p, sin_p, params, tm, False), outs, L // tm, "mla_prep_fwd")


def mla_prep_bwd(proj, cos_p, sin_p, params, dq, dk, dv, tm):
    L = proj.shape[0]
    return blocked_bwd(_mla_prep, _mla_prep_opds(proj, cos_p, sin_p, params, tm, True),
                       [rows(dq, tm), rows(dk, tm), rows(dv, tm)], L // tm, "mla_prep_bwd")


def _scores(q, k, scale, shift=None):
    s = _dot(q, k, _NT) * scale
    if shift is None:
        return s
    row = lax.broadcasted_iota(jnp.int32, s.shape, 0)
    col = lax.broadcasted_iota(jnp.int32, s.shape, 1)
    return jnp.where(col <= row + shift, s, -jnp.inf)


ATTN_ROWS = 512
ATTN_WIDE = 2


def attn_fwd(q, k, v, scale, t):
    L = q.shape[0]
    tq = ATTN_WIDE * t

    def body(q_ref, k_ref, v_ref, o_ref, lse_ref):
        i = pl.program_id(1)
        qb = q_ref[...]

        def step(j, carry, shift=None):
            m, l, acc = carry
            kj = k_ref[pl.ds(pl.multiple_of(j * t, t), t), :]
            vj = v_ref[pl.ds(pl.multiple_of(j * t, t), t), :]
            s = _scores(qb, kj, scale, shift)
            m_new = jnp.maximum(m, jnp.max(s, axis=-1, keepdims=True))
            p = jnp.exp(s - m_new)
            alpha = jnp.exp(m - m_new)
            return m_new, alpha * l + jnp.sum(p, axis=-1, keepdims=True), alpha * acc + _dot(p, vj, _NN)

        carry = (jnp.full((tq, 1), -jnp.inf, F32), jnp.zeros((tq, 1), F32), jnp.zeros((tq, MLA_V), F32))
        carry = lax.fori_loop(0, ATTN_WIDE * i, step, carry)
        for d in range(ATTN_WIDE):
            carry = step(ATTN_WIDE * i + d, carry, -d * t)
        m, l, acc = carry
        o_ref[...] = acc / l
        lse_ref[...] = jnp.broadcast_to(m + jnp.log(l), lse_ref.shape)

    hspec = lambda rows_, w: pl.BlockSpec((rows_, w), lambda h, i: (0, h))
    bspec = lambda w: pl.BlockSpec((tq, w), lambda h, i: (i, h))
    return pl.pallas_call(
        body, name="attn_fwd", grid=(MLA_HEADS, L // tq),
        in_specs=[bspec(MLA_DK), hspec(L, MLA_DK), hspec(L, MLA_V)],
        out_specs=[bspec(MLA_V), bspec(MLA_V)],
        out_shape=[jax.ShapeDtypeStruct((L, MLA_HEADS * MLA_V), F32)] * 2,
        compiler_params=_cparams(("parallel", "parallel")),
    )(q, k, v)


def attn_bwd_dq(q, k, v, o, lse, do, scale, t):
    L = q.shape[0]
    tq = ATTN_WIDE * t

    def body(q_ref, k_ref, v_ref, o_ref, lse_ref, do_ref, dq_ref):
        i = pl.program_id(1)
        qb, dob = q_ref[...], do_ref[...]
        delta = jnp.sum(dob * o_ref[...], axis=-1, keepdims=True)
        lse_c = jnp.max(lse_ref[...], axis=-1, keepdims=True)

        def step(j, dq, shift=None):
            kj = k_ref[pl.ds(pl.multiple_of(j * t, t), t), :]
            vj = v_ref[pl.ds(pl.multiple_of(j * t, t), t), :]
            p = jnp.exp(_scores(qb, kj, scale, shift) - lse_c)
            ds = p * (_dot(dob, vj, _NT) - delta) * scale
            return dq + _dot(ds, kj, _NN)

        dq = lax.fori_loop(0, ATTN_WIDE * i, step, jnp.zeros((tq, MLA_DK), F32))
        for d in range(ATTN_WIDE):
            dq = step(ATTN_WIDE * i + d, dq, -d * t)
        dq_ref[...] = dq

    hspec = lambda w: pl.BlockSpec((L, w), lambda h, i: (0, h))
    bspec = lambda w: pl.BlockSpec((tq, w), lambda h, i: (i, h))
    return pl.pallas_call(
        body, name="attn_bwd_dq", grid=(MLA_HEADS, L // tq),
        in_specs=[bspec(MLA_DK), hspec(MLA_DK), hspec(MLA_V), bspec(MLA_V), bspec(MLA_V), bspec(MLA_V)],
        out_specs=bspec(MLA_DK),
        out_shape=jax.ShapeDtypeStruct((L, MLA_HEADS * MLA_DK), F32),
        compiler_params=_cparams(("parallel", "parallel")),
    )(q, k, v, o, lse, do)


def attn_bwd_dkv(q, k, v, o, lse, do, scale, t):
    L = q.shape[0]
    tk = ATTN_WIDE * t

    def body(q_ref, k_ref, v_ref, o_ref, lse_ref, do_ref, dk_ref, dv_ref):
        j = pl.program_id(1)
        kb, vb = k_ref[...], v_ref[...]

        def step(i, carry, shift=None):
            dk, dv = carry
            r = pl.ds(pl.multiple_of(i * t, t), t)
            qi, doi = q_ref[r, :], do_ref[r, :]
            delta = jnp.sum(doi * o_ref[r, :], axis=-1, keepdims=True)
            lse_c = jnp.max(lse_ref[r, :], axis=-1, keepdims=True)
            p = jnp.exp(_scores(qi, kb, scale, shift) - lse_c)
            ds = p * (_dot(doi, vb, _NT) - delta) * scale
            return dk + _dot(ds, qi, _TN), dv + _dot(p, doi, _TN)

        carry = (jnp.zeros((tk, MLA_DK), F32), jnp.zeros((tk, MLA_V), F32))
        for d in range(ATTN_WIDE):
            carry = step(ATTN_WIDE * j + d, carry, d * t)
        dk, dv = lax.fori_loop(ATTN_WIDE * (j + 1), L // t, step, carry)
        dk_ref[...] = dk
        dv_ref[...] = dv

    hspec = lambda w: pl.BlockSpec((L, w), lambda h, j: (0, h))
    bspec = lambda w: pl.BlockSpec((tk, w), lambda h, j: (j, h))
    return pl.pallas_call(
        body, name="attn_bwd_dkv", grid=(MLA_HEADS, L // tk),
        in_specs=[hspec(MLA_DK), bspec(MLA_DK), bspec(MLA_V), hspec(MLA_V), hspec(MLA_V), hspec(MLA_V)],
        out_specs=[bspec(MLA_DK), bspec(MLA_V)],
        out_shape=[jax.ShapeDtypeStruct((L, MLA_HEADS * MLA_DK), F32), jax.ShapeDtypeStruct((L, MLA_HEADS * MLA_V), F32)],
        compiler_params=_cparams(("parallel", "parallel")),
    )(q, k, v, o, lse, do)


S5_LANES = S5_GB * S5_STATE


def _cmul(ar, ai, br, bi):
    return ar * br - ai * bi, ar * bi + ai * br


def _a_powers(ar, ai, reverse):
    a2 = _cmul(ar, ai, ar, ai)
    a4 = _cmul(*a2, *a2)
    row = lax.broadcasted_iota(jnp.int32, (8, ar.shape[1]), 0)
    e = (8 - row) if reverse else (row + 1)
    tr, ti = jnp.ones((8, ar.shape[1]), F32), jnp.zeros((8, ar.shape[1]), F32)
    for bit, (pr, pi) in ((1, (ar, ai)), (2, a2), (4, a4), (8, _cmul(*a4, *a4))):
        nr, ni = _cmul(tr, ti, pr, pi)
        sel = (e & bit) != 0
        tr, ti = jnp.where(sel, nr, tr), jnp.where(sel, ni, ti)
    pows = []
    for d, (pr, pi) in zip((1, 2, 4), ((ar, ai), a2, a4)):
        keep = (row < 8 - d) if reverse else (row >= d)
        pows.append((jnp.where(keep, pr, 0.0), jnp.where(keep, pi, 0.0)))
    return pows, (tr, ti)


def _scan8(xr, xi, pows, table, cr, ci, reverse):
    for d, (pr, pi) in zip((1, 2, 4), pows):
        shift = 8 - d if reverse else d
        mr, mi = _cmul(pr, pi, pltpu.roll(xr, shift, 0), pltpu.roll(xi, shift, 0))
        xr, xi = xr + mr, xi + mi
    mr, mi = _cmul(table[0], table[1], cr, ci)
    return xr + mr, xi + mi


def _row_of(x, r):
    row = lax.broadcasted_iota(jnp.int32, x.shape, 0)
    return jnp.sum(jnp.where(row == r, x, 0.0), axis=0, keepdims=True)


def _s5_scan_fwd(h_re, h_im, ar, ai, L):
    pows, table = _a_powers(ar, ai, False)

    def step(i, carry):
        r = pl.ds(pl.multiple_of(i * 8, 8), 8)
        xr, xi = _scan8(h_re[r, :], h_im[r, :], pows, table, carry[0], carry[1], False)
        h_re[r, :] = xr
        h_im[r, :] = xi
        return xr[7:8, :], xi[7:8, :]

    z = jnp.zeros((1, ar.shape[1]), F32)
    lax.fori_loop(0, L // 8, step, (z, z))


def _s5_specs(L):
    return [pl.BlockSpec((L, 128), lambda g: (0, g)),
            pl.BlockSpec((1, 128, S5_LANES), lambda g: (g, 0, 0)), pl.BlockSpec((1, 128, S5_LANES), lambda g: (g, 0, 0)),
            pl.BlockSpec((1, 1, S5_LANES), lambda g: (g, 0, 0)), pl.BlockSpec((1, 1, S5_LANES), lambda g: (g, 0, 0)),
            pl.BlockSpec((1, S5_LANES, 128), lambda g: (g, 0, 0)), pl.BlockSpec((1, S5_LANES, 128), lambda g: (g, 0, 0))]


def s5_fwd(u, w_re, w_im, a_re, a_im, c_re, c_im):
    L, D = u.shape

    def body(u_ref, wr, wi, ar, ai, cr, ci, y_ref, h_re, h_im):
        ub = u_ref[...]
        h_re[...] = _dot(ub, wr[0], _NN)
        h_im[...] = _dot(ub, wi[0], _NN)
        _s5_scan_fwd(h_re, h_im, ar[0], ai[0], L)
        y_ref[...] = _dot(h_re[...], cr[0], _NN) - _dot(h_im[...], ci[0], _NN)

    return pl.pallas_call(
        body, name="s5_fwd", grid=(D // 128,),
        in_specs=_s5_specs(L), out_specs=pl.BlockSpec((L, 128), lambda g: (0, g)),
        out_shape=jax.ShapeDtypeStruct((L, D), F32),
        scratch_shapes=[pltpu.VMEM((L, S5_LANES), F32), pltpu.VMEM((L, S5_LANES), F32)],
        compiler_params=_cparams(("parallel",)),
    )(u, w_re, w_im, a_re, a_im, c_re, c_im)


def s5_bwd(u, w_re, w_im, a_re, a_im, c_re, c_im, dy, tc):
    L, D = u.shape
    nch = L // tc

    def body(u_ref, wr, wi, ar_ref, ai_ref, cr, ci, dy_ref, du_ref, dwr, dwi, dar, dai, dcr, dci, h_re, h_im, g_re, g_im):
        ar, ai = ar_ref[0], ai_ref[0]
        ub = u_ref[...]
        h_re[...] = _dot(ub, wr[0], _NN)
        h_im[...] = _dot(ub, wi[0], _NN)
        _s5_scan_fwd(h_re, h_im, ar, ai, L)
        dyb = dy_ref[...]
        dcr[0] = _dot(h_re[...], dyb, _TN)
        dci[0] = -_dot(h_im[...], dyb, _TN)
        pows, table = _a_powers(ar, -ai, True)
        dwr[0] = jnp.zeros((128, S5_LANES), F32)
        dwi[0] = jnp.zeros((128, S5_LANES), F32)
        z1 = jnp.zeros((1, S5_LANES), F32)
        z8 = jnp.zeros((8, S5_LANES), F32)

        def chunk(cc, carry):
            c0 = pl.multiple_of((nch - 1 - cc) * tc, tc)
            rows_c = pl.ds(c0, tc)
            dyc = dy_ref[rows_c, :]
            g_re[...] = _dot(dyc, cr[0], _NT)
            g_im[...] = -_dot(dyc, ci[0], _NT)

            def step(ii, cy):
                gr_c, gi_c, acc_r, acc_i = cy
                i8 = pl.multiple_of((tc // 8 - 1 - ii) * 8, 8)
                rl = pl.ds(i8, 8)
                xr, xi = _scan8(g_re[rl, :], g_im[rl, :], pows, table, gr_c, gi_c, True)
                g_re[rl, :] = xr
                g_im[rl, :] = xi
                t0 = c0 + i8
                hb_r, hb_i = h_re[pl.ds(t0, 8), :], h_im[pl.ds(t0, 8), :]
                tp = pl.multiple_of(jnp.maximum(t0 - 8, 0), 8)
                first = (t0 > 0).astype(F32)
                pr = h_re[pl.ds(tp, 8), :][7:8, :] * first
                pi = h_im[pl.ds(tp, 8), :][7:8, :] * first
                row = lax.broadcasted_iota(jnp.int32, xr.shape, 0)
                hp_r = jnp.where(row == 0, pr, pltpu.roll(hb_r, 1, 0))
                hp_i = jnp.where(row == 0, pi, pltpu.roll(hb_i, 1, 0))
                return (xr[0:1, :], xi[0:1, :],
                        acc_r + xr * hp_r + xi * hp_i, acc_i + xi * hp_r - xr * hp_i)

            cy = lax.fori_loop(0, tc // 8, step, carry)
            uc = u_ref[rows_c, :]
            gr, gi = g_re[...], g_im[...]
            du_ref[rows_c, :] = _dot(gr, wr[0], _NT) + _dot(gi, wi[0], _NT)
            dwr[0] += _dot(uc, gr, _TN)
            dwi[0] += _dot(uc, gi, _TN)
            return cy

        _, _, acc_r, acc_i = lax.fori_loop(0, nch, chunk, (z1, z1, z8, z8))
        dar[0] = jnp.sum(acc_r, axis=0, keepdims=True)
        dai[0] = jnp.sum(acc_i, axis=0, keepdims=True)

    specs = _s5_specs(L)
    return pl.pallas_call(
        body, name="s5_bwd", grid=(D // 128,),
        in_specs=specs + [pl.BlockSpec((L, 128), lambda g: (0, g))],
        out_specs=[pl.BlockSpec((L, 128), lambda g: (0, g))] + specs[1:],
        out_shape=[jax.ShapeDtypeStruct((L, D), F32)] + [jax.ShapeDtypeStruct(x.shape, F32)
                                                        for x in (w_re, w_im, a_re, a_im, c_re, c_im)],
        scratch_shapes=[pltpu.VMEM((L, S5_LANES), F32), pltpu.VMEM((L, S5_LANES), F32),
                        pltpu.VMEM((tc, S5_LANES), F32), pltpu.VMEM((tc, S5_LANES), F32)],
        compiler_params=_cparams(("parallel",)),
    )(u, w_re, w_im, a_re, a_im, c_re, c_im, dy)


def _s5_discretize(lr, li, ldt, br, bi):
    dt = jnp.exp(ldt)
    mag = jnp.exp(lr * dt)
    ar, ai = mag * jnp.cos(li * dt), mag * jnp.sin(li * dt)
    den = lr * lr + li * li
    zr = ((ar - 1.0) * lr + ai * li) / den
    zi = (ai * lr - (ar - 1.0) * li) / den
    p = lax.broadcasted_iota(jnp.int32, (S5_STATE, S5_STATE * S5_GROUP), 0)
    c = lax.broadcasted_iota(jnp.int32, (S5_STATE, S5_STATE * S5_GROUP), 1)
    rep = (c // S5_GROUP == p).astype(F32)
    zr, zi = _dot(zr, rep, _NN, HI), _dot(zi, rep, _NN, HI)
    return ar, ai, zr * br - zi * bi, zr * bi + zi * br


def _conv_shift(x, d):
    row = lax.broadcasted_iota(jnp.int32, x.shape, 0)
    return jnp.where(row >= d, pltpu.roll(x, d, 0), 0.0)


def _conv_unshift(x, d):
    n = x.shape[0]
    row = lax.broadcasted_iota(jnp.int32, x.shape, 0)
    return jnp.where(row < n - d, pltpu.roll(x, n - d, 0), 0.0)


@functools.partial(jax.custom_vjp, nondiff_argnums=(1,))
def _shift_rows(x, d):
    return _conv_shift(x, d)


_shift_rows.defvjp(lambda x, d: (_conv_shift(x, d), None), lambda d, _, g: (_conv_unshift(g, d),))


def _conv_gate(ug, uv, wg0, wg1, wg2, wv0, wv1, wv2, bg, bv):
    def conv(u, w0, w1, w2, b):
        return u * w2 + _shift_rows(u, 1) * w1 + _shift_rows(u, 2) * w0 + b
    return (jax.nn.silu(conv(ug, wg0, wg1, wg2, bg)) * conv(uv, wv0, wv1, wv2, bv),)


def _rms_fn(x, gain):
    return (_rms(x, gain),)


def _softmax_rows(s):
    e = jnp.exp(s - lax.stop_gradient(jnp.max(s, axis=-1, keepdims=True)))
    return e / jnp.sum(e, axis=-1, keepdims=True)


def _xa_core(qp, k, v, q_gain):
    dh = qp.shape[1] // XA_HEADS
    outs = []
    for h in range(XA_HEADS):
        sl = slice(h * dh, (h + 1) * dh)
        p = _softmax_rows(_dot(_rms(qp[:, sl], q_gain), k[:, sl], _NT) * (dh ** -0.5))
        outs.append(_dot(p, v[:, sl], _NN))
    return (jnp.concatenate(outs, axis=1),)


def _mem_kv(mem, mem_gain, wk, wv, k_gain):
    m = _rms(mem, mem_gain)
    kp = _dot(m, wk, _NN)
    dh = kp.shape[1] // XA_HEADS
    k = jnp.concatenate([_rms(kp[:, h * dh:(h + 1) * dh], k_gain) for h in range(XA_HEADS)], axis=1)
    return k, _dot(m, wv, _NN)


def _s5_post(y, u, d):
    return (jax.nn.gelu(y + d * u),)


def _glu(a, b):
    return (a * jax.nn.sigmoid(b),)


def _lb_first(logits):
    e = jnp.exp(logits - lax.stop_gradient(jnp.max(logits, axis=0, keepdims=True)))
    return (_row_of(e, 0) / jnp.sum(e, axis=0, keepdims=True),)


def _loss_fn(y, t):
    e = y - t
    part = 0.5 * jnp.sum(e * e) / y.shape[1]
    return e * (1.0 / y.shape[1]), jnp.full((8, 128), part / (8 * 128), F32)


def _out(shape, dtype, tm):
    return (shape, dtype, (tm, shape[1]), lambda i: (i, 0))


def rms_fwd(h, gain, tm, dtype):
    return blocked_fwd(_rms_fn, [rows(h, tm), full(gain)], [_out(h.shape, dtype, tm)], h.shape[0] // tm, "rms_fwd")[0]


def rms_bwd(h, gain, dy, tm, residual):
    return blocked_bwd(_rms_fn, [rows(h, tm, 'blk'), full(gain, 'acc')], [rows(dy, tm)], h.shape[0] // tm, "rms_bwd",
                       plus=rows(residual, tm))


def adamw(w, g, m, v, name):
    R = w.shape[0]
    tm = _tile(R, 256)
    assert g.shape == w.shape == m.shape == v.shape, (name, w.shape, g.shape)

    def body(w_ref, g_ref, m_ref, v_ref, d_ref, nm_ref, nv_ref):
        g_ = g_ref[...]
        m_ = ADAM_B1 * m_ref[...] + (1.0 - ADAM_B1) * g_
        v_ = ADAM_B2 * v_ref[...] + (1.0 - ADAM_B2) * jnp.square(g_)
        m_hat = m_ / (1.0 - ADAM_B1 ** ADAM_STEP)
        v_hat = v_ / (1.0 - ADAM_B2 ** ADAM_STEP)
        d_ref[...] = -ADAM_LR * (m_hat / (jnp.sqrt(v_hat) + ADAM_EPS) + ADAM_WD * w_ref[...])
        nm_ref[...] = m_
        nv_ref[...] = v_

    spec = pl.BlockSpec((tm, w.shape[1]), lambda i: (i, 0))
    return pl.pallas_call(
        body, name=name, grid=(R // tm,), in_specs=[spec] * 4, out_specs=[spec] * 3,
        out_shape=[jax.ShapeDtypeStruct(w.shape, F32)] * 3, compiler_params=_cparams(("parallel",)),
    )(w, g, m, v)


def add2(x, y, name, out_dtype=F32):
    shape = x.shape
    x, y = x.reshape(-1, shape[-1]), y.reshape(-1, shape[-1])
    R, C = x.shape
    tm = _tile(R, 256)

    def body(x_ref, y_ref, o_ref):
        o_ref[...] = (x_ref[...] + y_ref[...]).astype(o_ref.dtype)

    spec = pl.BlockSpec((tm, C), lambda i: (i, 0))
    return pl.pallas_call(
        body, name=name, grid=(R // tm,), in_specs=[spec, spec], out_specs=spec,
        out_shape=jax.ShapeDtypeStruct((R, C), out_dtype), compiler_params=_cparams(("parallel",)),
    )(x, y).reshape(shape)


def add_chips(own, got, name):
    n, R, C = got.shape
    tm = _tile(R, 256)

    def body(*refs):
        acc = refs[0][...].astype(F32)
        for r in refs[1:-1]:
            acc = acc + r[...].astype(F32)
        refs[-1][...] = acc

    return pl.pallas_call(
        body, name=name, grid=(R // tm,),
        in_specs=[pl.BlockSpec((tm, C), lambda i: (i, 0))] + [pl.BlockSpec((None, tm, C), lambda i, j=j: (j, i, 0))
                                                            for j in range(n)],
        out_specs=pl.BlockSpec((tm, C), lambda i: (i, 0)),
        out_shape=jax.ShapeDtypeStruct((R, C), F32), compiler_params=_cparams(("parallel",)),
    )(own, *([got] * n))


_HBM = pl.BlockSpec(memory_space=pltpu.HBM)
N_CHIPS = 4


def _my_place():
    return lax.axis_index("x"), lax.axis_index("y"), lax.axis_index("c")


def _window(ref, axis, start, size):
    idx = [slice(None)] * len(ref.shape)
    idx[axis] = pl.ds(start, size)
    return ref.at[tuple(idx)]


def _comm_call(body, name, xs, out_shapes, n_remote, n_local):
    return pl.pallas_call(
        body, name=name, in_specs=[_HBM] * len(xs), out_specs=[_HBM] * len(out_shapes), out_shape=out_shapes,
        scratch_shapes=[pltpu.SemaphoreType.DMA((n_remote,)), pltpu.SemaphoreType.DMA((n_remote,)),
                        pltpu.SemaphoreType.DMA((max(n_local, 1),))],
        compiler_params=pltpu.CompilerParams(has_side_effects=True),
    )(*xs)


def _run(copies):
    for cp in copies:
        cp.start()
    for cp in copies:
        cp.wait()


def _other_chips(mx, my):
    return [(mx ^ (j >> 1), my ^ (j & 1)) for j in (1, 2, 3)]


def chip_gather(xs, axes, name):
    n = len(xs)
    shapes, final = [], []
    for x, ax in zip(xs, axes):
        s = list(x.shape)
        if ax is None:
            shapes.append([N_CHIPS] + s)
            final.append(shapes[-1])
        elif ax < x.ndim - 1:
            shapes.append(s[:ax] + [N_CHIPS] + s[ax:])
            final.append(s[:ax] + [N_CHIPS * s[ax]] + s[ax + 1:])
        else:
            assert s[ax] % 128 == 0, (name, s)
            shapes.append(s[:ax] + [N_CHIPS * s[ax]])
            final.append(shapes[-1])

    def body(*refs):
        x_refs, o_refs = refs[:n], refs[n:2 * n]
        send_sems, recv_sems, local_sems = refs[2 * n:]
        mx, my, mc = _my_place()
        q = 2 * mx + my
        copies = []
        for i, (x_ref, o_ref, ax) in enumerate(zip(x_refs, o_refs, axes)):
            if ax is None or ax < len(x_ref.shape) - 1:
                dst = o_ref.at[(slice(None),) * (ax or 0) + (q,)]
            else:
                dst = _window(o_ref, ax, q * x_ref.shape[ax], x_ref.shape[ax])
            copies.append(pltpu.make_async_copy(x_ref, dst, local_sems.at[i]))
            for j, (tx, ty) in enumerate(_other_chips(mx, my)):
                copies.append(pltpu.make_async_remote_copy(
                    src_ref=x_ref, dst_ref=dst, send_sem=send_sems.at[3 * i + j], recv_sem=recv_sems.at[3 * i + j],
                    device_id=(tx, ty, mc), device_id_type=MESH))
        _run(copies)

    out_shapes = [jax.ShapeDtypeStruct(tuple(s), x.dtype) for s, x in zip(shapes, xs)]
    return [o.reshape(f) for o, f in zip(_comm_call(body, name, xs, out_shapes, 3 * n, n), final)]


def gather_two_level(xs, name):
    n = len(xs)
    shapes = [jax.ShapeDtypeStruct((2, N_CHIPS) + x.shape[1:], x.dtype) for x in xs]

    def body(*refs):
        x_refs, o_refs = refs[:n], refs[n:2 * n]
        send_sems, recv_sems, local_sems = refs[2 * n:]
        mx, my, mc = _my_place()
        q = 2 * mx + my
        first, local, second = [], [], []
        for i, (x_ref, o_ref) in enumerate(zip(x_refs, o_refs)):
            local.append(pltpu.make_async_copy(x_ref.at[mc], o_ref.at[mc, q], local_sems.at[i]))
            for j, (tx, ty) in enumerate(_other_chips(mx, my)):
                first.append(pltpu.make_async_remote_copy(
                    src_ref=x_ref.at[mc], dst_ref=o_ref.at[mc, q], send_sem=send_sems.at[4 * i + j],
                    recv_sem=recv_sems.at[4 * i + j], device_id=(tx, ty, mc), device_id_type=MESH))
            second.append(pltpu.make_async_remote_copy(
                src_ref=o_ref.at[mc], dst_ref=o_ref.at[mc], send_sem=send_sems.at[4 * i + 3],
                recv_sem=recv_sems.at[4 * i + 3], device_id=(mx, my, 1 - mc), device_id_type=MESH))
        for cp in local + first:
            cp.start()
        for cp in local:
            cp.wait()
        for cp in first:
            cp.wait_recv()
        _run(second)
        for cp in first:
            cp.wait_send()

    return _comm_call(body, name, xs, shapes, 4 * n, n)


def pair_swap(xs, name, halves):
    n = len(xs)
    shapes = [jax.ShapeDtypeStruct(x.shape[1:] if halves else x.shape, x.dtype) for x in xs]

    def body(*refs):
        x_refs, o_refs = refs[:n], refs[n:2 * n]
        send_sems, recv_sems, _ = refs[2 * n:]
        mx, my, mc = _my_place()
        _run([pltpu.make_async_remote_copy(
            src_ref=x_ref.at[1 - mc] if halves else x_ref, dst_ref=o_ref, send_sem=send_sems.at[i],
            recv_sem=recv_sems.at[i], device_id=(mx, my, 1 - mc), device_id_type=MESH)
            for i, (x_ref, o_ref) in enumerate(zip(x_refs, o_refs))])

    return _comm_call(body, name, xs, shapes, n, 0)


def chip_all_to_all(xs, name):
    n = len(xs)
    shapes = [jax.ShapeDtypeStruct((N_CHIPS - 1,) + x.shape[1:], x.dtype) for x in xs]

    def body(*refs):
        x_refs, o_refs = refs[:n], refs[n:2 * n]
        send_sems, recv_sems, _ = refs[2 * n:]
        mx, my, mc = _my_place()
        copies = []
        for i, (x_ref, o_ref) in enumerate(zip(x_refs, o_refs)):
            for j, (tx, ty) in enumerate(_other_chips(mx, my)):
                copies.append(pltpu.make_async_remote_copy(
                    src_ref=x_ref.at[2 * tx + ty], dst_ref=o_ref.at[j], send_sem=send_sems.at[3 * i + j],
                    recv_sem=recv_sems.at[3 * i + j], device_id=(tx, ty, mc), device_id_type=MESH))
        _run(copies)

    return _comm_call(body, name, xs, shapes, 3 * n, 0)


WEIGHTS = ['norm_mix', 'norm_xa', 'norm_mem', 'norm_ffn', 'xa_wq', 'xa_wk', 'xa_wv', 'xa_wo', 'xa_q_norm', 'xa_k_norm',
           'ffn_w_up', 'ffn_conv_w', 'ffn_conv_b', 'ffn_w_down', 'hg_lb_logits', 'mix_w_in', 'hg_out_norm',
           'mla_q_a_norm', 'mla_w_uq', 'mla_kv_a_norm', 'mla_w_ukv', 'mla_qn_nope', 'mla_qn_rope', 'mla_kn_nope',
           'mla_kn_rope', 'mix_w_out', 's5_lam_re', 's5_lam_im', 's5_log_dt', 's5_b_re', 's5_b_im', 's5_c_re',
           's5_c_im', 's5_d', 's5_w_glu_a', 's5_w_glu_b']
INPUTS = ['x', 'mem', 'positions'] + WEIGHTS + ['loss_target'] + ['m_' + n for n in WEIGHTS] + ['v_' + n for n in WEIGHTS]
SHARD_AXIS = {'xa_wq': 1, 'xa_wk': 1, 'xa_wv': 1, 'xa_wo': 1, 'ffn_w_up': 2, 'ffn_conv_w': 2, 'ffn_w_down': 1,
              'mix_w_in': 2, 'mla_w_uq': 2, 'mla_w_ukv': 2, 'mix_w_out': 1, 's5_d': 1, 's5_w_glu_a': 1, 's5_w_glu_b': 1}
BIG = ['xa_wq', 'xa_wk', 'xa_wv', 'xa_wo', 'ffn_w_up', 'ffn_w_down', 'mix_w_in', 'mix_w_out', 's5_w_glu_a', 's5_w_glu_b']
SMALL_SHARDED = [n for n in WEIGHTS if n in SHARD_AXIS and n not in BIG]
REPLICATED = [n for n in WEIGHTS if n not in SHARD_AXIS]
SMALL = SMALL_SHARDED + REPLICATED
PACK_W = 1024
ROW_MULT = 16
W_IN_SHARD = IN_WIDTH // N_CHIPS
W_IN_SHARD_PAD = 640


def _pack(flats, mult=ROW_MULT):
    flat = jnp.concatenate([f.reshape(-1) for f in flats])
    unit = mult * PACK_W
    n = -(-flat.shape[0] // unit) * unit
    return jnp.pad(flat, (0, n - flat.shape[0])).reshape(n // PACK_W, PACK_W)


def _unpack(packed, shapes):
    flat, out, o = packed.reshape(-1), [], 0
    for s in shapes:
        n = math.prod(s)
        out.append(flat[o:o + n].reshape(s))
        o += n
    return out


def _rope_pad(w):
    z = jnp.zeros(w.shape[:-1] + (MLA_ROPE // 2,), w.dtype)
    return jnp.concatenate([w[..., :MLA_ROPE // 2], z, w[..., MLA_ROPE // 2:], z], axis=-1)


def _rope_unpad(g):
    return jnp.concatenate([g[..., :MLA_ROPE // 2], g[..., 64:64 + MLA_ROPE // 2]], axis=-1)


def _blockdiag_in(bb):
    nb = bb.shape[0] // S5_GB
    t = bb.reshape(nb, S5_GB, S5_STATE, S5_GROUP).transpose(0, 1, 3, 2)
    return jnp.einsum('bgmp,gh->bgmhp', t, jnp.eye(S5_GB, dtype=bb.dtype)).reshape(nb, S5_GB * S5_GROUP, S5_LANES)


def _blockdiag_in_t(dw):
    nb = dw.shape[0]
    t = jnp.einsum('bgmhp,gh->bgmp', dw.reshape(nb, S5_GB, S5_GROUP, S5_GB, S5_STATE), jnp.eye(S5_GB, dtype=dw.dtype))
    return t.transpose(0, 1, 3, 2).reshape(nb * S5_GB, S5_STATE, S5_GROUP)


def _blockdiag_out(c):
    nb = c.shape[0] // S5_GB
    t = c.reshape(nb, S5_GB, S5_GROUP, S5_STATE).transpose(0, 1, 3, 2)
    return jnp.einsum('bgpm,gh->bgphm', t, jnp.eye(S5_GB, dtype=c.dtype)).reshape(nb, S5_LANES, S5_GB * S5_GROUP)


def _blockdiag_out_t(dc):
    nb = dc.shape[0]
    t = jnp.einsum('bgphm,gh->bgpm', dc.reshape(nb, S5_GB, S5_STATE, S5_GB, S5_GROUP), jnp.eye(S5_GB, dtype=dc.dtype))
    return t.transpose(0, 1, 3, 2).reshape(nb * S5_GB, S5_GROUP, S5_STATE)


def _gather_weights(P):
    def halves(x):
        return x if x.shape[0] == 2 else x.reshape(2, x.shape[1] // 2, x.shape[2])

    xs = [halves(P[n].astype(BF16)) for n in BIG] + [halves(_pack([P[n] for n in SMALL_SHARDED], 2 * ROW_MULT)[None])]
    got = gather_two_level(xs, "gather_weights")
    full_w = {}
    for n, g in zip(BIG, got[:-1]):
        two_layers, by_rows = P[n].shape[0] == 2, SHARD_AXIS[n] == 1
        if two_layers and by_rows:
            full_w[n] = g.reshape(2, N_CHIPS * g.shape[2], g.shape[3])
        elif two_layers:
            full_w[n] = g.transpose(0, 2, 1, 3).reshape(2, g.shape[2], N_CHIPS * g.shape[3])
        elif by_rows:
            full_w[n] = g.transpose(1, 0, 2, 3).reshape(1, 2 * N_CHIPS * g.shape[2], g.shape[3])
        else:
            full_w[n] = g.transpose(0, 2, 1, 3).reshape(1, 2 * g.shape[2], N_CHIPS * g.shape[3])
    small = got[-1].transpose(1, 0, 2, 3).reshape(N_CHIPS, -1, PACK_W)
    per_chip = [_unpack(small[q], [P[n].shape for n in SMALL_SHARDED]) for q in range(N_CHIPS)]
    for i, n in enumerate(SMALL_SHARDED):
        full_w[n] = jnp.concatenate([per_chip[q][i] for q in range(N_CHIPS)], axis=SHARD_AXIS[n])
    return full_w


def _halves_first(x):
    return x.reshape(x.shape[0], 2, x.shape[1] // 2, x.shape[2]).transpose(1, 0, 2, 3)


def _reduce_and_update(GB, GS, P):
    mx, my, mc = _my_place()
    q = 2 * mx + my
    small = _pack([GS[n] for n in SMALL], 2 * N_CHIPS * ROW_MULT)
    xs = [GB[n] for n in BIG] + [_halves_first(small.reshape(N_CHIPS, -1, PACK_W))]
    theirs = pair_swap(xs, "grads_pair_swap", True)
    names = BIG + ['small']
    pair = [add2(lax.dynamic_index_in_dim(x, mc, 0, False), t, "grads_pair_sum_" + n, F32 if n == 'small' else BF16)
            for x, t, n in zip(xs, theirs, names)]
    got = chip_all_to_all(pair, "grads_chip_all_to_all")
    summed = [add_chips(lax.dynamic_index_in_dim(p, q, 0, False), g, "grads_chip_sum_" + n)
              for p, g, n in zip(pair, got, names)]
    other = pair_swap(summed, "grads_pair_join", False)
    joined = [lax.cond(mc == 0, lambda a, b: jnp.concatenate([a, b], axis=0), lambda a, b: jnp.concatenate([b, a], axis=0),
                       s, o) for s, o in zip(summed, other)]
    small_sum = chip_gather([joined[-1]], [0], "grads_small_gather")[0]
    g_small = dict(zip(SMALL, _unpack(small_sum, [GS[n].shape for n in SMALL])))
    for n in SMALL_SHARDED:
        s = P[n].shape[SHARD_AXIS[n]]
        g_small[n] = lax.dynamic_slice_in_dim(g_small[n], q * s, s, axis=SHARD_AXIS[n])

    grad, delta, new_m, new_v = {}, {}, {}, {}
    for n, g in zip(BIG, joined[:-1]):
        shape = P[n].shape
        if n == 'mix_w_in':
            g = g[:, :W_IN_SHARD]
        two_d = (g.shape[0], shape[-1])
        d, m_, v_ = adamw(P[n].reshape(two_d), g, P['m_' + n].reshape(two_d), P['v_' + n].reshape(two_d), "adamw_" + n)
        grad[n], delta[n], new_m[n], new_v[n] = (t.reshape(shape) for t in (g, d, m_, v_))
    packed = lambda prefix: _pack([P[prefix + n] for n in SMALL])
    d, m_, v_ = adamw(packed(''), _pack([g_small[n] for n in SMALL]), packed('m_'), packed('v_'), "adamw_small")
    shapes = [P[n].shape for n in SMALL]
    grad.update(g_small)
    for out, pk in ((delta, d), (new_m, m_), (new_v, v_)):
        out.update(zip(SMALL, _unpack(pk, shapes)))
    return grad, delta, new_m, new_v


def _row(v):
    return v.reshape(1, -1)


def _xattn_fwd(h, mem, W, lyr, tm):
    g_xa, g_mem = _row(W['norm_xa'][lyr]), _row(W['norm_mem'][lyr])
    g_q, g_k = _row(W['xa_q_norm'][lyr]), _row(W['xa_k_norm'][lyr])
    wq, wk, wv, wo = (W[n][lyr] for n in ('xa_wq', 'xa_wk', 'xa_wv', 'xa_wo'))
    L, D = h.shape
    M = mem.shape[0]
    hx = rms_fwd(h, g_xa, tm, MXU_DTYPE)
    qp = matmul(hx, wq, name="xa_q")
    kv_opds = [full(mem), full(g_mem), full(wk), full(wv), full(g_k)]
    k, v = blocked_fwd(_mem_kv, kv_opds, [((M, D), F32, (M, D), lambda i: (0, 0))] * 2, 1, "xa_mem_kv")
    o = blocked_fwd(_xa_core, [rows(qp, tm), full(k), full(v), full(g_q)], [_out((L, D), MXU_DTYPE, tm)], L // tm,
                    "xa_core")[0]
    out = matmul(o, wo, add=h, name="xa_o")
    return out, (h, hx, qp, k, v, o)


def _xattn_bwd(dout, saved, mem, W, lyr, tm, bufs):
    h, hx, qp, k, v, o = saved
    g_xa, g_mem = _row(W['norm_xa'][lyr]), _row(W['norm_mem'][lyr])
    g_q, g_k = _row(W['xa_q_norm'][lyr]), _row(W['xa_k_norm'][lyr])
    wq, wk, wv, wo = (W[n][lyr] for n in ('xa_wq', 'xa_wk', 'xa_wv', 'xa_wo'))
    L = h.shape[0]
    do = matmul(dout, wo, "nt", name="xa_do")
    bufs['xa_wo'] = matmul(o, dout, "tn", name="xa_dwo", into=(bufs['xa_wo'], lyr))
    dqp, dk, dv, d_gq = blocked_bwd(_xa_core, [rows(qp, tm, 'blk'), full(k, 'acc'), full(v, 'acc'), full(g_q, 'acc')],
                                    [rows(do, tm)], L // tm, "xa_core_bwd")
    bufs['xa_wq'] = matmul(hx, dqp, "tn", name="xa_dwq", into=(bufs['xa_wq'], lyr))
    dhx = matmul(dqp, wq, "nt", name="xa_dhx")
    dh, d_gxa = rms_bwd(h, g_xa, dhx, tm, dout)
    d_gmem, d_wk, d_wv, d_gk = blocked_bwd(
        _mem_kv, [full(mem), full(g_mem, 'acc'), full(wk, 'acc'), full(wv, 'acc'), full(g_k, 'acc')],
        [full(dk), full(dv)], 1, "xa_mem_kv_bwd")
    by_chip = lambda g: g.reshape(N_CHIPS, g.shape[0] // N_CHIPS, g.shape[1])
    grads = {'norm_xa': d_gxa, 'norm_mem': d_gmem, 'xa_q_norm': d_gq, 'xa_k_norm': d_gk,
             'xa_wk': by_chip(d_wk), 'xa_wv': by_chip(d_wv)}
    return dh, grads


def _conv_params(W, lyr):
    cw, cb = W['ffn_conv_w'][lyr], W['ffn_conv_b'][lyr]
    F = cw.shape[1] // 2
    return [cw[0:1, :F], cw[1:2, :F], cw[2:3, :F], cw[0:1, F:], cw[1:2, F:], cw[2:3, F:], _row(cb[:F]), _row(cb[F:])]


def _ffn_fwd(h, W, lyr, tm):
    L, D = h.shape
    w_up, w_down = W['ffn_w_up'][lyr], W['ffn_w_down'][lyr]
    F = w_down.shape[0]
    hf = rms_fwd(h, _row(W['norm_ffn'][lyr]), tm, MXU_DTYPE)
    ug = matmul(hf, w_up[:, :F], name="ffn_up_gate")
    uv = matmul(hf, w_up[:, F:], name="ffn_up_value")
    opds = [cols(ug, 128), cols(uv, 128)] + [cols(p, 128) for p in _conv_params(W, lyr)]
    a = blocked_fwd(_conv_gate, opds, [((L, F), MXU_DTYPE, (L, 128), lambda j: (0, j))], F // 128, "ffn_conv_gate")[0]
    out = matmul(a, w_down, add=h, name="ffn_down")
    return out, (h, hf, ug, uv, a)


def _ffn_bwd(dout, saved, W, lyr, tm, bufs):
    h, hf, ug, uv, a = saved
    w_up, w_down = W['ffn_w_up'][lyr], W['ffn_w_down'][lyr]
    F = w_down.shape[0]
    da = matmul(dout, w_down, "nt", name="ffn_da")
    bufs['ffn_w_down'] = matmul(a, dout, "tn", name="ffn_dwdown", into=(bufs['ffn_w_down'], lyr))
    opds = [cols(ug, 128, 'blk'), cols(uv, 128, 'blk')] + [cols(p, 128, 'blk') for p in _conv_params(W, lyr)]
    gs = blocked_bwd(_conv_gate, opds, [cols(da, 128)], F // 128, "ffn_conv_gate_bwd")
    dug, duv = gs[0], gs[1]
    d_cw = jnp.concatenate([jnp.concatenate(gs[2:5], axis=0), jnp.concatenate(gs[5:8], axis=0)], axis=1)
    d_cb = jnp.concatenate([gs[8], gs[9]], axis=1)[0]
    half = N_CHIPS // 2
    bufs['ffn_w_up'] = matmul(hf, dug, "tn", name="ffn_dwup_gate", into=(bufs['ffn_w_up'], lyr, 0), col_blocks=half)
    bufs['ffn_w_up'] = matmul(hf, duv, "tn", name="ffn_dwup_value", into=(bufs['ffn_w_up'], lyr, half), col_blocks=half)
    dhf = matmul(dug, w_up[:, :F], "nt", name="ffn_dhf_gate")
    dhf = matmul(duv, w_up[:, F:], "nt", add=dhf, name="ffn_dhf_value")
    dh, d_g = rms_bwd(h, _row(W['norm_ffn'][lyr]), dhf, tm, dout)
    return dh, {'norm_ffn': d_g, 'ffn_conv_w': d_cw, 'ffn_conv_b': d_cb}


def _mla_params(W):
    w_uq = W['mla_w_uq'][0].reshape(MLA_Q_RANK, MLA_HEADS, MLA_QK)
    w_uq = jnp.concatenate([w_uq[..., :MLA_NOPE], _rope_pad(w_uq[..., MLA_NOPE:])], axis=-1)
    w_ukv = W['mla_w_ukv'][0].reshape(MLA_KV_RANK, MLA_HEADS, MLA_NOPE + MLA_V)
    w_ukv = jnp.concatenate([w_ukv[..., :MLA_NOPE].reshape(MLA_KV_RANK, -1), w_ukv[..., MLA_NOPE:].reshape(MLA_KV_RANK, -1)],
                            axis=1)
    return [_row(W['mla_q_a_norm'][0]), w_uq.reshape(MLA_Q_RANK, MLA_HEADS * MLA_DK), _row(W['mla_kv_a_norm'][0]), w_ukv,
            _row(W['mla_qn_nope'][0]), _row(_rope_pad(W['mla_qn_rope'][0])), _row(W['mla_kn_nope'][0]),
            _row(_rope_pad(W['mla_kn_rope'][0]))]


def _w_in_padded(W):
    w = W['mix_w_in'][0]
    return jnp.concatenate([w[:, :IN_WIDTH - MLA_ROPE], _rope_pad(w[:, IN_WIDTH - MLA_ROPE:])], axis=1)


def _mixer0_fwd(h, W, cos_p, sin_p, tm):
    L = h.shape[0]
    t = min(ATTN_ROWS, L // ATTN_WIDE)
    hn = rms_fwd(h, _row(W['norm_mix'][0]), tm, MXU_DTYPE)
    proj = matmul(hn, _w_in_padded(W), name="mix_in")
    logits = W['hg_lb_logits']
    lb = blocked_fwd(_lb_first, [full(logits)], [((1, HG_WIDTH), F32, (1, HG_WIDTH), lambda i: (0, 0))], 1, "hg_lb")[0]
    gain = _row(W['hg_out_norm'][0])
    o_hg, states = hgrn2_fwd(proj, lb, gain)
    mp = _mla_params(W)
    q, k, v = mla_prep_fwd(proj, cos_p, sin_p, mp, tm)
    scale = MLA_QK ** -0.5
    o_mla, lse = attn_fwd(q, k, v, scale, t)
    w_out = W['mix_w_out'][0]
    out = matmul(o_hg, w_out[:HG_WIDTH], add=h, name="mix_out_hg")
    out = matmul(o_mla, w_out[HG_WIDTH:], add=out, name="mix_out_mla")
    return out, (h, hn, proj, lb, o_hg, states, q, k, v, o_mla, lse)


def _mixer0_bwd(dout, saved, W, cos_p, sin_p, tm):
    h, hn, proj, lb, o_hg, states, q, k, v, o_mla, lse = saved
    L = h.shape[0]
    t = min(ATTN_ROWS, L // ATTN_WIDE)
    scale = MLA_QK ** -0.5
    w_out = W['mix_w_out'][0]
    gain = _row(W['hg_out_norm'][0])
    do_hg = matmul(dout, w_out[:HG_WIDTH], "nt", name="mix_do_hg")
    do_mla = matmul(dout, w_out[HG_WIDTH:], "nt", name="mix_do_mla")
    d_wout = jnp.concatenate([matmul(o_hg, dout, "tn", name="mix_dwout_hg"), matmul(o_mla, dout, "tn", name="mix_dwout_mla")],
                             axis=0)
    dq = attn_bwd_dq(q, k, v, o_mla, lse, do_mla, scale, t)
    dk, dv = attn_bwd_dkv(q, k, v, o_mla, lse, do_mla, scale, t)
    mp = _mla_params(W)
    d_mla, d_qa, d_wuq, d_kva, d_wukv, d_qnn, d_qnr, d_knn, d_knr = mla_prep_bwd(proj, cos_p, sin_p, mp, dq, dk, dv, tm)
    d_hg, d_lb, d_gain = hgrn2_bwd(proj, lb, gain, states, do_hg)
    w_in, n_hg = _w_in_padded(W), 4 * HG_WIDTH
    d_win = jnp.concatenate([matmul(hn, d_hg, "tn", name="mix_dwin_hg"), matmul(hn, d_mla, "tn", name="mix_dwin_mla")], axis=1)
    dhn = matmul(d_hg, w_in[:, :n_hg], "nt", name="mix_dhn_hg")
    dhn = matmul(d_mla, w_in[:, n_hg:], "nt", add=dhn, name="mix_dhn_mla")
    dh, d_g = rms_bwd(h, _row(W['norm_mix'][0]), dhn, tm, dout)
    logits = W['hg_lb_logits']
    d_logits = blocked_bwd(_lb_first, [full(logits, 'acc')], [full(d_lb)], 1, "hg_lb_bwd")[0]
    d_wuq = d_wuq.reshape(MLA_Q_RANK, MLA_HEADS, MLA_DK)
    d_wuq = jnp.concatenate([d_wuq[..., :MLA_NOPE], _rope_unpad(d_wuq[..., MLA_NOPE:])], axis=-1)
    hw = MLA_HEADS * MLA_NOPE
    d_wukv = jnp.concatenate([d_wukv[:, :hw].reshape(MLA_KV_RANK, MLA_HEADS, MLA_NOPE),
                              d_wukv[:, hw:].reshape(MLA_KV_RANK, MLA_HEADS, MLA_V)], axis=-1)
    d_win = jnp.concatenate([d_win[:, :IN_WIDTH - MLA_ROPE], _rope_unpad(d_win[:, IN_WIDTH - MLA_ROPE:])], axis=1)
    d_win = d_win.reshape(d_win.shape[0], N_CHIPS, W_IN_SHARD).transpose(1, 0, 2)
    d_win = jnp.pad(d_win, ((0, 0), (0, 0), (0, W_IN_SHARD_PAD - W_IN_SHARD)))
    d_wout = d_wout.reshape(N_CHIPS, d_wout.shape[0] // N_CHIPS, d_wout.shape[1])
    grads = {'norm_mix': d_g, 'hg_lb_logits': d_logits, 'mix_w_in': d_win, 'hg_out_norm': d_gain,
             'mla_q_a_norm': d_qa, 'mla_w_uq': d_wuq.reshape(1, MLA_Q_RANK, -1), 'mla_kv_a_norm': d_kva,
             'mla_w_ukv': d_wukv.reshape(1, MLA_KV_RANK, -1), 'mla_qn_nope': d_qnn, 'mla_qn_rope': _rope_unpad(d_qnr),
             'mla_kn_nope': d_knn, 'mla_kn_rope': _rope_unpad(d_knr), 'mix_w_out': d_wout}
    return dh,grads


def _s5_inputs(W):
    G = W['s5_lam_re'].shape[1]
    return [W['s5_lam_re'][0], W['s5_lam_im'][0], W['s5_log_dt'][0].reshape(G, 1),
            W['s5_b_re'][0].reshape(G, -1), W['s5_b_im'][0].reshape(G, -1)]


def _mixer1_fwd(h, W, tm):
    L, D = h.shape
    u = rms_fwd(h, _row(W['norm_mix'][1]), tm, F32)
    di = _s5_inputs(W)
    G = di[0].shape[0]
    sq, wide = ((G, S5_STATE), F32, (G, S5_STATE), lambda i: (0, 0)), ((G, S5_STATE * S5_GROUP), F32, (G, S5_STATE * S5_GROUP), lambda i: (0, 0))
    ar, ai, bbr, bbi = blocked_fwd(_s5_discretize, [full(a) for a in di], [sq, sq, wide, wide], 1, "s5_discretize")
    nb = G // S5_GB
    core = (_blockdiag_in(bbr.reshape(G, S5_STATE, S5_GROUP)), _blockdiag_in(bbi.reshape(G, S5_STATE, S5_GROUP)),
            ar.reshape(nb, 1, S5_LANES), ai.reshape(nb, 1, S5_LANES),
            _blockdiag_out(W['s5_c_re'][0]), _blockdiag_out(W['s5_c_im'][0]))
    y = s5_fwd(u, *core)
    d = W['s5_d']
    y2 = blocked_fwd(_s5_post, [rows(y, tm), rows(u, tm), full(d)], [_out((L, D), MXU_DTYPE, tm)], L // tm, "s5_post")[0]
    w_ab = jnp.concatenate([W['s5_w_glu_a'][0], W['s5_w_glu_b'][0]], axis=1)
    ab = matmul(y2, w_ab, name="s5_glu_in")
    mix = blocked_fwd(_glu, [rows(ab, tm, col=0, width=D), rows(ab, tm, col=1, width=D)], [_out((L, D), F32, tm)], L // tm,
                      "s5_glu")[0]
    return h + mix, (h, u, core, y, y2, ab)


def _mixer1_bwd(dout, saved, W, tm):
    h, u, core, y, y2, ab = saved
    L, D = h.shape
    da, db = blocked_bwd(_glu, [rows(ab, tm, 'blk', col=0, width=D), rows(ab, tm, 'blk', col=1, width=D)], [rows(dout, tm)],
                         L // tm, "s5_glu_bwd")
    w_a, w_b = W['s5_w_glu_a'][0], W['s5_w_glu_b'][0]
    dy2 = matmul(da, w_a, "nt", name="s5_dy2_a")
    dy2 = matmul(db, w_b, "nt", add=dy2, name="s5_dy2_b")
    d_wa = matmul(y2, da, "tn", name="s5_dwa")
    d_wb = matmul(y2, db, "tn", name="s5_dwb")
    d = W['s5_d']
    dy, du_skip, d_d = blocked_bwd(_s5_post, [rows(y, tm, 'blk'), rows(u, tm, 'blk'), full(d, 'acc')], [rows(dy2, tm)], L // tm,
                                   "s5_post_bwd")
    du, dwr, dwi, dar, dai, dcr, dci = s5_bwd(u, *core, dy, min(256, L))
    di = _s5_inputs(W)
    G = di[0].shape[0]
    cts = [dar.reshape(G, S5_STATE), dai.reshape(G, S5_STATE), _blockdiag_in_t(dwr).reshape(G, -1), _blockdiag_in_t(dwi).reshape(G, -1)]
    d_lr, d_li, d_ldt, d_br, d_bi = blocked_bwd(_s5_discretize, [full(a, 'acc') for a in di], [full(c) for c in cts], 1,
                                                "s5_discretize_bwd")
    dh, d_g = rms_bwd(h, _row(W['norm_mix'][1]), du + du_skip, tm, dout)
    bshape = W['s5_b_re'].shape
    grads = {'norm_mix': d_g, 's5_lam_re': d_lr[None], 's5_lam_im': d_li[None], 's5_log_dt': d_ldt.reshape(1, G),
             's5_b_re': d_br.reshape(bshape), 's5_b_im': d_bi.reshape(bshape), 's5_c_re': _blockdiag_out_t(dcr)[None],
             's5_c_im': _blockdiag_out_t(dci)[None], 's5_d': d_d, 's5_w_glu_a': d_wa.reshape(N_CHIPS, -1, D), 's5_w_glu_b': d_wb.reshape(N_CHIPS, -1, D)}
    return dh,grads


def kernel(x, mem, positions, norm_mix, norm_xa, norm_mem, norm_ffn, xa_wq, xa_wk, xa_wv, xa_wo, xa_q_norm, xa_k_norm, ffn_w_up, ffn_conv_w, ffn_conv_b, ffn_w_down, hg_lb_logits, mix_w_in, hg_out_norm, mla_q_a_norm, mla_w_uq, mla_kv_a_norm, mla_w_ukv, mla_qn_nope, mla_qn_rope, mla_kn_nope, mla_kn_rope, mix_w_out, s5_lam_re, s5_lam_im, s5_log_dt, s5_b_re, s5_b_im, s5_c_re, s5_c_im, s5_d, s5_w_glu_a, s5_w_glu_b, loss_target, m_norm_mix, m_norm_xa, m_norm_mem, m_norm_ffn, m_xa_wq, m_xa_wk, m_xa_wv, m_xa_wo, m_xa_q_norm, m_xa_k_norm, m_ffn_w_up, m_ffn_conv_w, m_ffn_conv_b, m_ffn_w_down, m_hg_lb_logits, m_mix_w_in, m_hg_out_norm, m_mla_q_a_norm, m_mla_w_uq, m_mla_kv_a_norm, m_mla_w_ukv, m_mla_qn_nope, m_mla_qn_rope, m_mla_kn_nope, m_mla_kn_rope, m_mix_w_out, m_s5_lam_re, m_s5_lam_im, m_s5_log_dt, m_s5_b_re, m_s5_b_im, m_s5_c_re, m_s5_c_im, m_s5_d, m_s5_w_glu_a, m_s5_w_glu_b, v_norm_mix, v_norm_xa, v_norm_mem, v_norm_ffn, v_xa_wq, v_xa_wk, v_xa_wv, v_xa_wo, v_xa_q_norm, v_xa_k_norm, v_ffn_w_up, v_ffn_conv_w, v_ffn_conv_b, v_ffn_w_down, v_hg_lb_logits, v_mix_w_in, v_hg_out_norm, v_mla_q_a_norm, v_mla_w_uq, v_mla_kv_a_norm, v_mla_w_ukv, v_mla_qn_nope, v_mla_qn_rope, v_mla_kn_nope, v_mla_kn_rope, v_mix_w_out, v_s5_lam_re, v_s5_lam_im, v_s5_log_dt, v_s5_b_re, v_s5_b_im, v_s5_c_re, v_s5_c_im, v_s5_d, v_s5_w_glu_a, v_s5_w_glu_b):
    P = dict(locals())
    assert sorted(P) == sorted(INPUTS) and norm_mix.shape[0] == 2 and mix_w_in.shape[0] == 1
    x, mem, target = P['x'][0], P['mem'][0], P['loss_target'][0]
    L, D = x.shape
    tm = min(256, L)

    W = {n: P[n] for n in REPLICATED}
    W.update(_gather_weights(P))

    inv_freq = 1.0 / (ROPE_BASE ** (jnp.arange(0, MLA_ROPE, 2, dtype=F32) / MLA_ROPE))
    ang = P['positions'][0].astype(F32)[:, None] * inv_freq
    cos, sin, z = jnp.cos(ang), jnp.sin(ang), jnp.zeros_like(ang)
    cos_p = jnp.concatenate([cos, z, cos, z], axis=1)
    sin_p = jnp.concatenate([-sin, z, sin, z], axis=1)

    h, s_mix0 = _mixer0_fwd(x, W, cos_p, sin_p, tm)
    h, s_xa0 = _xattn_fwd(h, mem, W, 0, tm)
    h, s_ffn0 = _ffn_fwd(h, W, 0, tm)
    h, s_mix1 = _mixer1_fwd(h, W, tm)
    h, s_xa1 = _xattn_fwd(h, mem, W, 1, tm)
    h, s_ffn1 = _ffn_fwd(h, W, 1, tm)
    n = L // tm
    dh, parts = blocked_fwd(_loss_fn, [rows(h, tm), rows(target, tm)],
                            [_out((L, D), F32, tm), ((n * 8, 128), F32, (8, 128), lambda i: (i, 0))], n, "loss")
    loss = lax.psum(jnp.sum(parts), ("x", "y", "c"))

    layered = {}

    def collect(g, lyr):
        for k_, v_ in g.items():
            layered.setdefault(k_, {})[lyr] = v_

    bufs = {n_: lax.empty(W[n_].shape, F32) for n_ in ('xa_wq', 'xa_wo', 'ffn_w_down')}
    d_ff = W['ffn_w_up'].shape[2] // N_CHIPS
    bufs['ffn_w_up'] = lax.empty((2, N_CHIPS, D, d_ff), F32)
    dh, g = _ffn_bwd(dh, s_ffn1, W, 1, tm, bufs)
    collect(g, 1)
    dh, g = _xattn_bwd(dh, s_xa1, mem, W, 1, tm, bufs)
    collect(g, 1)
    dh, g = _mixer1_bwd(dh, s_mix1, W, tm)
    collect(g, 1)
    dh, g = _ffn_bwd(dh, s_ffn0, W, 0, tm, bufs)
    collect(g, 0)
    dh, g = _xattn_bwd(dh, s_xa0, mem, W, 0, tm, bufs)
    collect(g, 0)
    dx, g = _mixer0_bwd(dh, s_mix0, W, cos_p, sin_p, tm)
    collect(g, 0)

    GB, GS = {}, {}
    for name in WEIGHTS:
        if name in bufs:
            b_ = bufs[name]
            GB[name] = b_ if b_.ndim == 4 else b_.reshape(2, N_CHIPS, b_.shape[1] // N_CHIPS, b_.shape[2])
            continue
        by_layer = [layered[name][lyr] for lyr in sorted(layered[name])]
        if name in BIG:
            GB[name] = _halves_first(by_layer[0]) if len(by_layer) == 1 else jnp.stack(by_layer)
        else:
            full_shape = W[name].shape
            GS[name] = (by_layer[0].reshape(full_shape) if len(by_layer) == 1
                        else jnp.stack([g_.reshape(full_shape[1:]) for g_ in by_layer]))

    outs = _reduce_and_update(GB, GS, P)
    return (loss, dx[None], *[d[n_] for d in outs for n_ in WEIGHTS])
```

```python
import functools
import math

import jax
import jax.numpy as jnp
import numpy as np
from jax import lax
from jax.experimental import pallas as pl
from jax.experimental.pallas import tpu as pltpu
from jax.experimental.pallas import tpu_sc as plsc

F32 = jnp.float32
BF16 = jnp.bfloat16
MXU_DTYPE = BF16
HI = lax.Precision.HIGHEST
V7X_VMEM_LIMIT_BYTES = 56 * 1024 * 1024
EPS = 1e-6
MESH = pl.DeviceIdType.MESH

HG_HEADS, HG_DIM = 4, 128
HG_WIDTH = HG_HEADS * HG_DIM
HG_SUB = 32
HG_BLOCK = 64
MLA_HEADS, MLA_Q_RANK, MLA_KV_RANK = 4, 256, 128
MLA_NOPE, MLA_ROPE, MLA_V = 128, 64, 128
MLA_QK = MLA_NOPE + MLA_ROPE
MLA_DK = 256
ROPE_BASE = 10000.0
IN_WIDTH = 4 * HG_WIDTH + MLA_Q_RANK + MLA_KV_RANK + MLA_ROPE
IN_PAD = 4 * HG_WIDTH + MLA_Q_RANK + MLA_KV_RANK + 128
S5_GROUP, S5_STATE = 16, 64
S5_GB = 8
DT_MIN, DT_MAX = 1e-3, 1e-1
XA_HEADS = 4
CONV_W = 3
ADAM_LR, ADAM_B1, ADAM_B2, ADAM_EPS, ADAM_WD, ADAM_STEP = 0.001, 0.9, 0.999, 1e-08, 0.01, 10


def _cparams(sem):
    return pltpu.CompilerParams(dimension_semantics=sem, vmem_limit_bytes=V7X_VMEM_LIMIT_BYTES)


class Opd:
    def __init__(self, arr, block, imap, grad=None, gshape=None, gimap=None):
        self.arr, self.block, self.imap, self.grad = arr, block, imap, grad
        self.gshape = arr.shape if gshape is None else gshape
        self.gimap = imap if gimap is None else gimap

    def spec(self):
        return pl.BlockSpec(self.block, self.imap)

    def gspec(self):
        return pl.BlockSpec(self.block, self.gimap)


def rows(arr, tm, grad=None, col=0, width=None):
    width = arr.shape[1] if width is None else width
    return Opd(arr, (tm, width), lambda i, c=col: (i, c), grad, (arr.shape[0], width), lambda i: (i, 0))


def cols(arr, tn, grad=None):
    return Opd(arr, (arr.shape[0], tn), lambda j: (0, j), grad)


def full(arr, grad=None):
    return Opd(arr, arr.shape, lambda i: (0, 0), grad)


def _load(ref):
    v = ref[...]
    return v.astype(F32) if jnp.issubdtype(v.dtype, jnp.floating) else v


def blocked_fwd(f, opds, outs, n, name):
    n_in = len(opds)

    def body(*refs):
        ys = f(*[_load(r) for r in refs[:n_in]])
        for r, y in zip(refs[n_in:], ys):
            r[...] = y.astype(r.dtype)

    res = pl.pallas_call(
        body, name=name, grid=(n,),
        in_specs=[o.spec() for o in opds],
        out_specs=[pl.BlockSpec(b, m) for (_, _, b, m) in outs],
        out_shape=[jax.ShapeDtypeStruct(s, d) for (s, d, _, _) in outs],
        compiler_params=_cparams(("parallel",)),
    )(*[o.arr for o in opds])
    return res


def blocked_bwd(f, opds, dys, n, name, plus=None):
    n_in, n_dy = len(opds), len(dys)
    diff = [i for i, o in enumerate(opds) if o.grad]
    extra = [] if plus is None else [plus]

    def body(*refs):
        vals = [_load(r) for r in refs[:n_in]]

        def fd(*dv):
            allv = list(vals)
            for i, v in zip(diff, dv):
                allv[i] = v
            return tuple(f(*allv))

        ys, vjp = jax.vjp(fd, *[vals[i] for i in diff])
        cts = tuple(_load(r).astype(y.dtype) for r, y in zip(refs[n_in:n_in + n_dy], ys))
        gs = list(vjp(cts))
        if extra:
            gs[0] = gs[0] + _load(refs[n_in + n_dy])
        for r, g, i in zip(refs[n_in + n_dy + len(extra):], gs, diff):
            if opds[i].grad == 'acc':
                @pl.when(pl.program_id(0) == 0)
                def _(r=r):
                    r[...] = jnp.zeros(r.shape, r.dtype)
                r[...] += g.astype(r.dtype)
            else:
                r[...] = g.astype(r.dtype)

    any_acc = any(opds[i].grad == 'acc' for i in diff)
    res = pl.pallas_call(
        body, name=name, grid=(n,),
        in_specs=[o.spec() for o in opds + dys + extra],
        out_specs=[opds[i].gspec() for i in diff],
        out_shape=[jax.ShapeDtypeStruct(opds[i].gshape, F32) for i in diff],
        compiler_params=_cparams(("arbitrary" if any_acc else "parallel",)),
    )(*[o.arr for o in opds + dys + extra])
    return res


def _tile(dim, want):
    for t in range(want - want % 16, 0, -16):
        if dim % t == 0:
            return t
    assert dim <= want, (dim, want)
    return dim


MATMUL_VMEM_BUDGET = 40 * 1024 * 1024
MATMUL_ROWS = 512


def _widest(N, fits):
    for t in range(N - N % 128, 0, -128):
        if N % t == 0 and fits(t):
            return t
    return N


def matmul(a, b, mode="nn", out_dtype=F32, add=None, name="matmul", into=None, col_blocks=None):
    sa, sb, so = a.dtype.itemsize, b.dtype.itemsize, jnp.dtype(out_dtype).itemsize
    has_add = add is not None
    if mode == "tn":
        (K, M), (K2, N) = a.shape, b.shape
        assert K == K2 and not has_add and out_dtype == F32, (a.shape, b.shape)
        tk = _tile(K, MATMUL_ROWS)
        tn = _widest(N, lambda t: 2 * (tk * M * sa + tk * t * sb + M * t * 4) <= MATMUL_VMEM_BUDGET)
        extra, alias = [], {}
        if into is not None:
            buf, lead = into[0], tuple(into[1:])
            if col_blocks is not None:
                assert N % col_blocks == 0 and (N // col_blocks) % 128 == 0 and tn >= N // col_blocks, (N, col_blocks, tn)
                tn = N // col_blocks
                assert buf.shape[len(lead):] == (M, tn), (buf.shape, lead, M, tn)
                out_spec = pl.BlockSpec((None,) * len(lead) + (M, tn), lambda j, k: lead[:-1] + (lead[-1] + j, 0, 0))
            else:
                assert buf.shape[len(lead):] == (M, N), (buf.shape, lead, M, N)
                out_spec = pl.BlockSpec((None,) * len(lead) + (M, tn), lambda j, k: lead + (0, j))
            out_shape = jax.ShapeDtypeStruct(buf.shape, F32)
            extra, alias = [buf], {2: 0}
        else:
            assert col_blocks is None
            out_spec = pl.BlockSpec((M, tn), lambda j, k: (0, j))
            out_shape = jax.ShapeDtypeStruct((M, N), F32)

        def body(a_ref, b_ref, *rest):
            o_ref = rest[-1]
            r = lax.dot_general(a_ref[...].astype(MXU_DTYPE), b_ref[...].astype(MXU_DTYPE), ((_TN), ((), ())),
                                preferred_element_type=F32)

            @pl.when(pl.program_id(1) == 0)
            def _():
                o_ref[...] = r

            @pl.when(pl.program_id(1) > 0)
            def _():
                o_ref[...] += r

        return pl.pallas_call(
            body, name=name, grid=(N // tn, K // tk),
            in_specs=[pl.BlockSpec((tk, M), lambda j, k: (k, 0)), pl.BlockSpec((tk, tn), lambda j, k: (k, j))]
            + [pl.BlockSpec(memory_space=pl.ANY)] * len(extra),
            out_specs=out_spec, out_shape=out_shape, input_output_aliases=alias,
            compiler_params=_cparams(("parallel", "arbitrary")),
        )(a, b, *extra)

    (M, K) = a.shape
    N = b.shape[1] if mode == "nn" else b.shape[0]
    assert K == (b.shape[0] if mode == "nn" else b.shape[1]), (a.shape, b.shape, mode)
    tm = _tile(M, MATMUL_ROWS)
    tn = _widest(N, lambda t: 2 * (tm * K * sa + K * t * sb + tm * t * (so + 4 * has_add)) <= MATMUL_VMEM_BUDGET)
    dims = ((_NN if mode == "nn" else _NT), ((), ()))

    def body(*refs):
        r = lax.dot_general(refs[0][...].astype(MXU_DTYPE), refs[1][...].astype(MXU_DTYPE), dims, preferred_element_type=F32)
        if has_add:
            r = r + refs[2][...].astype(F32)
        refs[-1][...] = r.astype(refs[-1].dtype)

    b_spec = pl.BlockSpec((K, tn), lambda j, i: (0, j)) if mode == "nn" else pl.BlockSpec((tn, K), lambda j, i: (j, 0))
    in_specs = [pl.BlockSpec((tm, K), lambda j, i: (i, 0)), b_spec]
    args = [a, b]
    if has_add:
        in_specs.append(pl.BlockSpec((tm, tn), lambda j, i: (i, j)))
        args.append(add)
    return pl.pallas_call(
        body, name=name, grid=(N // tn, M // tm),
        in_specs=in_specs,
        out_specs=pl.BlockSpec((tm, tn), lambda j, i: (i, j)),
        out_shape=jax.ShapeDtypeStruct((M, N), out_dtype),
        compiler_params=_cparams(("parallel", "parallel")),
    )(*args)


def _dot(a, b, dims, precision=None):
    if precision is None:
        a, b = a.astype(MXU_DTYPE), b.astype(MXU_DTYPE)
    return lax.dot_general(a, b, (dims, ((), ())), precision=precision, preferred_element_type=F32)


_NN = ((1,), (0,))
_NT = ((1,), (1,))
_TN = ((0,), (0,))


def _rms(x, gain):
    return x * lax.rsqrt(jnp.mean(x * x, axis=-1, keepdims=True) + EPS) * gain


def _hg_block(st_t, q, fl, iv, g, lb, gain):
    row = lax.broadcasted_iota(jnp.int32, (HG_SUB, HG_SUB), 0)
    col = lax.broadcasted_iota(jnp.int32, (HG_SUB, HG_SUB), 1)
    tri = (row >= col).astype(F32)
    outs, states = [], []
    for h in range(HG_HEADS):
        sl = slice(h * HG_DIM, (h + 1) * HG_DIM)
        st = st_t[h * HG_DIM:(h + 1) * HG_DIM, :]
        lbh = lb[:, sl]
        fg = lbh + (1.0 - lbh) * jax.nn.sigmoid(fl[:, sl])
        lf, kk, qf, v = jnp.log(fg), 1.0 - fg, jax.nn.silu(q[:, sl]), iv[:, sl]
        parts = []
        for s in range(q.shape[0] // HG_SUB):
            r = slice(s * HG_SUB, (s + 1) * HG_SUB)
            b = _dot(tri, lf[r], _NN, HI)
            b_mid = jnp.sum(lf[r][:HG_SUB // 2], axis=0, keepdims=True)
            b_end = jnp.sum(lf[r], axis=0, keepdims=True)
            sc = _dot(qf[r] * jnp.exp(b - b_mid), kk[r] * jnp.exp(b_mid - b), _NT) * tri
            parts.append(_dot(sc, v[r], _NN) + _dot(qf[r] * jnp.exp(b), st, _NT))
            st = st * jnp.exp(b_end) + _dot(v[r], kk[r] * jnp.exp(b_end - b), _TN)
        o = jnp.concatenate(parts, axis=0)
        outs.append(_rms(o, gain[:, sl]) * jax.nn.silu(g[:, sl]))
        states.append(st)
    return jnp.concatenate(states, axis=0), jnp.concatenate(outs, axis=1)


def _hg_specs(proj, nb):
    return [pl.BlockSpec((HG_BLOCK, HG_WIDTH), lambda i, c=c, f=nb: (f(i), c)) for c in range(4)]


def hgrn2_fwd(proj, lb, gain):
    L = proj.shape[0]
    n = L // HG_BLOCK

    def body(q, fl, iv, g, lb_r, gain_r, o_ref, st_ref, st):
        @pl.when(pl.program_id(0) == 0)
        def _():
            st[...] = jnp.zeros(st.shape, F32)

        st_ref[0] = st[...]
        new, o = _hg_block(st[...], q[...], fl[...], iv[...], g[...], lb_r[...], gain_r[...])
        st[...] = new
        o_ref[...] = o.astype(o_ref.dtype)

    pspec = pl.BlockSpec((1, HG_WIDTH), lambda i: (0, 0))
    return pl.pallas_call(
        body, name="hgrn2_fwd", grid=(n,),
        in_specs=_hg_specs(proj, lambda i: i) + [pspec, pspec],
        out_specs=[pl.BlockSpec((HG_BLOCK, HG_WIDTH), lambda i: (i, 0)),
                   pl.BlockSpec((1, HG_WIDTH, HG_DIM), lambda i: (i, 0, 0))],
        out_shape=[jax.ShapeDtypeStruct((L, HG_WIDTH), MXU_DTYPE),
                   jax.ShapeDtypeStruct((n, HG_WIDTH, HG_DIM), F32)],
        scratch_shapes=[pltpu.VMEM((HG_WIDTH, HG_DIM), F32)],
        compiler_params=_cparams(("arbitrary",)),
    )(proj, proj, proj, proj, lb, gain)


def hgrn2_bwd(proj, lb, gain, states, do):
    L = proj.shape[0]
    n = L // HG_BLOCK

    def body(q, fl, iv, g, lb_r, gain_r, st_r, do_r, dproj, dlb, dgain, dst):
        @pl.when(pl.program_id(0) == 0)
        def _():
            dst[...] = jnp.zeros(dst.shape, F32)
            dlb[...] = jnp.zeros(dlb.shape, F32)
            dgain[...] = jnp.zeros(dgain.shape, F32)

        _, vjp = jax.vjp(_hg_block, st_r[0], q[...], fl[...], iv[...], g[...], lb_r[...], gain_r[...])
        d_st, dq, dfl, div, dg, d_lb, d_gain = vjp((dst[...], do_r[...].astype(F32)))
        dst[...] = d_st
        dproj[:, 0 * HG_WIDTH:1 * HG_WIDTH] = dq
        dproj[:, 1 * HG_WIDTH:2 * HG_WIDTH] = dfl
        dproj[:, 2 * HG_WIDTH:3 * HG_WIDTH] = div
        dproj[:, 3 * HG_WIDTH:4 * HG_WIDTH] = dg
        dlb[...] += d_lb
        dgain[...] += d_gain

    rev = lambda i: n - 1 - i
    pspec = pl.BlockSpec((1, HG_WIDTH), lambda i: (0, 0))
    return pl.pallas_call(
        body, name="hgrn2_bwd", grid=(n,),
        in_specs=_hg_specs(proj, rev) + [pspec, pspec,
                                         pl.BlockSpec((1, HG_WIDTH, HG_DIM), lambda i: (rev(i), 0, 0)),
                                         pl.BlockSpec((HG_BLOCK, HG_WIDTH), lambda i: (rev(i), 0))],
        out_specs=[pl.BlockSpec((HG_BLOCK, 4 * HG_WIDTH), lambda i: (rev(i), 0)), pspec, pspec],
        out_shape=[jax.ShapeDtypeStruct((L, 4 * HG_WIDTH), F32),
                   jax.ShapeDtypeStruct((1, HG_WIDTH), F32), jax.ShapeDtypeStruct((1, HG_WIDTH), F32)],
        scratch_shapes=[pltpu.VMEM((HG_WIDTH, HG_DIM), F32)],
        compiler_params=_cparams(("arbitrary",)),
    )(proj, proj, proj, proj, lb, gain, states, do)


def _rope_rms(x, gain_p, cos_p, sin_p):
    n = x * lax.rsqrt(jnp.sum(x * x, axis=-1, keepdims=True) * (1.0 / MLA_ROPE) + EPS) * gain_p
    r = lax.broadcasted_iota(jnp.int32, (128, 128), 0)
    c = lax.broadcasted_iota(jnp.int32, (128, 128), 1)
    swap = (r == (c + 64) % 128).astype(F32)
    return n * cos_p + _dot(n, swap, _NN, HI) * sin_p


MLA_IN = MLA_Q_RANK + MLA_KV_RANK + 128


def _mla_prep(x, cos_p, sin_p, q_a, w_uq, kv_a, w_ukv, qn_nope, qn_rope, kn_nope, kn_rope):
    c_q, c_kv, kpe = x[:, :MLA_Q_RANK], x[:, MLA_Q_RANK:MLA_Q_RANK + MLA_KV_RANK], x[:, MLA_Q_RANK + MLA_KV_RANK:]
    q = _dot(_rms(c_q, q_a), w_uq, _NN)
    kv = _dot(_rms(c_kv, kv_a), w_ukv, _NN)
    k_pe = _rope_rms(kpe, kn_rope, cos_p, sin_p)
    qs, ks = [], []
    for h in range(MLA_HEADS):
        qs.append(_rms(q[:, h * MLA_DK:h * MLA_DK + MLA_NOPE], qn_nope))
        qs.append(_rope_rms(q[:, h * MLA_DK + MLA_NOPE:(h + 1) * MLA_DK], qn_rope, cos_p, sin_p))
        ks.append(_rms(kv[:, h * MLA_NOPE:(h + 1) * MLA_NOPE], kn_nope))
        ks.append(k_pe)
    return jnp.concatenate(qs, axis=1), jnp.concatenate(ks, axis=1), kv[:, MLA_HEADS * MLA_NOPE:]


def _mla_prep_opds(proj, cos_p, sin_p, params, tm, grads):
    g = (lambda k: k) if grads else (lambda k: None)
    assert (4 * HG_WIDTH) % MLA_IN == 0
    return ([rows(proj, tm, g('blk'), col=4 * HG_WIDTH // MLA_IN, width=MLA_IN), rows(cos_p, tm), rows(sin_p, tm)]
            + [full(p, g('acc')) for p in params])


def mla_prep_fwd(proj, cos_p, sin_p, params, tm):
    L = proj.shape[0]
    W = MLA_HEADS * MLA_DK
    rb = lambda w: (tm, w)
    outs = [((L, W), MXU_DTYPE, rb(W), lambda i: (i, 0)), ((L, W), MXU_DTYPE, rb(W), lambda i: (i, 0)),
            ((L, MLA_HEADS * MLA_V), MXU_DTYPE, rb(MLA_HEADS * MLA_V), lambda i: (i, 0))]
    return blocked_fwd(_mla_prep, _mla_prep_opds(proj, cos_p, sin_p, params, tm, False), outs, L // tm, "mla_prep_fwd")


def mla_prep_bwd(proj, cos_p, sin_p, params, dq, dk, dv, tm):
    L = proj.shape[0]
    return blocked_bwd(_mla_prep, _mla_prep_opds(proj, cos_p, sin_p, params, tm, True),
                       [rows(dq, tm), rows(dk, tm), rows(dv, tm)], L // tm, "mla_prep_bwd")


def _scores(q, k, scale, shift=None):
    s = _dot(q, k, _NT) * scale
    if shift is None:
        return s
    row = lax.broadcasted_iota(jnp.int32, s.shape, 0)
    col = lax.broadcasted_iota(jnp.int32, s.shape, 1)
    return jnp.where(col <= row + shift, s, -jnp.inf)


ATTN_ROWS = 512
ATTN_WIDE = 2


def attn_fwd(q, k, v, scale, t):
    L = q.shape[0]
    tq = ATTN_WIDE * t

    def body(q_ref, k_ref, v_ref, o_ref, lse_ref):
        i = pl.program_id(1)
        qb = q_ref[...]

        def step(j, carry, shift=None):
            m, l, acc = carry
            kj = k_ref[pl.ds(pl.multiple_of(j * t, t), t), :]
            vj = v_ref[pl.ds(pl.multiple_of(j * t, t), t), :]
            s = _scores(qb, kj, scale, shift)
            m_new = jnp.maximum(m, jnp.max(s, axis=-1, keepdims=True))
            p = jnp.exp(s - m_new)
            alpha = jnp.exp(m - m_new)
            return m_new, alpha * l + jnp.sum(p, axis=-1, keepdims=True), alpha * acc + _dot(p, vj, _NN)

        carry = (jnp.full((tq, 1), -jnp.inf, F32), jnp.zeros((tq, 1), F32), jnp.zeros((tq, MLA_V), F32))
        carry = lax.fori_loop(0, ATTN_WIDE * i, step, carry)
        for d in range(ATTN_WIDE):
            carry = step(ATTN_WIDE * i + d, carry, -d * t)
        m, l, acc = carry
        o_ref[...] = acc / l
        lse_ref[...] = jnp.broadcast_to(m + jnp.log(l), lse_ref.shape)

    hspec = lambda rows_, w: pl.BlockSpec((rows_, w), lambda h, i: (0, h))
    bspec = lambda w: pl.BlockSpec((tq, w), lambda h, i: (i, h))
    return pl.pallas_call(
        body, name="attn_fwd", grid=(MLA_HEADS, L // tq),
        in_specs=[bspec(MLA_DK), hspec(L, MLA_DK), hspec(L, MLA_V)],
        out_specs=[bspec(MLA_V), bspec(MLA_V)],
        out_shape=[jax.ShapeDtypeStruct((L, MLA_HEADS * MLA_V), F32)] * 2,
        compiler_params=_cparams(("parallel", "parallel")),
    )(q, k, v)


def attn_bwd_dq(q, k, v, o, lse, do, scale, t):
    L = q.shape[0]
    tq = ATTN_WIDE * t

    def body(q_ref, k_ref, v_ref, o_ref, lse_ref, do_ref, dq_ref):
        i = pl.program_id(1)
        qb, dob = q_ref[...], do_ref[...]
        delta = jnp.sum(dob * o_ref[...], axis=-1, keepdims=True)
        lse_c = jnp.max(lse_ref[...], axis=-1, keepdims=True)

        def step(j, dq, shift=None):
            kj = k_ref[pl.ds(pl.multiple_of(j * t, t), t), :]
            vj = v_ref[pl.ds(pl.multiple_of(j * t, t), t), :]
            p = jnp.exp(_scores(qb, kj, scale, shift) - lse_c)
            ds = p * (_dot(dob, vj, _NT) - delta) * scale
            return dq + _dot(ds, kj, _NN)

        dq = lax.fori_loop(0, ATTN_WIDE * i, step, jnp.zeros((tq, MLA_DK), F32))
        for d in range(ATTN_WIDE):
            dq = step(ATTN_WIDE * i + d, dq, -d * t)
        dq_ref[...] = dq

    hspec = lambda w: pl.BlockSpec((L, w), lambda h, i: (0, h))
    bspec = lambda w: pl.BlockSpec((tq, w), lambda h, i: (i, h))
    return pl.pallas_call(
        body, name="attn_bwd_dq", grid=(MLA_HEADS, L // tq),
        in_specs=[bspec(MLA_DK), hspec(MLA_DK), hspec(MLA_V), bspec(MLA_V), bspec(MLA_V), bspec(MLA_V)],
        out_specs=bspec(MLA_DK),
        out_shape=jax.ShapeDtypeStruct((L, MLA_HEADS * MLA_DK), F32),
        compiler_params=_cparams(("parallel", "parallel")),
    )(q, k, v, o, lse, do)


def attn_bwd_dkv(q, k, v, o, lse, do, scale, t):
    L = q.shape[0]
    tk = ATTN_WIDE * t

    def body(q_ref, k_ref, v_ref, o_ref, lse_ref, do_ref, dk_ref, dv_ref):
        j = pl.program_id(1)
        kb, vb = k_ref[...], v_ref[...]

        def step(i, carry, shift=None):
            dk, dv = carry
            r = pl.ds(pl.multiple_of(i * t, t), t)
            qi, doi = q_ref[r, :], do_ref[r, :]
            delta = jnp.sum(doi * o_ref[r, :], axis=-1, keepdims=True)
            lse_c = jnp.max(lse_ref[r, :], axis=-1, keepdims=True)
            p = jnp.exp(_scores(qi, kb, scale, shift) - lse_c)
            ds = p * (_dot(doi, vb, _NT) - delta) * scale
            return dk + _dot(ds, qi, _TN), dv + _dot(p, doi, _TN)

        carry = (jnp.zeros((tk, MLA_DK), F32), jnp.zeros((tk, MLA_V), F32))
        for d in range(ATTN_WIDE):
            carry = step(ATTN_WIDE * j + d, carry, d * t)
        dk, dv = lax.fori_loop(ATTN_WIDE * (j + 1), L // t, step, carry)
        dk_ref[...] = dk
        dv_ref[...] = dv

    hspec = lambda w: pl.BlockSpec((L, w), lambda h, j: (0, h))
    bspec = lambda w: pl.BlockSpec((tk, w), lambda h, j: (j, h))
    return pl.pallas_call(
        body, name="attn_bwd_dkv", grid=(MLA_HEADS, L // tk),
        in_specs=[hspec(MLA_DK), bspec(MLA_DK), bspec(MLA_V), hspec(MLA_V), hspec(MLA_V), hspec(MLA_V)],
        out_specs=[bspec(MLA_DK), bspec(MLA_V)],
        out_shape=[jax.ShapeDtypeStruct((L, MLA_HEADS * MLA_DK), F32), jax.ShapeDtypeStruct((L, MLA_HEADS * MLA_V), F32)],
        compiler_params=_cparams(("parallel", "parallel")),
    )(q, k, v, o, lse, do)


S5_LANES = S5_GB * S5_STATE


def _cmul(ar, ai, br, bi):
    return ar * br - ai * bi, ar * bi + ai * br


def _a_powers(ar, ai, reverse):
    a2 = _cmul(ar, ai, ar, ai)
    a4 = _cmul(*a2, *a2)
    row = lax.broadcasted_iota(jnp.int32, (8, ar.shape[1]), 0)
    e = (8 - row) if reverse else (row + 1)
    tr, ti = jnp.ones((8, ar.shape[1]), F32), jnp.zeros((8, ar.shape[1]), F32)
    for bit, (pr, pi) in ((1, (ar, ai)), (2, a2), (4, a4), (8, _cmul(*a4, *a4))):
        nr, ni = _cmul(tr, ti, pr, pi)
        sel = (e & bit) != 0
        tr, ti = jnp.where(sel, nr, tr), jnp.where(sel, ni, ti)
    pows = []
    for d, (pr, pi) in zip((1, 2, 4), ((ar, ai), a2, a4)):
        keep = (row < 8 - d) if reverse else (row >= d)
        pows.append((jnp.where(keep, pr, 0.0), jnp.where(keep, pi, 0.0)))
    return pows, (tr, ti)


def _scan8(xr, xi, pows, table, cr, ci, reverse):
    for d, (pr, pi) in zip((1, 2, 4), pows):
        shift = 8 - d if reverse else d
        mr, mi = _cmul(pr, pi, pltpu.roll(xr, shift, 0), pltpu.roll(xi, shift, 0))
        xr, xi = xr + mr, xi + mi
    mr, mi = _cmul(table[0], table[1], cr, ci)
    return xr + mr, xi + mi


def _row_of(x, r):
    row = lax.broadcasted_iota(jnp.int32, x.shape, 0)
    return jnp.sum(jnp.where(row == r, x, 0.0), axis=0, keepdims=True)


def _s5_scan_fwd(h_re, h_im, ar, ai, L):
    pows, table = _a_powers(ar, ai, False)

    def step(i, carry):
        r = pl.ds(pl.multiple_of(i * 8, 8), 8)
        xr, xi = _scan8(h_re[r, :], h_im[r, :], pows, table, carry[0], carry[1], False)
        h_re[r, :] = xr
        h_im[r, :] = xi
        return xr[7:8, :], xi[7:8, :]

    z = jnp.zeros((1, ar.shape[1]), F32)
    lax.fori_loop(0, L // 8, step, (z, z))


def _s5_specs(L):
    return [pl.BlockSpec((L, 128), lambda g: (0, g)),
            pl.BlockSpec((1, 128, S5_LANES), lambda g: (g, 0, 0)), pl.BlockSpec((1, 128, S5_LANES), lambda g: (g, 0, 0)),
            pl.BlockSpec((1, 1, S5_LANES), lambda g: (g, 0, 0)), pl.BlockSpec((1, 1, S5_LANES), lambda g: (g, 0, 0)),
            pl.BlockSpec((1, S5_LANES, 128), lambda g: (g, 0, 0)), pl.BlockSpec((1, S5_LANES, 128), lambda g: (g, 0, 0))]


def s5_fwd(u, w_re, w_im, a_re, a_im, c_re, c_im):
    L, D = u.shape

    def body(u_ref, wr, wi, ar, ai, cr, ci, y_ref, h_re, h_im):
        ub = u_ref[...]
        h_re[...] = _dot(ub, wr[0], _NN)
        h_im[...] = _dot(ub, wi[0], _NN)
        _s5_scan_fwd(h_re, h_im, ar[0], ai[0], L)
        y_ref[...] = _dot(h_re[...], cr[0], _NN) - _dot(h_im[...], ci[0], _NN)

    return pl.pallas_call(
        body, name="s5_fwd", grid=(D // 128,),
        in_specs=_s5_specs(L), out_specs=pl.BlockSpec((L, 128), lambda g: (0, g)),
        out_shape=jax.ShapeDtypeStruct((L, D), F32),
        scratch_shapes=[pltpu.VMEM((L, S5_LANES), F32), pltpu.VMEM((L, S5_LANES), F32)],
        compiler_params=_cparams(("parallel",)),
    )(u, w_re, w_im, a_re, a_im, c_re, c_im)


def s5_bwd(u, w_re, w_im, a_re, a_im, c_re, c_im, dy, tc):
    L, D = u.shape
    nch = L // tc

    def body(u_ref, wr, wi, ar_ref, ai_ref, cr, ci, dy_ref, du_ref, dwr, dwi, dar, dai, dcr, dci, h_re, h_im, g_re, g_im):
        ar, ai = ar_ref[0], ai_ref[0]
        ub = u_ref[...]
        h_re[...] = _dot(ub, wr[0], _NN)
        h_im[...] = _dot(ub, wi[0], _NN)
        _s5_scan_fwd(h_re, h_im, ar, ai, L)
        dyb = dy_ref[...]
        dcr[0] = _dot(h_re[...], dyb, _TN)
        dci[0] = -_dot(h_im[...], dyb, _TN)
        pows, table = _a_powers(ar, -ai, True)
        dwr[0] = jnp.zeros((128, S5_LANES), F32)
        dwi[0] = jnp.zeros((128, S5_LANES), F32)
        z1 = jnp.zeros((1, S5_LANES), F32)
        z8 = jnp.zeros((8, S5_LANES), F32)

        def chunk(cc, carry):
            c0 = pl.multiple_of((nch - 1 - cc) * tc, tc)
            rows_c = pl.ds(c0, tc)
            dyc = dy_ref[rows_c, :]
            g_re[...] = _dot(dyc, cr[0], _NT)
            g_im[...] = -_dot(dyc, ci[0], _NT)

            def step(ii, cy):
                gr_c, gi_c, acc_r, acc_i = cy
                i8 = pl.multiple_of((tc // 8 - 1 - ii) * 8, 8)
                rl = pl.ds(i8, 8)
                xr, xi = _scan8(g_re[rl, :], g_im[rl, :], pows, table, gr_c, gi_c, True)
                g_re[rl, :] = xr
                g_im[rl, :] = xi
                t0 = c0 + i8
                hb_r, hb_i = h_re[pl.ds(t0, 8), :], h_im[pl.ds(t0, 8), :]
                tp = pl.multiple_of(jnp.maximum(t0 - 8, 0), 8)
                first = (t0 > 0).astype(F32)
                pr = h_re[pl.ds(tp, 8), :][7:8, :] * first
                pi = h_im[pl.ds(tp, 8), :][7:8, :] * first
                row = lax.broadcasted_iota(jnp.int32, xr.shape, 0)
                hp_r = jnp.where(row == 0, pr, pltpu.roll(hb_r, 1, 0))
                hp_i = jnp.where(row == 0, pi, pltpu.roll(hb_i, 1, 0))
                return (xr[0:1, :], xi[0:1, :],
                        acc_r + xr * hp_r + xi * hp_i, acc_i + xi * hp_r - xr * hp_i)

            cy = lax.fori_loop(0, tc // 8, step, carry)
            uc = u_ref[rows_c, :]
            gr, gi = g_re[...], g_im[...]
            du_ref[rows_c, :] = _dot(gr, wr[0], _NT) + _dot(gi, wi[0], _NT)
            dwr[0] += _dot(uc, gr, _TN)
            dwi[0] += _dot(uc, gi, _TN)
            return cy

        _, _, acc_r, acc_i = lax.fori_loop(0, nch, chunk, (z1, z1, z8, z8))
        dar[0] = jnp.sum(acc_r, axis=0, keepdims=True)
        dai[0] = jnp.sum(acc_i, axis=0, keepdims=True)

    specs = _s5_specs(L)
    return pl.pallas_call(
        body, name="s5_bwd", grid=(D // 128,),
        in_specs=specs + [pl.BlockSpec((L, 128), lambda g: (0, g))],
        out_specs=[pl.BlockSpec((L, 128), lambda g: (0, g))] + specs[1:],
        out_shape=[jax.ShapeDtypeStruct((L, D), F32)] + [jax.ShapeDtypeStruct(x.shape, F32)
                                                        for x in (w_re, w_im, a_re, a_im, c_re, c_im)],
        scratch_shapes=[pltpu.VMEM((L, S5_LANES), F32), pltpu.VMEM((L, S5_LANES), F32),
                        pltpu.VMEM((tc, S5_LANES), F32), pltpu.VMEM((tc, S5_LANES), F32)],
        compiler_params=_cparams(("parallel",)),
    )(u, w_re, w_im, a_re, a_im, c_re, c_im, dy)


def _s5_discretize(lr, li, ldt, br, bi):
    dt = jnp.exp(ldt)
    mag = jnp.exp(lr * dt)
    ar, ai = mag * jnp.cos(li * dt), mag * jnp.sin(li * dt)
    den = lr * lr + li * li
    zr = ((ar - 1.0) * lr + ai * li) / den
    zi = (ai * lr - (ar - 1.0) * li) / den
    p = lax.broadcasted_iota(jnp.int32, (S5_STATE, S5_STATE * S5_GROUP), 0)
    c = lax.broadcasted_iota(jnp.int32, (S5_STATE, S5_STATE * S5_GROUP), 1)
    rep = (c // S5_GROUP == p).astype(F32)
    zr, zi = _dot(zr, rep, _NN, HI), _dot(zi, rep, _NN, HI)
    return ar, ai, zr * br - zi * bi, zr * bi + zi * br


def _conv_shift(x, d):
    row = lax.broadcasted_iota(jnp.int32, x.shape, 0)
    return jnp.where(row >= d, pltpu.roll(x, d, 0), 0.0)


def _conv_unshift(x, d):
    n = x.shape[0]
    row = lax.broadcasted_iota(jnp.int32, x.shape, 0)
    return jnp.where(row < n - d, pltpu.roll(x, n - d, 0), 0.0)


@functools.partial(jax.custom_vjp, nondiff_argnums=(1,))
def _shift_rows(x, d):
    return _conv_shift(x, d)


_shift_rows.defvjp(lambda x, d: (_conv_shift(x, d), None), lambda d, _, g: (_conv_unshift(g, d),))


def _conv_gate(ug, uv, wg0, wg1, wg2, wv0, wv1, wv2, bg, bv):
    def conv(u, w0, w1, w2, b):
        return u * w2 + _shift_rows(u, 1) * w1 + _shift_rows(u, 2) * w0 + b
    return (jax.nn.silu(conv(ug, wg0, wg1, wg2, bg)) * conv(uv, wv0, wv1, wv2, bv),)


def _rms_fn(x, gain):
    return (_rms(x, gain),)


def _softmax_rows(s):
    e = jnp.exp(s - lax.stop_gradient(jnp.max(s, axis=-1, keepdims=True)))
    return e / jnp.sum(e, axis=-1, keepdims=True)


def _xa_core(qp, k, v, q_gain):
    dh = qp.shape[1] // XA_HEADS
    outs = []
    for h in range(XA_HEADS):
        sl = slice(h * dh, (h + 1) * dh)
        p = _softmax_rows(_dot(_rms(qp[:, sl], q_gain), k[:, sl], _NT) * (dh ** -0.5))
        outs.append(_dot(p, v[:, sl], _NN))
    return (jnp.concatenate(outs, axis=1),)


def _mem_kv(mem, mem_gain, wk, wv, k_gain):
    m = _rms(mem, mem_gain)
    kp = _dot(m, wk, _NN)
    dh = kp.shape[1] // XA_HEADS
    k = jnp.concatenate([_rms(kp[:, h * dh:(h + 1) * dh], k_gain) for h in range(XA_HEADS)], axis=1)
    return k, _dot(m, wv, _NN)


def _s5_post(y, u, d):
    return (jax.nn.gelu(y + d * u),)


def _glu(a, b):
    return (a * jax.nn.sigmoid(b),)


def _lb_first(logits):
    e = jnp.exp(logits - lax.stop_gradient(jnp.max(logits, axis=0, keepdims=True)))
    return (_row_of(e, 0) / jnp.sum(e, axis=0, keepdims=True),)


def _loss_fn(y, t):
    e = y - t
    part = 0.5 * jnp.sum(e * e) / y.shape[1]
    return e * (1.0 / y.shape[1]), jnp.full((8, 128), part / (8 * 128), F32)


def _out(shape, dtype, tm):
    return (shape, dtype, (tm, shape[1]), lambda i: (i, 0))


def rms_fwd(h, gain, tm, dtype):
    return blocked_fwd(_rms_fn, [rows(h, tm), full(gain)], [_out(h.shape, dtype, tm)], h.shape[0] // tm, "rms_fwd")[0]


def rms_bwd(h, gain, dy, tm, residual):
    return blocked_bwd(_rms_fn, [rows(h, tm, 'blk'), full(gain, 'acc')], [rows(dy, tm)], h.shape[0] // tm, "rms_bwd",
                       plus=rows(residual, tm))


def adamw(w, g, m, v, name):
    R = w.shape[0]
    tm = _tile(R, 256)
    assert g.shape == w.shape == m.shape == v.shape, (name, w.shape, g.shape)

    def body(w_ref, g_ref, m_ref, v_ref, d_ref, nm_ref, nv_ref):
        g_ = g_ref[...]
        m_ = ADAM_B1 * m_ref[...] + (1.0 - ADAM_B1) * g_
        v_ = ADAM_B2 * v_ref[...] + (1.0 - ADAM_B2) * jnp.square(g_)
        m_hat = m_ / (1.0 - ADAM_B1 ** ADAM_STEP)
        v_hat = v_ / (1.0 - ADAM_B2 ** ADAM_STEP)
        d_ref[...] = -ADAM_LR * (m_hat / (jnp.sqrt(v_hat) + ADAM_EPS) + ADAM_WD * w_ref[...])
        nm_ref[...] = m_
        nv_ref[...] = v_

    spec = pl.BlockSpec((tm, w.shape[1]), lambda i: (i, 0))
    return pl.pallas_call(
        body, name=name, grid=(R // tm,), in_specs=[spec] * 4, out_specs=[spec] * 3,
        out_shape=[jax.ShapeDtypeStruct(w.shape, F32)] * 3, compiler_params=_cparams(("parallel",)),
    )(w, g, m, v)


def add2(x, y, name, out_dtype=F32):
    shape = x.shape
    x, y = x.reshape(-1, shape[-1]), y.reshape(-1, shape[-1])
    R, C = x.shape
    tm = _tile(R, 256)

    def body(x_ref, y_ref, o_ref):
        o_ref[...] = (x_ref[...] + y_ref[...]).astype(o_ref.dtype)

    spec = pl.BlockSpec((tm, C), lambda i: (i, 0))
    return pl.pallas_call(
        body, name=name, grid=(R // tm,), in_specs=[spec, spec], out_specs=spec,
        out_shape=jax.ShapeDtypeStruct((R, C), out_dtype), compiler_params=_cparams(("parallel",)),
    )(x, y).reshape(shape)


def add_chips(own, got, name):
    n, R, C = got.shape
    tm = _tile(R, 256)

    def body(*refs):
        acc = refs[0][...].astype(F32)
        for r in refs[1:-1]:
            acc = acc + r[...].astype(F32)
        refs[-1][...] = acc

    return pl.pallas_call(
        body, name=name, grid=(R // tm,),
        in_specs=[pl.BlockSpec((tm, C), lambda i: (i, 0))] + [pl.BlockSpec((None, tm, C), lambda i, j=j: (j, i, 0))
                                                            for j in range(n)],
        out_specs=pl.BlockSpec((tm, C), lambda i: (i, 0)),
        out_shape=jax.ShapeDtypeStruct((R, C), F32), compiler_params=_cparams(("parallel",)),
    )(own, *([got] * n))


_HBM = pl.BlockSpec(memory_space=pltpu.HBM)
N_CHIPS = 4


def _my_place():
    return lax.axis_index("x"), lax.axis_index("y"), lax.axis_index("c")


def _window(ref, axis, start, size):
    idx = [slice(None)] * len(ref.shape)
    idx[axis] = pl.ds(start, size)
    return ref.at[tuple(idx)]


def _comm_call(body, name, xs, out_shapes, n_remote, n_local):
    return pl.pallas_call(
        body, name=name, in_specs=[_HBM] * len(xs), out_specs=[_HBM] * len(out_shapes), out_shape=out_shapes,
        scratch_shapes=[pltpu.SemaphoreType.DMA((n_remote,)), pltpu.SemaphoreType.DMA((n_remote,)),
                        pltpu.SemaphoreType.DMA((max(n_local, 1),))],
        compiler_params=pltpu.CompilerParams(has_side_effects=True),
    )(*xs)


def _run(copies):
    for cp in copies:
        cp.start()
    for cp in copies:
        cp.wait()


def _other_chips(mx, my):
    return [(mx ^ (j >> 1), my ^ (j & 1)) for j in (1, 2, 3)]


def chip_gather(xs, axes, name):
    n = len(xs)
    shapes, final = [], []
    for x, ax in zip(xs, axes):
        s = list(x.shape)
        if ax is None:
            shapes.append([N_CHIPS] + s)
            final.append(shapes[-1])
        elif ax < x.ndim - 1:
            shapes.append(s[:ax] + [N_CHIPS] + s[ax:])
            final.append(s[:ax] + [N_CHIPS * s[ax]] + s[ax + 1:])
        else:
            assert s[ax] % 128 == 0, (name, s)
            shapes.append(s[:ax] + [N_CHIPS * s[ax]])
            final.append(shapes[-1])

    def body(*refs):
        x_refs, o_refs = refs[:n], refs[n:2 * n]
        send_sems, recv_sems, local_sems = refs[2 * n:]
        mx, my, mc = _my_place()
        q = 2 * mx + my
        copies = []
        for i, (x_ref, o_ref, ax) in enumerate(zip(x_refs, o_refs, axes)):
            if ax is None or ax < len(x_ref.shape) - 1:
                dst = o_ref.at[(slice(None),) * (ax or 0) + (q,)]
            else:
                dst = _window(o_ref, ax, q * x_ref.shape[ax], x_ref.shape[ax])
            copies.append(pltpu.make_async_copy(x_ref, dst, local_sems.at[i]))
            for j, (tx, ty) in enumerate(_other_chips(mx, my)):
                copies.append(pltpu.make_async_remote_copy(
                    src_ref=x_ref, dst_ref=dst, send_sem=send_sems.at[3 * i + j], recv_sem=recv_sems.at[3 * i + j],
                    device_id=(tx, ty, mc), device_id_type=MESH))
        _run(copies)

    out_shapes = [jax.ShapeDtypeStruct(tuple(s), x.dtype) for s, x in zip(shapes, xs)]
    return [o.reshape(f) for o, f in zip(_comm_call(body, name, xs, out_shapes, 3 * n, n), final)]


def gather_two_level(xs, name):
    n = len(xs)
    shapes = [jax.ShapeDtypeStruct((2, N_CHIPS) + x.shape[1:], x.dtype) for x in xs]

    def body(*refs):
        x_refs, o_refs = refs[:n], refs[n:2 * n]
        send_sems, recv_sems, local_sems = refs[2 * n:]
        mx, my, mc = _my_place()
        q = 2 * mx + my
        first, local, second = [], [], []
        for i, (x_ref, o_ref) in enumerate(zip(x_refs, o_refs)):
            local.append(pltpu.make_async_copy(x_ref.at[mc], o_ref.at[mc, q], local_sems.at[i]))
            for j, (tx, ty) in enumerate(_other_chips(mx, my)):
                first.append(pltpu.make_async_remote_copy(
                    src_ref=x_ref.at[mc], dst_ref=o_ref.at[mc, q], send_sem=send_sems.at[4 * i + j],
                    recv_sem=recv_sems.at[4 * i + j], device_id=(tx, ty, mc), device_id_type=MESH))
            second.append(pltpu.make_async_remote_copy(
                src_ref=o_ref.at[mc], dst_ref=o_ref.at[mc], send_sem=send_sems.at[4 * i + 3],
                recv_sem=recv_sems.at[4 * i + 3], device_id=(mx, my, 1 - mc), device_id_type=MESH))
        for cp in local + first:
            cp.start()
        for cp in local:
            cp.wait()
        for cp in first:
            cp.wait_recv()
        _run(second)
        for cp in first:
            cp.wait_send()

    return _comm_call(body, name, xs, shapes, 4 * n, n)


def gather_two_level_sequencer(xs, name, collective_id):
    n = len(xs)
    hbm = pltpu.MemorySpace.HBM
    x_refs = [jax.new_ref(x, memory_space=hbm) for x in xs]
    o_refs = [jax.empty_ref(jax.ShapeDtypeStruct((2, N_CHIPS) + x.shape[1:], x.dtype), memory_space=hbm) for x in xs]

    @pl.kernel(mesh=plsc.ScalarSubcoreMesh(axis_name="sequencer", num_cores=1), name=name,
               scratch_types=(pltpu.SemaphoreType.DMA((4 * n,)), pltpu.SemaphoreType.DMA((4 * n,)),
                              pltpu.SemaphoreType.DMA((n,))),
               compiler_params=pltpu.CompilerParams(collective_id=collective_id))
    def launch(send_sems, recv_sems, local_sems):
        mx, my, mc = _my_place()
        peers = [(tx, ty, mc) for tx, ty in _other_chips(mx, my)] + [(mx, my, 1 - mc)]
        barrier = pltpu.get_barrier_semaphore()
        for peer in peers:
            pl.semaphore_signal(barrier, inc=1, device_id=peer, device_id_type=MESH)
        pl.semaphore_wait(barrier, len(peers))
        q = 2 * mx + my
        first, local, second = [], [], []
        for i, (x_ref, o_ref) in enumerate(zip(x_refs, o_refs)):
            local.append(pltpu.make_async_copy(x_ref.at[mc], o_ref.at[mc, q], local_sems.at[i]))
            for j, peer in enumerate(peers[:3]):
                first.append(pltpu.make_async_remote_copy(
                    src_ref=x_ref.at[mc], dst_ref=o_ref.at[mc, q], send_sem=send_sems.at[4 * i + j],
                    recv_sem=recv_sems.at[4 * i + j], device_id=peer, device_id_type=MESH))
            second.append(pltpu.make_async_remote_copy(
                src_ref=o_ref.at[mc], dst_ref=o_ref.at[mc], send_sem=send_sems.at[4 * i + 3],
                recv_sem=recv_sems.at[4 * i + 3], device_id=peers[3], device_id_type=MESH))
        for cp in local + first:
            cp.start()
        for cp in local:
            cp.wait()
        for cp in first:
            cp.wait_recv()
        _run(second)
        for cp in first:
            cp.wait_send()

    launch()
    return [o[...] for o in o_refs]


def pair_swap(xs, name, halves):
    n = len(xs)
    shapes = [jax.ShapeDtypeStruct(x.shape[1:] if halves else x.shape, x.dtype) for x in xs]

    def body(*refs):
        x_refs, o_refs = refs[:n], refs[n:2 * n]
        send_sems, recv_sems, _ = refs[2 * n:]
        mx, my, mc = _my_place()
        _run([pltpu.make_async_remote_copy(
            src_ref=x_ref.at[1 - mc] if halves else x_ref, dst_ref=o_ref, send_sem=send_sems.at[i],
            recv_sem=recv_sems.at[i], device_id=(mx, my, 1 - mc), device_id_type=MESH)
            for i, (x_ref, o_ref) in enumerate(zip(x_refs, o_refs))])

    return _comm_call(body, name, xs, shapes, n, 0)


def chip_all_to_all(xs, name):
    n = len(xs)
    shapes = [jax.ShapeDtypeStruct((N_CHIPS - 1,) + x.shape[1:], x.dtype) for x in xs]

    def body(*refs):
        x_refs, o_refs = refs[:n], refs[n:2 * n]
        send_sems, recv_sems, _ = refs[2 * n:]
        mx, my, mc = _my_place()
        copies = []
        for i, (x_ref, o_ref) in enumerate(zip(x_refs, o_refs)):
            for j, (tx, ty) in enumerate(_other_chips(mx, my)):
                copies.append(pltpu.make_async_remote_copy(
                    src_ref=x_ref.at[2 * tx + ty], dst_ref=o_ref.at[j], send_sem=send_sems.at[3 * i + j],
                    recv_sem=recv_sems.at[3 * i + j], device_id=(tx, ty, mc), device_id_type=MESH))
        _run(copies)

    return _comm_call(body, name, xs, shapes, 3 * n, 0)


WEIGHTS = ['norm_mix', 'norm_xa', 'norm_mem', 'norm_ffn', 'xa_wq', 'xa_wk', 'xa_wv', 'xa_wo', 'xa_q_norm', 'xa_k_norm',
           'ffn_w_up', 'ffn_conv_w', 'ffn_conv_b', 'ffn_w_down', 'hg_lb_logits', 'mix_w_in', 'hg_out_norm',
           'mla_q_a_norm', 'mla_w_uq', 'mla_kv_a_norm', 'mla_w_ukv', 'mla_qn_nope', 'mla_qn_rope', 'mla_kn_nope',
           'mla_kn_rope', 'mix_w_out', 's5_lam_re', 's5_lam_im', 's5_log_dt', 's5_b_re', 's5_b_im', 's5_c_re',
           's5_c_im', 's5_d', 's5_w_glu_a', 's5_w_glu_b']
INPUTS = ['x', 'mem', 'positions'] + WEIGHTS + ['loss_target'] + ['m_' + n for n in WEIGHTS] + ['v_' + n for n in WEIGHTS]
SHARD_AXIS = {'xa_wq': 1, 'xa_wk': 1, 'xa_wv': 1, 'xa_wo': 1, 'ffn_w_up': 2, 'ffn_conv_w': 2, 'ffn_w_down': 1,
              'mix_w_in': 2, 'mla_w_uq': 2, 'mla_w_ukv': 2, 'mix_w_out': 1, 's5_d': 1, 's5_w_glu_a': 1, 's5_w_glu_b': 1}
BIG = ['xa_wq', 'xa_wk', 'xa_wv', 'xa_wo', 'ffn_w_up', 'ffn_w_down', 'mix_w_in', 'mix_w_out', 's5_w_glu_a', 's5_w_glu_b']
FIRST_NEEDED = ('mix_w_in', 'mix_w_out')
SMALL_SHARDED = [n for n in WEIGHTS if n in SHARD_AXIS and n not in BIG]
REPLICATED = [n for n in WEIGHTS if n not in SHARD_AXIS]
SMALL = SMALL_SHARDED + REPLICATED
PACK_W = 1024
ROW_MULT = 16
W_IN_SHARD = IN_WIDTH // N_CHIPS
W_IN_SHARD_PAD = 640


def _pack(flats, mult=ROW_MULT):
    flat = jnp.concatenate([f.reshape(-1) for f in flats])
    unit = mult * PACK_W
    n = -(-flat.shape[0] // unit) * unit
    return jnp.pad(flat, (0, n - flat.shape[0])).reshape(n // PACK_W, PACK_W)


def _unpack(packed, shapes):
    flat, out, o = packed.reshape(-1), [], 0
    for s in shapes:
        n = math.prod(s)
        out.append(flat[o:o + n].reshape(s))
        o += n
    return out


def _rope_pad(w):
    z = jnp.zeros(w.shape[:-1] + (MLA_ROPE // 2,), w.dtype)
    return jnp.concatenate([w[..., :MLA_ROPE // 2], z, w[..., MLA_ROPE // 2:], z], axis=-1)


def _rope_unpad(g):
    return jnp.concatenate([g[..., :MLA_ROPE // 2], g[..., 64:64 + MLA_ROPE // 2]], axis=-1)


def _blockdiag_in(bb):
    nb = bb.shape[0] // S5_GB
    t = bb.reshape(nb, S5_GB, S5_STATE, S5_GROUP).transpose(0, 1, 3, 2)
    return jnp.einsum('bgmp,gh->bgmhp', t, jnp.eye(S5_GB, dtype=bb.dtype)).reshape(nb, S5_GB * S5_GROUP, S5_LANES)


def _blockdiag_in_t(dw):
    nb = dw.shape[0]
    t = jnp.einsum('bgmhp,gh->bgmp', dw.reshape(nb, S5_GB, S5_GROUP, S5_GB, S5_STATE), jnp.eye(S5_GB, dtype=dw.dtype))
    return t.transpose(0, 1, 3, 2).reshape(nb * S5_GB, S5_STATE, S5_GROUP)


def _blockdiag_out(c):
    nb = c.shape[0] // S5_GB
    t = c.reshape(nb, S5_GB, S5_GROUP, S5_STATE).transpose(0, 1, 3, 2)
    return jnp.einsum('bgpm,gh->bgphm', t, jnp.eye(S5_GB, dtype=c.dtype)).reshape(nb, S5_LANES, S5_GB * S5_GROUP)


def _blockdiag_out_t(dc):
    nb = dc.shape[0]
    t = jnp.einsum('bgphm,gh->bgpm', dc.reshape(nb, S5_GB, S5_STATE, S5_GB, S5_GROUP), jnp.eye(S5_GB, dtype=dc.dtype))
    return t.transpose(0, 1, 3, 2).reshape(nb * S5_GB, S5_GROUP, S5_STATE)


def _gather_weights(P):
    def halves(x):
        return x if x.shape[0] == 2 else x.reshape(2, x.shape[1] // 2, x.shape[2])

    now = [n for n in BIG if n in FIRST_NEEDED]
    later = [n for n in BIG if n not in FIRST_NEEDED]
    xs = [halves(P[n].astype(BF16)) for n in now] + [halves(_pack([P[n] for n in SMALL_SHARDED], 2 * ROW_MULT)[None])]
    got = gather_two_level(xs, "gather_weights")
    got_later = gather_two_level_sequencer([halves(P[n].astype(BF16)) for n in later], "gather_weights_later", 1)
    full_w = {}
    for n, g in list(zip(now, got[:-1])) + list(zip(later, got_later)):
        two_layers, by_rows = P[n].shape[0] == 2, SHARD_AXIS[n] == 1
        if two_layers and by_rows:
            full_w[n] = g.reshape(2, N_CHIPS * g.shape[2], g.shape[3])
        elif two_layers:
            full_w[n] = g.transpose(0, 2, 1, 3).reshape(2, g.shape[2], N_CHIPS * g.shape[3])
        elif by_rows:
            full_w[n] = g.transpose(1, 0, 2, 3).reshape(1, 2 * N_CHIPS * g.shape[2], g.shape[3])
        else:
            full_w[n] = g.transpose(0, 2, 1, 3).reshape(1, 2 * g.shape[2], N_CHIPS * g.shape[3])
    small = got[-1].transpose(1, 0, 2, 3).reshape(N_CHIPS, -1, PACK_W)
    per_chip = [_unpack(small[q], [P[n].shape for n in SMALL_SHARDED]) for q in range(N_CHIPS)]
    for i, n in enumerate(SMALL_SHARDED):
        full_w[n] = jnp.concatenate([per_chip[q][i] for q in range(N_CHIPS)], axis=SHARD_AXIS[n])
    return full_w


def _halves_first(x):
    return x.reshape(x.shape[0], 2, x.shape[1] // 2, x.shape[2]).transpose(1, 0, 2, 3)


def _reduce_and_update(GB, GS, P):
    mx, my, mc = _my_place()
    q = 2 * mx + my
    small = _pack([GS[n] for n in SMALL], 2 * N_CHIPS * ROW_MULT)
    xs = [GB[n] for n in BIG] + [_halves_first(small.reshape(N_CHIPS, -1, PACK_W))]
    theirs = pair_swap(xs, "grads_pair_swap", True)
    names = BIG + ['small']
    pair = [add2(lax.dynamic_index_in_dim(x, mc, 0, False), t, "grads_pair_sum_" + n, F32 if n == 'small' else BF16)
            for x, t, n in zip(xs, theirs, names)]
    got = chip_all_to_all(pair, "grads_chip_all_to_all")
    summed = [add_chips(lax.dynamic_index_in_dim(p, q, 0, False), g, "grads_chip_sum_" + n)
              for p, g, n in zip(pair, got, names)]
    other = pair_swap(summed, "grads_pair_join", False)
    joined = [lax.cond(mc == 0, lambda a, b: jnp.concatenate([a, b], axis=0), lambda a, b: jnp.concatenate([b, a], axis=0),
                       s, o) for s, o in zip(summed, other)]
    small_sum = chip_gather([joined[-1]], [0], "grads_small_gather")[0]
    g_small = dict(zip(SMALL, _unpack(small_sum, [GS[n].shape for n in SMALL])))
    for n in SMALL_SHARDED:
        s = P[n].shape[SHARD_AXIS[n]]
        g_small[n] = lax.dynamic_slice_in_dim(g_small[n], q * s, s, axis=SHARD_AXIS[n])

    grad, delta, new_m, new_v = {}, {}, {}, {}
    for n, g in zip(BIG, joined[:-1]):
        shape = P[n].shape
        if n == 'mix_w_in':
            g = g[:, :W_IN_SHARD]
        two_d = (g.shape[0], shape[-1])
        d, m_, v_ = adamw(P[n].reshape(two_d), g, P['m_' + n].reshape(two_d), P['v_' + n].reshape(two_d), "adamw_" + n)
        grad[n], delta[n], new_m[n], new_v[n] = (t.reshape(shape) for t in (g, d, m_, v_))
    packed = lambda prefix: _pack([P[prefix + n] for n in SMALL])
    d, m_, v_ = adamw(packed(''), _pack([g_small[n] for n in SMALL]), packed('m_'), packed('v_'), "adamw_small")
    shapes = [P[n].shape for n in SMALL]
    grad.update(g_small)
    for out, pk in ((delta, d), (new_m, m_), (new_v, v_)):
        out.update(zip(SMALL, _unpack(pk, shapes)))
    return grad, delta, new_m, new_v


def _row(v):
    return v.reshape(1, -1)


def _xattn_fwd(h, mem, W, lyr, tm):
    g_xa, g_mem = _row(W['norm_xa'][lyr]), _row(W['norm_mem'][lyr])
    g_q, g_k = _row(W['xa_q_norm'][lyr]), _row(W['xa_k_norm'][lyr])
    wq, wk, wv, wo = (W[n][lyr] for n in ('xa_wq', 'xa_wk', 'xa_wv', 'xa_wo'))
    L, D = h.shape
    M = mem.shape[0]
    hx = rms_fwd(h, g_xa, tm, MXU_DTYPE)
    qp = matmul(hx, wq, name="xa_q")
    kv_opds = [full(mem), full(g_mem), full(wk), full(wv), full(g_k)]
    k, v = blocked_fwd(_mem_kv, kv_opds, [((M, D), F32, (M, D), lambda i: (0, 0))] * 2, 1, "xa_mem_kv")
    o = blocked_fwd(_xa_core, [rows(qp, tm), full(k), full(v), full(g_q)], [_out((L, D), MXU_DTYPE, tm)], L // tm,
                    "xa_core")[0]
    out = matmul(o, wo, add=h, name="xa_o")
    return out, (h, hx, qp, k, v, o)


def _xattn_bwd(dout, saved, mem, W, lyr, tm, bufs):
    h, hx, qp, k, v, o = saved
    g_xa, g_mem = _row(W['norm_xa'][lyr]), _row(W['norm_mem'][lyr])
    g_q, g_k = _row(W['xa_q_norm'][lyr]), _row(W['xa_k_norm'][lyr])
    wq, wk, wv, wo = (W[n][lyr] for n in ('xa_wq', 'xa_wk', 'xa_wv', 'xa_wo'))
    L = h.shape[0]
    do = matmul(dout, wo, "nt", name="xa_do")
    bufs['xa_wo'] = matmul(o, dout, "tn", name="xa_dwo", into=(bufs['xa_wo'], lyr))
    dqp, dk, dv, d_gq = blocked_bwd(_xa_core, [rows(qp, tm, 'blk'), full(k, 'acc'), full(v, 'acc'), full(g_q, 'acc')],
                                    [rows(do, tm)], L // tm, "xa_core_bwd")
    bufs['xa_wq'] = matmul(hx, dqp, "tn", name="xa_dwq", into=(bufs['xa_wq'], lyr))
    dhx = matmul(dqp, wq, "nt", name="xa_dhx")
    dh, d_gxa = rms_bwd(h, g_xa, dhx, tm, dout)
    d_gmem, d_wk, d_wv, d_gk = blocked_bwd(
        _mem_kv, [full(mem), full(g_mem, 'acc'), full(wk, 'acc'), full(wv, 'acc'), full(g_k, 'acc')],
        [full(dk), full(dv)], 1, "xa_mem_kv_bwd")
    by_chip = lambda g: g.reshape(N_CHIPS, g.shape[0] // N_CHIPS, g.shape[1])
    grads = {'norm_xa': d_gxa, 'norm_mem': d_gmem, 'xa_q_norm': d_gq, 'xa_k_norm': d_gk,
             'xa_wk': by_chip(d_wk), 'xa_wv': by_chip(d_wv)}
    return dh, grads


def _conv_params(W, lyr):
    cw, cb = W['ffn_conv_w'][lyr], W['ffn_conv_b'][lyr]
    F = cw.shape[1] // 2
    return [cw[0:1, :F], cw[1:2, :F], cw[2:3, :F], cw[0:1, F:], cw[1:2, F:], cw[2:3, F:], _row(cb[:F]), _row(cb[F:])]


def _ffn_fwd(h, W, lyr, tm):
    L, D = h.shape
    w_up, w_down = W['ffn_w_up'][lyr], W['ffn_w_down'][lyr]
    F = w_down.shape[0]
    hf = rms_fwd(h, _row(W['norm_ffn'][lyr]), tm, MXU_DTYPE)
    ug = matmul(hf, w_up[:, :F], name="ffn_up_gate")
    uv = matmul(hf, w_up[:, F:], name="ffn_up_value")
    opds = [cols(ug, 128), cols(uv, 128)] + [cols(p, 128) for p in _conv_params(W, lyr)]
    a = blocked_fwd(_conv_gate, opds, [((L, F), MXU_DTYPE, (L, 128), lambda j: (0, j))], F // 128, "ffn_conv_gate")[0]
    out = matmul(a, w_down, add=h, name="ffn_down")
    return out, (h, hf, ug, uv, a)


def _ffn_bwd(dout, saved, W, lyr, tm, bufs):
    h, hf, ug, uv, a = saved
    w_up, w_down = W['ffn_w_up'][lyr], W['ffn_w_down'][lyr]
    F = w_down.shape[0]
    da = matmul(dout, w_down, "nt", name="ffn_da")
    bufs['ffn_w_down'] = matmul(a, dout, "tn", name="ffn_dwdown", into=(bufs['ffn_w_down'], lyr))
    opds = [cols(ug, 128, 'blk'), cols(uv, 128, 'blk')] + [cols(p, 128, 'blk') for p in _conv_params(W, lyr)]
    gs = blocked_bwd(_conv_gate, opds, [cols(da, 128)], F // 128, "ffn_conv_gate_bwd")
    dug, duv = gs[0], gs[1]
    d_cw = jnp.concatenate([jnp.concatenate(gs[2:5], axis=0), jnp.concatenate(gs[5:8], axis=0)], axis=1)
    d_cb = jnp.concatenate([gs[8], gs[9]], axis=1)[0]
    half = N_CHIPS // 2
    bufs['ffn_w_up'] = matmul(hf, dug, "tn", name="ffn_dwup_gate", into=(bufs['ffn_w_up'], lyr, 0), col_blocks=half)
    bufs['ffn_w_up'] = matmul(hf, duv, "tn", name="ffn_dwup_value", into=(bufs['ffn_w_up'], lyr, half), col_blocks=half)
    dhf = matmul(dug, w_up[:, :F], "nt", name="ffn_dhf_gate")
    dhf = matmul(duv, w_up[:, F:], "nt", add=dhf, name="ffn_dhf_value")
    dh, d_g = rms_bwd(h, _row(W['norm_ffn'][lyr]), dhf, tm, dout)
    return dh, {'norm_ffn': d_g, 'ffn_conv_w': d_cw, 'ffn_conv_b': d_cb}


def _mla_params(W):
    w_uq = W['mla_w_uq'][0].reshape(MLA_Q_RANK, MLA_HEADS, MLA_QK)
    w_uq = jnp.concatenate([w_uq[..., :MLA_NOPE], _rope_pad(w_uq[..., MLA_NOPE:])], axis=-1)
    w_ukv = W['mla_w_ukv'][0].reshape(MLA_KV_RANK, MLA_HEADS, MLA_NOPE + MLA_V)
    w_ukv = jnp.concatenate([w_ukv[..., :MLA_NOPE].reshape(MLA_KV_RANK, -1), w_ukv[..., MLA_NOPE:].reshape(MLA_KV_RANK, -1)],
                            axis=1)
    return [_row(W['mla_q_a_norm'][0]), w_uq.reshape(MLA_Q_RANK, MLA_HEADS * MLA_DK), _row(W['mla_kv_a_norm'][0]), w_ukv,
            _row(W['mla_qn_nope'][0]), _row(_rope_pad(W['mla_qn_rope'][0])), _row(W['mla_kn_nope'][0]),
            _row(_rope_pad(W['mla_kn_rope'][0]))]


def _w_in_padded(W):
    w = W['mix_w_in'][0]
    return jnp.concatenate([w[:, :IN_WIDTH - MLA_ROPE], _rope_pad(w[:, IN_WIDTH - MLA_ROPE:])], axis=1)


def _mixer0_fwd(h, W, cos_p, sin_p, tm):
    L = h.shape[0]
    t = min(ATTN_ROWS, L // ATTN_WIDE)
    hn = rms_fwd(h, _row(W['norm_mix'][0]), tm, MXU_DTYPE)
    proj = matmul(hn, _w_in_padded(W), name="mix_in")
    logits = W['hg_lb_logits']
    lb = blocked_fwd(_lb_first, [full(logits)], [((1, HG_WIDTH), F32, (1, HG_WIDTH), lambda i: (0, 0))], 1, "hg_lb")[0]
    gain = _row(W['hg_out_norm'][0])
    o_hg, states = hgrn2_fwd(proj, lb, gain)
    mp = _mla_params(W)
    q, k, v = mla_prep_fwd(proj, cos_p, sin_p, mp, tm)
    scale = MLA_QK ** -0.5
    o_mla, lse = attn_fwd(q, k, v, scale, t)
    w_out = W['mix_w_out'][0]
    out = matmul(o_hg, w_out[:HG_WIDTH], add=h, name="mix_out_hg")
    out = matmul(o_mla, w_out[HG_WIDTH:], add=out, name="mix_out_mla")
    return out, (h, hn, proj, lb, o_hg, states, q, k, v, o_mla, lse)


def _mixer0_bwd(dout, saved, W, cos_p, sin_p, tm):
    h, hn, proj, lb, o_hg, states, q, k, v, o_mla, lse = saved
    L = h.shape[0]
    t = min(ATTN_ROWS, L // ATTN_WIDE)
    scale = MLA_QK ** -0.5
    w_out = W['mix_w_out'][0]
    gain = _row(W['hg_out_norm'][0])
    do_hg = matmul(dout, w_out[:HG_WIDTH], "nt", name="mix_do_hg")
    do_mla = matmul(dout, w_out[HG_WIDTH:], "nt", name="mix_do_mla")
    d_wout = jnp.concatenate([matmul(o_hg, dout, "tn", name="mix_dwout_hg"), matmul(o_mla, dout, "tn", name="mix_dwout_mla")],
                             axis=0)
    dq = attn_bwd_dq(q, k, v, o_mla, lse, do_mla, scale, t)
    dk, dv = attn_bwd_dkv(q, k, v, o_mla, lse, do_mla, scale, t)
    mp = _mla_params(W)
    d_mla, d_qa, d_wuq, d_kva, d_wukv, d_qnn, d_qnr, d_knn, d_knr = mla_prep_bwd(proj, cos_p, sin_p, mp, dq, dk, dv, tm)
    d_hg, d_lb, d_gain = hgrn2_bwd(proj, lb, gain, states, do_hg)
    w_in, n_hg = _w_in_padded(W), 4 * HG_WIDTH
    d_win = jnp.concatenate([matmul(hn, d_hg, "tn", name="mix_dwin_hg"), matmul(hn, d_mla, "tn", name="mix_dwin_mla")], axis=1)
    dhn = matmul(d_hg, w_in[:, :n_hg], "nt", name="mix_dhn_hg")
    dhn = matmul(d_mla, w_in[:, n_hg:], "nt", add=dhn, name="mix_dhn_mla")
    dh, d_g = rms_bwd(h, _row(W['norm_mix'][0]), dhn, tm, dout)
    logits = W['hg_lb_logits']
    d_logits = blocked_bwd(_lb_first, [full(logits, 'acc')], [full(d_lb)], 1, "hg_lb_bwd")[0]
    d_wuq = d_wuq.reshape(MLA_Q_RANK, MLA_HEADS, MLA_DK)
    d_wuq = jnp.concatenate([d_wuq[..., :MLA_NOPE], _rope_unpad(d_wuq[..., MLA_NOPE:])], axis=-1)
    hw = MLA_HEADS * MLA_NOPE
    d_wukv = jnp.concatenate([d_wukv[:, :hw].reshape(MLA_KV_RANK, MLA_HEADS, MLA_NOPE),
                              d_wukv[:, hw:].reshape(MLA_KV_RANK, MLA_HEADS, MLA_V)], axis=-1)
    d_win = jnp.concatenate([d_win[:, :IN_WIDTH - MLA_ROPE], _rope_unpad(d_win[:, IN_WIDTH - MLA_ROPE:])], axis=1)
    d_win = d_win.reshape(d_win.shape[0], N_CHIPS, W_IN_SHARD).transpose(1, 0, 2)
    d_win = jnp.pad(d_win, ((0, 0), (0, 0), (0, W_IN_SHARD_PAD - W_IN_SHARD)))
    d_wout = d_wout.reshape(N_CHIPS, d_wout.shape[0] // N_CHIPS, d_wout.shape[1])
    grads = {'norm_mix': d_g, 'hg_lb_logits': d_logits, 'mix_w_in': d_win, 'hg_out_norm': d_gain,
             'mla_q_a_norm': d_qa, 'mla_w_uq': d_wuq.reshape(1, MLA_Q_RANK, -1), 'mla_kv_a_norm': d_kva,
             'mla_w_ukv': d_wukv.reshape(1, MLA_KV_RANK, -1), 'mla_qn_nope': d_qnn, 'mla_qn_rope': _rope_unpad(d_qnr),
             'mla_kn_nope': d_knn, 'mla_kn_rope': _rope_unpad(d_knr), 'mix_w_out': d_wout}
    return dh,grads


def _s5_inputs(W):
    G = W['s5_lam_re'].shape[1]
    return [W['s5_lam_re'][0], W['s5_lam_im'][0], W['s5_log_dt'][0].reshape(G, 1),
            W['s5_b_re'][0].reshape(G, -1), W['s5_b_im'][0].reshape(G, -1)]


def _mixer1_fwd(h, W, tm):
    L, D = h.shape
    u = rms_fwd(h, _row(W['norm_mix'][1]), tm, F32)
    di = _s5_inputs(W)
    G = di[0].shape[0]
    sq, wide = ((G, S5_STATE), F32, (G, S5_STATE), lambda i: (0, 0)), ((G, S5_STATE * S5_GROUP), F32, (G, S5_STATE * S5_GROUP), lambda i: (0, 0))
    ar, ai, bbr, bbi = blocked_fwd(_s5_discretize, [full(a) for a in di], [sq, sq, wide, wide], 1, "s5_discretize")
    nb = G // S5_GB
    core = (_blockdiag_in(bbr.reshape(G, S5_STATE, S5_GROUP)), _blockdiag_in(bbi.reshape(G, S5_STATE, S5_GROUP)),
            ar.reshape(nb, 1, S5_LANES), ai.reshape(nb, 1, S5_LANES),
            _blockdiag_out(W['s5_c_re'][0]), _blockdiag_out(W['s5_c_im'][0]))
    y = s5_fwd(u, *core)
    d = W['s5_d']
    y2 = blocked_fwd(_s5_post, [rows(y, tm), rows(u, tm), full(d)], [_out((L, D), MXU_DTYPE, tm)], L // tm, "s5_post")[0]
    w_ab = jnp.concatenate([W['s5_w_glu_a'][0], W['s5_w_glu_b'][0]], axis=1)
    ab = matmul(y2, w_ab, name="s5_glu_in")
    mix = blocked_fwd(_glu, [rows(ab, tm, col=0, width=D), rows(ab, tm, col=1, width=D)], [_out((L, D), F32, tm)], L // tm,
                      "s5_glu")[0]
    return h + mix, (h, u, core, y, y2, ab)


def _mixer1_bwd(dout, saved, W, tm):
    h, u, core, y, y2, ab = saved
    L, D = h.shape
    da, db = blocked_bwd(_glu, [rows(ab, tm, 'blk', col=0, width=D), rows(ab, tm, 'blk', col=1, width=D)], [rows(dout, tm)],
                         L // tm, "s5_glu_bwd")
    w_a, w_b = W['s5_w_glu_a'][0], W['s5_w_glu_b'][0]
    dy2 = matmul(da, w_a, "nt", name="s5_dy2_a")
    dy2 = matmul(db, w_b, "nt", add=dy2, name="s5_dy2_b")
    d_wa = matmul(y2, da, "tn", name="s5_dwa")
    d_wb = matmul(y2, db, "tn", name="s5_dwb")
    d = W['s5_d']
    dy, du_skip, d_d = blocked_bwd(_s5_post, [rows(y, tm, 'blk'), rows(u, tm, 'blk'), full(d, 'acc')], [rows(dy2, tm)], L // tm,
                                   "s5_post_bwd")
    du, dwr, dwi, dar, dai, dcr, dci = s5_bwd(u, *core, dy, min(256, L))
    di = _s5_inputs(W)
    G = di[0].shape[0]
    cts = [dar.reshape(G, S5_STATE), dai.reshape(G, S5_STATE), _blockdiag_in_t(dwr).reshape(G, -1), _blockdiag_in_t(dwi).reshape(G, -1)]
    d_lr, d_li, d_ldt, d_br, d_bi = blocked_bwd(_s5_discretize, [full(a, 'acc') for a in di], [full(c) for c in cts], 1,
                                                "s5_discretize_bwd")
    dh, d_g = rms_bwd(h, _row(W['norm_mix'][1]), du + du_skip, tm, dout)
    bshape = W['s5_b_re'].shape
    grads = {'norm_mix': d_g, 's5_lam_re': d_lr[None], 's5_lam_im': d_li[None], 's5_log_dt': d_ldt.reshape(1, G),
             's5_b_re': d_br.reshape(bshape), 's5_b_im': d_bi.reshape(bshape), 's5_c_re': _blockdiag_out_t(dcr)[None],
             's5_c_im': _blockdiag_out_t(dci)[None], 's5_d': d_d, 's5_w_glu_a': d_wa.reshape(N_CHIPS, -1, D), 's5_w_glu_b': d_wb.reshape(N_CHIPS, -1, D)}
    return dh,grads


def kernel(x, mem, positions, norm_mix, norm_xa, norm_mem, norm_ffn, xa_wq, xa_wk, xa_wv, xa_wo, xa_q_norm, xa_k_norm, ffn_w_up, ffn_conv_w, ffn_conv_b, ffn_w_down, hg_lb_logits, mix_w_in, hg_out_norm, mla_q_a_norm, mla_w_uq, mla_kv_a_norm, mla_w_ukv, mla_qn_nope, mla_qn_rope, mla_kn_nope, mla_kn_rope, mix_w_out, s5_lam_re, s5_lam_im, s5_log_dt, s5_b_re, s5_b_im, s5_c_re, s5_c_im, s5_d, s5_w_glu_a, s5_w_glu_b, loss_target, m_norm_mix, m_norm_xa, m_norm_mem, m_norm_ffn, m_xa_wq, m_xa_wk, m_xa_wv, m_xa_wo, m_xa_q_norm, m_xa_k_norm, m_ffn_w_up, m_ffn_conv_w, m_ffn_conv_b, m_ffn_w_down, m_hg_lb_logits, m_mix_w_in, m_hg_out_norm, m_mla_q_a_norm, m_mla_w_uq, m_mla_kv_a_norm, m_mla_w_ukv, m_mla_qn_nope, m_mla_qn_rope, m_mla_kn_nope, m_mla_kn_rope, m_mix_w_out, m_s5_lam_re, m_s5_lam_im, m_s5_log_dt, m_s5_b_re, m_s5_b_im, m_s5_c_re, m_s5_c_im, m_s5_d, m_s5_w_glu_a, m_s5_w_glu_b, v_norm_mix, v_norm_xa, v_norm_mem, v_norm_ffn, v_xa_wq, v_xa_wk, v_xa_wv, v_xa_wo, v_xa_q_norm, v_xa_k_norm, v_ffn_w_up, v_ffn_conv_w, v_ffn_conv_b, v_ffn_w_down, v_hg_lb_logits, v_mix_w_in, v_hg_out_norm, v_mla_q_a_norm, v_mla_w_uq, v_mla_kv_a_norm, v_mla_w_ukv, v_mla_qn_nope, v_mla_qn_rope, v_mla_kn_nope, v_mla_kn_rope, v_mix_w_out, v_s5_lam_re, v_s5_lam_im, v_s5_log_dt, v_s5_b_re, v_s5_b_im, v_s5_c_re, v_s5_c_im, v_s5_d, v_s5_w_glu_a, v_s5_w_glu_b):
    P = dict(locals())
    assert sorted(P) == sorted(INPUTS) and norm_mix.shape[0] == 2 and mix_w_in.shape[0] == 1
    x, mem, target = P['x'][0], P['mem'][0], P['loss_target'][0]
    L, D = x.shape
    tm = min(256, L)

    W = {n: P[n] for n in REPLICATED}
    W.update(_gather_weights(P))

    inv_freq = 1.0 / (ROPE_BASE ** (jnp.arange(0, MLA_ROPE, 2, dtype=F32) / MLA_ROPE))
    ang = P['positions'][0].astype(F32)[:, None] * inv_freq
    cos, sin, z = jnp.cos(ang), jnp.sin(ang), jnp.zeros_like(ang)
    cos_p = jnp.concatenate([cos, z, cos, z], axis=1)
    sin_p = jnp.concatenate([-sin, z, sin, z], axis=1)

    h, s_mix0 = _mixer0_fwd(x, W, cos_p, sin_p, tm)
    h, s_xa0 = _xattn_fwd(h, mem, W, 0, tm)
    h, s_ffn0 = _ffn_fwd(h, W, 0, tm)
    h, s_mix1 = _mixer1_fwd(h, W, tm)
    h, s_xa1 = _xattn_fwd(h, mem, W, 1, tm)
    h, s_ffn1 = _ffn_fwd(h, W, 1, tm)
    n = L // tm
    dh, parts = blocked_fwd(_loss_fn, [rows(h, tm), rows(target, tm)],
                            [_out((L, D), F32, tm), ((n * 8, 128), F32, (8, 128), lambda i: (i, 0))], n, "loss")
    loss = lax.psum(jnp.sum(parts), ("x", "y", "c"))

    layered = {}

    def collect(g, lyr):
        for k_, v_ in g.items():
            layered.setdefault(k_, {})[lyr] = v_

    bufs = {n_: lax.empty(W[n_].shape, F32) for n_ in ('xa_wq', 'xa_wo', 'ffn_w_down')}
    d_ff = W['ffn_w_up'].shape[2] // N_CHIPS
    bufs['ffn_w_up'] = lax.empty((2, N_CHIPS, D, d_ff), F32)
    dh, g = _ffn_bwd(dh, s_ffn1, W, 1, tm, bufs)
    collect(g, 1)
    dh, g = _xattn_bwd(dh, s_xa1, mem, W, 1, tm, bufs)
    collect(g, 1)
    dh, g = _mixer1_bwd(dh, s_mix1, W, tm)
    collect(g, 1)
    dh, g = _ffn_bwd(dh, s_ffn0, W, 0, tm, bufs)
    collect(g, 0)
    dh, g = _xattn_bwd(dh, s_xa0, mem, W, 0, tm, bufs)
    collect(g, 0)
    dx, g = _mixer0_bwd(dh, s_mix0, W, cos_p, sin_p, tm)
    collect(g, 0)

    GB, GS = {}, {}
    for name in WEIGHTS:
        if name in bufs:
            b_ = bufs[name]
            GB[name] = b_ if b_.ndim == 4 else b_.reshape(2, N_CHIPS, b_.shape[1] // N_CHIPS, b_.shape[2])
            continue
        by_layer = [layered[name][lyr] for lyr in sorted(layered[name])]
        if name in BIG:
            GB[name] = _halves_first(by_layer[0]) if len(by_layer) == 1 else jnp.stack(by_layer)
        else:
            full_shape = W[name].shape
            GS[name] = (by_layer[0].reshape(full_shape) if len(by_layer) == 1
                        else jnp.stack([g_.reshape(full_shape[1:]) for g_ in by_layer]))

    outs = _reduce_and_update(GB, GS, P)
    return (loss, dx[None], *[d[n_] for d in outs for n_ in WEIGHTS])
```

```python
import functools
import math

import jax
import jax.numpy as jnp
import numpy as np
from jax import lax
from jax.experimental import pallas as pl
from jax.experimental.pallas import tpu as pltpu
from jax.experimental.pallas import tpu_sc as plsc

F32 = jnp.float32
BF16 = jnp.bfloat16
MXU_DTYPE = BF16
HI = lax.Precision.HIGHEST
V7X_VMEM_LIMIT_BYTES = 56 * 1024 * 1024
EPS = 1e-6
MESH = pl.DeviceIdType.MESH

HG_HEADS, HG_DIM = 4, 128
HG_WIDTH = HG_HEADS * HG_DIM
HG_SUB = 32
HG_BLOCK = 64
MLA_HEADS, MLA_Q_RANK, MLA_KV_RANK = 4, 256, 128
MLA_NOPE, MLA_ROPE, MLA_V = 128, 64, 128
MLA_QK = MLA_NOPE + MLA_ROPE
MLA_DK = 256
ROPE_BASE = 10000.0
IN_WIDTH = 4 * HG_WIDTH + MLA_Q_RANK + MLA_KV_RANK + MLA_ROPE
IN_PAD = 4 * HG_WIDTH + MLA_Q_RANK + MLA_KV_RANK + 128
S5_GROUP, S5_STATE = 16, 64
S5_GB = 8
DT_MIN, DT_MAX = 1e-3, 1e-1
XA_HEADS = 4
CONV_W = 3
ADAM_LR, ADAM_B1, ADAM_B2, ADAM_EPS, ADAM_WD, ADAM_STEP = 0.001, 0.9, 0.999, 1e-08, 0.01, 10


def _cparams(sem):
    return pltpu.CompilerParams(dimension_semantics=sem, vmem_limit_bytes=V7X_VMEM_LIMIT_BYTES)


class Opd:
    def __init__(self, arr, block, imap, grad=None, gshape=None, gimap=None):
        self.arr, self.block, self.imap, self.grad = arr, block, imap, grad
        self.gshape = arr.shape if gshape is None else gshape
        self.gimap = imap if gimap is None else gimap

    def spec(self):
        return pl.BlockSpec(self.block, self.imap)

    def gspec(self):
        return pl.BlockSpec(self.block, self.gimap)


def rows(arr, tm, grad=None, col=0, width=None):
    width = arr.shape[1] if width is None else width
    return Opd(arr, (tm, width), lambda i, c=col: (i, c), grad, (arr.shape[0], width), lambda i: (i, 0))


def cols(arr, tn, grad=None):
    return Opd(arr, (arr.shape[0], tn), lambda j: (0, j), grad)


def full(arr, grad=None):
    return Opd(arr, arr.shape, lambda i: (0, 0), grad)


def _load(ref):
    v = ref[...]
    return v.astype(F32) if jnp.issubdtype(v.dtype, jnp.floating) else v


def blocked_fwd(f, opds, outs, n, name):
    n_in = len(opds)

    def body(*refs):
        ys = f(*[_load(r) for r in refs[:n_in]])
        for r, y in zip(refs[n_in:], ys):
            r[...] = y.astype(r.dtype)

    res = pl.pallas_call(
        body, name=name, grid=(n,),
        in_specs=[o.spec() for o in opds],
        out_specs=[pl.BlockSpec(b, m) for (_, _, b, m) in outs],
        out_shape=[jax.ShapeDtypeStruct(s, d) for (s, d, _, _) in outs],
        compiler_params=_cparams(("parallel",)),
    )(*[o.arr for o in opds])
    return res


def blocked_bwd(f, opds, dys, n, name, plus=None):
    n_in, n_dy = len(opds), len(dys)
    diff = [i for i, o in enumerate(opds) if o.grad]
    extra = [] if plus is None else [plus]

    def body(*refs):
        vals = [_load(r) for r in refs[:n_in]]

        def fd(*dv):
            allv = list(vals)
            for i, v in zip(diff, dv):
                allv[i] = v
            return tuple(f(*allv))

        ys, vjp = jax.vjp(fd, *[vals[i] for i in diff])
        cts = tuple(_load(r).astype(y.dtype) for r, y in zip(refs[n_in:n_in + n_dy], ys))
        gs = list(vjp(cts))
        if extra:
            gs[0] = gs[0] + _load(refs[n_in + n_dy])
        for r, g, i in zip(refs[n_in + n_dy + len(extra):], gs, diff):
            if opds[i].grad == 'acc':
                @pl.when(pl.program_id(0) == 0)
                def _(r=r):
                    r[...] = jnp.zeros(r.shape, r.dtype)
                r[...] += g.astype(r.dtype)
            else:
                r[...] = g.astype(r.dtype)

    any_acc = any(opds[i].grad == 'acc' for i in diff)
    res = pl.pallas_call(
        body, name=name, grid=(n,),
        in_specs=[o.spec() for o in opds + dys + extra],
        out_specs=[opds[i].gspec() for i in diff],
        out_shape=[jax.ShapeDtypeStruct(opds[i].gshape, F32) for i in diff],
        compiler_params=_cparams(("arbitrary" if any_acc else "parallel",)),
    )(*[o.arr for o in opds + dys + extra])
    return res


def _tile(dim, want):
    for t in range(want - want % 16, 0, -16):
        if dim % t == 0:
            return t
    assert dim <= want, (dim, want)
    return dim


MATMUL_VMEM_BUDGET = 40 * 1024 * 1024
MATMUL_ROWS = 512


def _widest(N, fits):
    for t in range(N - N % 128, 0, -128):
        if N % t == 0 and fits(t):
            return t
    return N


def matmul(a, b, mode="nn", out_dtype=F32, add=None, name="matmul", into=None, col_blocks=None):
    sa, sb, so = a.dtype.itemsize, b.dtype.itemsize, jnp.dtype(out_dtype).itemsize
    has_add = add is not None
    if mode == "tn":
        (K, M), (K2, N) = a.shape, b.shape
        assert K == K2 and not has_add and out_dtype == F32, (a.shape, b.shape)
        tk = _tile(K, MATMUL_ROWS)
        tn = _widest(N, lambda t: 2 * (tk * M * sa + tk * t * sb + M * t * 4) <= MATMUL_VMEM_BUDGET)
        extra, alias = [], {}
        if into is not None:
            buf, lead = into[0], tuple(into[1:])
            if col_blocks is not None:
                assert N % col_blocks == 0 and (N // col_blocks) % 128 == 0 and tn >= N // col_blocks, (N, col_blocks, tn)
                tn = N // col_blocks
                assert buf.shape[len(lead):] == (M, tn), (buf.shape, lead, M, tn)
                out_spec = pl.BlockSpec((None,) * len(lead) + (M, tn), lambda j, k: lead[:-1] + (lead[-1] + j, 0, 0))
            else:
                assert buf.shape[len(lead):] == (M, N), (buf.shape, lead, M, N)
                out_spec = pl.BlockSpec((None,) * len(lead) + (M, tn), lambda j, k: lead + (0, j))
            out_shape = jax.ShapeDtypeStruct(buf.shape, F32)
            extra, alias = [buf], {2: 0}
        else:
            assert col_blocks is None
            out_spec = pl.BlockSpec((M, tn), lambda j, k: (0, j))
            out_shape = jax.ShapeDtypeStruct((M, N), F32)

        def body(a_ref, b_ref, *rest):
            o_ref = rest[-1]
            r = lax.dot_general(a_ref[...].astype(MXU_DTYPE), b_ref[...].astype(MXU_DTYPE), ((_TN), ((), ())),
                                preferred_element_type=F32)

            @pl.when(pl.program_id(1) == 0)
            def _():
                o_ref[...] = r

            @pl.when(pl.program_id(1) > 0)
            def _():
                o_ref[...] += r

        return pl.pallas_call(
            body, name=name, grid=(N // tn, K // tk),
            in_specs=[pl.BlockSpec((tk, M), lambda j, k: (k, 0)), pl.BlockSpec((tk, tn), lambda j, k: (k, j))]
            + [pl.BlockSpec(memory_space=pl.ANY)] * len(extra),
            out_specs=out_spec, out_shape=out_shape, input_output_aliases=alias,
            compiler_params=_cparams(("parallel", "arbitrary")),
        )(a, b, *extra)

    (M, K) = a.shape
    N = b.shape[1] if mode == "nn" else b.shape[0]
    assert K == (b.shape[0] if mode == "nn" else b.shape[1]), (a.shape, b.shape, mode)
    tm = _tile(M, MATMUL_ROWS)
    tn = _widest(N, lambda t: 2 * (tm * K * sa + K * t * sb + tm * t * (so + 4 * has_add)) <= MATMUL_VMEM_BUDGET)
    dims = ((_NN if mode == "nn" else _NT), ((), ()))

    def body(*refs):
        r = lax.dot_general(refs[0][...].astype(MXU_DTYPE), refs[1][...].astype(MXU_DTYPE), dims, preferred_element_type=F32)
        if has_add:
            r = r + refs[2][...].astype(F32)
        refs[-1][...] = r.astype(refs[-1].dtype)

    b_spec = pl.BlockSpec((K, tn), lambda j, i: (0, j)) if mode == "nn" else pl.BlockSpec((tn, K), lambda j, i: (j, 0))
    in_specs = [pl.BlockSpec((tm, K), lambda j, i: (i, 0)), b_spec]
    args = [a, b]
    if has_add:
        in_specs.append(pl.BlockSpec((tm, tn), lambda j, i: (i, j)))
        args.append(add)
    return pl.pallas_call(
        body, name=name, grid=(N // tn, M // tm),
        in_specs=in_specs,
        out_specs=pl.BlockSpec((tm, tn), lambda j, i: (i, j)),
        out_shape=jax.ShapeDtypeStruct((M, N), out_dtype),
        compiler_params=_cparams(("parallel", "parallel")),
    )(*args)


def _dot(a, b, dims, precision=None):
    if precision is None:
        a, b = a.astype(MXU_DTYPE), b.astype(MXU_DTYPE)
    return lax.dot_general(a, b, (dims, ((), ())), precision=precision, preferred_element_type=F32)


_NN = ((1,), (0,))
_NT = ((1,), (1,))
_TN = ((0,), (0,))


def _rms(x, gain):
    return x * lax.rsqrt(jnp.mean(x * x, axis=-1, keepdims=True) + EPS) * gain


def _hg_block(st_t, q, fl, iv, g, lb, gain):
    row = lax.broadcasted_iota(jnp.int32, (HG_SUB, HG_SUB), 0)
    col = lax.broadcasted_iota(jnp.int32, (HG_SUB, HG_SUB), 1)
    tri = (row >= col).astype(F32)
    outs, states = [], []
    for h in range(HG_HEADS):
        sl = slice(h * HG_DIM, (h + 1) * HG_DIM)
        st = st_t[h * HG_DIM:(h + 1) * HG_DIM, :]
        lbh = lb[:, sl]
        fg = lbh + (1.0 - lbh) * jax.nn.sigmoid(fl[:, sl])
        lf, kk, qf, v = jnp.log(fg), 1.0 - fg, jax.nn.silu(q[:, sl]), iv[:, sl]
        parts = []
        for s in range(q.shape[0] // HG_SUB):
            r = slice(s * HG_SUB, (s + 1) * HG_SUB)
            b = _dot(tri, lf[r], _NN, HI)
            b_mid = jnp.sum(lf[r][:HG_SUB // 2], axis=0, keepdims=True)
            b_end = jnp.sum(lf[r], axis=0, keepdims=True)
            sc = _dot(qf[r] * jnp.exp(b - b_mid), kk[r] * jnp.exp(b_mid - b), _NT) * tri
            parts.append(_dot(sc, v[r], _NN) + _dot(qf[r] * jnp.exp(b), st, _NT))
            st = st * jnp.exp(b_end) + _dot(v[r], kk[r] * jnp.exp(b_end - b), _TN)
        o = jnp.concatenate(parts, axis=0)
        outs.append(_rms(o, gain[:, sl]) * jax.nn.silu(g[:, sl]))
        states.append(st)
    return jnp.concatenate(states, axis=0), jnp.concatenate(outs, axis=1)


def _hg_specs(proj, nb):
    return [pl.BlockSpec((HG_BLOCK, HG_WIDTH), lambda i, c=c, f=nb: (f(i), c)) for c in range(4)]


def hgrn2_fwd(proj, lb, gain):
    L = proj.shape[0]
    n = L // HG_BLOCK

    def body(q, fl, iv, g, lb_r, gain_r, o_ref, st_ref, st):
        @pl.when(pl.program_id(0) == 0)
        def _():
            st[...] = jnp.zeros(st.shape, F32)

        st_ref[0] = st[...]
        new, o = _hg_block(st[...], q[...], fl[...], iv[...], g[...], lb_r[...], gain_r[...])
        st[...] = new
        o_ref[...] = o.astype(o_ref.dtype)

    pspec = pl.BlockSpec((1, HG_WIDTH), lambda i: (0, 0))
    return pl.pallas_call(
        body, name="hgrn2_fwd", grid=(n,),
        in_specs=_hg_specs(proj, lambda i: i) + [pspec, pspec],
        out_specs=[pl.BlockSpec((HG_BLOCK, HG_WIDTH), lambda i: (i, 0)),
                   pl.BlockSpec((1, HG_WIDTH, HG_DIM), lambda i: (i, 0, 0))],
        out_shape=[jax.ShapeDtypeStruct((L, HG_WIDTH), MXU_DTYPE),
                   jax.ShapeDtypeStruct((n, HG_WIDTH, HG_DIM), F32)],
        scratch_shapes=[pltpu.VMEM((HG_WIDTH, HG_DIM), F32)],
        compiler_params=_cparams(("arbitrary",)),
    )(proj, proj, proj, proj, lb, gain)


def hgrn2_bwd(proj, lb, gain, states, do):
    L = proj.shape[0]
    n = L // HG_BLOCK

    def body(q, fl, iv, g, lb_r, gain_r, st_r, do_r, dproj, dlb, dgain, dst):
        @pl.when(pl.program_id(0) == 0)
        def _():
            dst[...] = jnp.zeros(dst.shape, F32)
            dlb[...] = jnp.zeros(dlb.shape, F32)
            dgain[...] = jnp.zeros(dgain.shape, F32)

        _, vjp = jax.vjp(_hg_block, st_r[0], q[...], fl[...], iv[...], g[...], lb_r[...], gain_r[...])
        d_st, dq, dfl, div, dg, d_lb, d_gain = vjp((dst[...], do_r[...].astype(F32)))
        dst[...] = d_st
        dproj[:, 0 * HG_WIDTH:1 * HG_WIDTH] = dq
        dproj[:, 1 * HG_WIDTH:2 * HG_WIDTH] = dfl
        dproj[:, 2 * HG_WIDTH:3 * HG_WIDTH] = div
        dproj[:, 3 * HG_WIDTH:4 * HG_WIDTH] = dg
        dlb[...] += d_lb
        dgain[...] += d_gain

    rev = lambda i: n - 1 - i
    pspec = pl.BlockSpec((1, HG_WIDTH), lambda i: (0, 0))
    return pl.pallas_call(
        body, name="hgrn2_bwd", grid=(n,),
        in_specs=_hg_specs(proj, rev) + [pspec, pspec,
                                         pl.BlockSpec((1, HG_WIDTH, HG_DIM), lambda i: (rev(i), 0, 0)),
                                         pl.BlockSpec((HG_BLOCK, HG_WIDTH), lambda i: (rev(i), 0))],
        out_specs=[pl.BlockSpec((HG_BLOCK, 4 * HG_WIDTH), lambda i: (rev(i), 0)), pspec, pspec],
        out_shape=[jax.ShapeDtypeStruct((L, 4 * HG_WIDTH), F32),
                   jax.ShapeDtypeStruct((1, HG_WIDTH), F32), jax.ShapeDtypeStruct((1, HG_WIDTH), F32)],
        scratch_shapes=[pltpu.VMEM((HG_WIDTH, HG_DIM), F32)],
        compiler_params=_cparams(("arbitrary",)),
    )(proj, proj, proj, proj, lb, gain, states, do)


def _rope_rms(x, gain_p, cos_p, sin_p):
    n = x * lax.rsqrt(jnp.sum(x * x, axis=-1, keepdims=True) * (1.0 / MLA_ROPE) + EPS) * gain_p
    r = lax.broadcasted_iota(jnp.int32, (128, 128), 0)
    c = lax.broadcasted_iota(jnp.int32, (128, 128), 1)
    swap = (r == (c + 64) % 128).astype(F32)
    return n * cos_p + _dot(n, swap, _NN, HI) * sin_p


MLA_IN = MLA_Q_RANK + MLA_KV_RANK + 128


def _mla_prep(x, cos_p, sin_p, q_a, w_uq, kv_a, w_ukv, qn_nope, qn_rope, kn_nope, kn_rope):
    c_q, c_kv, kpe = x[:, :MLA_Q_RANK], x[:, MLA_Q_RANK:MLA_Q_RANK + MLA_KV_RANK], x[:, MLA_Q_RANK + MLA_KV_RANK:]
    q = _dot(_rms(c_q, q_a), w_uq, _NN)
    kv = _dot(_rms(c_kv, kv_a), w_ukv, _NN)
    k_pe = _rope_rms(kpe, kn_rope, cos_p, sin_p)
    qs, ks = [], []
    for h in range(MLA_HEADS):
        qs.append(_rms(q[:, h * MLA_DK:h * MLA_DK + MLA_NOPE], qn_nope))
        qs.append(_rope_rms(q[:, h * MLA_DK + MLA_NOPE:(h + 1) * MLA_DK], qn_rope, cos_p, sin_p))
        ks.append(_rms(kv[:, h * MLA_NOPE:(h + 1) * MLA_NOPE], kn_nope))
        ks.append(k_pe)
    return jnp.concatenate(qs, axis=1), jnp.concatenate(ks, axis=1), kv[:, MLA_HEADS * MLA_NOPE:]


def _mla_prep_opds(proj, cos_p, sin_p, params, tm, grads):
    g = (lambda k: k) if grads else (lambda k: None)
    assert (4 * HG_WIDTH) % MLA_IN == 0
    return ([rows(proj, tm, g('blk'), col=4 * HG_WIDTH // MLA_IN, width=MLA_IN), rows(cos_p, tm), rows(sin_p, tm)]
            + [full(p, g('acc')) for p in params])


def mla_prep_fwd(proj, cos_p, sin_p, params, tm):
    L = proj.shape[0]
    W = MLA_HEADS * MLA_DK
    rb = lambda w: (tm, w)
    outs = [((L, W), MXU_DTYPE, rb(W), lambda i: (i, 0)), ((L, W), MXU_DTYPE, rb(W), lambda i: (i, 0)),
            ((L, MLA_HEADS * MLA_V), MXU_DTYPE, rb(MLA_HEADS * MLA_V), lambda i: (i, 0))]
    return blocked_fwd(_mla_prep, _mla_prep_opds(proj, cos_p, sin_p, params, tm, False), outs, L // tm, "mla_prep_fwd")


def mla_prep_bwd(proj, cos_p, sin_p, params, dq, dk, dv, tm):
    L = proj.shape[0]
    return blocked_bwd(_mla_prep, _mla_prep_opds(proj, cos_p, sin_p, params, tm, True),
                       [rows(dq, tm), rows(dk, tm), rows(dv, tm)], L // tm, "mla_prep_bwd")


def _scores(q, k, scale, shift=None):
    s = _dot(q, k, _NT) * scale
    if shift is None:
        return s
    row = lax.broadcasted_iota(jnp.int32, s.shape, 0)
    col = lax.broadcasted_iota(jnp.int32, s.shape, 1)
    return jnp.where(col <= row + shift, s, -jnp.inf)


ATTN_ROWS = 512
ATTN_WIDE = 2


def attn_fwd(q, k, v, scale, t):
    L = q.shape[0]
    tq = ATTN_WIDE * t

    def body(q_ref, k_ref, v_ref, o_ref, lse_ref):
        i = pl.program_id(1)
        qb = q_ref[...]

        def step(j, carry, shift=None):
            m, l, acc = carry
            kj = k_ref[pl.ds(pl.multiple_of(j * t, t), t), :]
            vj = v_ref[pl.ds(pl.multiple_of(j * t, t), t), :]
            s = _scores(qb, kj, scale, shift)
            m_new = jnp.maximum(m, jnp.max(s, axis=-1, keepdims=True))
            p = jnp.exp(s - m_new)
            alpha = jnp.exp(m - m_new)
            return m_new, alpha * l + jnp.sum(p, axis=-1, keepdims=True), alpha * acc + _dot(p, vj, _NN)

        carry = (jnp.full((tq, 1), -jnp.inf, F32), jnp.zeros((tq, 1), F32), jnp.zeros((tq, MLA_V), F32))
        carry = lax.fori_loop(0, ATTN_WIDE * i, step, carry)
        for d in range(ATTN_WIDE):
            carry = step(ATTN_WIDE * i + d, carry, -d * t)
        m, l, acc = carry
        o_ref[...] = acc / l
        lse_ref[...] = jnp.broadcast_to(m + jnp.log(l), lse_ref.shape)

    hspec = lambda rows_, w: pl.BlockSpec((rows_, w), lambda h, i: (0, h))
    bspec = lambda w: pl.BlockSpec((tq, w), lambda h, i: (i, h))
    return pl.pallas_call(
        body, name="attn_fwd", grid=(MLA_HEADS, L // tq),
        in_specs=[bspec(MLA_DK), hspec(L, MLA_DK), hspec(L, MLA_V)],
        out_specs=[bspec(MLA_V), bspec(MLA_V)],
        out_shape=[jax.ShapeDtypeStruct((L, MLA_HEADS * MLA_V), F32)] * 2,
        compiler_params=_cparams(("parallel", "parallel")),
    )(q, k, v)


def attn_bwd_dq(q, k, v, o, lse, do, scale, t):
    L = q.shape[0]
    tq = ATTN_WIDE * t

    def body(q_ref, k_ref, v_ref, o_ref, lse_ref, do_ref, dq_ref):
        i = pl.program_id(1)
        qb, dob = q_ref[...], do_ref[...]
        delta = jnp.sum(dob * o_ref[...], axis=-1, keepdims=True)
        lse_c = jnp.max(lse_ref[...], axis=-1, keepdims=True)

        def step(j, dq, shift=None):
            kj = k_ref[pl.ds(pl.multiple_of(j * t, t), t), :]
            vj = v_ref[pl.ds(pl.multiple_of(j * t, t), t), :]
            p = jnp.exp(_scores(qb, kj, scale, shift) - lse_c)
            ds = p * (_dot(dob, vj, _NT) - delta) * scale
            return dq + _dot(ds, kj, _NN)

        dq = lax.fori_loop(0, ATTN_WIDE * i, step, jnp.zeros((tq, MLA_DK), F32))
        for d in range(ATTN_WIDE):
            dq = step(ATTN_WIDE * i + d, dq, -d * t)
        dq_ref[...] = dq

    hspec = lambda w: pl.BlockSpec((L, w), lambda h, i: (0, h))
    bspec = lambda w: pl.BlockSpec((tq, w), lambda h, i: (i, h))
    return pl.pallas_call(
        body, name="attn_bwd_dq", grid=(MLA_HEADS, L // tq),
        in_specs=[bspec(MLA_DK), hspec(MLA_DK), hspec(MLA_V), bspec(MLA_V), bspec(MLA_V), bspec(MLA_V)],
        out_specs=bspec(MLA_DK),
        out_shape=jax.ShapeDtypeStruct((L, MLA_HEADS * MLA_DK), F32),
        compiler_params=_cparams(("parallel", "parallel")),
    )(q, k, v, o, lse, do)


def attn_bwd_dkv(q, k, v, o, lse, do, scale, t):
    L = q.shape[0]
    tk = ATTN_WIDE * t

    def body(q_ref, k_ref, v_ref, o_ref, lse_ref, do_ref, dk_ref, dv_ref):
        j = pl.program_id(1)
        kb, vb = k_ref[...], v_ref[...]

        def step(i, carry, shift=None):
            dk, dv = carry
            r = pl.ds(pl.multiple_of(i * t, t), t)
            qi, doi = q_ref[r, :], do_ref[r, :]
            delta = jnp.sum(doi * o_ref[r, :], axis=-1, keepdims=True)
            lse_c = jnp.max(lse_ref[r, :], axis=-1, keepdims=True)
            p = jnp.exp(_scores(qi, kb, scale, shift) - lse_c)
            ds = p * (_dot(doi, vb, _NT) - delta) * scale
            return dk + _dot(ds, qi, _TN), dv + _dot(p, doi, _TN)

        carry = (jnp.zeros((tk, MLA_DK), F32), jnp.zeros((tk, MLA_V), F32))
        for d in range(ATTN_WIDE):
            carry = step(ATTN_WIDE * j + d, carry, d * t)
        dk, dv = lax.fori_loop(ATTN_WIDE * (j + 1), L // t, step, carry)
        dk_ref[...] = dk
        dv_ref[...] = dv

    hspec = lambda w: pl.BlockSpec((L, w), lambda h, j: (0, h))
    bspec = lambda w: pl.BlockSpec((tk, w), lambda h, j: (j, h))
    return pl.pallas_call(
        body, name="attn_bwd_dkv", grid=(MLA_HEADS, L // tk),
        in_specs=[hspec(MLA_DK), bspec(MLA_DK), bspec(MLA_V), hspec(MLA_V), hspec(MLA_V), hspec(MLA_V)],
        out_specs=[bspec(MLA_DK), bspec(MLA_V)],
        out_shape=[jax.ShapeDtypeStruct((L, MLA_HEADS * MLA_DK), F32), jax.ShapeDtypeStruct((L, MLA_HEADS * MLA_V), F32)],
        compiler_params=_cparams(("parallel", "parallel")),
    )(q, k, v, o, lse, do)


S5_LANES = S5_GB * S5_STATE


def _cmul(ar, ai, br, bi):
    return ar * br - ai * bi, ar * bi + ai * br


def _a_powers(ar, ai, reverse):
    a2 = _cmul(ar, ai, ar, ai)
    a4 = _cmul(*a2, *a2)
    row = lax.broadcasted_iota(jnp.int32, (8, ar.shape[1]), 0)
    e = (8 - row) if reverse else (row + 1)
    tr, ti = jnp.ones((8, ar.shape[1]), F32), jnp.zeros((8, ar.shape[1]), F32)
    for bit, (pr, pi) in ((1, (ar, ai)), (2, a2), (4, a4), (8, _cmul(*a4, *a4))):
        nr, ni = _cmul(tr, ti, pr, pi)
        sel = (e & bit) != 0
        tr, ti = jnp.where(sel, nr, tr), jnp.where(sel, ni, ti)
    pows = []
    for d, (pr, pi) in zip((1, 2, 4), ((ar, ai), a2, a4)):
        keep = (row < 8 - d) if reverse else (row >= d)
        pows.append((jnp.where(keep, pr, 0.0), jnp.where(keep, pi, 0.0)))
    return pows, (tr, ti)


def _scan8(xr, xi, pows, table, cr, ci, reverse):
    for d, (pr, pi) in zip((1, 2, 4), pows):
        shift = 8 - d if reverse else d
        mr, mi = _cmul(pr, pi, pltpu.roll(xr, shift, 0), pltpu.roll(xi, shift, 0))
        xr, xi = xr + mr, xi + mi
    mr, mi = _cmul(table[0], table[1], cr, ci)
    return xr + mr, xi + mi


def _row_of(x, r):
    row = lax.broadcasted_iota(jnp.int32, x.shape, 0)
    return jnp.sum(jnp.where(row == r, x, 0.0), axis=0, keepdims=True)


def _s5_scan_fwd(h_re, h_im, ar, ai, L):
    pows, table = _a_powers(ar, ai, False)

    def step(i, carry):
        r = pl.ds(pl.multiple_of(i * 8, 8), 8)
        xr, xi = _scan8(h_re[r, :], h_im[r, :], pows, table, carry[0], carry[1], False)
        h_re[r, :] = xr
        h_im[r, :] = xi
        return xr[7:8, :], xi[7:8, :]

    z = jnp.zeros((1, ar.shape[1]), F32)
    lax.fori_loop(0, L // 8, step, (z, z))


def _s5_specs(L):
    return [pl.BlockSpec((L, 128), lambda g: (0, g)),
            pl.BlockSpec((1, 128, S5_LANES), lambda g: (g, 0, 0)), pl.BlockSpec((1, 128, S5_LANES), lambda g: (g, 0, 0)),
            pl.BlockSpec((1, 1, S5_LANES), lambda g: (g, 0, 0)), pl.BlockSpec((1, 1, S5_LANES), lambda g: (g, 0, 0)),
            pl.BlockSpec((1, S5_LANES, 128), lambda g: (g, 0, 0)), pl.BlockSpec((1, S5_LANES, 128), lambda g: (g, 0, 0))]


def s5_fwd(u, w_re, w_im, a_re, a_im, c_re, c_im):
    L, D = u.shape

    def body(u_ref, wr, wi, ar, ai, cr, ci, y_ref, h_re, h_im):
        ub = u_ref[...]
        h_re[...] = _dot(ub, wr[0], _NN)
        h_im[...] = _dot(ub, wi[0], _NN)
        _s5_scan_fwd(h_re, h_im, ar[0], ai[0], L)
        y_ref[...] = _dot(h_re[...], cr[0], _NN) - _dot(h_im[...], ci[0], _NN)

    return pl.pallas_call(
        body, name="s5_fwd", grid=(D // 128,),
        in_specs=_s5_specs(L), out_specs=pl.BlockSpec((L, 128), lambda g: (0, g)),
        out_shape=jax.ShapeDtypeStruct((L, D), F32),
        scratch_shapes=[pltpu.VMEM((L, S5_LANES), F32), pltpu.VMEM((L, S5_LANES), F32)],
        compiler_params=_cparams(("parallel",)),
    )(u, w_re, w_im, a_re, a_im, c_re, c_im)


def s5_bwd(u, w_re, w_im, a_re, a_im, c_re, c_im, dy, tc):
    L, D = u.shape
    nch = L // tc

    def body(u_ref, wr, wi, ar_ref, ai_ref, cr, ci, dy_ref, du_ref, dwr, dwi, dar, dai, dcr, dci, h_re, h_im, g_re, g_im):
        ar, ai = ar_ref[0], ai_ref[0]
        ub = u_ref[...]
        h_re[...] = _dot(ub, wr[0], _NN)
        h_im[...] = _dot(ub, wi[0], _NN)
        _s5_scan_fwd(h_re, h_im, ar, ai, L)
        dyb = dy_ref[...]
        dcr[0] = _dot(h_re[...], dyb, _TN)
        dci[0] = -_dot(h_im[...], dyb, _TN)
        pows, table = _a_powers(ar, -ai, True)
        dwr[0] = jnp.zeros((128, S5_LANES), F32)
        dwi[0] = jnp.zeros((128, S5_LANES), F32)
        z1 = jnp.zeros((1, S5_LANES), F32)
        z8 = jnp.zeros((8, S5_LANES), F32)

        def chunk(cc, carry):
            c0 = pl.multiple_of((nch - 1 - cc) * tc, tc)
            rows_c = pl.ds(c0, tc)
            dyc = dy_ref[rows_c, :]
            g_re[...] = _dot(dyc, cr[0], _NT)
            g_im[...] = -_dot(dyc, ci[0], _NT)

            def step(ii, cy):
                gr_c, gi_c, acc_r, acc_i = cy
                i8 = pl.multiple_of((tc // 8 - 1 - ii) * 8, 8)
                rl = pl.ds(i8, 8)
                xr, xi = _scan8(g_re[rl, :], g_im[rl, :], pows, table, gr_c, gi_c, True)
                g_re[rl, :] = xr
                g_im[rl, :] = xi
                t0 = c0 + i8
                hb_r, hb_i = h_re[pl.ds(t0, 8), :], h_im[pl.ds(t0, 8), :]
                tp = pl.multiple_of(jnp.maximum(t0 - 8, 0), 8)
                first = (t0 > 0).astype(F32)
                pr = h_re[pl.ds(tp, 8), :][7:8, :] * first
                pi = h_im[pl.ds(tp, 8), :][7:8, :] * first
                row = lax.broadcasted_iota(jnp.int32, xr.shape, 0)
                hp_r = jnp.where(row == 0, pr, pltpu.roll(hb_r, 1, 0))
                hp_i = jnp.where(row == 0, pi, pltpu.roll(hb_i, 1, 0))
                return (xr[0:1, :], xi[0:1, :],
                        acc_r + xr * hp_r + xi * hp_i, acc_i + xi * hp_r - xr * hp_i)

            cy = lax.fori_loop(0, tc // 8, step, carry)
            uc = u_ref[rows_c, :]
            gr, gi = g_re[...], g_im[...]
            du_ref[rows_c, :] = _dot(gr, wr[0], _NT) + _dot(gi, wi[0], _NT)
            dwr[0] += _dot(uc, gr, _TN)
            dwi[0] += _dot(uc, gi, _TN)
            return cy

        _, _, acc_r, acc_i = lax.fori_loop(0, nch, chunk, (z1, z1, z8, z8))
        dar[0] = jnp.sum(acc_r, axis=0, keepdims=True)
        dai[0] = jnp.sum(acc_i, axis=0, keepdims=True)

    specs = _s5_specs(L)
    return pl.pallas_call(
        body, name="s5_bwd", grid=(D // 128,),
        in_specs=specs + [pl.BlockSpec((L, 128), lambda g: (0, g))],
        out_specs=[pl.BlockSpec((L, 128), lambda g: (0, g))] + specs[1:],
        out_shape=[jax.ShapeDtypeStruct((L, D), F32)] + [jax.ShapeDtypeStruct(x.shape, F32)
                                                        for x in (w_re, w_im, a_re, a_im, c_re, c_im)],
        scratch_shapes=[pltpu.VMEM((L, S5_LANES), F32), pltpu.VMEM((L, S5_LANES), F32),
                        pltpu.VMEM((tc, S5_LANES), F32), pltpu.VMEM((tc, S5_LANES), F32)],
        compiler_params=_cparams(("parallel",)),
    )(u, w_re, w_im, a_re, a_im, c_re, c_im, dy)


def _s5_discretize(lr, li, ldt, br, bi):
    dt = jnp.exp(ldt)
    mag = jnp.exp(lr * dt)
    ar, ai = mag * jnp.cos(li * dt), mag * jnp.sin(li * dt)
    den = lr * lr + li * li
    zr = ((ar - 1.0) * lr + ai * li) / den
    zi = (ai * lr - (ar - 1.0) * li) / den
    p = lax.broadcasted_iota(jnp.int32, (S5_STATE, S5_STATE * S5_GROUP), 0)
    c = lax.broadcasted_iota(jnp.int32, (S5_STATE, S5_STATE * S5_GROUP), 1)
    rep = (c // S5_GROUP == p).astype(F32)
    zr, zi = _dot(zr, rep, _NN, HI), _dot(zi, rep, _NN, HI)
    return ar, ai, zr * br - zi * bi, zr * bi + zi * br


def _conv_shift(x, d):
    row = lax.broadcasted_iota(jnp.int32, x.shape, 0)
    return jnp.where(row >= d, pltpu.roll(x, d, 0), 0.0)


def _conv_unshift(x, d):
    n = x.shape[0]
    row = lax.broadcasted_iota(jnp.int32, x.shape, 0)
    return jnp.where(row < n - d, pltpu.roll(x, n - d, 0), 0.0)


@functools.partial(jax.custom_vjp, nondiff_argnums=(1,))
def _shift_rows(x, d):
    return _conv_shift(x, d)


_shift_rows.defvjp(lambda x, d: (_conv_shift(x, d), None), lambda d, _, g: (_conv_unshift(g, d),))


def _conv_gate(ug, uv, wg0, wg1, wg2, wv0, wv1, wv2, bg, bv):
    def conv(u, w0, w1, w2, b):
        return u * w2 + _shift_rows(u, 1) * w1 + _shift_rows(u, 2) * w0 + b
    return (jax.nn.silu(conv(ug, wg0, wg1, wg2, bg)) * conv(uv, wv0, wv1, wv2, bv),)


def _rms_fn(x, gain):
    return (_rms(x, gain),)


def _softmax_rows(s):
    e = jnp.exp(s - lax.stop_gradient(jnp.max(s, axis=-1, keepdims=True)))
    return e / jnp.sum(e, axis=-1, keepdims=True)


def _xa_core(qp, k, v, q_gain):
    dh = qp.shape[1] // XA_HEADS
    outs = []
    for h in range(XA_HEADS):
        sl = slice(h * dh, (h + 1) * dh)
        p = _softmax_rows(_dot(_rms(qp[:, sl], q_gain), k[:, sl], _NT) * (dh ** -0.5))
        outs.append(_dot(p, v[:, sl], _NN))
    return (jnp.concatenate(outs, axis=1),)


def _mem_kv(mem, mem_gain, wk, wv, k_gain):
    m = _rms(mem, mem_gain)
    kp = _dot(m, wk, _NN)
    dh = kp.shape[1] // XA_HEADS
    k = jnp.concatenate([_rms(kp[:, h * dh:(h + 1) * dh], k_gain) for h in range(XA_HEADS)], axis=1)
    return k, _dot(m, wv, _NN)


def _s5_post(y, u, d):
    return (jax.nn.gelu(y + d * u),)


def _glu(a, b):
    return (a * jax.nn.sigmoid(b),)


def _lb_first(logits):
    e = jnp.exp(logits - lax.stop_gradient(jnp.max(logits, axis=0, keepdims=True)))
    return (_row_of(e, 0) / jnp.sum(e, axis=0, keepdims=True),)


def _loss_fn(y, t):
    e = y - t
    part = 0.5 * jnp.sum(e * e) / y.shape[1]
    return e * (1.0 / y.shape[1]), jnp.full((8, 128), part / (8 * 128), F32)


def _out(shape, dtype, tm):
    return (shape, dtype, (tm, shape[1]), lambda i: (i, 0))


def rms_fwd(h, gain, tm, dtype):
    return blocked_fwd(_rms_fn, [rows(h, tm), full(gain)], [_out(h.shape, dtype, tm)], h.shape[0] // tm, "rms_fwd")[0]


def rms_bwd(h, gain, dy, tm, residual):
    return blocked_bwd(_rms_fn, [rows(h, tm, 'blk'), full(gain, 'acc')], [rows(dy, tm)], h.shape[0] // tm, "rms_bwd",
                       plus=rows(residual, tm))


def adamw(w, g, m, v, name):
    R = w.shape[0]
    tm = _tile(R, 256)
    assert g.shape == w.shape == m.shape == v.shape, (name, w.shape, g.shape)

    def body(w_ref, g_ref, m_ref, v_ref, d_ref, nm_ref, nv_ref):
        g_ = g_ref[...]
        m_ = ADAM_B1 * m_ref[...] + (1.0 - ADAM_B1) * g_
        v_ = ADAM_B2 * v_ref[...] + (1.0 - ADAM_B2) * jnp.square(g_)
        m_hat = m_ / (1.0 - ADAM_B1 ** ADAM_STEP)
        v_hat = v_ / (1.0 - ADAM_B2 ** ADAM_STEP)
        d_ref[...] = -ADAM_LR * (m_hat / (jnp.sqrt(v_hat) + ADAM_EPS) + ADAM_WD * w_ref[...])
        nm_ref[...] = m_
        nv_ref[...] = v_

    spec = pl.BlockSpec((tm, w.shape[1]), lambda i: (i, 0))
    return pl.pallas_call(
        body, name=name, grid=(R // tm,), in_specs=[spec] * 4, out_specs=[spec] * 3,
        out_shape=[jax.ShapeDtypeStruct(w.shape, F32)] * 3, compiler_params=_cparams(("parallel",)),
    )(w, g, m, v)


def add2(x, y, name, out_dtype=F32):
    shape = x.shape
    x, y = x.reshape(-1, shape[-1]), y.reshape(-1, shape[-1])
    R, C = x.shape
    tm = _tile(R, 256)

    def body(x_ref, y_ref, o_ref):
        o_ref[...] = (x_ref[...] + y_ref[...]).astype(o_ref.dtype)

    spec = pl.BlockSpec((tm, C), lambda i: (i, 0))
    return pl.pallas_call(
        body, name=name, grid=(R // tm,), in_specs=[spec, spec], out_specs=spec,
        out_shape=jax.ShapeDtypeStruct((R, C), out_dtype), compiler_params=_cparams(("parallel",)),
    )(x, y).reshape(shape)


def add_chips(own, got, name):
    n, R, C = got.shape
    tm = _tile(R, 256)

    def body(*refs):
        acc = refs[0][...].astype(F32)
        for r in refs[1:-1]:
            acc = acc + r[...].astype(F32)
        refs[-1][...] = acc

    return pl.pallas_call(
        body, name=name, grid=(R // tm,),
        in_specs=[pl.BlockSpec((tm, C), lambda i: (i, 0))] + [pl.BlockSpec((None, tm, C), lambda i, j=j: (j, i, 0))
                                                            for j in range(n)],
        out_specs=pl.BlockSpec((tm, C), lambda i: (i, 0)),
        out_shape=jax.ShapeDtypeStruct((R, C), F32), compiler_params=_cparams(("parallel",)),
    )(own, *([got] * n))


_HBM = pl.BlockSpec(memory_space=pltpu.HBM)
N_CHIPS = 4


def _my_place():
    return lax.axis_index("x"), lax.axis_index("y"), lax.axis_index("c")


def _window(ref, axis, start, size):
    idx = [slice(None)] * len(ref.shape)
    idx[axis] = pl.ds(start, size)
    return ref.at[tuple(idx)]


def _comm_call(body, name, xs, out_shapes, n_remote, n_local):
    return pl.pallas_call(
        body, name=name, in_specs=[_HBM] * len(xs), out_specs=[_HBM] * len(out_shapes), out_shape=out_shapes,
        scratch_shapes=[pltpu.SemaphoreType.DMA((n_remote,)), pltpu.SemaphoreType.DMA((n_remote,)),
                        pltpu.SemaphoreType.DMA((max(n_local, 1),))],
        compiler_params=pltpu.CompilerParams(has_side_effects=True),
    )(*xs)


def _run(copies):
    for cp in copies:
        cp.start()
    for cp in copies:
        cp.wait()


def _other_chips(mx, my):
    return [(mx ^ (j >> 1), my ^ (j & 1)) for j in (1, 2, 3)]


def chip_gather(xs, axes, name):
    n = len(xs)
    shapes, final = [], []
    for x, ax in zip(xs, axes):
        s = list(x.shape)
        if ax is None:
            shapes.append([N_CHIPS] + s)
            final.append(shapes[-1])
        elif ax < x.ndim - 1:
            shapes.append(s[:ax] + [N_CHIPS] + s[ax:])
            final.append(s[:ax] + [N_CHIPS * s[ax]] + s[ax + 1:])
        else:
            assert s[ax] % 128 == 0, (name, s)
            shapes.append(s[:ax] + [N_CHIPS * s[ax]])
            final.append(shapes[-1])

    def body(*refs):
        x_refs, o_refs = refs[:n], refs[n:2 * n]
        send_sems, recv_sems, local_sems = refs[2 * n:]
        mx, my, mc = _my_place()
        q = 2 * mx + my
        copies = []
        for i, (x_ref, o_ref, ax) in enumerate(zip(x_refs, o_refs, axes)):
            if ax is None or ax < len(x_ref.shape) - 1:
                dst = o_ref.at[(slice(None),) * (ax or 0) + (q,)]
            else:
                dst = _window(o_ref, ax, q * x_ref.shape[ax], x_ref.shape[ax])
            copies.append(pltpu.make_async_copy(x_ref, dst, local_sems.at[i]))
            for j, (tx, ty) in enumerate(_other_chips(mx, my)):
                copies.append(pltpu.make_async_remote_copy(
                    src_ref=x_ref, dst_ref=dst, send_sem=send_sems.at[3 * i + j], recv_sem=recv_sems.at[3 * i + j],
                    device_id=(tx, ty, mc), device_id_type=MESH))
        _run(copies)

    out_shapes = [jax.ShapeDtypeStruct(tuple(s), x.dtype) for s, x in zip(shapes, xs)]
    return [o.reshape(f) for o, f in zip(_comm_call(body, name, xs, out_shapes, 3 * n, n), final)]


def gather_two_level(xs, name):
    n = len(xs)
    shapes = [jax.ShapeDtypeStruct((2, N_CHIPS) + x.shape[1:], x.dtype) for x in xs]

    def body(*refs):
        x_refs, o_refs = refs[:n], refs[n:2 * n]
        send_sems, recv_sems, local_sems = refs[2 * n:]
        mx, my, mc = _my_place()
        q = 2 * mx + my
        first, local, second = [], [], []
        for i, (x_ref, o_ref) in enumerate(zip(x_refs, o_refs)):
            local.append(pltpu.make_async_copy(x_ref.at[mc], o_ref.at[mc, q], local_sems.at[i]))
            for j, (tx, ty) in enumerate(_other_chips(mx, my)):
                first.append(pltpu.make_async_remote_copy(
                    src_ref=x_ref.at[mc], dst_ref=o_ref.at[mc, q], send_sem=send_sems.at[4 * i + j],
                    recv_sem=recv_sems.at[4 * i + j], device_id=(tx, ty, mc), device_id_type=MESH))
            second.append(pltpu.make_async_remote_copy(
                src_ref=o_ref.at[mc], dst_ref=o_ref.at[mc], send_sem=send_sems.at[4 * i + 3],
                recv_sem=recv_sems.at[4 * i + 3], device_id=(mx, my, 1 - mc), device_id_type=MESH))
        for cp in local + first:
            cp.start()
        for cp in local:
            cp.wait()
        for cp in first:
            cp.wait_recv()
        _run(second)
        for cp in first:
            cp.wait_send()

    return _comm_call(body, name, xs, shapes, 4 * n, n)


def gather_two_level_sequencer(xs, name, collective_id):
    n = len(xs)
    hbm = pltpu.MemorySpace.HBM
    x_refs = [jax.new_ref(x, memory_space=hbm) for x in xs]
    o_refs = [jax.empty_ref(jax.ShapeDtypeStruct((2, N_CHIPS) + x.shape[1:], x.dtype), memory_space=hbm) for x in xs]

    @pl.kernel(mesh=plsc.ScalarSubcoreMesh(axis_name="sequencer", num_cores=1), name=name,
               scratch_types=(pltpu.SemaphoreType.DMA((4 * n,)), pltpu.SemaphoreType.DMA((4 * n,)),
                              pltpu.SemaphoreType.DMA((n,))),
               compiler_params=pltpu.CompilerParams(collective_id=collective_id))
    def launch(send_sems, recv_sems, local_sems):
        mx, my, mc = _my_place()
        peers = [(tx, ty, mc) for tx, ty in _other_chips(mx, my)] + [(mx, my, 1 - mc)]
        barrier = pltpu.get_barrier_semaphore()
        for peer in peers:
            pl.semaphore_signal(barrier, inc=1, device_id=peer, device_id_type=MESH)
        pl.semaphore_wait(barrier, len(peers))
        q = 2 * mx + my
        first, local, second = [], [], []
        for i, (x_ref, o_ref) in enumerate(zip(x_refs, o_refs)):
            local.append(pltpu.make_async_copy(x_ref.at[mc], o_ref.at[mc, q], local_sems.at[i]))
            for j, peer in enumerate(peers[:3]):
                first.append(pltpu.make_async_remote_copy(
                    src_ref=x_ref.at[mc], dst_ref=o_ref.at[mc, q], send_sem=send_sems.at[4 * i + j],
                    recv_sem=recv_sems.at[4 * i + j], device_id=peer, device_id_type=MESH))
            second.append(pltpu.make_async_remote_copy(
                src_ref=o_ref.at[mc], dst_ref=o_ref.at[mc], send_sem=send_sems.at[4 * i + 3],
                recv_sem=recv_sems.at[4 * i + 3], device_id=peers[3], device_id_type=MESH))
        for cp in local + first:
            cp.start()
        for cp in local:
            cp.wait()
        for cp in first:
            cp.wait_recv()
        _run(second)
        for cp in first:
            cp.wait_send()

    launch()
    return [o[...] for o in o_refs]


def pair_swap(xs, name, halves):
    n = len(xs)
    shapes = [jax.ShapeDtypeStruct(x.shape[1:] if halves else x.shape, x.dtype) for x in xs]

    def body(*refs):
        x_refs, o_refs = refs[:n], refs[n:2 * n]
        send_sems, recv_sems, _ = refs[2 * n:]
        mx, my, mc = _my_place()
        _run([pltpu.make_async_remote_copy(
            src_ref=x_ref.at[1 - mc] if halves else x_ref, dst_ref=o_ref, send_sem=send_sems.at[i],
            recv_sem=recv_sems.at[i], device_id=(mx, my, 1 - mc), device_id_type=MESH)
            for i, (x_ref, o_ref) in enumerate(zip(x_refs, o_refs))])

    return _comm_call(body, name, xs, shapes, n, 0)


def chip_all_to_all(xs, name):
    n = len(xs)
    shapes = [jax.ShapeDtypeStruct((N_CHIPS - 1,) + x.shape[1:], x.dtype) for x in xs]

    def body(*refs):
        x_refs, o_refs = refs[:n], refs[n:2 * n]
        send_sems, recv_sems, _ = refs[2 * n:]
        mx, my, mc = _my_place()
        copies = []
        for i, (x_ref, o_ref) in enumerate(zip(x_refs, o_refs)):
            for j, (tx, ty) in enumerate(_other_chips(mx, my)):
                copies.append(pltpu.make_async_remote_copy(
                    src_ref=x_ref.at[2 * tx + ty], dst_ref=o_ref.at[j], send_sem=send_sems.at[3 * i + j],
                    recv_sem=recv_sems.at[3 * i + j], device_id=(tx, ty, mc), device_id_type=MESH))
        _run(copies)

    return _comm_call(body, name, xs, shapes, 3 * n, 0)


WEIGHTS = ['norm_mix', 'norm_xa', 'norm_mem', 'norm_ffn', 'xa_wq', 'xa_wk', 'xa_wv', 'xa_wo', 'xa_q_norm', 'xa_k_norm',
           'ffn_w_up', 'ffn_conv_w', 'ffn_conv_b', 'ffn_w_down', 'hg_lb_logits', 'mix_w_in', 'hg_out_norm',
           'mla_q_a_norm', 'mla_w_uq', 'mla_kv_a_norm', 'mla_w_ukv', 'mla_qn_nope', 'mla_qn_rope', 'mla_kn_nope',
           'mla_kn_rope', 'mix_w_out', 's5_lam_re', 's5_lam_im', 's5_log_dt', 's5_b_re', 's5_b_im', 's5_c_re',
           's5_c_im', 's5_d', 's5_w_glu_a', 's5_w_glu_b']
INPUTS = ['x', 'mem', 'positions'] + WEIGHTS + ['loss_target'] + ['m_' + n for n in WEIGHTS] + ['v_' + n for n in WEIGHTS]
SHARD_AXIS = {'xa_wq': 1, 'xa_wk': 1, 'xa_wv': 1, 'xa_wo': 1, 'ffn_w_up': 2, 'ffn_conv_w': 2, 'ffn_w_down': 1,
              'mix_w_in': 2, 'mla_w_uq': 2, 'mla_w_ukv': 2, 'mix_w_out': 1, 's5_d': 1, 's5_w_glu_a': 1, 's5_w_glu_b': 1}
BIG = ['xa_wq', 'xa_wk', 'xa_wv', 'xa_wo', 'ffn_w_up', 'ffn_w_down', 'mix_w_in', 'mix_w_out', 's5_w_glu_a', 's5_w_glu_b']
FIRST_NEEDED = ('mix_w_in', 'mix_w_out')
SMALL_SHARDED = [n for n in WEIGHTS if n in SHARD_AXIS and n not in BIG]
REPLICATED = [n for n in WEIGHTS if n not in SHARD_AXIS]
SMALL = SMALL_SHARDED + REPLICATED
PACK_W = 1024
ROW_MULT = 16
W_IN_SHARD = IN_WIDTH // N_CHIPS
W_IN_SHARD_PAD = 640


def _pack(flats, mult=ROW_MULT):
    flat = jnp.concatenate([f.reshape(-1) for f in flats])
    unit = mult * PACK_W
    n = -(-flat.shape[0] // unit) * unit
    return jnp.pad(flat, (0, n - flat.shape[0])).reshape(n // PACK_W, PACK_W)


def _unpack(packed, shapes):
    flat, out, o = packed.reshape(-1), [], 0
    for s in shapes:
        n = math.prod(s)
        out.append(flat[o:o + n].reshape(s))
        o += n
    return out


def _rope_pad(w):
    z = jnp.zeros(w.shape[:-1] + (MLA_ROPE // 2,), w.dtype)
    return jnp.concatenate([w[..., :MLA_ROPE // 2], z, w[..., MLA_ROPE // 2:], z], axis=-1)


def _rope_unpad(g):
    return jnp.concatenate([g[..., :MLA_ROPE // 2], g[..., 64:64 + MLA_ROPE // 2]], axis=-1)


def _blockdiag_in(bb):
    nb = bb.shape[0] // S5_GB
    t = bb.reshape(nb, S5_GB, S5_STATE, S5_GROUP).transpose(0, 1, 3, 2)
    return jnp.einsum('bgmp,gh->bgmhp', t, jnp.eye(S5_GB, dtype=bb.dtype)).reshape(nb, S5_GB * S5_GROUP, S5_LANES)


def _blockdiag_in_t(dw):
    nb = dw.shape[0]
    t = jnp.einsum('bgmhp,gh->bgmp', dw.reshape(nb, S5_GB, S5_GROUP, S5_GB, S5_STATE), jnp.eye(S5_GB, dtype=dw.dtype))
    return t.transpose(0, 1, 3, 2).reshape(nb * S5_GB, S5_STATE, S5_GROUP)


def _blockdiag_out(c):
    nb = c.shape[0] // S5_GB
    t = c.reshape(nb, S5_GB, S5_GROUP, S5_STATE).transpose(0, 1, 3, 2)
    return jnp.einsum('bgpm,gh->bgphm', t, jnp.eye(S5_GB, dtype=c.dtype)).reshape(nb, S5_LANES, S5_GB * S5_GROUP)


def _blockdiag_out_t(dc):
    nb = dc.shape[0]
    t = jnp.einsum('bgphm,gh->bgpm', dc.reshape(nb, S5_GB, S5_STATE, S5_GB, S5_GROUP), jnp.eye(S5_GB, dtype=dc.dtype))
    return t.transpose(0, 1, 3, 2).reshape(nb * S5_GB, S5_GROUP, S5_STATE)


def _gather_weights(P):
    def halves(x):
        return x if x.shape[0] == 2 else x.reshape(2, x.shape[1] // 2, x.shape[2])

    now = [n for n in BIG if n in FIRST_NEEDED]
    later = [n for n in BIG if n not in FIRST_NEEDED]
    xs = [halves(P[n].astype(BF16)) for n in now] + [halves(_pack([P[n] for n in SMALL_SHARDED], 2 * ROW_MULT)[None])]
    got = gather_two_level(xs, "gather_weights")
    got, xs_later = lax.optimization_barrier((got, [halves(P[n].astype(BF16)) for n in later]))
    got_later = gather_two_level_sequencer(xs_later, "gather_weights_later", 1)
    full_w = {}
    for n, g in list(zip(now, got[:-1])) + list(zip(later, got_later)):
        two_layers, by_rows = P[n].shape[0] == 2, SHARD_AXIS[n] == 1
        if two_layers and by_rows:
            full_w[n] = g.reshape(2, N_CHIPS * g.shape[2], g.shape[3])
        elif two_layers:
            full_w[n] = g.transpose(0, 2, 1, 3).reshape(2, g.shape[2], N_CHIPS * g.shape[3])
        elif by_rows:
            full_w[n] = g.transpose(1, 0, 2, 3).reshape(1, 2 * N_CHIPS * g.shape[2], g.shape[3])
        else:
            full_w[n] = g.transpose(0, 2, 1, 3).reshape(1, 2 * g.shape[2], N_CHIPS * g.shape[3])
    small = got[-1].transpose(1, 0, 2, 3).reshape(N_CHIPS, -1, PACK_W)
    per_chip = [_unpack(small[q], [P[n].shape for n in SMALL_SHARDED]) for q in range(N_CHIPS)]
    for i, n in enumerate(SMALL_SHARDED):
        full_w[n] = jnp.concatenate([per_chip[q][i] for q in range(N_CHIPS)], axis=SHARD_AXIS[n])
    return full_w


def _halves_first(x):
    return x.reshape(x.shape[0], 2, x.shape[1] // 2, x.shape[2]).transpose(1, 0, 2, 3)


def _reduce_and_update(GB, GS, P):
    mx, my, mc = _my_place()
    q = 2 * mx + my
    small = _pack([GS[n] for n in SMALL], 2 * N_CHIPS * ROW_MULT)
    xs = [GB[n] for n in BIG] + [_halves_first(small.reshape(N_CHIPS, -1, PACK_W))]
    theirs = pair_swap(xs, "grads_pair_swap", True)
    names = BIG + ['small']
    pair = [add2(lax.dynamic_index_in_dim(x, mc, 0, False), t, "grads_pair_sum_" + n, F32 if n == 'small' else BF16)
            for x, t, n in zip(xs, theirs, names)]
    got = chip_all_to_all(pair, "grads_chip_all_to_all")
    summed = [add_chips(lax.dynamic_index_in_dim(p, q, 0, False), g, "grads_chip_sum_" + n)
              for p, g, n in zip(pair, got, names)]
    other = pair_swap(summed, "grads_pair_join", False)
    joined = [lax.cond(mc == 0, lambda a, b: jnp.concatenate([a, b], axis=0), lambda a, b: jnp.concatenate([b, a], axis=0),
                       s, o) for s, o in zip(summed, other)]
    small_sum = chip_gather([joined[-1]], [0], "grads_small_gather")[0]
    g_small = dict(zip(SMALL, _unpack(small_sum, [GS[n].shape for n in SMALL])))
    for n in SMALL_SHARDED:
        s = P[n].shape[SHARD_AXIS[n]]
        g_small[n] = lax.dynamic_slice_in_dim(g_small[n], q * s, s, axis=SHARD_AXIS[n])

    grad, delta, new_m, new_v = {}, {}, {}, {}
    for n, g in zip(BIG, joined[:-1]):
        shape = P[n].shape
        if n == 'mix_w_in':
            g = g[:, :W_IN_SHARD]
        two_d = (g.shape[0], shape[-1])
        d, m_, v_ = adamw(P[n].reshape(two_d), g, P['m_' + n].reshape(two_d), P['v_' + n].reshape(two_d), "adamw_" + n)
        grad[n], delta[n], new_m[n], new_v[n] = (t.reshape(shape) for t in (g, d, m_, v_))
    packed = lambda prefix: _pack([P[prefix + n] for n in SMALL])
    d, m_, v_ = adamw(packed(''), _pack([g_small[n] for n in SMALL]), packed('m_'), packed('v_'), "adamw_small")
    shapes = [P[n].shape for n in SMALL]
    grad.update(g_small)
    for out, pk in ((delta, d), (new_m, m_), (new_v, v_)):
        out.update(zip(SMALL, _unpack(pk, shapes)))
    return grad, delta, new_m, new_v


def _row(v):
    return v.reshape(1, -1)


def _xattn_fwd(h, mem, W, lyr, tm):
    g_xa, g_mem = _row(W['norm_xa'][lyr]), _row(W['norm_mem'][lyr])
    g_q, g_k = _row(W['xa_q_norm'][lyr]), _row(W['xa_k_norm'][lyr])
    wq, wk, wv, wo = (W[n][lyr] for n in ('xa_wq', 'xa_wk', 'xa_wv', 'xa_wo'))
    L, D = h.shape
    M = mem.shape[0]
    hx = rms_fwd(h, g_xa, tm, MXU_DTYPE)
    qp = matmul(hx, wq, name="xa_q")
    kv_opds = [full(mem), full(g_mem), full(wk), full(wv), full(g_k)]
    k, v = blocked_fwd(_mem_kv, kv_opds, [((M, D), F32, (M, D), lambda i: (0, 0))] * 2, 1, "xa_mem_kv")
    o = blocked_fwd(_xa_core, [rows(qp, tm), full(k), full(v), full(g_q)], [_out((L, D), MXU_DTYPE, tm)], L // tm,
                    "xa_core")[0]
    out = matmul(o, wo, add=h, name="xa_o")
    return out, (h, hx, qp, k, v, o)


def _xattn_bwd(dout, saved, mem, W, lyr, tm, bufs):
    h, hx, qp, k, v, o = saved
    g_xa, g_mem = _row(W['norm_xa'][lyr]), _row(W['norm_mem'][lyr])
    g_q, g_k = _row(W['xa_q_norm'][lyr]), _row(W['xa_k_norm'][lyr])
    wq, wk, wv, wo = (W[n][lyr] for n in ('xa_wq', 'xa_wk', 'xa_wv', 'xa_wo'))
    L = h.shape[0]
    do = matmul(dout, wo, "nt", name="xa_do")
    bufs['xa_wo'] = matmul(o, dout, "tn", name="xa_dwo", into=(bufs['xa_wo'], lyr))
    dqp, dk, dv, d_gq = blocked_bwd(_xa_core, [rows(qp, tm, 'blk'), full(k, 'acc'), full(v, 'acc'), full(g_q, 'acc')],
                                    [rows(do, tm)], L // tm, "xa_core_bwd")
    bufs['xa_wq'] = matmul(hx, dqp, "tn", name="xa_dwq", into=(bufs['xa_wq'], lyr))
    dhx = matmul(dqp, wq, "nt", name="xa_dhx")
    dh, d_gxa = rms_bwd(h, g_xa, dhx, tm, dout)
    d_gmem, d_wk, d_wv, d_gk = blocked_bwd(
        _mem_kv, [full(mem), full(g_mem, 'acc'), full(wk, 'acc'), full(wv, 'acc'), full(g_k, 'acc')],
        [full(dk), full(dv)], 1, "xa_mem_kv_bwd")
    by_chip = lambda g: g.reshape(N_CHIPS, g.shape[0] // N_CHIPS, g.shape[1])
    grads = {'norm_xa': d_gxa, 'norm_mem': d_gmem, 'xa_q_norm': d_gq, 'xa_k_norm': d_gk,
             'xa_wk': by_chip(d_wk), 'xa_wv': by_chip(d_wv)}
    return dh, grads


def _conv_params(W, lyr):
    cw, cb = W['ffn_conv_w'][lyr], W['ffn_conv_b'][lyr]
    F = cw.shape[1] // 2
    return [cw[0:1, :F], cw[1:2, :F], cw[2:3, :F], cw[0:1, F:], cw[1:2, F:], cw[2:3, F:], _row(cb[:F]), _row(cb[F:])]


def _ffn_fwd(h, W, lyr, tm):
    L, D = h.shape
    w_up, w_down = W['ffn_w_up'][lyr], W['ffn_w_down'][lyr]
    F = w_down.shape[0]
    hf = rms_fwd(h, _row(W['norm_ffn'][lyr]), tm, MXU_DTYPE)
    ug = matmul(hf, w_up[:, :F], name="ffn_up_gate")
    uv = matmul(hf, w_up[:, F:], name="ffn_up_value")
    opds = [cols(ug, 128), cols(uv, 128)] + [cols(p, 128) for p in _conv_params(W, lyr)]
    a = blocked_fwd(_conv_gate, opds, [((L, F), MXU_DTYPE, (L, 128), lambda j: (0, j))], F // 128, "ffn_conv_gate")[0]
    out = matmul(a, w_down, add=h, name="ffn_down")
    return out, (h, hf, ug, uv, a)


def _ffn_bwd(dout, saved, W, lyr, tm, bufs):
    h, hf, ug, uv, a = saved
    w_up, w_down = W['ffn_w_up'][lyr], W['ffn_w_down'][lyr]
    F = w_down.shape[0]
    da = matmul(dout, w_down, "nt", name="ffn_da")
    bufs['ffn_w_down'] = matmul(a, dout, "tn", name="ffn_dwdown", into=(bufs['ffn_w_down'], lyr))
    opds = [cols(ug, 128, 'blk'), cols(uv, 128, 'blk')] + [cols(p, 128, 'blk') for p in _conv_params(W, lyr)]
    gs = blocked_bwd(_conv_gate, opds, [cols(da, 128)], F // 128, "ffn_conv_gate_bwd")
    dug, duv = gs[0], gs[1]
    d_cw = jnp.concatenate([jnp.concatenate(gs[2:5], axis=0), jnp.concatenate(gs[5:8], axis=0)], axis=1)
    d_cb = jnp.concatenate([gs[8], gs[9]], axis=1)[0]
    half = N_CHIPS // 2
    bufs['ffn_w_up'] = matmul(hf, dug, "tn", name="ffn_dwup_gate", into=(bufs['ffn_w_up'], lyr, 0), col_blocks=half)
    bufs['ffn_w_up'] = matmul(hf, duv, "tn", name="ffn_dwup_value", into=(bufs['ffn_w_up'], lyr, half), col_blocks=half)
    dhf = matmul(dug, w_up[:, :F], "nt", name="ffn_dhf_gate")
    dhf = matmul(duv, w_up[:, F:], "nt", add=dhf, name="ffn_dhf_value")
    dh, d_g = rms_bwd(h, _row(W['norm_ffn'][lyr]), dhf, tm, dout)
    return dh, {'norm_ffn': d_g, 'ffn_conv_w': d_cw, 'ffn_conv_b': d_cb}


def _mla_params(W):
    w_uq = W['mla_w_uq'][0].reshape(MLA_Q_RANK, MLA_HEADS, MLA_QK)
    w_uq = jnp.concatenate([w_uq[..., :MLA_NOPE], _rope_pad(w_uq[..., MLA_NOPE:])], axis=-1)
    w_ukv = W['mla_w_ukv'][0].reshape(MLA_KV_RANK, MLA_HEADS, MLA_NOPE + MLA_V)
    w_ukv = jnp.concatenate([w_ukv[..., :MLA_NOPE].reshape(MLA_KV_RANK, -1), w_ukv[..., MLA_NOPE:].reshape(MLA_KV_RANK, -1)],
                            axis=1)
    return [_row(W['mla_q_a_norm'][0]), w_uq.reshape(MLA_Q_RANK, MLA_HEADS * MLA_DK), _row(W['mla_kv_a_norm'][0]), w_ukv,
            _row(W['mla_qn_nope'][0]), _row(_rope_pad(W['mla_qn_rope'][0])), _row(W['mla_kn_nope'][0]),
            _row(_rope_pad(W['mla_kn_rope'][0]))]


def _w_in_padded(W):
    w = W['mix_w_in'][0]
    return jnp.concatenate([w[:, :IN_WIDTH - MLA_ROPE], _rope_pad(w[:, IN_WIDTH - MLA_ROPE:])], axis=1)


def _mixer0_fwd(h, W, cos_p, sin_p, tm):
    L = h.shape[0]
    t = min(ATTN_ROWS, L // ATTN_WIDE)
    hn = rms_fwd(h, _row(W['norm_mix'][0]), tm, MXU_DTYPE)
    proj = matmul(hn, _w_in_padded(W), name="mix_in")
    logits = W['hg_lb_logits']
    lb = blocked_fwd(_lb_first, [full(logits)], [((1, HG_WIDTH), F32, (1, HG_WIDTH), lambda i: (0, 0))], 1, "hg_lb")[0]
    gain = _row(W['hg_out_norm'][0])
    o_hg, states = hgrn2_fwd(proj, lb, gain)
    mp = _mla_params(W)
    q, k, v = mla_prep_fwd(proj, cos_p, sin_p, mp, tm)
    scale = MLA_QK ** -0.5
    o_mla, lse = attn_fwd(q, k, v, scale, t)
    w_out = W['mix_w_out'][0]
    out = matmul(o_hg, w_out[:HG_WIDTH], add=h, name="mix_out_hg")
    out = matmul(o_mla, w_out[HG_WIDTH:], add=out, name="mix_out_mla")
    return out, (h, hn, proj, lb, o_hg, states, q, k, v, o_mla, lse)


def _mixer0_bwd(dout, saved, W, cos_p, sin_p, tm):
    h, hn, proj, lb, o_hg, states, q, k, v, o_mla, lse = saved
    L = h.shape[0]
    t = min(ATTN_ROWS, L // ATTN_WIDE)
    scale = MLA_QK ** -0.5
    w_out = W['mix_w_out'][0]
    gain = _row(W['hg_out_norm'][0])
    do_hg = matmul(dout, w_out[:HG_WIDTH], "nt", name="mix_do_hg")
    do_mla = matmul(dout, w_out[HG_WIDTH:], "nt", name="mix_do_mla")
    d_wout = jnp.concatenate([matmul(o_hg, dout, "tn", name="mix_dwout_hg"), matmul(o_mla, dout, "tn", name="mix_dwout_mla")],
                             axis=0)
    dq = attn_bwd_dq(q, k, v, o_mla, lse, do_mla, scale, t)
    dk, dv = attn_bwd_dkv(q, k, v, o_mla, lse, do_mla, scale, t)
    mp = _mla_params(W)
    d_mla, d_qa, d_wuq, d_kva, d_wukv, d_qnn, d_qnr, d_knn, d_knr = mla_prep_bwd(proj, cos_p, sin_p, mp, dq, dk, dv, tm)
    d_hg, d_lb, d_gain = hgrn2_bwd(proj, lb, gain, states, do_hg)
    w_in, n_hg = _w_in_padded(W), 4 * HG_WIDTH
    d_win = jnp.concatenate([matmul(hn, d_hg, "tn", name="mix_dwin_hg"), matmul(hn, d_mla, "tn", name="mix_dwin_mla")], axis=1)
    dhn = matmul(d_hg, w_in[:, :n_hg], "nt", name="mix_dhn_hg")
    dhn = matmul(d_mla, w_in[:, n_hg:], "nt", add=dhn, name="mix_dhn_mla")
    dh, d_g = rms_bwd(h, _row(W['norm_mix'][0]), dhn, tm, dout)
    logits = W['hg_lb_logits']
    d_logits = blocked_bwd(_lb_first, [full(logits, 'acc')], [full(d_lb)], 1, "hg_lb_bwd")[0]
    d_wuq = d_wuq.reshape(MLA_Q_RANK, MLA_HEADS, MLA_DK)
    d_wuq = jnp.concatenate([d_wuq[..., :MLA_NOPE], _rope_unpad(d_wuq[..., MLA_NOPE:])], axis=-1)
    hw = MLA_HEADS * MLA_NOPE
    d_wukv = jnp.concatenate([d_wukv[:, :hw].reshape(MLA_KV_RANK, MLA_HEADS, MLA_NOPE),
                              d_wukv[:, hw:].reshape(MLA_KV_RANK, MLA_HEADS, MLA_V)], axis=-1)
    d_win = jnp.concatenate([d_win[:, :IN_WIDTH - MLA_ROPE], _rope_unpad(d_win[:, IN_WIDTH - MLA_ROPE:])], axis=1)
    d_win = d_win.reshape(d_win.shape[0], N_CHIPS, W_IN_SHARD).transpose(1, 0, 2)
    d_win = jnp.pad(d_win, ((0, 0), (0, 0), (0, W_IN_SHARD_PAD - W_IN_SHARD)))
    d_wout = d_wout.reshape(N_CHIPS, d_wout.shape[0] // N_CHIPS, d_wout.shape[1])
    grads = {'norm_mix': d_g, 'hg_lb_logits': d_logits, 'mix_w_in': d_win, 'hg_out_norm': d_gain,
             'mla_q_a_norm': d_qa, 'mla_w_uq': d_wuq.reshape(1, MLA_Q_RANK, -1), 'mla_kv_a_norm': d_kva,
             'mla_w_ukv': d_wukv.reshape(1, MLA_KV_RANK, -1), 'mla_qn_nope': d_qnn, 'mla_qn_rope': _rope_unpad(d_qnr),
             'mla_kn_nope': d_knn, 'mla_kn_rope': _rope_unpad(d_knr), 'mix_w_out': d_wout}
    return dh,grads


def _s5_inputs(W):
    G = W['s5_lam_re'].shape[1]
    return [W['s5_lam_re'][0], W['s5_lam_im'][0], W['s5_log_dt'][0].reshape(G, 1),
            W['s5_b_re'][0].reshape(G, -1), W['s5_b_im'][0].reshape(G, -1)]


def _mixer1_fwd(h, W, tm):
    L, D = h.shape
    u = rms_fwd(h, _row(W['norm_mix'][1]), tm, F32)
    di = _s5_inputs(W)
    G = di[0].shape[0]
    sq, wide = ((G, S5_STATE), F32, (G, S5_STATE), lambda i: (0, 0)), ((G, S5_STATE * S5_GROUP), F32, (G, S5_STATE * S5_GROUP), lambda i: (0, 0))
    ar, ai, bbr, bbi = blocked_fwd(_s5_discretize, [full(a) for a in di], [sq, sq, wide, wide], 1, "s5_discretize")
    nb = G // S5_GB
    core = (_blockdiag_in(bbr.reshape(G, S5_STATE, S5_GROUP)), _blockdiag_in(bbi.reshape(G, S5_STATE, S5_GROUP)),
            ar.reshape(nb, 1, S5_LANES), ai.reshape(nb, 1, S5_LANES),
            _blockdiag_out(W['s5_c_re'][0]), _blockdiag_out(W['s5_c_im'][0]))
    y = s5_fwd(u, *core)
    d = W['s5_d']
    y2 = blocked_fwd(_s5_post, [rows(y, tm), rows(u, tm), full(d)], [_out((L, D), MXU_DTYPE, tm)], L // tm, "s5_post")[0]
    w_ab = jnp.concatenate([W['s5_w_glu_a'][0], W['s5_w_glu_b'][0]], axis=1)
    ab = matmul(y2, w_ab, name="s5_glu_in")
    mix = blocked_fwd(_glu, [rows(ab, tm, col=0, width=D), rows(ab, tm, col=1, width=D)], [_out((L, D), F32, tm)], L // tm,
                      "s5_glu")[0]
    return h + mix, (h, u, core, y, y2, ab)


def _mixer1_bwd(dout, saved, W, tm):
    h, u, core, y, y2, ab = saved
    L, D = h.shape
    da, db = blocked_bwd(_glu, [rows(ab, tm, 'blk', col=0, width=D), rows(ab, tm, 'blk', col=1, width=D)], [rows(dout, tm)],
                         L // tm, "s5_glu_bwd")
    w_a, w_b = W['s5_w_glu_a'][0], W['s5_w_glu_b'][0]
    dy2 = matmul(da, w_a, "nt", name="s5_dy2_a")
    dy2 = matmul(db, w_b, "nt", add=dy2, name="s5_dy2_b")
    d_wa = matmul(y2, da, "tn", name="s5_dwa")
    d_wb = matmul(y2, db, "tn", name="s5_dwb")
    d = W['s5_d']
    dy, du_skip, d_d = blocked_bwd(_s5_post, [rows(y, tm, 'blk'), rows(u, tm, 'blk'), full(d, 'acc')], [rows(dy2, tm)], L // tm,
                                   "s5_post_bwd")
    du, dwr, dwi, dar, dai, dcr, dci = s5_bwd(u, *core, dy, min(256, L))
    di = _s5_inputs(W)
    G = di[0].shape[0]
    cts = [dar.reshape(G, S5_STATE), dai.reshape(G, S5_STATE), _blockdiag_in_t(dwr).reshape(G, -1), _blockdiag_in_t(dwi).reshape(G, -1)]
    d_lr, d_li, d_ldt, d_br, d_bi = blocked_bwd(_s5_discretize, [full(a, 'acc') for a in di], [full(c) for c in cts], 1,
                                                "s5_discretize_bwd")
    dh, d_g = rms_bwd(h, _row(W['norm_mix'][1]), du + du_skip, tm, dout)
    bshape = W['s5_b_re'].shape
    grads = {'norm_mix': d_g, 's5_lam_re': d_lr[None], 's5_lam_im': d_li[None], 's5_log_dt': d_ldt.reshape(1, G),
             's5_b_re': d_br.reshape(bshape), 's5_b_im': d_bi.reshape(bshape), 's5_c_re': _blockdiag_out_t(dcr)[None],
             's5_c_im': _blockdiag_out_t(dci)[None], 's5_d': d_d, 's5_w_glu_a': d_wa.reshape(N_CHIPS, -1, D), 's5_w_glu_b': d_wb.reshape(N_CHIPS, -1, D)}
    return dh,grads


def kernel(x, mem, positions, norm_mix, norm_xa, norm_mem, norm_ffn, xa_wq, xa_wk, xa_wv, xa_wo, xa_q_norm, xa_k_norm, ffn_w_up, ffn_conv_w, ffn_conv_b, ffn_w_down, hg_lb_logits, mix_w_in, hg_out_norm, mla_q_a_norm, mla_w_uq, mla_kv_a_norm, mla_w_ukv, mla_qn_nope, mla_qn_rope, mla_kn_nope, mla_kn_rope, mix_w_out, s5_lam_re, s5_lam_im, s5_log_dt, s5_b_re, s5_b_im, s5_c_re, s5_c_im, s5_d, s5_w_glu_a, s5_w_glu_b, loss_target, m_norm_mix, m_norm_xa, m_norm_mem, m_norm_ffn, m_xa_wq, m_xa_wk, m_xa_wv, m_xa_wo, m_xa_q_norm, m_xa_k_norm, m_ffn_w_up, m_ffn_conv_w, m_ffn_conv_b, m_ffn_w_down, m_hg_lb_logits, m_mix_w_in, m_hg_out_norm, m_mla_q_a_norm, m_mla_w_uq, m_mla_kv_a_norm, m_mla_w_ukv, m_mla_qn_nope, m_mla_qn_rope, m_mla_kn_nope, m_mla_kn_rope, m_mix_w_out, m_s5_lam_re, m_s5_lam_im, m_s5_log_dt, m_s5_b_re, m_s5_b_im, m_s5_c_re, m_s5_c_im, m_s5_d, m_s5_w_glu_a, m_s5_w_glu_b, v_norm_mix, v_norm_xa, v_norm_mem, v_norm_ffn, v_xa_wq, v_xa_wk, v_xa_wv, v_xa_wo, v_xa_q_norm, v_xa_k_norm, v_ffn_w_up, v_ffn_conv_w, v_ffn_conv_b, v_ffn_w_down, v_hg_lb_logits, v_mix_w_in, v_hg_out_norm, v_mla_q_a_norm, v_mla_w_uq, v_mla_kv_a_norm, v_mla_w_ukv, v_mla_qn_nope, v_mla_qn_rope, v_mla_kn_nope, v_mla_kn_rope, v_mix_w_out, v_s5_lam_re, v_s5_lam_im, v_s5_log_dt, v_s5_b_re, v_s5_b_im, v_s5_c_re, v_s5_c_im, v_s5_d, v_s5_w_glu_a, v_s5_w_glu_b):
    P = dict(locals())
    assert sorted(P) == sorted(INPUTS) and norm_mix.shape[0] == 2 and mix_w_in.shape[0] == 1
    x, mem, target = P['x'][0], P['mem'][0], P['loss_target'][0]
    L, D = x.shape
    tm = min(256, L)

    W = {n: P[n] for n in REPLICATED}
    W.update(_gather_weights(P))

    inv_freq = 1.0 / (ROPE_BASE ** (jnp.arange(0, MLA_ROPE, 2, dtype=F32) / MLA_ROPE))
    ang = P['positions'][0].astype(F32)[:, None] * inv_freq
    cos, sin, z = jnp.cos(ang), jnp.sin(ang), jnp.zeros_like(ang)
    cos_p = jnp.concatenate([cos, z, cos, z], axis=1)
    sin_p = jnp.concatenate([-sin, z, sin, z], axis=1)

    h, s_mix0 = _mixer0_fwd(x, W, cos_p, sin_p, tm)
    h, s_xa0 = _xattn_fwd(h, mem, W, 0, tm)
    h, s_ffn0 = _ffn_fwd(h, W, 0, tm)
    h, s_mix1 = _mixer1_fwd(h, W, tm)
    h, s_xa1 = _xattn_fwd(h, mem, W, 1, tm)
    h, s_ffn1 = _ffn_fwd(h, W, 1, tm)
    n = L // tm
    dh, parts = blocked_fwd(_loss_fn, [rows(h, tm), rows(target, tm)],
                            [_out((L, D), F32, tm), ((n * 8, 128), F32, (8, 128), lambda i: (i, 0))], n, "loss")
    loss = lax.psum(jnp.sum(parts), ("x", "y", "c"))

    layered = {}

    def collect(g, lyr):
        for k_, v_ in g.items():
            layered.setdefault(k_, {})[lyr] = v_

    bufs = {n_: lax.empty(W[n_].shape, F32) for n_ in ('xa_wq', 'xa_wo', 'ffn_w_down')}
    d_ff = W['ffn_w_up'].shape[2] // N_CHIPS
    bufs['ffn_w_up'] = lax.empty((2, N_CHIPS, D, d_ff), F32)
    dh, g = _ffn_bwd(dh, s_ffn1, W, 1, tm, bufs)
    collect(g, 1)
    dh, g = _xattn_bwd(dh, s_xa1, mem, W, 1, tm, bufs)
    collect(g, 1)
    dh, g = _mixer1_bwd(dh, s_mix1, W, tm)
    collect(g, 1)
    dh, g = _ffn_bwd(dh, s_ffn0, W, 0, tm, bufs)
    collect(g, 0)
    dh, g = _xattn_bwd(dh, s_xa0, mem, W, 0, tm, bufs)
    collect(g, 0)
    dx, g = _mixer0_bwd(dh, s_mix0, W, cos_p, sin_p, tm)
    collect(g, 0)

    GB, GS = {}, {}
    for name in WEIGHTS:
        if name in bufs:
            b_ = bufs[name]
            GB[name] = b_ if b_.ndim == 4 else b_.reshape(2, N_CHIPS, b_.shape[1] // N_CHIPS, b_.shape[2])
            continue
        by_layer = [layered[name][lyr] for lyr in sorted(layered[name])]
        if name in BIG:
            GB[name] = _halves_first(by_layer[0]) if len(by_layer) == 1 else jnp.stack(by_layer)
        else:
            full_shape = W[name].shape
            GS[name] = (by_layer[0].reshape(full_shape) if len(by_layer) == 1
                        else jnp.stack([g_.reshape(full_shape[1:]) for g_ in by_layer]))

    outs = _reduce_and_update(GB, GS, P)
    return (loss, dx[None], *[d[n_] for d in outs for n_ in WEIGHTS])
```

```python
import functools
import math

import jax
import jax.numpy as jnp
import numpy as np
from jax import lax
from jax.experimental import pallas as pl
from jax.experimental.pallas import tpu as pltpu
from jax.experimental.pallas import tpu_sc as plsc

F32 = jnp.float32
BF16 = jnp.bfloat16
MXU_DTYPE = BF16
HI = lax.Precision.HIGHEST
V7X_VMEM_LIMIT_BYTES = 56 * 1024 * 1024
EPS = 1e-6
MESH = pl.DeviceIdType.MESH

HG_HEADS, HG_DIM = 4, 128
HG_WIDTH = HG_HEADS * HG_DIM
HG_SUB = 32
HG_BLOCK = 64
MLA_HEADS, MLA_Q_RANK, MLA_KV_RANK = 4, 256, 128
MLA_NOPE, MLA_ROPE, MLA_V = 128, 64, 128
MLA_QK = MLA_NOPE + MLA_ROPE
MLA_DK = 256
ROPE_BASE = 10000.0
IN_WIDTH = 4 * HG_WIDTH + MLA_Q_RANK + MLA_KV_RANK + MLA_ROPE
IN_PAD = 4 * HG_WIDTH + MLA_Q_RANK + MLA_KV_RANK + 128
S5_GROUP, S5_STATE = 16, 64
S5_GB = 8
DT_MIN, DT_MAX = 1e-3, 1e-1
XA_HEADS = 4
CONV_W = 3
ADAM_LR, ADAM_B1, ADAM_B2, ADAM_EPS, ADAM_WD, ADAM_STEP = 0.001, 0.9, 0.999, 1e-08, 0.01, 10


def _cparams(sem):
    return pltpu.CompilerParams(dimension_semantics=sem, vmem_limit_bytes=V7X_VMEM_LIMIT_BYTES)


class Opd:
    def __init__(self, arr, block, imap, grad=None, gshape=None, gimap=None):
        self.arr, self.block, self.imap, self.grad = arr, block, imap, grad
        self.gshape = arr.shape if gshape is None else gshape
        self.gimap = imap if gimap is None else gimap

    def spec(self):
        return pl.BlockSpec(self.block, self.imap)

    def gspec(self):
        return pl.BlockSpec(self.block, self.gimap)


def rows(arr, tm, grad=None, col=0, width=None):
    width = arr.shape[1] if width is None else width
    return Opd(arr, (tm, width), lambda i, c=col: (i, c), grad, (arr.shape[0], width), lambda i: (i, 0))


def cols(arr, tn, grad=None):
    return Opd(arr, (arr.shape[0], tn), lambda j: (0, j), grad)


def full(arr, grad=None):
    return Opd(arr, arr.shape, lambda i: (0, 0), grad)


def _load(ref):
    v = ref[...]
    return v.astype(F32) if jnp.issubdtype(v.dtype, jnp.floating) else v


def blocked_fwd(f, opds, outs, n, name):
    n_in = len(opds)

    def body(*refs):
        ys = f(*[_load(r) for r in refs[:n_in]])
        for r, y in zip(refs[n_in:], ys):
            r[...] = y.astype(r.dtype)

    res = pl.pallas_call(
        body, name=name, grid=(n,),
        in_specs=[o.spec() for o in opds],
        out_specs=[pl.BlockSpec(b, m) for (_, _, b, m) in outs],
        out_shape=[jax.ShapeDtypeStruct(s, d) for (s, d, _, _) in outs],
        compiler_params=_cparams(("parallel",)),
    )(*[o.arr for o in opds])
    return res


def blocked_bwd(f, opds, dys, n, name, plus=None):
    n_in, n_dy = len(opds), len(dys)
    diff = [i for i, o in enumerate(opds) if o.grad]
    extra = [] if plus is None else [plus]

    def body(*refs):
        vals = [_load(r) for r in refs[:n_in]]

        def fd(*dv):
            allv = list(vals)
            for i, v in zip(diff, dv):
                allv[i] = v
            return tuple(f(*allv))

        ys, vjp = jax.vjp(fd, *[vals[i] for i in diff])
        cts = tuple(_load(r).astype(y.dtype) for r, y in zip(refs[n_in:n_in + n_dy], ys))
        gs = list(vjp(cts))
        if extra:
            gs[0] = gs[0] + _load(refs[n_in + n_dy])
        for r, g, i in zip(refs[n_in + n_dy + len(extra):], gs, diff):
            if opds[i].grad == 'acc':
                @pl.when(pl.program_id(0) == 0)
                def _(r=r):
                    r[...] = jnp.zeros(r.shape, r.dtype)
                r[...] += g.astype(r.dtype)
            else:
                r[...] = g.astype(r.dtype)

    any_acc = any(opds[i].grad == 'acc' for i in diff)
    res = pl.pallas_call(
        body, name=name, grid=(n,),
        in_specs=[o.spec() for o in opds + dys + extra],
        out_specs=[opds[i].gspec() for i in diff],
        out_shape=[jax.ShapeDtypeStruct(opds[i].gshape, F32) for i in diff],
        compiler_params=_cparams(("arbitrary" if any_acc else "parallel",)),
    )(*[o.arr for o in opds + dys + extra])
    return res


def _tile(dim, want):
    for t in range(want - want % 16, 0, -16):
        if dim % t == 0:
            return t
    assert dim <= want, (dim, want)
    return dim


MATMUL_VMEM_BUDGET = 40 * 1024 * 1024
MATMUL_ROWS = 512


def _widest(N, fits):
    for t in range(N - N % 128, 0, -128):
        if N % t == 0 and fits(t):
            return t
    return N


def matmul(a, b, mode="nn", out_dtype=F32, add=None, name="matmul", into=None, col_blocks=None):
    sa, sb, so = a.dtype.itemsize, b.dtype.itemsize, jnp.dtype(out_dtype).itemsize
    has_add = add is not None
    if mode == "tn":
        (K, M), (K2, N) = a.shape, b.shape
        assert K == K2 and not has_add and out_dtype == F32, (a.shape, b.shape)
        tk = _tile(K, MATMUL_ROWS)
        tn = _widest(N, lambda t: 2 * (tk * M * sa + tk * t * sb + M * t * 4) <= MATMUL_VMEM_BUDGET)
        extra, alias = [], {}
        if into is not None:
            buf, lead = into[0], tuple(into[1:])
            if col_blocks is not None:
                assert N % col_blocks == 0 and (N // col_blocks) % 128 == 0 and tn >= N // col_blocks, (N, col_blocks, tn)
                tn = N // col_blocks
                assert buf.shape[len(lead):] == (M, tn), (buf.shape, lead, M, tn)
                out_spec = pl.BlockSpec((None,) * len(lead) + (M, tn), lambda j, k: lead[:-1] + (lead[-1] + j, 0, 0))
            else:
                assert buf.shape[len(lead):] == (M, N), (buf.shape, lead, M, N)
                out_spec = pl.BlockSpec((None,) * len(lead) + (M, tn), lambda j, k: lead + (0, j))
            out_shape = jax.ShapeDtypeStruct(buf.shape, F32)
            extra, alias = [buf], {2: 0}
        else:
            assert col_blocks is None
            out_spec = pl.BlockSpec((M, tn), lambda j, k: (0, j))
            out_shape = jax.ShapeDtypeStruct((M, N), F32)

        def body(a_ref, b_ref, *rest):
            o_ref = rest[-1]
            r = lax.dot_general(a_ref[...].astype(MXU_DTYPE), b_ref[...].astype(MXU_DTYPE), ((_TN), ((), ())),
                                preferred_element_type=F32)

            @pl.when(pl.program_id(1) == 0)
            def _():
                o_ref[...] = r

            @pl.when(pl.program_id(1) > 0)
            def _():
                o_ref[...] += r

        return pl.pallas_call(
            body, name=name, grid=(N // tn, K // tk),
            in_specs=[pl.BlockSpec((tk, M), lambda j, k: (k, 0)), pl.BlockSpec((tk, tn), lambda j, k: (k, j))]
            + [pl.BlockSpec(memory_space=pl.ANY)] * len(extra),
            out_specs=out_spec, out_shape=out_shape, input_output_aliases=alias,
            compiler_params=_cparams(("parallel", "arbitrary")),
        )(a, b, *extra)

    (M, K) = a.shape
    N = b.shape[1] if mode == "nn" else b.shape[0]
    assert K == (b.shape[0] if mode == "nn" else b.shape[1]), (a.shape, b.shape, mode)
    tm = _tile(M, MATMUL_ROWS)
    tn = _widest(N, lambda t: 2 * (tm * K * sa + K * t * sb + tm * t * (so + 4 * has_add)) <= MATMUL_VMEM_BUDGET)
    dims = ((_NN if mode == "nn" else _NT), ((), ()))

    def body(*refs):
        r = lax.dot_general(refs[0][...].astype(MXU_DTYPE), refs[1][...].astype(MXU_DTYPE), dims, preferred_element_type=F32)
        if has_add:
            r = r + refs[2][...].astype(F32)
        refs[-1][...] = r.astype(refs[-1].dtype)

    b_spec = pl.BlockSpec((K, tn), lambda j, i: (0, j)) if mode == "nn" else pl.BlockSpec((tn, K), lambda j, i: (j, 0))
    in_specs = [pl.BlockSpec((tm, K), lambda j, i: (i, 0)), b_spec]
    args = [a, b]
    if has_add:
        in_specs.append(pl.BlockSpec((tm, tn), lambda j, i: (i, j)))
        args.append(add)
    return pl.pallas_call(
        body, name=name, grid=(N // tn, M // tm),
        in_specs=in_specs,
        out_specs=pl.BlockSpec((tm, tn), lambda j, i: (i, j)),
        out_shape=jax.ShapeDtypeStruct((M, N), out_dtype),
        compiler_params=_cparams(("parallel", "parallel")),
    )(*args)


def _dot(a, b, dims, precision=None):
    if precision is None:
        a, b = a.astype(MXU_DTYPE), b.astype(MXU_DTYPE)
    return lax.dot_general(a, b, (dims, ((), ())), precision=precision, preferred_element_type=F32)


_NN = ((1,), (0,))
_NT = ((1,), (1,))
_TN = ((0,), (0,))


def _rms(x, gain):
    return x * lax.rsqrt(jnp.mean(x * x, axis=-1, keepdims=True) + EPS) * gain


def _hg_block(st_t, q, fl, iv, g, lb, gain):
    row = lax.broadcasted_iota(jnp.int32, (HG_SUB, HG_SUB), 0)
    col = lax.broadcasted_iota(jnp.int32, (HG_SUB, HG_SUB), 1)
    tri = (row >= col).astype(F32)
    outs, states = [], []
    for h in range(HG_HEADS):
        sl = slice(h * HG_DIM, (h + 1) * HG_DIM)
        st = st_t[h * HG_DIM:(h + 1) * HG_DIM, :]
        lbh = lb[:, sl]
        fg = lbh + (1.0 - lbh) * jax.nn.sigmoid(fl[:, sl])
        lf, kk, qf, v = jnp.log(fg), 1.0 - fg, jax.nn.silu(q[:, sl]), iv[:, sl]
        parts = []
        for s in range(q.shape[0] // HG_SUB):
            r = slice(s * HG_SUB, (s + 1) * HG_SUB)
            b = _dot(tri, lf[r], _NN, HI)
            b_mid = jnp.sum(lf[r][:HG_SUB // 2], axis=0, keepdims=True)
            b_end = jnp.sum(lf[r], axis=0, keepdims=True)
            sc = _dot(qf[r] * jnp.exp(b - b_mid), kk[r] * jnp.exp(b_mid - b), _NT) * tri
            parts.append(_dot(sc, v[r], _NN) + _dot(qf[r] * jnp.exp(b), st, _NT))
            st = st * jnp.exp(b_end) + _dot(v[r], kk[r] * jnp.exp(b_end - b), _TN)
        o = jnp.concatenate(parts, axis=0)
        outs.append(_rms(o, gain[:, sl]) * jax.nn.silu(g[:, sl]))
        states.append(st)
    return jnp.concatenate(states, axis=0), jnp.concatenate(outs, axis=1)


def _hg_specs(proj, nb):
    return [pl.BlockSpec((HG_BLOCK, HG_WIDTH), lambda i, c=c, f=nb: (f(i), c)) for c in range(4)]


def hgrn2_fwd(proj, lb, gain):
    L = proj.shape[0]
    n = L // HG_BLOCK

    def body(q, fl, iv, g, lb_r, gain_r, o_ref, st_ref, st):
        @pl.when(pl.program_id(0) == 0)
        def _():
            st[...] = jnp.zeros(st.shape, F32)

        st_ref[0] = st[...]
        new, o = _hg_block(st[...], q[...], fl[...], iv[...], g[...], lb_r[...], gain_r[...])
        st[...] = new
        o_ref[...] = o.astype(o_ref.dtype)

    pspec = pl.BlockSpec((1, HG_WIDTH), lambda i: (0, 0))
    return pl.pallas_call(
        body, name="hgrn2_fwd", grid=(n,),
        in_specs=_hg_specs(proj, lambda i: i) + [pspec, pspec],
        out_specs=[pl.BlockSpec((HG_BLOCK, HG_WIDTH), lambda i: (i, 0)),
                   pl.BlockSpec((1, HG_WIDTH, HG_DIM), lambda i: (i, 0, 0))],
        out_shape=[jax.ShapeDtypeStruct((L, HG_WIDTH), MXU_DTYPE),
                   jax.ShapeDtypeStruct((n, HG_WIDTH, HG_DIM), F32)],
        scratch_shapes=[pltpu.VMEM((HG_WIDTH, HG_DIM), F32)],
        compiler_params=_cparams(("arbitrary",)),
    )(proj, proj, proj, proj, lb, gain)


def hgrn2_bwd(proj, lb, gain, states, do):
    L = proj.shape[0]
    n = L // HG_BLOCK

    def body(q, fl, iv, g, lb_r, gain_r, st_r, do_r, dproj, dlb, dgain, dst):
        @pl.when(pl.program_id(0) == 0)
        def _():
            dst[...] = jnp.zeros(dst.shape, F32)
            dlb[...] = jnp.zeros(dlb.shape, F32)
            dgain[...] = jnp.zeros(dgain.shape, F32)

        _, vjp = jax.vjp(_hg_block, st_r[0], q[...], fl[...], iv[...], g[...], lb_r[...], gain_r[...])
        d_st, dq, dfl, div, dg, d_lb, d_gain = vjp((dst[...], do_r[...].astype(F32)))
        dst[...] = d_st
        dproj[:, 0 * HG_WIDTH:1 * HG_WIDTH] = dq
        dproj[:, 1 * HG_WIDTH:2 * HG_WIDTH] = dfl
        dproj[:, 2 * HG_WIDTH:3 * HG_WIDTH] = div
        dproj[:, 3 * HG_WIDTH:4 * HG_WIDTH] = dg
        dlb[...] += d_lb
        dgain[...] += d_gain

    rev = lambda i: n - 1 - i
    pspec = pl.BlockSpec((1, HG_WIDTH), lambda i: (0, 0))
    return pl.pallas_call(
        body, name="hgrn2_bwd", grid=(n,),
        in_specs=_hg_specs(proj, rev) + [pspec, pspec,
                                         pl.BlockSpec((1, HG_WIDTH, HG_DIM), lambda i: (rev(i), 0, 0)),
                                         pl.BlockSpec((HG_BLOCK, HG_WIDTH), lambda i: (rev(i), 0))],
        out_specs=[pl.BlockSpec((HG_BLOCK, 4 * HG_WIDTH), lambda i: (rev(i), 0)), pspec, pspec],
        out_shape=[jax.ShapeDtypeStruct((L, 4 * HG_WIDTH), F32),
                   jax.ShapeDtypeStruct((1, HG_WIDTH), F32), jax.ShapeDtypeStruct((1, HG_WIDTH), F32)],
        scratch_shapes=[pltpu.VMEM((HG_WIDTH, HG_DIM), F32)],
        compiler_params=_cparams(("arbitrary",)),
    )(proj, proj, proj, proj, lb, gain, states, do)


def _rope_rms(x, gain_p, cos_p, sin_p):
    n = x * lax.rsqrt(jnp.sum(x * x, axis=-1, keepdims=True) * (1.0 / MLA_ROPE) + EPS) * gain_p
    r = lax.broadcasted_iota(jnp.int32, (128, 128), 0)
    c = lax.broadcasted_iota(jnp.int32, (128, 128), 1)
    swap = (r == (c + 64) % 128).astype(F32)
    return n * cos_p + _dot(n, swap, _NN, HI) * sin_p


MLA_IN = MLA_Q_RANK + MLA_KV_RANK + 128


def _mla_prep(x, cos_p, sin_p, q_a, w_uq, kv_a, w_ukv, qn_nope, qn_rope, kn_nope, kn_rope):
    c_q, c_kv, kpe = x[:, :MLA_Q_RANK], x[:, MLA_Q_RANK:MLA_Q_RANK + MLA_KV_RANK], x[:, MLA_Q_RANK + MLA_KV_RANK:]
    q = _dot(_rms(c_q, q_a), w_uq, _NN)
    kv = _dot(_rms(c_kv, kv_a), w_ukv, _NN)
    k_pe = _rope_rms(kpe, kn_rope, cos_p, sin_p)
    qs, ks = [], []
    for h in range(MLA_HEADS):
        qs.append(_rms(q[:, h * MLA_DK:h * MLA_DK + MLA_NOPE], qn_nope))
        qs.append(_rope_rms(q[:, h * MLA_DK + MLA_NOPE:(h + 1) * MLA_DK], qn_rope, cos_p, sin_p))
        ks.append(_rms(kv[:, h * MLA_NOPE:(h + 1) * MLA_NOPE], kn_nope))
        ks.append(k_pe)
    return jnp.concatenate(qs, axis=1), jnp.concatenate(ks, axis=1), kv[:, MLA_HEADS * MLA_NOPE:]


def _mla_prep_opds(proj, cos_p, sin_p, params, tm, grads):
    g = (lambda k: k) if grads else (lambda k: None)
    assert (4 * HG_WIDTH) % MLA_IN == 0
    return ([rows(proj, tm, g('blk'), col=4 * HG_WIDTH // MLA_IN, width=MLA_IN), rows(cos_p, tm), rows(sin_p, tm)]
            + [full(p, g('acc')) for p in params])


def mla_prep_fwd(proj, cos_p, sin_p, params, tm):
    L = proj.shape[0]
    W = MLA_HEADS * MLA_DK
    rb = lambda w: (tm, w)
    outs = [((L, W), MXU_DTYPE, rb(W), lambda i: (i, 0)), ((L, W), MXU_DTYPE, rb(W), lambda i: (i, 0)),
            ((L, MLA_HEADS * MLA_V), MXU_DTYPE, rb(MLA_HEADS * MLA_V), lambda i: (i, 0))]
    return blocked_fwd(_mla_prep, _mla_prep_opds(proj, cos_p, sin_p, params, tm, False), outs, L // tm, "mla_prep_fwd")


def mla_prep_bwd(proj, cos_p, sin_p, params, dq, dk, dv, tm):
    L = proj.shape[0]
    return blocked_bwd(_mla_prep, _mla_prep_opds(proj, cos_p, sin_p, params, tm, True),
                       [rows(dq, tm), rows(dk, tm), rows(dv, tm)], L // tm, "mla_prep_bwd")


def _scores(q, k, scale, shift=None):
    s = _dot(q, k, _NT) * scale
    if shift is None:
        return s
    row = lax.broadcasted_iota(jnp.int32, s.shape, 0)
    col = lax.broadcasted_iota(jnp.int32, s.shape, 1)
    return jnp.where(col <= row + shift, s, -jnp.inf)


ATTN_ROWS = 512
ATTN_WIDE = 2


def attn_fwd(q, k, v, scale, t):
    L = q.shape[0]
    tq = ATTN_WIDE * t

    def body(q_ref, k_ref, v_ref, o_ref, lse_ref):
        i = pl.program_id(1)
        qb = q_ref[...]

        def step(j, carry, shift=None):
            m, l, acc = carry
            kj = k_ref[pl.ds(pl.multiple_of(j * t, t), t), :]
            vj = v_ref[pl.ds(pl.multiple_of(j * t, t), t), :]
            s = _scores(qb, kj, scale, shift)
            m_new = jnp.maximum(m, jnp.max(s, axis=-1, keepdims=True))
            p = jnp.exp(s - m_new)
            alpha = jnp.exp(m - m_new)
            return m_new, alpha * l + jnp.sum(p, axis=-1, keepdims=True), alpha * acc + _dot(p, vj, _NN)

        carry = (jnp.full((tq, 1), -jnp.inf, F32), jnp.zeros((tq, 1), F32), jnp.zeros((tq, MLA_V), F32))
        carry = lax.fori_loop(0, ATTN_WIDE * i, step, carry)
        for d in range(ATTN_WIDE):
            carry = step(ATTN_WIDE * i + d, carry, -d * t)
        m, l, acc = carry
        o_ref[...] = acc / l
        lse_ref[...] = jnp.broadcast_to(m + jnp.log(l), lse_ref.shape)

    hspec = lambda rows_, w: pl.BlockSpec((rows_, w), lambda h, i: (0, h))
    bspec = lambda w: pl.BlockSpec((tq, w), lambda h, i: (i, h))
    return pl.pallas_call(
        body, name="attn_fwd", grid=(MLA_HEADS, L // tq),
        in_specs=[bspec(MLA_DK), hspec(L, MLA_DK), hspec(L, MLA_V)],
        out_specs=[bspec(MLA_V), bspec(MLA_V)],
        out_shape=[jax.ShapeDtypeStruct((L, MLA_HEADS * MLA_V), F32)] * 2,
        compiler_params=_cparams(("parallel", "parallel")),
    )(q, k, v)


def attn_bwd_dq(q, k, v, o, lse, do, scale, t):
    L = q.shape[0]
    tq = ATTN_WIDE * t

    def body(q_ref, k_ref, v_ref, o_ref, lse_ref, do_ref, dq_ref):
        i = pl.program_id(1)
        qb, dob = q_ref[...], do_ref[...]
        delta = jnp.sum(dob * o_ref[...], axis=-1, keepdims=True)
        lse_c = jnp.max(lse_ref[...], axis=-1, keepdims=True)

        def step(j, dq, shift=None):
            kj = k_ref[pl.ds(pl.multiple_of(j * t, t), t), :]
            vj = v_ref[pl.ds(pl.multiple_of(j * t, t), t), :]
            p = jnp.exp(_scores(qb, kj, scale, shift) - lse_c)
            ds = p * (_dot(dob, vj, _NT) - delta) * scale
            return dq + _dot(ds, kj, _NN)

        dq = lax.fori_loop(0, ATTN_WIDE * i, step, jnp.zeros((tq, MLA_DK), F32))
        for d in range(ATTN_WIDE):
            dq = step(ATTN_WIDE * i + d, dq, -d * t)
        dq_ref[...] = dq

    hspec = lambda w: pl.BlockSpec((L, w), lambda h, i: (0, h))
    bspec = lambda w: pl.BlockSpec((tq, w), lambda h, i: (i, h))
    return pl.pallas_call(
        body, name="attn_bwd_dq", grid=(MLA_HEADS, L // tq),
        in_specs=[bspec(MLA_DK), hspec(MLA_DK), hspec(MLA_V), bspec(MLA_V), bspec(MLA_V), bspec(MLA_V)],
        out_specs=bspec(MLA_DK),
        out_shape=jax.ShapeDtypeStruct((L, MLA_HEADS * MLA_DK), F32),
        compiler_params=_cparams(("parallel", "parallel")),
    )(q, k, v, o, lse, do)


def attn_bwd_dkv(q, k, v, o, lse, do, scale, t):
    L = q.shape[0]
    tk = ATTN_WIDE * t

    def body(q_ref, k_ref, v_ref, o_ref, lse_ref, do_ref, dk_ref, dv_ref):
        j = pl.program_id(1)
        kb, vb = k_ref[...], v_ref[...]

        def step(i, carry, shift=None):
            dk, dv = carry
            r = pl.ds(pl.multiple_of(i * t, t), t)
            qi, doi = q_ref[r, :], do_ref[r, :]
            delta = jnp.sum(doi * o_ref[r, :], axis=-1, keepdims=True)
            lse_c = jnp.max(lse_ref[r, :], axis=-1, keepdims=True)
            p = jnp.exp(_scores(qi, kb, scale, shift) - lse_c)
            ds = p * (_dot(doi, vb, _NT) - delta) * scale
            return dk + _dot(ds, qi, _TN), dv + _dot(p, doi, _TN)

        carry = (jnp.zeros((tk, MLA_DK), F32), jnp.zeros((tk, MLA_V), F32))
        for d in range(ATTN_WIDE):
            carry = step(ATTN_WIDE * j + d, carry, d * t)
        dk, dv = lax.fori_loop(ATTN_WIDE * (j + 1), L // t, step, carry)
        dk_ref[...] = dk
        dv_ref[...] = dv

    hspec = lambda w: pl.BlockSpec((L, w), lambda h, j: (0, h))
    bspec = lambda w: pl.BlockSpec((tk, w), lambda h, j: (j, h))
    return pl.pallas_call(
        body, name="attn_bwd_dkv", grid=(MLA_HEADS, L // tk),
        in_specs=[hspec(MLA_DK), bspec(MLA_DK), bspec(MLA_V), hspec(MLA_V), hspec(MLA_V), hspec(MLA_V)],
        out_specs=[bspec(MLA_DK), bspec(MLA_V)],
        out_shape=[jax.ShapeDtypeStruct((L, MLA_HEADS * MLA_DK), F32), jax.ShapeDtypeStruct((L, MLA_HEADS * MLA_V), F32)],
        compiler_params=_cparams(("parallel", "parallel")),
    )(q, k, v, o, lse, do)


S5_LANES = S5_GB * S5_STATE


def _cmul(ar, ai, br, bi):
    return ar * br - ai * bi, ar * bi + ai * br


def _a_powers(ar, ai, reverse):
    a2 = _cmul(ar, ai, ar, ai)
    a4 = _cmul(*a2, *a2)
    row = lax.broadcasted_iota(jnp.int32, (8, ar.shape[1]), 0)
    e = (8 - row) if reverse else (row + 1)
    tr, ti = jnp.ones((8, ar.shape[1]), F32), jnp.zeros((8, ar.shape[1]), F32)
    for bit, (pr, pi) in ((1, (ar, ai)), (2, a2), (4, a4), (8, _cmul(*a4, *a4))):
        nr, ni = _cmul(tr, ti, pr, pi)
        sel = (e & bit) != 0
        tr, ti = jnp.where(sel, nr, tr), jnp.where(sel, ni, ti)
    pows = []
    for d, (pr, pi) in zip((1, 2, 4), ((ar, ai), a2, a4)):
        keep = (row < 8 - d) if reverse else (row >= d)
        pows.append((jnp.where(keep, pr, 0.0), jnp.where(keep, pi, 0.0)))
    return pows, (tr, ti)


def _scan8(xr, xi, pows, table, cr, ci, reverse):
    for d, (pr, pi) in zip((1, 2, 4), pows):
        shift = 8 - d if reverse else d
        mr, mi = _cmul(pr, pi, pltpu.roll(xr, shift, 0), pltpu.roll(xi, shift, 0))
        xr, xi = xr + mr, xi + mi
    mr, mi = _cmul(table[0], table[1], cr, ci)
    return xr + mr, xi + mi


def _row_of(x, r):
    row = lax.broadcasted_iota(jnp.int32, x.shape, 0)
    return jnp.sum(jnp.where(row == r, x, 0.0), axis=0, keepdims=True)


def _s5_scan_fwd(h_re, h_im, ar, ai, L):
    pows, table = _a_powers(ar, ai, False)

    def step(i, carry):
        r = pl.ds(pl.multiple_of(i * 8, 8), 8)
        xr, xi = _scan8(h_re[r, :], h_im[r, :], pows, table, carry[0], carry[1], False)
        h_re[r, :] = xr
        h_im[r, :] = xi
        return xr[7:8, :], xi[7:8, :]

    z = jnp.zeros((1, ar.shape[1]), F32)
    lax.fori_loop(0, L // 8, step, (z, z))


def _s5_specs(L):
    return [pl.BlockSpec((L, 128), lambda g: (0, g)),
            pl.BlockSpec((1, 128, S5_LANES), lambda g: (g, 0, 0)), pl.BlockSpec((1, 128, S5_LANES), lambda g: (g, 0, 0)),
            pl.BlockSpec((1, 1, S5_LANES), lambda g: (g, 0, 0)), pl.BlockSpec((1, 1, S5_LANES), lambda g: (g, 0, 0)),
            pl.BlockSpec((1, S5_LANES, 128), lambda g: (g, 0, 0)), pl.BlockSpec((1, S5_LANES, 128), lambda g: (g, 0, 0))]


def s5_fwd(u, w_re, w_im, a_re, a_im, c_re, c_im):
    L, D = u.shape

    def body(u_ref, wr, wi, ar, ai, cr, ci, y_ref, h_re, h_im):
        ub = u_ref[...]
        h_re[...] = _dot(ub, wr[0], _NN)
        h_im[...] = _dot(ub, wi[0], _NN)
        _s5_scan_fwd(h_re, h_im, ar[0], ai[0], L)
        y_ref[...] = _dot(h_re[...], cr[0], _NN) - _dot(h_im[...], ci[0], _NN)

    return pl.pallas_call(
        body, name="s5_fwd", grid=(D // 128,),
        in_specs=_s5_specs(L), out_specs=pl.BlockSpec((L, 128), lambda g: (0, g)),
        out_shape=jax.ShapeDtypeStruct((L, D), F32),
        scratch_shapes=[pltpu.VMEM((L, S5_LANES), F32), pltpu.VMEM((L, S5_LANES), F32)],
        compiler_params=_cparams(("parallel",)),
    )(u, w_re, w_im, a_re, a_im, c_re, c_im)


def s5_bwd(u, w_re, w_im, a_re, a_im, c_re, c_im, dy, tc):
    L, D = u.shape
    nch = L // tc

    def body(u_ref, wr, wi, ar_ref, ai_ref, cr, ci, dy_ref, du_ref, dwr, dwi, dar, dai, dcr, dci, h_re, h_im, g_re, g_im):
        ar, ai = ar_ref[0], ai_ref[0]
        ub = u_ref[...]
        h_re[...] = _dot(ub, wr[0], _NN)
        h_im[...] = _dot(ub, wi[0], _NN)
        _s5_scan_fwd(h_re, h_im, ar, ai, L)
        dyb = dy_ref[...]
        dcr[0] = _dot(h_re[...], dyb, _TN)
        dci[0] = -_dot(h_im[...], dyb, _TN)
        pows, table = _a_powers(ar, -ai, True)
        dwr[0] = jnp.zeros((128, S5_LANES), F32)
        dwi[0] = jnp.zeros((128, S5_LANES), F32)
        z1 = jnp.zeros((1, S5_LANES), F32)
        z8 = jnp.zeros((8, S5_LANES), F32)

        def chunk(cc, carry):
            c0 = pl.multiple_of((nch - 1 - cc) * tc, tc)
            rows_c = pl.ds(c0, tc)
            dyc = dy_ref[rows_c, :]
            g_re[...] = _dot(dyc, cr[0], _NT)
            g_im[...] = -_dot(dyc, ci[0], _NT)

            def step(ii, cy):
                gr_c, gi_c, acc_r, acc_i = cy
                i8 = pl.multiple_of((tc // 8 - 1 - ii) * 8, 8)
                rl = pl.ds(i8, 8)
                xr, xi = _scan8(g_re[rl, :], g_im[rl, :], pows, table, gr_c, gi_c, True)
                g_re[rl, :] = xr
                g_im[rl, :] = xi
                t0 = c0 + i8
                hb_r, hb_i = h_re[pl.ds(t0, 8), :], h_im[pl.ds(t0, 8), :]
                tp = pl.multiple_of(jnp.maximum(t0 - 8, 0), 8)
                first = (t0 > 0).astype(F32)
                pr = h_re[pl.ds(tp, 8), :][7:8, :] * first
                pi = h_im[pl.ds(tp, 8), :][7:8, :] * first
                row = lax.broadcasted_iota(jnp.int32, xr.shape, 0)
                hp_r = jnp.where(row == 0, pr, pltpu.roll(hb_r, 1, 0))
                hp_i = jnp.where(row == 0, pi, pltpu.roll(hb_i, 1, 0))
                return (xr[0:1, :], xi[0:1, :],
                        acc_r + xr * hp_r + xi * hp_i, acc_i + xi * hp_r - xr * hp_i)

            cy = lax.fori_loop(0, tc // 8, step, carry)
            uc = u_ref[rows_c, :]
            gr, gi = g_re[...], g_im[...]
            du_ref[rows_c, :] = _dot(gr, wr[0], _NT) + _dot(gi, wi[0], _NT)
            dwr[0] += _dot(uc, gr, _TN)
            dwi[0] += _dot(uc, gi, _TN)
            return cy

        _, _, acc_r, acc_i = lax.fori_loop(0, nch, chunk, (z1, z1, z8, z8))
        dar[0] = jnp.sum(acc_r, axis=0, keepdims=True)
        dai[0] = jnp.sum(acc_i, axis=0, keepdims=True)

    specs = _s5_specs(L)
    return pl.pallas_call(
        body, name="s5_bwd", grid=(D // 128,),
        in_specs=specs + [pl.BlockSpec((L, 128), lambda g: (0, g))],
        out_specs=[pl.BlockSpec((L, 128), lambda g: (0, g))] + specs[1:],
        out_shape=[jax.ShapeDtypeStruct((L, D), F32)] + [jax.ShapeDtypeStruct(x.shape, F32)
                                                        for x in (w_re, w_im, a_re, a_im, c_re, c_im)],
        scratch_shapes=[pltpu.VMEM((L, S5_LANES), F32), pltpu.VMEM((L, S5_LANES), F32),
                        pltpu.VMEM((tc, S5_LANES), F32), pltpu.VMEM((tc, S5_LANES), F32)],
        compiler_params=_cparams(("parallel",)),
    )(u, w_re, w_im, a_re, a_im, c_re, c_im, dy)


def _s5_discretize(lr, li, ldt, br, bi):
    dt = jnp.exp(ldt)
    mag = jnp.exp(lr * dt)
    ar, ai = mag * jnp.cos(li * dt), mag * jnp.sin(li * dt)
    den = lr * lr + li * li
    zr = ((ar - 1.0) * lr + ai * li) / den
    zi = (ai * lr - (ar - 1.0) * li) / den
    p = lax.broadcasted_iota(jnp.int32, (S5_STATE, S5_STATE * S5_GROUP), 0)
    c = lax.broadcasted_iota(jnp.int32, (S5_STATE, S5_STATE * S5_GROUP), 1)
    rep = (c // S5_GROUP == p).astype(F32)
    zr, zi = _dot(zr, rep, _NN, HI), _dot(zi, rep, _NN, HI)
    return ar, ai, zr * br - zi * bi, zr * bi + zi * br


def _conv_shift(x, d):
    row = lax.broadcasted_iota(jnp.int32, x.shape, 0)
    return jnp.where(row >= d, pltpu.roll(x, d, 0), 0.0)


def _conv_unshift(x, d):
    n = x.shape[0]
    row = lax.broadcasted_iota(jnp.int32, x.shape, 0)
    return jnp.where(row < n - d, pltpu.roll(x, n - d, 0), 0.0)


@functools.partial(jax.custom_vjp, nondiff_argnums=(1,))
def _shift_rows(x, d):
    return _conv_shift(x, d)


_shift_rows.defvjp(lambda x, d: (_conv_shift(x, d), None), lambda d, _, g: (_conv_unshift(g, d),))


def _conv_gate(ug, uv, wg0, wg1, wg2, wv0, wv1, wv2, bg, bv):
    def conv(u, w0, w1, w2, b):
        return u * w2 + _shift_rows(u, 1) * w1 + _shift_rows(u, 2) * w0 + b
    return (jax.nn.silu(conv(ug, wg0, wg1, wg2, bg)) * conv(uv, wv0, wv1, wv2, bv),)


def _rms_fn(x, gain):
    return (_rms(x, gain),)


def _softmax_rows(s):
    e = jnp.exp(s - lax.stop_gradient(jnp.max(s, axis=-1, keepdims=True)))
    return e / jnp.sum(e, axis=-1, keepdims=True)


def _xa_core(qp, k, v, q_gain):
    dh = qp.shape[1] // XA_HEADS
    outs = []
    for h in range(XA_HEADS):
        sl = slice(h * dh, (h + 1) * dh)
        p = _softmax_rows(_dot(_rms(qp[:, sl], q_gain), k[:, sl], _NT) * (dh ** -0.5))
        outs.append(_dot(p, v[:, sl], _NN))
    return (jnp.concatenate(outs, axis=1),)


def _mem_kv(mem, mem_gain, wk, wv, k_gain):
    m = _rms(mem, mem_gain)
    kp = _dot(m, wk, _NN)
    dh = kp.shape[1] // XA_HEADS
    k = jnp.concatenate([_rms(kp[:, h * dh:(h + 1) * dh], k_gain) for h in range(XA_HEADS)], axis=1)
    return k, _dot(m, wv, _NN)


def _s5_post(y, u, d):
    return (jax.nn.gelu(y + d * u),)


def _glu(a, b):
    return (a * jax.nn.sigmoid(b),)


def _lb_first(logits):
    e = jnp.exp(logits - lax.stop_gradient(jnp.max(logits, axis=0, keepdims=True)))
    return (_row_of(e, 0) / jnp.sum(e, axis=0, keepdims=True),)


def _loss_fn(y, t):
    e = y - t
    part = 0.5 * jnp.sum(e * e) / y.shape[1]
    return e * (1.0 / y.shape[1]), jnp.full((8, 128), part / (8 * 128), F32)


def _out(shape, dtype, tm):
    return (shape, dtype, (tm, shape[1]), lambda i: (i, 0))


def rms_fwd(h, gain, tm, dtype):
    return blocked_fwd(_rms_fn, [rows(h, tm), full(gain)], [_out(h.shape, dtype, tm)], h.shape[0] // tm, "rms_fwd")[0]


def rms_bwd(h, gain, dy, tm, residual):
    return blocked_bwd(_rms_fn, [rows(h, tm, 'blk'), full(gain, 'acc')], [rows(dy, tm)], h.shape[0] // tm, "rms_bwd",
                       plus=rows(residual, tm))


def _adamw_math(w, g, m, v):
    m = ADAM_B1 * m + (1.0 - ADAM_B1) * g
    v = ADAM_B2 * v + (1.0 - ADAM_B2) * jnp.square(g)
    m_hat = m / (1.0 - ADAM_B1 ** ADAM_STEP)
    v_hat = v / (1.0 - ADAM_B2 ** ADAM_STEP)
    return -ADAM_LR * (m_hat / (jnp.sqrt(v_hat) + ADAM_EPS) + ADAM_WD * w), m, v


def adamw(w, g, m, v, name):
    R = w.shape[0]
    tm = _tile(R, 256)
    assert g.shape == w.shape == m.shape == v.shape, (name, w.shape, g.shape)

    def body(w_ref, g_ref, m_ref, v_ref, d_ref, nm_ref, nv_ref):
        d_ref[...], nm_ref[...], nv_ref[...] = _adamw_math(w_ref[...], g_ref[...], m_ref[...], v_ref[...])

    spec = pl.BlockSpec((tm, w.shape[1]), lambda i: (i, 0))
    return pl.pallas_call(
        body, name=name, grid=(R // tm,), in_specs=[spec] * 4, out_specs=[spec] * 3,
        out_shape=[jax.ShapeDtypeStruct(w.shape, F32)] * 3, compiler_params=_cparams(("parallel",)),
    )(w, g, m, v)


def adamw_layer(w, g, m, v, layer, bufs, name):
    R, C = g.shape
    tm = _tile(R, 256)
    assert w.shape[1:] == (R, C) and all(b.shape == w.shape for b in bufs), (name, w.shape, g.shape)

    def body(w_ref, g_ref, m_ref, v_ref, *rest):
        g_out, d_ref, nm_ref, nv_ref = rest[-4:]
        g_ = g_ref[...]
        g_out[...] = g_
        d_ref[...], nm_ref[...], nv_ref[...] = _adamw_math(w_ref[...], g_, m_ref[...], v_ref[...])

    lspec = pl.BlockSpec((None, tm, C), lambda i: (layer, i, 0))
    any_spec = pl.BlockSpec(memory_space=pl.ANY)
    return pl.pallas_call(
        body, name=name, grid=(R // tm,),
        in_specs=[lspec, pl.BlockSpec((tm, C), lambda i: (i, 0)), lspec, lspec] + [any_spec] * 4,
        out_specs=[lspec] * 4, out_shape=[jax.ShapeDtypeStruct(w.shape, F32)] * 4,
        input_output_aliases={4: 0, 5: 1, 6: 2, 7: 3}, compiler_params=_cparams(("parallel",)),
    )(w, g, m, v, *bufs)


def add2(x, y, name, out_dtype=F32):
    shape = x.shape
    x, y = x.reshape(-1, shape[-1]), y.reshape(-1, shape[-1])
    R, C = x.shape
    tm = _tile(R, 256)

    def body(x_ref, y_ref, o_ref):
        o_ref[...] = (x_ref[...] + y_ref[...]).astype(o_ref.dtype)

    spec = pl.BlockSpec((tm, C), lambda i: (i, 0))
    return pl.pallas_call(
        body, name=name, grid=(R // tm,), in_specs=[spec, spec], out_specs=spec,
        out_shape=jax.ShapeDtypeStruct((R, C), out_dtype), compiler_params=_cparams(("parallel",)),
    )(x, y).reshape(shape)


def add_chips(own, got, name):
    n, R, C = got.shape
    tm = _tile(R, 256)

    def body(*refs):
        acc = refs[0][...].astype(F32)
        for r in refs[1:-1]:
            acc = acc + r[...].astype(F32)
        refs[-1][...] = acc

    return pl.pallas_call(
        body, name=name, grid=(R // tm,),
        in_specs=[pl.BlockSpec((tm, C), lambda i: (i, 0))] + [pl.BlockSpec((None, tm, C), lambda i, j=j: (j, i, 0))
                                                            for j in range(n)],
        out_specs=pl.BlockSpec((tm, C), lambda i: (i, 0)),
        out_shape=jax.ShapeDtypeStruct((R, C), F32), compiler_params=_cparams(("parallel",)),
    )(own, *([got] * n))


_HBM = pl.BlockSpec(memory_space=pltpu.HBM)
N_CHIPS = 4


def _my_place():
    return lax.axis_index("x"), lax.axis_index("y"), lax.axis_index("c")


def _window(ref, axis, start, size):
    idx = [slice(None)] * len(ref.shape)
    idx[axis] = pl.ds(start, size)
    return ref.at[tuple(idx)]


def _comm_call(body, name, xs, out_shapes, n_remote, n_local, sequencer=None):
    sems = [pltpu.SemaphoreType.DMA((n_remote,)), pltpu.SemaphoreType.DMA((n_remote,)),
            pltpu.SemaphoreType.DMA((max(n_local, 1),))]
    if sequencer is None:
        return pl.pallas_call(
            body, name=name, in_specs=[_HBM] * len(xs), out_specs=[_HBM] * len(out_shapes), out_shape=out_shapes,
            scratch_shapes=sems, compiler_params=pltpu.CompilerParams(has_side_effects=True),
        )(*xs)
    peers_of, collective_id = sequencer
    hbm = pltpu.MemorySpace.HBM
    x_refs = [jax.new_ref(x, memory_space=hbm) for x in xs]
    o_refs = [jax.empty_ref(s, memory_space=hbm) for s in out_shapes]

    @pl.kernel(mesh=plsc.ScalarSubcoreMesh(axis_name="sequencer", num_cores=1), name=name, scratch_types=tuple(sems),
               compiler_params=pltpu.CompilerParams(collective_id=collective_id))
    def launch(send_sems, recv_sems, local_sems):
        peers = peers_of(*_my_place())
        barrier = pltpu.get_barrier_semaphore()
        for peer in peers:
            pl.semaphore_signal(barrier, inc=1, device_id=peer, device_id_type=MESH)
        pl.semaphore_wait(barrier, len(peers))
        body(*x_refs, *o_refs, send_sems, recv_sems, local_sems)

    launch()
    return [o[...] for o in o_refs]


def _sibling(mx, my, mc):
    return [(mx, my, 1 - mc)]


def _same_core_of_other_chips(mx, my, mc):
    return [(tx, ty, mc) for tx, ty in _other_chips(mx, my)]


def _run(copies):
    for cp in copies:
        cp.start()
    for cp in copies:
        cp.wait()


def _other_chips(mx, my):
    return [(mx ^ (j >> 1), my ^ (j & 1)) for j in (1, 2, 3)]


def chip_gather(xs, axes, name):
    n = len(xs)
    shapes, final = [], []
    for x, ax in zip(xs, axes):
        s = list(x.shape)
        if ax is None:
            shapes.append([N_CHIPS] + s)
            final.append(shapes[-1])
        elif ax < x.ndim - 1:
            shapes.append(s[:ax] + [N_CHIPS] + s[ax:])
            final.append(s[:ax] + [N_CHIPS * s[ax]] + s[ax + 1:])
        else:
            assert s[ax] % 128 == 0, (name, s)
            shapes.append(s[:ax] + [N_CHIPS * s[ax]])
            final.append(shapes[-1])

    def body(*refs):
        x_refs, o_refs = refs[:n], refs[n:2 * n]
        send_sems, recv_sems, local_sems = refs[2 * n:]
        mx, my, mc = _my_place()
        q = 2 * mx + my
        copies = []
        for i, (x_ref, o_ref, ax) in enumerate(zip(x_refs, o_refs, axes)):
            if ax is None or ax < len(x_ref.shape) - 1:
                dst = o_ref.at[(slice(None),) * (ax or 0) + (q,)]
            else:
                dst = _window(o_ref, ax, q * x_ref.shape[ax], x_ref.shape[ax])
            copies.append(pltpu.make_async_copy(x_ref, dst, local_sems.at[i]))
            for j, (tx, ty) in enumerate(_other_chips(mx, my)):
                copies.append(pltpu.make_async_remote_copy(
                    src_ref=x_ref, dst_ref=dst, send_sem=send_sems.at[3 * i + j], recv_sem=recv_sems.at[3 * i + j],
                    device_id=(tx, ty, mc), device_id_type=MESH))
        _run(copies)

    out_shapes = [jax.ShapeDtypeStruct(tuple(s), x.dtype) for s, x in zip(shapes, xs)]
    return [o.reshape(f) for o, f in zip(_comm_call(body, name, xs, out_shapes, 3 * n, n), final)]


def gather_two_level(xs, name):
    n = len(xs)
    shapes = [jax.ShapeDtypeStruct((2, N_CHIPS) + x.shape[1:], x.dtype) for x in xs]

    def body(*refs):
        x_refs, o_refs = refs[:n], refs[n:2 * n]
        send_sems, recv_sems, local_sems = refs[2 * n:]
        mx, my, mc = _my_place()
        q = 2 * mx + my
        first, local, second = [], [], []
        for i, (x_ref, o_ref) in enumerate(zip(x_refs, o_refs)):
            local.append(pltpu.make_async_copy(x_ref.at[mc], o_ref.at[mc, q], local_sems.at[i]))
            for j, (tx, ty) in enumerate(_other_chips(mx, my)):
                first.append(pltpu.make_async_remote_copy(
                    src_ref=x_ref.at[mc], dst_ref=o_ref.at[mc, q], send_sem=send_sems.at[4 * i + j],
                    recv_sem=recv_sems.at[4 * i + j], device_id=(tx, ty, mc), device_id_type=MESH))
            second.append(pltpu.make_async_remote_copy(
                src_ref=o_ref.at[mc], dst_ref=o_ref.at[mc], send_sem=send_sems.at[4 * i + 3],
                recv_sem=recv_sems.at[4 * i + 3], device_id=(mx, my, 1 - mc), device_id_type=MESH))
        for cp in local + first:
            cp.start()
        for cp in local:
            cp.wait()
        for cp in first:
            cp.wait_recv()
        _run(second)
        for cp in first:
            cp.wait_send()

    return _comm_call(body, name, xs, shapes, 4 * n, n)


def gather_two_level_sequencer(xs, name, collective_id):
    n = len(xs)
    hbm = pltpu.MemorySpace.HBM
    x_refs = [jax.new_ref(x, memory_space=hbm) for x in xs]
    o_refs = [jax.empty_ref(jax.ShapeDtypeStruct((2, N_CHIPS) + x.shape[1:], x.dtype), memory_space=hbm) for x in xs]

    @pl.kernel(mesh=plsc.ScalarSubcoreMesh(axis_name="sequencer", num_cores=1), name=name,
               scratch_types=(pltpu.SemaphoreType.DMA((4 * n,)), pltpu.SemaphoreType.DMA((4 * n,)),
                              pltpu.SemaphoreType.DMA((n,))),
               compiler_params=pltpu.CompilerParams(collective_id=collective_id))
    def launch(send_sems, recv_sems, local_sems):
        mx, my, mc = _my_place()
        peers = [(tx, ty, mc) for tx, ty in _other_chips(mx, my)] + [(mx, my, 1 - mc)]
        barrier = pltpu.get_barrier_semaphore()
        for peer in peers:
            pl.semaphore_signal(barrier, inc=1, device_id=peer, device_id_type=MESH)
        pl.semaphore_wait(barrier, len(peers))
        q = 2 * mx + my
        first, local, second = [], [], []
        for i, (x_ref, o_ref) in enumerate(zip(x_refs, o_refs)):
            local.append(pltpu.make_async_copy(x_ref.at[mc], o_ref.at[mc, q], local_sems.at[i]))
            for j, peer in enumerate(peers[:3]):
                first.append(pltpu.make_async_remote_copy(
                    src_ref=x_ref.at[mc], dst_ref=o_ref.at[mc, q], send_sem=send_sems.at[4 * i + j],
                    recv_sem=recv_sems.at[4 * i + j], device_id=peer, device_id_type=MESH))
            second.append(pltpu.make_async_remote_copy(
                src_ref=o_ref.at[mc], dst_ref=o_ref.at[mc], send_sem=send_sems.at[4 * i + 3],
                recv_sem=recv_sems.at[4 * i + 3], device_id=peers[3], device_id_type=MESH))
        for cp in local + first:
            cp.start()
        for cp in local:
            cp.wait()
        for cp in first:
            cp.wait_recv()
        _run(second)
        for cp in first:
            cp.wait_send()

    launch()
    return [o[...] for o in o_refs]


def pair_swap(xs, name, halves, collective_id=None):
    n = len(xs)
    shapes = [jax.ShapeDtypeStruct(x.shape[:1] + x.shape[2:] if halves else x.shape, x.dtype) for x in xs]

    def body(*refs):
        x_refs, o_refs = refs[:n], refs[n:2 * n]
        send_sems, recv_sems, _ = refs[2 * n:]
        mx, my, mc = _my_place()
        _run([pltpu.make_async_remote_copy(
            src_ref=x_ref.at[:, 1 - mc] if halves else x_ref, dst_ref=o_ref, send_sem=send_sems.at[i],
            recv_sem=recv_sems.at[i], device_id=(mx, my, 1 - mc), device_id_type=MESH)
            for i, (x_ref, o_ref) in enumerate(zip(x_refs, o_refs))])

    return _comm_call(body, name, xs, shapes, n, 0, None if collective_id is None else (_sibling, collective_id))


def chip_all_to_all(xs, name, collective_id=None):
    n = len(xs)
    shapes = [jax.ShapeDtypeStruct((N_CHIPS - 1,) + x.shape[1:], x.dtype) for x in xs]

    def body(*refs):
        x_refs, o_refs = refs[:n], refs[n:2 * n]
        send_sems, recv_sems, _ = refs[2 * n:]
        mx, my, mc = _my_place()
        copies = []
        for i, (x_ref, o_ref) in enumerate(zip(x_refs, o_refs)):
            for j, (tx, ty) in enumerate(_other_chips(mx, my)):
                copies.append(pltpu.make_async_remote_copy(
                    src_ref=x_ref.at[2 * tx + ty], dst_ref=o_ref.at[j], send_sem=send_sems.at[3 * i + j],
                    recv_sem=recv_sems.at[3 * i + j], device_id=(tx, ty, mc), device_id_type=MESH))
        _run(copies)

    return _comm_call(body, name, xs, shapes, 3 * n, 0,
                      None if collective_id is None else (_same_core_of_other_chips, collective_id))


WEIGHTS = ['norm_mix', 'norm_xa', 'norm_mem', 'norm_ffn', 'xa_wq', 'xa_wk', 'xa_wv', 'xa_wo', 'xa_q_norm', 'xa_k_norm',
           'ffn_w_up', 'ffn_conv_w', 'ffn_conv_b', 'ffn_w_down', 'hg_lb_logits', 'mix_w_in', 'hg_out_norm',
           'mla_q_a_norm', 'mla_w_uq', 'mla_kv_a_norm', 'mla_w_ukv', 'mla_qn_nope', 'mla_qn_rope', 'mla_kn_nope',
           'mla_kn_rope', 'mix_w_out', 's5_lam_re', 's5_lam_im', 's5_log_dt', 's5_b_re', 's5_b_im', 's5_c_re',
           's5_c_im', 's5_d', 's5_w_glu_a', 's5_w_glu_b']
INPUTS = ['x', 'mem', 'positions'] + WEIGHTS + ['loss_target'] + ['m_' + n for n in WEIGHTS] + ['v_' + n for n in WEIGHTS]
SHARD_AXIS = {'xa_wq': 1, 'xa_wk': 1, 'xa_wv': 1, 'xa_wo': 1, 'ffn_w_up': 2, 'ffn_conv_w': 2, 'ffn_w_down': 1,
              'mix_w_in': 2, 'mla_w_uq': 2, 'mla_w_ukv': 2, 'mix_w_out': 1, 's5_d': 1, 's5_w_glu_a': 1, 's5_w_glu_b': 1}
BIG = ['xa_wq', 'xa_wk', 'xa_wv', 'xa_wo', 'ffn_w_up', 'ffn_w_down', 'mix_w_in', 'mix_w_out', 's5_w_glu_a', 's5_w_glu_b']
FIRST_NEEDED = ('mix_w_in', 'mix_w_out')
SMALL_SHARDED = [n for n in WEIGHTS if n in SHARD_AXIS and n not in BIG]
REPLICATED = [n for n in WEIGHTS if n not in SHARD_AXIS]
SMALL = SMALL_SHARDED + REPLICATED
PACK_W = 1024
ROW_MULT = 16
W_IN_SHARD = IN_WIDTH // N_CHIPS
W_IN_SHARD_PAD = 640


def _pack(flats, mult=ROW_MULT):
    flat = jnp.concatenate([f.reshape(-1) for f in flats])
    unit = mult * PACK_W
    n = -(-flat.shape[0] // unit) * unit
    return jnp.pad(flat, (0, n - flat.shape[0])).reshape(n // PACK_W, PACK_W)


def _unpack(packed, shapes):
    flat, out, o = packed.reshape(-1), [], 0
    for s in shapes:
        n = math.prod(s)
        out.append(flat[o:o + n].reshape(s))
        o += n
    return out


def _rope_pad(w):
    z = jnp.zeros(w.shape[:-1] + (MLA_ROPE // 2,), w.dtype)
    return jnp.concatenate([w[..., :MLA_ROPE // 2], z, w[..., MLA_ROPE // 2:], z], axis=-1)


def _rope_unpad(g):
    return jnp.concatenate([g[..., :MLA_ROPE // 2], g[..., 64:64 + MLA_ROPE // 2]], axis=-1)


def _blockdiag_in(bb):
    nb = bb.shape[0] // S5_GB
    t = bb.reshape(nb, S5_GB, S5_STATE, S5_GROUP).transpose(0, 1, 3, 2)
    return jnp.einsum('bgmp,gh->bgmhp', t, jnp.eye(S5_GB, dtype=bb.dtype)).reshape(nb, S5_GB * S5_GROUP, S5_LANES)


def _blockdiag_in_t(dw):
    nb = dw.shape[0]
    t = jnp.einsum('bgmhp,gh->bgmp', dw.reshape(nb, S5_GB, S5_GROUP, S5_GB, S5_STATE), jnp.eye(S5_GB, dtype=dw.dtype))
    return t.transpose(0, 1, 3, 2).reshape(nb * S5_GB, S5_STATE, S5_GROUP)


def _blockdiag_out(c):
    nb = c.shape[0] // S5_GB
    t = c.reshape(nb, S5_GB, S5_GROUP, S5_STATE).transpose(0, 1, 3, 2)
    return jnp.einsum('bgpm,gh->bgphm', t, jnp.eye(S5_GB, dtype=c.dtype)).reshape(nb, S5_LANES, S5_GB * S5_GROUP)


def _blockdiag_out_t(dc):
    nb = dc.shape[0]
    t = jnp.einsum('bgphm,gh->bgpm', dc.reshape(nb, S5_GB, S5_STATE, S5_GB, S5_GROUP), jnp.eye(S5_GB, dtype=dc.dtype))
    return t.transpose(0, 1, 3, 2).reshape(nb * S5_GB, S5_GROUP, S5_STATE)


def _gather_weights(P):
    def halves(x):
        return x if x.shape[0] == 2 else x.reshape(2, x.shape[1] // 2, x.shape[2])

    now = [n for n in BIG if n in FIRST_NEEDED]
    later = [n for n in BIG if n not in FIRST_NEEDED]
    xs = [halves(P[n].astype(BF16)) for n in now] + [halves(_pack([P[n] for n in SMALL_SHARDED], 2 * ROW_MULT)[None])]
    got = gather_two_level(xs, "gather_weights")
    got, xs_later = lax.optimization_barrier((got, [halves(P[n].astype(BF16)) for n in later]))
    got_later = gather_two_level_sequencer(xs_later, "gather_weights_later", 1)
    full_w = {}
    for n, g in list(zip(now, got[:-1])) + list(zip(later, got_later)):
        two_layers, by_rows = P[n].shape[0] == 2, SHARD_AXIS[n] == 1
        if two_layers and by_rows:
            full_w[n] = g.reshape(2, N_CHIPS * g.shape[2], g.shape[3])
        elif two_layers:
            full_w[n] = g.transpose(0, 2, 1, 3).reshape(2, g.shape[2], N_CHIPS * g.shape[3])
        elif by_rows:
            full_w[n] = g.transpose(1, 0, 2, 3).reshape(1, 2 * N_CHIPS * g.shape[2], g.shape[3])
        else:
            full_w[n] = g.transpose(0, 2, 1, 3).reshape(1, 2 * g.shape[2], N_CHIPS * g.shape[3])
    small = got[-1].transpose(1, 0, 2, 3).reshape(N_CHIPS, -1, PACK_W)
    per_chip = [_unpack(small[q], [P[n].shape for n in SMALL_SHARDED]) for q in range(N_CHIPS)]
    for i, n in enumerate(SMALL_SHARDED):
        full_w[n] = jnp.concatenate([per_chip[q][i] for q in range(N_CHIPS)], axis=SHARD_AXIS[n])
    return full_w


def _halves_first(x):
    return x.reshape(x.shape[0], 2, x.shape[1] // 2, x.shape[2]).transpose(1, 0, 2, 3)


def _reduce_batch(items, small, P, results, tag, ids):
    mx, my, mc = _my_place()
    q = 2 * mx + my
    ids = ids or {}
    names = [f"{n}_{lyr}" for n, lyr, _ in items] + (['small'] if small is not None else [])
    xs = [g.reshape(N_CHIPS, 2, g.shape[1] // 2, g.shape[2]) for g in [g for _, _, g in items] + ([small] if small is not None else [])]
    theirs = pair_swap(xs, "grads_pair_swap_" + tag, True, ids.get('swap'))
    pair = [add2(lax.dynamic_index_in_dim(x, mc, 1, False), t, "grads_pair_sum_" + n, F32 if n == 'small' else BF16)
            for x, t, n in zip(xs, theirs, names)]
    got = chip_all_to_all(pair, "grads_chip_all_to_all_" + tag, ids.get('a2a'))
    summed = [add_chips(lax.dynamic_index_in_dim(p, q, 0, False), g, "grads_chip_sum_" + n)
              for p, g, n in zip(pair, got, names)]
    other = pair_swap(summed, "grads_pair_join_" + tag, False, ids.get('join'))
    joined = [lax.cond(mc == 0, lambda a, b: jnp.concatenate([a, b], axis=0), lambda a, b: jnp.concatenate([b, a], axis=0),
                       s, o) for s, o in zip(summed, other)]
    for (n, lyr, _), g in zip(items, joined):
        if n == 'mix_w_in':
            g = g[:, :W_IN_SHARD]
        view = (P[n].shape[0], g.shape[0], P[n].shape[-1])
        bufs = results.get(n) or [lax.empty(view, F32) for _ in range(4)]
        results[n] = adamw_layer(P[n].reshape(view), g, P['m_' + n].reshape(view), P['v_' + n].reshape(view), lyr, bufs,
                                 f"adamw_{n}_{lyr}")
    return joined[-1] if small is not None else None


def _update_small(small_quarter, GS, P):
    mx, my, _ = _my_place()
    q = 2 * mx + my
    small_sum = chip_gather([small_quarter], [0], "grads_small_gather")[0]
    g_small = dict(zip(SMALL, _unpack(small_sum, [GS[n].shape for n in SMALL])))
    for n in SMALL_SHARDED:
        s = P[n].shape[SHARD_AXIS[n]]
        g_small[n] = lax.dynamic_slice_in_dim(g_small[n], q * s, s, axis=SHARD_AXIS[n])
    grad, delta, new_m, new_v = {}, {}, {}, {}
    packed = lambda prefix: _pack([P[prefix + n] for n in SMALL])
    d, m_, v_ = adamw(packed(''), _pack([g_small[n] for n in SMALL]), packed('m_'), packed('v_'), "adamw_small")
    shapes = [P[n].shape for n in SMALL]
    grad.update(g_small)
    for out, pk in ((delta, d), (new_m, m_), (new_v, v_)):
        out.update(zip(SMALL, _unpack(pk, shapes)))
    return grad, delta, new_m, new_v


def _row(v):
    return v.reshape(1, -1)


def _xattn_fwd(h, mem, W, lyr, tm):
    g_xa, g_mem = _row(W['norm_xa'][lyr]), _row(W['norm_mem'][lyr])
    g_q, g_k = _row(W['xa_q_norm'][lyr]), _row(W['xa_k_norm'][lyr])
    wq, wk, wv, wo = (W[n][lyr] for n in ('xa_wq', 'xa_wk', 'xa_wv', 'xa_wo'))
    L, D = h.shape
    M = mem.shape[0]
    hx = rms_fwd(h, g_xa, tm, MXU_DTYPE)
    qp = matmul(hx, wq, name="xa_q")
    kv_opds = [full(mem), full(g_mem), full(wk), full(wv), full(g_k)]
    k, v = blocked_fwd(_mem_kv, kv_opds, [((M, D), F32, (M, D), lambda i: (0, 0))] * 2, 1, "xa_mem_kv")
    o = blocked_fwd(_xa_core, [rows(qp, tm), full(k), full(v), full(g_q)], [_out((L, D), MXU_DTYPE, tm)], L // tm,
                    "xa_core")[0]
    out = matmul(o, wo, add=h, name="xa_o")
    return out, (h, hx, qp, k, v, o)


def _xattn_bwd(dout, saved, mem, W, lyr, tm):
    h, hx, qp, k, v, o = saved
    g_xa, g_mem = _row(W['norm_xa'][lyr]), _row(W['norm_mem'][lyr])
    g_q, g_k = _row(W['xa_q_norm'][lyr]), _row(W['xa_k_norm'][lyr])
    wq, wk, wv, wo = (W[n][lyr] for n in ('xa_wq', 'xa_wk', 'xa_wv', 'xa_wo'))
    L = h.shape[0]
    do = matmul(dout, wo, "nt", name="xa_do")
    d_wo = matmul(o, dout, "tn", name="xa_dwo")
    dqp, dk, dv, d_gq = blocked_bwd(_xa_core, [rows(qp, tm, 'blk'), full(k, 'acc'), full(v, 'acc'), full(g_q, 'acc')],
                                    [rows(do, tm)], L // tm, "xa_core_bwd")
    d_wq = matmul(hx, dqp, "tn", name="xa_dwq")
    dhx = matmul(dqp, wq, "nt", name="xa_dhx")
    dh, d_gxa = rms_bwd(h, g_xa, dhx, tm, dout)
    d_gmem, d_wk, d_wv, d_gk = blocked_bwd(
        _mem_kv, [full(mem), full(g_mem, 'acc'), full(wk, 'acc'), full(wv, 'acc'), full(g_k, 'acc')],
        [full(dk), full(dv)], 1, "xa_mem_kv_bwd")
    by_chip = lambda g: g.reshape(N_CHIPS, g.shape[0] // N_CHIPS, g.shape[1])
    grads = {'norm_xa': d_gxa, 'norm_mem': d_gmem, 'xa_q_norm': d_gq, 'xa_k_norm': d_gk,
             'xa_wq': by_chip(d_wq), 'xa_wk': by_chip(d_wk), 'xa_wv': by_chip(d_wv), 'xa_wo': by_chip(d_wo)}
    return dh, grads


def _conv_params(W, lyr):
    cw, cb = W['ffn_conv_w'][lyr], W['ffn_conv_b'][lyr]
    F = cw.shape[1] // 2
    return [cw[0:1, :F], cw[1:2, :F], cw[2:3, :F], cw[0:1, F:], cw[1:2, F:], cw[2:3, F:], _row(cb[:F]), _row(cb[F:])]


def _ffn_fwd(h, W, lyr, tm):
    L, D = h.shape
    w_up, w_down = W['ffn_w_up'][lyr], W['ffn_w_down'][lyr]
    F = w_down.shape[0]
    hf = rms_fwd(h, _row(W['norm_ffn'][lyr]), tm, MXU_DTYPE)
    ug = matmul(hf, w_up[:, :F], name="ffn_up_gate")
    uv = matmul(hf, w_up[:, F:], name="ffn_up_value")
    opds = [cols(ug, 128), cols(uv, 128)] + [cols(p, 128) for p in _conv_params(W, lyr)]
    a = blocked_fwd(_conv_gate, opds, [((L, F), MXU_DTYPE, (L, 128), lambda j: (0, j))], F // 128, "ffn_conv_gate")[0]
    out = matmul(a, w_down, add=h, name="ffn_down")
    return out, (h, hf, ug, uv, a)


def _ffn_bwd(dout, saved, W, lyr, tm):
    h, hf, ug, uv, a = saved
    w_up, w_down = W['ffn_w_up'][lyr], W['ffn_w_down'][lyr]
    F = w_down.shape[0]
    da = matmul(dout, w_down, "nt", name="ffn_da")
    d_wdown = matmul(a, dout, "tn", name="ffn_dwdown")
    opds = [cols(ug, 128, 'blk'), cols(uv, 128, 'blk')] + [cols(p, 128, 'blk') for p in _conv_params(W, lyr)]
    gs = blocked_bwd(_conv_gate, opds, [cols(da, 128)], F // 128, "ffn_conv_gate_bwd")
    dug, duv = gs[0], gs[1]
    d_cw = jnp.concatenate([jnp.concatenate(gs[2:5], axis=0), jnp.concatenate(gs[5:8], axis=0)], axis=1)
    d_cb = jnp.concatenate([gs[8], gs[9]], axis=1)[0]
    half = N_CHIPS // 2
    d_wup = lax.empty((N_CHIPS, hf.shape[1], w_up.shape[1] // N_CHIPS), F32)
    d_wup = matmul(hf, dug, "tn", name="ffn_dwup_gate", into=(d_wup, 0), col_blocks=half)
    d_wup = matmul(hf, duv, "tn", name="ffn_dwup_value", into=(d_wup, half), col_blocks=half)
    dhf = matmul(dug, w_up[:, :F], "nt", name="ffn_dhf_gate")
    dhf = matmul(duv, w_up[:, F:], "nt", add=dhf, name="ffn_dhf_value")
    dh, d_g = rms_bwd(h, _row(W['norm_ffn'][lyr]), dhf, tm, dout)
    d_wdown = d_wdown.reshape(N_CHIPS, F // N_CHIPS, d_wdown.shape[1])
    return dh, {'norm_ffn': d_g, 'ffn_w_up': d_wup, 'ffn_conv_w': d_cw, 'ffn_conv_b': d_cb, 'ffn_w_down': d_wdown}


def _mla_params(W):
    w_uq = W['mla_w_uq'][0].reshape(MLA_Q_RANK, MLA_HEADS, MLA_QK)
    w_uq = jnp.concatenate([w_uq[..., :MLA_NOPE], _rope_pad(w_uq[..., MLA_NOPE:])], axis=-1)
    w_ukv = W['mla_w_ukv'][0].reshape(MLA_KV_RANK, MLA_HEADS, MLA_NOPE + MLA_V)
    w_ukv = jnp.concatenate([w_ukv[..., :MLA_NOPE].reshape(MLA_KV_RANK, -1), w_ukv[..., MLA_NOPE:].reshape(MLA_KV_RANK, -1)],
                            axis=1)
    return [_row(W['mla_q_a_norm'][0]), w_uq.reshape(MLA_Q_RANK, MLA_HEADS * MLA_DK), _row(W['mla_kv_a_norm'][0]), w_ukv,
            _row(W['mla_qn_nope'][0]), _row(_rope_pad(W['mla_qn_rope'][0])), _row(W['mla_kn_nope'][0]),
            _row(_rope_pad(W['mla_kn_rope'][0]))]


def _w_in_padded(W):
    w = W['mix_w_in'][0]
    return jnp.concatenate([w[:, :IN_WIDTH - MLA_ROPE], _rope_pad(w[:, IN_WIDTH - MLA_ROPE:])], axis=1)


def _mixer0_fwd(h, W, cos_p, sin_p, tm):
    L = h.shape[0]
    t = min(ATTN_ROWS, L // ATTN_WIDE)
    hn = rms_fwd(h, _row(W['norm_mix'][0]), tm, MXU_DTYPE)
    proj = matmul(hn, _w_in_padded(W), name="mix_in")
    logits = W['hg_lb_logits']
    lb = blocked_fwd(_lb_first, [full(logits)], [((1, HG_WIDTH), F32, (1, HG_WIDTH), lambda i: (0, 0))], 1, "hg_lb")[0]
    gain = _row(W['hg_out_norm'][0])
    o_hg, states = hgrn2_fwd(proj, lb, gain)
    mp = _mla_params(W)
    q, k, v = mla_prep_fwd(proj, cos_p, sin_p, mp, tm)
    scale = MLA_QK ** -0.5
    o_mla, lse = attn_fwd(q, k, v, scale, t)
    w_out = W['mix_w_out'][0]
    out = matmul(o_hg, w_out[:HG_WIDTH], add=h, name="mix_out_hg")
    out = matmul(o_mla, w_out[HG_WIDTH:], add=out, name="mix_out_mla")
    return out, (h, hn, proj, lb, o_hg, states, q, k, v, o_mla, lse)


def _mixer0_bwd(dout, saved, W, cos_p, sin_p, tm):
    h, hn, proj, lb, o_hg, states, q, k, v, o_mla, lse = saved
    L = h.shape[0]
    t = min(ATTN_ROWS, L // ATTN_WIDE)
    scale = MLA_QK ** -0.5
    w_out = W['mix_w_out'][0]
    gain = _row(W['hg_out_norm'][0])
    do_hg = matmul(dout, w_out[:HG_WIDTH], "nt", name="mix_do_hg")
    do_mla = matmul(dout, w_out[HG_WIDTH:], "nt", name="mix_do_mla")
    d_wout = jnp.concatenate([matmul(o_hg, dout, "tn", name="mix_dwout_hg"), matmul(o_mla, dout, "tn", name="mix_dwout_mla")],
                             axis=0)
    dq = attn_bwd_dq(q, k, v, o_mla, lse, do_mla, scale, t)
    dk, dv = attn_bwd_dkv(q, k, v, o_mla, lse, do_mla, scale, t)
    mp = _mla_params(W)
    d_mla, d_qa, d_wuq, d_kva, d_wukv, d_qnn, d_qnr, d_knn, d_knr = mla_prep_bwd(proj, cos_p, sin_p, mp, dq, dk, dv, tm)
    d_hg, d_lb, d_gain = hgrn2_bwd(proj, lb, gain, states, do_hg)
    w_in, n_hg = _w_in_padded(W), 4 * HG_WIDTH
    d_win = jnp.concatenate([matmul(hn, d_hg, "tn", name="mix_dwin_hg"), matmul(hn, d_mla, "tn", name="mix_dwin_mla")], axis=1)
    dhn = matmul(d_hg, w_in[:, :n_hg], "nt", name="mix_dhn_hg")
    dhn = matmul(d_mla, w_in[:, n_hg:], "nt", add=dhn, name="mix_dhn_mla")
    dh, d_g = rms_bwd(h, _row(W['norm_mix'][0]), dhn, tm, dout)
    logits = W['hg_lb_logits']
    d_logits = blocked_bwd(_lb_first, [full(logits, 'acc')], [full(d_lb)], 1, "hg_lb_bwd")[0]
    d_wuq = d_wuq.reshape(MLA_Q_RANK, MLA_HEADS, MLA_DK)
    d_wuq = jnp.concatenate([d_wuq[..., :MLA_NOPE], _rope_unpad(d_wuq[..., MLA_NOPE:])], axis=-1)
    hw = MLA_HEADS * MLA_NOPE
    d_wukv = jnp.concatenate([d_wukv[:, :hw].reshape(MLA_KV_RANK, MLA_HEADS, MLA_NOPE),
                              d_wukv[:, hw:].reshape(MLA_KV_RANK, MLA_HEADS, MLA_V)], axis=-1)
    d_win = jnp.concatenate([d_win[:, :IN_WIDTH - MLA_ROPE], _rope_unpad(d_win[:, IN_WIDTH - MLA_ROPE:])], axis=1)
    d_win = d_win.reshape(d_win.shape[0], N_CHIPS, W_IN_SHARD).transpose(1, 0, 2)
    d_win = jnp.pad(d_win, ((0, 0), (0, 0), (0, W_IN_SHARD_PAD - W_IN_SHARD)))
    d_wout = d_wout.reshape(N_CHIPS, d_wout.shape[0] // N_CHIPS, d_wout.shape[1])
    grads = {'norm_mix': d_g, 'hg_lb_logits': d_logits, 'mix_w_in': d_win, 'hg_out_norm': d_gain,
             'mla_q_a_norm': d_qa, 'mla_w_uq': d_wuq.reshape(1, MLA_Q_RANK, -1), 'mla_kv_a_norm': d_kva,
             'mla_w_ukv': d_wukv.reshape(1, MLA_KV_RANK, -1), 'mla_qn_nope': d_qnn, 'mla_qn_rope': _rope_unpad(d_qnr),
             'mla_kn_nope': d_knn, 'mla_kn_rope': _rope_unpad(d_knr), 'mix_w_out': d_wout}
    return dh,grads


def _s5_inputs(W):
    G = W['s5_lam_re'].shape[1]
    return [W['s5_lam_re'][0], W['s5_lam_im'][0], W['s5_log_dt'][0].reshape(G, 1),
            W['s5_b_re'][0].reshape(G, -1), W['s5_b_im'][0].reshape(G, -1)]


def _mixer1_fwd(h, W, tm):
    L, D = h.shape
    u = rms_fwd(h, _row(W['norm_mix'][1]), tm, F32)
    di = _s5_inputs(W)
    G = di[0].shape[0]
    sq, wide = ((G, S5_STATE), F32, (G, S5_STATE), lambda i: (0, 0)), ((G, S5_STATE * S5_GROUP), F32, (G, S5_STATE * S5_GROUP), lambda i: (0, 0))
    ar, ai, bbr, bbi = blocked_fwd(_s5_discretize, [full(a) for a in di], [sq, sq, wide, wide], 1, "s5_discretize")
    nb = G // S5_GB
    core = (_blockdiag_in(bbr.reshape(G, S5_STATE, S5_GROUP)), _blockdiag_in(bbi.reshape(G, S5_STATE, S5_GROUP)),
            ar.reshape(nb, 1, S5_LANES), ai.reshape(nb, 1, S5_LANES),
            _blockdiag_out(W['s5_c_re'][0]), _blockdiag_out(W['s5_c_im'][0]))
    y = s5_fwd(u, *core)
    d = W['s5_d']
    y2 = blocked_fwd(_s5_post, [rows(y, tm), rows(u, tm), full(d)], [_out((L, D), MXU_DTYPE, tm)], L // tm, "s5_post")[0]
    w_ab = jnp.concatenate([W['s5_w_glu_a'][0], W['s5_w_glu_b'][0]], axis=1)
    ab = matmul(y2, w_ab, name="s5_glu_in")
    mix = blocked_fwd(_glu, [rows(ab, tm, col=0, width=D), rows(ab, tm, col=1, width=D)], [_out((L, D), F32, tm)], L // tm,
                      "s5_glu")[0]
    return h + mix, (h, u, core, y, y2, ab)


def _mixer1_bwd(dout, saved, W, tm):
    h, u, core, y, y2, ab = saved
    L, D = h.shape
    da, db = blocked_bwd(_glu, [rows(ab, tm, 'blk', col=0, width=D), rows(ab, tm, 'blk', col=1, width=D)], [rows(dout, tm)],
                         L // tm, "s5_glu_bwd")
    w_a, w_b = W['s5_w_glu_a'][0], W['s5_w_glu_b'][0]
    dy2 = matmul(da, w_a, "nt", name="s5_dy2_a")
    dy2 = matmul(db, w_b, "nt", add=dy2, name="s5_dy2_b")
    d_wa = matmul(y2, da, "tn", name="s5_dwa")
    d_wb = matmul(y2, db, "tn", name="s5_dwb")
    d = W['s5_d']
    dy, du_skip, d_d = blocked_bwd(_s5_post, [rows(y, tm, 'blk'), rows(u, tm, 'blk'), full(d, 'acc')], [rows(dy2, tm)], L // tm,
                                   "s5_post_bwd")
    du, dwr, dwi, dar, dai, dcr, dci = s5_bwd(u, *core, dy, min(256, L))
    di = _s5_inputs(W)
    G = di[0].shape[0]
    cts = [dar.reshape(G, S5_STATE), dai.reshape(G, S5_STATE), _blockdiag_in_t(dwr).reshape(G, -1), _blockdiag_in_t(dwi).reshape(G, -1)]
    d_lr, d_li, d_ldt, d_br, d_bi = blocked_bwd(_s5_discretize, [full(a, 'acc') for a in di], [full(c) for c in cts], 1,
                                                "s5_discretize_bwd")
    dh, d_g = rms_bwd(h, _row(W['norm_mix'][1]), du + du_skip, tm, dout)
    bshape = W['s5_b_re'].shape
    grads = {'norm_mix': d_g, 's5_lam_re': d_lr[None], 's5_lam_im': d_li[None], 's5_log_dt': d_ldt.reshape(1, G),
             's5_b_re': d_br.reshape(bshape), 's5_b_im': d_bi.reshape(bshape), 's5_c_re': _blockdiag_out_t(dcr)[None],
             's5_c_im': _blockdiag_out_t(dci)[None], 's5_d': d_d, 's5_w_glu_a': d_wa.reshape(N_CHIPS, -1, D), 's5_w_glu_b': d_wb.reshape(N_CHIPS, -1, D)}
    return dh,grads


def kernel(x, mem, positions, norm_mix, norm_xa, norm_mem, norm_ffn, xa_wq, xa_wk, xa_wv, xa_wo, xa_q_norm, xa_k_norm, ffn_w_up, ffn_conv_w, ffn_conv_b, ffn_w_down, hg_lb_logits, mix_w_in, hg_out_norm, mla_q_a_norm, mla_w_uq, mla_kv_a_norm, mla_w_ukv, mla_qn_nope, mla_qn_rope, mla_kn_nope, mla_kn_rope, mix_w_out, s5_lam_re, s5_lam_im, s5_log_dt, s5_b_re, s5_b_im, s5_c_re, s5_c_im, s5_d, s5_w_glu_a, s5_w_glu_b, loss_target, m_norm_mix, m_norm_xa, m_norm_mem, m_norm_ffn, m_xa_wq, m_xa_wk, m_xa_wv, m_xa_wo, m_xa_q_norm, m_xa_k_norm, m_ffn_w_up, m_ffn_conv_w, m_ffn_conv_b, m_ffn_w_down, m_hg_lb_logits, m_mix_w_in, m_hg_out_norm, m_mla_q_a_norm, m_mla_w_uq, m_mla_kv_a_norm, m_mla_w_ukv, m_mla_qn_nope, m_mla_qn_rope, m_mla_kn_nope, m_mla_kn_rope, m_mix_w_out, m_s5_lam_re, m_s5_lam_im, m_s5_log_dt, m_s5_b_re, m_s5_b_im, m_s5_c_re, m_s5_c_im, m_s5_d, m_s5_w_glu_a, m_s5_w_glu_b, v_norm_mix, v_norm_xa, v_norm_mem, v_norm_ffn, v_xa_wq, v_xa_wk, v_xa_wv, v_xa_wo, v_xa_q_norm, v_xa_k_norm, v_ffn_w_up, v_ffn_conv_w, v_ffn_conv_b, v_ffn_w_down, v_hg_lb_logits, v_mix_w_in, v_hg_out_norm, v_mla_q_a_norm, v_mla_w_uq, v_mla_kv_a_norm, v_mla_w_ukv, v_mla_qn_nope, v_mla_qn_rope, v_mla_kn_nope, v_mla_kn_rope, v_mix_w_out, v_s5_lam_re, v_s5_lam_im, v_s5_log_dt, v_s5_b_re, v_s5_b_im, v_s5_c_re, v_s5_c_im, v_s5_d, v_s5_w_glu_a, v_s5_w_glu_b):
    P = dict(locals())
    assert sorted(P) == sorted(INPUTS) and norm_mix.shape[0] == 2 and mix_w_in.shape[0] == 1
    x, mem, target = P['x'][0], P['mem'][0], P['loss_target'][0]
    L, D = x.shape
    tm = min(256, L)

    W = {n: P[n] for n in REPLICATED}
    W.update(_gather_weights(P))

    inv_freq = 1.0 / (ROPE_BASE ** (jnp.arange(0, MLA_ROPE, 2, dtype=F32) / MLA_ROPE))
    ang = P['positions'][0].astype(F32)[:, None] * inv_freq
    cos, sin, z = jnp.cos(ang), jnp.sin(ang), jnp.zeros_like(ang)
    cos_p = jnp.concatenate([cos, z, cos, z], axis=1)
    sin_p = jnp.concatenate([-sin, z, sin, z], axis=1)

    h, s_mix0 = _mixer0_fwd(x, W, cos_p, sin_p, tm)
    h, s_xa0 = _xattn_fwd(h, mem, W, 0, tm)
    h, s_ffn0 = _ffn_fwd(h, W, 0, tm)
    h, s_mix1 = _mixer1_fwd(h, W, tm)
    h, s_xa1 = _xattn_fwd(h, mem, W, 1, tm)
    h, s_ffn1 = _ffn_fwd(h, W, 1, tm)
    n = L // tm
    dh, parts = blocked_fwd(_loss_fn, [rows(h, tm), rows(target, tm)],
                            [_out((L, D), F32, tm), ((n * 8, 128), F32, (8, 128), lambda i: (i, 0))], n, "loss")
    loss = lax.psum(jnp.sum(parts), ("x", "y", "c"))

    layered = {}

    def collect(g, lyr):
        for k_, v_ in g.items():
            layered.setdefault(k_, {})[lyr] = v_

    results = {}

    def big_items(lyr):
        return [(n_, 0 if P[n_].shape[0] == 1 else lyr, layered[n_][lyr]) for n_ in BIG if lyr in layered.get(n_, {})]

    dh, g = _ffn_bwd(dh, s_ffn1, W, 1, tm)
    collect(g, 1)
    dh, g = _xattn_bwd(dh, s_xa1, mem, W, 1, tm)
    collect(g, 1)
    dh, g = _mixer1_bwd(dh, s_mix1, W, tm)
    collect(g, 1)
    _reduce_batch(big_items(1), None, P, results, "late_layer", {'swap': 2, 'a2a': 3, 'join': 4})
    dh, g = _ffn_bwd(dh, s_ffn0, W, 0, tm)
    collect(g, 0)
    dh, g = _xattn_bwd(dh, s_xa0, mem, W, 0, tm)
    collect(g, 0)
    dx, g = _mixer0_bwd(dh, s_mix0, W, cos_p, sin_p, tm)
    collect(g, 0)

    GS = {}
    for name in SMALL:
        by_layer = [layered[name][lyr] for lyr in sorted(layered[name])]
        full_shape = W[name].shape
        GS[name] = (by_layer[0].reshape(full_shape) if len(by_layer) == 1
                    else jnp.stack([g_.reshape(full_shape[1:]) for g_ in by_layer]))
    small = _pack([GS[n_] for n_ in SMALL], 2 * N_CHIPS * ROW_MULT).reshape(N_CHIPS, -1, PACK_W)
    small_quarter = _reduce_batch(big_items(0), small, P, results, "first_layer", None)
    outs = list(_update_small(small_quarter, GS, P))
    for k_ in range(4):
        outs[k_].update({n_: results[n_][k_].reshape(P[n_].shape) for n_ in BIG})
    return (loss, dx[None], *[d[n_] for d in outs for n_ in WEIGHTS])
```

```python
import functools
import math

import jax
import jax.numpy as jnp
import numpy as np
from jax import lax
from jax.experimental import pallas as pl
from jax.experimental.pallas import tpu as pltpu
from jax.experimental.pallas import tpu_sc as plsc

F32 = jnp.float32
BF16 = jnp.bfloat16
MXU_DTYPE = BF16
HI = lax.Precision.HIGHEST
V7X_VMEM_LIMIT_BYTES = 56 * 1024 * 1024
EPS = 1e-6
MESH = pl.DeviceIdType.MESH

HG_HEADS, HG_DIM = 4, 128
HG_WIDTH = HG_HEADS * HG_DIM
HG_SUB = 32
HG_BLOCK = 64
MLA_HEADS, MLA_Q_RANK, MLA_KV_RANK = 4, 256, 128
MLA_NOPE, MLA_ROPE, MLA_V = 128, 64, 128
MLA_QK = MLA_NOPE + MLA_ROPE
MLA_DK = 256
ROPE_BASE = 10000.0
IN_WIDTH = 4 * HG_WIDTH + MLA_Q_RANK + MLA_KV_RANK + MLA_ROPE
IN_PAD = 4 * HG_WIDTH + MLA_Q_RANK + MLA_KV_RANK + 128
S5_GROUP, S5_STATE = 16, 64
S5_GB = 8
DT_MIN, DT_MAX = 1e-3, 1e-1
XA_HEADS = 4
CONV_W = 3
ADAM_LR, ADAM_B1, ADAM_B2, ADAM_EPS, ADAM_WD, ADAM_STEP = 0.001, 0.9, 0.999, 1e-08, 0.01, 10


def _cparams(sem):
    return pltpu.CompilerParams(dimension_semantics=sem, vmem_limit_bytes=V7X_VMEM_LIMIT_BYTES)


class Opd:
    def __init__(self, arr, block, imap, grad=None, gshape=None, gimap=None):
        self.arr, self.block, self.imap, self.grad = arr, block, imap, grad
        self.gshape = arr.shape if gshape is None else gshape
        self.gimap = imap if gimap is None else gimap

    def spec(self):
        return pl.BlockSpec(self.block, self.imap)

    def gspec(self):
        return pl.BlockSpec(self.block, self.gimap)


def rows(arr, tm, grad=None, col=0, width=None):
    width = arr.shape[1] if width is None else width
    return Opd(arr, (tm, width), lambda i, c=col: (i, c), grad, (arr.shape[0], width), lambda i: (i, 0))


def cols(arr, tn, grad=None):
    return Opd(arr, (arr.shape[0], tn), lambda j: (0, j), grad)


def full(arr, grad=None):
    return Opd(arr, arr.shape, lambda i: (0, 0), grad)


def _load(ref):
    v = ref[...]
    return v.astype(F32) if jnp.issubdtype(v.dtype, jnp.floating) else v


def blocked_fwd(f, opds, outs, n, name):
    n_in = len(opds)

    def body(*refs):
        ys = f(*[_load(r) for r in refs[:n_in]])
        for r, y in zip(refs[n_in:], ys):
            r[...] = y.astype(r.dtype)

    res = pl.pallas_call(
        body, name=name, grid=(n,),
        in_specs=[o.spec() for o in opds],
        out_specs=[pl.BlockSpec(b, m) for (_, _, b, m) in outs],
        out_shape=[jax.ShapeDtypeStruct(s, d) for (s, d, _, _) in outs],
        compiler_params=_cparams(("parallel",)),
    )(*[o.arr for o in opds])
    return res


def blocked_bwd(f, opds, dys, n, name, plus=None):
    n_in, n_dy = len(opds), len(dys)
    diff = [i for i, o in enumerate(opds) if o.grad]
    extra = [] if plus is None else [plus]

    def body(*refs):
        vals = [_load(r) for r in refs[:n_in]]

        def fd(*dv):
            allv = list(vals)
            for i, v in zip(diff, dv):
                allv[i] = v
            return tuple(f(*allv))

        ys, vjp = jax.vjp(fd, *[vals[i] for i in diff])
        cts = tuple(_load(r).astype(y.dtype) for r, y in zip(refs[n_in:n_in + n_dy], ys))
        gs = list(vjp(cts))
        if extra:
            gs[0] = gs[0] + _load(refs[n_in + n_dy])
        for r, g, i in zip(refs[n_in + n_dy + len(extra):], gs, diff):
            if opds[i].grad == 'acc':
                @pl.when(pl.program_id(0) == 0)
                def _(r=r):
                    r[...] = jnp.zeros(r.shape, r.dtype)
                r[...] += g.astype(r.dtype)
            else:
                r[...] = g.astype(r.dtype)

    any_acc = any(opds[i].grad == 'acc' for i in diff)
    res = pl.pallas_call(
        body, name=name, grid=(n,),
        in_specs=[o.spec() for o in opds + dys + extra],
        out_specs=[opds[i].gspec() for i in diff],
        out_shape=[jax.ShapeDtypeStruct(opds[i].gshape, F32) for i in diff],
        compiler_params=_cparams(("arbitrary" if any_acc else "parallel",)),
    )(*[o.arr for o in opds + dys + extra])
    return res


def _tile(dim, want):
    for t in range(want - want % 16, 0, -16):
        if dim % t == 0:
            return t
    assert dim <= want, (dim, want)
    return dim


MATMUL_VMEM_BUDGET = 40 * 1024 * 1024
MATMUL_ROWS = 512


def _widest(N, fits):
    for t in range(N - N % 128, 0, -128):
        if N % t == 0 and fits(t):
            return t
    return N


def matmul(a, b, mode="nn", out_dtype=F32, add=None, name="matmul", into=None, col_blocks=None):
    sa, sb, so = a.dtype.itemsize, b.dtype.itemsize, jnp.dtype(out_dtype).itemsize
    has_add = add is not None
    if mode == "tn":
        (K, M), (K2, N) = a.shape, b.shape
        assert K == K2 and not has_add and out_dtype == F32, (a.shape, b.shape)
        tk = _tile(K, MATMUL_ROWS)
        tn = _widest(N, lambda t: 2 * (tk * M * sa + tk * t * sb + M * t * 4) <= MATMUL_VMEM_BUDGET)
        extra, alias = [], {}
        if into is not None:
            buf, lead = into[0], tuple(into[1:])
            if col_blocks is not None:
                assert N % col_blocks == 0 and (N // col_blocks) % 128 == 0 and tn >= N // col_blocks, (N, col_blocks, tn)
                tn = N // col_blocks
                assert buf.shape[len(lead):] == (M, tn), (buf.shape, lead, M, tn)
                out_spec = pl.BlockSpec((None,) * len(lead) + (M, tn), lambda j, k: lead[:-1] + (lead[-1] + j, 0, 0))
            else:
                assert buf.shape[len(lead):] == (M, N), (buf.shape, lead, M, N)
                out_spec = pl.BlockSpec((None,) * len(lead) + (M, tn), lambda j, k: lead + (0, j))
            out_shape = jax.ShapeDtypeStruct(buf.shape, F32)
            extra, alias = [buf], {2: 0}
        else:
            assert col_blocks is None
            out_spec = pl.BlockSpec((M, tn), lambda j, k: (0, j))
            out_shape = jax.ShapeDtypeStruct((M, N), F32)

        def body(a_ref, b_ref, *rest):
            o_ref = rest[-1]
            r = lax.dot_general(a_ref[...].astype(MXU_DTYPE), b_ref[...].astype(MXU_DTYPE), ((_TN), ((), ())),
                                preferred_element_type=F32)

            @pl.when(pl.program_id(1) == 0)
            def _():
                o_ref[...] = r

            @pl.when(pl.program_id(1) > 0)
            def _():
                o_ref[...] += r

        return pl.pallas_call(
            body, name=name, grid=(N // tn, K // tk),
            in_specs=[pl.BlockSpec((tk, M), lambda j, k: (k, 0)), pl.BlockSpec((tk, tn), lambda j, k: (k, j))]
            + [pl.BlockSpec(memory_space=pl.ANY)] * len(extra),
            out_specs=out_spec, out_shape=out_shape, input_output_aliases=alias,
            compiler_params=_cparams(("parallel", "arbitrary")),
        )(a, b, *extra)

    (M, K) = a.shape
    N = b.shape[1] if mode == "nn" else b.shape[0]
    assert K == (b.shape[0] if mode == "nn" else b.shape[1]), (a.shape, b.shape, mode)
    tm = _tile(M, MATMUL_ROWS)
    tn = _widest(N, lambda t: 2 * (tm * K * sa + K * t * sb + tm * t * (so + 4 * has_add)) <= MATMUL_VMEM_BUDGET)
    dims = ((_NN if mode == "nn" else _NT), ((), ()))

    def body(*refs):
        r = lax.dot_general(refs[0][...].astype(MXU_DTYPE), refs[1][...].astype(MXU_DTYPE), dims, preferred_element_type=F32)
        if has_add:
            r = r + refs[2][...].astype(F32)
        refs[-1][...] = r.astype(refs[-1].dtype)

    b_spec = pl.BlockSpec((K, tn), lambda j, i: (0, j)) if mode == "nn" else pl.BlockSpec((tn, K), lambda j, i: (j, 0))
    in_specs = [pl.BlockSpec((tm, K), lambda j, i: (i, 0)), b_spec]
    args = [a, b]
    if has_add:
        in_specs.append(pl.BlockSpec((tm, tn), lambda j, i: (i, j)))
        args.append(add)
    return pl.pallas_call(
        body, name=name, grid=(N // tn, M // tm),
        in_specs=in_specs,
        out_specs=pl.BlockSpec((tm, tn), lambda j, i: (i, j)),
        out_shape=jax.ShapeDtypeStruct((M, N), out_dtype),
        compiler_params=_cparams(("parallel", "parallel")),
    )(*args)


def _dot(a, b, dims, precision=None):
    if precision is None:
        a, b = a.astype(MXU_DTYPE), b.astype(MXU_DTYPE)
    return lax.dot_general(a, b, (dims, ((), ())), precision=precision, preferred_element_type=F32)


_NN = ((1,), (0,))
_NT = ((1,), (1,))
_TN = ((0,), (0,))


def _rms(x, gain):
    return x * lax.rsqrt(jnp.mean(x * x, axis=-1, keepdims=True) + EPS) * gain


def _hg_block(st_t, q, fl, iv, g, lb, gain):
    row = lax.broadcasted_iota(jnp.int32, (HG_SUB, HG_SUB), 0)
    col = lax.broadcasted_iota(jnp.int32, (HG_SUB, HG_SUB), 1)
    tri = (row >= col).astype(F32)
    outs, states = [], []
    for h in range(HG_HEADS):
        sl = slice(h * HG_DIM, (h + 1) * HG_DIM)
        st = st_t[h * HG_DIM:(h + 1) * HG_DIM, :]
        lbh = lb[:, sl]
        fg = lbh + (1.0 - lbh) * jax.nn.sigmoid(fl[:, sl])
        lf, kk, qf, v = jnp.log(fg), 1.0 - fg, jax.nn.silu(q[:, sl]), iv[:, sl]
        parts = []
        for s in range(q.shape[0] // HG_SUB):
            r = slice(s * HG_SUB, (s + 1) * HG_SUB)
            b = _dot(tri, lf[r], _NN, HI)
            b_mid = jnp.sum(lf[r][:HG_SUB // 2], axis=0, keepdims=True)
            b_end = jnp.sum(lf[r], axis=0, keepdims=True)
            sc = _dot(qf[r] * jnp.exp(b - b_mid), kk[r] * jnp.exp(b_mid - b), _NT) * tri
            parts.append(_dot(sc, v[r], _NN) + _dot(qf[r] * jnp.exp(b), st, _NT))
            st = st * jnp.exp(b_end) + _dot(v[r], kk[r] * jnp.exp(b_end - b), _TN)
        o = jnp.concatenate(parts, axis=0)
        outs.append(_rms(o, gain[:, sl]) * jax.nn.silu(g[:, sl]))
        states.append(st)
    return jnp.concatenate(states, axis=0), jnp.concatenate(outs, axis=1)


def _hg_specs(proj, nb):
    return [pl.BlockSpec((HG_BLOCK, HG_WIDTH), lambda i, c=c, f=nb: (f(i), c)) for c in range(4)]


def hgrn2_fwd(proj, lb, gain):
    L = proj.shape[0]
    n = L // HG_BLOCK

    def body(q, fl, iv, g, lb_r, gain_r, o_ref, st_ref, st):
        @pl.when(pl.program_id(0) == 0)
        def _():
            st[...] = jnp.zeros(st.shape, F32)

        st_ref[0] = st[...]
        new, o = _hg_block(st[...], q[...], fl[...], iv[...], g[...], lb_r[...], gain_r[...])
        st[...] = new
        o_ref[...] = o.astype(o_ref.dtype)

    pspec = pl.BlockSpec((1, HG_WIDTH), lambda i: (0, 0))
    return pl.pallas_call(
        body, name="hgrn2_fwd", grid=(n,),
        in_specs=_hg_specs(proj, lambda i: i) + [pspec, pspec],
        out_specs=[pl.BlockSpec((HG_BLOCK, HG_WIDTH), lambda i: (i, 0)),
                   pl.BlockSpec((1, HG_WIDTH, HG_DIM), lambda i: (i, 0, 0))],
        out_shape=[jax.ShapeDtypeStruct((L, HG_WIDTH), MXU_DTYPE),
                   jax.ShapeDtypeStruct((n, HG_WIDTH, HG_DIM), F32)],
        scratch_shapes=[pltpu.VMEM((HG_WIDTH, HG_DIM), F32)],
        compiler_params=_cparams(("arbitrary",)),
    )(proj, proj, proj, proj, lb, gain)


def hgrn2_bwd(proj, lb, gain, states, do):
    L = proj.shape[0]
    n = L // HG_BLOCK

    def body(q, fl, iv, g, lb_r, gain_r, st_r, do_r, dproj, dlb, dgain, dst):
        @pl.when(pl.program_id(0) == 0)
        def _():
            dst[...] = jnp.zeros(dst.shape, F32)
            dlb[...] = jnp.zeros(dlb.shape, F32)
            dgain[...] = jnp.zeros(dgain.shape, F32)

        _, vjp = jax.vjp(_hg_block, st_r[0], q[...], fl[...], iv[...], g[...], lb_r[...], gain_r[...])
        d_st, dq, dfl, div, dg, d_lb, d_gain = vjp((dst[...], do_r[...].astype(F32)))
        dst[...] = d_st
        dproj[:, 0 * HG_WIDTH:1 * HG_WIDTH] = dq
        dproj[:, 1 * HG_WIDTH:2 * HG_WIDTH] = dfl
        dproj[:, 2 * HG_WIDTH:3 * HG_WIDTH] = div
        dproj[:, 3 * HG_WIDTH:4 * HG_WIDTH] = dg
        dlb[...] += d_lb
        dgain[...] += d_gain

    rev = lambda i: n - 1 - i
    pspec = pl.BlockSpec((1, HG_WIDTH), lambda i: (0, 0))
    return pl.pallas_call(
        body, name="hgrn2_bwd", grid=(n,),
        in_specs=_hg_specs(proj, rev) + [pspec, pspec,
                                         pl.BlockSpec((1, HG_WIDTH, HG_DIM), lambda i: (rev(i), 0, 0)),
                                         pl.BlockSpec((HG_BLOCK, HG_WIDTH), lambda i: (rev(i), 0))],
        out_specs=[pl.BlockSpec((HG_BLOCK, 4 * HG_WIDTH), lambda i: (rev(i), 0)), pspec, pspec],
        out_shape=[jax.ShapeDtypeStruct((L, 4 * HG_WIDTH), F32),
                   jax.ShapeDtypeStruct((1, HG_WIDTH), F32), jax.ShapeDtypeStruct((1, HG_WIDTH), F32)],
        scratch_shapes=[pltpu.VMEM((HG_WIDTH, HG_DIM), F32)],
        compiler_params=_cparams(("arbitrary",)),
    )(proj, proj, proj, proj, lb, gain, states, do)


def _rope_rms(x, gain_p, cos_p, sin_p):
    n = x * lax.rsqrt(jnp.sum(x * x, axis=-1, keepdims=True) * (1.0 / MLA_ROPE) + EPS) * gain_p
    r = lax.broadcasted_iota(jnp.int32, (128, 128), 0)
    c = lax.broadcasted_iota(jnp.int32, (128, 128), 1)
    swap = (r == (c + 64) % 128).astype(F32)
    return n * cos_p + _dot(n, swap, _NN, HI) * sin_p


MLA_IN = MLA_Q_RANK + MLA_KV_RANK + 128


def _mla_prep(x, cos_p, sin_p, q_a, w_uq, kv_a, w_ukv, qn_nope, qn_rope, kn_nope, kn_rope):
    c_q, c_kv, kpe = x[:, :MLA_Q_RANK], x[:, MLA_Q_RANK:MLA_Q_RANK + MLA_KV_RANK], x[:, MLA_Q_RANK + MLA_KV_RANK:]
    q = _dot(_rms(c_q, q_a), w_uq, _NN)
    kv = _dot(_rms(c_kv, kv_a), w_ukv, _NN)
    k_pe = _rope_rms(kpe, kn_rope, cos_p, sin_p)
    qs, ks = [], []
    for h in range(MLA_HEADS):
        qs.append(_rms(q[:, h * MLA_DK:h * MLA_DK + MLA_NOPE], qn_nope))
        qs.append(_rope_rms(q[:, h * MLA_DK + MLA_NOPE:(h + 1) * MLA_DK], qn_rope, cos_p, sin_p))
        ks.append(_rms(kv[:, h * MLA_NOPE:(h + 1) * MLA_NOPE], kn_nope))
        ks.append(k_pe)
    return jnp.concatenate(qs, axis=1), jnp.concatenate(ks, axis=1), kv[:, MLA_HEADS * MLA_NOPE:]


def _mla_prep_opds(proj, cos_p, sin_p, params, tm, grads):
    g = (lambda k: k) if grads else (lambda k: None)
    assert (4 * HG_WIDTH) % MLA_IN == 0
    return ([rows(proj, tm, g('blk'), col=4 * HG_WIDTH // MLA_IN, width=MLA_IN), rows(cos_p, tm), rows(sin_p, tm)]
            + [full(p, g('acc')) for p in params])


def mla_prep_fwd(proj, cos_p, sin_p, params, tm):
    L = proj.shape[0]
    W = MLA_HEADS * MLA_DK
    rb = lambda w: (tm, w)
    outs = [((L, W), MXU_DTYPE, rb(W), lambda i: (i, 0)), ((L, W), MXU_DTYPE, rb(W), lambda i: (i, 0)),
            ((L, MLA_HEADS * MLA_V), MXU_DTYPE, rb(MLA_HEADS * MLA_V), lambda i: (i, 0))]
    return blocked_fwd(_mla_prep, _mla_prep_opds(proj, cos_p, sin_p, params, tm, False), outs, L // tm, "mla_prep_fwd")


def mla_prep_bwd(proj, cos_p, sin_p, params, dq, dk, dv, tm):
    L = proj.shape[0]
    return blocked_bwd(_mla_prep, _mla_prep_opds(proj, cos_p, sin_p, params, tm, True),
                       [rows(dq, tm), rows(dk, tm), rows(dv, tm)], L // tm, "mla_prep_bwd")


def _scores(q, k, scale, shift=None):
    s = _dot(q, k, _NT) * scale
    if shift is None:
        return s
    row = lax.broadcasted_iota(jnp.int32, s.shape, 0)
    col = lax.broadcasted_iota(jnp.int32, s.shape, 1)
    return jnp.where(col <= row + shift, s, -jnp.inf)


ATTN_ROWS = 512
ATTN_WIDE = 2


def attn_fwd(q, k, v, scale, t):
    L = q.shape[0]
    tq = ATTN_WIDE * t

    def body(q_ref, k_ref, v_ref, o_ref, lse_ref):
        i = pl.program_id(1)
        qb = q_ref[...]

        def step(j, carry, shift=None):
            m, l, acc = carry
            kj = k_ref[pl.ds(pl.multiple_of(j * t, t), t), :]
            vj = v_ref[pl.ds(pl.multiple_of(j * t, t), t), :]
            s = _scores(qb, kj, scale, shift)
            m_new = jnp.maximum(m, jnp.max(s, axis=-1, keepdims=True))
            p = jnp.exp(s - m_new)
            alpha = jnp.exp(m - m_new)
            return m_new, alpha * l + jnp.sum(p, axis=-1, keepdims=True), alpha * acc + _dot(p, vj, _NN)

        carry = (jnp.full((tq, 1), -jnp.inf, F32), jnp.zeros((tq, 1), F32), jnp.zeros((tq, MLA_V), F32))
        carry = lax.fori_loop(0, ATTN_WIDE * i, step, carry)
        for d in range(ATTN_WIDE):
            carry = step(ATTN_WIDE * i + d, carry, -d * t)
        m, l, acc = carry
        o_ref[...] = acc / l
        lse_ref[...] = jnp.broadcast_to(m + jnp.log(l), lse_ref.shape)

    hspec = lambda rows_, w: pl.BlockSpec((rows_, w), lambda h, i: (0, h))
    bspec = lambda w: pl.BlockSpec((tq, w), lambda h, i: (i, h))
    return pl.pallas_call(
        body, name="attn_fwd", grid=(MLA_HEADS, L // tq),
        in_specs=[bspec(MLA_DK), hspec(L, MLA_DK), hspec(L, MLA_V)],
        out_specs=[bspec(MLA_V), bspec(MLA_V)],
        out_shape=[jax.ShapeDtypeStruct((L, MLA_HEADS * MLA_V), F32)] * 2,
        compiler_params=_cparams(("parallel", "parallel")),
    )(q, k, v)


def attn_bwd_dq(q, k, v, o, lse, do, scale, t):
    L = q.shape[0]
    tq = ATTN_WIDE * t

    def body(q_ref, k_ref, v_ref, o_ref, lse_ref, do_ref, dq_ref):
        i = pl.program_id(1)
        qb, dob = q_ref[...], do_ref[...]
        delta = jnp.sum(dob * o_ref[...], axis=-1, keepdims=True)
        lse_c = jnp.max(lse_ref[...], axis=-1, keepdims=True)

        def step(j, dq, shift=None):
            kj = k_ref[pl.ds(pl.multiple_of(j * t, t), t), :]
            vj = v_ref[pl.ds(pl.multiple_of(j * t, t), t), :]
            p = jnp.exp(_scores(qb, kj, scale, shift) - lse_c)
            ds = p * (_dot(dob, vj, _NT) - delta) * scale
            return dq + _dot(ds, kj, _NN)

        dq = lax.fori_loop(0, ATTN_WIDE * i, step, jnp.zeros((tq, MLA_DK), F32))
        for d in range(ATTN_WIDE):
            dq = step(ATTN_WIDE * i + d, dq, -d * t)
        dq_ref[...] = dq

    hspec = lambda w: pl.BlockSpec((L, w), lambda h, i: (0, h))
    bspec = lambda w: pl.BlockSpec((tq, w), lambda h, i: (i, h))
    return pl.pallas_call(
        body, name="attn_bwd_dq", grid=(MLA_HEADS, L // tq),
        in_specs=[bspec(MLA_DK), hspec(MLA_DK), hspec(MLA_V), bspec(MLA_V), bspec(MLA_V), bspec(MLA_V)],
        out_specs=bspec(MLA_DK),
        out_shape=jax.ShapeDtypeStruct((L, MLA_HEADS * MLA_DK), F32),
        compiler_params=_cparams(("parallel", "parallel")),
    )(q, k, v, o, lse, do)


def attn_bwd_dkv(q, k, v, o, lse, do, scale, t):
    L = q.shape[0]
    tk = ATTN_WIDE * t

    def body(q_ref, k_ref, v_ref, o_ref, lse_ref, do_ref, dk_ref, dv_ref):
        j = pl.program_id(1)
        kb, vb = k_ref[...], v_ref[...]

        def step(i, carry, shift=None):
            dk, dv = carry
            r = pl.ds(pl.multiple_of(i * t, t), t)
            qi, doi = q_ref[r, :], do_ref[r, :]
            delta = jnp.sum(doi * o_ref[r, :], axis=-1, keepdims=True)
            lse_c = jnp.max(lse_ref[r, :], axis=-1, keepdims=True)
            p = jnp.exp(_scores(qi, kb, scale, shift) - lse_c)
            ds = p * (_dot(doi, vb, _NT) - delta) * scale
            return dk + _dot(ds, qi, _TN), dv + _dot(p, doi, _TN)

        carry = (jnp.zeros((tk, MLA_DK), F32), jnp.zeros((tk, MLA_V), F32))
        for d in range(ATTN_WIDE):
            carry = step(ATTN_WIDE * j + d, carry, d * t)
        dk, dv = lax.fori_loop(ATTN_WIDE * (j + 1), L // t, step, carry)
        dk_ref[...] = dk
        dv_ref[...] = dv

    hspec = lambda w: pl.BlockSpec((L, w), lambda h, j: (0, h))
    bspec = lambda w: pl.BlockSpec((tk, w), lambda h, j: (j, h))
    return pl.pallas_call(
        body, name="attn_bwd_dkv", grid=(MLA_HEADS, L // tk),
        in_specs=[hspec(MLA_DK), bspec(MLA_DK), bspec(MLA_V), hspec(MLA_V), hspec(MLA_V), hspec(MLA_V)],
        out_specs=[bspec(MLA_DK), bspec(MLA_V)],
        out_shape=[jax.ShapeDtypeStruct((L, MLA_HEADS * MLA_DK), F32), jax.ShapeDtypeStruct((L, MLA_HEADS * MLA_V), F32)],
        compiler_params=_cparams(("parallel", "parallel")),
    )(q, k, v, o, lse, do)


S5_LANES = S5_GB * S5_STATE


def _cmul(ar, ai, br, bi):
    return ar * br - ai * bi, ar * bi + ai * br


def _a_powers(ar, ai, reverse):
    a2 = _cmul(ar, ai, ar, ai)
    a4 = _cmul(*a2, *a2)
    row = lax.broadcasted_iota(jnp.int32, (8, ar.shape[1]), 0)
    e = (8 - row) if reverse else (row + 1)
    tr, ti = jnp.ones((8, ar.shape[1]), F32), jnp.zeros((8, ar.shape[1]), F32)
    for bit, (pr, pi) in ((1, (ar, ai)), (2, a2), (4, a4), (8, _cmul(*a4, *a4))):
        nr, ni = _cmul(tr, ti, pr, pi)
        sel = (e & bit) != 0
        tr, ti = jnp.where(sel, nr, tr), jnp.where(sel, ni, ti)
    pows = []
    for d, (pr, pi) in zip((1, 2, 4), ((ar, ai), a2, a4)):
        keep = (row < 8 - d) if reverse else (row >= d)
        pows.append((jnp.where(keep, pr, 0.0), jnp.where(keep, pi, 0.0)))
    return pows, (tr, ti)


def _scan8(xr, xi, pows, table, cr, ci, reverse):
    for d, (pr, pi) in zip((1, 2, 4), pows):
        shift = 8 - d if reverse else d
        mr, mi = _cmul(pr, pi, pltpu.roll(xr, shift, 0), pltpu.roll(xi, shift, 0))
        xr, xi = xr + mr, xi + mi
    mr, mi = _cmul(table[0], table[1], cr, ci)
    return xr + mr, xi + mi


def _row_of(x, r):
    row = lax.broadcasted_iota(jnp.int32, x.shape, 0)
    return jnp.sum(jnp.where(row == r, x, 0.0), axis=0, keepdims=True)


def _s5_scan_fwd(h_re, h_im, ar, ai, L):
    pows, table = _a_powers(ar, ai, False)

    def step(i, carry):
        r = pl.ds(pl.multiple_of(i * 8, 8), 8)
        xr, xi = _scan8(h_re[r, :], h_im[r, :], pows, table, carry[0], carry[1], False)
        h_re[r, :] = xr
        h_im[r, :] = xi
        return xr[7:8, :], xi[7:8, :]

    z = jnp.zeros((1, ar.shape[1]), F32)
    lax.fori_loop(0, L // 8, step, (z, z))


def _s5_specs(L):
    return [pl.BlockSpec((L, 128), lambda g: (0, g)),
            pl.BlockSpec((1, 128, S5_LANES), lambda g: (g, 0, 0)), pl.BlockSpec((1, 128, S5_LANES), lambda g: (g, 0, 0)),
            pl.BlockSpec((1, 1, S5_LANES), lambda g: (g, 0, 0)), pl.BlockSpec((1, 1, S5_LANES), lambda g: (g, 0, 0)),
            pl.BlockSpec((1, S5_LANES, 128), lambda g: (g, 0, 0)), pl.BlockSpec((1, S5_LANES, 128), lambda g: (g, 0, 0))]


def s5_fwd(u, w_re, w_im, a_re, a_im, c_re, c_im):
    L, D = u.shape

    def body(u_ref, wr, wi, ar, ai, cr, ci, y_ref, h_re, h_im):
        ub = u_ref[...]
        h_re[...] = _dot(ub, wr[0], _NN)
        h_im[...] = _dot(ub, wi[0], _NN)
        _s5_scan_fwd(h_re, h_im, ar[0], ai[0], L)
        y_ref[...] = _dot(h_re[...], cr[0], _NN) - _dot(h_im[...], ci[0], _NN)

    return pl.pallas_call(
        body, name="s5_fwd", grid=(D // 128,),
        in_specs=_s5_specs(L), out_specs=pl.BlockSpec((L, 128), lambda g: (0, g)),
        out_shape=jax.ShapeDtypeStruct((L, D), F32),
        scratch_shapes=[pltpu.VMEM((L, S5_LANES), F32), pltpu.VMEM((L, S5_LANES), F32)],
        compiler_params=_cparams(("parallel",)),
    )(u, w_re, w_im, a_re, a_im, c_re, c_im)


def s5_bwd(u, w_re, w_im, a_re, a_im, c_re, c_im, dy, tc):
    L, D = u.shape
    nch = L // tc

    def body(u_ref, wr, wi, ar_ref, ai_ref, cr, ci, dy_ref, du_ref, dwr, dwi, dar, dai, dcr, dci, h_re, h_im, g_re, g_im):
        ar, ai = ar_ref[0], ai_ref[0]
        ub = u_ref[...]
        h_re[...] = _dot(ub, wr[0], _NN)
        h_im[...] = _dot(ub, wi[0], _NN)
        _s5_scan_fwd(h_re, h_im, ar, ai, L)
        dyb = dy_ref[...]
        dcr[0] = _dot(h_re[...], dyb, _TN)
        dci[0] = -_dot(h_im[...], dyb, _TN)
        pows, table = _a_powers(ar, -ai, True)
        dwr[0] = jnp.zeros((128, S5_LANES), F32)
        dwi[0] = jnp.zeros((128, S5_LANES), F32)
        z1 = jnp.zeros((1, S5_LANES), F32)
        z8 = jnp.zeros((8, S5_LANES), F32)

        def chunk(cc, carry):
            c0 = pl.multiple_of((nch - 1 - cc) * tc, tc)
            rows_c = pl.ds(c0, tc)
            dyc = dy_ref[rows_c, :]
            g_re[...] = _dot(dyc, cr[0], _NT)
            g_im[...] = -_dot(dyc, ci[0], _NT)

            def step(ii, cy):
                gr_c, gi_c, acc_r, acc_i = cy
                i8 = pl.multiple_of((tc // 8 - 1 - ii) * 8, 8)
                rl = pl.ds(i8, 8)
                xr, xi = _scan8(g_re[rl, :], g_im[rl, :], pows, table, gr_c, gi_c, True)
                g_re[rl, :] = xr
                g_im[rl, :] = xi
                t0 = c0 + i8
                hb_r, hb_i = h_re[pl.ds(t0, 8), :], h_im[pl.ds(t0, 8), :]
                tp = pl.multiple_of(jnp.maximum(t0 - 8, 0), 8)
                first = (t0 > 0).astype(F32)
                pr = h_re[pl.ds(tp, 8), :][7:8, :] * first
                pi = h_im[pl.ds(tp, 8), :][7:8, :] * first
                row = lax.broadcasted_iota(jnp.int32, xr.shape, 0)
                hp_r = jnp.where(row == 0, pr, pltpu.roll(hb_r, 1, 0))
                hp_i = jnp.where(row == 0, pi, pltpu.roll(hb_i, 1, 0))
                return (xr[0:1, :], xi[0:1, :],
                        acc_r + xr * hp_r + xi * hp_i, acc_i + xi * hp_r - xr * hp_i)

            cy = lax.fori_loop(0, tc // 8, step, carry)
            uc = u_ref[rows_c, :]
            gr, gi = g_re[...], g_im[...]
            du_ref[rows_c, :] = _dot(gr, wr[0], _NT) + _dot(gi, wi[0], _NT)
            dwr[0] += _dot(uc, gr, _TN)
            dwi[0] += _dot(uc, gi, _TN)
            return cy

        _, _, acc_r, acc_i = lax.fori_loop(0, nch, chunk, (z1, z1, z8, z8))
        dar[0] = jnp.sum(acc_r, axis=0, keepdims=True)
        dai[0] = jnp.sum(acc_i, axis=0, keepdims=True)

    specs = _s5_specs(L)
    return pl.pallas_call(
        body, name="s5_bwd", grid=(D // 128,),
        in_specs=specs + [pl.BlockSpec((L, 128), lambda g: (0, g))],
        out_specs=[pl.BlockSpec((L, 128), lambda g: (0, g))] + specs[1:],
        out_shape=[jax.ShapeDtypeStruct((L, D), F32)] + [jax.ShapeDtypeStruct(x.shape, F32)
                                                        for x in (w_re, w_im, a_re, a_im, c_re, c_im)],
        scratch_shapes=[pltpu.VMEM((L, S5_LANES), F32), pltpu.VMEM((L, S5_LANES), F32),
                        pltpu.VMEM((tc, S5_LANES), F32), pltpu.VMEM((tc, S5_LANES), F32)],
        compiler_params=_cparams(("parallel",)),
    )(u, w_re, w_im, a_re, a_im, c_re, c_im, dy)


def _s5_discretize(lr, li, ldt, br, bi):
    dt = jnp.exp(ldt)
    mag = jnp.exp(lr * dt)
    ar, ai = mag * jnp.cos(li * dt), mag * jnp.sin(li * dt)
    den = lr * lr + li * li
    zr = ((ar - 1.0) * lr + ai * li) / den
    zi = (ai * lr - (ar - 1.0) * li) / den
    p = lax.broadcasted_iota(jnp.int32, (S5_STATE, S5_STATE * S5_GROUP), 0)
    c = lax.broadcasted_iota(jnp.int32, (S5_STATE, S5_STATE * S5_GROUP), 1)
    rep = (c // S5_GROUP == p).astype(F32)
    zr, zi = _dot(zr, rep, _NN, HI), _dot(zi, rep, _NN, HI)
    return ar, ai, zr * br - zi * bi, zr * bi + zi * br


def _conv_shift(x, d):
    row = lax.broadcasted_iota(jnp.int32, x.shape, 0)
    return jnp.where(row >= d, pltpu.roll(x, d, 0), 0.0)


def _conv_unshift(x, d):
    n = x.shape[0]
    row = lax.broadcasted_iota(jnp.int32, x.shape, 0)
    return jnp.where(row < n - d, pltpu.roll(x, n - d, 0), 0.0)


@functools.partial(jax.custom_vjp, nondiff_argnums=(1,))
def _shift_rows(x, d):
    return _conv_shift(x, d)


_shift_rows.defvjp(lambda x, d: (_conv_shift(x, d), None), lambda d, _, g: (_conv_unshift(g, d),))


def _conv_gate(ug, uv, wg0, wg1, wg2, wv0, wv1, wv2, bg, bv):
    def conv(u, w0, w1, w2, b):
        return u * w2 + _shift_rows(u, 1) * w1 + _shift_rows(u, 2) * w0 + b
    return (jax.nn.silu(conv(ug, wg0, wg1, wg2, bg)) * conv(uv, wv0, wv1, wv2, bv),)


def _rms_fn(x, gain):
    return (_rms(x, gain),)


def _softmax_rows(s):
    e = jnp.exp(s - lax.stop_gradient(jnp.max(s, axis=-1, keepdims=True)))
    return e / jnp.sum(e, axis=-1, keepdims=True)


def _xa_core(qp, k, v, q_gain):
    dh = qp.shape[1] // XA_HEADS
    outs = []
    for h in range(XA_HEADS):
        sl = slice(h * dh, (h + 1) * dh)
        p = _softmax_rows(_dot(_rms(qp[:, sl], q_gain), k[:, sl], _NT) * (dh ** -0.5))
        outs.append(_dot(p, v[:, sl], _NN))
    return (jnp.concatenate(outs, axis=1),)


def _mem_kv(mem, mem_gain, wk, wv, k_gain):
    m = _rms(mem, mem_gain)
    kp = _dot(m, wk, _NN)
    dh = kp.shape[1] // XA_HEADS
    k = jnp.concatenate([_rms(kp[:, h * dh:(h + 1) * dh], k_gain) for h in range(XA_HEADS)], axis=1)
    return k, _dot(m, wv, _NN)


def _s5_post(y, u, d):
    return (jax.nn.gelu(y + d * u),)


def _glu(a, b):
    return (a * jax.nn.sigmoid(b),)


def _lb_first(logits):
    e = jnp.exp(logits - lax.stop_gradient(jnp.max(logits, axis=0, keepdims=True)))
    return (_row_of(e, 0) / jnp.sum(e, axis=0, keepdims=True),)


def _loss_fn(y, t):
    e = y - t
    part = 0.5 * jnp.sum(e * e) / y.shape[1]
    return e * (1.0 / y.shape[1]), jnp.full((8, 128), part / (8 * 128), F32)


def _out(shape, dtype, tm):
    return (shape, dtype, (tm, shape[1]), lambda i: (i, 0))


def rms_fwd(h, gain, tm, dtype):
    return blocked_fwd(_rms_fn, [rows(h, tm), full(gain)], [_out(h.shape, dtype, tm)], h.shape[0] // tm, "rms_fwd")[0]


def rms_bwd(h, gain, dy, tm, residual):
    return blocked_bwd(_rms_fn, [rows(h, tm, 'blk'), full(gain, 'acc')], [rows(dy, tm)], h.shape[0] // tm, "rms_bwd",
                       plus=rows(residual, tm))


def _adamw_math(w, g, m, v):
    m = ADAM_B1 * m + (1.0 - ADAM_B1) * g
    v = ADAM_B2 * v + (1.0 - ADAM_B2) * jnp.square(g)
    m_hat = m / (1.0 - ADAM_B1 ** ADAM_STEP)
    v_hat = v / (1.0 - ADAM_B2 ** ADAM_STEP)
    return -ADAM_LR * (m_hat / (jnp.sqrt(v_hat) + ADAM_EPS) + ADAM_WD * w), m, v


def adamw(w, g, m, v, name):
    R = w.shape[0]
    tm = _tile(R, 256)
    assert g.shape == w.shape == m.shape == v.shape, (name, w.shape, g.shape)

    def body(w_ref, g_ref, m_ref, v_ref, d_ref, nm_ref, nv_ref):
        d_ref[...], nm_ref[...], nv_ref[...] = _adamw_math(w_ref[...], g_ref[...], m_ref[...], v_ref[...])

    spec = pl.BlockSpec((tm, w.shape[1]), lambda i: (i, 0))
    return pl.pallas_call(
        body, name=name, grid=(R // tm,), in_specs=[spec] * 4, out_specs=[spec] * 3,
        out_shape=[jax.ShapeDtypeStruct(w.shape, F32)] * 3, compiler_params=_cparams(("parallel",)),
    )(w, g, m, v)


def adamw_layer(w, g, m, v, layer, bufs, name):
    R, C = g.shape
    tm = _tile(R, 256)
    assert w.shape[1:] == (R, C) and all(b.shape == w.shape for b in bufs), (name, w.shape, g.shape)

    def body(w_ref, g_ref, m_ref, v_ref, *rest):
        g_out, d_ref, nm_ref, nv_ref = rest[-4:]
        g_ = g_ref[...]
        g_out[...] = g_
        d_ref[...], nm_ref[...], nv_ref[...] = _adamw_math(w_ref[...], g_, m_ref[...], v_ref[...])

    lspec = pl.BlockSpec((None, tm, C), lambda i: (layer, i, 0))
    any_spec = pl.BlockSpec(memory_space=pl.ANY)
    return pl.pallas_call(
        body, name=name, grid=(R // tm,),
        in_specs=[lspec, pl.BlockSpec((tm, C), lambda i: (i, 0)), lspec, lspec] + [any_spec] * 4,
        out_specs=[lspec] * 4, out_shape=[jax.ShapeDtypeStruct(w.shape, F32)] * 4,
        input_output_aliases={4: 0, 5: 1, 6: 2, 7: 3}, compiler_params=_cparams(("parallel",)),
    )(w, g, m, v, *bufs)


def add2(x, y, name, out_dtype=F32):
    shape = x.shape
    x, y = x.reshape(-1, shape[-1]), y.reshape(-1, shape[-1])
    R, C = x.shape
    tm = _tile(R, 256)

    def body(x_ref, y_ref, o_ref):
        o_ref[...] = (x_ref[...] + y_ref[...]).astype(o_ref.dtype)

    spec = pl.BlockSpec((tm, C), lambda i: (i, 0))
    return pl.pallas_call(
        body, name=name, grid=(R // tm,), in_specs=[spec, spec], out_specs=spec,
        out_shape=jax.ShapeDtypeStruct((R, C), out_dtype), compiler_params=_cparams(("parallel",)),
    )(x, y).reshape(shape)


def add_chips(own, got, name):
    n, R, C = got.shape
    tm = _tile(R, 256)

    def body(*refs):
        acc = refs[0][...].astype(F32)
        for r in refs[1:-1]:
            acc = acc + r[...].astype(F32)
        refs[-1][...] = acc

    return pl.pallas_call(
        body, name=name, grid=(R // tm,),
        in_specs=[pl.BlockSpec((tm, C), lambda i: (i, 0))] + [pl.BlockSpec((None, tm, C), lambda i, j=j: (j, i, 0))
                                                            for j in range(n)],
        out_specs=pl.BlockSpec((tm, C), lambda i: (i, 0)),
        out_shape=jax.ShapeDtypeStruct((R, C), F32), compiler_params=_cparams(("parallel",)),
    )(own, *([got] * n))


_HBM = pl.BlockSpec(memory_space=pltpu.HBM)
N_CHIPS = 4


def _my_place():
    return lax.axis_index("x"), lax.axis_index("y"), lax.axis_index("c")


def _window(ref, axis, start, size):
    idx = [slice(None)] * len(ref.shape)
    idx[axis] = pl.ds(start, size)
    return ref.at[tuple(idx)]


def _comm_call(body, name, xs, out_shapes, n_remote, n_local, sequencer=None):
    sems = [pltpu.SemaphoreType.DMA((n_remote,)), pltpu.SemaphoreType.DMA((n_remote,)),
            pltpu.SemaphoreType.DMA((max(n_local, 1),))]
    if sequencer is None:
        return pl.pallas_call(
            body, name=name, in_specs=[_HBM] * len(xs), out_specs=[_HBM] * len(out_shapes), out_shape=out_shapes,
            scratch_shapes=sems, compiler_params=pltpu.CompilerParams(has_side_effects=True),
        )(*xs)
    peers_of, collective_id = sequencer
    hbm = pltpu.MemorySpace.HBM
    x_refs = [jax.new_ref(x, memory_space=hbm) for x in xs]
    o_refs = [jax.empty_ref(s, memory_space=hbm) for s in out_shapes]

    @pl.kernel(mesh=plsc.ScalarSubcoreMesh(axis_name="sequencer", num_cores=1), name=name, scratch_types=tuple(sems),
               compiler_params=pltpu.CompilerParams(collective_id=collective_id))
    def launch(send_sems, recv_sems, local_sems):
        peers = peers_of(*_my_place())
        barrier = pltpu.get_barrier_semaphore()
        for peer in peers:
            pl.semaphore_signal(barrier, inc=1, device_id=peer, device_id_type=MESH)
        pl.semaphore_wait(barrier, len(peers))
        body(*x_refs, *o_refs, send_sems, recv_sems, local_sems)

    launch()
    return [o[...] for o in o_refs]


def _sibling(mx, my, mc):
    return [(mx, my, 1 - mc)]


def _same_core_of_other_chips(mx, my, mc):
    return [(tx, ty, mc) for tx, ty in _other_chips(mx, my)]


def _run(copies):
    for cp in copies:
        cp.start()
    for cp in copies:
        cp.wait()


def _other_chips(mx, my):
    return [(mx ^ (j >> 1), my ^ (j & 1)) for j in (1, 2, 3)]


def chip_gather(xs, axes, name):
    n = len(xs)
    shapes, final = [], []
    for x, ax in zip(xs, axes):
        s = list(x.shape)
        if ax is None:
            shapes.append([N_CHIPS] + s)
            final.append(shapes[-1])
        elif ax < x.ndim - 1:
            shapes.append(s[:ax] + [N_CHIPS] + s[ax:])
            final.append(s[:ax] + [N_CHIPS * s[ax]] + s[ax + 1:])
        else:
            assert s[ax] % 128 == 0, (name, s)
            shapes.append(s[:ax] + [N_CHIPS * s[ax]])
            final.append(shapes[-1])

    def body(*refs):
        x_refs, o_refs = refs[:n], refs[n:2 * n]
        send_sems, recv_sems, local_sems = refs[2 * n:]
        mx, my, mc = _my_place()
        q = 2 * mx + my
        copies = []
        for i, (x_ref, o_ref, ax) in enumerate(zip(x_refs, o_refs, axes)):
            if ax is None or ax < len(x_ref.shape) - 1:
                dst = o_ref.at[(slice(None),) * (ax or 0) + (q,)]
            else:
                dst = _window(o_ref, ax, q * x_ref.shape[ax], x_ref.shape[ax])
            copies.append(pltpu.make_async_copy(x_ref, dst, local_sems.at[i]))
            for j, (tx, ty) in enumerate(_other_chips(mx, my)):
                copies.append(pltpu.make_async_remote_copy(
                    src_ref=x_ref, dst_ref=dst, send_sem=send_sems.at[3 * i + j], recv_sem=recv_sems.at[3 * i + j],
                    device_id=(tx, ty, mc), device_id_type=MESH))
        _run(copies)

    out_shapes = [jax.ShapeDtypeStruct(tuple(s), x.dtype) for s, x in zip(shapes, xs)]
    return [o.reshape(f) for o, f in zip(_comm_call(body, name, xs, out_shapes, 3 * n, n), final)]


def gather_two_level(xs, name):
    n = len(xs)
    shapes = [jax.ShapeDtypeStruct((2, N_CHIPS) + x.shape[1:], x.dtype) for x in xs]

    def body(*refs):
        x_refs, o_refs = refs[:n], refs[n:2 * n]
        send_sems, recv_sems, local_sems = refs[2 * n:]
        mx, my, mc = _my_place()
        q = 2 * mx + my
        first, local, second = [], [], []
        for i, (x_ref, o_ref) in enumerate(zip(x_refs, o_refs)):
            local.append(pltpu.make_async_copy(x_ref.at[mc], o_ref.at[mc, q], local_sems.at[i]))
            for j, (tx, ty) in enumerate(_other_chips(mx, my)):
                first.append(pltpu.make_async_remote_copy(
                    src_ref=x_ref.at[mc], dst_ref=o_ref.at[mc, q], send_sem=send_sems.at[4 * i + j],
                    recv_sem=recv_sems.at[4 * i + j], device_id=(tx, ty, mc), device_id_type=MESH))
            second.append(pltpu.make_async_remote_copy(
                src_ref=o_ref.at[mc], dst_ref=o_ref.at[mc], send_sem=send_sems.at[4 * i + 3],
                recv_sem=recv_sems.at[4 * i + 3], device_id=(mx, my, 1 - mc), device_id_type=MESH))
        for cp in local + first:
            cp.start()
        for cp in local:
            cp.wait()
        for cp in first:
            cp.wait_recv()
        _run(second)
        for cp in first:
            cp.wait_send()

    return _comm_call(body, name, xs, shapes, 4 * n, n)


def gather_two_level_sequencer(xs, name, collective_id):
    n = len(xs)
    hbm = pltpu.MemorySpace.HBM
    x_refs = [jax.new_ref(x, memory_space=hbm) for x in xs]
    o_refs = [jax.empty_ref(jax.ShapeDtypeStruct((2, N_CHIPS) + x.shape[1:], x.dtype), memory_space=hbm) for x in xs]

    @pl.kernel(mesh=plsc.ScalarSubcoreMesh(axis_name="sequencer", num_cores=1), name=name,
               scratch_types=(pltpu.SemaphoreType.DMA((4 * n,)), pltpu.SemaphoreType.DMA((4 * n,)),
                              pltpu.SemaphoreType.DMA((n,))),
               compiler_params=pltpu.CompilerParams(collective_id=collective_id))
    def launch(send_sems, recv_sems, local_sems):
        mx, my, mc = _my_place()
        peers = [(tx, ty, mc) for tx, ty in _other_chips(mx, my)] + [(mx, my, 1 - mc)]
        barrier = pltpu.get_barrier_semaphore()
        for peer in peers:
            pl.semaphore_signal(barrier, inc=1, device_id=peer, device_id_type=MESH)
        pl.semaphore_wait(barrier, len(peers))
        q = 2 * mx + my
        first, local, second = [], [], []
        for i, (x_ref, o_ref) in enumerate(zip(x_refs, o_refs)):
            local.append(pltpu.make_async_copy(x_ref.at[mc], o_ref.at[mc, q], local_sems.at[i]))
            for j, peer in enumerate(peers[:3]):
                first.append(pltpu.make_async_remote_copy(
                    src_ref=x_ref.at[mc], dst_ref=o_ref.at[mc, q], send_sem=send_sems.at[4 * i + j],
                    recv_sem=recv_sems.at[4 * i + j], device_id=peer, device_id_type=MESH))
            second.append(pltpu.make_async_remote_copy(
                src_ref=o_ref.at[mc], dst_ref=o_ref.at[mc], send_sem=send_sems.at[4 * i + 3],
                recv_sem=recv_sems.at[4 * i + 3], device_id=peers[3], device_id_type=MESH))
        for cp in local + first:
            cp.start()
        for cp in local:
            cp.wait()
        for cp in first:
            cp.wait_recv()
        _run(second)
        for cp in first:
            cp.wait_send()

    launch()
    return [o[...] for o in o_refs]


def pair_swap(xs, name, halves, collective_id=None):
    n = len(xs)
    shapes = [jax.ShapeDtypeStruct(x.shape[:1] + x.shape[2:] if halves else x.shape, x.dtype) for x in xs]

    def body(*refs):
        x_refs, o_refs = refs[:n], refs[n:2 * n]
        send_sems, recv_sems, _ = refs[2 * n:]
        mx, my, mc = _my_place()
        _run([pltpu.make_async_remote_copy(
            src_ref=x_ref.at[:, 1 - mc] if halves else x_ref, dst_ref=o_ref, send_sem=send_sems.at[i],
            recv_sem=recv_sems.at[i], device_id=(mx, my, 1 - mc), device_id_type=MESH)
            for i, (x_ref, o_ref) in enumerate(zip(x_refs, o_refs))])

    return _comm_call(body, name, xs, shapes, n, 0, None if collective_id is None else (_sibling, collective_id))


def chip_all_to_all(xs, name, collective_id=None):
    n = len(xs)
    shapes = [jax.ShapeDtypeStruct((N_CHIPS - 1,) + x.shape[1:], x.dtype) for x in xs]

    def body(*refs):
        x_refs, o_refs = refs[:n], refs[n:2 * n]
        send_sems, recv_sems, _ = refs[2 * n:]
        mx, my, mc = _my_place()
        copies = []
        for i, (x_ref, o_ref) in enumerate(zip(x_refs, o_refs)):
            for j, (tx, ty) in enumerate(_other_chips(mx, my)):
                copies.append(pltpu.make_async_remote_copy(
                    src_ref=x_ref.at[2 * tx + ty], dst_ref=o_ref.at[j], send_sem=send_sems.at[3 * i + j],
                    recv_sem=recv_sems.at[3 * i + j], device_id=(tx, ty, mc), device_id_type=MESH))
        _run(copies)

    return _comm_call(body, name, xs, shapes, 3 * n, 0,
                      None if collective_id is None else (_same_core_of_other_chips, collective_id))


WEIGHTS = ['norm_mix', 'norm_xa', 'norm_mem', 'norm_ffn', 'xa_wq', 'xa_wk', 'xa_wv', 'xa_wo', 'xa_q_norm', 'xa_k_norm',
           'ffn_w_up', 'ffn_conv_w', 'ffn_conv_b', 'ffn_w_down', 'hg_lb_logits', 'mix_w_in', 'hg_out_norm',
           'mla_q_a_norm', 'mla_w_uq', 'mla_kv_a_norm', 'mla_w_ukv', 'mla_qn_nope', 'mla_qn_rope', 'mla_kn_nope',
           'mla_kn_rope', 'mix_w_out', 's5_lam_re', 's5_lam_im', 's5_log_dt', 's5_b_re', 's5_b_im', 's5_c_re',
           's5_c_im', 's5_d', 's5_w_glu_a', 's5_w_glu_b']
INPUTS = ['x', 'mem', 'positions'] + WEIGHTS + ['loss_target'] + ['m_' + n for n in WEIGHTS] + ['v_' + n for n in WEIGHTS]
SHARD_AXIS = {'xa_wq': 1, 'xa_wk': 1, 'xa_wv': 1, 'xa_wo': 1, 'ffn_w_up': 2, 'ffn_conv_w': 2, 'ffn_w_down': 1,
              'mix_w_in': 2, 'mla_w_uq': 2, 'mla_w_ukv': 2, 'mix_w_out': 1, 's5_d': 1, 's5_w_glu_a': 1, 's5_w_glu_b': 1}
BIG = ['xa_wq', 'xa_wk', 'xa_wv', 'xa_wo', 'ffn_w_up', 'ffn_w_down', 'mix_w_in', 'mix_w_out', 's5_w_glu_a', 's5_w_glu_b']
FIRST_NEEDED = ('mix_w_in', 'mix_w_out')
SMALL_SHARDED = [n for n in WEIGHTS if n in SHARD_AXIS and n not in BIG]
REPLICATED = [n for n in WEIGHTS if n not in SHARD_AXIS]
SMALL = SMALL_SHARDED + REPLICATED
PACK_W = 1024
ROW_MULT = 16
W_IN_SHARD = IN_WIDTH // N_CHIPS
W_IN_SHARD_PAD = 640


def _pack(flats, mult=ROW_MULT):
    flat = jnp.concatenate([f.reshape(-1) for f in flats])
    unit = mult * PACK_W
    n = -(-flat.shape[0] // unit) * unit
    return jnp.pad(flat, (0, n - flat.shape[0])).reshape(n // PACK_W, PACK_W)


def _unpack(packed, shapes):
    flat, out, o = packed.reshape(-1), [], 0
    for s in shapes:
        n = math.prod(s)
        out.append(flat[o:o + n].reshape(s))
        o += n
    return out


def _rope_pad(w):
    z = jnp.zeros(w.shape[:-1] + (MLA_ROPE // 2,), w.dtype)
    return jnp.concatenate([w[..., :MLA_ROPE // 2], z, w[..., MLA_ROPE // 2:], z], axis=-1)


def _rope_unpad(g):
    return jnp.concatenate([g[..., :MLA_ROPE // 2], g[..., 64:64 + MLA_ROPE // 2]], axis=-1)


def _blockdiag_in(bb):
    nb = bb.shape[0] // S5_GB
    t = bb.reshape(nb, S5_GB, S5_STATE, S5_GROUP).transpose(0, 1, 3, 2)
    return jnp.einsum('bgmp,gh->bgmhp', t, jnp.eye(S5_GB, dtype=bb.dtype)).reshape(nb, S5_GB * S5_GROUP, S5_LANES)


def _blockdiag_in_t(dw):
    nb = dw.shape[0]
    t = jnp.einsum('bgmhp,gh->bgmp', dw.reshape(nb, S5_GB, S5_GROUP, S5_GB, S5_STATE), jnp.eye(S5_GB, dtype=dw.dtype))
    return t.transpose(0, 1, 3, 2).reshape(nb * S5_GB, S5_STATE, S5_GROUP)


def _blockdiag_out(c):
    nb = c.shape[0] // S5_GB
    t = c.reshape(nb, S5_GB, S5_GROUP, S5_STATE).transpose(0, 1, 3, 2)
    return jnp.einsum('bgpm,gh->bgphm', t, jnp.eye(S5_GB, dtype=c.dtype)).reshape(nb, S5_LANES, S5_GB * S5_GROUP)


def _blockdiag_out_t(dc):
    nb = dc.shape[0]
    t = jnp.einsum('bgphm,gh->bgpm', dc.reshape(nb, S5_GB, S5_STATE, S5_GB, S5_GROUP), jnp.eye(S5_GB, dtype=dc.dtype))
    return t.transpose(0, 1, 3, 2).reshape(nb * S5_GB, S5_GROUP, S5_STATE)


def _gather_weights(P):
    def halves(x):
        return x if x.shape[0] == 2 else x.reshape(2, x.shape[1] // 2, x.shape[2])

    now = [n for n in BIG if n in FIRST_NEEDED]
    later = [n for n in BIG if n not in FIRST_NEEDED]
    xs = [halves(P[n].astype(BF16)) for n in now] + [halves(_pack([P[n] for n in SMALL_SHARDED], 2 * ROW_MULT)[None])]
    got = gather_two_level(xs, "gather_weights")
    got, xs_later = lax.optimization_barrier((got, [halves(P[n].astype(BF16)) for n in later]))
    got_later = gather_two_level_sequencer(xs_later, "gather_weights_later", 1)
    full_w = {}
    for n, g in list(zip(now, got[:-1])) + list(zip(later, got_later)):
        two_layers, by_rows = P[n].shape[0] == 2, SHARD_AXIS[n] == 1
        if two_layers and by_rows:
            full_w[n] = g.reshape(2, N_CHIPS * g.shape[2], g.shape[3])
        elif two_layers:
            full_w[n] = g.transpose(0, 2, 1, 3).reshape(2, g.shape[2], N_CHIPS * g.shape[3])
        elif by_rows:
            full_w[n] = g.transpose(1, 0, 2, 3).reshape(1, 2 * N_CHIPS * g.shape[2], g.shape[3])
        else:
            full_w[n] = g.transpose(0, 2, 1, 3).reshape(1, 2 * g.shape[2], N_CHIPS * g.shape[3])
    small = got[-1].transpose(1, 0, 2, 3).reshape(N_CHIPS, -1, PACK_W)
    per_chip = [_unpack(small[q], [P[n].shape for n in SMALL_SHARDED]) for q in range(N_CHIPS)]
    for i, n in enumerate(SMALL_SHARDED):
        full_w[n] = jnp.concatenate([per_chip[q][i] for q in range(N_CHIPS)], axis=SHARD_AXIS[n])
    return full_w


def _halves_first(x):
    return x.reshape(x.shape[0], 2, x.shape[1] // 2, x.shape[2]).transpose(1, 0, 2, 3)


def _reduce_batch(items, small, P, results, tag, ids):
    mx, my, mc = _my_place()
    q = 2 * mx + my
    ids = ids or {}
    names = [f"{n}_{lyr}" for n, lyr, _ in items] + (['small'] if small is not None else [])
    xs = [g.reshape(N_CHIPS, 2, g.shape[1] // 2, g.shape[2]) for g in [g for _, _, g in items] + ([small] if small is not None else [])]
    def after(vals, tie):
        return (vals, None) if tie is None else lax.optimization_barrier((vals, tie))

    theirs = pair_swap(xs, "grads_pair_swap_" + tag, True, ids.get('swap'))
    theirs, tie = after(theirs, (yield None))
    pair = [add2(lax.dynamic_index_in_dim(x, mc, 1, False), t, "grads_pair_sum_" + n, F32 if n == 'small' else BF16)
            for x, t, n in zip(xs, theirs, names)]
    got = chip_all_to_all(pair, "grads_chip_all_to_all_" + tag, ids.get('a2a'))
    got, tie = after(got, (yield tie))
    summed = [add_chips(lax.dynamic_index_in_dim(p, q, 0, False), g, "grads_chip_sum_" + n)
              for p, g, n in zip(pair, got, names)]
    other = pair_swap(summed, "grads_pair_join_" + tag, False, ids.get('join'))
    other, tie = after(other, (yield tie))
    joined = [lax.cond(mc == 0, lambda a, b: jnp.concatenate([a, b], axis=0), lambda a, b: jnp.concatenate([b, a], axis=0),
                       s, o) for s, o in zip(summed, other)]
    for (n, lyr, _), g in zip(items, joined):
        if n == 'mix_w_in':
            g = g[:, :W_IN_SHARD]
        view = (P[n].shape[0], g.shape[0], P[n].shape[-1])
        bufs = results.get(n) or [lax.empty(view, F32) for _ in range(4)]
        results[n] = adamw_layer(P[n].reshape(view), g, P['m_' + n].reshape(view), P['v_' + n].reshape(view), lyr, bufs,
                                 f"adamw_{n}_{lyr}")
    if small is not None:
        results['small_quarter'] = joined[-1]
    yield tie


def _update_small(small_quarter, GS, P):
    mx, my, _ = _my_place()
    q = 2 * mx + my
    small_sum = chip_gather([small_quarter], [0], "grads_small_gather")[0]
    g_small = dict(zip(SMALL, _unpack(small_sum, [GS[n].shape for n in SMALL])))
    for n in SMALL_SHARDED:
        s = P[n].shape[SHARD_AXIS[n]]
        g_small[n] = lax.dynamic_slice_in_dim(g_small[n], q * s, s, axis=SHARD_AXIS[n])
    grad, delta, new_m, new_v = {}, {}, {}, {}
    packed = lambda prefix: _pack([P[prefix + n] for n in SMALL])
    d, m_, v_ = adamw(packed(''), _pack([g_small[n] for n in SMALL]), packed('m_'), packed('v_'), "adamw_small")
    shapes = [P[n].shape for n in SMALL]
    grad.update(g_small)
    for out, pk in ((delta, d), (new_m, m_), (new_v, v_)):
        out.update(zip(SMALL, _unpack(pk, shapes)))
    return grad, delta, new_m, new_v


def _row(v):
    return v.reshape(1, -1)


def _xattn_fwd(h, mem, W, lyr, tm):
    g_xa, g_mem = _row(W['norm_xa'][lyr]), _row(W['norm_mem'][lyr])
    g_q, g_k = _row(W['xa_q_norm'][lyr]), _row(W['xa_k_norm'][lyr])
    wq, wk, wv, wo = (W[n][lyr] for n in ('xa_wq', 'xa_wk', 'xa_wv', 'xa_wo'))
    L, D = h.shape
    M = mem.shape[0]
    hx = rms_fwd(h, g_xa, tm, MXU_DTYPE)
    qp = matmul(hx, wq, name="xa_q")
    kv_opds = [full(mem), full(g_mem), full(wk), full(wv), full(g_k)]
    k, v = blocked_fwd(_mem_kv, kv_opds, [((M, D), F32, (M, D), lambda i: (0, 0))] * 2, 1, "xa_mem_kv")
    o = blocked_fwd(_xa_core, [rows(qp, tm), full(k), full(v), full(g_q)], [_out((L, D), MXU_DTYPE, tm)], L // tm,
                    "xa_core")[0]
    out = matmul(o, wo, add=h, name="xa_o")
    return out, (h, hx, qp, k, v, o)


def _xattn_bwd(dout, saved, mem, W, lyr, tm):
    h, hx, qp, k, v, o = saved
    g_xa, g_mem = _row(W['norm_xa'][lyr]), _row(W['norm_mem'][lyr])
    g_q, g_k = _row(W['xa_q_norm'][lyr]), _row(W['xa_k_norm'][lyr])
    wq, wk, wv, wo = (W[n][lyr] for n in ('xa_wq', 'xa_wk', 'xa_wv', 'xa_wo'))
    L = h.shape[0]
    do = matmul(dout, wo, "nt", name="xa_do")
    d_wo = matmul(o, dout, "tn", name="xa_dwo")
    dqp, dk, dv, d_gq = blocked_bwd(_xa_core, [rows(qp, tm, 'blk'), full(k, 'acc'), full(v, 'acc'), full(g_q, 'acc')],
                                    [rows(do, tm)], L // tm, "xa_core_bwd")
    d_wq = matmul(hx, dqp, "tn", name="xa_dwq")
    dhx = matmul(dqp, wq, "nt", name="xa_dhx")
    dh, d_gxa = rms_bwd(h, g_xa, dhx, tm, dout)
    d_gmem, d_wk, d_wv, d_gk = blocked_bwd(
        _mem_kv, [full(mem), full(g_mem, 'acc'), full(wk, 'acc'), full(wv, 'acc'), full(g_k, 'acc')],
        [full(dk), full(dv)], 1, "xa_mem_kv_bwd")
    by_chip = lambda g: g.reshape(N_CHIPS, g.shape[0] // N_CHIPS, g.shape[1])
    grads = {'norm_xa': d_gxa, 'norm_mem': d_gmem, 'xa_q_norm': d_gq, 'xa_k_norm': d_gk,
             'xa_wq': by_chip(d_wq), 'xa_wk': by_chip(d_wk), 'xa_wv': by_chip(d_wv), 'xa_wo': by_chip(d_wo)}
    return dh, grads


def _conv_params(W, lyr):
    cw, cb = W['ffn_conv_w'][lyr], W['ffn_conv_b'][lyr]
    F = cw.shape[1] // 2
    return [cw[0:1, :F], cw[1:2, :F], cw[2:3, :F], cw[0:1, F:], cw[1:2, F:], cw[2:3, F:], _row(cb[:F]), _row(cb[F:])]


def _ffn_fwd(h, W, lyr, tm):
    L, D = h.shape
    w_up, w_down = W['ffn_w_up'][lyr], W['ffn_w_down'][lyr]
    F = w_down.shape[0]
    hf = rms_fwd(h, _row(W['norm_ffn'][lyr]), tm, MXU_DTYPE)
    ug = matmul(hf, w_up[:, :F], name="ffn_up_gate")
    uv = matmul(hf, w_up[:, F:], name="ffn_up_value")
    opds = [cols(ug, 128), cols(uv, 128)] + [cols(p, 128) for p in _conv_params(W, lyr)]
    a = blocked_fwd(_conv_gate, opds, [((L, F), MXU_DTYPE, (L, 128), lambda j: (0, j))], F // 128, "ffn_conv_gate")[0]
    out = matmul(a, w_down, add=h, name="ffn_down")
    return out, (h, hf, ug, uv, a)


def _ffn_bwd(dout, saved, W, lyr, tm):
    h, hf, ug, uv, a = saved
    w_up, w_down = W['ffn_w_up'][lyr], W['ffn_w_down'][lyr]
    F = w_down.shape[0]
    da = matmul(dout, w_down, "nt", name="ffn_da")
    d_wdown = matmul(a, dout, "tn", name="ffn_dwdown")
    opds = [cols(ug, 128, 'blk'), cols(uv, 128, 'blk')] + [cols(p, 128, 'blk') for p in _conv_params(W, lyr)]
    gs = blocked_bwd(_conv_gate, opds, [cols(da, 128)], F // 128, "ffn_conv_gate_bwd")
    dug, duv = gs[0], gs[1]
    d_cw = jnp.concatenate([jnp.concatenate(gs[2:5], axis=0), jnp.concatenate(gs[5:8], axis=0)], axis=1)
    d_cb = jnp.concatenate([gs[8], gs[9]], axis=1)[0]
    half = N_CHIPS // 2
    d_wup = lax.empty((N_CHIPS, hf.shape[1], w_up.shape[1] // N_CHIPS), F32)
    d_wup = matmul(hf, dug, "tn", name="ffn_dwup_gate", into=(d_wup, 0), col_blocks=half)
    d_wup = matmul(hf, duv, "tn", name="ffn_dwup_value", into=(d_wup, half), col_blocks=half)
    dhf = matmul(dug, w_up[:, :F], "nt", name="ffn_dhf_gate")
    dhf = matmul(duv, w_up[:, F:], "nt", add=dhf, name="ffn_dhf_value")
    dh, d_g = rms_bwd(h, _row(W['norm_ffn'][lyr]), dhf, tm, dout)
    d_wdown = d_wdown.reshape(N_CHIPS, F // N_CHIPS, d_wdown.shape[1])
    return dh, {'norm_ffn': d_g, 'ffn_w_up': d_wup, 'ffn_conv_w': d_cw, 'ffn_conv_b': d_cb, 'ffn_w_down': d_wdown}


def _mla_params(W):
    w_uq = W['mla_w_uq'][0].reshape(MLA_Q_RANK, MLA_HEADS, MLA_QK)
    w_uq = jnp.concatenate([w_uq[..., :MLA_NOPE], _rope_pad(w_uq[..., MLA_NOPE:])], axis=-1)
    w_ukv = W['mla_w_ukv'][0].reshape(MLA_KV_RANK, MLA_HEADS, MLA_NOPE + MLA_V)
    w_ukv = jnp.concatenate([w_ukv[..., :MLA_NOPE].reshape(MLA_KV_RANK, -1), w_ukv[..., MLA_NOPE:].reshape(MLA_KV_RANK, -1)],
                            axis=1)
    return [_row(W['mla_q_a_norm'][0]), w_uq.reshape(MLA_Q_RANK, MLA_HEADS * MLA_DK), _row(W['mla_kv_a_norm'][0]), w_ukv,
            _row(W['mla_qn_nope'][0]), _row(_rope_pad(W['mla_qn_rope'][0])), _row(W['mla_kn_nope'][0]),
            _row(_rope_pad(W['mla_kn_rope'][0]))]


def _w_in_padded(W):
    w = W['mix_w_in'][0]
    return jnp.concatenate([w[:, :IN_WIDTH - MLA_ROPE], _rope_pad(w[:, IN_WIDTH - MLA_ROPE:])], axis=1)


def _mixer0_fwd(h, W, cos_p, sin_p, tm):
    L = h.shape[0]
    t = min(ATTN_ROWS, L // ATTN_WIDE)
    hn = rms_fwd(h, _row(W['norm_mix'][0]), tm, MXU_DTYPE)
    proj = matmul(hn, _w_in_padded(W), name="mix_in")
    logits = W['hg_lb_logits']
    lb = blocked_fwd(_lb_first, [full(logits)], [((1, HG_WIDTH), F32, (1, HG_WIDTH), lambda i: (0, 0))], 1, "hg_lb")[0]
    gain = _row(W['hg_out_norm'][0])
    o_hg, states = hgrn2_fwd(proj, lb, gain)
    mp = _mla_params(W)
    q, k, v = mla_prep_fwd(proj, cos_p, sin_p, mp, tm)
    scale = MLA_QK ** -0.5
    o_mla, lse = attn_fwd(q, k, v, scale, t)
    w_out = W['mix_w_out'][0]
    out = matmul(o_hg, w_out[:HG_WIDTH], add=h, name="mix_out_hg")
    out = matmul(o_mla, w_out[HG_WIDTH:], add=out, name="mix_out_mla")
    return out, (h, hn, proj, lb, o_hg, states, q, k, v, o_mla, lse)


def _mixer0_bwd(dout, saved, W, cos_p, sin_p, tm):
    h, hn, proj, lb, o_hg, states, q, k, v, o_mla, lse = saved
    L = h.shape[0]
    t = min(ATTN_ROWS, L // ATTN_WIDE)
    scale = MLA_QK ** -0.5
    w_out = W['mix_w_out'][0]
    gain = _row(W['hg_out_norm'][0])
    do_hg = matmul(dout, w_out[:HG_WIDTH], "nt", name="mix_do_hg")
    do_mla = matmul(dout, w_out[HG_WIDTH:], "nt", name="mix_do_mla")
    d_wout = jnp.concatenate([matmul(o_hg, dout, "tn", name="mix_dwout_hg"), matmul(o_mla, dout, "tn", name="mix_dwout_mla")],
                             axis=0)
    dq = attn_bwd_dq(q, k, v, o_mla, lse, do_mla, scale, t)
    dk, dv = attn_bwd_dkv(q, k, v, o_mla, lse, do_mla, scale, t)
    mp = _mla_params(W)
    d_mla, d_qa, d_wuq, d_kva, d_wukv, d_qnn, d_qnr, d_knn, d_knr = mla_prep_bwd(proj, cos_p, sin_p, mp, dq, dk, dv, tm)
    d_hg, d_lb, d_gain = hgrn2_bwd(proj, lb, gain, states, do_hg)
    w_in, n_hg = _w_in_padded(W), 4 * HG_WIDTH
    d_win = jnp.concatenate([matmul(hn, d_hg, "tn", name="mix_dwin_hg"), matmul(hn, d_mla, "tn", name="mix_dwin_mla")], axis=1)
    dhn = matmul(d_hg, w_in[:, :n_hg], "nt", name="mix_dhn_hg")
    dhn = matmul(d_mla, w_in[:, n_hg:], "nt", add=dhn, name="mix_dhn_mla")
    dh, d_g = rms_bwd(h, _row(W['norm_mix'][0]), dhn, tm, dout)
    logits = W['hg_lb_logits']
    d_logits = blocked_bwd(_lb_first, [full(logits, 'acc')], [full(d_lb)], 1, "hg_lb_bwd")[0]
    d_wuq = d_wuq.reshape(MLA_Q_RANK, MLA_HEADS, MLA_DK)
    d_wuq = jnp.concatenate([d_wuq[..., :MLA_NOPE], _rope_unpad(d_wuq[..., MLA_NOPE:])], axis=-1)
    hw = MLA_HEADS * MLA_NOPE
    d_wukv = jnp.concatenate([d_wukv[:, :hw].reshape(MLA_KV_RANK, MLA_HEADS, MLA_NOPE),
                              d_wukv[:, hw:].reshape(MLA_KV_RANK, MLA_HEADS, MLA_V)], axis=-1)
    d_win = jnp.concatenate([d_win[:, :IN_WIDTH - MLA_ROPE], _rope_unpad(d_win[:, IN_WIDTH - MLA_ROPE:])], axis=1)
    d_win = d_win.reshape(d_win.shape[0], N_CHIPS, W_IN_SHARD).transpose(1, 0, 2)
    d_win = jnp.pad(d_win, ((0, 0), (0, 0), (0, W_IN_SHARD_PAD - W_IN_SHARD)))
    d_wout = d_wout.reshape(N_CHIPS, d_wout.shape[0] // N_CHIPS, d_wout.shape[1])
    grads = {'norm_mix': d_g, 'hg_lb_logits': d_logits, 'mix_w_in': d_win, 'hg_out_norm': d_gain,
             'mla_q_a_norm': d_qa, 'mla_w_uq': d_wuq.reshape(1, MLA_Q_RANK, -1), 'mla_kv_a_norm': d_kva,
             'mla_w_ukv': d_wukv.reshape(1, MLA_KV_RANK, -1), 'mla_qn_nope': d_qnn, 'mla_qn_rope': _rope_unpad(d_qnr),
             'mla_kn_nope': d_knn, 'mla_kn_rope': _rope_unpad(d_knr), 'mix_w_out': d_wout}
    return dh,grads


def _s5_inputs(W):
    G = W['s5_lam_re'].shape[1]
    return [W['s5_lam_re'][0], W['s5_lam_im'][0], W['s5_log_dt'][0].reshape(G, 1),
            W['s5_b_re'][0].reshape(G, -1), W['s5_b_im'][0].reshape(G, -1)]


def _mixer1_fwd(h, W, tm):
    L, D = h.shape
    u = rms_fwd(h, _row(W['norm_mix'][1]), tm, F32)
    di = _s5_inputs(W)
    G = di[0].shape[0]
    sq, wide = ((G, S5_STATE), F32, (G, S5_STATE), lambda i: (0, 0)), ((G, S5_STATE * S5_GROUP), F32, (G, S5_STATE * S5_GROUP), lambda i: (0, 0))
    ar, ai, bbr, bbi = blocked_fwd(_s5_discretize, [full(a) for a in di], [sq, sq, wide, wide], 1, "s5_discretize")
    nb = G // S5_GB
    core = (_blockdiag_in(bbr.reshape(G, S5_STATE, S5_GROUP)), _blockdiag_in(bbi.reshape(G, S5_STATE, S5_GROUP)),
            ar.reshape(nb, 1, S5_LANES), ai.reshape(nb, 1, S5_LANES),
            _blockdiag_out(W['s5_c_re'][0]), _blockdiag_out(W['s5_c_im'][0]))
    y = s5_fwd(u, *core)
    d = W['s5_d']
    y2 = blocked_fwd(_s5_post, [rows(y, tm), rows(u, tm), full(d)], [_out((L, D), MXU_DTYPE, tm)], L // tm, "s5_post")[0]
    w_ab = jnp.concatenate([W['s5_w_glu_a'][0], W['s5_w_glu_b'][0]], axis=1)
    ab = matmul(y2, w_ab, name="s5_glu_in")
    mix = blocked_fwd(_glu, [rows(ab, tm, col=0, width=D), rows(ab, tm, col=1, width=D)], [_out((L, D), F32, tm)], L // tm,
                      "s5_glu")[0]
    return h + mix, (h, u, core, y, y2, ab)


def _mixer1_bwd(dout, saved, W, tm):
    h, u, core, y, y2, ab = saved
    L, D = h.shape
    da, db = blocked_bwd(_glu, [rows(ab, tm, 'blk', col=0, width=D), rows(ab, tm, 'blk', col=1, width=D)], [rows(dout, tm)],
                         L // tm, "s5_glu_bwd")
    w_a, w_b = W['s5_w_glu_a'][0], W['s5_w_glu_b'][0]
    dy2 = matmul(da, w_a, "nt", name="s5_dy2_a")
    dy2 = matmul(db, w_b, "nt", add=dy2, name="s5_dy2_b")
    d_wa = matmul(y2, da, "tn", name="s5_dwa")
    d_wb = matmul(y2, db, "tn", name="s5_dwb")
    d = W['s5_d']
    dy, du_skip, d_d = blocked_bwd(_s5_post, [rows(y, tm, 'blk'), rows(u, tm, 'blk'), full(d, 'acc')], [rows(dy2, tm)], L // tm,
                                   "s5_post_bwd")
    du, dwr, dwi, dar, dai, dcr, dci = s5_bwd(u, *core, dy, min(256, L))
    di = _s5_inputs(W)
    G = di[0].shape[0]
    cts = [dar.reshape(G, S5_STATE), dai.reshape(G, S5_STATE), _blockdiag_in_t(dwr).reshape(G, -1), _blockdiag_in_t(dwi).reshape(G, -1)]
    d_lr, d_li, d_ldt, d_br, d_bi = blocked_bwd(_s5_discretize, [full(a, 'acc') for a in di], [full(c) for c in cts], 1,
                                                "s5_discretize_bwd")
    dh, d_g = rms_bwd(h, _row(W['norm_mix'][1]), du + du_skip, tm, dout)
    bshape = W['s5_b_re'].shape
    grads = {'norm_mix': d_g, 's5_lam_re': d_lr[None], 's5_lam_im': d_li[None], 's5_log_dt': d_ldt.reshape(1, G),
             's5_b_re': d_br.reshape(bshape), 's5_b_im': d_bi.reshape(bshape), 's5_c_re': _blockdiag_out_t(dcr)[None],
             's5_c_im': _blockdiag_out_t(dci)[None], 's5_d': d_d, 's5_w_glu_a': d_wa.reshape(N_CHIPS, -1, D), 's5_w_glu_b': d_wb.reshape(N_CHIPS, -1, D)}
    return dh,grads


def kernel(x, mem, positions, norm_mix, norm_xa, norm_mem, norm_ffn, xa_wq, xa_wk, xa_wv, xa_wo, xa_q_norm, xa_k_norm, ffn_w_up, ffn_conv_w, ffn_conv_b, ffn_w_down, hg_lb_logits, mix_w_in, hg_out_norm, mla_q_a_norm, mla_w_uq, mla_kv_a_norm, mla_w_ukv, mla_qn_nope, mla_qn_rope, mla_kn_nope, mla_kn_rope, mix_w_out, s5_lam_re, s5_lam_im, s5_log_dt, s5_b_re, s5_b_im, s5_c_re, s5_c_im, s5_d, s5_w_glu_a, s5_w_glu_b, loss_target, m_norm_mix, m_norm_xa, m_norm_mem, m_norm_ffn, m_xa_wq, m_xa_wk, m_xa_wv, m_xa_wo, m_xa_q_norm, m_xa_k_norm, m_ffn_w_up, m_ffn_conv_w, m_ffn_conv_b, m_ffn_w_down, m_hg_lb_logits, m_mix_w_in, m_hg_out_norm, m_mla_q_a_norm, m_mla_w_uq, m_mla_kv_a_norm, m_mla_w_ukv, m_mla_qn_nope, m_mla_qn_rope, m_mla_kn_nope, m_mla_kn_rope, m_mix_w_out, m_s5_lam_re, m_s5_lam_im, m_s5_log_dt, m_s5_b_re, m_s5_b_im, m_s5_c_re, m_s5_c_im, m_s5_d, m_s5_w_glu_a, m_s5_w_glu_b, v_norm_mix, v_norm_xa, v_norm_mem, v_norm_ffn, v_xa_wq, v_xa_wk, v_xa_wv, v_xa_wo, v_xa_q_norm, v_xa_k_norm, v_ffn_w_up, v_ffn_conv_w, v_ffn_conv_b, v_ffn_w_down, v_hg_lb_logits, v_mix_w_in, v_hg_out_norm, v_mla_q_a_norm, v_mla_w_uq, v_mla_kv_a_norm, v_mla_w_ukv, v_mla_qn_nope, v_mla_qn_rope, v_mla_kn_nope, v_mla_kn_rope, v_mix_w_out, v_s5_lam_re, v_s5_lam_im, v_s5_log_dt, v_s5_b_re, v_s5_b_im, v_s5_c_re, v_s5_c_im, v_s5_d, v_s5_w_glu_a, v_s5_w_glu_b):
    P = dict(locals())
    assert sorted(P) == sorted(INPUTS) and norm_mix.shape[0] == 2 and mix_w_in.shape[0] == 1
    x, mem, target = P['x'][0], P['mem'][0], P['loss_target'][0]
    L, D = x.shape
    tm = min(256, L)

    W = {n: P[n] for n in REPLICATED}
    W.update(_gather_weights(P))

    inv_freq = 1.0 / (ROPE_BASE ** (jnp.arange(0, MLA_ROPE, 2, dtype=F32) / MLA_ROPE))
    ang = P['positions'][0].astype(F32)[:, None] * inv_freq
    cos, sin, z = jnp.cos(ang), jnp.sin(ang), jnp.zeros_like(ang)
    cos_p = jnp.concatenate([cos, z, cos, z], axis=1)
    sin_p = jnp.concatenate([-sin, z, sin, z], axis=1)

    h, s_mix0 = _mixer0_fwd(x, W, cos_p, sin_p, tm)
    h, s_xa0 = _xattn_fwd(h, mem, W, 0, tm)
    h, s_ffn0 = _ffn_fwd(h, W, 0, tm)
    h, s_mix1 = _mixer1_fwd(h, W, tm)
    h, s_xa1 = _xattn_fwd(h, mem, W, 1, tm)
    h, s_ffn1 = _ffn_fwd(h, W, 1, tm)
    n = L // tm
    dh, parts = blocked_fwd(_loss_fn, [rows(h, tm), rows(target, tm)],
                            [_out((L, D), F32, tm), ((n * 8, 128), F32, (8, 128), lambda i: (i, 0))], n, "loss")
    loss = lax.psum(jnp.sum(parts), ("x", "y", "c"))

    layered = {}

    def collect(g, lyr):
        for k_, v_ in g.items():
            layered.setdefault(k_, {})[lyr] = v_

    results = {}

    def big_items(lyr):
        return [(n_, 0 if P[n_].shape[0] == 1 else lyr, layered[n_][lyr]) for n_ in BIG if lyr in layered.get(n_, {})]

    dh, g = _ffn_bwd(dh, s_ffn1, W, 1, tm)
    collect(g, 1)
    dh, g = _xattn_bwd(dh, s_xa1, mem, W, 1, tm)
    collect(g, 1)
    dh, g = _mixer1_bwd(dh, s_mix1, W, tm)
    collect(g, 1)
    late = _reduce_batch(big_items(1), None, P, results, "late_layer", {'swap': 2, 'a2a': 3, 'join': 4})
    next(late)
    dh, g = _ffn_bwd(dh, s_ffn0, W, 0, tm)
    collect(g, 0)
    dh = late.send(dh)
    dh, g = _xattn_bwd(dh, s_xa0, mem, W, 0, tm)
    collect(g, 0)
    dh = late.send(dh)
    dx, g = _mixer0_bwd(dh, s_mix0, W, cos_p, sin_p, tm)
    collect(g, 0)
    dx = late.send(dx)

    GS = {}
    for name in SMALL:
        by_layer = [layered[name][lyr] for lyr in sorted(layered[name])]
        full_shape = W[name].shape
        GS[name] = (by_layer[0].reshape(full_shape) if len(by_layer) == 1
                    else jnp.stack([g_.reshape(full_shape[1:]) for g_ in by_layer]))
    small = _pack([GS[n_] for n_ in SMALL], 2 * N_CHIPS * ROW_MULT).reshape(N_CHIPS, -1, PACK_W)
    for _ in _reduce_batch(big_items(0), small, P, results, "first_layer", None):
        pass
    outs = list(_update_small(results['small_quarter'], GS, P))
    for k_ in range(4):
        outs[k_].update({n_: results[n_][k_].reshape(P[n_].shape) for n_ in BIG})
    return (loss, dx[None], *[d[n_] for d in outs for n_ in WEIGHTS])
```

```python
import functools
import math

import jax
import jax.numpy as jnp
import numpy as np
from jax import lax
from jax.experimental import pallas as pl
from jax.experimental.pallas import tpu as pltpu
from jax.experimental.pallas import tpu_sc as plsc

F32 = jnp.float32
BF16 = jnp.bfloat16
MXU_DTYPE = BF16
HI = lax.Precision.HIGHEST
V7X_VMEM_LIMIT_BYTES = 56 * 1024 * 1024
EPS = 1e-6
MESH = pl.DeviceIdType.MESH

HG_HEADS, HG_DIM = 4, 128
HG_WIDTH = HG_HEADS * HG_DIM
HG_SUB = 32
HG_BLOCK = 64
MLA_HEADS, MLA_Q_RANK, MLA_KV_RANK = 4, 256, 128
MLA_NOPE, MLA_ROPE, MLA_V = 128, 64, 128
MLA_QK = MLA_NOPE + MLA_ROPE
MLA_DK = 256
ROPE_BASE = 10000.0
IN_WIDTH = 4 * HG_WIDTH + MLA_Q_RANK + MLA_KV_RANK + MLA_ROPE
IN_PAD = 4 * HG_WIDTH + MLA_Q_RANK + MLA_KV_RANK + 128
S5_GROUP, S5_STATE = 16, 64
S5_GB = 8
DT_MIN, DT_MAX = 1e-3, 1e-1
XA_HEADS = 4
CONV_W = 3
ADAM_LR, ADAM_B1, ADAM_B2, ADAM_EPS, ADAM_WD, ADAM_STEP = 0.001, 0.9, 0.999, 1e-08, 0.01, 10


def _cparams(sem):
    return pltpu.CompilerParams(dimension_semantics=sem, vmem_limit_bytes=V7X_VMEM_LIMIT_BYTES)


class Opd:
    def __init__(self, arr, block, imap, grad=None, gshape=None, gimap=None):
        self.arr, self.block, self.imap, self.grad = arr, block, imap, grad
        self.gshape = arr.shape if gshape is None else gshape
        self.gimap = imap if gimap is None else gimap

    def spec(self):
        return pl.BlockSpec(self.block, self.imap)

    def gspec(self):
        return pl.BlockSpec(self.block, self.gimap)


def rows(arr, tm, grad=None, col=0, width=None):
    width = arr.shape[1] if width is None else width
    return Opd(arr, (tm, width), lambda i, c=col: (i, c), grad, (arr.shape[0], width), lambda i: (i, 0))


def cols(arr, tn, grad=None):
    return Opd(arr, (arr.shape[0], tn), lambda j: (0, j), grad)


def full(arr, grad=None):
    return Opd(arr, arr.shape, lambda i: (0, 0), grad)


def _load(ref):
    v = ref[...]
    return v.astype(F32) if jnp.issubdtype(v.dtype, jnp.floating) else v


def blocked_fwd(f, opds, outs, n, name):
    n_in = len(opds)

    def body(*refs):
        ys = f(*[_load(r) for r in refs[:n_in]])
        for r, y in zip(refs[n_in:], ys):
            r[...] = y.astype(r.dtype)

    res = pl.pallas_call(
        body, name=name, grid=(n,),
        in_specs=[o.spec() for o in opds],
        out_specs=[pl.BlockSpec(b, m) for (_, _, b, m) in outs],
        out_shape=[jax.ShapeDtypeStruct(s, d) for (s, d, _, _) in outs],
        compiler_params=_cparams(("parallel",)),
    )(*[o.arr for o in opds])
    return res


def blocked_bwd(f, opds, dys, n, name, plus=None):
    n_in, n_dy = len(opds), len(dys)
    diff = [i for i, o in enumerate(opds) if o.grad]
    extra = [] if plus is None else [plus]

    def body(*refs):
        vals = [_load(r) for r in refs[:n_in]]

        def fd(*dv):
            allv = list(vals)
            for i, v in zip(diff, dv):
                allv[i] = v
            return tuple(f(*allv))

        ys, vjp = jax.vjp(fd, *[vals[i] for i in diff])
        cts = tuple(_load(r).astype(y.dtype) for r, y in zip(refs[n_in:n_in + n_dy], ys))
        gs = list(vjp(cts))
        if extra:
            gs[0] = gs[0] + _load(refs[n_in + n_dy])
        for r, g, i in zip(refs[n_in + n_dy + len(extra):], gs, diff):
            if opds[i].grad == 'acc':
                @pl.when(pl.program_id(0) == 0)
                def _(r=r):
                    r[...] = jnp.zeros(r.shape, r.dtype)
                r[...] += g.astype(r.dtype)
            else:
                r[...] = g.astype(r.dtype)

    any_acc = any(opds[i].grad == 'acc' for i in diff)
    res = pl.pallas_call(
        body, name=name, grid=(n,),
        in_specs=[o.spec() for o in opds + dys + extra],
        out_specs=[opds[i].gspec() for i in diff],
        out_shape=[jax.ShapeDtypeStruct(opds[i].gshape, F32) for i in diff],
        compiler_params=_cparams(("arbitrary" if any_acc else "parallel",)),
    )(*[o.arr for o in opds + dys + extra])
    return res


def _tile(dim, want):
    for t in range(want - want % 16, 0, -16):
        if dim % t == 0:
            return t
    assert dim <= want, (dim, want)
    return dim


MATMUL_VMEM_BUDGET = 40 * 1024 * 1024
MATMUL_ROWS = 512


def _widest(N, fits):
    for t in range(N - N % 128, 0, -128):
        if N % t == 0 and fits(t):
            return t
    return N


def matmul(a, b, mode="nn", out_dtype=F32, add=None, name="matmul", into=None, col_blocks=None):
    sa, sb, so = a.dtype.itemsize, b.dtype.itemsize, jnp.dtype(out_dtype).itemsize
    has_add = add is not None
    if mode == "tn":
        (K, M), (K2, N) = a.shape, b.shape
        assert K == K2 and not has_add and out_dtype == F32, (a.shape, b.shape)
        tk = _tile(K, MATMUL_ROWS)
        tn = _widest(N, lambda t: 2 * (tk * M * sa + tk * t * sb + M * t * 4) <= MATMUL_VMEM_BUDGET)
        extra, alias = [], {}
        if into is not None:
            buf, lead = into[0], tuple(into[1:])
            if col_blocks is not None:
                assert N % col_blocks == 0 and (N // col_blocks) % 128 == 0 and tn >= N // col_blocks, (N, col_blocks, tn)
                tn = N // col_blocks
                assert buf.shape[len(lead):] == (M, tn), (buf.shape, lead, M, tn)
                out_spec = pl.BlockSpec((None,) * len(lead) + (M, tn), lambda j, k: lead[:-1] + (lead[-1] + j, 0, 0))
            else:
                assert buf.shape[len(lead):] == (M, N), (buf.shape, lead, M, N)
                out_spec = pl.BlockSpec((None,) * len(lead) + (M, tn), lambda j, k: lead + (0, j))
            out_shape = jax.ShapeDtypeStruct(buf.shape, F32)
            extra, alias = [buf], {2: 0}
        else:
            assert col_blocks is None
            out_spec = pl.BlockSpec((M, tn), lambda j, k: (0, j))
            out_shape = jax.ShapeDtypeStruct((M, N), F32)

        def body(a_ref, b_ref, *rest):
            o_ref = rest[-1]
            r = lax.dot_general(a_ref[...].astype(MXU_DTYPE), b_ref[...].astype(MXU_DTYPE), ((_TN), ((), ())),
                                preferred_element_type=F32)

            @pl.when(pl.program_id(1) == 0)
            def _():
                o_ref[...] = r

            @pl.when(pl.program_id(1) > 0)
            def _():
                o_ref[...] += r

        return pl.pallas_call(
            body, name=name, grid=(N // tn, K // tk),
            in_specs=[pl.BlockSpec((tk, M), lambda j, k: (k, 0)), pl.BlockSpec((tk, tn), lambda j, k: (k, j))]
            + [pl.BlockSpec(memory_space=pl.ANY)] * len(extra),
            out_specs=out_spec, out_shape=out_shape, input_output_aliases=alias,
            compiler_params=_cparams(("parallel", "arbitrary")),
        )(a, b, *extra)

    (M, K) = a.shape
    N = b.shape[1] if mode == "nn" else b.shape[0]
    assert K == (b.shape[0] if mode == "nn" else b.shape[1]), (a.shape, b.shape, mode)
    tm = _tile(M, MATMUL_ROWS)
    tn = _widest(N, lambda t: 2 * (tm * K * sa + K * t * sb + tm * t * (so + 4 * has_add)) <= MATMUL_VMEM_BUDGET)
    dims = ((_NN if mode == "nn" else _NT), ((), ()))

    def body(*refs):
        r = lax.dot_general(refs[0][...].astype(MXU_DTYPE), refs[1][...].astype(MXU_DTYPE), dims, preferred_element_type=F32)
        if has_add:
            r = r + refs[2][...].astype(F32)
        refs[-1][...] = r.astype(refs[-1].dtype)

    b_spec = pl.BlockSpec((K, tn), lambda j, i: (0, j)) if mode == "nn" else pl.BlockSpec((tn, K), lambda j, i: (j, 0))
    in_specs = [pl.BlockSpec((tm, K), lambda j, i: (i, 0)), b_spec]
    args = [a, b]
    if has_add:
        in_specs.append(pl.BlockSpec((tm, tn), lambda j, i: (i, j)))
        args.append(add)
    return pl.pallas_call(
        body, name=name, grid=(N // tn, M // tm),
        in_specs=in_specs,
        out_specs=pl.BlockSpec((tm, tn), lambda j, i: (i, j)),
        out_shape=jax.ShapeDtypeStruct((M, N), out_dtype),
        compiler_params=_cparams(("parallel", "parallel")),
    )(*args)


def _dot(a, b, dims, precision=None):
    if precision is None:
        a, b = a.astype(MXU_DTYPE), b.astype(MXU_DTYPE)
    return lax.dot_general(a, b, (dims, ((), ())), precision=precision, preferred_element_type=F32)


_NN = ((1,), (0,))
_NT = ((1,), (1,))
_TN = ((0,), (0,))


def _rms(x, gain):
    return x * lax.rsqrt(jnp.mean(x * x, axis=-1, keepdims=True) + EPS) * gain


def _hg_block(st_t, q, fl, iv, g, lb, gain):
    row = lax.broadcasted_iota(jnp.int32, (HG_SUB, HG_SUB), 0)
    col = lax.broadcasted_iota(jnp.int32, (HG_SUB, HG_SUB), 1)
    tri = (row >= col).astype(F32)
    outs, states = [], []
    for h in range(HG_HEADS):
        sl = slice(h * HG_DIM, (h + 1) * HG_DIM)
        st = st_t[h * HG_DIM:(h + 1) * HG_DIM, :]
        lbh = lb[:, sl]
        fg = lbh + (1.0 - lbh) * jax.nn.sigmoid(fl[:, sl])
        lf, kk, qf, v = jnp.log(fg), 1.0 - fg, jax.nn.silu(q[:, sl]), iv[:, sl]
        parts = []
        for s in range(q.shape[0] // HG_SUB):
            r = slice(s * HG_SUB, (s + 1) * HG_SUB)
            b = _dot(tri, lf[r], _NN, HI)
            b_mid = jnp.sum(lf[r][:HG_SUB // 2], axis=0, keepdims=True)
            b_end = jnp.sum(lf[r], axis=0, keepdims=True)
            sc = _dot(qf[r] * jnp.exp(b - b_mid), kk[r] * jnp.exp(b_mid - b), _NT) * tri
            parts.append(_dot(sc, v[r], _NN) + _dot(qf[r] * jnp.exp(b), st, _NT))
            st = st * jnp.exp(b_end) + _dot(v[r], kk[r] * jnp.exp(b_end - b), _TN)
        o = jnp.concatenate(parts, axis=0)
        outs.append(_rms(o, gain[:, sl]) * jax.nn.silu(g[:, sl]))
        states.append(st)
    return jnp.concatenate(states, axis=0), jnp.concatenate(outs, axis=1)


def _hg_specs(proj, nb):
    return [pl.BlockSpec((HG_BLOCK, HG_WIDTH), lambda i, c=c, f=nb: (f(i), c)) for c in range(4)]


def hgrn2_fwd(proj, lb, gain):
    L = proj.shape[0]
    n = L // HG_BLOCK

    def body(q, fl, iv, g, lb_r, gain_r, o_ref, st_ref, st):
        @pl.when(pl.program_id(0) == 0)
        def _():
            st[...] = jnp.zeros(st.shape, F32)

        st_ref[0] = st[...]
        new, o = _hg_block(st[...], q[...], fl[...], iv[...], g[...], lb_r[...], gain_r[...])
        st[...] = new
        o_ref[...] = o.astype(o_ref.dtype)

    pspec = pl.BlockSpec((1, HG_WIDTH), lambda i: (0, 0))
    return pl.pallas_call(
        body, name="hgrn2_fwd", grid=(n,),
        in_specs=_hg_specs(proj, lambda i: i) + [pspec, pspec],
        out_specs=[pl.BlockSpec((HG_BLOCK, HG_WIDTH), lambda i: (i, 0)),
                   pl.BlockSpec((1, HG_WIDTH, HG_DIM), lambda i: (i, 0, 0))],
        out_shape=[jax.ShapeDtypeStruct((L, HG_WIDTH), MXU_DTYPE),
                   jax.ShapeDtypeStruct((n, HG_WIDTH, HG_DIM), F32)],
        scratch_shapes=[pltpu.VMEM((HG_WIDTH, HG_DIM), F32)],
        compiler_params=_cparams(("arbitrary",)),
    )(proj, proj, proj, proj, lb, gain)


def hgrn2_bwd(proj, lb, gain, states, do):
    L = proj.shape[0]
    n = L // HG_BLOCK

    def body(q, fl, iv, g, lb_r, gain_r, st_r, do_r, dproj, dlb, dgain, dst):
        @pl.when(pl.program_id(0) == 0)
        def _():
            dst[...] = jnp.zeros(dst.shape, F32)
            dlb[...] = jnp.zeros(dlb.shape, F32)
            dgain[...] = jnp.zeros(dgain.shape, F32)

        _, vjp = jax.vjp(_hg_block, st_r[0], q[...], fl[...], iv[...], g[...], lb_r[...], gain_r[...])
        d_st, dq, dfl, div, dg, d_lb, d_gain = vjp((dst[...], do_r[...].astype(F32)))
        dst[...] = d_st
        dproj[:, 0 * HG_WIDTH:1 * HG_WIDTH] = dq
        dproj[:, 1 * HG_WIDTH:2 * HG_WIDTH] = dfl
        dproj[:, 2 * HG_WIDTH:3 * HG_WIDTH] = div
        dproj[:, 3 * HG_WIDTH:4 * HG_WIDTH] = dg
        dlb[...] += d_lb
        dgain[...] += d_gain

    rev = lambda i: n - 1 - i
    pspec = pl.BlockSpec((1, HG_WIDTH), lambda i: (0, 0))
    return pl.pallas_call(
        body, name="hgrn2_bwd", grid=(n,),
        in_specs=_hg_specs(proj, rev) + [pspec, pspec,
                                         pl.BlockSpec((1, HG_WIDTH, HG_DIM), lambda i: (rev(i), 0, 0)),
                                         pl.BlockSpec((HG_BLOCK, HG_WIDTH), lambda i: (rev(i), 0))],
        out_specs=[pl.BlockSpec((HG_BLOCK, 4 * HG_WIDTH), lambda i: (rev(i), 0)), pspec, pspec],
        out_shape=[jax.ShapeDtypeStruct((L, 4 * HG_WIDTH), F32),
                   jax.ShapeDtypeStruct((1, HG_WIDTH), F32), jax.ShapeDtypeStruct((1, HG_WIDTH), F32)],
        scratch_shapes=[pltpu.VMEM((HG_WIDTH, HG_DIM), F32)],
        compiler_params=_cparams(("arbitrary",)),
    )(proj, proj, proj, proj, lb, gain, states, do)


def _rope_rms(x, gain_p, cos_p, sin_p):
    n = x * lax.rsqrt(jnp.sum(x * x, axis=-1, keepdims=True) * (1.0 / MLA_ROPE) + EPS) * gain_p
    r = lax.broadcasted_iota(jnp.int32, (128, 128), 0)
    c = lax.broadcasted_iota(jnp.int32, (128, 128), 1)
    swap = (r == (c + 64) % 128).astype(F32)
    return n * cos_p + _dot(n, swap, _NN, HI) * sin_p


MLA_IN = MLA_Q_RANK + MLA_KV_RANK + 128


def _mla_prep(x, cos_p, sin_p, q_a, w_uq, kv_a, w_ukv, qn_nope, qn_rope, kn_nope, kn_rope):
    c_q, c_kv, kpe = x[:, :MLA_Q_RANK], x[:, MLA_Q_RANK:MLA_Q_RANK + MLA_KV_RANK], x[:, MLA_Q_RANK + MLA_KV_RANK:]
    q = _dot(_rms(c_q, q_a), w_uq, _NN)
    kv = _dot(_rms(c_kv, kv_a), w_ukv, _NN)
    k_pe = _rope_rms(kpe, kn_rope, cos_p, sin_p)
    qs, ks = [], []
    for h in range(MLA_HEADS):
        qs.append(_rms(q[:, h * MLA_DK:h * MLA_DK + MLA_NOPE], qn_nope))
        qs.append(_rope_rms(q[:, h * MLA_DK + MLA_NOPE:(h + 1) * MLA_DK], qn_rope, cos_p, sin_p))
        ks.append(_rms(kv[:, h * MLA_NOPE:(h + 1) * MLA_NOPE], kn_nope))
        ks.append(k_pe)
    return jnp.concatenate(qs, axis=1), jnp.concatenate(ks, axis=1), kv[:, MLA_HEADS * MLA_NOPE:]


def _mla_prep_opds(proj, cos_p, sin_p, params, tm, grads):
    g = (lambda k: k) if grads else (lambda k: None)
    assert (4 * HG_WIDTH) % MLA_IN == 0
    return ([rows(proj, tm, g('blk'), col=4 * HG_WIDTH // MLA_IN, width=MLA_IN), rows(cos_p, tm), rows(sin_p, tm)]
            + [full(p, g('acc')) for p in params])


def mla_prep_fwd(proj, cos_p, sin_p, params, tm):
    L = proj.shape[0]
    W = MLA_HEADS * MLA_DK
    rb = lambda w: (tm, w)
    outs = [((L, W), MXU_DTYPE, rb(W), lambda i: (i, 0)), ((L, W), MXU_DTYPE, rb(W), lambda i: (i, 0)),
            ((L, MLA_HEADS * MLA_V), MXU_DTYPE, rb(MLA_HEADS * MLA_V), lambda i: (i, 0))]
    return blocked_fwd(_mla_prep, _mla_prep_opds(proj, cos_p, sin_p, params, tm, False), outs, L // tm, "mla_prep_fwd")


def mla_prep_bwd(proj, cos_p, sin_p, params, dq, dk, dv, tm):
    L = proj.shape[0]
    return blocked_bwd(_mla_prep, _mla_prep_opds(proj, cos_p, sin_p, params, tm, True),
                       [rows(dq, tm), rows(dk, tm), rows(dv, tm)], L // tm, "mla_prep_bwd")


def _scores(q, k, scale, shift=None):
    s = _dot(q, k, _NT) * scale
    if shift is None:
        return s
    row = lax.broadcasted_iota(jnp.int32, s.shape, 0)
    col = lax.broadcasted_iota(jnp.int32, s.shape, 1)
    return jnp.where(col <= row + shift, s, -jnp.inf)


ATTN_ROWS = 512
ATTN_WIDE = 2


def attn_fwd(q, k, v, scale, t):
    L = q.shape[0]
    tq = ATTN_WIDE * t

    def body(q_ref, k_ref, v_ref, o_ref, lse_ref):
        i = pl.program_id(1)
        qb = q_ref[...]

        def step(j, carry, shift=None):
            m, l, acc = carry
            kj = k_ref[pl.ds(pl.multiple_of(j * t, t), t), :]
            vj = v_ref[pl.ds(pl.multiple_of(j * t, t), t), :]
            s = _scores(qb, kj, scale, shift)
            m_new = jnp.maximum(m, jnp.max(s, axis=-1, keepdims=True))
            p = jnp.exp(s - m_new)
            alpha = jnp.exp(m - m_new)
            return m_new, alpha * l + jnp.sum(p, axis=-1, keepdims=True), alpha * acc + _dot(p, vj, _NN)

        carry = (jnp.full((tq, 1), -jnp.inf, F32), jnp.zeros((tq, 1), F32), jnp.zeros((tq, MLA_V), F32))
        carry = lax.fori_loop(0, ATTN_WIDE * i, step, carry)
        for d in range(ATTN_WIDE):
            carry = step(ATTN_WIDE * i + d, carry, -d * t)
        m, l, acc = carry
        o_ref[...] = acc / l
        lse_ref[...] = jnp.broadcast_to(m + jnp.log(l), lse_ref.shape)

    hspec = lambda rows_, w: pl.BlockSpec((rows_, w), lambda h, i: (0, h))
    bspec = lambda w: pl.BlockSpec((tq, w), lambda h, i: (i, h))
    return pl.pallas_call(
        body, name="attn_fwd", grid=(MLA_HEADS, L // tq),
        in_specs=[bspec(MLA_DK), hspec(L, MLA_DK), hspec(L, MLA_V)],
        out_specs=[bspec(MLA_V), bspec(MLA_V)],
        out_shape=[jax.ShapeDtypeStruct((L, MLA_HEADS * MLA_V), F32)] * 2,
        compiler_params=_cparams(("parallel", "parallel")),
    )(q, k, v)


def attn_bwd_dq(q, k, v, o, lse, do, scale, t):
    L = q.shape[0]
    tq = ATTN_WIDE * t

    def body(q_ref, k_ref, v_ref, o_ref, lse_ref, do_ref, dq_ref):
        i = pl.program_id(1)
        qb, dob = q_ref[...], do_ref[...]
        delta = jnp.sum(dob * o_ref[...], axis=-1, keepdims=True)
        lse_c = jnp.max(lse_ref[...], axis=-1, keepdims=True)

        def step(j, dq, shift=None):
            kj = k_ref[pl.ds(pl.multiple_of(j * t, t), t), :]
            vj = v_ref[pl.ds(pl.multiple_of(j * t, t), t), :]
            p = jnp.exp(_scores(qb, kj, scale, shift) - lse_c)
            ds = p * (_dot(dob, vj, _NT) - delta) * scale
            return dq + _dot(ds, kj, _NN)

        dq = lax.fori_loop(0, ATTN_WIDE * i, step, jnp.zeros((tq, MLA_DK), F32))
        for d in range(ATTN_WIDE):
            dq = step(ATTN_WIDE * i + d, dq, -d * t)
        dq_ref[...] = dq

    hspec = lambda w: pl.BlockSpec((L, w), lambda h, i: (0, h))
    bspec = lambda w: pl.BlockSpec((tq, w), lambda h, i: (i, h))
    return pl.pallas_call(
        body, name="attn_bwd_dq", grid=(MLA_HEADS, L // tq),
        in_specs=[bspec(MLA_DK), hspec(MLA_DK), hspec(MLA_V), bspec(MLA_V), bspec(MLA_V), bspec(MLA_V)],
        out_specs=bspec(MLA_DK),
        out_shape=jax.ShapeDtypeStruct((L, MLA_HEADS * MLA_DK), F32),
        compiler_params=_cparams(("parallel", "parallel")),
    )(q, k, v, o, lse, do)


def attn_bwd_dkv(q, k, v, o, lse, do, scale, t):
    L = q.shape[0]
    tk = ATTN_WIDE * t

    def body(q_ref, k_ref, v_ref, o_ref, lse_ref, do_ref, dk_ref, dv_ref):
        j = pl.program_id(1)
        kb, vb = k_ref[...], v_ref[...]

        def step(i, carry, shift=None):
            dk, dv = carry
            r = pl.ds(pl.multiple_of(i * t, t), t)
            qi, doi = q_ref[r, :], do_ref[r, :]
            delta = jnp.sum(doi * o_ref[r, :], axis=-1, keepdims=True)
            lse_c = jnp.max(lse_ref[r, :], axis=-1, keepdims=True)
            p = jnp.exp(_scores(qi, kb, scale, shift) - lse_c)
            ds = p * (_dot(doi, vb, _NT) - delta) * scale
            return dk + _dot(ds, qi, _TN), dv + _dot(p, doi, _TN)

        carry = (jnp.zeros((tk, MLA_DK), F32), jnp.zeros((tk, MLA_V), F32))
        for d in range(ATTN_WIDE):
            carry = step(ATTN_WIDE * j + d, carry, d * t)
        dk, dv = lax.fori_loop(ATTN_WIDE * (j + 1), L // t, step, carry)
        dk_ref[...] = dk
        dv_ref[...] = dv

    hspec = lambda w: pl.BlockSpec((L, w), lambda h, j: (0, h))
    bspec = lambda w: pl.BlockSpec((tk, w), lambda h, j: (j, h))
    return pl.pallas_call(
        body, name="attn_bwd_dkv", grid=(MLA_HEADS, L // tk),
        in_specs=[hspec(MLA_DK), bspec(MLA_DK), bspec(MLA_V), hspec(MLA_V), hspec(MLA_V), hspec(MLA_V)],
        out_specs=[bspec(MLA_DK), bspec(MLA_V)],
        out_shape=[jax.ShapeDtypeStruct((L, MLA_HEADS * MLA_DK), F32), jax.ShapeDtypeStruct((L, MLA_HEADS * MLA_V), F32)],
        compiler_params=_cparams(("parallel", "parallel")),
    )(q, k, v, o, lse, do)


S5_LANES = S5_GB * S5_STATE


def _cmul(ar, ai, br, bi):
    return ar * br - ai * bi, ar * bi + ai * br


def _a_powers(ar, ai, reverse):
    a2 = _cmul(ar, ai, ar, ai)
    a4 = _cmul(*a2, *a2)
    row = lax.broadcasted_iota(jnp.int32, (8, ar.shape[1]), 0)
    e = (8 - row) if reverse else (row + 1)
    tr, ti = jnp.ones((8, ar.shape[1]), F32), jnp.zeros((8, ar.shape[1]), F32)
    for bit, (pr, pi) in ((1, (ar, ai)), (2, a2), (4, a4), (8, _cmul(*a4, *a4))):
        nr, ni = _cmul(tr, ti, pr, pi)
        sel = (e & bit) != 0
        tr, ti = jnp.where(sel, nr, tr), jnp.where(sel, ni, ti)
    pows = []
    for d, (pr, pi) in zip((1, 2, 4), ((ar, ai), a2, a4)):
        keep = (row < 8 - d) if reverse else (row >= d)
        pows.append((jnp.where(keep, pr, 0.0), jnp.where(keep, pi, 0.0)))
    return pows, (tr, ti)


def _scan8(xr, xi, pows, table, cr, ci, reverse):
    for d, (pr, pi) in zip((1, 2, 4), pows):
        shift = 8 - d if reverse else d
        mr, mi = _cmul(pr, pi, pltpu.roll(xr, shift, 0), pltpu.roll(xi, shift, 0))
        xr, xi = xr + mr, xi + mi
    mr, mi = _cmul(table[0], table[1], cr, ci)
    return xr + mr, xi + mi


def _row_of(x, r):
    row = lax.broadcasted_iota(jnp.int32, x.shape, 0)
    return jnp.sum(jnp.where(row == r, x, 0.0), axis=0, keepdims=True)


def _s5_scan_fwd(h_re, h_im, ar, ai, L):
    pows, table = _a_powers(ar, ai, False)

    def step(i, carry):
        r = pl.ds(pl.multiple_of(i * 8, 8), 8)
        xr, xi = _scan8(h_re[r, :], h_im[r, :], pows, table, carry[0], carry[1], False)
        h_re[r, :] = xr
        h_im[r, :] = xi
        return xr[7:8, :], xi[7:8, :]

    z = jnp.zeros((1, ar.shape[1]), F32)
    lax.fori_loop(0, L // 8, step, (z, z))


def _s5_specs(L):
    return [pl.BlockSpec((L, 128), lambda g: (0, g)),
            pl.BlockSpec((1, 128, S5_LANES), lambda g: (g, 0, 0)), pl.BlockSpec((1, 128, S5_LANES), lambda g: (g, 0, 0)),
            pl.BlockSpec((1, 1, S5_LANES), lambda g: (g, 0, 0)), pl.BlockSpec((1, 1, S5_LANES), lambda g: (g, 0, 0)),
            pl.BlockSpec((1, S5_LANES, 128), lambda g: (g, 0, 0)), pl.BlockSpec((1, S5_LANES, 128), lambda g: (g, 0, 0))]


def s5_fwd(u, w_re, w_im, a_re, a_im, c_re, c_im):
    L, D = u.shape

    def body(u_ref, wr, wi, ar, ai, cr, ci, y_ref, h_re, h_im):
        ub = u_ref[...]
        h_re[...] = _dot(ub, wr[0], _NN)
        h_im[...] = _dot(ub, wi[0], _NN)
        _s5_scan_fwd(h_re, h_im, ar[0], ai[0], L)
        y_ref[...] = _dot(h_re[...], cr[0], _NN) - _dot(h_im[...], ci[0], _NN)

    return pl.pallas_call(
        body, name="s5_fwd", grid=(D // 128,),
        in_specs=_s5_specs(L), out_specs=pl.BlockSpec((L, 128), lambda g: (0, g)),
        out_shape=jax.ShapeDtypeStruct((L, D), F32),
        scratch_shapes=[pltpu.VMEM((L, S5_LANES), F32), pltpu.VMEM((L, S5_LANES), F32)],
        compiler_params=_cparams(("parallel",)),
    )(u, w_re, w_im, a_re, a_im, c_re, c_im)


def s5_bwd(u, w_re, w_im, a_re, a_im, c_re, c_im, dy, tc):
    L, D = u.shape
    nch = L // tc

    def body(u_ref, wr, wi, ar_ref, ai_ref, cr, ci, dy_ref, du_ref, dwr, dwi, dar, dai, dcr, dci, h_re, h_im, g_re, g_im):
        ar, ai = ar_ref[0], ai_ref[0]
        ub = u_ref[...]
        h_re[...] = _dot(ub, wr[0], _NN)
        h_im[...] = _dot(ub, wi[0], _NN)
        _s5_scan_fwd(h_re, h_im, ar, ai, L)
        dyb = dy_ref[...]
        dcr[0] = _dot(h_re[...], dyb, _TN)
        dci[0] = -_dot(h_im[...], dyb, _TN)
        pows, table = _a_powers(ar, -ai, True)
        dwr[0] = jnp.zeros((128, S5_LANES), F32)
        dwi[0] = jnp.zeros((128, S5_LANES), F32)
        z1 = jnp.zeros((1, S5_LANES), F32)
        z8 = jnp.zeros((8, S5_LANES), F32)

        def chunk(cc, carry):
            c0 = pl.multiple_of((nch - 1 - cc) * tc, tc)
            rows_c = pl.ds(c0, tc)
            dyc = dy_ref[rows_c, :]
            g_re[...] = _dot(dyc, cr[0], _NT)
            g_im[...] = -_dot(dyc, ci[0], _NT)

            def step(ii, cy):
                gr_c, gi_c, acc_r, acc_i = cy
                i8 = pl.multiple_of((tc // 8 - 1 - ii) * 8, 8)
                rl = pl.ds(i8, 8)
                xr, xi = _scan8(g_re[rl, :], g_im[rl, :], pows, table, gr_c, gi_c, True)
                g_re[rl, :] = xr
                g_im[rl, :] = xi
                t0 = c0 + i8
                hb_r, hb_i = h_re[pl.ds(t0, 8), :], h_im[pl.ds(t0, 8), :]
                tp = pl.multiple_of(jnp.maximum(t0 - 8, 0), 8)
                first = (t0 > 0).astype(F32)
                pr = h_re[pl.ds(tp, 8), :][7:8, :] * first
                pi = h_im[pl.ds(tp, 8), :][7:8, :] * first
                row = lax.broadcasted_iota(jnp.int32, xr.shape, 0)
                hp_r = jnp.where(row == 0, pr, pltpu.roll(hb_r, 1, 0))
                hp_i = jnp.where(row == 0, pi, pltpu.roll(hb_i, 1, 0))
                return (xr[0:1, :], xi[0:1, :],
                        acc_r + xr * hp_r + xi * hp_i, acc_i + xi * hp_r - xr * hp_i)

            cy = lax.fori_loop(0, tc // 8, step, carry)
            uc = u_ref[rows_c, :]
            gr, gi = g_re[...], g_im[...]
            du_ref[rows_c, :] = _dot(gr, wr[0], _NT) + _dot(gi, wi[0], _NT)
            dwr[0] += _dot(uc, gr, _TN)
            dwi[0] += _dot(uc, gi, _TN)
            return cy

        _, _, acc_r, acc_i = lax.fori_loop(0, nch, chunk, (z1, z1, z8, z8))
        dar[0] = jnp.sum(acc_r, axis=0, keepdims=True)
        dai[0] = jnp.sum(acc_i, axis=0, keepdims=True)

    specs = _s5_specs(L)
    return pl.pallas_call(
        body, name="s5_bwd", grid=(D // 128,),
        in_specs=specs + [pl.BlockSpec((L, 128), lambda g: (0, g))],
        out_specs=[pl.BlockSpec((L, 128), lambda g: (0, g))] + specs[1:],
        out_shape=[jax.ShapeDtypeStruct((L, D), F32)] + [jax.ShapeDtypeStruct(x.shape, F32)
                                                        for x in (w_re, w_im, a_re, a_im, c_re, c_im)],
        scratch_shapes=[pltpu.VMEM((L, S5_LANES), F32), pltpu.VMEM((L, S5_LANES), F32),
                        pltpu.VMEM((tc, S5_LANES), F32), pltpu.VMEM((tc, S5_LANES), F32)],
        compiler_params=_cparams(("parallel",)),
    )(u, w_re, w_im, a_re, a_im, c_re, c_im, dy)


def _s5_discretize(lr, li, ldt, br, bi):
    dt = jnp.exp(ldt)
    mag = jnp.exp(lr * dt)
    ar, ai = mag * jnp.cos(li * dt), mag * jnp.sin(li * dt)
    den = lr * lr + li * li
    zr = ((ar - 1.0) * lr + ai * li) / den
    zi = (ai * lr - (ar - 1.0) * li) / den
    p = lax.broadcasted_iota(jnp.int32, (S5_STATE, S5_STATE * S5_GROUP), 0)
    c = lax.broadcasted_iota(jnp.int32, (S5_STATE, S5_STATE * S5_GROUP), 1)
    rep = (c // S5_GROUP == p).astype(F32)
    zr, zi = _dot(zr, rep, _NN, HI), _dot(zi, rep, _NN, HI)
    return ar, ai, zr * br - zi * bi, zr * bi + zi * br


def _conv_shift(x, d):
    row = lax.broadcasted_iota(jnp.int32, x.shape, 0)
    return jnp.where(row >= d, pltpu.roll(x, d, 0), 0.0)


def _conv_unshift(x, d):
    n = x.shape[0]
    row = lax.broadcasted_iota(jnp.int32, x.shape, 0)
    return jnp.where(row < n - d, pltpu.roll(x, n - d, 0), 0.0)


@functools.partial(jax.custom_vjp, nondiff_argnums=(1,))
def _shift_rows(x, d):
    return _conv_shift(x, d)


_shift_rows.defvjp(lambda x, d: (_conv_shift(x, d), None), lambda d, _, g: (_conv_unshift(g, d),))


def _conv_gate(ug, uv, wg0, wg1, wg2, wv0, wv1, wv2, bg, bv):
    def conv(u, w0, w1, w2, b):
        return u * w2 + _shift_rows(u, 1) * w1 + _shift_rows(u, 2) * w0 + b
    return (jax.nn.silu(conv(ug, wg0, wg1, wg2, bg)) * conv(uv, wv0, wv1, wv2, bv),)


def _rms_fn(x, gain):
    return (_rms(x, gain),)


def _softmax_rows(s):
    e = jnp.exp(s - lax.stop_gradient(jnp.max(s, axis=-1, keepdims=True)))
    return e / jnp.sum(e, axis=-1, keepdims=True)


def _xa_core(qp, k, v, q_gain):
    dh = qp.shape[1] // XA_HEADS
    outs = []
    for h in range(XA_HEADS):
        sl = slice(h * dh, (h + 1) * dh)
        p = _softmax_rows(_dot(_rms(qp[:, sl], q_gain), k[:, sl], _NT) * (dh ** -0.5))
        outs.append(_dot(p, v[:, sl], _NN))
    return (jnp.concatenate(outs, axis=1),)


def _mem_kv(mem, mem_gain, wk, wv, k_gain):
    m = _rms(mem, mem_gain)
    kp = _dot(m, wk, _NN)
    dh = kp.shape[1] // XA_HEADS
    k = jnp.concatenate([_rms(kp[:, h * dh:(h + 1) * dh], k_gain) for h in range(XA_HEADS)], axis=1)
    return k, _dot(m, wv, _NN)


def _s5_post(y, u, d):
    return (jax.nn.gelu(y + d * u),)


def _glu(a, b):
    return (a * jax.nn.sigmoid(b),)


def _lb_first(logits):
    e = jnp.exp(logits - lax.stop_gradient(jnp.max(logits, axis=0, keepdims=True)))
    return (_row_of(e, 0) / jnp.sum(e, axis=0, keepdims=True),)


def _loss_fn(y, t):
    e = y - t
    part = 0.5 * jnp.sum(e * e) / y.shape[1]
    return e * (1.0 / y.shape[1]), jnp.full((8, 128), part / (8 * 128), F32)


def _out(shape, dtype, tm):
    return (shape, dtype, (tm, shape[1]), lambda i: (i, 0))


def rms_fwd(h, gain, tm, dtype):
    return blocked_fwd(_rms_fn, [rows(h, tm), full(gain)], [_out(h.shape, dtype, tm)], h.shape[0] // tm, "rms_fwd")[0]


def rms_bwd(h, gain, dy, tm, residual):
    return blocked_bwd(_rms_fn, [rows(h, tm, 'blk'), full(gain, 'acc')], [rows(dy, tm)], h.shape[0] // tm, "rms_bwd",
                       plus=rows(residual, tm))


def _adamw_math(w, g, m, v):
    m = ADAM_B1 * m + (1.0 - ADAM_B1) * g
    v = ADAM_B2 * v + (1.0 - ADAM_B2) * jnp.square(g)
    m_hat = m / (1.0 - ADAM_B1 ** ADAM_STEP)
    v_hat = v / (1.0 - ADAM_B2 ** ADAM_STEP)
    return -ADAM_LR * (m_hat / (jnp.sqrt(v_hat) + ADAM_EPS) + ADAM_WD * w), m, v


def adamw(w, g, m, v, name):
    R = w.shape[0]
    tm = _tile(R, 256)
    assert g.shape == w.shape == m.shape == v.shape, (name, w.shape, g.shape)

    def body(w_ref, g_ref, m_ref, v_ref, d_ref, nm_ref, nv_ref):
        d_ref[...], nm_ref[...], nv_ref[...] = _adamw_math(w_ref[...], g_ref[...], m_ref[...], v_ref[...])

    spec = pl.BlockSpec((tm, w.shape[1]), lambda i: (i, 0))
    return pl.pallas_call(
        body, name=name, grid=(R // tm,), in_specs=[spec] * 4, out_specs=[spec] * 3,
        out_shape=[jax.ShapeDtypeStruct(w.shape, F32)] * 3, compiler_params=_cparams(("parallel",)),
    )(w, g, m, v)


def adamw_layer(w, g, m, v, layer, bufs, name):
    R, C = g.shape
    tm = _tile(R, 256)
    assert w.shape[1:] == (R, C) and all(b.shape == w.shape for b in bufs), (name, w.shape, g.shape)

    def body(w_ref, g_ref, m_ref, v_ref, *rest):
        g_out, d_ref, nm_ref, nv_ref = rest[-4:]
        g_ = g_ref[...]
        g_out[...] = g_
        d_ref[...], nm_ref[...], nv_ref[...] = _adamw_math(w_ref[...], g_, m_ref[...], v_ref[...])

    lspec = pl.BlockSpec((None, tm, C), lambda i: (layer, i, 0))
    any_spec = pl.BlockSpec(memory_space=pl.ANY)
    return pl.pallas_call(
        body, name=name, grid=(R // tm,),
        in_specs=[lspec, pl.BlockSpec((tm, C), lambda i: (i, 0)), lspec, lspec] + [any_spec] * 4,
        out_specs=[lspec] * 4, out_shape=[jax.ShapeDtypeStruct(w.shape, F32)] * 4,
        input_output_aliases={4: 0, 5: 1, 6: 2, 7: 3}, compiler_params=_cparams(("parallel",)),
    )(w, g, m, v, *bufs)


def add2(x, y, name, out_dtype=F32):
    shape = x.shape
    x, y = x.reshape(-1, shape[-1]), y.reshape(-1, shape[-1])
    R, C = x.shape
    tm = _tile(R, 256)

    def body(x_ref, y_ref, o_ref):
        o_ref[...] = (x_ref[...] + y_ref[...]).astype(o_ref.dtype)

    spec = pl.BlockSpec((tm, C), lambda i: (i, 0))
    return pl.pallas_call(
        body, name=name, grid=(R // tm,), in_specs=[spec, spec], out_specs=spec,
        out_shape=jax.ShapeDtypeStruct((R, C), out_dtype), compiler_params=_cparams(("parallel",)),
    )(x, y).reshape(shape)


def add_chips(own, got, name):
    n, R, C = got.shape
    tm = _tile(R, 256)

    def body(*refs):
        acc = refs[0][...].astype(F32)
        for r in refs[1:-1]:
            acc = acc + r[...].astype(F32)
        refs[-1][...] = acc

    return pl.pallas_call(
        body, name=name, grid=(R // tm,),
        in_specs=[pl.BlockSpec((tm, C), lambda i: (i, 0))] + [pl.BlockSpec((None, tm, C), lambda i, j=j: (j, i, 0))
                                                            for j in range(n)],
        out_specs=pl.BlockSpec((tm, C), lambda i: (i, 0)),
        out_shape=jax.ShapeDtypeStruct((R, C), F32), compiler_params=_cparams(("parallel",)),
    )(own, *([got] * n))


_HBM = pl.BlockSpec(memory_space=pltpu.HBM)
N_CHIPS = 4


def _my_place():
    return lax.axis_index("x"), lax.axis_index("y"), lax.axis_index("c")


def _window(ref, axis, start, size):
    idx = [slice(None)] * len(ref.shape)
    idx[axis] = pl.ds(start, size)
    return ref.at[tuple(idx)]


def _comm_call(body, name, xs, out_shapes, n_remote, n_local, sequencer=None):
    sems = [pltpu.SemaphoreType.DMA((n_remote,)), pltpu.SemaphoreType.DMA((n_remote,)),
            pltpu.SemaphoreType.DMA((max(n_local, 1),))]
    if sequencer is None:
        return pl.pallas_call(
            body, name=name, in_specs=[_HBM] * len(xs), out_specs=[_HBM] * len(out_shapes), out_shape=out_shapes,
            scratch_shapes=sems, compiler_params=pltpu.CompilerParams(has_side_effects=True),
        )(*xs)
    peers_of, collective_id = sequencer
    hbm = pltpu.MemorySpace.HBM
    x_refs = [jax.new_ref(x, memory_space=hbm) for x in xs]
    o_refs = [jax.empty_ref(s, memory_space=hbm) for s in out_shapes]

    @pl.kernel(mesh=plsc.ScalarSubcoreMesh(axis_name="sequencer", num_cores=1), name=name, scratch_types=tuple(sems),
               compiler_params=pltpu.CompilerParams(collective_id=collective_id))
    def launch(send_sems, recv_sems, local_sems):
        peers = peers_of(*_my_place())
        barrier = pltpu.get_barrier_semaphore()
        for peer in peers:
            pl.semaphore_signal(barrier, inc=1, device_id=peer, device_id_type=MESH)
        pl.semaphore_wait(barrier, len(peers))
        body(*x_refs, *o_refs, send_sems, recv_sems, local_sems)

    launch()
    return [o[...] for o in o_refs]


def _sibling(mx, my, mc):
    return [(mx, my, 1 - mc)]


def _same_core_of_other_chips(mx, my, mc):
    return [(tx, ty, mc) for tx, ty in _other_chips(mx, my)]


def _run(copies):
    for cp in copies:
        cp.start()
    for cp in copies:
        cp.wait()


def _other_chips(mx, my):
    return [(mx ^ (j >> 1), my ^ (j & 1)) for j in (1, 2, 3)]


def chip_gather(xs, axes, name):
    n = len(xs)
    shapes, final = [], []
    for x, ax in zip(xs, axes):
        s = list(x.shape)
        if ax is None:
            shapes.append([N_CHIPS] + s)
            final.append(shapes[-1])
        elif ax < x.ndim - 1:
            shapes.append(s[:ax] + [N_CHIPS] + s[ax:])
            final.append(s[:ax] + [N_CHIPS * s[ax]] + s[ax + 1:])
        else:
            assert s[ax] % 128 == 0, (name, s)
            shapes.append(s[:ax] + [N_CHIPS * s[ax]])
            final.append(shapes[-1])

    def body(*refs):
        x_refs, o_refs = refs[:n], refs[n:2 * n]
        send_sems, recv_sems, local_sems = refs[2 * n:]
        mx, my, mc = _my_place()
        q = 2 * mx + my
        copies = []
        for i, (x_ref, o_ref, ax) in enumerate(zip(x_refs, o_refs, axes)):
            if ax is None or ax < len(x_ref.shape) - 1:
                dst = o_ref.at[(slice(None),) * (ax or 0) + (q,)]
            else:
                dst = _window(o_ref, ax, q * x_ref.shape[ax], x_ref.shape[ax])
            copies.append(pltpu.make_async_copy(x_ref, dst, local_sems.at[i]))
            for j, (tx, ty) in enumerate(_other_chips(mx, my)):
                copies.append(pltpu.make_async_remote_copy(
                    src_ref=x_ref, dst_ref=dst, send_sem=send_sems.at[3 * i + j], recv_sem=recv_sems.at[3 * i + j],
                    device_id=(tx, ty, mc), device_id_type=MESH))
        _run(copies)

    out_shapes = [jax.ShapeDtypeStruct(tuple(s), x.dtype) for s, x in zip(shapes, xs)]
    return [o.reshape(f) for o, f in zip(_comm_call(body, name, xs, out_shapes, 3 * n, n), final)]


def gather_two_level(xs, name):
    n = len(xs)
    shapes = [jax.ShapeDtypeStruct((2, N_CHIPS) + x.shape[1:], x.dtype) for x in xs]

    def body(*refs):
        x_refs, o_refs = refs[:n], refs[n:2 * n]
        send_sems, recv_sems, local_sems = refs[2 * n:]
        mx, my, mc = _my_place()
        q = 2 * mx + my
        first, local, second = [], [], []
        for i, (x_ref, o_ref) in enumerate(zip(x_refs, o_refs)):
            local.append(pltpu.make_async_copy(x_ref.at[mc], o_ref.at[mc, q], local_sems.at[i]))
            for j, (tx, ty) in enumerate(_other_chips(mx, my)):
                first.append(pltpu.make_async_remote_copy(
                    src_ref=x_ref.at[mc], dst_ref=o_ref.at[mc, q], send_sem=send_sems.at[4 * i + j],
                    recv_sem=recv_sems.at[4 * i + j], device_id=(tx, ty, mc), device_id_type=MESH))
            second.append(pltpu.make_async_remote_copy(
                src_ref=o_ref.at[mc], dst_ref=o_ref.at[mc], send_sem=send_sems.at[4 * i + 3],
                recv_sem=recv_sems.at[4 * i + 3], device_id=(mx, my, 1 - mc), device_id_type=MESH))
        for cp in local + first:
            cp.start()
        for cp in local:
            cp.wait()
        for cp in first:
            cp.wait_recv()
        _run(second)
        for cp in first:
            cp.wait_send()

    return _comm_call(body, name, xs, shapes, 4 * n, n)


def gather_two_level_sequencer(xs, name, collective_id):
    n = len(xs)
    hbm = pltpu.MemorySpace.HBM
    x_refs = [jax.new_ref(x, memory_space=hbm) for x in xs]
    o_refs = [jax.empty_ref(jax.ShapeDtypeStruct((2, N_CHIPS) + x.shape[1:], x.dtype), memory_space=hbm) for x in xs]

    @pl.kernel(mesh=plsc.ScalarSubcoreMesh(axis_name="sequencer", num_cores=1), name=name,
               scratch_types=(pltpu.SemaphoreType.DMA((4 * n,)), pltpu.SemaphoreType.DMA((4 * n,)),
                              pltpu.SemaphoreType.DMA((n,))),
               compiler_params=pltpu.CompilerParams(collective_id=collective_id))
    def launch(send_sems, recv_sems, local_sems):
        mx, my, mc = _my_place()
        peers = [(tx, ty, mc) for tx, ty in _other_chips(mx, my)] + [(mx, my, 1 - mc)]
        barrier = pltpu.get_barrier_semaphore()
        for peer in peers:
            pl.semaphore_signal(barrier, inc=1, device_id=peer, device_id_type=MESH)
        pl.semaphore_wait(barrier, len(peers))
        q = 2 * mx + my
        first, local, second = [], [], []
        for i, (x_ref, o_ref) in enumerate(zip(x_refs, o_refs)):
            local.append(pltpu.make_async_copy(x_ref.at[mc], o_ref.at[mc, q], local_sems.at[i]))
            for j, peer in enumerate(peers[:3]):
                first.append(pltpu.make_async_remote_copy(
                    src_ref=x_ref.at[mc], dst_ref=o_ref.at[mc, q], send_sem=send_sems.at[4 * i + j],
                    recv_sem=recv_sems.at[4 * i + j], device_id=peer, device_id_type=MESH))
            second.append(pltpu.make_async_remote_copy(
                src_ref=o_ref.at[mc], dst_ref=o_ref.at[mc], send_sem=send_sems.at[4 * i + 3],
                recv_sem=recv_sems.at[4 * i + 3], device_id=peers[3], device_id_type=MESH))
        for cp in local + first:
            cp.start()
        for cp in local:
            cp.wait()
        for cp in first:
            cp.wait_recv()
        _run(second)
        for cp in first:
            cp.wait_send()

    launch()
    return [o[...] for o in o_refs]


def pair_swap(xs, name, halves, collective_id=None):
    n = len(xs)
    shapes = [jax.ShapeDtypeStruct(x.shape[:1] + x.shape[2:] if halves else x.shape, x.dtype) for x in xs]

    def body(*refs):
        x_refs, o_refs = refs[:n], refs[n:2 * n]
        send_sems, recv_sems, _ = refs[2 * n:]
        mx, my, mc = _my_place()
        _run([pltpu.make_async_remote_copy(
            src_ref=x_ref.at[:, 1 - mc] if halves else x_ref, dst_ref=o_ref, send_sem=send_sems.at[i],
            recv_sem=recv_sems.at[i], device_id=(mx, my, 1 - mc), device_id_type=MESH)
            for i, (x_ref, o_ref) in enumerate(zip(x_refs, o_refs))])

    return _comm_call(body, name, xs, shapes, n, 0, None if collective_id is None else (_sibling, collective_id))


def chip_all_to_all(xs, name, collective_id=None):
    n = len(xs)
    shapes = [jax.ShapeDtypeStruct((N_CHIPS - 1,) + x.shape[1:], x.dtype) for x in xs]

    def body(*refs):
        x_refs, o_refs = refs[:n], refs[n:2 * n]
        send_sems, recv_sems, _ = refs[2 * n:]
        mx, my, mc = _my_place()
        copies = []
        for i, (x_ref, o_ref) in enumerate(zip(x_refs, o_refs)):
            for j, (tx, ty) in enumerate(_other_chips(mx, my)):
                copies.append(pltpu.make_async_remote_copy(
                    src_ref=x_ref.at[2 * tx + ty], dst_ref=o_ref.at[j], send_sem=send_sems.at[3 * i + j],
                    recv_sem=recv_sems.at[3 * i + j], device_id=(tx, ty, mc), device_id_type=MESH))
        _run(copies)

    return _comm_call(body, name, xs, shapes, 3 * n, 0,
                      None if collective_id is None else (_same_core_of_other_chips, collective_id))


WEIGHTS = ['norm_mix', 'norm_xa', 'norm_mem', 'norm_ffn', 'xa_wq', 'xa_wk', 'xa_wv', 'xa_wo', 'xa_q_norm', 'xa_k_norm',
           'ffn_w_up', 'ffn_conv_w', 'ffn_conv_b', 'ffn_w_down', 'hg_lb_logits', 'mix_w_in', 'hg_out_norm',
           'mla_q_a_norm', 'mla_w_uq', 'mla_kv_a_norm', 'mla_w_ukv', 'mla_qn_nope', 'mla_qn_rope', 'mla_kn_nope',
           'mla_kn_rope', 'mix_w_out', 's5_lam_re', 's5_lam_im', 's5_log_dt', 's5_b_re', 's5_b_im', 's5_c_re',
           's5_c_im', 's5_d', 's5_w_glu_a', 's5_w_glu_b']
INPUTS = ['x', 'mem', 'positions'] + WEIGHTS + ['loss_target'] + ['m_' + n for n in WEIGHTS] + ['v_' + n for n in WEIGHTS]
SHARD_AXIS = {'xa_wq': 1, 'xa_wk': 1, 'xa_wv': 1, 'xa_wo': 1, 'ffn_w_up': 2, 'ffn_conv_w': 2, 'ffn_w_down': 1,
              'mix_w_in': 2, 'mla_w_uq': 2, 'mla_w_ukv': 2, 'mix_w_out': 1, 's5_d': 1, 's5_w_glu_a': 1, 's5_w_glu_b': 1}
BIG = ['xa_wq', 'xa_wk', 'xa_wv', 'xa_wo', 'ffn_w_up', 'ffn_w_down', 'mix_w_in', 'mix_w_out', 's5_w_glu_a', 's5_w_glu_b']
FIRST_NEEDED = ('mix_w_in', 'mix_w_out')
SMALL_SHARDED = [n for n in WEIGHTS if n in SHARD_AXIS and n not in BIG]
REPLICATED = [n for n in WEIGHTS if n not in SHARD_AXIS]
SMALL = SMALL_SHARDED + REPLICATED
PACK_W = 1024
ROW_MULT = 16
W_IN_SHARD = IN_WIDTH // N_CHIPS
W_IN_SHARD_PAD = 640


def _pack(flats, mult=ROW_MULT):
    flat = jnp.concatenate([f.reshape(-1) for f in flats])
    unit = mult * PACK_W
    n = -(-flat.shape[0] // unit) * unit
    return jnp.pad(flat, (0, n - flat.shape[0])).reshape(n // PACK_W, PACK_W)


def _unpack(packed, shapes):
    flat, out, o = packed.reshape(-1), [], 0
    for s in shapes:
        n = math.prod(s)
        out.append(flat[o:o + n].reshape(s))
        o += n
    return out


def _rope_pad(w):
    z = jnp.zeros(w.shape[:-1] + (MLA_ROPE // 2,), w.dtype)
    return jnp.concatenate([w[..., :MLA_ROPE // 2], z, w[..., MLA_ROPE // 2:], z], axis=-1)


def _rope_unpad(g):
    return jnp.concatenate([g[..., :MLA_ROPE // 2], g[..., 64:64 + MLA_ROPE // 2]], axis=-1)


def _blockdiag_in(bb):
    nb = bb.shape[0] // S5_GB
    t = bb.reshape(nb, S5_GB, S5_STATE, S5_GROUP).transpose(0, 1, 3, 2)
    return jnp.einsum('bgmp,gh->bgmhp', t, jnp.eye(S5_GB, dtype=bb.dtype)).reshape(nb, S5_GB * S5_GROUP, S5_LANES)


def _blockdiag_in_t(dw):
    nb = dw.shape[0]
    t = jnp.einsum('bgmhp,gh->bgmp', dw.reshape(nb, S5_GB, S5_GROUP, S5_GB, S5_STATE), jnp.eye(S5_GB, dtype=dw.dtype))
    return t.transpose(0, 1, 3, 2).reshape(nb * S5_GB, S5_STATE, S5_GROUP)


def _blockdiag_out(c):
    nb = c.shape[0] // S5_GB
    t = c.reshape(nb, S5_GB, S5_GROUP, S5_STATE).transpose(0, 1, 3, 2)
    return jnp.einsum('bgpm,gh->bgphm', t, jnp.eye(S5_GB, dtype=c.dtype)).reshape(nb, S5_LANES, S5_GB * S5_GROUP)


def _blockdiag_out_t(dc):
    nb = dc.shape[0]
    t = jnp.einsum('bgphm,gh->bgpm', dc.reshape(nb, S5_GB, S5_STATE, S5_GB, S5_GROUP), jnp.eye(S5_GB, dtype=dc.dtype))
    return t.transpose(0, 1, 3, 2).reshape(nb * S5_GB, S5_GROUP, S5_STATE)


def _gather_weights(P):
    def halves(x):
        return x if x.shape[0] == 2 else x.reshape(2, x.shape[1] // 2, x.shape[2])

    now = [n for n in BIG if n in FIRST_NEEDED]
    later = [n for n in BIG if n not in FIRST_NEEDED]
    xs = [halves(P[n].astype(BF16)) for n in now] + [halves(_pack([P[n] for n in SMALL_SHARDED], 2 * ROW_MULT)[None])]
    got = gather_two_level(xs, "gather_weights")
    got, xs_later = lax.optimization_barrier((got, [halves(P[n].astype(BF16)) for n in later]))
    got_later = gather_two_level_sequencer(xs_later, "gather_weights_later", 1)
    full_w = {}
    for n, g in list(zip(now, got[:-1])) + list(zip(later, got_later)):
        two_layers, by_rows = P[n].shape[0] == 2, SHARD_AXIS[n] == 1
        if two_layers and by_rows:
            full_w[n] = g.reshape(2, N_CHIPS * g.shape[2], g.shape[3])
        elif two_layers:
            full_w[n] = g.transpose(0, 2, 1, 3).reshape(2, g.shape[2], N_CHIPS * g.shape[3])
        elif by_rows:
            full_w[n] = g.transpose(1, 0, 2, 3).reshape(1, 2 * N_CHIPS * g.shape[2], g.shape[3])
        else:
            full_w[n] = g.transpose(0, 2, 1, 3).reshape(1, 2 * g.shape[2], N_CHIPS * g.shape[3])
    small = got[-1].transpose(1, 0, 2, 3).reshape(N_CHIPS, -1, PACK_W)
    per_chip = [_unpack(small[q], [P[n].shape for n in SMALL_SHARDED]) for q in range(N_CHIPS)]
    for i, n in enumerate(SMALL_SHARDED):
        full_w[n] = jnp.concatenate([per_chip[q][i] for q in range(N_CHIPS)], axis=SHARD_AXIS[n])
    return full_w


def _halves_first(x):
    return x.reshape(x.shape[0], 2, x.shape[1] // 2, x.shape[2]).transpose(1, 0, 2, 3)


def _reduce_batch(items, small, P, results, tag, ids):
    mx, my, mc = _my_place()
    q = 2 * mx + my
    ids = ids or {}
    names = [f"{n}_{lyr}" for n, lyr, _ in items] + (['small'] if small is not None else [])
    xs = [g.reshape(N_CHIPS, 2, g.shape[1] // 2, g.shape[2]) for g in [g for _, _, g in items] + ([small] if small is not None else [])]
    def after(vals, tie):
        return (vals, None) if tie is None else lax.optimization_barrier((vals, tie))

    theirs = pair_swap(xs, "grads_pair_swap_" + tag, True, ids.get('swap'))
    theirs, tie = after(theirs, (yield None))
    pair = [add2(lax.dynamic_index_in_dim(x, mc, 1, False), t, "grads_pair_sum_" + n, F32 if n == 'small' else BF16)
            for x, t, n in zip(xs, theirs, names)]
    got = chip_all_to_all(pair, "grads_chip_all_to_all_" + tag, ids.get('a2a'))
    got, tie = after(got, (yield tie))
    summed = [add_chips(lax.dynamic_index_in_dim(p, q, 0, False), g, "grads_chip_sum_" + n)
              for p, g, n in zip(pair, got, names)]
    other = pair_swap(summed, "grads_pair_join_" + tag, False, ids.get('join'))
    other, tie = after(other, (yield tie))
    joined = [lax.cond(mc == 0, lambda a, b: jnp.concatenate([a, b], axis=0), lambda a, b: jnp.concatenate([b, a], axis=0),
                       s, o) for s, o in zip(summed, other)]
    for (n, lyr, _), g in zip(items, joined):
        if n == 'mix_w_in':
            g = g[:, :W_IN_SHARD]
        view = (P[n].shape[0], g.shape[0], P[n].shape[-1])
        bufs = results.get(n) or [lax.empty(view, F32) for _ in range(4)]
        results[n] = adamw_layer(P[n].reshape(view), g, P['m_' + n].reshape(view), P['v_' + n].reshape(view), lyr, bufs,
                                 f"adamw_{n}_{lyr}")
    if small is not None:
        results['small_quarter'] = joined[-1]
    yield tie


def _update_small(small_quarter, GS, P):
    mx, my, _ = _my_place()
    q = 2 * mx + my
    small_sum = chip_gather([small_quarter], [0], "grads_small_gather")[0]
    g_small = dict(zip(SMALL, _unpack(small_sum, [GS[n].shape for n in SMALL])))
    for n in SMALL_SHARDED:
        s = P[n].shape[SHARD_AXIS[n]]
        g_small[n] = lax.dynamic_slice_in_dim(g_small[n], q * s, s, axis=SHARD_AXIS[n])
    grad, delta, new_m, new_v = {}, {}, {}, {}
    packed = lambda prefix: _pack([P[prefix + n] for n in SMALL])
    d, m_, v_ = adamw(packed(''), _pack([g_small[n] for n in SMALL]), packed('m_'), packed('v_'), "adamw_small")
    shapes = [P[n].shape for n in SMALL]
    grad.update(g_small)
    for out, pk in ((delta, d), (new_m, m_), (new_v, v_)):
        out.update(zip(SMALL, _unpack(pk, shapes)))
    return grad, delta, new_m, new_v


def _row(v):
    return v.reshape(1, -1)


def _xattn_fwd(h, mem, W, lyr, tm):
    g_xa, g_mem = _row(W['norm_xa'][lyr]), _row(W['norm_mem'][lyr])
    g_q, g_k = _row(W['xa_q_norm'][lyr]), _row(W['xa_k_norm'][lyr])
    wq, wk, wv, wo = (W[n][lyr] for n in ('xa_wq', 'xa_wk', 'xa_wv', 'xa_wo'))
    L, D = h.shape
    M = mem.shape[0]
    hx = rms_fwd(h, g_xa, tm, MXU_DTYPE)
    qp = matmul(hx, wq, name="xa_q")
    kv_opds = [full(mem), full(g_mem), full(wk), full(wv), full(g_k)]
    k, v = blocked_fwd(_mem_kv, kv_opds, [((M, D), F32, (M, D), lambda i: (0, 0))] * 2, 1, "xa_mem_kv")
    o = blocked_fwd(_xa_core, [rows(qp, tm), full(k), full(v), full(g_q)], [_out((L, D), MXU_DTYPE, tm)], L // tm,
                    "xa_core")[0]
    out = matmul(o, wo, add=h, name="xa_o")
    return out, (h, hx, qp, k, v, o)


def _xattn_bwd(dout, saved, mem, W, lyr, tm):
    h, hx, qp, k, v, o = saved
    g_xa, g_mem = _row(W['norm_xa'][lyr]), _row(W['norm_mem'][lyr])
    g_q, g_k = _row(W['xa_q_norm'][lyr]), _row(W['xa_k_norm'][lyr])
    wq, wk, wv, wo = (W[n][lyr] for n in ('xa_wq', 'xa_wk', 'xa_wv', 'xa_wo'))
    L = h.shape[0]
    do = matmul(dout, wo, "nt", name="xa_do")
    d_wo = matmul(o, dout, "tn", name="xa_dwo")
    dqp, dk, dv, d_gq = blocked_bwd(_xa_core, [rows(qp, tm, 'blk'), full(k, 'acc'), full(v, 'acc'), full(g_q, 'acc')],
                                    [rows(do, tm)], L // tm, "xa_core_bwd")
    d_wq = matmul(hx, dqp, "tn", name="xa_dwq")
    dhx = matmul(dqp, wq, "nt", name="xa_dhx")
    dh, d_gxa = rms_bwd(h, g_xa, dhx, tm, dout)
    d_gmem, d_wk, d_wv, d_gk = blocked_bwd(
        _mem_kv, [full(mem), full(g_mem, 'acc'), full(wk, 'acc'), full(wv, 'acc'), full(g_k, 'acc')],
        [full(dk), full(dv)], 1, "xa_mem_kv_bwd")
    by_chip = lambda g: g.reshape(N_CHIPS, g.shape[0] // N_CHIPS, g.shape[1])
    grads = {'norm_xa': d_gxa, 'norm_mem': d_gmem, 'xa_q_norm': d_gq, 'xa_k_norm': d_gk,
             'xa_wq': by_chip(d_wq), 'xa_wk': by_chip(d_wk), 'xa_wv': by_chip(d_wv), 'xa_wo': by_chip(d_wo)}
    return dh, grads


def _conv_params(W, lyr):
    cw, cb = W['ffn_conv_w'][lyr], W['ffn_conv_b'][lyr]
    F = cw.shape[1] // 2
    return [cw[0:1, :F], cw[1:2, :F], cw[2:3, :F], cw[0:1, F:], cw[1:2, F:], cw[2:3, F:], _row(cb[:F]), _row(cb[F:])]


def _ffn_fwd(h, W, lyr, tm):
    L, D = h.shape
    w_up, w_down = W['ffn_w_up'][lyr], W['ffn_w_down'][lyr]
    F = w_down.shape[0]
    hf = rms_fwd(h, _row(W['norm_ffn'][lyr]), tm, MXU_DTYPE)
    ug = matmul(hf, w_up[:, :F], name="ffn_up_gate")
    uv = matmul(hf, w_up[:, F:], name="ffn_up_value")
    opds = [cols(ug, 128), cols(uv, 128)] + [cols(p, 128) for p in _conv_params(W, lyr)]
    a = blocked_fwd(_conv_gate, opds, [((L, F), MXU_DTYPE, (L, 128), lambda j: (0, j))], F // 128, "ffn_conv_gate")[0]
    out = matmul(a, w_down, add=h, name="ffn_down")
    return out, (h, hf, ug, uv, a)


def _ffn_bwd(dout, saved, W, lyr, tm):
    h, hf, ug, uv, a = saved
    w_up, w_down = W['ffn_w_up'][lyr], W['ffn_w_down'][lyr]
    F = w_down.shape[0]
    da = matmul(dout, w_down, "nt", name="ffn_da")
    d_wdown = matmul(a, dout, "tn", name="ffn_dwdown")
    opds = [cols(ug, 128, 'blk'), cols(uv, 128, 'blk')] + [cols(p, 128, 'blk') for p in _conv_params(W, lyr)]
    gs = blocked_bwd(_conv_gate, opds, [cols(da, 128)], F // 128, "ffn_conv_gate_bwd")
    dug, duv = gs[0], gs[1]
    d_cw = jnp.concatenate([jnp.concatenate(gs[2:5], axis=0), jnp.concatenate(gs[5:8], axis=0)], axis=1)
    d_cb = jnp.concatenate([gs[8], gs[9]], axis=1)[0]
    half = N_CHIPS // 2
    d_wup = lax.empty((N_CHIPS, hf.shape[1], w_up.shape[1] // N_CHIPS), F32)
    d_wup = matmul(hf, dug, "tn", name="ffn_dwup_gate", into=(d_wup, 0), col_blocks=half)
    d_wup = matmul(hf, duv, "tn", name="ffn_dwup_value", into=(d_wup, half), col_blocks=half)
    dhf = matmul(dug, w_up[:, :F], "nt", name="ffn_dhf_gate")
    dhf = matmul(duv, w_up[:, F:], "nt", add=dhf, name="ffn_dhf_value")
    dh, d_g = rms_bwd(h, _row(W['norm_ffn'][lyr]), dhf, tm, dout)
    d_wdown = d_wdown.reshape(N_CHIPS, F // N_CHIPS, d_wdown.shape[1])
    return dh, {'norm_ffn': d_g, 'ffn_w_up': d_wup, 'ffn_conv_w': d_cw, 'ffn_conv_b': d_cb, 'ffn_w_down': d_wdown}


def _mla_params(W):
    w_uq = W['mla_w_uq'][0].reshape(MLA_Q_RANK, MLA_HEADS, MLA_QK)
    w_uq = jnp.concatenate([w_uq[..., :MLA_NOPE], _rope_pad(w_uq[..., MLA_NOPE:])], axis=-1)
    w_ukv = W['mla_w_ukv'][0].reshape(MLA_KV_RANK, MLA_HEADS, MLA_NOPE + MLA_V)
    w_ukv = jnp.concatenate([w_ukv[..., :MLA_NOPE].reshape(MLA_KV_RANK, -1), w_ukv[..., MLA_NOPE:].reshape(MLA_KV_RANK, -1)],
                            axis=1)
    return [_row(W['mla_q_a_norm'][0]), w_uq.reshape(MLA_Q_RANK, MLA_HEADS * MLA_DK), _row(W['mla_kv_a_norm'][0]), w_ukv,
            _row(W['mla_qn_nope'][0]), _row(_rope_pad(W['mla_qn_rope'][0])), _row(W['mla_kn_nope'][0]),
            _row(_rope_pad(W['mla_kn_rope'][0]))]


def _w_in_padded(W):
    w = W['mix_w_in'][0]
    return jnp.concatenate([w[:, :IN_WIDTH - MLA_ROPE], _rope_pad(w[:, IN_WIDTH - MLA_ROPE:])], axis=1)


def _mixer0_fwd(h, W, cos_p, sin_p, tm):
    L = h.shape[0]
    t = min(ATTN_ROWS, L // ATTN_WIDE)
    hn = rms_fwd(h, _row(W['norm_mix'][0]), tm, MXU_DTYPE)
    proj = matmul(hn, _w_in_padded(W), name="mix_in")
    logits = W['hg_lb_logits']
    lb = blocked_fwd(_lb_first, [full(logits)], [((1, HG_WIDTH), F32, (1, HG_WIDTH), lambda i: (0, 0))], 1, "hg_lb")[0]
    gain = _row(W['hg_out_norm'][0])
    o_hg, states = hgrn2_fwd(proj, lb, gain)
    mp = _mla_params(W)
    q, k, v = mla_prep_fwd(proj, cos_p, sin_p, mp, tm)
    scale = MLA_QK ** -0.5
    o_mla, lse = attn_fwd(q, k, v, scale, t)
    w_out = W['mix_w_out'][0]
    out = matmul(o_hg, w_out[:HG_WIDTH], add=h, name="mix_out_hg")
    out = matmul(o_mla, w_out[HG_WIDTH:], add=out, name="mix_out_mla")
    return out, (h, hn, proj, lb, o_hg, states, q, k, v, o_mla, lse)


def _mixer0_bwd(dout, saved, W, cos_p, sin_p, tm):
    h, hn, proj, lb, o_hg, states, q, k, v, o_mla, lse = saved
    L = h.shape[0]
    t = min(ATTN_ROWS, L // ATTN_WIDE)
    scale = MLA_QK ** -0.5
    w_out = W['mix_w_out'][0]
    gain = _row(W['hg_out_norm'][0])
    do_hg = matmul(dout, w_out[:HG_WIDTH], "nt", name="mix_do_hg")
    do_mla = matmul(dout, w_out[HG_WIDTH:], "nt", name="mix_do_mla")
    d_wout = jnp.concatenate([matmul(o_hg, dout, "tn", name="mix_dwout_hg"), matmul(o_mla, dout, "tn", name="mix_dwout_mla")],
                             axis=0)
    dq = attn_bwd_dq(q, k, v, o_mla, lse, do_mla, scale, t)
    dk, dv = attn_bwd_dkv(q, k, v, o_mla, lse, do_mla, scale, t)
    mp = _mla_params(W)
    d_mla, d_qa, d_wuq, d_kva, d_wukv, d_qnn, d_qnr, d_knn, d_knr = mla_prep_bwd(proj, cos_p, sin_p, mp, dq, dk, dv, tm)
    d_hg, d_lb, d_gain = hgrn2_bwd(proj, lb, gain, states, do_hg)
    w_in, n_hg = _w_in_padded(W), 4 * HG_WIDTH
    d_win = jnp.concatenate([matmul(hn, d_hg, "tn", name="mix_dwin_hg"), matmul(hn, d_mla, "tn", name="mix_dwin_mla")], axis=1)
    dhn = matmul(d_hg, w_in[:, :n_hg], "nt", name="mix_dhn_hg")
    dhn = matmul(d_mla, w_in[:, n_hg:], "nt", add=dhn, name="mix_dhn_mla")
    dh, d_g = rms_bwd(h, _row(W['norm_mix'][0]), dhn, tm, dout)
    logits = W['hg_lb_logits']
    d_logits = blocked_bwd(_lb_first, [full(logits, 'acc')], [full(d_lb)], 1, "hg_lb_bwd")[0]
    d_wuq = d_wuq.reshape(MLA_Q_RANK, MLA_HEADS, MLA_DK)
    d_wuq = jnp.concatenate([d_wuq[..., :MLA_NOPE], _rope_unpad(d_wuq[..., MLA_NOPE:])], axis=-1)
    hw = MLA_HEADS * MLA_NOPE
    d_wukv = jnp.concatenate([d_wukv[:, :hw].reshape(MLA_KV_RANK, MLA_HEADS, MLA_NOPE),
                              d_wukv[:, hw:].reshape(MLA_KV_RANK, MLA_HEADS, MLA_V)], axis=-1)
    d_win = jnp.concatenate([d_win[:, :IN_WIDTH - MLA_ROPE], _rope_unpad(d_win[:, IN_WIDTH - MLA_ROPE:])], axis=1)
    d_win = d_win.reshape(d_win.shape[0], N_CHIPS, W_IN_SHARD).transpose(1, 0, 2)
    d_win = jnp.pad(d_win, ((0, 0), (0, 0), (0, W_IN_SHARD_PAD - W_IN_SHARD)))
    d_wout = d_wout.reshape(N_CHIPS, d_wout.shape[0] // N_CHIPS, d_wout.shape[1])
    grads = {'norm_mix': d_g, 'hg_lb_logits': d_logits, 'mix_w_in': d_win, 'hg_out_norm': d_gain,
             'mla_q_a_norm': d_qa, 'mla_w_uq': d_wuq.reshape(1, MLA_Q_RANK, -1), 'mla_kv_a_norm': d_kva,
             'mla_w_ukv': d_wukv.reshape(1, MLA_KV_RANK, -1), 'mla_qn_nope': d_qnn, 'mla_qn_rope': _rope_unpad(d_qnr),
             'mla_kn_nope': d_knn, 'mla_kn_rope': _rope_unpad(d_knr), 'mix_w_out': d_wout}
    return dh,grads


def _s5_inputs(W):
    G = W['s5_lam_re'].shape[1]
    return [W['s5_lam_re'][0], W['s5_lam_im'][0], W['s5_log_dt'][0].reshape(G, 1),
            W['s5_b_re'][0].reshape(G, -1), W['s5_b_im'][0].reshape(G, -1)]


def _mixer1_fwd(h, W, tm):
    L, D = h.shape
    u = rms_fwd(h, _row(W['norm_mix'][1]), tm, F32)
    di = _s5_inputs(W)
    G = di[0].shape[0]
    sq, wide = ((G, S5_STATE), F32, (G, S5_STATE), lambda i: (0, 0)), ((G, S5_STATE * S5_GROUP), F32, (G, S5_STATE * S5_GROUP), lambda i: (0, 0))
    ar, ai, bbr, bbi = blocked_fwd(_s5_discretize, [full(a) for a in di], [sq, sq, wide, wide], 1, "s5_discretize")
    nb = G // S5_GB
    core = (_blockdiag_in(bbr.reshape(G, S5_STATE, S5_GROUP)), _blockdiag_in(bbi.reshape(G, S5_STATE, S5_GROUP)),
            ar.reshape(nb, 1, S5_LANES), ai.reshape(nb, 1, S5_LANES),
            _blockdiag_out(W['s5_c_re'][0]), _blockdiag_out(W['s5_c_im'][0]))
    y = s5_fwd(u, *core)
    d = W['s5_d']
    y2 = blocked_fwd(_s5_post, [rows(y, tm), rows(u, tm), full(d)], [_out((L, D), MXU_DTYPE, tm)], L // tm, "s5_post")[0]
    w_ab = jnp.concatenate([W['s5_w_glu_a'][0], W['s5_w_glu_b'][0]], axis=1)
    ab = matmul(y2, w_ab, name="s5_glu_in")
    mix = blocked_fwd(_glu, [rows(ab, tm, col=0, width=D), rows(ab, tm, col=1, width=D)], [_out((L, D), F32, tm)], L // tm,
                      "s5_glu")[0]
    return h + mix, (h, u, core, y, y2, ab)


def _mixer1_bwd(dout, saved, W, tm):
    h, u, core, y, y2, ab = saved
    L, D = h.shape
    da, db = blocked_bwd(_glu, [rows(ab, tm, 'blk', col=0, width=D), rows(ab, tm, 'blk', col=1, width=D)], [rows(dout, tm)],
                         L // tm, "s5_glu_bwd")
    w_a, w_b = W['s5_w_glu_a'][0], W['s5_w_glu_b'][0]
    dy2 = matmul(da, w_a, "nt", name="s5_dy2_a")
    dy2 = matmul(db, w_b, "nt", add=dy2, name="s5_dy2_b")
    d_wa = matmul(y2, da, "tn", name="s5_dwa")
    d_wb = matmul(y2, db, "tn", name="s5_dwb")
    d = W['s5_d']
    dy, du_skip, d_d = blocked_bwd(_s5_post, [rows(y, tm, 'blk'), rows(u, tm, 'blk'), full(d, 'acc')], [rows(dy2, tm)], L // tm,
                                   "s5_post_bwd")
    du, dwr, dwi, dar, dai, dcr, dci = s5_bwd(u, *core, dy, min(256, L))
    di = _s5_inputs(W)
    G = di[0].shape[0]
    cts = [dar.reshape(G, S5_STATE), dai.reshape(G, S5_STATE), _blockdiag_in_t(dwr).reshape(G, -1), _blockdiag_in_t(dwi).reshape(G, -1)]
    d_lr, d_li, d_ldt, d_br, d_bi = blocked_bwd(_s5_discretize, [full(a, 'acc') for a in di], [full(c) for c in cts], 1,
                                                "s5_discretize_bwd")
    dh, d_g = rms_bwd(h, _row(W['norm_mix'][1]), du + du_skip, tm, dout)
    bshape = W['s5_b_re'].shape
    grads = {'norm_mix': d_g, 's5_lam_re': d_lr[None], 's5_lam_im': d_li[None], 's5_log_dt': d_ldt.reshape(1, G),
             's5_b_re': d_br.reshape(bshape), 's5_b_im': d_bi.reshape(bshape), 's5_c_re': _blockdiag_out_t(dcr)[None],
             's5_c_im': _blockdiag_out_t(dci)[None], 's5_d': d_d, 's5_w_glu_a': d_wa.reshape(N_CHIPS, -1, D), 's5_w_glu_b': d_wb.reshape(N_CHIPS, -1, D)}
    return dh,grads


def kernel(x, mem, positions, norm_mix, norm_xa, norm_mem, norm_ffn, xa_wq, xa_wk, xa_wv, xa_wo, xa_q_norm, xa_k_norm, ffn_w_up, ffn_conv_w, ffn_conv_b, ffn_w_down, hg_lb_logits, mix_w_in, hg_out_norm, mla_q_a_norm, mla_w_uq, mla_kv_a_norm, mla_w_ukv, mla_qn_nope, mla_qn_rope, mla_kn_nope, mla_kn_rope, mix_w_out, s5_lam_re, s5_lam_im, s5_log_dt, s5_b_re, s5_b_im, s5_c_re, s5_c_im, s5_d, s5_w_glu_a, s5_w_glu_b, loss_target, m_norm_mix, m_norm_xa, m_norm_mem, m_norm_ffn, m_xa_wq, m_xa_wk, m_xa_wv, m_xa_wo, m_xa_q_norm, m_xa_k_norm, m_ffn_w_up, m_ffn_conv_w, m_ffn_conv_b, m_ffn_w_down, m_hg_lb_logits, m_mix_w_in, m_hg_out_norm, m_mla_q_a_norm, m_mla_w_uq, m_mla_kv_a_norm, m_mla_w_ukv, m_mla_qn_nope, m_mla_qn_rope, m_mla_kn_nope, m_mla_kn_rope, m_mix_w_out, m_s5_lam_re, m_s5_lam_im, m_s5_log_dt, m_s5_b_re, m_s5_b_im, m_s5_c_re, m_s5_c_im, m_s5_d, m_s5_w_glu_a, m_s5_w_glu_b, v_norm_mix, v_norm_xa, v_norm_mem, v_norm_ffn, v_xa_wq, v_xa_wk, v_xa_wv, v_xa_wo, v_xa_q_norm, v_xa_k_norm, v_ffn_w_up, v_ffn_conv_w, v_ffn_conv_b, v_ffn_w_down, v_hg_lb_logits, v_mix_w_in, v_hg_out_norm, v_mla_q_a_norm, v_mla_w_uq, v_mla_kv_a_norm, v_mla_w_ukv, v_mla_qn_nope, v_mla_qn_rope, v_mla_kn_nope, v_mla_kn_rope, v_mix_w_out, v_s5_lam_re, v_s5_lam_im, v_s5_log_dt, v_s5_b_re, v_s5_b_im, v_s5_c_re, v_s5_c_im, v_s5_d, v_s5_w_glu_a, v_s5_w_glu_b):
    P = dict(locals())
    assert sorted(P) == sorted(INPUTS) and norm_mix.shape[0] == 2 and mix_w_in.shape[0] == 1
    x, mem, target = P['x'][0], P['mem'][0], P['loss_target'][0]
    L, D = x.shape
    tm = min(256, L)

    W = {n: P[n] for n in REPLICATED}
    W.update(_gather_weights(P))

    inv_freq = 1.0 / (ROPE_BASE ** (jnp.arange(0, MLA_ROPE, 2, dtype=F32) / MLA_ROPE))
    ang = P['positions'][0].astype(F32)[:, None] * inv_freq
    cos, sin, z = jnp.cos(ang), jnp.sin(ang), jnp.zeros_like(ang)
    cos_p = jnp.concatenate([cos, z, cos, z], axis=1)
    sin_p = jnp.concatenate([-sin, z, sin, z], axis=1)

    h, s_mix0 = _mixer0_fwd(x, W, cos_p, sin_p, tm)
    h, s_xa0 = _xattn_fwd(h, mem, W, 0, tm)
    h, s_ffn0 = _ffn_fwd(h, W, 0, tm)
    h, s_mix1 = _mixer1_fwd(h, W, tm)
    h, s_xa1 = _xattn_fwd(h, mem, W, 1, tm)
    h, s_ffn1 = _ffn_fwd(h, W, 1, tm)
    n = L // tm
    dh, parts = blocked_fwd(_loss_fn, [rows(h, tm), rows(target, tm)],
                            [_out((L, D), F32, tm), ((n * 8, 128), F32, (8, 128), lambda i: (i, 0))], n, "loss")
    loss = lax.psum(jnp.sum(parts), ("x", "y", "c"))

    layered = {}

    def collect(g, lyr):
        for k_, v_ in g.items():
            layered.setdefault(k_, {})[lyr] = v_

    results = {}

    def big_items(lyr):
        return [(n_, 0 if P[n_].shape[0] == 1 else lyr, layered[n_][lyr]) for n_ in BIG if lyr in layered.get(n_, {})]

    dh, g = _ffn_bwd(dh, s_ffn1, W, 1, tm)
    collect(g, 1)
    dh, g = _xattn_bwd(dh, s_xa1, mem, W, 1, tm)
    collect(g, 1)
    dh, g = _mixer1_bwd(dh, s_mix1, W, tm)
    collect(g, 1)
    late = _reduce_batch(big_items(1), None, P, results, "late_layer", {'swap': 2, 'a2a': 3, 'join': 4})
    next(late)
    dh, g = _ffn_bwd(dh, s_ffn0, W, 0, tm)
    collect(g, 0)
    dh = late.send(dh)
    dh, g = _xattn_bwd(dh, s_xa0, mem, W, 0, tm)
    collect(g, 0)
    dx, g = _mixer0_bwd(dh, s_mix0, W, cos_p, sin_p, tm)
    collect(g, 0)

    GS = {}
    for name in SMALL:
        by_layer = [layered[name][lyr] for lyr in sorted(layered[name])]
        full_shape = W[name].shape
        GS[name] = (by_layer[0].reshape(full_shape) if len(by_layer) == 1
                    else jnp.stack([g_.reshape(full_shape[1:]) for g_ in by_layer]))
    small = _pack([GS[n_] for n_ in SMALL], 2 * N_CHIPS * ROW_MULT).reshape(N_CHIPS, -1, PACK_W)
    first = _reduce_batch(big_items(0), small, P, results, "first_layer", {'swap': 5, 'a2a': 6, 'join': 7})
    next(first)
    dx = late.send(dx)
    first.send(None)
    late.send(None)
    late_name = big_items(1)[-1][0]
    results[late_name] = list(first.send(list(results[late_name])))
    first.send(None)
    outs = list(_update_small(results['small_quarter'], GS, P))
    for k_ in range(4):
        outs[k_].update({n_: results[n_][k_].reshape(P[n_].shape) for n_ in BIG})
    return (loss, dx[None], *[d[n_] for d in outs for n_ in WEIGHTS])
```

```python
import functools
import math

import jax
import jax.numpy as jnp
import numpy as np
from jax import lax
from jax.experimental import pallas as pl
from jax.experimental.pallas import tpu as pltpu
from jax.experimental.pallas import tpu_sc as plsc

F32 = jnp.float32
BF16 = jnp.bfloat16
MXU_DTYPE = BF16
HI = lax.Precision.HIGHEST
V7X_VMEM_LIMIT_BYTES = 56 * 1024 * 1024
EPS = 1e-6
MESH = pl.DeviceIdType.MESH

HG_HEADS, HG_DIM = 4, 128
HG_WIDTH = HG_HEADS * HG_DIM
HG_SUB = 32
HG_BLOCK = 64
MLA_HEADS, MLA_Q_RANK, MLA_KV_RANK = 4, 256, 128
MLA_NOPE, MLA_ROPE, MLA_V = 128, 64, 128
MLA_QK = MLA_NOPE + MLA_ROPE
MLA_DK = 256
ROPE_BASE = 10000.0
IN_WIDTH = 4 * HG_WIDTH + MLA_Q_RANK + MLA_KV_RANK + MLA_ROPE
IN_PAD = 4 * HG_WIDTH + MLA_Q_RANK + MLA_KV_RANK + 128
S5_GROUP, S5_STATE = 16, 64
S5_GB = 8
DT_MIN, DT_MAX = 1e-3, 1e-1
XA_HEADS = 4
CONV_W = 3
ADAM_LR, ADAM_B1, ADAM_B2, ADAM_EPS, ADAM_WD, ADAM_STEP = 0.001, 0.9, 0.999, 1e-08, 0.01, 10


def _cparams(sem):
    return pltpu.CompilerParams(dimension_semantics=sem, vmem_limit_bytes=V7X_VMEM_LIMIT_BYTES)


class Opd:
    def __init__(self, arr, block, imap, grad=None, gshape=None, gimap=None):
        self.arr, self.block, self.imap, self.grad = arr, block, imap, grad
        self.gshape = arr.shape if gshape is None else gshape
        self.gimap = imap if gimap is None else gimap

    def spec(self):
        return pl.BlockSpec(self.block, self.imap)

    def gspec(self):
        return pl.BlockSpec(self.block, self.gimap)


def rows(arr, tm, grad=None, col=0, width=None):
    width = arr.shape[1] if width is None else width
    return Opd(arr, (tm, width), lambda i, c=col: (i, c), grad, (arr.shape[0], width), lambda i: (i, 0))


def cols(arr, tn, grad=None):
    return Opd(arr, (arr.shape[0], tn), lambda j: (0, j), grad)


def full(arr, grad=None):
    return Opd(arr, arr.shape, lambda i: (0, 0), grad)


def _load(ref):
    v = ref[...]
    return v.astype(F32) if jnp.issubdtype(v.dtype, jnp.floating) else v


def blocked_fwd(f, opds, outs, n, name):
    n_in = len(opds)

    def body(*refs):
        ys = f(*[_load(r) for r in refs[:n_in]])
        for r, y in zip(refs[n_in:], ys):
            r[...] = y.astype(r.dtype)

    res = pl.pallas_call(
        body, name=name, grid=(n,),
        in_specs=[o.spec() for o in opds],
        out_specs=[pl.BlockSpec(b, m) for (_, _, b, m) in outs],
        out_shape=[jax.ShapeDtypeStruct(s, d) for (s, d, _, _) in outs],
        compiler_params=_cparams(("parallel",)),
    )(*[o.arr for o in opds])
    return res


def blocked_bwd(f, opds, dys, n, name, plus=None):
    n_in, n_dy = len(opds), len(dys)
    diff = [i for i, o in enumerate(opds) if o.grad]
    extra = [] if plus is None else [plus]

    def body(*refs):
        vals = [_load(r) for r in refs[:n_in]]

        def fd(*dv):
            allv = list(vals)
            for i, v in zip(diff, dv):
                allv[i] = v
            return tuple(f(*allv))

        ys, vjp = jax.vjp(fd, *[vals[i] for i in diff])
        cts = tuple(_load(r).astype(y.dtype) for r, y in zip(refs[n_in:n_in + n_dy], ys))
        gs = list(vjp(cts))
        if extra:
            gs[0] = gs[0] + _load(refs[n_in + n_dy])
        for r, g, i in zip(refs[n_in + n_dy + len(extra):], gs, diff):
            if opds[i].grad == 'acc':
                @pl.when(pl.program_id(0) == 0)
                def _(r=r):
                    r[...] = jnp.zeros(r.shape, r.dtype)
                r[...] += g.astype(r.dtype)
            else:
                r[...] = g.astype(r.dtype)

    any_acc = any(opds[i].grad == 'acc' for i in diff)
    res = pl.pallas_call(
        body, name=name, grid=(n,),
        in_specs=[o.spec() for o in opds + dys + extra],
        out_specs=[opds[i].gspec() for i in diff],
        out_shape=[jax.ShapeDtypeStruct(opds[i].gshape, F32) for i in diff],
        compiler_params=_cparams(("arbitrary" if any_acc else "parallel",)),
    )(*[o.arr for o in opds + dys + extra])
    return res


def _tile(dim, want):
    for t in range(want - want % 16, 0, -16):
        if dim % t == 0:
            return t
    assert dim <= want, (dim, want)
    return dim


MATMUL_VMEM_BUDGET = 40 * 1024 * 1024
MATMUL_ROWS = 512


def _widest(N, fits):
    for t in range(N - N % 128, 0, -128):
        if N % t == 0 and fits(t):
            return t
    return N


def matmul(a, b, mode="nn", out_dtype=F32, add=None, name="matmul", into=None, col_blocks=None):
    sa, sb, so = a.dtype.itemsize, b.dtype.itemsize, jnp.dtype(out_dtype).itemsize
    has_add = add is not None
    if mode == "tn":
        (K, M), (K2, N) = a.shape, b.shape
        assert K == K2 and not has_add and out_dtype == F32, (a.shape, b.shape)
        tk = _tile(K, MATMUL_ROWS)
        tn = _widest(N, lambda t: 2 * (tk * M * sa + tk * t * sb + M * t * 4) <= MATMUL_VMEM_BUDGET)
        extra, alias = [], {}
        if into is not None:
            buf, lead = into[0], tuple(into[1:])
            if col_blocks is not None:
                assert N % col_blocks == 0 and (N // col_blocks) % 128 == 0 and tn >= N // col_blocks, (N, col_blocks, tn)
                tn = N // col_blocks
                assert buf.shape[len(lead):] == (M, tn), (buf.shape, lead, M, tn)
                out_spec = pl.BlockSpec((None,) * len(lead) + (M, tn), lambda j, k: lead[:-1] + (lead[-1] + j, 0, 0))
            else:
                assert buf.shape[len(lead):] == (M, N), (buf.shape, lead, M, N)
                out_spec = pl.BlockSpec((None,) * len(lead) + (M, tn), lambda j, k: lead + (0, j))
            out_shape = jax.ShapeDtypeStruct(buf.shape, F32)
            extra, alias = [buf], {2: 0}
        else:
            assert col_blocks is None
            out_spec = pl.BlockSpec((M, tn), lambda j, k: (0, j))
            out_shape = jax.ShapeDtypeStruct((M, N), F32)

        def body(a_ref, b_ref, *rest):
            o_ref = rest[-1]
            r = lax.dot_general(a_ref[...].astype(MXU_DTYPE), b_ref[...].astype(MXU_DTYPE), ((_TN), ((), ())),
                                preferred_element_type=F32)

            @pl.when(pl.program_id(1) == 0)
            def _():
                o_ref[...] = r

            @pl.when(pl.program_id(1) > 0)
            def _():
                o_ref[...] += r

        return pl.pallas_call(
            body, name=name, grid=(N // tn, K // tk),
            in_specs=[pl.BlockSpec((tk, M), lambda j, k: (k, 0)), pl.BlockSpec((tk, tn), lambda j, k: (k, j))]
            + [pl.BlockSpec(memory_space=pl.ANY)] * len(extra),
            out_specs=out_spec, out_shape=out_shape, input_output_aliases=alias,
            compiler_params=_cparams(("parallel", "arbitrary")),
        )(a, b, *extra)

    (M, K) = a.shape
    N = b.shape[1] if mode == "nn" else b.shape[0]
    assert K == (b.shape[0] if mode == "nn" else b.shape[1]), (a.shape, b.shape, mode)
    tm = _tile(M, MATMUL_ROWS)
    tn = _widest(N, lambda t: 2 * (tm * K * sa + K * t * sb + tm * t * (so + 4 * has_add)) <= MATMUL_VMEM_BUDGET)
    dims = ((_NN if mode == "nn" else _NT), ((), ()))

    def body(*refs):
        r = lax.dot_general(refs[0][...].astype(MXU_DTYPE), refs[1][...].astype(MXU_DTYPE), dims, preferred_element_type=F32)
        if has_add:
            r = r + refs[2][...].astype(F32)
        refs[-1][...] = r.astype(refs[-1].dtype)

    b_spec = pl.BlockSpec((K, tn), lambda j, i: (0, j)) if mode == "nn" else pl.BlockSpec((tn, K), lambda j, i: (j, 0))
    in_specs = [pl.BlockSpec((tm, K), lambda j, i: (i, 0)), b_spec]
    args = [a, b]
    if has_add:
        in_specs.append(pl.BlockSpec((tm, tn), lambda j, i: (i, j)))
        args.append(add)
    return pl.pallas_call(
        body, name=name, grid=(N // tn, M // tm),
        in_specs=in_specs,
        out_specs=pl.BlockSpec((tm, tn), lambda j, i: (i, j)),
        out_shape=jax.ShapeDtypeStruct((M, N), out_dtype),
        compiler_params=_cparams(("parallel", "parallel")),
    )(*args)


def _dot(a, b, dims, precision=None):
    if precision is None:
        a, b = a.astype(MXU_DTYPE), b.astype(MXU_DTYPE)
    return lax.dot_general(a, b, (dims, ((), ())), precision=precision, preferred_element_type=F32)


_NN = ((1,), (0,))
_NT = ((1,), (1,))
_TN = ((0,), (0,))


def _rms(x, gain):
    return x * lax.rsqrt(jnp.mean(x * x, axis=-1, keepdims=True) + EPS) * gain


def _hg_block(st_t, q, fl, iv, g, lb, gain):
    row = lax.broadcasted_iota(jnp.int32, (HG_SUB, HG_SUB), 0)
    col = lax.broadcasted_iota(jnp.int32, (HG_SUB, HG_SUB), 1)
    tri = (row >= col).astype(F32)
    outs, states = [], []
    for h in range(HG_HEADS):
        sl = slice(h * HG_DIM, (h + 1) * HG_DIM)
        st = st_t[h * HG_DIM:(h + 1) * HG_DIM, :]
        lbh = lb[:, sl]
        fg = lbh + (1.0 - lbh) * jax.nn.sigmoid(fl[:, sl])
        lf, kk, qf, v = jnp.log(fg), 1.0 - fg, jax.nn.silu(q[:, sl]), iv[:, sl]
        parts = []
        for s in range(q.shape[0] // HG_SUB):
            r = slice(s * HG_SUB, (s + 1) * HG_SUB)
            b = _dot(tri, lf[r], _NN, HI)
            b_mid = jnp.sum(lf[r][:HG_SUB // 2], axis=0, keepdims=True)
            b_end = jnp.sum(lf[r], axis=0, keepdims=True)
            sc = _dot(qf[r] * jnp.exp(b - b_mid), kk[r] * jnp.exp(b_mid - b), _NT) * tri
            parts.append(_dot(sc, v[r], _NN) + _dot(qf[r] * jnp.exp(b), st, _NT))
            st = st * jnp.exp(b_end) + _dot(v[r], kk[r] * jnp.exp(b_end - b), _TN)
        o = jnp.concatenate(parts, axis=0)
        outs.append(_rms(o, gain[:, sl]) * jax.nn.silu(g[:, sl]))
        states.append(st)
    return jnp.concatenate(states, axis=0), jnp.concatenate(outs, axis=1)


def _hg_specs(proj, nb):
    return [pl.BlockSpec((HG_BLOCK, HG_WIDTH), lambda i, c=c, f=nb: (f(i), c)) for c in range(4)]


def hgrn2_fwd(proj, lb, gain):
    L = proj.shape[0]
    n = L // HG_BLOCK

    def body(q, fl, iv, g, lb_r, gain_r, o_ref, st_ref, st):
        @pl.when(pl.program_id(0) == 0)
        def _():
            st[...] = jnp.zeros(st.shape, F32)

        st_ref[0] = st[...]
        new, o = _hg_block(st[...], q[...], fl[...], iv[...], g[...], lb_r[...], gain_r[...])
        st[...] = new
        o_ref[...] = o.astype(o_ref.dtype)

    pspec = pl.BlockSpec((1, HG_WIDTH), lambda i: (0, 0))
    return pl.pallas_call(
        body, name="hgrn2_fwd", grid=(n,),
        in_specs=_hg_specs(proj, lambda i: i) + [pspec, pspec],
        out_specs=[pl.BlockSpec((HG_BLOCK, HG_WIDTH), lambda i: (i, 0)),
                   pl.BlockSpec((1, HG_WIDTH, HG_DIM), lambda i: (i, 0, 0))],
        out_shape=[jax.ShapeDtypeStruct((L, HG_WIDTH), MXU_DTYPE),
                   jax.ShapeDtypeStruct((n, HG_WIDTH, HG_DIM), F32)],
        scratch_shapes=[pltpu.VMEM((HG_WIDTH, HG_DIM), F32)],
        compiler_params=_cparams(("arbitrary",)),
    )(proj, proj, proj, proj, lb, gain)


def hgrn2_bwd(proj, lb, gain, states, do):
    L = proj.shape[0]
    n = L // HG_BLOCK

    def body(q, fl, iv, g, lb_r, gain_r, st_r, do_r, dproj, dlb, dgain, dst):
        @pl.when(pl.program_id(0) == 0)
        def _():
            dst[...] = jnp.zeros(dst.shape, F32)
            dlb[...] = jnp.zeros(dlb.shape, F32)
            dgain[...] = jnp.zeros(dgain.shape, F32)

        _, vjp = jax.vjp(_hg_block, st_r[0], q[...], fl[...], iv[...], g[...], lb_r[...], gain_r[...])
        d_st, dq, dfl, div, dg, d_lb, d_gain = vjp((dst[...], do_r[...].astype(F32)))
        dst[...] = d_st
        dproj[:, 0 * HG_WIDTH:1 * HG_WIDTH] = dq
        dproj[:, 1 * HG_WIDTH:2 * HG_WIDTH] = dfl
        dproj[:, 2 * HG_WIDTH:3 * HG_WIDTH] = div
        dproj[:, 3 * HG_WIDTH:4 * HG_WIDTH] = dg
        dlb[...] += d_lb
        dgain[...] += d_gain

    rev = lambda i: n - 1 - i
    pspec = pl.BlockSpec((1, HG_WIDTH), lambda i: (0, 0))
    return pl.pallas_call(
        body, name="hgrn2_bwd", grid=(n,),
        in_specs=_hg_specs(proj, rev) + [pspec, pspec,
                                         pl.BlockSpec((1, HG_WIDTH, HG_DIM), lambda i: (rev(i), 0, 0)),
                                         pl.BlockSpec((HG_BLOCK, HG_WIDTH), lambda i: (rev(i), 0))],
        out_specs=[pl.BlockSpec((HG_BLOCK, 4 * HG_WIDTH), lambda i: (rev(i), 0)), pspec, pspec],
        out_shape=[jax.ShapeDtypeStruct((L, 4 * HG_WIDTH), F32),
                   jax.ShapeDtypeStruct((1, HG_WIDTH), F32), jax.ShapeDtypeStruct((1, HG_WIDTH), F32)],
        scratch_shapes=[pltpu.VMEM((HG_WIDTH, HG_DIM), F32)],
        compiler_params=_cparams(("arbitrary",)),
    )(proj, proj, proj, proj, lb, gain, states, do)


def _rope_rms(x, gain_p, cos_p, sin_p):
    n = x * lax.rsqrt(jnp.sum(x * x, axis=-1, keepdims=True) * (1.0 / MLA_ROPE) + EPS) * gain_p
    r = lax.broadcasted_iota(jnp.int32, (128, 128), 0)
    c = lax.broadcasted_iota(jnp.int32, (128, 128), 1)
    swap = (r == (c + 64) % 128).astype(F32)
    return n * cos_p + _dot(n, swap, _NN, HI) * sin_p


MLA_IN = MLA_Q_RANK + MLA_KV_RANK + 128


def _mla_prep(x, cos_p, sin_p, q_a, w_uq, kv_a, w_ukv, qn_nope, qn_rope, kn_nope, kn_rope):
    c_q, c_kv, kpe = x[:, :MLA_Q_RANK], x[:, MLA_Q_RANK:MLA_Q_RANK + MLA_KV_RANK], x[:, MLA_Q_RANK + MLA_KV_RANK:]
    q = _dot(_rms(c_q, q_a), w_uq, _NN)
    kv = _dot(_rms(c_kv, kv_a), w_ukv, _NN)
    k_pe = _rope_rms(kpe, kn_rope, cos_p, sin_p)
    qs, ks = [], []
    for h in range(MLA_HEADS):
        qs.append(_rms(q[:, h * MLA_DK:h * MLA_DK + MLA_NOPE], qn_nope))
        qs.append(_rope_rms(q[:, h * MLA_DK + MLA_NOPE:(h + 1) * MLA_DK], qn_rope, cos_p, sin_p))
        ks.append(_rms(kv[:, h * MLA_NOPE:(h + 1) * MLA_NOPE], kn_nope))
        ks.append(k_pe)
    return jnp.concatenate(qs, axis=1), jnp.concatenate(ks, axis=1), kv[:, MLA_HEADS * MLA_NOPE:]


def _mla_prep_opds(proj, cos_p, sin_p, params, tm, grads):
    g = (lambda k: k) if grads else (lambda k: None)
    assert (4 * HG_WIDTH) % MLA_IN == 0
    return ([rows(proj, tm, g('blk'), col=4 * HG_WIDTH // MLA_IN, width=MLA_IN), rows(cos_p, tm), rows(sin_p, tm)]
            + [full(p, g('acc')) for p in params])


def mla_prep_fwd(proj, cos_p, sin_p, params, tm):
    L = proj.shape[0]
    W = MLA_HEADS * MLA_DK
    rb = lambda w: (tm, w)
    outs = [((L, W), MXU_DTYPE, rb(W), lambda i: (i, 0)), ((L, W), MXU_DTYPE, rb(W), lambda i: (i, 0)),
            ((L, MLA_HEADS * MLA_V), MXU_DTYPE, rb(MLA_HEADS * MLA_V), lambda i: (i, 0))]
    return blocked_fwd(_mla_prep, _mla_prep_opds(proj, cos_p, sin_p, params, tm, False), outs, L // tm, "mla_prep_fwd")


def mla_prep_bwd(proj, cos_p, sin_p, params, dq, dk, dv, tm):
    L = proj.shape[0]
    return blocked_bwd(_mla_prep, _mla_prep_opds(proj, cos_p, sin_p, params, tm, True),
                       [rows(dq, tm), rows(dk, tm), rows(dv, tm)], L // tm, "mla_prep_bwd")


def _scores(q, k, scale, shift=None):
    s = _dot(q, k, _NT) * scale
    if shift is None:
        return s
    row = lax.broadcasted_iota(jnp.int32, s.shape, 0)
    col = lax.broadcasted_iota(jnp.int32, s.shape, 1)
    return jnp.where(col <= row + shift, s, -jnp.inf)


ATTN_ROWS = 512
ATTN_WIDE = 2


def attn_fwd(q, k, v, scale, t):
    L = q.shape[0]
    tq = ATTN_WIDE * t

    def body(q_ref, k_ref, v_ref, o_ref, lse_ref):
        i = pl.program_id(1)
        qb = q_ref[...]

        def step(j, carry, shift=None):
            m, l, acc = carry
            kj = k_ref[pl.ds(pl.multiple_of(j * t, t), t), :]
            vj = v_ref[pl.ds(pl.multiple_of(j * t, t), t), :]
            s = _scores(qb, kj, scale, shift)
            m_new = jnp.maximum(m, jnp.max(s, axis=-1, keepdims=True))
            p = jnp.exp(s - m_new)
            alpha = jnp.exp(m - m_new)
            return m_new, alpha * l + jnp.sum(p, axis=-1, keepdims=True), alpha * acc + _dot(p, vj, _NN)

        carry = (jnp.full((tq, 1), -jnp.inf, F32), jnp.zeros((tq, 1), F32), jnp.zeros((tq, MLA_V), F32))
        carry = lax.fori_loop(0, ATTN_WIDE * i, step, carry)
        for d in range(ATTN_WIDE):
            carry = step(ATTN_WIDE * i + d, carry, -d * t)
        m, l, acc = carry
        o_ref[...] = acc / l
        lse_ref[...] = jnp.broadcast_to(m + jnp.log(l), lse_ref.shape)

    hspec = lambda rows_, w: pl.BlockSpec((rows_, w), lambda h, i: (0, h))
    bspec = lambda w: pl.BlockSpec((tq, w), lambda h, i: (i, h))
    return pl.pallas_call(
        body, name="attn_fwd", grid=(MLA_HEADS, L // tq),
        in_specs=[bspec(MLA_DK), hspec(L, MLA_DK), hspec(L, MLA_V)],
        out_specs=[bspec(MLA_V), bspec(MLA_V)],
        out_shape=[jax.ShapeDtypeStruct((L, MLA_HEADS * MLA_V), F32)] * 2,
        compiler_params=_cparams(("parallel", "parallel")),
    )(q, k, v)


def attn_bwd_dq(q, k, v, o, lse, do, scale, t):
    L = q.shape[0]
    tq = ATTN_WIDE * t

    def body(q_ref, k_ref, v_ref, o_ref, lse_ref, do_ref, dq_ref):
        i = pl.program_id(1)
        qb, dob = q_ref[...], do_ref[...]
        delta = jnp.sum(dob * o_ref[...], axis=-1, keepdims=True)
        lse_c = jnp.max(lse_ref[...], axis=-1, keepdims=True)

        def step(j, dq, shift=None):
            kj = k_ref[pl.ds(pl.multiple_of(j * t, t), t), :]
            vj = v_ref[pl.ds(pl.multiple_of(j * t, t), t), :]
            p = jnp.exp(_scores(qb, kj, scale, shift) - lse_c)
            ds = p * (_dot(dob, vj, _NT) - delta) * scale
            return dq + _dot(ds, kj, _NN)

        dq = lax.fori_loop(0, ATTN_WIDE * i, step, jnp.zeros((tq, MLA_DK), F32))
        for d in range(ATTN_WIDE):
            dq = step(ATTN_WIDE * i + d, dq, -d * t)
        dq_ref[...] = dq

    hspec = lambda w: pl.BlockSpec((L, w), lambda h, i: (0, h))
    bspec = lambda w: pl.BlockSpec((tq, w), lambda h, i: (i, h))
    return pl.pallas_call(
        body, name="attn_bwd_dq", grid=(MLA_HEADS, L // tq),
        in_specs=[bspec(MLA_DK), hspec(MLA_DK), hspec(MLA_V), bspec(MLA_V), bspec(MLA_V), bspec(MLA_V)],
        out_specs=bspec(MLA_DK),
        out_shape=jax.ShapeDtypeStruct((L, MLA_HEADS * MLA_DK), F32),
        compiler_params=_cparams(("parallel", "parallel")),
    )(q, k, v, o, lse, do)


def attn_bwd_dkv(q, k, v, o, lse, do, scale, t):
    L = q.shape[0]
    tk = ATTN_WIDE * t

    def body(q_ref, k_ref, v_ref, o_ref, lse_ref, do_ref, dk_ref, dv_ref):
        j = pl.program_id(1)
        kb, vb = k_ref[...], v_ref[...]

        def step(i, carry, shift=None):
            dk, dv = carry
            r = pl.ds(pl.multiple_of(i * t, t), t)
            qi, doi = q_ref[r, :], do_ref[r, :]
            delta = jnp.sum(doi * o_ref[r, :], axis=-1, keepdims=True)
            lse_c = jnp.max(lse_ref[r, :], axis=-1, keepdims=True)
            p = jnp.exp(_scores(qi, kb, scale, shift) - lse_c)
            ds = p * (_dot(doi, vb, _NT) - delta) * scale
            return dk + _dot(ds, qi, _TN), dv + _dot(p, doi, _TN)

        carry = (jnp.zeros((tk, MLA_DK), F32), jnp.zeros((tk, MLA_V), F32))
        for d in range(ATTN_WIDE):
            carry = step(ATTN_WIDE * j + d, carry, d * t)
        dk, dv = lax.fori_loop(ATTN_WIDE * (j + 1), L // t, step, carry)
        dk_ref[...] = dk
        dv_ref[...] = dv

    hspec = lambda w: pl.BlockSpec((L, w), lambda h, j: (0, h))
    bspec = lambda w: pl.BlockSpec((tk, w), lambda h, j: (j, h))
    return pl.pallas_call(
        body, name="attn_bwd_dkv", grid=(MLA_HEADS, L // tk),
        in_specs=[hspec(MLA_DK), bspec(MLA_DK), bspec(MLA_V), hspec(MLA_V), hspec(MLA_V), hspec(MLA_V)],
        out_specs=[bspec(MLA_DK), bspec(MLA_V)],
        out_shape=[jax.ShapeDtypeStruct((L, MLA_HEADS * MLA_DK), F32), jax.ShapeDtypeStruct((L, MLA_HEADS * MLA_V), F32)],
        compiler_params=_cparams(("parallel", "parallel")),
    )(q, k, v, o, lse, do)


S5_LANES = S5_GB * S5_STATE


def _cmul(ar, ai, br, bi):
    return ar * br - ai * bi, ar * bi + ai * br


def _a_powers(ar, ai, reverse):
    a2 = _cmul(ar, ai, ar, ai)
    a4 = _cmul(*a2, *a2)
    row = lax.broadcasted_iota(jnp.int32, (8, ar.shape[1]), 0)
    e = (8 - row) if reverse else (row + 1)
    tr, ti = jnp.ones((8, ar.shape[1]), F32), jnp.zeros((8, ar.shape[1]), F32)
    for bit, (pr, pi) in ((1, (ar, ai)), (2, a2), (4, a4), (8, _cmul(*a4, *a4))):
        nr, ni = _cmul(tr, ti, pr, pi)
        sel = (e & bit) != 0
        tr, ti = jnp.where(sel, nr, tr), jnp.where(sel, ni, ti)
    pows = []
    for d, (pr, pi) in zip((1, 2, 4), ((ar, ai), a2, a4)):
        keep = (row < 8 - d) if reverse else (row >= d)
        pows.append((jnp.where(keep, pr, 0.0), jnp.where(keep, pi, 0.0)))
    return pows, (tr, ti)


def _scan8(xr, xi, pows, table, cr, ci, reverse):
    for d, (pr, pi) in zip((1, 2, 4), pows):
        shift = 8 - d if reverse else d
        mr, mi = _cmul(pr, pi, pltpu.roll(xr, shift, 0), pltpu.roll(xi, shift, 0))
        xr, xi = xr + mr, xi + mi
    mr, mi = _cmul(table[0], table[1], cr, ci)
    return xr + mr, xi + mi


def _row_of(x, r):
    row = lax.broadcasted_iota(jnp.int32, x.shape, 0)
    return jnp.sum(jnp.where(row == r, x, 0.0), axis=0, keepdims=True)


def _s5_scan_fwd(h_re, h_im, ar, ai, L):
    pows, table = _a_powers(ar, ai, False)

    def step(i, carry):
        r = pl.ds(pl.multiple_of(i * 8, 8), 8)
        xr, xi = _scan8(h_re[r, :], h_im[r, :], pows, table, carry[0], carry[1], False)
        h_re[r, :] = xr
        h_im[r, :] = xi
        return xr[7:8, :], xi[7:8, :]

    z = jnp.zeros((1, ar.shape[1]), F32)
    lax.fori_loop(0, L // 8, step, (z, z))


def _s5_specs(L):
    return [pl.BlockSpec((L, 128), lambda g: (0, g)),
            pl.BlockSpec((1, 128, S5_LANES), lambda g: (g, 0, 0)), pl.BlockSpec((1, 128, S5_LANES), lambda g: (g, 0, 0)),
            pl.BlockSpec((1, 1, S5_LANES), lambda g: (g, 0, 0)), pl.BlockSpec((1, 1, S5_LANES), lambda g: (g, 0, 0)),
            pl.BlockSpec((1, S5_LANES, 128), lambda g: (g, 0, 0)), pl.BlockSpec((1, S5_LANES, 128), lambda g: (g, 0, 0))]


def s5_fwd(u, w_re, w_im, a_re, a_im, c_re, c_im):
    L, D = u.shape

    def body(u_ref, wr, wi, ar, ai, cr, ci, y_ref, h_re, h_im):
        ub = u_ref[...]
        h_re[...] = _dot(ub, wr[0], _NN)
        h_im[...] = _dot(ub, wi[0], _NN)
        _s5_scan_fwd(h_re, h_im, ar[0], ai[0], L)
        y_ref[...] = _dot(h_re[...], cr[0], _NN) - _dot(h_im[...], ci[0], _NN)

    return pl.pallas_call(
        body, name="s5_fwd", grid=(D // 128,),
        in_specs=_s5_specs(L), out_specs=pl.BlockSpec((L, 128), lambda g: (0, g)),
        out_shape=jax.ShapeDtypeStruct((L, D), F32),
        scratch_shapes=[pltpu.VMEM((L, S5_LANES), F32), pltpu.VMEM((L, S5_LANES), F32)],
        compiler_params=_cparams(("parallel",)),
    )(u, w_re, w_im, a_re, a_im, c_re, c_im)


def s5_bwd(u, w_re, w_im, a_re, a_im, c_re, c_im, dy, tc):
    L, D = u.shape
    nch = L // tc

    def body(u_ref, wr, wi, ar_ref, ai_ref, cr, ci, dy_ref, du_ref, dwr, dwi, dar, dai, dcr, dci, h_re, h_im, g_re, g_im):
        ar, ai = ar_ref[0], ai_ref[0]
        ub = u_ref[...]
        h_re[...] = _dot(ub, wr[0], _NN)
        h_im[...] = _dot(ub, wi[0], _NN)
        _s5_scan_fwd(h_re, h_im, ar, ai, L)
        dyb = dy_ref[...]
        dcr[0] = _dot(h_re[...], dyb, _TN)
        dci[0] = -_dot(h_im[...], dyb, _TN)
        pows, table = _a_powers(ar, -ai, True)
        dwr[0] = jnp.zeros((128, S5_LANES), F32)
        dwi[0] = jnp.zeros((128, S5_LANES), F32)
        z1 = jnp.zeros((1, S5_LANES), F32)
        z8 = jnp.zeros((8, S5_LANES), F32)

        def chunk(cc, carry):
            c0 = pl.multiple_of((nch - 1 - cc) * tc, tc)
            rows_c = pl.ds(c0, tc)
            dyc = dy_ref[rows_c, :]
            g_re[...] = _dot(dyc, cr[0], _NT)
            g_im[...] = -_dot(dyc, ci[0], _NT)

            def step(ii, cy):
                gr_c, gi_c, acc_r, acc_i = cy
                i8 = pl.multiple_of((tc // 8 - 1 - ii) * 8, 8)
                rl = pl.ds(i8, 8)
                xr, xi = _scan8(g_re[rl, :], g_im[rl, :], pows, table, gr_c, gi_c, True)
                g_re[rl, :] = xr
                g_im[rl, :] = xi
                t0 = c0 + i8
                hb_r, hb_i = h_re[pl.ds(t0, 8), :], h_im[pl.ds(t0, 8), :]
                tp = pl.multiple_of(jnp.maximum(t0 - 8, 0), 8)
                first = (t0 > 0).astype(F32)
                pr = h_re[pl.ds(tp, 8), :][7:8, :] * first
                pi = h_im[pl.ds(tp, 8), :][7:8, :] * first
                row = lax.broadcasted_iota(jnp.int32, xr.shape, 0)
                hp_r = jnp.where(row == 0, pr, pltpu.roll(hb_r, 1, 0))
                hp_i = jnp.where(row == 0, pi, pltpu.roll(hb_i, 1, 0))
                return (xr[0:1, :], xi[0:1, :],
                        acc_r + xr * hp_r + xi * hp_i, acc_i + xi * hp_r - xr * hp_i)

            cy = lax.fori_loop(0, tc // 8, step, carry)
            uc = u_ref[rows_c, :]
            gr, gi = g_re[...], g_im[...]
            du_ref[rows_c, :] = _dot(gr, wr[0], _NT) + _dot(gi, wi[0], _NT)
            dwr[0] += _dot(uc, gr, _TN)
            dwi[0] += _dot(uc, gi, _TN)
            return cy

        _, _, acc_r, acc_i = lax.fori_loop(0, nch, chunk, (z1, z1, z8, z8))
        dar[0] = jnp.sum(acc_r, axis=0, keepdims=True)
        dai[0] = jnp.sum(acc_i, axis=0, keepdims=True)

    specs = _s5_specs(L)
    return pl.pallas_call(
        body, name="s5_bwd", grid=(D // 128,),
        in_specs=specs + [pl.BlockSpec((L, 128), lambda g: (0, g))],
        out_specs=[pl.BlockSpec((L, 128), lambda g: (0, g))] + specs[1:],
        out_shape=[jax.ShapeDtypeStruct((L, D), F32)] + [jax.ShapeDtypeStruct(x.shape, F32)
                                                        for x in (w_re, w_im, a_re, a_im, c_re, c_im)],
        scratch_shapes=[pltpu.VMEM((L, S5_LANES), F32), pltpu.VMEM((L, S5_LANES), F32),
                        pltpu.VMEM((tc, S5_LANES), F32), pltpu.VMEM((tc, S5_LANES), F32)],
        compiler_params=_cparams(("parallel",)),
    )(u, w_re, w_im, a_re, a_im, c_re, c_im, dy)


def _s5_discretize(lr, li, ldt, br, bi):
    dt = jnp.exp(ldt)
    mag = jnp.exp(lr * dt)
    ar, ai = mag * jnp.cos(li * dt), mag * jnp.sin(li * dt)
    den = lr * lr + li * li
    zr = ((ar - 1.0) * lr + ai * li) / den
    zi = (ai * lr - (ar - 1.0) * li) / den
    p = lax.broadcasted_iota(jnp.int32, (S5_STATE, S5_STATE * S5_GROUP), 0)
    c = lax.broadcasted_iota(jnp.int32, (S5_STATE, S5_STATE * S5_GROUP), 1)
    rep = (c // S5_GROUP == p).astype(F32)
    zr, zi = _dot(zr, rep, _NN, HI), _dot(zi, rep, _NN, HI)
    return ar, ai, zr * br - zi * bi, zr * bi + zi * br


def _conv_shift(x, d):
    row = lax.broadcasted_iota(jnp.int32, x.shape, 0)
    return jnp.where(row >= d, pltpu.roll(x, d, 0), 0.0)


def _conv_unshift(x, d):
    n = x.shape[0]
    row = lax.broadcasted_iota(jnp.int32, x.shape, 0)
    return jnp.where(row < n - d, pltpu.roll(x, n - d, 0), 0.0)


@functools.partial(jax.custom_vjp, nondiff_argnums=(1,))
def _shift_rows(x, d):
    return _conv_shift(x, d)


_shift_rows.defvjp(lambda x, d: (_conv_shift(x, d), None), lambda d, _, g: (_conv_unshift(g, d),))


def _conv_gate(ug, uv, wg0, wg1, wg2, wv0, wv1, wv2, bg, bv):
    def conv(u, w0, w1, w2, b):
        return u * w2 + _shift_rows(u, 1) * w1 + _shift_rows(u, 2) * w0 + b
    return (jax.nn.silu(conv(ug, wg0, wg1, wg2, bg)) * conv(uv, wv0, wv1, wv2, bv),)


def _rms_fn(x, gain):
    return (_rms(x, gain),)


def _softmax_rows(s):
    e = jnp.exp(s - lax.stop_gradient(jnp.max(s, axis=-1, keepdims=True)))
    return e / jnp.sum(e, axis=-1, keepdims=True)


def _xa_core(qp, k, v, q_gain):
    dh = qp.shape[1] // XA_HEADS
    outs = []
    for h in range(XA_HEADS):
        sl = slice(h * dh, (h + 1) * dh)
        p = _softmax_rows(_dot(_rms(qp[:, sl], q_gain), k[:, sl], _NT) * (dh ** -0.5))
        outs.append(_dot(p, v[:, sl], _NN))
    return (jnp.concatenate(outs, axis=1),)


def _mem_kv(mem, mem_gain, wk, wv, k_gain):
    m = _rms(mem, mem_gain)
    kp = _dot(m, wk, _NN)
    dh = kp.shape[1] // XA_HEADS
    k = jnp.concatenate([_rms(kp[:, h * dh:(h + 1) * dh], k_gain) for h in range(XA_HEADS)], axis=1)
    return k, _dot(m, wv, _NN)


def _s5_post(y, u, d):
    return (jax.nn.gelu(y + d * u),)


def _glu(a, b):
    return (a * jax.nn.sigmoid(b),)


def _lb_first(logits):
    e = jnp.exp(logits - lax.stop_gradient(jnp.max(logits, axis=0, keepdims=True)))
    return (_row_of(e, 0) / jnp.sum(e, axis=0, keepdims=True),)


def _loss_fn(y, t):
    e = y - t
    part = 0.5 * jnp.sum(e * e) / y.shape[1]
    return e * (1.0 / y.shape[1]), jnp.full((8, 128), part / (8 * 128), F32)


def _out(shape, dtype, tm):
    return (shape, dtype, (tm, shape[1]), lambda i: (i, 0))


def rms_fwd(h, gain, tm, dtype):
    return blocked_fwd(_rms_fn, [rows(h, tm), full(gain)], [_out(h.shape, dtype, tm)], h.shape[0] // tm, "rms_fwd")[0]


def rms_bwd(h, gain, dy, tm, residual):
    return blocked_bwd(_rms_fn, [rows(h, tm, 'blk'), full(gain, 'acc')], [rows(dy, tm)], h.shape[0] // tm, "rms_bwd",
                       plus=rows(residual, tm))


def _adamw_math(w, g, m, v):
    m = ADAM_B1 * m + (1.0 - ADAM_B1) * g
    v = ADAM_B2 * v + (1.0 - ADAM_B2) * jnp.square(g)
    m_hat = m / (1.0 - ADAM_B1 ** ADAM_STEP)
    v_hat = v / (1.0 - ADAM_B2 ** ADAM_STEP)
    return -ADAM_LR * (m_hat / (jnp.sqrt(v_hat) + ADAM_EPS) + ADAM_WD * w), m, v


def adamw(w, g, m, v, name):
    R = w.shape[0]
    tm = _tile(R, 256)
    assert g.shape == w.shape == m.shape == v.shape, (name, w.shape, g.shape)

    def body(w_ref, g_ref, m_ref, v_ref, d_ref, nm_ref, nv_ref):
        d_ref[...], nm_ref[...], nv_ref[...] = _adamw_math(w_ref[...], g_ref[...], m_ref[...], v_ref[...])

    spec = pl.BlockSpec((tm, w.shape[1]), lambda i: (i, 0))
    return pl.pallas_call(
        body, name=name, grid=(R // tm,), in_specs=[spec] * 4, out_specs=[spec] * 3,
        out_shape=[jax.ShapeDtypeStruct(w.shape, F32)] * 3, compiler_params=_cparams(("parallel",)),
    )(w, g, m, v)


def adamw_layer(w, g, m, v, layer, bufs, name):
    R, C = g.shape
    tm = _tile(R, 256)
    assert w.shape[1:] == (R, C) and all(b.shape == w.shape for b in bufs), (name, w.shape, g.shape)

    def body(w_ref, g_ref, m_ref, v_ref, *rest):
        g_out, d_ref, nm_ref, nv_ref = rest[-4:]
        g_ = g_ref[...]
        g_out[...] = g_
        d_ref[...], nm_ref[...], nv_ref[...] = _adamw_math(w_ref[...], g_, m_ref[...], v_ref[...])

    lspec = pl.BlockSpec((None, tm, C), lambda i: (layer, i, 0))
    any_spec = pl.BlockSpec(memory_space=pl.ANY)
    return pl.pallas_call(
        body, name=name, grid=(R // tm,),
        in_specs=[lspec, pl.BlockSpec((tm, C), lambda i: (i, 0)), lspec, lspec] + [any_spec] * 4,
        out_specs=[lspec] * 4, out_shape=[jax.ShapeDtypeStruct(w.shape, F32)] * 4,
        input_output_aliases={4: 0, 5: 1, 6: 2, 7: 3}, compiler_params=_cparams(("parallel",)),
    )(w, g, m, v, *bufs)


def add2(x, y, name, out_dtype=F32):
    shape = x.shape
    x, y = x.reshape(-1, shape[-1]), y.reshape(-1, shape[-1])
    R, C = x.shape
    tm = _tile(R, 256)

    def body(x_ref, y_ref, o_ref):
        o_ref[...] = (x_ref[...] + y_ref[...]).astype(o_ref.dtype)

    spec = pl.BlockSpec((tm, C), lambda i: (i, 0))
    return pl.pallas_call(
        body, name=name, grid=(R // tm,), in_specs=[spec, spec], out_specs=spec,
        out_shape=jax.ShapeDtypeStruct((R, C), out_dtype), compiler_params=_cparams(("parallel",)),
    )(x, y).reshape(shape)


def add_chips(own, got, name):
    n, R, C = got.shape
    tm = _tile(R, 256)

    def body(*refs):
        acc = refs[0][...].astype(F32)
        for r in refs[1:-1]:
            acc = acc + r[...].astype(F32)
        refs[-1][...] = acc

    return pl.pallas_call(
        body, name=name, grid=(R // tm,),
        in_specs=[pl.BlockSpec((tm, C), lambda i: (i, 0))] + [pl.BlockSpec((None, tm, C), lambda i, j=j: (j, i, 0))
                                                            for j in range(n)],
        out_specs=pl.BlockSpec((tm, C), lambda i: (i, 0)),
        out_shape=jax.ShapeDtypeStruct((R, C), F32), compiler_params=_cparams(("parallel",)),
    )(own, *([got] * n))


_HBM = pl.BlockSpec(memory_space=pltpu.HBM)
N_CHIPS = 4


def _my_place():
    return lax.axis_index("x"), lax.axis_index("y"), lax.axis_index("c")


def _window(ref, axis, start, size):
    idx = [slice(None)] * len(ref.shape)
    idx[axis] = pl.ds(start, size)
    return ref.at[tuple(idx)]


def _comm_call(body, name, xs, out_shapes, n_remote, n_local, sequencer=None):
    sems = [pltpu.SemaphoreType.DMA((n_remote,)), pltpu.SemaphoreType.DMA((n_remote,)),
            pltpu.SemaphoreType.DMA((max(n_local, 1),))]
    if sequencer is None:
        return pl.pallas_call(
            body, name=name, in_specs=[_HBM] * len(xs), out_specs=[_HBM] * len(out_shapes), out_shape=out_shapes,
            scratch_shapes=sems, compiler_params=pltpu.CompilerParams(has_side_effects=True),
        )(*xs)
    peers_of, collective_id = sequencer
    hbm = pltpu.MemorySpace.HBM
    x_refs = [jax.new_ref(x, memory_space=hbm) for x in xs]
    o_refs = [jax.empty_ref(s, memory_space=hbm) for s in out_shapes]

    @pl.kernel(mesh=plsc.ScalarSubcoreMesh(axis_name="sequencer", num_cores=1), name=name, scratch_types=tuple(sems),
               compiler_params=pltpu.CompilerParams(collective_id=collective_id))
    def launch(send_sems, recv_sems, local_sems):
        peers = peers_of(*_my_place())
        barrier = pltpu.get_barrier_semaphore()
        for peer in peers:
            pl.semaphore_signal(barrier, inc=1, device_id=peer, device_id_type=MESH)
        pl.semaphore_wait(barrier, len(peers))
        body(*x_refs, *o_refs, send_sems, recv_sems, local_sems)

    launch()
    return [o[...] for o in o_refs]


def _sibling(mx, my, mc):
    return [(mx, my, 1 - mc)]


def _same_core_of_other_chips(mx, my, mc):
    return [(tx, ty, mc) for tx, ty in _other_chips(mx, my)]


def _run(copies):
    for cp in copies:
        cp.start()
    for cp in copies:
        cp.wait()


def _other_chips(mx, my):
    return [(mx ^ (j >> 1), my ^ (j & 1)) for j in (1, 2, 3)]


def chip_gather(xs, axes, name):
    n = len(xs)
    shapes, final = [], []
    for x, ax in zip(xs, axes):
        s = list(x.shape)
        if ax is None:
            shapes.append([N_CHIPS] + s)
            final.append(shapes[-1])
        elif ax < x.ndim - 1:
            shapes.append(s[:ax] + [N_CHIPS] + s[ax:])
            final.append(s[:ax] + [N_CHIPS * s[ax]] + s[ax + 1:])
        else:
            assert s[ax] % 128 == 0, (name, s)
            shapes.append(s[:ax] + [N_CHIPS * s[ax]])
            final.append(shapes[-1])

    def body(*refs):
        x_refs, o_refs = refs[:n], refs[n:2 * n]
        send_sems, recv_sems, local_sems = refs[2 * n:]
        mx, my, mc = _my_place()
        q = 2 * mx + my
        copies = []
        for i, (x_ref, o_ref, ax) in enumerate(zip(x_refs, o_refs, axes)):
            if ax is None or ax < len(x_ref.shape) - 1:
                dst = o_ref.at[(slice(None),) * (ax or 0) + (q,)]
            else:
                dst = _window(o_ref, ax, q * x_ref.shape[ax], x_ref.shape[ax])
            copies.append(pltpu.make_async_copy(x_ref, dst, local_sems.at[i]))
            for j, (tx, ty) in enumerate(_other_chips(mx, my)):
                copies.append(pltpu.make_async_remote_copy(
                    src_ref=x_ref, dst_ref=dst, send_sem=send_sems.at[3 * i + j], recv_sem=recv_sems.at[3 * i + j],
                    device_id=(tx, ty, mc), device_id_type=MESH))
        _run(copies)

    out_shapes = [jax.ShapeDtypeStruct(tuple(s), x.dtype) for s, x in zip(shapes, xs)]
    return [o.reshape(f) for o, f in zip(_comm_call(body, name, xs, out_shapes, 3 * n, n), final)]


def gather_two_level(xs, name):
    n = len(xs)
    shapes = [jax.ShapeDtypeStruct((2, N_CHIPS) + x.shape[1:], x.dtype) for x in xs]

    def body(*refs):
        x_refs, o_refs = refs[:n], refs[n:2 * n]
        send_sems, recv_sems, local_sems = refs[2 * n:]
        mx, my, mc = _my_place()
        q = 2 * mx + my
        first, local, second = [], [], []
        for i, (x_ref, o_ref) in enumerate(zip(x_refs, o_refs)):
            local.append(pltpu.make_async_copy(x_ref.at[mc], o_ref.at[mc, q], local_sems.at[i]))
            for j, (tx, ty) in enumerate(_other_chips(mx, my)):
                first.append(pltpu.make_async_remote_copy(
                    src_ref=x_ref.at[mc], dst_ref=o_ref.at[mc, q], send_sem=send_sems.at[4 * i + j],
                    recv_sem=recv_sems.at[4 * i + j], device_id=(tx, ty, mc), device_id_type=MESH))
            second.append(pltpu.make_async_remote_copy(
                src_ref=o_ref.at[mc], dst_ref=o_ref.at[mc], send_sem=send_sems.at[4 * i + 3],
                recv_sem=recv_sems.at[4 * i + 3], device_id=(mx, my, 1 - mc), device_id_type=MESH))
        for cp in local + first:
            cp.start()
        for cp in local:
            cp.wait()
        for cp in first:
            cp.wait_recv()
        _run(second)
        for cp in first:
            cp.wait_send()

    return _comm_call(body, name, xs, shapes, 4 * n, n)


def gather_two_level_sequencer(xs, name, collective_id):
    n = len(xs)
    hbm = pltpu.MemorySpace.HBM
    x_refs = [jax.new_ref(x, memory_space=hbm) for x in xs]
    o_refs = [jax.empty_ref(jax.ShapeDtypeStruct((2, N_CHIPS) + x.shape[1:], x.dtype), memory_space=hbm) for x in xs]

    @pl.kernel(mesh=plsc.ScalarSubcoreMesh(axis_name="sequencer", num_cores=1), name=name,
               scratch_types=(pltpu.SemaphoreType.DMA((4 * n,)), pltpu.SemaphoreType.DMA((4 * n,)),
                              pltpu.SemaphoreType.DMA((n,))),
               compiler_params=pltpu.CompilerParams(collective_id=collective_id))
    def launch(send_sems, recv_sems, local_sems):
        mx, my, mc = _my_place()
        peers = [(tx, ty, mc) for tx, ty in _other_chips(mx, my)] + [(mx, my, 1 - mc)]
        barrier = pltpu.get_barrier_semaphore()
        for peer in peers:
            pl.semaphore_signal(barrier, inc=1, device_id=peer, device_id_type=MESH)
        pl.semaphore_wait(barrier, len(peers))
        q = 2 * mx + my
        first, local, second = [], [], []
        for i, (x_ref, o_ref) in enumerate(zip(x_refs, o_refs)):
            local.append(pltpu.make_async_copy(x_ref.at[mc], o_ref.at[mc, q], local_sems.at[i]))
            for j, peer in enumerate(peers[:3]):
                first.append(pltpu.make_async_remote_copy(
                    src_ref=x_ref.at[mc], dst_ref=o_ref.at[mc, q], send_sem=send_sems.at[4 * i + j],
                    recv_sem=recv_sems.at[4 * i + j], device_id=peer, device_id_type=MESH))
            second.append(pltpu.make_async_remote_copy(
                src_ref=o_ref.at[mc], dst_ref=o_ref.at[mc], send_sem=send_sems.at[4 * i + 3],
                recv_sem=recv_sems.at[4 * i + 3], device_id=peers[3], device_id_type=MESH))
        for cp in local + first:
            cp.start()
        for cp in local:
            cp.wait()
        for cp in first:
            cp.wait_recv()
        _run(second)
        for cp in first:
            cp.wait_send()

    launch()
    return [o[...] for o in o_refs]


def pair_swap(xs, name, halves, collective_id=None):
    n = len(xs)
    shapes = [jax.ShapeDtypeStruct(x.shape[:1] + x.shape[2:] if halves else x.shape, x.dtype) for x in xs]

    def body(*refs):
        x_refs, o_refs = refs[:n], refs[n:2 * n]
        send_sems, recv_sems, _ = refs[2 * n:]
        mx, my, mc = _my_place()
        _run([pltpu.make_async_remote_copy(
            src_ref=x_ref.at[:, 1 - mc] if halves else x_ref, dst_ref=o_ref, send_sem=send_sems.at[i],
            recv_sem=recv_sems.at[i], device_id=(mx, my, 1 - mc), device_id_type=MESH)
            for i, (x_ref, o_ref) in enumerate(zip(x_refs, o_refs))])

    return _comm_call(body, name, xs, shapes, n, 0, None if collective_id is None else (_sibling, collective_id))


def chip_all_to_all(xs, name, collective_id=None):
    n = len(xs)
    shapes = [jax.ShapeDtypeStruct((N_CHIPS - 1,) + x.shape[1:], x.dtype) for x in xs]

    def body(*refs):
        x_refs, o_refs = refs[:n], refs[n:2 * n]
        send_sems, recv_sems, _ = refs[2 * n:]
        mx, my, mc = _my_place()
        copies = []
        for i, (x_ref, o_ref) in enumerate(zip(x_refs, o_refs)):
            for j, (tx, ty) in enumerate(_other_chips(mx, my)):
                copies.append(pltpu.make_async_remote_copy(
                    src_ref=x_ref.at[2 * tx + ty], dst_ref=o_ref.at[j], send_sem=send_sems.at[3 * i + j],
                    recv_sem=recv_sems.at[3 * i + j], device_id=(tx, ty, mc), device_id_type=MESH))
        _run(copies)

    return _comm_call(body, name, xs, shapes, 3 * n, 0,
                      None if collective_id is None else (_same_core_of_other_chips, collective_id))


WEIGHTS = ['norm_mix', 'norm_xa', 'norm_mem', 'norm_ffn', 'xa_wq', 'xa_wk', 'xa_wv', 'xa_wo', 'xa_q_norm', 'xa_k_norm',
           'ffn_w_up', 'ffn_conv_w', 'ffn_conv_b', 'ffn_w_down', 'hg_lb_logits', 'mix_w_in', 'hg_out_norm',
           'mla_q_a_norm', 'mla_w_uq', 'mla_kv_a_norm', 'mla_w_ukv', 'mla_qn_nope', 'mla_qn_rope', 'mla_kn_nope',
           'mla_kn_rope', 'mix_w_out', 's5_lam_re', 's5_lam_im', 's5_log_dt', 's5_b_re', 's5_b_im', 's5_c_re',
           's5_c_im', 's5_d', 's5_w_glu_a', 's5_w_glu_b']
INPUTS = ['x', 'mem', 'positions'] + WEIGHTS + ['loss_target'] + ['m_' + n for n in WEIGHTS] + ['v_' + n for n in WEIGHTS]
SHARD_AXIS = {'xa_wq': 1, 'xa_wk': 1, 'xa_wv': 1, 'xa_wo': 1, 'ffn_w_up': 2, 'ffn_conv_w': 2, 'ffn_w_down': 1,
              'mix_w_in': 2, 'mla_w_uq': 2, 'mla_w_ukv': 2, 'mix_w_out': 1, 's5_d': 1, 's5_w_glu_a': 1, 's5_w_glu_b': 1}
BIG = ['xa_wq', 'xa_wk', 'xa_wv', 'xa_wo', 'ffn_w_up', 'ffn_w_down', 'mix_w_in', 'mix_w_out', 's5_w_glu_a', 's5_w_glu_b']
FIRST_NEEDED = ('mix_w_in', 'mix_w_out')
SMALL_SHARDED = [n for n in WEIGHTS if n in SHARD_AXIS and n not in BIG]
REPLICATED = [n for n in WEIGHTS if n not in SHARD_AXIS]
SMALL = SMALL_SHARDED + REPLICATED
PACK_W = 1024
ROW_MULT = 16
W_IN_SHARD = IN_WIDTH // N_CHIPS
W_IN_SHARD_PAD = 640


def _pack(flats, mult=ROW_MULT):
    flat = jnp.concatenate([f.reshape(-1) for f in flats])
    unit = mult * PACK_W
    n = -(-flat.shape[0] // unit) * unit
    return jnp.pad(flat, (0, n - flat.shape[0])).reshape(n // PACK_W, PACK_W)


def _unpack(packed, shapes):
    flat, out, o = packed.reshape(-1), [], 0
    for s in shapes:
        n = math.prod(s)
        out.append(flat[o:o + n].reshape(s))
        o += n
    return out


def _rope_pad(w):
    z = jnp.zeros(w.shape[:-1] + (MLA_ROPE // 2,), w.dtype)
    return jnp.concatenate([w[..., :MLA_ROPE // 2], z, w[..., MLA_ROPE // 2:], z], axis=-1)


def _rope_unpad(g):
    return jnp.concatenate([g[..., :MLA_ROPE // 2], g[..., 64:64 + MLA_ROPE // 2]], axis=-1)


def _blockdiag_in(bb):
    nb = bb.shape[0] // S5_GB
    t = bb.reshape(nb, S5_GB, S5_STATE, S5_GROUP).transpose(0, 1, 3, 2)
    return jnp.einsum('bgmp,gh->bgmhp', t, jnp.eye(S5_GB, dtype=bb.dtype)).reshape(nb, S5_GB * S5_GROUP, S5_LANES)


def _blockdiag_in_t(dw):
    nb = dw.shape[0]
    t = jnp.einsum('bgmhp,gh->bgmp', dw.reshape(nb, S5_GB, S5_GROUP, S5_GB, S5_STATE), jnp.eye(S5_GB, dtype=dw.dtype))
    return t.transpose(0, 1, 3, 2).reshape(nb * S5_GB, S5_STATE, S5_GROUP)


def _blockdiag_out(c):
    nb = c.shape[0] // S5_GB
    t = c.reshape(nb, S5_GB, S5_GROUP, S5_STATE).transpose(0, 1, 3, 2)
    return jnp.einsum('bgpm,gh->bgphm', t, jnp.eye(S5_GB, dtype=c.dtype)).reshape(nb, S5_LANES, S5_GB * S5_GROUP)


def _blockdiag_out_t(dc):
    nb = dc.shape[0]
    t = jnp.einsum('bgphm,gh->bgpm', dc.reshape(nb, S5_GB, S5_STATE, S5_GB, S5_GROUP), jnp.eye(S5_GB, dtype=dc.dtype))
    return t.transpose(0, 1, 3, 2).reshape(nb * S5_GB, S5_GROUP, S5_STATE)


def _gather_weights(P):
    def halves(x):
        return x if x.shape[0] == 2 else x.reshape(2, x.shape[1] // 2, x.shape[2])

    now = [n for n in BIG if n in FIRST_NEEDED]
    later = [n for n in BIG if n not in FIRST_NEEDED]
    xs = [halves(P[n].astype(BF16)) for n in now] + [halves(_pack([P[n] for n in SMALL_SHARDED], 2 * ROW_MULT)[None])]
    got = gather_two_level(xs, "gather_weights")
    got, xs_later = lax.optimization_barrier((got, [halves(P[n].astype(BF16)) for n in later]))
    got_later = gather_two_level_sequencer(xs_later, "gather_weights_later", 1)
    full_w = {}
    for n, g in list(zip(now, got[:-1])) + list(zip(later, got_later)):
        two_layers, by_rows = P[n].shape[0] == 2, SHARD_AXIS[n] == 1
        if two_layers and by_rows:
            full_w[n] = g.reshape(2, N_CHIPS * g.shape[2], g.shape[3])
        elif two_layers:
            full_w[n] = g.transpose(0, 2, 1, 3).reshape(2, g.shape[2], N_CHIPS * g.shape[3])
        elif by_rows:
            full_w[n] = g.transpose(1, 0, 2, 3).reshape(1, 2 * N_CHIPS * g.shape[2], g.shape[3])
        else:
            full_w[n] = g.transpose(0, 2, 1, 3).reshape(1, 2 * g.shape[2], N_CHIPS * g.shape[3])
    small = got[-1].transpose(1, 0, 2, 3).reshape(N_CHIPS, -1, PACK_W)
    per_chip = [_unpack(small[q], [P[n].shape for n in SMALL_SHARDED]) for q in range(N_CHIPS)]
    for i, n in enumerate(SMALL_SHARDED):
        full_w[n] = jnp.concatenate([per_chip[q][i] for q in range(N_CHIPS)], axis=SHARD_AXIS[n])
    return full_w


def _halves_first(x):
    return x.reshape(x.shape[0], 2, x.shape[1] // 2, x.shape[2]).transpose(1, 0, 2, 3)


def _reduce_batch(items, small, P, results, tag, ids):
    mx, my, mc = _my_place()
    q = 2 * mx + my
    ids = ids or {}
    names = [f"{n}_{lyr}" for n, lyr, _ in items] + (['small'] if small is not None else [])
    xs = [g.reshape(N_CHIPS, 2, g.shape[1] // 2, g.shape[2]) for g in [g for _, _, g in items] + ([small] if small is not None else [])]
    def after(vals, tie):
        return (vals, None) if tie is None else lax.optimization_barrier((vals, tie))

    theirs = pair_swap(xs, "grads_pair_swap_" + tag, True, ids.get('swap'))
    theirs, tie = after(theirs, (yield None))
    pair = [add2(lax.dynamic_index_in_dim(x, mc, 1, False), t, "grads_pair_sum_" + n, F32 if n == 'small' else BF16)
            for x, t, n in zip(xs, theirs, names)]
    got = chip_all_to_all(pair, "grads_chip_all_to_all_" + tag, ids.get('a2a'))
    got, tie = after(got, (yield pair[-1] if tie is None else tie))
    summed = [add_chips(lax.dynamic_index_in_dim(p, q, 0, False), g, "grads_chip_sum_" + n)
              for p, g, n in zip(pair, got, names)]
    other = pair_swap(summed, "grads_pair_join_" + tag, False, ids.get('join'))
    other, tie = after(other, (yield summed[-1] if tie is None else tie))
    joined = [lax.cond(mc == 0, lambda a, b: jnp.concatenate([a, b], axis=0), lambda a, b: jnp.concatenate([b, a], axis=0),
                       s, o) for s, o in zip(summed, other)]
    for (n, lyr, _), g in zip(items, joined):
        if n == 'mix_w_in':
            g = g[:, :W_IN_SHARD]
        view = (P[n].shape[0], g.shape[0], P[n].shape[-1])
        bufs = results.get(n) or [lax.empty(view, F32) for _ in range(4)]
        results[n] = adamw_layer(P[n].reshape(view), g, P['m_' + n].reshape(view), P['v_' + n].reshape(view), lyr, bufs,
                                 f"adamw_{n}_{lyr}")
    if small is not None:
        results['small_quarter'] = joined[-1]
    yield tie


def _update_small(small_quarter, GS, P):
    mx, my, _ = _my_place()
    q = 2 * mx + my
    small_sum = chip_gather([small_quarter], [0], "grads_small_gather")[0]
    g_small = dict(zip(SMALL, _unpack(small_sum, [GS[n].shape for n in SMALL])))
    for n in SMALL_SHARDED:
        s = P[n].shape[SHARD_AXIS[n]]
        g_small[n] = lax.dynamic_slice_in_dim(g_small[n], q * s, s, axis=SHARD_AXIS[n])
    grad, delta, new_m, new_v = {}, {}, {}, {}
    packed = lambda prefix: _pack([P[prefix + n] for n in SMALL])
    d, m_, v_ = adamw(packed(''), _pack([g_small[n] for n in SMALL]), packed('m_'), packed('v_'), "adamw_small")
    shapes = [P[n].shape for n in SMALL]
    grad.update(g_small)
    for out, pk in ((delta, d), (new_m, m_), (new_v, v_)):
        out.update(zip(SMALL, _unpack(pk, shapes)))
    return grad, delta, new_m, new_v


def _row(v):
    return v.reshape(1, -1)


def _xattn_fwd(h, mem, W, lyr, tm):
    g_xa, g_mem = _row(W['norm_xa'][lyr]), _row(W['norm_mem'][lyr])
    g_q, g_k = _row(W['xa_q_norm'][lyr]), _row(W['xa_k_norm'][lyr])
    wq, wk, wv, wo = (W[n][lyr] for n in ('xa_wq', 'xa_wk', 'xa_wv', 'xa_wo'))
    L, D = h.shape
    M = mem.shape[0]
    hx = rms_fwd(h, g_xa, tm, MXU_DTYPE)
    qp = matmul(hx, wq, name="xa_q")
    kv_opds = [full(mem), full(g_mem), full(wk), full(wv), full(g_k)]
    k, v = blocked_fwd(_mem_kv, kv_opds, [((M, D), F32, (M, D), lambda i: (0, 0))] * 2, 1, "xa_mem_kv")
    o = blocked_fwd(_xa_core, [rows(qp, tm), full(k), full(v), full(g_q)], [_out((L, D), MXU_DTYPE, tm)], L // tm,
                    "xa_core")[0]
    out = matmul(o, wo, add=h, name="xa_o")
    return out, (h, hx, qp, k, v, o)


def _xattn_bwd(dout, saved, mem, W, lyr, tm):
    h, hx, qp, k, v, o = saved
    g_xa, g_mem = _row(W['norm_xa'][lyr]), _row(W['norm_mem'][lyr])
    g_q, g_k = _row(W['xa_q_norm'][lyr]), _row(W['xa_k_norm'][lyr])
    wq, wk, wv, wo = (W[n][lyr] for n in ('xa_wq', 'xa_wk', 'xa_wv', 'xa_wo'))
    L = h.shape[0]
    do = matmul(dout, wo, "nt", name="xa_do")
    d_wo = matmul(o, dout, "tn", name="xa_dwo")
    dqp, dk, dv, d_gq = blocked_bwd(_xa_core, [rows(qp, tm, 'blk'), full(k, 'acc'), full(v, 'acc'), full(g_q, 'acc')],
                                    [rows(do, tm)], L // tm, "xa_core_bwd")
    d_wq = matmul(hx, dqp, "tn", name="xa_dwq")
    dhx = matmul(dqp, wq, "nt", name="xa_dhx")
    dh, d_gxa = rms_bwd(h, g_xa, dhx, tm, dout)
    d_gmem, d_wk, d_wv, d_gk = blocked_bwd(
        _mem_kv, [full(mem), full(g_mem, 'acc'), full(wk, 'acc'), full(wv, 'acc'), full(g_k, 'acc')],
        [full(dk), full(dv)], 1, "xa_mem_kv_bwd")
    by_chip = lambda g: g.reshape(N_CHIPS, g.shape[0] // N_CHIPS, g.shape[1])
    grads = {'norm_xa': d_gxa, 'norm_mem': d_gmem, 'xa_q_norm': d_gq, 'xa_k_norm': d_gk,
             'xa_wq': by_chip(d_wq), 'xa_wk': by_chip(d_wk), 'xa_wv': by_chip(d_wv), 'xa_wo': by_chip(d_wo)}
    return dh, grads


def _conv_params(W, lyr):
    cw, cb = W['ffn_conv_w'][lyr], W['ffn_conv_b'][lyr]
    F = cw.shape[1] // 2
    return [cw[0:1, :F], cw[1:2, :F], cw[2:3, :F], cw[0:1, F:], cw[1:2, F:], cw[2:3, F:], _row(cb[:F]), _row(cb[F:])]


def _ffn_fwd(h, W, lyr, tm):
    L, D = h.shape
    w_up, w_down = W['ffn_w_up'][lyr], W['ffn_w_down'][lyr]
    F = w_down.shape[0]
    hf = rms_fwd(h, _row(W['norm_ffn'][lyr]), tm, MXU_DTYPE)
    ug = matmul(hf, w_up[:, :F], name="ffn_up_gate")
    uv = matmul(hf, w_up[:, F:], name="ffn_up_value")
    opds = [cols(ug, 128), cols(uv, 128)] + [cols(p, 128) for p in _conv_params(W, lyr)]
    a = blocked_fwd(_conv_gate, opds, [((L, F), MXU_DTYPE, (L, 128), lambda j: (0, j))], F // 128, "ffn_conv_gate")[0]
    out = matmul(a, w_down, add=h, name="ffn_down")
    return out, (h, hf, ug, uv, a)


def _ffn_bwd(dout, saved, W, lyr, tm):
    h, hf, ug, uv, a = saved
    w_up, w_down = W['ffn_w_up'][lyr], W['ffn_w_down'][lyr]
    F = w_down.shape[0]
    da = matmul(dout, w_down, "nt", name="ffn_da")
    d_wdown = matmul(a, dout, "tn", name="ffn_dwdown")
    opds = [cols(ug, 128, 'blk'), cols(uv, 128, 'blk')] + [cols(p, 128, 'blk') for p in _conv_params(W, lyr)]
    gs = blocked_bwd(_conv_gate, opds, [cols(da, 128)], F // 128, "ffn_conv_gate_bwd")
    dug, duv = gs[0], gs[1]
    d_cw = jnp.concatenate([jnp.concatenate(gs[2:5], axis=0), jnp.concatenate(gs[5:8], axis=0)], axis=1)
    d_cb = jnp.concatenate([gs[8], gs[9]], axis=1)[0]
    half = N_CHIPS // 2
    d_wup = lax.empty((N_CHIPS, hf.shape[1], w_up.shape[1] // N_CHIPS), F32)
    d_wup = matmul(hf, dug, "tn", name="ffn_dwup_gate", into=(d_wup, 0), col_blocks=half)
    d_wup = matmul(hf, duv, "tn", name="ffn_dwup_value", into=(d_wup, half), col_blocks=half)
    dhf = matmul(dug, w_up[:, :F], "nt", name="ffn_dhf_gate")
    dhf = matmul(duv, w_up[:, F:], "nt", add=dhf, name="ffn_dhf_value")
    dh, d_g = rms_bwd(h, _row(W['norm_ffn'][lyr]), dhf, tm, dout)
    d_wdown = d_wdown.reshape(N_CHIPS, F // N_CHIPS, d_wdown.shape[1])
    return dh, {'norm_ffn': d_g, 'ffn_w_up': d_wup, 'ffn_conv_w': d_cw, 'ffn_conv_b': d_cb, 'ffn_w_down': d_wdown}


def _mla_params(W):
    w_uq = W['mla_w_uq'][0].reshape(MLA_Q_RANK, MLA_HEADS, MLA_QK)
    w_uq = jnp.concatenate([w_uq[..., :MLA_NOPE], _rope_pad(w_uq[..., MLA_NOPE:])], axis=-1)
    w_ukv = W['mla_w_ukv'][0].reshape(MLA_KV_RANK, MLA_HEADS, MLA_NOPE + MLA_V)
    w_ukv = jnp.concatenate([w_ukv[..., :MLA_NOPE].reshape(MLA_KV_RANK, -1), w_ukv[..., MLA_NOPE:].reshape(MLA_KV_RANK, -1)],
                            axis=1)
    return [_row(W['mla_q_a_norm'][0]), w_uq.reshape(MLA_Q_RANK, MLA_HEADS * MLA_DK), _row(W['mla_kv_a_norm'][0]), w_ukv,
            _row(W['mla_qn_nope'][0]), _row(_rope_pad(W['mla_qn_rope'][0])), _row(W['mla_kn_nope'][0]),
            _row(_rope_pad(W['mla_kn_rope'][0]))]


def _w_in_padded(W):
    w = W['mix_w_in'][0]
    return jnp.concatenate([w[:, :IN_WIDTH - MLA_ROPE], _rope_pad(w[:, IN_WIDTH - MLA_ROPE:])], axis=1)


def _mixer0_fwd(h, W, cos_p, sin_p, tm):
    L = h.shape[0]
    t = min(ATTN_ROWS, L // ATTN_WIDE)
    hn = rms_fwd(h, _row(W['norm_mix'][0]), tm, MXU_DTYPE)
    proj = matmul(hn, _w_in_padded(W), name="mix_in")
    logits = W['hg_lb_logits']
    lb = blocked_fwd(_lb_first, [full(logits)], [((1, HG_WIDTH), F32, (1, HG_WIDTH), lambda i: (0, 0))], 1, "hg_lb")[0]
    gain = _row(W['hg_out_norm'][0])
    o_hg, states = hgrn2_fwd(proj, lb, gain)
    mp = _mla_params(W)
    q, k, v = mla_prep_fwd(proj, cos_p, sin_p, mp, tm)
    scale = MLA_QK ** -0.5
    o_mla, lse = attn_fwd(q, k, v, scale, t)
    w_out = W['mix_w_out'][0]
    out = matmul(o_hg, w_out[:HG_WIDTH], add=h, name="mix_out_hg")
    out = matmul(o_mla, w_out[HG_WIDTH:], add=out, name="mix_out_mla")
    return out, (h, hn, proj, lb, o_hg, states, q, k, v, o_mla, lse)


def _mixer0_bwd(dout, saved, W, cos_p, sin_p, tm):
    h, hn, proj, lb, o_hg, states, q, k, v, o_mla, lse = saved
    L = h.shape[0]
    t = min(ATTN_ROWS, L // ATTN_WIDE)
    scale = MLA_QK ** -0.5
    w_out = W['mix_w_out'][0]
    gain = _row(W['hg_out_norm'][0])
    do_hg = matmul(dout, w_out[:HG_WIDTH], "nt", name="mix_do_hg")
    do_mla = matmul(dout, w_out[HG_WIDTH:], "nt", name="mix_do_mla")
    d_wout = jnp.concatenate([matmul(o_hg, dout, "tn", name="mix_dwout_hg"), matmul(o_mla, dout, "tn", name="mix_dwout_mla")],
                             axis=0)
    dq = attn_bwd_dq(q, k, v, o_mla, lse, do_mla, scale, t)
    dk, dv = attn_bwd_dkv(q, k, v, o_mla, lse, do_mla, scale, t)
    mp = _mla_params(W)
    d_mla, d_qa, d_wuq, d_kva, d_wukv, d_qnn, d_qnr, d_knn, d_knr = mla_prep_bwd(proj, cos_p, sin_p, mp, dq, dk, dv, tm)
    d_hg, d_lb, d_gain = hgrn2_bwd(proj, lb, gain, states, do_hg)
    w_in, n_hg = _w_in_padded(W), 4 * HG_WIDTH
    d_win = jnp.concatenate([matmul(hn, d_hg, "tn", name="mix_dwin_hg"), matmul(hn, d_mla, "tn", name="mix_dwin_mla")], axis=1)
    dhn = matmul(d_hg, w_in[:, :n_hg], "nt", name="mix_dhn_hg")
    dhn = matmul(d_mla, w_in[:, n_hg:], "nt", add=dhn, name="mix_dhn_mla")
    dh, d_g = rms_bwd(h, _row(W['norm_mix'][0]), dhn, tm, dout)
    logits = W['hg_lb_logits']
    d_logits = blocked_bwd(_lb_first, [full(logits, 'acc')], [full(d_lb)], 1, "hg_lb_bwd")[0]
    d_wuq = d_wuq.reshape(MLA_Q_RANK, MLA_HEADS, MLA_DK)
    d_wuq = jnp.concatenate([d_wuq[..., :MLA_NOPE], _rope_unpad(d_wuq[..., MLA_NOPE:])], axis=-1)
    hw = MLA_HEADS * MLA_NOPE
    d_wukv = jnp.concatenate([d_wukv[:, :hw].reshape(MLA_KV_RANK, MLA_HEADS, MLA_NOPE),
                              d_wukv[:, hw:].reshape(MLA_KV_RANK, MLA_HEADS, MLA_V)], axis=-1)
    d_win = jnp.concatenate([d_win[:, :IN_WIDTH - MLA_ROPE], _rope_unpad(d_win[:, IN_WIDTH - MLA_ROPE:])], axis=1)
    d_win = d_win.reshape(d_win.shape[0], N_CHIPS, W_IN_SHARD).transpose(1, 0, 2)
    d_win = jnp.pad(d_win, ((0, 0), (0, 0), (0, W_IN_SHARD_PAD - W_IN_SHARD)))
    d_wout = d_wout.reshape(N_CHIPS, d_wout.shape[0] // N_CHIPS, d_wout.shape[1])
    grads = {'norm_mix': d_g, 'hg_lb_logits': d_logits, 'mix_w_in': d_win, 'hg_out_norm': d_gain,
             'mla_q_a_norm': d_qa, 'mla_w_uq': d_wuq.reshape(1, MLA_Q_RANK, -1), 'mla_kv_a_norm': d_kva,
             'mla_w_ukv': d_wukv.reshape(1, MLA_KV_RANK, -1), 'mla_qn_nope': d_qnn, 'mla_qn_rope': _rope_unpad(d_qnr),
             'mla_kn_nope': d_knn, 'mla_kn_rope': _rope_unpad(d_knr), 'mix_w_out': d_wout}
    return dh,grads


def _s5_inputs(W):
    G = W['s5_lam_re'].shape[1]
    return [W['s5_lam_re'][0], W['s5_lam_im'][0], W['s5_log_dt'][0].reshape(G, 1),
            W['s5_b_re'][0].reshape(G, -1), W['s5_b_im'][0].reshape(G, -1)]


def _mixer1_fwd(h, W, tm):
    L, D = h.shape
    u = rms_fwd(h, _row(W['norm_mix'][1]), tm, F32)
    di = _s5_inputs(W)
    G = di[0].shape[0]
    sq, wide = ((G, S5_STATE), F32, (G, S5_STATE), lambda i: (0, 0)), ((G, S5_STATE * S5_GROUP), F32, (G, S5_STATE * S5_GROUP), lambda i: (0, 0))
    ar, ai, bbr, bbi = blocked_fwd(_s5_discretize, [full(a) for a in di], [sq, sq, wide, wide], 1, "s5_discretize")
    nb = G // S5_GB
    core = (_blockdiag_in(bbr.reshape(G, S5_STATE, S5_GROUP)), _blockdiag_in(bbi.reshape(G, S5_STATE, S5_GROUP)),
            ar.reshape(nb, 1, S5_LANES), ai.reshape(nb, 1, S5_LANES),
            _blockdiag_out(W['s5_c_re'][0]), _blockdiag_out(W['s5_c_im'][0]))
    y = s5_fwd(u, *core)
    d = W['s5_d']
    y2 = blocked_fwd(_s5_post, [rows(y, tm), rows(u, tm), full(d)], [_out((L, D), MXU_DTYPE, tm)], L // tm, "s5_post")[0]
    w_ab = jnp.concatenate([W['s5_w_glu_a'][0], W['s5_w_glu_b'][0]], axis=1)
    ab = matmul(y2, w_ab, name="s5_glu_in")
    mix = blocked_fwd(_glu, [rows(ab, tm, col=0, width=D), rows(ab, tm, col=1, width=D)], [_out((L, D), F32, tm)], L // tm,
                      "s5_glu")[0]
    return h + mix, (h, u, core, y, y2, ab)


def _mixer1_bwd(dout, saved, W, tm):
    h, u, core, y, y2, ab = saved
    L, D = h.shape
    da, db = blocked_bwd(_glu, [rows(ab, tm, 'blk', col=0, width=D), rows(ab, tm, 'blk', col=1, width=D)], [rows(dout, tm)],
                         L // tm, "s5_glu_bwd")
    w_a, w_b = W['s5_w_glu_a'][0], W['s5_w_glu_b'][0]
    dy2 = matmul(da, w_a, "nt", name="s5_dy2_a")
    dy2 = matmul(db, w_b, "nt", add=dy2, name="s5_dy2_b")
    d_wa = matmul(y2, da, "tn", name="s5_dwa")
    d_wb = matmul(y2, db, "tn", name="s5_dwb")
    d = W['s5_d']
    dy, du_skip, d_d = blocked_bwd(_s5_post, [rows(y, tm, 'blk'), rows(u, tm, 'blk'), full(d, 'acc')], [rows(dy2, tm)], L // tm,
                                   "s5_post_bwd")
    du, dwr, dwi, dar, dai, dcr, dci = s5_bwd(u, *core, dy, min(256, L))
    di = _s5_inputs(W)
    G = di[0].shape[0]
    cts = [dar.reshape(G, S5_STATE), dai.reshape(G, S5_STATE), _blockdiag_in_t(dwr).reshape(G, -1), _blockdiag_in_t(dwi).reshape(G, -1)]
    d_lr, d_li, d_ldt, d_br, d_bi = blocked_bwd(_s5_discretize, [full(a, 'acc') for a in di], [full(c) for c in cts], 1,
                                                "s5_discretize_bwd")
    dh, d_g = rms_bwd(h, _row(W['norm_mix'][1]), du + du_skip, tm, dout)
    bshape = W['s5_b_re'].shape
    grads = {'norm_mix': d_g, 's5_lam_re': d_lr[None], 's5_lam_im': d_li[None], 's5_log_dt': d_ldt.reshape(1, G),
             's5_b_re': d_br.reshape(bshape), 's5_b_im': d_bi.reshape(bshape), 's5_c_re': _blockdiag_out_t(dcr)[None],
             's5_c_im': _blockdiag_out_t(dci)[None], 's5_d': d_d, 's5_w_glu_a': d_wa.reshape(N_CHIPS, -1, D), 's5_w_glu_b': d_wb.reshape(N_CHIPS, -1, D)}
    return dh,grads


def kernel(x, mem, positions, norm_mix, norm_xa, norm_mem, norm_ffn, xa_wq, xa_wk, xa_wv, xa_wo, xa_q_norm, xa_k_norm, ffn_w_up, ffn_conv_w, ffn_conv_b, ffn_w_down, hg_lb_logits, mix_w_in, hg_out_norm, mla_q_a_norm, mla_w_uq, mla_kv_a_norm, mla_w_ukv, mla_qn_nope, mla_qn_rope, mla_kn_nope, mla_kn_rope, mix_w_out, s5_lam_re, s5_lam_im, s5_log_dt, s5_b_re, s5_b_im, s5_c_re, s5_c_im, s5_d, s5_w_glu_a, s5_w_glu_b, loss_target, m_norm_mix, m_norm_xa, m_norm_mem, m_norm_ffn, m_xa_wq, m_xa_wk, m_xa_wv, m_xa_wo, m_xa_q_norm, m_xa_k_norm, m_ffn_w_up, m_ffn_conv_w, m_ffn_conv_b, m_ffn_w_down, m_hg_lb_logits, m_mix_w_in, m_hg_out_norm, m_mla_q_a_norm, m_mla_w_uq, m_mla_kv_a_norm, m_mla_w_ukv, m_mla_qn_nope, m_mla_qn_rope, m_mla_kn_nope, m_mla_kn_rope, m_mix_w_out, m_s5_lam_re, m_s5_lam_im, m_s5_log_dt, m_s5_b_re, m_s5_b_im, m_s5_c_re, m_s5_c_im, m_s5_d, m_s5_w_glu_a, m_s5_w_glu_b, v_norm_mix, v_norm_xa, v_norm_mem, v_norm_ffn, v_xa_wq, v_xa_wk, v_xa_wv, v_xa_wo, v_xa_q_norm, v_xa_k_norm, v_ffn_w_up, v_ffn_conv_w, v_ffn_conv_b, v_ffn_w_down, v_hg_lb_logits, v_mix_w_in, v_hg_out_norm, v_mla_q_a_norm, v_mla_w_uq, v_mla_kv_a_norm, v_mla_w_ukv, v_mla_qn_nope, v_mla_qn_rope, v_mla_kn_nope, v_mla_kn_rope, v_mix_w_out, v_s5_lam_re, v_s5_lam_im, v_s5_log_dt, v_s5_b_re, v_s5_b_im, v_s5_c_re, v_s5_c_im, v_s5_d, v_s5_w_glu_a, v_s5_w_glu_b):
    P = dict(locals())
    assert sorted(P) == sorted(INPUTS) and norm_mix.shape[0] == 2 and mix_w_in.shape[0] == 1
    x, mem, target = P['x'][0], P['mem'][0], P['loss_target'][0]
    L, D = x.shape
    tm = min(256, L)

    W = {n: P[n] for n in REPLICATED}
    W.update(_gather_weights(P))

    inv_freq = 1.0 / (ROPE_BASE ** (jnp.arange(0, MLA_ROPE, 2, dtype=F32) / MLA_ROPE))
    ang = P['positions'][0].astype(F32)[:, None] * inv_freq
    cos, sin, z = jnp.cos(ang), jnp.sin(ang), jnp.zeros_like(ang)
    cos_p = jnp.concatenate([cos, z, cos, z], axis=1)
    sin_p = jnp.concatenate([-sin, z, sin, z], axis=1)

    h, s_mix0 = _mixer0_fwd(x, W, cos_p, sin_p, tm)
    h, s_xa0 = _xattn_fwd(h, mem, W, 0, tm)
    h, s_ffn0 = _ffn_fwd(h, W, 0, tm)
    h, s_mix1 = _mixer1_fwd(h, W, tm)
    h, s_xa1 = _xattn_fwd(h, mem, W, 1, tm)
    h, s_ffn1 = _ffn_fwd(h, W, 1, tm)
    n = L // tm
    dh, parts = blocked_fwd(_loss_fn, [rows(h, tm), rows(target, tm)],
                            [_out((L, D), F32, tm), ((n * 8, 128), F32, (8, 128), lambda i: (i, 0))], n, "loss")
    loss = lax.psum(jnp.sum(parts), ("x", "y", "c"))

    layered = {}

    def collect(g, lyr):
        for k_, v_ in g.items():
            layered.setdefault(k_, {})[lyr] = v_

    results = {}

    def big_items(lyr):
        return [(n_, 0 if P[n_].shape[0] == 1 else lyr, layered[n_][lyr]) for n_ in BIG if lyr in layered.get(n_, {})]

    dh, g = _ffn_bwd(dh, s_ffn1, W, 1, tm)
    collect(g, 1)
    dh, g = _xattn_bwd(dh, s_xa1, mem, W, 1, tm)
    collect(g, 1)
    dh, g = _mixer1_bwd(dh, s_mix1, W, tm)
    collect(g, 1)
    late = _reduce_batch(big_items(1), None, P, results, "late_layer", {'swap': 2, 'a2a': 3, 'join': 4})
    next(late)
    dh, g = _ffn_bwd(dh, s_ffn0, W, 0, tm)
    collect(g, 0)
    dh = late.send(dh)
    dh, g = _xattn_bwd(dh, s_xa0, mem, W, 0, tm)
    collect(g, 0)
    dx, g = _mixer0_bwd(dh, s_mix0, W, cos_p, sin_p, tm)
    collect(g, 0)

    GS = {}
    for name in SMALL:
        by_layer = [layered[name][lyr] for lyr in sorted(layered[name])]
        full_shape = W[name].shape
        GS[name] = (by_layer[0].reshape(full_shape) if len(by_layer) == 1
                    else jnp.stack([g_.reshape(full_shape[1:]) for g_ in by_layer]))
    small = _pack([GS[n_] for n_ in SMALL], 2 * N_CHIPS * ROW_MULT).reshape(N_CHIPS, -1, PACK_W)
    first = _reduce_batch(big_items(0), small, P, results, "first_layer", {'swap': 5, 'a2a': 6, 'join': 7})
    next(first)
    dx = late.send(dx)
    last_pair_sum = first.send(None)
    late.send(last_pair_sum)
    late_name = big_items(1)[-1][0]
    results[late_name] = list(first.send(list(results[late_name])))
    first.send(None)
    outs = list(_update_small(results['small_quarter'], GS, P))
    for k_ in range(4):
        outs[k_].update({n_: results[n_][k_].reshape(P[n_].shape) for n_ in BIG})
    return (loss, dx[None], *[d[n_] for d in outs for n_ in WEIGHTS])
```

```python
import functools
import math

import jax
import jax.numpy as jnp
import numpy as np
from jax import lax
from jax.experimental import pallas as pl
from jax.experimental.pallas import tpu as pltpu
from jax.experimental.pallas import tpu_sc as plsc

F32 = jnp.float32
BF16 = jnp.bfloat16
MXU_DTYPE = BF16
HI = lax.Precision.HIGHEST
V7X_VMEM_LIMIT_BYTES = 56 * 1024 * 1024
EPS = 1e-6
MESH = pl.DeviceIdType.MESH

HG_HEADS, HG_DIM = 4, 128
HG_WIDTH = HG_HEADS * HG_DIM
HG_SUB = 32
HG_BLOCK = 64
MLA_HEADS, MLA_Q_RANK, MLA_KV_RANK = 4, 256, 128
MLA_NOPE, MLA_ROPE, MLA_V = 128, 64, 128
MLA_QK = MLA_NOPE + MLA_ROPE
MLA_DK = 256
ROPE_BASE = 10000.0
IN_WIDTH = 4 * HG_WIDTH + MLA_Q_RANK + MLA_KV_RANK + MLA_ROPE
IN_PAD = 4 * HG_WIDTH + MLA_Q_RANK + MLA_KV_RANK + 128
S5_GROUP, S5_STATE = 16, 64
S5_GB = 8
DT_MIN, DT_MAX = 1e-3, 1e-1
XA_HEADS = 4
CONV_W = 3
ADAM_LR, ADAM_B1, ADAM_B2, ADAM_EPS, ADAM_WD, ADAM_STEP = 0.001, 0.9, 0.999, 1e-08, 0.01, 10


def _cparams(sem):
    return pltpu.CompilerParams(dimension_semantics=sem, vmem_limit_bytes=V7X_VMEM_LIMIT_BYTES)


class Opd:
    def __init__(self, arr, block, imap, grad=None, gshape=None, gimap=None):
        self.arr, self.block, self.imap, self.grad = arr, block, imap, grad
        self.gshape = arr.shape if gshape is None else gshape
        self.gimap = imap if gimap is None else gimap

    def spec(self):
        return pl.BlockSpec(self.block, self.imap)

    def gspec(self):
        return pl.BlockSpec(self.block, self.gimap)


def rows(arr, tm, grad=None, col=0, width=None):
    width = arr.shape[1] if width is None else width
    return Opd(arr, (tm, width), lambda i, c=col: (i, c), grad, (arr.shape[0], width), lambda i: (i, 0))


def cols(arr, tn, grad=None):
    return Opd(arr, (arr.shape[0], tn), lambda j: (0, j), grad)


def full(arr, grad=None):
    return Opd(arr, arr.shape, lambda i: (0, 0), grad)


def _load(ref):
    v = ref[...]
    return v.astype(F32) if jnp.issubdtype(v.dtype, jnp.floating) else v


def blocked_fwd(f, opds, outs, n, name):
    n_in = len(opds)

    def body(*refs):
        ys = f(*[_load(r) for r in refs[:n_in]])
        for r, y in zip(refs[n_in:], ys):
            r[...] = y.astype(r.dtype)

    res = pl.pallas_call(
        body, name=name, grid=(n,),
        in_specs=[o.spec() for o in opds],
        out_specs=[pl.BlockSpec(b, m) for (_, _, b, m) in outs],
        out_shape=[jax.ShapeDtypeStruct(s, d) for (s, d, _, _) in outs],
        compiler_params=_cparams(("parallel",)),
    )(*[o.arr for o in opds])
    return res


def blocked_bwd(f, opds, dys, n, name, plus=None):
    n_in, n_dy = len(opds), len(dys)
    diff = [i for i, o in enumerate(opds) if o.grad]
    extra = [] if plus is None else [plus]

    def body(*refs):
        vals = [_load(r) for r in refs[:n_in]]

        def fd(*dv):
            allv = list(vals)
            for i, v in zip(diff, dv):
                allv[i] = v
            return tuple(f(*allv))

        ys, vjp = jax.vjp(fd, *[vals[i] for i in diff])
        cts = tuple(_load(r).astype(y.dtype) for r, y in zip(refs[n_in:n_in + n_dy], ys))
        gs = list(vjp(cts))
        if extra:
            gs[0] = gs[0] + _load(refs[n_in + n_dy])
        for r, g, i in zip(refs[n_in + n_dy + len(extra):], gs, diff):
            if opds[i].grad == 'acc':
                @pl.when(pl.program_id(0) == 0)
                def _(r=r):
                    r[...] = jnp.zeros(r.shape, r.dtype)
                r[...] += g.astype(r.dtype)
            else:
                r[...] = g.astype(r.dtype)

    any_acc = any(opds[i].grad == 'acc' for i in diff)
    res = pl.pallas_call(
        body, name=name, grid=(n,),
        in_specs=[o.spec() for o in opds + dys + extra],
        out_specs=[opds[i].gspec() for i in diff],
        out_shape=[jax.ShapeDtypeStruct(opds[i].gshape, F32) for i in diff],
        compiler_params=_cparams(("arbitrary" if any_acc else "parallel",)),
    )(*[o.arr for o in opds + dys + extra])
    return res


def _tile(dim, want):
    for t in range(want - want % 16, 0, -16):
        if dim % t == 0:
            return t
    assert dim <= want, (dim, want)
    return dim


MATMUL_VMEM_BUDGET = 40 * 1024 * 1024
MATMUL_ROWS = 512


def _widest(N, fits):
    for t in range(N - N % 128, 0, -128):
        if N % t == 0 and fits(t):
            return t
    return N


def matmul(a, b, mode="nn", out_dtype=F32, add=None, name="matmul", into=None, col_blocks=None):
    sa, sb, so = a.dtype.itemsize, b.dtype.itemsize, jnp.dtype(out_dtype).itemsize
    has_add = add is not None
    if mode == "tn":
        (K, M), (K2, N) = a.shape, b.shape
        assert K == K2 and not has_add and out_dtype == F32, (a.shape, b.shape)
        tk = _tile(K, MATMUL_ROWS)
        tn = _widest(N, lambda t: 2 * (tk * M * sa + tk * t * sb + M * t * 4) <= MATMUL_VMEM_BUDGET)
        extra, alias = [], {}
        if into is not None:
            buf, lead = into[0], tuple(into[1:])
            if col_blocks is not None:
                assert N % col_blocks == 0 and (N // col_blocks) % 128 == 0 and tn >= N // col_blocks, (N, col_blocks, tn)
                tn = N // col_blocks
                assert buf.shape[len(lead):] == (M, tn), (buf.shape, lead, M, tn)
                out_spec = pl.BlockSpec((None,) * len(lead) + (M, tn), lambda j, k: lead[:-1] + (lead[-1] + j, 0, 0))
            else:
                assert buf.shape[len(lead):] == (M, N), (buf.shape, lead, M, N)
                out_spec = pl.BlockSpec((None,) * len(lead) + (M, tn), lambda j, k: lead + (0, j))
            out_shape = jax.ShapeDtypeStruct(buf.shape, F32)
            extra, alias = [buf], {2: 0}
        else:
            assert col_blocks is None
            out_spec = pl.BlockSpec((M, tn), lambda j, k: (0, j))
            out_shape = jax.ShapeDtypeStruct((M, N), F32)

        def body(a_ref, b_ref, *rest):
            o_ref = rest[-1]
            r = lax.dot_general(a_ref[...].astype(MXU_DTYPE), b_ref[...].astype(MXU_DTYPE), ((_TN), ((), ())),
                                preferred_element_type=F32)

            @pl.when(pl.program_id(1) == 0)
            def _():
                o_ref[...] = r

            @pl.when(pl.program_id(1) > 0)
            def _():
                o_ref[...] += r

        return pl.pallas_call(
            body, name=name, grid=(N // tn, K // tk),
            in_specs=[pl.BlockSpec((tk, M), lambda j, k: (k, 0)), pl.BlockSpec((tk, tn), lambda j, k: (k, j))]
            + [pl.BlockSpec(memory_space=pl.ANY)] * len(extra),
            out_specs=out_spec, out_shape=out_shape, input_output_aliases=alias,
            compiler_params=_cparams(("parallel", "arbitrary")),
        )(a, b, *extra)

    (M, K) = a.shape
    N = b.shape[1] if mode == "nn" else b.shape[0]
    assert K == (b.shape[0] if mode == "nn" else b.shape[1]), (a.shape, b.shape, mode)
    tm = _tile(M, MATMUL_ROWS)
    tn = _widest(N, lambda t: 2 * (tm * K * sa + K * t * sb + tm * t * (so + 4 * has_add)) <= MATMUL_VMEM_BUDGET)
    dims = ((_NN if mode == "nn" else _NT), ((), ()))

    def body(*refs):
        r = lax.dot_general(refs[0][...].astype(MXU_DTYPE), refs[1][...].astype(MXU_DTYPE), dims, preferred_element_type=F32)
        if has_add:
            r = r + refs[2][...].astype(F32)
        refs[-1][...] = r.astype(refs[-1].dtype)

    b_spec = pl.BlockSpec((K, tn), lambda j, i: (0, j)) if mode == "nn" else pl.BlockSpec((tn, K), lambda j, i: (j, 0))
    in_specs = [pl.BlockSpec((tm, K), lambda j, i: (i, 0)), b_spec]
    args = [a, b]
    if has_add:
        in_specs.append(pl.BlockSpec((tm, tn), lambda j, i: (i, j)))
        args.append(add)
    return pl.pallas_call(
        body, name=name, grid=(N // tn, M // tm),
        in_specs=in_specs,
        out_specs=pl.BlockSpec((tm, tn), lambda j, i: (i, j)),
        out_shape=jax.ShapeDtypeStruct((M, N), out_dtype),
        compiler_params=_cparams(("parallel", "parallel")),
    )(*args)


def _dot(a, b, dims, precision=None):
    if precision is None:
        a, b = a.astype(MXU_DTYPE), b.astype(MXU_DTYPE)
    return lax.dot_general(a, b, (dims, ((), ())), precision=precision, preferred_element_type=F32)


_NN = ((1,), (0,))
_NT = ((1,), (1,))
_TN = ((0,), (0,))


def _rms(x, gain):
    return x * lax.rsqrt(jnp.mean(x * x, axis=-1, keepdims=True) + EPS) * gain


def _hg_block(st_t, q, fl, iv, g, lb, gain):
    row = lax.broadcasted_iota(jnp.int32, (HG_SUB, HG_SUB), 0)
    col = lax.broadcasted_iota(jnp.int32, (HG_SUB, HG_SUB), 1)
    tri = (row >= col).astype(F32)
    outs, states = [], []
    for h in range(HG_HEADS):
        sl = slice(h * HG_DIM, (h + 1) * HG_DIM)
        st = st_t[h * HG_DIM:(h + 1) * HG_DIM, :]
        lbh = lb[:, sl]
        fg = lbh + (1.0 - lbh) * jax.nn.sigmoid(fl[:, sl])
        lf, kk, qf, v = jnp.log(fg), 1.0 - fg, jax.nn.silu(q[:, sl]), iv[:, sl]
        parts = []
        for s in range(q.shape[0] // HG_SUB):
            r = slice(s * HG_SUB, (s + 1) * HG_SUB)
            b = _dot(tri, lf[r], _NN, HI)
            b_mid = jnp.sum(lf[r][:HG_SUB // 2], axis=0, keepdims=True)
            b_end = jnp.sum(lf[r], axis=0, keepdims=True)
            sc = _dot(qf[r] * jnp.exp(b - b_mid), kk[r] * jnp.exp(b_mid - b), _NT) * tri
            parts.append(_dot(sc, v[r], _NN) + _dot(qf[r] * jnp.exp(b), st, _NT))
            st = st * jnp.exp(b_end) + _dot(v[r], kk[r] * jnp.exp(b_end - b), _TN)
        o = jnp.concatenate(parts, axis=0)
        outs.append(_rms(o, gain[:, sl]) * jax.nn.silu(g[:, sl]))
        states.append(st)
    return jnp.concatenate(states, axis=0), jnp.concatenate(outs, axis=1)


def _hg_specs(proj, nb):
    return [pl.BlockSpec((HG_BLOCK, HG_WIDTH), lambda i, c=c, f=nb: (f(i), c)) for c in range(4)]


def hgrn2_fwd(proj, lb, gain):
    L = proj.shape[0]
    n = L // HG_BLOCK

    def body(q, fl, iv, g, lb_r, gain_r, o_ref, st_ref, st):
        @pl.when(pl.program_id(0) == 0)
        def _():
            st[...] = jnp.zeros(st.shape, F32)

        st_ref[0] = st[...]
        new, o = _hg_block(st[...], q[...], fl[...], iv[...], g[...], lb_r[...], gain_r[...])
        st[...] = new
        o_ref[...] = o.astype(o_ref.dtype)

    pspec = pl.BlockSpec((1, HG_WIDTH), lambda i: (0, 0))
    return pl.pallas_call(
        body, name="hgrn2_fwd", grid=(n,),
        in_specs=_hg_specs(proj, lambda i: i) + [pspec, pspec],
        out_specs=[pl.BlockSpec((HG_BLOCK, HG_WIDTH), lambda i: (i, 0)),
                   pl.BlockSpec((1, HG_WIDTH, HG_DIM), lambda i: (i, 0, 0))],
        out_shape=[jax.ShapeDtypeStruct((L, HG_WIDTH), MXU_DTYPE),
                   jax.ShapeDtypeStruct((n, HG_WIDTH, HG_DIM), F32)],
        scratch_shapes=[pltpu.VMEM((HG_WIDTH, HG_DIM), F32)],
        compiler_params=_cparams(("arbitrary",)),
    )(proj, proj, proj, proj, lb, gain)


def hgrn2_bwd(proj, lb, gain, states, do):
    L = proj.shape[0]
    n = L // HG_BLOCK

    def body(q, fl, iv, g, lb_r, gain_r, st_r, do_r, dproj, dlb, dgain, dst):
        @pl.when(pl.program_id(0) == 0)
        def _():
            dst[...] = jnp.zeros(dst.shape, F32)
            dlb[...] = jnp.zeros(dlb.shape, F32)
            dgain[...] = jnp.zeros(dgain.shape, F32)

        _, vjp = jax.vjp(_hg_block, st_r[0], q[...], fl[...], iv[...], g[...], lb_r[...], gain_r[...])
        d_st, dq, dfl, div, dg, d_lb, d_gain = vjp((dst[...], do_r[...].astype(F32)))
        dst[...] = d_st
        dproj[:, 0 * HG_WIDTH:1 * HG_WIDTH] = dq
        dproj[:, 1 * HG_WIDTH:2 * HG_WIDTH] = dfl
        dproj[:, 2 * HG_WIDTH:3 * HG_WIDTH] = div
        dproj[:, 3 * HG_WIDTH:4 * HG_WIDTH] = dg
        dlb[...] += d_lb
        dgain[...] += d_gain

    rev = lambda i: n - 1 - i
    pspec = pl.BlockSpec((1, HG_WIDTH), lambda i: (0, 0))
    return pl.pallas_call(
        body, name="hgrn2_bwd", grid=(n,),
        in_specs=_hg_specs(proj, rev) + [pspec, pspec,
                                         pl.BlockSpec((1, HG_WIDTH, HG_DIM), lambda i: (rev(i), 0, 0)),
                                         pl.BlockSpec((HG_BLOCK, HG_WIDTH), lambda i: (rev(i), 0))],
        out_specs=[pl.BlockSpec((HG_BLOCK, 4 * HG_WIDTH), lambda i: (rev(i), 0)), pspec, pspec],
        out_shape=[jax.ShapeDtypeStruct((L, 4 * HG_WIDTH), F32),
                   jax.ShapeDtypeStruct((1, HG_WIDTH), F32), jax.ShapeDtypeStruct((1, HG_WIDTH), F32)],
        scratch_shapes=[pltpu.VMEM((HG_WIDTH, HG_DIM), F32)],
        compiler_params=_cparams(("arbitrary",)),
    )(proj, proj, proj, proj, lb, gain, states, do)


def _rope_rms(x, gain_p, cos_p, sin_p):
    n = x * lax.rsqrt(jnp.sum(x * x, axis=-1, keepdims=True) * (1.0 / MLA_ROPE) + EPS) * gain_p
    r = lax.broadcasted_iota(jnp.int32, (128, 128), 0)
    c = lax.broadcasted_iota(jnp.int32, (128, 128), 1)
    swap = (r == (c + 64) % 128).astype(F32)
    return n * cos_p + _dot(n, swap, _NN, HI) * sin_p


MLA_IN = MLA_Q_RANK + MLA_KV_RANK + 128


def _mla_prep(x, cos_p, sin_p, q_a, w_uq, kv_a, w_ukv, qn_nope, qn_rope, kn_nope, kn_rope):
    c_q, c_kv, kpe = x[:, :MLA_Q_RANK], x[:, MLA_Q_RANK:MLA_Q_RANK + MLA_KV_RANK], x[:, MLA_Q_RANK + MLA_KV_RANK:]
    q = _dot(_rms(c_q, q_a), w_uq, _NN)
    kv = _dot(_rms(c_kv, kv_a), w_ukv, _NN)
    k_pe = _rope_rms(kpe, kn_rope, cos_p, sin_p)
    qs, ks = [], []
    for h in range(MLA_HEADS):
        qs.append(_rms(q[:, h * MLA_DK:h * MLA_DK + MLA_NOPE], qn_nope))
        qs.append(_rope_rms(q[:, h * MLA_DK + MLA_NOPE:(h + 1) * MLA_DK], qn_rope, cos_p, sin_p))
        ks.append(_rms(kv[:, h * MLA_NOPE:(h + 1) * MLA_NOPE], kn_nope))
        ks.append(k_pe)
    return jnp.concatenate(qs, axis=1), jnp.concatenate(ks, axis=1), kv[:, MLA_HEADS * MLA_NOPE:]


def _mla_prep_opds(proj, cos_p, sin_p, params, tm, grads):
    g = (lambda k: k) if grads else (lambda k: None)
    assert (4 * HG_WIDTH) % MLA_IN == 0
    return ([rows(proj, tm, g('blk'), col=4 * HG_WIDTH // MLA_IN, width=MLA_IN), rows(cos_p, tm), rows(sin_p, tm)]
            + [full(p, g('acc')) for p in params])


def mla_prep_fwd(proj, cos_p, sin_p, params, tm):
    L = proj.shape[0]
    W = MLA_HEADS * MLA_DK
    rb = lambda w: (tm, w)
    outs = [((L, W), MXU_DTYPE, rb(W), lambda i: (i, 0)), ((L, W), MXU_DTYPE, rb(W), lambda i: (i, 0)),
            ((L, MLA_HEADS * MLA_V), MXU_DTYPE, rb(MLA_HEADS * MLA_V), lambda i: (i, 0))]
    return blocked_fwd(_mla_prep, _mla_prep_opds(proj, cos_p, sin_p, params, tm, False), outs, L // tm, "mla_prep_fwd")


def mla_prep_bwd(proj, cos_p, sin_p, params, dq, dk, dv, tm):
    L = proj.shape[0]
    return blocked_bwd(_mla_prep, _mla_prep_opds(proj, cos_p, sin_p, params, tm, True),
                       [rows(dq, tm), rows(dk, tm), rows(dv, tm)], L // tm, "mla_prep_bwd")


def _scores(q, k, scale, shift=None):
    s = _dot(q, k, _NT) * scale
    if shift is None:
        return s
    row = lax.broadcasted_iota(jnp.int32, s.shape, 0)
    col = lax.broadcasted_iota(jnp.int32, s.shape, 1)
    return jnp.where(col <= row + shift, s, -jnp.inf)


ATTN_ROWS = 512
ATTN_WIDE = 2


def attn_fwd(q, k, v, scale, t):
    L = q.shape[0]
    tq = ATTN_WIDE * t

    def body(q_ref, k_ref, v_ref, o_ref, lse_ref):
        i = pl.program_id(1)
        qb = q_ref[...]

        def step(j, carry, shift=None):
            m, l, acc = carry
            kj = k_ref[pl.ds(pl.multiple_of(j * t, t), t), :]
            vj = v_ref[pl.ds(pl.multiple_of(j * t, t), t), :]
            s = _scores(qb, kj, scale, shift)
            m_new = jnp.maximum(m, jnp.max(s, axis=-1, keepdims=True))
            p = jnp.exp(s - m_new)
            alpha = jnp.exp(m - m_new)
            return m_new, alpha * l + jnp.sum(p, axis=-1, keepdims=True), alpha * acc + _dot(p, vj, _NN)

        carry = (jnp.full((tq, 1), -jnp.inf, F32), jnp.zeros((tq, 1), F32), jnp.zeros((tq, MLA_V), F32))
        carry = lax.fori_loop(0, ATTN_WIDE * i, step, carry)
        for d in range(ATTN_WIDE):
            carry = step(ATTN_WIDE * i + d, carry, -d * t)
        m, l, acc = carry
        o_ref[...] = acc / l
        lse_ref[...] = jnp.broadcast_to(m + jnp.log(l), lse_ref.shape)

    hspec = lambda rows_, w: pl.BlockSpec((rows_, w), lambda h, i: (0, h))
    bspec = lambda w: pl.BlockSpec((tq, w), lambda h, i: (i, h))
    return pl.pallas_call(
        body, name="attn_fwd", grid=(MLA_HEADS, L // tq),
        in_specs=[bspec(MLA_DK), hspec(L, MLA_DK), hspec(L, MLA_V)],
        out_specs=[bspec(MLA_V), bspec(MLA_V)],
        out_shape=[jax.ShapeDtypeStruct((L, MLA_HEADS * MLA_V), F32)] * 2,
        compiler_params=_cparams(("parallel", "parallel")),
    )(q, k, v)


def attn_bwd_dq(q, k, v, o, lse, do, scale, t):
    L = q.shape[0]
    tq = ATTN_WIDE * t

    def body(q_ref, k_ref, v_ref, o_ref, lse_ref, do_ref, dq_ref):
        i = pl.program_id(1)
        qb, dob = q_ref[...], do_ref[...]
        delta = jnp.sum(dob * o_ref[...], axis=-1, keepdims=True)
        lse_c = jnp.max(lse_ref[...], axis=-1, keepdims=True)

        def step(j, dq, shift=None):
            kj = k_ref[pl.ds(pl.multiple_of(j * t, t), t), :]
            vj = v_ref[pl.ds(pl.multiple_of(j * t, t), t), :]
            p = jnp.exp(_scores(qb, kj, scale, shift) - lse_c)
            ds = p * (_dot(dob, vj, _NT) - delta) * scale
            return dq + _dot(ds, kj, _NN)

        dq = lax.fori_loop(0, ATTN_WIDE * i, step, jnp.zeros((tq, MLA_DK), F32))
        for d in range(ATTN_WIDE):
            dq = step(ATTN_WIDE * i + d, dq, -d * t)
        dq_ref[...] = dq

    hspec = lambda w: pl.BlockSpec((L, w), lambda h, i: (0, h))
    bspec = lambda w: pl.BlockSpec((tq, w), lambda h, i: (i, h))
    return pl.pallas_call(
        body, name="attn_bwd_dq", grid=(MLA_HEADS, L // tq),
        in_specs=[bspec(MLA_DK), hspec(MLA_DK), hspec(MLA_V), bspec(MLA_V), bspec(MLA_V), bspec(MLA_V)],
        out_specs=bspec(MLA_DK),
        out_shape=jax.ShapeDtypeStruct((L, MLA_HEADS * MLA_DK), F32),
        compiler_params=_cparams(("parallel", "parallel")),
    )(q, k, v, o, lse, do)


def attn_bwd_dkv(q, k, v, o, lse, do, scale, t):
    L = q.shape[0]
    tk = ATTN_WIDE * t

    def body(q_ref, k_ref, v_ref, o_ref, lse_ref, do_ref, dk_ref, dv_ref):
        j = pl.program_id(1)
        kb, vb = k_ref[...], v_ref[...]

        def step(i, carry, shift=None):
            dk, dv = carry
            r = pl.ds(pl.multiple_of(i * t, t), t)
            qi, doi = q_ref[r, :], do_ref[r, :]
            delta = jnp.sum(doi * o_ref[r, :], axis=-1, keepdims=True)
            lse_c = jnp.max(lse_ref[r, :], axis=-1, keepdims=True)
            p = jnp.exp(_scores(qi, kb, scale, shift) - lse_c)
            ds = p * (_dot(doi, vb, _NT) - delta) * scale
            return dk + _dot(ds, qi, _TN), dv + _dot(p, doi, _TN)

        carry = (jnp.zeros((tk, MLA_DK), F32), jnp.zeros((tk, MLA_V), F32))
        for d in range(ATTN_WIDE):
            carry = step(ATTN_WIDE * j + d, carry, d * t)
        dk, dv = lax.fori_loop(ATTN_WIDE * (j + 1), L // t, step, carry)
        dk_ref[...] = dk
        dv_ref[...] = dv

    hspec = lambda w: pl.BlockSpec((L, w), lambda h, j: (0, h))
    bspec = lambda w: pl.BlockSpec((tk, w), lambda h, j: (j, h))
    return pl.pallas_call(
        body, name="attn_bwd_dkv", grid=(MLA_HEADS, L // tk),
        in_specs=[hspec(MLA_DK), bspec(MLA_DK), bspec(MLA_V), hspec(MLA_V), hspec(MLA_V), hspec(MLA_V)],
        out_specs=[bspec(MLA_DK), bspec(MLA_V)],
        out_shape=[jax.ShapeDtypeStruct((L, MLA_HEADS * MLA_DK), F32), jax.ShapeDtypeStruct((L, MLA_HEADS * MLA_V), F32)],
        compiler_params=_cparams(("parallel", "parallel")),
    )(q, k, v, o, lse, do)


S5_LANES = S5_GB * S5_STATE


def _cmul(ar, ai, br, bi):
    return ar * br - ai * bi, ar * bi + ai * br


def _a_powers(ar, ai, reverse):
    a2 = _cmul(ar, ai, ar, ai)
    a4 = _cmul(*a2, *a2)
    row = lax.broadcasted_iota(jnp.int32, (8, ar.shape[1]), 0)
    e = (8 - row) if reverse else (row + 1)
    tr, ti = jnp.ones((8, ar.shape[1]), F32), jnp.zeros((8, ar.shape[1]), F32)
    for bit, (pr, pi) in ((1, (ar, ai)), (2, a2), (4, a4), (8, _cmul(*a4, *a4))):
        nr, ni = _cmul(tr, ti, pr, pi)
        sel = (e & bit) != 0
        tr, ti = jnp.where(sel, nr, tr), jnp.where(sel, ni, ti)
    pows = []
    for d, (pr, pi) in zip((1, 2, 4), ((ar, ai), a2, a4)):
        keep = (row < 8 - d) if reverse else (row >= d)
        pows.append((jnp.where(keep, pr, 0.0), jnp.where(keep, pi, 0.0)))
    return pows, (tr, ti)


def _scan8(xr, xi, pows, table, cr, ci, reverse):
    for d, (pr, pi) in zip((1, 2, 4), pows):
        shift = 8 - d if reverse else d
        mr, mi = _cmul(pr, pi, pltpu.roll(xr, shift, 0), pltpu.roll(xi, shift, 0))
        xr, xi = xr + mr, xi + mi
    mr, mi = _cmul(table[0], table[1], cr, ci)
    return xr + mr, xi + mi


def _row_of(x, r):
    row = lax.broadcasted_iota(jnp.int32, x.shape, 0)
    return jnp.sum(jnp.where(row == r, x, 0.0), axis=0, keepdims=True)


def _s5_scan_fwd(h_re, h_im, ar, ai, L):
    pows, table = _a_powers(ar, ai, False)

    def step(i, carry):
        r = pl.ds(pl.multiple_of(i * 8, 8), 8)
        xr, xi = _scan8(h_re[r, :], h_im[r, :], pows, table, carry[0], carry[1], False)
        h_re[r, :] = xr
        h_im[r, :] = xi
        return xr[7:8, :], xi[7:8, :]

    z = jnp.zeros((1, ar.shape[1]), F32)
    lax.fori_loop(0, L // 8, step, (z, z))


def _s5_specs(L):
    return [pl.BlockSpec((L, 128), lambda g: (0, g)),
            pl.BlockSpec((1, 128, S5_LANES), lambda g: (g, 0, 0)), pl.BlockSpec((1, 128, S5_LANES), lambda g: (g, 0, 0)),
            pl.BlockSpec((1, 1, S5_LANES), lambda g: (g, 0, 0)), pl.BlockSpec((1, 1, S5_LANES), lambda g: (g, 0, 0)),
            pl.BlockSpec((1, S5_LANES, 128), lambda g: (g, 0, 0)), pl.BlockSpec((1, S5_LANES, 128), lambda g: (g, 0, 0))]


def s5_fwd(u, w_re, w_im, a_re, a_im, c_re, c_im):
    L, D = u.shape

    def body(u_ref, wr, wi, ar, ai, cr, ci, y_ref, h_re, h_im):
        ub = u_ref[...]
        h_re[...] = _dot(ub, wr[0], _NN)
        h_im[...] = _dot(ub, wi[0], _NN)
        _s5_scan_fwd(h_re, h_im, ar[0], ai[0], L)
        y_ref[...] = _dot(h_re[...], cr[0], _NN) - _dot(h_im[...], ci[0], _NN)

    return pl.pallas_call(
        body, name="s5_fwd", grid=(D // 128,),
        in_specs=_s5_specs(L), out_specs=pl.BlockSpec((L, 128), lambda g: (0, g)),
        out_shape=jax.ShapeDtypeStruct((L, D), F32),
        scratch_shapes=[pltpu.VMEM((L, S5_LANES), F32), pltpu.VMEM((L, S5_LANES), F32)],
        compiler_params=_cparams(("parallel",)),
    )(u, w_re, w_im, a_re, a_im, c_re, c_im)


def s5_bwd(u, w_re, w_im, a_re, a_im, c_re, c_im, dy, tc):
    L, D = u.shape
    nch = L // tc

    def body(u_ref, wr, wi, ar_ref, ai_ref, cr, ci, dy_ref, du_ref, dwr, dwi, dar, dai, dcr, dci, h_re, h_im, g_re, g_im):
        ar, ai = ar_ref[0], ai_ref[0]
        ub = u_ref[...]
        h_re[...] = _dot(ub, wr[0], _NN)
        h_im[...] = _dot(ub, wi[0], _NN)
        _s5_scan_fwd(h_re, h_im, ar, ai, L)
        dyb = dy_ref[...]
        dcr[0] = _dot(h_re[...], dyb, _TN)
        dci[0] = -_dot(h_im[...], dyb, _TN)
        pows, table = _a_powers(ar, -ai, True)
        dwr[0] = jnp.zeros((128, S5_LANES), F32)
        dwi[0] = jnp.zeros((128, S5_LANES), F32)
        z1 = jnp.zeros((1, S5_LANES), F32)
        z8 = jnp.zeros((8, S5_LANES), F32)

        def chunk(cc, carry):
            c0 = pl.multiple_of((nch - 1 - cc) * tc, tc)
            rows_c = pl.ds(c0, tc)
            dyc = dy_ref[rows_c, :]
            g_re[...] = _dot(dyc, cr[0], _NT)
            g_im[...] = -_dot(dyc, ci[0], _NT)

            def step(ii, cy):
                gr_c, gi_c, acc_r, acc_i = cy
                i8 = pl.multiple_of((tc // 8 - 1 - ii) * 8, 8)
                rl = pl.ds(i8, 8)
                xr, xi = _scan8(g_re[rl, :], g_im[rl, :], pows, table, gr_c, gi_c, True)
                g_re[rl, :] = xr
                g_im[rl, :] = xi
                t0 = c0 + i8
                hb_r, hb_i = h_re[pl.ds(t0, 8), :], h_im[pl.ds(t0, 8), :]
                tp = pl.multiple_of(jnp.maximum(t0 - 8, 0), 8)
                first = (t0 > 0).astype(F32)
                pr = h_re[pl.ds(tp, 8), :][7:8, :] * first
                pi = h_im[pl.ds(tp, 8), :][7:8, :] * first
                row = lax.broadcasted_iota(jnp.int32, xr.shape, 0)
                hp_r = jnp.where(row == 0, pr, pltpu.roll(hb_r, 1, 0))
                hp_i = jnp.where(row == 0, pi, pltpu.roll(hb_i, 1, 0))
                return (xr[0:1, :], xi[0:1, :],
                        acc_r + xr * hp_r + xi * hp_i, acc_i + xi * hp_r - xr * hp_i)

            cy = lax.fori_loop(0, tc // 8, step, carry)
            uc = u_ref[rows_c, :]
            gr, gi = g_re[...], g_im[...]
            du_ref[rows_c, :] = _dot(gr, wr[0], _NT) + _dot(gi, wi[0], _NT)
            dwr[0] += _dot(uc, gr, _TN)
            dwi[0] += _dot(uc, gi, _TN)
            return cy

        _, _, acc_r, acc_i = lax.fori_loop(0, nch, chunk, (z1, z1, z8, z8))
        dar[0] = jnp.sum(acc_r, axis=0, keepdims=True)
        dai[0] = jnp.sum(acc_i, axis=0, keepdims=True)

    specs = _s5_specs(L)
    return pl.pallas_call(
        body, name="s5_bwd", grid=(D // 128,),
        in_specs=specs + [pl.BlockSpec((L, 128), lambda g: (0, g))],
        out_specs=[pl.BlockSpec((L, 128), lambda g: (0, g))] + specs[1:],
        out_shape=[jax.ShapeDtypeStruct((L, D), F32)] + [jax.ShapeDtypeStruct(x.shape, F32)
                                                        for x in (w_re, w_im, a_re, a_im, c_re, c_im)],
        scratch_shapes=[pltpu.VMEM((L, S5_LANES), F32), pltpu.VMEM((L, S5_LANES), F32),
                        pltpu.VMEM((tc, S5_LANES), F32), pltpu.VMEM((tc, S5_LANES), F32)],
        compiler_params=_cparams(("parallel",)),
    )(u, w_re, w_im, a_re, a_im, c_re, c_im, dy)


def _s5_discretize(lr, li, ldt, br, bi):
    dt = jnp.exp(ldt)
    mag = jnp.exp(lr * dt)
    ar, ai = mag * jnp.cos(li * dt), mag * jnp.sin(li * dt)
    den = lr * lr + li * li
    zr = ((ar - 1.0) * lr + ai * li) / den
    zi = (ai * lr - (ar - 1.0) * li) / den
    p = lax.broadcasted_iota(jnp.int32, (S5_STATE, S5_STATE * S5_GROUP), 0)
    c = lax.broadcasted_iota(jnp.int32, (S5_STATE, S5_STATE * S5_GROUP), 1)
    rep = (c // S5_GROUP == p).astype(F32)
    zr, zi = _dot(zr, rep, _NN, HI), _dot(zi, rep, _NN, HI)
    return ar, ai, zr * br - zi * bi, zr * bi + zi * br


def _conv_shift(x, d):
    row = lax.broadcasted_iota(jnp.int32, x.shape, 0)
    return jnp.where(row >= d, pltpu.roll(x, d, 0), 0.0)


def _conv_unshift(x, d):
    n = x.shape[0]
    row = lax.broadcasted_iota(jnp.int32, x.shape, 0)
    return jnp.where(row < n - d, pltpu.roll(x, n - d, 0), 0.0)


@functools.partial(jax.custom_vjp, nondiff_argnums=(1,))
def _shift_rows(x, d):
    return _conv_shift(x, d)


_shift_rows.defvjp(lambda x, d: (_conv_shift(x, d), None), lambda d, _, g: (_conv_unshift(g, d),))


def _conv_gate(ug, uv, wg0, wg1, wg2, wv0, wv1, wv2, bg, bv):
    def conv(u, w0, w1, w2, b):
        return u * w2 + _shift_rows(u, 1) * w1 + _shift_rows(u, 2) * w0 + b
    return (jax.nn.silu(conv(ug, wg0, wg1, wg2, bg)) * conv(uv, wv0, wv1, wv2, bv),)


def _rms_fn(x, gain):
    return (_rms(x, gain),)


def _softmax_rows(s):
    e = jnp.exp(s - lax.stop_gradient(jnp.max(s, axis=-1, keepdims=True)))
    return e / jnp.sum(e, axis=-1, keepdims=True)


def _xa_core(qp, k, v, q_gain):
    dh = qp.shape[1] // XA_HEADS
    outs = []
    for h in range(XA_HEADS):
        sl = slice(h * dh, (h + 1) * dh)
        p = _softmax_rows(_dot(_rms(qp[:, sl], q_gain), k[:, sl], _NT) * (dh ** -0.5))
        outs.append(_dot(p, v[:, sl], _NN))
    return (jnp.concatenate(outs, axis=1),)


def _mem_kv(mem, mem_gain, wk, wv, k_gain):
    m = _rms(mem, mem_gain)
    kp = _dot(m, wk, _NN)
    dh = kp.shape[1] // XA_HEADS
    k = jnp.concatenate([_rms(kp[:, h * dh:(h + 1) * dh], k_gain) for h in range(XA_HEADS)], axis=1)
    return k, _dot(m, wv, _NN)


def _s5_post(y, u, d):
    return (jax.nn.gelu(y + d * u),)


def _glu(a, b):
    return (a * jax.nn.sigmoid(b),)


def _lb_first(logits):
    e = jnp.exp(logits - lax.stop_gradient(jnp.max(logits, axis=0, keepdims=True)))
    return (_row_of(e, 0) / jnp.sum(e, axis=0, keepdims=True),)


def _loss_fn(y, t):
    e = y - t
    part = 0.5 * jnp.sum(e * e) / y.shape[1]
    return e * (1.0 / y.shape[1]), jnp.full((8, 128), part / (8 * 128), F32)


def _out(shape, dtype, tm):
    return (shape, dtype, (tm, shape[1]), lambda i: (i, 0))


def rms_fwd(h, gain, tm, dtype):
    return blocked_fwd(_rms_fn, [rows(h, tm), full(gain)], [_out(h.shape, dtype, tm)], h.shape[0] // tm, "rms_fwd")[0]


def rms_bwd(h, gain, dy, tm, residual):
    return blocked_bwd(_rms_fn, [rows(h, tm, 'blk'), full(gain, 'acc')], [rows(dy, tm)], h.shape[0] // tm, "rms_bwd",
                       plus=rows(residual, tm))


def _adamw_math(w, g, m, v):
    m = ADAM_B1 * m + (1.0 - ADAM_B1) * g
    v = ADAM_B2 * v + (1.0 - ADAM_B2) * jnp.square(g)
    m_hat = m / (1.0 - ADAM_B1 ** ADAM_STEP)
    v_hat = v / (1.0 - ADAM_B2 ** ADAM_STEP)
    return -ADAM_LR * (m_hat / (jnp.sqrt(v_hat) + ADAM_EPS) + ADAM_WD * w), m, v


def adamw(w, g, m, v, name):
    R = w.shape[0]
    tm = _tile(R, 256)
    assert g.shape == w.shape == m.shape == v.shape, (name, w.shape, g.shape)

    def body(w_ref, g_ref, m_ref, v_ref, d_ref, nm_ref, nv_ref):
        d_ref[...], nm_ref[...], nv_ref[...] = _adamw_math(w_ref[...], g_ref[...], m_ref[...], v_ref[...])

    spec = pl.BlockSpec((tm, w.shape[1]), lambda i: (i, 0))
    return pl.pallas_call(
        body, name=name, grid=(R // tm,), in_specs=[spec] * 4, out_specs=[spec] * 3,
        out_shape=[jax.ShapeDtypeStruct(w.shape, F32)] * 3, compiler_params=_cparams(("parallel",)),
    )(w, g, m, v)


def _index_operand(i):
    return jnp.reshape(i, (1,)).astype(jnp.int32)


def adamw_layer(w, mine, other, core, m, v, layer, bufs, name):
    H, C = mine.shape
    tm = _tile(H, 256)
    nb = H // tm
    assert w.shape[1:] == (2 * H, C) and all(b.shape == w.shape for b in bufs), (name, w.shape, mine.shape)

    def body(c_ref, w_ref, mine_ref, other_ref, m_ref, v_ref, *rest):
        g_out, d_ref, nm_ref, nv_ref = rest[-4:]
        g_ = jnp.where(pl.program_id(0) // nb == c_ref[0], mine_ref[...], other_ref[...])
        g_out[...] = g_
        d_ref[...], nm_ref[...], nv_ref[...] = _adamw_math(w_ref[...], g_, m_ref[...], v_ref[...])

    lspec = pl.BlockSpec((None, tm, C), lambda i, c: (layer, i, 0))
    half = lambda sign: pl.BlockSpec(
        (tm, C), lambda i, c: (jnp.clip(i - (c[0] if sign else 1 - c[0]) * nb, 0, nb - 1), 0))
    any_spec = pl.BlockSpec(memory_space=pl.ANY)
    grid_spec = pltpu.PrefetchScalarGridSpec(
        num_scalar_prefetch=1, grid=(2 * nb,),
        in_specs=[lspec, half(True), half(False), lspec, lspec] + [any_spec] * 4, out_specs=[lspec] * 4)
    return pl.pallas_call(
        body, name=name, grid_spec=grid_spec, out_shape=[jax.ShapeDtypeStruct(w.shape, F32)] * 4,
        input_output_aliases={6: 0, 7: 1, 8: 2, 9: 3}, compiler_params=_cparams(("parallel",)),
    )(_index_operand(core), w, mine, other, m, v, *bufs)


def add_own_half(x, core, theirs, name, out_dtype):
    nq, _, H, C = x.shape
    tm = _tile(H, 256)

    def body(c_ref, x_ref, t_ref, o_ref):
        o_ref[...] = (x_ref[...] + t_ref[...]).astype(o_ref.dtype)

    spec = pl.BlockSpec((None, tm, C), lambda q, i, c: (q, i, 0))
    grid_spec = pltpu.PrefetchScalarGridSpec(
        num_scalar_prefetch=1, grid=(nq, H // tm),
        in_specs=[pl.BlockSpec((None, None, tm, C), lambda q, i, c: (q, c[0], i, 0)), spec], out_specs=spec)
    return pl.pallas_call(
        body, name=name, grid_spec=grid_spec, out_shape=jax.ShapeDtypeStruct((nq, H, C), out_dtype),
        compiler_params=_cparams(("parallel", "parallel")),
    )(_index_operand(core), x, theirs)


def add_chips(pair, chip, got, name):
    n, R, C = got.shape
    tm = _tile(R, 256)

    def body(q_ref, *refs):
        acc = refs[0][...].astype(F32)
        for r in refs[1:-1]:
            acc = acc + r[...].astype(F32)
        refs[-1][...] = acc

    grid_spec = pltpu.PrefetchScalarGridSpec(
        num_scalar_prefetch=1, grid=(R // tm,),
        in_specs=[pl.BlockSpec((None, tm, C), lambda i, q: (q[0], i, 0))]
        + [pl.BlockSpec((None, tm, C), lambda i, q, j=j: (j, i, 0)) for j in range(n)],
        out_specs=pl.BlockSpec((tm, C), lambda i, q: (i, 0)))
    return pl.pallas_call(
        body, name=name, grid_spec=grid_spec, out_shape=jax.ShapeDtypeStruct((R, C), F32),
        compiler_params=_cparams(("parallel",)),
    )(_index_operand(chip), pair, *([got] * n))


_HBM = pl.BlockSpec(memory_space=pltpu.HBM)
N_CHIPS = 4


def _my_place():
    return lax.axis_index("x"), lax.axis_index("y"), lax.axis_index("c")


def _window(ref, axis, start, size):
    idx = [slice(None)] * len(ref.shape)
    idx[axis] = pl.ds(start, size)
    return ref.at[tuple(idx)]


def _comm_call(body, name, xs, out_shapes, n_remote, n_local, sequencer=None):
    sems = [pltpu.SemaphoreType.DMA((n_remote,)), pltpu.SemaphoreType.DMA((n_remote,)),
            pltpu.SemaphoreType.DMA((max(n_local, 1),))]
    if sequencer is None:
        return pl.pallas_call(
            body, name=name, in_specs=[_HBM] * len(xs), out_specs=[_HBM] * len(out_shapes), out_shape=out_shapes,
            scratch_shapes=sems, compiler_params=pltpu.CompilerParams(has_side_effects=True),
        )(*xs)
    peers_of, collective_id = sequencer
    hbm = pltpu.MemorySpace.HBM
    x_refs = [jax.new_ref(x, memory_space=hbm) for x in xs]
    o_refs = [jax.empty_ref(s, memory_space=hbm) for s in out_shapes]

    @pl.kernel(mesh=plsc.ScalarSubcoreMesh(axis_name="sequencer", num_cores=1), name=name, scratch_types=tuple(sems),
               compiler_params=pltpu.CompilerParams(collective_id=collective_id))
    def launch(send_sems, recv_sems, local_sems):
        peers = peers_of(*_my_place())
        barrier = pltpu.get_barrier_semaphore()
        for peer in peers:
            pl.semaphore_signal(barrier, inc=1, device_id=peer, device_id_type=MESH)
        pl.semaphore_wait(barrier, len(peers))
        body(*x_refs, *o_refs, send_sems, recv_sems, local_sems)

    launch()
    return [o[...] for o in o_refs]


def _sibling(mx, my, mc):
    return [(mx, my, 1 - mc)]


def _same_core_of_other_chips(mx, my, mc):
    return [(tx, ty, mc) for tx, ty in _other_chips(mx, my)]


def _run(copies):
    for cp in copies:
        cp.start()
    for cp in copies:
        cp.wait()


def _other_chips(mx, my):
    return [(mx ^ (j >> 1), my ^ (j & 1)) for j in (1, 2, 3)]


def chip_gather(xs, axes, name):
    n = len(xs)
    shapes, final = [], []
    for x, ax in zip(xs, axes):
        s = list(x.shape)
        if ax is None:
            shapes.append([N_CHIPS] + s)
            final.append(shapes[-1])
        elif ax < x.ndim - 1:
            shapes.append(s[:ax] + [N_CHIPS] + s[ax:])
            final.append(s[:ax] + [N_CHIPS * s[ax]] + s[ax + 1:])
        else:
            assert s[ax] % 128 == 0, (name, s)
            shapes.append(s[:ax] + [N_CHIPS * s[ax]])
            final.append(shapes[-1])

    def body(*refs):
        x_refs, o_refs = refs[:n], refs[n:2 * n]
        send_sems, recv_sems, local_sems = refs[2 * n:]
        mx, my, mc = _my_place()
        q = 2 * mx + my
        copies = []
        for i, (x_ref, o_ref, ax) in enumerate(zip(x_refs, o_refs, axes)):
            if ax is None or ax < len(x_ref.shape) - 1:
                dst = o_ref.at[(slice(None),) * (ax or 0) + (q,)]
            else:
                dst = _window(o_ref, ax, q * x_ref.shape[ax], x_ref.shape[ax])
            copies.append(pltpu.make_async_copy(x_ref, dst, local_sems.at[i]))
            for j, (tx, ty) in enumerate(_other_chips(mx, my)):
                copies.append(pltpu.make_async_remote_copy(
                    src_ref=x_ref, dst_ref=dst, send_sem=send_sems.at[3 * i + j], recv_sem=recv_sems.at[3 * i + j],
                    device_id=(tx, ty, mc), device_id_type=MESH))
        _run(copies)

    out_shapes = [jax.ShapeDtypeStruct(tuple(s), x.dtype) for s, x in zip(shapes, xs)]
    return [o.reshape(f) for o, f in zip(_comm_call(body, name, xs, out_shapes, 3 * n, n), final)]


def gather_two_level(xs, name):
    n = len(xs)
    shapes = [jax.ShapeDtypeStruct((2, N_CHIPS) + x.shape[1:], x.dtype) for x in xs]

    def body(*refs):
        x_refs, o_refs = refs[:n], refs[n:2 * n]
        send_sems, recv_sems, local_sems = refs[2 * n:]
        mx, my, mc = _my_place()
        q = 2 * mx + my
        first, local, second = [], [], []
        for i, (x_ref, o_ref) in enumerate(zip(x_refs, o_refs)):
            local.append(pltpu.make_async_copy(x_ref.at[mc], o_ref.at[mc, q], local_sems.at[i]))
            for j, (tx, ty) in enumerate(_other_chips(mx, my)):
                first.append(pltpu.make_async_remote_copy(
                    src_ref=x_ref.at[mc], dst_ref=o_ref.at[mc, q], send_sem=send_sems.at[4 * i + j],
                    recv_sem=recv_sems.at[4 * i + j], device_id=(tx, ty, mc), device_id_type=MESH))
            second.append(pltpu.make_async_remote_copy(
                src_ref=o_ref.at[mc], dst_ref=o_ref.at[mc], send_sem=send_sems.at[4 * i + 3],
                recv_sem=recv_sems.at[4 * i + 3], device_id=(mx, my, 1 - mc), device_id_type=MESH))
        for cp in local + first:
            cp.start()
        for cp in local:
            cp.wait()
        for cp in first:
            cp.wait_recv()
        _run(second)
        for cp in first:
            cp.wait_send()

    return _comm_call(body, name, xs, shapes, 4 * n, n)


def gather_two_level_sequencer(xs, name, collective_id):
    n = len(xs)
    hbm = pltpu.MemorySpace.HBM
    x_refs = [jax.new_ref(x, memory_space=hbm) for x in xs]
    o_refs = [jax.empty_ref(jax.ShapeDtypeStruct((2, N_CHIPS) + x.shape[1:], x.dtype), memory_space=hbm) for x in xs]

    @pl.kernel(mesh=plsc.ScalarSubcoreMesh(axis_name="sequencer", num_cores=1), name=name,
               scratch_types=(pltpu.SemaphoreType.DMA((4 * n,)), pltpu.SemaphoreType.DMA((4 * n,)),
                              pltpu.SemaphoreType.DMA((n,))),
               compiler_params=pltpu.CompilerParams(collective_id=collective_id))
    def launch(send_sems, recv_sems, local_sems):
        mx, my, mc = _my_place()
        peers = [(tx, ty, mc) for tx, ty in _other_chips(mx, my)] + [(mx, my, 1 - mc)]
        barrier = pltpu.get_barrier_semaphore()
        for peer in peers:
            pl.semaphore_signal(barrier, inc=1, device_id=peer, device_id_type=MESH)
        pl.semaphore_wait(barrier, len(peers))
        q = 2 * mx + my
        first, local, second = [], [], []
        for i, (x_ref, o_ref) in enumerate(zip(x_refs, o_refs)):
            local.append(pltpu.make_async_copy(x_ref.at[mc], o_ref.at[mc, q], local_sems.at[i]))
            for j, peer in enumerate(peers[:3]):
                first.append(pltpu.make_async_remote_copy(
                    src_ref=x_ref.at[mc], dst_ref=o_ref.at[mc, q], send_sem=send_sems.at[4 * i + j],
                    recv_sem=recv_sems.at[4 * i + j], device_id=peer, device_id_type=MESH))
            second.append(pltpu.make_async_remote_copy(
                src_ref=o_ref.at[mc], dst_ref=o_ref.at[mc], send_sem=send_sems.at[4 * i + 3],
                recv_sem=recv_sems.at[4 * i + 3], device_id=peers[3], device_id_type=MESH))
        for cp in local + first:
            cp.start()
        for cp in local:
            cp.wait()
        for cp in first:
            cp.wait_recv()
        _run(second)
        for cp in first:
            cp.wait_send()

    launch()
    return [o[...] for o in o_refs]


def pair_swap(xs, name, halves, collective_id=None):
    n = len(xs)
    shapes = [jax.ShapeDtypeStruct(x.shape[:1] + x.shape[2:] if halves else x.shape, x.dtype) for x in xs]

    def body(*refs):
        x_refs, o_refs = refs[:n], refs[n:2 * n]
        send_sems, recv_sems, _ = refs[2 * n:]
        mx, my, mc = _my_place()
        _run([pltpu.make_async_remote_copy(
            src_ref=x_ref.at[:, 1 - mc] if halves else x_ref, dst_ref=o_ref, send_sem=send_sems.at[i],
            recv_sem=recv_sems.at[i], device_id=(mx, my, 1 - mc), device_id_type=MESH)
            for i, (x_ref, o_ref) in enumerate(zip(x_refs, o_refs))])

    return _comm_call(body, name, xs, shapes, n, 0, None if collective_id is None else (_sibling, collective_id))


def chip_all_to_all(xs, name, collective_id=None):
    n = len(xs)
    shapes = [jax.ShapeDtypeStruct((N_CHIPS - 1,) + x.shape[1:], x.dtype) for x in xs]

    def body(*refs):
        x_refs, o_refs = refs[:n], refs[n:2 * n]
        send_sems, recv_sems, _ = refs[2 * n:]
        mx, my, mc = _my_place()
        copies = []
        for i, (x_ref, o_ref) in enumerate(zip(x_refs, o_refs)):
            for j, (tx, ty) in enumerate(_other_chips(mx, my)):
                copies.append(pltpu.make_async_remote_copy(
                    src_ref=x_ref.at[2 * tx + ty], dst_ref=o_ref.at[j], send_sem=send_sems.at[3 * i + j],
                    recv_sem=recv_sems.at[3 * i + j], device_id=(tx, ty, mc), device_id_type=MESH))
        _run(copies)

    return _comm_call(body, name, xs, shapes, 3 * n, 0,
                      None if collective_id is None else (_same_core_of_other_chips, collective_id))


WEIGHTS = ['norm_mix', 'norm_xa', 'norm_mem', 'norm_ffn', 'xa_wq', 'xa_wk', 'xa_wv', 'xa_wo', 'xa_q_norm', 'xa_k_norm',
           'ffn_w_up', 'ffn_conv_w', 'ffn_conv_b', 'ffn_w_down', 'hg_lb_logits', 'mix_w_in', 'hg_out_norm',
           'mla_q_a_norm', 'mla_w_uq', 'mla_kv_a_norm', 'mla_w_ukv', 'mla_qn_nope', 'mla_qn_rope', 'mla_kn_nope',
           'mla_kn_rope', 'mix_w_out', 's5_lam_re', 's5_lam_im', 's5_log_dt', 's5_b_re', 's5_b_im', 's5_c_re',
           's5_c_im', 's5_d', 's5_w_glu_a', 's5_w_glu_b']
INPUTS = ['x', 'mem', 'positions'] + WEIGHTS + ['loss_target'] + ['m_' + n for n in WEIGHTS] + ['v_' + n for n in WEIGHTS]
SHARD_AXIS = {'xa_wq': 1, 'xa_wk': 1, 'xa_wv': 1, 'xa_wo': 1, 'ffn_w_up': 2, 'ffn_conv_w': 2, 'ffn_w_down': 1,
              'mix_w_in': 2, 'mla_w_uq': 2, 'mla_w_ukv': 2, 'mix_w_out': 1, 's5_d': 1, 's5_w_glu_a': 1, 's5_w_glu_b': 1}
BIG = ['xa_wq', 'xa_wk', 'xa_wv', 'xa_wo', 'ffn_w_up', 'ffn_w_down', 'mix_w_in', 'mix_w_out', 's5_w_glu_a', 's5_w_glu_b']
FIRST_NEEDED = ('mix_w_in', 'mix_w_out')
SMALL_SHARDED = [n for n in WEIGHTS if n in SHARD_AXIS and n not in BIG]
REPLICATED = [n for n in WEIGHTS if n not in SHARD_AXIS]
SMALL = SMALL_SHARDED + REPLICATED
PACK_W = 1024
ROW_MULT = 16
W_IN_SHARD = IN_WIDTH // N_CHIPS
W_IN_SHARD_PAD = 640


def _pack(flats, mult=ROW_MULT):
    flat = jnp.concatenate([f.reshape(-1) for f in flats])
    unit = mult * PACK_W
    n = -(-flat.shape[0] // unit) * unit
    return jnp.pad(flat, (0, n - flat.shape[0])).reshape(n // PACK_W, PACK_W)


def _unpack(packed, shapes):
    flat, out, o = packed.reshape(-1), [], 0
    for s in shapes:
        n = math.prod(s)
        out.append(flat[o:o + n].reshape(s))
        o += n
    return out


def _rope_pad(w):
    z = jnp.zeros(w.shape[:-1] + (MLA_ROPE // 2,), w.dtype)
    return jnp.concatenate([w[..., :MLA_ROPE // 2], z, w[..., MLA_ROPE // 2:], z], axis=-1)


def _rope_unpad(g):
    return jnp.concatenate([g[..., :MLA_ROPE // 2], g[..., 64:64 + MLA_ROPE // 2]], axis=-1)


def _blockdiag_in(bb):
    nb = bb.shape[0] // S5_GB
    t = bb.reshape(nb, S5_GB, S5_STATE, S5_GROUP).transpose(0, 1, 3, 2)
    return jnp.einsum('bgmp,gh->bgmhp', t, jnp.eye(S5_GB, dtype=bb.dtype)).reshape(nb, S5_GB * S5_GROUP, S5_LANES)


def _blockdiag_in_t(dw):
    nb = dw.shape[0]
    t = jnp.einsum('bgmhp,gh->bgmp', dw.reshape(nb, S5_GB, S5_GROUP, S5_GB, S5_STATE), jnp.eye(S5_GB, dtype=dw.dtype))
    return t.transpose(0, 1, 3, 2).reshape(nb * S5_GB, S5_STATE, S5_GROUP)


def _blockdiag_out(c):
    nb = c.shape[0] // S5_GB
    t = c.reshape(nb, S5_GB, S5_GROUP, S5_STATE).transpose(0, 1, 3, 2)
    return jnp.einsum('bgpm,gh->bgphm', t, jnp.eye(S5_GB, dtype=c.dtype)).reshape(nb, S5_LANES, S5_GB * S5_GROUP)


def _blockdiag_out_t(dc):
    nb = dc.shape[0]
    t = jnp.einsum('bgphm,gh->bgpm', dc.reshape(nb, S5_GB, S5_STATE, S5_GB, S5_GROUP), jnp.eye(S5_GB, dtype=dc.dtype))
    return t.transpose(0, 1, 3, 2).reshape(nb * S5_GB, S5_GROUP, S5_STATE)


def _gather_weights(P):
    def halves(x):
        return x if x.shape[0] == 2 else x.reshape(2, x.shape[1] // 2, x.shape[2])

    now = [n for n in BIG if n in FIRST_NEEDED]
    later = [n for n in BIG if n not in FIRST_NEEDED]
    xs = [halves(P[n].astype(BF16)) for n in now] + [halves(_pack([P[n] for n in SMALL_SHARDED], 2 * ROW_MULT)[None])]
    got = gather_two_level(xs, "gather_weights")
    got, xs_later = lax.optimization_barrier((got, [halves(P[n].astype(BF16)) for n in later]))
    got_later = gather_two_level_sequencer(xs_later, "gather_weights_later", 1)
    full_w = {}
    for n, g in list(zip(now, got[:-1])) + list(zip(later, got_later)):
        two_layers, by_rows = P[n].shape[0] == 2, SHARD_AXIS[n] == 1
        if two_layers and by_rows:
            full_w[n] = g.reshape(2, N_CHIPS * g.shape[2], g.shape[3])
        elif two_layers:
            full_w[n] = g.transpose(0, 2, 1, 3).reshape(2, g.shape[2], N_CHIPS * g.shape[3])
        elif by_rows:
            full_w[n] = g.transpose(1, 0, 2, 3).reshape(1, 2 * N_CHIPS * g.shape[2], g.shape[3])
        else:
            full_w[n] = g.transpose(0, 2, 1, 3).reshape(1, 2 * g.shape[2], N_CHIPS * g.shape[3])
    small = got[-1].transpose(1, 0, 2, 3).reshape(N_CHIPS, -1, PACK_W)
    per_chip = [_unpack(small[q], [P[n].shape for n in SMALL_SHARDED]) for q in range(N_CHIPS)]
    for i, n in enumerate(SMALL_SHARDED):
        full_w[n] = jnp.concatenate([per_chip[q][i] for q in range(N_CHIPS)], axis=SHARD_AXIS[n])
    return full_w


def _halves_first(x):
    return x.reshape(x.shape[0], 2, x.shape[1] // 2, x.shape[2]).transpose(1, 0, 2, 3)


def _reduce_batch(items, small, P, results, tag, ids):
    mx, my, mc = _my_place()
    q = 2 * mx + my
    ids = ids or {}
    names = [f"{n}_{lyr}" for n, lyr, _ in items] + (['small'] if small is not None else [])
    xs = [g.reshape(N_CHIPS, 2, g.shape[1] // 2, g.shape[2]) for g in [g for _, _, g in items] + ([small] if small is not None else [])]
    def after(vals, tie):
        return (vals, None) if tie is None else lax.optimization_barrier((vals, tie))

    theirs = pair_swap(xs, "grads_pair_swap_" + tag, True, ids.get('swap'))
    theirs, tie = after(theirs, (yield None))
    pair = [add_own_half(x, mc, t, "grads_pair_sum_" + n, F32 if n == 'small' else BF16) for x, t, n in zip(xs, theirs, names)]
    got = chip_all_to_all(pair, "grads_chip_all_to_all_" + tag, ids.get('a2a'))
    got, tie = after(got, (yield pair[-1] if tie is None else tie))
    summed = [add_chips(p, q, g, "grads_chip_sum_" + n) for p, g, n in zip(pair, got, names)]
    other = pair_swap(summed, "grads_pair_join_" + tag, False, ids.get('join'))
    other, tie = after(other, (yield summed[-1] if tie is None else tie))
    for (n, lyr, _), s, o in zip(items, summed, other):
        if n == 'mix_w_in':
            s, o = s[:, :W_IN_SHARD], o[:, :W_IN_SHARD]
        view = (P[n].shape[0], 2 * s.shape[0], P[n].shape[-1])
        bufs = results.get(n) or [lax.empty(view, F32) for _ in range(4)]
        results[n] = adamw_layer(P[n].reshape(view), s, o, mc, P['m_' + n].reshape(view), P['v_' + n].reshape(view), lyr,
                                 bufs, f"adamw_{n}_{lyr}")
    if small is not None:
        results['small_quarter'] = lax.cond(mc == 0, lambda a, b: jnp.concatenate([a, b], axis=0),
                                            lambda a, b: jnp.concatenate([b, a], axis=0), summed[-1], other[-1])
    yield tie


def _update_small(small_quarter, GS, P):
    mx, my, _ = _my_place()
    q = 2 * mx + my
    small_sum = chip_gather([small_quarter], [0], "grads_small_gather")[0]
    g_small = dict(zip(SMALL, _unpack(small_sum, [GS[n].shape for n in SMALL])))
    for n in SMALL_SHARDED:
        s = P[n].shape[SHARD_AXIS[n]]
        g_small[n] = lax.dynamic_slice_in_dim(g_small[n], q * s, s, axis=SHARD_AXIS[n])
    grad, delta, new_m, new_v = {}, {}, {}, {}
    packed = lambda prefix: _pack([P[prefix + n] for n in SMALL])
    d, m_, v_ = adamw(packed(''), _pack([g_small[n] for n in SMALL]), packed('m_'), packed('v_'), "adamw_small")
    shapes = [P[n].shape for n in SMALL]
    grad.update(g_small)
    for out, pk in ((delta, d), (new_m, m_), (new_v, v_)):
        out.update(zip(SMALL, _unpack(pk, shapes)))
    return grad, delta, new_m, new_v


def _row(v):
    return v.reshape(1, -1)


def _xattn_fwd(h, mem, W, lyr, tm):
    g_xa, g_mem = _row(W['norm_xa'][lyr]), _row(W['norm_mem'][lyr])
    g_q, g_k = _row(W['xa_q_norm'][lyr]), _row(W['xa_k_norm'][lyr])
    wq, wk, wv, wo = (W[n][lyr] for n in ('xa_wq', 'xa_wk', 'xa_wv', 'xa_wo'))
    L, D = h.shape
    M = mem.shape[0]
    hx = rms_fwd(h, g_xa, tm, MXU_DTYPE)
    qp = matmul(hx, wq, name="xa_q")
    kv_opds = [full(mem), full(g_mem), full(wk), full(wv), full(g_k)]
    k, v = blocked_fwd(_mem_kv, kv_opds, [((M, D), F32, (M, D), lambda i: (0, 0))] * 2, 1, "xa_mem_kv")
    o = blocked_fwd(_xa_core, [rows(qp, tm), full(k), full(v), full(g_q)], [_out((L, D), MXU_DTYPE, tm)], L // tm,
                    "xa_core")[0]
    out = matmul(o, wo, add=h, name="xa_o")
    return out, (h, hx, qp, k, v, o)


def _xattn_bwd(dout, saved, mem, W, lyr, tm):
    h, hx, qp, k, v, o = saved
    g_xa, g_mem = _row(W['norm_xa'][lyr]), _row(W['norm_mem'][lyr])
    g_q, g_k = _row(W['xa_q_norm'][lyr]), _row(W['xa_k_norm'][lyr])
    wq, wk, wv, wo = (W[n][lyr] for n in ('xa_wq', 'xa_wk', 'xa_wv', 'xa_wo'))
    L = h.shape[0]
    do = matmul(dout, wo, "nt", name="xa_do")
    d_wo = matmul(o, dout, "tn", name="xa_dwo")
    dqp, dk, dv, d_gq = blocked_bwd(_xa_core, [rows(qp, tm, 'blk'), full(k, 'acc'), full(v, 'acc'), full(g_q, 'acc')],
                                    [rows(do, tm)], L // tm, "xa_core_bwd")
    d_wq = matmul(hx, dqp, "tn", name="xa_dwq")
    dhx = matmul(dqp, wq, "nt", name="xa_dhx")
    dh, d_gxa = rms_bwd(h, g_xa, dhx, tm, dout)
    d_gmem, d_wk, d_wv, d_gk = blocked_bwd(
        _mem_kv, [full(mem), full(g_mem, 'acc'), full(wk, 'acc'), full(wv, 'acc'), full(g_k, 'acc')],
        [full(dk), full(dv)], 1, "xa_mem_kv_bwd")
    by_chip = lambda g: g.reshape(N_CHIPS, g.shape[0] // N_CHIPS, g.shape[1])
    grads = {'norm_xa': d_gxa, 'norm_mem': d_gmem, 'xa_q_norm': d_gq, 'xa_k_norm': d_gk,
             'xa_wq': by_chip(d_wq), 'xa_wk': by_chip(d_wk), 'xa_wv': by_chip(d_wv), 'xa_wo': by_chip(d_wo)}
    return dh, grads


def _conv_params(W, lyr):
    cw, cb = W['ffn_conv_w'][lyr], W['ffn_conv_b'][lyr]
    F = cw.shape[1] // 2
    return [cw[0:1, :F], cw[1:2, :F], cw[2:3, :F], cw[0:1, F:], cw[1:2, F:], cw[2:3, F:], _row(cb[:F]), _row(cb[F:])]


def _ffn_fwd(h, W, lyr, tm):
    L, D = h.shape
    w_up, w_down = W['ffn_w_up'][lyr], W['ffn_w_down'][lyr]
    F = w_down.shape[0]
    hf = rms_fwd(h, _row(W['norm_ffn'][lyr]), tm, MXU_DTYPE)
    ug = matmul(hf, w_up[:, :F], name="ffn_up_gate")
    uv = matmul(hf, w_up[:, F:], name="ffn_up_value")
    opds = [cols(ug, 128), cols(uv, 128)] + [cols(p, 128) for p in _conv_params(W, lyr)]
    a = blocked_fwd(_conv_gate, opds, [((L, F), MXU_DTYPE, (L, 128), lambda j: (0, j))], F // 128, "ffn_conv_gate")[0]
    out = matmul(a, w_down, add=h, name="ffn_down")
    return out, (h, hf, ug, uv, a)


def _ffn_bwd(dout, saved, W, lyr, tm):
    h, hf, ug, uv, a = saved
    w_up, w_down = W['ffn_w_up'][lyr], W['ffn_w_down'][lyr]
    F = w_down.shape[0]
    da = matmul(dout, w_down, "nt", name="ffn_da")
    d_wdown = matmul(a, dout, "tn", name="ffn_dwdown")
    opds = [cols(ug, 128, 'blk'), cols(uv, 128, 'blk')] + [cols(p, 128, 'blk') for p in _conv_params(W, lyr)]
    gs = blocked_bwd(_conv_gate, opds, [cols(da, 128)], F // 128, "ffn_conv_gate_bwd")
    dug, duv = gs[0], gs[1]
    d_cw = jnp.concatenate([jnp.concatenate(gs[2:5], axis=0), jnp.concatenate(gs[5:8], axis=0)], axis=1)
    d_cb = jnp.concatenate([gs[8], gs[9]], axis=1)[0]
    half = N_CHIPS // 2
    d_wup = lax.empty((N_CHIPS, hf.shape[1], w_up.shape[1] // N_CHIPS), F32)
    d_wup = matmul(hf, dug, "tn", name="ffn_dwup_gate", into=(d_wup, 0), col_blocks=half)
    d_wup = matmul(hf, duv, "tn", name="ffn_dwup_value", into=(d_wup, half), col_blocks=half)
    dhf = matmul(dug, w_up[:, :F], "nt", name="ffn_dhf_gate")
    dhf = matmul(duv, w_up[:, F:], "nt", add=dhf, name="ffn_dhf_value")
    dh, d_g = rms_bwd(h, _row(W['norm_ffn'][lyr]), dhf, tm, dout)
    d_wdown = d_wdown.reshape(N_CHIPS, F // N_CHIPS, d_wdown.shape[1])
    return dh, {'norm_ffn': d_g, 'ffn_w_up': d_wup, 'ffn_conv_w': d_cw, 'ffn_conv_b': d_cb, 'ffn_w_down': d_wdown}


def _mla_params(W):
    w_uq = W['mla_w_uq'][0].reshape(MLA_Q_RANK, MLA_HEADS, MLA_QK)
    w_uq = jnp.concatenate([w_uq[..., :MLA_NOPE], _rope_pad(w_uq[..., MLA_NOPE:])], axis=-1)
    w_ukv = W['mla_w_ukv'][0].reshape(MLA_KV_RANK, MLA_HEADS, MLA_NOPE + MLA_V)
    w_ukv = jnp.concatenate([w_ukv[..., :MLA_NOPE].reshape(MLA_KV_RANK, -1), w_ukv[..., MLA_NOPE:].reshape(MLA_KV_RANK, -1)],
                            axis=1)
    return [_row(W['mla_q_a_norm'][0]), w_uq.reshape(MLA_Q_RANK, MLA_HEADS * MLA_DK), _row(W['mla_kv_a_norm'][0]), w_ukv,
            _row(W['mla_qn_nope'][0]), _row(_rope_pad(W['mla_qn_rope'][0])), _row(W['mla_kn_nope'][0]),
            _row(_rope_pad(W['mla_kn_rope'][0]))]


def _w_in_padded(W):
    w = W['mix_w_in'][0]
    return jnp.concatenate([w[:, :IN_WIDTH - MLA_ROPE], _rope_pad(w[:, IN_WIDTH - MLA_ROPE:])], axis=1)


def _mixer0_fwd(h, W, cos_p, sin_p, tm):
    L = h.shape[0]
    t = min(ATTN_ROWS, L // ATTN_WIDE)
    hn = rms_fwd(h, _row(W['norm_mix'][0]), tm, MXU_DTYPE)
    proj = matmul(hn, _w_in_padded(W), name="mix_in")
    logits = W['hg_lb_logits']
    lb = blocked_fwd(_lb_first, [full(logits)], [((1, HG_WIDTH), F32, (1, HG_WIDTH), lambda i: (0, 0))], 1, "hg_lb")[0]
    gain = _row(W['hg_out_norm'][0])
    o_hg, states = hgrn2_fwd(proj, lb, gain)
    mp = _mla_params(W)
    q, k, v = mla_prep_fwd(proj, cos_p, sin_p, mp, tm)
    scale = MLA_QK ** -0.5
    o_mla, lse = attn_fwd(q, k, v, scale, t)
    w_out = W['mix_w_out'][0]
    out = matmul(o_hg, w_out[:HG_WIDTH], add=h, name="mix_out_hg")
    out = matmul(o_mla, w_out[HG_WIDTH:], add=out, name="mix_out_mla")
    return out, (h, hn, proj, lb, o_hg, states, q, k, v, o_mla, lse)


def _mixer0_bwd(dout, saved, W, cos_p, sin_p, tm):
    h, hn, proj, lb, o_hg, states, q, k, v, o_mla, lse = saved
    L = h.shape[0]
    t = min(ATTN_ROWS, L // ATTN_WIDE)
    scale = MLA_QK ** -0.5
    w_out = W['mix_w_out'][0]
    gain = _row(W['hg_out_norm'][0])
    do_hg = matmul(dout, w_out[:HG_WIDTH], "nt", name="mix_do_hg")
    do_mla = matmul(dout, w_out[HG_WIDTH:], "nt", name="mix_do_mla")
    d_wout = jnp.concatenate([matmul(o_hg, dout, "tn", name="mix_dwout_hg"), matmul(o_mla, dout, "tn", name="mix_dwout_mla")],
                             axis=0)
    dq = attn_bwd_dq(q, k, v, o_mla, lse, do_mla, scale, t)
    dk, dv = attn_bwd_dkv(q, k, v, o_mla, lse, do_mla, scale, t)
    mp = _mla_params(W)
    d_mla, d_qa, d_wuq, d_kva, d_wukv, d_qnn, d_qnr, d_knn, d_knr = mla_prep_bwd(proj, cos_p, sin_p, mp, dq, dk, dv, tm)
    d_hg, d_lb, d_gain = hgrn2_bwd(proj, lb, gain, states, do_hg)
    w_in, n_hg = _w_in_padded(W), 4 * HG_WIDTH
    d_win = jnp.concatenate([matmul(hn, d_hg, "tn", name="mix_dwin_hg"), matmul(hn, d_mla, "tn", name="mix_dwin_mla")], axis=1)
    dhn = matmul(d_hg, w_in[:, :n_hg], "nt", name="mix_dhn_hg")
    dhn = matmul(d_mla, w_in[:, n_hg:], "nt", add=dhn, name="mix_dhn_mla")
    dh, d_g = rms_bwd(h, _row(W['norm_mix'][0]), dhn, tm, dout)
    logits = W['hg_lb_logits']
    d_logits = blocked_bwd(_lb_first, [full(logits, 'acc')], [full(d_lb)], 1, "hg_lb_bwd")[0]
    d_wuq = d_wuq.reshape(MLA_Q_RANK, MLA_HEADS, MLA_DK)
    d_wuq = jnp.concatenate([d_wuq[..., :MLA_NOPE], _rope_unpad(d_wuq[..., MLA_NOPE:])], axis=-1)
    hw = MLA_HEADS * MLA_NOPE
    d_wukv = jnp.concatenate([d_wukv[:, :hw].reshape(MLA_KV_RANK, MLA_HEADS, MLA_NOPE),
                              d_wukv[:, hw:].reshape(MLA_KV_RANK, MLA_HEADS, MLA_V)], axis=-1)
    d_win = jnp.concatenate([d_win[:, :IN_WIDTH - MLA_ROPE], _rope_unpad(d_win[:, IN_WIDTH - MLA_ROPE:])], axis=1)
    d_win = d_win.reshape(d_win.shape[0], N_CHIPS, W_IN_SHARD).transpose(1, 0, 2)
    d_win = jnp.pad(d_win, ((0, 0), (0, 0), (0, W_IN_SHARD_PAD - W_IN_SHARD)))
    d_wout = d_wout.reshape(N_CHIPS, d_wout.shape[0] // N_CHIPS, d_wout.shape[1])
    grads = {'norm_mix': d_g, 'hg_lb_logits': d_logits, 'mix_w_in': d_win, 'hg_out_norm': d_gain,
             'mla_q_a_norm': d_qa, 'mla_w_uq': d_wuq.reshape(1, MLA_Q_RANK, -1), 'mla_kv_a_norm': d_kva,
             'mla_w_ukv': d_wukv.reshape(1, MLA_KV_RANK, -1), 'mla_qn_nope': d_qnn, 'mla_qn_rope': _rope_unpad(d_qnr),
             'mla_kn_nope': d_knn, 'mla_kn_rope': _rope_unpad(d_knr), 'mix_w_out': d_wout}
    return dh,grads


def _s5_inputs(W):
    G = W['s5_lam_re'].shape[1]
    return [W['s5_lam_re'][0], W['s5_lam_im'][0], W['s5_log_dt'][0].reshape(G, 1),
            W['s5_b_re'][0].reshape(G, -1), W['s5_b_im'][0].reshape(G, -1)]


def _mixer1_fwd(h, W, tm):
    L, D = h.shape
    u = rms_fwd(h, _row(W['norm_mix'][1]), tm, F32)
    di = _s5_inputs(W)
    G = di[0].shape[0]
    sq, wide = ((G, S5_STATE), F32, (G, S5_STATE), lambda i: (0, 0)), ((G, S5_STATE * S5_GROUP), F32, (G, S5_STATE * S5_GROUP), lambda i: (0, 0))
    ar, ai, bbr, bbi = blocked_fwd(_s5_discretize, [full(a) for a in di], [sq, sq, wide, wide], 1, "s5_discretize")
    nb = G // S5_GB
    core = (_blockdiag_in(bbr.reshape(G, S5_STATE, S5_GROUP)), _blockdiag_in(bbi.reshape(G, S5_STATE, S5_GROUP)),
            ar.reshape(nb, 1, S5_LANES), ai.reshape(nb, 1, S5_LANES),
            _blockdiag_out(W['s5_c_re'][0]), _blockdiag_out(W['s5_c_im'][0]))
    y = s5_fwd(u, *core)
    d = W['s5_d']
    y2 = blocked_fwd(_s5_post, [rows(y, tm), rows(u, tm), full(d)], [_out((L, D), MXU_DTYPE, tm)], L // tm, "s5_post")[0]
    w_ab = jnp.concatenate([W['s5_w_glu_a'][0], W['s5_w_glu_b'][0]], axis=1)
    ab = matmul(y2, w_ab, name="s5_glu_in")
    mix = blocked_fwd(_glu, [rows(ab, tm, col=0, width=D), rows(ab, tm, col=1, width=D)], [_out((L, D), F32, tm)], L // tm,
                      "s5_glu")[0]
    return h + mix, (h, u, core, y, y2, ab)


def _mixer1_bwd(dout, saved, W, tm):
    h, u, core, y, y2, ab = saved
    L, D = h.shape
    da, db = blocked_bwd(_glu, [rows(ab, tm, 'blk', col=0, width=D), rows(ab, tm, 'blk', col=1, width=D)], [rows(dout, tm)],
                         L // tm, "s5_glu_bwd")
    w_a, w_b = W['s5_w_glu_a'][0], W['s5_w_glu_b'][0]
    dy2 = matmul(da, w_a, "nt", name="s5_dy2_a")
    dy2 = matmul(db, w_b, "nt", add=dy2, name="s5_dy2_b")
    d_wa = matmul(y2, da, "tn", name="s5_dwa")
    d_wb = matmul(y2, db, "tn", name="s5_dwb")
    d = W['s5_d']
    dy, du_skip, d_d = blocked_bwd(_s5_post, [rows(y, tm, 'blk'), rows(u, tm, 'blk'), full(d, 'acc')], [rows(dy2, tm)], L // tm,
                                   "s5_post_bwd")
    du, dwr, dwi, dar, dai, dcr, dci = s5_bwd(u, *core, dy, min(256, L))
    di = _s5_inputs(W)
    G = di[0].shape[0]
    cts = [dar.reshape(G, S5_STATE), dai.reshape(G, S5_STATE), _blockdiag_in_t(dwr).reshape(G, -1), _blockdiag_in_t(dwi).reshape(G, -1)]
    d_lr, d_li, d_ldt, d_br, d_bi = blocked_bwd(_s5_discretize, [full(a, 'acc') for a in di], [full(c) for c in cts], 1,
                                                "s5_discretize_bwd")
    dh, d_g = rms_bwd(h, _row(W['norm_mix'][1]), du + du_skip, tm, dout)
    bshape = W['s5_b_re'].shape
    grads = {'norm_mix': d_g, 's5_lam_re': d_lr[None], 's5_lam_im': d_li[None], 's5_log_dt': d_ldt.reshape(1, G),
             's5_b_re': d_br.reshape(bshape), 's5_b_im': d_bi.reshape(bshape), 's5_c_re': _blockdiag_out_t(dcr)[None],
             's5_c_im': _blockdiag_out_t(dci)[None], 's5_d': d_d, 's5_w_glu_a': d_wa.reshape(N_CHIPS, -1, D), 's5_w_glu_b': d_wb.reshape(N_CHIPS, -1, D)}
    return dh,grads


def kernel(x, mem, positions, norm_mix, norm_xa, norm_mem, norm_ffn, xa_wq, xa_wk, xa_wv, xa_wo, xa_q_norm, xa_k_norm, ffn_w_up, ffn_conv_w, ffn_conv_b, ffn_w_down, hg_lb_logits, mix_w_in, hg_out_norm, mla_q_a_norm, mla_w_uq, mla_kv_a_norm, mla_w_ukv, mla_qn_nope, mla_qn_rope, mla_kn_nope, mla_kn_rope, mix_w_out, s5_lam_re, s5_lam_im, s5_log_dt, s5_b_re, s5_b_im, s5_c_re, s5_c_im, s5_d, s5_w_glu_a, s5_w_glu_b, loss_target, m_norm_mix, m_norm_xa, m_norm_mem, m_norm_ffn, m_xa_wq, m_xa_wk, m_xa_wv, m_xa_wo, m_xa_q_norm, m_xa_k_norm, m_ffn_w_up, m_ffn_conv_w, m_ffn_conv_b, m_ffn_w_down, m_hg_lb_logits, m_mix_w_in, m_hg_out_norm, m_mla_q_a_norm, m_mla_w_uq, m_mla_kv_a_norm, m_mla_w_ukv, m_mla_qn_nope, m_mla_qn_rope, m_mla_kn_nope, m_mla_kn_rope, m_mix_w_out, m_s5_lam_re, m_s5_lam_im, m_s5_log_dt, m_s5_b_re, m_s5_b_im, m_s5_c_re, m_s5_c_im, m_s5_d, m_s5_w_glu_a, m_s5_w_glu_b, v_norm_mix, v_norm_xa, v_norm_mem, v_norm_ffn, v_xa_wq, v_xa_wk, v_xa_wv, v_xa_wo, v_xa_q_norm, v_xa_k_norm, v_ffn_w_up, v_ffn_conv_w, v_ffn_conv_b, v_ffn_w_down, v_hg_lb_logits, v_mix_w_in, v_hg_out_norm, v_mla_q_a_norm, v_mla_w_uq, v_mla_kv_a_norm, v_mla_w_ukv, v_mla_qn_nope, v_mla_qn_rope, v_mla_kn_nope, v_mla_kn_rope, v_mix_w_out, v_s5_lam_re, v_s5_lam_im, v_s5_log_dt, v_s5_b_re, v_s5_b_im, v_s5_c_re, v_s5_c_im, v_s5_d, v_s5_w_glu_a, v_s5_w_glu_b):
    P = dict(locals())
    assert sorted(P) == sorted(INPUTS) and norm_mix.shape[0] == 2 and mix_w_in.shape[0] == 1
    x, mem, target = P['x'][0], P['mem'][0], P['loss_target'][0]
    L, D = x.shape
    tm = min(256, L)

    W = {n: P[n] for n in REPLICATED}
    W.update(_gather_weights(P))

    inv_freq = 1.0 / (ROPE_BASE ** (jnp.arange(0, MLA_ROPE, 2, dtype=F32) / MLA_ROPE))
    ang = P['positions'][0].astype(F32)[:, None] * inv_freq
    cos, sin, z = jnp.cos(ang), jnp.sin(ang), jnp.zeros_like(ang)
    cos_p = jnp.concatenate([cos, z, cos, z], axis=1)
    sin_p = jnp.concatenate([-sin, z, sin, z], axis=1)

    h, s_mix0 = _mixer0_fwd(x, W, cos_p, sin_p, tm)
    h, s_xa0 = _xattn_fwd(h, mem, W, 0, tm)
    h, s_ffn0 = _ffn_fwd(h, W, 0, tm)
    h, s_mix1 = _mixer1_fwd(h, W, tm)
    h, s_xa1 = _xattn_fwd(h, mem, W, 1, tm)
    h, s_ffn1 = _ffn_fwd(h, W, 1, tm)
    n = L // tm
    dh, parts = blocked_fwd(_loss_fn, [rows(h, tm), rows(target, tm)],
                            [_out((L, D), F32, tm), ((n * 8, 128), F32, (8, 128), lambda i: (i, 0))], n, "loss")
    loss = lax.psum(jnp.sum(parts), ("x", "y", "c"))

    layered = {}

    def collect(g, lyr):
        for k_, v_ in g.items():
            layered.setdefault(k_, {})[lyr] = v_

    results = {}

    def big_items(lyr):
        return [(n_, 0 if P[n_].shape[0] == 1 else lyr, layered[n_][lyr]) for n_ in BIG if lyr in layered.get(n_, {})]

    dh, g = _ffn_bwd(dh, s_ffn1, W, 1, tm)
    collect(g, 1)
    dh, g = _xattn_bwd(dh, s_xa1, mem, W, 1, tm)
    collect(g, 1)
    dh, g = _mixer1_bwd(dh, s_mix1, W, tm)
    collect(g, 1)
    late = _reduce_batch(big_items(1), None, P, results, "late_layer", {'swap': 2, 'a2a': 3, 'join': 4})
    next(late)
    dh, g = _ffn_bwd(dh, s_ffn0, W, 0, tm)
    collect(g, 0)
    dh = late.send(dh)
    dh, g = _xattn_bwd(dh, s_xa0, mem, W, 0, tm)
    collect(g, 0)
    dx, g = _mixer0_bwd(dh, s_mix0, W, cos_p, sin_p, tm)
    collect(g, 0)

    GS = {}
    for name in SMALL:
        by_layer = [layered[name][lyr] for lyr in sorted(layered[name])]
        full_shape = W[name].shape
        GS[name] = (by_layer[0].reshape(full_shape) if len(by_layer) == 1
                    else jnp.stack([g_.reshape(full_shape[1:]) for g_ in by_layer]))
    small = _pack([GS[n_] for n_ in SMALL], 2 * N_CHIPS * ROW_MULT).reshape(N_CHIPS, -1, PACK_W)
    first = _reduce_batch(big_items(0), small, P, results, "first_layer", {'swap': 5, 'a2a': 6, 'join': 7})
    next(first)
    dx = late.send(dx)
    last_pair_sum = first.send(None)
    late.send(last_pair_sum)
    late_name = big_items(1)[-1][0]
    results[late_name] = list(first.send(list(results[late_name])))
    first.send(None)
    outs = list(_update_small(results['small_quarter'], GS, P))
    for k_ in range(4):
        outs[k_].update({n_: results[n_][k_].reshape(P[n_].shape) for n_ in BIG})
    return (loss, dx[None], *[d[n_] for d in outs for n_ in WEIGHTS])
```

```python
import functools
import math

import jax
import jax.numpy as jnp
import numpy as np
from jax import lax
from jax.experimental import pallas as pl
from jax.experimental.pallas import tpu as pltpu
from jax.experimental.pallas import tpu_sc as plsc

F32 = jnp.float32
BF16 = jnp.bfloat16
MXU_DTYPE = BF16
HI = lax.Precision.HIGHEST
V7X_VMEM_LIMIT_BYTES = 56 * 1024 * 1024
EPS = 1e-6
MESH = pl.DeviceIdType.MESH

HG_HEADS, HG_DIM = 4, 128
HG_WIDTH = HG_HEADS * HG_DIM
HG_SUB = 32
HG_BLOCK = 64
MLA_HEADS, MLA_Q_RANK, MLA_KV_RANK = 4, 256, 128
MLA_NOPE, MLA_ROPE, MLA_V = 128, 64, 128
MLA_QK = MLA_NOPE + MLA_ROPE
MLA_DK = 256
ROPE_BASE = 10000.0
IN_WIDTH = 4 * HG_WIDTH + MLA_Q_RANK + MLA_KV_RANK + MLA_ROPE
IN_PAD = 4 * HG_WIDTH + MLA_Q_RANK + MLA_KV_RANK + 128
S5_GROUP, S5_STATE = 16, 64
S5_GB = 8
DT_MIN, DT_MAX = 1e-3, 1e-1
XA_HEADS = 4
CONV_W = 3
ADAM_LR, ADAM_B1, ADAM_B2, ADAM_EPS, ADAM_WD, ADAM_STEP = 0.001, 0.9, 0.999, 1e-08, 0.01, 10


def _cparams(sem):
    return pltpu.CompilerParams(dimension_semantics=sem, vmem_limit_bytes=V7X_VMEM_LIMIT_BYTES)


class Opd:
    def __init__(self, arr, block, imap, grad=None, gshape=None, gimap=None):
        self.arr, self.block, self.imap, self.grad = arr, block, imap, grad
        self.gshape = arr.shape if gshape is None else gshape
        self.gimap = imap if gimap is None else gimap

    def spec(self):
        return pl.BlockSpec(self.block, self.imap)

    def gspec(self):
        return pl.BlockSpec(self.block, self.gimap)


def rows(arr, tm, grad=None, col=0, width=None):
    width = arr.shape[1] if width is None else width
    return Opd(arr, (tm, width), lambda i, c=col: (i, c), grad, (arr.shape[0], width), lambda i: (i, 0))


def cols(arr, tn, grad=None):
    return Opd(arr, (arr.shape[0], tn), lambda j: (0, j), grad)


def full(arr, grad=None):
    return Opd(arr, arr.shape, lambda i: (0, 0), grad)


def _load(ref):
    v = ref[...]
    return v.astype(F32) if jnp.issubdtype(v.dtype, jnp.floating) else v


def blocked_fwd(f, opds, outs, n, name):
    n_in = len(opds)

    def body(*refs):
        ys = f(*[_load(r) for r in refs[:n_in]])
        for r, y in zip(refs[n_in:], ys):
            r[...] = y.astype(r.dtype)

    res = pl.pallas_call(
        body, name=name, grid=(n,),
        in_specs=[o.spec() for o in opds],
        out_specs=[pl.BlockSpec(b, m) for (_, _, b, m) in outs],
        out_shape=[jax.ShapeDtypeStruct(s, d) for (s, d, _, _) in outs],
        compiler_params=_cparams(("parallel",)),
    )(*[o.arr for o in opds])
    return res


def blocked_bwd(f, opds, dys, n, name, plus=None):
    n_in, n_dy = len(opds), len(dys)
    diff = [i for i, o in enumerate(opds) if o.grad]
    extra = [] if plus is None else [plus]

    def body(*refs):
        vals = [_load(r) for r in refs[:n_in]]

        def fd(*dv):
            allv = list(vals)
            for i, v in zip(diff, dv):
                allv[i] = v
            return tuple(f(*allv))

        ys, vjp = jax.vjp(fd, *[vals[i] for i in diff])
        cts = tuple(_load(r).astype(y.dtype) for r, y in zip(refs[n_in:n_in + n_dy], ys))
        gs = list(vjp(cts))
        if extra:
            gs[0] = gs[0] + _load(refs[n_in + n_dy])
        for r, g, i in zip(refs[n_in + n_dy + len(extra):], gs, diff):
            if opds[i].grad == 'acc':
                @pl.when(pl.program_id(0) == 0)
                def _(r=r):
                    r[...] = jnp.zeros(r.shape, r.dtype)
                r[...] += g.astype(r.dtype)
            else:
                r[...] = g.astype(r.dtype)

    any_acc = any(opds[i].grad == 'acc' for i in diff)
    res = pl.pallas_call(
        body, name=name, grid=(n,),
        in_specs=[o.spec() for o in opds + dys + extra],
        out_specs=[opds[i].gspec() for i in diff],
        out_shape=[jax.ShapeDtypeStruct(opds[i].gshape, F32) for i in diff],
        compiler_params=_cparams(("arbitrary" if any_acc else "parallel",)),
    )(*[o.arr for o in opds + dys + extra])
    return res


def _tile(dim, want):
    for t in range(want - want % 16, 0, -16):
        if dim % t == 0:
            return t
    assert dim <= want, (dim, want)
    return dim


MATMUL_VMEM_BUDGET = 40 * 1024 * 1024
MATMUL_ROWS = 512


def _widest(N, fits):
    for t in range(N - N % 128, 0, -128):
        if N % t == 0 and fits(t):
            return t
    return N


def matmul(a, b, mode="nn", out_dtype=F32, add=None, name="matmul", into=None, col_blocks=None):
    sa, sb, so = a.dtype.itemsize, b.dtype.itemsize, jnp.dtype(out_dtype).itemsize
    has_add = add is not None
    if mode == "tn":
        (K, M), (K2, N) = a.shape, b.shape
        assert K == K2 and not has_add and out_dtype == F32, (a.shape, b.shape)
        tk = _tile(K, MATMUL_ROWS)
        tn = _widest(N, lambda t: 2 * (tk * M * sa + tk * t * sb + M * t * 4) <= MATMUL_VMEM_BUDGET)
        extra, alias = [], {}
        if into is not None:
            buf, lead = into[0], tuple(into[1:])
            if col_blocks is not None:
                assert N % col_blocks == 0 and (N // col_blocks) % 128 == 0 and tn >= N // col_blocks, (N, col_blocks, tn)
                tn = N // col_blocks
                assert buf.shape[len(lead):] == (M, tn), (buf.shape, lead, M, tn)
                out_spec = pl.BlockSpec((None,) * len(lead) + (M, tn), lambda j, k: lead[:-1] + (lead[-1] + j, 0, 0))
            else:
                assert buf.shape[len(lead):] == (M, N), (buf.shape, lead, M, N)
                out_spec = pl.BlockSpec((None,) * len(lead) + (M, tn), lambda j, k: lead + (0, j))
            out_shape = jax.ShapeDtypeStruct(buf.shape, F32)
            extra, alias = [buf], {2: 0}
        else:
            assert col_blocks is None
            out_spec = pl.BlockSpec((M, tn), lambda j, k: (0, j))
            out_shape = jax.ShapeDtypeStruct((M, N), F32)

        def body(a_ref, b_ref, *rest):
            o_ref = rest[-1]
            r = lax.dot_general(a_ref[...].astype(MXU_DTYPE), b_ref[...].astype(MXU_DTYPE), ((_TN), ((), ())),
                                preferred_element_type=F32)

            @pl.when(pl.program_id(1) == 0)
            def _():
                o_ref[...] = r

            @pl.when(pl.program_id(1) > 0)
            def _():
                o_ref[...] += r

        return pl.pallas_call(
            body, name=name, grid=(N // tn, K // tk),
            in_specs=[pl.BlockSpec((tk, M), lambda j, k: (k, 0)), pl.BlockSpec((tk, tn), lambda j, k: (k, j))]
            + [pl.BlockSpec(memory_space=pl.ANY)] * len(extra),
            out_specs=out_spec, out_shape=out_shape, input_output_aliases=alias,
            compiler_params=_cparams(("parallel", "arbitrary")),
        )(a, b, *extra)

    (M, K) = a.shape
    N = b.shape[1] if mode == "nn" else b.shape[0]
    assert K == (b.shape[0] if mode == "nn" else b.shape[1]), (a.shape, b.shape, mode)
    tm = _tile(M, MATMUL_ROWS)
    tn = _widest(N, lambda t: 2 * (tm * K * sa + K * t * sb + tm * t * (so + 4 * has_add)) <= MATMUL_VMEM_BUDGET)
    dims = ((_NN if mode == "nn" else _NT), ((), ()))

    def body(*refs):
        r = lax.dot_general(refs[0][...].astype(MXU_DTYPE), refs[1][...].astype(MXU_DTYPE), dims, preferred_element_type=F32)
        if has_add:
            r = r + refs[2][...].astype(F32)
        refs[-1][...] = r.astype(refs[-1].dtype)

    b_spec = pl.BlockSpec((K, tn), lambda j, i: (0, j)) if mode == "nn" else pl.BlockSpec((tn, K), lambda j, i: (j, 0))
    in_specs = [pl.BlockSpec((tm, K), lambda j, i: (i, 0)), b_spec]
    args = [a, b]
    if has_add:
        in_specs.append(pl.BlockSpec((tm, tn), lambda j, i: (i, j)))
        args.append(add)
    return pl.pallas_call(
        body, name=name, grid=(N // tn, M // tm),
        in_specs=in_specs,
        out_specs=pl.BlockSpec((tm, tn), lambda j, i: (i, j)),
        out_shape=jax.ShapeDtypeStruct((M, N), out_dtype),
        compiler_params=_cparams(("parallel", "parallel")),
    )(*args)


def _dot(a, b, dims, precision=None):
    if precision is None:
        a, b = a.astype(MXU_DTYPE), b.astype(MXU_DTYPE)
    return lax.dot_general(a, b, (dims, ((), ())), precision=precision, preferred_element_type=F32)


_NN = ((1,), (0,))
_NT = ((1,), (1,))
_TN = ((0,), (0,))


def _rms(x, gain):
    return x * lax.rsqrt(jnp.mean(x * x, axis=-1, keepdims=True) + EPS) * gain


def _hg_block(st_t, q, fl, iv, g, lb, gain):
    row = lax.broadcasted_iota(jnp.int32, (HG_SUB, HG_SUB), 0)
    col = lax.broadcasted_iota(jnp.int32, (HG_SUB, HG_SUB), 1)
    tri = (row >= col).astype(F32)
    outs, states = [], []
    for h in range(HG_HEADS):
        sl = slice(h * HG_DIM, (h + 1) * HG_DIM)
        st = st_t[h * HG_DIM:(h + 1) * HG_DIM, :]
        lbh = lb[:, sl]
        fg = lbh + (1.0 - lbh) * jax.nn.sigmoid(fl[:, sl])
        lf, kk, qf, v = jnp.log(fg), 1.0 - fg, jax.nn.silu(q[:, sl]), iv[:, sl]
        parts = []
        for s in range(q.shape[0] // HG_SUB):
            r = slice(s * HG_SUB, (s + 1) * HG_SUB)
            b = _dot(tri, lf[r], _NN, HI)
            b_mid = jnp.sum(lf[r][:HG_SUB // 2], axis=0, keepdims=True)
            b_end = jnp.sum(lf[r], axis=0, keepdims=True)
            sc = _dot(qf[r] * jnp.exp(b - b_mid), kk[r] * jnp.exp(b_mid - b), _NT) * tri
            parts.append(_dot(sc, v[r], _NN) + _dot(qf[r] * jnp.exp(b), st, _NT))
            st = st * jnp.exp(b_end) + _dot(v[r], kk[r] * jnp.exp(b_end - b), _TN)
        o = jnp.concatenate(parts, axis=0)
        outs.append(_rms(o, gain[:, sl]) * jax.nn.silu(g[:, sl]))
        states.append(st)
    return jnp.concatenate(states, axis=0), jnp.concatenate(outs, axis=1)


def _hg_specs(proj, nb):
    return [pl.BlockSpec((HG_BLOCK, HG_WIDTH), lambda i, c=c, f=nb: (f(i), c)) for c in range(4)]


def hgrn2_fwd(proj, lb, gain):
    L = proj.shape[0]
    n = L // HG_BLOCK

    def body(q, fl, iv, g, lb_r, gain_r, o_ref, st_ref, st):
        @pl.when(pl.program_id(0) == 0)
        def _():
            st[...] = jnp.zeros(st.shape, F32)

        st_ref[0] = st[...]
        new, o = _hg_block(st[...], q[...], fl[...], iv[...], g[...], lb_r[...], gain_r[...])
        st[...] = new
        o_ref[...] = o.astype(o_ref.dtype)

    pspec = pl.BlockSpec((1, HG_WIDTH), lambda i: (0, 0))
    return pl.pallas_call(
        body, name="hgrn2_fwd", grid=(n,),
        in_specs=_hg_specs(proj, lambda i: i) + [pspec, pspec],
        out_specs=[pl.BlockSpec((HG_BLOCK, HG_WIDTH), lambda i: (i, 0)),
                   pl.BlockSpec((1, HG_WIDTH, HG_DIM), lambda i: (i, 0, 0))],
        out_shape=[jax.ShapeDtypeStruct((L, HG_WIDTH), MXU_DTYPE),
                   jax.ShapeDtypeStruct((n, HG_WIDTH, HG_DIM), F32)],
        scratch_shapes=[pltpu.VMEM((HG_WIDTH, HG_DIM), F32)],
        compiler_params=_cparams(("arbitrary",)),
    )(proj, proj, proj, proj, lb, gain)


def hgrn2_bwd(proj, lb, gain, states, do):
    L = proj.shape[0]
    n = L // HG_BLOCK

    def body(q, fl, iv, g, lb_r, gain_r, st_r, do_r, dproj, dlb, dgain, dst):
        @pl.when(pl.program_id(0) == 0)
        def _():
            dst[...] = jnp.zeros(dst.shape, F32)
            dlb[...] = jnp.zeros(dlb.shape, F32)
            dgain[...] = jnp.zeros(dgain.shape, F32)

        _, vjp = jax.vjp(_hg_block, st_r[0], q[...], fl[...], iv[...], g[...], lb_r[...], gain_r[...])
        d_st, dq, dfl, div, dg, d_lb, d_gain = vjp((dst[...], do_r[...].astype(F32)))
        dst[...] = d_st
        dproj[:, 0 * HG_WIDTH:1 * HG_WIDTH] = dq
        dproj[:, 1 * HG_WIDTH:2 * HG_WIDTH] = dfl
        dproj[:, 2 * HG_WIDTH:3 * HG_WIDTH] = div
        dproj[:, 3 * HG_WIDTH:4 * HG_WIDTH] = dg
        dlb[...] += d_lb
        dgain[...] += d_gain

    rev = lambda i: n - 1 - i
    pspec = pl.BlockSpec((1, HG_WIDTH), lambda i: (0, 0))
    return pl.pallas_call(
        body, name="hgrn2_bwd", grid=(n,),
        in_specs=_hg_specs(proj, rev) + [pspec, pspec,
                                         pl.BlockSpec((1, HG_WIDTH, HG_DIM), lambda i: (rev(i), 0, 0)),
                                         pl.BlockSpec((HG_BLOCK, HG_WIDTH), lambda i: (rev(i), 0))],
        out_specs=[pl.BlockSpec((HG_BLOCK, 4 * HG_WIDTH), lambda i: (rev(i), 0)), pspec, pspec],
        out_shape=[jax.ShapeDtypeStruct((L, 4 * HG_WIDTH), F32),
                   jax.ShapeDtypeStruct((1, HG_WIDTH), F32), jax.ShapeDtypeStruct((1, HG_WIDTH), F32)],
        scratch_shapes=[pltpu.VMEM((HG_WIDTH, HG_DIM), F32)],
        compiler_params=_cparams(("arbitrary",)),
    )(proj, proj, proj, proj, lb, gain, states, do)


def _rope_rms(x, gain_p, cos_p, sin_p):
    n = x * lax.rsqrt(jnp.sum(x * x, axis=-1, keepdims=True) * (1.0 / MLA_ROPE) + EPS) * gain_p
    r = lax.broadcasted_iota(jnp.int32, (128, 128), 0)
    c = lax.broadcasted_iota(jnp.int32, (128, 128), 1)
    swap = (r == (c + 64) % 128).astype(F32)
    return n * cos_p + _dot(n, swap, _NN, HI) * sin_p


MLA_IN = MLA_Q_RANK + MLA_KV_RANK + 128


def _mla_prep(x, cos_p, sin_p, q_a, w_uq, kv_a, w_ukv, qn_nope, qn_rope, kn_nope, kn_rope):
    c_q, c_kv, kpe = x[:, :MLA_Q_RANK], x[:, MLA_Q_RANK:MLA_Q_RANK + MLA_KV_RANK], x[:, MLA_Q_RANK + MLA_KV_RANK:]
    q = _dot(_rms(c_q, q_a), w_uq, _NN)
    kv = _dot(_rms(c_kv, kv_a), w_ukv, _NN)
    k_pe = _rope_rms(kpe, kn_rope, cos_p, sin_p)
    qs, ks = [], []
    for h in range(MLA_HEADS):
        qs.append(_rms(q[:, h * MLA_DK:h * MLA_DK + MLA_NOPE], qn_nope))
        qs.append(_rope_rms(q[:, h * MLA_DK + MLA_NOPE:(h + 1) * MLA_DK], qn_rope, cos_p, sin_p))
        ks.append(_rms(kv[:, h * MLA_NOPE:(h + 1) * MLA_NOPE], kn_nope))
        ks.append(k_pe)
    return jnp.concatenate(qs, axis=1), jnp.concatenate(ks, axis=1), kv[:, MLA_HEADS * MLA_NOPE:]


def _mla_prep_opds(proj, cos_p, sin_p, params, tm, grads):
    g = (lambda k: k) if grads else (lambda k: None)
    assert (4 * HG_WIDTH) % MLA_IN == 0
    return ([rows(proj, tm, g('blk'), col=4 * HG_WIDTH // MLA_IN, width=MLA_IN), rows(cos_p, tm), rows(sin_p, tm)]
            + [full(p, g('acc')) for p in params])


def mla_prep_fwd(proj, cos_p, sin_p, params, tm):
    L = proj.shape[0]
    W = MLA_HEADS * MLA_DK
    rb = lambda w: (tm, w)
    outs = [((L, W), MXU_DTYPE, rb(W), lambda i: (i, 0)), ((L, W), MXU_DTYPE, rb(W), lambda i: (i, 0)),
            ((L, MLA_HEADS * MLA_V), MXU_DTYPE, rb(MLA_HEADS * MLA_V), lambda i: (i, 0))]
    return blocked_fwd(_mla_prep, _mla_prep_opds(proj, cos_p, sin_p, params, tm, False), outs, L // tm, "mla_prep_fwd")


def mla_prep_bwd(proj, cos_p, sin_p, params, dq, dk, dv, tm):
    L = proj.shape[0]
    return blocked_bwd(_mla_prep, _mla_prep_opds(proj, cos_p, sin_p, params, tm, True),
                       [rows(dq, tm), rows(dk, tm), rows(dv, tm)], L // tm, "mla_prep_bwd")


def _scores(q, k, scale, shift=None):
    s = _dot(q, k, _NT) * scale
    if shift is None:
        return s
    row = lax.broadcasted_iota(jnp.int32, s.shape, 0)
    col = lax.broadcasted_iota(jnp.int32, s.shape, 1)
    return jnp.where(col <= row + shift, s, -jnp.inf)


ATTN_ROWS = 512
ATTN_WIDE = 2


def attn_fwd(q, k, v, scale, t):
    L = q.shape[0]
    tq = ATTN_WIDE * t

    def body(q_ref, k_ref, v_ref, o_ref, lse_ref):
        i = pl.program_id(1)
        qb = q_ref[...]

        def step(j, carry, shift=None):
            m, l, acc = carry
            kj = k_ref[pl.ds(pl.multiple_of(j * t, t), t), :]
            vj = v_ref[pl.ds(pl.multiple_of(j * t, t), t), :]
            s = _scores(qb, kj, scale, shift)
            m_new = jnp.maximum(m, jnp.max(s, axis=-1, keepdims=True))
            p = jnp.exp(s - m_new)
            alpha = jnp.exp(m - m_new)
            return m_new, alpha * l + jnp.sum(p, axis=-1, keepdims=True), alpha * acc + _dot(p, vj, _NN)

        carry = (jnp.full((tq, 1), -jnp.inf, F32), jnp.zeros((tq, 1), F32), jnp.zeros((tq, MLA_V), F32))
        carry = lax.fori_loop(0, ATTN_WIDE * i, step, carry)
        for d in range(ATTN_WIDE):
            carry = step(ATTN_WIDE * i + d, carry, -d * t)
        m, l, acc = carry
        o_ref[...] = acc / l
        lse_ref[...] = jnp.broadcast_to(m + jnp.log(l), lse_ref.shape)

    hspec = lambda rows_, w: pl.BlockSpec((rows_, w), lambda h, i: (0, h))
    bspec = lambda w: pl.BlockSpec((tq, w), lambda h, i: (i, h))
    return pl.pallas_call(
        body, name="attn_fwd", grid=(MLA_HEADS, L // tq),
        in_specs=[bspec(MLA_DK), hspec(L, MLA_DK), hspec(L, MLA_V)],
        out_specs=[bspec(MLA_V), bspec(MLA_V)],
        out_shape=[jax.ShapeDtypeStruct((L, MLA_HEADS * MLA_V), F32)] * 2,
        compiler_params=_cparams(("parallel", "parallel")),
    )(q, k, v)


def attn_bwd_dq(q, k, v, o, lse, do, scale, t):
    L = q.shape[0]
    tq = ATTN_WIDE * t

    def body(q_ref, k_ref, v_ref, o_ref, lse_ref, do_ref, dq_ref):
        i = pl.program_id(1)
        qb, dob = q_ref[...], do_ref[...]
        delta = jnp.sum(dob * o_ref[...], axis=-1, keepdims=True)
        lse_c = jnp.max(lse_ref[...], axis=-1, keepdims=True)

        def step(j, dq, shift=None):
            kj = k_ref[pl.ds(pl.multiple_of(j * t, t), t), :]
            vj = v_ref[pl.ds(pl.multiple_of(j * t, t), t), :]
            p = jnp.exp(_scores(qb, kj, scale, shift) - lse_c)
            ds = p * (_dot(dob, vj, _NT) - delta) * scale
            return dq + _dot(ds, kj, _NN)

        dq = lax.fori_loop(0, ATTN_WIDE * i, step, jnp.zeros((tq, MLA_DK), F32))
        for d in range(ATTN_WIDE):
            dq = step(ATTN_WIDE * i + d, dq, -d * t)
        dq_ref[...] = dq

    hspec = lambda w: pl.BlockSpec((L, w), lambda h, i: (0, h))
    bspec = lambda w: pl.BlockSpec((tq, w), lambda h, i: (i, h))
    return pl.pallas_call(
        body, name="attn_bwd_dq", grid=(MLA_HEADS, L // tq),
        in_specs=[bspec(MLA_DK), hspec(MLA_DK), hspec(MLA_V), bspec(MLA_V), bspec(MLA_V), bspec(MLA_V)],
        out_specs=bspec(MLA_DK),
        out_shape=jax.ShapeDtypeStruct((L, MLA_HEADS * MLA_DK), F32),
        compiler_params=_cparams(("parallel", "parallel")),
    )(q, k, v, o, lse, do)


def attn_bwd_dkv(q, k, v, o, lse, do, scale, t):
    L = q.shape[0]
    tk = ATTN_WIDE * t

    def body(q_ref, k_ref, v_ref, o_ref, lse_ref, do_ref, dk_ref, dv_ref):
        j = pl.program_id(1)
        kb, vb = k_ref[...], v_ref[...]

        def step(i, carry, shift=None):
            dk, dv = carry
            r = pl.ds(pl.multiple_of(i * t, t), t)
            qi, doi = q_ref[r, :], do_ref[r, :]
            delta = jnp.sum(doi * o_ref[r, :], axis=-1, keepdims=True)
            lse_c = jnp.max(lse_ref[r, :], axis=-1, keepdims=True)
            p = jnp.exp(_scores(qi, kb, scale, shift) - lse_c)
            ds = p * (_dot(doi, vb, _NT) - delta) * scale
            return dk + _dot(ds, qi, _TN), dv + _dot(p, doi, _TN)

        carry = (jnp.zeros((tk, MLA_DK), F32), jnp.zeros((tk, MLA_V), F32))
        for d in range(ATTN_WIDE):
            carry = step(ATTN_WIDE * j + d, carry, d * t)
        dk, dv = lax.fori_loop(ATTN_WIDE * (j + 1), L // t, step, carry)
        dk_ref[...] = dk
        dv_ref[...] = dv

    hspec = lambda w: pl.BlockSpec((L, w), lambda h, j: (0, h))
    bspec = lambda w: pl.BlockSpec((tk, w), lambda h, j: (j, h))
    return pl.pallas_call(
        body, name="attn_bwd_dkv", grid=(MLA_HEADS, L // tk),
        in_specs=[hspec(MLA_DK), bspec(MLA_DK), bspec(MLA_V), hspec(MLA_V), hspec(MLA_V), hspec(MLA_V)],
        out_specs=[bspec(MLA_DK), bspec(MLA_V)],
        out_shape=[jax.ShapeDtypeStruct((L, MLA_HEADS * MLA_DK), F32), jax.ShapeDtypeStruct((L, MLA_HEADS * MLA_V), F32)],
        compiler_params=_cparams(("parallel", "parallel")),
    )(q, k, v, o, lse, do)


S5_LANES = S5_GB * S5_STATE


def _cmul(ar, ai, br, bi):
    return ar * br - ai * bi, ar * bi + ai * br


def _a_powers(ar, ai, reverse):
    a2 = _cmul(ar, ai, ar, ai)
    a4 = _cmul(*a2, *a2)
    row = lax.broadcasted_iota(jnp.int32, (8, ar.shape[1]), 0)
    e = (8 - row) if reverse else (row + 1)
    tr, ti = jnp.ones((8, ar.shape[1]), F32), jnp.zeros((8, ar.shape[1]), F32)
    for bit, (pr, pi) in ((1, (ar, ai)), (2, a2), (4, a4), (8, _cmul(*a4, *a4))):
        nr, ni = _cmul(tr, ti, pr, pi)
        sel = (e & bit) != 0
        tr, ti = jnp.where(sel, nr, tr), jnp.where(sel, ni, ti)
    pows = []
    for d, (pr, pi) in zip((1, 2, 4), ((ar, ai), a2, a4)):
        keep = (row < 8 - d) if reverse else (row >= d)
        pows.append((jnp.where(keep, pr, 0.0), jnp.where(keep, pi, 0.0)))
    return pows, (tr, ti)


def _scan8(xr, xi, pows, table, cr, ci, reverse):
    for d, (pr, pi) in zip((1, 2, 4), pows):
        shift = 8 - d if reverse else d
        mr, mi = _cmul(pr, pi, pltpu.roll(xr, shift, 0), pltpu.roll(xi, shift, 0))
        xr, xi = xr + mr, xi + mi
    mr, mi = _cmul(table[0], table[1], cr, ci)
    return xr + mr, xi + mi


def _row_of(x, r):
    row = lax.broadcasted_iota(jnp.int32, x.shape, 0)
    return jnp.sum(jnp.where(row == r, x, 0.0), axis=0, keepdims=True)


def _s5_scan_fwd(h_re, h_im, ar, ai, L):
    pows, table = _a_powers(ar, ai, False)

    def step(i, carry):
        r = pl.ds(pl.multiple_of(i * 8, 8), 8)
        xr, xi = _scan8(h_re[r, :], h_im[r, :], pows, table, carry[0], carry[1], False)
        h_re[r, :] = xr
        h_im[r, :] = xi
        return xr[7:8, :], xi[7:8, :]

    z = jnp.zeros((1, ar.shape[1]), F32)
    lax.fori_loop(0, L // 8, step, (z, z))


def _s5_specs(L):
    return [pl.BlockSpec((L, 128), lambda g: (0, g)),
            pl.BlockSpec((1, 128, S5_LANES), lambda g: (g, 0, 0)), pl.BlockSpec((1, 128, S5_LANES), lambda g: (g, 0, 0)),
            pl.BlockSpec((1, 1, S5_LANES), lambda g: (g, 0, 0)), pl.BlockSpec((1, 1, S5_LANES), lambda g: (g, 0, 0)),
            pl.BlockSpec((1, S5_LANES, 128), lambda g: (g, 0, 0)), pl.BlockSpec((1, S5_LANES, 128), lambda g: (g, 0, 0))]


def s5_fwd(u, w_re, w_im, a_re, a_im, c_re, c_im):
    L, D = u.shape

    def body(u_ref, wr, wi, ar, ai, cr, ci, y_ref, h_re, h_im):
        ub = u_ref[...]
        h_re[...] = _dot(ub, wr[0], _NN)
        h_im[...] = _dot(ub, wi[0], _NN)
        _s5_scan_fwd(h_re, h_im, ar[0], ai[0], L)
        y_ref[...] = _dot(h_re[...], cr[0], _NN) - _dot(h_im[...], ci[0], _NN)

    return pl.pallas_call(
        body, name="s5_fwd", grid=(D // 128,),
        in_specs=_s5_specs(L), out_specs=pl.BlockSpec((L, 128), lambda g: (0, g)),
        out_shape=jax.ShapeDtypeStruct((L, D), F32),
        scratch_shapes=[pltpu.VMEM((L, S5_LANES), F32), pltpu.VMEM((L, S5_LANES), F32)],
        compiler_params=_cparams(("parallel",)),
    )(u, w_re, w_im, a_re, a_im, c_re, c_im)


def s5_bwd(u, w_re, w_im, a_re, a_im, c_re, c_im, dy, tc):
    L, D = u.shape
    nch = L // tc

    def body(u_ref, wr, wi, ar_ref, ai_ref, cr, ci, dy_ref, du_ref, dwr, dwi, dar, dai, dcr, dci, h_re, h_im, g_re, g_im):
        ar, ai = ar_ref[0], ai_ref[0]
        ub = u_ref[...]
        h_re[...] = _dot(ub, wr[0], _NN)
        h_im[...] = _dot(ub, wi[0], _NN)
        _s5_scan_fwd(h_re, h_im, ar, ai, L)
        dyb = dy_ref[...]
        dcr[0] = _dot(h_re[...], dyb, _TN)
        dci[0] = -_dot(h_im[...], dyb, _TN)
        pows, table = _a_powers(ar, -ai, True)
        dwr[0] = jnp.zeros((128, S5_LANES), F32)
        dwi[0] = jnp.zeros((128, S5_LANES), F32)
        z1 = jnp.zeros((1, S5_LANES), F32)
        z8 = jnp.zeros((8, S5_LANES), F32)

        def chunk(cc, carry):
            c0 = pl.multiple_of((nch - 1 - cc) * tc, tc)
            rows_c = pl.ds(c0, tc)
            dyc = dy_ref[rows_c, :]
            g_re[...] = _dot(dyc, cr[0], _NT)
            g_im[...] = -_dot(dyc, ci[0], _NT)

            def step(ii, cy):
                gr_c, gi_c, acc_r, acc_i = cy
                i8 = pl.multiple_of((tc // 8 - 1 - ii) * 8, 8)
                rl = pl.ds(i8, 8)
                xr, xi = _scan8(g_re[rl, :], g_im[rl, :], pows, table, gr_c, gi_c, True)
                g_re[rl, :] = xr
                g_im[rl, :] = xi
                t0 = c0 + i8
                hb_r, hb_i = h_re[pl.ds(t0, 8), :], h_im[pl.ds(t0, 8), :]
                tp = pl.multiple_of(jnp.maximum(t0 - 8, 0), 8)
                first = (t0 > 0).astype(F32)
                pr = h_re[pl.ds(tp, 8), :][7:8, :] * first
                pi = h_im[pl.ds(tp, 8), :][7:8, :] * first
                row = lax.broadcasted_iota(jnp.int32, xr.shape, 0)
                hp_r = jnp.where(row == 0, pr, pltpu.roll(hb_r, 1, 0))
                hp_i = jnp.where(row == 0, pi, pltpu.roll(hb_i, 1, 0))
                return (xr[0:1, :], xi[0:1, :],
                        acc_r + xr * hp_r + xi * hp_i, acc_i + xi * hp_r - xr * hp_i)

            cy = lax.fori_loop(0, tc // 8, step, carry)
            uc = u_ref[rows_c, :]
            gr, gi = g_re[...], g_im[...]
            du_ref[rows_c, :] = _dot(gr, wr[0], _NT) + _dot(gi, wi[0], _NT)
            dwr[0] += _dot(uc, gr, _TN)
            dwi[0] += _dot(uc, gi, _TN)
            return cy

        _, _, acc_r, acc_i = lax.fori_loop(0, nch, chunk, (z1, z1, z8, z8))
        dar[0] = jnp.sum(acc_r, axis=0, keepdims=True)
        dai[0] = jnp.sum(acc_i, axis=0, keepdims=True)

    specs = _s5_specs(L)
    return pl.pallas_call(
        body, name="s5_bwd", grid=(D // 128,),
        in_specs=specs + [pl.BlockSpec((L, 128), lambda g: (0, g))],
        out_specs=[pl.BlockSpec((L, 128), lambda g: (0, g))] + specs[1:],
        out_shape=[jax.ShapeDtypeStruct((L, D), F32)] + [jax.ShapeDtypeStruct(x.shape, F32)
                                                        for x in (w_re, w_im, a_re, a_im, c_re, c_im)],
        scratch_shapes=[pltpu.VMEM((L, S5_LANES), F32), pltpu.VMEM((L, S5_LANES), F32),
                        pltpu.VMEM((tc, S5_LANES), F32), pltpu.VMEM((tc, S5_LANES), F32)],
        compiler_params=_cparams(("parallel",)),
    )(u, w_re, w_im, a_re, a_im, c_re, c_im, dy)


def _s5_discretize(lr, li, ldt, br, bi):
    dt = jnp.exp(ldt)
    mag = jnp.exp(lr * dt)
    ar, ai = mag * jnp.cos(li * dt), mag * jnp.sin(li * dt)
    den = lr * lr + li * li
    zr = ((ar - 1.0) * lr + ai * li) / den
    zi = (ai * lr - (ar - 1.0) * li) / den
    p = lax.broadcasted_iota(jnp.int32, (S5_STATE, S5_STATE * S5_GROUP), 0)
    c = lax.broadcasted_iota(jnp.int32, (S5_STATE, S5_STATE * S5_GROUP), 1)
    rep = (c // S5_GROUP == p).astype(F32)
    zr, zi = _dot(zr, rep, _NN, HI), _dot(zi, rep, _NN, HI)
    return ar, ai, zr * br - zi * bi, zr * bi + zi * br


def _conv_shift(x, d):
    row = lax.broadcasted_iota(jnp.int32, x.shape, 0)
    return jnp.where(row >= d, pltpu.roll(x, d, 0), 0.0)


def _conv_unshift(x, d):
    n = x.shape[0]
    row = lax.broadcasted_iota(jnp.int32, x.shape, 0)
    return jnp.where(row < n - d, pltpu.roll(x, n - d, 0), 0.0)


@functools.partial(jax.custom_vjp, nondiff_argnums=(1,))
def _shift_rows(x, d):
    return _conv_shift(x, d)


_shift_rows.defvjp(lambda x, d: (_conv_shift(x, d), None), lambda d, _, g: (_conv_unshift(g, d),))


def _conv_gate(ug, uv, wg0, wg1, wg2, wv0, wv1, wv2, bg, bv):
    def conv(u, w0, w1, w2, b):
        return u * w2 + _shift_rows(u, 1) * w1 + _shift_rows(u, 2) * w0 + b
    return (jax.nn.silu(conv(ug, wg0, wg1, wg2, bg)) * conv(uv, wv0, wv1, wv2, bv),)


def _rms_fn(x, gain):
    return (_rms(x, gain),)


def _softmax_rows(s):
    e = jnp.exp(s - lax.stop_gradient(jnp.max(s, axis=-1, keepdims=True)))
    return e / jnp.sum(e, axis=-1, keepdims=True)


def _xa_core(qp, k, v, q_gain):
    dh = qp.shape[1] // XA_HEADS
    outs = []
    for h in range(XA_HEADS):
        sl = slice(h * dh, (h + 1) * dh)
        p = _softmax_rows(_dot(_rms(qp[:, sl], q_gain), k[:, sl], _NT) * (dh ** -0.5))
        outs.append(_dot(p, v[:, sl], _NN))
    return (jnp.concatenate(outs, axis=1),)


def _mem_kv(mem, mem_gain, wk, wv, k_gain):
    m = _rms(mem, mem_gain)
    kp = _dot(m, wk, _NN)
    dh = kp.shape[1] // XA_HEADS
    k = jnp.concatenate([_rms(kp[:, h * dh:(h + 1) * dh], k_gain) for h in range(XA_HEADS)], axis=1)
    return k, _dot(m, wv, _NN)


def _s5_post(y, u, d):
    return (jax.nn.gelu(y + d * u),)


def _glu(a, b):
    return (a * jax.nn.sigmoid(b),)


def _lb_first(logits):
    e = jnp.exp(logits - lax.stop_gradient(jnp.max(logits, axis=0, keepdims=True)))
    return (_row_of(e, 0) / jnp.sum(e, axis=0, keepdims=True),)


def _loss_fn(y, t):
    e = y - t
    part = 0.5 * jnp.sum(e * e) / y.shape[1]
    return e * (1.0 / y.shape[1]), jnp.full((8, 128), part / (8 * 128), F32)


def _out(shape, dtype, tm):
    return (shape, dtype, (tm, shape[1]), lambda i: (i, 0))


def rms_fwd(h, gain, tm, dtype):
    return blocked_fwd(_rms_fn, [rows(h, tm), full(gain)], [_out(h.shape, dtype, tm)], h.shape[0] // tm, "rms_fwd")[0]


def rms_bwd(h, gain, dy, tm, residual):
    return blocked_bwd(_rms_fn, [rows(h, tm, 'blk'), full(gain, 'acc')], [rows(dy, tm)], h.shape[0] // tm, "rms_bwd",
                       plus=rows(residual, tm))


def _adamw_math(w, g, m, v):
    m = ADAM_B1 * m + (1.0 - ADAM_B1) * g
    v = ADAM_B2 * v + (1.0 - ADAM_B2) * jnp.square(g)
    m_hat = m / (1.0 - ADAM_B1 ** ADAM_STEP)
    v_hat = v / (1.0 - ADAM_B2 ** ADAM_STEP)
    return -ADAM_LR * (m_hat / (jnp.sqrt(v_hat) + ADAM_EPS) + ADAM_WD * w), m, v


def adamw(w, g, m, v, name):
    R = w.shape[0]
    tm = _tile(R, 256)
    assert g.shape == w.shape == m.shape == v.shape, (name, w.shape, g.shape)

    def body(w_ref, g_ref, m_ref, v_ref, d_ref, nm_ref, nv_ref):
        d_ref[...], nm_ref[...], nv_ref[...] = _adamw_math(w_ref[...], g_ref[...], m_ref[...], v_ref[...])

    spec = pl.BlockSpec((tm, w.shape[1]), lambda i: (i, 0))
    return pl.pallas_call(
        body, name=name, grid=(R // tm,), in_specs=[spec] * 4, out_specs=[spec] * 3,
        out_shape=[jax.ShapeDtypeStruct(w.shape, F32)] * 3, compiler_params=_cparams(("parallel",)),
    )(w, g, m, v)


def _index_operand(i):
    return jnp.reshape(i, (1,)).astype(jnp.int32)


def adamw_layer(w, mine, other, core, m, v, layer, bufs, name):
    H, C = mine.shape
    tm = _tile(H, 256)
    nb = H // tm
    assert w.shape[1:] == (2 * H, C) and all(b.shape == w.shape for b in bufs), (name, w.shape, mine.shape)

    def body(c_ref, w_ref, mine_ref, other_ref, m_ref, v_ref, *rest):
        g_out, d_ref, nm_ref, nv_ref = rest[-4:]
        g_ = jnp.where(pl.program_id(0) // nb == c_ref[0], mine_ref[...], other_ref[...])
        g_out[...] = g_
        d_ref[...], nm_ref[...], nv_ref[...] = _adamw_math(w_ref[...], g_, m_ref[...], v_ref[...])

    lspec = pl.BlockSpec((None, tm, C), lambda i, c: (layer, i, 0))
    half = lambda sign: pl.BlockSpec(
        (tm, C), lambda i, c: (jnp.clip(i - (c[0] if sign else 1 - c[0]) * nb, 0, nb - 1), 0))
    any_spec = pl.BlockSpec(memory_space=pl.ANY)
    grid_spec = pltpu.PrefetchScalarGridSpec(
        num_scalar_prefetch=1, grid=(2 * nb,),
        in_specs=[lspec, half(True), half(False), lspec, lspec] + [any_spec] * 4, out_specs=[lspec] * 4)
    return pl.pallas_call(
        body, name=name, grid_spec=grid_spec, out_shape=[jax.ShapeDtypeStruct(w.shape, F32)] * 4,
        input_output_aliases={6: 0, 7: 1, 8: 2, 9: 3}, compiler_params=_cparams(("parallel",)),
    )(_index_operand(core), w, mine, other, m, v, *bufs)


def add_own_half(x, core, theirs, name, out_dtype):
    nq, _, H, C = x.shape
    tm = _tile(H, 256)

    def body(c_ref, x_ref, t_ref, o_ref):
        o_ref[...] = (x_ref[...] + t_ref[...]).astype(o_ref.dtype)

    spec = pl.BlockSpec((None, tm, C), lambda q, i, c: (q, i, 0))
    grid_spec = pltpu.PrefetchScalarGridSpec(
        num_scalar_prefetch=1, grid=(nq, H // tm),
        in_specs=[pl.BlockSpec((None, None, tm, C), lambda q, i, c: (q, c[0], i, 0)), spec], out_specs=spec)
    return pl.pallas_call(
        body, name=name, grid_spec=grid_spec, out_shape=jax.ShapeDtypeStruct((nq, H, C), out_dtype),
        compiler_params=_cparams(("parallel", "parallel")),
    )(_index_operand(core), x, theirs)


def add_chips(pair, chip, got, name):
    n, R, C = got.shape
    tm = _tile(R, 256)

    def body(q_ref, *refs):
        acc = refs[0][...].astype(F32)
        for r in refs[1:-1]:
            acc = acc + r[...].astype(F32)
        refs[-1][...] = acc

    grid_spec = pltpu.PrefetchScalarGridSpec(
        num_scalar_prefetch=1, grid=(R // tm,),
        in_specs=[pl.BlockSpec((None, tm, C), lambda i, q: (q[0], i, 0))]
        + [pl.BlockSpec((None, tm, C), lambda i, q, j=j: (j, i, 0)) for j in range(n)],
        out_specs=pl.BlockSpec((tm, C), lambda i, q: (i, 0)))
    return pl.pallas_call(
        body, name=name, grid_spec=grid_spec, out_shape=jax.ShapeDtypeStruct((R, C), F32),
        compiler_params=_cparams(("parallel",)),
    )(_index_operand(chip), pair, *([got] * n))


_HBM = pl.BlockSpec(memory_space=pltpu.HBM)
N_CHIPS = 4


def _my_place():
    return lax.axis_index("x"), lax.axis_index("y"), lax.axis_index("c")


def _window(ref, axis, start, size):
    idx = [slice(None)] * len(ref.shape)
    idx[axis] = pl.ds(start, size)
    return ref.at[tuple(idx)]


def _comm_call(body, name, xs, out_shapes, n_remote, n_local, sequencer=None):
    sems = [pltpu.SemaphoreType.DMA((n_remote,)), pltpu.SemaphoreType.DMA((n_remote,)),
            pltpu.SemaphoreType.DMA((max(n_local, 1),))]
    if sequencer is None:
        return pl.pallas_call(
            body, name=name, in_specs=[_HBM] * len(xs), out_specs=[_HBM] * len(out_shapes), out_shape=out_shapes,
            scratch_shapes=sems, compiler_params=pltpu.CompilerParams(has_side_effects=True),
        )(*xs)
    peers_of, collective_id = sequencer
    hbm = pltpu.MemorySpace.HBM
    x_refs = [jax.new_ref(x, memory_space=hbm) for x in xs]
    o_refs = [jax.empty_ref(s, memory_space=hbm) for s in out_shapes]

    @pl.kernel(mesh=plsc.ScalarSubcoreMesh(axis_name="sequencer", num_cores=1), name=name, scratch_types=tuple(sems),
               compiler_params=pltpu.CompilerParams(collective_id=collective_id))
    def launch(send_sems, recv_sems, local_sems):
        peers = peers_of(*_my_place())
        barrier = pltpu.get_barrier_semaphore()
        for peer in peers:
            pl.semaphore_signal(barrier, inc=1, device_id=peer, device_id_type=MESH)
        pl.semaphore_wait(barrier, len(peers))
        body(*x_refs, *o_refs, send_sems, recv_sems, local_sems)

    launch()
    return [o[...] for o in o_refs]


def _sibling(mx, my, mc):
    return [(mx, my, 1 - mc)]


def _same_core_of_other_chips(mx, my, mc):
    return [(tx, ty, mc) for tx, ty in _other_chips(mx, my)]


def _run(copies):
    for cp in copies:
        cp.start()
    for cp in copies:
        cp.wait()


def _other_chips(mx, my):
    return [(mx ^ (j >> 1), my ^ (j & 1)) for j in (1, 2, 3)]


def chip_gather(xs, axes, name):
    n = len(xs)
    shapes, final = [], []
    for x, ax in zip(xs, axes):
        s = list(x.shape)
        if ax is None:
            shapes.append([N_CHIPS] + s)
            final.append(shapes[-1])
        elif ax < x.ndim - 1:
            shapes.append(s[:ax] + [N_CHIPS] + s[ax:])
            final.append(s[:ax] + [N_CHIPS * s[ax]] + s[ax + 1:])
        else:
            assert s[ax] % 128 == 0, (name, s)
            shapes.append(s[:ax] + [N_CHIPS * s[ax]])
            final.append(shapes[-1])

    def body(*refs):
        x_refs, o_refs = refs[:n], refs[n:2 * n]
        send_sems, recv_sems, local_sems = refs[2 * n:]
        mx, my, mc = _my_place()
        q = 2 * mx + my
        copies = []
        for i, (x_ref, o_ref, ax) in enumerate(zip(x_refs, o_refs, axes)):
            if ax is None or ax < len(x_ref.shape) - 1:
                dst = o_ref.at[(slice(None),) * (ax or 0) + (q,)]
            else:
                dst = _window(o_ref, ax, q * x_ref.shape[ax], x_ref.shape[ax])
            copies.append(pltpu.make_async_copy(x_ref, dst, local_sems.at[i]))
            for j, (tx, ty) in enumerate(_other_chips(mx, my)):
                copies.append(pltpu.make_async_remote_copy(
                    src_ref=x_ref, dst_ref=dst, send_sem=send_sems.at[3 * i + j], recv_sem=recv_sems.at[3 * i + j],
                    device_id=(tx, ty, mc), device_id_type=MESH))
        _run(copies)

    out_shapes = [jax.ShapeDtypeStruct(tuple(s), x.dtype) for s, x in zip(shapes, xs)]
    return [o.reshape(f) for o, f in zip(_comm_call(body, name, xs, out_shapes, 3 * n, n), final)]


def gather_two_level(xs, name):
    n = len(xs)
    shapes = [jax.ShapeDtypeStruct((2, N_CHIPS) + x.shape[1:], x.dtype) for x in xs]

    def body(*refs):
        x_refs, o_refs = refs[:n], refs[n:2 * n]
        send_sems, recv_sems, local_sems = refs[2 * n:]
        mx, my, mc = _my_place()
        q = 2 * mx + my
        first, local, second = [], [], []
        for i, (x_ref, o_ref) in enumerate(zip(x_refs, o_refs)):
            local.append(pltpu.make_async_copy(x_ref.at[mc], o_ref.at[mc, q], local_sems.at[i]))
            for j, (tx, ty) in enumerate(_other_chips(mx, my)):
                first.append(pltpu.make_async_remote_copy(
                    src_ref=x_ref.at[mc], dst_ref=o_ref.at[mc, q], send_sem=send_sems.at[4 * i + j],
                    recv_sem=recv_sems.at[4 * i + j], device_id=(tx, ty, mc), device_id_type=MESH))
            second.append(pltpu.make_async_remote_copy(
                src_ref=o_ref.at[mc], dst_ref=o_ref.at[mc], send_sem=send_sems.at[4 * i + 3],
                recv_sem=recv_sems.at[4 * i + 3], device_id=(mx, my, 1 - mc), device_id_type=MESH))
        for cp in local + first:
            cp.start()
        for cp in local:
            cp.wait()
        for cp in first:
            cp.wait_recv()
        _run(second)
        for cp in first:
            cp.wait_send()

    return _comm_call(body, name, xs, shapes, 4 * n, n)


def gather_two_level_sequencer(xs, name, collective_id):
    n = len(xs)
    hbm = pltpu.MemorySpace.HBM
    x_refs = [jax.new_ref(x, memory_space=hbm) for x in xs]
    o_refs = [jax.empty_ref(jax.ShapeDtypeStruct((2, N_CHIPS) + x.shape[1:], x.dtype), memory_space=hbm) for x in xs]

    @pl.kernel(mesh=plsc.ScalarSubcoreMesh(axis_name="sequencer", num_cores=1), name=name,
               scratch_types=(pltpu.SemaphoreType.DMA((4 * n,)), pltpu.SemaphoreType.DMA((4 * n,)),
                              pltpu.SemaphoreType.DMA((n,))),
               compiler_params=pltpu.CompilerParams(collective_id=collective_id))
    def launch(send_sems, recv_sems, local_sems):
        mx, my, mc = _my_place()
        peers = [(tx, ty, mc) for tx, ty in _other_chips(mx, my)] + [(mx, my, 1 - mc)]
        barrier = pltpu.get_barrier_semaphore()
        for peer in peers:
            pl.semaphore_signal(barrier, inc=1, device_id=peer, device_id_type=MESH)
        pl.semaphore_wait(barrier, len(peers))
        q = 2 * mx + my
        first, local, second = [], [], []
        for i, (x_ref, o_ref) in enumerate(zip(x_refs, o_refs)):
            local.append(pltpu.make_async_copy(x_ref.at[mc], o_ref.at[mc, q], local_sems.at[i]))
            for j, peer in enumerate(peers[:3]):
                first.append(pltpu.make_async_remote_copy(
                    src_ref=x_ref.at[mc], dst_ref=o_ref.at[mc, q], send_sem=send_sems.at[4 * i + j],
                    recv_sem=recv_sems.at[4 * i + j], device_id=peer, device_id_type=MESH))
            second.append(pltpu.make_async_remote_copy(
                src_ref=o_ref.at[mc], dst_ref=o_ref.at[mc], send_sem=send_sems.at[4 * i + 3],
                recv_sem=recv_sems.at[4 * i + 3], device_id=peers[3], device_id_type=MESH))
        for cp in local + first:
            cp.start()
        for cp in local:
            cp.wait()
        for cp in first:
            cp.wait_recv()
        _run(second)
        for cp in first:
            cp.wait_send()

    launch()
    return [o[...] for o in o_refs]


def pair_swap(xs, name, halves, collective_id=None):
    n = len(xs)
    shapes = [jax.ShapeDtypeStruct(x.shape[:1] + x.shape[2:] if halves else x.shape, x.dtype) for x in xs]

    def body(*refs):
        x_refs, o_refs = refs[:n], refs[n:2 * n]
        send_sems, recv_sems, _ = refs[2 * n:]
        mx, my, mc = _my_place()
        _run([pltpu.make_async_remote_copy(
            src_ref=x_ref.at[:, 1 - mc] if halves else x_ref, dst_ref=o_ref, send_sem=send_sems.at[i],
            recv_sem=recv_sems.at[i], device_id=(mx, my, 1 - mc), device_id_type=MESH)
            for i, (x_ref, o_ref) in enumerate(zip(x_refs, o_refs))])

    return _comm_call(body, name, xs, shapes, n, 0, None if collective_id is None else (_sibling, collective_id))


def chip_all_to_all(xs, name, collective_id=None):
    n = len(xs)
    shapes = [jax.ShapeDtypeStruct((N_CHIPS - 1,) + x.shape[1:], x.dtype) for x in xs]

    def body(*refs):
        x_refs, o_refs = refs[:n], refs[n:2 * n]
        send_sems, recv_sems, _ = refs[2 * n:]
        mx, my, mc = _my_place()
        copies = []
        for i, (x_ref, o_ref) in enumerate(zip(x_refs, o_refs)):
            for j, (tx, ty) in enumerate(_other_chips(mx, my)):
                copies.append(pltpu.make_async_remote_copy(
                    src_ref=x_ref.at[2 * tx + ty], dst_ref=o_ref.at[j], send_sem=send_sems.at[3 * i + j],
                    recv_sem=recv_sems.at[3 * i + j], device_id=(tx, ty, mc), device_id_type=MESH))
        _run(copies)

    return _comm_call(body, name, xs, shapes, 3 * n, 0,
                      None if collective_id is None else (_same_core_of_other_chips, collective_id))


WEIGHTS = ['norm_mix', 'norm_xa', 'norm_mem', 'norm_ffn', 'xa_wq', 'xa_wk', 'xa_wv', 'xa_wo', 'xa_q_norm', 'xa_k_norm',
           'ffn_w_up', 'ffn_conv_w', 'ffn_conv_b', 'ffn_w_down', 'hg_lb_logits', 'mix_w_in', 'hg_out_norm',
           'mla_q_a_norm', 'mla_w_uq', 'mla_kv_a_norm', 'mla_w_ukv', 'mla_qn_nope', 'mla_qn_rope', 'mla_kn_nope',
           'mla_kn_rope', 'mix_w_out', 's5_lam_re', 's5_lam_im', 's5_log_dt', 's5_b_re', 's5_b_im', 's5_c_re',
           's5_c_im', 's5_d', 's5_w_glu_a', 's5_w_glu_b']
INPUTS = ['x', 'mem', 'positions'] + WEIGHTS + ['loss_target'] + ['m_' + n for n in WEIGHTS] + ['v_' + n for n in WEIGHTS]
SHARD_AXIS = {'xa_wq': 1, 'xa_wk': 1, 'xa_wv': 1, 'xa_wo': 1, 'ffn_w_up': 2, 'ffn_conv_w': 2, 'ffn_w_down': 1,
              'mix_w_in': 2, 'mla_w_uq': 2, 'mla_w_ukv': 2, 'mix_w_out': 1, 's5_d': 1, 's5_w_glu_a': 1, 's5_w_glu_b': 1}
BIG = ['xa_wq', 'xa_wk', 'xa_wv', 'xa_wo', 'ffn_w_up', 'ffn_w_down', 'mix_w_in', 'mix_w_out', 's5_w_glu_a', 's5_w_glu_b']
FIRST_NEEDED = ('mix_w_in', 'mix_w_out')
SMALL_SHARDED = [n for n in WEIGHTS if n in SHARD_AXIS and n not in BIG]
REPLICATED = [n for n in WEIGHTS if n not in SHARD_AXIS]
SMALL = SMALL_SHARDED + REPLICATED
PACK_W = 1024
ROW_MULT = 16
W_IN_SHARD = IN_WIDTH // N_CHIPS
W_IN_SHARD_PAD = 640


def _pack(flats, mult=ROW_MULT):
    flat = jnp.concatenate([f.reshape(-1) for f in flats])
    unit = mult * PACK_W
    n = -(-flat.shape[0] // unit) * unit
    return jnp.pad(flat, (0, n - flat.shape[0])).reshape(n // PACK_W, PACK_W)


def _unpack(packed, shapes):
    flat, out, o = packed.reshape(-1), [], 0
    for s in shapes:
        n = math.prod(s)
        out.append(flat[o:o + n].reshape(s))
        o += n
    return out


def _rope_pad(w):
    z = jnp.zeros(w.shape[:-1] + (MLA_ROPE // 2,), w.dtype)
    return jnp.concatenate([w[..., :MLA_ROPE // 2], z, w[..., MLA_ROPE // 2:], z], axis=-1)


def _rope_unpad(g):
    return jnp.concatenate([g[..., :MLA_ROPE // 2], g[..., 64:64 + MLA_ROPE // 2]], axis=-1)


def _blockdiag_in(bb):
    nb = bb.shape[0] // S5_GB
    t = bb.reshape(nb, S5_GB, S5_STATE, S5_GROUP).transpose(0, 1, 3, 2)
    return jnp.einsum('bgmp,gh->bgmhp', t, jnp.eye(S5_GB, dtype=bb.dtype)).reshape(nb, S5_GB * S5_GROUP, S5_LANES)


def _blockdiag_in_t(dw):
    nb = dw.shape[0]
    t = jnp.einsum('bgmhp,gh->bgmp', dw.reshape(nb, S5_GB, S5_GROUP, S5_GB, S5_STATE), jnp.eye(S5_GB, dtype=dw.dtype))
    return t.transpose(0, 1, 3, 2).reshape(nb * S5_GB, S5_STATE, S5_GROUP)


def _blockdiag_out(c):
    nb = c.shape[0] // S5_GB
    t = c.reshape(nb, S5_GB, S5_GROUP, S5_STATE).transpose(0, 1, 3, 2)
    return jnp.einsum('bgpm,gh->bgphm', t, jnp.eye(S5_GB, dtype=c.dtype)).reshape(nb, S5_LANES, S5_GB * S5_GROUP)


def _blockdiag_out_t(dc):
    nb = dc.shape[0]
    t = jnp.einsum('bgphm,gh->bgpm', dc.reshape(nb, S5_GB, S5_STATE, S5_GB, S5_GROUP), jnp.eye(S5_GB, dtype=dc.dtype))
    return t.transpose(0, 1, 3, 2).reshape(nb * S5_GB, S5_GROUP, S5_STATE)


def _gather_weights(P):
    def halves(x):
        return x if x.shape[0] == 2 else x.reshape(2, x.shape[1] // 2, x.shape[2])

    now = [n for n in BIG if n in FIRST_NEEDED]
    later = [n for n in BIG if n not in FIRST_NEEDED]
    xs = [halves(P[n].astype(BF16)) for n in now] + [halves(_pack([P[n] for n in SMALL_SHARDED], 2 * ROW_MULT)[None])]
    got = gather_two_level(xs, "gather_weights")
    got, xs_later = lax.optimization_barrier((got, [halves(P[n].astype(BF16)) for n in later]))
    got_later = gather_two_level_sequencer(xs_later, "gather_weights_later", 1)
    full_w = {}
    for n, g in list(zip(now, got[:-1])) + list(zip(later, got_later)):
        two_layers, by_rows = P[n].shape[0] == 2, SHARD_AXIS[n] == 1
        if two_layers and by_rows:
            full_w[n] = g.reshape(2, N_CHIPS * g.shape[2], g.shape[3])
        elif two_layers:
            full_w[n] = g.transpose(0, 2, 1, 3).reshape(2, g.shape[2], N_CHIPS * g.shape[3])
        elif by_rows:
            full_w[n] = g.transpose(1, 0, 2, 3).reshape(1, 2 * N_CHIPS * g.shape[2], g.shape[3])
        else:
            full_w[n] = g.transpose(0, 2, 1, 3).reshape(1, 2 * g.shape[2], N_CHIPS * g.shape[3])
    small = got[-1].transpose(1, 0, 2, 3).reshape(N_CHIPS, -1, PACK_W)
    per_chip = [_unpack(small[q], [P[n].shape for n in SMALL_SHARDED]) for q in range(N_CHIPS)]
    for i, n in enumerate(SMALL_SHARDED):
        full_w[n] = jnp.concatenate([per_chip[q][i] for q in range(N_CHIPS)], axis=SHARD_AXIS[n])
    return full_w


def _reduce_batch(items, small, P, results, tag, ids):
    mx, my, mc = _my_place()
    q = 2 * mx + my
    ids = ids or {}
    names = [f"{n}_{lyr}" for n, lyr, _ in items] + (['small'] if small is not None else [])
    xs = [g.reshape(N_CHIPS, 2, g.shape[1] // 2, g.shape[2]) for g in [g for _, _, g in items] + ([small] if small is not None else [])]
    def after(vals, tie):
        return (vals, None) if tie is None else lax.optimization_barrier((vals, tie))

    theirs = pair_swap(xs, "grads_pair_swap_" + tag, True, ids.get('swap'))
    theirs, tie = after(theirs, (yield None))
    pair = [add_own_half(x, mc, t, "grads_pair_sum_" + n, F32 if n == 'small' else BF16) for x, t, n in zip(xs, theirs, names)]
    got = chip_all_to_all(pair, "grads_chip_all_to_all_" + tag, ids.get('a2a'))
    got, tie = after(got, (yield pair[-1] if tie is None else tie))
    summed = [add_chips(p, q, g, "grads_chip_sum_" + n) for p, g, n in zip(pair, got, names)]
    other = pair_swap(summed, "grads_pair_join_" + tag, False, ids.get('join'))
    other, tie = after(other, (yield summed[-1] if tie is None else tie))
    for (n, lyr, _), s, o in zip(items, summed, other):
        if n == 'mix_w_in':
            s, o = s[:, :W_IN_SHARD], o[:, :W_IN_SHARD]
        view = (P[n].shape[0], 2 * s.shape[0], P[n].shape[-1])
        bufs = results.get(n) or [lax.empty(view, F32) for _ in range(4)]
        results[n] = adamw_layer(P[n].reshape(view), s, o, mc, P['m_' + n].reshape(view), P['v_' + n].reshape(view), lyr,
                                 bufs, f"adamw_{n}_{lyr}")
    if small is not None:
        results['small_quarter'] = lax.cond(mc == 0, lambda a, b: jnp.concatenate([a, b], axis=0),
                                            lambda a, b: jnp.concatenate([b, a], axis=0), summed[-1], other[-1])
    yield tie


def _update_small(small_quarter, GS, P):
    mx, my, _ = _my_place()
    q = 2 * mx + my
    small_sum = chip_gather([small_quarter], [0], "grads_small_gather")[0]
    g_small = dict(zip(SMALL, _unpack(small_sum, [GS[n].shape for n in SMALL])))
    for n in SMALL_SHARDED:
        s = P[n].shape[SHARD_AXIS[n]]
        g_small[n] = lax.dynamic_slice_in_dim(g_small[n], q * s, s, axis=SHARD_AXIS[n])
    grad, delta, new_m, new_v = {}, {}, {}, {}
    packed = lambda prefix: _pack([P[prefix + n] for n in SMALL])
    d, m_, v_ = adamw(packed(''), _pack([g_small[n] for n in SMALL]), packed('m_'), packed('v_'), "adamw_small")
    shapes = [P[n].shape for n in SMALL]
    grad.update(g_small)
    for out, pk in ((delta, d), (new_m, m_), (new_v, v_)):
        out.update(zip(SMALL, _unpack(pk, shapes)))
    return grad, delta, new_m, new_v


def _row(v):
    return v.reshape(1, -1)


def _xattn_fwd(h, mem, W, lyr, tm):
    g_xa, g_mem = _row(W['norm_xa'][lyr]), _row(W['norm_mem'][lyr])
    g_q, g_k = _row(W['xa_q_norm'][lyr]), _row(W['xa_k_norm'][lyr])
    wq, wk, wv, wo = (W[n][lyr] for n in ('xa_wq', 'xa_wk', 'xa_wv', 'xa_wo'))
    L, D = h.shape
    M = mem.shape[0]
    hx = rms_fwd(h, g_xa, tm, MXU_DTYPE)
    qp = matmul(hx, wq, name="xa_q")
    kv_opds = [full(mem), full(g_mem), full(wk), full(wv), full(g_k)]
    k, v = blocked_fwd(_mem_kv, kv_opds, [((M, D), F32, (M, D), lambda i: (0, 0))] * 2, 1, "xa_mem_kv")
    o = blocked_fwd(_xa_core, [rows(qp, tm), full(k), full(v), full(g_q)], [_out((L, D), MXU_DTYPE, tm)], L // tm,
                    "xa_core")[0]
    out = matmul(o, wo, add=h, name="xa_o")
    return out, (h, hx, qp, k, v, o)


def _xattn_bwd(dout, saved, mem, W, lyr, tm):
    h, hx, qp, k, v, o = saved
    g_xa, g_mem = _row(W['norm_xa'][lyr]), _row(W['norm_mem'][lyr])
    g_q, g_k = _row(W['xa_q_norm'][lyr]), _row(W['xa_k_norm'][lyr])
    wq, wk, wv, wo = (W[n][lyr] for n in ('xa_wq', 'xa_wk', 'xa_wv', 'xa_wo'))
    L = h.shape[0]
    do = matmul(dout, wo, "nt", name="xa_do")
    d_wo = matmul(o, dout, "tn", name="xa_dwo")
    dqp, dk, dv, d_gq = blocked_bwd(_xa_core, [rows(qp, tm, 'blk'), full(k, 'acc'), full(v, 'acc'), full(g_q, 'acc')],
                                    [rows(do, tm)], L // tm, "xa_core_bwd")
    d_wq = matmul(hx, dqp, "tn", name="xa_dwq")
    dhx = matmul(dqp, wq, "nt", name="xa_dhx")
    dh, d_gxa = rms_bwd(h, g_xa, dhx, tm, dout)
    d_gmem, d_wk, d_wv, d_gk = blocked_bwd(
        _mem_kv, [full(mem), full(g_mem, 'acc'), full(wk, 'acc'), full(wv, 'acc'), full(g_k, 'acc')],
        [full(dk), full(dv)], 1, "xa_mem_kv_bwd")
    by_chip = lambda g: g.reshape(N_CHIPS, g.shape[0] // N_CHIPS, g.shape[1])
    grads = {'norm_xa': d_gxa, 'norm_mem': d_gmem, 'xa_q_norm': d_gq, 'xa_k_norm': d_gk,
             'xa_wq': by_chip(d_wq), 'xa_wk': by_chip(d_wk), 'xa_wv': by_chip(d_wv), 'xa_wo': by_chip(d_wo)}
    return dh, grads


def _conv_params(W, lyr):
    cw, cb = W['ffn_conv_w'][lyr], W['ffn_conv_b'][lyr]
    F = cw.shape[1] // 2
    return [cw[0:1, :F], cw[1:2, :F], cw[2:3, :F], cw[0:1, F:], cw[1:2, F:], cw[2:3, F:], _row(cb[:F]), _row(cb[F:])]


def _ffn_fwd(h, W, lyr, tm):
    L, D = h.shape
    w_up, w_down = W['ffn_w_up'][lyr], W['ffn_w_down'][lyr]
    F = w_down.shape[0]
    hf = rms_fwd(h, _row(W['norm_ffn'][lyr]), tm, MXU_DTYPE)
    ug = matmul(hf, w_up[:, :F], name="ffn_up_gate")
    uv = matmul(hf, w_up[:, F:], name="ffn_up_value")
    opds = [cols(ug, 128), cols(uv, 128)] + [cols(p, 128) for p in _conv_params(W, lyr)]
    a = blocked_fwd(_conv_gate, opds, [((L, F), MXU_DTYPE, (L, 128), lambda j: (0, j))], F // 128, "ffn_conv_gate")[0]
    out = matmul(a, w_down, add=h, name="ffn_down")
    return out, (h, hf, ug, uv, a)


def _ffn_bwd(dout, saved, W, lyr, tm):
    h, hf, ug, uv, a = saved
    w_up, w_down = W['ffn_w_up'][lyr], W['ffn_w_down'][lyr]
    F = w_down.shape[0]
    da = matmul(dout, w_down, "nt", name="ffn_da")
    d_wdown = matmul(a, dout, "tn", name="ffn_dwdown")
    opds = [cols(ug, 128, 'blk'), cols(uv, 128, 'blk')] + [cols(p, 128, 'blk') for p in _conv_params(W, lyr)]
    gs = blocked_bwd(_conv_gate, opds, [cols(da, 128)], F // 128, "ffn_conv_gate_bwd")
    dug, duv = gs[0], gs[1]
    d_cw = jnp.concatenate([jnp.concatenate(gs[2:5], axis=0), jnp.concatenate(gs[5:8], axis=0)], axis=1)
    d_cb = jnp.concatenate([gs[8], gs[9]], axis=1)[0]
    half = N_CHIPS // 2
    d_wup = lax.empty((N_CHIPS, hf.shape[1], w_up.shape[1] // N_CHIPS), F32)
    d_wup = matmul(hf, dug, "tn", name="ffn_dwup_gate", into=(d_wup, 0), col_blocks=half)
    d_wup = matmul(hf, duv, "tn", name="ffn_dwup_value", into=(d_wup, half), col_blocks=half)
    dhf = matmul(dug, w_up[:, :F], "nt", name="ffn_dhf_gate")
    dhf = matmul(duv, w_up[:, F:], "nt", add=dhf, name="ffn_dhf_value")
    dh, d_g = rms_bwd(h, _row(W['norm_ffn'][lyr]), dhf, tm, dout)
    d_wdown = d_wdown.reshape(N_CHIPS, F // N_CHIPS, d_wdown.shape[1])
    return dh, {'norm_ffn': d_g, 'ffn_w_up': d_wup, 'ffn_conv_w': d_cw, 'ffn_conv_b': d_cb, 'ffn_w_down': d_wdown}


def _mla_params(W):
    w_uq = W['mla_w_uq'][0].reshape(MLA_Q_RANK, MLA_HEADS, MLA_QK)
    w_uq = jnp.concatenate([w_uq[..., :MLA_NOPE], _rope_pad(w_uq[..., MLA_NOPE:])], axis=-1)
    w_ukv = W['mla_w_ukv'][0].reshape(MLA_KV_RANK, MLA_HEADS, MLA_NOPE + MLA_V)
    w_ukv = jnp.concatenate([w_ukv[..., :MLA_NOPE].reshape(MLA_KV_RANK, -1), w_ukv[..., MLA_NOPE:].reshape(MLA_KV_RANK, -1)],
                            axis=1)
    return [_row(W['mla_q_a_norm'][0]), w_uq.reshape(MLA_Q_RANK, MLA_HEADS * MLA_DK), _row(W['mla_kv_a_norm'][0]), w_ukv,
            _row(W['mla_qn_nope'][0]), _row(_rope_pad(W['mla_qn_rope'][0])), _row(W['mla_kn_nope'][0]),
            _row(_rope_pad(W['mla_kn_rope'][0]))]


def _w_in_padded(W):
    w = W['mix_w_in'][0]
    return jnp.concatenate([w[:, :IN_WIDTH - MLA_ROPE], _rope_pad(w[:, IN_WIDTH - MLA_ROPE:])], axis=1)


def _mixer0_fwd(h, W, cos_p, sin_p, tm):
    L = h.shape[0]
    t = min(ATTN_ROWS, L // ATTN_WIDE)
    hn = rms_fwd(h, _row(W['norm_mix'][0]), tm, MXU_DTYPE)
    proj = matmul(hn, _w_in_padded(W), name="mix_in")
    logits = W['hg_lb_logits']
    lb = blocked_fwd(_lb_first, [full(logits)], [((1, HG_WIDTH), F32, (1, HG_WIDTH), lambda i: (0, 0))], 1, "hg_lb")[0]
    gain = _row(W['hg_out_norm'][0])
    o_hg, states = hgrn2_fwd(proj, lb, gain)
    mp = _mla_params(W)
    q, k, v = mla_prep_fwd(proj, cos_p, sin_p, mp, tm)
    scale = MLA_QK ** -0.5
    o_mla, lse = attn_fwd(q, k, v, scale, t)
    w_out = W['mix_w_out'][0]
    out = matmul(o_hg, w_out[:HG_WIDTH], add=h, name="mix_out_hg")
    out = matmul(o_mla, w_out[HG_WIDTH:], add=out, name="mix_out_mla")
    return out, (h, hn, proj, lb, o_hg, states, q, k, v, o_mla, lse)


def _mixer0_bwd(dout, saved, W, cos_p, sin_p, tm, part_way=None):
    h, hn, proj, lb, o_hg, states, q, k, v, o_mla, lse = saved
    L = h.shape[0]
    t = min(ATTN_ROWS, L // ATTN_WIDE)
    scale = MLA_QK ** -0.5
    w_out = W['mix_w_out'][0]
    gain = _row(W['hg_out_norm'][0])
    do_hg = matmul(dout, w_out[:HG_WIDTH], "nt", name="mix_do_hg")
    do_mla = matmul(dout, w_out[HG_WIDTH:], "nt", name="mix_do_mla")
    d_wout = jnp.concatenate([matmul(o_hg, dout, "tn", name="mix_dwout_hg"), matmul(o_mla, dout, "tn", name="mix_dwout_mla")],
                             axis=0)
    dq = attn_bwd_dq(q, k, v, o_mla, lse, do_mla, scale, t)
    if part_way is not None:
        dq = part_way(dq)
    dk, dv = attn_bwd_dkv(q, k, v, o_mla, lse, do_mla, scale, t)
    mp = _mla_params(W)
    d_mla, d_qa, d_wuq, d_kva, d_wukv, d_qnn, d_qnr, d_knn, d_knr = mla_prep_bwd(proj, cos_p, sin_p, mp, dq, dk, dv, tm)
    d_hg, d_lb, d_gain = hgrn2_bwd(proj, lb, gain, states, do_hg)
    w_in, n_hg = _w_in_padded(W), 4 * HG_WIDTH
    d_win = jnp.concatenate([matmul(hn, d_hg, "tn", name="mix_dwin_hg"), matmul(hn, d_mla, "tn", name="mix_dwin_mla")], axis=1)
    dhn = matmul(d_hg, w_in[:, :n_hg], "nt", name="mix_dhn_hg")
    dhn = matmul(d_mla, w_in[:, n_hg:], "nt", add=dhn, name="mix_dhn_mla")
    dh, d_g = rms_bwd(h, _row(W['norm_mix'][0]), dhn, tm, dout)
    logits = W['hg_lb_logits']
    d_logits = blocked_bwd(_lb_first, [full(logits, 'acc')], [full(d_lb)], 1, "hg_lb_bwd")[0]
    d_wuq = d_wuq.reshape(MLA_Q_RANK, MLA_HEADS, MLA_DK)
    d_wuq = jnp.concatenate([d_wuq[..., :MLA_NOPE], _rope_unpad(d_wuq[..., MLA_NOPE:])], axis=-1)
    hw = MLA_HEADS * MLA_NOPE
    d_wukv = jnp.concatenate([d_wukv[:, :hw].reshape(MLA_KV_RANK, MLA_HEADS, MLA_NOPE),
                              d_wukv[:, hw:].reshape(MLA_KV_RANK, MLA_HEADS, MLA_V)], axis=-1)
    d_win = jnp.concatenate([d_win[:, :IN_WIDTH - MLA_ROPE], _rope_unpad(d_win[:, IN_WIDTH - MLA_ROPE:])], axis=1)
    d_win = d_win.reshape(d_win.shape[0], N_CHIPS, W_IN_SHARD).transpose(1, 0, 2)
    d_win = jnp.pad(d_win, ((0, 0), (0, 0), (0, W_IN_SHARD_PAD - W_IN_SHARD)))
    d_wout = d_wout.reshape(N_CHIPS, d_wout.shape[0] // N_CHIPS, d_wout.shape[1])
    grads = {'norm_mix': d_g, 'hg_lb_logits': d_logits, 'mix_w_in': d_win, 'hg_out_norm': d_gain,
             'mla_q_a_norm': d_qa, 'mla_w_uq': d_wuq.reshape(1, MLA_Q_RANK, -1), 'mla_kv_a_norm': d_kva,
             'mla_w_ukv': d_wukv.reshape(1, MLA_KV_RANK, -1), 'mla_qn_nope': d_qnn, 'mla_qn_rope': _rope_unpad(d_qnr),
             'mla_kn_nope': d_knn, 'mla_kn_rope': _rope_unpad(d_knr), 'mix_w_out': d_wout}
    return dh,grads


def _s5_inputs(W):
    G = W['s5_lam_re'].shape[1]
    return [W['s5_lam_re'][0], W['s5_lam_im'][0], W['s5_log_dt'][0].reshape(G, 1),
            W['s5_b_re'][0].reshape(G, -1), W['s5_b_im'][0].reshape(G, -1)]


def _mixer1_fwd(h, W, tm):
    L, D = h.shape
    u = rms_fwd(h, _row(W['norm_mix'][1]), tm, F32)
    di = _s5_inputs(W)
    G = di[0].shape[0]
    sq, wide = ((G, S5_STATE), F32, (G, S5_STATE), lambda i: (0, 0)), ((G, S5_STATE * S5_GROUP), F32, (G, S5_STATE * S5_GROUP), lambda i: (0, 0))
    ar, ai, bbr, bbi = blocked_fwd(_s5_discretize, [full(a) for a in di], [sq, sq, wide, wide], 1, "s5_discretize")
    nb = G // S5_GB
    core = (_blockdiag_in(bbr.reshape(G, S5_STATE, S5_GROUP)), _blockdiag_in(bbi.reshape(G, S5_STATE, S5_GROUP)),
            ar.reshape(nb, 1, S5_LANES), ai.reshape(nb, 1, S5_LANES),
            _blockdiag_out(W['s5_c_re'][0]), _blockdiag_out(W['s5_c_im'][0]))
    y = s5_fwd(u, *core)
    d = W['s5_d']
    y2 = blocked_fwd(_s5_post, [rows(y, tm), rows(u, tm), full(d)], [_out((L, D), MXU_DTYPE, tm)], L // tm, "s5_post")[0]
    w_ab = jnp.concatenate([W['s5_w_glu_a'][0], W['s5_w_glu_b'][0]], axis=1)
    ab = matmul(y2, w_ab, name="s5_glu_in")
    mix = blocked_fwd(_glu, [rows(ab, tm, col=0, width=D), rows(ab, tm, col=1, width=D)], [_out((L, D), F32, tm)], L // tm,
                      "s5_glu")[0]
    return h + mix, (h, u, core, y, y2, ab)


def _mixer1_bwd(dout, saved, W, tm):
    h, u, core, y, y2, ab = saved
    L, D = h.shape
    da, db = blocked_bwd(_glu, [rows(ab, tm, 'blk', col=0, width=D), rows(ab, tm, 'blk', col=1, width=D)], [rows(dout, tm)],
                         L // tm, "s5_glu_bwd")
    w_a, w_b = W['s5_w_glu_a'][0], W['s5_w_glu_b'][0]
    dy2 = matmul(da, w_a, "nt", name="s5_dy2_a")
    dy2 = matmul(db, w_b, "nt", add=dy2, name="s5_dy2_b")
    d_wa = matmul(y2, da, "tn", name="s5_dwa")
    d_wb = matmul(y2, db, "tn", name="s5_dwb")
    d = W['s5_d']
    dy, du_skip, d_d = blocked_bwd(_s5_post, [rows(y, tm, 'blk'), rows(u, tm, 'blk'), full(d, 'acc')], [rows(dy2, tm)], L // tm,
                                   "s5_post_bwd")
    du, dwr, dwi, dar, dai, dcr, dci = s5_bwd(u, *core, dy, min(256, L))
    di = _s5_inputs(W)
    G = di[0].shape[0]
    cts = [dar.reshape(G, S5_STATE), dai.reshape(G, S5_STATE), _blockdiag_in_t(dwr).reshape(G, -1), _blockdiag_in_t(dwi).reshape(G, -1)]
    d_lr, d_li, d_ldt, d_br, d_bi = blocked_bwd(_s5_discretize, [full(a, 'acc') for a in di], [full(c) for c in cts], 1,
                                                "s5_discretize_bwd")
    dh, d_g = rms_bwd(h, _row(W['norm_mix'][1]), du + du_skip, tm, dout)
    bshape = W['s5_b_re'].shape
    grads = {'norm_mix': d_g, 's5_lam_re': d_lr[None], 's5_lam_im': d_li[None], 's5_log_dt': d_ldt.reshape(1, G),
             's5_b_re': d_br.reshape(bshape), 's5_b_im': d_bi.reshape(bshape), 's5_c_re': _blockdiag_out_t(dcr)[None],
             's5_c_im': _blockdiag_out_t(dci)[None], 's5_d': d_d, 's5_w_glu_a': d_wa.reshape(N_CHIPS, -1, D), 's5_w_glu_b': d_wb.reshape(N_CHIPS, -1, D)}
    return dh,grads


def kernel(x, mem, positions, norm_mix, norm_xa, norm_mem, norm_ffn, xa_wq, xa_wk, xa_wv, xa_wo, xa_q_norm, xa_k_norm, ffn_w_up, ffn_conv_w, ffn_conv_b, ffn_w_down, hg_lb_logits, mix_w_in, hg_out_norm, mla_q_a_norm, mla_w_uq, mla_kv_a_norm, mla_w_ukv, mla_qn_nope, mla_qn_rope, mla_kn_nope, mla_kn_rope, mix_w_out, s5_lam_re, s5_lam_im, s5_log_dt, s5_b_re, s5_b_im, s5_c_re, s5_c_im, s5_d, s5_w_glu_a, s5_w_glu_b, loss_target, m_norm_mix, m_norm_xa, m_norm_mem, m_norm_ffn, m_xa_wq, m_xa_wk, m_xa_wv, m_xa_wo, m_xa_q_norm, m_xa_k_norm, m_ffn_w_up, m_ffn_conv_w, m_ffn_conv_b, m_ffn_w_down, m_hg_lb_logits, m_mix_w_in, m_hg_out_norm, m_mla_q_a_norm, m_mla_w_uq, m_mla_kv_a_norm, m_mla_w_ukv, m_mla_qn_nope, m_mla_qn_rope, m_mla_kn_nope, m_mla_kn_rope, m_mix_w_out, m_s5_lam_re, m_s5_lam_im, m_s5_log_dt, m_s5_b_re, m_s5_b_im, m_s5_c_re, m_s5_c_im, m_s5_d, m_s5_w_glu_a, m_s5_w_glu_b, v_norm_mix, v_norm_xa, v_norm_mem, v_norm_ffn, v_xa_wq, v_xa_wk, v_xa_wv, v_xa_wo, v_xa_q_norm, v_xa_k_norm, v_ffn_w_up, v_ffn_conv_w, v_ffn_conv_b, v_ffn_w_down, v_hg_lb_logits, v_mix_w_in, v_hg_out_norm, v_mla_q_a_norm, v_mla_w_uq, v_mla_kv_a_norm, v_mla_w_ukv, v_mla_qn_nope, v_mla_qn_rope, v_mla_kn_nope, v_mla_kn_rope, v_mix_w_out, v_s5_lam_re, v_s5_lam_im, v_s5_log_dt, v_s5_b_re, v_s5_b_im, v_s5_c_re, v_s5_c_im, v_s5_d, v_s5_w_glu_a, v_s5_w_glu_b):
    P = dict(locals())
    assert sorted(P) == sorted(INPUTS) and norm_mix.shape[0] == 2 and mix_w_in.shape[0] == 1
    x, mem, target = P['x'][0], P['mem'][0], P['loss_target'][0]
    L, D = x.shape
    tm = min(256, L)

    W = {n: P[n] for n in REPLICATED}
    W.update(_gather_weights(P))

    inv_freq = 1.0 / (ROPE_BASE ** (jnp.arange(0, MLA_ROPE, 2, dtype=F32) / MLA_ROPE))
    ang = P['positions'][0].astype(F32)[:, None] * inv_freq
    cos, sin, z = jnp.cos(ang), jnp.sin(ang), jnp.zeros_like(ang)
    cos_p = jnp.concatenate([cos, z, cos, z], axis=1)
    sin_p = jnp.concatenate([-sin, z, sin, z], axis=1)

    h, s_mix0 = _mixer0_fwd(x, W, cos_p, sin_p, tm)
    h, s_xa0 = _xattn_fwd(h, mem, W, 0, tm)
    h, s_ffn0 = _ffn_fwd(h, W, 0, tm)
    h, s_mix1 = _mixer1_fwd(h, W, tm)
    h, s_xa1 = _xattn_fwd(h, mem, W, 1, tm)
    h, s_ffn1 = _ffn_fwd(h, W, 1, tm)
    n = L // tm
    dh, parts = blocked_fwd(_loss_fn, [rows(h, tm), rows(target, tm)],
                            [_out((L, D), F32, tm), ((n * 8, 128), F32, (8, 128), lambda i: (i, 0))], n, "loss")
    loss = lax.psum(jnp.sum(parts), ("x", "y", "c"))

    layered = {}

    def collect(g, lyr):
        for k_, v_ in g.items():
            layered.setdefault(k_, {})[lyr] = v_

    results = {}

    def big_items(lyr, names):
        return [(n_, 0 if P[n_].shape[0] == 1 else lyr, layered[n_][lyr]) for n_ in names]

    per_layer = [n_ for n_ in BIG if P[n_].shape[0] == 2]
    second_mixer = ['s5_w_glu_a', 's5_w_glu_b']
    first_mixer = ['mix_w_in', 'mix_w_out']
    assert sorted(per_layer + second_mixer + first_mixer) == sorted(BIG)

    dh, g = _ffn_bwd(dh, s_ffn1, W, 1, tm)
    collect(g, 1)
    dh, g = _xattn_bwd(dh, s_xa1, mem, W, 1, tm)
    collect(g, 1)
    dh, g = _mixer1_bwd(dh, s_mix1, W, tm)
    collect(g, 1)
    late = _reduce_batch(big_items(1, per_layer + second_mixer), None, P, results, "late_layer", {'swap': 2, 'a2a': 3, 'join': 4})
    next(late)
    dh, g = _ffn_bwd(dh, s_ffn0, W, 0, tm)
    collect(g, 0)
    dh = late.send(dh)
    dh, g = _xattn_bwd(dh, s_xa0, mem, W, 0, tm)
    collect(g, 0)
    mid = _reduce_batch(big_items(0, per_layer), None, P, results, "first_layer", {'swap': 5, 'a2a': 6, 'join': 7})
    next(mid)
    dx, g = _mixer0_bwd(dh, s_mix0, W, cos_p, sin_p, tm, part_way=mid.send)
    collect(g, 0)

    GS = {}
    for name in SMALL:
        by_layer = [layered[name][lyr] for lyr in sorted(layered[name])]
        full_shape = W[name].shape
        GS[name] = (by_layer[0].reshape(full_shape) if len(by_layer) == 1
                    else jnp.stack([g_.reshape(full_shape[1:]) for g_ in by_layer]))
    small = _pack([GS[n_] for n_ in SMALL], 2 * N_CHIPS * ROW_MULT).reshape(N_CHIPS, -1, PACK_W)
    dx = late.send(dx)
    dx = mid.send(dx)
    last = _reduce_batch(big_items(0, first_mixer), small, P, results, "first_mixer", {'swap': 8, 'a2a': 9, 'join': 10})
    next(last)
    late.send(None)
    last_pair_sum = last.send(None)
    mid.send(last_pair_sum)
    mid_name = per_layer[-1]
    results[mid_name] = list(last.send(list(results[mid_name])))
    last.send(None)
    outs = list(_update_small(results['small_quarter'], GS, P))
    for k_ in range(4):
        outs[k_].update({n_: results[n_][k_].reshape(P[n_].shape) for n_ in BIG})
    return (loss, dx[None], *[d[n_] for d in outs for n_ in WEIGHTS])
```

```python
import functools
import math

import jax
import jax.numpy as jnp
import numpy as np
from jax import lax
from jax.experimental import pallas as pl
from jax.experimental.pallas import tpu as pltpu
from jax.experimental.pallas import tpu_sc as plsc

F32 = jnp.float32
BF16 = jnp.bfloat16
MXU_DTYPE = BF16
HI = lax.Precision.HIGHEST
V7X_VMEM_LIMIT_BYTES = 56 * 1024 * 1024
EPS = 1e-6
MESH = pl.DeviceIdType.MESH

HG_HEADS, HG_DIM = 4, 128
HG_WIDTH = HG_HEADS * HG_DIM
HG_SUB = 32
HG_BLOCK = 64
MLA_HEADS, MLA_Q_RANK, MLA_KV_RANK = 4, 256, 128
MLA_NOPE, MLA_ROPE, MLA_V = 128, 64, 128
MLA_QK = MLA_NOPE + MLA_ROPE
MLA_DK = 256
ROPE_BASE = 10000.0
IN_WIDTH = 4 * HG_WIDTH + MLA_Q_RANK + MLA_KV_RANK + MLA_ROPE
IN_PAD = 4 * HG_WIDTH + MLA_Q_RANK + MLA_KV_RANK + 128
S5_GROUP, S5_STATE = 16, 64
S5_GB = 8
DT_MIN, DT_MAX = 1e-3, 1e-1
XA_HEADS = 4
CONV_W = 3
ADAM_LR, ADAM_B1, ADAM_B2, ADAM_EPS, ADAM_WD, ADAM_STEP = 0.001, 0.9, 0.999, 1e-08, 0.01, 10


def _cparams(sem):
    return pltpu.CompilerParams(dimension_semantics=sem, vmem_limit_bytes=V7X_VMEM_LIMIT_BYTES)


class Opd:
    def __init__(self, arr, block, imap, grad=None, gshape=None, gimap=None):
        self.arr, self.block, self.imap, self.grad = arr, block, imap, grad
        self.gshape = arr.shape if gshape is None else gshape
        self.gimap = imap if gimap is None else gimap

    def spec(self):
        return pl.BlockSpec(self.block, self.imap)

    def gspec(self):
        return pl.BlockSpec(self.block, self.gimap)


def rows(arr, tm, grad=None, col=0, width=None):
    width = arr.shape[1] if width is None else width
    return Opd(arr, (tm, width), lambda i, c=col: (i, c), grad, (arr.shape[0], width), lambda i: (i, 0))


def cols(arr, tn, grad=None):
    return Opd(arr, (arr.shape[0], tn), lambda j: (0, j), grad)


def full(arr, grad=None):
    return Opd(arr, arr.shape, lambda i: (0, 0), grad)


def _load(ref):
    v = ref[...]
    return v.astype(F32) if jnp.issubdtype(v.dtype, jnp.floating) else v


def blocked_fwd(f, opds, outs, n, name):
    n_in = len(opds)

    def body(*refs):
        ys = f(*[_load(r) for r in refs[:n_in]])
        for r, y in zip(refs[n_in:], ys):
            r[...] = y.astype(r.dtype)

    res = pl.pallas_call(
        body, name=name, grid=(n,),
        in_specs=[o.spec() for o in opds],
        out_specs=[pl.BlockSpec(b, m) for (_, _, b, m) in outs],
        out_shape=[jax.ShapeDtypeStruct(s, d) for (s, d, _, _) in outs],
        compiler_params=_cparams(("parallel",)),
    )(*[o.arr for o in opds])
    return res


def blocked_bwd(f, opds, dys, n, name, plus=None):
    n_in, n_dy = len(opds), len(dys)
    diff = [i for i, o in enumerate(opds) if o.grad]
    extra = [] if plus is None else [plus]

    def body(*refs):
        vals = [_load(r) for r in refs[:n_in]]

        def fd(*dv):
            allv = list(vals)
            for i, v in zip(diff, dv):
                allv[i] = v
            return tuple(f(*allv))

        ys, vjp = jax.vjp(fd, *[vals[i] for i in diff])
        cts = tuple(_load(r).astype(y.dtype) for r, y in zip(refs[n_in:n_in + n_dy], ys))
        gs = list(vjp(cts))
        if extra:
            gs[0] = gs[0] + _load(refs[n_in + n_dy])
        for r, g, i in zip(refs[n_in + n_dy + len(extra):], gs, diff):
            if opds[i].grad == 'acc':
                @pl.when(pl.program_id(0) == 0)
                def _(r=r):
                    r[...] = jnp.zeros(r.shape, r.dtype)
                r[...] += g.astype(r.dtype)
            else:
                r[...] = g.astype(r.dtype)

    any_acc = any(opds[i].grad == 'acc' for i in diff)
    res = pl.pallas_call(
        body, name=name, grid=(n,),
        in_specs=[o.spec() for o in opds + dys + extra],
        out_specs=[opds[i].gspec() for i in diff],
        out_shape=[jax.ShapeDtypeStruct(opds[i].gshape, F32) for i in diff],
        compiler_params=_cparams(("arbitrary" if any_acc else "parallel",)),
    )(*[o.arr for o in opds + dys + extra])
    return res


def _tile(dim, want):
    for t in range(want - want % 16, 0, -16):
        if dim % t == 0:
            return t
    assert dim <= want, (dim, want)
    return dim


MATMUL_VMEM_BUDGET = 40 * 1024 * 1024
MATMUL_ROWS = 512


def _widest(N, fits):
    for t in range(N - N % 128, 0, -128):
        if N % t == 0 and fits(t):
            return t
    return N


def matmul(a, b, mode="nn", out_dtype=F32, add=None, name="matmul", into=None, col_blocks=None):
    sa, sb, so = a.dtype.itemsize, b.dtype.itemsize, jnp.dtype(out_dtype).itemsize
    has_add = add is not None
    if mode == "tn":
        (K, M), (K2, N) = a.shape, b.shape
        assert K == K2 and not has_add and out_dtype == F32, (a.shape, b.shape)
        tk = _tile(K, MATMUL_ROWS)
        tn = _widest(N, lambda t: 2 * (tk * M * sa + tk * t * sb + M * t * 4) <= MATMUL_VMEM_BUDGET)
        extra, alias = [], {}
        if into is not None:
            buf, lead = into[0], tuple(into[1:])
            if col_blocks is not None:
                assert N % col_blocks == 0 and (N // col_blocks) % 128 == 0 and tn >= N // col_blocks, (N, col_blocks, tn)
                tn = N // col_blocks
                assert buf.shape[len(lead):] == (M, tn), (buf.shape, lead, M, tn)
                out_spec = pl.BlockSpec((None,) * len(lead) + (M, tn), lambda j, k: lead[:-1] + (lead[-1] + j, 0, 0))
            else:
                assert buf.shape[len(lead):] == (M, N), (buf.shape, lead, M, N)
                out_spec = pl.BlockSpec((None,) * len(lead) + (M, tn), lambda j, k: lead + (0, j))
            out_shape = jax.ShapeDtypeStruct(buf.shape, F32)
            extra, alias = [buf], {2: 0}
        else:
            assert col_blocks is None
            out_spec = pl.BlockSpec((M, tn), lambda j, k: (0, j))
            out_shape = jax.ShapeDtypeStruct((M, N), F32)

        def body(a_ref, b_ref, *rest):
            o_ref = rest[-1]
            r = lax.dot_general(a_ref[...].astype(MXU_DTYPE), b_ref[...].astype(MXU_DTYPE), ((_TN), ((), ())),
                                preferred_element_type=F32)

            @pl.when(pl.program_id(1) == 0)
            def _():
                o_ref[...] = r

            @pl.when(pl.program_id(1) > 0)
            def _():
                o_ref[...] += r

        return pl.pallas_call(
            body, name=name, grid=(N // tn, K // tk),
            in_specs=[pl.BlockSpec((tk, M), lambda j, k: (k, 0)), pl.BlockSpec((tk, tn), lambda j, k: (k, j))]
            + [pl.BlockSpec(memory_space=pl.ANY)] * len(extra),
            out_specs=out_spec, out_shape=out_shape, input_output_aliases=alias,
            compiler_params=_cparams(("parallel", "arbitrary")),
        )(a, b, *extra)

    (M, K) = a.shape
    N = b.shape[1] if mode == "nn" else b.shape[0]
    assert K == (b.shape[0] if mode == "nn" else b.shape[1]), (a.shape, b.shape, mode)
    tm = _tile(M, MATMUL_ROWS)
    tn = _widest(N, lambda t: 2 * (tm * K * sa + K * t * sb + tm * t * (so + 4 * has_add)) <= MATMUL_VMEM_BUDGET)
    dims = ((_NN if mode == "nn" else _NT), ((), ()))

    def body(*refs):
        r = lax.dot_general(refs[0][...].astype(MXU_DTYPE), refs[1][...].astype(MXU_DTYPE), dims, preferred_element_type=F32)
        if has_add:
            r = r + refs[2][...].astype(F32)
        refs[-1][...] = r.astype(refs[-1].dtype)

    b_spec = pl.BlockSpec((K, tn), lambda j, i: (0, j)) if mode == "nn" else pl.BlockSpec((tn, K), lambda j, i: (j, 0))
    in_specs = [pl.BlockSpec((tm, K), lambda j, i: (i, 0)), b_spec]
    args = [a, b]
    if has_add:
        in_specs.append(pl.BlockSpec((tm, tn), lambda j, i: (i, j)))
        args.append(add)
    return pl.pallas_call(
        body, name=name, grid=(N // tn, M // tm),
        in_specs=in_specs,
        out_specs=pl.BlockSpec((tm, tn), lambda j, i: (i, j)),
        out_shape=jax.ShapeDtypeStruct((M, N), out_dtype),
        compiler_params=_cparams(("parallel", "parallel")),
    )(*args)


def _dot(a, b, dims, precision=None):
    if precision is None:
        a, b = a.astype(MXU_DTYPE), b.astype(MXU_DTYPE)
    return lax.dot_general(a, b, (dims, ((), ())), precision=precision, preferred_element_type=F32)


_NN = ((1,), (0,))
_NT = ((1,), (1,))
_TN = ((0,), (0,))


def _rms(x, gain):
    return x * lax.rsqrt(jnp.mean(x * x, axis=-1, keepdims=True) + EPS) * gain


def _hg_block(st_t, q, fl, iv, g, lb, gain):
    row = lax.broadcasted_iota(jnp.int32, (HG_SUB, HG_SUB), 0)
    col = lax.broadcasted_iota(jnp.int32, (HG_SUB, HG_SUB), 1)
    tri = (row >= col).astype(F32)
    heads = [slice(h * HG_DIM, (h + 1) * HG_DIM) for h in range(HG_HEADS)]
    fg = lb + (1.0 - lb) * jax.nn.sigmoid(fl)
    lf, kk, qf = jnp.log(fg), 1.0 - fg, jax.nn.silu(q)
    sts = [st_t[sl, :] for sl in heads]
    parts = [[] for _ in heads]
    for s in range(q.shape[0] // HG_SUB):
        r = slice(s * HG_SUB, (s + 1) * HG_SUB)
        b = lf[r]
        d = 1
        while d < HG_SUB:
            b = b + _shift_rows(b, d)
            d *= 2
        b_mid = jnp.sum(lf[r][:HG_SUB // 2], axis=0, keepdims=True)
        b_end = jnp.sum(lf[r], axis=0, keepdims=True)
        q_in, k_in = qf[r] * jnp.exp(b - b_mid), kk[r] * jnp.exp(b_mid - b)
        q_st, k_st, decay = qf[r] * jnp.exp(b), kk[r] * jnp.exp(b_end - b), jnp.exp(b_end)
        for h, sl in enumerate(heads):
            sc = _dot(q_in[:, sl], k_in[:, sl], _NT) * tri
            parts[h].append(_dot(sc, iv[r, sl], _NN) + _dot(q_st[:, sl], sts[h], _NT))
            sts[h] = sts[h] * decay[:, sl] + _dot(iv[r, sl], k_st[:, sl], _TN)
    outs = [_rms(jnp.concatenate(parts[h], axis=0), gain[:, sl]) * jax.nn.silu(g[:, sl]) for h, sl in enumerate(heads)]
    return jnp.concatenate(sts, axis=0), jnp.concatenate(outs, axis=1)


def _hg_specs(proj, nb):
    return [pl.BlockSpec((HG_BLOCK, HG_WIDTH), lambda i, c=c, f=nb: (f(i), c)) for c in range(4)]


def hgrn2_fwd(proj, lb, gain):
    L = proj.shape[0]
    n = L // HG_BLOCK

    def body(q, fl, iv, g, lb_r, gain_r, o_ref, st_ref, st):
        @pl.when(pl.program_id(0) == 0)
        def _():
            st[...] = jnp.zeros(st.shape, F32)

        st_ref[0] = st[...]
        new, o = _hg_block(st[...], q[...], fl[...], iv[...], g[...], lb_r[...], gain_r[...])
        st[...] = new
        o_ref[...] = o.astype(o_ref.dtype)

    pspec = pl.BlockSpec((1, HG_WIDTH), lambda i: (0, 0))
    return pl.pallas_call(
        body, name="hgrn2_fwd", grid=(n,),
        in_specs=_hg_specs(proj, lambda i: i) + [pspec, pspec],
        out_specs=[pl.BlockSpec((HG_BLOCK, HG_WIDTH), lambda i: (i, 0)),
                   pl.BlockSpec((1, HG_WIDTH, HG_DIM), lambda i: (i, 0, 0))],
        out_shape=[jax.ShapeDtypeStruct((L, HG_WIDTH), MXU_DTYPE),
                   jax.ShapeDtypeStruct((n, HG_WIDTH, HG_DIM), F32)],
        scratch_shapes=[pltpu.VMEM((HG_WIDTH, HG_DIM), F32)],
        compiler_params=_cparams(("arbitrary",)),
    )(proj, proj, proj, proj, lb, gain)


def hgrn2_bwd(proj, lb, gain, states, do):
    L = proj.shape[0]
    n = L // HG_BLOCK

    def body(q, fl, iv, g, lb_r, gain_r, st_r, do_r, dproj, dlb, dgain, dst):
        @pl.when(pl.program_id(0) == 0)
        def _():
            dst[...] = jnp.zeros(dst.shape, F32)
            dlb[...] = jnp.zeros(dlb.shape, F32)
            dgain[...] = jnp.zeros(dgain.shape, F32)

        _, vjp = jax.vjp(_hg_block, st_r[0], q[...], fl[...], iv[...], g[...], lb_r[...], gain_r[...])
        d_st, dq, dfl, div, dg, d_lb, d_gain = vjp((dst[...], do_r[...].astype(F32)))
        dst[...] = d_st
        dproj[:, 0 * HG_WIDTH:1 * HG_WIDTH] = dq
        dproj[:, 1 * HG_WIDTH:2 * HG_WIDTH] = dfl
        dproj[:, 2 * HG_WIDTH:3 * HG_WIDTH] = div
        dproj[:, 3 * HG_WIDTH:4 * HG_WIDTH] = dg
        dlb[...] += d_lb
        dgain[...] += d_gain

    rev = lambda i: n - 1 - i
    pspec = pl.BlockSpec((1, HG_WIDTH), lambda i: (0, 0))
    return pl.pallas_call(
        body, name="hgrn2_bwd", grid=(n,),
        in_specs=_hg_specs(proj, rev) + [pspec, pspec,
                                         pl.BlockSpec((1, HG_WIDTH, HG_DIM), lambda i: (rev(i), 0, 0)),
                                         pl.BlockSpec((HG_BLOCK, HG_WIDTH), lambda i: (rev(i), 0))],
        out_specs=[pl.BlockSpec((HG_BLOCK, 4 * HG_WIDTH), lambda i: (rev(i), 0)), pspec, pspec],
        out_shape=[jax.ShapeDtypeStruct((L, 4 * HG_WIDTH), F32),
                   jax.ShapeDtypeStruct((1, HG_WIDTH), F32), jax.ShapeDtypeStruct((1, HG_WIDTH), F32)],
        scratch_shapes=[pltpu.VMEM((HG_WIDTH, HG_DIM), F32)],
        compiler_params=_cparams(("arbitrary",)),
    )(proj, proj, proj, proj, lb, gain, states, do)


def _rope_rms(x, gain_p, cos_p, sin_p):
    n = x * lax.rsqrt(jnp.sum(x * x, axis=-1, keepdims=True) * (1.0 / MLA_ROPE) + EPS) * gain_p
    r = lax.broadcasted_iota(jnp.int32, (128, 128), 0)
    c = lax.broadcasted_iota(jnp.int32, (128, 128), 1)
    swap = (r == (c + 64) % 128).astype(F32)
    return n * cos_p + _dot(n, swap, _NN, HI) * sin_p


MLA_IN = MLA_Q_RANK + MLA_KV_RANK + 128


def _mla_prep(x, cos_p, sin_p, q_a, w_uq, kv_a, w_ukv, qn_nope, qn_rope, kn_nope, kn_rope):
    c_q, c_kv, kpe = x[:, :MLA_Q_RANK], x[:, MLA_Q_RANK:MLA_Q_RANK + MLA_KV_RANK], x[:, MLA_Q_RANK + MLA_KV_RANK:]
    q = _dot(_rms(c_q, q_a), w_uq, _NN)
    kv = _dot(_rms(c_kv, kv_a), w_ukv, _NN)
    k_pe = _rope_rms(kpe, kn_rope, cos_p, sin_p)
    qs, ks = [], []
    for h in range(MLA_HEADS):
        qs.append(_rms(q[:, h * MLA_DK:h * MLA_DK + MLA_NOPE], qn_nope))
        qs.append(_rope_rms(q[:, h * MLA_DK + MLA_NOPE:(h + 1) * MLA_DK], qn_rope, cos_p, sin_p))
        ks.append(_rms(kv[:, h * MLA_NOPE:(h + 1) * MLA_NOPE], kn_nope))
        ks.append(k_pe)
    return jnp.concatenate(qs, axis=1), jnp.concatenate(ks, axis=1), kv[:, MLA_HEADS * MLA_NOPE:]


def _mla_prep_opds(proj, cos_p, sin_p, params, tm, grads):
    g = (lambda k: k) if grads else (lambda k: None)
    assert (4 * HG_WIDTH) % MLA_IN == 0
    return ([rows(proj, tm, g('blk'), col=4 * HG_WIDTH // MLA_IN, width=MLA_IN), rows(cos_p, tm), rows(sin_p, tm)]
            + [full(p, g('acc')) for p in params])


def mla_prep_fwd(proj, cos_p, sin_p, params, tm):
    L = proj.shape[0]
    W = MLA_HEADS * MLA_DK
    rb = lambda w: (tm, w)
    outs = [((L, W), MXU_DTYPE, rb(W), lambda i: (i, 0)), ((L, W), MXU_DTYPE, rb(W), lambda i: (i, 0)),
            ((L, MLA_HEADS * MLA_V), MXU_DTYPE, rb(MLA_HEADS * MLA_V), lambda i: (i, 0))]
    return blocked_fwd(_mla_prep, _mla_prep_opds(proj, cos_p, sin_p, params, tm, False), outs, L // tm, "mla_prep_fwd")


def mla_prep_bwd(proj, cos_p, sin_p, params, dq, dk, dv, tm):
    L = proj.shape[0]
    return blocked_bwd(_mla_prep, _mla_prep_opds(proj, cos_p, sin_p, params, tm, True),
                       [rows(dq, tm), rows(dk, tm), rows(dv, tm)], L // tm, "mla_prep_bwd")


def _scores(q, k, scale, shift=None):
    s = _dot(q, k, _NT) * scale
    if shift is None:
        return s
    row = lax.broadcasted_iota(jnp.int32, s.shape, 0)
    col = lax.broadcasted_iota(jnp.int32, s.shape, 1)
    return jnp.where(col <= row + shift, s, -jnp.inf)


ATTN_ROWS = 512
ATTN_WIDE = 2


def attn_fwd(q, k, v, scale, t):
    L = q.shape[0]
    tq = ATTN_WIDE * t

    def body(q_ref, k_ref, v_ref, o_ref, lse_ref):
        i = pl.program_id(1)
        qb = q_ref[...]

        def step(j, carry, shift=None):
            m, l, acc = carry
            kj = k_ref[pl.ds(pl.multiple_of(j * t, t), t), :]
            vj = v_ref[pl.ds(pl.multiple_of(j * t, t), t), :]
            s = _scores(qb, kj, scale, shift)
            m_new = jnp.maximum(m, jnp.max(s, axis=-1, keepdims=True))
            p = jnp.exp(s - m_new)
            alpha = jnp.exp(m - m_new)
            return m_new, alpha * l + jnp.sum(p, axis=-1, keepdims=True), alpha * acc + _dot(p, vj, _NN)

        carry = (jnp.full((tq, 1), -jnp.inf, F32), jnp.zeros((tq, 1), F32), jnp.zeros((tq, MLA_V), F32))
        carry = lax.fori_loop(0, ATTN_WIDE * i, step, carry)
        for d in range(ATTN_WIDE):
            carry = step(ATTN_WIDE * i + d, carry, -d * t)
        m, l, acc = carry
        o_ref[...] = acc / l
        lse_ref[...] = jnp.broadcast_to(m + jnp.log(l), lse_ref.shape)

    hspec = lambda rows_, w: pl.BlockSpec((rows_, w), lambda h, i: (0, h))
    bspec = lambda w: pl.BlockSpec((tq, w), lambda h, i: (i, h))
    return pl.pallas_call(
        body, name="attn_fwd", grid=(MLA_HEADS, L // tq),
        in_specs=[bspec(MLA_DK), hspec(L, MLA_DK), hspec(L, MLA_V)],
        out_specs=[bspec(MLA_V), bspec(MLA_V)],
        out_shape=[jax.ShapeDtypeStruct((L, MLA_HEADS * MLA_V), F32)] * 2,
        compiler_params=_cparams(("parallel", "parallel")),
    )(q, k, v)


def attn_bwd_dq(q, k, v, o, lse, do, scale, t):
    L = q.shape[0]
    tq = ATTN_WIDE * t

    def body(q_ref, k_ref, v_ref, o_ref, lse_ref, do_ref, dq_ref):
        i = pl.program_id(1)
        qb, dob = q_ref[...], do_ref[...]
        delta = jnp.sum(dob * o_ref[...], axis=-1, keepdims=True)
        lse_c = jnp.max(lse_ref[...], axis=-1, keepdims=True)

        def step(j, dq, shift=None):
            kj = k_ref[pl.ds(pl.multiple_of(j * t, t), t), :]
            vj = v_ref[pl.ds(pl.multiple_of(j * t, t), t), :]
            p = jnp.exp(_scores(qb, kj, scale, shift) - lse_c)
            ds = p * (_dot(dob, vj, _NT) - delta) * scale
            return dq + _dot(ds, kj, _NN)

        dq = lax.fori_loop(0, ATTN_WIDE * i, step, jnp.zeros((tq, MLA_DK), F32))
        for d in range(ATTN_WIDE):
            dq = step(ATTN_WIDE * i + d, dq, -d * t)
        dq_ref[...] = dq

    hspec = lambda w: pl.BlockSpec((L, w), lambda h, i: (0, h))
    bspec = lambda w: pl.BlockSpec((tq, w), lambda h, i: (i, h))
    return pl.pallas_call(
        body, name="attn_bwd_dq", grid=(MLA_HEADS, L // tq),
        in_specs=[bspec(MLA_DK), hspec(MLA_DK), hspec(MLA_V), bspec(MLA_V), bspec(MLA_V), bspec(MLA_V)],
        out_specs=bspec(MLA_DK),
        out_shape=jax.ShapeDtypeStruct((L, MLA_HEADS * MLA_DK), F32),
        compiler_params=_cparams(("parallel", "parallel")),
    )(q, k, v, o, lse, do)


def attn_bwd_dkv(q, k, v, o, lse, do, scale, t):
    L = q.shape[0]
    tk = ATTN_WIDE * t

    def body(q_ref, k_ref, v_ref, o_ref, lse_ref, do_ref, dk_ref, dv_ref):
        j = pl.program_id(1)
        kb, vb = k_ref[...], v_ref[...]

        def step(i, carry, shift=None):
            dk, dv = carry
            r = pl.ds(pl.multiple_of(i * t, t), t)
            qi, doi = q_ref[r, :], do_ref[r, :]
            delta = jnp.sum(doi * o_ref[r, :], axis=-1, keepdims=True)
            lse_c = jnp.max(lse_ref[r, :], axis=-1, keepdims=True)
            p = jnp.exp(_scores(qi, kb, scale, shift) - lse_c)
            ds = p * (_dot(doi, vb, _NT) - delta) * scale
            return dk + _dot(ds, qi, _TN), dv + _dot(p, doi, _TN)

        carry = (jnp.zeros((tk, MLA_DK), F32), jnp.zeros((tk, MLA_V), F32))
        for d in range(ATTN_WIDE):
            carry = step(ATTN_WIDE * j + d, carry, d * t)
        dk, dv = lax.fori_loop(ATTN_WIDE * (j + 1), L // t, step, carry)
        dk_ref[...] = dk
        dv_ref[...] = dv

    hspec = lambda w: pl.BlockSpec((L, w), lambda h, j: (0, h))
    bspec = lambda w: pl.BlockSpec((tk, w), lambda h, j: (j, h))
    return pl.pallas_call(
        body, name="attn_bwd_dkv", grid=(MLA_HEADS, L // tk),
        in_specs=[hspec(MLA_DK), bspec(MLA_DK), bspec(MLA_V), hspec(MLA_V), hspec(MLA_V), hspec(MLA_V)],
        out_specs=[bspec(MLA_DK), bspec(MLA_V)],
        out_shape=[jax.ShapeDtypeStruct((L, MLA_HEADS * MLA_DK), F32), jax.ShapeDtypeStruct((L, MLA_HEADS * MLA_V), F32)],
        compiler_params=_cparams(("parallel", "parallel")),
    )(q, k, v, o, lse, do)


S5_LANES = S5_GB * S5_STATE


def _cmul(ar, ai, br, bi):
    return ar * br - ai * bi, ar * bi + ai * br


def _a_powers(ar, ai, reverse):
    a2 = _cmul(ar, ai, ar, ai)
    a4 = _cmul(*a2, *a2)
    row = lax.broadcasted_iota(jnp.int32, (8, ar.shape[1]), 0)
    e = (8 - row) if reverse else (row + 1)
    tr, ti = jnp.ones((8, ar.shape[1]), F32), jnp.zeros((8, ar.shape[1]), F32)
    for bit, (pr, pi) in ((1, (ar, ai)), (2, a2), (4, a4), (8, _cmul(*a4, *a4))):
        nr, ni = _cmul(tr, ti, pr, pi)
        sel = (e & bit) != 0
        tr, ti = jnp.where(sel, nr, tr), jnp.where(sel, ni, ti)
    pows = []
    for d, (pr, pi) in zip((1, 2, 4), ((ar, ai), a2, a4)):
        keep = (row < 8 - d) if reverse else (row >= d)
        pows.append((jnp.where(keep, pr, 0.0), jnp.where(keep, pi, 0.0)))
    return pows, (tr, ti)


def _scan8(xr, xi, pows, table, cr, ci, reverse):
    for d, (pr, pi) in zip((1, 2, 4), pows):
        shift = 8 - d if reverse else d
        mr, mi = _cmul(pr, pi, pltpu.roll(xr, shift, 0), pltpu.roll(xi, shift, 0))
        xr, xi = xr + mr, xi + mi
    mr, mi = _cmul(table[0], table[1], cr, ci)
    return xr + mr, xi + mi


def _row_of(x, r):
    row = lax.broadcasted_iota(jnp.int32, x.shape, 0)
    return jnp.sum(jnp.where(row == r, x, 0.0), axis=0, keepdims=True)


def _s5_scan_fwd(h_re, h_im, ar, ai, L):
    pows, table = _a_powers(ar, ai, False)

    def step(i, carry):
        r = pl.ds(pl.multiple_of(i * 8, 8), 8)
        xr, xi = _scan8(h_re[r, :], h_im[r, :], pows, table, carry[0], carry[1], False)
        h_re[r, :] = xr
        h_im[r, :] = xi
        return xr[7:8, :], xi[7:8, :]

    z = jnp.zeros((1, ar.shape[1]), F32)
    lax.fori_loop(0, L // 8, step, (z, z))


def _s5_specs(L):
    return [pl.BlockSpec((L, 128), lambda g: (0, g)),
            pl.BlockSpec((1, 128, S5_LANES), lambda g: (g, 0, 0)), pl.BlockSpec((1, 128, S5_LANES), lambda g: (g, 0, 0)),
            pl.BlockSpec((1, 1, S5_LANES), lambda g: (g, 0, 0)), pl.BlockSpec((1, 1, S5_LANES), lambda g: (g, 0, 0)),
            pl.BlockSpec((1, S5_LANES, 128), lambda g: (g, 0, 0)), pl.BlockSpec((1, S5_LANES, 128), lambda g: (g, 0, 0))]


def s5_fwd(u, w_re, w_im, a_re, a_im, c_re, c_im):
    L, D = u.shape

    def body(u_ref, wr, wi, ar, ai, cr, ci, y_ref, h_re, h_im):
        ub = u_ref[...]
        h_re[...] = _dot(ub, wr[0], _NN)
        h_im[...] = _dot(ub, wi[0], _NN)
        _s5_scan_fwd(h_re, h_im, ar[0], ai[0], L)
        y_ref[...] = _dot(h_re[...], cr[0], _NN) - _dot(h_im[...], ci[0], _NN)

    return pl.pallas_call(
        body, name="s5_fwd", grid=(D // 128,),
        in_specs=_s5_specs(L), out_specs=pl.BlockSpec((L, 128), lambda g: (0, g)),
        out_shape=jax.ShapeDtypeStruct((L, D), F32),
        scratch_shapes=[pltpu.VMEM((L, S5_LANES), F32), pltpu.VMEM((L, S5_LANES), F32)],
        compiler_params=_cparams(("parallel",)),
    )(u, w_re, w_im, a_re, a_im, c_re, c_im)


def s5_bwd(u, w_re, w_im, a_re, a_im, c_re, c_im, dy, tc):
    L, D = u.shape
    nch = L // tc

    def body(u_ref, wr, wi, ar_ref, ai_ref, cr, ci, dy_ref, du_ref, dwr, dwi, dar, dai, dcr, dci, h_re, h_im, g_re, g_im):
        ar, ai = ar_ref[0], ai_ref[0]
        ub = u_ref[...]
        h_re[...] = _dot(ub, wr[0], _NN)
        h_im[...] = _dot(ub, wi[0], _NN)
        _s5_scan_fwd(h_re, h_im, ar, ai, L)
        dyb = dy_ref[...]
        dcr[0] = _dot(h_re[...], dyb, _TN)
        dci[0] = -_dot(h_im[...], dyb, _TN)
        pows, table = _a_powers(ar, -ai, True)
        dwr[0] = jnp.zeros((128, S5_LANES), F32)
        dwi[0] = jnp.zeros((128, S5_LANES), F32)
        z1 = jnp.zeros((1, S5_LANES), F32)
        z8 = jnp.zeros((8, S5_LANES), F32)

        def chunk(cc, carry):
            c0 = pl.multiple_of((nch - 1 - cc) * tc, tc)
            rows_c = pl.ds(c0, tc)
            dyc = dy_ref[rows_c, :]
            g_re[...] = _dot(dyc, cr[0], _NT)
            g_im[...] = -_dot(dyc, ci[0], _NT)

            def step(ii, cy):
                gr_c, gi_c, acc_r, acc_i = cy
                i8 = pl.multiple_of((tc // 8 - 1 - ii) * 8, 8)
                rl = pl.ds(i8, 8)
                xr, xi = _scan8(g_re[rl, :], g_im[rl, :], pows, table, gr_c, gi_c, True)
                g_re[rl, :] = xr
                g_im[rl, :] = xi
                t0 = c0 + i8
                hb_r, hb_i = h_re[pl.ds(t0, 8), :], h_im[pl.ds(t0, 8), :]
                tp = pl.multiple_of(jnp.maximum(t0 - 8, 0), 8)
                first = (t0 > 0).astype(F32)
                pr = h_re[pl.ds(tp, 8), :][7:8, :] * first
                pi = h_im[pl.ds(tp, 8), :][7:8, :] * first
                row = lax.broadcasted_iota(jnp.int32, xr.shape, 0)
                hp_r = jnp.where(row == 0, pr, pltpu.roll(hb_r, 1, 0))
                hp_i = jnp.where(row == 0, pi, pltpu.roll(hb_i, 1, 0))
                return (xr[0:1, :], xi[0:1, :],
                        acc_r + xr * hp_r + xi * hp_i, acc_i + xi * hp_r - xr * hp_i)

            cy = lax.fori_loop(0, tc // 8, step, carry)
            uc = u_ref[rows_c, :]
            gr, gi = g_re[...], g_im[...]
            du_ref[rows_c, :] = _dot(gr, wr[0], _NT) + _dot(gi, wi[0], _NT)
            dwr[0] += _dot(uc, gr, _TN)
            dwi[0] += _dot(uc, gi, _TN)
            return cy

        _, _, acc_r, acc_i = lax.fori_loop(0, nch, chunk, (z1, z1, z8, z8))
        dar[0] = jnp.sum(acc_r, axis=0, keepdims=True)
        dai[0] = jnp.sum(acc_i, axis=0, keepdims=True)

    specs = _s5_specs(L)
    return pl.pallas_call(
        body, name="s5_bwd", grid=(D // 128,),
        in_specs=specs + [pl.BlockSpec((L, 128), lambda g: (0, g))],
        out_specs=[pl.BlockSpec((L, 128), lambda g: (0, g))] + specs[1:],
        out_shape=[jax.ShapeDtypeStruct((L, D), F32)] + [jax.ShapeDtypeStruct(x.shape, F32)
                                                        for x in (w_re, w_im, a_re, a_im, c_re, c_im)],
        scratch_shapes=[pltpu.VMEM((L, S5_LANES), F32), pltpu.VMEM((L, S5_LANES), F32),
                        pltpu.VMEM((tc, S5_LANES), F32), pltpu.VMEM((tc, S5_LANES), F32)],
        compiler_params=_cparams(("parallel",)),
    )(u, w_re, w_im, a_re, a_im, c_re, c_im, dy)


def _s5_discretize(lr, li, ldt, br, bi):
    dt = jnp.exp(ldt)
    mag = jnp.exp(lr * dt)
    ar, ai = mag * jnp.cos(li * dt), mag * jnp.sin(li * dt)
    den = lr * lr + li * li
    zr = ((ar - 1.0) * lr + ai * li) / den
    zi = (ai * lr - (ar - 1.0) * li) / den
    p = lax.broadcasted_iota(jnp.int32, (S5_STATE, S5_STATE * S5_GROUP), 0)
    c = lax.broadcasted_iota(jnp.int32, (S5_STATE, S5_STATE * S5_GROUP), 1)
    rep = (c // S5_GROUP == p).astype(F32)
    zr, zi = _dot(zr, rep, _NN, HI), _dot(zi, rep, _NN, HI)
    return ar, ai, zr * br - zi * bi, zr * bi + zi * br


def _conv_shift(x, d):
    row = lax.broadcasted_iota(jnp.int32, x.shape, 0)
    return jnp.where(row >= d, pltpu.roll(x, d, 0), 0.0)


def _conv_unshift(x, d):
    n = x.shape[0]
    row = lax.broadcasted_iota(jnp.int32, x.shape, 0)
    return jnp.where(row < n - d, pltpu.roll(x, n - d, 0), 0.0)


@functools.partial(jax.custom_vjp, nondiff_argnums=(1,))
def _shift_rows(x, d):
    return _conv_shift(x, d)


_shift_rows.defvjp(lambda x, d: (_conv_shift(x, d), None), lambda d, _, g: (_conv_unshift(g, d),))


def _conv_gate(ug, uv, wg0, wg1, wg2, wv0, wv1, wv2, bg, bv):
    def conv(u, w0, w1, w2, b):
        return u * w2 + _shift_rows(u, 1) * w1 + _shift_rows(u, 2) * w0 + b
    return (jax.nn.silu(conv(ug, wg0, wg1, wg2, bg)) * conv(uv, wv0, wv1, wv2, bv),)


def _rms_fn(x, gain):
    return (_rms(x, gain),)


def _softmax_rows(s):
    e = jnp.exp(s - lax.stop_gradient(jnp.max(s, axis=-1, keepdims=True)))
    return e / jnp.sum(e, axis=-1, keepdims=True)


def _xa_core(qp, k, v, q_gain):
    dh = qp.shape[1] // XA_HEADS
    outs = []
    for h in range(XA_HEADS):
        sl = slice(h * dh, (h + 1) * dh)
        p = _softmax_rows(_dot(_rms(qp[:, sl], q_gain), k[:, sl], _NT) * (dh ** -0.5))
        outs.append(_dot(p, v[:, sl], _NN))
    return (jnp.concatenate(outs, axis=1),)


def _mem_kv(mem, mem_gain, wk, wv, k_gain):
    m = _rms(mem, mem_gain)
    kp = _dot(m, wk, _NN)
    dh = kp.shape[1] // XA_HEADS
    k = jnp.concatenate([_rms(kp[:, h * dh:(h + 1) * dh], k_gain) for h in range(XA_HEADS)], axis=1)
    return k, _dot(m, wv, _NN)


def _s5_post(y, u, d):
    return (jax.nn.gelu(y + d * u),)


def _glu(a, b):
    return (a * jax.nn.sigmoid(b),)


def _lb_first(logits):
    e = jnp.exp(logits - lax.stop_gradient(jnp.max(logits, axis=0, keepdims=True)))
    return (_row_of(e, 0) / jnp.sum(e, axis=0, keepdims=True),)


def _loss_fn(y, t):
    e = y - t
    part = 0.5 * jnp.sum(e * e) / y.shape[1]
    return e * (1.0 / y.shape[1]), jnp.full((8, 128), part / (8 * 128), F32)


def _out(shape, dtype, tm):
    return (shape, dtype, (tm, shape[1]), lambda i: (i, 0))


def rms_fwd(h, gain, tm, dtype):
    return blocked_fwd(_rms_fn, [rows(h, tm), full(gain)], [_out(h.shape, dtype, tm)], h.shape[0] // tm, "rms_fwd")[0]


def rms_bwd(h, gain, dy, tm, residual):
    return blocked_bwd(_rms_fn, [rows(h, tm, 'blk'), full(gain, 'acc')], [rows(dy, tm)], h.shape[0] // tm, "rms_bwd",
                       plus=rows(residual, tm))


def _adamw_math(w, g, m, v):
    m = ADAM_B1 * m + (1.0 - ADAM_B1) * g
    v = ADAM_B2 * v + (1.0 - ADAM_B2) * jnp.square(g)
    m_hat = m / (1.0 - ADAM_B1 ** ADAM_STEP)
    v_hat = v / (1.0 - ADAM_B2 ** ADAM_STEP)
    return -ADAM_LR * (m_hat / (jnp.sqrt(v_hat) + ADAM_EPS) + ADAM_WD * w), m, v


def adamw(w, g, m, v, name):
    R = w.shape[0]
    tm = _tile(R, 256)
    assert g.shape == w.shape == m.shape == v.shape, (name, w.shape, g.shape)

    def body(w_ref, g_ref, m_ref, v_ref, d_ref, nm_ref, nv_ref):
        d_ref[...], nm_ref[...], nv_ref[...] = _adamw_math(w_ref[...], g_ref[...], m_ref[...], v_ref[...])

    spec = pl.BlockSpec((tm, w.shape[1]), lambda i: (i, 0))
    return pl.pallas_call(
        body, name=name, grid=(R // tm,), in_specs=[spec] * 4, out_specs=[spec] * 3,
        out_shape=[jax.ShapeDtypeStruct(w.shape, F32)] * 3, compiler_params=_cparams(("parallel",)),
    )(w, g, m, v)


def _index_operand(i):
    return jnp.reshape(i, (1,)).astype(jnp.int32)


def adamw_layer(w, mine, other, core, m, v, layer, bufs, name):
    H, C = mine.shape
    tm = _tile(H, 256)
    nb = H // tm
    assert w.shape[1:] == (2 * H, C) and all(b.shape == w.shape for b in bufs), (name, w.shape, mine.shape)

    def body(c_ref, w_ref, mine_ref, other_ref, m_ref, v_ref, *rest):
        g_out, d_ref, nm_ref, nv_ref = rest[-4:]
        g_ = jnp.where(pl.program_id(0) // nb == c_ref[0], mine_ref[...], other_ref[...])
        g_out[...] = g_
        d_ref[...], nm_ref[...], nv_ref[...] = _adamw_math(w_ref[...], g_, m_ref[...], v_ref[...])

    lspec = pl.BlockSpec((None, tm, C), lambda i, c: (layer, i, 0))
    half = lambda sign: pl.BlockSpec(
        (tm, C), lambda i, c: (jnp.clip(i - (c[0] if sign else 1 - c[0]) * nb, 0, nb - 1), 0))
    any_spec = pl.BlockSpec(memory_space=pl.ANY)
    grid_spec = pltpu.PrefetchScalarGridSpec(
        num_scalar_prefetch=1, grid=(2 * nb,),
        in_specs=[lspec, half(True), half(False), lspec, lspec] + [any_spec] * 4, out_specs=[lspec] * 4)
    return pl.pallas_call(
        body, name=name, grid_spec=grid_spec, out_shape=[jax.ShapeDtypeStruct(w.shape, F32)] * 4,
        input_output_aliases={6: 0, 7: 1, 8: 2, 9: 3}, compiler_params=_cparams(("parallel",)),
    )(_index_operand(core), w, mine, other, m, v, *bufs)


def add_own_half(x, core, theirs, name, out_dtype):
    nq, _, H, C = x.shape
    tm = _tile(H, 256)

    def body(c_ref, x_ref, t_ref, o_ref):
        o_ref[...] = (x_ref[...] + t_ref[...]).astype(o_ref.dtype)

    spec = pl.BlockSpec((None, tm, C), lambda q, i, c: (q, i, 0))
    grid_spec = pltpu.PrefetchScalarGridSpec(
        num_scalar_prefetch=1, grid=(nq, H // tm),
        in_specs=[pl.BlockSpec((None, None, tm, C), lambda q, i, c: (q, c[0], i, 0)), spec], out_specs=spec)
    return pl.pallas_call(
        body, name=name, grid_spec=grid_spec, out_shape=jax.ShapeDtypeStruct((nq, H, C), out_dtype),
        compiler_params=_cparams(("parallel", "parallel")),
    )(_index_operand(core), x, theirs)


def add_chips(pair, chip, got, name):
    n, R, C = got.shape
    tm = _tile(R, 256)

    def body(q_ref, *refs):
        acc = refs[0][...].astype(F32)
        for r in refs[1:-1]:
            acc = acc + r[...].astype(F32)
        refs[-1][...] = acc

    grid_spec = pltpu.PrefetchScalarGridSpec(
        num_scalar_prefetch=1, grid=(R // tm,),
        in_specs=[pl.BlockSpec((None, tm, C), lambda i, q: (q[0], i, 0))]
        + [pl.BlockSpec((None, tm, C), lambda i, q, j=j: (j, i, 0)) for j in range(n)],
        out_specs=pl.BlockSpec((tm, C), lambda i, q: (i, 0)))
    return pl.pallas_call(
        body, name=name, grid_spec=grid_spec, out_shape=jax.ShapeDtypeStruct((R, C), F32),
        compiler_params=_cparams(("parallel",)),
    )(_index_operand(chip), pair, *([got] * n))


_HBM = pl.BlockSpec(memory_space=pltpu.HBM)
N_CHIPS = 4


def _my_place():
    return lax.axis_index("x"), lax.axis_index("y"), lax.axis_index("c")


def _window(ref, axis, start, size):
    idx = [slice(None)] * len(ref.shape)
    idx[axis] = pl.ds(start, size)
    return ref.at[tuple(idx)]


def _comm_call(body, name, xs, out_shapes, n_remote, n_local, sequencer=None):
    sems = [pltpu.SemaphoreType.DMA((n_remote,)), pltpu.SemaphoreType.DMA((n_remote,)),
            pltpu.SemaphoreType.DMA((max(n_local, 1),))]
    if sequencer is None:
        return pl.pallas_call(
            body, name=name, in_specs=[_HBM] * len(xs), out_specs=[_HBM] * len(out_shapes), out_shape=out_shapes,
            scratch_shapes=sems, compiler_params=pltpu.CompilerParams(has_side_effects=True),
        )(*xs)
    peers_of, collective_id = sequencer
    hbm = pltpu.MemorySpace.HBM
    x_refs = [jax.new_ref(x, memory_space=hbm) for x in xs]
    o_refs = [jax.empty_ref(s, memory_space=hbm) for s in out_shapes]

    @pl.kernel(mesh=plsc.ScalarSubcoreMesh(axis_name="sequencer", num_cores=1), name=name, scratch_types=tuple(sems),
               compiler_params=pltpu.CompilerParams(collective_id=collective_id))
    def launch(send_sems, recv_sems, local_sems):
        peers = peers_of(*_my_place())
        barrier = pltpu.get_barrier_semaphore()
        for peer in peers:
            pl.semaphore_signal(barrier, inc=1, device_id=peer, device_id_type=MESH)
        pl.semaphore_wait(barrier, len(peers))
        body(*x_refs, *o_refs, send_sems, recv_sems, local_sems)

    launch()
    return [o[...] for o in o_refs]


def _sibling(mx, my, mc):
    return [(mx, my, 1 - mc)]


def _same_core_of_other_chips(mx, my, mc):
    return [(tx, ty, mc) for tx, ty in _other_chips(mx, my)]


def _run(copies):
    for cp in copies:
        cp.start()
    for cp in copies:
        cp.wait()


def _other_chips(mx, my):
    return [(mx ^ (j >> 1), my ^ (j & 1)) for j in (1, 2, 3)]


def chip_gather(xs, axes, name):
    n = len(xs)
    shapes, final = [], []
    for x, ax in zip(xs, axes):
        s = list(x.shape)
        if ax is None:
            shapes.append([N_CHIPS] + s)
            final.append(shapes[-1])
        elif ax < x.ndim - 1:
            shapes.append(s[:ax] + [N_CHIPS] + s[ax:])
            final.append(s[:ax] + [N_CHIPS * s[ax]] + s[ax + 1:])
        else:
            assert s[ax] % 128 == 0, (name, s)
            shapes.append(s[:ax] + [N_CHIPS * s[ax]])
            final.append(shapes[-1])

    def body(*refs):
        x_refs, o_refs = refs[:n], refs[n:2 * n]
        send_sems, recv_sems, local_sems = refs[2 * n:]
        mx, my, mc = _my_place()
        q = 2 * mx + my
        copies = []
        for i, (x_ref, o_ref, ax) in enumerate(zip(x_refs, o_refs, axes)):
            if ax is None or ax < len(x_ref.shape) - 1:
                dst = o_ref.at[(slice(None),) * (ax or 0) + (q,)]
            else:
                dst = _window(o_ref, ax, q * x_ref.shape[ax], x_ref.shape[ax])
            copies.append(pltpu.make_async_copy(x_ref, dst, local_sems.at[i]))
            for j, (tx, ty) in enumerate(_other_chips(mx, my)):
                copies.append(pltpu.make_async_remote_copy(
                    src_ref=x_ref, dst_ref=dst, send_sem=send_sems.at[3 * i + j], recv_sem=recv_sems.at[3 * i + j],
                    device_id=(tx, ty, mc), device_id_type=MESH))
        _run(copies)

    out_shapes = [jax.ShapeDtypeStruct(tuple(s), x.dtype) for s, x in zip(shapes, xs)]
    return [o.reshape(f) for o, f in zip(_comm_call(body, name, xs, out_shapes, 3 * n, n), final)]


def gather_two_level(xs, name):
    n = len(xs)
    shapes = [jax.ShapeDtypeStruct((2, N_CHIPS) + x.shape[1:], x.dtype) for x in xs]

    def body(*refs):
        x_refs, o_refs = refs[:n], refs[n:2 * n]
        send_sems, recv_sems, local_sems = refs[2 * n:]
        mx, my, mc = _my_place()
        q = 2 * mx + my
        first, local, second = [], [], []
        for i, (x_ref, o_ref) in enumerate(zip(x_refs, o_refs)):
            local.append(pltpu.make_async_copy(x_ref.at[mc], o_ref.at[mc, q], local_sems.at[i]))
            for j, (tx, ty) in enumerate(_other_chips(mx, my)):
                first.append(pltpu.make_async_remote_copy(
                    src_ref=x_ref.at[mc], dst_ref=o_ref.at[mc, q], send_sem=send_sems.at[4 * i + j],
                    recv_sem=recv_sems.at[4 * i + j], device_id=(tx, ty, mc), device_id_type=MESH))
            second.append(pltpu.make_async_remote_copy(
                src_ref=o_ref.at[mc], dst_ref=o_ref.at[mc], send_sem=send_sems.at[4 * i + 3],
                recv_sem=recv_sems.at[4 * i + 3], device_id=(mx, my, 1 - mc), device_id_type=MESH))
        for cp in local + first:
            cp.start()
        for cp in local:
            cp.wait()
        for cp in first:
            cp.wait_recv()
        _run(second)
        for cp in first:
            cp.wait_send()

    return _comm_call(body, name, xs, shapes, 4 * n, n)


def gather_two_level_sequencer(xs, name, collective_id):
    n = len(xs)
    hbm = pltpu.MemorySpace.HBM
    x_refs = [jax.new_ref(x, memory_space=hbm) for x in xs]
    o_refs = [jax.empty_ref(jax.ShapeDtypeStruct((2, N_CHIPS) + x.shape[1:], x.dtype), memory_space=hbm) for x in xs]

    @pl.kernel(mesh=plsc.ScalarSubcoreMesh(axis_name="sequencer", num_cores=1), name=name,
               scratch_types=(pltpu.SemaphoreType.DMA((4 * n,)), pltpu.SemaphoreType.DMA((4 * n,)),
                              pltpu.SemaphoreType.DMA((n,))),
               compiler_params=pltpu.CompilerParams(collective_id=collective_id))
    def launch(send_sems, recv_sems, local_sems):
        mx, my, mc = _my_place()
        peers = [(tx, ty, mc) for tx, ty in _other_chips(mx, my)] + [(mx, my, 1 - mc)]
        barrier = pltpu.get_barrier_semaphore()
        for peer in peers:
            pl.semaphore_signal(barrier, inc=1, device_id=peer, device_id_type=MESH)
        pl.semaphore_wait(barrier, len(peers))
        q = 2 * mx + my
        first, local, second = [], [], []
        for i, (x_ref, o_ref) in enumerate(zip(x_refs, o_refs)):
            local.append(pltpu.make_async_copy(x_ref.at[mc], o_ref.at[mc, q], local_sems.at[i]))
            for j, peer in enumerate(peers[:3]):
                first.append(pltpu.make_async_remote_copy(
                    src_ref=x_ref.at[mc], dst_ref=o_ref.at[mc, q], send_sem=send_sems.at[4 * i + j],
                    recv_sem=recv_sems.at[4 * i + j], device_id=peer, device_id_type=MESH))
            second.append(pltpu.make_async_remote_copy(
                src_ref=o_ref.at[mc], dst_ref=o_ref.at[mc], send_sem=send_sems.at[4 * i + 3],
                recv_sem=recv_sems.at[4 * i + 3], device_id=peers[3], device_id_type=MESH))
        for cp in local + first:
            cp.start()
        for cp in local:
            cp.wait()
        for cp in first:
            cp.wait_recv()
        _run(second)
        for cp in first:
            cp.wait_send()

    launch()
    return [o[...] for o in o_refs]


def pair_swap(xs, name, halves, collective_id=None):
    n = len(xs)
    shapes = [jax.ShapeDtypeStruct(x.shape[:1] + x.shape[2:] if halves else x.shape, x.dtype) for x in xs]

    def body(*refs):
        x_refs, o_refs = refs[:n], refs[n:2 * n]
        send_sems, recv_sems, _ = refs[2 * n:]
        mx, my, mc = _my_place()
        _run([pltpu.make_async_remote_copy(
            src_ref=x_ref.at[:, 1 - mc] if halves else x_ref, dst_ref=o_ref, send_sem=send_sems.at[i],
            recv_sem=recv_sems.at[i], device_id=(mx, my, 1 - mc), device_id_type=MESH)
            for i, (x_ref, o_ref) in enumerate(zip(x_refs, o_refs))])

    return _comm_call(body, name, xs, shapes, n, 0, None if collective_id is None else (_sibling, collective_id))


def chip_all_to_all(xs, name, collective_id=None):
    n = len(xs)
    shapes = [jax.ShapeDtypeStruct((N_CHIPS - 1,) + x.shape[1:], x.dtype) for x in xs]

    def body(*refs):
        x_refs, o_refs = refs[:n], refs[n:2 * n]
        send_sems, recv_sems, _ = refs[2 * n:]
        mx, my, mc = _my_place()
        copies = []
        for i, (x_ref, o_ref) in enumerate(zip(x_refs, o_refs)):
            for j, (tx, ty) in enumerate(_other_chips(mx, my)):
                copies.append(pltpu.make_async_remote_copy(
                    src_ref=x_ref.at[2 * tx + ty], dst_ref=o_ref.at[j], send_sem=send_sems.at[3 * i + j],
                    recv_sem=recv_sems.at[3 * i + j], device_id=(tx, ty, mc), device_id_type=MESH))
        _run(copies)

    return _comm_call(body, name, xs, shapes, 3 * n, 0,
                      None if collective_id is None else (_same_core_of_other_chips, collective_id))


WEIGHTS = ['norm_mix', 'norm_xa', 'norm_mem', 'norm_ffn', 'xa_wq', 'xa_wk', 'xa_wv', 'xa_wo', 'xa_q_norm', 'xa_k_norm',
           'ffn_w_up', 'ffn_conv_w', 'ffn_conv_b', 'ffn_w_down', 'hg_lb_logits', 'mix_w_in', 'hg_out_norm',
           'mla_q_a_norm', 'mla_w_uq', 'mla_kv_a_norm', 'mla_w_ukv', 'mla_qn_nope', 'mla_qn_rope', 'mla_kn_nope',
           'mla_kn_rope', 'mix_w_out', 's5_lam_re', 's5_lam_im', 's5_log_dt', 's5_b_re', 's5_b_im', 's5_c_re',
           's5_c_im', 's5_d', 's5_w_glu_a', 's5_w_glu_b']
INPUTS = ['x', 'mem', 'positions'] + WEIGHTS + ['loss_target'] + ['m_' + n for n in WEIGHTS] + ['v_' + n for n in WEIGHTS]
SHARD_AXIS = {'xa_wq': 1, 'xa_wk': 1, 'xa_wv': 1, 'xa_wo': 1, 'ffn_w_up': 2, 'ffn_conv_w': 2, 'ffn_w_down': 1,
              'mix_w_in': 2, 'mla_w_uq': 2, 'mla_w_ukv': 2, 'mix_w_out': 1, 's5_d': 1, 's5_w_glu_a': 1, 's5_w_glu_b': 1}
BIG = ['xa_wq', 'xa_wk', 'xa_wv', 'xa_wo', 'ffn_w_up', 'ffn_w_down', 'mix_w_in', 'mix_w_out', 's5_w_glu_a', 's5_w_glu_b']
FIRST_NEEDED = ('mix_w_in', 'mix_w_out')
SMALL_SHARDED = [n for n in WEIGHTS if n in SHARD_AXIS and n not in BIG]
REPLICATED = [n for n in WEIGHTS if n not in SHARD_AXIS]
SMALL = SMALL_SHARDED + REPLICATED
PACK_W = 1024
ROW_MULT = 16
W_IN_SHARD = IN_WIDTH // N_CHIPS
W_IN_SHARD_PAD = 640


def _pack(flats, mult=ROW_MULT):
    flat = jnp.concatenate([f.reshape(-1) for f in flats])
    unit = mult * PACK_W
    n = -(-flat.shape[0] // unit) * unit
    return jnp.pad(flat, (0, n - flat.shape[0])).reshape(n // PACK_W, PACK_W)


def _unpack(packed, shapes):
    flat, out, o = packed.reshape(-1), [], 0
    for s in shapes:
        n = math.prod(s)
        out.append(flat[o:o + n].reshape(s))
        o += n
    return out


def _rope_pad(w):
    z = jnp.zeros(w.shape[:-1] + (MLA_ROPE // 2,), w.dtype)
    return jnp.concatenate([w[..., :MLA_ROPE // 2], z, w[..., MLA_ROPE // 2:], z], axis=-1)


def _rope_unpad(g):
    return jnp.concatenate([g[..., :MLA_ROPE // 2], g[..., 64:64 + MLA_ROPE // 2]], axis=-1)


def _blockdiag_in(bb):
    nb = bb.shape[0] // S5_GB
    t = bb.reshape(nb, S5_GB, S5_STATE, S5_GROUP).transpose(0, 1, 3, 2)
    return jnp.einsum('bgmp,gh->bgmhp', t, jnp.eye(S5_GB, dtype=bb.dtype)).reshape(nb, S5_GB * S5_GROUP, S5_LANES)


def _blockdiag_in_t(dw):
    nb = dw.shape[0]
    t = jnp.einsum('bgmhp,gh->bgmp', dw.reshape(nb, S5_GB, S5_GROUP, S5_GB, S5_STATE), jnp.eye(S5_GB, dtype=dw.dtype))
    return t.transpose(0, 1, 3, 2).reshape(nb * S5_GB, S5_STATE, S5_GROUP)


def _blockdiag_out(c):
    nb = c.shape[0] // S5_GB
    t = c.reshape(nb, S5_GB, S5_GROUP, S5_STATE).transpose(0, 1, 3, 2)
    return jnp.einsum('bgpm,gh->bgphm', t, jnp.eye(S5_GB, dtype=c.dtype)).reshape(nb, S5_LANES, S5_GB * S5_GROUP)


def _blockdiag_out_t(dc):
    nb = dc.shape[0]
    t = jnp.einsum('bgphm,gh->bgpm', dc.reshape(nb, S5_GB, S5_STATE, S5_GB, S5_GROUP), jnp.eye(S5_GB, dtype=dc.dtype))
    return t.transpose(0, 1, 3, 2).reshape(nb * S5_GB, S5_GROUP, S5_STATE)


def _gather_weights(P):
    def halves(x):
        return x if x.shape[0] == 2 else x.reshape(2, x.shape[1] // 2, x.shape[2])

    now = [n for n in BIG if n in FIRST_NEEDED]
    later = [n for n in BIG if n not in FIRST_NEEDED]
    xs = [halves(P[n].astype(BF16)) for n in now] + [halves(_pack([P[n] for n in SMALL_SHARDED], 2 * ROW_MULT)[None])]
    got = gather_two_level(xs, "gather_weights")
    got, xs_later = lax.optimization_barrier((got, [halves(P[n].astype(BF16)) for n in later]))
    got_later = gather_two_level_sequencer(xs_later, "gather_weights_later", 1)
    full_w = {}
    for n, g in list(zip(now, got[:-1])) + list(zip(later, got_later)):
        two_layers, by_rows = P[n].shape[0] == 2, SHARD_AXIS[n] == 1
        if two_layers and by_rows:
            full_w[n] = g.reshape(2, N_CHIPS * g.shape[2], g.shape[3])
        elif two_layers:
            full_w[n] = g.transpose(0, 2, 1, 3).reshape(2, g.shape[2], N_CHIPS * g.shape[3])
        elif by_rows:
            full_w[n] = g.transpose(1, 0, 2, 3).reshape(1, 2 * N_CHIPS * g.shape[2], g.shape[3])
        else:
            full_w[n] = g.transpose(0, 2, 1, 3).reshape(1, 2 * g.shape[2], N_CHIPS * g.shape[3])
    small = got[-1].transpose(1, 0, 2, 3).reshape(N_CHIPS, -1, PACK_W)
    per_chip = [_unpack(small[q], [P[n].shape for n in SMALL_SHARDED]) for q in range(N_CHIPS)]
    for i, n in enumerate(SMALL_SHARDED):
        full_w[n] = jnp.concatenate([per_chip[q][i] for q in range(N_CHIPS)], axis=SHARD_AXIS[n])
    return full_w


def _reduce_batch(items, small, P, results, tag, ids):
    mx, my, mc = _my_place()
    q = 2 * mx + my
    ids = ids or {}
    names = [f"{n}_{lyr}" for n, lyr, _ in items] + (['small'] if small is not None else [])
    xs = [g.reshape(N_CHIPS, 2, g.shape[1] // 2, g.shape[2]) for g in [g for _, _, g in items] + ([small] if small is not None else [])]
    def after(vals, tie):
        return (vals, None) if tie is None else lax.optimization_barrier((vals, tie))

    theirs = pair_swap(xs, "grads_pair_swap_" + tag, True, ids.get('swap'))
    theirs, tie = after(theirs, (yield None))
    pair = [add_own_half(x, mc, t, "grads_pair_sum_" + n, F32 if n == 'small' else BF16) for x, t, n in zip(xs, theirs, names)]
    got = chip_all_to_all(pair, "grads_chip_all_to_all_" + tag, ids.get('a2a'))
    got, tie = after(got, (yield pair[-1] if tie is None else tie))
    summed = [add_chips(p, q, g, "grads_chip_sum_" + n) for p, g, n in zip(pair, got, names)]
    other = pair_swap(summed, "grads_pair_join_" + tag, False, ids.get('join'))
    other, tie = after(other, (yield summed[-1] if tie is None else tie))
    for (n, lyr, _), s, o in zip(items, summed, other):
        if n == 'mix_w_in':
            s, o = s[:, :W_IN_SHARD], o[:, :W_IN_SHARD]
        view = (P[n].shape[0], 2 * s.shape[0], P[n].shape[-1])
        bufs = results.get(n) or [lax.empty(view, F32) for _ in range(4)]
        results[n] = adamw_layer(P[n].reshape(view), s, o, mc, P['m_' + n].reshape(view), P['v_' + n].reshape(view), lyr,
                                 bufs, f"adamw_{n}_{lyr}")
    if small is not None:
        results['small_quarter'] = lax.cond(mc == 0, lambda a, b: jnp.concatenate([a, b], axis=0),
                                            lambda a, b: jnp.concatenate([b, a], axis=0), summed[-1], other[-1])
    yield tie


def _update_small(small_quarter, GS, P):
    mx, my, _ = _my_place()
    q = 2 * mx + my
    small_sum = chip_gather([small_quarter], [0], "grads_small_gather")[0]
    g_small = dict(zip(SMALL, _unpack(small_sum, [GS[n].shape for n in SMALL])))
    for n in SMALL_SHARDED:
        s = P[n].shape[SHARD_AXIS[n]]
        g_small[n] = lax.dynamic_slice_in_dim(g_small[n], q * s, s, axis=SHARD_AXIS[n])
    grad, delta, new_m, new_v = {}, {}, {}, {}
    packed = lambda prefix: _pack([P[prefix + n] for n in SMALL])
    d, m_, v_ = adamw(packed(''), _pack([g_small[n] for n in SMALL]), packed('m_'), packed('v_'), "adamw_small")
    shapes = [P[n].shape for n in SMALL]
    grad.update(g_small)
    for out, pk in ((delta, d), (new_m, m_), (new_v, v_)):
        out.update(zip(SMALL, _unpack(pk, shapes)))
    return grad, delta, new_m, new_v


def _row(v):
    return v.reshape(1, -1)


def _xattn_fwd(h, mem, W, lyr, tm):
    g_xa, g_mem = _row(W['norm_xa'][lyr]), _row(W['norm_mem'][lyr])
    g_q, g_k = _row(W['xa_q_norm'][lyr]), _row(W['xa_k_norm'][lyr])
    wq, wk, wv, wo = (W[n][lyr] for n in ('xa_wq', 'xa_wk', 'xa_wv', 'xa_wo'))
    L, D = h.shape
    M = mem.shape[0]
    hx = rms_fwd(h, g_xa, tm, MXU_DTYPE)
    qp = matmul(hx, wq, name="xa_q")
    kv_opds = [full(mem), full(g_mem), full(wk), full(wv), full(g_k)]
    k, v = blocked_fwd(_mem_kv, kv_opds, [((M, D), F32, (M, D), lambda i: (0, 0))] * 2, 1, "xa_mem_kv")
    o = blocked_fwd(_xa_core, [rows(qp, tm), full(k), full(v), full(g_q)], [_out((L, D), MXU_DTYPE, tm)], L // tm,
                    "xa_core")[0]
    out = matmul(o, wo, add=h, name="xa_o")
    return out, (h, hx, qp, k, v, o)


def _xattn_bwd(dout, saved, mem, W, lyr, tm):
    h, hx, qp, k, v, o = saved
    g_xa, g_mem = _row(W['norm_xa'][lyr]), _row(W['norm_mem'][lyr])
    g_q, g_k = _row(W['xa_q_norm'][lyr]), _row(W['xa_k_norm'][lyr])
    wq, wk, wv, wo = (W[n][lyr] for n in ('xa_wq', 'xa_wk', 'xa_wv', 'xa_wo'))
    L = h.shape[0]
    do = matmul(dout, wo, "nt", name="xa_do")
    d_wo = matmul(o, dout, "tn", name="xa_dwo")
    dqp, dk, dv, d_gq = blocked_bwd(_xa_core, [rows(qp, tm, 'blk'), full(k, 'acc'), full(v, 'acc'), full(g_q, 'acc')],
                                    [rows(do, tm)], L // tm, "xa_core_bwd")
    d_wq = matmul(hx, dqp, "tn", name="xa_dwq")
    dhx = matmul(dqp, wq, "nt", name="xa_dhx")
    dh, d_gxa = rms_bwd(h, g_xa, dhx, tm, dout)
    d_gmem, d_wk, d_wv, d_gk = blocked_bwd(
        _mem_kv, [full(mem), full(g_mem, 'acc'), full(wk, 'acc'), full(wv, 'acc'), full(g_k, 'acc')],
        [full(dk), full(dv)], 1, "xa_mem_kv_bwd")
    by_chip = lambda g: g.reshape(N_CHIPS, g.shape[0] // N_CHIPS, g.shape[1])
    grads = {'norm_xa': d_gxa, 'norm_mem': d_gmem, 'xa_q_norm': d_gq, 'xa_k_norm': d_gk,
             'xa_wq': by_chip(d_wq), 'xa_wk': by_chip(d_wk), 'xa_wv': by_chip(d_wv), 'xa_wo': by_chip(d_wo)}
    return dh, grads


def _conv_params(W, lyr):
    cw, cb = W['ffn_conv_w'][lyr], W['ffn_conv_b'][lyr]
    F = cw.shape[1] // 2
    return [cw[0:1, :F], cw[1:2, :F], cw[2:3, :F], cw[0:1, F:], cw[1:2, F:], cw[2:3, F:], _row(cb[:F]), _row(cb[F:])]


def _ffn_fwd(h, W, lyr, tm):
    L, D = h.shape
    w_up, w_down = W['ffn_w_up'][lyr], W['ffn_w_down'][lyr]
    F = w_down.shape[0]
    hf = rms_fwd(h, _row(W['norm_ffn'][lyr]), tm, MXU_DTYPE)
    ug = matmul(hf, w_up[:, :F], name="ffn_up_gate")
    uv = matmul(hf, w_up[:, F:], name="ffn_up_value")
    opds = [cols(ug, 128), cols(uv, 128)] + [cols(p, 128) for p in _conv_params(W, lyr)]
    a = blocked_fwd(_conv_gate, opds, [((L, F), MXU_DTYPE, (L, 128), lambda j: (0, j))], F // 128, "ffn_conv_gate")[0]
    out = matmul(a, w_down, add=h, name="ffn_down")
    return out, (h, hf, ug, uv, a)


def _ffn_bwd(dout, saved, W, lyr, tm):
    h, hf, ug, uv, a = saved
    w_up, w_down = W['ffn_w_up'][lyr], W['ffn_w_down'][lyr]
    F = w_down.shape[0]
    da = matmul(dout, w_down, "nt", name="ffn_da")
    d_wdown = matmul(a, dout, "tn", name="ffn_dwdown")
    opds = [cols(ug, 128, 'blk'), cols(uv, 128, 'blk')] + [cols(p, 128, 'blk') for p in _conv_params(W, lyr)]
    gs = blocked_bwd(_conv_gate, opds, [cols(da, 128)], F // 128, "ffn_conv_gate_bwd")
    dug, duv = gs[0], gs[1]
    d_cw = jnp.concatenate([jnp.concatenate(gs[2:5], axis=0), jnp.concatenate(gs[5:8], axis=0)], axis=1)
    d_cb = jnp.concatenate([gs[8], gs[9]], axis=1)[0]
    half = N_CHIPS // 2
    d_wup = lax.empty((N_CHIPS, hf.shape[1], w_up.shape[1] // N_CHIPS), F32)
    d_wup = matmul(hf, dug, "tn", name="ffn_dwup_gate", into=(d_wup, 0), col_blocks=half)
    d_wup = matmul(hf, duv, "tn", name="ffn_dwup_value", into=(d_wup, half), col_blocks=half)
    dhf = matmul(dug, w_up[:, :F], "nt", name="ffn_dhf_gate")
    dhf = matmul(duv, w_up[:, F:], "nt", add=dhf, name="ffn_dhf_value")
    dh, d_g = rms_bwd(h, _row(W['norm_ffn'][lyr]), dhf, tm, dout)
    d_wdown = d_wdown.reshape(N_CHIPS, F // N_CHIPS, d_wdown.shape[1])
    return dh, {'norm_ffn': d_g, 'ffn_w_up': d_wup, 'ffn_conv_w': d_cw, 'ffn_conv_b': d_cb, 'ffn_w_down': d_wdown}


def _mla_params(W):
    w_uq = W['mla_w_uq'][0].reshape(MLA_Q_RANK, MLA_HEADS, MLA_QK)
    w_uq = jnp.concatenate([w_uq[..., :MLA_NOPE], _rope_pad(w_uq[..., MLA_NOPE:])], axis=-1)
    w_ukv = W['mla_w_ukv'][0].reshape(MLA_KV_RANK, MLA_HEADS, MLA_NOPE + MLA_V)
    w_ukv = jnp.concatenate([w_ukv[..., :MLA_NOPE].reshape(MLA_KV_RANK, -1), w_ukv[..., MLA_NOPE:].reshape(MLA_KV_RANK, -1)],
                            axis=1)
    return [_row(W['mla_q_a_norm'][0]), w_uq.reshape(MLA_Q_RANK, MLA_HEADS * MLA_DK), _row(W['mla_kv_a_norm'][0]), w_ukv,
            _row(W['mla_qn_nope'][0]), _row(_rope_pad(W['mla_qn_rope'][0])), _row(W['mla_kn_nope'][0]),
            _row(_rope_pad(W['mla_kn_rope'][0]))]


def _w_in_padded(W):
    w = W['mix_w_in'][0]
    return jnp.concatenate([w[:, :IN_WIDTH - MLA_ROPE], _rope_pad(w[:, IN_WIDTH - MLA_ROPE:])], axis=1)


def _mixer0_fwd(h, W, cos_p, sin_p, tm):
    L = h.shape[0]
    t = min(ATTN_ROWS, L // ATTN_WIDE)
    hn = rms_fwd(h, _row(W['norm_mix'][0]), tm, MXU_DTYPE)
    proj = matmul(hn, _w_in_padded(W), name="mix_in")
    logits = W['hg_lb_logits']
    lb = blocked_fwd(_lb_first, [full(logits)], [((1, HG_WIDTH), F32, (1, HG_WIDTH), lambda i: (0, 0))], 1, "hg_lb")[0]
    gain = _row(W['hg_out_norm'][0])
    o_hg, states = hgrn2_fwd(proj, lb, gain)
    mp = _mla_params(W)
    q, k, v = mla_prep_fwd(proj, cos_p, sin_p, mp, tm)
    scale = MLA_QK ** -0.5
    o_mla, lse = attn_fwd(q, k, v, scale, t)
    w_out = W['mix_w_out'][0]
    out = matmul(o_hg, w_out[:HG_WIDTH], add=h, name="mix_out_hg")
    out = matmul(o_mla, w_out[HG_WIDTH:], add=out, name="mix_out_mla")
    return out, (h, hn, proj, lb, o_hg, states, q, k, v, o_mla, lse)


def _mixer0_bwd(dout, saved, W, cos_p, sin_p, tm, part_way=None):
    h, hn, proj, lb, o_hg, states, q, k, v, o_mla, lse = saved
    L = h.shape[0]
    t = min(ATTN_ROWS, L // ATTN_WIDE)
    scale = MLA_QK ** -0.5
    w_out = W['mix_w_out'][0]
    gain = _row(W['hg_out_norm'][0])
    do_hg = matmul(dout, w_out[:HG_WIDTH], "nt", name="mix_do_hg")
    do_mla = matmul(dout, w_out[HG_WIDTH:], "nt", name="mix_do_mla")
    d_wout = jnp.concatenate([matmul(o_hg, dout, "tn", name="mix_dwout_hg"), matmul(o_mla, dout, "tn", name="mix_dwout_mla")],
                             axis=0)
    dq = attn_bwd_dq(q, k, v, o_mla, lse, do_mla, scale, t)
    if part_way is not None:
        dq = part_way(dq)
    dk, dv = attn_bwd_dkv(q, k, v, o_mla, lse, do_mla, scale, t)
    mp = _mla_params(W)
    d_mla, d_qa, d_wuq, d_kva, d_wukv, d_qnn, d_qnr, d_knn, d_knr = mla_prep_bwd(proj, cos_p, sin_p, mp, dq, dk, dv, tm)
    d_hg, d_lb, d_gain = hgrn2_bwd(proj, lb, gain, states, do_hg)
    w_in, n_hg = _w_in_padded(W), 4 * HG_WIDTH
    d_win = jnp.concatenate([matmul(hn, d_hg, "tn", name="mix_dwin_hg"), matmul(hn, d_mla, "tn", name="mix_dwin_mla")], axis=1)
    dhn = matmul(d_hg, w_in[:, :n_hg], "nt", name="mix_dhn_hg")
    dhn = matmul(d_mla, w_in[:, n_hg:], "nt", add=dhn, name="mix_dhn_mla")
    dh, d_g = rms_bwd(h, _row(W['norm_mix'][0]), dhn, tm, dout)
    logits = W['hg_lb_logits']
    d_logits = blocked_bwd(_lb_first, [full(logits, 'acc')], [full(d_lb)], 1, "hg_lb_bwd")[0]
    d_wuq = d_wuq.reshape(MLA_Q_RANK, MLA_HEADS, MLA_DK)
    d_wuq = jnp.concatenate([d_wuq[..., :MLA_NOPE], _rope_unpad(d_wuq[..., MLA_NOPE:])], axis=-1)
    hw = MLA_HEADS * MLA_NOPE
    d_wukv = jnp.concatenate([d_wukv[:, :hw].reshape(MLA_KV_RANK, MLA_HEADS, MLA_NOPE),
                              d_wukv[:, hw:].reshape(MLA_KV_RANK, MLA_HEADS, MLA_V)], axis=-1)
    d_win = jnp.concatenate([d_win[:, :IN_WIDTH - MLA_ROPE], _rope_unpad(d_win[:, IN_WIDTH - MLA_ROPE:])], axis=1)
    d_win = d_win.reshape(d_win.shape[0], N_CHIPS, W_IN_SHARD).transpose(1, 0, 2)
    d_win = jnp.pad(d_win, ((0, 0), (0, 0), (0, W_IN_SHARD_PAD - W_IN_SHARD)))
    d_wout = d_wout.reshape(N_CHIPS, d_wout.shape[0] // N_CHIPS, d_wout.shape[1])
    grads = {'norm_mix': d_g, 'hg_lb_logits': d_logits, 'mix_w_in': d_win, 'hg_out_norm': d_gain,
             'mla_q_a_norm': d_qa, 'mla_w_uq': d_wuq.reshape(1, MLA_Q_RANK, -1), 'mla_kv_a_norm': d_kva,
             'mla_w_ukv': d_wukv.reshape(1, MLA_KV_RANK, -1), 'mla_qn_nope': d_qnn, 'mla_qn_rope': _rope_unpad(d_qnr),
             'mla_kn_nope': d_knn, 'mla_kn_rope': _rope_unpad(d_knr), 'mix_w_out': d_wout}
    return dh,grads


def _s5_inputs(W):
    G = W['s5_lam_re'].shape[1]
    return [W['s5_lam_re'][0], W['s5_lam_im'][0], W['s5_log_dt'][0].reshape(G, 1),
            W['s5_b_re'][0].reshape(G, -1), W['s5_b_im'][0].reshape(G, -1)]


def _mixer1_fwd(h, W, tm):
    L, D = h.shape
    u = rms_fwd(h, _row(W['norm_mix'][1]), tm, F32)
    di = _s5_inputs(W)
    G = di[0].shape[0]
    sq, wide = ((G, S5_STATE), F32, (G, S5_STATE), lambda i: (0, 0)), ((G, S5_STATE * S5_GROUP), F32, (G, S5_STATE * S5_GROUP), lambda i: (0, 0))
    ar, ai, bbr, bbi = blocked_fwd(_s5_discretize, [full(a) for a in di], [sq, sq, wide, wide], 1, "s5_discretize")
    nb = G // S5_GB
    core = (_blockdiag_in(bbr.reshape(G, S5_STATE, S5_GROUP)), _blockdiag_in(bbi.reshape(G, S5_STATE, S5_GROUP)),
            ar.reshape(nb, 1, S5_LANES), ai.reshape(nb, 1, S5_LANES),
            _blockdiag_out(W['s5_c_re'][0]), _blockdiag_out(W['s5_c_im'][0]))
    y = s5_fwd(u, *core)
    d = W['s5_d']
    y2 = blocked_fwd(_s5_post, [rows(y, tm), rows(u, tm), full(d)], [_out((L, D), MXU_DTYPE, tm)], L // tm, "s5_post")[0]
    w_ab = jnp.concatenate([W['s5_w_glu_a'][0], W['s5_w_glu_b'][0]], axis=1)
    ab = matmul(y2, w_ab, name="s5_glu_in")
    mix = blocked_fwd(_glu, [rows(ab, tm, col=0, width=D), rows(ab, tm, col=1, width=D)], [_out((L, D), F32, tm)], L // tm,
                      "s5_glu")[0]
    return h + mix, (h, u, core, y, y2, ab)


def _mixer1_bwd(dout, saved, W, tm):
    h, u, core, y, y2, ab = saved
    L, D = h.shape
    da, db = blocked_bwd(_glu, [rows(ab, tm, 'blk', col=0, width=D), rows(ab, tm, 'blk', col=1, width=D)], [rows(dout, tm)],
                         L // tm, "s5_glu_bwd")
    w_a, w_b = W['s5_w_glu_a'][0], W['s5_w_glu_b'][0]
    dy2 = matmul(da, w_a, "nt", name="s5_dy2_a")
    dy2 = matmul(db, w_b, "nt", add=dy2, name="s5_dy2_b")
    d_wa = matmul(y2, da, "tn", name="s5_dwa")
    d_wb = matmul(y2, db, "tn", name="s5_dwb")
    d = W['s5_d']
    dy, du_skip, d_d = blocked_bwd(_s5_post, [rows(y, tm, 'blk'), rows(u, tm, 'blk'), full(d, 'acc')], [rows(dy2, tm)], L // tm,
                                   "s5_post_bwd")
    du, dwr, dwi, dar, dai, dcr, dci = s5_bwd(u, *core, dy, min(256, L))
    di = _s5_inputs(W)
    G = di[0].shape[0]
    cts = [dar.reshape(G, S5_STATE), dai.reshape(G, S5_STATE), _blockdiag_in_t(dwr).reshape(G, -1), _blockdiag_in_t(dwi).reshape(G, -1)]
    d_lr, d_li, d_ldt, d_br, d_bi = blocked_bwd(_s5_discretize, [full(a, 'acc') for a in di], [full(c) for c in cts], 1,
                                                "s5_discretize_bwd")
    dh, d_g = rms_bwd(h, _row(W['norm_mix'][1]), du + du_skip, tm, dout)
    bshape = W['s5_b_re'].shape
    grads = {'norm_mix': d_g, 's5_lam_re': d_lr[None], 's5_lam_im': d_li[None], 's5_log_dt': d_ldt.reshape(1, G),
             's5_b_re': d_br.reshape(bshape), 's5_b_im': d_bi.reshape(bshape), 's5_c_re': _blockdiag_out_t(dcr)[None],
             's5_c_im': _blockdiag_out_t(dci)[None], 's5_d': d_d, 's5_w_glu_a': d_wa.reshape(N_CHIPS, -1, D), 's5_w_glu_b': d_wb.reshape(N_CHIPS, -1, D)}
    return dh,grads


def kernel(x, mem, positions, norm_mix, norm_xa, norm_mem, norm_ffn, xa_wq, xa_wk, xa_wv, xa_wo, xa_q_norm, xa_k_norm, ffn_w_up, ffn_conv_w, ffn_conv_b, ffn_w_down, hg_lb_logits, mix_w_in, hg_out_norm, mla_q_a_norm, mla_w_uq, mla_kv_a_norm, mla_w_ukv, mla_qn_nope, mla_qn_rope, mla_kn_nope, mla_kn_rope, mix_w_out, s5_lam_re, s5_lam_im, s5_log_dt, s5_b_re, s5_b_im, s5_c_re, s5_c_im, s5_d, s5_w_glu_a, s5_w_glu_b, loss_target, m_norm_mix, m_norm_xa, m_norm_mem, m_norm_ffn, m_xa_wq, m_xa_wk, m_xa_wv, m_xa_wo, m_xa_q_norm, m_xa_k_norm, m_ffn_w_up, m_ffn_conv_w, m_ffn_conv_b, m_ffn_w_down, m_hg_lb_logits, m_mix_w_in, m_hg_out_norm, m_mla_q_a_norm, m_mla_w_uq, m_mla_kv_a_norm, m_mla_w_ukv, m_mla_qn_nope, m_mla_qn_rope, m_mla_kn_nope, m_mla_kn_rope, m_mix_w_out, m_s5_lam_re, m_s5_lam_im, m_s5_log_dt, m_s5_b_re, m_s5_b_im, m_s5_c_re, m_s5_c_im, m_s5_d, m_s5_w_glu_a, m_s5_w_glu_b, v_norm_mix, v_norm_xa, v_norm_mem, v_norm_ffn, v_xa_wq, v_xa_wk, v_xa_wv, v_xa_wo, v_xa_q_norm, v_xa_k_norm, v_ffn_w_up, v_ffn_conv_w, v_ffn_conv_b, v_ffn_w_down, v_hg_lb_logits, v_mix_w_in, v_hg_out_norm, v_mla_q_a_norm, v_mla_w_uq, v_mla_kv_a_norm, v_mla_w_ukv, v_mla_qn_nope, v_mla_qn_rope, v_mla_kn_nope, v_mla_kn_rope, v_mix_w_out, v_s5_lam_re, v_s5_lam_im, v_s5_log_dt, v_s5_b_re, v_s5_b_im, v_s5_c_re, v_s5_c_im, v_s5_d, v_s5_w_glu_a, v_s5_w_glu_b):
    P = dict(locals())
    assert sorted(P) == sorted(INPUTS) and norm_mix.shape[0] == 2 and mix_w_in.shape[0] == 1
    x, mem, target = P['x'][0], P['mem'][0], P['loss_target'][0]
    L, D = x.shape
    tm = min(256, L)

    W = {n: P[n] for n in REPLICATED}
    W.update(_gather_weights(P))

    inv_freq = 1.0 / (ROPE_BASE ** (jnp.arange(0, MLA_ROPE, 2, dtype=F32) / MLA_ROPE))
    ang = P['positions'][0].astype(F32)[:, None] * inv_freq
    cos, sin, z = jnp.cos(ang), jnp.sin(ang), jnp.zeros_like(ang)
    cos_p = jnp.concatenate([cos, z, cos, z], axis=1)
    sin_p = jnp.concatenate([-sin, z, sin, z], axis=1)

    h, s_mix0 = _mixer0_fwd(x, W, cos_p, sin_p, tm)
    h, s_xa0 = _xattn_fwd(h, mem, W, 0, tm)
    h, s_ffn0 = _ffn_fwd(h, W, 0, tm)
    h, s_mix1 = _mixer1_fwd(h, W, tm)
    h, s_xa1 = _xattn_fwd(h, mem, W, 1, tm)
    h, s_ffn1 = _ffn_fwd(h, W, 1, tm)
    n = L // tm
    dh, parts = blocked_fwd(_loss_fn, [rows(h, tm), rows(target, tm)],
                            [_out((L, D), F32, tm), ((n * 8, 128), F32, (8, 128), lambda i: (i, 0))], n, "loss")
    loss = lax.psum(jnp.sum(parts), ("x", "y", "c"))

    layered = {}

    def collect(g, lyr):
        for k_, v_ in g.items():
            layered.setdefault(k_, {})[lyr] = v_

    results = {}

    def big_items(lyr, names):
        return [(n_, 0 if P[n_].shape[0] == 1 else lyr, layered[n_][lyr]) for n_ in names]

    per_layer = [n_ for n_ in BIG if P[n_].shape[0] == 2]
    second_mixer = ['s5_w_glu_a', 's5_w_glu_b']
    first_mixer = ['mix_w_in', 'mix_w_out']
    assert sorted(per_layer + second_mixer + first_mixer) == sorted(BIG)

    dh, g = _ffn_bwd(dh, s_ffn1, W, 1, tm)
    collect(g, 1)
    dh, g = _xattn_bwd(dh, s_xa1, mem, W, 1, tm)
    collect(g, 1)
    dh, g = _mixer1_bwd(dh, s_mix1, W, tm)
    collect(g, 1)
    late = _reduce_batch(big_items(1, per_layer + second_mixer), None, P, results, "late_layer", {'swap': 2, 'a2a': 3, 'join': 4})
    next(late)
    dh, g = _ffn_bwd(dh, s_ffn0, W, 0, tm)
    collect(g, 0)
    dh = late.send(dh)
    dh, g = _xattn_bwd(dh, s_xa0, mem, W, 0, tm)
    collect(g, 0)
    mid = _reduce_batch(big_items(0, per_layer), None, P, results, "first_layer", {'swap': 5, 'a2a': 6, 'join': 7})
    next(mid)
    dx, g = _mixer0_bwd(dh, s_mix0, W, cos_p, sin_p, tm, part_way=mid.send)
    collect(g, 0)

    GS = {}
    for name in SMALL:
        by_layer = [layered[name][lyr] for lyr in sorted(layered[name])]
        full_shape = W[name].shape
        GS[name] = (by_layer[0].reshape(full_shape) if len(by_layer) == 1
                    else jnp.stack([g_.reshape(full_shape[1:]) for g_ in by_layer]))
    small = _pack([GS[n_] for n_ in SMALL], 2 * N_CHIPS * ROW_MULT).reshape(N_CHIPS, -1, PACK_W)
    dx = late.send(dx)
    dx = mid.send(dx)
    last = _reduce_batch(big_items(0, first_mixer), small, P, results, "first_mixer", {'swap': 8, 'a2a': 9, 'join': 10})
    next(last)
    late.send(None)
    last_pair_sum = last.send(None)
    mid.send(last_pair_sum)
    mid_name = per_layer[-1]
    results[mid_name] = list(last.send(list(results[mid_name])))
    last.send(None)
    outs = list(_update_small(results['small_quarter'], GS, P))
    for k_ in range(4):
        outs[k_].update({n_: results[n_][k_].reshape(P[n_].shape) for n_ in BIG})
    return (loss, dx[None], *[d[n_] for d in outs for n_ in WEIGHTS])
```

```python
import functools
import math

import jax
import jax.numpy as jnp
import numpy as np
from jax import lax
from jax.experimental import pallas as pl
from jax.experimental.pallas import tpu as pltpu
from jax.experimental.pallas import tpu_sc as plsc

F32 = jnp.float32
BF16 = jnp.bfloat16
MXU_DTYPE = BF16
HI = lax.Precision.HIGHEST
V7X_VMEM_LIMIT_BYTES = 56 * 1024 * 1024
EPS = 1e-6
MESH = pl.DeviceIdType.MESH

HG_HEADS, HG_DIM = 4, 128
HG_WIDTH = HG_HEADS * HG_DIM
HG_SUB = 32
HG_BLOCK = 64
MLA_HEADS, MLA_Q_RANK, MLA_KV_RANK = 4, 256, 128
MLA_NOPE, MLA_ROPE, MLA_V = 128, 64, 128
MLA_QK = MLA_NOPE + MLA_ROPE
MLA_DK = 256
ROPE_BASE = 10000.0
IN_WIDTH = 4 * HG_WIDTH + MLA_Q_RANK + MLA_KV_RANK + MLA_ROPE
IN_PAD = 4 * HG_WIDTH + MLA_Q_RANK + MLA_KV_RANK + 128
S5_GROUP, S5_STATE = 16, 64
S5_GB = 8
DT_MIN, DT_MAX = 1e-3, 1e-1
XA_HEADS = 4
CONV_W = 3
ADAM_LR, ADAM_B1, ADAM_B2, ADAM_EPS, ADAM_WD, ADAM_STEP = 0.001, 0.9, 0.999, 1e-08, 0.01, 10


def _cparams(sem):
    return pltpu.CompilerParams(dimension_semantics=sem, vmem_limit_bytes=V7X_VMEM_LIMIT_BYTES)


class Opd:
    def __init__(self, arr, block, imap, grad=None, gshape=None, gimap=None):
        self.arr, self.block, self.imap, self.grad = arr, block, imap, grad
        self.gshape = arr.shape if gshape is None else gshape
        self.gimap = imap if gimap is None else gimap

    def spec(self):
        return pl.BlockSpec(self.block, self.imap)

    def gspec(self):
        return pl.BlockSpec(self.block, self.gimap)


def rows(arr, tm, grad=None, col=0, width=None):
    width = arr.shape[1] if width is None else width
    return Opd(arr, (tm, width), lambda i, c=col: (i, c), grad, (arr.shape[0], width), lambda i: (i, 0))


def cols(arr, tn, grad=None):
    return Opd(arr, (arr.shape[0], tn), lambda j: (0, j), grad)


def full(arr, grad=None):
    return Opd(arr, arr.shape, lambda i: (0, 0), grad)


def _load(ref):
    v = ref[...]
    return v.astype(F32) if jnp.issubdtype(v.dtype, jnp.floating) else v


def blocked_fwd(f, opds, outs, n, name):
    n_in = len(opds)

    def body(*refs):
        ys = f(*[_load(r) for r in refs[:n_in]])
        for r, y in zip(refs[n_in:], ys):
            r[...] = y.astype(r.dtype)

    res = pl.pallas_call(
        body, name=name, grid=(n,),
        in_specs=[o.spec() for o in opds],
        out_specs=[pl.BlockSpec(b, m) for (_, _, b, m) in outs],
        out_shape=[jax.ShapeDtypeStruct(s, d) for (s, d, _, _) in outs],
        compiler_params=_cparams(("parallel",)),
    )(*[o.arr for o in opds])
    return res


def blocked_bwd(f, opds, dys, n, name, plus=None):
    n_in, n_dy = len(opds), len(dys)
    diff = [i for i, o in enumerate(opds) if o.grad]
    extra = [] if plus is None else [plus]

    def body(*refs):
        vals = [_load(r) for r in refs[:n_in]]

        def fd(*dv):
            allv = list(vals)
            for i, v in zip(diff, dv):
                allv[i] = v
            return tuple(f(*allv))

        ys, vjp = jax.vjp(fd, *[vals[i] for i in diff])
        cts = tuple(_load(r).astype(y.dtype) for r, y in zip(refs[n_in:n_in + n_dy], ys))
        gs = list(vjp(cts))
        if extra:
            gs[0] = gs[0] + _load(refs[n_in + n_dy])
        for r, g, i in zip(refs[n_in + n_dy + len(extra):], gs, diff):
            if opds[i].grad == 'acc':
                @pl.when(pl.program_id(0) == 0)
                def _(r=r):
                    r[...] = jnp.zeros(r.shape, r.dtype)
                r[...] += g.astype(r.dtype)
            else:
                r[...] = g.astype(r.dtype)

    any_acc = any(opds[i].grad == 'acc' for i in diff)
    res = pl.pallas_call(
        body, name=name, grid=(n,),
        in_specs=[o.spec() for o in opds + dys + extra],
        out_specs=[opds[i].gspec() for i in diff],
        out_shape=[jax.ShapeDtypeStruct(opds[i].gshape, F32) for i in diff],
        compiler_params=_cparams(("arbitrary" if any_acc else "parallel",)),
    )(*[o.arr for o in opds + dys + extra])
    return res


def _tile(dim, want):
    for t in range(want - want % 16, 0, -16):
        if dim % t == 0:
            return t
    assert dim <= want, (dim, want)
    return dim


MATMUL_VMEM_BUDGET = 40 * 1024 * 1024
MATMUL_ROWS = 512


def _widest(N, fits):
    for t in range(N - N % 128, 0, -128):
        if N % t == 0 and fits(t):
            return t
    return N


def matmul(a, b, mode="nn", out_dtype=F32, add=None, name="matmul", into=None, col_blocks=None):
    sa, sb, so = a.dtype.itemsize, b.dtype.itemsize, jnp.dtype(out_dtype).itemsize
    has_add = add is not None
    if mode == "tn":
        (K, M), (K2, N) = a.shape, b.shape
        assert K == K2 and not has_add and out_dtype == F32, (a.shape, b.shape)
        tk = _tile(K, MATMUL_ROWS)
        tn = _widest(N, lambda t: 2 * (tk * M * sa + tk * t * sb + M * t * 4) <= MATMUL_VMEM_BUDGET)
        extra, alias = [], {}
        if into is not None:
            buf, lead = into[0], tuple(into[1:])
            if col_blocks is not None:
                assert N % col_blocks == 0 and (N // col_blocks) % 128 == 0 and tn >= N // col_blocks, (N, col_blocks, tn)
                tn = N // col_blocks
                assert buf.shape[len(lead):] == (M, tn), (buf.shape, lead, M, tn)
                out_spec = pl.BlockSpec((None,) * len(lead) + (M, tn), lambda j, k: lead[:-1] + (lead[-1] + j, 0, 0))
            else:
                assert buf.shape[len(lead):] == (M, N), (buf.shape, lead, M, N)
                out_spec = pl.BlockSpec((None,) * len(lead) + (M, tn), lambda j, k: lead + (0, j))
            out_shape = jax.ShapeDtypeStruct(buf.shape, F32)
            extra, alias = [buf], {2: 0}
        else:
            assert col_blocks is None
            out_spec = pl.BlockSpec((M, tn), lambda j, k: (0, j))
            out_shape = jax.ShapeDtypeStruct((M, N), F32)

        def body(a_ref, b_ref, *rest):
            o_ref = rest[-1]
            r = lax.dot_general(a_ref[...].astype(MXU_DTYPE), b_ref[...].astype(MXU_DTYPE), ((_TN), ((), ())),
                                preferred_element_type=F32)

            @pl.when(pl.program_id(1) == 0)
            def _():
                o_ref[...] = r

            @pl.when(pl.program_id(1) > 0)
            def _():
                o_ref[...] += r

        return pl.pallas_call(
            body, name=name, grid=(N // tn, K // tk),
            in_specs=[pl.BlockSpec((tk, M), lambda j, k: (k, 0)), pl.BlockSpec((tk, tn), lambda j, k: (k, j))]
            + [pl.BlockSpec(memory_space=pl.ANY)] * len(extra),
            out_specs=out_spec, out_shape=out_shape, input_output_aliases=alias,
            compiler_params=_cparams(("parallel", "arbitrary")),
        )(a, b, *extra)

    (M, K) = a.shape
    N = b.shape[1] if mode == "nn" else b.shape[0]
    assert K == (b.shape[0] if mode == "nn" else b.shape[1]), (a.shape, b.shape, mode)
    tm = _tile(M, MATMUL_ROWS)
    tn = _widest(N, lambda t: 2 * (tm * K * sa + K * t * sb + tm * t * (so + 4 * has_add)) <= MATMUL_VMEM_BUDGET)
    dims = ((_NN if mode == "nn" else _NT), ((), ()))

    def body(*refs):
        r = lax.dot_general(refs[0][...].astype(MXU_DTYPE), refs[1][...].astype(MXU_DTYPE), dims, preferred_element_type=F32)
        if has_add:
            r = r + refs[2][...].astype(F32)
        refs[-1][...] = r.astype(refs[-1].dtype)

    b_spec = pl.BlockSpec((K, tn), lambda j, i: (0, j)) if mode == "nn" else pl.BlockSpec((tn, K), lambda j, i: (j, 0))
    in_specs = [pl.BlockSpec((tm, K), lambda j, i: (i, 0)), b_spec]
    args = [a, b]
    if has_add:
        in_specs.append(pl.BlockSpec((tm, tn), lambda j, i: (i, j)))
        args.append(add)
    return pl.pallas_call(
        body, name=name, grid=(N // tn, M // tm),
        in_specs=in_specs,
        out_specs=pl.BlockSpec((tm, tn), lambda j, i: (i, j)),
        out_shape=jax.ShapeDtypeStruct((M, N), out_dtype),
        compiler_params=_cparams(("parallel", "parallel")),
    )(*args)


def _dot(a, b, dims, precision=None):
    if precision is None:
        a, b = a.astype(MXU_DTYPE), b.astype(MXU_DTYPE)
    return lax.dot_general(a, b, (dims, ((), ())), precision=precision, preferred_element_type=F32)


_NN = ((1,), (0,))
_NT = ((1,), (1,))
_TN = ((0,), (0,))


def _rms(x, gain):
    return x * lax.rsqrt(jnp.mean(x * x, axis=-1, keepdims=True) + EPS) * gain


def _hg_block(st_t, q, fl, iv, g, lb, gain):
    row = lax.broadcasted_iota(jnp.int32, (HG_SUB, HG_SUB), 0)
    col = lax.broadcasted_iota(jnp.int32, (HG_SUB, HG_SUB), 1)
    tri = (row >= col).astype(F32)
    heads = [slice(h * HG_DIM, (h + 1) * HG_DIM) for h in range(HG_HEADS)]
    fg = lb + (1.0 - lb) * jax.nn.sigmoid(fl)
    lf, kk, qf = jnp.log(fg), 1.0 - fg, jax.nn.silu(q)
    sts = [st_t[sl, :] for sl in heads]
    parts = [[] for _ in heads]
    for s in range(q.shape[0] // HG_SUB):
        r = slice(s * HG_SUB, (s + 1) * HG_SUB)
        b = lf[r]
        d = 1
        while d < HG_SUB:
            b = b + _shift_rows(b, d)
            d *= 2
        b_mid = jnp.sum(lf[r][:HG_SUB // 2], axis=0, keepdims=True)
        b_end = jnp.sum(lf[r], axis=0, keepdims=True)
        q_in, k_in = qf[r] * jnp.exp(b - b_mid), kk[r] * jnp.exp(b_mid - b)
        q_st, k_st, decay = qf[r] * jnp.exp(b), kk[r] * jnp.exp(b_end - b), jnp.exp(b_end)
        for h, sl in enumerate(heads):
            sc = _dot(q_in[:, sl], k_in[:, sl], _NT) * tri
            parts[h].append(_dot(sc, iv[r, sl], _NN) + _dot(q_st[:, sl], sts[h], _NT))
            sts[h] = sts[h] * decay[:, sl] + _dot(iv[r, sl], k_st[:, sl], _TN)
    outs = [_rms(jnp.concatenate(parts[h], axis=0), gain[:, sl]) * jax.nn.silu(g[:, sl]) for h, sl in enumerate(heads)]
    return jnp.concatenate(sts, axis=0), jnp.concatenate(outs, axis=1)


def _hg_specs(proj, nb):
    return [pl.BlockSpec((HG_BLOCK, HG_WIDTH), lambda i, c=c, f=nb: (f(i), c)) for c in range(4)]


def hgrn2_fwd(proj, lb, gain):
    L = proj.shape[0]
    n = L // HG_BLOCK

    def body(q, fl, iv, g, lb_r, gain_r, o_ref, st_ref, st):
        @pl.when(pl.program_id(0) == 0)
        def _():
            st[...] = jnp.zeros(st.shape, F32)

        st_ref[0] = st[...]
        new, o = _hg_block(st[...], q[...], fl[...], iv[...], g[...], lb_r[...], gain_r[...])
        st[...] = new
        o_ref[...] = o.astype(o_ref.dtype)

    pspec = pl.BlockSpec((1, HG_WIDTH), lambda i: (0, 0))
    return pl.pallas_call(
        body, name="hgrn2_fwd", grid=(n,),
        in_specs=_hg_specs(proj, lambda i: i) + [pspec, pspec],
        out_specs=[pl.BlockSpec((HG_BLOCK, HG_WIDTH), lambda i: (i, 0)),
                   pl.BlockSpec((1, HG_WIDTH, HG_DIM), lambda i: (i, 0, 0))],
        out_shape=[jax.ShapeDtypeStruct((L, HG_WIDTH), MXU_DTYPE),
                   jax.ShapeDtypeStruct((n, HG_WIDTH, HG_DIM), F32)],
        scratch_shapes=[pltpu.VMEM((HG_WIDTH, HG_DIM), F32)],
        compiler_params=_cparams(("arbitrary",)),
    )(proj, proj, proj, proj, lb, gain)


def hgrn2_bwd(proj, lb, gain, states, do):
    L = proj.shape[0]
    n = L // HG_BLOCK

    def body(q, fl, iv, g, lb_r, gain_r, st_r, do_r, dproj, dlb, dgain, dst):
        @pl.when(pl.program_id(0) == 0)
        def _():
            dst[...] = jnp.zeros(dst.shape, F32)
            dlb[...] = jnp.zeros(dlb.shape, F32)
            dgain[...] = jnp.zeros(dgain.shape, F32)

        _, vjp = jax.vjp(_hg_block, st_r[0], q[...], fl[...], iv[...], g[...], lb_r[...], gain_r[...])
        d_st, dq, dfl, div, dg, d_lb, d_gain = vjp((dst[...], do_r[...].astype(F32)))
        dst[...] = d_st
        dproj[:, 0 * HG_WIDTH:1 * HG_WIDTH] = dq
        dproj[:, 1 * HG_WIDTH:2 * HG_WIDTH] = dfl
        dproj[:, 2 * HG_WIDTH:3 * HG_WIDTH] = div
        dproj[:, 3 * HG_WIDTH:4 * HG_WIDTH] = dg
        dlb[...] += d_lb
        dgain[...] += d_gain

    rev = lambda i: n - 1 - i
    pspec = pl.BlockSpec((1, HG_WIDTH), lambda i: (0, 0))
    return pl.pallas_call(
        body, name="hgrn2_bwd", grid=(n,),
        in_specs=_hg_specs(proj, rev) + [pspec, pspec,
                                         pl.BlockSpec((1, HG_WIDTH, HG_DIM), lambda i: (rev(i), 0, 0)),
                                         pl.BlockSpec((HG_BLOCK, HG_WIDTH), lambda i: (rev(i), 0))],
        out_specs=[pl.BlockSpec((HG_BLOCK, 4 * HG_WIDTH), lambda i: (rev(i), 0)), pspec, pspec],
        out_shape=[jax.ShapeDtypeStruct((L, 4 * HG_WIDTH), F32),
                   jax.ShapeDtypeStruct((1, HG_WIDTH), F32), jax.ShapeDtypeStruct((1, HG_WIDTH), F32)],
        scratch_shapes=[pltpu.VMEM((HG_WIDTH, HG_DIM), F32)],
        compiler_params=_cparams(("arbitrary",)),
    )(proj, proj, proj, proj, lb, gain, states, do)


def _rope_rms(x, gain_p, cos_p, sin_p):
    n = x * lax.rsqrt(jnp.sum(x * x, axis=-1, keepdims=True) * (1.0 / MLA_ROPE) + EPS) * gain_p
    r = lax.broadcasted_iota(jnp.int32, (128, 128), 0)
    c = lax.broadcasted_iota(jnp.int32, (128, 128), 1)
    swap = (r == (c + 64) % 128).astype(F32)
    return n * cos_p + _dot(n, swap, _NN, HI) * sin_p


MLA_IN = MLA_Q_RANK + MLA_KV_RANK + 128


def _mla_prep(x, cos_p, sin_p, q_a, w_uq, kv_a, w_ukv, qn_nope, qn_rope, kn_nope, kn_rope):
    c_q, c_kv, kpe = x[:, :MLA_Q_RANK], x[:, MLA_Q_RANK:MLA_Q_RANK + MLA_KV_RANK], x[:, MLA_Q_RANK + MLA_KV_RANK:]
    q = _dot(_rms(c_q, q_a), w_uq, _NN)
    kv = _dot(_rms(c_kv, kv_a), w_ukv, _NN)
    k_pe = _rope_rms(kpe, kn_rope, cos_p, sin_p)
    qs, ks = [], []
    for h in range(MLA_HEADS):
        qs.append(_rms(q[:, h * MLA_DK:h * MLA_DK + MLA_NOPE], qn_nope))
        qs.append(_rope_rms(q[:, h * MLA_DK + MLA_NOPE:(h + 1) * MLA_DK], qn_rope, cos_p, sin_p))
        ks.append(_rms(kv[:, h * MLA_NOPE:(h + 1) * MLA_NOPE], kn_nope))
        ks.append(k_pe)
    return jnp.concatenate(qs, axis=1), jnp.concatenate(ks, axis=1), kv[:, MLA_HEADS * MLA_NOPE:]


def _mla_prep_opds(proj, cos_p, sin_p, params, tm, grads):
    g = (lambda k: k) if grads else (lambda k: None)
    assert (4 * HG_WIDTH) % MLA_IN == 0
    return ([rows(proj, tm, g('blk'), col=4 * HG_WIDTH // MLA_IN, width=MLA_IN), rows(cos_p, tm), rows(sin_p, tm)]
            + [full(p, g('acc')) for p in params])


def mla_prep_fwd(proj, cos_p, sin_p, params, tm):
    L = proj.shape[0]
    W = MLA_HEADS * MLA_DK
    rb = lambda w: (tm, w)
    outs = [((L, W), MXU_DTYPE, rb(W), lambda i: (i, 0)), ((L, W), MXU_DTYPE, rb(W), lambda i: (i, 0)),
            ((L, MLA_HEADS * MLA_V), MXU_DTYPE, rb(MLA_HEADS * MLA_V), lambda i: (i, 0))]
    return blocked_fwd(_mla_prep, _mla_prep_opds(proj, cos_p, sin_p, params, tm, False), outs, L // tm, "mla_prep_fwd")


def mla_prep_bwd(proj, cos_p, sin_p, params, dq, dk, dv, tm):
    L = proj.shape[0]
    return blocked_bwd(_mla_prep, _mla_prep_opds(proj, cos_p, sin_p, params, tm, True),
                       [rows(dq, tm), rows(dk, tm), rows(dv, tm)], L // tm, "mla_prep_bwd")


def _scores(q, k, scale, shift=None):
    s = _dot(q, k, _NT) * scale
    if shift is None:
        return s
    row = lax.broadcasted_iota(jnp.int32, s.shape, 0)
    col = lax.broadcasted_iota(jnp.int32, s.shape, 1)
    return jnp.where(col <= row + shift, s, -jnp.inf)


ATTN_ROWS = 512
ATTN_WIDE = 2


def attn_fwd(q, k, v, scale, t):
    L = q.shape[0]
    tq = ATTN_WIDE * t

    def body(q_ref, k_ref, v_ref, o_ref, lse_ref):
        i = pl.program_id(1)
        qb = q_ref[...]

        def step(j, carry, shift=None):
            m, l, acc = carry
            kj = k_ref[pl.ds(pl.multiple_of(j * t, t), t), :]
            vj = v_ref[pl.ds(pl.multiple_of(j * t, t), t), :]
            s = _scores(qb, kj, scale, shift)
            m_new = jnp.maximum(m, jnp.max(s, axis=-1, keepdims=True))
            p = jnp.exp(s - m_new)
            alpha = jnp.exp(m - m_new)
            return m_new, alpha * l + jnp.sum(p, axis=-1, keepdims=True), alpha * acc + _dot(p, vj, _NN)

        carry = (jnp.full((tq, 1), -jnp.inf, F32), jnp.zeros((tq, 1), F32), jnp.zeros((tq, MLA_V), F32))
        carry = lax.fori_loop(0, ATTN_WIDE * i, step, carry)
        for d in range(ATTN_WIDE):
            carry = step(ATTN_WIDE * i + d, carry, -d * t)
        m, l, acc = carry
        o_ref[...] = acc / l
        lse_ref[...] = jnp.broadcast_to(m + jnp.log(l), lse_ref.shape)

    hspec = lambda rows_, w: pl.BlockSpec((rows_, w), lambda h, i: (0, h))
    bspec = lambda w: pl.BlockSpec((tq, w), lambda h, i: (i, h))
    return pl.pallas_call(
        body, name="attn_fwd", grid=(MLA_HEADS, L // tq),
        in_specs=[bspec(MLA_DK), hspec(L, MLA_DK), hspec(L, MLA_V)],
        out_specs=[bspec(MLA_V), bspec(MLA_V)],
        out_shape=[jax.ShapeDtypeStruct((L, MLA_HEADS * MLA_V), F32)] * 2,
        compiler_params=_cparams(("parallel", "parallel")),
    )(q, k, v)


def attn_bwd_dq(q, k, v, o, lse, do, scale, t):
    L = q.shape[0]
    tq = ATTN_WIDE * t

    def body(q_ref, k_ref, v_ref, o_ref, lse_ref, do_ref, dq_ref):
        i = pl.program_id(1)
        qb, dob = q_ref[...], do_ref[...]
        delta = jnp.sum(dob * o_ref[...], axis=-1, keepdims=True)
        lse_c = jnp.max(lse_ref[...], axis=-1, keepdims=True)

        def step(j, dq, shift=None):
            kj = k_ref[pl.ds(pl.multiple_of(j * t, t), t), :]
            vj = v_ref[pl.ds(pl.multiple_of(j * t, t), t), :]
            p = jnp.exp(_scores(qb, kj, scale, shift) - lse_c)
            ds = p * (_dot(dob, vj, _NT) - delta) * scale
            return dq + _dot(ds, kj, _NN)

        dq = lax.fori_loop(0, ATTN_WIDE * i, step, jnp.zeros((tq, MLA_DK), F32))
        for d in range(ATTN_WIDE):
            dq = step(ATTN_WIDE * i + d, dq, -d * t)
        dq_ref[...] = dq

    hspec = lambda w: pl.BlockSpec((L, w), lambda h, i: (0, h))
    bspec = lambda w: pl.BlockSpec((tq, w), lambda h, i: (i, h))
    return pl.pallas_call(
        body, name="attn_bwd_dq", grid=(MLA_HEADS, L // tq),
        in_specs=[bspec(MLA_DK), hspec(MLA_DK), hspec(MLA_V), bspec(MLA_V), bspec(MLA_V), bspec(MLA_V)],
        out_specs=bspec(MLA_DK),
        out_shape=jax.ShapeDtypeStruct((L, MLA_HEADS * MLA_DK), F32),
        compiler_params=_cparams(("parallel", "parallel")),
    )(q, k, v, o, lse, do)


def attn_bwd_dkv(q, k, v, o, lse, do, scale, t):
    L = q.shape[0]
    tk = ATTN_WIDE * t

    def body(q_ref, k_ref, v_ref, o_ref, lse_ref, do_ref, dk_ref, dv_ref):
        j = pl.program_id(1)
        kb, vb = k_ref[...], v_ref[...]

        def step(i, carry, shift=None):
            dk, dv = carry
            r = pl.ds(pl.multiple_of(i * t, t), t)
            qi, doi = q_ref[r, :], do_ref[r, :]
            delta = jnp.sum(doi * o_ref[r, :], axis=-1, keepdims=True)
            lse_c = jnp.max(lse_ref[r, :], axis=-1, keepdims=True)
            p = jnp.exp(_scores(qi, kb, scale, shift) - lse_c)
            ds = p * (_dot(doi, vb, _NT) - delta) * scale
            return dk + _dot(ds, qi, _TN), dv + _dot(p, doi, _TN)

        carry = (jnp.zeros((tk, MLA_DK), F32), jnp.zeros((tk, MLA_V), F32))
        for d in range(ATTN_WIDE):
            carry = step(ATTN_WIDE * j + d, carry, d * t)
        dk, dv = lax.fori_loop(ATTN_WIDE * (j + 1), L // t, step, carry)
        dk_ref[...] = dk
        dv_ref[...] = dv

    hspec = lambda w: pl.BlockSpec((L, w), lambda h, j: (0, h))
    bspec = lambda w: pl.BlockSpec((tk, w), lambda h, j: (j, h))
    return pl.pallas_call(
        body, name="attn_bwd_dkv", grid=(MLA_HEADS, L // tk),
        in_specs=[hspec(MLA_DK), bspec(MLA_DK), bspec(MLA_V), hspec(MLA_V), hspec(MLA_V), hspec(MLA_V)],
        out_specs=[bspec(MLA_DK), bspec(MLA_V)],
        out_shape=[jax.ShapeDtypeStruct((L, MLA_HEADS * MLA_DK), F32), jax.ShapeDtypeStruct((L, MLA_HEADS * MLA_V), F32)],
        compiler_params=_cparams(("parallel", "parallel")),
    )(q, k, v, o, lse, do)


S5_LANES = S5_GB * S5_STATE
S5_BWD_CHUNK = 1024


def _cmul(ar, ai, br, bi):
    return ar * br - ai * bi, ar * bi + ai * br


def _a_powers(ar, ai, reverse):
    a2 = _cmul(ar, ai, ar, ai)
    a4 = _cmul(*a2, *a2)
    row = lax.broadcasted_iota(jnp.int32, (8, ar.shape[1]), 0)
    e = (8 - row) if reverse else (row + 1)
    tr, ti = jnp.ones((8, ar.shape[1]), F32), jnp.zeros((8, ar.shape[1]), F32)
    for bit, (pr, pi) in ((1, (ar, ai)), (2, a2), (4, a4), (8, _cmul(*a4, *a4))):
        nr, ni = _cmul(tr, ti, pr, pi)
        sel = (e & bit) != 0
        tr, ti = jnp.where(sel, nr, tr), jnp.where(sel, ni, ti)
    pows = []
    for d, (pr, pi) in zip((1, 2, 4), ((ar, ai), a2, a4)):
        keep = (row < 8 - d) if reverse else (row >= d)
        pows.append((jnp.where(keep, pr, 0.0), jnp.where(keep, pi, 0.0)))
    return pows, (tr, ti)


def _scan8(xr, xi, pows, table, cr, ci, reverse):
    for d, (pr, pi) in zip((1, 2, 4), pows):
        shift = 8 - d if reverse else d
        mr, mi = _cmul(pr, pi, pltpu.roll(xr, shift, 0), pltpu.roll(xi, shift, 0))
        xr, xi = xr + mr, xi + mi
    mr, mi = _cmul(table[0], table[1], cr, ci)
    return xr + mr, xi + mi


def _row_of(x, r):
    row = lax.broadcasted_iota(jnp.int32, x.shape, 0)
    return jnp.sum(jnp.where(row == r, x, 0.0), axis=0, keepdims=True)


def _s5_scan_fwd(h_re, h_im, ar, ai, L):
    pows, table = _a_powers(ar, ai, False)

    def step(i, carry):
        r = pl.ds(pl.multiple_of(i * 8, 8), 8)
        xr, xi = _scan8(h_re[r, :], h_im[r, :], pows, table, carry[0], carry[1], False)
        h_re[r, :] = xr
        h_im[r, :] = xi
        return xr[7:8, :], xi[7:8, :]

    z = jnp.zeros((1, ar.shape[1]), F32)
    lax.fori_loop(0, L // 8, step, (z, z))


def _s5_specs(L):
    return [pl.BlockSpec((L, 128), lambda g: (0, g)),
            pl.BlockSpec((1, 128, S5_LANES), lambda g: (g, 0, 0)), pl.BlockSpec((1, 128, S5_LANES), lambda g: (g, 0, 0)),
            pl.BlockSpec((1, 1, S5_LANES), lambda g: (g, 0, 0)), pl.BlockSpec((1, 1, S5_LANES), lambda g: (g, 0, 0)),
            pl.BlockSpec((1, S5_LANES, 128), lambda g: (g, 0, 0)), pl.BlockSpec((1, S5_LANES, 128), lambda g: (g, 0, 0))]


def s5_fwd(u, w_re, w_im, a_re, a_im, c_re, c_im):
    L, D = u.shape

    def body(u_ref, wr, wi, ar, ai, cr, ci, y_ref, h_re, h_im):
        ub = u_ref[...]
        h_re[...] = _dot(ub, wr[0], _NN)
        h_im[...] = _dot(ub, wi[0], _NN)
        _s5_scan_fwd(h_re, h_im, ar[0], ai[0], L)
        y_ref[...] = _dot(h_re[...], cr[0], _NN) - _dot(h_im[...], ci[0], _NN)

    return pl.pallas_call(
        body, name="s5_fwd", grid=(D // 128,),
        in_specs=_s5_specs(L), out_specs=pl.BlockSpec((L, 128), lambda g: (0, g)),
        out_shape=jax.ShapeDtypeStruct((L, D), F32),
        scratch_shapes=[pltpu.VMEM((L, S5_LANES), F32), pltpu.VMEM((L, S5_LANES), F32)],
        compiler_params=_cparams(("parallel",)),
    )(u, w_re, w_im, a_re, a_im, c_re, c_im)


def s5_bwd(u, w_re, w_im, a_re, a_im, c_re, c_im, dy, tc):
    L, D = u.shape
    nch = L // tc

    def body(u_ref, wr, wi, ar_ref, ai_ref, cr, ci, dy_ref, du_ref, dwr, dwi, dar, dai, dcr, dci, h_re, h_im, g_re, g_im):
        ar, ai = ar_ref[0], ai_ref[0]
        ub = u_ref[...]
        h_re[...] = _dot(ub, wr[0], _NN)
        h_im[...] = _dot(ub, wi[0], _NN)
        _s5_scan_fwd(h_re, h_im, ar, ai, L)
        dyb = dy_ref[...]
        dcr[0] = _dot(h_re[...], dyb, _TN)
        dci[0] = -_dot(h_im[...], dyb, _TN)
        pows, table = _a_powers(ar, -ai, True)
        dwr[0] = jnp.zeros((128, S5_LANES), F32)
        dwi[0] = jnp.zeros((128, S5_LANES), F32)
        z1 = jnp.zeros((1, S5_LANES), F32)
        z8 = jnp.zeros((8, S5_LANES), F32)

        def chunk(cc, carry):
            c0 = pl.multiple_of((nch - 1 - cc) * tc, tc)
            rows_c = pl.ds(c0, tc)
            dyc = dy_ref[rows_c, :]
            g_re[...] = _dot(dyc, cr[0], _NT)
            g_im[...] = -_dot(dyc, ci[0], _NT)

            def step(ii, cy):
                gr_c, gi_c, acc_r, acc_i = cy
                i8 = pl.multiple_of((tc // 8 - 1 - ii) * 8, 8)
                rl = pl.ds(i8, 8)
                xr, xi = _scan8(g_re[rl, :], g_im[rl, :], pows, table, gr_c, gi_c, True)
                g_re[rl, :] = xr
                g_im[rl, :] = xi
                t0 = c0 + i8
                hb_r, hb_i = h_re[pl.ds(t0, 8), :], h_im[pl.ds(t0, 8), :]
                tp = pl.multiple_of(jnp.maximum(t0 - 8, 0), 8)
                first = (t0 > 0).astype(F32)
                pr = h_re[pl.ds(tp, 8), :][7:8, :] * first
                pi = h_im[pl.ds(tp, 8), :][7:8, :] * first
                row = lax.broadcasted_iota(jnp.int32, xr.shape, 0)
                hp_r = jnp.where(row == 0, pr, pltpu.roll(hb_r, 1, 0))
                hp_i = jnp.where(row == 0, pi, pltpu.roll(hb_i, 1, 0))
                return (xr[0:1, :], xi[0:1, :],
                        acc_r + xr * hp_r + xi * hp_i, acc_i + xi * hp_r - xr * hp_i)

            cy = lax.fori_loop(0, tc // 8, step, carry)
            uc = u_ref[rows_c, :]
            gr, gi = g_re[...], g_im[...]
            du_ref[rows_c, :] = _dot(gr, wr[0], _NT) + _dot(gi, wi[0], _NT)
            dwr[0] += _dot(uc, gr, _TN)
            dwi[0] += _dot(uc, gi, _TN)
            return cy

        _, _, acc_r, acc_i = lax.fori_loop(0, nch, chunk, (z1, z1, z8, z8))
        dar[0] = jnp.sum(acc_r, axis=0, keepdims=True)
        dai[0] = jnp.sum(acc_i, axis=0, keepdims=True)

    specs = _s5_specs(L)
    return pl.pallas_call(
        body, name="s5_bwd", grid=(D // 128,),
        in_specs=specs + [pl.BlockSpec((L, 128), lambda g: (0, g))],
        out_specs=[pl.BlockSpec((L, 128), lambda g: (0, g))] + specs[1:],
        out_shape=[jax.ShapeDtypeStruct((L, D), F32)] + [jax.ShapeDtypeStruct(x.shape, F32)
                                                        for x in (w_re, w_im, a_re, a_im, c_re, c_im)],
        scratch_shapes=[pltpu.VMEM((L, S5_LANES), F32), pltpu.VMEM((L, S5_LANES), F32),
                        pltpu.VMEM((tc, S5_LANES), F32), pltpu.VMEM((tc, S5_LANES), F32)],
        compiler_params=_cparams(("parallel",)),
    )(u, w_re, w_im, a_re, a_im, c_re, c_im, dy)


def _s5_discretize(lr, li, ldt, br, bi):
    dt = jnp.exp(ldt)
    mag = jnp.exp(lr * dt)
    ar, ai = mag * jnp.cos(li * dt), mag * jnp.sin(li * dt)
    den = lr * lr + li * li
    zr = ((ar - 1.0) * lr + ai * li) / den
    zi = (ai * lr - (ar - 1.0) * li) / den
    p = lax.broadcasted_iota(jnp.int32, (S5_STATE, S5_STATE * S5_GROUP), 0)
    c = lax.broadcasted_iota(jnp.int32, (S5_STATE, S5_STATE * S5_GROUP), 1)
    rep = (c // S5_GROUP == p).astype(F32)
    zr, zi = _dot(zr, rep, _NN, HI), _dot(zi, rep, _NN, HI)
    return ar, ai, zr * br - zi * bi, zr * bi + zi * br


def _conv_shift(x, d):
    row = lax.broadcasted_iota(jnp.int32, x.shape, 0)
    return jnp.where(row >= d, pltpu.roll(x, d, 0), 0.0)


def _conv_unshift(x, d):
    n = x.shape[0]
    row = lax.broadcasted_iota(jnp.int32, x.shape, 0)
    return jnp.where(row < n - d, pltpu.roll(x, n - d, 0), 0.0)


@functools.partial(jax.custom_vjp, nondiff_argnums=(1,))
def _shift_rows(x, d):
    return _conv_shift(x, d)


_shift_rows.defvjp(lambda x, d: (_conv_shift(x, d), None), lambda d, _, g: (_conv_unshift(g, d),))


def _conv_gate(ug, uv, wg0, wg1, wg2, wv0, wv1, wv2, bg, bv):
    def conv(u, w0, w1, w2, b):
        return u * w2 + _shift_rows(u, 1) * w1 + _shift_rows(u, 2) * w0 + b
    return (jax.nn.silu(conv(ug, wg0, wg1, wg2, bg)) * conv(uv, wv0, wv1, wv2, bv),)


def _rms_fn(x, gain):
    return (_rms(x, gain),)


def _softmax_rows(s):
    e = jnp.exp(s - lax.stop_gradient(jnp.max(s, axis=-1, keepdims=True)))
    return e / jnp.sum(e, axis=-1, keepdims=True)


def _xa_core(qp, k, v, q_gain):
    dh = qp.shape[1] // XA_HEADS
    outs = []
    for h in range(XA_HEADS):
        sl = slice(h * dh, (h + 1) * dh)
        p = _softmax_rows(_dot(_rms(qp[:, sl], q_gain), k[:, sl], _NT) * (dh ** -0.5))
        outs.append(_dot(p, v[:, sl], _NN))
    return (jnp.concatenate(outs, axis=1),)


def _mem_kv(mem, mem_gain, wk, wv, k_gain):
    m = _rms(mem, mem_gain)
    kp = _dot(m, wk, _NN)
    dh = kp.shape[1] // XA_HEADS
    k = jnp.concatenate([_rms(kp[:, h * dh:(h + 1) * dh], k_gain) for h in range(XA_HEADS)], axis=1)
    return k, _dot(m, wv, _NN)


def _s5_post(y, u, d):
    return (jax.nn.gelu(y + d * u),)


def _glu(a, b):
    return (a * jax.nn.sigmoid(b),)


def _lb_first(logits):
    e = jnp.exp(logits - lax.stop_gradient(jnp.max(logits, axis=0, keepdims=True)))
    return (_row_of(e, 0) / jnp.sum(e, axis=0, keepdims=True),)


def _loss_fn(y, t):
    e = y - t
    part = 0.5 * jnp.sum(e * e) / y.shape[1]
    return e * (1.0 / y.shape[1]), jnp.full((8, 128), part / (8 * 128), F32)


def _out(shape, dtype, tm):
    return (shape, dtype, (tm, shape[1]), lambda i: (i, 0))


def rms_fwd(h, gain, tm, dtype):
    return blocked_fwd(_rms_fn, [rows(h, tm), full(gain)], [_out(h.shape, dtype, tm)], h.shape[0] // tm, "rms_fwd")[0]


def rms_bwd(h, gain, dy, tm, residual):
    return blocked_bwd(_rms_fn, [rows(h, tm, 'blk'), full(gain, 'acc')], [rows(dy, tm)], h.shape[0] // tm, "rms_bwd",
                       plus=rows(residual, tm))


def _adamw_math(w, g, m, v):
    m = ADAM_B1 * m + (1.0 - ADAM_B1) * g
    v = ADAM_B2 * v + (1.0 - ADAM_B2) * jnp.square(g)
    m_hat = m / (1.0 - ADAM_B1 ** ADAM_STEP)
    v_hat = v / (1.0 - ADAM_B2 ** ADAM_STEP)
    return -ADAM_LR * (m_hat / (jnp.sqrt(v_hat) + ADAM_EPS) + ADAM_WD * w), m, v


def adamw(w, g, m, v, name):
    R = w.shape[0]
    tm = _tile(R, 256)
    assert g.shape == w.shape == m.shape == v.shape, (name, w.shape, g.shape)

    def body(w_ref, g_ref, m_ref, v_ref, d_ref, nm_ref, nv_ref):
        d_ref[...], nm_ref[...], nv_ref[...] = _adamw_math(w_ref[...], g_ref[...], m_ref[...], v_ref[...])

    spec = pl.BlockSpec((tm, w.shape[1]), lambda i: (i, 0))
    return pl.pallas_call(
        body, name=name, grid=(R // tm,), in_specs=[spec] * 4, out_specs=[spec] * 3,
        out_shape=[jax.ShapeDtypeStruct(w.shape, F32)] * 3, compiler_params=_cparams(("parallel",)),
    )(w, g, m, v)


def _index_operand(i):
    return jnp.reshape(i, (1,)).astype(jnp.int32)


def adamw_layer(w, mine, other, core, m, v, layer, bufs, name):
    H, C = mine.shape
    tm = _tile(H, 256)
    nb = H // tm
    assert w.shape[1:] == (2 * H, C) and all(b.shape == w.shape for b in bufs), (name, w.shape, mine.shape)

    def body(c_ref, w_ref, mine_ref, other_ref, m_ref, v_ref, *rest):
        g_out, d_ref, nm_ref, nv_ref = rest[-4:]
        g_ = jnp.where(pl.program_id(0) // nb == c_ref[0], mine_ref[...], other_ref[...])
        g_out[...] = g_
        d_ref[...], nm_ref[...], nv_ref[...] = _adamw_math(w_ref[...], g_, m_ref[...], v_ref[...])

    lspec = pl.BlockSpec((None, tm, C), lambda i, c: (layer, i, 0))
    half = lambda sign: pl.BlockSpec(
        (tm, C), lambda i, c: (jnp.clip(i - (c[0] if sign else 1 - c[0]) * nb, 0, nb - 1), 0))
    any_spec = pl.BlockSpec(memory_space=pl.ANY)
    grid_spec = pltpu.PrefetchScalarGridSpec(
        num_scalar_prefetch=1, grid=(2 * nb,),
        in_specs=[lspec, half(True), half(False), lspec, lspec] + [any_spec] * 4, out_specs=[lspec] * 4)
    return pl.pallas_call(
        body, name=name, grid_spec=grid_spec, out_shape=[jax.ShapeDtypeStruct(w.shape, F32)] * 4,
        input_output_aliases={6: 0, 7: 1, 8: 2, 9: 3}, compiler_params=_cparams(("parallel",)),
    )(_index_operand(core), w, mine, other, m, v, *bufs)


def add_own_half(x, core, theirs, name, out_dtype):
    nq, _, H, C = x.shape
    tm = _tile(H, 256)

    def body(c_ref, x_ref, t_ref, o_ref):
        o_ref[...] = (x_ref[...] + t_ref[...]).astype(o_ref.dtype)

    spec = pl.BlockSpec((None, tm, C), lambda q, i, c: (q, i, 0))
    grid_spec = pltpu.PrefetchScalarGridSpec(
        num_scalar_prefetch=1, grid=(nq, H // tm),
        in_specs=[pl.BlockSpec((None, None, tm, C), lambda q, i, c: (q, c[0], i, 0)), spec], out_specs=spec)
    return pl.pallas_call(
        body, name=name, grid_spec=grid_spec, out_shape=jax.ShapeDtypeStruct((nq, H, C), out_dtype),
        compiler_params=_cparams(("parallel", "parallel")),
    )(_index_operand(core), x, theirs)


def add_chips(pair, chip, got, name):
    n, R, C = got.shape
    tm = _tile(R, 256)

    def body(q_ref, *refs):
        acc = refs[0][...].astype(F32)
        for r in refs[1:-1]:
            acc = acc + r[...].astype(F32)
        refs[-1][...] = acc

    grid_spec = pltpu.PrefetchScalarGridSpec(
        num_scalar_prefetch=1, grid=(R // tm,),
        in_specs=[pl.BlockSpec((None, tm, C), lambda i, q: (q[0], i, 0))]
        + [pl.BlockSpec((None, tm, C), lambda i, q, j=j: (j, i, 0)) for j in range(n)],
        out_specs=pl.BlockSpec((tm, C), lambda i, q: (i, 0)))
    return pl.pallas_call(
        body, name=name, grid_spec=grid_spec, out_shape=jax.ShapeDtypeStruct((R, C), F32),
        compiler_params=_cparams(("parallel",)),
    )(_index_operand(chip), pair, *([got] * n))


_HBM = pl.BlockSpec(memory_space=pltpu.HBM)
N_CHIPS = 4


def _my_place():
    return lax.axis_index("x"), lax.axis_index("y"), lax.axis_index("c")


def _window(ref, axis, start, size):
    idx = [slice(None)] * len(ref.shape)
    idx[axis] = pl.ds(start, size)
    return ref.at[tuple(idx)]


def _comm_call(body, name, xs, out_shapes, n_remote, n_local, sequencer=None):
    sems = [pltpu.SemaphoreType.DMA((n_remote,)), pltpu.SemaphoreType.DMA((n_remote,)),
            pltpu.SemaphoreType.DMA((max(n_local, 1),))]
    if sequencer is None:
        return pl.pallas_call(
            body, name=name, in_specs=[_HBM] * len(xs), out_specs=[_HBM] * len(out_shapes), out_shape=out_shapes,
            scratch_shapes=sems, compiler_params=pltpu.CompilerParams(has_side_effects=True),
        )(*xs)
    peers_of, collective_id = sequencer
    hbm = pltpu.MemorySpace.HBM
    x_refs = [jax.new_ref(x, memory_space=hbm) for x in xs]
    o_refs = [jax.empty_ref(s, memory_space=hbm) for s in out_shapes]

    @pl.kernel(mesh=plsc.ScalarSubcoreMesh(axis_name="sequencer", num_cores=1), name=name, scratch_types=tuple(sems),
               compiler_params=pltpu.CompilerParams(collective_id=collective_id))
    def launch(send_sems, recv_sems, local_sems):
        peers = peers_of(*_my_place())
        barrier = pltpu.get_barrier_semaphore()
        for peer in peers:
            pl.semaphore_signal(barrier, inc=1, device_id=peer, device_id_type=MESH)
        pl.semaphore_wait(barrier, len(peers))
        body(*x_refs, *o_refs, send_sems, recv_sems, local_sems)

    launch()
    return [o[...] for o in o_refs]


def _sibling(mx, my, mc):
    return [(mx, my, 1 - mc)]


def _same_core_of_other_chips(mx, my, mc):
    return [(tx, ty, mc) for tx, ty in _other_chips(mx, my)]


def _run(copies):
    for cp in copies:
        cp.start()
    for cp in copies:
        cp.wait()


def _other_chips(mx, my):
    return [(mx ^ (j >> 1), my ^ (j & 1)) for j in (1, 2, 3)]


def chip_gather(xs, axes, name):
    n = len(xs)
    shapes, final = [], []
    for x, ax in zip(xs, axes):
        s = list(x.shape)
        if ax is None:
            shapes.append([N_CHIPS] + s)
            final.append(shapes[-1])
        elif ax < x.ndim - 1:
            shapes.append(s[:ax] + [N_CHIPS] + s[ax:])
            final.append(s[:ax] + [N_CHIPS * s[ax]] + s[ax + 1:])
        else:
            assert s[ax] % 128 == 0, (name, s)
            shapes.append(s[:ax] + [N_CHIPS * s[ax]])
            final.append(shapes[-1])

    def body(*refs):
        x_refs, o_refs = refs[:n], refs[n:2 * n]
        send_sems, recv_sems, local_sems = refs[2 * n:]
        mx, my, mc = _my_place()
        q = 2 * mx + my
        copies = []
        for i, (x_ref, o_ref, ax) in enumerate(zip(x_refs, o_refs, axes)):
            if ax is None or ax < len(x_ref.shape) - 1:
                dst = o_ref.at[(slice(None),) * (ax or 0) + (q,)]
            else:
                dst = _window(o_ref, ax, q * x_ref.shape[ax], x_ref.shape[ax])
            copies.append(pltpu.make_async_copy(x_ref, dst, local_sems.at[i]))
            for j, (tx, ty) in enumerate(_other_chips(mx, my)):
                copies.append(pltpu.make_async_remote_copy(
                    src_ref=x_ref, dst_ref=dst, send_sem=send_sems.at[3 * i + j], recv_sem=recv_sems.at[3 * i + j],
                    device_id=(tx, ty, mc), device_id_type=MESH))
        _run(copies)

    out_shapes = [jax.ShapeDtypeStruct(tuple(s), x.dtype) for s, x in zip(shapes, xs)]
    return [o.reshape(f) for o, f in zip(_comm_call(body, name, xs, out_shapes, 3 * n, n), final)]


def device_gather(x, name):
    H, C = x.shape

    def body(x_ref, o_ref, send_sems, recv_sems, local_sems):
        mx, my, mc = _my_place()
        dst = o_ref.at[2 * mx + my, mc]
        copies = [pltpu.make_async_copy(x_ref, dst, local_sems.at[0])]
        others = [(mx, my, 1 - mc)] + [(tx, ty, c) for tx, ty in _other_chips(mx, my) for c in (mc, 1 - mc)]
        for j, peer in enumerate(others):
            copies.append(pltpu.make_async_remote_copy(src_ref=x_ref, dst_ref=dst, send_sem=send_sems.at[j],
                                                       recv_sem=recv_sems.at[j], device_id=peer, device_id_type=MESH))
        _run(copies)

    out = _comm_call(body, name, [x], [jax.ShapeDtypeStruct((N_CHIPS, 2, H, C), x.dtype)], 7, 1)[0]
    return out.reshape(N_CHIPS * 2 * H, C)


def gather_two_level(xs, name):
    n = len(xs)
    shapes = [jax.ShapeDtypeStruct((2, N_CHIPS) + x.shape[1:], x.dtype) for x in xs]

    def body(*refs):
        x_refs, o_refs = refs[:n], refs[n:2 * n]
        send_sems, recv_sems, local_sems = refs[2 * n:]
        mx, my, mc = _my_place()
        q = 2 * mx + my
        first, local, second = [], [], []
        for i, (x_ref, o_ref) in enumerate(zip(x_refs, o_refs)):
            local.append(pltpu.make_async_copy(x_ref.at[mc], o_ref.at[mc, q], local_sems.at[i]))
            for j, (tx, ty) in enumerate(_other_chips(mx, my)):
                first.append(pltpu.make_async_remote_copy(
                    src_ref=x_ref.at[mc], dst_ref=o_ref.at[mc, q], send_sem=send_sems.at[4 * i + j],
                    recv_sem=recv_sems.at[4 * i + j], device_id=(tx, ty, mc), device_id_type=MESH))
            second.append(pltpu.make_async_remote_copy(
                src_ref=o_ref.at[mc], dst_ref=o_ref.at[mc], send_sem=send_sems.at[4 * i + 3],
                recv_sem=recv_sems.at[4 * i + 3], device_id=(mx, my, 1 - mc), device_id_type=MESH))
        for cp in local + first:
            cp.start()
        for cp in local:
            cp.wait()
        for cp in first:
            cp.wait_recv()
        _run(second)
        for cp in first:
            cp.wait_send()

    return _comm_call(body, name, xs, shapes, 4 * n, n)


def gather_two_level_sequencer(xs, name, collective_id):
    n = len(xs)
    hbm = pltpu.MemorySpace.HBM
    x_refs = [jax.new_ref(x, memory_space=hbm) for x in xs]
    o_refs = [jax.empty_ref(jax.ShapeDtypeStruct((2, N_CHIPS) + x.shape[1:], x.dtype), memory_space=hbm) for x in xs]

    @pl.kernel(mesh=plsc.ScalarSubcoreMesh(axis_name="sequencer", num_cores=1), name=name,
               scratch_types=(pltpu.SemaphoreType.DMA((4 * n,)), pltpu.SemaphoreType.DMA((4 * n,)),
                              pltpu.SemaphoreType.DMA((n,))),
               compiler_params=pltpu.CompilerParams(collective_id=collective_id))
    def launch(send_sems, recv_sems, local_sems):
        mx, my, mc = _my_place()
        peers = [(tx, ty, mc) for tx, ty in _other_chips(mx, my)] + [(mx, my, 1 - mc)]
        barrier = pltpu.get_barrier_semaphore()
        for peer in peers:
            pl.semaphore_signal(barrier, inc=1, device_id=peer, device_id_type=MESH)
        pl.semaphore_wait(barrier, len(peers))
        q = 2 * mx + my
        first, local, second = [], [], []
        for i, (x_ref, o_ref) in enumerate(zip(x_refs, o_refs)):
            local.append(pltpu.make_async_copy(x_ref.at[mc], o_ref.at[mc, q], local_sems.at[i]))
            for j, peer in enumerate(peers[:3]):
                first.append(pltpu.make_async_remote_copy(
                    src_ref=x_ref.at[mc], dst_ref=o_ref.at[mc, q], send_sem=send_sems.at[4 * i + j],
                    recv_sem=recv_sems.at[4 * i + j], device_id=peer, device_id_type=MESH))
            second.append(pltpu.make_async_remote_copy(
                src_ref=o_ref.at[mc], dst_ref=o_ref.at[mc], send_sem=send_sems.at[4 * i + 3],
                recv_sem=recv_sems.at[4 * i + 3], device_id=peers[3], device_id_type=MESH))
        for cp in local + first:
            cp.start()
        for cp in local:
            cp.wait()
        for cp in first:
            cp.wait_recv()
        _run(second)
        for cp in first:
            cp.wait_send()

    launch()
    return [o[...] for o in o_refs]


def pair_swap(xs, name, halves, collective_id=None):
    n = len(xs)
    shapes = [jax.ShapeDtypeStruct(x.shape[:1] + x.shape[2:] if halves else x.shape, x.dtype) for x in xs]

    def body(*refs):
        x_refs, o_refs = refs[:n], refs[n:2 * n]
        send_sems, recv_sems, _ = refs[2 * n:]
        mx, my, mc = _my_place()
        _run([pltpu.make_async_remote_copy(
            src_ref=x_ref.at[:, 1 - mc] if halves else x_ref, dst_ref=o_ref, send_sem=send_sems.at[i],
            recv_sem=recv_sems.at[i], device_id=(mx, my, 1 - mc), device_id_type=MESH)
            for i, (x_ref, o_ref) in enumerate(zip(x_refs, o_refs))])

    return _comm_call(body, name, xs, shapes, n, 0, None if collective_id is None else (_sibling, collective_id))


def chip_all_to_all(xs, name, collective_id=None):
    n = len(xs)
    shapes = [jax.ShapeDtypeStruct((N_CHIPS - 1,) + x.shape[1:], x.dtype) for x in xs]

    def body(*refs):
        x_refs, o_refs = refs[:n], refs[n:2 * n]
        send_sems, recv_sems, _ = refs[2 * n:]
        mx, my, mc = _my_place()
        copies = []
        for i, (x_ref, o_ref) in enumerate(zip(x_refs, o_refs)):
            for j, (tx, ty) in enumerate(_other_chips(mx, my)):
                copies.append(pltpu.make_async_remote_copy(
                    src_ref=x_ref.at[2 * tx + ty], dst_ref=o_ref.at[j], send_sem=send_sems.at[3 * i + j],
                    recv_sem=recv_sems.at[3 * i + j], device_id=(tx, ty, mc), device_id_type=MESH))
        _run(copies)

    return _comm_call(body, name, xs, shapes, 3 * n, 0,
                      None if collective_id is None else (_same_core_of_other_chips, collective_id))


WEIGHTS = ['norm_mix', 'norm_xa', 'norm_mem', 'norm_ffn', 'xa_wq', 'xa_wk', 'xa_wv', 'xa_wo', 'xa_q_norm', 'xa_k_norm',
           'ffn_w_up', 'ffn_conv_w', 'ffn_conv_b', 'ffn_w_down', 'hg_lb_logits', 'mix_w_in', 'hg_out_norm',
           'mla_q_a_norm', 'mla_w_uq', 'mla_kv_a_norm', 'mla_w_ukv', 'mla_qn_nope', 'mla_qn_rope', 'mla_kn_nope',
           'mla_kn_rope', 'mix_w_out', 's5_lam_re', 's5_lam_im', 's5_log_dt', 's5_b_re', 's5_b_im', 's5_c_re',
           's5_c_im', 's5_d', 's5_w_glu_a', 's5_w_glu_b']
INPUTS = ['x', 'mem', 'positions'] + WEIGHTS + ['loss_target'] + ['m_' + n for n in WEIGHTS] + ['v_' + n for n in WEIGHTS]
SHARD_AXIS = {'xa_wq': 1, 'xa_wk': 1, 'xa_wv': 1, 'xa_wo': 1, 'ffn_w_up': 2, 'ffn_conv_w': 2, 'ffn_w_down': 1,
              'mix_w_in': 2, 'mla_w_uq': 2, 'mla_w_ukv': 2, 'mix_w_out': 1, 's5_d': 1, 's5_w_glu_a': 1, 's5_w_glu_b': 1}
BIG = ['xa_wq', 'xa_wk', 'xa_wv', 'xa_wo', 'ffn_w_up', 'ffn_w_down', 'mix_w_in', 'mix_w_out', 's5_w_glu_a', 's5_w_glu_b']
FIRST_NEEDED = ('mix_w_in', 'mix_w_out')
SMALL_SHARDED = [n for n in WEIGHTS if n in SHARD_AXIS and n not in BIG]
REPLICATED = [n for n in WEIGHTS if n not in SHARD_AXIS]
SMALL = SMALL_SHARDED + REPLICATED
PACK_W = 1024
ROW_MULT = 16
W_IN_SHARD = IN_WIDTH // N_CHIPS
W_IN_SHARD_PAD = 640


def _pack(flats, mult=ROW_MULT):
    flat = jnp.concatenate([f.reshape(-1) for f in flats])
    unit = mult * PACK_W
    n = -(-flat.shape[0] // unit) * unit
    return jnp.pad(flat, (0, n - flat.shape[0])).reshape(n // PACK_W, PACK_W)


def _unpack(packed, shapes):
    flat, out, o = packed.reshape(-1), [], 0
    for s in shapes:
        n = math.prod(s)
        out.append(flat[o:o + n].reshape(s))
        o += n
    return out


def _rope_pad(w):
    z = jnp.zeros(w.shape[:-1] + (MLA_ROPE // 2,), w.dtype)
    return jnp.concatenate([w[..., :MLA_ROPE // 2], z, w[..., MLA_ROPE // 2:], z], axis=-1)


def _rope_unpad(g):
    return jnp.concatenate([g[..., :MLA_ROPE // 2], g[..., 64:64 + MLA_ROPE // 2]], axis=-1)


def _blockdiag_in(bb):
    nb = bb.shape[0] // S5_GB
    t = bb.reshape(nb, S5_GB, S5_STATE, S5_GROUP).transpose(0, 1, 3, 2)
    return jnp.einsum('bgmp,gh->bgmhp', t, jnp.eye(S5_GB, dtype=bb.dtype)).reshape(nb, S5_GB * S5_GROUP, S5_LANES)


def _blockdiag_in_t(dw):
    nb = dw.shape[0]
    t = jnp.einsum('bgmhp,gh->bgmp', dw.reshape(nb, S5_GB, S5_GROUP, S5_GB, S5_STATE), jnp.eye(S5_GB, dtype=dw.dtype))
    return t.transpose(0, 1, 3, 2).reshape(nb * S5_GB, S5_STATE, S5_GROUP)


def _blockdiag_out(c):
    nb = c.shape[0] // S5_GB
    t = c.reshape(nb, S5_GB, S5_GROUP, S5_STATE).transpose(0, 1, 3, 2)
    return jnp.einsum('bgpm,gh->bgphm', t, jnp.eye(S5_GB, dtype=c.dtype)).reshape(nb, S5_LANES, S5_GB * S5_GROUP)


def _blockdiag_out_t(dc):
    nb = dc.shape[0]
    t = jnp.einsum('bgphm,gh->bgpm', dc.reshape(nb, S5_GB, S5_STATE, S5_GB, S5_GROUP), jnp.eye(S5_GB, dtype=dc.dtype))
    return t.transpose(0, 1, 3, 2).reshape(nb * S5_GB, S5_GROUP, S5_STATE)


def _gather_weights(P):
    def halves(x):
        return x if x.shape[0] == 2 else x.reshape(2, x.shape[1] // 2, x.shape[2])

    now = [n for n in BIG if n in FIRST_NEEDED]
    later = [n for n in BIG if n not in FIRST_NEEDED]
    xs = [halves(P[n].astype(BF16)) for n in now] + [halves(_pack([P[n] for n in SMALL_SHARDED], 2 * ROW_MULT)[None])]
    got = gather_two_level(xs, "gather_weights")
    got, xs_later = lax.optimization_barrier((got, [halves(P[n].astype(BF16)) for n in later]))
    got_later = gather_two_level_sequencer(xs_later, "gather_weights_later", 1)
    full_w = {}
    for n, g in list(zip(now, got[:-1])) + list(zip(later, got_later)):
        two_layers, by_rows = P[n].shape[0] == 2, SHARD_AXIS[n] == 1
        if two_layers and by_rows:
            full_w[n] = g.reshape(2, N_CHIPS * g.shape[2], g.shape[3])
        elif two_layers:
            full_w[n] = g.transpose(0, 2, 1, 3).reshape(2, g.shape[2], N_CHIPS * g.shape[3])
        elif by_rows:
            full_w[n] = g.transpose(1, 0, 2, 3).reshape(1, 2 * N_CHIPS * g.shape[2], g.shape[3])
        else:
            full_w[n] = g.transpose(0, 2, 1, 3).reshape(1, 2 * g.shape[2], N_CHIPS * g.shape[3])
    small = got[-1].transpose(1, 0, 2, 3).reshape(N_CHIPS, -1, PACK_W)
    per_chip = [_unpack(small[q], [P[n].shape for n in SMALL_SHARDED]) for q in range(N_CHIPS)]
    for i, n in enumerate(SMALL_SHARDED):
        full_w[n] = jnp.concatenate([per_chip[q][i] for q in range(N_CHIPS)], axis=SHARD_AXIS[n])
    return full_w


def _reduce_batch(items, small, P, results, tag, ids):
    mx, my, mc = _my_place()
    q = 2 * mx + my
    ids = ids or {}
    names = [f"{n}_{lyr}" for n, lyr, _ in items] + (['small'] if small is not None else [])
    xs = [g.reshape(N_CHIPS, 2, g.shape[1] // 2, g.shape[2]) for g in [g for _, _, g in items] + ([small] if small is not None else [])]
    def after(vals, tie):
        return (vals, None) if tie is None else lax.optimization_barrier((vals, tie))

    theirs = pair_swap(xs, "grads_pair_swap_" + tag, True, ids.get('swap'))
    theirs, tie = after(theirs, (yield None))
    pair = [add_own_half(x, mc, t, "grads_pair_sum_" + n, F32 if n == 'small' else BF16) for x, t, n in zip(xs, theirs, names)]
    got = chip_all_to_all(pair, "grads_chip_all_to_all_" + tag, ids.get('a2a'))
    got, tie = after(got, (yield pair[-1] if tie is None else tie))
    summed = [add_chips(p, q, g, "grads_chip_sum_" + n) for p, g, n in zip(pair, got, names)]
    other = pair_swap(summed[:len(items)], "grads_pair_join_" + tag, False, ids.get('join'))
    if small is not None:
        results['small_sum'] = device_gather(summed[-1], "grads_small_gather")
    other, tie = after(other, (yield summed[-1] if tie is None else tie))
    for (n, lyr, _), s, o in zip(items, summed, other):
        if n == 'mix_w_in':
            s, o = s[:, :W_IN_SHARD], o[:, :W_IN_SHARD]
        view = (P[n].shape[0], 2 * s.shape[0], P[n].shape[-1])
        bufs = results.get(n) or [lax.empty(view, F32) for _ in range(4)]
        results[n] = adamw_layer(P[n].reshape(view), s, o, mc, P['m_' + n].reshape(view), P['v_' + n].reshape(view), lyr,
                                 bufs, f"adamw_{n}_{lyr}")
    yield tie


def _update_small(small_sum, GS, P):
    mx, my, _ = _my_place()
    q = 2 * mx + my
    g_small = dict(zip(SMALL, _unpack(small_sum, [GS[n].shape for n in SMALL])))
    for n in SMALL_SHARDED:
        s = P[n].shape[SHARD_AXIS[n]]
        g_small[n] = lax.dynamic_slice_in_dim(g_small[n], q * s, s, axis=SHARD_AXIS[n])
    grad, delta, new_m, new_v = {}, {}, {}, {}
    packed = lambda prefix: _pack([P[prefix + n] for n in SMALL])
    d, m_, v_ = adamw(packed(''), _pack([g_small[n] for n in SMALL]), packed('m_'), packed('v_'), "adamw_small")
    shapes = [P[n].shape for n in SMALL]
    grad.update(g_small)
    for out, pk in ((delta, d), (new_m, m_), (new_v, v_)):
        out.update(zip(SMALL, _unpack(pk, shapes)))
    return grad, delta, new_m, new_v


def _row(v):
    return v.reshape(1, -1)


def _xattn_fwd(h, mem, W, lyr, tm):
    g_xa, g_mem = _row(W['norm_xa'][lyr]), _row(W['norm_mem'][lyr])
    g_q, g_k = _row(W['xa_q_norm'][lyr]), _row(W['xa_k_norm'][lyr])
    wq, wk, wv, wo = (W[n][lyr] for n in ('xa_wq', 'xa_wk', 'xa_wv', 'xa_wo'))
    L, D = h.shape
    M = mem.shape[0]
    hx = rms_fwd(h, g_xa, tm, MXU_DTYPE)
    qp = matmul(hx, wq, name="xa_q")
    kv_opds = [full(mem), full(g_mem), full(wk), full(wv), full(g_k)]
    k, v = blocked_fwd(_mem_kv, kv_opds, [((M, D), F32, (M, D), lambda i: (0, 0))] * 2, 1, "xa_mem_kv")
    o = blocked_fwd(_xa_core, [rows(qp, tm), full(k), full(v), full(g_q)], [_out((L, D), MXU_DTYPE, tm)], L // tm,
                    "xa_core")[0]
    out = matmul(o, wo, add=h, name="xa_o")
    return out, (h, hx, qp, k, v, o)


def _xattn_bwd(dout, saved, mem, W, lyr, tm):
    h, hx, qp, k, v, o = saved
    g_xa, g_mem = _row(W['norm_xa'][lyr]), _row(W['norm_mem'][lyr])
    g_q, g_k = _row(W['xa_q_norm'][lyr]), _row(W['xa_k_norm'][lyr])
    wq, wk, wv, wo = (W[n][lyr] for n in ('xa_wq', 'xa_wk', 'xa_wv', 'xa_wo'))
    L = h.shape[0]
    do = matmul(dout, wo, "nt", name="xa_do")
    d_wo = matmul(o, dout, "tn", name="xa_dwo")
    dqp, dk, dv, d_gq = blocked_bwd(_xa_core, [rows(qp, tm, 'blk'), full(k, 'acc'), full(v, 'acc'), full(g_q, 'acc')],
                                    [rows(do, tm)], L // tm, "xa_core_bwd")
    d_wq = matmul(hx, dqp, "tn", name="xa_dwq")
    dhx = matmul(dqp, wq, "nt", name="xa_dhx")
    dh, d_gxa = rms_bwd(h, g_xa, dhx, tm, dout)
    d_gmem, d_wk, d_wv, d_gk = blocked_bwd(
        _mem_kv, [full(mem), full(g_mem, 'acc'), full(wk, 'acc'), full(wv, 'acc'), full(g_k, 'acc')],
        [full(dk), full(dv)], 1, "xa_mem_kv_bwd")
    by_chip = lambda g: g.reshape(N_CHIPS, g.shape[0] // N_CHIPS, g.shape[1])
    grads = {'norm_xa': d_gxa, 'norm_mem': d_gmem, 'xa_q_norm': d_gq, 'xa_k_norm': d_gk,
             'xa_wq': by_chip(d_wq), 'xa_wk': by_chip(d_wk), 'xa_wv': by_chip(d_wv), 'xa_wo': by_chip(d_wo)}
    return dh, grads


def _conv_params(W, lyr):
    cw, cb = W['ffn_conv_w'][lyr], W['ffn_conv_b'][lyr]
    F = cw.shape[1] // 2
    return [cw[0:1, :F], cw[1:2, :F], cw[2:3, :F], cw[0:1, F:], cw[1:2, F:], cw[2:3, F:], _row(cb[:F]), _row(cb[F:])]


def _ffn_fwd(h, W, lyr, tm):
    L, D = h.shape
    w_up, w_down = W['ffn_w_up'][lyr], W['ffn_w_down'][lyr]
    F = w_down.shape[0]
    hf = rms_fwd(h, _row(W['norm_ffn'][lyr]), tm, MXU_DTYPE)
    ug = matmul(hf, w_up[:, :F], name="ffn_up_gate")
    uv = matmul(hf, w_up[:, F:], name="ffn_up_value")
    opds = [cols(ug, 128), cols(uv, 128)] + [cols(p, 128) for p in _conv_params(W, lyr)]
    a = blocked_fwd(_conv_gate, opds, [((L, F), MXU_DTYPE, (L, 128), lambda j: (0, j))], F // 128, "ffn_conv_gate")[0]
    out = matmul(a, w_down, add=h, name="ffn_down")
    return out, (h, hf, ug, uv, a)


def _ffn_bwd(dout, saved, W, lyr, tm):
    h, hf, ug, uv, a = saved
    w_up, w_down = W['ffn_w_up'][lyr], W['ffn_w_down'][lyr]
    F = w_down.shape[0]
    da = matmul(dout, w_down, "nt", name="ffn_da")
    d_wdown = matmul(a, dout, "tn", name="ffn_dwdown")
    opds = [cols(ug, 128, 'blk'), cols(uv, 128, 'blk')] + [cols(p, 128, 'blk') for p in _conv_params(W, lyr)]
    gs = blocked_bwd(_conv_gate, opds, [cols(da, 128)], F // 128, "ffn_conv_gate_bwd")
    dug, duv = gs[0], gs[1]
    d_cw = jnp.concatenate([jnp.concatenate(gs[2:5], axis=0), jnp.concatenate(gs[5:8], axis=0)], axis=1)
    d_cb = jnp.concatenate([gs[8], gs[9]], axis=1)[0]
    half = N_CHIPS // 2
    d_wup = lax.empty((N_CHIPS, hf.shape[1], w_up.shape[1] // N_CHIPS), F32)
    d_wup = matmul(hf, dug, "tn", name="ffn_dwup_gate", into=(d_wup, 0), col_blocks=half)
    d_wup = matmul(hf, duv, "tn", name="ffn_dwup_value", into=(d_wup, half), col_blocks=half)
    dhf = matmul(dug, w_up[:, :F], "nt", name="ffn_dhf_gate")
    dhf = matmul(duv, w_up[:, F:], "nt", add=dhf, name="ffn_dhf_value")
    dh, d_g = rms_bwd(h, _row(W['norm_ffn'][lyr]), dhf, tm, dout)
    d_wdown = d_wdown.reshape(N_CHIPS, F // N_CHIPS, d_wdown.shape[1])
    return dh, {'norm_ffn': d_g, 'ffn_w_up': d_wup, 'ffn_conv_w': d_cw, 'ffn_conv_b': d_cb, 'ffn_w_down': d_wdown}


def _mla_params(W):
    w_uq = W['mla_w_uq'][0].reshape(MLA_Q_RANK, MLA_HEADS, MLA_QK)
    w_uq = jnp.concatenate([w_uq[..., :MLA_NOPE], _rope_pad(w_uq[..., MLA_NOPE:])], axis=-1)
    w_ukv = W['mla_w_ukv'][0].reshape(MLA_KV_RANK, MLA_HEADS, MLA_NOPE + MLA_V)
    w_ukv = jnp.concatenate([w_ukv[..., :MLA_NOPE].reshape(MLA_KV_RANK, -1), w_ukv[..., MLA_NOPE:].reshape(MLA_KV_RANK, -1)],
                            axis=1)
    return [_row(W['mla_q_a_norm'][0]), w_uq.reshape(MLA_Q_RANK, MLA_HEADS * MLA_DK), _row(W['mla_kv_a_norm'][0]), w_ukv,
            _row(W['mla_qn_nope'][0]), _row(_rope_pad(W['mla_qn_rope'][0])), _row(W['mla_kn_nope'][0]),
            _row(_rope_pad(W['mla_kn_rope'][0]))]


def _w_in_padded(W):
    w = W['mix_w_in'][0]
    return jnp.concatenate([w[:, :IN_WIDTH - MLA_ROPE], _rope_pad(w[:, IN_WIDTH - MLA_ROPE:])], axis=1)


def _mixer0_fwd(h, W, cos_p, sin_p, tm):
    L = h.shape[0]
    t = min(ATTN_ROWS, L // ATTN_WIDE)
    hn = rms_fwd(h, _row(W['norm_mix'][0]), tm, MXU_DTYPE)
    proj = matmul(hn, _w_in_padded(W), name="mix_in")
    logits = W['hg_lb_logits']
    lb = blocked_fwd(_lb_first, [full(logits)], [((1, HG_WIDTH), F32, (1, HG_WIDTH), lambda i: (0, 0))], 1, "hg_lb")[0]
    gain = _row(W['hg_out_norm'][0])
    o_hg, states = hgrn2_fwd(proj, lb, gain)
    mp = _mla_params(W)
    q, k, v = mla_prep_fwd(proj, cos_p, sin_p, mp, tm)
    scale = MLA_QK ** -0.5
    o_mla, lse = attn_fwd(q, k, v, scale, t)
    w_out = W['mix_w_out'][0]
    out = matmul(o_hg, w_out[:HG_WIDTH], add=h, name="mix_out_hg")
    out = matmul(o_mla, w_out[HG_WIDTH:], add=out, name="mix_out_mla")
    return out, (h, hn, proj, lb, o_hg, states, q, k, v, o_mla, lse)


def _mixer0_bwd(dout, saved, W, cos_p, sin_p, tm, part_way=None):
    h, hn, proj, lb, o_hg, states, q, k, v, o_mla, lse = saved
    L = h.shape[0]
    t = min(ATTN_ROWS, L // ATTN_WIDE)
    scale = MLA_QK ** -0.5
    w_out = W['mix_w_out'][0]
    gain = _row(W['hg_out_norm'][0])
    do_hg = matmul(dout, w_out[:HG_WIDTH], "nt", name="mix_do_hg")
    do_mla = matmul(dout, w_out[HG_WIDTH:], "nt", name="mix_do_mla")
    d_wout = jnp.concatenate([matmul(o_hg, dout, "tn", name="mix_dwout_hg"), matmul(o_mla, dout, "tn", name="mix_dwout_mla")],
                             axis=0)
    dq = attn_bwd_dq(q, k, v, o_mla, lse, do_mla, scale, t)
    if part_way is not None:
        dq = part_way(dq)
    dk, dv = attn_bwd_dkv(q, k, v, o_mla, lse, do_mla, scale, t)
    mp = _mla_params(W)
    d_mla, d_qa, d_wuq, d_kva, d_wukv, d_qnn, d_qnr, d_knn, d_knr = mla_prep_bwd(proj, cos_p, sin_p, mp, dq, dk, dv, tm)
    d_hg, d_lb, d_gain = hgrn2_bwd(proj, lb, gain, states, do_hg)
    w_in, n_hg = _w_in_padded(W), 4 * HG_WIDTH
    d_win = jnp.concatenate([matmul(hn, d_hg, "tn", name="mix_dwin_hg"), matmul(hn, d_mla, "tn", name="mix_dwin_mla")], axis=1)
    dhn = matmul(d_hg, w_in[:, :n_hg], "nt", name="mix_dhn_hg")
    dhn = matmul(d_mla, w_in[:, n_hg:], "nt", add=dhn, name="mix_dhn_mla")
    dh, d_g = rms_bwd(h, _row(W['norm_mix'][0]), dhn, tm, dout)
    logits = W['hg_lb_logits']
    d_logits = blocked_bwd(_lb_first, [full(logits, 'acc')], [full(d_lb)], 1, "hg_lb_bwd")[0]
    d_wuq = d_wuq.reshape(MLA_Q_RANK, MLA_HEADS, MLA_DK)
    d_wuq = jnp.concatenate([d_wuq[..., :MLA_NOPE], _rope_unpad(d_wuq[..., MLA_NOPE:])], axis=-1)
    hw = MLA_HEADS * MLA_NOPE
    d_wukv = jnp.concatenate([d_wukv[:, :hw].reshape(MLA_KV_RANK, MLA_HEADS, MLA_NOPE),
                              d_wukv[:, hw:].reshape(MLA_KV_RANK, MLA_HEADS, MLA_V)], axis=-1)
    d_win = jnp.concatenate([d_win[:, :IN_WIDTH - MLA_ROPE], _rope_unpad(d_win[:, IN_WIDTH - MLA_ROPE:])], axis=1)
    d_win = d_win.reshape(d_win.shape[0], N_CHIPS, W_IN_SHARD).transpose(1, 0, 2)
    d_win = jnp.pad(d_win, ((0, 0), (0, 0), (0, W_IN_SHARD_PAD - W_IN_SHARD)))
    d_wout = d_wout.reshape(N_CHIPS, d_wout.shape[0] // N_CHIPS, d_wout.shape[1])
    grads = {'norm_mix': d_g, 'hg_lb_logits': d_logits, 'mix_w_in': d_win, 'hg_out_norm': d_gain,
             'mla_q_a_norm': d_qa, 'mla_w_uq': d_wuq.reshape(1, MLA_Q_RANK, -1), 'mla_kv_a_norm': d_kva,
             'mla_w_ukv': d_wukv.reshape(1, MLA_KV_RANK, -1), 'mla_qn_nope': d_qnn, 'mla_qn_rope': _rope_unpad(d_qnr),
             'mla_kn_nope': d_knn, 'mla_kn_rope': _rope_unpad(d_knr), 'mix_w_out': d_wout}
    return dh,grads


def _s5_inputs(W):
    G = W['s5_lam_re'].shape[1]
    return [W['s5_lam_re'][0], W['s5_lam_im'][0], W['s5_log_dt'][0].reshape(G, 1),
            W['s5_b_re'][0].reshape(G, -1), W['s5_b_im'][0].reshape(G, -1)]


def _mixer1_fwd(h, W, tm):
    L, D = h.shape
    u = rms_fwd(h, _row(W['norm_mix'][1]), tm, F32)
    di = _s5_inputs(W)
    G = di[0].shape[0]
    sq, wide = ((G, S5_STATE), F32, (G, S5_STATE), lambda i: (0, 0)), ((G, S5_STATE * S5_GROUP), F32, (G, S5_STATE * S5_GROUP), lambda i: (0, 0))
    ar, ai, bbr, bbi = blocked_fwd(_s5_discretize, [full(a) for a in di], [sq, sq, wide, wide], 1, "s5_discretize")
    nb = G // S5_GB
    core = (_blockdiag_in(bbr.reshape(G, S5_STATE, S5_GROUP)), _blockdiag_in(bbi.reshape(G, S5_STATE, S5_GROUP)),
            ar.reshape(nb, 1, S5_LANES), ai.reshape(nb, 1, S5_LANES),
            _blockdiag_out(W['s5_c_re'][0]), _blockdiag_out(W['s5_c_im'][0]))
    y = s5_fwd(u, *core)
    d = W['s5_d']
    y2 = blocked_fwd(_s5_post, [rows(y, tm), rows(u, tm), full(d)], [_out((L, D), MXU_DTYPE, tm)], L // tm, "s5_post")[0]
    w_ab = jnp.concatenate([W['s5_w_glu_a'][0], W['s5_w_glu_b'][0]], axis=1)
    ab = matmul(y2, w_ab, name="s5_glu_in")
    mix = blocked_fwd(_glu, [rows(ab, tm, col=0, width=D), rows(ab, tm, col=1, width=D)], [_out((L, D), F32, tm)], L // tm,
                      "s5_glu")[0]
    return h + mix, (h, u, core, y, y2, ab)


def _mixer1_bwd(dout, saved, W, tm):
    h, u, core, y, y2, ab = saved
    L, D = h.shape
    da, db = blocked_bwd(_glu, [rows(ab, tm, 'blk', col=0, width=D), rows(ab, tm, 'blk', col=1, width=D)], [rows(dout, tm)],
                         L // tm, "s5_glu_bwd")
    w_a, w_b = W['s5_w_glu_a'][0], W['s5_w_glu_b'][0]
    dy2 = matmul(da, w_a, "nt", name="s5_dy2_a")
    dy2 = matmul(db, w_b, "nt", add=dy2, name="s5_dy2_b")
    d_wa = matmul(y2, da, "tn", name="s5_dwa")
    d_wb = matmul(y2, db, "tn", name="s5_dwb")
    d = W['s5_d']
    dy, du_skip, d_d = blocked_bwd(_s5_post, [rows(y, tm, 'blk'), rows(u, tm, 'blk'), full(d, 'acc')], [rows(dy2, tm)], L // tm,
                                   "s5_post_bwd")
    du, dwr, dwi, dar, dai, dcr, dci = s5_bwd(u, *core, dy, min(S5_BWD_CHUNK, L))
    di = _s5_inputs(W)
    G = di[0].shape[0]
    cts = [dar.reshape(G, S5_STATE), dai.reshape(G, S5_STATE), _blockdiag_in_t(dwr).reshape(G, -1), _blockdiag_in_t(dwi).reshape(G, -1)]
    d_lr, d_li, d_ldt, d_br, d_bi = blocked_bwd(_s5_discretize, [full(a, 'acc') for a in di], [full(c) for c in cts], 1,
                                                "s5_discretize_bwd")
    dh, d_g = rms_bwd(h, _row(W['norm_mix'][1]), du + du_skip, tm, dout)
    bshape = W['s5_b_re'].shape
    grads = {'norm_mix': d_g, 's5_lam_re': d_lr[None], 's5_lam_im': d_li[None], 's5_log_dt': d_ldt.reshape(1, G),
             's5_b_re': d_br.reshape(bshape), 's5_b_im': d_bi.reshape(bshape), 's5_c_re': _blockdiag_out_t(dcr)[None],
             's5_c_im': _blockdiag_out_t(dci)[None], 's5_d': d_d, 's5_w_glu_a': d_wa.reshape(N_CHIPS, -1, D), 's5_w_glu_b': d_wb.reshape(N_CHIPS, -1, D)}
    return dh,grads


def kernel(x, mem, positions, norm_mix, norm_xa, norm_mem, norm_ffn, xa_wq, xa_wk, xa_wv, xa_wo, xa_q_norm, xa_k_norm, ffn_w_up, ffn_conv_w, ffn_conv_b, ffn_w_down, hg_lb_logits, mix_w_in, hg_out_norm, mla_q_a_norm, mla_w_uq, mla_kv_a_norm, mla_w_ukv, mla_qn_nope, mla_qn_rope, mla_kn_nope, mla_kn_rope, mix_w_out, s5_lam_re, s5_lam_im, s5_log_dt, s5_b_re, s5_b_im, s5_c_re, s5_c_im, s5_d, s5_w_glu_a, s5_w_glu_b, loss_target, m_norm_mix, m_norm_xa, m_norm_mem, m_norm_ffn, m_xa_wq, m_xa_wk, m_xa_wv, m_xa_wo, m_xa_q_norm, m_xa_k_norm, m_ffn_w_up, m_ffn_conv_w, m_ffn_conv_b, m_ffn_w_down, m_hg_lb_logits, m_mix_w_in, m_hg_out_norm, m_mla_q_a_norm, m_mla_w_uq, m_mla_kv_a_norm, m_mla_w_ukv, m_mla_qn_nope, m_mla_qn_rope, m_mla_kn_nope, m_mla_kn_rope, m_mix_w_out, m_s5_lam_re, m_s5_lam_im, m_s5_log_dt, m_s5_b_re, m_s5_b_im, m_s5_c_re, m_s5_c_im, m_s5_d, m_s5_w_glu_a, m_s5_w_glu_b, v_norm_mix, v_norm_xa, v_norm_mem, v_norm_ffn, v_xa_wq, v_xa_wk, v_xa_wv, v_xa_wo, v_xa_q_norm, v_xa_k_norm, v_ffn_w_up, v_ffn_conv_w, v_ffn_conv_b, v_ffn_w_down, v_hg_lb_logits, v_mix_w_in, v_hg_out_norm, v_mla_q_a_norm, v_mla_w_uq, v_mla_kv_a_norm, v_mla_w_ukv, v_mla_qn_nope, v_mla_qn_rope, v_mla_kn_nope, v_mla_kn_rope, v_mix_w_out, v_s5_lam_re, v_s5_lam_im, v_s5_log_dt, v_s5_b_re, v_s5_b_im, v_s5_c_re, v_s5_c_im, v_s5_d, v_s5_w_glu_a, v_s5_w_glu_b):
    P = dict(locals())
    assert sorted(P) == sorted(INPUTS) and norm_mix.shape[0] == 2 and mix_w_in.shape[0] == 1
    x, mem, target = P['x'][0], P['mem'][0], P['loss_target'][0]
    L, D = x.shape
    tm = min(256, L)

    W = {n: P[n] for n in REPLICATED}
    W.update(_gather_weights(P))

    inv_freq = 1.0 / (ROPE_BASE ** (jnp.arange(0, MLA_ROPE, 2, dtype=F32) / MLA_ROPE))
    ang = P['positions'][0].astype(F32)[:, None] * inv_freq
    cos, sin, z = jnp.cos(ang), jnp.sin(ang), jnp.zeros_like(ang)
    cos_p = jnp.concatenate([cos, z, cos, z], axis=1)
    sin_p = jnp.concatenate([-sin, z, sin, z], axis=1)

    h, s_mix0 = _mixer0_fwd(x, W, cos_p, sin_p, tm)
    h, s_xa0 = _xattn_fwd(h, mem, W, 0, tm)
    h, s_ffn0 = _ffn_fwd(h, W, 0, tm)
    h, s_mix1 = _mixer1_fwd(h, W, tm)
    h, s_xa1 = _xattn_fwd(h, mem, W, 1, tm)
    h, s_ffn1 = _ffn_fwd(h, W, 1, tm)
    n = L // tm
    dh, parts = blocked_fwd(_loss_fn, [rows(h, tm), rows(target, tm)],
                            [_out((L, D), F32, tm), ((n * 8, 128), F32, (8, 128), lambda i: (i, 0))], n, "loss")
    loss = lax.psum(jnp.sum(parts), ("x", "y", "c"))

    layered = {}

    def collect(g, lyr):
        for k_, v_ in g.items():
            layered.setdefault(k_, {})[lyr] = v_

    results = {}

    def big_items(lyr, names):
        return [(n_, 0 if P[n_].shape[0] == 1 else lyr, layered[n_][lyr]) for n_ in names]

    per_layer = [n_ for n_ in BIG if P[n_].shape[0] == 2]
    second_mixer = ['s5_w_glu_a', 's5_w_glu_b']
    first_mixer = ['mix_w_in', 'mix_w_out']
    assert sorted(per_layer + second_mixer + first_mixer) == sorted(BIG)

    dh, g = _ffn_bwd(dh, s_ffn1, W, 1, tm)
    collect(g, 1)
    dh, g = _xattn_bwd(dh, s_xa1, mem, W, 1, tm)
    collect(g, 1)
    dh, g = _mixer1_bwd(dh, s_mix1, W, tm)
    collect(g, 1)
    late = _reduce_batch(big_items(1, per_layer + second_mixer), None, P, results, "late_layer", {'swap': 2, 'a2a': 3, 'join': 4})
    next(late)
    dh, g = _ffn_bwd(dh, s_ffn0, W, 0, tm)
    collect(g, 0)
    dh = late.send(dh)
    dh, g = _xattn_bwd(dh, s_xa0, mem, W, 0, tm)
    collect(g, 0)
    mid = _reduce_batch(big_items(0, per_layer), None, P, results, "first_layer", {'swap': 5, 'a2a': 6, 'join': 7})
    next(mid)
    dx, g = _mixer0_bwd(dh, s_mix0, W, cos_p, sin_p, tm, part_way=mid.send)
    collect(g, 0)

    GS = {}
    for name in SMALL:
        by_layer = [layered[name][lyr] for lyr in sorted(layered[name])]
        full_shape = W[name].shape
        GS[name] = (by_layer[0].reshape(full_shape) if len(by_layer) == 1
                    else jnp.stack([g_.reshape(full_shape[1:]) for g_ in by_layer]))
    small = _pack([GS[n_] for n_ in SMALL], 2 * N_CHIPS * ROW_MULT).reshape(N_CHIPS, -1, PACK_W)
    dx = late.send(dx)
    dx = mid.send(dx)
    last = _reduce_batch(big_items(0, first_mixer), small, P, results, "first_mixer", {'swap': 8, 'a2a': 9, 'join': 10})
    next(last)
    late.send(None)
    last_pair_sum = last.send(None)
    mid.send(last_pair_sum)
    mid_name = per_layer[-1]
    results[mid_name] = list(last.send(list(results[mid_name])))
    last.send(None)
    outs = list(_update_small(results['small_sum'], GS, P))
    for k_ in range(4):
        outs[k_].update({n_: results[n_][k_].reshape(P[n_].shape) for n_ in BIG})
    return (loss, dx[None], *[d[n_] for d in outs for n_ in WEIGHTS])
```

```python
import functools
import math

import jax
import jax.numpy as jnp
import numpy as np
from jax import lax
from jax.experimental import pallas as pl
from jax.experimental.pallas import tpu as pltpu
from jax.experimental.pallas import tpu_sc as plsc

F32 = jnp.float32
BF16 = jnp.bfloat16
MXU_DTYPE = BF16
HI = lax.Precision.HIGHEST
V7X_VMEM_LIMIT_BYTES = 56 * 1024 * 1024
EPS = 1e-6
MESH = pl.DeviceIdType.MESH

HG_HEADS, HG_DIM = 4, 128
HG_WIDTH = HG_HEADS * HG_DIM
HG_SUB = 32
HG_BLOCK = 64
MLA_HEADS, MLA_Q_RANK, MLA_KV_RANK = 4, 256, 128
MLA_NOPE, MLA_ROPE, MLA_V = 128, 64, 128
MLA_QK = MLA_NOPE + MLA_ROPE
MLA_DK = 256
ROPE_BASE = 10000.0
IN_WIDTH = 4 * HG_WIDTH + MLA_Q_RANK + MLA_KV_RANK + MLA_ROPE
IN_PAD = 4 * HG_WIDTH + MLA_Q_RANK + MLA_KV_RANK + 128
S5_GROUP, S5_STATE = 16, 64
S5_GB = 8
DT_MIN, DT_MAX = 1e-3, 1e-1
XA_HEADS = 4
CONV_W = 3
ADAM_LR, ADAM_B1, ADAM_B2, ADAM_EPS, ADAM_WD, ADAM_STEP = 0.001, 0.9, 0.999, 1e-08, 0.01, 10


def _cparams(sem):
    return pltpu.CompilerParams(dimension_semantics=sem, vmem_limit_bytes=V7X_VMEM_LIMIT_BYTES)


class Opd:
    def __init__(self, arr, block, imap, grad=None, gshape=None, gimap=None):
        self.arr, self.block, self.imap, self.grad = arr, block, imap, grad
        self.gshape = arr.shape if gshape is None else gshape
        self.gimap = imap if gimap is None else gimap

    def spec(self):
        return pl.BlockSpec(self.block, self.imap)

    def gspec(self):
        return pl.BlockSpec(self.block, self.gimap)


def rows(arr, tm, grad=None, col=0, width=None):
    width = arr.shape[1] if width is None else width
    return Opd(arr, (tm, width), lambda i, c=col: (i, c), grad, (arr.shape[0], width), lambda i: (i, 0))


def cols(arr, tn, grad=None):
    return Opd(arr, (arr.shape[0], tn), lambda j: (0, j), grad)


def full(arr, grad=None):
    return Opd(arr, arr.shape, lambda i: (0, 0), grad)


def _load(ref):
    v = ref[...]
    return v.astype(F32) if jnp.issubdtype(v.dtype, jnp.floating) else v


def blocked_fwd(f, opds, outs, n, name):
    n_in = len(opds)

    def body(*refs):
        ys = f(*[_load(r) for r in refs[:n_in]])
        for r, y in zip(refs[n_in:], ys):
            r[...] = y.astype(r.dtype)

    res = pl.pallas_call(
        body, name=name, grid=(n,),
        in_specs=[o.spec() for o in opds],
        out_specs=[pl.BlockSpec(b, m) for (_, _, b, m) in outs],
        out_shape=[jax.ShapeDtypeStruct(s, d) for (s, d, _, _) in outs],
        compiler_params=_cparams(("parallel",)),
    )(*[o.arr for o in opds])
    return res


def blocked_bwd(f, opds, dys, n, name, plus=None):
    n_in, n_dy = len(opds), len(dys)
    diff = [i for i, o in enumerate(opds) if o.grad]
    extra = [] if plus is None else [plus]

    def body(*refs):
        vals = [_load(r) for r in refs[:n_in]]

        def fd(*dv):
            allv = list(vals)
            for i, v in zip(diff, dv):
                allv[i] = v
            return tuple(f(*allv))

        ys, vjp = jax.vjp(fd, *[vals[i] for i in diff])
        cts = tuple(_load(r).astype(y.dtype) for r, y in zip(refs[n_in:n_in + n_dy], ys))
        gs = list(vjp(cts))
        if extra:
            gs[0] = gs[0] + _load(refs[n_in + n_dy])
        for r, g, i in zip(refs[n_in + n_dy + len(extra):], gs, diff):
            if opds[i].grad == 'acc':
                @pl.when(pl.program_id(0) == 0)
                def _(r=r):
                    r[...] = jnp.zeros(r.shape, r.dtype)
                r[...] += g.astype(r.dtype)
            else:
                r[...] = g.astype(r.dtype)

    any_acc = any(opds[i].grad == 'acc' for i in diff)
    res = pl.pallas_call(
        body, name=name, grid=(n,),
        in_specs=[o.spec() for o in opds + dys + extra],
        out_specs=[opds[i].gspec() for i in diff],
        out_shape=[jax.ShapeDtypeStruct(opds[i].gshape, F32) for i in diff],
        compiler_params=_cparams(("arbitrary" if any_acc else "parallel",)),
    )(*[o.arr for o in opds + dys + extra])
    return res


def _tile(dim, want):
    for t in range(want - want % 16, 0, -16):
        if dim % t == 0:
            return t
    assert dim <= want, (dim, want)
    return dim


MATMUL_VMEM_BUDGET = 40 * 1024 * 1024
MATMUL_ROWS = 512


def _widest(N, fits):
    for t in range(N - N % 128, 0, -128):
        if N % t == 0 and fits(t):
            return t
    return N


def matmul(a, b, mode="nn", out_dtype=F32, add=None, name="matmul", into=None, col_blocks=None):
    sa, sb, so = a.dtype.itemsize, b.dtype.itemsize, jnp.dtype(out_dtype).itemsize
    has_add = add is not None
    if mode == "tn":
        (K, M), (K2, N) = a.shape, b.shape
        assert K == K2 and not has_add and out_dtype == F32, (a.shape, b.shape)
        tk = _tile(K, MATMUL_ROWS)
        tn = _widest(N, lambda t: 2 * (tk * M * sa + tk * t * sb + M * t * 4) <= MATMUL_VMEM_BUDGET)
        extra, alias = [], {}
        if into is not None:
            buf, lead = into[0], tuple(into[1:])
            if col_blocks is not None:
                assert N % col_blocks == 0 and (N // col_blocks) % 128 == 0 and tn >= N // col_blocks, (N, col_blocks, tn)
                tn = N // col_blocks
                assert buf.shape[len(lead):] == (M, tn), (buf.shape, lead, M, tn)
                out_spec = pl.BlockSpec((None,) * len(lead) + (M, tn), lambda j, k: lead[:-1] + (lead[-1] + j, 0, 0))
            else:
                assert buf.shape[len(lead):] == (M, N), (buf.shape, lead, M, N)
                out_spec = pl.BlockSpec((None,) * len(lead) + (M, tn), lambda j, k: lead + (0, j))
            out_shape = jax.ShapeDtypeStruct(buf.shape, F32)
            extra, alias = [buf], {2: 0}
        else:
            assert col_blocks is None
            out_spec = pl.BlockSpec((M, tn), lambda j, k: (0, j))
            out_shape = jax.ShapeDtypeStruct((M, N), F32)

        def body(a_ref, b_ref, *rest):
            o_ref = rest[-1]
            r = lax.dot_general(a_ref[...].astype(MXU_DTYPE), b_ref[...].astype(MXU_DTYPE), ((_TN), ((), ())),
                                preferred_element_type=F32)

            @pl.when(pl.program_id(1) == 0)
            def _():
                o_ref[...] = r

            @pl.when(pl.program_id(1) > 0)
            def _():
                o_ref[...] += r

        return pl.pallas_call(
            body, name=name, grid=(N // tn, K // tk),
            in_specs=[pl.BlockSpec((tk, M), lambda j, k: (k, 0)), pl.BlockSpec((tk, tn), lambda j, k: (k, j))]
            + [pl.BlockSpec(memory_space=pl.ANY)] * len(extra),
            out_specs=out_spec, out_shape=out_shape, input_output_aliases=alias,
            compiler_params=_cparams(("parallel", "arbitrary")),
        )(a, b, *extra)

    (M, K) = a.shape
    N = b.shape[1] if mode == "nn" else b.shape[0]
    assert K == (b.shape[0] if mode == "nn" else b.shape[1]), (a.shape, b.shape, mode)
    tm = _tile(M, MATMUL_ROWS)
    tn = _widest(N, lambda t: 2 * (tm * K * sa + K * t * sb + tm * t * (so + 4 * has_add)) <= MATMUL_VMEM_BUDGET)
    dims = ((_NN if mode == "nn" else _NT), ((), ()))

    def body(*refs):
        r = lax.dot_general(refs[0][...].astype(MXU_DTYPE), refs[1][...].astype(MXU_DTYPE), dims, preferred_element_type=F32)
        if has_add:
            r = r + refs[2][...].astype(F32)
        refs[-1][...] = r.astype(refs[-1].dtype)

    b_spec = pl.BlockSpec((K, tn), lambda j, i: (0, j)) if mode == "nn" else pl.BlockSpec((tn, K), lambda j, i: (j, 0))
    in_specs = [pl.BlockSpec((tm, K), lambda j, i: (i, 0)), b_spec]
    args = [a, b]
    if has_add:
        in_specs.append(pl.BlockSpec((tm, tn), lambda j, i: (i, j)))
        args.append(add)
    return pl.pallas_call(
        body, name=name, grid=(N // tn, M // tm),
        in_specs=in_specs,
        out_specs=pl.BlockSpec((tm, tn), lambda j, i: (i, j)),
        out_shape=jax.ShapeDtypeStruct((M, N), out_dtype),
        compiler_params=_cparams(("parallel", "parallel")),
    )(*args)


def _dot(a, b, dims, precision=None):
    if precision is None:
        a, b = a.astype(MXU_DTYPE), b.astype(MXU_DTYPE)
    return lax.dot_general(a, b, (dims, ((), ())), precision=precision, preferred_element_type=F32)


_NN = ((1,), (0,))
_NT = ((1,), (1,))
_TN = ((0,), (0,))


def _rms(x, gain):
    return x * lax.rsqrt(jnp.mean(x * x, axis=-1, keepdims=True) + EPS) * gain


def _hg_block(st_t, q, fl, iv, g, lb, gain):
    row = lax.broadcasted_iota(jnp.int32, (HG_SUB, HG_SUB), 0)
    col = lax.broadcasted_iota(jnp.int32, (HG_SUB, HG_SUB), 1)
    tri = (row >= col).astype(F32)
    heads = [slice(h * HG_DIM, (h + 1) * HG_DIM) for h in range(HG_HEADS)]
    fg = lb + (1.0 - lb) * jax.nn.sigmoid(fl)
    lf, kk, qf = jnp.log(fg), 1.0 - fg, jax.nn.silu(q)
    sts = [st_t[sl, :] for sl in heads]
    parts = [[] for _ in heads]
    for s in range(q.shape[0] // HG_SUB):
        r = slice(s * HG_SUB, (s + 1) * HG_SUB)
        b = lf[r]
        d = 1
        while d < HG_SUB:
            b = b + _shift_rows(b, d)
            d *= 2
        b_mid = jnp.sum(lf[r][:HG_SUB // 2], axis=0, keepdims=True)
        b_end = jnp.sum(lf[r], axis=0, keepdims=True)
        q_in, k_in = qf[r] * jnp.exp(b - b_mid), kk[r] * jnp.exp(b_mid - b)
        q_st, k_st, decay = qf[r] * jnp.exp(b), kk[r] * jnp.exp(b_end - b), jnp.exp(b_end)
        for h, sl in enumerate(heads):
            sc = _dot(q_in[:, sl], k_in[:, sl], _NT) * tri
            parts[h].append(_dot(sc, iv[r, sl], _NN) + _dot(q_st[:, sl], sts[h], _NT))
            sts[h] = sts[h] * decay[:, sl] + _dot(iv[r, sl], k_st[:, sl], _TN)
    outs = [_rms(jnp.concatenate(parts[h], axis=0), gain[:, sl]) * jax.nn.silu(g[:, sl]) for h, sl in enumerate(heads)]
    return jnp.concatenate(sts, axis=0), jnp.concatenate(outs, axis=1)


def _hg_specs(proj, nb):
    return [pl.BlockSpec((HG_BLOCK, HG_WIDTH), lambda i, c=c, f=nb: (f(i), c)) for c in range(4)]


def hgrn2_fwd(proj, lb, gain):
    L = proj.shape[0]
    n = L // HG_BLOCK

    def body(q, fl, iv, g, lb_r, gain_r, o_ref, st_ref, st):
        @pl.when(pl.program_id(0) == 0)
        def _():
            st[...] = jnp.zeros(st.shape, F32)

        st_ref[0] = st[...]
        new, o = _hg_block(st[...], q[...], fl[...], iv[...], g[...], lb_r[...], gain_r[...])
        st[...] = new
        o_ref[...] = o.astype(o_ref.dtype)

    pspec = pl.BlockSpec((1, HG_WIDTH), lambda i: (0, 0))
    return pl.pallas_call(
        body, name="hgrn2_fwd", grid=(n,),
        in_specs=_hg_specs(proj, lambda i: i) + [pspec, pspec],
        out_specs=[pl.BlockSpec((HG_BLOCK, HG_WIDTH), lambda i: (i, 0)),
                   pl.BlockSpec((1, HG_WIDTH, HG_DIM), lambda i: (i, 0, 0))],
        out_shape=[jax.ShapeDtypeStruct((L, HG_WIDTH), MXU_DTYPE),
                   jax.ShapeDtypeStruct((n, HG_WIDTH, HG_DIM), F32)],
        scratch_shapes=[pltpu.VMEM((HG_WIDTH, HG_DIM), F32)],
        compiler_params=_cparams(("arbitrary",)),
    )(proj, proj, proj, proj, lb, gain)


def hgrn2_bwd(proj, lb, gain, states, do):
    L = proj.shape[0]
    n = L // HG_BLOCK

    def body(q, fl, iv, g, lb_r, gain_r, st_r, do_r, dproj, dlb, dgain, dst):
        @pl.when(pl.program_id(0) == 0)
        def _():
            dst[...] = jnp.zeros(dst.shape, F32)
            dlb[...] = jnp.zeros(dlb.shape, F32)
            dgain[...] = jnp.zeros(dgain.shape, F32)

        _, vjp = jax.vjp(_hg_block, st_r[0], q[...], fl[...], iv[...], g[...], lb_r[...], gain_r[...])
        d_st, dq, dfl, div, dg, d_lb, d_gain = vjp((dst[...], do_r[...].astype(F32)))
        dst[...] = d_st
        dproj[:, 0 * HG_WIDTH:1 * HG_WIDTH] = dq
        dproj[:, 1 * HG_WIDTH:2 * HG_WIDTH] = dfl
        dproj[:, 2 * HG_WIDTH:3 * HG_WIDTH] = div
        dproj[:, 3 * HG_WIDTH:4 * HG_WIDTH] = dg
        dlb[...] += d_lb
        dgain[...] += d_gain

    rev = lambda i: n - 1 - i
    pspec = pl.BlockSpec((1, HG_WIDTH), lambda i: (0, 0))
    return pl.pallas_call(
        body, name="hgrn2_bwd", grid=(n,),
        in_specs=_hg_specs(proj, rev) + [pspec, pspec,
                                         pl.BlockSpec((1, HG_WIDTH, HG_DIM), lambda i: (rev(i), 0, 0)),
                                         pl.BlockSpec((HG_BLOCK, HG_WIDTH), lambda i: (rev(i), 0))],
        out_specs=[pl.BlockSpec((HG_BLOCK, 4 * HG_WIDTH), lambda i: (rev(i), 0)), pspec, pspec],
        out_shape=[jax.ShapeDtypeStruct((L, 4 * HG_WIDTH), F32),
                   jax.ShapeDtypeStruct((1, HG_WIDTH), F32), jax.ShapeDtypeStruct((1, HG_WIDTH), F32)],
        scratch_shapes=[pltpu.VMEM((HG_WIDTH, HG_DIM), F32)],
        compiler_params=_cparams(("arbitrary",)),
    )(proj, proj, proj, proj, lb, gain, states, do)


def _rope_rms(x, gain_p, cos_p, sin_p):
    n = x * lax.rsqrt(jnp.sum(x * x, axis=-1, keepdims=True) * (1.0 / MLA_ROPE) + EPS) * gain_p
    r = lax.broadcasted_iota(jnp.int32, (128, 128), 0)
    c = lax.broadcasted_iota(jnp.int32, (128, 128), 1)
    swap = (r == (c + 64) % 128).astype(F32)
    return n * cos_p + _dot(n, swap, _NN, HI) * sin_p


MLA_IN = MLA_Q_RANK + MLA_KV_RANK + 128


def _mla_prep(x, cos_p, sin_p, q_a, w_uq, kv_a, w_ukv, qn_nope, qn_rope, kn_nope, kn_rope):
    c_q, c_kv, kpe = x[:, :MLA_Q_RANK], x[:, MLA_Q_RANK:MLA_Q_RANK + MLA_KV_RANK], x[:, MLA_Q_RANK + MLA_KV_RANK:]
    q = _dot(_rms(c_q, q_a), w_uq, _NN)
    kv = _dot(_rms(c_kv, kv_a), w_ukv, _NN)
    k_pe = _rope_rms(kpe, kn_rope, cos_p, sin_p)
    qs, ks = [], []
    for h in range(MLA_HEADS):
        qs.append(_rms(q[:, h * MLA_DK:h * MLA_DK + MLA_NOPE], qn_nope))
        qs.append(_rope_rms(q[:, h * MLA_DK + MLA_NOPE:(h + 1) * MLA_DK], qn_rope, cos_p, sin_p))
        ks.append(_rms(kv[:, h * MLA_NOPE:(h + 1) * MLA_NOPE], kn_nope))
        ks.append(k_pe)
    return jnp.concatenate(qs, axis=1), jnp.concatenate(ks, axis=1), kv[:, MLA_HEADS * MLA_NOPE:]


def _mla_prep_opds(proj, cos_p, sin_p, params, tm, grads):
    g = (lambda k: k) if grads else (lambda k: None)
    assert (4 * HG_WIDTH) % MLA_IN == 0
    return ([rows(proj, tm, g('blk'), col=4 * HG_WIDTH // MLA_IN, width=MLA_IN), rows(cos_p, tm), rows(sin_p, tm)]
            + [full(p, g('acc')) for p in params])


def mla_prep_fwd(proj, cos_p, sin_p, params, tm):
    L = proj.shape[0]
    W = MLA_HEADS * MLA_DK
    rb = lambda w: (tm, w)
    outs = [((L, W), MXU_DTYPE, rb(W), lambda i: (i, 0)), ((L, W), MXU_DTYPE, rb(W), lambda i: (i, 0)),
            ((L, MLA_HEADS * MLA_V), MXU_DTYPE, rb(MLA_HEADS * MLA_V), lambda i: (i, 0))]
    return blocked_fwd(_mla_prep, _mla_prep_opds(proj, cos_p, sin_p, params, tm, False), outs, L // tm, "mla_prep_fwd")


def mla_prep_bwd(proj, cos_p, sin_p, params, dq, dk, dv, tm):
    L = proj.shape[0]
    return blocked_bwd(_mla_prep, _mla_prep_opds(proj, cos_p, sin_p, params, tm, True),
                       [rows(dq, tm), rows(dk, tm), rows(dv, tm)], L // tm, "mla_prep_bwd")


def _scores(q, k, scale, shift=None):
    s = _dot(q, k, _NT) * scale
    if shift is None:
        return s
    row = lax.broadcasted_iota(jnp.int32, s.shape, 0)
    col = lax.broadcasted_iota(jnp.int32, s.shape, 1)
    return jnp.where(col <= row + shift, s, -jnp.inf)


ATTN_ROWS = 512
ATTN_WIDE = 2


def attn_fwd(q, k, v, scale, t):
    L = q.shape[0]
    tq = ATTN_WIDE * t

    def body(q_ref, k_ref, v_ref, o_ref, lse_ref):
        i = pl.program_id(1)
        qb = q_ref[...]

        def step(j, carry, shift=None):
            m, l, acc = carry
            kj = k_ref[pl.ds(pl.multiple_of(j * t, t), t), :]
            vj = v_ref[pl.ds(pl.multiple_of(j * t, t), t), :]
            s = _scores(qb, kj, scale, shift)
            m_new = jnp.maximum(m, jnp.max(s, axis=-1, keepdims=True))
            p = jnp.exp(s - m_new)
            alpha = jnp.exp(m - m_new)
            return m_new, alpha * l + jnp.sum(p, axis=-1, keepdims=True), alpha * acc + _dot(p, vj, _NN)

        carry = (jnp.full((tq, 1), -jnp.inf, F32), jnp.zeros((tq, 1), F32), jnp.zeros((tq, MLA_V), F32))
        carry = lax.fori_loop(0, ATTN_WIDE * i, step, carry)
        for d in range(ATTN_WIDE):
            carry = step(ATTN_WIDE * i + d, carry, -d * t)
        m, l, acc = carry
        o_ref[...] = acc / l
        lse_ref[...] = jnp.broadcast_to(m + jnp.log(l), lse_ref.shape)

    hspec = lambda rows_, w: pl.BlockSpec((rows_, w), lambda h, i: (0, h))
    bspec = lambda w: pl.BlockSpec((tq, w), lambda h, i: (i, h))
    return pl.pallas_call(
        body, name="attn_fwd", grid=(MLA_HEADS, L // tq),
        in_specs=[bspec(MLA_DK), hspec(L, MLA_DK), hspec(L, MLA_V)],
        out_specs=[bspec(MLA_V), bspec(MLA_V)],
        out_shape=[jax.ShapeDtypeStruct((L, MLA_HEADS * MLA_V), F32)] * 2,
        compiler_params=_cparams(("parallel", "parallel")),
    )(q, k, v)


def attn_bwd_dq(q, k, v, o, lse, do, scale, t):
    L = q.shape[0]
    tq = ATTN_WIDE * t

    def body(q_ref, k_ref, v_ref, o_ref, lse_ref, do_ref, dq_ref):
        i = pl.program_id(1)
        qb, dob = q_ref[...], do_ref[...]
        delta = jnp.sum(dob * o_ref[...], axis=-1, keepdims=True)
        lse_c = jnp.max(lse_ref[...], axis=-1, keepdims=True)

        def step(j, dq, shift=None):
            kj = k_ref[pl.ds(pl.multiple_of(j * t, t), t), :]
            vj = v_ref[pl.ds(pl.multiple_of(j * t, t), t), :]
            p = jnp.exp(_scores(qb, kj, scale, shift) - lse_c)
            ds = p * (_dot(dob, vj, _NT) - delta) * scale
            return dq + _dot(ds, kj, _NN)

        dq = lax.fori_loop(0, ATTN_WIDE * i, step, jnp.zeros((tq, MLA_DK), F32))
        for d in range(ATTN_WIDE):
            dq = step(ATTN_WIDE * i + d, dq, -d * t)
        dq_ref[...] = dq

    hspec = lambda w: pl.BlockSpec((L, w), lambda h, i: (0, h))
    bspec = lambda w: pl.BlockSpec((tq, w), lambda h, i: (i, h))
    return pl.pallas_call(
        body, name="attn_bwd_dq", grid=(MLA_HEADS, L // tq),
        in_specs=[bspec(MLA_DK), hspec(MLA_DK), hspec(MLA_V), bspec(MLA_V), bspec(MLA_V), bspec(MLA_V)],
        out_specs=bspec(MLA_DK),
        out_shape=jax.ShapeDtypeStruct((L, MLA_HEADS * MLA_DK), F32),
        compiler_params=_cparams(("parallel", "parallel")),
    )(q, k, v, o, lse, do)


def attn_bwd_dkv(q, k, v, o, lse, do, scale, t):
    L = q.shape[0]
    tk = ATTN_WIDE * t

    def body(q_ref, k_ref, v_ref, o_ref, lse_ref, do_ref, dk_ref, dv_ref):
        j = pl.program_id(1)
        kb, vb = k_ref[...], v_ref[...]

        def step(i, carry, shift=None):
            dk, dv = carry
            r = pl.ds(pl.multiple_of(i * t, t), t)
            qi, doi = q_ref[r, :], do_ref[r, :]
            delta = jnp.sum(doi * o_ref[r, :], axis=-1, keepdims=True)
            lse_c = jnp.max(lse_ref[r, :], axis=-1, keepdims=True)
            p = jnp.exp(_scores(qi, kb, scale, shift) - lse_c)
            ds = p * (_dot(doi, vb, _NT) - delta) * scale
            return dk + _dot(ds, qi, _TN), dv + _dot(p, doi, _TN)

        carry = (jnp.zeros((tk, MLA_DK), F32), jnp.zeros((tk, MLA_V), F32))
        for d in range(ATTN_WIDE):
            carry = step(ATTN_WIDE * j + d, carry, d * t)
        dk, dv = lax.fori_loop(ATTN_WIDE * (j + 1), L // t, step, carry)
        dk_ref[...] = dk
        dv_ref[...] = dv

    hspec = lambda w: pl.BlockSpec((L, w), lambda h, j: (0, h))
    bspec = lambda w: pl.BlockSpec((tk, w), lambda h, j: (j, h))
    return pl.pallas_call(
        body, name="attn_bwd_dkv", grid=(MLA_HEADS, L // tk),
        in_specs=[hspec(MLA_DK), bspec(MLA_DK), bspec(MLA_V), hspec(MLA_V), hspec(MLA_V), hspec(MLA_V)],
        out_specs=[bspec(MLA_DK), bspec(MLA_V)],
        out_shape=[jax.ShapeDtypeStruct((L, MLA_HEADS * MLA_DK), F32), jax.ShapeDtypeStruct((L, MLA_HEADS * MLA_V), F32)],
        compiler_params=_cparams(("parallel", "parallel")),
    )(q, k, v, o, lse, do)


S5_LANES = S5_GB * S5_STATE
S5_BWD_CHUNK = 1024


def _cmul(ar, ai, br, bi):
    return ar * br - ai * bi, ar * bi + ai * br


def _a_powers(ar, ai, reverse):
    a2 = _cmul(ar, ai, ar, ai)
    a4 = _cmul(*a2, *a2)
    row = lax.broadcasted_iota(jnp.int32, (8, ar.shape[1]), 0)
    e = (8 - row) if reverse else (row + 1)
    tr, ti = jnp.ones((8, ar.shape[1]), F32), jnp.zeros((8, ar.shape[1]), F32)
    for bit, (pr, pi) in ((1, (ar, ai)), (2, a2), (4, a4), (8, _cmul(*a4, *a4))):
        nr, ni = _cmul(tr, ti, pr, pi)
        sel = (e & bit) != 0
        tr, ti = jnp.where(sel, nr, tr), jnp.where(sel, ni, ti)
    pows = []
    for d, (pr, pi) in zip((1, 2, 4), ((ar, ai), a2, a4)):
        keep = (row < 8 - d) if reverse else (row >= d)
        pows.append((jnp.where(keep, pr, 0.0), jnp.where(keep, pi, 0.0)))
    return pows, (tr, ti)


def _scan8(xr, xi, pows, table, cr, ci, reverse):
    for d, (pr, pi) in zip((1, 2, 4), pows):
        shift = 8 - d if reverse else d
        mr, mi = _cmul(pr, pi, pltpu.roll(xr, shift, 0), pltpu.roll(xi, shift, 0))
        xr, xi = xr + mr, xi + mi
    mr, mi = _cmul(table[0], table[1], cr, ci)
    return xr + mr, xi + mi


def _row_of(x, r):
    row = lax.broadcasted_iota(jnp.int32, x.shape, 0)
    return jnp.sum(jnp.where(row == r, x, 0.0), axis=0, keepdims=True)


def _s5_scan_fwd(h_re, h_im, ar, ai, L):
    pows, table = _a_powers(ar, ai, False)

    def step(i, carry):
        r = pl.ds(pl.multiple_of(i * 8, 8), 8)
        xr, xi = _scan8(h_re[r, :], h_im[r, :], pows, table, carry[0], carry[1], False)
        h_re[r, :] = xr
        h_im[r, :] = xi
        return xr[7:8, :], xi[7:8, :]

    z = jnp.zeros((1, ar.shape[1]), F32)
    lax.fori_loop(0, L // 8, step, (z, z))


def _s5_specs(L):
    return [pl.BlockSpec((L, 128), lambda g: (0, g)),
            pl.BlockSpec((1, 128, S5_LANES), lambda g: (g, 0, 0)), pl.BlockSpec((1, 128, S5_LANES), lambda g: (g, 0, 0)),
            pl.BlockSpec((1, 1, S5_LANES), lambda g: (g, 0, 0)), pl.BlockSpec((1, 1, S5_LANES), lambda g: (g, 0, 0)),
            pl.BlockSpec((1, S5_LANES, 128), lambda g: (g, 0, 0)), pl.BlockSpec((1, S5_LANES, 128), lambda g: (g, 0, 0))]


def s5_fwd(u, w_re, w_im, a_re, a_im, c_re, c_im):
    L, D = u.shape

    def body(u_ref, wr, wi, ar, ai, cr, ci, y_ref, h_re, h_im):
        ub = u_ref[...]
        h_re[...] = _dot(ub, wr[0], _NN)
        h_im[...] = _dot(ub, wi[0], _NN)
        _s5_scan_fwd(h_re, h_im, ar[0], ai[0], L)
        y_ref[...] = _dot(h_re[...], cr[0], _NN) - _dot(h_im[...], ci[0], _NN)

    return pl.pallas_call(
        body, name="s5_fwd", grid=(D // 128,),
        in_specs=_s5_specs(L), out_specs=pl.BlockSpec((L, 128), lambda g: (0, g)),
        out_shape=jax.ShapeDtypeStruct((L, D), F32),
        scratch_shapes=[pltpu.VMEM((L, S5_LANES), F32), pltpu.VMEM((L, S5_LANES), F32)],
        compiler_params=_cparams(("parallel",)),
    )(u, w_re, w_im, a_re, a_im, c_re, c_im)


def s5_bwd(u, w_re, w_im, a_re, a_im, c_re, c_im, dy, du_plus, tc):
    L, D = u.shape
    nch = L // tc

    def body(u_ref, wr, wi, ar_ref, ai_ref, cr, ci, dy_ref, plus_ref, du_ref, dwr, dwi, dar, dai, dcr, dci, h_re, h_im,
             g_re, g_im):
        ar, ai = ar_ref[0], ai_ref[0]
        ub = u_ref[...]
        h_re[...] = _dot(ub, wr[0], _NN)
        h_im[...] = _dot(ub, wi[0], _NN)
        _s5_scan_fwd(h_re, h_im, ar, ai, L)
        dyb = dy_ref[...]
        dcr[0] = _dot(h_re[...], dyb, _TN)
        dci[0] = -_dot(h_im[...], dyb, _TN)
        pows, table = _a_powers(ar, -ai, True)
        dwr[0] = jnp.zeros((128, S5_LANES), F32)
        dwi[0] = jnp.zeros((128, S5_LANES), F32)
        z1 = jnp.zeros((1, S5_LANES), F32)
        z8 = jnp.zeros((8, S5_LANES), F32)

        def chunk(cc, carry):
            c0 = pl.multiple_of((nch - 1 - cc) * tc, tc)
            rows_c = pl.ds(c0, tc)
            dyc = dy_ref[rows_c, :]
            g_re[...] = _dot(dyc, cr[0], _NT)
            g_im[...] = -_dot(dyc, ci[0], _NT)

            def step(ii, cy):
                gr_c, gi_c, acc_r, acc_i = cy
                i8 = pl.multiple_of((tc // 8 - 1 - ii) * 8, 8)
                rl = pl.ds(i8, 8)
                xr, xi = _scan8(g_re[rl, :], g_im[rl, :], pows, table, gr_c, gi_c, True)
                g_re[rl, :] = xr
                g_im[rl, :] = xi
                t0 = c0 + i8
                hb_r, hb_i = h_re[pl.ds(t0, 8), :], h_im[pl.ds(t0, 8), :]
                tp = pl.multiple_of(jnp.maximum(t0 - 8, 0), 8)
                first = (t0 > 0).astype(F32)
                pr = h_re[pl.ds(tp, 8), :][7:8, :] * first
                pi = h_im[pl.ds(tp, 8), :][7:8, :] * first
                row = lax.broadcasted_iota(jnp.int32, xr.shape, 0)
                hp_r = jnp.where(row == 0, pr, pltpu.roll(hb_r, 1, 0))
                hp_i = jnp.where(row == 0, pi, pltpu.roll(hb_i, 1, 0))
                return (xr[0:1, :], xi[0:1, :],
                        acc_r + xr * hp_r + xi * hp_i, acc_i + xi * hp_r - xr * hp_i)

            cy = lax.fori_loop(0, tc // 8, step, carry)
            uc = u_ref[rows_c, :]
            gr, gi = g_re[...], g_im[...]
            du_ref[rows_c, :] = _dot(gr, wr[0], _NT) + _dot(gi, wi[0], _NT) + plus_ref[rows_c, :]
            dwr[0] += _dot(uc, gr, _TN)
            dwi[0] += _dot(uc, gi, _TN)
            return cy

        _, _, acc_r, acc_i = lax.fori_loop(0, nch, chunk, (z1, z1, z8, z8))
        dar[0] = jnp.sum(acc_r, axis=0, keepdims=True)
        dai[0] = jnp.sum(acc_i, axis=0, keepdims=True)

    specs = _s5_specs(L)
    return pl.pallas_call(
        body, name="s5_bwd", grid=(D // 128,),
        in_specs=specs + [pl.BlockSpec((L, 128), lambda g: (0, g))] * 2,
        out_specs=[pl.BlockSpec((L, 128), lambda g: (0, g))] + specs[1:],
        out_shape=[jax.ShapeDtypeStruct((L, D), F32)] + [jax.ShapeDtypeStruct(x.shape, F32)
                                                        for x in (w_re, w_im, a_re, a_im, c_re, c_im)],
        scratch_shapes=[pltpu.VMEM((L, S5_LANES), F32), pltpu.VMEM((L, S5_LANES), F32),
                        pltpu.VMEM((tc, S5_LANES), F32), pltpu.VMEM((tc, S5_LANES), F32)],
        compiler_params=_cparams(("parallel",)),
    )(u, w_re, w_im, a_re, a_im, c_re, c_im, dy, du_plus)


def _s5_discretize(lr, li, ldt, br, bi):
    dt = jnp.exp(ldt)
    mag = jnp.exp(lr * dt)
    ar, ai = mag * jnp.cos(li * dt), mag * jnp.sin(li * dt)
    den = lr * lr + li * li
    zr = ((ar - 1.0) * lr + ai * li) / den
    zi = (ai * lr - (ar - 1.0) * li) / den
    p = lax.broadcasted_iota(jnp.int32, (S5_STATE, S5_STATE * S5_GROUP), 0)
    c = lax.broadcasted_iota(jnp.int32, (S5_STATE, S5_STATE * S5_GROUP), 1)
    rep = (c // S5_GROUP == p).astype(F32)
    zr, zi = _dot(zr, rep, _NN, HI), _dot(zi, rep, _NN, HI)
    return ar, ai, zr * br - zi * bi, zr * bi + zi * br


def _conv_shift(x, d):
    row = lax.broadcasted_iota(jnp.int32, x.shape, 0)
    return jnp.where(row >= d, pltpu.roll(x, d, 0), 0.0)


def _conv_unshift(x, d):
    n = x.shape[0]
    row = lax.broadcasted_iota(jnp.int32, x.shape, 0)
    return jnp.where(row < n - d, pltpu.roll(x, n - d, 0), 0.0)


@functools.partial(jax.custom_vjp, nondiff_argnums=(1,))
def _shift_rows(x, d):
    return _conv_shift(x, d)


_shift_rows.defvjp(lambda x, d: (_conv_shift(x, d), None), lambda d, _, g: (_conv_unshift(g, d),))


def _conv_gate(ug, uv, wg0, wg1, wg2, wv0, wv1, wv2, bg, bv):
    def conv(u, w0, w1, w2, b):
        return u * w2 + _shift_rows(u, 1) * w1 + _shift_rows(u, 2) * w0 + b
    return (jax.nn.silu(conv(ug, wg0, wg1, wg2, bg)) * conv(uv, wv0, wv1, wv2, bv),)


def _rms_fn(x, gain):
    return (_rms(x, gain),)


def _softmax_rows(s):
    e = jnp.exp(s - lax.stop_gradient(jnp.max(s, axis=-1, keepdims=True)))
    return e / jnp.sum(e, axis=-1, keepdims=True)


def _xa_core(qp, k, v, q_gain):
    dh = qp.shape[1] // XA_HEADS
    outs = []
    for h in range(XA_HEADS):
        sl = slice(h * dh, (h + 1) * dh)
        p = _softmax_rows(_dot(_rms(qp[:, sl], q_gain), k[:, sl], _NT) * (dh ** -0.5))
        outs.append(_dot(p, v[:, sl], _NN))
    return (jnp.concatenate(outs, axis=1),)


def _mem_kv(mem, mem_gain, wk, wv, k_gain):
    m = _rms(mem, mem_gain)
    kp = _dot(m, wk, _NN)
    dh = kp.shape[1] // XA_HEADS
    k = jnp.concatenate([_rms(kp[:, h * dh:(h + 1) * dh], k_gain) for h in range(XA_HEADS)], axis=1)
    return k, _dot(m, wv, _NN)


def _s5_post(y, u, d):
    return (jax.nn.gelu(y + d * u),)


def _glu(a, b):
    return (a * jax.nn.sigmoid(b),)


def _lb_first(logits):
    e = jnp.exp(logits - lax.stop_gradient(jnp.max(logits, axis=0, keepdims=True)))
    return (_row_of(e, 0) / jnp.sum(e, axis=0, keepdims=True),)


def _loss_fn(y, t):
    e = y - t
    part = 0.5 * jnp.sum(e * e) / y.shape[1]
    return e * (1.0 / y.shape[1]), jnp.full((8, 128), part / (8 * 128), F32)


def _out(shape, dtype, tm):
    return (shape, dtype, (tm, shape[1]), lambda i: (i, 0))


def rms_fwd(h, gain, tm, dtype):
    return blocked_fwd(_rms_fn, [rows(h, tm), full(gain)], [_out(h.shape, dtype, tm)], h.shape[0] // tm, "rms_fwd")[0]


def rms_bwd(h, gain, dy, tm, residual):
    return blocked_bwd(_rms_fn, [rows(h, tm, 'blk'), full(gain, 'acc')], [rows(dy, tm)], h.shape[0] // tm, "rms_bwd",
                       plus=rows(residual, tm))


def _adamw_math(w, g, m, v):
    m = ADAM_B1 * m + (1.0 - ADAM_B1) * g
    v = ADAM_B2 * v + (1.0 - ADAM_B2) * jnp.square(g)
    m_hat = m / (1.0 - ADAM_B1 ** ADAM_STEP)
    v_hat = v / (1.0 - ADAM_B2 ** ADAM_STEP)
    return -ADAM_LR * (m_hat / (jnp.sqrt(v_hat) + ADAM_EPS) + ADAM_WD * w), m, v


def adamw(w, g, m, v, name):
    R = w.shape[0]
    tm = _tile(R, 256)
    assert g.shape == w.shape == m.shape == v.shape, (name, w.shape, g.shape)

    def body(w_ref, g_ref, m_ref, v_ref, d_ref, nm_ref, nv_ref):
        d_ref[...], nm_ref[...], nv_ref[...] = _adamw_math(w_ref[...], g_ref[...], m_ref[...], v_ref[...])

    spec = pl.BlockSpec((tm, w.shape[1]), lambda i: (i, 0))
    return pl.pallas_call(
        body, name=name, grid=(R // tm,), in_specs=[spec] * 4, out_specs=[spec] * 3,
        out_shape=[jax.ShapeDtypeStruct(w.shape, F32)] * 3, compiler_params=_cparams(("parallel",)),
    )(w, g, m, v)


def _index_operand(i):
    return jnp.reshape(i, (1,)).astype(jnp.int32)


def adamw_layer(w, mine, other, core, m, v, layer, bufs, name):
    H, C = mine.shape
    tm = _tile(H, 256)
    nb = H // tm
    assert w.shape[1:] == (2 * H, C) and all(b.shape == w.shape for b in bufs), (name, w.shape, mine.shape)

    def body(c_ref, w_ref, mine_ref, other_ref, m_ref, v_ref, *rest):
        g_out, d_ref, nm_ref, nv_ref = rest[-4:]
        g_ = jnp.where(pl.program_id(0) // nb == c_ref[0], mine_ref[...], other_ref[...])
        g_out[...] = g_
        d_ref[...], nm_ref[...], nv_ref[...] = _adamw_math(w_ref[...], g_, m_ref[...], v_ref[...])

    lspec = pl.BlockSpec((None, tm, C), lambda i, c: (layer, i, 0))
    half = lambda sign: pl.BlockSpec(
        (tm, C), lambda i, c: (jnp.clip(i - (c[0] if sign else 1 - c[0]) * nb, 0, nb - 1), 0))
    any_spec = pl.BlockSpec(memory_space=pl.ANY)
    grid_spec = pltpu.PrefetchScalarGridSpec(
        num_scalar_prefetch=1, grid=(2 * nb,),
        in_specs=[lspec, half(True), half(False), lspec, lspec] + [any_spec] * 4, out_specs=[lspec] * 4)
    return pl.pallas_call(
        body, name=name, grid_spec=grid_spec, out_shape=[jax.ShapeDtypeStruct(w.shape, F32)] * 4,
        input_output_aliases={6: 0, 7: 1, 8: 2, 9: 3}, compiler_params=_cparams(("parallel",)),
    )(_index_operand(core), w, mine, other, m, v, *bufs)


def add_own_half(x, core, theirs, name, out_dtype):
    nq, _, H, C = x.shape
    tm = _tile(H, 256)

    def body(c_ref, x_ref, t_ref, o_ref):
        o_ref[...] = (x_ref[...] + t_ref[...]).astype(o_ref.dtype)

    spec = pl.BlockSpec((None, tm, C), lambda q, i, c: (q, i, 0))
    grid_spec = pltpu.PrefetchScalarGridSpec(
        num_scalar_prefetch=1, grid=(nq, H // tm),
        in_specs=[pl.BlockSpec((None, None, tm, C), lambda q, i, c: (q, c[0], i, 0)), spec], out_specs=spec)
    return pl.pallas_call(
        body, name=name, grid_spec=grid_spec, out_shape=jax.ShapeDtypeStruct((nq, H, C), out_dtype),
        compiler_params=_cparams(("parallel", "parallel")),
    )(_index_operand(core), x, theirs)


def add_chips(pair, chip, got, name):
    n, R, C = got.shape
    tm = _tile(R, 256)

    def body(q_ref, *refs):
        acc = refs[0][...].astype(F32)
        for r in refs[1:-1]:
            acc = acc + r[...].astype(F32)
        refs[-1][...] = acc

    grid_spec = pltpu.PrefetchScalarGridSpec(
        num_scalar_prefetch=1, grid=(R // tm,),
        in_specs=[pl.BlockSpec((None, tm, C), lambda i, q: (q[0], i, 0))]
        + [pl.BlockSpec((None, tm, C), lambda i, q, j=j: (j, i, 0)) for j in range(n)],
        out_specs=pl.BlockSpec((tm, C), lambda i, q: (i, 0)))
    return pl.pallas_call(
        body, name=name, grid_spec=grid_spec, out_shape=jax.ShapeDtypeStruct((R, C), F32),
        compiler_params=_cparams(("parallel",)),
    )(_index_operand(chip), pair, *([got] * n))


_HBM = pl.BlockSpec(memory_space=pltpu.HBM)
N_CHIPS = 4


def _my_place():
    return lax.axis_index("x"), lax.axis_index("y"), lax.axis_index("c")


def _comm_call(body, name, xs, out_shapes, n_remote, n_local, sequencer=None):
    sems = [pltpu.SemaphoreType.DMA((n_remote,)), pltpu.SemaphoreType.DMA((n_remote,)),
            pltpu.SemaphoreType.DMA((max(n_local, 1),))]
    if sequencer is None:
        return pl.pallas_call(
            body, name=name, in_specs=[_HBM] * len(xs), out_specs=[_HBM] * len(out_shapes), out_shape=out_shapes,
            scratch_shapes=sems, compiler_params=pltpu.CompilerParams(has_side_effects=True),
        )(*xs)
    peers_of, collective_id = sequencer
    hbm = pltpu.MemorySpace.HBM
    x_refs = [jax.new_ref(x, memory_space=hbm) for x in xs]
    o_refs = [jax.empty_ref(s, memory_space=hbm) for s in out_shapes]

    @pl.kernel(mesh=plsc.ScalarSubcoreMesh(axis_name="sequencer", num_cores=1), name=name, scratch_types=tuple(sems),
               compiler_params=pltpu.CompilerParams(collective_id=collective_id))
    def launch(send_sems, recv_sems, local_sems):
        peers = peers_of(*_my_place())
        barrier = pltpu.get_barrier_semaphore()
        for peer in peers:
            pl.semaphore_signal(barrier, inc=1, device_id=peer, device_id_type=MESH)
        pl.semaphore_wait(barrier, len(peers))
        body(*x_refs, *o_refs, send_sems, recv_sems, local_sems)

    launch()
    return [o[...] for o in o_refs]


def _sibling(mx, my, mc):
    return [(mx, my, 1 - mc)]


def _same_core_of_other_chips(mx, my, mc):
    return [(tx, ty, mc) for tx, ty in _other_chips(mx, my)]


def _run(copies):
    for cp in copies:
        cp.start()
    for cp in copies:
        cp.wait()


def _other_chips(mx, my):
    return [(mx ^ (j >> 1), my ^ (j & 1)) for j in (1, 2, 3)]


def device_gather(x, name):
    H, C = x.shape

    def body(x_ref, o_ref, send_sems, recv_sems, local_sems):
        mx, my, mc = _my_place()
        dst = o_ref.at[2 * mx + my, mc]
        copies = [pltpu.make_async_copy(x_ref, dst, local_sems.at[0])]
        others = [(mx, my, 1 - mc)] + [(tx, ty, c) for tx, ty in _other_chips(mx, my) for c in (mc, 1 - mc)]
        for j, peer in enumerate(others):
            copies.append(pltpu.make_async_remote_copy(src_ref=x_ref, dst_ref=dst, send_sem=send_sems.at[j],
                                                       recv_sem=recv_sems.at[j], device_id=peer, device_id_type=MESH))
        _run(copies)

    out = _comm_call(body, name, [x], [jax.ShapeDtypeStruct((N_CHIPS, 2, H, C), x.dtype)], 7, 1)[0]
    return out.reshape(N_CHIPS * 2 * H, C)


def gather_two_level(xs, name):
    n = len(xs)
    shapes = [jax.ShapeDtypeStruct((2, N_CHIPS) + x.shape[1:], x.dtype) for x in xs]

    def body(*refs):
        x_refs, o_refs = refs[:n], refs[n:2 * n]
        send_sems, recv_sems, local_sems = refs[2 * n:]
        mx, my, mc = _my_place()
        q = 2 * mx + my
        first, local, second = [], [], []
        for i, (x_ref, o_ref) in enumerate(zip(x_refs, o_refs)):
            local.append(pltpu.make_async_copy(x_ref.at[mc], o_ref.at[mc, q], local_sems.at[i]))
            for j, (tx, ty) in enumerate(_other_chips(mx, my)):
                first.append(pltpu.make_async_remote_copy(
                    src_ref=x_ref.at[mc], dst_ref=o_ref.at[mc, q], send_sem=send_sems.at[4 * i + j],
                    recv_sem=recv_sems.at[4 * i + j], device_id=(tx, ty, mc), device_id_type=MESH))
            second.append(pltpu.make_async_remote_copy(
                src_ref=o_ref.at[mc], dst_ref=o_ref.at[mc], send_sem=send_sems.at[4 * i + 3],
                recv_sem=recv_sems.at[4 * i + 3], device_id=(mx, my, 1 - mc), device_id_type=MESH))
        for cp in local + first:
            cp.start()
        for cp in local:
            cp.wait()
        for cp in first:
            cp.wait_recv()
        _run(second)
        for cp in first:
            cp.wait_send()

    return _comm_call(body, name, xs, shapes, 4 * n, n)


def gather_two_level_sequencer(xs, name, collective_id):
    n = len(xs)
    hbm = pltpu.MemorySpace.HBM
    x_refs = [jax.new_ref(x, memory_space=hbm) for x in xs]
    o_refs = [jax.empty_ref(jax.ShapeDtypeStruct((2, N_CHIPS) + x.shape[1:], x.dtype), memory_space=hbm) for x in xs]

    @pl.kernel(mesh=plsc.ScalarSubcoreMesh(axis_name="sequencer", num_cores=1), name=name,
               scratch_types=(pltpu.SemaphoreType.DMA((4 * n,)), pltpu.SemaphoreType.DMA((4 * n,)),
                              pltpu.SemaphoreType.DMA((n,))),
               compiler_params=pltpu.CompilerParams(collective_id=collective_id))
    def launch(send_sems, recv_sems, local_sems):
        mx, my, mc = _my_place()
        peers = [(tx, ty, mc) for tx, ty in _other_chips(mx, my)] + [(mx, my, 1 - mc)]
        barrier = pltpu.get_barrier_semaphore()
        for peer in peers:
            pl.semaphore_signal(barrier, inc=1, device_id=peer, device_id_type=MESH)
        pl.semaphore_wait(barrier, len(peers))
        q = 2 * mx + my
        first, local, second = [], [], []
        for i, (x_ref, o_ref) in enumerate(zip(x_refs, o_refs)):
            local.append(pltpu.make_async_copy(x_ref.at[mc], o_ref.at[mc, q], local_sems.at[i]))
            for j, peer in enumerate(peers[:3]):
                first.append(pltpu.make_async_remote_copy(
                    src_ref=x_ref.at[mc], dst_ref=o_ref.at[mc, q], send_sem=send_sems.at[4 * i + j],
                    recv_sem=recv_sems.at[4 * i + j], device_id=peer, device_id_type=MESH))
            second.append(pltpu.make_async_remote_copy(
                src_ref=o_ref.at[mc], dst_ref=o_ref.at[mc], send_sem=send_sems.at[4 * i + 3],
                recv_sem=recv_sems.at[4 * i + 3], device_id=peers[3], device_id_type=MESH))
        for cp in local + first:
            cp.start()
        for cp in local:
            cp.wait()
        for cp in first:
            cp.wait_recv()
        _run(second)
        for cp in first:
            cp.wait_send()

    launch()
    return [o[...] for o in o_refs]


def pair_swap(xs, name, halves, collective_id=None):
    n = len(xs)
    shapes = [jax.ShapeDtypeStruct(x.shape[:1] + x.shape[2:] if halves else x.shape, x.dtype) for x in xs]

    def body(*refs):
        x_refs, o_refs = refs[:n], refs[n:2 * n]
        send_sems, recv_sems, _ = refs[2 * n:]
        mx, my, mc = _my_place()
        _run([pltpu.make_async_remote_copy(
            src_ref=x_ref.at[:, 1 - mc] if halves else x_ref, dst_ref=o_ref, send_sem=send_sems.at[i],
            recv_sem=recv_sems.at[i], device_id=(mx, my, 1 - mc), device_id_type=MESH)
            for i, (x_ref, o_ref) in enumerate(zip(x_refs, o_refs))])

    return _comm_call(body, name, xs, shapes, n, 0, None if collective_id is None else (_sibling, collective_id))


def chip_all_to_all(xs, name, collective_id=None):
    n = len(xs)
    shapes = [jax.ShapeDtypeStruct((N_CHIPS - 1,) + x.shape[1:], x.dtype) for x in xs]

    def body(*refs):
        x_refs, o_refs = refs[:n], refs[n:2 * n]
        send_sems, recv_sems, _ = refs[2 * n:]
        mx, my, mc = _my_place()
        copies = []
        for i, (x_ref, o_ref) in enumerate(zip(x_refs, o_refs)):
            for j, (tx, ty) in enumerate(_other_chips(mx, my)):
                copies.append(pltpu.make_async_remote_copy(
                    src_ref=x_ref.at[2 * tx + ty], dst_ref=o_ref.at[j], send_sem=send_sems.at[3 * i + j],
                    recv_sem=recv_sems.at[3 * i + j], device_id=(tx, ty, mc), device_id_type=MESH))
        _run(copies)

    return _comm_call(body, name, xs, shapes, 3 * n, 0,
                      None if collective_id is None else (_same_core_of_other_chips, collective_id))


WEIGHTS = ['norm_mix', 'norm_xa', 'norm_mem', 'norm_ffn', 'xa_wq', 'xa_wk', 'xa_wv', 'xa_wo', 'xa_q_norm', 'xa_k_norm',
           'ffn_w_up', 'ffn_conv_w', 'ffn_conv_b', 'ffn_w_down', 'hg_lb_logits', 'mix_w_in', 'hg_out_norm',
           'mla_q_a_norm', 'mla_w_uq', 'mla_kv_a_norm', 'mla_w_ukv', 'mla_qn_nope', 'mla_qn_rope', 'mla_kn_nope',
           'mla_kn_rope', 'mix_w_out', 's5_lam_re', 's5_lam_im', 's5_log_dt', 's5_b_re', 's5_b_im', 's5_c_re',
           's5_c_im', 's5_d', 's5_w_glu_a', 's5_w_glu_b']
INPUTS = ['x', 'mem', 'positions'] + WEIGHTS + ['loss_target'] + ['m_' + n for n in WEIGHTS] + ['v_' + n for n in WEIGHTS]
SHARD_AXIS = {'xa_wq': 1, 'xa_wk': 1, 'xa_wv': 1, 'xa_wo': 1, 'ffn_w_up': 2, 'ffn_conv_w': 2, 'ffn_w_down': 1,
              'mix_w_in': 2, 'mla_w_uq': 2, 'mla_w_ukv': 2, 'mix_w_out': 1, 's5_d': 1, 's5_w_glu_a': 1, 's5_w_glu_b': 1}
BIG = ['xa_wq', 'xa_wk', 'xa_wv', 'xa_wo', 'ffn_w_up', 'ffn_w_down', 'mix_w_in', 'mix_w_out', 's5_w_glu_a', 's5_w_glu_b']
FIRST_NEEDED = ('mix_w_in', 'mix_w_out')
SMALL_SHARDED = [n for n in WEIGHTS if n in SHARD_AXIS and n not in BIG]
REPLICATED = [n for n in WEIGHTS if n not in SHARD_AXIS]
SMALL = SMALL_SHARDED + REPLICATED
PACK_W = 1024
ROW_MULT = 16
W_IN_SHARD = IN_WIDTH // N_CHIPS
W_IN_SHARD_PAD = 640


def _pack(flats, mult=ROW_MULT):
    flat = jnp.concatenate([f.reshape(-1) for f in flats])
    unit = mult * PACK_W
    n = -(-flat.shape[0] // unit) * unit
    return jnp.pad(flat, (0, n - flat.shape[0])).reshape(n // PACK_W, PACK_W)


def _unpack(packed, shapes):
    flat, out, o = packed.reshape(-1), [], 0
    for s in shapes:
        n = math.prod(s)
        out.append(flat[o:o + n].reshape(s))
        o += n
    return out


def _rope_pad(w):
    z = jnp.zeros(w.shape[:-1] + (MLA_ROPE // 2,), w.dtype)
    return jnp.concatenate([w[..., :MLA_ROPE // 2], z, w[..., MLA_ROPE // 2:], z], axis=-1)


def _rope_unpad(g):
    return jnp.concatenate([g[..., :MLA_ROPE // 2], g[..., 64:64 + MLA_ROPE // 2]], axis=-1)


def _blockdiag_in(bb):
    nb = bb.shape[0] // S5_GB
    t = bb.reshape(nb, S5_GB, S5_STATE, S5_GROUP).transpose(0, 1, 3, 2)
    return jnp.einsum('bgmp,gh->bgmhp', t, jnp.eye(S5_GB, dtype=bb.dtype)).reshape(nb, S5_GB * S5_GROUP, S5_LANES)


def _blockdiag_in_t(dw):
    nb = dw.shape[0]
    t = jnp.einsum('bgmhp,gh->bgmp', dw.reshape(nb, S5_GB, S5_GROUP, S5_GB, S5_STATE), jnp.eye(S5_GB, dtype=dw.dtype))
    return t.transpose(0, 1, 3, 2).reshape(nb * S5_GB, S5_STATE, S5_GROUP)


def _blockdiag_out(c):
    nb = c.shape[0] // S5_GB
    t = c.reshape(nb, S5_GB, S5_GROUP, S5_STATE).transpose(0, 1, 3, 2)
    return jnp.einsum('bgpm,gh->bgphm', t, jnp.eye(S5_GB, dtype=c.dtype)).reshape(nb, S5_LANES, S5_GB * S5_GROUP)


def _blockdiag_out_t(dc):
    nb = dc.shape[0]
    t = jnp.einsum('bgphm,gh->bgpm', dc.reshape(nb, S5_GB, S5_STATE, S5_GB, S5_GROUP), jnp.eye(S5_GB, dtype=dc.dtype))
    return t.transpose(0, 1, 3, 2).reshape(nb * S5_GB, S5_GROUP, S5_STATE)


def _gather_weights(P):
    def halves(x):
        return x if x.shape[0] == 2 else x.reshape(2, x.shape[1] // 2, x.shape[2])

    now = [n for n in BIG if n in FIRST_NEEDED]
    later = [n for n in BIG if n not in FIRST_NEEDED]
    xs = [halves(P[n].astype(BF16)) for n in now] + [halves(_pack([P[n] for n in SMALL_SHARDED], 2 * ROW_MULT)[None])]
    got = gather_two_level(xs, "gather_weights")
    got, xs_later = lax.optimization_barrier((got, [halves(P[n].astype(BF16)) for n in later]))
    got_later = gather_two_level_sequencer(xs_later, "gather_weights_later", 1)
    full_w = {}
    for n, g in list(zip(now, got[:-1])) + list(zip(later, got_later)):
        two_layers, by_rows = P[n].shape[0] == 2, SHARD_AXIS[n] == 1
        if two_layers and by_rows:
            full_w[n] = g.reshape(2, N_CHIPS * g.shape[2], g.shape[3])
        elif two_layers:
            full_w[n] = g.transpose(0, 2, 1, 3).reshape(2, g.shape[2], N_CHIPS * g.shape[3])
        elif by_rows:
            full_w[n] = g.transpose(1, 0, 2, 3).reshape(1, 2 * N_CHIPS * g.shape[2], g.shape[3])
        else:
            full_w[n] = g.transpose(0, 2, 1, 3).reshape(1, 2 * g.shape[2], N_CHIPS * g.shape[3])
    small = got[-1].transpose(1, 0, 2, 3).reshape(N_CHIPS, -1, PACK_W)
    per_chip = [_unpack(small[q], [P[n].shape for n in SMALL_SHARDED]) for q in range(N_CHIPS)]
    for i, n in enumerate(SMALL_SHARDED):
        full_w[n] = jnp.concatenate([per_chip[q][i] for q in range(N_CHIPS)], axis=SHARD_AXIS[n])
    return full_w


def _reduce_batch(items, small, P, results, tag, ids):
    mx, my, mc = _my_place()
    q = 2 * mx + my
    ids = ids or {}
    names = [f"{n}_{lyr}" for n, lyr, _ in items] + (['small'] if small is not None else [])
    xs = [g.reshape(N_CHIPS, 2, g.shape[1] // 2, g.shape[2]) for g in [g for _, _, g in items] + ([small] if small is not None else [])]
    def after(vals, tie):
        return (vals, None) if tie is None else lax.optimization_barrier((vals, tie))

    theirs = pair_swap(xs, "grads_pair_swap_" + tag, True, ids.get('swap'))
    theirs, tie = after(theirs, (yield None))
    pair = [add_own_half(x, mc, t, "grads_pair_sum_" + n, F32 if n == 'small' else BF16) for x, t, n in zip(xs, theirs, names)]
    got = chip_all_to_all(pair, "grads_chip_all_to_all_" + tag, ids.get('a2a'))
    got, tie = after(got, (yield pair[-1] if tie is None else tie))
    summed = [add_chips(p, q, g, "grads_chip_sum_" + n) for p, g, n in zip(pair, got, names)]
    other = pair_swap(summed[:len(items)], "grads_pair_join_" + tag, False, ids.get('join'))
    if small is not None:
        results['small_sum'] = device_gather(summed[-1], "grads_small_gather")
    other, tie = after(other, (yield summed[-1] if tie is None else tie))
    for (n, lyr, _), s, o in zip(items, summed, other):
        if n == 'mix_w_in':
            s, o = s[:, :W_IN_SHARD], o[:, :W_IN_SHARD]
        view = (P[n].shape[0], 2 * s.shape[0], P[n].shape[-1])
        bufs = results.get(n) or [lax.empty(view, F32) for _ in range(4)]
        results[n] = adamw_layer(P[n].reshape(view), s, o, mc, P['m_' + n].reshape(view), P['v_' + n].reshape(view), lyr,
                                 bufs, f"adamw_{n}_{lyr}")
    yield tie


def _update_small(small_sum, GS, P):
    mx, my, _ = _my_place()
    q = 2 * mx + my
    g_small = dict(zip(SMALL, _unpack(small_sum, [GS[n].shape for n in SMALL])))
    for n in SMALL_SHARDED:
        s = P[n].shape[SHARD_AXIS[n]]
        g_small[n] = lax.dynamic_slice_in_dim(g_small[n], q * s, s, axis=SHARD_AXIS[n])
    grad, delta, new_m, new_v = {}, {}, {}, {}
    packed = lambda prefix: _pack([P[prefix + n] for n in SMALL])
    d, m_, v_ = adamw(packed(''), _pack([g_small[n] for n in SMALL]), packed('m_'), packed('v_'), "adamw_small")
    shapes = [P[n].shape for n in SMALL]
    grad.update(g_small)
    for out, pk in ((delta, d), (new_m, m_), (new_v, v_)):
        out.update(zip(SMALL, _unpack(pk, shapes)))
    return grad, delta, new_m, new_v


def _row(v):
    return v.reshape(1, -1)


def _xattn_fwd(h, mem, W, lyr, tm):
    g_xa, g_mem = _row(W['norm_xa'][lyr]), _row(W['norm_mem'][lyr])
    g_q, g_k = _row(W['xa_q_norm'][lyr]), _row(W['xa_k_norm'][lyr])
    wq, wk, wv, wo = (W[n][lyr] for n in ('xa_wq', 'xa_wk', 'xa_wv', 'xa_wo'))
    L, D = h.shape
    M = mem.shape[0]
    hx = rms_fwd(h, g_xa, tm, MXU_DTYPE)
    qp = matmul(hx, wq, name="xa_q")
    kv_opds = [full(mem), full(g_mem), full(wk), full(wv), full(g_k)]
    k, v = blocked_fwd(_mem_kv, kv_opds, [((M, D), F32, (M, D), lambda i: (0, 0))] * 2, 1, "xa_mem_kv")
    o = blocked_fwd(_xa_core, [rows(qp, tm), full(k), full(v), full(g_q)], [_out((L, D), MXU_DTYPE, tm)], L // tm,
                    "xa_core")[0]
    out = matmul(o, wo, add=h, name="xa_o")
    return out, (h, hx, qp, k, v, o)


def _xattn_bwd(dout, saved, mem, W, lyr, tm):
    h, hx, qp, k, v, o = saved
    g_xa, g_mem = _row(W['norm_xa'][lyr]), _row(W['norm_mem'][lyr])
    g_q, g_k = _row(W['xa_q_norm'][lyr]), _row(W['xa_k_norm'][lyr])
    wq, wk, wv, wo = (W[n][lyr] for n in ('xa_wq', 'xa_wk', 'xa_wv', 'xa_wo'))
    L = h.shape[0]
    do = matmul(dout, wo, "nt", name="xa_do")
    d_wo = matmul(o, dout, "tn", name="xa_dwo")
    dqp, dk, dv, d_gq = blocked_bwd(_xa_core, [rows(qp, tm, 'blk'), full(k, 'acc'), full(v, 'acc'), full(g_q, 'acc')],
                                    [rows(do, tm)], L // tm, "xa_core_bwd")
    d_wq = matmul(hx, dqp, "tn", name="xa_dwq")
    dhx = matmul(dqp, wq, "nt", name="xa_dhx")
    dh, d_gxa = rms_bwd(h, g_xa, dhx, tm, dout)
    d_gmem, d_wk, d_wv, d_gk = blocked_bwd(
        _mem_kv, [full(mem), full(g_mem, 'acc'), full(wk, 'acc'), full(wv, 'acc'), full(g_k, 'acc')],
        [full(dk), full(dv)], 1, "xa_mem_kv_bwd")
    by_chip = lambda g: g.reshape(N_CHIPS, g.shape[0] // N_CHIPS, g.shape[1])
    grads = {'norm_xa': d_gxa, 'norm_mem': d_gmem, 'xa_q_norm': d_gq, 'xa_k_norm': d_gk,
             'xa_wq': by_chip(d_wq), 'xa_wk': by_chip(d_wk), 'xa_wv': by_chip(d_wv), 'xa_wo': by_chip(d_wo)}
    return dh, grads


def _conv_params(W, lyr):
    cw, cb = W['ffn_conv_w'][lyr], W['ffn_conv_b'][lyr]
    F = cw.shape[1] // 2
    return [cw[0:1, :F], cw[1:2, :F], cw[2:3, :F], cw[0:1, F:], cw[1:2, F:], cw[2:3, F:], _row(cb[:F]), _row(cb[F:])]


def _ffn_fwd(h, W, lyr, tm):
    L, D = h.shape
    w_up, w_down = W['ffn_w_up'][lyr], W['ffn_w_down'][lyr]
    F = w_down.shape[0]
    hf = rms_fwd(h, _row(W['norm_ffn'][lyr]), tm, MXU_DTYPE)
    ug = matmul(hf, w_up[:, :F], name="ffn_up_gate")
    uv = matmul(hf, w_up[:, F:], name="ffn_up_value")
    opds = [cols(ug, 128), cols(uv, 128)] + [cols(p, 128) for p in _conv_params(W, lyr)]
    a = blocked_fwd(_conv_gate, opds, [((L, F), MXU_DTYPE, (L, 128), lambda j: (0, j))], F // 128, "ffn_conv_gate")[0]
    out = matmul(a, w_down, add=h, name="ffn_down")
    return out, (h, hf, ug, uv, a)


def _ffn_bwd(dout, saved, W, lyr, tm):
    h, hf, ug, uv, a = saved
    w_up, w_down = W['ffn_w_up'][lyr], W['ffn_w_down'][lyr]
    F = w_down.shape[0]
    da = matmul(dout, w_down, "nt", name="ffn_da")
    d_wdown = matmul(a, dout, "tn", name="ffn_dwdown")
    opds = [cols(ug, 128, 'blk'), cols(uv, 128, 'blk')] + [cols(p, 128, 'blk') for p in _conv_params(W, lyr)]
    gs = blocked_bwd(_conv_gate, opds, [cols(da, 128)], F // 128, "ffn_conv_gate_bwd")
    dug, duv = gs[0], gs[1]
    d_cw = jnp.concatenate([jnp.concatenate(gs[2:5], axis=0), jnp.concatenate(gs[5:8], axis=0)], axis=1)
    d_cb = jnp.concatenate([gs[8], gs[9]], axis=1)[0]
    half = N_CHIPS // 2
    d_wup = lax.empty((N_CHIPS, hf.shape[1], w_up.shape[1] // N_CHIPS), F32)
    d_wup = matmul(hf, dug, "tn", name="ffn_dwup_gate", into=(d_wup, 0), col_blocks=half)
    d_wup = matmul(hf, duv, "tn", name="ffn_dwup_value", into=(d_wup, half), col_blocks=half)
    dhf = matmul(dug, w_up[:, :F], "nt", name="ffn_dhf_gate")
    dhf = matmul(duv, w_up[:, F:], "nt", add=dhf, name="ffn_dhf_value")
    dh, d_g = rms_bwd(h, _row(W['norm_ffn'][lyr]), dhf, tm, dout)
    d_wdown = d_wdown.reshape(N_CHIPS, F // N_CHIPS, d_wdown.shape[1])
    return dh, {'norm_ffn': d_g, 'ffn_w_up': d_wup, 'ffn_conv_w': d_cw, 'ffn_conv_b': d_cb, 'ffn_w_down': d_wdown}


def _mla_params(W):
    w_uq = W['mla_w_uq'][0].reshape(MLA_Q_RANK, MLA_HEADS, MLA_QK)
    w_uq = jnp.concatenate([w_uq[..., :MLA_NOPE], _rope_pad(w_uq[..., MLA_NOPE:])], axis=-1)
    w_ukv = W['mla_w_ukv'][0].reshape(MLA_KV_RANK, MLA_HEADS, MLA_NOPE + MLA_V)
    w_ukv = jnp.concatenate([w_ukv[..., :MLA_NOPE].reshape(MLA_KV_RANK, -1), w_ukv[..., MLA_NOPE:].reshape(MLA_KV_RANK, -1)],
                            axis=1)
    return [_row(W['mla_q_a_norm'][0]), w_uq.reshape(MLA_Q_RANK, MLA_HEADS * MLA_DK), _row(W['mla_kv_a_norm'][0]), w_ukv,
            _row(W['mla_qn_nope'][0]), _row(_rope_pad(W['mla_qn_rope'][0])), _row(W['mla_kn_nope'][0]),
            _row(_rope_pad(W['mla_kn_rope'][0]))]


def _w_in_padded(W):
    w = W['mix_w_in'][0]
    return jnp.concatenate([w[:, :IN_WIDTH - MLA_ROPE], _rope_pad(w[:, IN_WIDTH - MLA_ROPE:])], axis=1)


def _mixer0_fwd(h, W, cos_p, sin_p, tm):
    L = h.shape[0]
    t = min(ATTN_ROWS, L // ATTN_WIDE)
    hn = rms_fwd(h, _row(W['norm_mix'][0]), tm, MXU_DTYPE)
    proj = matmul(hn, _w_in_padded(W), name="mix_in")
    logits = W['hg_lb_logits']
    lb = blocked_fwd(_lb_first, [full(logits)], [((1, HG_WIDTH), F32, (1, HG_WIDTH), lambda i: (0, 0))], 1, "hg_lb")[0]
    gain = _row(W['hg_out_norm'][0])
    o_hg, states = hgrn2_fwd(proj, lb, gain)
    mp = _mla_params(W)
    q, k, v = mla_prep_fwd(proj, cos_p, sin_p, mp, tm)
    scale = MLA_QK ** -0.5
    o_mla, lse = attn_fwd(q, k, v, scale, t)
    w_out = W['mix_w_out'][0]
    out = matmul(o_hg, w_out[:HG_WIDTH], add=h, name="mix_out_hg")
    out = matmul(o_mla, w_out[HG_WIDTH:], add=out, name="mix_out_mla")
    return out, (h, hn, proj, lb, o_hg, states, q, k, v, o_mla, lse)


def _mixer0_bwd(dout, saved, W, cos_p, sin_p, tm, part_way=None):
    h, hn, proj, lb, o_hg, states, q, k, v, o_mla, lse = saved
    L = h.shape[0]
    t = min(ATTN_ROWS, L // ATTN_WIDE)
    scale = MLA_QK ** -0.5
    w_out = W['mix_w_out'][0]
    gain = _row(W['hg_out_norm'][0])
    do_hg = matmul(dout, w_out[:HG_WIDTH], "nt", name="mix_do_hg")
    do_mla = matmul(dout, w_out[HG_WIDTH:], "nt", name="mix_do_mla")
    d_wout = jnp.concatenate([matmul(o_hg, dout, "tn", name="mix_dwout_hg"), matmul(o_mla, dout, "tn", name="mix_dwout_mla")],
                             axis=0)
    dq = attn_bwd_dq(q, k, v, o_mla, lse, do_mla, scale, t)
    if part_way is not None:
        dq = part_way(dq)
    dk, dv = attn_bwd_dkv(q, k, v, o_mla, lse, do_mla, scale, t)
    mp = _mla_params(W)
    d_mla, d_qa, d_wuq, d_kva, d_wukv, d_qnn, d_qnr, d_knn, d_knr = mla_prep_bwd(proj, cos_p, sin_p, mp, dq, dk, dv, tm)
    d_hg, d_lb, d_gain = hgrn2_bwd(proj, lb, gain, states, do_hg)
    w_in, n_hg = _w_in_padded(W), 4 * HG_WIDTH
    d_win = jnp.concatenate([matmul(hn, d_hg, "tn", name="mix_dwin_hg"), matmul(hn, d_mla, "tn", name="mix_dwin_mla")], axis=1)
    dhn = matmul(d_hg, w_in[:, :n_hg], "nt", name="mix_dhn_hg")
    dhn = matmul(d_mla, w_in[:, n_hg:], "nt", add=dhn, name="mix_dhn_mla")
    dh, d_g = rms_bwd(h, _row(W['norm_mix'][0]), dhn, tm, dout)
    logits = W['hg_lb_logits']
    d_logits = blocked_bwd(_lb_first, [full(logits, 'acc')], [full(d_lb)], 1, "hg_lb_bwd")[0]
    d_wuq = d_wuq.reshape(MLA_Q_RANK, MLA_HEADS, MLA_DK)
    d_wuq = jnp.concatenate([d_wuq[..., :MLA_NOPE], _rope_unpad(d_wuq[..., MLA_NOPE:])], axis=-1)
    hw = MLA_HEADS * MLA_NOPE
    d_wukv = jnp.concatenate([d_wukv[:, :hw].reshape(MLA_KV_RANK, MLA_HEADS, MLA_NOPE),
                              d_wukv[:, hw:].reshape(MLA_KV_RANK, MLA_HEADS, MLA_V)], axis=-1)
    d_win = jnp.concatenate([d_win[:, :IN_WIDTH - MLA_ROPE], _rope_unpad(d_win[:, IN_WIDTH - MLA_ROPE:])], axis=1)
    d_win = d_win.reshape(d_win.shape[0], N_CHIPS, W_IN_SHARD).transpose(1, 0, 2)
    d_win = jnp.pad(d_win, ((0, 0), (0, 0), (0, W_IN_SHARD_PAD - W_IN_SHARD)))
    d_wout = d_wout.reshape(N_CHIPS, d_wout.shape[0] // N_CHIPS, d_wout.shape[1])
    grads = {'norm_mix': d_g, 'hg_lb_logits': d_logits, 'mix_w_in': d_win, 'hg_out_norm': d_gain,
             'mla_q_a_norm': d_qa, 'mla_w_uq': d_wuq.reshape(1, MLA_Q_RANK, -1), 'mla_kv_a_norm': d_kva,
             'mla_w_ukv': d_wukv.reshape(1, MLA_KV_RANK, -1), 'mla_qn_nope': d_qnn, 'mla_qn_rope': _rope_unpad(d_qnr),
             'mla_kn_nope': d_knn, 'mla_kn_rope': _rope_unpad(d_knr), 'mix_w_out': d_wout}
    return dh,grads


def _s5_inputs(W):
    G = W['s5_lam_re'].shape[1]
    return [W['s5_lam_re'][0], W['s5_lam_im'][0], W['s5_log_dt'][0].reshape(G, 1),
            W['s5_b_re'][0].reshape(G, -1), W['s5_b_im'][0].reshape(G, -1)]


def _mixer1_fwd(h, W, tm):
    L, D = h.shape
    u = rms_fwd(h, _row(W['norm_mix'][1]), tm, F32)
    di = _s5_inputs(W)
    G = di[0].shape[0]
    sq, wide = ((G, S5_STATE), F32, (G, S5_STATE), lambda i: (0, 0)), ((G, S5_STATE * S5_GROUP), F32, (G, S5_STATE * S5_GROUP), lambda i: (0, 0))
    ar, ai, bbr, bbi = blocked_fwd(_s5_discretize, [full(a) for a in di], [sq, sq, wide, wide], 1, "s5_discretize")
    nb = G // S5_GB
    core = (_blockdiag_in(bbr.reshape(G, S5_STATE, S5_GROUP)), _blockdiag_in(bbi.reshape(G, S5_STATE, S5_GROUP)),
            ar.reshape(nb, 1, S5_LANES), ai.reshape(nb, 1, S5_LANES),
            _blockdiag_out(W['s5_c_re'][0]), _blockdiag_out(W['s5_c_im'][0]))
    y = s5_fwd(u, *core)
    d = W['s5_d']
    y2 = blocked_fwd(_s5_post, [rows(y, tm), rows(u, tm), full(d)], [_out((L, D), MXU_DTYPE, tm)], L // tm, "s5_post")[0]
    w_ab = jnp.concatenate([W['s5_w_glu_a'][0], W['s5_w_glu_b'][0]], axis=1)
    ab = matmul(y2, w_ab, name="s5_glu_in")
    out = blocked_fwd(lambda a, b, res: (res + _glu(a, b)[0],),
                      [rows(ab, tm, col=0, width=D), rows(ab, tm, col=1, width=D), rows(h, tm)], [_out((L, D), F32, tm)],
                      L // tm, "s5_glu")[0]
    return out, (h, u, core, y, y2, ab)


def _mixer1_bwd(dout, saved, W, tm):
    h, u, core, y, y2, ab = saved
    L, D = h.shape
    da, db = blocked_bwd(_glu, [rows(ab, tm, 'blk', col=0, width=D), rows(ab, tm, 'blk', col=1, width=D)], [rows(dout, tm)],
                         L // tm, "s5_glu_bwd")
    w_a, w_b = W['s5_w_glu_a'][0], W['s5_w_glu_b'][0]
    dy2 = matmul(da, w_a, "nt", name="s5_dy2_a")
    dy2 = matmul(db, w_b, "nt", add=dy2, name="s5_dy2_b")
    d_wa = matmul(y2, da, "tn", name="s5_dwa")
    d_wb = matmul(y2, db, "tn", name="s5_dwb")
    d = W['s5_d']
    dy, du_skip, d_d = blocked_bwd(_s5_post, [rows(y, tm, 'blk'), rows(u, tm, 'blk'), full(d, 'acc')], [rows(dy2, tm)], L // tm,
                                   "s5_post_bwd")
    du, dwr, dwi, dar, dai, dcr, dci = s5_bwd(u, *core, dy, du_skip, min(S5_BWD_CHUNK, L))
    di = _s5_inputs(W)
    G = di[0].shape[0]
    cts = [dar.reshape(G, S5_STATE), dai.reshape(G, S5_STATE), _blockdiag_in_t(dwr).reshape(G, -1), _blockdiag_in_t(dwi).reshape(G, -1)]
    d_lr, d_li, d_ldt, d_br, d_bi = blocked_bwd(_s5_discretize, [full(a, 'acc') for a in di], [full(c) for c in cts], 1,
                                                "s5_discretize_bwd")
    dh, d_g = rms_bwd(h, _row(W['norm_mix'][1]), du, tm, dout)
    bshape = W['s5_b_re'].shape
    grads = {'norm_mix': d_g, 's5_lam_re': d_lr[None], 's5_lam_im': d_li[None], 's5_log_dt': d_ldt.reshape(1, G),
             's5_b_re': d_br.reshape(bshape), 's5_b_im': d_bi.reshape(bshape), 's5_c_re': _blockdiag_out_t(dcr)[None],
             's5_c_im': _blockdiag_out_t(dci)[None], 's5_d': d_d, 's5_w_glu_a': d_wa.reshape(N_CHIPS, -1, D), 's5_w_glu_b': d_wb.reshape(N_CHIPS, -1, D)}
    return dh,grads


def kernel(x, mem, positions, norm_mix, norm_xa, norm_mem, norm_ffn, xa_wq, xa_wk, xa_wv, xa_wo, xa_q_norm, xa_k_norm, ffn_w_up, ffn_conv_w, ffn_conv_b, ffn_w_down, hg_lb_logits, mix_w_in, hg_out_norm, mla_q_a_norm, mla_w_uq, mla_kv_a_norm, mla_w_ukv, mla_qn_nope, mla_qn_rope, mla_kn_nope, mla_kn_rope, mix_w_out, s5_lam_re, s5_lam_im, s5_log_dt, s5_b_re, s5_b_im, s5_c_re, s5_c_im, s5_d, s5_w_glu_a, s5_w_glu_b, loss_target, m_norm_mix, m_norm_xa, m_norm_mem, m_norm_ffn, m_xa_wq, m_xa_wk, m_xa_wv, m_xa_wo, m_xa_q_norm, m_xa_k_norm, m_ffn_w_up, m_ffn_conv_w, m_ffn_conv_b, m_ffn_w_down, m_hg_lb_logits, m_mix_w_in, m_hg_out_norm, m_mla_q_a_norm, m_mla_w_uq, m_mla_kv_a_norm, m_mla_w_ukv, m_mla_qn_nope, m_mla_qn_rope, m_mla_kn_nope, m_mla_kn_rope, m_mix_w_out, m_s5_lam_re, m_s5_lam_im, m_s5_log_dt, m_s5_b_re, m_s5_b_im, m_s5_c_re, m_s5_c_im, m_s5_d, m_s5_w_glu_a, m_s5_w_glu_b, v_norm_mix, v_norm_xa, v_norm_mem, v_norm_ffn, v_xa_wq, v_xa_wk, v_xa_wv, v_xa_wo, v_xa_q_norm, v_xa_k_norm, v_ffn_w_up, v_ffn_conv_w, v_ffn_conv_b, v_ffn_w_down, v_hg_lb_logits, v_mix_w_in, v_hg_out_norm, v_mla_q_a_norm, v_mla_w_uq, v_mla_kv_a_norm, v_mla_w_ukv, v_mla_qn_nope, v_mla_qn_rope, v_mla_kn_nope, v_mla_kn_rope, v_mix_w_out, v_s5_lam_re, v_s5_lam_im, v_s5_log_dt, v_s5_b_re, v_s5_b_im, v_s5_c_re, v_s5_c_im, v_s5_d, v_s5_w_glu_a, v_s5_w_glu_b):
    P = dict(locals())
    assert sorted(P) == sorted(INPUTS) and norm_mix.shape[0] == 2 and mix_w_in.shape[0] == 1
    x, mem, target = P['x'][0], P['mem'][0], P['loss_target'][0]
    L, D = x.shape
    tm = min(256, L)

    W = {n: P[n] for n in REPLICATED}
    W.update(_gather_weights(P))

    inv_freq = 1.0 / (ROPE_BASE ** (jnp.arange(0, MLA_ROPE, 2, dtype=F32) / MLA_ROPE))
    ang = P['positions'][0].astype(F32)[:, None] * inv_freq
    cos, sin, z = jnp.cos(ang), jnp.sin(ang), jnp.zeros_like(ang)
    cos_p = jnp.concatenate([cos, z, cos, z], axis=1)
    sin_p = jnp.concatenate([-sin, z, sin, z], axis=1)

    h, s_mix0 = _mixer0_fwd(x, W, cos_p, sin_p, tm)
    h, s_xa0 = _xattn_fwd(h, mem, W, 0, tm)
    h, s_ffn0 = _ffn_fwd(h, W, 0, tm)
    h, s_mix1 = _mixer1_fwd(h, W, tm)
    h, s_xa1 = _xattn_fwd(h, mem, W, 1, tm)
    h, s_ffn1 = _ffn_fwd(h, W, 1, tm)
    n = L // tm
    dh, parts = blocked_fwd(_loss_fn, [rows(h, tm), rows(target, tm)],
                            [_out((L, D), F32, tm), ((n * 8, 128), F32, (8, 128), lambda i: (i, 0))], n, "loss")
    loss = lax.psum(jnp.sum(parts), ("x", "y", "c"))

    layered = {}

    def collect(g, lyr):
        for k_, v_ in g.items():
            layered.setdefault(k_, {})[lyr] = v_

    results = {}

    def big_items(lyr, names):
        return [(n_, 0 if P[n_].shape[0] == 1 else lyr, layered[n_][lyr]) for n_ in names]

    per_layer = [n_ for n_ in BIG if P[n_].shape[0] == 2]
    second_mixer = ['s5_w_glu_a', 's5_w_glu_b']
    first_mixer = ['mix_w_in', 'mix_w_out']
    assert sorted(per_layer + second_mixer + first_mixer) == sorted(BIG)

    dh, g = _ffn_bwd(dh, s_ffn1, W, 1, tm)
    collect(g, 1)
    dh, g = _xattn_bwd(dh, s_xa1, mem, W, 1, tm)
    collect(g, 1)
    dh, g = _mixer1_bwd(dh, s_mix1, W, tm)
    collect(g, 1)
    late = _reduce_batch(big_items(1, per_layer + second_mixer), None, P, results, "late_layer", {'swap': 2, 'a2a': 3, 'join': 4})
    next(late)
    dh, g = _ffn_bwd(dh, s_ffn0, W, 0, tm)
    collect(g, 0)
    dh = late.send(dh)
    dh, g = _xattn_bwd(dh, s_xa0, mem, W, 0, tm)
    collect(g, 0)
    mid = _reduce_batch(big_items(0, per_layer), None, P, results, "first_layer", {'swap': 5, 'a2a': 6, 'join': 7})
    next(mid)
    dx, g = _mixer0_bwd(dh, s_mix0, W, cos_p, sin_p, tm, part_way=mid.send)
    collect(g, 0)

    GS = {}
    for name in SMALL:
        by_layer = [layered[name][lyr] for lyr in sorted(layered[name])]
        full_shape = W[name].shape
        GS[name] = (by_layer[0].reshape(full_shape) if len(by_layer) == 1
                    else jnp.stack([g_.reshape(full_shape[1:]) for g_ in by_layer]))
    small = _pack([GS[n_] for n_ in SMALL], 2 * N_CHIPS * ROW_MULT).reshape(N_CHIPS, -1, PACK_W)
    dx = late.send(dx)
    dx = mid.send(dx)
    last = _reduce_batch(big_items(0, first_mixer), small, P, results, "first_mixer", {'swap': 8, 'a2a': 9, 'join': 10})
    next(last)
    late.send(None)
    last_pair_sum = last.send(None)
    mid.send(last_pair_sum)
    mid_name = per_layer[-1]
    results[mid_name] = list(last.send(list(results[mid_name])))
    last.send(None)
    outs = list(_update_small(results['small_sum'], GS, P))
    for k_ in range(4):
        outs[k_].update({n_: results[n_][k_].reshape(P[n_].shape) for n_ in BIG})
    return (loss, dx[None], *[d[n_] for d in outs for n_ in WEIGHTS])
```

```python
import functools
import math

import jax
import jax.numpy as jnp
import numpy as np
from jax import lax
from jax.experimental import pallas as pl
from jax.experimental.pallas import tpu as pltpu
from jax.experimental.pallas import tpu_sc as plsc

F32 = jnp.float32
BF16 = jnp.bfloat16
MXU_DTYPE = BF16
HI = lax.Precision.HIGHEST
V7X_VMEM_LIMIT_BYTES = 56 * 1024 * 1024
EPS = 1e-6
MESH = pl.DeviceIdType.MESH

HG_HEADS, HG_DIM = 4, 128
HG_WIDTH = HG_HEADS * HG_DIM
HG_SUB = 32
HG_BLOCK = 64
MLA_HEADS, MLA_Q_RANK, MLA_KV_RANK = 4, 256, 128
MLA_NOPE, MLA_ROPE, MLA_V = 128, 64, 128
MLA_QK = MLA_NOPE + MLA_ROPE
MLA_DK = 256
ROPE_BASE = 10000.0
IN_WIDTH = 4 * HG_WIDTH + MLA_Q_RANK + MLA_KV_RANK + MLA_ROPE
IN_PAD = 4 * HG_WIDTH + MLA_Q_RANK + MLA_KV_RANK + 128
S5_GROUP, S5_STATE = 16, 64
S5_GB = 8
DT_MIN, DT_MAX = 1e-3, 1e-1
XA_HEADS = 4
CONV_W = 3
ADAM_LR, ADAM_B1, ADAM_B2, ADAM_EPS, ADAM_WD, ADAM_STEP = 0.001, 0.9, 0.999, 1e-08, 0.01, 10


def _cparams(sem):
    return pltpu.CompilerParams(dimension_semantics=sem, vmem_limit_bytes=V7X_VMEM_LIMIT_BYTES)


class Opd:
    def __init__(self, arr, block, imap, grad=None, gshape=None, gimap=None):
        self.arr, self.block, self.imap, self.grad = arr, block, imap, grad
        self.gshape = arr.shape if gshape is None else gshape
        self.gimap = imap if gimap is None else gimap

    def spec(self):
        return pl.BlockSpec(self.block, self.imap)

    def gspec(self):
        return pl.BlockSpec(self.block, self.gimap)


def rows(arr, tm, grad=None, col=0, width=None):
    width = arr.shape[1] if width is None else width
    return Opd(arr, (tm, width), lambda i, c=col: (i, c), grad, (arr.shape[0], width), lambda i: (i, 0))


def cols(arr, tn, grad=None):
    return Opd(arr, (arr.shape[0], tn), lambda j: (0, j), grad)


def full(arr, grad=None):
    return Opd(arr, arr.shape, lambda i: (0, 0), grad)


def _load(ref):
    v = ref[...]
    return v.astype(F32) if jnp.issubdtype(v.dtype, jnp.floating) else v


def blocked_fwd(f, opds, outs, n, name):
    n_in = len(opds)

    def body(*refs):
        ys = f(*[_load(r) for r in refs[:n_in]])
        for r, y in zip(refs[n_in:], ys):
            r[...] = y.astype(r.dtype)

    res = pl.pallas_call(
        body, name=name, grid=(n,),
        in_specs=[o.spec() for o in opds],
        out_specs=[pl.BlockSpec(b, m) for (_, _, b, m) in outs],
        out_shape=[jax.ShapeDtypeStruct(s, d) for (s, d, _, _) in outs],
        compiler_params=_cparams(("parallel",)),
    )(*[o.arr for o in opds])
    return res


def blocked_bwd(f, opds, dys, n, name, plus=None):
    n_in, n_dy = len(opds), len(dys)
    diff = [i for i, o in enumerate(opds) if o.grad]
    extra = [] if plus is None else [plus]

    def body(*refs):
        vals = [_load(r) for r in refs[:n_in]]

        def fd(*dv):
            allv = list(vals)
            for i, v in zip(diff, dv):
                allv[i] = v
            return tuple(f(*allv))

        ys, vjp = jax.vjp(fd, *[vals[i] for i in diff])
        cts = tuple(_load(r).astype(y.dtype) for r, y in zip(refs[n_in:n_in + n_dy], ys))
        gs = list(vjp(cts))
        if extra:
            gs[0] = gs[0] + _load(refs[n_in + n_dy])
        for r, g, i in zip(refs[n_in + n_dy + len(extra):], gs, diff):
            if opds[i].grad == 'acc':
                @pl.when(pl.program_id(0) == 0)
                def _(r=r):
                    r[...] = jnp.zeros(r.shape, r.dtype)
                r[...] += g.astype(r.dtype)
            else:
                r[...] = g.astype(r.dtype)

    any_acc = any(opds[i].grad == 'acc' for i in diff)
    res = pl.pallas_call(
        body, name=name, grid=(n,),
        in_specs=[o.spec() for o in opds + dys + extra],
        out_specs=[opds[i].gspec() for i in diff],
        out_shape=[jax.ShapeDtypeStruct(opds[i].gshape, F32) for i in diff],
        compiler_params=_cparams(("arbitrary" if any_acc else "parallel",)),
    )(*[o.arr for o in opds + dys + extra])
    return res


def _tile(dim, want):
    for t in range(want - want % 16, 0, -16):
        if dim % t == 0:
            return t
    assert dim <= want, (dim, want)
    return dim


MATMUL_VMEM_BUDGET = 40 * 1024 * 1024
MATMUL_ROWS = 512


def _widest(N, fits):
    for t in range(N - N % 128, 0, -128):
        if N % t == 0 and fits(t):
            return t
    return N


def matmul(a, b, mode="nn", out_dtype=F32, add=None, name="matmul", into=None, col_blocks=None):
    sa, sb, so = a.dtype.itemsize, b.dtype.itemsize, jnp.dtype(out_dtype).itemsize
    has_add = add is not None
    if mode == "tn":
        (K, M), (K2, N) = a.shape, b.shape
        assert K == K2 and not has_add and out_dtype == F32, (a.shape, b.shape)
        tk = _tile(K, MATMUL_ROWS)
        tn = _widest(N, lambda t: 2 * (tk * M * sa + tk * t * sb + M * t * 4) <= MATMUL_VMEM_BUDGET)
        extra, alias = [], {}
        if into is not None:
            buf, lead = into[0], tuple(into[1:])
            if col_blocks is not None:
                assert N % col_blocks == 0 and (N // col_blocks) % 128 == 0 and tn >= N // col_blocks, (N, col_blocks, tn)
                tn = N // col_blocks
                assert buf.shape[len(lead):] == (M, tn), (buf.shape, lead, M, tn)
                out_spec = pl.BlockSpec((None,) * len(lead) + (M, tn), lambda j, k: lead[:-1] + (lead[-1] + j, 0, 0))
            else:
                assert buf.shape[len(lead):] == (M, N), (buf.shape, lead, M, N)
                out_spec = pl.BlockSpec((None,) * len(lead) + (M, tn), lambda j, k: lead + (0, j))
            out_shape = jax.ShapeDtypeStruct(buf.shape, F32)
            extra, alias = [buf], {2: 0}
        else:
            assert col_blocks is None
            out_spec = pl.BlockSpec((M, tn), lambda j, k: (0, j))
            out_shape = jax.ShapeDtypeStruct((M, N), F32)

        def body(a_ref, b_ref, *rest):
            o_ref = rest[-1]
            r = lax.dot_general(a_ref[...].astype(MXU_DTYPE), b_ref[...].astype(MXU_DTYPE), ((_TN), ((), ())),
                                preferred_element_type=F32)

            @pl.when(pl.program_id(1) == 0)
            def _():
                o_ref[...] = r

            @pl.when(pl.program_id(1) > 0)
            def _():
                o_ref[...] += r

        return pl.pallas_call(
            body, name=name, grid=(N // tn, K // tk),
            in_specs=[pl.BlockSpec((tk, M), lambda j, k: (k, 0)), pl.BlockSpec((tk, tn), lambda j, k: (k, j))]
            + [pl.BlockSpec(memory_space=pl.ANY)] * len(extra),
            out_specs=out_spec, out_shape=out_shape, input_output_aliases=alias,
            compiler_params=_cparams(("parallel", "arbitrary")),
        )(a, b, *extra)

    (M, K) = a.shape
    N = b.shape[1] if mode == "nn" else b.shape[0]
    assert K == (b.shape[0] if mode == "nn" else b.shape[1]), (a.shape, b.shape, mode)
    tm = _tile(M, MATMUL_ROWS)
    tn = _widest(N, lambda t: 2 * (tm * K * sa + K * t * sb + tm * t * (so + 4 * has_add)) <= MATMUL_VMEM_BUDGET)
    dims = ((_NN if mode == "nn" else _NT), ((), ()))

    def body(*refs):
        r = lax.dot_general(refs[0][...].astype(MXU_DTYPE), refs[1][...].astype(MXU_DTYPE), dims, preferred_element_type=F32)
        if has_add:
            r = r + refs[2][...].astype(F32)
        refs[-1][...] = r.astype(refs[-1].dtype)

    b_spec = pl.BlockSpec((K, tn), lambda j, i: (0, j)) if mode == "nn" else pl.BlockSpec((tn, K), lambda j, i: (j, 0))
    in_specs = [pl.BlockSpec((tm, K), lambda j, i: (i, 0)), b_spec]
    args = [a, b]
    if has_add:
        in_specs.append(pl.BlockSpec((tm, tn), lambda j, i: (i, j)))
        args.append(add)
    return pl.pallas_call(
        body, name=name, grid=(N // tn, M // tm),
        in_specs=in_specs,
        out_specs=pl.BlockSpec((tm, tn), lambda j, i: (i, j)),
        out_shape=jax.ShapeDtypeStruct((M, N), out_dtype),
        compiler_params=_cparams(("parallel", "parallel")),
    )(*args)


def _dot(a, b, dims, precision=None):
    if precision is None:
        a, b = a.astype(MXU_DTYPE), b.astype(MXU_DTYPE)
    return lax.dot_general(a, b, (dims, ((), ())), precision=precision, preferred_element_type=F32)


_NN = ((1,), (0,))
_NT = ((1,), (1,))
_TN = ((0,), (0,))


def _rms(x, gain):
    return x * lax.rsqrt(jnp.mean(x * x, axis=-1, keepdims=True) + EPS) * gain


def _hg_block(st_t, q, fl, iv, g, lb, gain):
    row = lax.broadcasted_iota(jnp.int32, (HG_SUB, HG_SUB), 0)
    col = lax.broadcasted_iota(jnp.int32, (HG_SUB, HG_SUB), 1)
    tri = (row >= col).astype(F32)
    heads = [slice(h * HG_DIM, (h + 1) * HG_DIM) for h in range(HG_HEADS)]
    fg = lb + (1.0 - lb) * jax.nn.sigmoid(fl)
    lf, kk, qf = jnp.log(fg), 1.0 - fg, jax.nn.silu(q)
    sts = [st_t[sl, :] for sl in heads]
    parts = [[] for _ in heads]
    for s in range(q.shape[0] // HG_SUB):
        r = slice(s * HG_SUB, (s + 1) * HG_SUB)
        b = lf[r]
        d = 1
        while d < HG_SUB:
            b = b + _shift_rows(b, d)
            d *= 2
        b_mid = jnp.sum(lf[r][:HG_SUB // 2], axis=0, keepdims=True)
        b_end = jnp.sum(lf[r], axis=0, keepdims=True)
        q_in, k_in = qf[r] * jnp.exp(b - b_mid), kk[r] * jnp.exp(b_mid - b)
        q_st, k_st, decay = qf[r] * jnp.exp(b), kk[r] * jnp.exp(b_end - b), jnp.exp(b_end)
        for h, sl in enumerate(heads):
            sc = _dot(q_in[:, sl], k_in[:, sl], _NT) * tri
            parts[h].append(_dot(sc, iv[r, sl], _NN) + _dot(q_st[:, sl], sts[h], _NT))
            sts[h] = sts[h] * decay[:, sl] + _dot(iv[r, sl], k_st[:, sl], _TN)
    outs = [_rms(jnp.concatenate(parts[h], axis=0), gain[:, sl]) * jax.nn.silu(g[:, sl]) for h, sl in enumerate(heads)]
    return jnp.concatenate(sts, axis=0), jnp.concatenate(outs, axis=1)


def _hg_specs(proj, nb):
    return [pl.BlockSpec((HG_BLOCK, HG_WIDTH), lambda i, c=c, f=nb: (f(i), c)) for c in range(4)]


def hgrn2_fwd(proj, lb, gain):
    L = proj.shape[0]
    n = L // HG_BLOCK

    def body(q, fl, iv, g, lb_r, gain_r, o_ref, st_ref, st):
        @pl.when(pl.program_id(0) == 0)
        def _():
            st[...] = jnp.zeros(st.shape, F32)

        st_ref[0] = st[...]
        new, o = _hg_block(st[...], q[...], fl[...], iv[...], g[...], lb_r[...], gain_r[...])
        st[...] = new
        o_ref[...] = o.astype(o_ref.dtype)

    pspec = pl.BlockSpec((1, HG_WIDTH), lambda i: (0, 0))
    return pl.pallas_call(
        body, name="hgrn2_fwd", grid=(n,),
        in_specs=_hg_specs(proj, lambda i: i) + [pspec, pspec],
        out_specs=[pl.BlockSpec((HG_BLOCK, HG_WIDTH), lambda i: (i, 0)),
                   pl.BlockSpec((1, HG_WIDTH, HG_DIM), lambda i: (i, 0, 0))],
        out_shape=[jax.ShapeDtypeStruct((L, HG_WIDTH), MXU_DTYPE),
                   jax.ShapeDtypeStruct((n, HG_WIDTH, HG_DIM), F32)],
        scratch_shapes=[pltpu.VMEM((HG_WIDTH, HG_DIM), F32)],
        compiler_params=_cparams(("arbitrary",)),
    )(proj, proj, proj, proj, lb, gain)


def hgrn2_bwd(proj, lb, gain, states, do):
    L = proj.shape[0]
    n = L // HG_BLOCK

    def body(q, fl, iv, g, lb_r, gain_r, st_r, do_r, dproj, dlb, dgain, dst):
        @pl.when(pl.program_id(0) == 0)
        def _():
            dst[...] = jnp.zeros(dst.shape, F32)
            dlb[...] = jnp.zeros(dlb.shape, F32)
            dgain[...] = jnp.zeros(dgain.shape, F32)

        _, vjp = jax.vjp(_hg_block, st_r[0], q[...], fl[...], iv[...], g[...], lb_r[...], gain_r[...])
        d_st, dq, dfl, div, dg, d_lb, d_gain = vjp((dst[...], do_r[...].astype(F32)))
        dst[...] = d_st
        dproj[:, 0 * HG_WIDTH:1 * HG_WIDTH] = dq
        dproj[:, 1 * HG_WIDTH:2 * HG_WIDTH] = dfl
        dproj[:, 2 * HG_WIDTH:3 * HG_WIDTH] = div
        dproj[:, 3 * HG_WIDTH:4 * HG_WIDTH] = dg
        dlb[...] += d_lb
        dgain[...] += d_gain

    rev = lambda i: n - 1 - i
    pspec = pl.BlockSpec((1, HG_WIDTH), lambda i: (0, 0))
    return pl.pallas_call(
        body, name="hgrn2_bwd", grid=(n,),
        in_specs=_hg_specs(proj, rev) + [pspec, pspec,
                                         pl.BlockSpec((1, HG_WIDTH, HG_DIM), lambda i: (rev(i), 0, 0)),
                                         pl.BlockSpec((HG_BLOCK, HG_WIDTH), lambda i: (rev(i), 0))],
        out_specs=[pl.BlockSpec((HG_BLOCK, 4 * HG_WIDTH), lambda i: (rev(i), 0)), pspec, pspec],
        out_shape=[jax.ShapeDtypeStruct((L, 4 * HG_WIDTH), F32),
                   jax.ShapeDtypeStruct((1, HG_WIDTH), F32), jax.ShapeDtypeStruct((1, HG_WIDTH), F32)],
        scratch_shapes=[pltpu.VMEM((HG_WIDTH, HG_DIM), F32)],
        compiler_params=_cparams(("arbitrary",)),
    )(proj, proj, proj, proj, lb, gain, states, do)


def _rope_rms(x, gain_p, cos_p, sin_p):
    n = x * lax.rsqrt(jnp.sum(x * x, axis=-1, keepdims=True) * (1.0 / MLA_ROPE) + EPS) * gain_p
    r = lax.broadcasted_iota(jnp.int32, (128, 128), 0)
    c = lax.broadcasted_iota(jnp.int32, (128, 128), 1)
    swap = (r == (c + 64) % 128).astype(F32)
    return n * cos_p + _dot(n, swap, _NN, HI) * sin_p


MLA_IN = MLA_Q_RANK + MLA_KV_RANK + 128


def _mla_prep(x, cos_p, sin_p, q_a, w_uq, kv_a, w_ukv, qn_nope, qn_rope, kn_nope, kn_rope):
    c_q, c_kv, kpe = x[:, :MLA_Q_RANK], x[:, MLA_Q_RANK:MLA_Q_RANK + MLA_KV_RANK], x[:, MLA_Q_RANK + MLA_KV_RANK:]
    q = _dot(_rms(c_q, q_a), w_uq, _NN)
    kv = _dot(_rms(c_kv, kv_a), w_ukv, _NN)
    k_pe = _rope_rms(kpe, kn_rope, cos_p, sin_p)
    qs, ks = [], []
    for h in range(MLA_HEADS):
        qs.append(_rms(q[:, h * MLA_DK:h * MLA_DK + MLA_NOPE], qn_nope))
        qs.append(_rope_rms(q[:, h * MLA_DK + MLA_NOPE:(h + 1) * MLA_DK], qn_rope, cos_p, sin_p))
        ks.append(_rms(kv[:, h * MLA_NOPE:(h + 1) * MLA_NOPE], kn_nope))
        ks.append(k_pe)
    return jnp.concatenate(qs, axis=1), jnp.concatenate(ks, axis=1), kv[:, MLA_HEADS * MLA_NOPE:]


def _mla_prep_opds(proj, cos_p, sin_p, params, tm, grads):
    g = (lambda k: k) if grads else (lambda k: None)
    assert (4 * HG_WIDTH) % MLA_IN == 0
    return ([rows(proj, tm, g('blk'), col=4 * HG_WIDTH // MLA_IN, width=MLA_IN), rows(cos_p, tm), rows(sin_p, tm)]
            + [full(p, g('acc')) for p in params])


def mla_prep_fwd(proj, cos_p, sin_p, params, tm):
    L = proj.shape[0]
    W = MLA_HEADS * MLA_DK
    rb = lambda w: (tm, w)
    outs = [((L, W), MXU_DTYPE, rb(W), lambda i: (i, 0)), ((L, W), MXU_DTYPE, rb(W), lambda i: (i, 0)),
            ((L, MLA_HEADS * MLA_V), MXU_DTYPE, rb(MLA_HEADS * MLA_V), lambda i: (i, 0))]
    return blocked_fwd(_mla_prep, _mla_prep_opds(proj, cos_p, sin_p, params, tm, False), outs, L // tm, "mla_prep_fwd")


def mla_prep_bwd(proj, cos_p, sin_p, params, dq, dk, dv, tm):
    L = proj.shape[0]
    return blocked_bwd(_mla_prep, _mla_prep_opds(proj, cos_p, sin_p, params, tm, True),
                       [rows(dq, tm), rows(dk, tm), rows(dv, tm)], L // tm, "mla_prep_bwd")


def _scores(q, k, scale, shift=None):
    s = _dot(q, k, _NT) * scale
    if shift is None:
        return s
    row = lax.broadcasted_iota(jnp.int32, s.shape, 0)
    col = lax.broadcasted_iota(jnp.int32, s.shape, 1)
    return jnp.where(col <= row + shift, s, -jnp.inf)


ATTN_ROWS = 512
ATTN_WIDE = 2


def attn_fwd(q, k, v, scale, t):
    L = q.shape[0]
    tq = ATTN_WIDE * t

    def body(q_ref, k_ref, v_ref, o_ref, lse_ref):
        i = pl.program_id(1)
        qb = q_ref[...]

        def step(j, carry, shift=None):
            m, l, acc = carry
            kj = k_ref[pl.ds(pl.multiple_of(j * t, t), t), :]
            vj = v_ref[pl.ds(pl.multiple_of(j * t, t), t), :]
            s = _scores(qb, kj, scale, shift)
            m_new = jnp.maximum(m, jnp.max(s, axis=-1, keepdims=True))
            p = jnp.exp(s - m_new)
            alpha = jnp.exp(m - m_new)
            return m_new, alpha * l + jnp.sum(p, axis=-1, keepdims=True), alpha * acc + _dot(p, vj, _NN)

        carry = (jnp.full((tq, 1), -jnp.inf, F32), jnp.zeros((tq, 1), F32), jnp.zeros((tq, MLA_V), F32))
        carry = lax.fori_loop(0, ATTN_WIDE * i, step, carry)
        for d in range(ATTN_WIDE):
            carry = step(ATTN_WIDE * i + d, carry, -d * t)
        m, l, acc = carry
        o_ref[...] = acc / l
        lse_ref[...] = jnp.broadcast_to(m + jnp.log(l), lse_ref.shape)

    hspec = lambda rows_, w: pl.BlockSpec((rows_, w), lambda h, i: (0, h))
    bspec = lambda w: pl.BlockSpec((tq, w), lambda h, i: (i, h))
    return pl.pallas_call(
        body, name="attn_fwd", grid=(MLA_HEADS, L // tq),
        in_specs=[bspec(MLA_DK), hspec(L, MLA_DK), hspec(L, MLA_V)],
        out_specs=[bspec(MLA_V), bspec(MLA_V)],
        out_shape=[jax.ShapeDtypeStruct((L, MLA_HEADS * MLA_V), F32)] * 2,
        compiler_params=_cparams(("parallel", "parallel")),
    )(q, k, v)


def attn_bwd_dq(q, k, v, o, lse, do, scale, t):
    L = q.shape[0]
    tq = ATTN_WIDE * t

    def body(q_ref, k_ref, v_ref, o_ref, lse_ref, do_ref, dq_ref):
        i = pl.program_id(1)
        qb, dob = q_ref[...], do_ref[...]
        delta = jnp.sum(dob * o_ref[...], axis=-1, keepdims=True)
        lse_c = jnp.max(lse_ref[...], axis=-1, keepdims=True)

        def step(j, dq, shift=None):
            kj = k_ref[pl.ds(pl.multiple_of(j * t, t), t), :]
            vj = v_ref[pl.ds(pl.multiple_of(j * t, t), t), :]
            p = jnp.exp(_scores(qb, kj, scale, shift) - lse_c)
            ds = p * (_dot(dob, vj, _NT) - delta) * scale
            return dq + _dot(ds, kj, _NN)

        dq = lax.fori_loop(0, ATTN_WIDE * i, step, jnp.zeros((tq, MLA_DK), F32))
        for d in range(ATTN_WIDE):
            dq = step(ATTN_WIDE * i + d, dq, -d * t)
        dq_ref[...] = dq

    hspec = lambda w: pl.BlockSpec((L, w), lambda h, i: (0, h))
    bspec = lambda w: pl.BlockSpec((tq, w), lambda h, i: (i, h))
    return pl.pallas_call(
        body, name="attn_bwd_dq", grid=(MLA_HEADS, L // tq),
        in_specs=[bspec(MLA_DK), hspec(MLA_DK), hspec(MLA_V), bspec(MLA_V), bspec(MLA_V), bspec(MLA_V)],
        out_specs=bspec(MLA_DK),
        out_shape=jax.ShapeDtypeStruct((L, MLA_HEADS * MLA_DK), F32),
        compiler_params=_cparams(("parallel", "parallel")),
    )(q, k, v, o, lse, do)


def attn_bwd_dkv(q, k, v, o, lse, do, scale, t):
    L = q.shape[0]
    tk = ATTN_WIDE * t

    def body(q_ref, k_ref, v_ref, o_ref, lse_ref, do_ref, dk_ref, dv_ref):
        j = pl.program_id(1)
        kb, vb = k_ref[...], v_ref[...]

        def step(i, carry, shift=None):
            dk, dv = carry
            r = pl.ds(pl.multiple_of(i * t, t), t)
            qi, doi = q_ref[r, :], do_ref[r, :]
            delta = jnp.sum(doi * o_ref[r, :], axis=-1, keepdims=True)
            lse_c = jnp.max(lse_ref[r, :], axis=-1, keepdims=True)
            p = jnp.exp(_scores(qi, kb, scale, shift) - lse_c)
            ds = p * (_dot(doi, vb, _NT) - delta) * scale
            return dk + _dot(ds, qi, _TN), dv + _dot(p, doi, _TN)

        carry = (jnp.zeros((tk, MLA_DK), F32), jnp.zeros((tk, MLA_V), F32))
        for d in range(ATTN_WIDE):
            carry = step(ATTN_WIDE * j + d, carry, d * t)
        dk, dv = lax.fori_loop(ATTN_WIDE * (j + 1), L // t, step, carry)
        dk_ref[...] = dk
        dv_ref[...] = dv

    hspec = lambda w: pl.BlockSpec((L, w), lambda h, j: (0, h))
    bspec = lambda w: pl.BlockSpec((tk, w), lambda h, j: (j, h))
    return pl.pallas_call(
        body, name="attn_bwd_dkv", grid=(MLA_HEADS, L // tk),
        in_specs=[hspec(MLA_DK), bspec(MLA_DK), bspec(MLA_V), hspec(MLA_V), hspec(MLA_V), hspec(MLA_V)],
        out_specs=[bspec(MLA_DK), bspec(MLA_V)],
        out_shape=[jax.ShapeDtypeStruct((L, MLA_HEADS * MLA_DK), F32), jax.ShapeDtypeStruct((L, MLA_HEADS * MLA_V), F32)],
        compiler_params=_cparams(("parallel", "parallel")),
    )(q, k, v, o, lse, do)


S5_LANES = S5_GB * S5_STATE
S5_BWD_CHUNK = 1024


def _cmul(ar, ai, br, bi):
    return ar * br - ai * bi, ar * bi + ai * br


def _a_powers(ar, ai, reverse):
    a2 = _cmul(ar, ai, ar, ai)
    a4 = _cmul(*a2, *a2)
    row = lax.broadcasted_iota(jnp.int32, (8, ar.shape[1]), 0)
    e = (8 - row) if reverse else (row + 1)
    tr, ti = jnp.ones((8, ar.shape[1]), F32), jnp.zeros((8, ar.shape[1]), F32)
    for bit, (pr, pi) in ((1, (ar, ai)), (2, a2), (4, a4), (8, _cmul(*a4, *a4))):
        nr, ni = _cmul(tr, ti, pr, pi)
        sel = (e & bit) != 0
        tr, ti = jnp.where(sel, nr, tr), jnp.where(sel, ni, ti)
    pows = []
    for d, (pr, pi) in zip((1, 2, 4), ((ar, ai), a2, a4)):
        keep = (row < 8 - d) if reverse else (row >= d)
        pows.append((jnp.where(keep, pr, 0.0), jnp.where(keep, pi, 0.0)))
    return pows, (tr, ti)


def _scan8(xr, xi, pows, table, cr, ci, reverse):
    for d, (pr, pi) in zip((1, 2, 4), pows):
        shift = 8 - d if reverse else d
        mr, mi = _cmul(pr, pi, pltpu.roll(xr, shift, 0), pltpu.roll(xi, shift, 0))
        xr, xi = xr + mr, xi + mi
    mr, mi = _cmul(table[0], table[1], cr, ci)
    return xr + mr, xi + mi


def _row_of(x, r):
    row = lax.broadcasted_iota(jnp.int32, x.shape, 0)
    return jnp.sum(jnp.where(row == r, x, 0.0), axis=0, keepdims=True)


def _s5_scan_fwd(h_re, h_im, ar, ai, L):
    pows, table = _a_powers(ar, ai, False)

    def step(i, carry):
        r = pl.ds(pl.multiple_of(i * 8, 8), 8)
        xr, xi = _scan8(h_re[r, :], h_im[r, :], pows, table, carry[0], carry[1], False)
        h_re[r, :] = xr
        h_im[r, :] = xi
        return xr[7:8, :], xi[7:8, :]

    z = jnp.zeros((1, ar.shape[1]), F32)
    lax.fori_loop(0, L // 8, step, (z, z))


def _s5_specs(L):
    return [pl.BlockSpec((L, 128), lambda g: (0, g)),
            pl.BlockSpec((1, 128, S5_LANES), lambda g: (g, 0, 0)), pl.BlockSpec((1, 128, S5_LANES), lambda g: (g, 0, 0)),
            pl.BlockSpec((1, 1, S5_LANES), lambda g: (g, 0, 0)), pl.BlockSpec((1, 1, S5_LANES), lambda g: (g, 0, 0)),
            pl.BlockSpec((1, S5_LANES, 128), lambda g: (g, 0, 0)), pl.BlockSpec((1, S5_LANES, 128), lambda g: (g, 0, 0))]


def s5_fwd(u, w_re, w_im, a_re, a_im, c_re, c_im):
    L, D = u.shape

    def body(u_ref, wr, wi, ar, ai, cr, ci, y_ref, h_re, h_im):
        ub = u_ref[...]
        h_re[...] = _dot(ub, wr[0], _NN)
        h_im[...] = _dot(ub, wi[0], _NN)
        _s5_scan_fwd(h_re, h_im, ar[0], ai[0], L)
        y_ref[...] = _dot(h_re[...], cr[0], _NN) - _dot(h_im[...], ci[0], _NN)

    return pl.pallas_call(
        body, name="s5_fwd", grid=(D // 128,),
        in_specs=_s5_specs(L), out_specs=pl.BlockSpec((L, 128), lambda g: (0, g)),
        out_shape=jax.ShapeDtypeStruct((L, D), F32),
        scratch_shapes=[pltpu.VMEM((L, S5_LANES), F32), pltpu.VMEM((L, S5_LANES), F32)],
        compiler_params=_cparams(("parallel",)),
    )(u, w_re, w_im, a_re, a_im, c_re, c_im)


def s5_bwd(u, w_re, w_im, a_re, a_im, c_re, c_im, dy, du_plus, tc):
    L, D = u.shape
    nch = L // tc

    def body(u_ref, wr, wi, ar_ref, ai_ref, cr, ci, dy_ref, plus_ref, du_ref, dwr, dwi, dar, dai, dcr, dci, h_re, h_im,
             g_re, g_im):
        ar, ai = ar_ref[0], ai_ref[0]
        ub = u_ref[...]
        h_re[...] = _dot(ub, wr[0], _NN)
        h_im[...] = _dot(ub, wi[0], _NN)
        _s5_scan_fwd(h_re, h_im, ar, ai, L)
        dyb = dy_ref[...]
        dcr[0] = _dot(h_re[...], dyb, _TN)
        dci[0] = -_dot(h_im[...], dyb, _TN)
        pows, table = _a_powers(ar, -ai, True)
        dwr[0] = jnp.zeros((128, S5_LANES), F32)
        dwi[0] = jnp.zeros((128, S5_LANES), F32)
        z1 = jnp.zeros((1, S5_LANES), F32)
        z8 = jnp.zeros((8, S5_LANES), F32)

        def chunk(cc, carry):
            c0 = pl.multiple_of((nch - 1 - cc) * tc, tc)
            rows_c = pl.ds(c0, tc)
            dyc = dy_ref[rows_c, :]
            g_re[...] = _dot(dyc, cr[0], _NT)
            g_im[...] = -_dot(dyc, ci[0], _NT)

            def step(ii, cy):
                gr_c, gi_c, acc_r, acc_i = cy
                i8 = pl.multiple_of((tc // 8 - 1 - ii) * 8, 8)
                rl = pl.ds(i8, 8)
                xr, xi = _scan8(g_re[rl, :], g_im[rl, :], pows, table, gr_c, gi_c, True)
                g_re[rl, :] = xr
                g_im[rl, :] = xi
                t0 = c0 + i8
                hb_r, hb_i = h_re[pl.ds(t0, 8), :], h_im[pl.ds(t0, 8), :]
                tp = pl.multiple_of(jnp.maximum(t0 - 8, 0), 8)
                first = (t0 > 0).astype(F32)
                pr = h_re[pl.ds(tp, 8), :][7:8, :] * first
                pi = h_im[pl.ds(tp, 8), :][7:8, :] * first
                row = lax.broadcasted_iota(jnp.int32, xr.shape, 0)
                hp_r = jnp.where(row == 0, pr, pltpu.roll(hb_r, 1, 0))
                hp_i = jnp.where(row == 0, pi, pltpu.roll(hb_i, 1, 0))
                return (xr[0:1, :], xi[0:1, :],
                        acc_r + xr * hp_r + xi * hp_i, acc_i + xi * hp_r - xr * hp_i)

            cy = lax.fori_loop(0, tc // 8, step, carry)
            uc = u_ref[rows_c, :]
            gr, gi = g_re[...], g_im[...]
            du_ref[rows_c, :] = _dot(gr, wr[0], _NT) + _dot(gi, wi[0], _NT) + plus_ref[rows_c, :]
            dwr[0] += _dot(uc, gr, _TN)
            dwi[0] += _dot(uc, gi, _TN)
            return cy

        _, _, acc_r, acc_i = lax.fori_loop(0, nch, chunk, (z1, z1, z8, z8))
        dar[0] = jnp.sum(acc_r, axis=0, keepdims=True)
        dai[0] = jnp.sum(acc_i, axis=0, keepdims=True)

    specs = _s5_specs(L)
    return pl.pallas_call(
        body, name="s5_bwd", grid=(D // 128,),
        in_specs=specs + [pl.BlockSpec((L, 128), lambda g: (0, g))] * 2,
        out_specs=[pl.BlockSpec((L, 128), lambda g: (0, g))] + specs[1:],
        out_shape=[jax.ShapeDtypeStruct((L, D), F32)] + [jax.ShapeDtypeStruct(x.shape, F32)
                                                        for x in (w_re, w_im, a_re, a_im, c_re, c_im)],
        scratch_shapes=[pltpu.VMEM((L, S5_LANES), F32), pltpu.VMEM((L, S5_LANES), F32),
                        pltpu.VMEM((tc, S5_LANES), F32), pltpu.VMEM((tc, S5_LANES), F32)],
        compiler_params=_cparams(("parallel",)),
    )(u, w_re, w_im, a_re, a_im, c_re, c_im, dy, du_plus)


def _s5_discretize(lr, li, ldt, br, bi):
    dt = jnp.exp(ldt)
    mag = jnp.exp(lr * dt)
    ar, ai = mag * jnp.cos(li * dt), mag * jnp.sin(li * dt)
    den = lr * lr + li * li
    zr = ((ar - 1.0) * lr + ai * li) / den
    zi = (ai * lr - (ar - 1.0) * li) / den
    p = lax.broadcasted_iota(jnp.int32, (S5_STATE, S5_STATE * S5_GROUP), 0)
    c = lax.broadcasted_iota(jnp.int32, (S5_STATE, S5_STATE * S5_GROUP), 1)
    rep = (c // S5_GROUP == p).astype(F32)
    zr, zi = _dot(zr, rep, _NN, HI), _dot(zi, rep, _NN, HI)
    return ar, ai, zr * br - zi * bi, zr * bi + zi * br


def _conv_shift(x, d):
    row = lax.broadcasted_iota(jnp.int32, x.shape, 0)
    return jnp.where(row >= d, pltpu.roll(x, d, 0), 0.0)


def _conv_unshift(x, d):
    n = x.shape[0]
    row = lax.broadcasted_iota(jnp.int32, x.shape, 0)
    return jnp.where(row < n - d, pltpu.roll(x, n - d, 0), 0.0)


@functools.partial(jax.custom_vjp, nondiff_argnums=(1,))
def _shift_rows(x, d):
    return _conv_shift(x, d)


_shift_rows.defvjp(lambda x, d: (_conv_shift(x, d), None), lambda d, _, g: (_conv_unshift(g, d),))


def _conv_gate(ug, uv, wg0, wg1, wg2, wv0, wv1, wv2, bg, bv):
    def conv(u, w0, w1, w2, b):
        return u * w2 + _shift_rows(u, 1) * w1 + _shift_rows(u, 2) * w0 + b
    return (jax.nn.silu(conv(ug, wg0, wg1, wg2, bg)) * conv(uv, wv0, wv1, wv2, bv),)


def _rms_fn(x, gain):
    return (_rms(x, gain),)


def _softmax_rows(s):
    e = jnp.exp(s - lax.stop_gradient(jnp.max(s, axis=-1, keepdims=True)))
    return e / jnp.sum(e, axis=-1, keepdims=True)


def _xa_core(qp, k, v, q_gain):
    dh = qp.shape[1] // XA_HEADS
    outs = []
    for h in range(XA_HEADS):
        sl = slice(h * dh, (h + 1) * dh)
        p = _softmax_rows(_dot(_rms(qp[:, sl], q_gain), k[:, sl], _NT) * (dh ** -0.5))
        outs.append(_dot(p, v[:, sl], _NN))
    return (jnp.concatenate(outs, axis=1),)


def _mem_kv(mem, mem_gain, wk, wv, k_gain):
    m = _rms(mem, mem_gain)
    kp = _dot(m, wk, _NN)
    dh = kp.shape[1] // XA_HEADS
    k = jnp.concatenate([_rms(kp[:, h * dh:(h + 1) * dh], k_gain) for h in range(XA_HEADS)], axis=1)
    return k, _dot(m, wv, _NN)


def _s5_post(y, u, d):
    return (jax.nn.gelu(y + d * u),)


def _glu(a, b):
    return (a * jax.nn.sigmoid(b),)


def _lb_first(logits):
    e = jnp.exp(logits - lax.stop_gradient(jnp.max(logits, axis=0, keepdims=True)))
    return (_row_of(e, 0) / jnp.sum(e, axis=0, keepdims=True),)


def _loss_fn(y, t):
    e = y - t
    part = 0.5 * jnp.sum(e * e) / y.shape[1]
    return e * (1.0 / y.shape[1]), jnp.full((8, 128), part / (8 * 128), F32)


def _out(shape, dtype, tm):
    return (shape, dtype, (tm, shape[1]), lambda i: (i, 0))


def rms_fwd(h, gain, tm, dtype):
    return blocked_fwd(_rms_fn, [rows(h, tm), full(gain)], [_out(h.shape, dtype, tm)], h.shape[0] // tm, "rms_fwd")[0]


def rms_bwd(h, gain, dy, tm, residual):
    return blocked_bwd(_rms_fn, [rows(h, tm, 'blk'), full(gain, 'acc')], [rows(dy, tm)], h.shape[0] // tm, "rms_bwd",
                       plus=rows(residual, tm))


def _adamw_math(w, g, m, v):
    m = ADAM_B1 * m + (1.0 - ADAM_B1) * g
    v = ADAM_B2 * v + (1.0 - ADAM_B2) * jnp.square(g)
    m_hat = m / (1.0 - ADAM_B1 ** ADAM_STEP)
    v_hat = v / (1.0 - ADAM_B2 ** ADAM_STEP)
    return -ADAM_LR * (m_hat / (jnp.sqrt(v_hat) + ADAM_EPS) + ADAM_WD * w), m, v


def adamw(w, g, m, v, name):
    R = w.shape[0]
    tm = _tile(R, 256)
    assert g.shape == w.shape == m.shape == v.shape, (name, w.shape, g.shape)

    def body(w_ref, g_ref, m_ref, v_ref, d_ref, nm_ref, nv_ref):
        d_ref[...], nm_ref[...], nv_ref[...] = _adamw_math(w_ref[...], g_ref[...], m_ref[...], v_ref[...])

    spec = pl.BlockSpec((tm, w.shape[1]), lambda i: (i, 0))
    return pl.pallas_call(
        body, name=name, grid=(R // tm,), in_specs=[spec] * 4, out_specs=[spec] * 3,
        out_shape=[jax.ShapeDtypeStruct(w.shape, F32)] * 3, compiler_params=_cparams(("parallel",)),
    )(w, g, m, v)


def _index_operand(i):
    return jnp.reshape(i, (1,)).astype(jnp.int32)


def adamw_layer(w, mine, other, core, m, v, layer, bufs, name):
    H, C = mine.shape
    tm = _tile(H, 256)
    nb = H // tm
    assert w.shape[1:] == (2 * H, C) and all(b.shape == w.shape for b in bufs), (name, w.shape, mine.shape)

    def body(c_ref, w_ref, mine_ref, other_ref, m_ref, v_ref, *rest):
        g_out, d_ref, nm_ref, nv_ref = rest[-4:]
        g_ = jnp.where(pl.program_id(0) // nb == c_ref[0], mine_ref[...], other_ref[...])
        g_out[...] = g_
        d_ref[...], nm_ref[...], nv_ref[...] = _adamw_math(w_ref[...], g_, m_ref[...], v_ref[...])

    lspec = pl.BlockSpec((None, tm, C), lambda i, c: (layer, i, 0))
    half = lambda sign: pl.BlockSpec(
        (tm, C), lambda i, c: (jnp.clip(i - (c[0] if sign else 1 - c[0]) * nb, 0, nb - 1), 0))
    any_spec = pl.BlockSpec(memory_space=pl.ANY)
    grid_spec = pltpu.PrefetchScalarGridSpec(
        num_scalar_prefetch=1, grid=(2 * nb,),
        in_specs=[lspec, half(True), half(False), lspec, lspec] + [any_spec] * 4, out_specs=[lspec] * 4)
    return pl.pallas_call(
        body, name=name, grid_spec=grid_spec, out_shape=[jax.ShapeDtypeStruct(w.shape, F32)] * 4,
        input_output_aliases={6: 0, 7: 1, 8: 2, 9: 3}, compiler_params=_cparams(("parallel",)),
    )(_index_operand(core), w, mine, other, m, v, *bufs)


def add_own_half(x, core, theirs, name, out_dtype):
    nq, _, H, C = x.shape
    tm = _tile(H, 256)

    def body(c_ref, x_ref, t_ref, o_ref):
        o_ref[...] = (x_ref[...] + t_ref[...]).astype(o_ref.dtype)

    spec = pl.BlockSpec((None, tm, C), lambda q, i, c: (q, i, 0))
    grid_spec = pltpu.PrefetchScalarGridSpec(
        num_scalar_prefetch=1, grid=(nq, H // tm),
        in_specs=[pl.BlockSpec((None, None, tm, C), lambda q, i, c: (q, c[0], i, 0)), spec], out_specs=spec)
    return pl.pallas_call(
        body, name=name, grid_spec=grid_spec, out_shape=jax.ShapeDtypeStruct((nq, H, C), out_dtype),
        compiler_params=_cparams(("parallel", "parallel")),
    )(_index_operand(core), x, theirs)


def add_chips(pair, chip, got, name):
    n, R, C = got.shape
    tm = _tile(R, 256)

    def body(q_ref, *refs):
        acc = refs[0][...].astype(F32)
        for r in refs[1:-1]:
            acc = acc + r[...].astype(F32)
        refs[-1][...] = acc

    grid_spec = pltpu.PrefetchScalarGridSpec(
        num_scalar_prefetch=1, grid=(R // tm,),
        in_specs=[pl.BlockSpec((None, tm, C), lambda i, q: (q[0], i, 0))]
        + [pl.BlockSpec((None, tm, C), lambda i, q, j=j: (j, i, 0)) for j in range(n)],
        out_specs=pl.BlockSpec((tm, C), lambda i, q: (i, 0)))
    return pl.pallas_call(
        body, name=name, grid_spec=grid_spec, out_shape=jax.ShapeDtypeStruct((R, C), F32),
        compiler_params=_cparams(("parallel",)),
    )(_index_operand(chip), pair, *([got] * n))


_HBM = pl.BlockSpec(memory_space=pltpu.HBM)
N_CHIPS = 4


def _my_place():
    return lax.axis_index("x"), lax.axis_index("y"), lax.axis_index("c")


def _comm_call(body, name, xs, out_shapes, n_remote, n_local, sequencer=None):
    sems = [pltpu.SemaphoreType.DMA((n_remote,)), pltpu.SemaphoreType.DMA((n_remote,)),
            pltpu.SemaphoreType.DMA((max(n_local, 1),))]
    if sequencer is None:
        return pl.pallas_call(
            body, name=name, in_specs=[_HBM] * len(xs), out_specs=[_HBM] * len(out_shapes), out_shape=out_shapes,
            scratch_shapes=sems, compiler_params=pltpu.CompilerParams(has_side_effects=True),
        )(*xs)
    peers_of, collective_id = sequencer
    hbm = pltpu.MemorySpace.HBM
    x_refs = [jax.new_ref(x, memory_space=hbm) for x in xs]
    o_refs = [jax.empty_ref(s, memory_space=hbm) for s in out_shapes]

    @pl.kernel(mesh=plsc.ScalarSubcoreMesh(axis_name="sequencer", num_cores=1), name=name, scratch_types=tuple(sems),
               compiler_params=pltpu.CompilerParams(collective_id=collective_id))
    def launch(send_sems, recv_sems, local_sems):
        peers = peers_of(*_my_place())
        barrier = pltpu.get_barrier_semaphore()
        for peer in peers:
            pl.semaphore_signal(barrier, inc=1, device_id=peer, device_id_type=MESH)
        pl.semaphore_wait(barrier, len(peers))
        body(*x_refs, *o_refs, send_sems, recv_sems, local_sems)

    launch()
    return [o[...] for o in o_refs]


def _sibling(mx, my, mc):
    return [(mx, my, 1 - mc)]


def _same_core_of_other_chips(mx, my, mc):
    return [(tx, ty, mc) for tx, ty in _other_chips(mx, my)]


def _run(copies):
    for cp in copies:
        cp.start()
    for cp in copies:
        cp.wait()


def _other_chips(mx, my):
    return [(mx ^ (j >> 1), my ^ (j & 1)) for j in (1, 2, 3)]


def device_gather(x, name):
    H, C = x.shape

    def body(x_ref, o_ref, send_sems, recv_sems, local_sems):
        mx, my, mc = _my_place()
        dst = o_ref.at[2 * mx + my, mc]
        copies = [pltpu.make_async_copy(x_ref, dst, local_sems.at[0])]
        others = [(mx, my, 1 - mc)] + [(tx, ty, c) for tx, ty in _other_chips(mx, my) for c in (mc, 1 - mc)]
        for j, peer in enumerate(others):
            copies.append(pltpu.make_async_remote_copy(src_ref=x_ref, dst_ref=dst, send_sem=send_sems.at[j],
                                                       recv_sem=recv_sems.at[j], device_id=peer, device_id_type=MESH))
        _run(copies)

    out = _comm_call(body, name, [x], [jax.ShapeDtypeStruct((N_CHIPS, 2, H, C), x.dtype)], 7, 1)[0]
    return out.reshape(N_CHIPS * 2 * H, C)


def gather_two_level(xs, name):
    n = len(xs)
    shapes = [jax.ShapeDtypeStruct((2, N_CHIPS) + x.shape[1:], x.dtype) for x in xs]

    def body(*refs):
        x_refs, o_refs = refs[:n], refs[n:2 * n]
        send_sems, recv_sems, local_sems = refs[2 * n:]
        mx, my, mc = _my_place()
        q = 2 * mx + my
        first, local, second = [], [], []
        for i, (x_ref, o_ref) in enumerate(zip(x_refs, o_refs)):
            local.append(pltpu.make_async_copy(x_ref.at[mc], o_ref.at[mc, q], local_sems.at[i]))
            for j, (tx, ty) in enumerate(_other_chips(mx, my)):
                first.append(pltpu.make_async_remote_copy(
                    src_ref=x_ref.at[mc], dst_ref=o_ref.at[mc, q], send_sem=send_sems.at[4 * i + j],
                    recv_sem=recv_sems.at[4 * i + j], device_id=(tx, ty, mc), device_id_type=MESH))
            second.append(pltpu.make_async_remote_copy(
                src_ref=o_ref.at[mc], dst_ref=o_ref.at[mc], send_sem=send_sems.at[4 * i + 3],
                recv_sem=recv_sems.at[4 * i + 3], device_id=(mx, my, 1 - mc), device_id_type=MESH))
        for cp in local + first:
            cp.start()
        for cp in local:
            cp.wait()
        for cp in first:
            cp.wait_recv()
        _run(second)
        for cp in first:
            cp.wait_send()

    return _comm_call(body, name, xs, shapes, 4 * n, n)


def gather_two_level_sequencer(xs, name, collective_id):
    n = len(xs)
    hbm = pltpu.MemorySpace.HBM
    x_refs = [jax.new_ref(x, memory_space=hbm) for x in xs]
    o_refs = [jax.empty_ref(jax.ShapeDtypeStruct((2, N_CHIPS) + x.shape[1:], x.dtype), memory_space=hbm) for x in xs]

    @pl.kernel(mesh=plsc.ScalarSubcoreMesh(axis_name="sequencer", num_cores=1), name=name,
               scratch_types=(pltpu.SemaphoreType.DMA((4 * n,)), pltpu.SemaphoreType.DMA((4 * n,)),
                              pltpu.SemaphoreType.DMA((n,))),
               compiler_params=pltpu.CompilerParams(collective_id=collective_id))
    def launch(send_sems, recv_sems, local_sems):
        mx, my, mc = _my_place()
        peers = [(tx, ty, mc) for tx, ty in _other_chips(mx, my)] + [(mx, my, 1 - mc)]
        barrier = pltpu.get_barrier_semaphore()
        for peer in peers:
            pl.semaphore_signal(barrier, inc=1, device_id=peer, device_id_type=MESH)
        pl.semaphore_wait(barrier, len(peers))
        q = 2 * mx + my
        first, local, second = [], [], []
        for i, (x_ref, o_ref) in enumerate(zip(x_refs, o_refs)):
            local.append(pltpu.make_async_copy(x_ref.at[mc], o_ref.at[mc, q], local_sems.at[i]))
            for j, peer in enumerate(peers[:3]):
                first.append(pltpu.make_async_remote_copy(
                    src_ref=x_ref.at[mc], dst_ref=o_ref.at[mc, q], send_sem=send_sems.at[4 * i + j],
                    recv_sem=recv_sems.at[4 * i + j], device_id=peer, device_id_type=MESH))
            second.append(pltpu.make_async_remote_copy(
                src_ref=o_ref.at[mc], dst_ref=o_ref.at[mc], send_sem=send_sems.at[4 * i + 3],
                recv_sem=recv_sems.at[4 * i + 3], device_id=peers[3], device_id_type=MESH))
        for cp in local + first:
            cp.start()
        for cp in local:
            cp.wait()
        for cp in first:
            cp.wait_recv()
        _run(second)
        for cp in first:
            cp.wait_send()

    launch()
    return [o[...] for o in o_refs]


def pair_swap(xs, name, halves, collective_id=None):
    n = len(xs)
    shapes = [jax.ShapeDtypeStruct(x.shape[:1] + x.shape[2:] if halves else x.shape, x.dtype) for x in xs]

    def body(*refs):
        x_refs, o_refs = refs[:n], refs[n:2 * n]
        send_sems, recv_sems, _ = refs[2 * n:]
        mx, my, mc = _my_place()
        _run([pltpu.make_async_remote_copy(
            src_ref=x_ref.at[:, 1 - mc] if halves else x_ref, dst_ref=o_ref, send_sem=send_sems.at[i],
            recv_sem=recv_sems.at[i], device_id=(mx, my, 1 - mc), device_id_type=MESH)
            for i, (x_ref, o_ref) in enumerate(zip(x_refs, o_refs))])

    return _comm_call(body, name, xs, shapes, n, 0, None if collective_id is None else (_sibling, collective_id))


def chip_all_to_all(xs, name, collective_id=None):
    n = len(xs)
    shapes = [jax.ShapeDtypeStruct((N_CHIPS - 1,) + x.shape[1:], x.dtype) for x in xs]

    def body(*refs):
        x_refs, o_refs = refs[:n], refs[n:2 * n]
        send_sems, recv_sems, _ = refs[2 * n:]
        mx, my, mc = _my_place()
        copies = []
        for i, (x_ref, o_ref) in enumerate(zip(x_refs, o_refs)):
            for j, (tx, ty) in enumerate(_other_chips(mx, my)):
                copies.append(pltpu.make_async_remote_copy(
                    src_ref=x_ref.at[2 * tx + ty], dst_ref=o_ref.at[j], send_sem=send_sems.at[3 * i + j],
                    recv_sem=recv_sems.at[3 * i + j], device_id=(tx, ty, mc), device_id_type=MESH))
        _run(copies)

    return _comm_call(body, name, xs, shapes, 3 * n, 0,
                      None if collective_id is None else (_same_core_of_other_chips, collective_id))


WEIGHTS = ['norm_mix', 'norm_xa', 'norm_mem', 'norm_ffn', 'xa_wq', 'xa_wk', 'xa_wv', 'xa_wo', 'xa_q_norm', 'xa_k_norm',
           'ffn_w_up', 'ffn_conv_w', 'ffn_conv_b', 'ffn_w_down', 'hg_lb_logits', 'mix_w_in', 'hg_out_norm',
           'mla_q_a_norm', 'mla_w_uq', 'mla_kv_a_norm', 'mla_w_ukv', 'mla_qn_nope', 'mla_qn_rope', 'mla_kn_nope',
           'mla_kn_rope', 'mix_w_out', 's5_lam_re', 's5_lam_im', 's5_log_dt', 's5_b_re', 's5_b_im', 's5_c_re',
           's5_c_im', 's5_d', 's5_w_glu_a', 's5_w_glu_b']
INPUTS = ['x', 'mem', 'positions'] + WEIGHTS + ['loss_target'] + ['m_' + n for n in WEIGHTS] + ['v_' + n for n in WEIGHTS]
SHARD_AXIS = {'xa_wq': 1, 'xa_wk': 1, 'xa_wv': 1, 'xa_wo': 1, 'ffn_w_up': 2, 'ffn_conv_w': 2, 'ffn_w_down': 1,
              'mix_w_in': 2, 'mla_w_uq': 2, 'mla_w_ukv': 2, 'mix_w_out': 1, 's5_d': 1, 's5_w_glu_a': 1, 's5_w_glu_b': 1}
BIG = ['xa_wq', 'xa_wk', 'xa_wv', 'xa_wo', 'ffn_w_up', 'ffn_w_down', 'mix_w_in', 'mix_w_out', 's5_w_glu_a', 's5_w_glu_b']
FIRST_NEEDED = ('mix_w_in', 'mix_w_out')
SMALL_SHARDED = [n for n in WEIGHTS if n in SHARD_AXIS and n not in BIG]
REPLICATED = [n for n in WEIGHTS if n not in SHARD_AXIS]
SMALL = SMALL_SHARDED + REPLICATED
PACK_W = 1024
ROW_MULT = 16
W_IN_SHARD = IN_WIDTH // N_CHIPS
W_IN_SHARD_PAD = 640


def _pack(flats, mult=ROW_MULT):
    flat = jnp.concatenate([f.reshape(-1) for f in flats])
    unit = mult * PACK_W
    n = -(-flat.shape[0] // unit) * unit
    return jnp.pad(flat, (0, n - flat.shape[0])).reshape(n // PACK_W, PACK_W)


def _unpack(packed, shapes):
    flat, out, o = packed.reshape(-1), [], 0
    for s in shapes:
        n = math.prod(s)
        out.append(flat[o:o + n].reshape(s))
        o += n
    return out


def _rope_pad(w):
    z = jnp.zeros(w.shape[:-1] + (MLA_ROPE // 2,), w.dtype)
    return jnp.concatenate([w[..., :MLA_ROPE // 2], z, w[..., MLA_ROPE // 2:], z], axis=-1)


def _rope_unpad(g):
    return jnp.concatenate([g[..., :MLA_ROPE // 2], g[..., 64:64 + MLA_ROPE // 2]], axis=-1)


def _blockdiag_in(bb):
    nb = bb.shape[0] // S5_GB
    t = bb.reshape(nb, S5_GB, S5_STATE, S5_GROUP).transpose(0, 1, 3, 2)
    return jnp.einsum('bgmp,gh->bgmhp', t, jnp.eye(S5_GB, dtype=bb.dtype)).reshape(nb, S5_GB * S5_GROUP, S5_LANES)


def _blockdiag_in_t(dw):
    nb = dw.shape[0]
    t = jnp.einsum('bgmhp,gh->bgmp', dw.reshape(nb, S5_GB, S5_GROUP, S5_GB, S5_STATE), jnp.eye(S5_GB, dtype=dw.dtype))
    return t.transpose(0, 1, 3, 2).reshape(nb * S5_GB, S5_STATE, S5_GROUP)


def _blockdiag_out(c):
    nb = c.shape[0] // S5_GB
    t = c.reshape(nb, S5_GB, S5_GROUP, S5_STATE).transpose(0, 1, 3, 2)
    return jnp.einsum('bgpm,gh->bgphm', t, jnp.eye(S5_GB, dtype=c.dtype)).reshape(nb, S5_LANES, S5_GB * S5_GROUP)


def _blockdiag_out_t(dc):
    nb = dc.shape[0]
    t = jnp.einsum('bgphm,gh->bgpm', dc.reshape(nb, S5_GB, S5_STATE, S5_GB, S5_GROUP), jnp.eye(S5_GB, dtype=dc.dtype))
    return t.transpose(0, 1, 3, 2).reshape(nb * S5_GB, S5_GROUP, S5_STATE)


def _gather_weights(P):
    def halves(x):
        return x if x.shape[0] == 2 else x.reshape(2, x.shape[1] // 2, x.shape[2])

    now = [n for n in BIG if n in FIRST_NEEDED]
    later = [n for n in BIG if n not in FIRST_NEEDED]
    xs = [halves(P[n].astype(BF16)) for n in now] + [halves(_pack([P[n] for n in SMALL_SHARDED], 2 * ROW_MULT)[None])]
    got = gather_two_level(xs, "gather_weights")
    got, xs_later = lax.optimization_barrier((got, [halves(P[n].astype(BF16)) for n in later]))
    got_later = gather_two_level_sequencer(xs_later, "gather_weights_later", 1)
    full_w = {}
    for n, g in list(zip(now, got[:-1])) + list(zip(later, got_later)):
        two_layers, by_rows = P[n].shape[0] == 2, SHARD_AXIS[n] == 1
        if two_layers and by_rows:
            full_w[n] = g.reshape(2, N_CHIPS * g.shape[2], g.shape[3])
        elif two_layers:
            full_w[n] = g.transpose(0, 2, 1, 3).reshape(2, g.shape[2], N_CHIPS * g.shape[3])
        elif by_rows:
            full_w[n] = g.transpose(1, 0, 2, 3).reshape(1, 2 * N_CHIPS * g.shape[2], g.shape[3])
        else:
            full_w[n] = g.transpose(0, 2, 1, 3).reshape(1, 2 * g.shape[2], N_CHIPS * g.shape[3])
    small = got[-1].transpose(1, 0, 2, 3).reshape(N_CHIPS, -1, PACK_W)
    per_chip = [_unpack(small[q], [P[n].shape for n in SMALL_SHARDED]) for q in range(N_CHIPS)]
    for i, n in enumerate(SMALL_SHARDED):
        full_w[n] = jnp.concatenate([per_chip[q][i] for q in range(N_CHIPS)], axis=SHARD_AXIS[n])
    return full_w


def _reduce_batch(items, small, P, results, tag, ids):
    mx, my, mc = _my_place()
    q = 2 * mx + my
    ids = ids or {}
    names = [f"{n}_{lyr}" for n, lyr, _ in items] + (['small'] if small is not None else [])
    xs = [g.reshape(N_CHIPS, 2, g.shape[1] // 2, g.shape[2]) for g in [g for _, _, g in items] + ([small] if small is not None else [])]
    def after(vals, tie):
        return (vals, None) if tie is None else lax.optimization_barrier((vals, tie))

    theirs = pair_swap(xs, "grads_pair_swap_" + tag, True, ids.get('swap'))
    theirs, tie = after(theirs, (yield None))
    pair = [add_own_half(x, mc, t, "grads_pair_sum_" + n, F32 if n == 'small' else BF16) for x, t, n in zip(xs, theirs, names)]
    got = chip_all_to_all(pair, "grads_chip_all_to_all_" + tag, ids.get('a2a'))
    if tie is not None:
        tie, _ = lax.optimization_barrier((tie, pair[-1]))
    got, tie = after(got, (yield pair[-1] if tie is None else tie))
    summed = [add_chips(p, q, g, "grads_chip_sum_" + n) for p, g, n in zip(pair, got, names)]
    other = pair_swap(summed[:len(items)], "grads_pair_join_" + tag, False, ids.get('join'))
    if small is not None:
        results['small_sum'] = device_gather(summed[-1], "grads_small_gather")
    other, tie = after(other, (yield summed[-1] if tie is None else tie))
    for (n, lyr, _), s, o in zip(items, summed, other):
        if n == 'mix_w_in':
            s, o = s[:, :W_IN_SHARD], o[:, :W_IN_SHARD]
        view = (P[n].shape[0], 2 * s.shape[0], P[n].shape[-1])
        bufs = results.get(n) or [lax.empty(view, F32) for _ in range(4)]
        results[n] = adamw_layer(P[n].reshape(view), s, o, mc, P['m_' + n].reshape(view), P['v_' + n].reshape(view), lyr,
                                 bufs, f"adamw_{n}_{lyr}")
    yield tie


def _update_small(small_sum, GS, P):
    mx, my, _ = _my_place()
    q = 2 * mx + my
    g_small = dict(zip(SMALL, _unpack(small_sum, [GS[n].shape for n in SMALL])))
    for n in SMALL_SHARDED:
        s = P[n].shape[SHARD_AXIS[n]]
        g_small[n] = lax.dynamic_slice_in_dim(g_small[n], q * s, s, axis=SHARD_AXIS[n])
    grad, delta, new_m, new_v = {}, {}, {}, {}
    packed = lambda prefix: _pack([P[prefix + n] for n in SMALL])
    d, m_, v_ = adamw(packed(''), _pack([g_small[n] for n in SMALL]), packed('m_'), packed('v_'), "adamw_small")
    shapes = [P[n].shape for n in SMALL]
    grad.update(g_small)
    for out, pk in ((delta, d), (new_m, m_), (new_v, v_)):
        out.update(zip(SMALL, _unpack(pk, shapes)))
    return grad, delta, new_m, new_v


def _row(v):
    return v.reshape(1, -1)


def _xattn_fwd(h, mem, W, lyr, tm):
    g_xa, g_mem = _row(W['norm_xa'][lyr]), _row(W['norm_mem'][lyr])
    g_q, g_k = _row(W['xa_q_norm'][lyr]), _row(W['xa_k_norm'][lyr])
    wq, wk, wv, wo = (W[n][lyr] for n in ('xa_wq', 'xa_wk', 'xa_wv', 'xa_wo'))
    L, D = h.shape
    M = mem.shape[0]
    hx = rms_fwd(h, g_xa, tm, MXU_DTYPE)
    qp = matmul(hx, wq, name="xa_q")
    kv_opds = [full(mem), full(g_mem), full(wk), full(wv), full(g_k)]
    k, v = blocked_fwd(_mem_kv, kv_opds, [((M, D), F32, (M, D), lambda i: (0, 0))] * 2, 1, "xa_mem_kv")
    o = blocked_fwd(_xa_core, [rows(qp, tm), full(k), full(v), full(g_q)], [_out((L, D), MXU_DTYPE, tm)], L // tm,
                    "xa_core")[0]
    out = matmul(o, wo, add=h, name="xa_o")
    return out, (h, hx, qp, k, v, o)


def _xattn_bwd(dout, saved, mem, W, lyr, tm):
    h, hx, qp, k, v, o = saved
    g_xa, g_mem = _row(W['norm_xa'][lyr]), _row(W['norm_mem'][lyr])
    g_q, g_k = _row(W['xa_q_norm'][lyr]), _row(W['xa_k_norm'][lyr])
    wq, wk, wv, wo = (W[n][lyr] for n in ('xa_wq', 'xa_wk', 'xa_wv', 'xa_wo'))
    L = h.shape[0]
    do = matmul(dout, wo, "nt", name="xa_do")
    d_wo = matmul(o, dout, "tn", name="xa_dwo")
    dqp, dk, dv, d_gq = blocked_bwd(_xa_core, [rows(qp, tm, 'blk'), full(k, 'acc'), full(v, 'acc'), full(g_q, 'acc')],
                                    [rows(do, tm)], L // tm, "xa_core_bwd")
    d_wq = matmul(hx, dqp, "tn", name="xa_dwq")
    dhx = matmul(dqp, wq, "nt", name="xa_dhx")
    dh, d_gxa = rms_bwd(h, g_xa, dhx, tm, dout)
    d_gmem, d_wk, d_wv, d_gk = blocked_bwd(
        _mem_kv, [full(mem), full(g_mem, 'acc'), full(wk, 'acc'), full(wv, 'acc'), full(g_k, 'acc')],
        [full(dk), full(dv)], 1, "xa_mem_kv_bwd")
    by_chip = lambda g: g.reshape(N_CHIPS, g.shape[0] // N_CHIPS, g.shape[1])
    grads = {'norm_xa': d_gxa, 'norm_mem': d_gmem, 'xa_q_norm': d_gq, 'xa_k_norm': d_gk,
             'xa_wq': by_chip(d_wq), 'xa_wk': by_chip(d_wk), 'xa_wv': by_chip(d_wv), 'xa_wo': by_chip(d_wo)}
    return dh, grads


def _conv_params(W, lyr):
    cw, cb = W['ffn_conv_w'][lyr], W['ffn_conv_b'][lyr]
    F = cw.shape[1] // 2
    return [cw[0:1, :F], cw[1:2, :F], cw[2:3, :F], cw[0:1, F:], cw[1:2, F:], cw[2:3, F:], _row(cb[:F]), _row(cb[F:])]


def _ffn_fwd(h, W, lyr, tm):
    L, D = h.shape
    w_up, w_down = W['ffn_w_up'][lyr], W['ffn_w_down'][lyr]
    F = w_down.shape[0]
    hf = rms_fwd(h, _row(W['norm_ffn'][lyr]), tm, MXU_DTYPE)
    ug = matmul(hf, w_up[:, :F], name="ffn_up_gate")
    uv = matmul(hf, w_up[:, F:], name="ffn_up_value")
    opds = [cols(ug, 128), cols(uv, 128)] + [cols(p, 128) for p in _conv_params(W, lyr)]
    a = blocked_fwd(_conv_gate, opds, [((L, F), MXU_DTYPE, (L, 128), lambda j: (0, j))], F // 128, "ffn_conv_gate")[0]
    out = matmul(a, w_down, add=h, name="ffn_down")
    return out, (h, hf, ug, uv, a)


def _ffn_bwd(dout, saved, W, lyr, tm):
    h, hf, ug, uv, a = saved
    w_up, w_down = W['ffn_w_up'][lyr], W['ffn_w_down'][lyr]
    F = w_down.shape[0]
    da = matmul(dout, w_down, "nt", name="ffn_da")
    d_wdown = matmul(a, dout, "tn", name="ffn_dwdown")
    opds = [cols(ug, 128, 'blk'), cols(uv, 128, 'blk')] + [cols(p, 128, 'blk') for p in _conv_params(W, lyr)]
    gs = blocked_bwd(_conv_gate, opds, [cols(da, 128)], F // 128, "ffn_conv_gate_bwd")
    dug, duv = gs[0], gs[1]
    d_cw = jnp.concatenate([jnp.concatenate(gs[2:5], axis=0), jnp.concatenate(gs[5:8], axis=0)], axis=1)
    d_cb = jnp.concatenate([gs[8], gs[9]], axis=1)[0]
    half = N_CHIPS // 2
    d_wup = lax.empty((N_CHIPS, hf.shape[1], w_up.shape[1] // N_CHIPS), F32)
    d_wup = matmul(hf, dug, "tn", name="ffn_dwup_gate", into=(d_wup, 0), col_blocks=half)
    d_wup = matmul(hf, duv, "tn", name="ffn_dwup_value", into=(d_wup, half), col_blocks=half)
    dhf = matmul(dug, w_up[:, :F], "nt", name="ffn_dhf_gate")
    dhf = matmul(duv, w_up[:, F:], "nt", add=dhf, name="ffn_dhf_value")
    dh, d_g = rms_bwd(h, _row(W['norm_ffn'][lyr]), dhf, tm, dout)
    d_wdown = d_wdown.reshape(N_CHIPS, F // N_CHIPS, d_wdown.shape[1])
    return dh, {'norm_ffn': d_g, 'ffn_w_up': d_wup, 'ffn_conv_w': d_cw, 'ffn_conv_b': d_cb, 'ffn_w_down': d_wdown}


def _mla_params(W):
    w_uq = W['mla_w_uq'][0].reshape(MLA_Q_RANK, MLA_HEADS, MLA_QK)
    w_uq = jnp.concatenate([w_uq[..., :MLA_NOPE], _rope_pad(w_uq[..., MLA_NOPE:])], axis=-1)
    w_ukv = W['mla_w_ukv'][0].reshape(MLA_KV_RANK, MLA_HEADS, MLA_NOPE + MLA_V)
    w_ukv = jnp.concatenate([w_ukv[..., :MLA_NOPE].reshape(MLA_KV_RANK, -1), w_ukv[..., MLA_NOPE:].reshape(MLA_KV_RANK, -1)],
                            axis=1)
    return [_row(W['mla_q_a_norm'][0]), w_uq.reshape(MLA_Q_RANK, MLA_HEADS * MLA_DK), _row(W['mla_kv_a_norm'][0]), w_ukv,
            _row(W['mla_qn_nope'][0]), _row(_rope_pad(W['mla_qn_rope'][0])), _row(W['mla_kn_nope'][0]),
            _row(_rope_pad(W['mla_kn_rope'][0]))]


def _w_in_padded(W):
    w = W['mix_w_in'][0]
    return jnp.concatenate([w[:, :IN_WIDTH - MLA_ROPE], _rope_pad(w[:, IN_WIDTH - MLA_ROPE:])], axis=1)


def _mixer0_fwd(h, W, cos_p, sin_p, tm):
    L = h.shape[0]
    t = min(ATTN_ROWS, L // ATTN_WIDE)
    hn = rms_fwd(h, _row(W['norm_mix'][0]), tm, MXU_DTYPE)
    proj = matmul(hn, _w_in_padded(W), name="mix_in")
    logits = W['hg_lb_logits']
    lb = blocked_fwd(_lb_first, [full(logits)], [((1, HG_WIDTH), F32, (1, HG_WIDTH), lambda i: (0, 0))], 1, "hg_lb")[0]
    gain = _row(W['hg_out_norm'][0])
    o_hg, states = hgrn2_fwd(proj, lb, gain)
    mp = _mla_params(W)
    q, k, v = mla_prep_fwd(proj, cos_p, sin_p, mp, tm)
    scale = MLA_QK ** -0.5
    o_mla, lse = attn_fwd(q, k, v, scale, t)
    w_out = W['mix_w_out'][0]
    out = matmul(o_hg, w_out[:HG_WIDTH], add=h, name="mix_out_hg")
    out = matmul(o_mla, w_out[HG_WIDTH:], add=out, name="mix_out_mla")
    return out, (h, hn, proj, lb, o_hg, states, q, k, v, o_mla, lse)


def _mixer0_bwd(dout, saved, W, cos_p, sin_p, tm, part_way=None):
    h, hn, proj, lb, o_hg, states, q, k, v, o_mla, lse = saved
    L = h.shape[0]
    t = min(ATTN_ROWS, L // ATTN_WIDE)
    scale = MLA_QK ** -0.5
    w_out = W['mix_w_out'][0]
    gain = _row(W['hg_out_norm'][0])
    do_hg = matmul(dout, w_out[:HG_WIDTH], "nt", name="mix_do_hg")
    do_mla = matmul(dout, w_out[HG_WIDTH:], "nt", name="mix_do_mla")
    d_wout = jnp.concatenate([matmul(o_hg, dout, "tn", name="mix_dwout_hg"), matmul(o_mla, dout, "tn", name="mix_dwout_mla")],
                             axis=0)
    if part_way is not None:
        do_mla = part_way(do_mla)
    dq = attn_bwd_dq(q, k, v, o_mla, lse, do_mla, scale, t)
    dk, dv = attn_bwd_dkv(q, k, v, o_mla, lse, do_mla, scale, t)
    mp = _mla_params(W)
    d_mla, d_qa, d_wuq, d_kva, d_wukv, d_qnn, d_qnr, d_knn, d_knr = mla_prep_bwd(proj, cos_p, sin_p, mp, dq, dk, dv, tm)
    d_hg, d_lb, d_gain = hgrn2_bwd(proj, lb, gain, states, do_hg)
    w_in, n_hg = _w_in_padded(W), 4 * HG_WIDTH
    d_win = jnp.concatenate([matmul(hn, d_hg, "tn", name="mix_dwin_hg"), matmul(hn, d_mla, "tn", name="mix_dwin_mla")], axis=1)
    dhn = matmul(d_hg, w_in[:, :n_hg], "nt", name="mix_dhn_hg")
    dhn = matmul(d_mla, w_in[:, n_hg:], "nt", add=dhn, name="mix_dhn_mla")
    dh, d_g = rms_bwd(h, _row(W['norm_mix'][0]), dhn, tm, dout)
    logits = W['hg_lb_logits']
    d_logits = blocked_bwd(_lb_first, [full(logits, 'acc')], [full(d_lb)], 1, "hg_lb_bwd")[0]
    d_wuq = d_wuq.reshape(MLA_Q_RANK, MLA_HEADS, MLA_DK)
    d_wuq = jnp.concatenate([d_wuq[..., :MLA_NOPE], _rope_unpad(d_wuq[..., MLA_NOPE:])], axis=-1)
    hw = MLA_HEADS * MLA_NOPE
    d_wukv = jnp.concatenate([d_wukv[:, :hw].reshape(MLA_KV_RANK, MLA_HEADS, MLA_NOPE),
                              d_wukv[:, hw:].reshape(MLA_KV_RANK, MLA_HEADS, MLA_V)], axis=-1)
    d_win = jnp.concatenate([d_win[:, :IN_WIDTH - MLA_ROPE], _rope_unpad(d_win[:, IN_WIDTH - MLA_ROPE:])], axis=1)
    d_win = d_win.reshape(d_win.shape[0], N_CHIPS, W_IN_SHARD).transpose(1, 0, 2)
    d_win = jnp.pad(d_win, ((0, 0), (0, 0), (0, W_IN_SHARD_PAD - W_IN_SHARD)))
    d_wout = d_wout.reshape(N_CHIPS, d_wout.shape[0] // N_CHIPS, d_wout.shape[1])
    grads = {'norm_mix': d_g, 'hg_lb_logits': d_logits, 'mix_w_in': d_win, 'hg_out_norm': d_gain,
             'mla_q_a_norm': d_qa, 'mla_w_uq': d_wuq.reshape(1, MLA_Q_RANK, -1), 'mla_kv_a_norm': d_kva,
             'mla_w_ukv': d_wukv.reshape(1, MLA_KV_RANK, -1), 'mla_qn_nope': d_qnn, 'mla_qn_rope': _rope_unpad(d_qnr),
             'mla_kn_nope': d_knn, 'mla_kn_rope': _rope_unpad(d_knr), 'mix_w_out': d_wout}
    return dh,grads


def _s5_inputs(W):
    G = W['s5_lam_re'].shape[1]
    return [W['s5_lam_re'][0], W['s5_lam_im'][0], W['s5_log_dt'][0].reshape(G, 1),
            W['s5_b_re'][0].reshape(G, -1), W['s5_b_im'][0].reshape(G, -1)]


def _mixer1_fwd(h, W, tm):
    L, D = h.shape
    u = rms_fwd(h, _row(W['norm_mix'][1]), tm, F32)
    di = _s5_inputs(W)
    G = di[0].shape[0]
    sq, wide = ((G, S5_STATE), F32, (G, S5_STATE), lambda i: (0, 0)), ((G, S5_STATE * S5_GROUP), F32, (G, S5_STATE * S5_GROUP), lambda i: (0, 0))
    ar, ai, bbr, bbi = blocked_fwd(_s5_discretize, [full(a) for a in di], [sq, sq, wide, wide], 1, "s5_discretize")
    nb = G // S5_GB
    core = (_blockdiag_in(bbr.reshape(G, S5_STATE, S5_GROUP)), _blockdiag_in(bbi.reshape(G, S5_STATE, S5_GROUP)),
            ar.reshape(nb, 1, S5_LANES), ai.reshape(nb, 1, S5_LANES),
            _blockdiag_out(W['s5_c_re'][0]), _blockdiag_out(W['s5_c_im'][0]))
    y = s5_fwd(u, *core)
    d = W['s5_d']
    y2 = blocked_fwd(_s5_post, [rows(y, tm), rows(u, tm), full(d)], [_out((L, D), MXU_DTYPE, tm)], L // tm, "s5_post")[0]
    w_ab = jnp.concatenate([W['s5_w_glu_a'][0], W['s5_w_glu_b'][0]], axis=1)
    ab = matmul(y2, w_ab, name="s5_glu_in")
    out = blocked_fwd(lambda a, b, res: (res + _glu(a, b)[0],),
                      [rows(ab, tm, col=0, width=D), rows(ab, tm, col=1, width=D), rows(h, tm)], [_out((L, D), F32, tm)],
                      L // tm, "s5_glu")[0]
    return out, (h, u, core, y, y2, ab)


def _mixer1_bwd(dout, saved, W, tm):
    h, u, core, y, y2, ab = saved
    L, D = h.shape
    da, db = blocked_bwd(_glu, [rows(ab, tm, 'blk', col=0, width=D), rows(ab, tm, 'blk', col=1, width=D)], [rows(dout, tm)],
                         L // tm, "s5_glu_bwd")
    w_a, w_b = W['s5_w_glu_a'][0], W['s5_w_glu_b'][0]
    dy2 = matmul(da, w_a, "nt", name="s5_dy2_a")
    dy2 = matmul(db, w_b, "nt", add=dy2, name="s5_dy2_b")
    d_wa = matmul(y2, da, "tn", name="s5_dwa")
    d_wb = matmul(y2, db, "tn", name="s5_dwb")
    d = W['s5_d']
    dy, du_skip, d_d = blocked_bwd(_s5_post, [rows(y, tm, 'blk'), rows(u, tm, 'blk'), full(d, 'acc')], [rows(dy2, tm)], L // tm,
                                   "s5_post_bwd")
    du, dwr, dwi, dar, dai, dcr, dci = s5_bwd(u, *core, dy, du_skip, min(S5_BWD_CHUNK, L))
    di = _s5_inputs(W)
    G = di[0].shape[0]
    cts = [dar.reshape(G, S5_STATE), dai.reshape(G, S5_STATE), _blockdiag_in_t(dwr).reshape(G, -1), _blockdiag_in_t(dwi).reshape(G, -1)]
    d_lr, d_li, d_ldt, d_br, d_bi = blocked_bwd(_s5_discretize, [full(a, 'acc') for a in di], [full(c) for c in cts], 1,
                                                "s5_discretize_bwd")
    dh, d_g = rms_bwd(h, _row(W['norm_mix'][1]), du, tm, dout)
    bshape = W['s5_b_re'].shape
    grads = {'norm_mix': d_g, 's5_lam_re': d_lr[None], 's5_lam_im': d_li[None], 's5_log_dt': d_ldt.reshape(1, G),
             's5_b_re': d_br.reshape(bshape), 's5_b_im': d_bi.reshape(bshape), 's5_c_re': _blockdiag_out_t(dcr)[None],
             's5_c_im': _blockdiag_out_t(dci)[None], 's5_d': d_d, 's5_w_glu_a': d_wa.reshape(N_CHIPS, -1, D), 's5_w_glu_b': d_wb.reshape(N_CHIPS, -1, D)}
    return dh,grads


def kernel(x, mem, positions, norm_mix, norm_xa, norm_mem, norm_ffn, xa_wq, xa_wk, xa_wv, xa_wo, xa_q_norm, xa_k_norm, ffn_w_up, ffn_conv_w, ffn_conv_b, ffn_w_down, hg_lb_logits, mix_w_in, hg_out_norm, mla_q_a_norm, mla_w_uq, mla_kv_a_norm, mla_w_ukv, mla_qn_nope, mla_qn_rope, mla_kn_nope, mla_kn_rope, mix_w_out, s5_lam_re, s5_lam_im, s5_log_dt, s5_b_re, s5_b_im, s5_c_re, s5_c_im, s5_d, s5_w_glu_a, s5_w_glu_b, loss_target, m_norm_mix, m_norm_xa, m_norm_mem, m_norm_ffn, m_xa_wq, m_xa_wk, m_xa_wv, m_xa_wo, m_xa_q_norm, m_xa_k_norm, m_ffn_w_up, m_ffn_conv_w, m_ffn_conv_b, m_ffn_w_down, m_hg_lb_logits, m_mix_w_in, m_hg_out_norm, m_mla_q_a_norm, m_mla_w_uq, m_mla_kv_a_norm, m_mla_w_ukv, m_mla_qn_nope, m_mla_qn_rope, m_mla_kn_nope, m_mla_kn_rope, m_mix_w_out, m_s5_lam_re, m_s5_lam_im, m_s5_log_dt, m_s5_b_re, m_s5_b_im, m_s5_c_re, m_s5_c_im, m_s5_d, m_s5_w_glu_a, m_s5_w_glu_b, v_norm_mix, v_norm_xa, v_norm_mem, v_norm_ffn, v_xa_wq, v_xa_wk, v_xa_wv, v_xa_wo, v_xa_q_norm, v_xa_k_norm, v_ffn_w_up, v_ffn_conv_w, v_ffn_conv_b, v_ffn_w_down, v_hg_lb_logits, v_mix_w_in, v_hg_out_norm, v_mla_q_a_norm, v_mla_w_uq, v_mla_kv_a_norm, v_mla_w_ukv, v_mla_qn_nope, v_mla_qn_rope, v_mla_kn_nope, v_mla_kn_rope, v_mix_w_out, v_s5_lam_re, v_s5_lam_im, v_s5_log_dt, v_s5_b_re, v_s5_b_im, v_s5_c_re, v_s5_c_im, v_s5_d, v_s5_w_glu_a, v_s5_w_glu_b):
    P = dict(locals())
    assert sorted(P) == sorted(INPUTS) and norm_mix.shape[0] == 2 and mix_w_in.shape[0] == 1
    x, mem, target = P['x'][0], P['mem'][0], P['loss_target'][0]
    L, D = x.shape
    tm = min(256, L)

    W = {n: P[n] for n in REPLICATED}
    W.update(_gather_weights(P))

    inv_freq = 1.0 / (ROPE_BASE ** (jnp.arange(0, MLA_ROPE, 2, dtype=F32) / MLA_ROPE))
    ang = P['positions'][0].astype(F32)[:, None] * inv_freq
    cos, sin, z = jnp.cos(ang), jnp.sin(ang), jnp.zeros_like(ang)
    cos_p = jnp.concatenate([cos, z, cos, z], axis=1)
    sin_p = jnp.concatenate([-sin, z, sin, z], axis=1)

    h, s_mix0 = _mixer0_fwd(x, W, cos_p, sin_p, tm)
    h, s_xa0 = _xattn_fwd(h, mem, W, 0, tm)
    h, s_ffn0 = _ffn_fwd(h, W, 0, tm)
    h, s_mix1 = _mixer1_fwd(h, W, tm)
    h, s_xa1 = _xattn_fwd(h, mem, W, 1, tm)
    h, s_ffn1 = _ffn_fwd(h, W, 1, tm)
    n = L // tm
    dh, parts = blocked_fwd(_loss_fn, [rows(h, tm), rows(target, tm)],
                            [_out((L, D), F32, tm), ((n * 8, 128), F32, (8, 128), lambda i: (i, 0))], n, "loss")
    loss = lax.psum(jnp.sum(parts), ("x", "y", "c"))

    layered = {}

    def collect(g, lyr):
        for k_, v_ in g.items():
            layered.setdefault(k_, {})[lyr] = v_

    results = {}

    def big_items(lyr, names):
        return [(n_, 0 if P[n_].shape[0] == 1 else lyr, layered[n_][lyr]) for n_ in names]

    per_layer = [n_ for n_ in BIG if P[n_].shape[0] == 2]
    second_mixer = ['s5_w_glu_a', 's5_w_glu_b']
    first_mixer = ['mix_w_in', 'mix_w_out']
    assert sorted(per_layer + second_mixer + first_mixer) == sorted(BIG)

    dh, g = _ffn_bwd(dh, s_ffn1, W, 1, tm)
    collect(g, 1)
    dh, g = _xattn_bwd(dh, s_xa1, mem, W, 1, tm)
    collect(g, 1)
    dh, g = _mixer1_bwd(dh, s_mix1, W, tm)
    collect(g, 1)
    late = _reduce_batch(big_items(1, per_layer + second_mixer), None, P, results, "late_layer", {'swap': 2, 'a2a': 3, 'join': 4})
    next(late)
    dh, g = _ffn_bwd(dh, s_ffn0, W, 0, tm)
    collect(g, 0)
    dh = late.send(dh)
    dh, g = _xattn_bwd(dh, s_xa0, mem, W, 0, tm)
    collect(g, 0)
    mid = _reduce_batch(big_items(0, per_layer), None, P, results, "first_layer", {'swap': 5, 'a2a': 6, 'join': 7})
    next(mid)
    dx, g = _mixer0_bwd(dh, s_mix0, W, cos_p, sin_p, tm, part_way=mid.send)
    collect(g, 0)

    GS = {}
    for name in SMALL:
        by_layer = [layered[name][lyr] for lyr in sorted(layered[name])]
        full_shape = W[name].shape
        GS[name] = (by_layer[0].reshape(full_shape) if len(by_layer) == 1
                    else jnp.stack([g_.reshape(full_shape[1:]) for g_ in by_layer]))
    small = _pack([GS[n_] for n_ in SMALL], 2 * N_CHIPS * ROW_MULT).reshape(N_CHIPS, -1, PACK_W)
    dx = late.send(dx)
    dx = mid.send(dx)
    last = _reduce_batch(big_items(0, first_mixer), small, P, results, "first_mixer", {'swap': 8, 'a2a': 9, 'join': 10})
    next(last)
    late.send(None)
    last_pair_sum = last.send(None)
    mid.send(last_pair_sum)
    mid_name = per_layer[-1]
    results[mid_name] = list(last.send(list(results[mid_name])))
    last.send(None)
    outs = list(_update_small(results['small_sum'], GS, P))
    for k_ in range(4):
        outs[k_].update({n_: results[n_][k_].reshape(P[n_].shape) for n_ in BIG})
    return (loss, dx[None], *[d[n_] for d in outs for n_ in WEIGHTS])
```

```python
import functools
import math

import jax
import jax.numpy as jnp
import numpy as np
from jax import lax
from jax.experimental import pallas as pl
from jax.experimental.pallas import tpu as pltpu
from jax.experimental.pallas import tpu_sc as plsc

F32 = jnp.float32
BF16 = jnp.bfloat16
MXU_DTYPE = BF16
HI = lax.Precision.HIGHEST
V7X_VMEM_LIMIT_BYTES = 56 * 1024 * 1024
EPS = 1e-6
MESH = pl.DeviceIdType.MESH

HG_HEADS, HG_DIM = 4, 128
HG_WIDTH = HG_HEADS * HG_DIM
HG_SUB = 32
HG_BLOCK = 64
MLA_HEADS, MLA_Q_RANK, MLA_KV_RANK = 4, 256, 128
MLA_NOPE, MLA_ROPE, MLA_V = 128, 64, 128
MLA_QK = MLA_NOPE + MLA_ROPE
MLA_DK = 256
ROPE_BASE = 10000.0
IN_WIDTH = 4 * HG_WIDTH + MLA_Q_RANK + MLA_KV_RANK + MLA_ROPE
IN_PAD = 4 * HG_WIDTH + MLA_Q_RANK + MLA_KV_RANK + 128
S5_GROUP, S5_STATE = 16, 64
S5_GB = 8
DT_MIN, DT_MAX = 1e-3, 1e-1
XA_HEADS = 4
CONV_W = 3
ADAM_LR, ADAM_B1, ADAM_B2, ADAM_EPS, ADAM_WD, ADAM_STEP = 0.001, 0.9, 0.999, 1e-08, 0.01, 10


def _cparams(sem):
    return pltpu.CompilerParams(dimension_semantics=sem, vmem_limit_bytes=V7X_VMEM_LIMIT_BYTES)


class Opd:
    def __init__(self, arr, block, imap, grad=None, gshape=None, gimap=None):
        self.arr, self.block, self.imap, self.grad = arr, block, imap, grad
        self.gshape = arr.shape if gshape is None else gshape
        self.gimap = imap if gimap is None else gimap

    def spec(self):
        return pl.BlockSpec(self.block, self.imap)

    def gspec(self):
        return pl.BlockSpec(self.block, self.gimap)


def rows(arr, tm, grad=None, col=0, width=None):
    width = arr.shape[1] if width is None else width
    return Opd(arr, (tm, width), lambda i, c=col: (i, c), grad, (arr.shape[0], width), lambda i: (i, 0))


def cols(arr, tn, grad=None):
    return Opd(arr, (arr.shape[0], tn), lambda j: (0, j), grad)


def full(arr, grad=None):
    return Opd(arr, arr.shape, lambda i: (0, 0), grad)


def _load(ref):
    v = ref[...]
    return v.astype(F32) if jnp.issubdtype(v.dtype, jnp.floating) else v


def blocked_fwd(f, opds, outs, n, name):
    n_in = len(opds)

    def body(*refs):
        ys = f(*[_load(r) for r in refs[:n_in]])
        for r, y in zip(refs[n_in:], ys):
            r[...] = y.astype(r.dtype)

    res = pl.pallas_call(
        body, name=name, grid=(n,),
        in_specs=[o.spec() for o in opds],
        out_specs=[pl.BlockSpec(b, m) for (_, _, b, m) in outs],
        out_shape=[jax.ShapeDtypeStruct(s, d) for (s, d, _, _) in outs],
        compiler_params=_cparams(("parallel",)),
    )(*[o.arr for o in opds])
    return res


def blocked_bwd(f, opds, dys, n, name, plus=None):
    n_in, n_dy = len(opds), len(dys)
    diff = [i for i, o in enumerate(opds) if o.grad]
    extra = [] if plus is None else [plus]

    def body(*refs):
        vals = [_load(r) for r in refs[:n_in]]

        def fd(*dv):
            allv = list(vals)
            for i, v in zip(diff, dv):
                allv[i] = v
            return tuple(f(*allv))

        ys, vjp = jax.vjp(fd, *[vals[i] for i in diff])
        cts = tuple(_load(r).astype(y.dtype) for r, y in zip(refs[n_in:n_in + n_dy], ys))
        gs = list(vjp(cts))
        if extra:
            gs[0] = gs[0] + _load(refs[n_in + n_dy])
        for r, g, i in zip(refs[n_in + n_dy + len(extra):], gs, diff):
            if opds[i].grad == 'acc':
                @pl.when(pl.program_id(0) == 0)
                def _(r=r):
                    r[...] = jnp.zeros(r.shape, r.dtype)
                r[...] += g.astype(r.dtype)
            else:
                r[...] = g.astype(r.dtype)

    any_acc = any(opds[i].grad == 'acc' for i in diff)
    res = pl.pallas_call(
        body, name=name, grid=(n,),
        in_specs=[o.spec() for o in opds + dys + extra],
        out_specs=[opds[i].gspec() for i in diff],
        out_shape=[jax.ShapeDtypeStruct(opds[i].gshape, F32) for i in diff],
        compiler_params=_cparams(("arbitrary" if any_acc else "parallel",)),
    )(*[o.arr for o in opds + dys + extra])
    return res


def _tile(dim, want):
    for t in range(want - want % 16, 0, -16):
        if dim % t == 0:
            return t
    assert dim <= want, (dim, want)
    return dim


MATMUL_VMEM_BUDGET = 40 * 1024 * 1024
MATMUL_ROWS = 1024


def _widest(N, fits):
    for t in range(N - N % 128, 0, -128):
        if N % t == 0 and fits(t):
            return t
    return N


def matmul(a, b, mode="nn", out_dtype=F32, add=None, name="matmul", into=None, col_blocks=None):
    sa, sb, so = a.dtype.itemsize, b.dtype.itemsize, jnp.dtype(out_dtype).itemsize
    has_add = add is not None
    if mode == "tn":
        (K, M), (K2, N) = a.shape, b.shape
        assert K == K2 and not has_add and out_dtype == F32, (a.shape, b.shape)
        tk = _tile(K, MATMUL_ROWS)
        tn = _widest(N, lambda t: 2 * (tk * M * sa + tk * t * sb + M * t * 4) <= MATMUL_VMEM_BUDGET)
        extra, alias = [], {}
        if into is not None:
            buf, lead = into[0], tuple(into[1:])
            if col_blocks is not None:
                assert N % col_blocks == 0 and (N // col_blocks) % 128 == 0 and tn >= N // col_blocks, (N, col_blocks, tn)
                tn = N // col_blocks
                assert buf.shape[len(lead):] == (M, tn), (buf.shape, lead, M, tn)
                out_spec = pl.BlockSpec((None,) * len(lead) + (M, tn), lambda j, k: lead[:-1] + (lead[-1] + j, 0, 0))
            else:
                assert buf.shape[len(lead):] == (M, N), (buf.shape, lead, M, N)
                out_spec = pl.BlockSpec((None,) * len(lead) + (M, tn), lambda j, k: lead + (0, j))
            out_shape = jax.ShapeDtypeStruct(buf.shape, F32)
            extra, alias = [buf], {2: 0}
        else:
            assert col_blocks is None
            out_spec = pl.BlockSpec((M, tn), lambda j, k: (0, j))
            out_shape = jax.ShapeDtypeStruct((M, N), F32)

        def body(a_ref, b_ref, *rest):
            o_ref = rest[-1]
            r = lax.dot_general(a_ref[...].astype(MXU_DTYPE), b_ref[...].astype(MXU_DTYPE), ((_TN), ((), ())),
                                preferred_element_type=F32)

            @pl.when(pl.program_id(1) == 0)
            def _():
                o_ref[...] = r

            @pl.when(pl.program_id(1) > 0)
            def _():
                o_ref[...] += r

        return pl.pallas_call(
            body, name=name, grid=(N // tn, K // tk),
            in_specs=[pl.BlockSpec((tk, M), lambda j, k: (k, 0)), pl.BlockSpec((tk, tn), lambda j, k: (k, j))]
            + [pl.BlockSpec(memory_space=pl.ANY)] * len(extra),
            out_specs=out_spec, out_shape=out_shape, input_output_aliases=alias,
            compiler_params=_cparams(("parallel", "arbitrary")),
        )(a, b, *extra)

    (M, K) = a.shape
    N = b.shape[1] if mode == "nn" else b.shape[0]
    assert K == (b.shape[0] if mode == "nn" else b.shape[1]), (a.shape, b.shape, mode)
    tm = _tile(M, MATMUL_ROWS)
    tn = _widest(N, lambda t: 2 * (tm * K * sa + K * t * sb + tm * t * (so + 4 * has_add)) <= MATMUL_VMEM_BUDGET)
    dims = ((_NN if mode == "nn" else _NT), ((), ()))

    def body(*refs):
        r = lax.dot_general(refs[0][...].astype(MXU_DTYPE), refs[1][...].astype(MXU_DTYPE), dims, preferred_element_type=F32)
        if has_add:
            r = r + refs[2][...].astype(F32)
        refs[-1][...] = r.astype(refs[-1].dtype)

    b_spec = pl.BlockSpec((K, tn), lambda j, i: (0, j)) if mode == "nn" else pl.BlockSpec((tn, K), lambda j, i: (j, 0))
    in_specs = [pl.BlockSpec((tm, K), lambda j, i: (i, 0)), b_spec]
    args = [a, b]
    if has_add:
        in_specs.append(pl.BlockSpec((tm, tn), lambda j, i: (i, j)))
        args.append(add)
    return pl.pallas_call(
        body, name=name, grid=(N // tn, M // tm),
        in_specs=in_specs,
        out_specs=pl.BlockSpec((tm, tn), lambda j, i: (i, j)),
        out_shape=jax.ShapeDtypeStruct((M, N), out_dtype),
        compiler_params=_cparams(("parallel", "parallel")),
    )(*args)


def _dot(a, b, dims, precision=None):
    if precision is None:
        a, b = a.astype(MXU_DTYPE), b.astype(MXU_DTYPE)
    return lax.dot_general(a, b, (dims, ((), ())), precision=precision, preferred_element_type=F32)


_NN = ((1,), (0,))
_NT = ((1,), (1,))
_TN = ((0,), (0,))


def _rms(x, gain):
    return x * lax.rsqrt(jnp.mean(x * x, axis=-1, keepdims=True) + EPS) * gain


def _hg_block(st_t, q, fl, iv, g, lb, gain):
    row = lax.broadcasted_iota(jnp.int32, (HG_SUB, HG_SUB), 0)
    col = lax.broadcasted_iota(jnp.int32, (HG_SUB, HG_SUB), 1)
    tri = (row >= col).astype(F32)
    heads = [slice(h * HG_DIM, (h + 1) * HG_DIM) for h in range(HG_HEADS)]
    fg = lb + (1.0 - lb) * jax.nn.sigmoid(fl)
    lf, kk, qf = jnp.log(fg), 1.0 - fg, jax.nn.silu(q)
    sts = [st_t[sl, :] for sl in heads]
    parts = [[] for _ in heads]
    for s in range(q.shape[0] // HG_SUB):
        r = slice(s * HG_SUB, (s + 1) * HG_SUB)
        b = lf[r]
        d = 1
        while d < HG_SUB:
            b = b + _shift_rows(b, d)
            d *= 2
        b_mid = jnp.sum(lf[r][:HG_SUB // 2], axis=0, keepdims=True)
        b_end = jnp.sum(lf[r], axis=0, keepdims=True)
        q_in, k_in = qf[r] * jnp.exp(b - b_mid), kk[r] * jnp.exp(b_mid - b)
        q_st, k_st, decay = qf[r] * jnp.exp(b), kk[r] * jnp.exp(b_end - b), jnp.exp(b_end)
        for h, sl in enumerate(heads):
            sc = _dot(q_in[:, sl], k_in[:, sl], _NT) * tri
            parts[h].append(_dot(sc, iv[r, sl], _NN) + _dot(q_st[:, sl], sts[h], _NT))
            sts[h] = sts[h] * decay[:, sl] + _dot(iv[r, sl], k_st[:, sl], _TN)
    outs = [_rms(jnp.concatenate(parts[h], axis=0), gain[:, sl]) * jax.nn.silu(g[:, sl]) for h, sl in enumerate(heads)]
    return jnp.concatenate(sts, axis=0), jnp.concatenate(outs, axis=1)


def _hg_specs(proj, nb):
    return [pl.BlockSpec((HG_BLOCK, HG_WIDTH), lambda i, c=c, f=nb: (f(i), c)) for c in range(4)]


def hgrn2_fwd(proj, lb, gain):
    L = proj.shape[0]
    n = L // HG_BLOCK

    def body(q, fl, iv, g, lb_r, gain_r, o_ref, st_ref, st):
        @pl.when(pl.program_id(0) == 0)
        def _():
            st[...] = jnp.zeros(st.shape, F32)

        st_ref[0] = st[...]
        new, o = _hg_block(st[...], q[...], fl[...], iv[...], g[...], lb_r[...], gain_r[...])
        st[...] = new
        o_ref[...] = o.astype(o_ref.dtype)

    pspec = pl.BlockSpec((1, HG_WIDTH), lambda i: (0, 0))
    return pl.pallas_call(
        body, name="hgrn2_fwd", grid=(n,),
        in_specs=_hg_specs(proj, lambda i: i) + [pspec, pspec],
        out_specs=[pl.BlockSpec((HG_BLOCK, HG_WIDTH), lambda i: (i, 0)),
                   pl.BlockSpec((1, HG_WIDTH, HG_DIM), lambda i: (i, 0, 0))],
        out_shape=[jax.ShapeDtypeStruct((L, HG_WIDTH), MXU_DTYPE),
                   jax.ShapeDtypeStruct((n, HG_WIDTH, HG_DIM), F32)],
        scratch_shapes=[pltpu.VMEM((HG_WIDTH, HG_DIM), F32)],
        compiler_params=_cparams(("arbitrary",)),
    )(proj, proj, proj, proj, lb, gain)


def hgrn2_bwd(proj, lb, gain, states, do):
    L = proj.shape[0]
    n = L // HG_BLOCK

    def body(q, fl, iv, g, lb_r, gain_r, st_r, do_r, dproj, dlb, dgain, dst):
        @pl.when(pl.program_id(0) == 0)
        def _():
            dst[...] = jnp.zeros(dst.shape, F32)
            dlb[...] = jnp.zeros(dlb.shape, F32)
            dgain[...] = jnp.zeros(dgain.shape, F32)

        _, vjp = jax.vjp(_hg_block, st_r[0], q[...], fl[...], iv[...], g[...], lb_r[...], gain_r[...])
        d_st, dq, dfl, div, dg, d_lb, d_gain = vjp((dst[...], do_r[...].astype(F32)))
        dst[...] = d_st
        dproj[:, 0 * HG_WIDTH:1 * HG_WIDTH] = dq
        dproj[:, 1 * HG_WIDTH:2 * HG_WIDTH] = dfl
        dproj[:, 2 * HG_WIDTH:3 * HG_WIDTH] = div
        dproj[:, 3 * HG_WIDTH:4 * HG_WIDTH] = dg
        dlb[...] += d_lb
        dgain[...] += d_gain

    rev = lambda i: n - 1 - i
    pspec = pl.BlockSpec((1, HG_WIDTH), lambda i: (0, 0))
    return pl.pallas_call(
        body, name="hgrn2_bwd", grid=(n,),
        in_specs=_hg_specs(proj, rev) + [pspec, pspec,
                                         pl.BlockSpec((1, HG_WIDTH, HG_DIM), lambda i: (rev(i), 0, 0)),
                                         pl.BlockSpec((HG_BLOCK, HG_WIDTH), lambda i: (rev(i), 0))],
        out_specs=[pl.BlockSpec((HG_BLOCK, 4 * HG_WIDTH), lambda i: (rev(i), 0)), pspec, pspec],
        out_shape=[jax.ShapeDtypeStruct((L, 4 * HG_WIDTH), F32),
                   jax.ShapeDtypeStruct((1, HG_WIDTH), F32), jax.ShapeDtypeStruct((1, HG_WIDTH), F32)],
        scratch_shapes=[pltpu.VMEM((HG_WIDTH, HG_DIM), F32)],
        compiler_params=_cparams(("arbitrary",)),
    )(proj, proj, proj, proj, lb, gain, states, do)


def _rope_rms(x, gain_p, cos_p, sin_p):
    n = x * lax.rsqrt(jnp.sum(x * x, axis=-1, keepdims=True) * (1.0 / MLA_ROPE) + EPS) * gain_p
    r = lax.broadcasted_iota(jnp.int32, (128, 128), 0)
    c = lax.broadcasted_iota(jnp.int32, (128, 128), 1)
    swap = (r == (c + 64) % 128).astype(F32)
    return n * cos_p + _dot(n, swap, _NN, HI) * sin_p


MLA_IN = MLA_Q_RANK + MLA_KV_RANK + 128


def _mla_prep(x, cos_p, sin_p, q_a, w_uq, kv_a, w_ukv, qn_nope, qn_rope, kn_nope, kn_rope):
    c_q, c_kv, kpe = x[:, :MLA_Q_RANK], x[:, MLA_Q_RANK:MLA_Q_RANK + MLA_KV_RANK], x[:, MLA_Q_RANK + MLA_KV_RANK:]
    q = _dot(_rms(c_q, q_a), w_uq, _NN)
    kv = _dot(_rms(c_kv, kv_a), w_ukv, _NN)
    k_pe = _rope_rms(kpe, kn_rope, cos_p, sin_p)
    qs, ks = [], []
    for h in range(MLA_HEADS):
        qs.append(_rms(q[:, h * MLA_DK:h * MLA_DK + MLA_NOPE], qn_nope))
        qs.append(_rope_rms(q[:, h * MLA_DK + MLA_NOPE:(h + 1) * MLA_DK], qn_rope, cos_p, sin_p))
        ks.append(_rms(kv[:, h * MLA_NOPE:(h + 1) * MLA_NOPE], kn_nope))
        ks.append(k_pe)
    return jnp.concatenate(qs, axis=1), jnp.concatenate(ks, axis=1), kv[:, MLA_HEADS * MLA_NOPE:]


def _mla_prep_opds(proj, cos_p, sin_p, params, tm, grads):
    g = (lambda k: k) if grads else (lambda k: None)
    assert (4 * HG_WIDTH) % MLA_IN == 0
    return ([rows(proj, tm, g('blk'), col=4 * HG_WIDTH // MLA_IN, width=MLA_IN), rows(cos_p, tm), rows(sin_p, tm)]
            + [full(p, g('acc')) for p in params])


def mla_prep_fwd(proj, cos_p, sin_p, params, tm):
    L = proj.shape[0]
    W = MLA_HEADS * MLA_DK
    rb = lambda w: (tm, w)
    outs = [((L, W), MXU_DTYPE, rb(W), lambda i: (i, 0)), ((L, W), MXU_DTYPE, rb(W), lambda i: (i, 0)),
            ((L, MLA_HEADS * MLA_V), MXU_DTYPE, rb(MLA_HEADS * MLA_V), lambda i: (i, 0))]
    return blocked_fwd(_mla_prep, _mla_prep_opds(proj, cos_p, sin_p, params, tm, False), outs, L // tm, "mla_prep_fwd")


def mla_prep_bwd(proj, cos_p, sin_p, params, dq, dk, dv, tm):
    L = proj.shape[0]
    return blocked_bwd(_mla_prep, _mla_prep_opds(proj, cos_p, sin_p, params, tm, True),
                       [rows(dq, tm), rows(dk, tm), rows(dv, tm)], L // tm, "mla_prep_bwd")


def _scores(q, k, scale, shift=None):
    s = _dot(q, k, _NT) * scale
    if shift is None:
        return s
    row = lax.broadcasted_iota(jnp.int32, s.shape, 0)
    col = lax.broadcasted_iota(jnp.int32, s.shape, 1)
    return jnp.where(col <= row + shift, s, -jnp.inf)


ATTN_ROWS = 512
ATTN_WIDE = 2


def attn_fwd(q, k, v, scale, t):
    L = q.shape[0]
    tq = ATTN_WIDE * t

    def body(q_ref, k_ref, v_ref, o_ref, lse_ref):
        i = pl.program_id(1)
        qb = q_ref[...]

        def step(j, carry, shift=None):
            m, l, acc = carry
            kj = k_ref[pl.ds(pl.multiple_of(j * t, t), t), :]
            vj = v_ref[pl.ds(pl.multiple_of(j * t, t), t), :]
            s = _scores(qb, kj, scale, shift)
            m_new = jnp.maximum(m, jnp.max(s, axis=-1, keepdims=True))
            p = jnp.exp(s - m_new)
            alpha = jnp.exp(m - m_new)
            return m_new, alpha * l + jnp.sum(p, axis=-1, keepdims=True), alpha * acc + _dot(p, vj, _NN)

        carry = (jnp.full((tq, 1), -jnp.inf, F32), jnp.zeros((tq, 1), F32), jnp.zeros((tq, MLA_V), F32))
        carry = lax.fori_loop(0, ATTN_WIDE * i, step, carry)
        for d in range(ATTN_WIDE):
            carry = step(ATTN_WIDE * i + d, carry, -d * t)
        m, l, acc = carry
        o_ref[...] = acc / l
        lse_ref[...] = jnp.broadcast_to(m + jnp.log(l), lse_ref.shape)

    hspec = lambda rows_, w: pl.BlockSpec((rows_, w), lambda h, i: (0, h))
    bspec = lambda w: pl.BlockSpec((tq, w), lambda h, i: (i, h))
    return pl.pallas_call(
        body, name="attn_fwd", grid=(MLA_HEADS, L // tq),
        in_specs=[bspec(MLA_DK), hspec(L, MLA_DK), hspec(L, MLA_V)],
        out_specs=[bspec(MLA_V), bspec(MLA_V)],
        out_shape=[jax.ShapeDtypeStruct((L, MLA_HEADS * MLA_V), F32)] * 2,
        compiler_params=_cparams(("parallel", "parallel")),
    )(q, k, v)


def attn_bwd_dq(q, k, v, o, lse, do, scale, t):
    L = q.shape[0]
    tq = ATTN_WIDE * t

    def body(q_ref, k_ref, v_ref, o_ref, lse_ref, do_ref, dq_ref):
        i = pl.program_id(1)
        qb, dob = q_ref[...], do_ref[...]
        delta = jnp.sum(dob * o_ref[...], axis=-1, keepdims=True)
        lse_c = jnp.max(lse_ref[...], axis=-1, keepdims=True)

        def step(j, dq, shift=None):
            kj = k_ref[pl.ds(pl.multiple_of(j * t, t), t), :]
            vj = v_ref[pl.ds(pl.multiple_of(j * t, t), t), :]
            p = jnp.exp(_scores(qb, kj, scale, shift) - lse_c)
            ds = p * (_dot(dob, vj, _NT) - delta) * scale
            return dq + _dot(ds, kj, _NN)

        dq = lax.fori_loop(0, ATTN_WIDE * i, step, jnp.zeros((tq, MLA_DK), F32))
        for d in range(ATTN_WIDE):
            dq = step(ATTN_WIDE * i + d, dq, -d * t)
        dq_ref[...] = dq

    hspec = lambda w: pl.BlockSpec((L, w), lambda h, i: (0, h))
    bspec = lambda w: pl.BlockSpec((tq, w), lambda h, i: (i, h))
    return pl.pallas_call(
        body, name="attn_bwd_dq", grid=(MLA_HEADS, L // tq),
        in_specs=[bspec(MLA_DK), hspec(MLA_DK), hspec(MLA_V), bspec(MLA_V), bspec(MLA_V), bspec(MLA_V)],
        out_specs=bspec(MLA_DK),
        out_shape=jax.ShapeDtypeStruct((L, MLA_HEADS * MLA_DK), F32),
        compiler_params=_cparams(("parallel", "parallel")),
    )(q, k, v, o, lse, do)


def attn_bwd_dkv(q, k, v, o, lse, do, scale, t):
    L = q.shape[0]
    tk = ATTN_WIDE * t

    def body(q_ref, k_ref, v_ref, o_ref, lse_ref, do_ref, dk_ref, dv_ref):
        j = pl.program_id(1)
        kb, vb = k_ref[...], v_ref[...]

        def step(i, carry, shift=None):
            dk, dv = carry
            r = pl.ds(pl.multiple_of(i * t, t), t)
            qi, doi = q_ref[r, :], do_ref[r, :]
            delta = jnp.sum(doi * o_ref[r, :], axis=-1, keepdims=True)
            lse_c = jnp.max(lse_ref[r, :], axis=-1, keepdims=True)
            p = jnp.exp(_scores(qi, kb, scale, shift) - lse_c)
            ds = p * (_dot(doi, vb, _NT) - delta) * scale
            return dk + _dot(ds, qi, _TN), dv + _dot(p, doi, _TN)

        carry = (jnp.zeros((tk, MLA_DK), F32), jnp.zeros((tk, MLA_V), F32))
        for d in range(ATTN_WIDE):
            carry = step(ATTN_WIDE * j + d, carry, d * t)
        dk, dv = lax.fori_loop(ATTN_WIDE * (j + 1), L // t, step, carry)
        dk_ref[...] = dk
        dv_ref[...] = dv

    hspec = lambda w: pl.BlockSpec((L, w), lambda h, j: (0, h))
    bspec = lambda w: pl.BlockSpec((tk, w), lambda h, j: (j, h))
    return pl.pallas_call(
        body, name="attn_bwd_dkv", grid=(MLA_HEADS, L // tk),
        in_specs=[hspec(MLA_DK), bspec(MLA_DK), bspec(MLA_V), hspec(MLA_V), hspec(MLA_V), hspec(MLA_V)],
        out_specs=[bspec(MLA_DK), bspec(MLA_V)],
        out_shape=[jax.ShapeDtypeStruct((L, MLA_HEADS * MLA_DK), F32), jax.ShapeDtypeStruct((L, MLA_HEADS * MLA_V), F32)],
        compiler_params=_cparams(("parallel", "parallel")),
    )(q, k, v, o, lse, do)


S5_LANES = S5_GB * S5_STATE
S5_BWD_CHUNK = 1024


def _cmul(ar, ai, br, bi):
    return ar * br - ai * bi, ar * bi + ai * br


def _a_powers(ar, ai, reverse):
    a2 = _cmul(ar, ai, ar, ai)
    a4 = _cmul(*a2, *a2)
    row = lax.broadcasted_iota(jnp.int32, (8, ar.shape[1]), 0)
    e = (8 - row) if reverse else (row + 1)
    tr, ti = jnp.ones((8, ar.shape[1]), F32), jnp.zeros((8, ar.shape[1]), F32)
    for bit, (pr, pi) in ((1, (ar, ai)), (2, a2), (4, a4), (8, _cmul(*a4, *a4))):
        nr, ni = _cmul(tr, ti, pr, pi)
        sel = (e & bit) != 0
        tr, ti = jnp.where(sel, nr, tr), jnp.where(sel, ni, ti)
    pows = []
    for d, (pr, pi) in zip((1, 2, 4), ((ar, ai), a2, a4)):
        keep = (row < 8 - d) if reverse else (row >= d)
        pows.append((jnp.where(keep, pr, 0.0), jnp.where(keep, pi, 0.0)))
    return pows, (tr, ti)


def _scan8(xr, xi, pows, table, cr, ci, reverse):
    for d, (pr, pi) in zip((1, 2, 4), pows):
        shift = 8 - d if reverse else d
        mr, mi = _cmul(pr, pi, pltpu.roll(xr, shift, 0), pltpu.roll(xi, shift, 0))
        xr, xi = xr + mr, xi + mi
    mr, mi = _cmul(table[0], table[1], cr, ci)
    return xr + mr, xi + mi


def _row_of(x, r):
    row = lax.broadcasted_iota(jnp.int32, x.shape, 0)
    return jnp.sum(jnp.where(row == r, x, 0.0), axis=0, keepdims=True)


def _s5_scan_fwd(h_re, h_im, ar, ai, L):
    pows, table = _a_powers(ar, ai, False)

    def step(i, carry):
        r = pl.ds(pl.multiple_of(i * 8, 8), 8)
        xr, xi = _scan8(h_re[r, :], h_im[r, :], pows, table, carry[0], carry[1], False)
        h_re[r, :] = xr
        h_im[r, :] = xi
        return xr[7:8, :], xi[7:8, :]

    z = jnp.zeros((1, ar.shape[1]), F32)
    lax.fori_loop(0, L // 8, step, (z, z))


def _s5_specs(L):
    return [pl.BlockSpec((L, 128), lambda g: (0, g)),
            pl.BlockSpec((1, 128, S5_LANES), lambda g: (g, 0, 0)), pl.BlockSpec((1, 128, S5_LANES), lambda g: (g, 0, 0)),
            pl.BlockSpec((1, 1, S5_LANES), lambda g: (g, 0, 0)), pl.BlockSpec((1, 1, S5_LANES), lambda g: (g, 0, 0)),
            pl.BlockSpec((1, S5_LANES, 128), lambda g: (g, 0, 0)), pl.BlockSpec((1, S5_LANES, 128), lambda g: (g, 0, 0))]


def s5_fwd(u, w_re, w_im, a_re, a_im, c_re, c_im):
    L, D = u.shape

    def body(u_ref, wr, wi, ar, ai, cr, ci, y_ref, h_re, h_im):
        ub = u_ref[...]
        h_re[...] = _dot(ub, wr[0], _NN)
        h_im[...] = _dot(ub, wi[0], _NN)
        _s5_scan_fwd(h_re, h_im, ar[0], ai[0], L)
        y_ref[...] = _dot(h_re[...], cr[0], _NN) - _dot(h_im[...], ci[0], _NN)

    return pl.pallas_call(
        body, name="s5_fwd", grid=(D // 128,),
        in_specs=_s5_specs(L), out_specs=pl.BlockSpec((L, 128), lambda g: (0, g)),
        out_shape=jax.ShapeDtypeStruct((L, D), F32),
        scratch_shapes=[pltpu.VMEM((L, S5_LANES), F32), pltpu.VMEM((L, S5_LANES), F32)],
        compiler_params=_cparams(("parallel",)),
    )(u, w_re, w_im, a_re, a_im, c_re, c_im)


def s5_bwd(u, w_re, w_im, a_re, a_im, c_re, c_im, dy, du_plus, tc):
    L, D = u.shape
    nch = L // tc

    def body(u_ref, wr, wi, ar_ref, ai_ref, cr, ci, dy_ref, plus_ref, du_ref, dwr, dwi, dar, dai, dcr, dci, h_re, h_im,
             g_re, g_im):
        ar, ai = ar_ref[0], ai_ref[0]
        ub = u_ref[...]
        h_re[...] = _dot(ub, wr[0], _NN)
        h_im[...] = _dot(ub, wi[0], _NN)
        _s5_scan_fwd(h_re, h_im, ar, ai, L)
        dyb = dy_ref[...]
        dcr[0] = _dot(h_re[...], dyb, _TN)
        dci[0] = -_dot(h_im[...], dyb, _TN)
        pows, table = _a_powers(ar, -ai, True)
        dwr[0] = jnp.zeros((128, S5_LANES), F32)
        dwi[0] = jnp.zeros((128, S5_LANES), F32)
        z1 = jnp.zeros((1, S5_LANES), F32)
        z8 = jnp.zeros((8, S5_LANES), F32)

        def chunk(cc, carry):
            c0 = pl.multiple_of((nch - 1 - cc) * tc, tc)
            rows_c = pl.ds(c0, tc)
            dyc = dy_ref[rows_c, :]
            g_re[...] = _dot(dyc, cr[0], _NT)
            g_im[...] = -_dot(dyc, ci[0], _NT)

            def step(ii, cy):
                gr_c, gi_c, acc_r, acc_i = cy
                i8 = pl.multiple_of((tc // 8 - 1 - ii) * 8, 8)
                rl = pl.ds(i8, 8)
                xr, xi = _scan8(g_re[rl, :], g_im[rl, :], pows, table, gr_c, gi_c, True)
                g_re[rl, :] = xr
                g_im[rl, :] = xi
                t0 = c0 + i8
                hb_r, hb_i = h_re[pl.ds(t0, 8), :], h_im[pl.ds(t0, 8), :]
                tp = pl.multiple_of(jnp.maximum(t0 - 8, 0), 8)
                first = (t0 > 0).astype(F32)
                pr = h_re[pl.ds(tp, 8), :][7:8, :] * first
                pi = h_im[pl.ds(tp, 8), :][7:8, :] * first
                row = lax.broadcasted_iota(jnp.int32, xr.shape, 0)
                hp_r = jnp.where(row == 0, pr, pltpu.roll(hb_r, 1, 0))
                hp_i = jnp.where(row == 0, pi, pltpu.roll(hb_i, 1, 0))
                return (xr[0:1, :], xi[0:1, :],
                        acc_r + xr * hp_r + xi * hp_i, acc_i + xi * hp_r - xr * hp_i)

            cy = lax.fori_loop(0, tc // 8, step, carry)
            uc = u_ref[rows_c, :]
            gr, gi = g_re[...], g_im[...]
            du_ref[rows_c, :] = _dot(gr, wr[0], _NT) + _dot(gi, wi[0], _NT) + plus_ref[rows_c, :]
            dwr[0] += _dot(uc, gr, _TN)
            dwi[0] += _dot(uc, gi, _TN)
            return cy

        _, _, acc_r, acc_i = lax.fori_loop(0, nch, chunk, (z1, z1, z8, z8))
        dar[0] = jnp.sum(acc_r, axis=0, keepdims=True)
        dai[0] = jnp.sum(acc_i, axis=0, keepdims=True)

    specs = _s5_specs(L)
    return pl.pallas_call(
        body, name="s5_bwd", grid=(D // 128,),
        in_specs=specs + [pl.BlockSpec((L, 128), lambda g: (0, g))] * 2,
        out_specs=[pl.BlockSpec((L, 128), lambda g: (0, g))] + specs[1:],
        out_shape=[jax.ShapeDtypeStruct((L, D), F32)] + [jax.ShapeDtypeStruct(x.shape, F32)
                                                        for x in (w_re, w_im, a_re, a_im, c_re, c_im)],
        scratch_shapes=[pltpu.VMEM((L, S5_LANES), F32), pltpu.VMEM((L, S5_LANES), F32),
                        pltpu.VMEM((tc, S5_LANES), F32), pltpu.VMEM((tc, S5_LANES), F32)],
        compiler_params=_cparams(("parallel",)),
    )(u, w_re, w_im, a_re, a_im, c_re, c_im, dy, du_plus)


def _s5_discretize(lr, li, ldt, br, bi):
    dt = jnp.exp(ldt)
    mag = jnp.exp(lr * dt)
    ar, ai = mag * jnp.cos(li * dt), mag * jnp.sin(li * dt)
    den = lr * lr + li * li
    zr = ((ar - 1.0) * lr + ai * li) / den
    zi = (ai * lr - (ar - 1.0) * li) / den
    p = lax.broadcasted_iota(jnp.int32, (S5_STATE, S5_STATE * S5_GROUP), 0)
    c = lax.broadcasted_iota(jnp.int32, (S5_STATE, S5_STATE * S5_GROUP), 1)
    rep = (c // S5_GROUP == p).astype(F32)
    zr, zi = _dot(zr, rep, _NN, HI), _dot(zi, rep, _NN, HI)
    return ar, ai, zr * br - zi * bi, zr * bi + zi * br


def _conv_shift(x, d):
    row = lax.broadcasted_iota(jnp.int32, x.shape, 0)
    return jnp.where(row >= d, pltpu.roll(x, d, 0), 0.0)


def _conv_unshift(x, d):
    n = x.shape[0]
    row = lax.broadcasted_iota(jnp.int32, x.shape, 0)
    return jnp.where(row < n - d, pltpu.roll(x, n - d, 0), 0.0)


@functools.partial(jax.custom_vjp, nondiff_argnums=(1,))
def _shift_rows(x, d):
    return _conv_shift(x, d)


_shift_rows.defvjp(lambda x, d: (_conv_shift(x, d), None), lambda d, _, g: (_conv_unshift(g, d),))


def _conv_gate(ug, uv, wg0, wg1, wg2, wv0, wv1, wv2, bg, bv):
    def conv(u, w0, w1, w2, b):
        return u * w2 + _shift_rows(u, 1) * w1 + _shift_rows(u, 2) * w0 + b
    return (jax.nn.silu(conv(ug, wg0, wg1, wg2, bg)) * conv(uv, wv0, wv1, wv2, bv),)


def _rms_fn(x, gain):
    return (_rms(x, gain),)


def _softmax_rows(s):
    e = jnp.exp(s - lax.stop_gradient(jnp.max(s, axis=-1, keepdims=True)))
    return e / jnp.sum(e, axis=-1, keepdims=True)


def _xa_core(qp, k, v, q_gain):
    dh = qp.shape[1] // XA_HEADS
    outs = []
    for h in range(XA_HEADS):
        sl = slice(h * dh, (h + 1) * dh)
        p = _softmax_rows(_dot(_rms(qp[:, sl], q_gain), k[:, sl], _NT) * (dh ** -0.5))
        outs.append(_dot(p, v[:, sl], _NN))
    return (jnp.concatenate(outs, axis=1),)


def _mem_kv(mem, mem_gain, wk, wv, k_gain):
    m = _rms(mem, mem_gain)
    kp = _dot(m, wk, _NN)
    dh = kp.shape[1] // XA_HEADS
    k = jnp.concatenate([_rms(kp[:, h * dh:(h + 1) * dh], k_gain) for h in range(XA_HEADS)], axis=1)
    return k, _dot(m, wv, _NN)


def _s5_post(y, u, d):
    return (jax.nn.gelu(y + d * u),)


def _glu(a, b):
    return (a * jax.nn.sigmoid(b),)


def _lb_first(logits):
    e = jnp.exp(logits - lax.stop_gradient(jnp.max(logits, axis=0, keepdims=True)))
    return (_row_of(e, 0) / jnp.sum(e, axis=0, keepdims=True),)


def _loss_fn(y, t):
    e = y - t
    part = 0.5 * jnp.sum(e * e) / y.shape[1]
    return e * (1.0 / y.shape[1]), jnp.full((8, 128), part / (8 * 128), F32)


def _out(shape, dtype, tm):
    return (shape, dtype, (tm, shape[1]), lambda i: (i, 0))


LIGHT_ROWS = 512


def rms_fwd(h, gain, dtype):
    tm = min(LIGHT_ROWS, h.shape[0])
    return blocked_fwd(_rms_fn, [rows(h, tm), full(gain)], [_out(h.shape, dtype, tm)], h.shape[0] // tm, "rms_fwd")[0]


def rms_bwd(h, gain, dy, residual):
    tm = min(LIGHT_ROWS, h.shape[0])
    return blocked_bwd(_rms_fn, [rows(h, tm, 'blk'), full(gain, 'acc')], [rows(dy, tm)], h.shape[0] // tm, "rms_bwd",
                       plus=rows(residual, tm))


def _adamw_math(w, g, m, v):
    m = ADAM_B1 * m + (1.0 - ADAM_B1) * g
    v = ADAM_B2 * v + (1.0 - ADAM_B2) * jnp.square(g)
    m_hat = m / (1.0 - ADAM_B1 ** ADAM_STEP)
    v_hat = v / (1.0 - ADAM_B2 ** ADAM_STEP)
    return -ADAM_LR * (m_hat / (jnp.sqrt(v_hat) + ADAM_EPS) + ADAM_WD * w), m, v


def adamw(w, g, m, v, name):
    R = w.shape[0]
    tm = _tile(R, 256)
    assert g.shape == w.shape == m.shape == v.shape, (name, w.shape, g.shape)

    def body(w_ref, g_ref, m_ref, v_ref, d_ref, nm_ref, nv_ref):
        d_ref[...], nm_ref[...], nv_ref[...] = _adamw_math(w_ref[...], g_ref[...], m_ref[...], v_ref[...])

    spec = pl.BlockSpec((tm, w.shape[1]), lambda i: (i, 0))
    return pl.pallas_call(
        body, name=name, grid=(R // tm,), in_specs=[spec] * 4, out_specs=[spec] * 3,
        out_shape=[jax.ShapeDtypeStruct(w.shape, F32)] * 3, compiler_params=_cparams(("parallel",)),
    )(w, g, m, v)


def _index_operand(i):
    return jnp.reshape(i, (1,)).astype(jnp.int32)


def adamw_layer(w, mine, other, core, m, v, layer, bufs, name):
    H, C = mine.shape
    tm = _tile(H, 256)
    nb = H // tm
    assert w.shape[1:] == (2 * H, C) and all(b.shape == w.shape for b in bufs), (name, w.shape, mine.shape)

    def body(c_ref, w_ref, mine_ref, other_ref, m_ref, v_ref, *rest):
        g_out, d_ref, nm_ref, nv_ref = rest[-4:]
        g_ = jnp.where(pl.program_id(0) // nb == c_ref[0], mine_ref[...], other_ref[...])
        g_out[...] = g_
        d_ref[...], nm_ref[...], nv_ref[...] = _adamw_math(w_ref[...], g_, m_ref[...], v_ref[...])

    lspec = pl.BlockSpec((None, tm, C), lambda i, c: (layer, i, 0))
    half = lambda sign: pl.BlockSpec(
        (tm, C), lambda i, c: (jnp.clip(i - (c[0] if sign else 1 - c[0]) * nb, 0, nb - 1), 0))
    any_spec = pl.BlockSpec(memory_space=pl.ANY)
    grid_spec = pltpu.PrefetchScalarGridSpec(
        num_scalar_prefetch=1, grid=(2 * nb,),
        in_specs=[lspec, half(True), half(False), lspec, lspec] + [any_spec] * 4, out_specs=[lspec] * 4)
    return pl.pallas_call(
        body, name=name, grid_spec=grid_spec, out_shape=[jax.ShapeDtypeStruct(w.shape, F32)] * 4,
        input_output_aliases={6: 0, 7: 1, 8: 2, 9: 3}, compiler_params=_cparams(("parallel",)),
    )(_index_operand(core), w, mine, other, m, v, *bufs)


def add_own_half(x, core, theirs, name, out_dtype):
    nq, _, H, C = x.shape
    tm = _tile(H, 256)

    def body(c_ref, x_ref, t_ref, o_ref):
        o_ref[...] = (x_ref[...] + t_ref[...]).astype(o_ref.dtype)

    spec = pl.BlockSpec((None, tm, C), lambda q, i, c: (q, i, 0))
    grid_spec = pltpu.PrefetchScalarGridSpec(
        num_scalar_prefetch=1, grid=(nq, H // tm),
        in_specs=[pl.BlockSpec((None, None, tm, C), lambda q, i, c: (q, c[0], i, 0)), spec], out_specs=spec)
    return pl.pallas_call(
        body, name=name, grid_spec=grid_spec, out_shape=jax.ShapeDtypeStruct((nq, H, C), out_dtype),
        compiler_params=_cparams(("parallel", "parallel")),
    )(_index_operand(core), x, theirs)


def add_chips(pair, chip, got, name):
    n, R, C = got.shape
    tm = _tile(R, 256)

    def body(q_ref, *refs):
        acc = refs[0][...].astype(F32)
        for r in refs[1:-1]:
            acc = acc + r[...].astype(F32)
        refs[-1][...] = acc

    grid_spec = pltpu.PrefetchScalarGridSpec(
        num_scalar_prefetch=1, grid=(R // tm,),
        in_specs=[pl.BlockSpec((None, tm, C), lambda i, q: (q[0], i, 0))]
        + [pl.BlockSpec((None, tm, C), lambda i, q, j=j: (j, i, 0)) for j in range(n)],
        out_specs=pl.BlockSpec((tm, C), lambda i, q: (i, 0)))
    return pl.pallas_call(
        body, name=name, grid_spec=grid_spec, out_shape=jax.ShapeDtypeStruct((R, C), F32),
        compiler_params=_cparams(("parallel",)),
    )(_index_operand(chip), pair, *([got] * n))


_HBM = pl.BlockSpec(memory_space=pltpu.HBM)
N_CHIPS = 4


def _my_place():
    return lax.axis_index("x"), lax.axis_index("y"), lax.axis_index("c")


def _comm_call(body, name, xs, out_shapes, n_remote, n_local, sequencer=None):
    sems = [pltpu.SemaphoreType.DMA((n_remote,)), pltpu.SemaphoreType.DMA((n_remote,)),
            pltpu.SemaphoreType.DMA((max(n_local, 1),))]
    if sequencer is None:
        return pl.pallas_call(
            body, name=name, in_specs=[_HBM] * len(xs), out_specs=[_HBM] * len(out_shapes), out_shape=out_shapes,
            scratch_shapes=sems, compiler_params=pltpu.CompilerParams(has_side_effects=True),
        )(*xs)
    peers_of, collective_id = sequencer
    hbm = pltpu.MemorySpace.HBM
    x_refs = [jax.new_ref(x, memory_space=hbm) for x in xs]
    o_refs = [jax.empty_ref(s, memory_space=hbm) for s in out_shapes]

    @pl.kernel(mesh=plsc.ScalarSubcoreMesh(axis_name="sequencer", num_cores=1), name=name, scratch_types=tuple(sems),
               compiler_params=pltpu.CompilerParams(collective_id=collective_id))
    def launch(send_sems, recv_sems, local_sems):
        peers = peers_of(*_my_place())
        barrier = pltpu.get_barrier_semaphore()
        for peer in peers:
            pl.semaphore_signal(barrier, inc=1, device_id=peer, device_id_type=MESH)
        pl.semaphore_wait(barrier, len(peers))
        body(*x_refs, *o_refs, send_sems, recv_sems, local_sems)

    launch()
    return [o[...] for o in o_refs]


def _sibling(mx, my, mc):
    return [(mx, my, 1 - mc)]


def _same_core_of_other_chips(mx, my, mc):
    return [(tx, ty, mc) for tx, ty in _other_chips(mx, my)]


def _run(copies):
    for cp in copies:
        cp.start()
    for cp in copies:
        cp.wait()


def _other_chips(mx, my):
    return [(mx ^ (j >> 1), my ^ (j & 1)) for j in (1, 2, 3)]


def device_gather(x, name):
    H, C = x.shape

    def body(x_ref, o_ref, send_sems, recv_sems, local_sems):
        mx, my, mc = _my_place()
        dst = o_ref.at[2 * mx + my, mc]
        copies = [pltpu.make_async_copy(x_ref, dst, local_sems.at[0])]
        others = [(mx, my, 1 - mc)] + [(tx, ty, c) for tx, ty in _other_chips(mx, my) for c in (mc, 1 - mc)]
        for j, peer in enumerate(others):
            copies.append(pltpu.make_async_remote_copy(src_ref=x_ref, dst_ref=dst, send_sem=send_sems.at[j],
                                                       recv_sem=recv_sems.at[j], device_id=peer, device_id_type=MESH))
        _run(copies)

    out = _comm_call(body, name, [x], [jax.ShapeDtypeStruct((N_CHIPS, 2, H, C), x.dtype)], 7, 1)[0]
    return out.reshape(N_CHIPS * 2 * H, C)


def gather_two_level(xs, name):
    n = len(xs)
    shapes = [jax.ShapeDtypeStruct((2, N_CHIPS) + x.shape[1:], x.dtype) for x in xs]

    def body(*refs):
        x_refs, o_refs = refs[:n], refs[n:2 * n]
        send_sems, recv_sems, local_sems = refs[2 * n:]
        mx, my, mc = _my_place()
        q = 2 * mx + my
        first, local, second = [], [], []
        for i, (x_ref, o_ref) in enumerate(zip(x_refs, o_refs)):
            local.append(pltpu.make_async_copy(x_ref.at[mc], o_ref.at[mc, q], local_sems.at[i]))
            for j, (tx, ty) in enumerate(_other_chips(mx, my)):
                first.append(pltpu.make_async_remote_copy(
                    src_ref=x_ref.at[mc], dst_ref=o_ref.at[mc, q], send_sem=send_sems.at[4 * i + j],
                    recv_sem=recv_sems.at[4 * i + j], device_id=(tx, ty, mc), device_id_type=MESH))
            second.append(pltpu.make_async_remote_copy(
                src_ref=o_ref.at[mc], dst_ref=o_ref.at[mc], send_sem=send_sems.at[4 * i + 3],
                recv_sem=recv_sems.at[4 * i + 3], device_id=(mx, my, 1 - mc), device_id_type=MESH))
        for cp in local + first:
            cp.start()
        for cp in local:
            cp.wait()
        for cp in first:
            cp.wait_recv()
        _run(second)
        for cp in first:
            cp.wait_send()

    return _comm_call(body, name, xs, shapes, 4 * n, n)


def gather_two_level_sequencer(xs, name, collective_id):
    n = len(xs)
    hbm = pltpu.MemorySpace.HBM
    x_refs = [jax.new_ref(x, memory_space=hbm) for x in xs]
    o_refs = [jax.empty_ref(jax.ShapeDtypeStruct((2, N_CHIPS) + x.shape[1:], x.dtype), memory_space=hbm) for x in xs]

    @pl.kernel(mesh=plsc.ScalarSubcoreMesh(axis_name="sequencer", num_cores=1), name=name,
               scratch_types=(pltpu.SemaphoreType.DMA((4 * n,)), pltpu.SemaphoreType.DMA((4 * n,)),
                              pltpu.SemaphoreType.DMA((n,))),
               compiler_params=pltpu.CompilerParams(collective_id=collective_id))
    def launch(send_sems, recv_sems, local_sems):
        mx, my, mc = _my_place()
        peers = [(tx, ty, mc) for tx, ty in _other_chips(mx, my)] + [(mx, my, 1 - mc)]
        barrier = pltpu.get_barrier_semaphore()
        for peer in peers:
            pl.semaphore_signal(barrier, inc=1, device_id=peer, device_id_type=MESH)
        pl.semaphore_wait(barrier, len(peers))
        q = 2 * mx + my
        first, local, second = [], [], []
        for i, (x_ref, o_ref) in enumerate(zip(x_refs, o_refs)):
            local.append(pltpu.make_async_copy(x_ref.at[mc], o_ref.at[mc, q], local_sems.at[i]))
            for j, peer in enumerate(peers[:3]):
                first.append(pltpu.make_async_remote_copy(
                    src_ref=x_ref.at[mc], dst_ref=o_ref.at[mc, q], send_sem=send_sems.at[4 * i + j],
                    recv_sem=recv_sems.at[4 * i + j], device_id=peer, device_id_type=MESH))
            second.append(pltpu.make_async_remote_copy(
                src_ref=o_ref.at[mc], dst_ref=o_ref.at[mc], send_sem=send_sems.at[4 * i + 3],
                recv_sem=recv_sems.at[4 * i + 3], device_id=peers[3], device_id_type=MESH))
        for cp in local + first:
            cp.start()
        for cp in local:
            cp.wait()
        for cp in first:
            cp.wait_recv()
        _run(second)
        for cp in first:
            cp.wait_send()

    launch()
    return [o[...] for o in o_refs]


def pair_swap(xs, name, halves, collective_id=None):
    n = len(xs)
    shapes = [jax.ShapeDtypeStruct(x.shape[:1] + x.shape[2:] if halves else x.shape, x.dtype) for x in xs]

    def body(*refs):
        x_refs, o_refs = refs[:n], refs[n:2 * n]
        send_sems, recv_sems, _ = refs[2 * n:]
        mx, my, mc = _my_place()
        _run([pltpu.make_async_remote_copy(
            src_ref=x_ref.at[:, 1 - mc] if halves else x_ref, dst_ref=o_ref, send_sem=send_sems.at[i],
            recv_sem=recv_sems.at[i], device_id=(mx, my, 1 - mc), device_id_type=MESH)
            for i, (x_ref, o_ref) in enumerate(zip(x_refs, o_refs))])

    return _comm_call(body, name, xs, shapes, n, 0, None if collective_id is None else (_sibling, collective_id))


def chip_all_to_all(xs, name, collective_id=None):
    n = len(xs)
    shapes = [jax.ShapeDtypeStruct((N_CHIPS - 1,) + x.shape[1:], x.dtype) for x in xs]

    def body(*refs):
        x_refs, o_refs = refs[:n], refs[n:2 * n]
        send_sems, recv_sems, _ = refs[2 * n:]
        mx, my, mc = _my_place()
        copies = []
        for i, (x_ref, o_ref) in enumerate(zip(x_refs, o_refs)):
            for j, (tx, ty) in enumerate(_other_chips(mx, my)):
                copies.append(pltpu.make_async_remote_copy(
                    src_ref=x_ref.at[2 * tx + ty], dst_ref=o_ref.at[j], send_sem=send_sems.at[3 * i + j],
                    recv_sem=recv_sems.at[3 * i + j], device_id=(tx, ty, mc), device_id_type=MESH))
        _run(copies)

    return _comm_call(body, name, xs, shapes, 3 * n, 0,
                      None if collective_id is None else (_same_core_of_other_chips, collective_id))


WEIGHTS = ['norm_mix', 'norm_xa', 'norm_mem', 'norm_ffn', 'xa_wq', 'xa_wk', 'xa_wv', 'xa_wo', 'xa_q_norm', 'xa_k_norm',
           'ffn_w_up', 'ffn_conv_w', 'ffn_conv_b', 'ffn_w_down', 'hg_lb_logits', 'mix_w_in', 'hg_out_norm',
           'mla_q_a_norm', 'mla_w_uq', 'mla_kv_a_norm', 'mla_w_ukv', 'mla_qn_nope', 'mla_qn_rope', 'mla_kn_nope',
           'mla_kn_rope', 'mix_w_out', 's5_lam_re', 's5_lam_im', 's5_log_dt', 's5_b_re', 's5_b_im', 's5_c_re',
           's5_c_im', 's5_d', 's5_w_glu_a', 's5_w_glu_b']
INPUTS = ['x', 'mem', 'positions'] + WEIGHTS + ['loss_target'] + ['m_' + n for n in WEIGHTS] + ['v_' + n for n in WEIGHTS]
SHARD_AXIS = {'xa_wq': 1, 'xa_wk': 1, 'xa_wv': 1, 'xa_wo': 1, 'ffn_w_up': 2, 'ffn_conv_w': 2, 'ffn_w_down': 1,
              'mix_w_in': 2, 'mla_w_uq': 2, 'mla_w_ukv': 2, 'mix_w_out': 1, 's5_d': 1, 's5_w_glu_a': 1, 's5_w_glu_b': 1}
BIG = ['xa_wq', 'xa_wk', 'xa_wv', 'xa_wo', 'ffn_w_up', 'ffn_w_down', 'mix_w_in', 'mix_w_out', 's5_w_glu_a', 's5_w_glu_b']
FIRST_NEEDED = ('mix_w_in', 'mix_w_out')
SMALL_SHARDED = [n for n in WEIGHTS if n in SHARD_AXIS and n not in BIG]
REPLICATED = [n for n in WEIGHTS if n not in SHARD_AXIS]
SMALL = SMALL_SHARDED + REPLICATED
PACK_W = 1024
ROW_MULT = 16
W_IN_SHARD = IN_WIDTH // N_CHIPS
W_IN_SHARD_PAD = 640


def _pack(flats, mult=ROW_MULT):
    flat = jnp.concatenate([f.reshape(-1) for f in flats])
    unit = mult * PACK_W
    n = -(-flat.shape[0] // unit) * unit
    return jnp.pad(flat, (0, n - flat.shape[0])).reshape(n // PACK_W, PACK_W)


def _unpack(packed, shapes):
    flat, out, o = packed.reshape(-1), [], 0
    for s in shapes:
        n = math.prod(s)
        out.append(flat[o:o + n].reshape(s))
        o += n
    return out


def _rope_pad(w):
    z = jnp.zeros(w.shape[:-1] + (MLA_ROPE // 2,), w.dtype)
    return jnp.concatenate([w[..., :MLA_ROPE // 2], z, w[..., MLA_ROPE // 2:], z], axis=-1)


def _rope_unpad(g):
    return jnp.concatenate([g[..., :MLA_ROPE // 2], g[..., 64:64 + MLA_ROPE // 2]], axis=-1)


def _blockdiag_in(bb):
    nb = bb.shape[0] // S5_GB
    t = bb.reshape(nb, S5_GB, S5_STATE, S5_GROUP).transpose(0, 1, 3, 2)
    return jnp.einsum('bgmp,gh->bgmhp', t, jnp.eye(S5_GB, dtype=bb.dtype)).reshape(nb, S5_GB * S5_GROUP, S5_LANES)


def _blockdiag_in_t(dw):
    nb = dw.shape[0]
    t = jnp.einsum('bgmhp,gh->bgmp', dw.reshape(nb, S5_GB, S5_GROUP, S5_GB, S5_STATE), jnp.eye(S5_GB, dtype=dw.dtype))
    return t.transpose(0, 1, 3, 2).reshape(nb * S5_GB, S5_STATE, S5_GROUP)


def _blockdiag_out(c):
    nb = c.shape[0] // S5_GB
    t = c.reshape(nb, S5_GB, S5_GROUP, S5_STATE).transpose(0, 1, 3, 2)
    return jnp.einsum('bgpm,gh->bgphm', t, jnp.eye(S5_GB, dtype=c.dtype)).reshape(nb, S5_LANES, S5_GB * S5_GROUP)


def _blockdiag_out_t(dc):
    nb = dc.shape[0]
    t = jnp.einsum('bgphm,gh->bgpm', dc.reshape(nb, S5_GB, S5_STATE, S5_GB, S5_GROUP), jnp.eye(S5_GB, dtype=dc.dtype))
    return t.transpose(0, 1, 3, 2).reshape(nb * S5_GB, S5_GROUP, S5_STATE)


def _gather_weights(P):
    def halves(x):
        return x if x.shape[0] == 2 else x.reshape(2, x.shape[1] // 2, x.shape[2])

    now = [n for n in BIG if n in FIRST_NEEDED]
    later = [n for n in BIG if n not in FIRST_NEEDED]
    xs = [halves(P[n].astype(BF16)) for n in now] + [halves(_pack([P[n] for n in SMALL_SHARDED], 2 * ROW_MULT)[None])]
    got = gather_two_level(xs, "gather_weights")
    got, xs_later = lax.optimization_barrier((got, [halves(P[n].astype(BF16)) for n in later]))
    got_later = gather_two_level_sequencer(xs_later, "gather_weights_later", 1)
    full_w = {}
    for n, g in list(zip(now, got[:-1])) + list(zip(later, got_later)):
        two_layers, by_rows = P[n].shape[0] == 2, SHARD_AXIS[n] == 1
        if two_layers and by_rows:
            full_w[n] = g.reshape(2, N_CHIPS * g.shape[2], g.shape[3])
        elif two_layers:
            full_w[n] = g.transpose(0, 2, 1, 3).reshape(2, g.shape[2], N_CHIPS * g.shape[3])
        elif by_rows:
            full_w[n] = g.transpose(1, 0, 2, 3).reshape(1, 2 * N_CHIPS * g.shape[2], g.shape[3])
        else:
            full_w[n] = g.transpose(0, 2, 1, 3).reshape(1, 2 * g.shape[2], N_CHIPS * g.shape[3])
    small = got[-1].transpose(1, 0, 2, 3).reshape(N_CHIPS, -1, PACK_W)
    per_chip = [_unpack(small[q], [P[n].shape for n in SMALL_SHARDED]) for q in range(N_CHIPS)]
    for i, n in enumerate(SMALL_SHARDED):
        full_w[n] = jnp.concatenate([per_chip[q][i] for q in range(N_CHIPS)], axis=SHARD_AXIS[n])
    return full_w


def _reduce_batch(items, small, P, results, tag, ids):
    mx, my, mc = _my_place()
    q = 2 * mx + my
    ids = ids or {}
    names = [f"{n}_{lyr}" for n, lyr, _ in items] + (['small'] if small is not None else [])
    xs = [g.reshape(N_CHIPS, 2, g.shape[1] // 2, g.shape[2]) for g in [g for _, _, g in items] + ([small] if small is not None else [])]
    def after(vals, tie):
        return (vals, None) if tie is None else lax.optimization_barrier((vals, tie))

    theirs = pair_swap(xs, "grads_pair_swap_" + tag, True, ids.get('swap'))
    theirs, tie = after(theirs, (yield None))
    pair = [add_own_half(x, mc, t, "grads_pair_sum_" + n, F32 if n == 'small' else BF16) for x, t, n in zip(xs, theirs, names)]
    got = chip_all_to_all(pair, "grads_chip_all_to_all_" + tag, ids.get('a2a'))
    if tie is not None:
        tie, _ = lax.optimization_barrier((tie, pair[-1]))
    got, tie = after(got, (yield pair[-1] if tie is None else tie))
    summed = [add_chips(p, q, g, "grads_chip_sum_" + n) for p, g, n in zip(pair, got, names)]
    other = pair_swap(summed[:len(items)], "grads_pair_join_" + tag, False, ids.get('join'))
    if small is not None:
        results['small_sum'] = device_gather(summed[-1], "grads_small_gather")
    other, tie = after(other, (yield summed[-1] if tie is None else tie))
    for (n, lyr, _), s, o in zip(items, summed, other):
        if n == 'mix_w_in':
            s, o = s[:, :W_IN_SHARD], o[:, :W_IN_SHARD]
        view = (P[n].shape[0], 2 * s.shape[0], P[n].shape[-1])
        bufs = results.get(n) or [lax.empty(view, F32) for _ in range(4)]
        results[n] = adamw_layer(P[n].reshape(view), s, o, mc, P['m_' + n].reshape(view), P['v_' + n].reshape(view), lyr,
                                 bufs, f"adamw_{n}_{lyr}")
    yield tie


def _update_small(small_sum, GS, P):
    mx, my, _ = _my_place()
    q = 2 * mx + my
    g_small = dict(zip(SMALL, _unpack(small_sum, [GS[n].shape for n in SMALL])))
    for n in SMALL_SHARDED:
        s = P[n].shape[SHARD_AXIS[n]]
        g_small[n] = lax.dynamic_slice_in_dim(g_small[n], q * s, s, axis=SHARD_AXIS[n])
    grad, delta, new_m, new_v = {}, {}, {}, {}
    packed = lambda prefix: _pack([P[prefix + n] for n in SMALL])
    d, m_, v_ = adamw(packed(''), _pack([g_small[n] for n in SMALL]), packed('m_'), packed('v_'), "adamw_small")
    shapes = [P[n].shape for n in SMALL]
    grad.update(g_small)
    for out, pk in ((delta, d), (new_m, m_), (new_v, v_)):
        out.update(zip(SMALL, _unpack(pk, shapes)))
    return grad, delta, new_m, new_v


def _row(v):
    return v.reshape(1, -1)


def _xattn_fwd(h, mem, W, lyr, tm):
    g_xa, g_mem = _row(W['norm_xa'][lyr]), _row(W['norm_mem'][lyr])
    g_q, g_k = _row(W['xa_q_norm'][lyr]), _row(W['xa_k_norm'][lyr])
    wq, wk, wv, wo = (W[n][lyr] for n in ('xa_wq', 'xa_wk', 'xa_wv', 'xa_wo'))
    L, D = h.shape
    M = mem.shape[0]
    hx = rms_fwd(h, g_xa, MXU_DTYPE)
    qp = matmul(hx, wq, name="xa_q")
    kv_opds = [full(mem), full(g_mem), full(wk), full(wv), full(g_k)]
    k, v = blocked_fwd(_mem_kv, kv_opds, [((M, D), F32, (M, D), lambda i: (0, 0))] * 2, 1, "xa_mem_kv")
    o = blocked_fwd(_xa_core, [rows(qp, tm), full(k), full(v), full(g_q)], [_out((L, D), MXU_DTYPE, tm)], L // tm,
                    "xa_core")[0]
    out = matmul(o, wo, add=h, name="xa_o")
    return out, (h, hx, qp, k, v, o)


def _xattn_bwd(dout, saved, mem, W, lyr, tm):
    h, hx, qp, k, v, o = saved
    g_xa, g_mem = _row(W['norm_xa'][lyr]), _row(W['norm_mem'][lyr])
    g_q, g_k = _row(W['xa_q_norm'][lyr]), _row(W['xa_k_norm'][lyr])
    wq, wk, wv, wo = (W[n][lyr] for n in ('xa_wq', 'xa_wk', 'xa_wv', 'xa_wo'))
    L = h.shape[0]
    do = matmul(dout, wo, "nt", name="xa_do")
    d_wo = matmul(o, dout, "tn", name="xa_dwo")
    dqp, dk, dv, d_gq = blocked_bwd(_xa_core, [rows(qp, tm, 'blk'), full(k, 'acc'), full(v, 'acc'), full(g_q, 'acc')],
                                    [rows(do, tm)], L // tm, "xa_core_bwd")
    d_wq = matmul(hx, dqp, "tn", name="xa_dwq")
    dhx = matmul(dqp, wq, "nt", name="xa_dhx")
    dh, d_gxa = rms_bwd(h, g_xa, dhx, dout)
    d_gmem, d_wk, d_wv, d_gk = blocked_bwd(
        _mem_kv, [full(mem), full(g_mem, 'acc'), full(wk, 'acc'), full(wv, 'acc'), full(g_k, 'acc')],
        [full(dk), full(dv)], 1, "xa_mem_kv_bwd")
    by_chip = lambda g: g.reshape(N_CHIPS, g.shape[0] // N_CHIPS, g.shape[1])
    grads = {'norm_xa': d_gxa, 'norm_mem': d_gmem, 'xa_q_norm': d_gq, 'xa_k_norm': d_gk,
             'xa_wq': by_chip(d_wq), 'xa_wk': by_chip(d_wk), 'xa_wv': by_chip(d_wv), 'xa_wo': by_chip(d_wo)}
    return dh, grads


def _conv_params(W, lyr):
    cw, cb = W['ffn_conv_w'][lyr], W['ffn_conv_b'][lyr]
    F = cw.shape[1] // 2
    return [cw[0:1, :F], cw[1:2, :F], cw[2:3, :F], cw[0:1, F:], cw[1:2, F:], cw[2:3, F:], _row(cb[:F]), _row(cb[F:])]


def _ffn_fwd(h, W, lyr, tm):
    L, D = h.shape
    w_up, w_down = W['ffn_w_up'][lyr], W['ffn_w_down'][lyr]
    F = w_down.shape[0]
    hf = rms_fwd(h, _row(W['norm_ffn'][lyr]), MXU_DTYPE)
    ug = matmul(hf, w_up[:, :F], name="ffn_up_gate")
    uv = matmul(hf, w_up[:, F:], name="ffn_up_value")
    opds = [cols(ug, 128), cols(uv, 128)] + [cols(p, 128) for p in _conv_params(W, lyr)]
    a = blocked_fwd(_conv_gate, opds, [((L, F), MXU_DTYPE, (L, 128), lambda j: (0, j))], F // 128, "ffn_conv_gate")[0]
    out = matmul(a, w_down, add=h, name="ffn_down")
    return out, (h, hf, ug, uv, a)


def _ffn_bwd(dout, saved, W, lyr, tm):
    h, hf, ug, uv, a = saved
    w_up, w_down = W['ffn_w_up'][lyr], W['ffn_w_down'][lyr]
    F = w_down.shape[0]
    da = matmul(dout, w_down, "nt", name="ffn_da")
    d_wdown = matmul(a, dout, "tn", name="ffn_dwdown")
    opds = [cols(ug, 128, 'blk'), cols(uv, 128, 'blk')] + [cols(p, 128, 'blk') for p in _conv_params(W, lyr)]
    gs = blocked_bwd(_conv_gate, opds, [cols(da, 128)], F // 128, "ffn_conv_gate_bwd")
    dug, duv = gs[0], gs[1]
    d_cw = jnp.concatenate([jnp.concatenate(gs[2:5], axis=0), jnp.concatenate(gs[5:8], axis=0)], axis=1)
    d_cb = jnp.concatenate([gs[8], gs[9]], axis=1)[0]
    half = N_CHIPS // 2
    d_wup = lax.empty((N_CHIPS, hf.shape[1], w_up.shape[1] // N_CHIPS), F32)
    d_wup = matmul(hf, dug, "tn", name="ffn_dwup_gate", into=(d_wup, 0), col_blocks=half)
    d_wup = matmul(hf, duv, "tn", name="ffn_dwup_value", into=(d_wup, half), col_blocks=half)
    dhf = matmul(dug, w_up[:, :F], "nt", name="ffn_dhf_gate")
    dhf = matmul(duv, w_up[:, F:], "nt", add=dhf, name="ffn_dhf_value")
    dh, d_g = rms_bwd(h, _row(W['norm_ffn'][lyr]), dhf, dout)
    d_wdown = d_wdown.reshape(N_CHIPS, F // N_CHIPS, d_wdown.shape[1])
    return dh, {'norm_ffn': d_g, 'ffn_w_up': d_wup, 'ffn_conv_w': d_cw, 'ffn_conv_b': d_cb, 'ffn_w_down': d_wdown}


def _mla_params(W):
    w_uq = W['mla_w_uq'][0].reshape(MLA_Q_RANK, MLA_HEADS, MLA_QK)
    w_uq = jnp.concatenate([w_uq[..., :MLA_NOPE], _rope_pad(w_uq[..., MLA_NOPE:])], axis=-1)
    w_ukv = W['mla_w_ukv'][0].reshape(MLA_KV_RANK, MLA_HEADS, MLA_NOPE + MLA_V)
    w_ukv = jnp.concatenate([w_ukv[..., :MLA_NOPE].reshape(MLA_KV_RANK, -1), w_ukv[..., MLA_NOPE:].reshape(MLA_KV_RANK, -1)],
                            axis=1)
    return [_row(W['mla_q_a_norm'][0]), w_uq.reshape(MLA_Q_RANK, MLA_HEADS * MLA_DK), _row(W['mla_kv_a_norm'][0]), w_ukv,
            _row(W['mla_qn_nope'][0]), _row(_rope_pad(W['mla_qn_rope'][0])), _row(W['mla_kn_nope'][0]),
            _row(_rope_pad(W['mla_kn_rope'][0]))]


def _w_in_padded(W):
    w = W['mix_w_in'][0]
    return jnp.concatenate([w[:, :IN_WIDTH - MLA_ROPE], _rope_pad(w[:, IN_WIDTH - MLA_ROPE:])], axis=1)


def _mixer0_fwd(h, W, cos_p, sin_p, tm):
    L = h.shape[0]
    t = min(ATTN_ROWS, L // ATTN_WIDE)
    hn = rms_fwd(h, _row(W['norm_mix'][0]), MXU_DTYPE)
    proj = matmul(hn, _w_in_padded(W), name="mix_in")
    logits = W['hg_lb_logits']
    lb = blocked_fwd(_lb_first, [full(logits)], [((1, HG_WIDTH), F32, (1, HG_WIDTH), lambda i: (0, 0))], 1, "hg_lb")[0]
    gain = _row(W['hg_out_norm'][0])
    o_hg, states = hgrn2_fwd(proj, lb, gain)
    mp = _mla_params(W)
    q, k, v = mla_prep_fwd(proj, cos_p, sin_p, mp, tm)
    scale = MLA_QK ** -0.5
    o_mla, lse = attn_fwd(q, k, v, scale, t)
    w_out = W['mix_w_out'][0]
    out = matmul(o_hg, w_out[:HG_WIDTH], add=h, name="mix_out_hg")
    out = matmul(o_mla, w_out[HG_WIDTH:], add=out, name="mix_out_mla")
    return out, (h, hn, proj, lb, o_hg, states, q, k, v, o_mla, lse)


def _mixer0_bwd(dout, saved, W, cos_p, sin_p, tm, part_way=None):
    h, hn, proj, lb, o_hg, states, q, k, v, o_mla, lse = saved
    L = h.shape[0]
    t = min(ATTN_ROWS, L // ATTN_WIDE)
    scale = MLA_QK ** -0.5
    w_out = W['mix_w_out'][0]
    gain = _row(W['hg_out_norm'][0])
    do_hg = matmul(dout, w_out[:HG_WIDTH], "nt", name="mix_do_hg")
    do_mla = matmul(dout, w_out[HG_WIDTH:], "nt", name="mix_do_mla")
    d_wout = jnp.concatenate([matmul(o_hg, dout, "tn", name="mix_dwout_hg"), matmul(o_mla, dout, "tn", name="mix_dwout_mla")],
                             axis=0)
    if part_way is not None:
        do_mla = part_way(do_mla)
    dq = attn_bwd_dq(q, k, v, o_mla, lse, do_mla, scale, t)
    dk, dv = attn_bwd_dkv(q, k, v, o_mla, lse, do_mla, scale, t)
    mp = _mla_params(W)
    d_mla, d_qa, d_wuq, d_kva, d_wukv, d_qnn, d_qnr, d_knn, d_knr = mla_prep_bwd(proj, cos_p, sin_p, mp, dq, dk, dv, tm)
    d_hg, d_lb, d_gain = hgrn2_bwd(proj, lb, gain, states, do_hg)
    w_in, n_hg = _w_in_padded(W), 4 * HG_WIDTH
    d_win = jnp.concatenate([matmul(hn, d_hg, "tn", name="mix_dwin_hg"), matmul(hn, d_mla, "tn", name="mix_dwin_mla")], axis=1)
    dhn = matmul(d_hg, w_in[:, :n_hg], "nt", name="mix_dhn_hg")
    dhn = matmul(d_mla, w_in[:, n_hg:], "nt", add=dhn, name="mix_dhn_mla")
    dh, d_g = rms_bwd(h, _row(W['norm_mix'][0]), dhn, dout)
    logits = W['hg_lb_logits']
    d_logits = blocked_bwd(_lb_first, [full(logits, 'acc')], [full(d_lb)], 1, "hg_lb_bwd")[0]
    d_wuq = d_wuq.reshape(MLA_Q_RANK, MLA_HEADS, MLA_DK)
    d_wuq = jnp.concatenate([d_wuq[..., :MLA_NOPE], _rope_unpad(d_wuq[..., MLA_NOPE:])], axis=-1)
    hw = MLA_HEADS * MLA_NOPE
    d_wukv = jnp.concatenate([d_wukv[:, :hw].reshape(MLA_KV_RANK, MLA_HEADS, MLA_NOPE),
                              d_wukv[:, hw:].reshape(MLA_KV_RANK, MLA_HEADS, MLA_V)], axis=-1)
    d_win = jnp.concatenate([d_win[:, :IN_WIDTH - MLA_ROPE], _rope_unpad(d_win[:, IN_WIDTH - MLA_ROPE:])], axis=1)
    d_win = d_win.reshape(d_win.shape[0], N_CHIPS, W_IN_SHARD).transpose(1, 0, 2)
    d_win = jnp.pad(d_win, ((0, 0), (0, 0), (0, W_IN_SHARD_PAD - W_IN_SHARD)))
    d_wout = d_wout.reshape(N_CHIPS, d_wout.shape[0] // N_CHIPS, d_wout.shape[1])
    grads = {'norm_mix': d_g, 'hg_lb_logits': d_logits, 'mix_w_in': d_win, 'hg_out_norm': d_gain,
             'mla_q_a_norm': d_qa, 'mla_w_uq': d_wuq.reshape(1, MLA_Q_RANK, -1), 'mla_kv_a_norm': d_kva,
             'mla_w_ukv': d_wukv.reshape(1, MLA_KV_RANK, -1), 'mla_qn_nope': d_qnn, 'mla_qn_rope': _rope_unpad(d_qnr),
             'mla_kn_nope': d_knn, 'mla_kn_rope': _rope_unpad(d_knr), 'mix_w_out': d_wout}
    return dh,grads


def _s5_inputs(W):
    G = W['s5_lam_re'].shape[1]
    return [W['s5_lam_re'][0], W['s5_lam_im'][0], W['s5_log_dt'][0].reshape(G, 1),
            W['s5_b_re'][0].reshape(G, -1), W['s5_b_im'][0].reshape(G, -1)]


def _mixer1_fwd(h, W, tm):
    L, D = h.shape
    u = rms_fwd(h, _row(W['norm_mix'][1]), F32)
    di = _s5_inputs(W)
    G = di[0].shape[0]
    sq, wide = ((G, S5_STATE), F32, (G, S5_STATE), lambda i: (0, 0)), ((G, S5_STATE * S5_GROUP), F32, (G, S5_STATE * S5_GROUP), lambda i: (0, 0))
    ar, ai, bbr, bbi = blocked_fwd(_s5_discretize, [full(a) for a in di], [sq, sq, wide, wide], 1, "s5_discretize")
    nb = G // S5_GB
    core = (_blockdiag_in(bbr.reshape(G, S5_STATE, S5_GROUP)), _blockdiag_in(bbi.reshape(G, S5_STATE, S5_GROUP)),
            ar.reshape(nb, 1, S5_LANES), ai.reshape(nb, 1, S5_LANES),
            _blockdiag_out(W['s5_c_re'][0]), _blockdiag_out(W['s5_c_im'][0]))
    y = s5_fwd(u, *core)
    d = W['s5_d']
    y2 = blocked_fwd(_s5_post, [rows(y, tm), rows(u, tm), full(d)], [_out((L, D), MXU_DTYPE, tm)], L // tm, "s5_post")[0]
    w_ab = jnp.concatenate([W['s5_w_glu_a'][0], W['s5_w_glu_b'][0]], axis=1)
    ab = matmul(y2, w_ab, name="s5_glu_in")
    out = blocked_fwd(lambda a, b, res: (res + _glu(a, b)[0],),
                      [rows(ab, tm, col=0, width=D), rows(ab, tm, col=1, width=D), rows(h, tm)], [_out((L, D), F32, tm)],
                      L // tm, "s5_glu")[0]
    return out, (h, u, core, y, y2, ab)


def _mixer1_bwd(dout, saved, W, tm):
    h, u, core, y, y2, ab = saved
    L, D = h.shape
    da, db = blocked_bwd(_glu, [rows(ab, tm, 'blk', col=0, width=D), rows(ab, tm, 'blk', col=1, width=D)], [rows(dout, tm)],
                         L // tm, "s5_glu_bwd")
    w_a, w_b = W['s5_w_glu_a'][0], W['s5_w_glu_b'][0]
    dy2 = matmul(da, w_a, "nt", name="s5_dy2_a")
    dy2 = matmul(db, w_b, "nt", add=dy2, name="s5_dy2_b")
    d_wa = matmul(y2, da, "tn", name="s5_dwa")
    d_wb = matmul(y2, db, "tn", name="s5_dwb")
    d = W['s5_d']
    dy, du_skip, d_d = blocked_bwd(_s5_post, [rows(y, tm, 'blk'), rows(u, tm, 'blk'), full(d, 'acc')], [rows(dy2, tm)], L // tm,
                                   "s5_post_bwd")
    du, dwr, dwi, dar, dai, dcr, dci = s5_bwd(u, *core, dy, du_skip, min(S5_BWD_CHUNK, L))
    di = _s5_inputs(W)
    G = di[0].shape[0]
    cts = [dar.reshape(G, S5_STATE), dai.reshape(G, S5_STATE), _blockdiag_in_t(dwr).reshape(G, -1), _blockdiag_in_t(dwi).reshape(G, -1)]
    d_lr, d_li, d_ldt, d_br, d_bi = blocked_bwd(_s5_discretize, [full(a, 'acc') for a in di], [full(c) for c in cts], 1,
                                                "s5_discretize_bwd")
    dh, d_g = rms_bwd(h, _row(W['norm_mix'][1]), du, dout)
    bshape = W['s5_b_re'].shape
    grads = {'norm_mix': d_g, 's5_lam_re': d_lr[None], 's5_lam_im': d_li[None], 's5_log_dt': d_ldt.reshape(1, G),
             's5_b_re': d_br.reshape(bshape), 's5_b_im': d_bi.reshape(bshape), 's5_c_re': _blockdiag_out_t(dcr)[None],
             's5_c_im': _blockdiag_out_t(dci)[None], 's5_d': d_d, 's5_w_glu_a': d_wa.reshape(N_CHIPS, -1, D), 's5_w_glu_b': d_wb.reshape(N_CHIPS, -1, D)}
    return dh,grads


def kernel(x, mem, positions, norm_mix, norm_xa, norm_mem, norm_ffn, xa_wq, xa_wk, xa_wv, xa_wo, xa_q_norm, xa_k_norm, ffn_w_up, ffn_conv_w, ffn_conv_b, ffn_w_down, hg_lb_logits, mix_w_in, hg_out_norm, mla_q_a_norm, mla_w_uq, mla_kv_a_norm, mla_w_ukv, mla_qn_nope, mla_qn_rope, mla_kn_nope, mla_kn_rope, mix_w_out, s5_lam_re, s5_lam_im, s5_log_dt, s5_b_re, s5_b_im, s5_c_re, s5_c_im, s5_d, s5_w_glu_a, s5_w_glu_b, loss_target, m_norm_mix, m_norm_xa, m_norm_mem, m_norm_ffn, m_xa_wq, m_xa_wk, m_xa_wv, m_xa_wo, m_xa_q_norm, m_xa_k_norm, m_ffn_w_up, m_ffn_conv_w, m_ffn_conv_b, m_ffn_w_down, m_hg_lb_logits, m_mix_w_in, m_hg_out_norm, m_mla_q_a_norm, m_mla_w_uq, m_mla_kv_a_norm, m_mla_w_ukv, m_mla_qn_nope, m_mla_qn_rope, m_mla_kn_nope, m_mla_kn_rope, m_mix_w_out, m_s5_lam_re, m_s5_lam_im, m_s5_log_dt, m_s5_b_re, m_s5_b_im, m_s5_c_re, m_s5_c_im, m_s5_d, m_s5_w_glu_a, m_s5_w_glu_b, v_norm_mix, v_norm_xa, v_norm_mem, v_norm_ffn, v_xa_wq, v_xa_wk, v_xa_wv, v_xa_wo, v_xa_q_norm, v_xa_k_norm, v_ffn_w_up, v_ffn_conv_w, v_ffn_conv_b, v_ffn_w_down, v_hg_lb_logits, v_mix_w_in, v_hg_out_norm, v_mla_q_a_norm, v_mla_w_uq, v_mla_kv_a_norm, v_mla_w_ukv, v_mla_qn_nope, v_mla_qn_rope, v_mla_kn_nope, v_mla_kn_rope, v_mix_w_out, v_s5_lam_re, v_s5_lam_im, v_s5_log_dt, v_s5_b_re, v_s5_b_im, v_s5_c_re, v_s5_c_im, v_s5_d, v_s5_w_glu_a, v_s5_w_glu_b):
    P = dict(locals())
    assert sorted(P) == sorted(INPUTS) and norm_mix.shape[0] == 2 and mix_w_in.shape[0] == 1
    x, mem, target = P['x'][0], P['mem'][0], P['loss_target'][0]
    L, D = x.shape
    tm = min(256, L)

    W = {n: P[n] for n in REPLICATED}
    W.update(_gather_weights(P))

    inv_freq = 1.0 / (ROPE_BASE ** (jnp.arange(0, MLA_ROPE, 2, dtype=F32) / MLA_ROPE))
    ang = P['positions'][0].astype(F32)[:, None] * inv_freq
    cos, sin, z = jnp.cos(ang), jnp.sin(ang), jnp.zeros_like(ang)
    cos_p = jnp.concatenate([cos, z, cos, z], axis=1)
    sin_p = jnp.concatenate([-sin, z, sin, z], axis=1)

    h, s_mix0 = _mixer0_fwd(x, W, cos_p, sin_p, tm)
    h, s_xa0 = _xattn_fwd(h, mem, W, 0, tm)
    h, s_ffn0 = _ffn_fwd(h, W, 0, tm)
    h, s_mix1 = _mixer1_fwd(h, W, tm)
    h, s_xa1 = _xattn_fwd(h, mem, W, 1, tm)
    h, s_ffn1 = _ffn_fwd(h, W, 1, tm)
    n = L // tm
    dh, parts = blocked_fwd(_loss_fn, [rows(h, tm), rows(target, tm)],
                            [_out((L, D), F32, tm), ((n * 8, 128), F32, (8, 128), lambda i: (i, 0))], n, "loss")
    loss = lax.psum(jnp.sum(parts), ("x", "y", "c"))

    layered = {}

    def collect(g, lyr):
        for k_, v_ in g.items():
            layered.setdefault(k_, {})[lyr] = v_

    results = {}

    def big_items(lyr, names):
        return [(n_, 0 if P[n_].shape[0] == 1 else lyr, layered[n_][lyr]) for n_ in names]

    per_layer = [n_ for n_ in BIG if P[n_].shape[0] == 2]
    second_mixer = ['s5_w_glu_a', 's5_w_glu_b']
    first_mixer = ['mix_w_in', 'mix_w_out']
    assert sorted(per_layer + second_mixer + first_mixer) == sorted(BIG)

    dh, g = _ffn_bwd(dh, s_ffn1, W, 1, tm)
    collect(g, 1)
    dh, g = _xattn_bwd(dh, s_xa1, mem, W, 1, tm)
    collect(g, 1)
    dh, g = _mixer1_bwd(dh, s_mix1, W, tm)
    collect(g, 1)
    late = _reduce_batch(big_items(1, per_layer + second_mixer), None, P, results, "late_layer", {'swap': 2, 'a2a': 3, 'join': 4})
    next(late)
    dh, g = _ffn_bwd(dh, s_ffn0, W, 0, tm)
    collect(g, 0)
    dh = late.send(dh)
    dh, g = _xattn_bwd(dh, s_xa0, mem, W, 0, tm)
    collect(g, 0)
    mid = _reduce_batch(big_items(0, per_layer), None, P, results, "first_layer", {'swap': 5, 'a2a': 6, 'join': 7})
    next(mid)
    dx, g = _mixer0_bwd(dh, s_mix0, W, cos_p, sin_p, tm, part_way=mid.send)
    collect(g, 0)

    GS = {}
    for name in SMALL:
        by_layer = [layered[name][lyr] for lyr in sorted(layered[name])]
        full_shape = W[name].shape
        GS[name] = (by_layer[0].reshape(full_shape) if len(by_layer) == 1
                    else jnp.stack([g_.reshape(full_shape[1:]) for g_ in by_layer]))
    small = _pack([GS[n_] for n_ in SMALL], 2 * N_CHIPS * ROW_MULT).reshape(N_CHIPS, -1, PACK_W)
    dx = late.send(dx)
    dx = mid.send(dx)
    last = _reduce_batch(big_items(0, first_mixer), small, P, results, "first_mixer", {'swap': 8, 'a2a': 9, 'join': 10})
    next(last)
    late.send(None)
    last_pair_sum = last.send(None)
    mid.send(last_pair_sum)
    mid_name = per_layer[-1]
    results[mid_name] = list(last.send(list(results[mid_name])))
    last.send(None)
    outs = list(_update_small(results['small_sum'], GS, P))
    for k_ in range(4):
        outs[k_].update({n_: results[n_][k_].reshape(P[n_].shape) for n_ in BIG})
    return (loss, dx[None], *[d[n_] for d in outs for n_ in WEIGHTS])
```

```python
import functools
import math

import jax
import jax.numpy as jnp
import numpy as np
from jax import lax
from jax.experimental import pallas as pl
from jax.experimental.pallas import tpu as pltpu
from jax.experimental.pallas import tpu_sc as plsc

F32 = jnp.float32
BF16 = jnp.bfloat16
MXU_DTYPE = BF16
HI = lax.Precision.HIGHEST
V7X_VMEM_LIMIT_BYTES = 56 * 1024 * 1024
EPS = 1e-6
MESH = pl.DeviceIdType.MESH

HG_HEADS, HG_DIM = 4, 128
HG_WIDTH = HG_HEADS * HG_DIM
HG_SUB = 32
HG_BLOCK = 64
MLA_HEADS, MLA_Q_RANK, MLA_KV_RANK = 4, 256, 128
MLA_NOPE, MLA_ROPE, MLA_V = 128, 64, 128
MLA_QK = MLA_NOPE + MLA_ROPE
MLA_DK = 256
ROPE_BASE = 10000.0
IN_WIDTH = 4 * HG_WIDTH + MLA_Q_RANK + MLA_KV_RANK + MLA_ROPE
IN_PAD = 4 * HG_WIDTH + MLA_Q_RANK + MLA_KV_RANK + 128
S5_GROUP, S5_STATE = 16, 64
S5_GB = 8
DT_MIN, DT_MAX = 1e-3, 1e-1
XA_HEADS = 4
CONV_W = 3
ADAM_LR, ADAM_B1, ADAM_B2, ADAM_EPS, ADAM_WD, ADAM_STEP = 0.001, 0.9, 0.999, 1e-08, 0.01, 10


def _cparams(sem):
    return pltpu.CompilerParams(dimension_semantics=sem, vmem_limit_bytes=V7X_VMEM_LIMIT_BYTES)


class Opd:
    def __init__(self, arr, block, imap, grad=None, gshape=None, gimap=None):
        self.arr, self.block, self.imap, self.grad = arr, block, imap, grad
        self.gshape = arr.shape if gshape is None else gshape
        self.gimap = imap if gimap is None else gimap

    def spec(self):
        return pl.BlockSpec(self.block, self.imap)

    def gspec(self):
        return pl.BlockSpec(self.block, self.gimap)


def rows(arr, tm, grad=None, col=0, width=None):
    width = arr.shape[1] if width is None else width
    return Opd(arr, (tm, width), lambda i, c=col: (i, c), grad, (arr.shape[0], width), lambda i: (i, 0))


def cols(arr, tn, grad=None):
    return Opd(arr, (arr.shape[0], tn), lambda j: (0, j), grad)


def full(arr, grad=None):
    return Opd(arr, arr.shape, lambda i: (0, 0), grad)


def _load(ref):
    v = ref[...]
    return v.astype(F32) if jnp.issubdtype(v.dtype, jnp.floating) else v


def blocked_fwd(f, opds, outs, n, name):
    n_in = len(opds)

    def body(*refs):
        ys = f(*[_load(r) for r in refs[:n_in]])
        for r, y in zip(refs[n_in:], ys):
            r[...] = y.astype(r.dtype)

    res = pl.pallas_call(
        body, name=name, grid=(n,),
        in_specs=[o.spec() for o in opds],
        out_specs=[pl.BlockSpec(b, m) for (_, _, b, m) in outs],
        out_shape=[jax.ShapeDtypeStruct(s, d) for (s, d, _, _) in outs],
        compiler_params=_cparams(("parallel",)),
    )(*[o.arr for o in opds])
    return res


def blocked_bwd(f, opds, dys, n, name, plus=None):
    n_in, n_dy = len(opds), len(dys)
    diff = [i for i, o in enumerate(opds) if o.grad]
    extra = [] if plus is None else [plus]

    def body(*refs):
        vals = [_load(r) for r in refs[:n_in]]

        def fd(*dv):
            allv = list(vals)
            for i, v in zip(diff, dv):
                allv[i] = v
            return tuple(f(*allv))

        ys, vjp = jax.vjp(fd, *[vals[i] for i in diff])
        cts = tuple(_load(r).astype(y.dtype) for r, y in zip(refs[n_in:n_in + n_dy], ys))
        gs = list(vjp(cts))
        if extra:
            gs[0] = gs[0] + _load(refs[n_in + n_dy])
        for r, g, i in zip(refs[n_in + n_dy + len(extra):], gs, diff):
            if opds[i].grad == 'acc':
                @pl.when(pl.program_id(0) == 0)
                def _(r=r):
                    r[...] = jnp.zeros(r.shape, r.dtype)
                r[...] += g.astype(r.dtype)
            else:
                r[...] = g.astype(r.dtype)

    any_acc = any(opds[i].grad == 'acc' for i in diff)
    res = pl.pallas_call(
        body, name=name, grid=(n,),
        in_specs=[o.spec() for o in opds + dys + extra],
        out_specs=[opds[i].gspec() for i in diff],
        out_shape=[jax.ShapeDtypeStruct(opds[i].gshape, F32) for i in diff],
        compiler_params=_cparams(("arbitrary" if any_acc else "parallel",)),
    )(*[o.arr for o in opds + dys + extra])
    return res


def _tile(dim, want):
    for t in range(want - want % 16, 0, -16):
        if dim % t == 0:
            return t
    assert dim <= want, (dim, want)
    return dim


MATMUL_VMEM_BUDGET = 40 * 1024 * 1024
MATMUL_ROWS = 512
MATMUL_CHUNK = 1024


def _widest(N, fits):
    for t in range(N - N % 128, 0, -128):
        if N % t == 0 and fits(t):
            return t
    return N


def matmul(a, b, mode="nn", out_dtype=F32, add=None, name="matmul", into=None, col_blocks=None):
    sa, sb, so = a.dtype.itemsize, b.dtype.itemsize, jnp.dtype(out_dtype).itemsize
    has_add = add is not None
    if mode == "tn":
        (K, M), (K2, N) = a.shape, b.shape
        assert K == K2 and not has_add and out_dtype == F32, (a.shape, b.shape)
        tk = _tile(K, MATMUL_CHUNK)
        tn = _widest(N, lambda t: 2 * (tk * M * sa + tk * t * sb + M * t * 4) <= MATMUL_VMEM_BUDGET)
        extra, alias = [], {}
        if into is not None:
            buf, lead = into[0], tuple(into[1:])
            if col_blocks is not None:
                assert N % col_blocks == 0 and (N // col_blocks) % 128 == 0 and tn >= N // col_blocks, (N, col_blocks, tn)
                tn = N // col_blocks
                assert buf.shape[len(lead):] == (M, tn), (buf.shape, lead, M, tn)
                out_spec = pl.BlockSpec((None,) * len(lead) + (M, tn), lambda j, k: lead[:-1] + (lead[-1] + j, 0, 0))
            else:
                assert buf.shape[len(lead):] == (M, N), (buf.shape, lead, M, N)
                out_spec = pl.BlockSpec((None,) * len(lead) + (M, tn), lambda j, k: lead + (0, j))
            out_shape = jax.ShapeDtypeStruct(buf.shape, F32)
            extra, alias = [buf], {2: 0}
        else:
            assert col_blocks is None
            out_spec = pl.BlockSpec((M, tn), lambda j, k: (0, j))
            out_shape = jax.ShapeDtypeStruct((M, N), F32)

        def body(a_ref, b_ref, *rest):
            o_ref = rest[-1]
            r = lax.dot_general(a_ref[...].astype(MXU_DTYPE), b_ref[...].astype(MXU_DTYPE), ((_TN), ((), ())),
                                preferred_element_type=F32)

            @pl.when(pl.program_id(1) == 0)
            def _():
                o_ref[...] = r

            @pl.when(pl.program_id(1) > 0)
            def _():
                o_ref[...] += r

        return pl.pallas_call(
            body, name=name, grid=(N // tn, K // tk),
            in_specs=[pl.BlockSpec((tk, M), lambda j, k: (k, 0)), pl.BlockSpec((tk, tn), lambda j, k: (k, j))]
            + [pl.BlockSpec(memory_space=pl.ANY)] * len(extra),
            out_specs=out_spec, out_shape=out_shape, input_output_aliases=alias,
            compiler_params=_cparams(("parallel", "arbitrary")),
        )(a, b, *extra)

    (M, K) = a.shape
    N = b.shape[1] if mode == "nn" else b.shape[0]
    assert K == (b.shape[0] if mode == "nn" else b.shape[1]), (a.shape, b.shape, mode)
    tm = _tile(M, MATMUL_ROWS)
    tn = _widest(N, lambda t: 2 * (tm * K * sa + K * t * sb + tm * t * (so + 4 * has_add)) <= MATMUL_VMEM_BUDGET)
    dims = ((_NN if mode == "nn" else _NT), ((), ()))

    def body(*refs):
        r = lax.dot_general(refs[0][...].astype(MXU_DTYPE), refs[1][...].astype(MXU_DTYPE), dims, preferred_element_type=F32)
        if has_add:
            r = r + refs[2][...].astype(F32)
        refs[-1][...] = r.astype(refs[-1].dtype)

    b_spec = pl.BlockSpec((K, tn), lambda j, i: (0, j)) if mode == "nn" else pl.BlockSpec((tn, K), lambda j, i: (j, 0))
    in_specs = [pl.BlockSpec((tm, K), lambda j, i: (i, 0)), b_spec]
    args = [a, b]
    if has_add:
        in_specs.append(pl.BlockSpec((tm, tn), lambda j, i: (i, j)))
        args.append(add)
    return pl.pallas_call(
        body, name=name, grid=(N // tn, M // tm),
        in_specs=in_specs,
        out_specs=pl.BlockSpec((tm, tn), lambda j, i: (i, j)),
        out_shape=jax.ShapeDtypeStruct((M, N), out_dtype),
        compiler_params=_cparams(("parallel", "parallel")),
    )(*args)


def _dot(a, b, dims, precision=None):
    if precision is None:
        a, b = a.astype(MXU_DTYPE), b.astype(MXU_DTYPE)
    return lax.dot_general(a, b, (dims, ((), ())), precision=precision, preferred_element_type=F32)


_NN = ((1,), (0,))
_NT = ((1,), (1,))
_TN = ((0,), (0,))


def _rms(x, gain):
    return x * lax.rsqrt(jnp.mean(x * x, axis=-1, keepdims=True) + EPS) * gain


def _hg_block(st_t, q, fl, iv, g, lb, gain):
    row = lax.broadcasted_iota(jnp.int32, (HG_SUB, HG_SUB), 0)
    col = lax.broadcasted_iota(jnp.int32, (HG_SUB, HG_SUB), 1)
    tri = (row >= col).astype(F32)
    heads = [slice(h * HG_DIM, (h + 1) * HG_DIM) for h in range(HG_HEADS)]
    fg = lb + (1.0 - lb) * jax.nn.sigmoid(fl)
    lf, kk, qf = jnp.log(fg), 1.0 - fg, jax.nn.silu(q)
    sts = [st_t[sl, :] for sl in heads]
    parts = [[] for _ in heads]
    for s in range(q.shape[0] // HG_SUB):
        r = slice(s * HG_SUB, (s + 1) * HG_SUB)
        b = lf[r]
        d = 1
        while d < HG_SUB:
            b = b + _shift_rows(b, d)
            d *= 2
        b_mid = jnp.sum(lf[r][:HG_SUB // 2], axis=0, keepdims=True)
        b_end = jnp.sum(lf[r], axis=0, keepdims=True)
        q_in, k_in = qf[r] * jnp.exp(b - b_mid), kk[r] * jnp.exp(b_mid - b)
        q_st, k_st, decay = qf[r] * jnp.exp(b), kk[r] * jnp.exp(b_end - b), jnp.exp(b_end)
        for h, sl in enumerate(heads):
            sc = _dot(q_in[:, sl], k_in[:, sl], _NT) * tri
            parts[h].append(_dot(sc, iv[r, sl], _NN) + _dot(q_st[:, sl], sts[h], _NT))
            sts[h] = sts[h] * decay[:, sl] + _dot(iv[r, sl], k_st[:, sl], _TN)
    outs = [_rms(jnp.concatenate(parts[h], axis=0), gain[:, sl]) * jax.nn.silu(g[:, sl]) for h, sl in enumerate(heads)]
    return jnp.concatenate(sts, axis=0), jnp.concatenate(outs, axis=1)


def _hg_specs(proj, nb):
    return [pl.BlockSpec((HG_BLOCK, HG_WIDTH), lambda i, c=c, f=nb: (f(i), c)) for c in range(4)]


def hgrn2_fwd(proj, lb, gain):
    L = proj.shape[0]
    n = L // HG_BLOCK

    def body(q, fl, iv, g, lb_r, gain_r, o_ref, st_ref, st):
        @pl.when(pl.program_id(0) == 0)
        def _():
            st[...] = jnp.zeros(st.shape, F32)

        st_ref[0] = st[...]
        new, o = _hg_block(st[...], q[...], fl[...], iv[...], g[...], lb_r[...], gain_r[...])
        st[...] = new
        o_ref[...] = o.astype(o_ref.dtype)

    pspec = pl.BlockSpec((1, HG_WIDTH), lambda i: (0, 0))
    return pl.pallas_call(
        body, name="hgrn2_fwd", grid=(n,),
        in_specs=_hg_specs(proj, lambda i: i) + [pspec, pspec],
        out_specs=[pl.BlockSpec((HG_BLOCK, HG_WIDTH), lambda i: (i, 0)),
                   pl.BlockSpec((1, HG_WIDTH, HG_DIM), lambda i: (i, 0, 0))],
        out_shape=[jax.ShapeDtypeStruct((L, HG_WIDTH), MXU_DTYPE),
                   jax.ShapeDtypeStruct((n, HG_WIDTH, HG_DIM), F32)],
        scratch_shapes=[pltpu.VMEM((HG_WIDTH, HG_DIM), F32)],
        compiler_params=_cparams(("arbitrary",)),
    )(proj, proj, proj, proj, lb, gain)


def hgrn2_bwd(proj, lb, gain, states, do):
    L = proj.shape[0]
    n = L // HG_BLOCK

    def body(q, fl, iv, g, lb_r, gain_r, st_r, do_r, dproj, dlb, dgain, dst):
        @pl.when(pl.program_id(0) == 0)
        def _():
            dst[...] = jnp.zeros(dst.shape, F32)
            dlb[...] = jnp.zeros(dlb.shape, F32)
            dgain[...] = jnp.zeros(dgain.shape, F32)

        _, vjp = jax.vjp(_hg_block, st_r[0], q[...], fl[...], iv[...], g[...], lb_r[...], gain_r[...])
        d_st, dq, dfl, div, dg, d_lb, d_gain = vjp((dst[...], do_r[...].astype(F32)))
        dst[...] = d_st
        dproj[:, 0 * HG_WIDTH:1 * HG_WIDTH] = dq
        dproj[:, 1 * HG_WIDTH:2 * HG_WIDTH] = dfl
        dproj[:, 2 * HG_WIDTH:3 * HG_WIDTH] = div
        dproj[:, 3 * HG_WIDTH:4 * HG_WIDTH] = dg
        dlb[...] += d_lb
        dgain[...] += d_gain

    rev = lambda i: n - 1 - i
    pspec = pl.BlockSpec((1, HG_WIDTH), lambda i: (0, 0))
    return pl.pallas_call(
        body, name="hgrn2_bwd", grid=(n,),
        in_specs=_hg_specs(proj, rev) + [pspec, pspec,
                                         pl.BlockSpec((1, HG_WIDTH, HG_DIM), lambda i: (rev(i), 0, 0)),
                                         pl.BlockSpec((HG_BLOCK, HG_WIDTH), lambda i: (rev(i), 0))],
        out_specs=[pl.BlockSpec((HG_BLOCK, 4 * HG_WIDTH), lambda i: (rev(i), 0)), pspec, pspec],
        out_shape=[jax.ShapeDtypeStruct((L, 4 * HG_WIDTH), F32),
                   jax.ShapeDtypeStruct((1, HG_WIDTH), F32), jax.ShapeDtypeStruct((1, HG_WIDTH), F32)],
        scratch_shapes=[pltpu.VMEM((HG_WIDTH, HG_DIM), F32)],
        compiler_params=_cparams(("arbitrary",)),
    )(proj, proj, proj, proj, lb, gain, states, do)


def _rope_rms(x, gain_p, cos_p, sin_p):
    n = x * lax.rsqrt(jnp.sum(x * x, axis=-1, keepdims=True) * (1.0 / MLA_ROPE) + EPS) * gain_p
    r = lax.broadcasted_iota(jnp.int32, (128, 128), 0)
    c = lax.broadcasted_iota(jnp.int32, (128, 128), 1)
    swap = (r == (c + 64) % 128).astype(F32)
    return n * cos_p + _dot(n, swap, _NN, HI) * sin_p


MLA_IN = MLA_Q_RANK + MLA_KV_RANK + 128


def _mla_prep(x, cos_p, sin_p, q_a, w_uq, kv_a, w_ukv, qn_nope, qn_rope, kn_nope, kn_rope):
    c_q, c_kv, kpe = x[:, :MLA_Q_RANK], x[:, MLA_Q_RANK:MLA_Q_RANK + MLA_KV_RANK], x[:, MLA_Q_RANK + MLA_KV_RANK:]
    q = _dot(_rms(c_q, q_a), w_uq, _NN)
    kv = _dot(_rms(c_kv, kv_a), w_ukv, _NN)
    k_pe = _rope_rms(kpe, kn_rope, cos_p, sin_p)
    qs, ks = [], []
    for h in range(MLA_HEADS):
        qs.append(_rms(q[:, h * MLA_DK:h * MLA_DK + MLA_NOPE], qn_nope))
        qs.append(_rope_rms(q[:, h * MLA_DK + MLA_NOPE:(h + 1) * MLA_DK], qn_rope, cos_p, sin_p))
        ks.append(_rms(kv[:, h * MLA_NOPE:(h + 1) * MLA_NOPE], kn_nope))
        ks.append(k_pe)
    return jnp.concatenate(qs, axis=1), jnp.concatenate(ks, axis=1), kv[:, MLA_HEADS * MLA_NOPE:]


def _mla_prep_opds(proj, cos_p, sin_p, params, tm, grads):
    g = (lambda k: k) if grads else (lambda k: None)
    assert (4 * HG_WIDTH) % MLA_IN == 0
    return ([rows(proj, tm, g('blk'), col=4 * HG_WIDTH // MLA_IN, width=MLA_IN), rows(cos_p, tm), rows(sin_p, tm)]
            + [full(p, g('acc')) for p in params])


def mla_prep_fwd(proj, cos_p, sin_p, params, tm):
    L = proj.shape[0]
    W = MLA_HEADS * MLA_DK
    rb = lambda w: (tm, w)
    outs = [((L, W), MXU_DTYPE, rb(W), lambda i: (i, 0)), ((L, W), MXU_DTYPE, rb(W), lambda i: (i, 0)),
            ((L, MLA_HEADS * MLA_V), MXU_DTYPE, rb(MLA_HEADS * MLA_V), lambda i: (i, 0))]
    return blocked_fwd(_mla_prep, _mla_prep_opds(proj, cos_p, sin_p, params, tm, False), outs, L // tm, "mla_prep_fwd")


def mla_prep_bwd(proj, cos_p, sin_p, params, dq, dk, dv, tm):
    L = proj.shape[0]
    return blocked_bwd(_mla_prep, _mla_prep_opds(proj, cos_p, sin_p, params, tm, True),
                       [rows(dq, tm), rows(dk, tm), rows(dv, tm)], L // tm, "mla_prep_bwd")


def _scores(q, k, scale, shift=None):
    s = _dot(q, k, _NT) * scale
    if shift is None:
        return s
    row = lax.broadcasted_iota(jnp.int32, s.shape, 0)
    col = lax.broadcasted_iota(jnp.int32, s.shape, 1)
    return jnp.where(col <= row + shift, s, -jnp.inf)


ATTN_ROWS = 512
ATTN_WIDE = 2


def attn_fwd(q, k, v, scale, t):
    L = q.shape[0]
    tq = ATTN_WIDE * t

    def body(q_ref, k_ref, v_ref, o_ref, lse_ref):
        i = pl.program_id(1)
        qb = q_ref[...]

        def step(j, carry, shift=None):
            m, l, acc = carry
            kj = k_ref[pl.ds(pl.multiple_of(j * t, t), t), :]
            vj = v_ref[pl.ds(pl.multiple_of(j * t, t), t), :]
            s = _scores(qb, kj, scale, shift)
            m_new = jnp.maximum(m, jnp.max(s, axis=-1, keepdims=True))
            p = jnp.exp(s - m_new)
            alpha = jnp.exp(m - m_new)
            return m_new, alpha * l + jnp.sum(p, axis=-1, keepdims=True), alpha * acc + _dot(p, vj, _NN)

        carry = (jnp.full((tq, 1), -jnp.inf, F32), jnp.zeros((tq, 1), F32), jnp.zeros((tq, MLA_V), F32))
        carry = lax.fori_loop(0, ATTN_WIDE * i, step, carry)
        for d in range(ATTN_WIDE):
            carry = step(ATTN_WIDE * i + d, carry, -d * t)
        m, l, acc = carry
        o_ref[...] = acc / l
        lse_ref[...] = jnp.broadcast_to(m + jnp.log(l), lse_ref.shape)

    hspec = lambda rows_, w: pl.BlockSpec((rows_, w), lambda h, i: (0, h))
    bspec = lambda w: pl.BlockSpec((tq, w), lambda h, i: (i, h))
    return pl.pallas_call(
        body, name="attn_fwd", grid=(MLA_HEADS, L // tq),
        in_specs=[bspec(MLA_DK), hspec(L, MLA_DK), hspec(L, MLA_V)],
        out_specs=[bspec(MLA_V), bspec(MLA_V)],
        out_shape=[jax.ShapeDtypeStruct((L, MLA_HEADS * MLA_V), F32)] * 2,
        compiler_params=_cparams(("parallel", "parallel")),
    )(q, k, v)


def attn_bwd_dq(q, k, v, o, lse, do, scale, t):
    L = q.shape[0]
    tq = ATTN_WIDE * t

    def body(q_ref, k_ref, v_ref, o_ref, lse_ref, do_ref, dq_ref):
        i = pl.program_id(1)
        qb, dob = q_ref[...], do_ref[...]
        delta = jnp.sum(dob * o_ref[...], axis=-1, keepdims=True)
        lse_c = jnp.max(lse_ref[...], axis=-1, keepdims=True)

        def step(j, dq, shift=None):
            kj = k_ref[pl.ds(pl.multiple_of(j * t, t), t), :]
            vj = v_ref[pl.ds(pl.multiple_of(j * t, t), t), :]
            p = jnp.exp(_scores(qb, kj, scale, shift) - lse_c)
            ds = p * (_dot(dob, vj, _NT) - delta) * scale
            return dq + _dot(ds, kj, _NN)

        dq = lax.fori_loop(0, ATTN_WIDE * i, step, jnp.zeros((tq, MLA_DK), F32))
        for d in range(ATTN_WIDE):
            dq = step(ATTN_WIDE * i + d, dq, -d * t)
        dq_ref[...] = dq

    hspec = lambda w: pl.BlockSpec((L, w), lambda h, i: (0, h))
    bspec = lambda w: pl.BlockSpec((tq, w), lambda h, i: (i, h))
    return pl.pallas_call(
        body, name="attn_bwd_dq", grid=(MLA_HEADS, L // tq),
        in_specs=[bspec(MLA_DK), hspec(MLA_DK), hspec(MLA_V), bspec(MLA_V), bspec(MLA_V), bspec(MLA_V)],
        out_specs=bspec(MLA_DK),
        out_shape=jax.ShapeDtypeStruct((L, MLA_HEADS * MLA_DK), F32),
        compiler_params=_cparams(("parallel", "parallel")),
    )(q, k, v, o, lse, do)


def attn_bwd_dkv(q, k, v, o, lse, do, scale, t):
    L = q.shape[0]
    tk = ATTN_WIDE * t

    def body(q_ref, k_ref, v_ref, o_ref, lse_ref, do_ref, dk_ref, dv_ref):
        j = pl.program_id(1)
        kb, vb = k_ref[...], v_ref[...]

        def step(i, carry, shift=None):
            dk, dv = carry
            r = pl.ds(pl.multiple_of(i * t, t), t)
            qi, doi = q_ref[r, :], do_ref[r, :]
            delta = jnp.sum(doi * o_ref[r, :], axis=-1, keepdims=True)
            lse_c = jnp.max(lse_ref[r, :], axis=-1, keepdims=True)
            p = jnp.exp(_scores(qi, kb, scale, shift) - lse_c)
            ds = p * (_dot(doi, vb, _NT) - delta) * scale
            return dk + _dot(ds, qi, _TN), dv + _dot(p, doi, _TN)

        carry = (jnp.zeros((tk, MLA_DK), F32), jnp.zeros((tk, MLA_V), F32))
        for d in range(ATTN_WIDE):
            carry = step(ATTN_WIDE * j + d, carry, d * t)
        dk, dv = lax.fori_loop(ATTN_WIDE * (j + 1), L // t, step, carry)
        dk_ref[...] = dk
        dv_ref[...] = dv

    hspec = lambda w: pl.BlockSpec((L, w), lambda h, j: (0, h))
    bspec = lambda w: pl.BlockSpec((tk, w), lambda h, j: (j, h))
    return pl.pallas_call(
        body, name="attn_bwd_dkv", grid=(MLA_HEADS, L // tk),
        in_specs=[hspec(MLA_DK), bspec(MLA_DK), bspec(MLA_V), hspec(MLA_V), hspec(MLA_V), hspec(MLA_V)],
        out_specs=[bspec(MLA_DK), bspec(MLA_V)],
        out_shape=[jax.ShapeDtypeStruct((L, MLA_HEADS * MLA_DK), F32), jax.ShapeDtypeStruct((L, MLA_HEADS * MLA_V), F32)],
        compiler_params=_cparams(("parallel", "parallel")),
    )(q, k, v, o, lse, do)


S5_LANES = S5_GB * S5_STATE
S5_BWD_CHUNK = 1024


def _cmul(ar, ai, br, bi):
    return ar * br - ai * bi, ar * bi + ai * br


def _a_powers(ar, ai, reverse):
    a2 = _cmul(ar, ai, ar, ai)
    a4 = _cmul(*a2, *a2)
    row = lax.broadcasted_iota(jnp.int32, (8, ar.shape[1]), 0)
    e = (8 - row) if reverse else (row + 1)
    tr, ti = jnp.ones((8, ar.shape[1]), F32), jnp.zeros((8, ar.shape[1]), F32)
    for bit, (pr, pi) in ((1, (ar, ai)), (2, a2), (4, a4), (8, _cmul(*a4, *a4))):
        nr, ni = _cmul(tr, ti, pr, pi)
        sel = (e & bit) != 0
        tr, ti = jnp.where(sel, nr, tr), jnp.where(sel, ni, ti)
    pows = []
    for d, (pr, pi) in zip((1, 2, 4), ((ar, ai), a2, a4)):
        keep = (row < 8 - d) if reverse else (row >= d)
        pows.append((jnp.where(keep, pr, 0.0), jnp.where(keep, pi, 0.0)))
    return pows, (tr, ti)


def _scan8(xr, xi, pows, table, cr, ci, reverse):
    for d, (pr, pi) in zip((1, 2, 4), pows):
        shift = 8 - d if reverse else d
        mr, mi = _cmul(pr, pi, pltpu.roll(xr, shift, 0), pltpu.roll(xi, shift, 0))
        xr, xi = xr + mr, xi + mi
    mr, mi = _cmul(table[0], table[1], cr, ci)
    return xr + mr, xi + mi


def _row_of(x, r):
    row = lax.broadcasted_iota(jnp.int32, x.shape, 0)
    return jnp.sum(jnp.where(row == r, x, 0.0), axis=0, keepdims=True)


def _s5_scan_fwd(h_re, h_im, ar, ai, L):
    pows, table = _a_powers(ar, ai, False)

    def step(i, carry):
        r = pl.ds(pl.multiple_of(i * 8, 8), 8)
        xr, xi = _scan8(h_re[r, :], h_im[r, :], pows, table, carry[0], carry[1], False)
        h_re[r, :] = xr
        h_im[r, :] = xi
        return xr[7:8, :], xi[7:8, :]

    z = jnp.zeros((1, ar.shape[1]), F32)
    lax.fori_loop(0, L // 8, step, (z, z))


def _s5_specs(L):
    return [pl.BlockSpec((L, 128), lambda g: (0, g)),
            pl.BlockSpec((1, 128, S5_LANES), lambda g: (g, 0, 0)), pl.BlockSpec((1, 128, S5_LANES), lambda g: (g, 0, 0)),
            pl.BlockSpec((1, 1, S5_LANES), lambda g: (g, 0, 0)), pl.BlockSpec((1, 1, S5_LANES), lambda g: (g, 0, 0)),
            pl.BlockSpec((1, S5_LANES, 128), lambda g: (g, 0, 0)), pl.BlockSpec((1, S5_LANES, 128), lambda g: (g, 0, 0))]


def s5_fwd(u, w_re, w_im, a_re, a_im, c_re, c_im):
    L, D = u.shape

    def body(u_ref, wr, wi, ar, ai, cr, ci, y_ref, h_re, h_im):
        ub = u_ref[...]
        h_re[...] = _dot(ub, wr[0], _NN)
        h_im[...] = _dot(ub, wi[0], _NN)
        _s5_scan_fwd(h_re, h_im, ar[0], ai[0], L)
        y_ref[...] = _dot(h_re[...], cr[0], _NN) - _dot(h_im[...], ci[0], _NN)

    return pl.pallas_call(
        body, name="s5_fwd", grid=(D // 128,),
        in_specs=_s5_specs(L), out_specs=pl.BlockSpec((L, 128), lambda g: (0, g)),
        out_shape=jax.ShapeDtypeStruct((L, D), F32),
        scratch_shapes=[pltpu.VMEM((L, S5_LANES), F32), pltpu.VMEM((L, S5_LANES), F32)],
        compiler_params=_cparams(("parallel",)),
    )(u, w_re, w_im, a_re, a_im, c_re, c_im)


def s5_bwd(u, w_re, w_im, a_re, a_im, c_re, c_im, dy, du_plus, tc):
    L, D = u.shape
    nch = L // tc

    def body(u_ref, wr, wi, ar_ref, ai_ref, cr, ci, dy_ref, plus_ref, du_ref, dwr, dwi, dar, dai, dcr, dci, h_re, h_im,
             g_re, g_im):
        ar, ai = ar_ref[0], ai_ref[0]
        ub = u_ref[...]
        h_re[...] = _dot(ub, wr[0], _NN)
        h_im[...] = _dot(ub, wi[0], _NN)
        _s5_scan_fwd(h_re, h_im, ar, ai, L)
        dyb = dy_ref[...]
        dcr[0] = _dot(h_re[...], dyb, _TN)
        dci[0] = -_dot(h_im[...], dyb, _TN)
        pows, table = _a_powers(ar, -ai, True)
        dwr[0] = jnp.zeros((128, S5_LANES), F32)
        dwi[0] = jnp.zeros((128, S5_LANES), F32)
        z1 = jnp.zeros((1, S5_LANES), F32)
        z8 = jnp.zeros((8, S5_LANES), F32)

        def chunk(cc, carry):
            c0 = pl.multiple_of((nch - 1 - cc) * tc, tc)
            rows_c = pl.ds(c0, tc)
            dyc = dy_ref[rows_c, :]
            g_re[...] = _dot(dyc, cr[0], _NT)
            g_im[...] = -_dot(dyc, ci[0], _NT)

            def step(ii, cy):
                gr_c, gi_c, acc_r, acc_i = cy
                i8 = pl.multiple_of((tc // 8 - 1 - ii) * 8, 8)
                rl = pl.ds(i8, 8)
                xr, xi = _scan8(g_re[rl, :], g_im[rl, :], pows, table, gr_c, gi_c, True)
                g_re[rl, :] = xr
                g_im[rl, :] = xi
                t0 = c0 + i8
                hb_r, hb_i = h_re[pl.ds(t0, 8), :], h_im[pl.ds(t0, 8), :]
                tp = pl.multiple_of(jnp.maximum(t0 - 8, 0), 8)
                first = (t0 > 0).astype(F32)
                pr = h_re[pl.ds(tp, 8), :][7:8, :] * first
                pi = h_im[pl.ds(tp, 8), :][7:8, :] * first
                row = lax.broadcasted_iota(jnp.int32, xr.shape, 0)
                hp_r = jnp.where(row == 0, pr, pltpu.roll(hb_r, 1, 0))
                hp_i = jnp.where(row == 0, pi, pltpu.roll(hb_i, 1, 0))
                return (xr[0:1, :], xi[0:1, :],
                        acc_r + xr * hp_r + xi * hp_i, acc_i + xi * hp_r - xr * hp_i)

            cy = lax.fori_loop(0, tc // 8, step, carry)
            uc = u_ref[rows_c, :]
            gr, gi = g_re[...], g_im[...]
            du_ref[rows_c, :] = _dot(gr, wr[0], _NT) + _dot(gi, wi[0], _NT) + plus_ref[rows_c, :]
            dwr[0] += _dot(uc, gr, _TN)
            dwi[0] += _dot(uc, gi, _TN)
            return cy

        _, _, acc_r, acc_i = lax.fori_loop(0, nch, chunk, (z1, z1, z8, z8))
        dar[0] = jnp.sum(acc_r, axis=0, keepdims=True)
        dai[0] = jnp.sum(acc_i, axis=0, keepdims=True)

    specs = _s5_specs(L)
    return pl.pallas_call(
        body, name="s5_bwd", grid=(D // 128,),
        in_specs=specs + [pl.BlockSpec((L, 128), lambda g: (0, g))] * 2,
        out_specs=[pl.BlockSpec((L, 128), lambda g: (0, g))] + specs[1:],
        out_shape=[jax.ShapeDtypeStruct((L, D), F32)] + [jax.ShapeDtypeStruct(x.shape, F32)
                                                        for x in (w_re, w_im, a_re, a_im, c_re, c_im)],
        scratch_shapes=[pltpu.VMEM((L, S5_LANES), F32), pltpu.VMEM((L, S5_LANES), F32),
                        pltpu.VMEM((tc, S5_LANES), F32), pltpu.VMEM((tc, S5_LANES), F32)],
        compiler_params=_cparams(("parallel",)),
    )(u, w_re, w_im, a_re, a_im, c_re, c_im, dy, du_plus)


def _s5_discretize(lr, li, ldt, br, bi):
    dt = jnp.exp(ldt)
    mag = jnp.exp(lr * dt)
    ar, ai = mag * jnp.cos(li * dt), mag * jnp.sin(li * dt)
    den = lr * lr + li * li
    zr = ((ar - 1.0) * lr + ai * li) / den
    zi = (ai * lr - (ar - 1.0) * li) / den
    p = lax.broadcasted_iota(jnp.int32, (S5_STATE, S5_STATE * S5_GROUP), 0)
    c = lax.broadcasted_iota(jnp.int32, (S5_STATE, S5_STATE * S5_GROUP), 1)
    rep = (c // S5_GROUP == p).astype(F32)
    zr, zi = _dot(zr, rep, _NN, HI), _dot(zi, rep, _NN, HI)
    return ar, ai, zr * br - zi * bi, zr * bi + zi * br


def _conv_shift(x, d):
    row = lax.broadcasted_iota(jnp.int32, x.shape, 0)
    return jnp.where(row >= d, pltpu.roll(x, d, 0), 0.0)


def _conv_unshift(x, d):
    n = x.shape[0]
    row = lax.broadcasted_iota(jnp.int32, x.shape, 0)
    return jnp.where(row < n - d, pltpu.roll(x, n - d, 0), 0.0)


@functools.partial(jax.custom_vjp, nondiff_argnums=(1,))
def _shift_rows(x, d):
    return _conv_shift(x, d)


_shift_rows.defvjp(lambda x, d: (_conv_shift(x, d), None), lambda d, _, g: (_conv_unshift(g, d),))


def _conv_gate(ug, uv, wg0, wg1, wg2, wv0, wv1, wv2, bg, bv):
    def conv(u, w0, w1, w2, b):
        return u * w2 + _shift_rows(u, 1) * w1 + _shift_rows(u, 2) * w0 + b
    return (jax.nn.silu(conv(ug, wg0, wg1, wg2, bg)) * conv(uv, wv0, wv1, wv2, bv),)


def _rms_fn(x, gain):
    return (_rms(x, gain),)


def _softmax_rows(s):
    e = jnp.exp(s - lax.stop_gradient(jnp.max(s, axis=-1, keepdims=True)))
    return e / jnp.sum(e, axis=-1, keepdims=True)


def _xa_core(qp, k, v, q_gain):
    dh = qp.shape[1] // XA_HEADS
    outs = []
    for h in range(XA_HEADS):
        sl = slice(h * dh, (h + 1) * dh)
        p = _softmax_rows(_dot(_rms(qp[:, sl], q_gain), k[:, sl], _NT) * (dh ** -0.5))
        outs.append(_dot(p, v[:, sl], _NN))
    return (jnp.concatenate(outs, axis=1),)


def _mem_kv(mem, mem_gain, wk, wv, k_gain):
    m = _rms(mem, mem_gain)
    kp = _dot(m, wk, _NN)
    dh = kp.shape[1] // XA_HEADS
    k = jnp.concatenate([_rms(kp[:, h * dh:(h + 1) * dh], k_gain) for h in range(XA_HEADS)], axis=1)
    return k, _dot(m, wv, _NN)


def _s5_post(y, u, d):
    return (jax.nn.gelu(y + d * u),)


def _glu(a, b):
    return (a * jax.nn.sigmoid(b),)


def _lb_first(logits):
    e = jnp.exp(logits - lax.stop_gradient(jnp.max(logits, axis=0, keepdims=True)))
    return (_row_of(e, 0) / jnp.sum(e, axis=0, keepdims=True),)


def _loss_fn(y, t):
    e = y - t
    part = 0.5 * jnp.sum(e * e) / y.shape[1]
    return e * (1.0 / y.shape[1]), jnp.full((8, 128), part / (8 * 128), F32)


def _out(shape, dtype, tm):
    return (shape, dtype, (tm, shape[1]), lambda i: (i, 0))


LIGHT_ROWS = 512


def rms_fwd(h, gain, dtype):
    tm = min(LIGHT_ROWS, h.shape[0])
    return blocked_fwd(_rms_fn, [rows(h, tm), full(gain)], [_out(h.shape, dtype, tm)], h.shape[0] // tm, "rms_fwd")[0]


def rms_bwd(h, gain, dy, residual):
    tm = min(LIGHT_ROWS, h.shape[0])
    return blocked_bwd(_rms_fn, [rows(h, tm, 'blk'), full(gain, 'acc')], [rows(dy, tm)], h.shape[0] // tm, "rms_bwd",
                       plus=rows(residual, tm))


def _adamw_math(w, g, m, v):
    m = ADAM_B1 * m + (1.0 - ADAM_B1) * g
    v = ADAM_B2 * v + (1.0 - ADAM_B2) * jnp.square(g)
    m_hat = m / (1.0 - ADAM_B1 ** ADAM_STEP)
    v_hat = v / (1.0 - ADAM_B2 ** ADAM_STEP)
    return -ADAM_LR * (m_hat / (jnp.sqrt(v_hat) + ADAM_EPS) + ADAM_WD * w), m, v


def adamw(w, g, m, v, name):
    R = w.shape[0]
    tm = _tile(R, 256)
    assert g.shape == w.shape == m.shape == v.shape, (name, w.shape, g.shape)

    def body(w_ref, g_ref, m_ref, v_ref, d_ref, nm_ref, nv_ref):
        d_ref[...], nm_ref[...], nv_ref[...] = _adamw_math(w_ref[...], g_ref[...], m_ref[...], v_ref[...])

    spec = pl.BlockSpec((tm, w.shape[1]), lambda i: (i, 0))
    return pl.pallas_call(
        body, name=name, grid=(R // tm,), in_specs=[spec] * 4, out_specs=[spec] * 3,
        out_shape=[jax.ShapeDtypeStruct(w.shape, F32)] * 3, compiler_params=_cparams(("parallel",)),
    )(w, g, m, v)


def _index_operand(i):
    return jnp.reshape(i, (1,)).astype(jnp.int32)


def adamw_layer(w, mine, other, core, m, v, layer, bufs, name):
    H, C = mine.shape
    tm = _tile(H, 256)
    nb = H // tm
    assert w.shape[1:] == (2 * H, C) and all(b.shape == w.shape for b in bufs), (name, w.shape, mine.shape)

    def body(c_ref, w_ref, mine_ref, other_ref, m_ref, v_ref, *rest):
        g_out, d_ref, nm_ref, nv_ref = rest[-4:]
        g_ = jnp.where(pl.program_id(0) // nb == c_ref[0], mine_ref[...], other_ref[...])
        g_out[...] = g_
        d_ref[...], nm_ref[...], nv_ref[...] = _adamw_math(w_ref[...], g_, m_ref[...], v_ref[...])

    lspec = pl.BlockSpec((None, tm, C), lambda i, c: (layer, i, 0))
    half = lambda sign: pl.BlockSpec(
        (tm, C), lambda i, c: (jnp.clip(i - (c[0] if sign else 1 - c[0]) * nb, 0, nb - 1), 0))
    any_spec = pl.BlockSpec(memory_space=pl.ANY)
    grid_spec = pltpu.PrefetchScalarGridSpec(
        num_scalar_prefetch=1, grid=(2 * nb,),
        in_specs=[lspec, half(True), half(False), lspec, lspec] + [any_spec] * 4, out_specs=[lspec] * 4)
    return pl.pallas_call(
        body, name=name, grid_spec=grid_spec, out_shape=[jax.ShapeDtypeStruct(w.shape, F32)] * 4,
        input_output_aliases={6: 0, 7: 1, 8: 2, 9: 3}, compiler_params=_cparams(("parallel",)),
    )(_index_operand(core), w, mine, other, m, v, *bufs)


def add_own_half(x, core, theirs, name, out_dtype):
    nq, _, H, C = x.shape
    tm = _tile(H, 256)

    def body(c_ref, x_ref, t_ref, o_ref):
        o_ref[...] = (x_ref[...] + t_ref[...]).astype(o_ref.dtype)

    spec = pl.BlockSpec((None, tm, C), lambda q, i, c: (q, i, 0))
    grid_spec = pltpu.PrefetchScalarGridSpec(
        num_scalar_prefetch=1, grid=(nq, H // tm),
        in_specs=[pl.BlockSpec((None, None, tm, C), lambda q, i, c: (q, c[0], i, 0)), spec], out_specs=spec)
    return pl.pallas_call(
        body, name=name, grid_spec=grid_spec, out_shape=jax.ShapeDtypeStruct((nq, H, C), out_dtype),
        compiler_params=_cparams(("parallel", "parallel")),
    )(_index_operand(core), x, theirs)


def add_chips(pair, chip, got, name):
    n, R, C = got.shape
    tm = _tile(R, 256)

    def body(q_ref, *refs):
        acc = refs[0][...].astype(F32)
        for r in refs[1:-1]:
            acc = acc + r[...].astype(F32)
        refs[-1][...] = acc

    grid_spec = pltpu.PrefetchScalarGridSpec(
        num_scalar_prefetch=1, grid=(R // tm,),
        in_specs=[pl.BlockSpec((None, tm, C), lambda i, q: (q[0], i, 0))]
        + [pl.BlockSpec((None, tm, C), lambda i, q, j=j: (j, i, 0)) for j in range(n)],
        out_specs=pl.BlockSpec((tm, C), lambda i, q: (i, 0)))
    return pl.pallas_call(
        body, name=name, grid_spec=grid_spec, out_shape=jax.ShapeDtypeStruct((R, C), F32),
        compiler_params=_cparams(("parallel",)),
    )(_index_operand(chip), pair, *([got] * n))


_HBM = pl.BlockSpec(memory_space=pltpu.HBM)
N_CHIPS = 4


def _my_place():
    return lax.axis_index("x"), lax.axis_index("y"), lax.axis_index("c")


def _comm_call(body, name, xs, out_shapes, n_remote, n_local, sequencer=None):
    sems = [pltpu.SemaphoreType.DMA((n_remote,)), pltpu.SemaphoreType.DMA((n_remote,)),
            pltpu.SemaphoreType.DMA((max(n_local, 1),))]
    if sequencer is None:
        return pl.pallas_call(
            body, name=name, in_specs=[_HBM] * len(xs), out_specs=[_HBM] * len(out_shapes), out_shape=out_shapes,
            scratch_shapes=sems, compiler_params=pltpu.CompilerParams(has_side_effects=True),
        )(*xs)
    peers_of, collective_id = sequencer
    hbm = pltpu.MemorySpace.HBM
    x_refs = [jax.new_ref(x, memory_space=hbm) for x in xs]
    o_refs = [jax.empty_ref(s, memory_space=hbm) for s in out_shapes]

    @pl.kernel(mesh=plsc.ScalarSubcoreMesh(axis_name="sequencer", num_cores=1), name=name, scratch_types=tuple(sems),
               compiler_params=pltpu.CompilerParams(collective_id=collective_id))
    def launch(send_sems, recv_sems, local_sems):
        peers = peers_of(*_my_place())
        barrier = pltpu.get_barrier_semaphore()
        for peer in peers:
            pl.semaphore_signal(barrier, inc=1, device_id=peer, device_id_type=MESH)
        pl.semaphore_wait(barrier, len(peers))
        body(*x_refs, *o_refs, send_sems, recv_sems, local_sems)

    launch()
    return [o[...] for o in o_refs]


def _sibling(mx, my, mc):
    return [(mx, my, 1 - mc)]


def _same_core_of_other_chips(mx, my, mc):
    return [(tx, ty, mc) for tx, ty in _other_chips(mx, my)]


def _run(copies):
    for cp in copies:
        cp.start()
    for cp in copies:
        cp.wait()


def _other_chips(mx, my):
    return [(mx ^ (j >> 1), my ^ (j & 1)) for j in (1, 2, 3)]


def device_gather(x, name):
    H, C = x.shape

    def body(x_ref, o_ref, send_sems, recv_sems, local_sems):
        mx, my, mc = _my_place()
        dst = o_ref.at[2 * mx + my, mc]
        copies = [pltpu.make_async_copy(x_ref, dst, local_sems.at[0])]
        others = [(mx, my, 1 - mc)] + [(tx, ty, c) for tx, ty in _other_chips(mx, my) for c in (mc, 1 - mc)]
        for j, peer in enumerate(others):
            copies.append(pltpu.make_async_remote_copy(src_ref=x_ref, dst_ref=dst, send_sem=send_sems.at[j],
                                                       recv_sem=recv_sems.at[j], device_id=peer, device_id_type=MESH))
        _run(copies)

    out = _comm_call(body, name, [x], [jax.ShapeDtypeStruct((N_CHIPS, 2, H, C), x.dtype)], 7, 1)[0]
    return out.reshape(N_CHIPS * 2 * H, C)


def gather_two_level(xs, name):
    n = len(xs)
    shapes = [jax.ShapeDtypeStruct((2, N_CHIPS) + x.shape[1:], x.dtype) for x in xs]

    def body(*refs):
        x_refs, o_refs = refs[:n], refs[n:2 * n]
        send_sems, recv_sems, local_sems = refs[2 * n:]
        mx, my, mc = _my_place()
        q = 2 * mx + my
        first, local, second = [], [], []
        for i, (x_ref, o_ref) in enumerate(zip(x_refs, o_refs)):
            local.append(pltpu.make_async_copy(x_ref.at[mc], o_ref.at[mc, q], local_sems.at[i]))
            for j, (tx, ty) in enumerate(_other_chips(mx, my)):
                first.append(pltpu.make_async_remote_copy(
                    src_ref=x_ref.at[mc], dst_ref=o_ref.at[mc, q], send_sem=send_sems.at[4 * i + j],
                    recv_sem=recv_sems.at[4 * i + j], device_id=(tx, ty, mc), device_id_type=MESH))
            second.append(pltpu.make_async_remote_copy(
                src_ref=o_ref.at[mc], dst_ref=o_ref.at[mc], send_sem=send_sems.at[4 * i + 3],
                recv_sem=recv_sems.at[4 * i + 3], device_id=(mx, my, 1 - mc), device_id_type=MESH))
        for cp in local + first:
            cp.start()
        for cp in local:
            cp.wait()
        for cp in first:
            cp.wait_recv()
        _run(second)
        for cp in first:
            cp.wait_send()

    return _comm_call(body, name, xs, shapes, 4 * n, n)


def gather_two_level_sequencer(xs, name, collective_id):
    n = len(xs)
    hbm = pltpu.MemorySpace.HBM
    x_refs = [jax.new_ref(x, memory_space=hbm) for x in xs]
    o_refs = [jax.empty_ref(jax.ShapeDtypeStruct((2, N_CHIPS) + x.shape[1:], x.dtype), memory_space=hbm) for x in xs]

    @pl.kernel(mesh=plsc.ScalarSubcoreMesh(axis_name="sequencer", num_cores=1), name=name,
               scratch_types=(pltpu.SemaphoreType.DMA((4 * n,)), pltpu.SemaphoreType.DMA((4 * n,)),
                              pltpu.SemaphoreType.DMA((n,))),
               compiler_params=pltpu.CompilerParams(collective_id=collective_id))
    def launch(send_sems, recv_sems, local_sems):
        mx, my, mc = _my_place()
        peers = [(tx, ty, mc) for tx, ty in _other_chips(mx, my)] + [(mx, my, 1 - mc)]
        barrier = pltpu.get_barrier_semaphore()
        for peer in peers:
            pl.semaphore_signal(barrier, inc=1, device_id=peer, device_id_type=MESH)
        pl.semaphore_wait(barrier, len(peers))
        q = 2 * mx + my
        first, local, second = [], [], []
        for i, (x_ref, o_ref) in enumerate(zip(x_refs, o_refs)):
            local.append(pltpu.make_async_copy(x_ref.at[mc], o_ref.at[mc, q], local_sems.at[i]))
            for j, peer in enumerate(peers[:3]):
                first.append(pltpu.make_async_remote_copy(
                    src_ref=x_ref.at[mc], dst_ref=o_ref.at[mc, q], send_sem=send_sems.at[4 * i + j],
                    recv_sem=recv_sems.at[4 * i + j], device_id=peer, device_id_type=MESH))
            second.append(pltpu.make_async_remote_copy(
                src_ref=o_ref.at[mc], dst_ref=o_ref.at[mc], send_sem=send_sems.at[4 * i + 3],
                recv_sem=recv_sems.at[4 * i + 3], device_id=peers[3], device_id_type=MESH))
        for cp in local + first:
            cp.start()
        for cp in local:
            cp.wait()
        for cp in first:
            cp.wait_recv()
        _run(second)
        for cp in first:
            cp.wait_send()

    launch()
    return [o[...] for o in o_refs]


def pair_swap(xs, name, halves, collective_id=None):
    n = len(xs)
    shapes = [jax.ShapeDtypeStruct(x.shape[:1] + x.shape[2:] if halves else x.shape, x.dtype) for x in xs]

    def body(*refs):
        x_refs, o_refs = refs[:n], refs[n:2 * n]
        send_sems, recv_sems, _ = refs[2 * n:]
        mx, my, mc = _my_place()
        _run([pltpu.make_async_remote_copy(
            src_ref=x_ref.at[:, 1 - mc] if halves else x_ref, dst_ref=o_ref, send_sem=send_sems.at[i],
            recv_sem=recv_sems.at[i], device_id=(mx, my, 1 - mc), device_id_type=MESH)
            for i, (x_ref, o_ref) in enumerate(zip(x_refs, o_refs))])

    return _comm_call(body, name, xs, shapes, n, 0, None if collective_id is None else (_sibling, collective_id))


def chip_all_to_all(xs, name, collective_id=None):
    n = len(xs)
    shapes = [jax.ShapeDtypeStruct((N_CHIPS - 1,) + x.shape[1:], x.dtype) for x in xs]

    def body(*refs):
        x_refs, o_refs = refs[:n], refs[n:2 * n]
        send_sems, recv_sems, _ = refs[2 * n:]
        mx, my, mc = _my_place()
        copies = []
        for i, (x_ref, o_ref) in enumerate(zip(x_refs, o_refs)):
            for j, (tx, ty) in enumerate(_other_chips(mx, my)):
                copies.append(pltpu.make_async_remote_copy(
                    src_ref=x_ref.at[2 * tx + ty], dst_ref=o_ref.at[j], send_sem=send_sems.at[3 * i + j],
                    recv_sem=recv_sems.at[3 * i + j], device_id=(tx, ty, mc), device_id_type=MESH))
        _run(copies)

    return _comm_call(body, name, xs, shapes, 3 * n, 0,
                      None if collective_id is None else (_same_core_of_other_chips, collective_id))


WEIGHTS = ['norm_mix', 'norm_xa', 'norm_mem', 'norm_ffn', 'xa_wq', 'xa_wk', 'xa_wv', 'xa_wo', 'xa_q_norm', 'xa_k_norm',
           'ffn_w_up', 'ffn_conv_w', 'ffn_conv_b', 'ffn_w_down', 'hg_lb_logits', 'mix_w_in', 'hg_out_norm',
           'mla_q_a_norm', 'mla_w_uq', 'mla_kv_a_norm', 'mla_w_ukv', 'mla_qn_nope', 'mla_qn_rope', 'mla_kn_nope',
           'mla_kn_rope', 'mix_w_out', 's5_lam_re', 's5_lam_im', 's5_log_dt', 's5_b_re', 's5_b_im', 's5_c_re',
           's5_c_im', 's5_d', 's5_w_glu_a', 's5_w_glu_b']
INPUTS = ['x', 'mem', 'positions'] + WEIGHTS + ['loss_target'] + ['m_' + n for n in WEIGHTS] + ['v_' + n for n in WEIGHTS]
SHARD_AXIS = {'xa_wq': 1, 'xa_wk': 1, 'xa_wv': 1, 'xa_wo': 1, 'ffn_w_up': 2, 'ffn_conv_w': 2, 'ffn_w_down': 1,
              'mix_w_in': 2, 'mla_w_uq': 2, 'mla_w_ukv': 2, 'mix_w_out': 1, 's5_d': 1, 's5_w_glu_a': 1, 's5_w_glu_b': 1}
BIG = ['xa_wq', 'xa_wk', 'xa_wv', 'xa_wo', 'ffn_w_up', 'ffn_w_down', 'mix_w_in', 'mix_w_out', 's5_w_glu_a', 's5_w_glu_b']
FIRST_NEEDED = ('mix_w_in', 'mix_w_out')
SMALL_SHARDED = [n for n in WEIGHTS if n in SHARD_AXIS and n not in BIG]
REPLICATED = [n for n in WEIGHTS if n not in SHARD_AXIS]
SMALL = SMALL_SHARDED + REPLICATED
PACK_W = 1024
ROW_MULT = 16
W_IN_SHARD = IN_WIDTH // N_CHIPS
W_IN_SHARD_PAD = 640


def _pack(flats, mult=ROW_MULT):
    flat = jnp.concatenate([f.reshape(-1) for f in flats])
    unit = mult * PACK_W
    n = -(-flat.shape[0] // unit) * unit
    return jnp.pad(flat, (0, n - flat.shape[0])).reshape(n // PACK_W, PACK_W)


def _unpack(packed, shapes):
    flat, out, o = packed.reshape(-1), [], 0
    for s in shapes:
        n = math.prod(s)
        out.append(flat[o:o + n].reshape(s))
        o += n
    return out


def _rope_pad(w):
    z = jnp.zeros(w.shape[:-1] + (MLA_ROPE // 2,), w.dtype)
    return jnp.concatenate([w[..., :MLA_ROPE // 2], z, w[..., MLA_ROPE // 2:], z], axis=-1)


def _rope_unpad(g):
    return jnp.concatenate([g[..., :MLA_ROPE // 2], g[..., 64:64 + MLA_ROPE // 2]], axis=-1)


def _blockdiag_in(bb):
    nb = bb.shape[0] // S5_GB
    t = bb.reshape(nb, S5_GB, S5_STATE, S5_GROUP).transpose(0, 1, 3, 2)
    return jnp.einsum('bgmp,gh->bgmhp', t, jnp.eye(S5_GB, dtype=bb.dtype)).reshape(nb, S5_GB * S5_GROUP, S5_LANES)


def _blockdiag_in_t(dw):
    nb = dw.shape[0]
    t = jnp.einsum('bgmhp,gh->bgmp', dw.reshape(nb, S5_GB, S5_GROUP, S5_GB, S5_STATE), jnp.eye(S5_GB, dtype=dw.dtype))
    return t.transpose(0, 1, 3, 2).reshape(nb * S5_GB, S5_STATE, S5_GROUP)


def _blockdiag_out(c):
    nb = c.shape[0] // S5_GB
    t = c.reshape(nb, S5_GB, S5_GROUP, S5_STATE).transpose(0, 1, 3, 2)
    return jnp.einsum('bgpm,gh->bgphm', t, jnp.eye(S5_GB, dtype=c.dtype)).reshape(nb, S5_LANES, S5_GB * S5_GROUP)


def _blockdiag_out_t(dc):
    nb = dc.shape[0]
    t = jnp.einsum('bgphm,gh->bgpm', dc.reshape(nb, S5_GB, S5_STATE, S5_GB, S5_GROUP), jnp.eye(S5_GB, dtype=dc.dtype))
    return t.transpose(0, 1, 3, 2).reshape(nb * S5_GB, S5_GROUP, S5_STATE)


def _gather_weights(P):
    def halves(x):
        return x if x.shape[0] == 2 else x.reshape(2, x.shape[1] // 2, x.shape[2])

    now = [n for n in BIG if n in FIRST_NEEDED]
    later = [n for n in BIG if n not in FIRST_NEEDED]
    xs = [halves(P[n].astype(BF16)) for n in now] + [halves(_pack([P[n] for n in SMALL_SHARDED], 2 * ROW_MULT)[None])]
    got = gather_two_level(xs, "gather_weights")
    got, xs_later = lax.optimization_barrier((got, [halves(P[n].astype(BF16)) for n in later]))
    got_later = gather_two_level_sequencer(xs_later, "gather_weights_later", 1)
    full_w = {}
    for n, g in list(zip(now, got[:-1])) + list(zip(later, got_later)):
        two_layers, by_rows = P[n].shape[0] == 2, SHARD_AXIS[n] == 1
        if two_layers and by_rows:
            full_w[n] = g.reshape(2, N_CHIPS * g.shape[2], g.shape[3])
        elif two_layers:
            full_w[n] = g.transpose(0, 2, 1, 3).reshape(2, g.shape[2], N_CHIPS * g.shape[3])
        elif by_rows:
            full_w[n] = g.transpose(1, 0, 2, 3).reshape(1, 2 * N_CHIPS * g.shape[2], g.shape[3])
        else:
            full_w[n] = g.transpose(0, 2, 1, 3).reshape(1, 2 * g.shape[2], N_CHIPS * g.shape[3])
    small = got[-1].transpose(1, 0, 2, 3).reshape(N_CHIPS, -1, PACK_W)
    per_chip = [_unpack(small[q], [P[n].shape for n in SMALL_SHARDED]) for q in range(N_CHIPS)]
    for i, n in enumerate(SMALL_SHARDED):
        full_w[n] = jnp.concatenate([per_chip[q][i] for q in range(N_CHIPS)], axis=SHARD_AXIS[n])
    return full_w


def _reduce_batch(items, small, P, results, tag, ids):
    mx, my, mc = _my_place()
    q = 2 * mx + my
    ids = ids or {}
    names = [f"{n}_{lyr}" for n, lyr, _ in items] + (['small'] if small is not None else [])
    xs = [g.reshape(N_CHIPS, 2, g.shape[1] // 2, g.shape[2]) for g in [g for _, _, g in items] + ([small] if small is not None else [])]
    def after(vals, tie):
        return (vals, None) if tie is None else lax.optimization_barrier((vals, tie))

    theirs = pair_swap(xs, "grads_pair_swap_" + tag, True, ids.get('swap'))
    theirs, tie = after(theirs, (yield None))
    pair = [add_own_half(x, mc, t, "grads_pair_sum_" + n, F32 if n == 'small' else BF16) for x, t, n in zip(xs, theirs, names)]
    got = chip_all_to_all(pair, "grads_chip_all_to_all_" + tag, ids.get('a2a'))
    if tie is not None:
        tie, _ = lax.optimization_barrier((tie, pair[-1]))
    got, tie = after(got, (yield pair[-1] if tie is None else tie))
    summed = [add_chips(p, q, g, "grads_chip_sum_" + n) for p, g, n in zip(pair, got, names)]
    other = pair_swap(summed[:len(items)], "grads_pair_join_" + tag, False, ids.get('join'))
    if small is not None:
        results['small_sum'] = device_gather(summed[-1], "grads_small_gather")
    other, tie = after(other, (yield summed[-1] if tie is None else tie))
    for (n, lyr, _), s, o in zip(items, summed, other):
        if n == 'mix_w_in':
            s, o = s[:, :W_IN_SHARD], o[:, :W_IN_SHARD]
        view = (P[n].shape[0], 2 * s.shape[0], P[n].shape[-1])
        bufs = results.get(n) or [lax.empty(view, F32) for _ in range(4)]
        results[n] = adamw_layer(P[n].reshape(view), s, o, mc, P['m_' + n].reshape(view), P['v_' + n].reshape(view), lyr,
                                 bufs, f"adamw_{n}_{lyr}")
    yield tie


def _update_small(small_sum, GS, P):
    mx, my, _ = _my_place()
    q = 2 * mx + my
    g_small = dict(zip(SMALL, _unpack(small_sum, [GS[n].shape for n in SMALL])))
    for n in SMALL_SHARDED:
        s = P[n].shape[SHARD_AXIS[n]]
        g_small[n] = lax.dynamic_slice_in_dim(g_small[n], q * s, s, axis=SHARD_AXIS[n])
    grad, delta, new_m, new_v = {}, {}, {}, {}
    packed = lambda prefix: _pack([P[prefix + n] for n in SMALL])
    d, m_, v_ = adamw(packed(''), _pack([g_small[n] for n in SMALL]), packed('m_'), packed('v_'), "adamw_small")
    shapes = [P[n].shape for n in SMALL]
    grad.update(g_small)
    for out, pk in ((delta, d), (new_m, m_), (new_v, v_)):
        out.update(zip(SMALL, _unpack(pk, shapes)))
    return grad, delta, new_m, new_v


def _row(v):
    return v.reshape(1, -1)


def _xattn_fwd(h, mem, W, lyr, tm):
    g_xa, g_mem = _row(W['norm_xa'][lyr]), _row(W['norm_mem'][lyr])
    g_q, g_k = _row(W['xa_q_norm'][lyr]), _row(W['xa_k_norm'][lyr])
    wq, wk, wv, wo = (W[n][lyr] for n in ('xa_wq', 'xa_wk', 'xa_wv', 'xa_wo'))
    L, D = h.shape
    M = mem.shape[0]
    hx = rms_fwd(h, g_xa, MXU_DTYPE)
    qp = matmul(hx, wq, name="xa_q")
    kv_opds = [full(mem), full(g_mem), full(wk), full(wv), full(g_k)]
    k, v = blocked_fwd(_mem_kv, kv_opds, [((M, D), F32, (M, D), lambda i: (0, 0))] * 2, 1, "xa_mem_kv")
    o = blocked_fwd(_xa_core, [rows(qp, tm), full(k), full(v), full(g_q)], [_out((L, D), MXU_DTYPE, tm)], L // tm,
                    "xa_core")[0]
    out = matmul(o, wo, add=h, name="xa_o")
    return out, (h, hx, qp, k, v, o)


def _xattn_bwd(dout, saved, mem, W, lyr, tm):
    h, hx, qp, k, v, o = saved
    g_xa, g_mem = _row(W['norm_xa'][lyr]), _row(W['norm_mem'][lyr])
    g_q, g_k = _row(W['xa_q_norm'][lyr]), _row(W['xa_k_norm'][lyr])
    wq, wk, wv, wo = (W[n][lyr] for n in ('xa_wq', 'xa_wk', 'xa_wv', 'xa_wo'))
    L = h.shape[0]
    do = matmul(dout, wo, "nt", name="xa_do")
    d_wo = matmul(o, dout, "tn", name="xa_dwo")
    dqp, dk, dv, d_gq = blocked_bwd(_xa_core, [rows(qp, tm, 'blk'), full(k, 'acc'), full(v, 'acc'), full(g_q, 'acc')],
                                    [rows(do, tm)], L // tm, "xa_core_bwd")
    d_wq = matmul(hx, dqp, "tn", name="xa_dwq")
    dhx = matmul(dqp, wq, "nt", name="xa_dhx")
    dh, d_gxa = rms_bwd(h, g_xa, dhx, dout)
    d_gmem, d_wk, d_wv, d_gk = blocked_bwd(
        _mem_kv, [full(mem), full(g_mem, 'acc'), full(wk, 'acc'), full(wv, 'acc'), full(g_k, 'acc')],
        [full(dk), full(dv)], 1, "xa_mem_kv_bwd")
    by_chip = lambda g: g.reshape(N_CHIPS, g.shape[0] // N_CHIPS, g.shape[1])
    grads = {'norm_xa': d_gxa, 'norm_mem': d_gmem, 'xa_q_norm': d_gq, 'xa_k_norm': d_gk,
             'xa_wq': by_chip(d_wq), 'xa_wk': by_chip(d_wk), 'xa_wv': by_chip(d_wv), 'xa_wo': by_chip(d_wo)}
    return dh, grads


def _conv_params(W, lyr):
    cw, cb = W['ffn_conv_w'][lyr], W['ffn_conv_b'][lyr]
    F = cw.shape[1] // 2
    return [cw[0:1, :F], cw[1:2, :F], cw[2:3, :F], cw[0:1, F:], cw[1:2, F:], cw[2:3, F:], _row(cb[:F]), _row(cb[F:])]


def _ffn_fwd(h, W, lyr, tm):
    L, D = h.shape
    w_up, w_down = W['ffn_w_up'][lyr], W['ffn_w_down'][lyr]
    F = w_down.shape[0]
    hf = rms_fwd(h, _row(W['norm_ffn'][lyr]), MXU_DTYPE)
    ug = matmul(hf, w_up[:, :F], name="ffn_up_gate")
    uv = matmul(hf, w_up[:, F:], name="ffn_up_value")
    opds = [cols(ug, 128), cols(uv, 128)] + [cols(p, 128) for p in _conv_params(W, lyr)]
    a = blocked_fwd(_conv_gate, opds, [((L, F), MXU_DTYPE, (L, 128), lambda j: (0, j))], F // 128, "ffn_conv_gate")[0]
    out = matmul(a, w_down, add=h, name="ffn_down")
    return out, (h, hf, ug, uv, a)


def _ffn_bwd(dout, saved, W, lyr, tm):
    h, hf, ug, uv, a = saved
    w_up, w_down = W['ffn_w_up'][lyr], W['ffn_w_down'][lyr]
    F = w_down.shape[0]
    da = matmul(dout, w_down, "nt", name="ffn_da")
    d_wdown = matmul(a, dout, "tn", name="ffn_dwdown")
    opds = [cols(ug, 128, 'blk'), cols(uv, 128, 'blk')] + [cols(p, 128, 'blk') for p in _conv_params(W, lyr)]
    gs = blocked_bwd(_conv_gate, opds, [cols(da, 128)], F // 128, "ffn_conv_gate_bwd")
    dug, duv = gs[0], gs[1]
    d_cw = jnp.concatenate([jnp.concatenate(gs[2:5], axis=0), jnp.concatenate(gs[5:8], axis=0)], axis=1)
    d_cb = jnp.concatenate([gs[8], gs[9]], axis=1)[0]
    half = N_CHIPS // 2
    d_wup = lax.empty((N_CHIPS, hf.shape[1], w_up.shape[1] // N_CHIPS), F32)
    d_wup = matmul(hf, dug, "tn", name="ffn_dwup_gate", into=(d_wup, 0), col_blocks=half)
    d_wup = matmul(hf, duv, "tn", name="ffn_dwup_value", into=(d_wup, half), col_blocks=half)
    dhf = matmul(dug, w_up[:, :F], "nt", name="ffn_dhf_gate")
    dhf = matmul(duv, w_up[:, F:], "nt", add=dhf, name="ffn_dhf_value")
    dh, d_g = rms_bwd(h, _row(W['norm_ffn'][lyr]), dhf, dout)
    d_wdown = d_wdown.reshape(N_CHIPS, F // N_CHIPS, d_wdown.shape[1])
    return dh, {'norm_ffn': d_g, 'ffn_w_up': d_wup, 'ffn_conv_w': d_cw, 'ffn_conv_b': d_cb, 'ffn_w_down': d_wdown}


def _mla_params(W):
    w_uq = W['mla_w_uq'][0].reshape(MLA_Q_RANK, MLA_HEADS, MLA_QK)
    w_uq = jnp.concatenate([w_uq[..., :MLA_NOPE], _rope_pad(w_uq[..., MLA_NOPE:])], axis=-1)
    w_ukv = W['mla_w_ukv'][0].reshape(MLA_KV_RANK, MLA_HEADS, MLA_NOPE + MLA_V)
    w_ukv = jnp.concatenate([w_ukv[..., :MLA_NOPE].reshape(MLA_KV_RANK, -1), w_ukv[..., MLA_NOPE:].reshape(MLA_KV_RANK, -1)],
                            axis=1)
    return [_row(W['mla_q_a_norm'][0]), w_uq.reshape(MLA_Q_RANK, MLA_HEADS * MLA_DK), _row(W['mla_kv_a_norm'][0]), w_ukv,
            _row(W['mla_qn_nope'][0]), _row(_rope_pad(W['mla_qn_rope'][0])), _row(W['mla_kn_nope'][0]),
            _row(_rope_pad(W['mla_kn_rope'][0]))]


def _w_in_padded(W):
    w = W['mix_w_in'][0]
    return jnp.concatenate([w[:, :IN_WIDTH - MLA_ROPE], _rope_pad(w[:, IN_WIDTH - MLA_ROPE:])], axis=1)


def _mixer0_fwd(h, W, cos_p, sin_p, tm):
    L = h.shape[0]
    t = min(ATTN_ROWS, L // ATTN_WIDE)
    hn = rms_fwd(h, _row(W['norm_mix'][0]), MXU_DTYPE)
    proj = matmul(hn, _w_in_padded(W), name="mix_in")
    logits = W['hg_lb_logits']
    lb = blocked_fwd(_lb_first, [full(logits)], [((1, HG_WIDTH), F32, (1, HG_WIDTH), lambda i: (0, 0))], 1, "hg_lb")[0]
    gain = _row(W['hg_out_norm'][0])
    o_hg, states = hgrn2_fwd(proj, lb, gain)
    mp = _mla_params(W)
    q, k, v = mla_prep_fwd(proj, cos_p, sin_p, mp, tm)
    scale = MLA_QK ** -0.5
    o_mla, lse = attn_fwd(q, k, v, scale, t)
    w_out = W['mix_w_out'][0]
    out = matmul(o_hg, w_out[:HG_WIDTH], add=h, name="mix_out_hg")
    out = matmul(o_mla, w_out[HG_WIDTH:], add=out, name="mix_out_mla")
    return out, (h, hn, proj, lb, o_hg, states, q, k, v, o_mla, lse)


def _mixer0_bwd(dout, saved, W, cos_p, sin_p, tm, part_way=None):
    h, hn, proj, lb, o_hg, states, q, k, v, o_mla, lse = saved
    L = h.shape[0]
    t = min(ATTN_ROWS, L // ATTN_WIDE)
    scale = MLA_QK ** -0.5
    w_out = W['mix_w_out'][0]
    gain = _row(W['hg_out_norm'][0])
    do_hg = matmul(dout, w_out[:HG_WIDTH], "nt", name="mix_do_hg")
    do_mla = matmul(dout, w_out[HG_WIDTH:], "nt", name="mix_do_mla")
    d_wout = jnp.concatenate([matmul(o_hg, dout, "tn", name="mix_dwout_hg"), matmul(o_mla, dout, "tn", name="mix_dwout_mla")],
                             axis=0)
    if part_way is not None:
        do_mla = part_way(do_mla)
    dq = attn_bwd_dq(q, k, v, o_mla, lse, do_mla, scale, t)
    dk, dv = attn_bwd_dkv(q, k, v, o_mla, lse, do_mla, scale, t)
    mp = _mla_params(W)
    d_mla, d_qa, d_wuq, d_kva, d_wukv, d_qnn, d_qnr, d_knn, d_knr = mla_prep_bwd(proj, cos_p, sin_p, mp, dq, dk, dv, tm)
    d_hg, d_lb, d_gain = hgrn2_bwd(proj, lb, gain, states, do_hg)
    w_in, n_hg = _w_in_padded(W), 4 * HG_WIDTH
    d_win = jnp.concatenate([matmul(hn, d_hg, "tn", name="mix_dwin_hg"), matmul(hn, d_mla, "tn", name="mix_dwin_mla")], axis=1)
    dhn = matmul(d_hg, w_in[:, :n_hg], "nt", name="mix_dhn_hg")
    dhn = matmul(d_mla, w_in[:, n_hg:], "nt", add=dhn, name="mix_dhn_mla")
    dh, d_g = rms_bwd(h, _row(W['norm_mix'][0]), dhn, dout)
    logits = W['hg_lb_logits']
    d_logits = blocked_bwd(_lb_first, [full(logits, 'acc')], [full(d_lb)], 1, "hg_lb_bwd")[0]
    d_wuq = d_wuq.reshape(MLA_Q_RANK, MLA_HEADS, MLA_DK)
    d_wuq = jnp.concatenate([d_wuq[..., :MLA_NOPE], _rope_unpad(d_wuq[..., MLA_NOPE:])], axis=-1)
    hw = MLA_HEADS * MLA_NOPE
    d_wukv = jnp.concatenate([d_wukv[:, :hw].reshape(MLA_KV_RANK, MLA_HEADS, MLA_NOPE),
                              d_wukv[:, hw:].reshape(MLA_KV_RANK, MLA_HEADS, MLA_V)], axis=-1)
    d_win = jnp.concatenate([d_win[:, :IN_WIDTH - MLA_ROPE], _rope_unpad(d_win[:, IN_WIDTH - MLA_ROPE:])], axis=1)
    d_win = d_win.reshape(d_win.shape[0], N_CHIPS, W_IN_SHARD).transpose(1, 0, 2)
    d_win = jnp.pad(d_win, ((0, 0), (0, 0), (0, W_IN_SHARD_PAD - W_IN_SHARD)))
    d_wout = d_wout.reshape(N_CHIPS, d_wout.shape[0] // N_CHIPS, d_wout.shape[1])
    grads = {'norm_mix': d_g, 'hg_lb_logits': d_logits, 'mix_w_in': d_win, 'hg_out_norm': d_gain,
             'mla_q_a_norm': d_qa, 'mla_w_uq': d_wuq.reshape(1, MLA_Q_RANK, -1), 'mla_kv_a_norm': d_kva,
             'mla_w_ukv': d_wukv.reshape(1, MLA_KV_RANK, -1), 'mla_qn_nope': d_qnn, 'mla_qn_rope': _rope_unpad(d_qnr),
             'mla_kn_nope': d_knn, 'mla_kn_rope': _rope_unpad(d_knr), 'mix_w_out': d_wout}
    return dh,grads


def _s5_inputs(W):
    G = W['s5_lam_re'].shape[1]
    return [W['s5_lam_re'][0], W['s5_lam_im'][0], W['s5_log_dt'][0].reshape(G, 1),
            W['s5_b_re'][0].reshape(G, -1), W['s5_b_im'][0].reshape(G, -1)]


def _mixer1_fwd(h, W, tm):
    L, D = h.shape
    u = rms_fwd(h, _row(W['norm_mix'][1]), F32)
    di = _s5_inputs(W)
    G = di[0].shape[0]
    sq, wide = ((G, S5_STATE), F32, (G, S5_STATE), lambda i: (0, 0)), ((G, S5_STATE * S5_GROUP), F32, (G, S5_STATE * S5_GROUP), lambda i: (0, 0))
    ar, ai, bbr, bbi = blocked_fwd(_s5_discretize, [full(a) for a in di], [sq, sq, wide, wide], 1, "s5_discretize")
    nb = G // S5_GB
    core = (_blockdiag_in(bbr.reshape(G, S5_STATE, S5_GROUP)), _blockdiag_in(bbi.reshape(G, S5_STATE, S5_GROUP)),
            ar.reshape(nb, 1, S5_LANES), ai.reshape(nb, 1, S5_LANES),
            _blockdiag_out(W['s5_c_re'][0]), _blockdiag_out(W['s5_c_im'][0]))
    y = s5_fwd(u, *core)
    d = W['s5_d']
    y2 = blocked_fwd(_s5_post, [rows(y, tm), rows(u, tm), full(d)], [_out((L, D), MXU_DTYPE, tm)], L // tm, "s5_post")[0]
    w_ab = jnp.concatenate([W['s5_w_glu_a'][0], W['s5_w_glu_b'][0]], axis=1)
    ab = matmul(y2, w_ab, name="s5_glu_in")
    out = blocked_fwd(lambda a, b, res: (res + _glu(a, b)[0],),
                      [rows(ab, tm, col=0, width=D), rows(ab, tm, col=1, width=D), rows(h, tm)], [_out((L, D), F32, tm)],
                      L // tm, "s5_glu")[0]
    return out, (h, u, core, y, y2, ab)


def _mixer1_bwd(dout, saved, W, tm):
    h, u, core, y, y2, ab = saved
    L, D = h.shape
    da, db = blocked_bwd(_glu, [rows(ab, tm, 'blk', col=0, width=D), rows(ab, tm, 'blk', col=1, width=D)], [rows(dout, tm)],
                         L // tm, "s5_glu_bwd")
    w_a, w_b = W['s5_w_glu_a'][0], W['s5_w_glu_b'][0]
    dy2 = matmul(da, w_a, "nt", name="s5_dy2_a")
    dy2 = matmul(db, w_b, "nt", add=dy2, name="s5_dy2_b")
    d_wa = matmul(y2, da, "tn", name="s5_dwa")
    d_wb = matmul(y2, db, "tn", name="s5_dwb")
    d = W['s5_d']
    dy, du_skip, d_d = blocked_bwd(_s5_post, [rows(y, tm, 'blk'), rows(u, tm, 'blk'), full(d, 'acc')], [rows(dy2, tm)], L // tm,
                                   "s5_post_bwd")
    du, dwr, dwi, dar, dai, dcr, dci = s5_bwd(u, *core, dy, du_skip, min(S5_BWD_CHUNK, L))
    di = _s5_inputs(W)
    G = di[0].shape[0]
    cts = [dar.reshape(G, S5_STATE), dai.reshape(G, S5_STATE), _blockdiag_in_t(dwr).reshape(G, -1), _blockdiag_in_t(dwi).reshape(G, -1)]
    d_lr, d_li, d_ldt, d_br, d_bi = blocked_bwd(_s5_discretize, [full(a, 'acc') for a in di], [full(c) for c in cts], 1,
                                                "s5_discretize_bwd")
    dh, d_g = rms_bwd(h, _row(W['norm_mix'][1]), du, dout)
    bshape = W['s5_b_re'].shape
    grads = {'norm_mix': d_g, 's5_lam_re': d_lr[None], 's5_lam_im': d_li[None], 's5_log_dt': d_ldt.reshape(1, G),
             's5_b_re': d_br.reshape(bshape), 's5_b_im': d_bi.reshape(bshape), 's5_c_re': _blockdiag_out_t(dcr)[None],
             's5_c_im': _blockdiag_out_t(dci)[None], 's5_d': d_d, 's5_w_glu_a': d_wa.reshape(N_CHIPS, -1, D), 's5_w_glu_b': d_wb.reshape(N_CHIPS, -1, D)}
    return dh,grads


def kernel(x, mem, positions, norm_mix, norm_xa, norm_mem, norm_ffn, xa_wq, xa_wk, xa_wv, xa_wo, xa_q_norm, xa_k_norm, ffn_w_up, ffn_conv_w, ffn_conv_b, ffn_w_down, hg_lb_logits, mix_w_in, hg_out_norm, mla_q_a_norm, mla_w_uq, mla_kv_a_norm, mla_w_ukv, mla_qn_nope, mla_qn_rope, mla_kn_nope, mla_kn_rope, mix_w_out, s5_lam_re, s5_lam_im, s5_log_dt, s5_b_re, s5_b_im, s5_c_re, s5_c_im, s5_d, s5_w_glu_a, s5_w_glu_b, loss_target, m_norm_mix, m_norm_xa, m_norm_mem, m_norm_ffn, m_xa_wq, m_xa_wk, m_xa_wv, m_xa_wo, m_xa_q_norm, m_xa_k_norm, m_ffn_w_up, m_ffn_conv_w, m_ffn_conv_b, m_ffn_w_down, m_hg_lb_logits, m_mix_w_in, m_hg_out_norm, m_mla_q_a_norm, m_mla_w_uq, m_mla_kv_a_norm, m_mla_w_ukv, m_mla_qn_nope, m_mla_qn_rope, m_mla_kn_nope, m_mla_kn_rope, m_mix_w_out, m_s5_lam_re, m_s5_lam_im, m_s5_log_dt, m_s5_b_re, m_s5_b_im, m_s5_c_re, m_s5_c_im, m_s5_d, m_s5_w_glu_a, m_s5_w_glu_b, v_norm_mix, v_norm_xa, v_norm_mem, v_norm_ffn, v_xa_wq, v_xa_wk, v_xa_wv, v_xa_wo, v_xa_q_norm, v_xa_k_norm, v_ffn_w_up, v_ffn_conv_w, v_ffn_conv_b, v_ffn_w_down, v_hg_lb_logits, v_mix_w_in, v_hg_out_norm, v_mla_q_a_norm, v_mla_w_uq, v_mla_kv_a_norm, v_mla_w_ukv, v_mla_qn_nope, v_mla_qn_rope, v_mla_kn_nope, v_mla_kn_rope, v_mix_w_out, v_s5_lam_re, v_s5_lam_im, v_s5_log_dt, v_s5_b_re, v_s5_b_im, v_s5_c_re, v_s5_c_im, v_s5_d, v_s5_w_glu_a, v_s5_w_glu_b):
    P = dict(locals())
    assert sorted(P) == sorted(INPUTS) and norm_mix.shape[0] == 2 and mix_w_in.shape[0] == 1
    x, mem, target = P['x'][0], P['mem'][0], P['loss_target'][0]
    L, D = x.shape
    tm = min(256, L)

    W = {n: P[n] for n in REPLICATED}
    W.update(_gather_weights(P))

    inv_freq = 1.0 / (ROPE_BASE ** (jnp.arange(0, MLA_ROPE, 2, dtype=F32) / MLA_ROPE))
    ang = P['positions'][0].astype(F32)[:, None] * inv_freq
    cos, sin, z = jnp.cos(ang), jnp.sin(ang), jnp.zeros_like(ang)
    cos_p = jnp.concatenate([cos, z, cos, z], axis=1)
    sin_p = jnp.concatenate([-sin, z, sin, z], axis=1)

    h, s_mix0 = _mixer0_fwd(x, W, cos_p, sin_p, tm)
    h, s_xa0 = _xattn_fwd(h, mem, W, 0, tm)
    h, s_ffn0 = _ffn_fwd(h, W, 0, tm)
    h, s_mix1 = _mixer1_fwd(h, W, tm)
    h, s_xa1 = _xattn_fwd(h, mem, W, 1, tm)
    h, s_ffn1 = _ffn_fwd(h, W, 1, tm)
    n = L // tm
    dh, parts = blocked_fwd(_loss_fn, [rows(h, tm), rows(target, tm)],
                            [_out((L, D), F32, tm), ((n * 8, 128), F32, (8, 128), lambda i: (i, 0))], n, "loss")
    loss = lax.psum(jnp.sum(parts), ("x", "y", "c"))

    layered = {}

    def collect(g, lyr):
        for k_, v_ in g.items():
            layered.setdefault(k_, {})[lyr] = v_

    results = {}

    def big_items(lyr, names):
        return [(n_, 0 if P[n_].shape[0] == 1 else lyr, layered[n_][lyr]) for n_ in names]

    per_layer = [n_ for n_ in BIG if P[n_].shape[0] == 2]
    second_mixer = ['s5_w_glu_a', 's5_w_glu_b']
    first_mixer = ['mix_w_in', 'mix_w_out']
    assert sorted(per_layer + second_mixer + first_mixer) == sorted(BIG)

    dh, g = _ffn_bwd(dh, s_ffn1, W, 1, tm)
    collect(g, 1)
    dh, g = _xattn_bwd(dh, s_xa1, mem, W, 1, tm)
    collect(g, 1)
    dh, g = _mixer1_bwd(dh, s_mix1, W, tm)
    collect(g, 1)
    late = _reduce_batch(big_items(1, per_layer + second_mixer), None, P, results, "late_layer", {'swap': 2, 'a2a': 3, 'join': 4})
    next(late)
    dh, g = _ffn_bwd(dh, s_ffn0, W, 0, tm)
    collect(g, 0)
    dh = late.send(dh)
    dh, g = _xattn_bwd(dh, s_xa0, mem, W, 0, tm)
    collect(g, 0)
    mid = _reduce_batch(big_items(0, per_layer), None, P, results, "first_layer", {'swap': 5, 'a2a': 6, 'join': 7})
    next(mid)
    dx, g = _mixer0_bwd(dh, s_mix0, W, cos_p, sin_p, tm, part_way=mid.send)
    collect(g, 0)

    GS = {}
    for name in SMALL:
        by_layer = [layered[name][lyr] for lyr in sorted(layered[name])]
        full_shape = W[name].shape
        GS[name] = (by_layer[0].reshape(full_shape) if len(by_layer) == 1
                    else jnp.stack([g_.reshape(full_shape[1:]) for g_ in by_layer]))
    small = _pack([GS[n_] for n_ in SMALL], 2 * N_CHIPS * ROW_MULT).reshape(N_CHIPS, -1, PACK_W)
    dx = late.send(dx)
    dx = mid.send(dx)
    last = _reduce_batch(big_items(0, first_mixer), small, P, results, "first_mixer", {'swap': 8, 'a2a': 9, 'join': 10})
    next(last)
    late.send(None)
    last_pair_sum = last.send(None)
    mid.send(last_pair_sum)
    mid_name = per_layer[-1]
    results[mid_name] = list(last.send(list(results[mid_name])))
    last.send(None)
    outs = list(_update_small(results['small_sum'], GS, P))
    for k_ in range(4):
        outs[k_].update({n_: results[n_][k_].reshape(P[n_].shape) for n_ in BIG})
    return (loss, dx[None], *[d[n_] for d in outs for n_ in WEIGHTS])
```

```python
import functools
import math

import jax
import jax.numpy as jnp
import numpy as np
from jax import lax
from jax.experimental import pallas as pl
from jax.experimental.pallas import tpu as pltpu
from jax.experimental.pallas import tpu_sc as plsc

F32 = jnp.float32
BF16 = jnp.bfloat16
MXU_DTYPE = BF16
HI = lax.Precision.HIGHEST
V7X_VMEM_LIMIT_BYTES = 56 * 1024 * 1024
EPS = 1e-6
MESH = pl.DeviceIdType.MESH

HG_HEADS, HG_DIM = 4, 128
HG_WIDTH = HG_HEADS * HG_DIM
HG_SUB = 32
HG_BLOCK = 64
MLA_HEADS, MLA_Q_RANK, MLA_KV_RANK = 4, 256, 128
MLA_NOPE, MLA_ROPE, MLA_V = 128, 64, 128
MLA_QK = MLA_NOPE + MLA_ROPE
MLA_DK = 256
ROPE_BASE = 10000.0
IN_WIDTH = 4 * HG_WIDTH + MLA_Q_RANK + MLA_KV_RANK + MLA_ROPE
IN_PAD = 4 * HG_WIDTH + MLA_Q_RANK + MLA_KV_RANK + 128
S5_GROUP, S5_STATE = 16, 64
S5_GB = 8
DT_MIN, DT_MAX = 1e-3, 1e-1
XA_HEADS = 4
CONV_W = 3
ADAM_LR, ADAM_B1, ADAM_B2, ADAM_EPS, ADAM_WD, ADAM_STEP = 0.001, 0.9, 0.999, 1e-08, 0.01, 10


def _cparams(sem):
    return pltpu.CompilerParams(dimension_semantics=sem, vmem_limit_bytes=V7X_VMEM_LIMIT_BYTES)


class Opd:
    def __init__(self, arr, block, imap, grad=None, gshape=None, gimap=None):
        self.arr, self.block, self.imap, self.grad = arr, block, imap, grad
        self.gshape = arr.shape if gshape is None else gshape
        self.gimap = imap if gimap is None else gimap

    def spec(self):
        return pl.BlockSpec(self.block, self.imap)

    def gspec(self):
        return pl.BlockSpec(self.block, self.gimap)


def rows(arr, tm, grad=None, col=0, width=None):
    width = arr.shape[1] if width is None else width
    return Opd(arr, (tm, width), lambda i, c=col: (i, c), grad, (arr.shape[0], width), lambda i: (i, 0))


def cols(arr, tn, grad=None):
    return Opd(arr, (arr.shape[0], tn), lambda j: (0, j), grad)


def full(arr, grad=None):
    return Opd(arr, arr.shape, lambda i: (0, 0), grad)


def _load(ref):
    v = ref[...]
    return v.astype(F32) if jnp.issubdtype(v.dtype, jnp.floating) else v


def blocked_fwd(f, opds, outs, n, name):
    n_in = len(opds)

    def body(*refs):
        ys = f(*[_load(r) for r in refs[:n_in]])
        for r, y in zip(refs[n_in:], ys):
            r[...] = y.astype(r.dtype)

    res = pl.pallas_call(
        body, name=name, grid=(n,),
        in_specs=[o.spec() for o in opds],
        out_specs=[pl.BlockSpec(b, m) for (_, _, b, m) in outs],
        out_shape=[jax.ShapeDtypeStruct(s, d) for (s, d, _, _) in outs],
        compiler_params=_cparams(("parallel",)),
    )(*[o.arr for o in opds])
    return res


def blocked_bwd(f, opds, dys, n, name, plus=None):
    n_in, n_dy = len(opds), len(dys)
    diff = [i for i, o in enumerate(opds) if o.grad]
    extra = [] if plus is None else [plus]

    def body(*refs):
        vals = [_load(r) for r in refs[:n_in]]

        def fd(*dv):
            allv = list(vals)
            for i, v in zip(diff, dv):
                allv[i] = v
            return tuple(f(*allv))

        ys, vjp = jax.vjp(fd, *[vals[i] for i in diff])
        cts = tuple(_load(r).astype(y.dtype) for r, y in zip(refs[n_in:n_in + n_dy], ys))
        gs = list(vjp(cts))
        if extra:
            gs[0] = gs[0] + _load(refs[n_in + n_dy])
        for r, g, i in zip(refs[n_in + n_dy + len(extra):], gs, diff):
            if opds[i].grad == 'acc':
                @pl.when(pl.program_id(0) == 0)
                def _(r=r):
                    r[...] = jnp.zeros(r.shape, r.dtype)
                r[...] += g.astype(r.dtype)
            else:
                r[...] = g.astype(r.dtype)

    any_acc = any(opds[i].grad == 'acc' for i in diff)
    res = pl.pallas_call(
        body, name=name, grid=(n,),
        in_specs=[o.spec() for o in opds + dys + extra],
        out_specs=[opds[i].gspec() for i in diff],
        out_shape=[jax.ShapeDtypeStruct(opds[i].gshape, F32) for i in diff],
        compiler_params=_cparams(("arbitrary" if any_acc else "parallel",)),
    )(*[o.arr for o in opds + dys + extra])
    return res


def _tile(dim, want):
    for t in range(want - want % 16, 0, -16):
        if dim % t == 0:
            return t
    assert dim <= want, (dim, want)
    return dim


MATMUL_VMEM_BUDGET = 40 * 1024 * 1024
MATMUL_ROWS = 512
MATMUL_CHUNK = 1024


def _widest(N, fits):
    for t in range(N - N % 128, 0, -128):
        if N % t == 0 and fits(t):
            return t
    return N


def matmul(a, b, mode="nn", out_dtype=F32, add=None, name="matmul", into=None, col_blocks=None):
    sa, sb, so = a.dtype.itemsize, b.dtype.itemsize, jnp.dtype(out_dtype).itemsize
    has_add = add is not None
    if mode == "tn":
        (K, M), (K2, N) = a.shape, b.shape
        assert K == K2 and not has_add and out_dtype == F32, (a.shape, b.shape)
        tk = _tile(K, MATMUL_CHUNK)
        tn = _widest(N, lambda t: 2 * (tk * M * sa + tk * t * sb + M * t * 4) <= MATMUL_VMEM_BUDGET)
        extra, alias = [], {}
        if into is not None:
            buf, lead = into[0], tuple(into[1:])
            if col_blocks is not None:
                assert N % col_blocks == 0 and (N // col_blocks) % 128 == 0 and tn >= N // col_blocks, (N, col_blocks, tn)
                tn = N // col_blocks
                assert buf.shape[len(lead):] == (M, tn), (buf.shape, lead, M, tn)
                out_spec = pl.BlockSpec((None,) * len(lead) + (M, tn), lambda j, k: lead[:-1] + (lead[-1] + j, 0, 0))
            else:
                assert buf.shape[len(lead):] == (M, N), (buf.shape, lead, M, N)
                out_spec = pl.BlockSpec((None,) * len(lead) + (M, tn), lambda j, k: lead + (0, j))
            out_shape = jax.ShapeDtypeStruct(buf.shape, F32)
            extra, alias = [buf], {2: 0}
        else:
            assert col_blocks is None
            out_spec = pl.BlockSpec((M, tn), lambda j, k: (0, j))
            out_shape = jax.ShapeDtypeStruct((M, N), F32)

        def body(a_ref, b_ref, *rest):
            o_ref = rest[-1]
            r = lax.dot_general(a_ref[...].astype(MXU_DTYPE), b_ref[...].astype(MXU_DTYPE), ((_TN), ((), ())),
                                preferred_element_type=F32)

            @pl.when(pl.program_id(1) == 0)
            def _():
                o_ref[...] = r

            @pl.when(pl.program_id(1) > 0)
            def _():
                o_ref[...] += r

        return pl.pallas_call(
            body, name=name, grid=(N // tn, K // tk),
            in_specs=[pl.BlockSpec((tk, M), lambda j, k: (k, 0)), pl.BlockSpec((tk, tn), lambda j, k: (k, j))]
            + [pl.BlockSpec(memory_space=pl.ANY)] * len(extra),
            out_specs=out_spec, out_shape=out_shape, input_output_aliases=alias,
            compiler_params=_cparams(("parallel", "arbitrary")),
        )(a, b, *extra)

    (M, K) = a.shape
    N = b.shape[1] if mode == "nn" else b.shape[0]
    assert K == (b.shape[0] if mode == "nn" else b.shape[1]), (a.shape, b.shape, mode)
    tm = _tile(M, MATMUL_ROWS)
    tn = _widest(N, lambda t: 2 * (tm * K * sa + K * t * sb + tm * t * (so + 4 * has_add)) <= MATMUL_VMEM_BUDGET)
    dims = ((_NN if mode == "nn" else _NT), ((), ()))

    def body(*refs):
        r = lax.dot_general(refs[0][...].astype(MXU_DTYPE), refs[1][...].astype(MXU_DTYPE), dims, preferred_element_type=F32)
        if has_add:
            r = r + refs[2][...].astype(F32)
        refs[-1][...] = r.astype(refs[-1].dtype)

    b_spec = pl.BlockSpec((K, tn), lambda j, i: (0, j)) if mode == "nn" else pl.BlockSpec((tn, K), lambda j, i: (j, 0))
    in_specs = [pl.BlockSpec((tm, K), lambda j, i: (i, 0)), b_spec]
    args = [a, b]
    if has_add:
        in_specs.append(pl.BlockSpec((tm, tn), lambda j, i: (i, j)))
        args.append(add)
    return pl.pallas_call(
        body, name=name, grid=(N // tn, M // tm),
        in_specs=in_specs,
        out_specs=pl.BlockSpec((tm, tn), lambda j, i: (i, j)),
        out_shape=jax.ShapeDtypeStruct((M, N), out_dtype),
        compiler_params=_cparams(("parallel", "parallel")),
    )(*args)


def _dot(a, b, dims, precision=None):
    if precision is None:
        a, b = a.astype(MXU_DTYPE), b.astype(MXU_DTYPE)
    return lax.dot_general(a, b, (dims, ((), ())), precision=precision, preferred_element_type=F32)


_NN = ((1,), (0,))
_NT = ((1,), (1,))
_TN = ((0,), (0,))


def _rms(x, gain):
    return x * lax.rsqrt(jnp.mean(x * x, axis=-1, keepdims=True) + EPS) * gain


def _hg_block(st_t, q, fl, iv, g, lb, gain):
    row = lax.broadcasted_iota(jnp.int32, (HG_SUB, HG_SUB), 0)
    col = lax.broadcasted_iota(jnp.int32, (HG_SUB, HG_SUB), 1)
    tri = (row >= col).astype(F32)
    heads = [slice(h * HG_DIM, (h + 1) * HG_DIM) for h in range(HG_HEADS)]
    fg = lb + (1.0 - lb) * jax.nn.sigmoid(fl)
    lf, kk, qf = jnp.log(fg), 1.0 - fg, jax.nn.silu(q)
    sts = [st_t[sl, :] for sl in heads]
    parts = [[] for _ in heads]
    for s in range(q.shape[0] // HG_SUB):
        r = slice(s * HG_SUB, (s + 1) * HG_SUB)
        b = lf[r]
        d = 1
        while d < HG_SUB:
            b = b + _shift_rows(b, d)
            d *= 2
        b_mid = jnp.sum(lf[r][:HG_SUB // 2], axis=0, keepdims=True)
        b_end = jnp.sum(lf[r], axis=0, keepdims=True)
        q_in, k_in = qf[r] * jnp.exp(b - b_mid), kk[r] * jnp.exp(b_mid - b)
        q_st, k_st, decay = qf[r] * jnp.exp(b), kk[r] * jnp.exp(b_end - b), jnp.exp(b_end)
        for h, sl in enumerate(heads):
            sc = _dot(q_in[:, sl], k_in[:, sl], _NT) * tri
            parts[h].append(_dot(sc, iv[r, sl], _NN) + _dot(q_st[:, sl], sts[h], _NT))
            sts[h] = sts[h] * decay[:, sl] + _dot(iv[r, sl], k_st[:, sl], _TN)
    outs = [_rms(jnp.concatenate(parts[h], axis=0), gain[:, sl]) * jax.nn.silu(g[:, sl]) for h, sl in enumerate(heads)]
    return jnp.concatenate(sts, axis=0), jnp.concatenate(outs, axis=1)


def _hg_specs(proj, nb):
    return [pl.BlockSpec((HG_BLOCK, HG_WIDTH), lambda i, c=c, f=nb: (f(i), c)) for c in range(4)]


def hgrn2_fwd(proj, lb, gain):
    L = proj.shape[0]
    n = L // HG_BLOCK

    def body(q, fl, iv, g, lb_r, gain_r, o_ref, st_ref, st):
        @pl.when(pl.program_id(0) == 0)
        def _():
            st[...] = jnp.zeros(st.shape, F32)

        st_ref[0] = st[...]
        new, o = _hg_block(st[...], q[...], fl[...], iv[...], g[...], lb_r[...], gain_r[...])
        st[...] = new
        o_ref[...] = o.astype(o_ref.dtype)

    pspec = pl.BlockSpec((1, HG_WIDTH), lambda i: (0, 0))
    return pl.pallas_call(
        body, name="hgrn2_fwd", grid=(n,),
        in_specs=_hg_specs(proj, lambda i: i) + [pspec, pspec],
        out_specs=[pl.BlockSpec((HG_BLOCK, HG_WIDTH), lambda i: (i, 0)),
                   pl.BlockSpec((1, HG_WIDTH, HG_DIM), lambda i: (i, 0, 0))],
        out_shape=[jax.ShapeDtypeStruct((L, HG_WIDTH), MXU_DTYPE),
                   jax.ShapeDtypeStruct((n, HG_WIDTH, HG_DIM), F32)],
        scratch_shapes=[pltpu.VMEM((HG_WIDTH, HG_DIM), F32)],
        compiler_params=_cparams(("arbitrary",)),
    )(proj, proj, proj, proj, lb, gain)


def hgrn2_bwd(proj, lb, gain, states, do):
    L = proj.shape[0]
    n = L // HG_BLOCK

    def body(q, fl, iv, g, lb_r, gain_r, st_r, do_r, dproj, dlb, dgain, dst):
        @pl.when(pl.program_id(0) == 0)
        def _():
            dst[...] = jnp.zeros(dst.shape, F32)
            dlb[...] = jnp.zeros(dlb.shape, F32)
            dgain[...] = jnp.zeros(dgain.shape, F32)

        _, vjp = jax.vjp(_hg_block, st_r[0], q[...], fl[...], iv[...], g[...], lb_r[...], gain_r[...])
        d_st, dq, dfl, div, dg, d_lb, d_gain = vjp((dst[...], do_r[...].astype(F32)))
        dst[...] = d_st
        dproj[:, 0 * HG_WIDTH:1 * HG_WIDTH] = dq
        dproj[:, 1 * HG_WIDTH:2 * HG_WIDTH] = dfl
        dproj[:, 2 * HG_WIDTH:3 * HG_WIDTH] = div
        dproj[:, 3 * HG_WIDTH:4 * HG_WIDTH] = dg
        dlb[...] += d_lb
        dgain[...] += d_gain

    rev = lambda i: n - 1 - i
    pspec = pl.BlockSpec((1, HG_WIDTH), lambda i: (0, 0))
    return pl.pallas_call(
        body, name="hgrn2_bwd", grid=(n,),
        in_specs=_hg_specs(proj, rev) + [pspec, pspec,
                                         pl.BlockSpec((1, HG_WIDTH, HG_DIM), lambda i: (rev(i), 0, 0)),
                                         pl.BlockSpec((HG_BLOCK, HG_WIDTH), lambda i: (rev(i), 0))],
        out_specs=[pl.BlockSpec((HG_BLOCK, 4 * HG_WIDTH), lambda i: (rev(i), 0)), pspec, pspec],
        out_shape=[jax.ShapeDtypeStruct((L, 4 * HG_WIDTH), F32),
                   jax.ShapeDtypeStruct((1, HG_WIDTH), F32), jax.ShapeDtypeStruct((1, HG_WIDTH), F32)],
        scratch_shapes=[pltpu.VMEM((HG_WIDTH, HG_DIM), F32)],
        compiler_params=_cparams(("arbitrary",)),
    )(proj, proj, proj, proj, lb, gain, states, do)


def _rope_rms(x, gain_p, cos_p, sin_p):
    n = x * lax.rsqrt(jnp.sum(x * x, axis=-1, keepdims=True) * (1.0 / MLA_ROPE) + EPS) * gain_p
    r = lax.broadcasted_iota(jnp.int32, (128, 128), 0)
    c = lax.broadcasted_iota(jnp.int32, (128, 128), 1)
    swap = (r == (c + 64) % 128).astype(F32)
    return n * cos_p + _dot(n, swap, _NN, HI) * sin_p


MLA_IN = MLA_Q_RANK + MLA_KV_RANK + 128


def _mla_prep(x, cos_p, sin_p, q_a, w_uq, kv_a, w_ukv, qn_nope, qn_rope, kn_nope, kn_rope):
    c_q, c_kv, kpe = x[:, :MLA_Q_RANK], x[:, MLA_Q_RANK:MLA_Q_RANK + MLA_KV_RANK], x[:, MLA_Q_RANK + MLA_KV_RANK:]
    q = _dot(_rms(c_q, q_a), w_uq, _NN)
    kv = _dot(_rms(c_kv, kv_a), w_ukv, _NN)
    k_pe = _rope_rms(kpe, kn_rope, cos_p, sin_p)
    qs, ks = [], []
    for h in range(MLA_HEADS):
        qs.append(_rms(q[:, h * MLA_DK:h * MLA_DK + MLA_NOPE], qn_nope))
        qs.append(_rope_rms(q[:, h * MLA_DK + MLA_NOPE:(h + 1) * MLA_DK], qn_rope, cos_p, sin_p))
        ks.append(_rms(kv[:, h * MLA_NOPE:(h + 1) * MLA_NOPE], kn_nope))
        ks.append(k_pe)
    return jnp.concatenate(qs, axis=1), jnp.concatenate(ks, axis=1), kv[:, MLA_HEADS * MLA_NOPE:]


def _mla_prep_opds(proj, cos_p, sin_p, params, tm, grads):
    g = (lambda k: k) if grads else (lambda k: None)
    assert (4 * HG_WIDTH) % MLA_IN == 0
    return ([rows(proj, tm, g('blk'), col=4 * HG_WIDTH // MLA_IN, width=MLA_IN), rows(cos_p, tm), rows(sin_p, tm)]
            + [full(p, g('acc')) for p in params])


def mla_prep_fwd(proj, cos_p, sin_p, params, tm):
    L = proj.shape[0]
    W = MLA_HEADS * MLA_DK
    rb = lambda w: (tm, w)
    outs = [((L, W), MXU_DTYPE, rb(W), lambda i: (i, 0)), ((L, W), MXU_DTYPE, rb(W), lambda i: (i, 0)),
            ((L, MLA_HEADS * MLA_V), MXU_DTYPE, rb(MLA_HEADS * MLA_V), lambda i: (i, 0))]
    return blocked_fwd(_mla_prep, _mla_prep_opds(proj, cos_p, sin_p, params, tm, False), outs, L // tm, "mla_prep_fwd")


def mla_prep_bwd(proj, cos_p, sin_p, params, dq, dk, dv, tm):
    L = proj.shape[0]
    return blocked_bwd(_mla_prep, _mla_prep_opds(proj, cos_p, sin_p, params, tm, True),
                       [rows(dq, tm), rows(dk, tm), rows(dv, tm)], L // tm, "mla_prep_bwd")


def _scores(q, k, scale, shift=None):
    s = _dot(q, k, _NT) * scale
    if shift is None:
        return s
    row = lax.broadcasted_iota(jnp.int32, s.shape, 0)
    col = lax.broadcasted_iota(jnp.int32, s.shape, 1)
    return jnp.where(col <= row + shift, s, -jnp.inf)


ATTN_ROWS = 1024
ATTN_WIDE = 1


def attn_fwd(q, k, v, scale, t):
    L = q.shape[0]
    tq = ATTN_WIDE * t

    def body(q_ref, k_ref, v_ref, o_ref, lse_ref):
        i = pl.program_id(1)
        qb = q_ref[...]

        def step(j, carry, shift=None):
            m, l, acc = carry
            kj = k_ref[pl.ds(pl.multiple_of(j * t, t), t), :]
            vj = v_ref[pl.ds(pl.multiple_of(j * t, t), t), :]
            s = _scores(qb, kj, scale, shift)
            m_new = jnp.maximum(m, jnp.max(s, axis=-1, keepdims=True))
            p = jnp.exp(s - m_new)
            alpha = jnp.exp(m - m_new)
            return m_new, alpha * l + jnp.sum(p, axis=-1, keepdims=True), alpha * acc + _dot(p, vj, _NN)

        carry = (jnp.full((tq, 1), -jnp.inf, F32), jnp.zeros((tq, 1), F32), jnp.zeros((tq, MLA_V), F32))
        carry = lax.fori_loop(0, ATTN_WIDE * i, step, carry)
        for d in range(ATTN_WIDE):
            carry = step(ATTN_WIDE * i + d, carry, -d * t)
        m, l, acc = carry
        o_ref[...] = acc / l
        lse_ref[...] = jnp.broadcast_to(m + jnp.log(l), lse_ref.shape)

    hspec = lambda rows_, w: pl.BlockSpec((rows_, w), lambda h, i: (0, h))
    bspec = lambda w: pl.BlockSpec((tq, w), lambda h, i: (i, h))
    return pl.pallas_call(
        body, name="attn_fwd", grid=(MLA_HEADS, L // tq),
        in_specs=[bspec(MLA_DK), hspec(L, MLA_DK), hspec(L, MLA_V)],
        out_specs=[bspec(MLA_V), bspec(MLA_V)],
        out_shape=[jax.ShapeDtypeStruct((L, MLA_HEADS * MLA_V), F32)] * 2,
        compiler_params=_cparams(("parallel", "parallel")),
    )(q, k, v)


def attn_bwd_dq(q, k, v, o, lse, do, scale, t):
    L = q.shape[0]
    tq = ATTN_WIDE * t

    def body(q_ref, k_ref, v_ref, o_ref, lse_ref, do_ref, dq_ref):
        i = pl.program_id(1)
        qb, dob = q_ref[...], do_ref[...]
        delta = jnp.sum(dob * o_ref[...], axis=-1, keepdims=True)
        lse_c = jnp.max(lse_ref[...], axis=-1, keepdims=True)

        def step(j, dq, shift=None):
            kj = k_ref[pl.ds(pl.multiple_of(j * t, t), t), :]
            vj = v_ref[pl.ds(pl.multiple_of(j * t, t), t), :]
            p = jnp.exp(_scores(qb, kj, scale, shift) - lse_c)
            ds = p * (_dot(dob, vj, _NT) - delta) * scale
            return dq + _dot(ds, kj, _NN)

        dq = lax.fori_loop(0, ATTN_WIDE * i, step, jnp.zeros((tq, MLA_DK), F32))
        for d in range(ATTN_WIDE):
            dq = step(ATTN_WIDE * i + d, dq, -d * t)
        dq_ref[...] = dq

    hspec = lambda w: pl.BlockSpec((L, w), lambda h, i: (0, h))
    bspec = lambda w: pl.BlockSpec((tq, w), lambda h, i: (i, h))
    return pl.pallas_call(
        body, name="attn_bwd_dq", grid=(MLA_HEADS, L // tq),
        in_specs=[bspec(MLA_DK), hspec(MLA_DK), hspec(MLA_V), bspec(MLA_V), bspec(MLA_V), bspec(MLA_V)],
        out_specs=bspec(MLA_DK),
        out_shape=jax.ShapeDtypeStruct((L, MLA_HEADS * MLA_DK), F32),
        compiler_params=_cparams(("parallel", "parallel")),
    )(q, k, v, o, lse, do)


def attn_bwd_dkv(q, k, v, o, lse, do, scale, t):
    L = q.shape[0]
    tk = ATTN_WIDE * t

    def body(q_ref, k_ref, v_ref, o_ref, lse_ref, do_ref, dk_ref, dv_ref):
        j = pl.program_id(1)
        kb, vb = k_ref[...], v_ref[...]

        def step(i, carry, shift=None):
            dk, dv = carry
            r = pl.ds(pl.multiple_of(i * t, t), t)
            qi, doi = q_ref[r, :], do_ref[r, :]
            delta = jnp.sum(doi * o_ref[r, :], axis=-1, keepdims=True)
            lse_c = jnp.max(lse_ref[r, :], axis=-1, keepdims=True)
            p = jnp.exp(_scores(qi, kb, scale, shift) - lse_c)
            ds = p * (_dot(doi, vb, _NT) - delta) * scale
            return dk + _dot(ds, qi, _TN), dv + _dot(p, doi, _TN)

        carry = (jnp.zeros((tk, MLA_DK), F32), jnp.zeros((tk, MLA_V), F32))
        for d in range(ATTN_WIDE):
            carry = step(ATTN_WIDE * j + d, carry, d * t)
        dk, dv = lax.fori_loop(ATTN_WIDE * (j + 1), L // t, step, carry)
        dk_ref[...] = dk
        dv_ref[...] = dv

    hspec = lambda w: pl.BlockSpec((L, w), lambda h, j: (0, h))
    bspec = lambda w: pl.BlockSpec((tk, w), lambda h, j: (j, h))
    return pl.pallas_call(
        body, name="attn_bwd_dkv", grid=(MLA_HEADS, L // tk),
        in_specs=[hspec(MLA_DK), bspec(MLA_DK), bspec(MLA_V), hspec(MLA_V), hspec(MLA_V), hspec(MLA_V)],
        out_specs=[bspec(MLA_DK), bspec(MLA_V)],
        out_shape=[jax.ShapeDtypeStruct((L, MLA_HEADS * MLA_DK), F32), jax.ShapeDtypeStruct((L, MLA_HEADS * MLA_V), F32)],
        compiler_params=_cparams(("parallel", "parallel")),
    )(q, k, v, o, lse, do)


S5_LANES = S5_GB * S5_STATE
S5_BWD_CHUNK = 1024


def _cmul(ar, ai, br, bi):
    return ar * br - ai * bi, ar * bi + ai * br


def _a_powers(ar, ai, reverse):
    a2 = _cmul(ar, ai, ar, ai)
    a4 = _cmul(*a2, *a2)
    row = lax.broadcasted_iota(jnp.int32, (8, ar.shape[1]), 0)
    e = (8 - row) if reverse else (row + 1)
    tr, ti = jnp.ones((8, ar.shape[1]), F32), jnp.zeros((8, ar.shape[1]), F32)
    for bit, (pr, pi) in ((1, (ar, ai)), (2, a2), (4, a4), (8, _cmul(*a4, *a4))):
        nr, ni = _cmul(tr, ti, pr, pi)
        sel = (e & bit) != 0
        tr, ti = jnp.where(sel, nr, tr), jnp.where(sel, ni, ti)
    pows = []
    for d, (pr, pi) in zip((1, 2, 4), ((ar, ai), a2, a4)):
        keep = (row < 8 - d) if reverse else (row >= d)
        pows.append((jnp.where(keep, pr, 0.0), jnp.where(keep, pi, 0.0)))
    return pows, (tr, ti)


def _scan8(xr, xi, pows, table, cr, ci, reverse):
    for d, (pr, pi) in zip((1, 2, 4), pows):
        shift = 8 - d if reverse else d
        mr, mi = _cmul(pr, pi, pltpu.roll(xr, shift, 0), pltpu.roll(xi, shift, 0))
        xr, xi = xr + mr, xi + mi
    mr, mi = _cmul(table[0], table[1], cr, ci)
    return xr + mr, xi + mi


def _row_of(x, r):
    row = lax.broadcasted_iota(jnp.int32, x.shape, 0)
    return jnp.sum(jnp.where(row == r, x, 0.0), axis=0, keepdims=True)


def _s5_scan_fwd(h_re, h_im, ar, ai, L):
    pows, table = _a_powers(ar, ai, False)

    def step(i, carry):
        r = pl.ds(pl.multiple_of(i * 8, 8), 8)
        xr, xi = _scan8(h_re[r, :], h_im[r, :], pows, table, carry[0], carry[1], False)
        h_re[r, :] = xr
        h_im[r, :] = xi
        return xr[7:8, :], xi[7:8, :]

    z = jnp.zeros((1, ar.shape[1]), F32)
    lax.fori_loop(0, L // 8, step, (z, z))


def _s5_specs(L):
    return [pl.BlockSpec((L, 128), lambda g: (0, g)),
            pl.BlockSpec((1, 128, S5_LANES), lambda g: (g, 0, 0)), pl.BlockSpec((1, 128, S5_LANES), lambda g: (g, 0, 0)),
            pl.BlockSpec((1, 1, S5_LANES), lambda g: (g, 0, 0)), pl.BlockSpec((1, 1, S5_LANES), lambda g: (g, 0, 0)),
            pl.BlockSpec((1, S5_LANES, 128), lambda g: (g, 0, 0)), pl.BlockSpec((1, S5_LANES, 128), lambda g: (g, 0, 0))]


def s5_fwd(u, w_re, w_im, a_re, a_im, c_re, c_im):
    L, D = u.shape

    def body(u_ref, wr, wi, ar, ai, cr, ci, y_ref, h_re, h_im):
        ub = u_ref[...]
        h_re[...] = _dot(ub, wr[0], _NN)
        h_im[...] = _dot(ub, wi[0], _NN)
        _s5_scan_fwd(h_re, h_im, ar[0], ai[0], L)
        y_ref[...] = _dot(h_re[...], cr[0], _NN) - _dot(h_im[...], ci[0], _NN)

    return pl.pallas_call(
        body, name="s5_fwd", grid=(D // 128,),
        in_specs=_s5_specs(L), out_specs=pl.BlockSpec((L, 128), lambda g: (0, g)),
        out_shape=jax.ShapeDtypeStruct((L, D), F32),
        scratch_shapes=[pltpu.VMEM((L, S5_LANES), F32), pltpu.VMEM((L, S5_LANES), F32)],
        compiler_params=_cparams(("parallel",)),
    )(u, w_re, w_im, a_re, a_im, c_re, c_im)


def s5_bwd(u, w_re, w_im, a_re, a_im, c_re, c_im, dy, du_plus, tc):
    L, D = u.shape
    nch = L // tc

    def body(u_ref, wr, wi, ar_ref, ai_ref, cr, ci, dy_ref, plus_ref, du_ref, dwr, dwi, dar, dai, dcr, dci, h_re, h_im,
             g_re, g_im):
        ar, ai = ar_ref[0], ai_ref[0]
        ub = u_ref[...]
        h_re[...] = _dot(ub, wr[0], _NN)
        h_im[...] = _dot(ub, wi[0], _NN)
        _s5_scan_fwd(h_re, h_im, ar, ai, L)
        dyb = dy_ref[...]
        dcr[0] = _dot(h_re[...], dyb, _TN)
        dci[0] = -_dot(h_im[...], dyb, _TN)
        pows, table = _a_powers(ar, -ai, True)
        dwr[0] = jnp.zeros((128, S5_LANES), F32)
        dwi[0] = jnp.zeros((128, S5_LANES), F32)
        z1 = jnp.zeros((1, S5_LANES), F32)
        z8 = jnp.zeros((8, S5_LANES), F32)

        def chunk(cc, carry):
            c0 = pl.multiple_of((nch - 1 - cc) * tc, tc)
            rows_c = pl.ds(c0, tc)
            dyc = dy_ref[rows_c, :]
            g_re[...] = _dot(dyc, cr[0], _NT)
            g_im[...] = -_dot(dyc, ci[0], _NT)

            def step(ii, cy):
                gr_c, gi_c, acc_r, acc_i = cy
                i8 = pl.multiple_of((tc // 8 - 1 - ii) * 8, 8)
                rl = pl.ds(i8, 8)
                xr, xi = _scan8(g_re[rl, :], g_im[rl, :], pows, table, gr_c, gi_c, True)
                g_re[rl, :] = xr
                g_im[rl, :] = xi
                t0 = c0 + i8
                hb_r, hb_i = h_re[pl.ds(t0, 8), :], h_im[pl.ds(t0, 8), :]
                tp = pl.multiple_of(jnp.maximum(t0 - 8, 0), 8)
                first = (t0 > 0).astype(F32)
                pr = h_re[pl.ds(tp, 8), :][7:8, :] * first
                pi = h_im[pl.ds(tp, 8), :][7:8, :] * first
                row = lax.broadcasted_iota(jnp.int32, xr.shape, 0)
                hp_r = jnp.where(row == 0, pr, pltpu.roll(hb_r, 1, 0))
                hp_i = jnp.where(row == 0, pi, pltpu.roll(hb_i, 1, 0))
                return (xr[0:1, :], xi[0:1, :],
                        acc_r + xr * hp_r + xi * hp_i, acc_i + xi * hp_r - xr * hp_i)

            cy = lax.fori_loop(0, tc // 8, step, carry)
            uc = u_ref[rows_c, :]
            gr, gi = g_re[...], g_im[...]
            du_ref[rows_c, :] = _dot(gr, wr[0], _NT) + _dot(gi, wi[0], _NT) + plus_ref[rows_c, :]
            dwr[0] += _dot(uc, gr, _TN)
            dwi[0] += _dot(uc, gi, _TN)
            return cy

        _, _, acc_r, acc_i = lax.fori_loop(0, nch, chunk, (z1, z1, z8, z8))
        dar[0] = jnp.sum(acc_r, axis=0, keepdims=True)
        dai[0] = jnp.sum(acc_i, axis=0, keepdims=True)

    specs = _s5_specs(L)
    return pl.pallas_call(
        body, name="s5_bwd", grid=(D // 128,),
        in_specs=specs + [pl.BlockSpec((L, 128), lambda g: (0, g))] * 2,
        out_specs=[pl.BlockSpec((L, 128), lambda g: (0, g))] + specs[1:],
        out_shape=[jax.ShapeDtypeStruct((L, D), F32)] + [jax.ShapeDtypeStruct(x.shape, F32)
                                                        for x in (w_re, w_im, a_re, a_im, c_re, c_im)],
        scratch_shapes=[pltpu.VMEM((L, S5_LANES), F32), pltpu.VMEM((L, S5_LANES), F32),
                        pltpu.VMEM((tc, S5_LANES), F32), pltpu.VMEM((tc, S5_LANES), F32)],
        compiler_params=_cparams(("parallel",)),
    )(u, w_re, w_im, a_re, a_im, c_re, c_im, dy, du_plus)


def _s5_discretize(lr, li, ldt, br, bi):
    dt = jnp.exp(ldt)
    mag = jnp.exp(lr * dt)
    ar, ai = mag * jnp.cos(li * dt), mag * jnp.sin(li * dt)
    den = lr * lr + li * li
    zr = ((ar - 1.0) * lr + ai * li) / den
    zi = (ai * lr - (ar - 1.0) * li) / den
    p = lax.broadcasted_iota(jnp.int32, (S5_STATE, S5_STATE * S5_GROUP), 0)
    c = lax.broadcasted_iota(jnp.int32, (S5_STATE, S5_STATE * S5_GROUP), 1)
    rep = (c // S5_GROUP == p).astype(F32)
    zr, zi = _dot(zr, rep, _NN, HI), _dot(zi, rep, _NN, HI)
    return ar, ai, zr * br - zi * bi, zr * bi + zi * br


def _conv_shift(x, d):
    row = lax.broadcasted_iota(jnp.int32, x.shape, 0)
    return jnp.where(row >= d, pltpu.roll(x, d, 0), 0.0)


def _conv_unshift(x, d):
    n = x.shape[0]
    row = lax.broadcasted_iota(jnp.int32, x.shape, 0)
    return jnp.where(row < n - d, pltpu.roll(x, n - d, 0), 0.0)


@functools.partial(jax.custom_vjp, nondiff_argnums=(1,))
def _shift_rows(x, d):
    return _conv_shift(x, d)


_shift_rows.defvjp(lambda x, d: (_conv_shift(x, d), None), lambda d, _, g: (_conv_unshift(g, d),))


def _conv_gate(ug, uv, wg0, wg1, wg2, wv0, wv1, wv2, bg, bv):
    def conv(u, w0, w1, w2, b):
        return u * w2 + _shift_rows(u, 1) * w1 + _shift_rows(u, 2) * w0 + b
    return (jax.nn.silu(conv(ug, wg0, wg1, wg2, bg)) * conv(uv, wv0, wv1, wv2, bv),)


def _rms_fn(x, gain):
    return (_rms(x, gain),)


def _softmax_rows(s):
    e = jnp.exp(s - lax.stop_gradient(jnp.max(s, axis=-1, keepdims=True)))
    return e / jnp.sum(e, axis=-1, keepdims=True)


def _xa_core(qp, k, v, q_gain):
    dh = qp.shape[1] // XA_HEADS
    outs = []
    for h in range(XA_HEADS):
        sl = slice(h * dh, (h + 1) * dh)
        p = _softmax_rows(_dot(_rms(qp[:, sl], q_gain), k[:, sl], _NT) * (dh ** -0.5))
        outs.append(_dot(p, v[:, sl], _NN))
    return (jnp.concatenate(outs, axis=1),)


def _mem_kv(mem, mem_gain, wk, wv, k_gain):
    m = _rms(mem, mem_gain)
    kp = _dot(m, wk, _NN)
    dh = kp.shape[1] // XA_HEADS
    k = jnp.concatenate([_rms(kp[:, h * dh:(h + 1) * dh], k_gain) for h in range(XA_HEADS)], axis=1)
    return k, _dot(m, wv, _NN)


def _s5_post(y, u, d):
    return (jax.nn.gelu(y + d * u),)


def _glu(a, b):
    return (a * jax.nn.sigmoid(b),)


def _lb_first(logits):
    e = jnp.exp(logits - lax.stop_gradient(jnp.max(logits, axis=0, keepdims=True)))
    return (_row_of(e, 0) / jnp.sum(e, axis=0, keepdims=True),)


def _loss_fn(y, t):
    e = y - t
    part = 0.5 * jnp.sum(e * e) / y.shape[1]
    return e * (1.0 / y.shape[1]), jnp.full((8, 128), part / (8 * 128), F32)


def _out(shape, dtype, tm):
    return (shape, dtype, (tm, shape[1]), lambda i: (i, 0))


LIGHT_ROWS = 512


def rms_fwd(h, gain, dtype):
    tm = min(LIGHT_ROWS, h.shape[0])
    return blocked_fwd(_rms_fn, [rows(h, tm), full(gain)], [_out(h.shape, dtype, tm)], h.shape[0] // tm, "rms_fwd")[0]


def rms_bwd(h, gain, dy, residual):
    tm = min(LIGHT_ROWS, h.shape[0])
    return blocked_bwd(_rms_fn, [rows(h, tm, 'blk'), full(gain, 'acc')], [rows(dy, tm)], h.shape[0] // tm, "rms_bwd",
                       plus=rows(residual, tm))


def _adamw_math(w, g, m, v):
    m = ADAM_B1 * m + (1.0 - ADAM_B1) * g
    v = ADAM_B2 * v + (1.0 - ADAM_B2) * jnp.square(g)
    m_hat = m / (1.0 - ADAM_B1 ** ADAM_STEP)
    v_hat = v / (1.0 - ADAM_B2 ** ADAM_STEP)
    return -ADAM_LR * (m_hat / (jnp.sqrt(v_hat) + ADAM_EPS) + ADAM_WD * w), m, v


def adamw(w, g, m, v, name):
    R = w.shape[0]
    tm = _tile(R, 256)
    assert g.shape == w.shape == m.shape == v.shape, (name, w.shape, g.shape)

    def body(w_ref, g_ref, m_ref, v_ref, d_ref, nm_ref, nv_ref):
        d_ref[...], nm_ref[...], nv_ref[...] = _adamw_math(w_ref[...], g_ref[...], m_ref[...], v_ref[...])

    spec = pl.BlockSpec((tm, w.shape[1]), lambda i: (i, 0))
    return pl.pallas_call(
        body, name=name, grid=(R // tm,), in_specs=[spec] * 4, out_specs=[spec] * 3,
        out_shape=[jax.ShapeDtypeStruct(w.shape, F32)] * 3, compiler_params=_cparams(("parallel",)),
    )(w, g, m, v)


def _index_operand(i):
    return jnp.reshape(i, (1,)).astype(jnp.int32)


def adamw_layer(w, mine, other, core, m, v, layer, bufs, name):
    H, C = mine.shape
    tm = _tile(H, 256)
    nb = H // tm
    assert w.shape[1:] == (2 * H, C) and all(b.shape == w.shape for b in bufs), (name, w.shape, mine.shape)

    def body(c_ref, w_ref, mine_ref, other_ref, m_ref, v_ref, *rest):
        g_out, d_ref, nm_ref, nv_ref = rest[-4:]
        g_ = jnp.where(pl.program_id(0) // nb == c_ref[0], mine_ref[...], other_ref[...])
        g_out[...] = g_
        d_ref[...], nm_ref[...], nv_ref[...] = _adamw_math(w_ref[...], g_, m_ref[...], v_ref[...])

    lspec = pl.BlockSpec((None, tm, C), lambda i, c: (layer, i, 0))
    half = lambda sign: pl.BlockSpec(
        (tm, C), lambda i, c: (jnp.clip(i - (c[0] if sign else 1 - c[0]) * nb, 0, nb - 1), 0))
    any_spec = pl.BlockSpec(memory_space=pl.ANY)
    grid_spec = pltpu.PrefetchScalarGridSpec(
        num_scalar_prefetch=1, grid=(2 * nb,),
        in_specs=[lspec, half(True), half(False), lspec, lspec] + [any_spec] * 4, out_specs=[lspec] * 4)
    return pl.pallas_call(
        body, name=name, grid_spec=grid_spec, out_shape=[jax.ShapeDtypeStruct(w.shape, F32)] * 4,
        input_output_aliases={6: 0, 7: 1, 8: 2, 9: 3}, compiler_params=_cparams(("parallel",)),
    )(_index_operand(core), w, mine, other, m, v, *bufs)


def add_own_half(x, core, theirs, name, out_dtype):
    nq, _, H, C = x.shape
    tm = _tile(H, 256)

    def body(c_ref, x_ref, t_ref, o_ref):
        o_ref[...] = (x_ref[...] + t_ref[...]).astype(o_ref.dtype)

    spec = pl.BlockSpec((None, tm, C), lambda q, i, c: (q, i, 0))
    grid_spec = pltpu.PrefetchScalarGridSpec(
        num_scalar_prefetch=1, grid=(nq, H // tm),
        in_specs=[pl.BlockSpec((None, None, tm, C), lambda q, i, c: (q, c[0], i, 0)), spec], out_specs=spec)
    return pl.pallas_call(
        body, name=name, grid_spec=grid_spec, out_shape=jax.ShapeDtypeStruct((nq, H, C), out_dtype),
        compiler_params=_cparams(("parallel", "parallel")),
    )(_index_operand(core), x, theirs)


def add_chips(pair, chip, got, name):
    n, R, C = got.shape
    tm = _tile(R, 256)

    def body(q_ref, *refs):
        acc = refs[0][...].astype(F32)
        for r in refs[1:-1]:
            acc = acc + r[...].astype(F32)
        refs[-1][...] = acc

    grid_spec = pltpu.PrefetchScalarGridSpec(
        num_scalar_prefetch=1, grid=(R // tm,),
        in_specs=[pl.BlockSpec((None, tm, C), lambda i, q: (q[0], i, 0))]
        + [pl.BlockSpec((None, tm, C), lambda i, q, j=j: (j, i, 0)) for j in range(n)],
        out_specs=pl.BlockSpec((tm, C), lambda i, q: (i, 0)))
    return pl.pallas_call(
        body, name=name, grid_spec=grid_spec, out_shape=jax.ShapeDtypeStruct((R, C), F32),
        compiler_params=_cparams(("parallel",)),
    )(_index_operand(chip), pair, *([got] * n))


_HBM = pl.BlockSpec(memory_space=pltpu.HBM)
N_CHIPS = 4


def _my_place():
    return lax.axis_index("x"), lax.axis_index("y"), lax.axis_index("c")


def _comm_call(body, name, xs, out_shapes, n_remote, n_local, sequencer=None):
    sems = [pltpu.SemaphoreType.DMA((n_remote,)), pltpu.SemaphoreType.DMA((n_remote,)),
            pltpu.SemaphoreType.DMA((max(n_local, 1),))]
    if sequencer is None:
        return pl.pallas_call(
            body, name=name, in_specs=[_HBM] * len(xs), out_specs=[_HBM] * len(out_shapes), out_shape=out_shapes,
            scratch_shapes=sems, compiler_params=pltpu.CompilerParams(has_side_effects=True),
        )(*xs)
    peers_of, collective_id = sequencer
    hbm = pltpu.MemorySpace.HBM
    x_refs = [jax.new_ref(x, memory_space=hbm) for x in xs]
    o_refs = [jax.empty_ref(s, memory_space=hbm) for s in out_shapes]

    @pl.kernel(mesh=plsc.ScalarSubcoreMesh(axis_name="sequencer", num_cores=1), name=name, scratch_types=tuple(sems),
               compiler_params=pltpu.CompilerParams(collective_id=collective_id))
    def launch(send_sems, recv_sems, local_sems):
        peers = peers_of(*_my_place())
        barrier = pltpu.get_barrier_semaphore()
        for peer in peers:
            pl.semaphore_signal(barrier, inc=1, device_id=peer, device_id_type=MESH)
        pl.semaphore_wait(barrier, len(peers))
        body(*x_refs, *o_refs, send_sems, recv_sems, local_sems)

    launch()
    return [o[...] for o in o_refs]


def _sibling(mx, my, mc):
    return [(mx, my, 1 - mc)]


def _same_core_of_other_chips(mx, my, mc):
    return [(tx, ty, mc) for tx, ty in _other_chips(mx, my)]


def _run(copies):
    for cp in copies:
        cp.start()
    for cp in copies:
        cp.wait()


def _other_chips(mx, my):
    return [(mx ^ (j >> 1), my ^ (j & 1)) for j in (1, 2, 3)]


def device_gather(x, name):
    H, C = x.shape

    def body(x_ref, o_ref, send_sems, recv_sems, local_sems):
        mx, my, mc = _my_place()
        dst = o_ref.at[2 * mx + my, mc]
        copies = [pltpu.make_async_copy(x_ref, dst, local_sems.at[0])]
        others = [(mx, my, 1 - mc)] + [(tx, ty, c) for tx, ty in _other_chips(mx, my) for c in (mc, 1 - mc)]
        for j, peer in enumerate(others):
            copies.append(pltpu.make_async_remote_copy(src_ref=x_ref, dst_ref=dst, send_sem=send_sems.at[j],
                                                       recv_sem=recv_sems.at[j], device_id=peer, device_id_type=MESH))
        _run(copies)

    out = _comm_call(body, name, [x], [jax.ShapeDtypeStruct((N_CHIPS, 2, H, C), x.dtype)], 7, 1)[0]
    return out.reshape(N_CHIPS * 2 * H, C)


def gather_two_level(xs, name):
    n = len(xs)
    shapes = [jax.ShapeDtypeStruct((2, N_CHIPS) + x.shape[1:], x.dtype) for x in xs]

    def body(*refs):
        x_refs, o_refs = refs[:n], refs[n:2 * n]
        send_sems, recv_sems, local_sems = refs[2 * n:]
        mx, my, mc = _my_place()
        q = 2 * mx + my
        first, local, second = [], [], []
        for i, (x_ref, o_ref) in enumerate(zip(x_refs, o_refs)):
            local.append(pltpu.make_async_copy(x_ref.at[mc], o_ref.at[mc, q], local_sems.at[i]))
            for j, (tx, ty) in enumerate(_other_chips(mx, my)):
                first.append(pltpu.make_async_remote_copy(
                    src_ref=x_ref.at[mc], dst_ref=o_ref.at[mc, q], send_sem=send_sems.at[4 * i + j],
                    recv_sem=recv_sems.at[4 * i + j], device_id=(tx, ty, mc), device_id_type=MESH))
            second.append(pltpu.make_async_remote_copy(
                src_ref=o_ref.at[mc], dst_ref=o_ref.at[mc], send_sem=send_sems.at[4 * i + 3],
                recv_sem=recv_sems.at[4 * i + 3], device_id=(mx, my, 1 - mc), device_id_type=MESH))
        for cp in local + first:
            cp.start()
        for cp in local:
            cp.wait()
        for cp in first:
            cp.wait_recv()
        _run(second)
        for cp in first:
            cp.wait_send()

    return _comm_call(body, name, xs, shapes, 4 * n, n)


def gather_two_level_sequencer(xs, name, collective_id):
    n = len(xs)
    hbm = pltpu.MemorySpace.HBM
    x_refs = [jax.new_ref(x, memory_space=hbm) for x in xs]
    o_refs = [jax.empty_ref(jax.ShapeDtypeStruct((2, N_CHIPS) + x.shape[1:], x.dtype), memory_space=hbm) for x in xs]

    @pl.kernel(mesh=plsc.ScalarSubcoreMesh(axis_name="sequencer", num_cores=1), name=name,
               scratch_types=(pltpu.SemaphoreType.DMA((4 * n,)), pltpu.SemaphoreType.DMA((4 * n,)),
                              pltpu.SemaphoreType.DMA((n,))),
               compiler_params=pltpu.CompilerParams(collective_id=collective_id))
    def launch(send_sems, recv_sems, local_sems):
        mx, my, mc = _my_place()
        peers = [(tx, ty, mc) for tx, ty in _other_chips(mx, my)] + [(mx, my, 1 - mc)]
        barrier = pltpu.get_barrier_semaphore()
        for peer in peers:
            pl.semaphore_signal(barrier, inc=1, device_id=peer, device_id_type=MESH)
        pl.semaphore_wait(barrier, len(peers))
        q = 2 * mx + my
        first, local, second = [], [], []
        for i, (x_ref, o_ref) in enumerate(zip(x_refs, o_refs)):
            local.append(pltpu.make_async_copy(x_ref.at[mc], o_ref.at[mc, q], local_sems.at[i]))
            for j, peer in enumerate(peers[:3]):
                first.append(pltpu.make_async_remote_copy(
                    src_ref=x_ref.at[mc], dst_ref=o_ref.at[mc, q], send_sem=send_sems.at[4 * i + j],
                    recv_sem=recv_sems.at[4 * i + j], device_id=peer, device_id_type=MESH))
            second.append(pltpu.make_async_remote_copy(
                src_ref=o_ref.at[mc], dst_ref=o_ref.at[mc], send_sem=send_sems.at[4 * i + 3],
                recv_sem=recv_sems.at[4 * i + 3], device_id=peers[3], device_id_type=MESH))
        for cp in local + first:
            cp.start()
        for cp in local:
            cp.wait()
        for cp in first:
            cp.wait_recv()
        _run(second)
        for cp in first:
            cp.wait_send()

    launch()
    return [o[...] for o in o_refs]


def pair_swap(xs, name, halves, collective_id=None):
    n = len(xs)
    shapes = [jax.ShapeDtypeStruct(x.shape[:1] + x.shape[2:] if halves else x.shape, x.dtype) for x in xs]

    def body(*refs):
        x_refs, o_refs = refs[:n], refs[n:2 * n]
        send_sems, recv_sems, _ = refs[2 * n:]
        mx, my, mc = _my_place()
        _run([pltpu.make_async_remote_copy(
            src_ref=x_ref.at[:, 1 - mc] if halves else x_ref, dst_ref=o_ref, send_sem=send_sems.at[i],
            recv_sem=recv_sems.at[i], device_id=(mx, my, 1 - mc), device_id_type=MESH)
            for i, (x_ref, o_ref) in enumerate(zip(x_refs, o_refs))])

    return _comm_call(body, name, xs, shapes, n, 0, None if collective_id is None else (_sibling, collective_id))


def chip_all_to_all(xs, name, collective_id=None):
    n = len(xs)
    shapes = [jax.ShapeDtypeStruct((N_CHIPS - 1,) + x.shape[1:], x.dtype) for x in xs]

    def body(*refs):
        x_refs, o_refs = refs[:n], refs[n:2 * n]
        send_sems, recv_sems, _ = refs[2 * n:]
        mx, my, mc = _my_place()
        copies = []
        for i, (x_ref, o_ref) in enumerate(zip(x_refs, o_refs)):
            for j, (tx, ty) in enumerate(_other_chips(mx, my)):
                copies.append(pltpu.make_async_remote_copy(
                    src_ref=x_ref.at[2 * tx + ty], dst_ref=o_ref.at[j], send_sem=send_sems.at[3 * i + j],
                    recv_sem=recv_sems.at[3 * i + j], device_id=(tx, ty, mc), device_id_type=MESH))
        _run(copies)

    return _comm_call(body, name, xs, shapes, 3 * n, 0,
                      None if collective_id is None else (_same_core_of_other_chips, collective_id))


WEIGHTS = ['norm_mix', 'norm_xa', 'norm_mem', 'norm_ffn', 'xa_wq', 'xa_wk', 'xa_wv', 'xa_wo', 'xa_q_norm', 'xa_k_norm',
           'ffn_w_up', 'ffn_conv_w', 'ffn_conv_b', 'ffn_w_down', 'hg_lb_logits', 'mix_w_in', 'hg_out_norm',
           'mla_q_a_norm', 'mla_w_uq', 'mla_kv_a_norm', 'mla_w_ukv', 'mla_qn_nope', 'mla_qn_rope', 'mla_kn_nope',
           'mla_kn_rope', 'mix_w_out', 's5_lam_re', 's5_lam_im', 's5_log_dt', 's5_b_re', 's5_b_im', 's5_c_re',
           's5_c_im', 's5_d', 's5_w_glu_a', 's5_w_glu_b']
INPUTS = ['x', 'mem', 'positions'] + WEIGHTS + ['loss_target'] + ['m_' + n for n in WEIGHTS] + ['v_' + n for n in WEIGHTS]
SHARD_AXIS = {'xa_wq': 1, 'xa_wk': 1, 'xa_wv': 1, 'xa_wo': 1, 'ffn_w_up': 2, 'ffn_conv_w': 2, 'ffn_w_down': 1,
              'mix_w_in': 2, 'mla_w_uq': 2, 'mla_w_ukv': 2, 'mix_w_out': 1, 's5_d': 1, 's5_w_glu_a': 1, 's5_w_glu_b': 1}
BIG = ['xa_wq', 'xa_wk', 'xa_wv', 'xa_wo', 'ffn_w_up', 'ffn_w_down', 'mix_w_in', 'mix_w_out', 's5_w_glu_a', 's5_w_glu_b']
FIRST_NEEDED = ('mix_w_in', 'mix_w_out')
SMALL_SHARDED = [n for n in WEIGHTS if n in SHARD_AXIS and n not in BIG]
REPLICATED = [n for n in WEIGHTS if n not in SHARD_AXIS]
SMALL = SMALL_SHARDED + REPLICATED
PACK_W = 1024
ROW_MULT = 16
W_IN_SHARD = IN_WIDTH // N_CHIPS
W_IN_SHARD_PAD = 640


def _pack(flats, mult=ROW_MULT):
    flat = jnp.concatenate([f.reshape(-1) for f in flats])
    unit = mult * PACK_W
    n = -(-flat.shape[0] // unit) * unit
    return jnp.pad(flat, (0, n - flat.shape[0])).reshape(n // PACK_W, PACK_W)


def _unpack(packed, shapes):
    flat, out, o = packed.reshape(-1), [], 0
    for s in shapes:
        n = math.prod(s)
        out.append(flat[o:o + n].reshape(s))
        o += n
    return out


def _rope_pad(w):
    z = jnp.zeros(w.shape[:-1] + (MLA_ROPE // 2,), w.dtype)
    return jnp.concatenate([w[..., :MLA_ROPE // 2], z, w[..., MLA_ROPE // 2:], z], axis=-1)


def _rope_unpad(g):
    return jnp.concatenate([g[..., :MLA_ROPE // 2], g[..., 64:64 + MLA_ROPE // 2]], axis=-1)


def _blockdiag_in(bb):
    nb = bb.shape[0] // S5_GB
    t = bb.reshape(nb, S5_GB, S5_STATE, S5_GROUP).transpose(0, 1, 3, 2)
    return jnp.einsum('bgmp,gh->bgmhp', t, jnp.eye(S5_GB, dtype=bb.dtype)).reshape(nb, S5_GB * S5_GROUP, S5_LANES)


def _blockdiag_in_t(dw):
    nb = dw.shape[0]
    t = jnp.einsum('bgmhp,gh->bgmp', dw.reshape(nb, S5_GB, S5_GROUP, S5_GB, S5_STATE), jnp.eye(S5_GB, dtype=dw.dtype))
    return t.transpose(0, 1, 3, 2).reshape(nb * S5_GB, S5_STATE, S5_GROUP)


def _blockdiag_out(c):
    nb = c.shape[0] // S5_GB
    t = c.reshape(nb, S5_GB, S5_GROUP, S5_STATE).transpose(0, 1, 3, 2)
    return jnp.einsum('bgpm,gh->bgphm', t, jnp.eye(S5_GB, dtype=c.dtype)).reshape(nb, S5_LANES, S5_GB * S5_GROUP)


def _blockdiag_out_t(dc):
    nb = dc.shape[0]
    t = jnp.einsum('bgphm,gh->bgpm', dc.reshape(nb, S5_GB, S5_STATE, S5_GB, S5_GROUP), jnp.eye(S5_GB, dtype=dc.dtype))
    return t.transpose(0, 1, 3, 2).reshape(nb * S5_GB, S5_GROUP, S5_STATE)


def _gather_weights(P):
    def halves(x):
        return x if x.shape[0] == 2 else x.reshape(2, x.shape[1] // 2, x.shape[2])

    now = [n for n in BIG if n in FIRST_NEEDED]
    later = [n for n in BIG if n not in FIRST_NEEDED]
    xs = [halves(P[n].astype(BF16)) for n in now] + [halves(_pack([P[n] for n in SMALL_SHARDED], 2 * ROW_MULT)[None])]
    got = gather_two_level(xs, "gather_weights")
    got, xs_later = lax.optimization_barrier((got, [halves(P[n].astype(BF16)) for n in later]))
    got_later = gather_two_level_sequencer(xs_later, "gather_weights_later", 1)
    full_w = {}
    for n, g in list(zip(now, got[:-1])) + list(zip(later, got_later)):
        two_layers, by_rows = P[n].shape[0] == 2, SHARD_AXIS[n] == 1
        if two_layers and by_rows:
            full_w[n] = g.reshape(2, N_CHIPS * g.shape[2], g.shape[3])
        elif two_layers:
            full_w[n] = g.transpose(0, 2, 1, 3).reshape(2, g.shape[2], N_CHIPS * g.shape[3])
        elif by_rows:
            full_w[n] = g.transpose(1, 0, 2, 3).reshape(1, 2 * N_CHIPS * g.shape[2], g.shape[3])
        else:
            full_w[n] = g.transpose(0, 2, 1, 3).reshape(1, 2 * g.shape[2], N_CHIPS * g.shape[3])
    small = got[-1].transpose(1, 0, 2, 3).reshape(N_CHIPS, -1, PACK_W)
    per_chip = [_unpack(small[q], [P[n].shape for n in SMALL_SHARDED]) for q in range(N_CHIPS)]
    for i, n in enumerate(SMALL_SHARDED):
        full_w[n] = jnp.concatenate([per_chip[q][i] for q in range(N_CHIPS)], axis=SHARD_AXIS[n])
    return full_w


def _reduce_batch(items, small, P, results, tag, ids):
    mx, my, mc = _my_place()
    q = 2 * mx + my
    ids = ids or {}
    names = [f"{n}_{lyr}" for n, lyr, _ in items] + (['small'] if small is not None else [])
    xs = [g.reshape(N_CHIPS, 2, g.shape[1] // 2, g.shape[2]) for g in [g for _, _, g in items] + ([small] if small is not None else [])]
    def after(vals, tie):
        return (vals, None) if tie is None else lax.optimization_barrier((vals, tie))

    theirs = pair_swap(xs, "grads_pair_swap_" + tag, True, ids.get('swap'))
    theirs, tie = after(theirs, (yield None))
    pair = [add_own_half(x, mc, t, "grads_pair_sum_" + n, F32 if n == 'small' else BF16) for x, t, n in zip(xs, theirs, names)]
    got = chip_all_to_all(pair, "grads_chip_all_to_all_" + tag, ids.get('a2a'))
    if tie is not None:
        tie, _ = lax.optimization_barrier((tie, pair[-1]))
    got, tie = after(got, (yield pair[-1] if tie is None else tie))
    summed = [add_chips(p, q, g, "grads_chip_sum_" + n) for p, g, n in zip(pair, got, names)]
    other = pair_swap(summed[:len(items)], "grads_pair_join_" + tag, False, ids.get('join'))
    if small is not None:
        results['small_sum'] = device_gather(summed[-1], "grads_small_gather")
    other, tie = after(other, (yield summed[-1] if tie is None else tie))
    for (n, lyr, _), s, o in zip(items, summed, other):
        if n == 'mix_w_in':
            s, o = s[:, :W_IN_SHARD], o[:, :W_IN_SHARD]
        view = (P[n].shape[0], 2 * s.shape[0], P[n].shape[-1])
        bufs = results.get(n) or [lax.empty(view, F32) for _ in range(4)]
        results[n] = adamw_layer(P[n].reshape(view), s, o, mc, P['m_' + n].reshape(view), P['v_' + n].reshape(view), lyr,
                                 bufs, f"adamw_{n}_{lyr}")
    yield tie


def _update_small(small_sum, GS, P):
    mx, my, _ = _my_place()
    q = 2 * mx + my
    g_small = dict(zip(SMALL, _unpack(small_sum, [GS[n].shape for n in SMALL])))
    for n in SMALL_SHARDED:
        s = P[n].shape[SHARD_AXIS[n]]
        g_small[n] = lax.dynamic_slice_in_dim(g_small[n], q * s, s, axis=SHARD_AXIS[n])
    grad, delta, new_m, new_v = {}, {}, {}, {}
    packed = lambda prefix: _pack([P[prefix + n] for n in SMALL])
    d, m_, v_ = adamw(packed(''), _pack([g_small[n] for n in SMALL]), packed('m_'), packed('v_'), "adamw_small")
    shapes = [P[n].shape for n in SMALL]
    grad.update(g_small)
    for out, pk in ((delta, d), (new_m, m_), (new_v, v_)):
        out.update(zip(SMALL, _unpack(pk, shapes)))
    return grad, delta, new_m, new_v


def _row(v):
    return v.reshape(1, -1)


def _xattn_fwd(h, mem, W, lyr, tm):
    g_xa, g_mem = _row(W['norm_xa'][lyr]), _row(W['norm_mem'][lyr])
    g_q, g_k = _row(W['xa_q_norm'][lyr]), _row(W['xa_k_norm'][lyr])
    wq, wk, wv, wo = (W[n][lyr] for n in ('xa_wq', 'xa_wk', 'xa_wv', 'xa_wo'))
    L, D = h.shape
    M = mem.shape[0]
    hx = rms_fwd(h, g_xa, MXU_DTYPE)
    qp = matmul(hx, wq, name="xa_q")
    kv_opds = [full(mem), full(g_mem), full(wk), full(wv), full(g_k)]
    k, v = blocked_fwd(_mem_kv, kv_opds, [((M, D), F32, (M, D), lambda i: (0, 0))] * 2, 1, "xa_mem_kv")
    o = blocked_fwd(_xa_core, [rows(qp, tm), full(k), full(v), full(g_q)], [_out((L, D), MXU_DTYPE, tm)], L // tm,
                    "xa_core")[0]
    out = matmul(o, wo, add=h, name="xa_o")
    return out, (h, hx, qp, k, v, o)


def _xattn_bwd(dout, saved, mem, W, lyr, tm):
    h, hx, qp, k, v, o = saved
    g_xa, g_mem = _row(W['norm_xa'][lyr]), _row(W['norm_mem'][lyr])
    g_q, g_k = _row(W['xa_q_norm'][lyr]), _row(W['xa_k_norm'][lyr])
    wq, wk, wv, wo = (W[n][lyr] for n in ('xa_wq', 'xa_wk', 'xa_wv', 'xa_wo'))
    L = h.shape[0]
    do = matmul(dout, wo, "nt", name="xa_do")
    d_wo = matmul(o, dout, "tn", name="xa_dwo")
    dqp, dk, dv, d_gq = blocked_bwd(_xa_core, [rows(qp, tm, 'blk'), full(k, 'acc'), full(v, 'acc'), full(g_q, 'acc')],
                                    [rows(do, tm)], L // tm, "xa_core_bwd")
    d_wq = matmul(hx, dqp, "tn", name="xa_dwq")
    dhx = matmul(dqp, wq, "nt", name="xa_dhx")
    dh, d_gxa = rms_bwd(h, g_xa, dhx, dout)
    d_gmem, d_wk, d_wv, d_gk = blocked_bwd(
        _mem_kv, [full(mem), full(g_mem, 'acc'), full(wk, 'acc'), full(wv, 'acc'), full(g_k, 'acc')],
        [full(dk), full(dv)], 1, "xa_mem_kv_bwd")
    by_chip = lambda g: g.reshape(N_CHIPS, g.shape[0] // N_CHIPS, g.shape[1])
    grads = {'norm_xa': d_gxa, 'norm_mem': d_gmem, 'xa_q_norm': d_gq, 'xa_k_norm': d_gk,
             'xa_wq': by_chip(d_wq), 'xa_wk': by_chip(d_wk), 'xa_wv': by_chip(d_wv), 'xa_wo': by_chip(d_wo)}
    return dh, grads


def _conv_params(W, lyr):
    cw, cb = W['ffn_conv_w'][lyr], W['ffn_conv_b'][lyr]
    F = cw.shape[1] // 2
    return [cw[0:1, :F], cw[1:2, :F], cw[2:3, :F], cw[0:1, F:], cw[1:2, F:], cw[2:3, F:], _row(cb[:F]), _row(cb[F:])]


def _ffn_fwd(h, W, lyr, tm):
    L, D = h.shape
    w_up, w_down = W['ffn_w_up'][lyr], W['ffn_w_down'][lyr]
    F = w_down.shape[0]
    hf = rms_fwd(h, _row(W['norm_ffn'][lyr]), MXU_DTYPE)
    ug = matmul(hf, w_up[:, :F], name="ffn_up_gate")
    uv = matmul(hf, w_up[:, F:], name="ffn_up_value")
    opds = [cols(ug, 128), cols(uv, 128)] + [cols(p, 128) for p in _conv_params(W, lyr)]
    a = blocked_fwd(_conv_gate, opds, [((L, F), MXU_DTYPE, (L, 128), lambda j: (0, j))], F // 128, "ffn_conv_gate")[0]
    out = matmul(a, w_down, add=h, name="ffn_down")
    return out, (h, hf, ug, uv, a)


def _ffn_bwd(dout, saved, W, lyr, tm):
    h, hf, ug, uv, a = saved
    w_up, w_down = W['ffn_w_up'][lyr], W['ffn_w_down'][lyr]
    F = w_down.shape[0]
    da = matmul(dout, w_down, "nt", name="ffn_da")
    d_wdown = matmul(a, dout, "tn", name="ffn_dwdown")
    opds = [cols(ug, 128, 'blk'), cols(uv, 128, 'blk')] + [cols(p, 128, 'blk') for p in _conv_params(W, lyr)]
    gs = blocked_bwd(_conv_gate, opds, [cols(da, 128)], F // 128, "ffn_conv_gate_bwd")
    dug, duv = gs[0], gs[1]
    d_cw = jnp.concatenate([jnp.concatenate(gs[2:5], axis=0), jnp.concatenate(gs[5:8], axis=0)], axis=1)
    d_cb = jnp.concatenate([gs[8], gs[9]], axis=1)[0]
    half = N_CHIPS // 2
    d_wup = lax.empty((N_CHIPS, hf.shape[1], w_up.shape[1] // N_CHIPS), F32)
    d_wup = matmul(hf, dug, "tn", name="ffn_dwup_gate", into=(d_wup, 0), col_blocks=half)
    d_wup = matmul(hf, duv, "tn", name="ffn_dwup_value", into=(d_wup, half), col_blocks=half)
    dhf = matmul(dug, w_up[:, :F], "nt", name="ffn_dhf_gate")
    dhf = matmul(duv, w_up[:, F:], "nt", add=dhf, name="ffn_dhf_value")
    dh, d_g = rms_bwd(h, _row(W['norm_ffn'][lyr]), dhf, dout)
    d_wdown = d_wdown.reshape(N_CHIPS, F // N_CHIPS, d_wdown.shape[1])
    return dh, {'norm_ffn': d_g, 'ffn_w_up': d_wup, 'ffn_conv_w': d_cw, 'ffn_conv_b': d_cb, 'ffn_w_down': d_wdown}


def _mla_params(W):
    w_uq = W['mla_w_uq'][0].reshape(MLA_Q_RANK, MLA_HEADS, MLA_QK)
    w_uq = jnp.concatenate([w_uq[..., :MLA_NOPE], _rope_pad(w_uq[..., MLA_NOPE:])], axis=-1)
    w_ukv = W['mla_w_ukv'][0].reshape(MLA_KV_RANK, MLA_HEADS, MLA_NOPE + MLA_V)
    w_ukv = jnp.concatenate([w_ukv[..., :MLA_NOPE].reshape(MLA_KV_RANK, -1), w_ukv[..., MLA_NOPE:].reshape(MLA_KV_RANK, -1)],
                            axis=1)
    return [_row(W['mla_q_a_norm'][0]), w_uq.reshape(MLA_Q_RANK, MLA_HEADS * MLA_DK), _row(W['mla_kv_a_norm'][0]), w_ukv,
            _row(W['mla_qn_nope'][0]), _row(_rope_pad(W['mla_qn_rope'][0])), _row(W['mla_kn_nope'][0]),
            _row(_rope_pad(W['mla_kn_rope'][0]))]


def _w_in_padded(W):
    w = W['mix_w_in'][0]
    return jnp.concatenate([w[:, :IN_WIDTH - MLA_ROPE], _rope_pad(w[:, IN_WIDTH - MLA_ROPE:])], axis=1)


def _mixer0_fwd(h, W, cos_p, sin_p, tm):
    L = h.shape[0]
    t = min(ATTN_ROWS, L // ATTN_WIDE)
    hn = rms_fwd(h, _row(W['norm_mix'][0]), MXU_DTYPE)
    proj = matmul(hn, _w_in_padded(W), name="mix_in")
    logits = W['hg_lb_logits']
    lb = blocked_fwd(_lb_first, [full(logits)], [((1, HG_WIDTH), F32, (1, HG_WIDTH), lambda i: (0, 0))], 1, "hg_lb")[0]
    gain = _row(W['hg_out_norm'][0])
    o_hg, states = hgrn2_fwd(proj, lb, gain)
    mp = _mla_params(W)
    q, k, v = mla_prep_fwd(proj, cos_p, sin_p, mp, tm)
    scale = MLA_QK ** -0.5
    o_mla, lse = attn_fwd(q, k, v, scale, t)
    w_out = W['mix_w_out'][0]
    out = matmul(o_hg, w_out[:HG_WIDTH], add=h, name="mix_out_hg")
    out = matmul(o_mla, w_out[HG_WIDTH:], add=out, name="mix_out_mla")
    return out, (h, hn, proj, lb, o_hg, states, q, k, v, o_mla, lse)


def _mixer0_bwd(dout, saved, W, cos_p, sin_p, tm, part_way=None):
    h, hn, proj, lb, o_hg, states, q, k, v, o_mla, lse = saved
    L = h.shape[0]
    t = min(ATTN_ROWS, L // ATTN_WIDE)
    scale = MLA_QK ** -0.5
    w_out = W['mix_w_out'][0]
    gain = _row(W['hg_out_norm'][0])
    do_hg = matmul(dout, w_out[:HG_WIDTH], "nt", name="mix_do_hg")
    do_mla = matmul(dout, w_out[HG_WIDTH:], "nt", name="mix_do_mla")
    d_wout = jnp.concatenate([matmul(o_hg, dout, "tn", name="mix_dwout_hg"), matmul(o_mla, dout, "tn", name="mix_dwout_mla")],
                             axis=0)
    if part_way is not None:
        do_mla = part_way(do_mla)
    dq = attn_bwd_dq(q, k, v, o_mla, lse, do_mla, scale, t)
    dk, dv = attn_bwd_dkv(q, k, v, o_mla, lse, do_mla, scale, t)
    mp = _mla_params(W)
    d_mla, d_qa, d_wuq, d_kva, d_wukv, d_qnn, d_qnr, d_knn, d_knr = mla_prep_bwd(proj, cos_p, sin_p, mp, dq, dk, dv, tm)
    d_hg, d_lb, d_gain = hgrn2_bwd(proj, lb, gain, states, do_hg)
    w_in, n_hg = _w_in_padded(W), 4 * HG_WIDTH
    d_win = jnp.concatenate([matmul(hn, d_hg, "tn", name="mix_dwin_hg"), matmul(hn, d_mla, "tn", name="mix_dwin_mla")], axis=1)
    dhn = matmul(d_hg, w_in[:, :n_hg], "nt", name="mix_dhn_hg")
    dhn = matmul(d_mla, w_in[:, n_hg:], "nt", add=dhn, name="mix_dhn_mla")
    dh, d_g = rms_bwd(h, _row(W['norm_mix'][0]), dhn, dout)
    logits = W['hg_lb_logits']
    d_logits = blocked_bwd(_lb_first, [full(logits, 'acc')], [full(d_lb)], 1, "hg_lb_bwd")[0]
    d_wuq = d_wuq.reshape(MLA_Q_RANK, MLA_HEADS, MLA_DK)
    d_wuq = jnp.concatenate([d_wuq[..., :MLA_NOPE], _rope_unpad(d_wuq[..., MLA_NOPE:])], axis=-1)
    hw = MLA_HEADS * MLA_NOPE
    d_wukv = jnp.concatenate([d_wukv[:, :hw].reshape(MLA_KV_RANK, MLA_HEADS, MLA_NOPE),
                              d_wukv[:, hw:].reshape(MLA_KV_RANK, MLA_HEADS, MLA_V)], axis=-1)
    d_win = jnp.concatenate([d_win[:, :IN_WIDTH - MLA_ROPE], _rope_unpad(d_win[:, IN_WIDTH - MLA_ROPE:])], axis=1)
    d_win = d_win.reshape(d_win.shape[0], N_CHIPS, W_IN_SHARD).transpose(1, 0, 2)
    d_win = jnp.pad(d_win, ((0, 0), (0, 0), (0, W_IN_SHARD_PAD - W_IN_SHARD)))
    d_wout = d_wout.reshape(N_CHIPS, d_wout.shape[0] // N_CHIPS, d_wout.shape[1])
    grads = {'norm_mix': d_g, 'hg_lb_logits': d_logits, 'mix_w_in': d_win, 'hg_out_norm': d_gain,
             'mla_q_a_norm': d_qa, 'mla_w_uq': d_wuq.reshape(1, MLA_Q_RANK, -1), 'mla_kv_a_norm': d_kva,
             'mla_w_ukv': d_wukv.reshape(1, MLA_KV_RANK, -1), 'mla_qn_nope': d_qnn, 'mla_qn_rope': _rope_unpad(d_qnr),
             'mla_kn_nope': d_knn, 'mla_kn_rope': _rope_unpad(d_knr), 'mix_w_out': d_wout}
    return dh,grads


def _s5_inputs(W):
    G = W['s5_lam_re'].shape[1]
    return [W['s5_lam_re'][0], W['s5_lam_im'][0], W['s5_log_dt'][0].reshape(G, 1),
            W['s5_b_re'][0].reshape(G, -1), W['s5_b_im'][0].reshape(G, -1)]


def _mixer1_fwd(h, W, tm):
    L, D = h.shape
    u = rms_fwd(h, _row(W['norm_mix'][1]), F32)
    di = _s5_inputs(W)
    G = di[0].shape[0]
    sq, wide = ((G, S5_STATE), F32, (G, S5_STATE), lambda i: (0, 0)), ((G, S5_STATE * S5_GROUP), F32, (G, S5_STATE * S5_GROUP), lambda i: (0, 0))
    ar, ai, bbr, bbi = blocked_fwd(_s5_discretize, [full(a) for a in di], [sq, sq, wide, wide], 1, "s5_discretize")
    nb = G // S5_GB
    core = (_blockdiag_in(bbr.reshape(G, S5_STATE, S5_GROUP)), _blockdiag_in(bbi.reshape(G, S5_STATE, S5_GROUP)),
            ar.reshape(nb, 1, S5_LANES), ai.reshape(nb, 1, S5_LANES),
            _blockdiag_out(W['s5_c_re'][0]), _blockdiag_out(W['s5_c_im'][0]))
    y = s5_fwd(u, *core)
    d = W['s5_d']
    y2 = blocked_fwd(_s5_post, [rows(y, tm), rows(u, tm), full(d)], [_out((L, D), MXU_DTYPE, tm)], L // tm, "s5_post")[0]
    w_ab = jnp.concatenate([W['s5_w_glu_a'][0], W['s5_w_glu_b'][0]], axis=1)
    ab = matmul(y2, w_ab, name="s5_glu_in")
    out = blocked_fwd(lambda a, b, res: (res + _glu(a, b)[0],),
                      [rows(ab, tm, col=0, width=D), rows(ab, tm, col=1, width=D), rows(h, tm)], [_out((L, D), F32, tm)],
                      L // tm, "s5_glu")[0]
    return out, (h, u, core, y, y2, ab)


def _mixer1_bwd(dout, saved, W, tm):
    h, u, core, y, y2, ab = saved
    L, D = h.shape
    da, db = blocked_bwd(_glu, [rows(ab, tm, 'blk', col=0, width=D), rows(ab, tm, 'blk', col=1, width=D)], [rows(dout, tm)],
                         L // tm, "s5_glu_bwd")
    w_a, w_b = W['s5_w_glu_a'][0], W['s5_w_glu_b'][0]
    dy2 = matmul(da, w_a, "nt", name="s5_dy2_a")
    dy2 = matmul(db, w_b, "nt", add=dy2, name="s5_dy2_b")
    d_wa = matmul(y2, da, "tn", name="s5_dwa")
    d_wb = matmul(y2, db, "tn", name="s5_dwb")
    d = W['s5_d']
    dy, du_skip, d_d = blocked_bwd(_s5_post, [rows(y, tm, 'blk'), rows(u, tm, 'blk'), full(d, 'acc')], [rows(dy2, tm)], L // tm,
                                   "s5_post_bwd")
    du, dwr, dwi, dar, dai, dcr, dci = s5_bwd(u, *core, dy, du_skip, min(S5_BWD_CHUNK, L))
    di = _s5_inputs(W)
    G = di[0].shape[0]
    cts = [dar.reshape(G, S5_STATE), dai.reshape(G, S5_STATE), _blockdiag_in_t(dwr).reshape(G, -1), _blockdiag_in_t(dwi).reshape(G, -1)]
    d_lr, d_li, d_ldt, d_br, d_bi = blocked_bwd(_s5_discretize, [full(a, 'acc') for a in di], [full(c) for c in cts], 1,
                                                "s5_discretize_bwd")
    dh, d_g = rms_bwd(h, _row(W['norm_mix'][1]), du, dout)
    bshape = W['s5_b_re'].shape
    grads = {'norm_mix': d_g, 's5_lam_re': d_lr[None], 's5_lam_im': d_li[None], 's5_log_dt': d_ldt.reshape(1, G),
             's5_b_re': d_br.reshape(bshape), 's5_b_im': d_bi.reshape(bshape), 's5_c_re': _blockdiag_out_t(dcr)[None],
             's5_c_im': _blockdiag_out_t(dci)[None], 's5_d': d_d, 's5_w_glu_a': d_wa.reshape(N_CHIPS, -1, D), 's5_w_glu_b': d_wb.reshape(N_CHIPS, -1, D)}
    return dh,grads


def kernel(x, mem, positions, norm_mix, norm_xa, norm_mem, norm_ffn, xa_wq, xa_wk, xa_wv, xa_wo, xa_q_norm, xa_k_norm, ffn_w_up, ffn_conv_w, ffn_conv_b, ffn_w_down, hg_lb_logits, mix_w_in, hg_out_norm, mla_q_a_norm, mla_w_uq, mla_kv_a_norm, mla_w_ukv, mla_qn_nope, mla_qn_rope, mla_kn_nope, mla_kn_rope, mix_w_out, s5_lam_re, s5_lam_im, s5_log_dt, s5_b_re, s5_b_im, s5_c_re, s5_c_im, s5_d, s5_w_glu_a, s5_w_glu_b, loss_target, m_norm_mix, m_norm_xa, m_norm_mem, m_norm_ffn, m_xa_wq, m_xa_wk, m_xa_wv, m_xa_wo, m_xa_q_norm, m_xa_k_norm, m_ffn_w_up, m_ffn_conv_w, m_ffn_conv_b, m_ffn_w_down, m_hg_lb_logits, m_mix_w_in, m_hg_out_norm, m_mla_q_a_norm, m_mla_w_uq, m_mla_kv_a_norm, m_mla_w_ukv, m_mla_qn_nope, m_mla_qn_rope, m_mla_kn_nope, m_mla_kn_rope, m_mix_w_out, m_s5_lam_re, m_s5_lam_im, m_s5_log_dt, m_s5_b_re, m_s5_b_im, m_s5_c_re, m_s5_c_im, m_s5_d, m_s5_w_glu_a, m_s5_w_glu_b, v_norm_mix, v_norm_xa, v_norm_mem, v_norm_ffn, v_xa_wq, v_xa_wk, v_xa_wv, v_xa_wo, v_xa_q_norm, v_xa_k_norm, v_ffn_w_up, v_ffn_conv_w, v_ffn_conv_b, v_ffn_w_down, v_hg_lb_logits, v_mix_w_in, v_hg_out_norm, v_mla_q_a_norm, v_mla_w_uq, v_mla_kv_a_norm, v_mla_w_ukv, v_mla_qn_nope, v_mla_qn_rope, v_mla_kn_nope, v_mla_kn_rope, v_mix_w_out, v_s5_lam_re, v_s5_lam_im, v_s5_log_dt, v_s5_b_re, v_s5_b_im, v_s5_c_re, v_s5_c_im, v_s5_d, v_s5_w_glu_a, v_s5_w_glu_b):
    P = dict(locals())
    assert sorted(P) == sorted(INPUTS) and norm_mix.shape[0] == 2 and mix_w_in.shape[0] == 1
    x, mem, target = P['x'][0], P['mem'][0], P['loss_target'][0]
    L, D = x.shape
    tm = min(256, L)

    W = {n: P[n] for n in REPLICATED}
    W.update(_gather_weights(P))

    inv_freq = 1.0 / (ROPE_BASE ** (jnp.arange(0, MLA_ROPE, 2, dtype=F32) / MLA_ROPE))
    ang = P['positions'][0].astype(F32)[:, None] * inv_freq
    cos, sin, z = jnp.cos(ang), jnp.sin(ang), jnp.zeros_like(ang)
    cos_p = jnp.concatenate([cos, z, cos, z], axis=1)
    sin_p = jnp.concatenate([-sin, z, sin, z], axis=1)

    h, s_mix0 = _mixer0_fwd(x, W, cos_p, sin_p, tm)
    h, s_xa0 = _xattn_fwd(h, mem, W, 0, tm)
    h, s_ffn0 = _ffn_fwd(h, W, 0, tm)
    h, s_mix1 = _mixer1_fwd(h, W, tm)
    h, s_xa1 = _xattn_fwd(h, mem, W, 1, tm)
    h, s_ffn1 = _ffn_fwd(h, W, 1, tm)
    n = L // tm
    dh, parts = blocked_fwd(_loss_fn, [rows(h, tm), rows(target, tm)],
                            [_out((L, D), F32, tm), ((n * 8, 128), F32, (8, 128), lambda i: (i, 0))], n, "loss")
    loss = lax.psum(jnp.sum(parts), ("x", "y", "c"))

    layered = {}

    def collect(g, lyr):
        for k_, v_ in g.items():
            layered.setdefault(k_, {})[lyr] = v_

    results = {}

    def big_items(lyr, names):
        return [(n_, 0 if P[n_].shape[0] == 1 else lyr, layered[n_][lyr]) for n_ in names]

    per_layer = [n_ for n_ in BIG if P[n_].shape[0] == 2]
    second_mixer = ['s5_w_glu_a', 's5_w_glu_b']
    first_mixer = ['mix_w_in', 'mix_w_out']
    assert sorted(per_layer + second_mixer + first_mixer) == sorted(BIG)

    dh, g = _ffn_bwd(dh, s_ffn1, W, 1, tm)
    collect(g, 1)
    dh, g = _xattn_bwd(dh, s_xa1, mem, W, 1, tm)
    collect(g, 1)
    dh, g = _mixer1_bwd(dh, s_mix1, W, tm)
    collect(g, 1)
    late = _reduce_batch(big_items(1, per_layer + second_mixer), None, P, results, "late_layer", {'swap': 2, 'a2a': 3, 'join': 4})
    next(late)
    dh, g = _ffn_bwd(dh, s_ffn0, W, 0, tm)
    collect(g, 0)
    dh = late.send(dh)
    dh, g = _xattn_bwd(dh, s_xa0, mem, W, 0, tm)
    collect(g, 0)
    mid = _reduce_batch(big_items(0, per_layer), None, P, results, "first_layer", {'swap': 5, 'a2a': 6, 'join': 7})
    next(mid)
    dx, g = _mixer0_bwd(dh, s_mix0, W, cos_p, sin_p, tm, part_way=mid.send)
    collect(g, 0)

    GS = {}
    for name in SMALL:
        by_layer = [layered[name][lyr] for lyr in sorted(layered[name])]
        full_shape = W[name].shape
        GS[name] = (by_layer[0].reshape(full_shape) if len(by_layer) == 1
                    else jnp.stack([g_.reshape(full_shape[1:]) for g_ in by_layer]))
    small = _pack([GS[n_] for n_ in SMALL], 2 * N_CHIPS * ROW_MULT).reshape(N_CHIPS, -1, PACK_W)
    dx = late.send(dx)
    dx = mid.send(dx)
    last = _reduce_batch(big_items(0, first_mixer), small, P, results, "first_mixer", {'swap': 8, 'a2a': 9, 'join': 10})
    next(last)
    late.send(None)
    last_pair_sum = last.send(None)
    mid.send(last_pair_sum)
    mid_name = per_layer[-1]
    results[mid_name] = list(last.send(list(results[mid_name])))
    last.send(None)
    outs = list(_update_small(results['small_sum'], GS, P))
    for k_ in range(4):
        outs[k_].update({n_: results[n_][k_].reshape(P[n_].shape) for n_ in BIG})
    return (loss, dx[None], *[d[n_] for d in outs for n_ in WEIGHTS])
```

```python
import functools
import math

import jax
import jax.numpy as jnp
import numpy as np
from jax import lax
from jax.experimental import pallas as pl
from jax.experimental.pallas import tpu as pltpu
from jax.experimental.pallas import tpu_sc as plsc

F32 = jnp.float32
BF16 = jnp.bfloat16
MXU_DTYPE = BF16
HI = lax.Precision.HIGHEST
V7X_VMEM_LIMIT_BYTES = 56 * 1024 * 1024
EPS = 1e-6
MESH = pl.DeviceIdType.MESH

HG_HEADS, HG_DIM = 4, 128
HG_WIDTH = HG_HEADS * HG_DIM
HG_SUB = 32
HG_BLOCK = 64
MLA_HEADS, MLA_Q_RANK, MLA_KV_RANK = 4, 256, 128
MLA_NOPE, MLA_ROPE, MLA_V = 128, 64, 128
MLA_QK = MLA_NOPE + MLA_ROPE
MLA_DK = 256
ROPE_BASE = 10000.0
IN_WIDTH = 4 * HG_WIDTH + MLA_Q_RANK + MLA_KV_RANK + MLA_ROPE
IN_PAD = 4 * HG_WIDTH + MLA_Q_RANK + MLA_KV_RANK + 128
S5_GROUP, S5_STATE = 16, 64
S5_GB = 8
DT_MIN, DT_MAX = 1e-3, 1e-1
XA_HEADS = 4
CONV_W = 3
ADAM_LR, ADAM_B1, ADAM_B2, ADAM_EPS, ADAM_WD, ADAM_STEP = 0.001, 0.9, 0.999, 1e-08, 0.01, 10


def _cparams(sem):
    return pltpu.CompilerParams(dimension_semantics=sem, vmem_limit_bytes=V7X_VMEM_LIMIT_BYTES)


class Opd:
    def __init__(self, arr, block, imap, grad=None, gshape=None, gimap=None):
        self.arr, self.block, self.imap, self.grad = arr, block, imap, grad
        self.gshape = arr.shape if gshape is None else gshape
        self.gimap = imap if gimap is None else gimap

    def spec(self):
        return pl.BlockSpec(self.block, self.imap)

    def gspec(self):
        return pl.BlockSpec(self.block, self.gimap)


def rows(arr, tm, grad=None, col=0, width=None):
    width = arr.shape[1] if width is None else width
    return Opd(arr, (tm, width), lambda i, c=col: (i, c), grad, (arr.shape[0], width), lambda i: (i, 0))


def cols(arr, tn, grad=None):
    return Opd(arr, (arr.shape[0], tn), lambda j: (0, j), grad)


def full(arr, grad=None):
    return Opd(arr, arr.shape, lambda i: (0, 0), grad)


def _load(ref):
    v = ref[...]
    return v.astype(F32) if jnp.issubdtype(v.dtype, jnp.floating) else v


def blocked_fwd(f, opds, outs, n, name):
    n_in = len(opds)

    def body(*refs):
        ys = f(*[_load(r) for r in refs[:n_in]])
        for r, y in zip(refs[n_in:], ys):
            r[...] = y.astype(r.dtype)

    res = pl.pallas_call(
        body, name=name, grid=(n,),
        in_specs=[o.spec() for o in opds],
        out_specs=[pl.BlockSpec(b, m) for (_, _, b, m) in outs],
        out_shape=[jax.ShapeDtypeStruct(s, d) for (s, d, _, _) in outs],
        compiler_params=_cparams(("parallel",)),
    )(*[o.arr for o in opds])
    return res


def blocked_bwd(f, opds, dys, n, name, plus=None):
    n_in, n_dy = len(opds), len(dys)
    diff = [i for i, o in enumerate(opds) if o.grad]
    extra = [] if plus is None else [plus]

    def body(*refs):
        vals = [_load(r) for r in refs[:n_in]]

        def fd(*dv):
            allv = list(vals)
            for i, v in zip(diff, dv):
                allv[i] = v
            return tuple(f(*allv))

        ys, vjp = jax.vjp(fd, *[vals[i] for i in diff])
        cts = tuple(_load(r).astype(y.dtype) for r, y in zip(refs[n_in:n_in + n_dy], ys))
        gs = list(vjp(cts))
        if extra:
            gs[0] = gs[0] + _load(refs[n_in + n_dy])
        for r, g, i in zip(refs[n_in + n_dy + len(extra):], gs, diff):
            if opds[i].grad == 'acc':
                @pl.when(pl.program_id(0) == 0)
                def _(r=r):
                    r[...] = jnp.zeros(r.shape, r.dtype)
                r[...] += g.astype(r.dtype)
            else:
                r[...] = g.astype(r.dtype)

    any_acc = any(opds[i].grad == 'acc' for i in diff)
    res = pl.pallas_call(
        body, name=name, grid=(n,),
        in_specs=[o.spec() for o in opds + dys + extra],
        out_specs=[opds[i].gspec() for i in diff],
        out_shape=[jax.ShapeDtypeStruct(opds[i].gshape, F32) for i in diff],
        compiler_params=_cparams(("arbitrary" if any_acc else "parallel",)),
    )(*[o.arr for o in opds + dys + extra])
    return res


def _tile(dim, want):
    for t in range(want - want % 16, 0, -16):
        if dim % t == 0:
            return t
    assert dim <= want, (dim, want)
    return dim


MATMUL_VMEM_BUDGET = 40 * 1024 * 1024
MATMUL_ROWS = 512
MATMUL_CHUNK = 1024


def _widest(N, fits):
    for t in range(N - N % 128, 0, -128):
        if N % t == 0 and fits(t):
            return t
    return N


def matmul(a, b, mode="nn", out_dtype=F32, add=None, name="matmul", into=None, col_blocks=None):
    sa, sb, so = a.dtype.itemsize, b.dtype.itemsize, jnp.dtype(out_dtype).itemsize
    has_add = add is not None
    if mode == "tn":
        (K, M), (K2, N) = a.shape, b.shape
        assert K == K2 and not has_add and out_dtype == F32, (a.shape, b.shape)
        tk = _tile(K, MATMUL_CHUNK)
        tn = _widest(N, lambda t: 2 * (tk * M * sa + tk * t * sb + M * t * 4) <= MATMUL_VMEM_BUDGET)
        extra, alias = [], {}
        if into is not None:
            buf, lead = into[0], tuple(into[1:])
            if col_blocks is not None:
                assert N % col_blocks == 0 and (N // col_blocks) % 128 == 0 and tn >= N // col_blocks, (N, col_blocks, tn)
                tn = N // col_blocks
                assert buf.shape[len(lead):] == (M, tn), (buf.shape, lead, M, tn)
                out_spec = pl.BlockSpec((None,) * len(lead) + (M, tn), lambda j, k: lead[:-1] + (lead[-1] + j, 0, 0))
            else:
                assert buf.shape[len(lead):] == (M, N), (buf.shape, lead, M, N)
                out_spec = pl.BlockSpec((None,) * len(lead) + (M, tn), lambda j, k: lead + (0, j))
            out_shape = jax.ShapeDtypeStruct(buf.shape, F32)
            extra, alias = [buf], {2: 0}
        else:
            assert col_blocks is None
            out_spec = pl.BlockSpec((M, tn), lambda j, k: (0, j))
            out_shape = jax.ShapeDtypeStruct((M, N), F32)

        def body(a_ref, b_ref, *rest):
            o_ref = rest[-1]
            r = lax.dot_general(a_ref[...].astype(MXU_DTYPE), b_ref[...].astype(MXU_DTYPE), ((_TN), ((), ())),
                                preferred_element_type=F32)

            @pl.when(pl.program_id(1) == 0)
            def _():
                o_ref[...] = r

            @pl.when(pl.program_id(1) > 0)
            def _():
                o_ref[...] += r

        return pl.pallas_call(
            body, name=name, grid=(N // tn, K // tk),
            in_specs=[pl.BlockSpec((tk, M), lambda j, k: (k, 0)), pl.BlockSpec((tk, tn), lambda j, k: (k, j))]
            + [pl.BlockSpec(memory_space=pl.ANY)] * len(extra),
            out_specs=out_spec, out_shape=out_shape, input_output_aliases=alias,
            compiler_params=_cparams(("parallel", "arbitrary")),
        )(a, b, *extra)

    (M, K) = a.shape
    N = b.shape[1] if mode == "nn" else b.shape[0]
    assert K == (b.shape[0] if mode == "nn" else b.shape[1]), (a.shape, b.shape, mode)
    tm = _tile(M, MATMUL_ROWS)
    tn = _widest(N, lambda t: 2 * (tm * K * sa + K * t * sb + tm * t * (so + 4 * has_add)) <= MATMUL_VMEM_BUDGET)
    dims = ((_NN if mode == "nn" else _NT), ((), ()))

    def body(*refs):
        r = lax.dot_general(refs[0][...].astype(MXU_DTYPE), refs[1][...].astype(MXU_DTYPE), dims, preferred_element_type=F32)
        if has_add:
            r = r + refs[2][...].astype(F32)
        refs[-1][...] = r.astype(refs[-1].dtype)

    b_spec = pl.BlockSpec((K, tn), lambda j, i: (0, j)) if mode == "nn" else pl.BlockSpec((tn, K), lambda j, i: (j, 0))
    in_specs = [pl.BlockSpec((tm, K), lambda j, i: (i, 0)), b_spec]
    args = [a, b]
    if has_add:
        in_specs.append(pl.BlockSpec((tm, tn), lambda j, i: (i, j)))
        args.append(add)
    return pl.pallas_call(
        body, name=name, grid=(N // tn, M // tm),
        in_specs=in_specs,
        out_specs=pl.BlockSpec((tm, tn), lambda j, i: (i, j)),
        out_shape=jax.ShapeDtypeStruct((M, N), out_dtype),
        compiler_params=_cparams(("parallel", "parallel")),
    )(*args)


def _dot(a, b, dims, precision=None):
    if precision is None:
        a, b = a.astype(MXU_DTYPE), b.astype(MXU_DTYPE)
    return lax.dot_general(a, b, (dims, ((), ())), precision=precision, preferred_element_type=F32)


_NN = ((1,), (0,))
_NT = ((1,), (1,))
_TN = ((0,), (0,))


def _rms(x, gain):
    return x * lax.rsqrt(jnp.mean(x * x, axis=-1, keepdims=True) + EPS) * gain


def _hg_block(st_t, q, fl, iv, g, lb, gain):
    row = lax.broadcasted_iota(jnp.int32, (HG_SUB, HG_SUB), 0)
    col = lax.broadcasted_iota(jnp.int32, (HG_SUB, HG_SUB), 1)
    tri = (row >= col).astype(F32)
    heads = [slice(h * HG_DIM, (h + 1) * HG_DIM) for h in range(HG_HEADS)]
    fg = lb + (1.0 - lb) * jax.nn.sigmoid(fl)
    lf, kk, qf = jnp.log(fg), 1.0 - fg, jax.nn.silu(q)
    sts = [st_t[sl, :] for sl in heads]
    parts = [[] for _ in heads]
    for s in range(q.shape[0] // HG_SUB):
        r = slice(s * HG_SUB, (s + 1) * HG_SUB)
        b = lf[r]
        d = 1
        while d < HG_SUB:
            b = b + _shift_rows(b, d)
            d *= 2
        b_mid = jnp.sum(lf[r][:HG_SUB // 2], axis=0, keepdims=True)
        b_end = jnp.sum(lf[r], axis=0, keepdims=True)
        q_in, k_in = qf[r] * jnp.exp(b - b_mid), kk[r] * jnp.exp(b_mid - b)
        q_st, k_st, decay = qf[r] * jnp.exp(b), kk[r] * jnp.exp(b_end - b), jnp.exp(b_end)
        for h, sl in enumerate(heads):
            sc = _dot(q_in[:, sl], k_in[:, sl], _NT) * tri
            parts[h].append(_dot(sc, iv[r, sl], _NN) + _dot(q_st[:, sl], sts[h], _NT))
            sts[h] = sts[h] * decay[:, sl] + _dot(iv[r, sl], k_st[:, sl], _TN)
    outs = [_rms(jnp.concatenate(parts[h], axis=0), gain[:, sl]) * jax.nn.silu(g[:, sl]) for h, sl in enumerate(heads)]
    return jnp.concatenate(sts, axis=0), jnp.concatenate(outs, axis=1)


def _hg_specs(proj, nb):
    return [pl.BlockSpec((HG_BLOCK, HG_WIDTH), lambda i, c=c, f=nb: (f(i), c)) for c in range(4)]


def hgrn2_fwd(proj, lb, gain):
    L = proj.shape[0]
    n = L // HG_BLOCK

    def body(q, fl, iv, g, lb_r, gain_r, o_ref, st_ref, st):
        @pl.when(pl.program_id(0) == 0)
        def _():
            st[...] = jnp.zeros(st.shape, F32)

        st_ref[0] = st[...]
        new, o = _hg_block(st[...], q[...], fl[...], iv[...], g[...], lb_r[...], gain_r[...])
        st[...] = new
        o_ref[...] = o.astype(o_ref.dtype)

    pspec = pl.BlockSpec((1, HG_WIDTH), lambda i: (0, 0))
    return pl.pallas_call(
        body, name="hgrn2_fwd", grid=(n,),
        in_specs=_hg_specs(proj, lambda i: i) + [pspec, pspec],
        out_specs=[pl.BlockSpec((HG_BLOCK, HG_WIDTH), lambda i: (i, 0)),
                   pl.BlockSpec((1, HG_WIDTH, HG_DIM), lambda i: (i, 0, 0))],
        out_shape=[jax.ShapeDtypeStruct((L, HG_WIDTH), MXU_DTYPE),
                   jax.ShapeDtypeStruct((n, HG_WIDTH, HG_DIM), F32)],
        scratch_shapes=[pltpu.VMEM((HG_WIDTH, HG_DIM), F32)],
        compiler_params=_cparams(("arbitrary",)),
    )(proj, proj, proj, proj, lb, gain)


def hgrn2_bwd(proj, lb, gain, states, do):
    L = proj.shape[0]
    n = L // HG_BLOCK

    def body(q, fl, iv, g, lb_r, gain_r, st_r, do_r, dproj, dlb, dgain, dst):
        @pl.when(pl.program_id(0) == 0)
        def _():
            dst[...] = jnp.zeros(dst.shape, F32)
            dlb[...] = jnp.zeros(dlb.shape, F32)
            dgain[...] = jnp.zeros(dgain.shape, F32)

        _, vjp = jax.vjp(_hg_block, st_r[0], q[...], fl[...], iv[...], g[...], lb_r[...], gain_r[...])
        d_st, dq, dfl, div, dg, d_lb, d_gain = vjp((dst[...], do_r[...].astype(F32)))
        dst[...] = d_st
        dproj[:, 0 * HG_WIDTH:1 * HG_WIDTH] = dq
        dproj[:, 1 * HG_WIDTH:2 * HG_WIDTH] = dfl
        dproj[:, 2 * HG_WIDTH:3 * HG_WIDTH] = div
        dproj[:, 3 * HG_WIDTH:4 * HG_WIDTH] = dg
        dlb[...] += d_lb
        dgain[...] += d_gain

    rev = lambda i: n - 1 - i
    pspec = pl.BlockSpec((1, HG_WIDTH), lambda i: (0, 0))
    return pl.pallas_call(
        body, name="hgrn2_bwd", grid=(n,),
        in_specs=_hg_specs(proj, rev) + [pspec, pspec,
                                         pl.BlockSpec((1, HG_WIDTH, HG_DIM), lambda i: (rev(i), 0, 0)),
                                         pl.BlockSpec((HG_BLOCK, HG_WIDTH), lambda i: (rev(i), 0))],
        out_specs=[pl.BlockSpec((HG_BLOCK, 4 * HG_WIDTH), lambda i: (rev(i), 0)), pspec, pspec],
        out_shape=[jax.ShapeDtypeStruct((L, 4 * HG_WIDTH), F32),
                   jax.ShapeDtypeStruct((1, HG_WIDTH), F32), jax.ShapeDtypeStruct((1, HG_WIDTH), F32)],
        scratch_shapes=[pltpu.VMEM((HG_WIDTH, HG_DIM), F32)],
        compiler_params=_cparams(("arbitrary",)),
    )(proj, proj, proj, proj, lb, gain, states, do)


def _rope_rms(x, gain_p, cos_p, sin_p):
    n = x * lax.rsqrt(jnp.sum(x * x, axis=-1, keepdims=True) * (1.0 / MLA_ROPE) + EPS) * gain_p
    r = lax.broadcasted_iota(jnp.int32, (128, 128), 0)
    c = lax.broadcasted_iota(jnp.int32, (128, 128), 1)
    swap = (r == (c + 64) % 128).astype(F32)
    return n * cos_p + _dot(n, swap, _NN, HI) * sin_p


MLA_IN = MLA_Q_RANK + MLA_KV_RANK + 128


def _mla_prep(x, cos_p, sin_p, q_a, w_uq, kv_a, w_ukv, qn_nope, qn_rope, kn_nope, kn_rope):
    c_q, c_kv, kpe = x[:, :MLA_Q_RANK], x[:, MLA_Q_RANK:MLA_Q_RANK + MLA_KV_RANK], x[:, MLA_Q_RANK + MLA_KV_RANK:]
    q = _dot(_rms(c_q, q_a), w_uq, _NN)
    kv = _dot(_rms(c_kv, kv_a), w_ukv, _NN)
    k_pe = _rope_rms(kpe, kn_rope, cos_p, sin_p)
    qs, ks = [], []
    for h in range(MLA_HEADS):
        qs.append(_rms(q[:, h * MLA_DK:h * MLA_DK + MLA_NOPE], qn_nope))
        qs.append(_rope_rms(q[:, h * MLA_DK + MLA_NOPE:(h + 1) * MLA_DK], qn_rope, cos_p, sin_p))
        ks.append(_rms(kv[:, h * MLA_NOPE:(h + 1) * MLA_NOPE], kn_nope))
        ks.append(k_pe)
    return jnp.concatenate(qs, axis=1), jnp.concatenate(ks, axis=1), kv[:, MLA_HEADS * MLA_NOPE:]


def _mla_prep_opds(proj, cos_p, sin_p, params, tm, grads):
    g = (lambda k: k) if grads else (lambda k: None)
    assert (4 * HG_WIDTH) % MLA_IN == 0
    return ([rows(proj, tm, g('blk'), col=4 * HG_WIDTH // MLA_IN, width=MLA_IN), rows(cos_p, tm), rows(sin_p, tm)]
            + [full(p, g('acc')) for p in params])


def mla_prep_fwd(proj, cos_p, sin_p, params, tm):
    L = proj.shape[0]
    W = MLA_HEADS * MLA_DK
    rb = lambda w: (tm, w)
    outs = [((L, W), MXU_DTYPE, rb(W), lambda i: (i, 0)), ((L, W), MXU_DTYPE, rb(W), lambda i: (i, 0)),
            ((L, MLA_HEADS * MLA_V), MXU_DTYPE, rb(MLA_HEADS * MLA_V), lambda i: (i, 0))]
    return blocked_fwd(_mla_prep, _mla_prep_opds(proj, cos_p, sin_p, params, tm, False), outs, L // tm, "mla_prep_fwd")


def mla_prep_bwd(proj, cos_p, sin_p, params, dq, dk, dv, tm):
    L = proj.shape[0]
    return blocked_bwd(_mla_prep, _mla_prep_opds(proj, cos_p, sin_p, params, tm, True),
                       [rows(dq, tm), rows(dk, tm), rows(dv, tm)], L // tm, "mla_prep_bwd")


def _scores(q, k, scale, shift=None):
    s = _dot(q, k, _NT) * scale
    if shift is None:
        return s
    row = lax.broadcasted_iota(jnp.int32, s.shape, 0)
    col = lax.broadcasted_iota(jnp.int32, s.shape, 1)
    return jnp.where(col <= row + shift, s, -jnp.inf)


ATTN_ROWS = 1024
ATTN_WIDE = 1


def attn_fwd(q, k, v, scale, t):
    L = q.shape[0]
    tq = ATTN_WIDE * t

    def body(q_ref, k_ref, v_ref, o_ref, lse_ref):
        i = pl.program_id(1)
        qb = q_ref[...]

        def step(j, carry, shift=None):
            m, l, acc = carry
            kj = k_ref[pl.ds(pl.multiple_of(j * t, t), t), :]
            vj = v_ref[pl.ds(pl.multiple_of(j * t, t), t), :]
            s = _scores(qb, kj, scale, shift)
            m_new = jnp.maximum(m, jnp.max(s, axis=-1, keepdims=True))
            p = jnp.exp(s - m_new)
            alpha = jnp.exp(m - m_new)
            return m_new, alpha * l + jnp.sum(p, axis=-1, keepdims=True), alpha * acc + _dot(p, vj, _NN)

        carry = (jnp.full((tq, 1), -jnp.inf, F32), jnp.zeros((tq, 1), F32), jnp.zeros((tq, MLA_V), F32))
        carry = lax.fori_loop(0, ATTN_WIDE * i, step, carry)
        for d in range(ATTN_WIDE):
            carry = step(ATTN_WIDE * i + d, carry, -d * t)
        m, l, acc = carry
        o_ref[...] = acc / l
        lse_ref[...] = jnp.broadcast_to(m + jnp.log(l), lse_ref.shape)

    hspec = lambda rows_, w: pl.BlockSpec((rows_, w), lambda h, i: (0, h))
    bspec = lambda w: pl.BlockSpec((tq, w), lambda h, i: (i, h))
    return pl.pallas_call(
        body, name="attn_fwd", grid=(MLA_HEADS, L // tq),
        in_specs=[bspec(MLA_DK), hspec(L, MLA_DK), hspec(L, MLA_V)],
        out_specs=[bspec(MLA_V), bspec(MLA_V)],
        out_shape=[jax.ShapeDtypeStruct((L, MLA_HEADS * MLA_V), F32)] * 2,
        compiler_params=_cparams(("parallel", "parallel")),
    )(q, k, v)


def attn_bwd_dq(q, k, v, o, lse, do, scale, t):
    L = q.shape[0]
    tq = ATTN_WIDE * t

    def body(q_ref, k_ref, v_ref, o_ref, lse_ref, do_ref, dq_ref):
        i = pl.program_id(1)
        qb, dob = q_ref[...], do_ref[...]
        delta = jnp.sum(dob * o_ref[...], axis=-1, keepdims=True)
        lse_c = jnp.max(lse_ref[...], axis=-1, keepdims=True)

        def step(j, dq, shift=None):
            kj = k_ref[pl.ds(pl.multiple_of(j * t, t), t), :]
            vj = v_ref[pl.ds(pl.multiple_of(j * t, t), t), :]
            p = jnp.exp(_scores(qb, kj, scale, shift) - lse_c)
            ds = p * (_dot(dob, vj, _NT) - delta) * scale
            return dq + _dot(ds, kj, _NN)

        dq = lax.fori_loop(0, ATTN_WIDE * i, step, jnp.zeros((tq, MLA_DK), F32))
        for d in range(ATTN_WIDE):
            dq = step(ATTN_WIDE * i + d, dq, -d * t)
        dq_ref[...] = dq

    hspec = lambda w: pl.BlockSpec((L, w), lambda h, i: (0, h))
    bspec = lambda w: pl.BlockSpec((tq, w), lambda h, i: (i, h))
    return pl.pallas_call(
        body, name="attn_bwd_dq", grid=(MLA_HEADS, L // tq),
        in_specs=[bspec(MLA_DK), hspec(MLA_DK), hspec(MLA_V), bspec(MLA_V), bspec(MLA_V), bspec(MLA_V)],
        out_specs=bspec(MLA_DK),
        out_shape=jax.ShapeDtypeStruct((L, MLA_HEADS * MLA_DK), F32),
        compiler_params=_cparams(("parallel", "parallel")),
    )(q, k, v, o, lse, do)


def attn_bwd_dkv(q, k, v, o, lse, do, scale, t):
    L = q.shape[0]
    tk = ATTN_WIDE * t

    def body(q_ref, k_ref, v_ref, o_ref, lse_ref, do_ref, dk_ref, dv_ref):
        j = pl.program_id(1)
        kb, vb = k_ref[...], v_ref[...]

        def step(i, carry, shift=None):
            dk, dv = carry
            r = pl.ds(pl.multiple_of(i * t, t), t)
            qi, doi = q_ref[r, :], do_ref[r, :]
            delta = jnp.sum(doi * o_ref[r, :], axis=-1, keepdims=True)
            lse_c = jnp.max(lse_ref[r, :], axis=-1, keepdims=True)
            p = jnp.exp(_scores(qi, kb, scale, shift) - lse_c)
            ds = p * (_dot(doi, vb, _NT) - delta) * scale
            return dk + _dot(ds, qi, _TN), dv + _dot(p, doi, _TN)

        carry = (jnp.zeros((tk, MLA_DK), F32), jnp.zeros((tk, MLA_V), F32))
        for d in range(ATTN_WIDE):
            carry = step(ATTN_WIDE * j + d, carry, d * t)
        dk, dv = lax.fori_loop(ATTN_WIDE * (j + 1), L // t, step, carry)
        dk_ref[...] = dk
        dv_ref[...] = dv

    hspec = lambda w: pl.BlockSpec((L, w), lambda h, j: (0, h))
    bspec = lambda w: pl.BlockSpec((tk, w), lambda h, j: (j, h))
    return pl.pallas_call(
        body, name="attn_bwd_dkv", grid=(MLA_HEADS, L // tk),
        in_specs=[hspec(MLA_DK), bspec(MLA_DK), bspec(MLA_V), hspec(MLA_V), hspec(MLA_V), hspec(MLA_V)],
        out_specs=[bspec(MLA_DK), bspec(MLA_V)],
        out_shape=[jax.ShapeDtypeStruct((L, MLA_HEADS * MLA_DK), F32), jax.ShapeDtypeStruct((L, MLA_HEADS * MLA_V), F32)],
        compiler_params=_cparams(("parallel", "parallel")),
    )(q, k, v, o, lse, do)


S5_LANES = S5_GB * S5_STATE
S5_BWD_CHUNK = 2048


def _cmul(ar, ai, br, bi):
    return ar * br - ai * bi, ar * bi + ai * br


def _a_powers(ar, ai, reverse):
    a2 = _cmul(ar, ai, ar, ai)
    a4 = _cmul(*a2, *a2)
    row = lax.broadcasted_iota(jnp.int32, (8, ar.shape[1]), 0)
    e = (8 - row) if reverse else (row + 1)
    tr, ti = jnp.ones((8, ar.shape[1]), F32), jnp.zeros((8, ar.shape[1]), F32)
    for bit, (pr, pi) in ((1, (ar, ai)), (2, a2), (4, a4), (8, _cmul(*a4, *a4))):
        nr, ni = _cmul(tr, ti, pr, pi)
        sel = (e & bit) != 0
        tr, ti = jnp.where(sel, nr, tr), jnp.where(sel, ni, ti)
    pows = []
    for d, (pr, pi) in zip((1, 2, 4), ((ar, ai), a2, a4)):
        keep = (row < 8 - d) if reverse else (row >= d)
        pows.append((jnp.where(keep, pr, 0.0), jnp.where(keep, pi, 0.0)))
    return pows, (tr, ti)


def _scan8(xr, xi, pows, table, cr, ci, reverse):
    for d, (pr, pi) in zip((1, 2, 4), pows):
        shift = 8 - d if reverse else d
        mr, mi = _cmul(pr, pi, pltpu.roll(xr, shift, 0), pltpu.roll(xi, shift, 0))
        xr, xi = xr + mr, xi + mi
    mr, mi = _cmul(table[0], table[1], cr, ci)
    return xr + mr, xi + mi


def _row_of(x, r):
    row = lax.broadcasted_iota(jnp.int32, x.shape, 0)
    return jnp.sum(jnp.where(row == r, x, 0.0), axis=0, keepdims=True)


def _s5_scan_fwd(h_re, h_im, ar, ai, L):
    pows, table = _a_powers(ar, ai, False)

    def step(i, carry):
        r = pl.ds(pl.multiple_of(i * 8, 8), 8)
        xr, xi = _scan8(h_re[r, :], h_im[r, :], pows, table, carry[0], carry[1], False)
        h_re[r, :] = xr
        h_im[r, :] = xi
        return xr[7:8, :], xi[7:8, :]

    z = jnp.zeros((1, ar.shape[1]), F32)
    lax.fori_loop(0, L // 8, step, (z, z))


def _s5_specs(L):
    return [pl.BlockSpec((L, 128), lambda g: (0, g)),
            pl.BlockSpec((1, 128, S5_LANES), lambda g: (g, 0, 0)), pl.BlockSpec((1, 128, S5_LANES), lambda g: (g, 0, 0)),
            pl.BlockSpec((1, 1, S5_LANES), lambda g: (g, 0, 0)), pl.BlockSpec((1, 1, S5_LANES), lambda g: (g, 0, 0)),
            pl.BlockSpec((1, S5_LANES, 128), lambda g: (g, 0, 0)), pl.BlockSpec((1, S5_LANES, 128), lambda g: (g, 0, 0))]


def s5_fwd(u, w_re, w_im, a_re, a_im, c_re, c_im):
    L, D = u.shape

    def body(u_ref, wr, wi, ar, ai, cr, ci, y_ref, h_re, h_im):
        ub = u_ref[...]
        h_re[...] = _dot(ub, wr[0], _NN)
        h_im[...] = _dot(ub, wi[0], _NN)
        _s5_scan_fwd(h_re, h_im, ar[0], ai[0], L)
        y_ref[...] = _dot(h_re[...], cr[0], _NN) - _dot(h_im[...], ci[0], _NN)

    return pl.pallas_call(
        body, name="s5_fwd", grid=(D // 128,),
        in_specs=_s5_specs(L), out_specs=pl.BlockSpec((L, 128), lambda g: (0, g)),
        out_shape=jax.ShapeDtypeStruct((L, D), F32),
        scratch_shapes=[pltpu.VMEM((L, S5_LANES), F32), pltpu.VMEM((L, S5_LANES), F32)],
        compiler_params=_cparams(("parallel",)),
    )(u, w_re, w_im, a_re, a_im, c_re, c_im)


def s5_bwd(u, w_re, w_im, a_re, a_im, c_re, c_im, dy, du_plus, tc):
    L, D = u.shape
    nch = L // tc

    def body(u_ref, wr, wi, ar_ref, ai_ref, cr, ci, dy_ref, plus_ref, du_ref, dwr, dwi, dar, dai, dcr, dci, h_re, h_im,
             g_re, g_im):
        ar, ai = ar_ref[0], ai_ref[0]
        ub = u_ref[...]
        h_re[...] = _dot(ub, wr[0], _NN)
        h_im[...] = _dot(ub, wi[0], _NN)
        _s5_scan_fwd(h_re, h_im, ar, ai, L)
        dyb = dy_ref[...]
        dcr[0] = _dot(h_re[...], dyb, _TN)
        dci[0] = -_dot(h_im[...], dyb, _TN)
        pows, table = _a_powers(ar, -ai, True)
        dwr[0] = jnp.zeros((128, S5_LANES), F32)
        dwi[0] = jnp.zeros((128, S5_LANES), F32)
        z1 = jnp.zeros((1, S5_LANES), F32)
        z8 = jnp.zeros((8, S5_LANES), F32)

        def chunk(cc, carry):
            c0 = pl.multiple_of((nch - 1 - cc) * tc, tc)
            rows_c = pl.ds(c0, tc)
            dyc = dy_ref[rows_c, :]
            g_re[...] = _dot(dyc, cr[0], _NT)
            g_im[...] = -_dot(dyc, ci[0], _NT)

            def step(ii, cy):
                gr_c, gi_c, acc_r, acc_i = cy
                i8 = pl.multiple_of((tc // 8 - 1 - ii) * 8, 8)
                rl = pl.ds(i8, 8)
                xr, xi = _scan8(g_re[rl, :], g_im[rl, :], pows, table, gr_c, gi_c, True)
                g_re[rl, :] = xr
                g_im[rl, :] = xi
                t0 = c0 + i8
                hb_r, hb_i = h_re[pl.ds(t0, 8), :], h_im[pl.ds(t0, 8), :]
                tp = pl.multiple_of(jnp.maximum(t0 - 8, 0), 8)
                first = (t0 > 0).astype(F32)
                pr = h_re[pl.ds(tp, 8), :][7:8, :] * first
                pi = h_im[pl.ds(tp, 8), :][7:8, :] * first
                row = lax.broadcasted_iota(jnp.int32, xr.shape, 0)
                hp_r = jnp.where(row == 0, pr, pltpu.roll(hb_r, 1, 0))
                hp_i = jnp.where(row == 0, pi, pltpu.roll(hb_i, 1, 0))
                return (xr[0:1, :], xi[0:1, :],
                        acc_r + xr * hp_r + xi * hp_i, acc_i + xi * hp_r - xr * hp_i)

            cy = lax.fori_loop(0, tc // 8, step, carry)
            uc = u_ref[rows_c, :]
            gr, gi = g_re[...], g_im[...]
            du_ref[rows_c, :] = _dot(gr, wr[0], _NT) + _dot(gi, wi[0], _NT) + plus_ref[rows_c, :]
            dwr[0] += _dot(uc, gr, _TN)
            dwi[0] += _dot(uc, gi, _TN)
            return cy

        _, _, acc_r, acc_i = lax.fori_loop(0, nch, chunk, (z1, z1, z8, z8))
        dar[0] = jnp.sum(acc_r, axis=0, keepdims=True)
        dai[0] = jnp.sum(acc_i, axis=0, keepdims=True)

    specs = _s5_specs(L)
    return pl.pallas_call(
        body, name="s5_bwd", grid=(D // 128,),
        in_specs=specs + [pl.BlockSpec((L, 128), lambda g: (0, g))] * 2,
        out_specs=[pl.BlockSpec((L, 128), lambda g: (0, g))] + specs[1:],
        out_shape=[jax.ShapeDtypeStruct((L, D), F32)] + [jax.ShapeDtypeStruct(x.shape, F32)
                                                        for x in (w_re, w_im, a_re, a_im, c_re, c_im)],
        scratch_shapes=[pltpu.VMEM((L, S5_LANES), F32), pltpu.VMEM((L, S5_LANES), F32),
                        pltpu.VMEM((tc, S5_LANES), F32), pltpu.VMEM((tc, S5_LANES), F32)],
        compiler_params=_cparams(("parallel",)),
    )(u, w_re, w_im, a_re, a_im, c_re, c_im, dy, du_plus)


def _s5_discretize(lr, li, ldt, br, bi):
    dt = jnp.exp(ldt)
    mag = jnp.exp(lr * dt)
    ar, ai = mag * jnp.cos(li * dt), mag * jnp.sin(li * dt)
    den = lr * lr + li * li
    zr = ((ar - 1.0) * lr + ai * li) / den
    zi = (ai * lr - (ar - 1.0) * li) / den
    p = lax.broadcasted_iota(jnp.int32, (S5_STATE, S5_STATE * S5_GROUP), 0)
    c = lax.broadcasted_iota(jnp.int32, (S5_STATE, S5_STATE * S5_GROUP), 1)
    rep = (c // S5_GROUP == p).astype(F32)
    zr, zi = _dot(zr, rep, _NN, HI), _dot(zi, rep, _NN, HI)
    return ar, ai, zr * br - zi * bi, zr * bi + zi * br


def _conv_shift(x, d):
    row = lax.broadcasted_iota(jnp.int32, x.shape, 0)
    return jnp.where(row >= d, pltpu.roll(x, d, 0), 0.0)


def _conv_unshift(x, d):
    n = x.shape[0]
    row = lax.broadcasted_iota(jnp.int32, x.shape, 0)
    return jnp.where(row < n - d, pltpu.roll(x, n - d, 0), 0.0)


@functools.partial(jax.custom_vjp, nondiff_argnums=(1,))
def _shift_rows(x, d):
    return _conv_shift(x, d)


_shift_rows.defvjp(lambda x, d: (_conv_shift(x, d), None), lambda d, _, g: (_conv_unshift(g, d),))


def _conv_gate(ug, uv, wg0, wg1, wg2, wv0, wv1, wv2, bg, bv):
    def conv(u, w0, w1, w2, b):
        return u * w2 + _shift_rows(u, 1) * w1 + _shift_rows(u, 2) * w0 + b
    return (jax.nn.silu(conv(ug, wg0, wg1, wg2, bg)) * conv(uv, wv0, wv1, wv2, bv),)


def _rms_fn(x, gain):
    return (_rms(x, gain),)


def _softmax_rows(s):
    e = jnp.exp(s - lax.stop_gradient(jnp.max(s, axis=-1, keepdims=True)))
    return e / jnp.sum(e, axis=-1, keepdims=True)


def _xa_core(qp, k, v, q_gain):
    dh = qp.shape[1] // XA_HEADS
    outs = []
    for h in range(XA_HEADS):
        sl = slice(h * dh, (h + 1) * dh)
        p = _softmax_rows(_dot(_rms(qp[:, sl], q_gain), k[:, sl], _NT) * (dh ** -0.5))
        outs.append(_dot(p, v[:, sl], _NN))
    return (jnp.concatenate(outs, axis=1),)


def _mem_kv(mem, mem_gain, wk, wv, k_gain):
    m = _rms(mem, mem_gain)
    kp = _dot(m, wk, _NN)
    dh = kp.shape[1] // XA_HEADS
    k = jnp.concatenate([_rms(kp[:, h * dh:(h + 1) * dh], k_gain) for h in range(XA_HEADS)], axis=1)
    return k, _dot(m, wv, _NN)


def _s5_post(y, u, d):
    return (jax.nn.gelu(y + d * u),)


def _glu(a, b):
    return (a * jax.nn.sigmoid(b),)


def _lb_first(logits):
    e = jnp.exp(logits - lax.stop_gradient(jnp.max(logits, axis=0, keepdims=True)))
    return (_row_of(e, 0) / jnp.sum(e, axis=0, keepdims=True),)


def _loss_fn(y, t):
    e = y - t
    part = 0.5 * jnp.sum(e * e) / y.shape[1]
    return e * (1.0 / y.shape[1]), jnp.full((8, 128), part / (8 * 128), F32)


def _out(shape, dtype, tm):
    return (shape, dtype, (tm, shape[1]), lambda i: (i, 0))


LIGHT_ROWS = 512


def rms_fwd(h, gain, dtype):
    tm = min(LIGHT_ROWS, h.shape[0])
    return blocked_fwd(_rms_fn, [rows(h, tm), full(gain)], [_out(h.shape, dtype, tm)], h.shape[0] // tm, "rms_fwd")[0]


def rms_bwd(h, gain, dy, residual):
    tm = min(LIGHT_ROWS, h.shape[0])
    return blocked_bwd(_rms_fn, [rows(h, tm, 'blk'), full(gain, 'acc')], [rows(dy, tm)], h.shape[0] // tm, "rms_bwd",
                       plus=rows(residual, tm))


def _adamw_math(w, g, m, v):
    m = ADAM_B1 * m + (1.0 - ADAM_B1) * g
    v = ADAM_B2 * v + (1.0 - ADAM_B2) * jnp.square(g)
    m_hat = m / (1.0 - ADAM_B1 ** ADAM_STEP)
    v_hat = v / (1.0 - ADAM_B2 ** ADAM_STEP)
    return -ADAM_LR * (m_hat / (jnp.sqrt(v_hat) + ADAM_EPS) + ADAM_WD * w), m, v


def adamw(w, g, m, v, name):
    R = w.shape[0]
    tm = _tile(R, 256)
    assert g.shape == w.shape == m.shape == v.shape, (name, w.shape, g.shape)

    def body(w_ref, g_ref, m_ref, v_ref, d_ref, nm_ref, nv_ref):
        d_ref[...], nm_ref[...], nv_ref[...] = _adamw_math(w_ref[...], g_ref[...], m_ref[...], v_ref[...])

    spec = pl.BlockSpec((tm, w.shape[1]), lambda i: (i, 0))
    return pl.pallas_call(
        body, name=name, grid=(R // tm,), in_specs=[spec] * 4, out_specs=[spec] * 3,
        out_shape=[jax.ShapeDtypeStruct(w.shape, F32)] * 3, compiler_params=_cparams(("parallel",)),
    )(w, g, m, v)


def _index_operand(i):
    return jnp.reshape(i, (1,)).astype(jnp.int32)


def adamw_layer(w, mine, other, core, m, v, layer, bufs, name):
    H, C = mine.shape
    tm = _tile(H, 256)
    nb = H // tm
    assert w.shape[1:] == (2 * H, C) and all(b.shape == w.shape for b in bufs), (name, w.shape, mine.shape)

    def body(c_ref, w_ref, mine_ref, other_ref, m_ref, v_ref, *rest):
        g_out, d_ref, nm_ref, nv_ref = rest[-4:]
        g_ = jnp.where(pl.program_id(0) // nb == c_ref[0], mine_ref[...], other_ref[...])
        g_out[...] = g_
        d_ref[...], nm_ref[...], nv_ref[...] = _adamw_math(w_ref[...], g_, m_ref[...], v_ref[...])

    lspec = pl.BlockSpec((None, tm, C), lambda i, c: (layer, i, 0))
    half = lambda sign: pl.BlockSpec(
        (tm, C), lambda i, c: (jnp.clip(i - (c[0] if sign else 1 - c[0]) * nb, 0, nb - 1), 0))
    any_spec = pl.BlockSpec(memory_space=pl.ANY)
    grid_spec = pltpu.PrefetchScalarGridSpec(
        num_scalar_prefetch=1, grid=(2 * nb,),
        in_specs=[lspec, half(True), half(False), lspec, lspec] + [any_spec] * 4, out_specs=[lspec] * 4)
    return pl.pallas_call(
        body, name=name, grid_spec=grid_spec, out_shape=[jax.ShapeDtypeStruct(w.shape, F32)] * 4,
        input_output_aliases={6: 0, 7: 1, 8: 2, 9: 3}, compiler_params=_cparams(("parallel",)),
    )(_index_operand(core), w, mine, other, m, v, *bufs)


def add_own_half(x, core, theirs, name, out_dtype):
    nq, _, H, C = x.shape
    tm = _tile(H, 256)

    def body(c_ref, x_ref, t_ref, o_ref):
        o_ref[...] = (x_ref[...] + t_ref[...]).astype(o_ref.dtype)

    spec = pl.BlockSpec((None, tm, C), lambda q, i, c: (q, i, 0))
    grid_spec = pltpu.PrefetchScalarGridSpec(
        num_scalar_prefetch=1, grid=(nq, H // tm),
        in_specs=[pl.BlockSpec((None, None, tm, C), lambda q, i, c: (q, c[0], i, 0)), spec], out_specs=spec)
    return pl.pallas_call(
        body, name=name, grid_spec=grid_spec, out_shape=jax.ShapeDtypeStruct((nq, H, C), out_dtype),
        compiler_params=_cparams(("parallel", "parallel")),
    )(_index_operand(core), x, theirs)


def add_chips(pair, chip, got, name):
    n, R, C = got.shape
    tm = _tile(R, 256)

    def body(q_ref, *refs):
        acc = refs[0][...].astype(F32)
        for r in refs[1:-1]:
            acc = acc + r[...].astype(F32)
        refs[-1][...] = acc

    grid_spec = pltpu.PrefetchScalarGridSpec(
        num_scalar_prefetch=1, grid=(R // tm,),
        in_specs=[pl.BlockSpec((None, tm, C), lambda i, q: (q[0], i, 0))]
        + [pl.BlockSpec((None, tm, C), lambda i, q, j=j: (j, i, 0)) for j in range(n)],
        out_specs=pl.BlockSpec((tm, C), lambda i, q: (i, 0)))
    return pl.pallas_call(
        body, name=name, grid_spec=grid_spec, out_shape=jax.ShapeDtypeStruct((R, C), F32),
        compiler_params=_cparams(("parallel",)),
    )(_index_operand(chip), pair, *([got] * n))


_HBM = pl.BlockSpec(memory_space=pltpu.HBM)
N_CHIPS = 4


def _my_place():
    return lax.axis_index("x"), lax.axis_index("y"), lax.axis_index("c")


def _comm_call(body, name, xs, out_shapes, n_remote, n_local, sequencer=None):
    sems = [pltpu.SemaphoreType.DMA((n_remote,)), pltpu.SemaphoreType.DMA((n_remote,)),
            pltpu.SemaphoreType.DMA((max(n_local, 1),))]
    if sequencer is None:
        return pl.pallas_call(
            body, name=name, in_specs=[_HBM] * len(xs), out_specs=[_HBM] * len(out_shapes), out_shape=out_shapes,
            scratch_shapes=sems, compiler_params=pltpu.CompilerParams(has_side_effects=True),
        )(*xs)
    peers_of, collective_id = sequencer
    hbm = pltpu.MemorySpace.HBM
    x_refs = [jax.new_ref(x, memory_space=hbm) for x in xs]
    o_refs = [jax.empty_ref(s, memory_space=hbm) for s in out_shapes]

    @pl.kernel(mesh=plsc.ScalarSubcoreMesh(axis_name="sequencer", num_cores=1), name=name, scratch_types=tuple(sems),
               compiler_params=pltpu.CompilerParams(collective_id=collective_id))
    def launch(send_sems, recv_sems, local_sems):
        peers = peers_of(*_my_place())
        barrier = pltpu.get_barrier_semaphore()
        for peer in peers:
            pl.semaphore_signal(barrier, inc=1, device_id=peer, device_id_type=MESH)
        pl.semaphore_wait(barrier, len(peers))
        body(*x_refs, *o_refs, send_sems, recv_sems, local_sems)

    launch()
    return [o[...] for o in o_refs]


def _sibling(mx, my, mc):
    return [(mx, my, 1 - mc)]


def _same_core_of_other_chips(mx, my, mc):
    return [(tx, ty, mc) for tx, ty in _other_chips(mx, my)]


def _run(copies):
    for cp in copies:
        cp.start()
    for cp in copies:
        cp.wait()


def _other_chips(mx, my):
    return [(mx ^ (j >> 1), my ^ (j & 1)) for j in (1, 2, 3)]


def device_gather(x, name):
    H, C = x.shape

    def body(x_ref, o_ref, send_sems, recv_sems, local_sems):
        mx, my, mc = _my_place()
        dst = o_ref.at[2 * mx + my, mc]
        copies = [pltpu.make_async_copy(x_ref, dst, local_sems.at[0])]
        others = [(mx, my, 1 - mc)] + [(tx, ty, c) for tx, ty in _other_chips(mx, my) for c in (mc, 1 - mc)]
        for j, peer in enumerate(others):
            copies.append(pltpu.make_async_remote_copy(src_ref=x_ref, dst_ref=dst, send_sem=send_sems.at[j],
                                                       recv_sem=recv_sems.at[j], device_id=peer, device_id_type=MESH))
        _run(copies)

    out = _comm_call(body, name, [x], [jax.ShapeDtypeStruct((N_CHIPS, 2, H, C), x.dtype)], 7, 1)[0]
    return out.reshape(N_CHIPS * 2 * H, C)


def gather_two_level(xs, name):
    n = len(xs)
    shapes = [jax.ShapeDtypeStruct((2, N_CHIPS) + x.shape[1:], x.dtype) for x in xs]

    def body(*refs):
        x_refs, o_refs = refs[:n], refs[n:2 * n]
        send_sems, recv_sems, local_sems = refs[2 * n:]
        mx, my, mc = _my_place()
        q = 2 * mx + my
        first, local, second = [], [], []
        for i, (x_ref, o_ref) in enumerate(zip(x_refs, o_refs)):
            local.append(pltpu.make_async_copy(x_ref.at[mc], o_ref.at[mc, q], local_sems.at[i]))
            for j, (tx, ty) in enumerate(_other_chips(mx, my)):
                first.append(pltpu.make_async_remote_copy(
                    src_ref=x_ref.at[mc], dst_ref=o_ref.at[mc, q], send_sem=send_sems.at[4 * i + j],
                    recv_sem=recv_sems.at[4 * i + j], device_id=(tx, ty, mc), device_id_type=MESH))
            second.append(pltpu.make_async_remote_copy(
                src_ref=o_ref.at[mc], dst_ref=o_ref.at[mc], send_sem=send_sems.at[4 * i + 3],
                recv_sem=recv_sems.at[4 * i + 3], device_id=(mx, my, 1 - mc), device_id_type=MESH))
        for cp in local + first:
            cp.start()
        for cp in local:
            cp.wait()
        for cp in first:
            cp.wait_recv()
        _run(second)
        for cp in first:
            cp.wait_send()

    return _comm_call(body, name, xs, shapes, 4 * n, n)


def gather_two_level_sequencer(xs, name, collective_id):
    n = len(xs)
    hbm = pltpu.MemorySpace.HBM
    x_refs = [jax.new_ref(x, memory_space=hbm) for x in xs]
    o_refs = [jax.empty_ref(jax.ShapeDtypeStruct((2, N_CHIPS) + x.shape[1:], x.dtype), memory_space=hbm) for x in xs]

    @pl.kernel(mesh=plsc.ScalarSubcoreMesh(axis_name="sequencer", num_cores=1), name=name,
               scratch_types=(pltpu.SemaphoreType.DMA((4 * n,)), pltpu.SemaphoreType.DMA((4 * n,)),
                              pltpu.SemaphoreType.DMA((n,))),
               compiler_params=pltpu.CompilerParams(collective_id=collective_id))
    def launch(send_sems, recv_sems, local_sems):
        mx, my, mc = _my_place()
        peers = [(tx, ty, mc) for tx, ty in _other_chips(mx, my)] + [(mx, my, 1 - mc)]
        barrier = pltpu.get_barrier_semaphore()
        for peer in peers:
            pl.semaphore_signal(barrier, inc=1, device_id=peer, device_id_type=MESH)
        pl.semaphore_wait(barrier, len(peers))
        q = 2 * mx + my
        first, local, second = [], [], []
        for i, (x_ref, o_ref) in enumerate(zip(x_refs, o_refs)):
            local.append(pltpu.make_async_copy(x_ref.at[mc], o_ref.at[mc, q], local_sems.at[i]))
            for j, peer in enumerate(peers[:3]):
                first.append(pltpu.make_async_remote_copy(
                    src_ref=x_ref.at[mc], dst_ref=o_ref.at[mc, q], send_sem=send_sems.at[4 * i + j],
                    recv_sem=recv_sems.at[4 * i + j], device_id=peer, device_id_type=MESH))
            second.append(pltpu.make_async_remote_copy(
                src_ref=o_ref.at[mc], dst_ref=o_ref.at[mc], send_sem=send_sems.at[4 * i + 3],
                recv_sem=recv_sems.at[4 * i + 3], device_id=peers[3], device_id_type=MESH))
        for cp in local + first:
            cp.start()
        for cp in local:
            cp.wait()
        for cp in first:
            cp.wait_recv()
        _run(second)
        for cp in first:
            cp.wait_send()

    launch()
    return [o[...] for o in o_refs]


def pair_swap(xs, name, halves, collective_id=None):
    n = len(xs)
    shapes = [jax.ShapeDtypeStruct(x.shape[:1] + x.shape[2:] if halves else x.shape, x.dtype) for x in xs]

    def body(*refs):
        x_refs, o_refs = refs[:n], refs[n:2 * n]
        send_sems, recv_sems, _ = refs[2 * n:]
        mx, my, mc = _my_place()
        _run([pltpu.make_async_remote_copy(
            src_ref=x_ref.at[:, 1 - mc] if halves else x_ref, dst_ref=o_ref, send_sem=send_sems.at[i],
            recv_sem=recv_sems.at[i], device_id=(mx, my, 1 - mc), device_id_type=MESH)
            for i, (x_ref, o_ref) in enumerate(zip(x_refs, o_refs))])

    return _comm_call(body, name, xs, shapes, n, 0, None if collective_id is None else (_sibling, collective_id))


def chip_all_to_all(xs, name, collective_id=None):
    n = len(xs)
    shapes = [jax.ShapeDtypeStruct((N_CHIPS - 1,) + x.shape[1:], x.dtype) for x in xs]

    def body(*refs):
        x_refs, o_refs = refs[:n], refs[n:2 * n]
        send_sems, recv_sems, _ = refs[2 * n:]
        mx, my, mc = _my_place()
        copies = []
        for i, (x_ref, o_ref) in enumerate(zip(x_refs, o_refs)):
            for j, (tx, ty) in enumerate(_other_chips(mx, my)):
                copies.append(pltpu.make_async_remote_copy(
                    src_ref=x_ref.at[2 * tx + ty], dst_ref=o_ref.at[j], send_sem=send_sems.at[3 * i + j],
                    recv_sem=recv_sems.at[3 * i + j], device_id=(tx, ty, mc), device_id_type=MESH))
        _run(copies)

    return _comm_call(body, name, xs, shapes, 3 * n, 0,
                      None if collective_id is None else (_same_core_of_other_chips, collective_id))


WEIGHTS = ['norm_mix', 'norm_xa', 'norm_mem', 'norm_ffn', 'xa_wq', 'xa_wk', 'xa_wv', 'xa_wo', 'xa_q_norm', 'xa_k_norm',
           'ffn_w_up', 'ffn_conv_w', 'ffn_conv_b', 'ffn_w_down', 'hg_lb_logits', 'mix_w_in', 'hg_out_norm',
           'mla_q_a_norm', 'mla_w_uq', 'mla_kv_a_norm', 'mla_w_ukv', 'mla_qn_nope', 'mla_qn_rope', 'mla_kn_nope',
           'mla_kn_rope', 'mix_w_out', 's5_lam_re', 's5_lam_im', 's5_log_dt', 's5_b_re', 's5_b_im', 's5_c_re',
           's5_c_im', 's5_d', 's5_w_glu_a', 's5_w_glu_b']
INPUTS = ['x', 'mem', 'positions'] + WEIGHTS + ['loss_target'] + ['m_' + n for n in WEIGHTS] + ['v_' + n for n in WEIGHTS]
SHARD_AXIS = {'xa_wq': 1, 'xa_wk': 1, 'xa_wv': 1, 'xa_wo': 1, 'ffn_w_up': 2, 'ffn_conv_w': 2, 'ffn_w_down': 1,
              'mix_w_in': 2, 'mla_w_uq': 2, 'mla_w_ukv': 2, 'mix_w_out': 1, 's5_d': 1, 's5_w_glu_a': 1, 's5_w_glu_b': 1}
BIG = ['xa_wq', 'xa_wk', 'xa_wv', 'xa_wo', 'ffn_w_up', 'ffn_w_down', 'mix_w_in', 'mix_w_out', 's5_w_glu_a', 's5_w_glu_b']
FIRST_NEEDED = ('mix_w_in', 'mix_w_out')
SMALL_SHARDED = [n for n in WEIGHTS if n in SHARD_AXIS and n not in BIG]
REPLICATED = [n for n in WEIGHTS if n not in SHARD_AXIS]
SMALL = SMALL_SHARDED + REPLICATED
PACK_W = 1024
ROW_MULT = 16
W_IN_SHARD = IN_WIDTH // N_CHIPS
W_IN_SHARD_PAD = 640


def _pack(flats, mult=ROW_MULT):
    flat = jnp.concatenate([f.reshape(-1) for f in flats])
    unit = mult * PACK_W
    n = -(-flat.shape[0] // unit) * unit
    return jnp.pad(flat, (0, n - flat.shape[0])).reshape(n // PACK_W, PACK_W)


def _unpack(packed, shapes):
    flat, out, o = packed.reshape(-1), [], 0
    for s in shapes:
        n = math.prod(s)
        out.append(flat[o:o + n].reshape(s))
        o += n
    return out


def _rope_pad(w):
    z = jnp.zeros(w.shape[:-1] + (MLA_ROPE // 2,), w.dtype)
    return jnp.concatenate([w[..., :MLA_ROPE // 2], z, w[..., MLA_ROPE // 2:], z], axis=-1)


def _rope_unpad(g):
    return jnp.concatenate([g[..., :MLA_ROPE // 2], g[..., 64:64 + MLA_ROPE // 2]], axis=-1)


def _blockdiag_in(bb):
    nb = bb.shape[0] // S5_GB
    t = bb.reshape(nb, S5_GB, S5_STATE, S5_GROUP).transpose(0, 1, 3, 2)
    return jnp.einsum('bgmp,gh->bgmhp', t, jnp.eye(S5_GB, dtype=bb.dtype)).reshape(nb, S5_GB * S5_GROUP, S5_LANES)


def _blockdiag_in_t(dw):
    nb = dw.shape[0]
    t = jnp.einsum('bgmhp,gh->bgmp', dw.reshape(nb, S5_GB, S5_GROUP, S5_GB, S5_STATE), jnp.eye(S5_GB, dtype=dw.dtype))
    return t.transpose(0, 1, 3, 2).reshape(nb * S5_GB, S5_STATE, S5_GROUP)


def _blockdiag_out(c):
    nb = c.shape[0] // S5_GB
    t = c.reshape(nb, S5_GB, S5_GROUP, S5_STATE).transpose(0, 1, 3, 2)
    return jnp.einsum('bgpm,gh->bgphm', t, jnp.eye(S5_GB, dtype=c.dtype)).reshape(nb, S5_LANES, S5_GB * S5_GROUP)


def _blockdiag_out_t(dc):
    nb = dc.shape[0]
    t = jnp.einsum('bgphm,gh->bgpm', dc.reshape(nb, S5_GB, S5_STATE, S5_GB, S5_GROUP), jnp.eye(S5_GB, dtype=dc.dtype))
    return t.transpose(0, 1, 3, 2).reshape(nb * S5_GB, S5_GROUP, S5_STATE)


def _gather_weights(P):
    def halves(x):
        return x if x.shape[0] == 2 else x.reshape(2, x.shape[1] // 2, x.shape[2])

    now = [n for n in BIG if n in FIRST_NEEDED]
    later = [n for n in BIG if n not in FIRST_NEEDED]
    xs = [halves(P[n].astype(BF16)) for n in now] + [halves(_pack([P[n] for n in SMALL_SHARDED], 2 * ROW_MULT)[None])]
    got = gather_two_level(xs, "gather_weights")
    got, xs_later = lax.optimization_barrier((got, [halves(P[n].astype(BF16)) for n in later]))
    got_later = gather_two_level_sequencer(xs_later, "gather_weights_later", 1)
    full_w = {}
    for n, g in list(zip(now, got[:-1])) + list(zip(later, got_later)):
        two_layers, by_rows = P[n].shape[0] == 2, SHARD_AXIS[n] == 1
        if two_layers and by_rows:
            full_w[n] = g.reshape(2, N_CHIPS * g.shape[2], g.shape[3])
        elif two_layers:
            full_w[n] = g.transpose(0, 2, 1, 3).reshape(2, g.shape[2], N_CHIPS * g.shape[3])
        elif by_rows:
            full_w[n] = g.transpose(1, 0, 2, 3).reshape(1, 2 * N_CHIPS * g.shape[2], g.shape[3])
        else:
            full_w[n] = g.transpose(0, 2, 1, 3).reshape(1, 2 * g.shape[2], N_CHIPS * g.shape[3])
    small = got[-1].transpose(1, 0, 2, 3).reshape(N_CHIPS, -1, PACK_W)
    per_chip = [_unpack(small[q], [P[n].shape for n in SMALL_SHARDED]) for q in range(N_CHIPS)]
    for i, n in enumerate(SMALL_SHARDED):
        full_w[n] = jnp.concatenate([per_chip[q][i] for q in range(N_CHIPS)], axis=SHARD_AXIS[n])
    return full_w


def _reduce_batch(items, small, P, results, tag, ids):
    mx, my, mc = _my_place()
    q = 2 * mx + my
    ids = ids or {}
    names = [f"{n}_{lyr}" for n, lyr, _ in items] + (['small'] if small is not None else [])
    xs = [g.reshape(N_CHIPS, 2, g.shape[1] // 2, g.shape[2]) for g in [g for _, _, g in items] + ([small] if small is not None else [])]
    def after(vals, tie):
        return (vals, None) if tie is None else lax.optimization_barrier((vals, tie))

    theirs = pair_swap(xs, "grads_pair_swap_" + tag, True, ids.get('swap'))
    theirs, tie = after(theirs, (yield None))
    pair = [add_own_half(x, mc, t, "grads_pair_sum_" + n, F32 if n == 'small' else BF16) for x, t, n in zip(xs, theirs, names)]
    got = chip_all_to_all(pair, "grads_chip_all_to_all_" + tag, ids.get('a2a'))
    if tie is not None:
        tie, _ = lax.optimization_barrier((tie, pair[-1]))
    got, tie = after(got, (yield pair[-1] if tie is None else tie))
    summed = [add_chips(p, q, g, "grads_chip_sum_" + n) for p, g, n in zip(pair, got, names)]
    other = pair_swap(summed[:len(items)], "grads_pair_join_" + tag, False, ids.get('join'))
    if small is not None:
        results['small_sum'] = device_gather(summed[-1], "grads_small_gather")
    other, tie = after(other, (yield summed[-1] if tie is None else tie))
    for (n, lyr, _), s, o in zip(items, summed, other):
        if n == 'mix_w_in':
            s, o = s[:, :W_IN_SHARD], o[:, :W_IN_SHARD]
        view = (P[n].shape[0], 2 * s.shape[0], P[n].shape[-1])
        bufs = results.get(n) or [lax.empty(view, F32) for _ in range(4)]
        results[n] = adamw_layer(P[n].reshape(view), s, o, mc, P['m_' + n].reshape(view), P['v_' + n].reshape(view), lyr,
                                 bufs, f"adamw_{n}_{lyr}")
    yield tie


def _update_small(small_sum, GS, P):
    mx, my, _ = _my_place()
    q = 2 * mx + my
    g_small = dict(zip(SMALL, _unpack(small_sum, [GS[n].shape for n in SMALL])))
    for n in SMALL_SHARDED:
        s = P[n].shape[SHARD_AXIS[n]]
        g_small[n] = lax.dynamic_slice_in_dim(g_small[n], q * s, s, axis=SHARD_AXIS[n])
    grad, delta, new_m, new_v = {}, {}, {}, {}
    packed = lambda prefix: _pack([P[prefix + n] for n in SMALL])
    d, m_, v_ = adamw(packed(''), _pack([g_small[n] for n in SMALL]), packed('m_'), packed('v_'), "adamw_small")
    shapes = [P[n].shape for n in SMALL]
    grad.update(g_small)
    for out, pk in ((delta, d), (new_m, m_), (new_v, v_)):
        out.update(zip(SMALL, _unpack(pk, shapes)))
    return grad, delta, new_m, new_v


def _row(v):
    return v.reshape(1, -1)


def _xattn_fwd(h, mem, W, lyr, tm):
    g_xa, g_mem = _row(W['norm_xa'][lyr]), _row(W['norm_mem'][lyr])
    g_q, g_k = _row(W['xa_q_norm'][lyr]), _row(W['xa_k_norm'][lyr])
    wq, wk, wv, wo = (W[n][lyr] for n in ('xa_wq', 'xa_wk', 'xa_wv', 'xa_wo'))
    L, D = h.shape
    M = mem.shape[0]
    hx = rms_fwd(h, g_xa, MXU_DTYPE)
    qp = matmul(hx, wq, name="xa_q")
    kv_opds = [full(mem), full(g_mem), full(wk), full(wv), full(g_k)]
    k, v = blocked_fwd(_mem_kv, kv_opds, [((M, D), F32, (M, D), lambda i: (0, 0))] * 2, 1, "xa_mem_kv")
    o = blocked_fwd(_xa_core, [rows(qp, tm), full(k), full(v), full(g_q)], [_out((L, D), MXU_DTYPE, tm)], L // tm,
                    "xa_core")[0]
    out = matmul(o, wo, add=h, name="xa_o")
    return out, (h, hx, qp, k, v, o)


def _xattn_bwd(dout, saved, mem, W, lyr, tm):
    h, hx, qp, k, v, o = saved
    g_xa, g_mem = _row(W['norm_xa'][lyr]), _row(W['norm_mem'][lyr])
    g_q, g_k = _row(W['xa_q_norm'][lyr]), _row(W['xa_k_norm'][lyr])
    wq, wk, wv, wo = (W[n][lyr] for n in ('xa_wq', 'xa_wk', 'xa_wv', 'xa_wo'))
    L = h.shape[0]
    do = matmul(dout, wo, "nt", name="xa_do")
    d_wo = matmul(o, dout, "tn", name="xa_dwo")
    dqp, dk, dv, d_gq = blocked_bwd(_xa_core, [rows(qp, tm, 'blk'), full(k, 'acc'), full(v, 'acc'), full(g_q, 'acc')],
                                    [rows(do, tm)], L // tm, "xa_core_bwd")
    d_wq = matmul(hx, dqp, "tn", name="xa_dwq")
    dhx = matmul(dqp, wq, "nt", name="xa_dhx")
    dh, d_gxa = rms_bwd(h, g_xa, dhx, dout)
    d_gmem, d_wk, d_wv, d_gk = blocked_bwd(
        _mem_kv, [full(mem), full(g_mem, 'acc'), full(wk, 'acc'), full(wv, 'acc'), full(g_k, 'acc')],
        [full(dk), full(dv)], 1, "xa_mem_kv_bwd")
    by_chip = lambda g: g.reshape(N_CHIPS, g.shape[0] // N_CHIPS, g.shape[1])
    grads = {'norm_xa': d_gxa, 'norm_mem': d_gmem, 'xa_q_norm': d_gq, 'xa_k_norm': d_gk,
             'xa_wq': by_chip(d_wq), 'xa_wk': by_chip(d_wk), 'xa_wv': by_chip(d_wv), 'xa_wo': by_chip(d_wo)}
    return dh, grads


def _conv_params(W, lyr):
    cw, cb = W['ffn_conv_w'][lyr], W['ffn_conv_b'][lyr]
    F = cw.shape[1] // 2
    return [cw[0:1, :F], cw[1:2, :F], cw[2:3, :F], cw[0:1, F:], cw[1:2, F:], cw[2:3, F:], _row(cb[:F]), _row(cb[F:])]


def _ffn_fwd(h, W, lyr, tm):
    L, D = h.shape
    w_up, w_down = W['ffn_w_up'][lyr], W['ffn_w_down'][lyr]
    F = w_down.shape[0]
    hf = rms_fwd(h, _row(W['norm_ffn'][lyr]), MXU_DTYPE)
    ug = matmul(hf, w_up[:, :F], name="ffn_up_gate")
    uv = matmul(hf, w_up[:, F:], name="ffn_up_value")
    opds = [cols(ug, 128), cols(uv, 128)] + [cols(p, 128) for p in _conv_params(W, lyr)]
    a = blocked_fwd(_conv_gate, opds, [((L, F), MXU_DTYPE, (L, 128), lambda j: (0, j))], F // 128, "ffn_conv_gate")[0]
    out = matmul(a, w_down, add=h, name="ffn_down")
    return out, (h, hf, ug, uv, a)


def _ffn_bwd(dout, saved, W, lyr, tm):
    h, hf, ug, uv, a = saved
    w_up, w_down = W['ffn_w_up'][lyr], W['ffn_w_down'][lyr]
    F = w_down.shape[0]
    da = matmul(dout, w_down, "nt", name="ffn_da")
    d_wdown = matmul(a, dout, "tn", name="ffn_dwdown")
    opds = [cols(ug, 128, 'blk'), cols(uv, 128, 'blk')] + [cols(p, 128, 'blk') for p in _conv_params(W, lyr)]
    gs = blocked_bwd(_conv_gate, opds, [cols(da, 128)], F // 128, "ffn_conv_gate_bwd")
    dug, duv = gs[0], gs[1]
    d_cw = jnp.concatenate([jnp.concatenate(gs[2:5], axis=0), jnp.concatenate(gs[5:8], axis=0)], axis=1)
    d_cb = jnp.concatenate([gs[8], gs[9]], axis=1)[0]
    half = N_CHIPS // 2
    d_wup = lax.empty((N_CHIPS, hf.shape[1], w_up.shape[1] // N_CHIPS), F32)
    d_wup = matmul(hf, dug, "tn", name="ffn_dwup_gate", into=(d_wup, 0), col_blocks=half)
    d_wup = matmul(hf, duv, "tn", name="ffn_dwup_value", into=(d_wup, half), col_blocks=half)
    dhf = matmul(dug, w_up[:, :F], "nt", name="ffn_dhf_gate")
    dhf = matmul(duv, w_up[:, F:], "nt", add=dhf, name="ffn_dhf_value")
    dh, d_g = rms_bwd(h, _row(W['norm_ffn'][lyr]), dhf, dout)
    d_wdown = d_wdown.reshape(N_CHIPS, F // N_CHIPS, d_wdown.shape[1])
    return dh, {'norm_ffn': d_g, 'ffn_w_up': d_wup, 'ffn_conv_w': d_cw, 'ffn_conv_b': d_cb, 'ffn_w_down': d_wdown}


def _mla_params(W):
    w_uq = W['mla_w_uq'][0].reshape(MLA_Q_RANK, MLA_HEADS, MLA_QK)
    w_uq = jnp.concatenate([w_uq[..., :MLA_NOPE], _rope_pad(w_uq[..., MLA_NOPE:])], axis=-1)
    w_ukv = W['mla_w_ukv'][0].reshape(MLA_KV_RANK, MLA_HEADS, MLA_NOPE + MLA_V)
    w_ukv = jnp.concatenate([w_ukv[..., :MLA_NOPE].reshape(MLA_KV_RANK, -1), w_ukv[..., MLA_NOPE:].reshape(MLA_KV_RANK, -1)],
                            axis=1)
    return [_row(W['mla_q_a_norm'][0]), w_uq.reshape(MLA_Q_RANK, MLA_HEADS * MLA_DK), _row(W['mla_kv_a_norm'][0]), w_ukv,
            _row(W['mla_qn_nope'][0]), _row(_rope_pad(W['mla_qn_rope'][0])), _row(W['mla_kn_nope'][0]),
            _row(_rope_pad(W['mla_kn_rope'][0]))]


def _w_in_padded(W):
    w = W['mix_w_in'][0]
    return jnp.concatenate([w[:, :IN_WIDTH - MLA_ROPE], _rope_pad(w[:, IN_WIDTH - MLA_ROPE:])], axis=1)


def _mixer0_fwd(h, W, cos_p, sin_p, tm):
    L = h.shape[0]
    t = min(ATTN_ROWS, L // ATTN_WIDE)
    hn = rms_fwd(h, _row(W['norm_mix'][0]), MXU_DTYPE)
    proj = matmul(hn, _w_in_padded(W), name="mix_in")
    logits = W['hg_lb_logits']
    lb = blocked_fwd(_lb_first, [full(logits)], [((1, HG_WIDTH), F32, (1, HG_WIDTH), lambda i: (0, 0))], 1, "hg_lb")[0]
    gain = _row(W['hg_out_norm'][0])
    o_hg, states = hgrn2_fwd(proj, lb, gain)
    mp = _mla_params(W)
    q, k, v = mla_prep_fwd(proj, cos_p, sin_p, mp, tm)
    scale = MLA_QK ** -0.5
    o_mla, lse = attn_fwd(q, k, v, scale, t)
    w_out = W['mix_w_out'][0]
    out = matmul(o_hg, w_out[:HG_WIDTH], add=h, name="mix_out_hg")
    out = matmul(o_mla, w_out[HG_WIDTH:], add=out, name="mix_out_mla")
    return out, (h, hn, proj, lb, o_hg, states, q, k, v, o_mla, lse)


def _mixer0_bwd(dout, saved, W, cos_p, sin_p, tm, part_way=None):
    h, hn, proj, lb, o_hg, states, q, k, v, o_mla, lse = saved
    L = h.shape[0]
    t = min(ATTN_ROWS, L // ATTN_WIDE)
    scale = MLA_QK ** -0.5
    w_out = W['mix_w_out'][0]
    gain = _row(W['hg_out_norm'][0])
    do_hg = matmul(dout, w_out[:HG_WIDTH], "nt", name="mix_do_hg")
    do_mla = matmul(dout, w_out[HG_WIDTH:], "nt", name="mix_do_mla")
    d_wout = jnp.concatenate([matmul(o_hg, dout, "tn", name="mix_dwout_hg"), matmul(o_mla, dout, "tn", name="mix_dwout_mla")],
                             axis=0)
    if part_way is not None:
        do_mla = part_way(do_mla)
    dq = attn_bwd_dq(q, k, v, o_mla, lse, do_mla, scale, t)
    dk, dv = attn_bwd_dkv(q, k, v, o_mla, lse, do_mla, scale, t)
    mp = _mla_params(W)
    d_mla, d_qa, d_wuq, d_kva, d_wukv, d_qnn, d_qnr, d_knn, d_knr = mla_prep_bwd(proj, cos_p, sin_p, mp, dq, dk, dv, tm)
    d_hg, d_lb, d_gain = hgrn2_bwd(proj, lb, gain, states, do_hg)
    w_in, n_hg = _w_in_padded(W), 4 * HG_WIDTH
    d_win = jnp.concatenate([matmul(hn, d_hg, "tn", name="mix_dwin_hg"), matmul(hn, d_mla, "tn", name="mix_dwin_mla")], axis=1)
    dhn = matmul(d_hg, w_in[:, :n_hg], "nt", name="mix_dhn_hg")
    dhn = matmul(d_mla, w_in[:, n_hg:], "nt", add=dhn, name="mix_dhn_mla")
    dh, d_g = rms_bwd(h, _row(W['norm_mix'][0]), dhn, dout)
    logits = W['hg_lb_logits']
    d_logits = blocked_bwd(_lb_first, [full(logits, 'acc')], [full(d_lb)], 1, "hg_lb_bwd")[0]
    d_wuq = d_wuq.reshape(MLA_Q_RANK, MLA_HEADS, MLA_DK)
    d_wuq = jnp.concatenate([d_wuq[..., :MLA_NOPE], _rope_unpad(d_wuq[..., MLA_NOPE:])], axis=-1)
    hw = MLA_HEADS * MLA_NOPE
    d_wukv = jnp.concatenate([d_wukv[:, :hw].reshape(MLA_KV_RANK, MLA_HEADS, MLA_NOPE),
                              d_wukv[:, hw:].reshape(MLA_KV_RANK, MLA_HEADS, MLA_V)], axis=-1)
    d_win = jnp.concatenate([d_win[:, :IN_WIDTH - MLA_ROPE], _rope_unpad(d_win[:, IN_WIDTH - MLA_ROPE:])], axis=1)
    d_win = d_win.reshape(d_win.shape[0], N_CHIPS, W_IN_SHARD).transpose(1, 0, 2)
    d_win = jnp.pad(d_win, ((0, 0), (0, 0), (0, W_IN_SHARD_PAD - W_IN_SHARD)))
    d_wout = d_wout.reshape(N_CHIPS, d_wout.shape[0] // N_CHIPS, d_wout.shape[1])
    grads = {'norm_mix': d_g, 'hg_lb_logits': d_logits, 'mix_w_in': d_win, 'hg_out_norm': d_gain,
             'mla_q_a_norm': d_qa, 'mla_w_uq': d_wuq.reshape(1, MLA_Q_RANK, -1), 'mla_kv_a_norm': d_kva,
             'mla_w_ukv': d_wukv.reshape(1, MLA_KV_RANK, -1), 'mla_qn_nope': d_qnn, 'mla_qn_rope': _rope_unpad(d_qnr),
             'mla_kn_nope': d_knn, 'mla_kn_rope': _rope_unpad(d_knr), 'mix_w_out': d_wout}
    return dh,grads


def _s5_inputs(W):
    G = W['s5_lam_re'].shape[1]
    return [W['s5_lam_re'][0], W['s5_lam_im'][0], W['s5_log_dt'][0].reshape(G, 1),
            W['s5_b_re'][0].reshape(G, -1), W['s5_b_im'][0].reshape(G, -1)]


def _mixer1_fwd(h, W, tm):
    L, D = h.shape
    u = rms_fwd(h, _row(W['norm_mix'][1]), F32)
    di = _s5_inputs(W)
    G = di[0].shape[0]
    sq, wide = ((G, S5_STATE), F32, (G, S5_STATE), lambda i: (0, 0)), ((G, S5_STATE * S5_GROUP), F32, (G, S5_STATE * S5_GROUP), lambda i: (0, 0))
    ar, ai, bbr, bbi = blocked_fwd(_s5_discretize, [full(a) for a in di], [sq, sq, wide, wide], 1, "s5_discretize")
    nb = G // S5_GB
    core = (_blockdiag_in(bbr.reshape(G, S5_STATE, S5_GROUP)), _blockdiag_in(bbi.reshape(G, S5_STATE, S5_GROUP)),
            ar.reshape(nb, 1, S5_LANES), ai.reshape(nb, 1, S5_LANES),
            _blockdiag_out(W['s5_c_re'][0]), _blockdiag_out(W['s5_c_im'][0]))
    y = s5_fwd(u, *core)
    d = W['s5_d']
    y2 = blocked_fwd(_s5_post, [rows(y, tm), rows(u, tm), full(d)], [_out((L, D), MXU_DTYPE, tm)], L // tm, "s5_post")[0]
    w_ab = jnp.concatenate([W['s5_w_glu_a'][0], W['s5_w_glu_b'][0]], axis=1)
    ab = matmul(y2, w_ab, name="s5_glu_in")
    out = blocked_fwd(lambda a, b, res: (res + _glu(a, b)[0],),
                      [rows(ab, tm, col=0, width=D), rows(ab, tm, col=1, width=D), rows(h, tm)], [_out((L, D), F32, tm)],
                      L // tm, "s5_glu")[0]
    return out, (h, u, core, y, y2, ab)


def _mixer1_bwd(dout, saved, W, tm):
    h, u, core, y, y2, ab = saved
    L, D = h.shape
    da, db = blocked_bwd(_glu, [rows(ab, tm, 'blk', col=0, width=D), rows(ab, tm, 'blk', col=1, width=D)], [rows(dout, tm)],
                         L // tm, "s5_glu_bwd")
    w_a, w_b = W['s5_w_glu_a'][0], W['s5_w_glu_b'][0]
    dy2 = matmul(da, w_a, "nt", name="s5_dy2_a")
    dy2 = matmul(db, w_b, "nt", add=dy2, name="s5_dy2_b")
    d_wa = matmul(y2, da, "tn", name="s5_dwa")
    d_wb = matmul(y2, db, "tn", name="s5_dwb")
    d = W['s5_d']
    dy, du_skip, d_d = blocked_bwd(_s5_post, [rows(y, tm, 'blk'), rows(u, tm, 'blk'), full(d, 'acc')], [rows(dy2, tm)], L // tm,
                                   "s5_post_bwd")
    du, dwr, dwi, dar, dai, dcr, dci = s5_bwd(u, *core, dy, du_skip, min(S5_BWD_CHUNK, L))
    di = _s5_inputs(W)
    G = di[0].shape[0]
    cts = [dar.reshape(G, S5_STATE), dai.reshape(G, S5_STATE), _blockdiag_in_t(dwr).reshape(G, -1), _blockdiag_in_t(dwi).reshape(G, -1)]
    d_lr, d_li, d_ldt, d_br, d_bi = blocked_bwd(_s5_discretize, [full(a, 'acc') for a in di], [full(c) for c in cts], 1,
                                                "s5_discretize_bwd")
    dh, d_g = rms_bwd(h, _row(W['norm_mix'][1]), du, dout)
    bshape = W['s5_b_re'].shape
    grads = {'norm_mix': d_g, 's5_lam_re': d_lr[None], 's5_lam_im': d_li[None], 's5_log_dt': d_ldt.reshape(1, G),
             's5_b_re': d_br.reshape(bshape), 's5_b_im': d_bi.reshape(bshape), 's5_c_re': _blockdiag_out_t(dcr)[None],
             's5_c_im': _blockdiag_out_t(dci)[None], 's5_d': d_d, 's5_w_glu_a': d_wa.reshape(N_CHIPS, -1, D), 's5_w_glu_b': d_wb.reshape(N_CHIPS, -1, D)}
    return dh,grads


def kernel(x, mem, positions, norm_mix, norm_xa, norm_mem, norm_ffn, xa_wq, xa_wk, xa_wv, xa_wo, xa_q_norm, xa_k_norm, ffn_w_up, ffn_conv_w, ffn_conv_b, ffn_w_down, hg_lb_logits, mix_w_in, hg_out_norm, mla_q_a_norm, mla_w_uq, mla_kv_a_norm, mla_w_ukv, mla_qn_nope, mla_qn_rope, mla_kn_nope, mla_kn_rope, mix_w_out, s5_lam_re, s5_lam_im, s5_log_dt, s5_b_re, s5_b_im, s5_c_re, s5_c_im, s5_d, s5_w_glu_a, s5_w_glu_b, loss_target, m_norm_mix, m_norm_xa, m_norm_mem, m_norm_ffn, m_xa_wq, m_xa_wk, m_xa_wv, m_xa_wo, m_xa_q_norm, m_xa_k_norm, m_ffn_w_up, m_ffn_conv_w, m_ffn_conv_b, m_ffn_w_down, m_hg_lb_logits, m_mix_w_in, m_hg_out_norm, m_mla_q_a_norm, m_mla_w_uq, m_mla_kv_a_norm, m_mla_w_ukv, m_mla_qn_nope, m_mla_qn_rope, m_mla_kn_nope, m_mla_kn_rope, m_mix_w_out, m_s5_lam_re, m_s5_lam_im, m_s5_log_dt, m_s5_b_re, m_s5_b_im, m_s5_c_re, m_s5_c_im, m_s5_d, m_s5_w_glu_a, m_s5_w_glu_b, v_norm_mix, v_norm_xa, v_norm_mem, v_norm_ffn, v_xa_wq, v_xa_wk, v_xa_wv, v_xa_wo, v_xa_q_norm, v_xa_k_norm, v_ffn_w_up, v_ffn_conv_w, v_ffn_conv_b, v_ffn_w_down, v_hg_lb_logits, v_mix_w_in, v_hg_out_norm, v_mla_q_a_norm, v_mla_w_uq, v_mla_kv_a_norm, v_mla_w_ukv, v_mla_qn_nope, v_mla_qn_rope, v_mla_kn_nope, v_mla_kn_rope, v_mix_w_out, v_s5_lam_re, v_s5_lam_im, v_s5_log_dt, v_s5_b_re, v_s5_b_im, v_s5_c_re, v_s5_c_im, v_s5_d, v_s5_w_glu_a, v_s5_w_glu_b):
    P = dict(locals())
    assert sorted(P) == sorted(INPUTS) and norm_mix.shape[0] == 2 and mix_w_in.shape[0] == 1
    x, mem, target = P['x'][0], P['mem'][0], P['loss_target'][0]
    L, D = x.shape
    tm = min(256, L)

    W = {n: P[n] for n in REPLICATED}
    W.update(_gather_weights(P))

    inv_freq = 1.0 / (ROPE_BASE ** (jnp.arange(0, MLA_ROPE, 2, dtype=F32) / MLA_ROPE))
    ang = P['positions'][0].astype(F32)[:, None] * inv_freq
    cos, sin, z = jnp.cos(ang), jnp.sin(ang), jnp.zeros_like(ang)
    cos_p = jnp.concatenate([cos, z, cos, z], axis=1)
    sin_p = jnp.concatenate([-sin, z, sin, z], axis=1)

    h, s_mix0 = _mixer0_fwd(x, W, cos_p, sin_p, tm)
    h, s_xa0 = _xattn_fwd(h, mem, W, 0, tm)
    h, s_ffn0 = _ffn_fwd(h, W, 0, tm)
    h, s_mix1 = _mixer1_fwd(h, W, tm)
    h, s_xa1 = _xattn_fwd(h, mem, W, 1, tm)
    h, s_ffn1 = _ffn_fwd(h, W, 1, tm)
    n = L // tm
    dh, parts = blocked_fwd(_loss_fn, [rows(h, tm), rows(target, tm)],
                            [_out((L, D), F32, tm), ((n * 8, 128), F32, (8, 128), lambda i: (i, 0))], n, "loss")
    loss = lax.psum(jnp.sum(parts), ("x", "y", "c"))

    layered = {}

    def collect(g, lyr):
        for k_, v_ in g.items():
            layered.setdefault(k_, {})[lyr] = v_

    results = {}

    def big_items(lyr, names):
        return [(n_, 0 if P[n_].shape[0] == 1 else lyr, layered[n_][lyr]) for n_ in names]

    per_layer = [n_ for n_ in BIG if P[n_].shape[0] == 2]
    second_mixer = ['s5_w_glu_a', 's5_w_glu_b']
    first_mixer = ['mix_w_in', 'mix_w_out']
    assert sorted(per_layer + second_mixer + first_mixer) == sorted(BIG)

    dh, g = _ffn_bwd(dh, s_ffn1, W, 1, tm)
    collect(g, 1)
    dh, g = _xattn_bwd(dh, s_xa1, mem, W, 1, tm)
    collect(g, 1)
    dh, g = _mixer1_bwd(dh, s_mix1, W, tm)
    collect(g, 1)
    late = _reduce_batch(big_items(1, per_layer + second_mixer), None, P, results, "late_layer", {'swap': 2, 'a2a': 3, 'join': 4})
    next(late)
    dh, g = _ffn_bwd(dh, s_ffn0, W, 0, tm)
    collect(g, 0)
    dh = late.send(dh)
    dh, g = _xattn_bwd(dh, s_xa0, mem, W, 0, tm)
    collect(g, 0)
    mid = _reduce_batch(big_items(0, per_layer), None, P, results, "first_layer", {'swap': 5, 'a2a': 6, 'join': 7})
    next(mid)
    dx, g = _mixer0_bwd(dh, s_mix0, W, cos_p, sin_p, tm, part_way=mid.send)
    collect(g, 0)

    GS = {}
    for name in SMALL:
        by_layer = [layered[name][lyr] for lyr in sorted(layered[name])]
        full_shape = W[name].shape
        GS[name] = (by_layer[0].reshape(full_shape) if len(by_layer) == 1
                    else jnp.stack([g_.reshape(full_shape[1:]) for g_ in by_layer]))
    small = _pack([GS[n_] for n_ in SMALL], 2 * N_CHIPS * ROW_MULT).reshape(N_CHIPS, -1, PACK_W)
    dx = late.send(dx)
    dx = mid.send(dx)
    last = _reduce_batch(big_items(0, first_mixer), small, P, results, "first_mixer", {'swap': 8, 'a2a': 9, 'join': 10})
    next(last)
    late.send(None)
    last_pair_sum = last.send(None)
    mid.send(last_pair_sum)
    mid_name = per_layer[-1]
    results[mid_name] = list(last.send(list(results[mid_name])))
    last.send(None)
    outs = list(_update_small(results['small_sum'], GS, P))
    for k_ in range(4):
        outs[k_].update({n_: results[n_][k_].reshape(P[n_].shape) for n_ in BIG})
    return (loss, dx[None], *[d[n_] for d in outs for n_ in WEIGHTS])
```
